```python
import jax, jax.numpy as jnp
from jax import lax
import numpy as np

D_MODEL = 1024
BATCH = 8
SEQ = 8192
DEPTH = 1

CONV_CH = D_MODEL // 2
N_CONV_GROUPS = 8
HEAD_DIM = 64
N_HEADS = (D_MODEL // 2) // HEAD_DIM
ATTN_W = N_HEADS * HEAD_DIM
MIX_W = CONV_CH + ATTN_W
CONV_K = 3
D_FF = ((8 * D_MODEL) // 3 + 255) // 256 * 256
Q_BLOCK = 128
EPS = 1e-6
FORGET_BIAS_INIT = 3.0
IN_COLS = 3 * CONV_CH + 3 * ATTN_W + N_HEADS

kernel_name = "hymba_conv_fox_convffn_layer"


def rmsnorm(x, g):
    xf = x.astype(jnp.float32)
    y = xf * lax.rsqrt(jnp.mean(xf * xf, axis=-1, keepdims=True) + EPS)
    return (y * g.astype(jnp.float32)).astype(x.dtype)


def causal_dwconv(x, w):
    s = x.shape[1]
    xp = jnp.pad(x, ((0, 0), (CONV_K - 1, 0), (0, 0)))
    return sum(xp[:, j:j + s, :] * w[j] for j in range(CONV_K))


def fox_attention(q, k, v, log_f):
    b, s, h, dh = q.shape
    nblk = s // Q_BLOCK
    scale = 1.0 / np.sqrt(dh).astype(np.float32)
    F = jnp.transpose(jnp.cumsum(log_f, axis=1), (0, 2, 1))
    q_blocks = jnp.transpose(q.reshape(b, nblk, Q_BLOCK, h, dh), (1, 0, 2, 3, 4))
    f_blocks = jnp.transpose(F.reshape(b, h, nblk, Q_BLOCK), (2, 0, 1, 3))
    kpos = jnp.arange(s)

    def one_block(args):
        qb, fqb, i = args
        qpos = i * Q_BLOCK + jnp.arange(Q_BLOCK)
        sc = jnp.einsum('bqhd,bkhd->bhqk', qb, k,
                        preferred_element_type=jnp.float32) * scale
        logits = sc + fqb[..., None] - F[:, :, None, :]
        mask = kpos[None, :] <= qpos[:, None]
        logits = jnp.where(mask[None, None], logits, -jnp.inf)
        p = jax.nn.softmax(logits, axis=-1)
        return jnp.einsum('bhqk,bkhd->bqhd', p.astype(v.dtype), v)

    out = lax.map(one_block, (q_blocks, f_blocks, jnp.arange(nblk)))
    return jnp.transpose(out, (1, 0, 2, 3, 4)).reshape(b, s, h * dh)


def _fwd_setup_inputs(seed: int = 0) -> dict:
    key = jax.random.key(seed)
    ks = jax.random.split(key, 16)
    nrm = lambda k, shape, sc: jax.random.normal(k, shape, jnp.float32) * sc
    gain = lambda k, shape: 1.0 + 0.02 * jax.random.normal(k, shape, jnp.float32)
    return {
        "x": jax.random.normal(ks[0], (BATCH, SEQ, D_MODEL), jnp.float32),
        "g_mix": gain(ks[1], (DEPTH, D_MODEL)),
        "w_in": nrm(ks[2], (DEPTH, D_MODEL, IN_COLS), D_MODEL ** -0.5),
        "b_f": FORGET_BIAS_INIT + 0.1 * jax.random.normal(ks[3], (DEPTH, N_HEADS), jnp.float32),
        "w_conv": nrm(ks[4], (DEPTH, CONV_K, CONV_CH), CONV_K ** -0.5),
        "g_conv_out": gain(ks[5], (DEPTH, CONV_CH)),
        "g_attn_out": gain(ks[6], (DEPTH, ATTN_W)),
        "w_o": nrm(ks[7], (DEPTH, MIX_W, D_MODEL), MIX_W ** -0.5),
        "g_ffn": gain(ks[8], (DEPTH, D_MODEL)),
        "w_up": nrm(ks[9], (DEPTH, D_MODEL, 2 * D_FF), D_MODEL ** -0.5),
        "w_ffn_conv": nrm(ks[10], (DEPTH, CONV_K, 2 * D_FF), CONV_K ** -0.5),
        "w_down": nrm(ks[11], (DEPTH, D_FF, D_MODEL), D_FF ** -0.5),
        "g_final": gain(ks[12], (D_MODEL,)),
    }


def _fwd_reference(x, g_mix, w_in, b_f, w_conv, g_conv_out, g_attn_out, w_o,
              g_ffn, w_up, w_ffn_conv, w_down, g_final):
    b, s, _ = x.shape
    o1 = CONV_CH
    o2 = 2 * CONV_CH
    o3 = 3 * CONV_CH
    o4 = o3 + ATTN_W
    o5 = o4 + ATTN_W
    o6 = o5 + ATTN_W
    for l in range(DEPTH):
        h = rmsnorm(x, g_mix[l])
        z = h @ w_in[l]
        gb, gc, xc = z[..., :o1], z[..., o1:o2], z[..., o2:o3]
        q = z[..., o3:o4].reshape(b, s, N_HEADS, HEAD_DIM)
        k = z[..., o4:o5].reshape(b, s, N_HEADS, HEAD_DIM)
        v = z[..., o5:o6].reshape(b, s, N_HEADS, HEAD_DIM)
        f_logit = z[..., o6:]
        y_conv = gb * causal_dwconv(gc * xc, w_conv[l])
        log_f = jax.nn.log_sigmoid((f_logit + b_f[l]).astype(jnp.float32))
        y_attn = fox_attention(q, k, v, log_f)
        mix = jnp.concatenate([rmsnorm(y_conv, g_conv_out[l]),
                               rmsnorm(y_attn, g_attn_out[l])], axis=-1)
        x = x + mix @ w_o[l]
        h = rmsnorm(x, g_ffn[l])
        u = causal_dwconv(h @ w_up[l], w_ffn_conv[l])
        a, g = u[..., :D_FF], u[..., D_FF:]
        x = x + (jax.nn.silu(g) * a) @ w_down[l]
    return rmsnorm(x, g_final)


import jax as _jax
import jax.numpy as _jnp

TWIN_FORMAT = 'train_step'
FWD_PARAMS = ['x', 'g_mix', 'w_in', 'b_f', 'w_conv', 'g_conv_out', 'g_attn_out', 'w_o', 'g_ffn', 'w_up', 'w_ffn_conv', 'w_down', 'g_final']
TWIN_WEIGHTS = ['g_mix', 'w_in', 'b_f', 'w_conv', 'g_conv_out', 'g_attn_out', 'w_o', 'g_ffn', 'w_up', 'w_ffn_conv', 'w_down', 'g_final']
TWIN_DIFF_INPUT = 'x'
TWIN_INPUTS = ['x', 'g_mix', 'w_in', 'b_f', 'w_conv', 'g_conv_out', 'g_attn_out', 'w_o', 'g_ffn', 'w_up', 'w_ffn_conv', 'w_down', 'g_final', 'loss_target', 'm_g_mix', 'm_w_in', 'm_b_f', 'm_w_conv', 'm_g_conv_out', 'm_g_attn_out', 'm_w_o', 'm_g_ffn', 'm_w_up', 'm_w_ffn_conv', 'm_w_down', 'm_g_final', 'v_g_mix', 'v_w_in', 'v_b_f', 'v_w_conv', 'v_g_conv_out', 'v_g_attn_out', 'v_w_o', 'v_g_ffn', 'v_w_up', 'v_w_ffn_conv', 'v_w_down', 'v_g_final']
TWIN_OUTPUTS = ['loss', 'grad_x', 'grad_g_mix', 'grad_w_in', 'grad_b_f', 'grad_w_conv', 'grad_g_conv_out', 'grad_g_attn_out', 'grad_w_o', 'grad_g_ffn', 'grad_w_up', 'grad_w_ffn_conv', 'grad_w_down', 'grad_g_final', 'delta_g_mix', 'delta_w_in', 'delta_b_f', 'delta_w_conv', 'delta_g_conv_out', 'delta_g_attn_out', 'delta_w_o', 'delta_g_ffn', 'delta_w_up', 'delta_w_ffn_conv', 'delta_w_down', 'delta_g_final', 'new_m_g_mix', 'new_m_w_in', 'new_m_b_f', 'new_m_w_conv', 'new_m_g_conv_out', 'new_m_g_attn_out', 'new_m_w_o', 'new_m_g_ffn', 'new_m_w_up', 'new_m_w_ffn_conv', 'new_m_w_down', 'new_m_g_final', 'new_v_g_mix', 'new_v_w_in', 'new_v_b_f', 'new_v_w_conv', 'new_v_g_conv_out', 'new_v_g_attn_out', 'new_v_w_o', 'new_v_g_ffn', 'new_v_w_up', 'new_v_w_ffn_conv', 'new_v_w_down', 'new_v_g_final']
TWIN_LEAF_KINDS = {'loss': 'loss', 'grad_x': 'grad_x', 'grad_g_mix': 'grad_w', 'grad_w_in': 'grad_w', 'grad_b_f': 'grad_w', 'grad_w_conv': 'grad_w', 'grad_g_conv_out': 'grad_w', 'grad_g_attn_out': 'grad_w', 'grad_w_o': 'grad_w', 'grad_g_ffn': 'grad_w', 'grad_w_up': 'grad_w', 'grad_w_ffn_conv': 'grad_w', 'grad_w_down': 'grad_w', 'grad_g_final': 'grad_w', 'delta_g_mix': 'delta_w', 'delta_w_in': 'delta_w', 'delta_b_f': 'delta_w', 'delta_w_conv': 'delta_w', 'delta_g_conv_out': 'delta_w', 'delta_g_attn_out': 'delta_w', 'delta_w_o': 'delta_w', 'delta_g_ffn': 'delta_w', 'delta_w_up': 'delta_w', 'delta_w_ffn_conv': 'delta_w', 'delta_w_down': 'delta_w', 'delta_g_final': 'delta_w', 'new_m_g_mix': 'new_m', 'new_m_w_in': 'new_m', 'new_m_b_f': 'new_m', 'new_m_w_conv': 'new_m', 'new_m_g_conv_out': 'new_m', 'new_m_g_attn_out': 'new_m', 'new_m_w_o': 'new_m', 'new_m_g_ffn': 'new_m', 'new_m_w_up': 'new_m', 'new_m_w_ffn_conv': 'new_m', 'new_m_w_down': 'new_m', 'new_m_g_final': 'new_m', 'new_v_g_mix': 'new_v', 'new_v_w_in': 'new_v', 'new_v_b_f': 'new_v', 'new_v_w_conv': 'new_v', 'new_v_g_conv_out': 'new_v', 'new_v_g_attn_out': 'new_v', 'new_v_w_o': 'new_v', 'new_v_g_ffn': 'new_v', 'new_v_w_up': 'new_v', 'new_v_w_ffn_conv': 'new_v', 'new_v_w_down': 'new_v', 'new_v_g_final': 'new_v'}


def _forward(args):
    return _fwd_reference(*[args[k] for k in FWD_PARAMS])


def _output_shape():
    def fwd():
        inp = _fwd_setup_inputs(0)
        return _fwd_reference(*[inp[k] for k in FWD_PARAMS])
    out = _jax.eval_shape(fwd)
    return out.shape, out.dtype

N_MICROBATCH = 1
ADAM_LR = 0.001
ADAM_B1 = 0.9
ADAM_B2 = 0.999
ADAM_EPS = 1e-08
ADAM_WD = 0.01
ADAM_STEP = 10
PER_EXAMPLE_BATCH_AXIS = {'x': 0, 'loss_target': 0}
SHARED_INPUTS = []
_WEIGHT_DTYPES = {'g_mix': _jnp.float32, 'w_in': _jnp.float32, 'b_f': _jnp.float32, 'w_conv': _jnp.float32, 'g_conv_out': _jnp.float32, 'g_attn_out': _jnp.float32, 'w_o': _jnp.float32, 'g_ffn': _jnp.float32, 'w_up': _jnp.float32, 'w_ffn_conv': _jnp.float32, 'w_down': _jnp.float32, 'g_final': _jnp.float32}
MOMENT_SCALE = {'g_mix': 3.208921e-01, 'w_in': 1.848222e-01, 'b_f': 1.187530e+00, 'w_conv': 1.951571e-01, 'g_conv_out': 1.970812e-01, 'g_attn_out': 1.987705e-01, 'w_o': 1.925950e-01, 'g_ffn': 1.484169e-01, 'w_up': 6.106905e-02, 'w_ffn_conv': 6.132374e-02, 'w_down': 9.994006e-02, 'g_final': 6.404161e+01}


def _to_microbatches(a, axis):
    t = _jnp.moveaxis(a, axis, 0)
    t = t.reshape((N_MICROBATCH, t.shape[0] // N_MICROBATCH) + t.shape[1:])
    return _jnp.moveaxis(t, 1, axis + 1)


def setup_inputs(seed: int = 0) -> dict:
    inp = _fwd_setup_inputs(seed)
    key = _jax.random.fold_in(_jax.random.key(seed), 7919)
    shape, _ = _output_shape()
    out = dict(inp)
    out["loss_target"] = _jax.random.normal(_jax.random.fold_in(key, 0), shape, _jnp.float32)
    for i, name in enumerate(TWIN_WEIGHTS):
        w = inp[name].astype(_jnp.float32)
        if MOMENT_SCALE is None:
            s = _jnp.sqrt(_jnp.mean(_jnp.square(w)) + 1e-30)
        else:
            s = MOMENT_SCALE[name]
        km, kv = _jax.random.split(_jax.random.fold_in(key, i + 1))
        out[name] = w
        out["m_" + name] = s * _jax.random.normal(km, w.shape, _jnp.float32)
        out["v_" + name] = (s * s) * _jax.random.uniform(kv, w.shape, _jnp.float32, 0.5, 1.5)
    if N_MICROBATCH > 1:
        for name, axis in PER_EXAMPLE_BATCH_AXIS.items():
            out[name] = _to_microbatches(out[name], axis)
    return {'x': out['x'], 'g_mix': out['g_mix'], 'w_in': out['w_in'], 'b_f': out['b_f'], 'w_conv': out['w_conv'], 'g_conv_out': out['g_conv_out'], 'g_attn_out': out['g_attn_out'], 'w_o': out['w_o'], 'g_ffn': out['g_ffn'], 'w_up': out['w_up'], 'w_ffn_conv': out['w_ffn_conv'], 'w_down': out['w_down'], 'g_final': out['g_final'], 'loss_target': out['loss_target'], 'm_g_mix': out['m_g_mix'], 'm_w_in': out['m_w_in'], 'm_b_f': out['m_b_f'], 'm_w_conv': out['m_w_conv'], 'm_g_conv_out': out['m_g_conv_out'], 'm_g_attn_out': out['m_g_attn_out'], 'm_w_o': out['m_w_o'], 'm_g_ffn': out['m_g_ffn'], 'm_w_up': out['m_w_up'], 'm_w_ffn_conv': out['m_w_ffn_conv'], 'm_w_down': out['m_w_down'], 'm_g_final': out['m_g_final'], 'v_g_mix': out['v_g_mix'], 'v_w_in': out['v_w_in'], 'v_b_f': out['v_b_f'], 'v_w_conv': out['v_w_conv'], 'v_g_conv_out': out['v_g_conv_out'], 'v_g_attn_out': out['v_g_attn_out'], 'v_w_o': out['v_w_o'], 'v_g_ffn': out['v_g_ffn'], 'v_w_up': out['v_w_up'], 'v_w_ffn_conv': out['v_w_ffn_conv'], 'v_w_down': out['v_w_down'], 'v_g_final': out['v_g_final']}


def _loss(weights, diff, rest, loss_target):
    with _jax.named_scope("forward"):
        args = {**rest, TWIN_DIFF_INPUT: diff, **{k: w.astype(_WEIGHT_DTYPES[k]) for k, w in weights.items()}}
        y = _forward(args)
    with _jax.named_scope("loss_head"):
        err = _jnp.square(y.astype(_jnp.float32) - loss_target)
        return 0.5 * _jnp.sum(_jnp.mean(err, axis=-1)) if err.ndim else 0.5 * err


def _adamw(w, g, m, v):
    m = ADAM_B1 * m + (1.0 - ADAM_B1) * g
    v = ADAM_B2 * v + (1.0 - ADAM_B2) * _jnp.square(g)
    m_hat = m / (1.0 - ADAM_B1 ** ADAM_STEP)
    v_hat = v / (1.0 - ADAM_B2 ** ADAM_STEP)
    delta = -ADAM_LR * (m_hat / (_jnp.sqrt(v_hat) + ADAM_EPS) + ADAM_WD * w)
    return delta, m, v


def reference(x, g_mix, w_in, b_f, w_conv, g_conv_out, g_attn_out, w_o, g_ffn, w_up, w_ffn_conv, w_down, g_final, loss_target, m_g_mix, m_w_in, m_b_f, m_w_conv, m_g_conv_out, m_g_attn_out, m_w_o, m_g_ffn, m_w_up, m_w_ffn_conv, m_w_down, m_g_final, v_g_mix, v_w_in, v_b_f, v_w_conv, v_g_conv_out, v_g_attn_out, v_w_o, v_g_ffn, v_w_up, v_w_ffn_conv, v_w_down, v_g_final):
    given = dict(x=x, g_mix=g_mix, w_in=w_in, b_f=b_f, w_conv=w_conv, g_conv_out=g_conv_out, g_attn_out=g_attn_out, w_o=w_o, g_ffn=g_ffn, w_up=w_up, w_ffn_conv=w_ffn_conv, w_down=w_down, g_final=g_final, loss_target=loss_target, m_g_mix=m_g_mix, m_w_in=m_w_in, m_b_f=m_b_f, m_w_conv=m_w_conv, m_g_conv_out=m_g_conv_out, m_g_attn_out=m_g_attn_out, m_w_o=m_w_o, m_g_ffn=m_g_ffn, m_w_up=m_w_up, m_w_ffn_conv=m_w_ffn_conv, m_w_down=m_w_down, m_g_final=m_g_final, v_g_mix=v_g_mix, v_w_in=v_w_in, v_b_f=v_b_f, v_w_conv=v_w_conv, v_g_conv_out=v_g_conv_out, v_g_attn_out=v_g_attn_out, v_w_o=v_w_o, v_g_ffn=v_g_ffn, v_w_up=v_w_up, v_w_ffn_conv=v_w_ffn_conv, v_w_down=v_w_down, v_g_final=v_g_final)
    weights = {n: given[n] for n in TWIN_WEIGHTS}
    shared = {n: given[n] for n in SHARED_INPUTS}
    per_example = {n: given[n] for n in ['x']}
    grad_fn = _jax.value_and_grad(_loss, argnums=(0, 1))

    def one_microbatch(ex, loss_target):
        ex = dict(ex)
        diff = ex.pop(TWIN_DIFF_INPUT)
        return grad_fn(weights, diff, {**shared, **ex}, loss_target)

    if N_MICROBATCH == 1:
        loss, (grad_w, grad_x) = one_microbatch(per_example, given["loss_target"])
    else:
        def body(carry, xs):
            loss_sum, grad_sum = carry
            l_k, (gw_k, gx_k) = one_microbatch(xs[0], xs[1])
            with _jax.named_scope("update"):
                return (loss_sum + l_k, _jax.tree.map(_jnp.add, grad_sum, gw_k)), gx_k

        init = (_jnp.zeros((), _jnp.float32), _jax.tree.map(_jnp.zeros_like, weights))
        (loss, grad_w), grad_x = _jax.lax.scan(body, init, (per_example, given["loss_target"]))
    with _jax.named_scope("update"):
        delta_w, new_m, new_v = {}, {}, {}
        for n in TWIN_WEIGHTS:
            delta_w[n], new_m[n], new_v[n] = _adamw(weights[n], grad_w[n], given["m_" + n], given["v_" + n])
    return (loss, grad_x, *[grad_w[n] for n in TWIN_WEIGHTS], *[delta_w[n] for n in TWIN_WEIGHTS],
            *[new_m[n] for n in TWIN_WEIGHTS], *[new_v[n] for n in TWIN_WEIGHTS])
```

```python
import functools

import jax
import jax.numpy as jnp
from jax import lax
from jax.experimental import pallas as pl
from jax.experimental.pallas import tpu as pltpu

F32, BF16 = jnp.float32, jnp.bfloat16
MESH = pl.DeviceIdType.MESH

D_MODEL = 1024
CONV_CH = 512
ATTN_W = 512
N_HEADS = 8
HEAD_DIM = 64
N_PAIRS = N_HEADS // 2
D_FF = 2816
IN_COLS = 3 * CONV_CH + 3 * ATTN_W + N_HEADS
EPS = 1e-6
Q_SCALE = 0.125
N_CHIPS = 4
LANES = 128
HALO = 8

ADAM_LR, ADAM_B1, ADAM_B2, ADAM_EPS, ADAM_WD, ADAM_STEP = 0.001, 0.9, 0.999, 1e-08, 0.01, 10

TM_ROWS = 512
TM_MM = 1024
TK_TN = 512
TQ = 512
TM_FFN = 1024
TN_FFN = 256
VMEM_LIMIT = 52 * 2**20


def _cp(sem, vmem=VMEM_LIMIT):
    return pltpu.CompilerParams(dimension_semantics=sem, vmem_limit_bytes=vmem)


def _bf(a):
    return a if a.dtype == BF16 else a.astype(BF16)


def _mm(mode, a_list, b_list, out_dtype, tm, tn, name, add=None):
    n_p = len(a_list)
    m_dim = a_list[0].shape[0]
    n_dim = b_list[0].shape[1 if mode == "nn" else 0]
    tm, tn = min(tm, m_dim), min(tn, n_dim)
    assert m_dim % tm == 0 and n_dim % tn == 0
    dims = (((1,), (0,)), ((), ())) if mode == "nn" else (((1,), (1,)), ((), ()))
    in_specs = [pl.BlockSpec((tm, a.shape[1]), lambda m, n: (m, 0)) for a in a_list]
    for b in b_list:
        if mode == "nn":
            in_specs.append(pl.BlockSpec((b.shape[0], tn), lambda m, n: (0, n)))
        else:
            in_specs.append(pl.BlockSpec((tn, b.shape[1]), lambda m, n: (n, 0)))
    if add is not None:
        in_specs.append(pl.BlockSpec((tm, tn), lambda m, n: (m, n)))

    def body(*refs):
        o_ref = refs[-1]
        acc = None
        for i in range(n_p):
            d = lax.dot_general(_bf(refs[i][...]), _bf(refs[n_p + i][...]), dims,
                                preferred_element_type=F32)
            acc = d if acc is None else acc + d
        if add is not None:
            acc = refs[2 * n_p][...] + acc
        o_ref[...] = acc.astype(out_dtype)

    args = list(a_list) + list(b_list) + ([add] if add is not None else [])
    return pl.pallas_call(
        body, name=name, grid=(m_dim // tm, n_dim // tn), in_specs=in_specs,
        out_specs=pl.BlockSpec((tm, tn), lambda m, n: (m, n)),
        out_shape=jax.ShapeDtypeStruct((m_dim, n_dim), out_dtype),
        compiler_params=_cp(("parallel", "parallel")))(*args)


def _mm_tn(a, b, tm, tn, name):
    k_dim, m_dim = a.shape
    n_dim = b.shape[1]
    tm, tn, tk = min(tm, m_dim), min(tn, n_dim), min(TK_TN, k_dim)
    assert m_dim % tm == 0 and n_dim % tn == 0 and k_dim % tk == 0

    def body(a_ref, b_ref, o_ref):
        @pl.when(pl.program_id(2) == 0)
        def _():
            o_ref[...] = jnp.zeros_like(o_ref)
        o_ref[...] += lax.dot_general(_bf(a_ref[...]), _bf(b_ref[...]), (((0,), (0,)), ((), ())),
                                      preferred_element_type=F32)

    return pl.pallas_call(
        body, name=name, grid=(m_dim // tm, n_dim // tn, k_dim // tk),
        in_specs=[pl.BlockSpec((tk, tm), lambda m, n, k: (k, m)), pl.BlockSpec((tk, tn), lambda m, n, k: (k, n))],
        out_specs=pl.BlockSpec((tm, tn), lambda m, n, k: (m, n)),
        out_shape=jax.ShapeDtypeStruct((m_dim, n_dim), F32),
        compiler_params=_cp(("parallel", "parallel", "arbitrary")))(a, b)


def _rstd(x):
    return lax.rsqrt(jnp.mean(x * x, axis=-1, keepdims=True) + EPS)


def _rms_fwd(x, g, name):
    s, d = x.shape
    tm = min(TM_ROWS, s)

    def body(x_ref, g_ref, h_ref):
        xv = x_ref[...]
        h_ref[...] = (xv * _rstd(xv) * g_ref[...]).astype(BF16)

    return pl.pallas_call(
        body, name=name, grid=(s // tm,),
        in_specs=[pl.BlockSpec((tm, d), lambda i: (i, 0)), pl.BlockSpec((1, d), lambda i: (0, 0))],
        out_specs=pl.BlockSpec((tm, d), lambda i: (i, 0)),
        out_shape=jax.ShapeDtypeStruct((s, d), BF16), compiler_params=_cp(("parallel",)))(x, g)


def _rms_bwd(x, dh, g, dres, name, with_bf16):
    s, d = x.shape
    tm = min(TM_ROWS, s)

    def body(x_ref, dh_ref, g_ref, dres_ref, dx_ref, *rest):
        gg_ref = rest[-1]

        @pl.when(pl.program_id(0) == 0)
        def _():
            gg_ref[...] = jnp.zeros_like(gg_ref)

        xv = x_ref[...]
        xn = xv * _rstd(xv)
        dhv = dh_ref[...]
        gg_ref[...] += jnp.sum(dhv * xn, axis=0, keepdims=True)
        t = dhv * g_ref[...]
        dx = dres_ref[...] + _rstd(xv) * (t - xn * jnp.mean(t * xn, axis=-1, keepdims=True))
        dx_ref[...] = dx
        if with_bf16:
            rest[0][...] = dx.astype(BF16)

    row = pl.BlockSpec((tm, d), lambda i: (i, 0))
    vec = pl.BlockSpec((1, d), lambda i: (0, 0))
    out_specs = [row] + ([row] if with_bf16 else []) + [vec]
    out_shape = ([jax.ShapeDtypeStruct((s, d), F32)] + ([jax.ShapeDtypeStruct((s, d), BF16)] if with_bf16 else [])
                 + [jax.ShapeDtypeStruct((1, d), F32)])
    return pl.pallas_call(
        body, name=name, grid=(s // tm,), in_specs=[row, row, vec, row], out_specs=out_specs, out_shape=out_shape,
        compiler_params=_cp(("arbitrary",)))(x, dh, g, dres)


def _loss_head(x3, target, g):
    s, d = x3.shape
    tm = min(TM_ROWS, s)

    def body(x_ref, t_ref, g_ref, dx_ref, dxb_ref, loss_ref, gg_ref):
        @pl.when(pl.program_id(0) == 0)
        def _():
            gg_ref[...] = jnp.zeros_like(gg_ref)
            loss_ref[...] = jnp.zeros_like(loss_ref)

        xv = x_ref[...]
        r = _rstd(xv)
        xn = xv * r
        gv = g_ref[...]
        err = xn * gv - t_ref[...]
        loss_ref[...] += 0.5 * jnp.sum(jnp.mean(err * err, axis=-1, keepdims=True), axis=0, keepdims=True)
        dy = err * (1.0 / d)
        gg_ref[...] += jnp.sum(dy * xn, axis=0, keepdims=True)
        t = dy * gv
        dx = r * (t - xn * jnp.mean(t * xn, axis=-1, keepdims=True))
        dx_ref[...] = dx
        dxb_ref[...] = dx.astype(BF16)

    row = pl.BlockSpec((tm, d), lambda i: (i, 0))
    vec = pl.BlockSpec((1, d), lambda i: (0, 0))
    return pl.pallas_call(
        body, name="loss_head", grid=(s // tm,), in_specs=[row, row, vec],
        out_specs=[row, row, pl.BlockSpec((1, LANES), lambda i: (0, 0)), vec],
        out_shape=[jax.ShapeDtypeStruct((s, d), F32), jax.ShapeDtypeStruct((s, d), BF16),
                   jax.ShapeDtypeStruct((1, LANES), F32), jax.ShapeDtypeStruct((1, d), F32)],
        compiler_params=_cp(("arbitrary",)))(x3, target, g)


def _prev_halo_spec(tm, width, col):
    return pl.BlockSpec((HALO, width), lambda i, *_: (jnp.maximum(i * (tm // HALO) - 1, 0), col))


def _next_halo_spec(tm, width, col, s):
    return pl.BlockSpec((HALO, width), lambda i, *_: (jnp.minimum((i + 1) * (tm // HALO), s // HALO - 1), col))


def _shift_down(x, k):
    return pltpu.roll(x, k, 0)


def _shift_up(x, k):
    return pltpu.roll(x, x.shape[0] - k, 0)


def _conv_taps(x_ext, w):
    return w[0:1, :] * _shift_down(x_ext, 2) + w[1:2, :] * _shift_down(x_ext, 1) + w[2:3, :] * x_ext


def _conv_taps_t(d_ext, w):
    return w[2:3, :] * d_ext + w[1:2, :] * _shift_up(d_ext, 1) + w[0:1, :] * _shift_up(d_ext, 2)


def _mixer_fwd(z_a, o_attn, w_conv, g_conv_out, g_attn_out):
    s = z_a.shape[0]
    c = CONV_CH
    tm = min(TM_ROWS, s)

    def body(gb_ref, gc_ref, xc_ref, gcp_ref, xcp_ref, o_ref, w_ref, gco_ref, gao_ref, mix_ref):
        i = pl.program_id(0)
        cx = gc_ref[...] * xc_ref[...]
        cx_prev = jnp.where(i > 0, gcp_ref[...] * xcp_ref[...], 0.0)
        conv = _conv_taps(jnp.concatenate([cx_prev, cx], axis=0), w_ref[...])[HALO:]
        y = gb_ref[...] * conv
        mix_ref[:, 0:c] = (y * _rstd(y) * gco_ref[...]).astype(BF16)
        o = o_ref[...]
        mix_ref[:, c:2 * c] = (o * _rstd(o) * gao_ref[...]).astype(BF16)

    col = lambda j: pl.BlockSpec((tm, c), lambda i: (i, j))
    vec = pl.BlockSpec((1, c), lambda i: (0, 0))
    return pl.pallas_call(
        body, name="mixer_fwd", grid=(s // tm,),
        in_specs=[col(0), col(1), col(2), _prev_halo_spec(tm, c, 1), _prev_halo_spec(tm, c, 2), col(0),
                  pl.BlockSpec((3, c), lambda i: (0, 0)), vec, vec],
        out_specs=pl.BlockSpec((tm, 2 * c), lambda i: (i, 0)),
        out_shape=jax.ShapeDtypeStruct((s, 2 * c), BF16),
        compiler_params=_cp(("parallel",)))(z_a, z_a, z_a, z_a, z_a, o_attn, w_conv, g_conv_out, g_attn_out)


def _mixer_bwd(z_a, o_attn, dmix, w_conv, g_conv_out, g_attn_out):
    s = z_a.shape[0]
    c = CONV_CH
    tm = min(TM_ROWS, s)
    n_blk = s // tm

    def body(gb_ref, gc_ref, xc_ref, gcp_ref, xcp_ref, gbn_ref, gcn_ref, xcn_ref, o_ref, dnc_ref, dncn_ref, dna_ref,
             w_ref, gco_ref, gao_ref, dz_ref, do_ref, delta_ref, gw_ref, ggco_ref, ggao_ref):
        i = pl.program_id(0)

        @pl.when(i == 0)
        def _():
            gw_ref[...] = jnp.zeros_like(gw_ref)
            ggco_ref[...] = jnp.zeros_like(ggco_ref)
            ggao_ref[...] = jnp.zeros_like(ggao_ref)

        w = w_ref[...]
        zeros = jnp.zeros((HALO, c), F32)
        gb_e = jnp.concatenate([zeros, gb_ref[...], gbn_ref[...]], axis=0)
        cx_prev = jnp.where(i > 0, gcp_ref[...] * xcp_ref[...], 0.0)
        gc_e = jnp.concatenate([zeros, gc_ref[...], gcn_ref[...]], axis=0)
        xc_e = jnp.concatenate([zeros, xc_ref[...], xcn_ref[...]], axis=0)
        cx_e = jnp.concatenate([cx_prev, gc_ref[...] * xc_ref[...], gcn_ref[...] * xcn_ref[...]], axis=0)
        dn_next = jnp.where(i < n_blk - 1, dncn_ref[...], 0.0)
        dn_e = jnp.concatenate([zeros, dnc_ref[...], dn_next], axis=0)

        conv_e = _conv_taps(cx_e, w)
        y_e = gb_e * conv_e
        r_e = _rstd(y_e)
        yn_e = y_e * r_e
        t_e = dn_e * gco_ref[...]
        dy_e = r_e * (t_e - yn_e * jnp.mean(t_e * yn_e, axis=-1, keepdims=True))
        dconv_e = dy_e * gb_e
        dcx_e = _conv_taps_t(dconv_e, w)
        blk = slice(HALO, HALO + tm)
        dz_ref[:, 0:c] = (dy_e * conv_e)[blk].astype(BF16)
        dz_ref[:, c:2 * c] = (dcx_e * xc_e)[blk].astype(BF16)
        dz_ref[:, 2 * c:3 * c] = (dcx_e * gc_e)[blk].astype(BF16)
        ggco_ref[...] += jnp.sum((dn_e * yn_e)[blk], axis=0, keepdims=True)
        dconv = dconv_e[blk]
        gw_ref[0:1, :] += jnp.sum(dconv * _shift_down(cx_e, 2)[blk], axis=0, keepdims=True)
        gw_ref[1:2, :] += jnp.sum(dconv * _shift_down(cx_e, 1)[blk], axis=0, keepdims=True)
        gw_ref[2:3, :] += jnp.sum(dconv * cx_e[blk], axis=0, keepdims=True)

        o = o_ref[...]
        ra = _rstd(o)
        on = o * ra
        dna = dna_ref[...]
        ggao_ref[...] += jnp.sum(dna * on, axis=0, keepdims=True)
        ta = dna * gao_ref[...]
        do = ra * (ta - on * jnp.mean(ta * on, axis=-1, keepdims=True))
        do_ref[...] = do.astype(BF16)
        prod = do * o
        head_a = lax.broadcasted_iota(jnp.int32, (tm, LANES), 1) < HEAD_DIM
        for p in range(N_PAIRS):
            pb = prod[:, p * LANES:(p + 1) * LANES]
            sa = jnp.sum(jnp.where(head_a, pb, 0.0), axis=-1, keepdims=True)
            sb = jnp.sum(jnp.where(head_a, 0.0, pb), axis=-1, keepdims=True)
            delta_ref[:, p * LANES:(p + 1) * LANES] = jnp.where(head_a, sa, sb)

    col = lambda j: pl.BlockSpec((tm, c), lambda i: (i, j))
    vec = pl.BlockSpec((1, c), lambda i: (0, 0))
    w3 = pl.BlockSpec((3, c), lambda i: (0, 0))
    return pl.pallas_call(
        body, name="mixer_bwd", grid=(n_blk,),
        in_specs=[col(0), col(1), col(2), _prev_halo_spec(tm, c, 1), _prev_halo_spec(tm, c, 2),
                  _next_halo_spec(tm, c, 0, s), _next_halo_spec(tm, c, 1, s), _next_halo_spec(tm, c, 2, s),
                  col(0), col(0), _next_halo_spec(tm, c, 0, s), col(1), w3, vec, vec],
        out_specs=[pl.BlockSpec((tm, 3 * c), lambda i: (i, 0)), col(0), col(0), w3, vec, vec],
        out_shape=[jax.ShapeDtypeStruct((s, 3 * c), BF16), jax.ShapeDtypeStruct((s, c), BF16),
                   jax.ShapeDtypeStruct((s, c), F32), jax.ShapeDtypeStruct((3, c), F32),
                   jax.ShapeDtypeStruct((1, c), F32), jax.ShapeDtypeStruct((1, c), F32)],
        compiler_params=_cp(("arbitrary",)))(
            z_a, z_a, z_a, z_a, z_a, z_a, z_a, z_a, o_attn, dmix, dmix, dmix, w_conv, g_conv_out, g_attn_out)


def _gate_fwd(f, b_pad):
    s = f.shape[0]
    tm = min(TQ, s)

    def body(f_ref, b_ref, fb_ref, ft_ref, carry):
        @pl.when(pl.program_id(0) == 0)
        def _():
            carry[...] = jnp.zeros_like(carry)

        z = f_ref[...] + b_ref[...]
        x = jnp.minimum(z, 0.0) - jnp.log1p(jnp.exp(-jnp.abs(z)))
        row = lax.broadcasted_iota(jnp.int32, (tm, LANES), 0)
        sh = 1
        while sh < tm:
            x = x + jnp.where(row >= sh, _shift_down(x, sh), 0.0)
            sh *= 2
        x = x + carry[0:1, :]
        carry[...] = jnp.broadcast_to(x[tm - 1:tm, :], carry.shape)
        head_a = lax.broadcasted_iota(jnp.int32, (tm, LANES), 1) < HEAD_DIM
        for p in range(N_PAIRS):
            fa = jnp.broadcast_to(x[:, 2 * p:2 * p + 1], (tm, LANES))
            fbv = jnp.broadcast_to(x[:, 2 * p + 1:2 * p + 2], (tm, LANES))
            fb_ref[:, p * LANES:(p + 1) * LANES] = jnp.where(head_a, fa, fbv)
        ft_ref[0] = x.T[0:N_HEADS, :]

    return pl.pallas_call(
        body, name="gate_fwd", grid=(s // tm,),
        in_specs=[pl.BlockSpec((tm, LANES), lambda i: (i, 0)), pl.BlockSpec((1, LANES), lambda i: (0, 0))],
        out_specs=[pl.BlockSpec((tm, N_PAIRS * LANES), lambda i: (i, 0)),
                   pl.BlockSpec((1, N_HEADS, tm), lambda i: (i, 0, 0))],
        out_shape=[jax.ShapeDtypeStruct((s, N_PAIRS * LANES), F32),
                   jax.ShapeDtypeStruct((s // tm, N_HEADS, tm), F32)],
        scratch_shapes=[pltpu.VMEM((HALO, LANES), F32)],
        compiler_params=_cp(("arbitrary",)))(f, b_pad)


def _gate_bwd(f, b_pad, d_f):
    s = f.shape[0]
    tm = min(TQ, s)
    n_blk = s // tm

    def body(f_ref, b_ref, d_ref, df_ref, gb_ref, carry):
        @pl.when(pl.program_id(0) == 0)
        def _():
            carry[...] = jnp.zeros_like(carry)
            gb_ref[...] = jnp.zeros_like(gb_ref)

        x = d_ref[...]
        row = lax.broadcasted_iota(jnp.int32, (tm, LANES), 0)
        sh = 1
        while sh < tm:
            x = x + jnp.where(row < tm - sh, _shift_up(x, sh), 0.0)
            sh *= 2
        x = x + carry[0:1, :]
        carry[...] = jnp.broadcast_to(x[0:1, :], carry.shape)
        z = f_ref[...] + b_ref[...]
        d = x * (1.0 / (1.0 + jnp.exp(z)))
        df_ref[...] = d.astype(BF16)
        gb_ref[...] += jnp.sum(d, axis=0, keepdims=True)

    rev = pl.BlockSpec((tm, LANES), lambda i: (n_blk - 1 - i, 0))
    vec = pl.BlockSpec((1, LANES), lambda i: (0, 0))
    return pl.pallas_call(
        body, name="gate_bwd", grid=(n_blk,), in_specs=[rev, vec, rev], out_specs=[rev, vec],
        out_shape=[jax.ShapeDtypeStruct((s, LANES), BF16), jax.ShapeDtypeStruct((1, LANES), F32)],
        scratch_shapes=[pltpu.VMEM((HALO, LANES), F32)],
        compiler_params=_cp(("arbitrary",)))(f, b_pad, d_f)


_NT = (((1,), (1,)), ((), ()))
_NN = (((1,), (0,)), ((), ()))
_TN = (((0,), (0,)), ((), ()))


def _head_masks(shape):
    lane = lax.broadcasted_iota(jnp.int32, shape, len(shape) - 1)
    return lane < HEAD_DIM


def _pick_row(ft, h):
    rows = lax.broadcasted_iota(jnp.int32, ft.shape, 0)
    return jnp.sum(jnp.where(rows == h, ft, 0.0), axis=0, keepdims=True)


def _attn_fwd(qkv, fb, ft):
    s = qkv.shape[0]
    tq = min(TQ, s)
    n_q = s // tq
    neg = -1e30

    def body(q_ref, k_ref, v_ref, fb_ref, ft_ref, o_ref, g_ref):
        p = pl.program_id(0)
        i = pl.program_id(1)
        head_a = _head_masks((tq, LANES))
        q2 = q_ref[...] * Q_SCALE
        zero = jnp.zeros_like(q2)
        q_a, q_b = jnp.where(head_a, q2, zero), jnp.where(head_a, zero, q2)
        fq2 = fb_ref[...]
        fq_a, fq_b = fq2[:, 0:1], fq2[:, HEAD_DIM:HEAD_DIM + 1]
        causal = lax.broadcasted_iota(jnp.int32, (tq, tq), 1) <= lax.broadcasted_iota(jnp.int32, (tq, tq), 0)

        def one_head(q_h, fq_h, fk_h, k2, m, l, masked):
            sc = (lax.dot_general(q_h, k2, _NT, preferred_element_type=F32) + fq_h) - fk_h
            if masked:
                sc = jnp.where(causal, sc, -jnp.inf)
            m_new = jnp.maximum(m, jnp.max(sc, axis=-1, keepdims=True))
            pr = jnp.exp(sc - m_new)
            alpha = jnp.exp(m - m_new)
            return pr.astype(BF16), m_new, alpha * l + jnp.sum(pr, axis=-1, keepdims=True), alpha

        def step(kb, carry, masked):
            m_a, l_a, m_b, l_b, acc = carry
            rows = pl.ds(pl.multiple_of(kb * tq, tq), tq)
            k2, v2 = k_ref[rows, :], v_ref[rows, :]
            ftv = ft_ref[kb]
            p_a, m_a, l_a, al_a = one_head(q_a, fq_a, _pick_row(ftv, 2 * p), k2, m_a, l_a, masked)
            p_b, m_b, l_b, al_b = one_head(q_b, fq_b, _pick_row(ftv, 2 * p + 1), k2, m_b, l_b, masked)
            zv = jnp.zeros_like(v2)
            head_k = _head_masks(v2.shape)
            pv = (lax.dot_general(p_a, jnp.where(head_k, v2, zv), _NN, preferred_element_type=F32)
                  + lax.dot_general(p_b, jnp.where(head_k, zv, v2), _NN, preferred_element_type=F32))
            acc = acc * jnp.where(head_a, al_a, al_b) + pv
            return m_a, l_a, m_b, l_b, acc

        col = lambda v: jnp.full((tq, 1), v, F32)
        carry = (col(neg), col(0.0), col(neg), col(0.0), jnp.zeros((tq, LANES), F32))
        carry = lax.fori_loop(0, i, lambda kb, cr: step(kb, cr, False), carry)
        m_a, l_a, m_b, l_b, acc = step(i, carry, True)
        o_ref[...] = acc / jnp.where(head_a, l_a, l_b)
        g_ref[...] = fq2 - jnp.where(head_a, m_a + jnp.log(l_a), m_b + jnp.log(l_b))

    qblk = lambda off: pl.BlockSpec((tq, LANES), lambda p, i: (i, off + p))
    full = lambda off: pl.BlockSpec((s, LANES), lambda p, i: (0, off + p))
    return pl.pallas_call(
        body, name="attn_fwd", grid=(N_PAIRS, n_q),
        in_specs=[qblk(0), full(N_PAIRS), full(2 * N_PAIRS), qblk(0),
                  pl.BlockSpec((n_q, N_HEADS, tq), lambda p, i: (0, 0, 0))],
        out_specs=[qblk(0), qblk(0)],
        out_shape=[jax.ShapeDtypeStruct((s, ATTN_W), F32), jax.ShapeDtypeStruct((s, ATTN_W), F32)],
        compiler_params=_cp(("parallel", "arbitrary")))(qkv, qkv, qkv, fb, ft)


def _attn_bwd(qkv, do, g, delta, ft):
    s = qkv.shape[0]
    tq = min(TQ, s)
    n_q = s // tq

    def body(q_ref, do_ref, g_ref, dl_ref, k_ref, v_ref, ft_ref, dq_ref, dfq_ref, dk_ref, dv_ref, dfk_ref):
        p = pl.program_id(0)
        i = pl.program_id(1)

        @pl.when(i == 0)
        def _():
            dk_ref[...] = jnp.zeros_like(dk_ref)
            dv_ref[...] = jnp.zeros_like(dv_ref)
            dfk_ref[...] = jnp.zeros_like(dfk_ref)

        head_a = _head_masks((tq, LANES))
        q2 = q_ref[...] * Q_SCALE
        do2 = do_ref[...]
        zero = jnp.zeros_like(q2)
        q_a, q_b = jnp.where(head_a, q2, zero), jnp.where(head_a, zero, q2)
        do_a, do_b = jnp.where(head_a, do2, zero), jnp.where(head_a, zero, do2)
        g2, dl2 = g_ref[...], dl_ref[...]
        g_a, g_b = g2[:, 0:1], g2[:, HEAD_DIM:HEAD_DIM + 1]
        dl_a, dl_b = dl2[:, 0:1], dl2[:, HEAD_DIM:HEAD_DIM + 1]
        causal = lax.broadcasted_iota(jnp.int32, (tq, tq), 1) <= lax.broadcasted_iota(jnp.int32, (tq, tq), 0)
        rows8 = lax.broadcasted_iota(jnp.int32, (N_HEADS, tq), 0)

        def one_head(q_h, do_h, g_h, dl_h, fk_h, k2, v2, masked):
            sc = (lax.dot_general(q_h, k2, _NT, preferred_element_type=F32) + g_h) - fk_h
            pr = jnp.exp(sc)
            if masked:
                pr = jnp.where(causal, pr, 0.0)
            dp = lax.dot_general(do_h, v2, _NT, preferred_element_type=F32)
            ds = pr * (dp - dl_h)
            return (pr.astype(BF16), ds.astype(BF16), jnp.sum(ds, axis=0, keepdims=True),
                    jnp.sum(ds, axis=1, keepdims=True))

        def step(j, carry, masked):
            dq, r_a, r_b = carry
            rows = pl.ds(pl.multiple_of(j * tq, tq), tq)
            k2, v2 = k_ref[rows, :], v_ref[rows, :]
            k_a, k_b = jnp.where(head_a, k2, zero), jnp.where(head_a, zero, k2)
            ftv = ft_ref[j]
            p_a, ds_a, c_a, s_a = one_head(q_a, do_a, g_a, dl_a, _pick_row(ftv, 2 * p), k2, v2, masked)
            p_b, ds_b, c_b, s_b = one_head(q_b, do_b, g_b, dl_b, _pick_row(ftv, 2 * p + 1), k2, v2, masked)
            dv_ref[rows, :] += (lax.dot_general(p_a, do_a, _TN, preferred_element_type=F32)
                                + lax.dot_general(p_b, do_b, _TN, preferred_element_type=F32))
            dk_ref[rows, :] += (lax.dot_general(ds_a, q_a, _TN, preferred_element_type=F32)
                                + lax.dot_general(ds_b, q_b, _TN, preferred_element_type=F32))
            dfk_ref[0, j] += jnp.where(rows8 == 0, -c_a, jnp.where(rows8 == 1, -c_b, 0.0))
            dq = dq + (lax.dot_general(ds_a, k_a, _NN, preferred_element_type=F32)
                       + lax.dot_general(ds_b, k_b, _NN, preferred_element_type=F32))
            return dq, r_a + s_a, r_b + s_b

        zcol = jnp.zeros((tq, 1), F32)
        carry = lax.fori_loop(0, i, lambda j, cr: step(j, cr, False), (jnp.zeros((tq, LANES), F32), zcol, zcol))
        dq, r_a, r_b = step(i, carry, True)
        dq_ref[...] = (dq * Q_SCALE).astype(BF16)
        dfq_ref[...] = jnp.where(head_a, r_a, r_b)

    qblk = lambda off: pl.BlockSpec((tq, LANES), lambda p, i: (i, off + p))
    full = lambda off: pl.BlockSpec((s, LANES), lambda p, i: (0, off + p))
    return pl.pallas_call(
        body, name="attn_bwd", grid=(N_PAIRS, n_q),
        in_specs=[qblk(0), qblk(0), qblk(0), qblk(0), full(N_PAIRS), full(2 * N_PAIRS),
                  pl.BlockSpec((n_q, N_HEADS, tq), lambda p, i: (0, 0, 0))],
        out_specs=[qblk(0), qblk(0), full(0), full(0),
                   pl.BlockSpec((1, n_q, N_HEADS, tq), lambda p, i: (p, 0, 0, 0))],
        out_shape=[jax.ShapeDtypeStruct((s, ATTN_W), BF16), jax.ShapeDtypeStruct((s, ATTN_W), F32),
                   jax.ShapeDtypeStruct((s, ATTN_W), F32), jax.ShapeDtypeStruct((s, ATTN_W), F32),
                   jax.ShapeDtypeStruct((N_PAIRS, n_q, N_HEADS, tq), F32)],
        compiler_params=_cp(("parallel", "arbitrary")))(qkv, do, g, delta, qkv, qkv, ft)


def _ffn_act_fwd(up, w_ffn):
    s = up.shape[0]
    tm, tn = min(TM_FFN, s), TN_FFN
    nb = D_FF // tn

    def body(a_ref, g_ref, ap_ref, gp_ref, wa_ref, wg_ref, act_ref):
        i = pl.program_id(1)

        def conv(blk_ref, prev_ref, w_ref):
            prev = jnp.where(i > 0, prev_ref[...], 0.0)
            return _conv_taps(jnp.concatenate([prev, blk_ref[...]], axis=0), w_ref[...])[HALO:]

        u_a, u_g = conv(a_ref, ap_ref, wa_ref), conv(g_ref, gp_ref, wg_ref)
        act_ref[...] = (u_g * (1.0 / (1.0 + jnp.exp(-u_g))) * u_a).astype(BF16)

    blk = lambda off: pl.BlockSpec((tm, tn), lambda n, i: (i, off + n))
    prev = lambda off: pl.BlockSpec((HALO, tn), lambda n, i: (jnp.maximum(i * (tm // HALO) - 1, 0), off + n))
    wsp = lambda off: pl.BlockSpec((3, tn), lambda n, i: (0, off + n))
    return pl.pallas_call(
        body, name="ffn_act_fwd", grid=(nb, s // tm),
        in_specs=[blk(0), blk(nb), prev(0), prev(nb), wsp(0), wsp(nb)],
        out_specs=pl.BlockSpec((tm, tn), lambda n, i: (i, n)),
        out_shape=jax.ShapeDtypeStruct((s, D_FF), BF16),
        compiler_params=_cp(("parallel", "parallel")))(up, up, up, up, w_ffn, w_ffn)


def _ffn_act_bwd(up, dact, w_ffn):
    s = up.shape[0]
    tm, tn = min(TM_FFN, s), TN_FFN
    nb = D_FF // tn
    n_blk = s // tm

    def body(a_ref, g_ref, ap_ref, gp_ref, an_ref, gn_ref, d_ref, dn_ref, wa_ref, wg_ref,
             da_ref, dg_ref, gwa_ref, gwg_ref):
        i = pl.program_id(1)

        @pl.when(i == 0)
        def _():
            gwa_ref[...] = jnp.zeros_like(gwa_ref)
            gwg_ref[...] = jnp.zeros_like(gwg_ref)

        def ext(blk_ref, prev_ref, next_ref):
            return jnp.concatenate([jnp.where(i > 0, prev_ref[...], 0.0), blk_ref[...], next_ref[...]], axis=0)

        wa, wg = wa_ref[...], wg_ref[...]
        up_a, up_g = ext(a_ref, ap_ref, an_ref), ext(g_ref, gp_ref, gn_ref)
        u_a, u_g = _conv_taps(up_a, wa), _conv_taps(up_g, wg)
        d_e = jnp.concatenate([jnp.zeros((HALO, tn), F32), d_ref[...],
                               jnp.where(i < n_blk - 1, dn_ref[...], 0.0)], axis=0)
        sig = 1.0 / (1.0 + jnp.exp(-u_g))
        du_a = d_e * (u_g * sig)
        du_g = d_e * u_a * (sig * (1.0 + u_g * (1.0 - sig)))
        blk = slice(HALO, HALO + tm)
        da_ref[...] = _conv_taps_t(du_a, wa)[blk].astype(BF16)
        dg_ref[...] = _conv_taps_t(du_g, wg)[blk].astype(BF16)
        for gw_ref, upv, du in ((gwa_ref, up_a[blk], du_a), (gwg_ref, up_g[blk], du_g)):
            gw_ref[0:1, :] += jnp.sum(upv * _shift_up(du, 2)[blk], axis=0, keepdims=True)
            gw_ref[1:2, :] += jnp.sum(upv * _shift_up(du, 1)[blk], axis=0, keepdims=True)
            gw_ref[2:3, :] += jnp.sum(upv * du[blk], axis=0, keepdims=True)

    blk = lambda off: pl.BlockSpec((tm, tn), lambda n, i: (i, off + n))
    prev = lambda off: pl.BlockSpec((HALO, tn), lambda n, i: (jnp.maximum(i * (tm // HALO) - 1, 0), off + n))
    nxt = lambda off: pl.BlockSpec(
        (HALO, tn), lambda n, i: (jnp.minimum((i + 1) * (tm // HALO), s // HALO - 1), off + n))
    wsp = lambda off: pl.BlockSpec((3, tn), lambda n, i: (0, off + n))
    return pl.pallas_call(
        body, name="ffn_act_bwd", grid=(nb, n_blk),
        in_specs=[blk(0), blk(nb), prev(0), prev(nb), nxt(0), nxt(nb), blk(0), nxt(0), wsp(0), wsp(nb)],
        out_specs=[blk(0), blk(0), wsp(0), wsp(0)],
        out_shape=[jax.ShapeDtypeStruct((s, D_FF), BF16), jax.ShapeDtypeStruct((s, D_FF), BF16),
                   jax.ShapeDtypeStruct((3, D_FF), F32), jax.ShapeDtypeStruct((3, D_FF), F32)],
        compiler_params=_cp(("parallel", "arbitrary")))(up, up, up, up, up, up, dact, dact, w_ffn, w_ffn)


def _adamw(w, g, m, v, name):
    r, c = w.shape
    tr = next((t for t in (512, 352, 256, 128, 64, 32, 16, 8) if r > t and r % t == 0), r)

    def body(w_ref, g_ref, m_ref, v_ref, d_ref, nm_ref, nv_ref):
        gv = g_ref[...]
        m_new = ADAM_B1 * m_ref[...] + (1.0 - ADAM_B1) * gv
        v_new = ADAM_B2 * v_ref[...] + (1.0 - ADAM_B2) * (gv * gv)
        m_hat = m_new / (1.0 - ADAM_B1 ** ADAM_STEP)
        v_hat = v_new / (1.0 - ADAM_B2 ** ADAM_STEP)
        d_ref[...] = -ADAM_LR * (m_hat / (jnp.sqrt(v_hat) + ADAM_EPS) + ADAM_WD * w_ref[...])
        nm_ref[...] = m_new
        nv_ref[...] = v_new

    spec = pl.BlockSpec((tr, c), lambda i: (i, 0))
    shp = jax.ShapeDtypeStruct((r, c), F32)
    return pl.pallas_call(
        body, name=name, grid=(r // tr,), in_specs=[spec] * 4, out_specs=[spec] * 3, out_shape=[shp] * 3,
        compiler_params=_cp(("parallel",)))(w, g, m, v)


def _sum_rows_block(h):
    return h // 4 if h % (4 * HALO) == 0 else h


def _pair_sum(parts, from_sibling, c):
    _, n, h, _ = parts.shape
    tr = _sum_rows_block(h)

    def body(sel_ref, a_ref, b_ref, o_ref):
        o_ref[...] = a_ref[0] + b_ref[...]

    blk = pl.BlockSpec((1, tr, LANES), lambda j, i, sel_ref: (j, i, 0))
    grid_spec = pltpu.PrefetchScalarGridSpec(
        num_scalar_prefetch=1, grid=(n, h // tr),
        in_specs=[pl.BlockSpec((1, 1, tr, LANES), lambda j, i, sel_ref: (sel_ref[0], j, i, 0)), blk], out_specs=blk)
    return pl.pallas_call(
        body, name="pair_sum", grid_spec=grid_spec, out_shape=jax.ShapeDtypeStruct((n, h, LANES), F32),
        compiler_params=_cp(("parallel", "parallel")))(jnp.reshape(c, (1,)).astype(jnp.int32), parts, from_sibling)


def _chip_sum(pair, got, my_chip):
    _, h, _ = pair.shape
    tr = _sum_rows_block(h)

    def body(sel_ref, p_ref, g0_ref, g1_ref, g2_ref, o_ref):
        o_ref[...] = ((p_ref[...] + g0_ref[...]) + g1_ref[...]) + g2_ref[...]

    slot = lambda k: pl.BlockSpec((1, tr, LANES), lambda i, sel_ref: (k, i, 0))
    grid_spec = pltpu.PrefetchScalarGridSpec(
        num_scalar_prefetch=1, grid=(h // tr,),
        in_specs=[pl.BlockSpec((1, tr, LANES), lambda i, sel_ref: (sel_ref[0], i, 0)), slot(0), slot(1), slot(2)],
        out_specs=slot(0))
    return pl.pallas_call(
        body, name="chip_sum", grid_spec=grid_spec, out_shape=jax.ShapeDtypeStruct((1, h, LANES), F32),
        compiler_params=_cp(("parallel",)))(jnp.reshape(my_chip, (1,)).astype(jnp.int32), pair, got, got, got)


def _place():
    return lax.axis_index("x"), lax.axis_index("y"), lax.axis_index("c")


def _other_chips(x, y):
    return [(1 - x, y), (x, 1 - y), (1 - x, 1 - y)]


def _all_gather_chips(shard):
    r = shard.shape[0]
    h = r // 2

    def body(src_ref, out_ref, send_sems, recv_sems, local_sem):
        x, y, c = _place()
        my_chip = 2 * x + y
        chips = _other_chips(x, y)

        def half(chip, which):
            return out_ref.at[chip, pl.ds(which * h, h), :]

        def copy(k, chip, which, to, src=None):
            return pltpu.make_async_remote_copy(
                src_ref=half(chip, which) if src is None else src, dst_ref=half(chip, which),
                send_sem=send_sems.at[k], recv_sem=recv_sems.at[k], device_id=to, device_id_type=MESH)

        mine = pltpu.make_async_copy(src_ref, out_ref.at[my_chip], local_sem)
        mine.start()
        first = [copy(j, my_chip, c, (cx, cy, c), src=src_ref.at[pl.ds(c * h, h), :])
                 for j, (cx, cy) in enumerate(chips)]
        for cp in first:
            cp.start()
        passed = [copy(3 + j, 2 * cx + cy, c, (x, y, 1 - c)) for j, (cx, cy) in enumerate(chips)]
        for j, (cx, cy) in enumerate(chips):
            copy(j, 2 * cx + cy, c, (x, y, c)).wait_recv()
            passed[j].start()
        for j, (cx, cy) in enumerate(chips):
            copy(3 + j, 2 * cx + cy, 1 - c, (x, y, c)).wait_recv()
        for cp in first + passed:
            cp.wait_send()
        mine.wait()

    return pl.pallas_call(
        body, name="all_gather_weights", out_shape=jax.ShapeDtypeStruct((N_CHIPS, r, LANES), shard.dtype),
        in_specs=[pl.BlockSpec(memory_space=pl.ANY)], out_specs=pl.BlockSpec(memory_space=pl.ANY),
        scratch_shapes=[pltpu.SemaphoreType.DMA((6,)), pltpu.SemaphoreType.DMA((6,)), pltpu.SemaphoreType.DMA])(shard)


def _send_other_half(parts):
    _, n, h, _ = parts.shape

    def body(a_ref, out_ref, send_sem, recv_sem):
        x, y, c = _place()
        cp = pltpu.make_async_remote_copy(
            src_ref=a_ref.at[1 - c], dst_ref=out_ref, send_sem=send_sem, recv_sem=recv_sem,
            device_id=(x, y, 1 - c), device_id_type=MESH)
        cp.start()
        cp.wait()

    return pl.pallas_call(
        body, name="pair_exchange", out_shape=jax.ShapeDtypeStruct((n, h, LANES), parts.dtype),
        in_specs=[pl.BlockSpec(memory_space=pl.ANY)], out_specs=pl.BlockSpec(memory_space=pl.ANY),
        scratch_shapes=[pltpu.SemaphoreType.DMA, pltpu.SemaphoreType.DMA])(parts)


def _join_halves(mine):
    _, h, _ = mine.shape

    def body(a_ref, out_ref, send_sem, recv_sem, local_sem):
        x, y, c = _place()
        own = pltpu.make_async_copy(a_ref, out_ref.at[pl.ds(c, 1)], local_sem)
        own.start()
        cp = pltpu.make_async_remote_copy(
            src_ref=a_ref, dst_ref=out_ref.at[pl.ds(c, 1)], send_sem=send_sem, recv_sem=recv_sem,
            device_id=(x, y, 1 - c), device_id_type=MESH)
        cp.start()
        cp.wait_send()
        pltpu.make_async_remote_copy(
            src_ref=a_ref, dst_ref=out_ref.at[pl.ds(1 - c, 1)], send_sem=send_sem, recv_sem=recv_sem,
            device_id=(x, y, 1 - c), device_id_type=MESH).wait_recv()
        own.wait()

    return pl.pallas_call(
        body, name="half_exchange", out_shape=jax.ShapeDtypeStruct((2, h, LANES), mine.dtype),
        in_specs=[pl.BlockSpec(memory_space=pl.ANY)], out_specs=pl.BlockSpec(memory_space=pl.ANY),
        scratch_shapes=[pltpu.SemaphoreType.DMA, pltpu.SemaphoreType.DMA, pltpu.SemaphoreType.DMA])(mine)


def _scatter_to_chips(part):
    _, h, _ = part.shape

    def body(p_ref, out_ref, send_sems, recv_sems):
        x, y, c = _place()
        copies = [pltpu.make_async_remote_copy(
            src_ref=p_ref.at[2 * cx + cy], dst_ref=out_ref.at[k], send_sem=send_sems.at[k], recv_sem=recv_sems.at[k],
            device_id=(cx, cy, c), device_id_type=MESH) for k, (cx, cy) in enumerate(_other_chips(x, y))]
        for cp in copies:
            cp.start()
        for cp in copies:
            cp.wait()

    return pl.pallas_call(
        body, name="scatter_grads", out_shape=jax.ShapeDtypeStruct((3, h, LANES), part.dtype),
        in_specs=[pl.BlockSpec(memory_space=pl.ANY)], out_specs=pl.BlockSpec(memory_space=pl.ANY),
        scratch_shapes=[pltpu.SemaphoreType.DMA((3,)), pltpu.SemaphoreType.DMA((3,))])(part)


def _all_reduce_small(packet):
    rows, width = packet.shape
    n_dev = 8

    def body(x_ref, out_ref, gath, send_sems, recv_sems):
        x, y, c = _place()
        me, sibling = (x, y, c), (x, y, 1 - c)
        chips = _other_chips(x, y)

        def slot(px, py, pc):
            return gath.at[pl.ds((4 * px + 2 * py + pc) * rows, rows), :]

        def copy(k, block, to, src=None):
            return pltpu.make_async_remote_copy(
                src_ref=slot(*block) if src is None else src, dst_ref=slot(*block), send_sem=send_sems.at[k],
                recv_sem=recv_sems.at[k], device_id=to, device_id_type=MESH)

        first = [copy(0, me, sibling, src=x_ref)]
        first += [copy(1 + j, me, (*chip, c), src=x_ref) for j, chip in enumerate(chips)]
        for cp in first:
            cp.start()
        gath[pl.ds((4 * x + 2 * y + c) * rows, rows), :] = x_ref[...]
        passed = [copy(4 + j, (*chip, c), sibling) for j, chip in enumerate(chips)]
        for j, chip in enumerate(chips):
            copy(1 + j, (*chip, c), me).wait_recv()
            passed[j].start()
        copy(0, sibling, me).wait_recv()
        for j, chip in enumerate(chips):
            copy(4 + j, (*chip, 1 - c), me).wait_recv()
        for cp in first + passed:
            cp.wait_send()
        acc = gath[0:rows, :]
        for d in range(1, n_dev):
            acc = acc + gath[d * rows:(d + 1) * rows, :]
        out_ref[...] = acc

    return pl.pallas_call(
        body, name="all_reduce_small", out_shape=jax.ShapeDtypeStruct((rows, width), F32),
        in_specs=[pl.BlockSpec(memory_space=pltpu.VMEM)], out_specs=pl.BlockSpec(memory_space=pltpu.VMEM),
        scratch_shapes=[pltpu.VMEM((n_dev * rows, width), F32), pltpu.SemaphoreType.DMA((7,)),
                        pltpu.SemaphoreType.DMA((7,))])(packet)


def _flat_rows(parts, dtype, row_multiple):
    flat = jnp.concatenate([p.astype(dtype).reshape(-1) for p in parts])
    rows = -(-flat.shape[0] // LANES)
    rows = -(-rows // row_multiple) * row_multiple
    return jnp.pad(flat, (0, rows * LANES - flat.shape[0])).reshape(rows, LANES)


def _unflatten(flat2d, shapes):
    flat = flat2d.reshape(-1)
    out, off = [], 0
    for shp in shapes:
        n = 1
        for dim in shp:
            n *= dim
        out.append(flat[off:off + n].reshape(shp))
        off += n
    return out


def _reduce_scatter_grads(parts):
    x, y, c = _place()
    pair = _pair_sum(parts, _send_other_half(parts), c)
    total = _chip_sum(pair, _scatter_to_chips(pair), 2 * x + y)
    return _join_halves(total).reshape(-1, LANES)


def kernel(x, g_mix, w_in, b_f, w_conv, g_conv_out, g_attn_out, w_o, g_ffn, w_up, w_ffn_conv, w_down, g_final, loss_target, m_g_mix, m_w_in, m_b_f, m_w_conv, m_g_conv_out, m_g_attn_out, m_w_o, m_g_ffn, m_w_up, m_w_ffn_conv, m_w_down, m_g_final, v_g_mix, v_w_in, v_b_f, v_w_conv, v_g_conv_out, v_g_attn_out, v_w_o, v_g_ffn, v_w_up, v_w_ffn_conv, v_w_down, v_g_final):
    s = x.shape[1]
    x0 = x[0]
    target = loss_target[0]
    d = D_MODEL
    sharded = [w_in[0], w_o[0], w_up[0], w_down[0]]
    small_sharded = [w_conv[0], w_ffn_conv[0]]
    shard_shapes = [a.shape for a in sharded]
    small_shapes = [a.shape for a in small_sharded]

    small_bits = lax.bitcast_convert_type(jnp.concatenate([a.reshape(-1) for a in small_sharded]), BF16).reshape(-1)
    n_small = small_bits.shape[0]
    gathered = _all_gather_chips(_flat_rows(sharded + [small_bits], BF16, 32))
    per_chip = [_unflatten(gathered[j], shard_shapes + [(n_small,)]) for j in range(N_CHIPS)]
    w_in_full = jnp.concatenate([pc[0] for pc in per_chip], axis=1)
    w_o_full = jnp.concatenate([pc[1] for pc in per_chip], axis=0)
    w_up_full = jnp.concatenate([pc[2] for pc in per_chip], axis=1)
    w_down_full = jnp.concatenate([pc[3] for pc in per_chip], axis=0)
    smalls = [_unflatten(lax.bitcast_convert_type(pc[4].reshape(-1, 2), F32), small_shapes) for pc in per_chip]
    w_conv_full = jnp.concatenate([sm[0] for sm in smalls], axis=1)
    w_ffn_full = jnp.concatenate([sm[1] for sm in smalls], axis=1)
    c3 = 3 * CONV_CH
    w_a, w_b = w_in_full[:, :c3], w_in_full[:, c3:c3 + 3 * ATTN_W]
    w_c = jnp.pad(w_in_full[:, c3 + 3 * ATTN_W:], ((0, 0), (0, LANES - N_HEADS)))
    w_q, w_k, w_v = (w_b[:, i * ATTN_W:(i + 1) * ATTN_W] for i in range(3))
    w_up_lin, w_up_gate = w_up_full[:, :D_FF], w_up_full[:, D_FF:]
    b_pad = jnp.pad(b_f, ((0, 0), (0, LANES - N_HEADS)))

    h1 = _rms_fwd(x0, g_mix, "rms_mix")
    z_a = _mm("nn", [h1], [w_a], F32, TM_MM, 512, "in_proj_conv")
    qkv = _mm("nn", [h1], [w_b], BF16, TM_MM, 512, "in_proj_qkv")
    f_log = _mm("nn", [h1], [w_c], F32, TM_MM, LANES, "in_proj_gate")
    fb, ft = _gate_fwd(f_log, b_pad)
    o_attn, g_att = _attn_fwd(qkv, fb, ft)
    mix = _mixer_fwd(z_a, o_attn, w_conv_full, g_conv_out, g_attn_out)
    x2 = _mm("nn", [mix], [w_o_full], F32, TM_MM, 512, "out_proj", add=x0)
    h2 = _rms_fwd(x2, g_ffn, "rms_ffn")
    up = _mm("nn", [h2], [w_up_full], F32, TM_MM, 512, "up_proj")
    act = _ffn_act_fwd(up, w_ffn_full)
    x3 = _mm("nn", [act], [w_down_full], F32, 512, 512, "down_proj", add=x2)

    dx3, dx3_b, loss_row, gg_final = _loss_head(x3, target, g_final.reshape(1, d))
    dact = _mm("nt", [dx3_b], [w_down_full], F32, TM_MM, 1408, "d_act")
    gw_down = _mm_tn(act, dx3_b, 1408, 1024, "gw_down")
    dup_lin, dup_gate, gwf_lin, gwf_gate = _ffn_act_bwd(up, dact, w_ffn_full)
    dh2 = _mm("nt", [dup_lin, dup_gate], [w_up_lin, w_up_gate], F32, 512, 512, "d_h2")
    gw_up_lin = _mm_tn(h2, dup_lin, 1024, 1408, "gw_up_lin")
    gw_up_gate = _mm_tn(h2, dup_gate, 1024, 1408, "gw_up_gate")
    dx2, dx2_b, gg_ffn = _rms_bwd(x2, dh2, g_ffn, dx3, "rms_ffn_bwd", True)
    dmix = _mm("nt", [dx2_b], [w_o_full], F32, TM_MM, 512, "d_mix")
    gw_o = _mm_tn(mix, dx2_b, 1024, 1024, "gw_o")
    dz_a, d_o, delta, gw_conv, gg_conv_out, gg_attn_out = _mixer_bwd(z_a, o_attn, dmix, w_conv_full, g_conv_out,
                                                                     g_attn_out)
    dq, dfq, dk, dv, dfk = _attn_bwd(qkv, d_o, g_att, delta, ft)
    d_f = jnp.transpose(dfk[:, :, 0:2, :], (1, 3, 0, 2)).reshape(s, N_HEADS) + dfq[:, ::HEAD_DIM]
    df_b, gb_f = _gate_bwd(f_log, b_pad, jnp.pad(d_f, ((0, 0), (0, LANES - N_HEADS))))
    dh1 = _mm("nt", [dz_a, dq, dk, dv, df_b], [w_a, w_q, w_k, w_v, w_c], F32, TM_MM, 512, "d_h1")
    gw_a = _mm_tn(h1, dz_a, 1024, c3, "gw_in_conv")
    gw_q = _mm_tn(h1, dq, 1024, ATTN_W, "gw_in_q")
    gw_k = _mm_tn(h1, dk, 1024, ATTN_W, "gw_in_k")
    gw_v = _mm_tn(h1, dv, 1024, ATTN_W, "gw_in_v")
    gw_c = _mm_tn(h1, df_b, 1024, LANES, "gw_in_gate")
    grad_x, gg_mix = _rms_bwd(x0, dh1, g_mix, dx2, "rms_mix_bwd", False)

    gw_in = jnp.concatenate([gw_a, gw_q, gw_k, gw_v, gw_c[:, :N_HEADS]], axis=1)
    gw_up = jnp.concatenate([gw_up_lin, gw_up_gate], axis=1)
    gw_ffn = jnp.concatenate([gwf_lin, gwf_gate], axis=1)
    n_in, n_up, n_cv, n_ff = IN_COLS // N_CHIPS, 2 * D_FF // N_CHIPS, CONV_CH // N_CHIPS, 2 * D_FF // N_CHIPS
    r_o, r_dn = d // N_CHIPS, D_FF // N_CHIPS
    chip_parts = [_flat_rows(
        [gw_in[:, j * n_in:(j + 1) * n_in], gw_o[j * r_o:(j + 1) * r_o], gw_up[:, j * n_up:(j + 1) * n_up],
         gw_down[j * r_dn:(j + 1) * r_dn], gw_conv[:, j * n_cv:(j + 1) * n_cv], gw_ffn[:, j * n_ff:(j + 1) * n_ff]],
        F32, 8 * HALO) for j in range(N_CHIPS)]
    reduced = _reduce_scatter_grads(jnp.stack([cp.reshape(2, -1, LANES) for cp in chip_parts], axis=1))
    g_w_in, g_w_o, g_w_up, g_w_down, g_w_conv, g_w_ffn = _unflatten(reduced, shard_shapes + small_shapes)

    half = CONV_CH
    packet = jnp.zeros((8, d), F32)
    packet = packet.at[0].set(gg_mix[0]).at[1, :half].set(gg_conv_out[0]).at[1, half:].set(gg_attn_out[0])
    packet = packet.at[2].set(gg_ffn[0]).at[3].set(gg_final[0]).at[4, :N_HEADS].set(gb_f[0, :N_HEADS])
    packet = packet.at[4, LANES].set(loss_row[0, 0])
    tot = _all_reduce_small(packet)
    loss = tot[4, LANES]
    g_g_mix, g_g_conv_out, g_g_attn_out = tot[0:1], tot[1:2, :half], tot[1:2, half:]
    g_g_ffn, g_g_final, g_b_f = tot[2:3], tot[3], tot[4:5, :N_HEADS]

    def adam_big(w, g, m, v, name):
        dl, nm, nv = _adamw(w[0], g, m[0], v[0], name)
        return dl[None], nm[None], nv[None]

    u_w_in = adam_big(w_in, g_w_in, m_w_in, v_w_in, "adam_w_in")
    u_w_o = adam_big(w_o, g_w_o, m_w_o, v_w_o, "adam_w_o")
    u_w_up = adam_big(w_up, g_w_up, m_w_up, v_w_up, "adam_w_up")
    u_w_down = adam_big(w_down, g_w_down, m_w_down, v_w_down, "adam_w_down")

    small_w = [g_mix, b_f, g_conv_out, g_attn_out, g_ffn, g_final, w_conv, w_ffn_conv]
    small_g = [g_g_mix, g_b_f, g_g_conv_out, g_g_attn_out, g_g_ffn, g_g_final, g_w_conv, g_w_ffn]
    small_m = [m_g_mix, m_b_f, m_g_conv_out, m_g_attn_out, m_g_ffn, m_g_final, m_w_conv, m_w_ffn_conv]
    small_v = [v_g_mix, v_b_f, v_g_conv_out, v_g_attn_out, v_g_ffn, v_g_final, v_w_conv, v_w_ffn_conv]
    shapes = [a.shape for a in small_w]
    pack = lambda arrs: _flat_rows(arrs, F32, 8)
    sd, sm, sv = _adamw(pack(small_w), pack(small_g), pack(small_m), pack(small_v), "adam_small")
    sd, sm, sv = _unflatten(sd, shapes), _unflatten(sm, shapes), _unflatten(sv, shapes)
    (d_g_mix, d_b_f, d_g_conv_out, d_g_attn_out, d_g_ffn, d_g_final, d_w_conv, d_w_ffn) = sd
    (nm_g_mix, nm_b_f, nm_g_conv_out, nm_g_attn_out, nm_g_ffn, nm_g_final, nm_w_conv, nm_w_ffn) = sm
    (nv_g_mix, nv_b_f, nv_g_conv_out, nv_g_attn_out, nv_g_ffn, nv_g_final, nv_w_conv, nv_w_ffn) = sv

    grads = (g_g_mix, g_w_in[None], g_b_f, g_w_conv[None], g_g_conv_out, g_g_attn_out, g_w_o[None], g_g_ffn,
             g_w_up[None], g_w_ffn[None], g_w_down[None], g_g_final)
    deltas = (d_g_mix, u_w_in[0], d_b_f, d_w_conv, d_g_conv_out, d_g_attn_out, u_w_o[0], d_g_ffn, u_w_up[0],
              d_w_ffn, u_w_down[0], d_g_final)
    new_m = (nm_g_mix, u_w_in[1], nm_b_f, nm_w_conv, nm_g_conv_out, nm_g_attn_out, u_w_o[1], nm_g_ffn, u_w_up[1],
             nm_w_ffn, u_w_down[1], nm_g_final)
    new_v = (nv_g_mix, u_w_in[2], nv_b_f, nv_w_conv, nv_g_conv_out, nv_g_attn_out, u_w_o[2], nv_g_ffn, u_w_up[2],
             nv_w_ffn, u_w_down[2], nv_g_final)
    return (loss, grad_x[None], *grads, *deltas, *new_m, *new_v)
```

```python
import functools

import jax
import jax.numpy as jnp
from jax import lax
from jax.experimental import pallas as pl
from jax.experimental.pallas import tpu as pltpu

F32, BF16 = jnp.float32, jnp.bfloat16
MESH = pl.DeviceIdType.MESH

D_MODEL = 1024
CONV_CH = 512
ATTN_W = 512
N_HEADS = 8
HEAD_DIM = 64
N_PAIRS = N_HEADS // 2
D_FF = 2816
IN_COLS = 3 * CONV_CH + 3 * ATTN_W + N_HEADS
EPS = 1e-6
Q_SCALE = 0.125
EXP_ZERO = 104.0
N_CHIPS = 4
LANES = 128
HALO = 8

ADAM_LR, ADAM_B1, ADAM_B2, ADAM_EPS, ADAM_WD, ADAM_STEP = 0.001, 0.9, 0.999, 1e-08, 0.01, 10

TM_ROWS = 512
TM_MM = 1024
TK_TN = 512
TQ = 512
TM_FFN = 1024
TN_FFN = 256
VMEM_LIMIT = 52 * 2**20


def _cp(sem, vmem=VMEM_LIMIT):
    return pltpu.CompilerParams(dimension_semantics=sem, vmem_limit_bytes=vmem)


def _bf(a):
    return a if a.dtype == BF16 else a.astype(BF16)


def _mm(mode, a_list, b_list, out_dtype, tm, tn, name, add=None):
    n_p = len(a_list)
    m_dim = a_list[0].shape[0]
    n_dim = b_list[0].shape[1 if mode == "nn" else 0]
    tm, tn = min(tm, m_dim), min(tn, n_dim)
    assert m_dim % tm == 0 and n_dim % tn == 0
    dims = (((1,), (0,)), ((), ())) if mode == "nn" else (((1,), (1,)), ((), ()))
    in_specs = [pl.BlockSpec((tm, a.shape[1]), lambda m, n: (m, 0)) for a in a_list]
    for b in b_list:
        if mode == "nn":
            in_specs.append(pl.BlockSpec((b.shape[0], tn), lambda m, n: (0, n)))
        else:
            in_specs.append(pl.BlockSpec((tn, b.shape[1]), lambda m, n: (n, 0)))
    if add is not None:
        in_specs.append(pl.BlockSpec((tm, tn), lambda m, n: (m, n)))

    def body(*refs):
        o_ref = refs[-1]
        acc = None
        for i in range(n_p):
            d = lax.dot_general(_bf(refs[i][...]), _bf(refs[n_p + i][...]), dims,
                                preferred_element_type=F32)
            acc = d if acc is None else acc + d
        if add is not None:
            acc = refs[2 * n_p][...] + acc
        o_ref[...] = acc.astype(out_dtype)

    args = list(a_list) + list(b_list) + ([add] if add is not None else [])
    return pl.pallas_call(
        body, name=name, grid=(m_dim // tm, n_dim // tn), in_specs=in_specs,
        out_specs=pl.BlockSpec((tm, tn), lambda m, n: (m, n)),
        out_shape=jax.ShapeDtypeStruct((m_dim, n_dim), out_dtype),
        compiler_params=_cp(("parallel", "parallel")))(*args)


def _mm_tn(a, b, tm, tn, name):
    k_dim, m_dim = a.shape
    n_dim = b.shape[1]
    tm, tn, tk = min(tm, m_dim), min(tn, n_dim), min(TK_TN, k_dim)
    assert m_dim % tm == 0 and n_dim % tn == 0 and k_dim % tk == 0

    def body(a_ref, b_ref, o_ref):
        @pl.when(pl.program_id(2) == 0)
        def _():
            o_ref[...] = jnp.zeros_like(o_ref)
        o_ref[...] += lax.dot_general(_bf(a_ref[...]), _bf(b_ref[...]), (((0,), (0,)), ((), ())),
                                      preferred_element_type=F32)

    return pl.pallas_call(
        body, name=name, grid=(m_dim // tm, n_dim // tn, k_dim // tk),
        in_specs=[pl.BlockSpec((tk, tm), lambda m, n, k: (k, m)), pl.BlockSpec((tk, tn), lambda m, n, k: (k, n))],
        out_specs=pl.BlockSpec((tm, tn), lambda m, n, k: (m, n)),
        out_shape=jax.ShapeDtypeStruct((m_dim, n_dim), F32),
        compiler_params=_cp(("parallel", "parallel", "arbitrary")))(a, b)


def _rstd(x):
    return lax.rsqrt(jnp.mean(x * x, axis=-1, keepdims=True) + EPS)


def _rms_fwd(x, g, name):
    s, d = x.shape
    tm = min(TM_ROWS, s)

    def body(x_ref, g_ref, h_ref):
        xv = x_ref[...]
        h_ref[...] = (xv * _rstd(xv) * g_ref[...]).astype(BF16)

    return pl.pallas_call(
        body, name=name, grid=(s // tm,),
        in_specs=[pl.BlockSpec((tm, d), lambda i: (i, 0)), pl.BlockSpec((1, d), lambda i: (0, 0))],
        out_specs=pl.BlockSpec((tm, d), lambda i: (i, 0)),
        out_shape=jax.ShapeDtypeStruct((s, d), BF16), compiler_params=_cp(("parallel",)))(x, g)


def _rms_bwd(x, dh, g, dres, name, with_bf16):
    s, d = x.shape
    tm = min(TM_ROWS, s)

    def body(x_ref, dh_ref, g_ref, dres_ref, dx_ref, *rest):
        gg_ref = rest[-1]

        @pl.when(pl.program_id(0) == 0)
        def _():
            gg_ref[...] = jnp.zeros_like(gg_ref)

        xv = x_ref[...]
        xn = xv * _rstd(xv)
        dhv = dh_ref[...]
        gg_ref[...] += jnp.sum(dhv * xn, axis=0, keepdims=True)
        t = dhv * g_ref[...]
        dx = dres_ref[...] + _rstd(xv) * (t - xn * jnp.mean(t * xn, axis=-1, keepdims=True))
        dx_ref[...] = dx
        if with_bf16:
            rest[0][...] = dx.astype(BF16)

    row = pl.BlockSpec((tm, d), lambda i: (i, 0))
    vec = pl.BlockSpec((1, d), lambda i: (0, 0))
    out_specs = [row] + ([row] if with_bf16 else []) + [vec]
    out_shape = ([jax.ShapeDtypeStruct((s, d), F32)] + ([jax.ShapeDtypeStruct((s, d), BF16)] if with_bf16 else [])
                 + [jax.ShapeDtypeStruct((1, d), F32)])
    return pl.pallas_call(
        body, name=name, grid=(s // tm,), in_specs=[row, row, vec, row], out_specs=out_specs, out_shape=out_shape,
        compiler_params=_cp(("arbitrary",)))(x, dh, g, dres)


def _loss_head(x3, target, g):
    s, d = x3.shape
    tm = min(TM_ROWS, s)

    def body(x_ref, t_ref, g_ref, dx_ref, dxb_ref, loss_ref, gg_ref):
        @pl.when(pl.program_id(0) == 0)
        def _():
            gg_ref[...] = jnp.zeros_like(gg_ref)
            loss_ref[...] = jnp.zeros_like(loss_ref)

        xv = x_ref[...]
        r = _rstd(xv)
        xn = xv * r
        gv = g_ref[...]
        err = xn * gv - t_ref[...]
        loss_ref[...] += 0.5 * jnp.sum(jnp.mean(err * err, axis=-1, keepdims=True), axis=0, keepdims=True)
        dy = err * (1.0 / d)
        gg_ref[...] += jnp.sum(dy * xn, axis=0, keepdims=True)
        t = dy * gv
        dx = r * (t - xn * jnp.mean(t * xn, axis=-1, keepdims=True))
        dx_ref[...] = dx
        dxb_ref[...] = dx.astype(BF16)

    row = pl.BlockSpec((tm, d), lambda i: (i, 0))
    vec = pl.BlockSpec((1, d), lambda i: (0, 0))
    return pl.pallas_call(
        body, name="loss_head", grid=(s // tm,), in_specs=[row, row, vec],
        out_specs=[row, row, pl.BlockSpec((1, LANES), lambda i: (0, 0)), vec],
        out_shape=[jax.ShapeDtypeStruct((s, d), F32), jax.ShapeDtypeStruct((s, d), BF16),
                   jax.ShapeDtypeStruct((1, LANES), F32), jax.ShapeDtypeStruct((1, d), F32)],
        compiler_params=_cp(("arbitrary",)))(x3, target, g)


def _prev_halo_spec(tm, width, col):
    return pl.BlockSpec((HALO, width), lambda i, *_: (jnp.maximum(i * (tm // HALO) - 1, 0), col))


def _next_halo_spec(tm, width, col, s):
    return pl.BlockSpec((HALO, width), lambda i, *_: (jnp.minimum((i + 1) * (tm // HALO), s // HALO - 1), col))


def _shift_down(x, k):
    return pltpu.roll(x, k, 0)


def _shift_up(x, k):
    return pltpu.roll(x, x.shape[0] - k, 0)


def _conv_taps(x_ext, w):
    return w[0:1, :] * _shift_down(x_ext, 2) + w[1:2, :] * _shift_down(x_ext, 1) + w[2:3, :] * x_ext


def _conv_taps_t(d_ext, w):
    return w[2:3, :] * d_ext + w[1:2, :] * _shift_up(d_ext, 1) + w[0:1, :] * _shift_up(d_ext, 2)


def _mixer_fwd(z_a, o_attn, w_conv, g_conv_out, g_attn_out):
    s = z_a.shape[0]
    c = CONV_CH
    tm = min(TM_ROWS, s)

    def body(gb_ref, gc_ref, xc_ref, gcp_ref, xcp_ref, o_ref, w_ref, gco_ref, gao_ref, mix_ref):
        i = pl.program_id(0)
        cx = gc_ref[...] * xc_ref[...]
        cx_prev = jnp.where(i > 0, gcp_ref[...] * xcp_ref[...], 0.0)
        conv = _conv_taps(jnp.concatenate([cx_prev, cx], axis=0), w_ref[...])[HALO:]
        y = gb_ref[...] * conv
        mix_ref[:, 0:c] = (y * _rstd(y) * gco_ref[...]).astype(BF16)
        o = o_ref[...]
        mix_ref[:, c:2 * c] = (o * _rstd(o) * gao_ref[...]).astype(BF16)

    col = lambda j: pl.BlockSpec((tm, c), lambda i: (i, j))
    vec = pl.BlockSpec((1, c), lambda i: (0, 0))
    return pl.pallas_call(
        body, name="mixer_fwd", grid=(s // tm,),
        in_specs=[col(0), col(1), col(2), _prev_halo_spec(tm, c, 1), _prev_halo_spec(tm, c, 2), col(0),
                  pl.BlockSpec((3, c), lambda i: (0, 0)), vec, vec],
        out_specs=pl.BlockSpec((tm, 2 * c), lambda i: (i, 0)),
        out_shape=jax.ShapeDtypeStruct((s, 2 * c), BF16),
        compiler_params=_cp(("parallel",)))(z_a, z_a, z_a, z_a, z_a, o_attn, w_conv, g_conv_out, g_attn_out)


def _mixer_bwd(z_a, o_attn, dmix, w_conv, g_conv_out, g_attn_out):
    s = z_a.shape[0]
    c = CONV_CH
    tm = min(TM_ROWS, s)
    n_blk = s // tm

    def body(gb_ref, gc_ref, xc_ref, gcp_ref, xcp_ref, gbn_ref, gcn_ref, xcn_ref, o_ref, dnc_ref, dncn_ref, dna_ref,
             w_ref, gco_ref, gao_ref, dz_ref, do_ref, delta_ref, gw_ref, ggco_ref, ggao_ref):
        i = pl.program_id(0)

        @pl.when(i == 0)
        def _():
            gw_ref[...] = jnp.zeros_like(gw_ref)
            ggco_ref[...] = jnp.zeros_like(ggco_ref)
            ggao_ref[...] = jnp.zeros_like(ggao_ref)

        w = w_ref[...]
        zeros = jnp.zeros((HALO, c), F32)
        gb_e = jnp.concatenate([zeros, gb_ref[...], gbn_ref[...]], axis=0)
        cx_prev = jnp.where(i > 0, gcp_ref[...] * xcp_ref[...], 0.0)
        gc_e = jnp.concatenate([zeros, gc_ref[...], gcn_ref[...]], axis=0)
        xc_e = jnp.concatenate([zeros, xc_ref[...], xcn_ref[...]], axis=0)
        cx_e = jnp.concatenate([cx_prev, gc_ref[...] * xc_ref[...], gcn_ref[...] * xcn_ref[...]], axis=0)
        dn_next = jnp.where(i < n_blk - 1, dncn_ref[...], 0.0)
        dn_e = jnp.concatenate([zeros, dnc_ref[...], dn_next], axis=0)

        conv_e = _conv_taps(cx_e, w)
        y_e = gb_e * conv_e
        r_e = _rstd(y_e)
        yn_e = y_e * r_e
        t_e = dn_e * gco_ref[...]
        dy_e = r_e * (t_e - yn_e * jnp.mean(t_e * yn_e, axis=-1, keepdims=True))
        dconv_e = dy_e * gb_e
        dcx_e = _conv_taps_t(dconv_e, w)
        blk = slice(HALO, HALO + tm)
        dz_ref[:, 0:c] = (dy_e * conv_e)[blk].astype(BF16)
        dz_ref[:, c:2 * c] = (dcx_e * xc_e)[blk].astype(BF16)
        dz_ref[:, 2 * c:3 * c] = (dcx_e * gc_e)[blk].astype(BF16)
        ggco_ref[...] += jnp.sum((dn_e * yn_e)[blk], axis=0, keepdims=True)
        dconv = dconv_e[blk]
        gw_ref[0:1, :] += jnp.sum(dconv * _shift_down(cx_e, 2)[blk], axis=0, keepdims=True)
        gw_ref[1:2, :] += jnp.sum(dconv * _shift_down(cx_e, 1)[blk], axis=0, keepdims=True)
        gw_ref[2:3, :] += jnp.sum(dconv * cx_e[blk], axis=0, keepdims=True)

        o = o_ref[...]
        ra = _rstd(o)
        on = o * ra
        dna = dna_ref[...]
        ggao_ref[...] += jnp.sum(dna * on, axis=0, keepdims=True)
        ta = dna * gao_ref[...]
        do = ra * (ta - on * jnp.mean(ta * on, axis=-1, keepdims=True))
        do_ref[...] = do.astype(BF16)
        prod = do * o
        head_a = lax.broadcasted_iota(jnp.int32, (tm, LANES), 1) < HEAD_DIM
        for p in range(N_PAIRS):
            pb = prod[:, p * LANES:(p + 1) * LANES]
            sa = jnp.sum(jnp.where(head_a, pb, 0.0), axis=-1, keepdims=True)
            sb = jnp.sum(jnp.where(head_a, 0.0, pb), axis=-1, keepdims=True)
            delta_ref[:, p * LANES:(p + 1) * LANES] = jnp.where(head_a, sa, sb)

    col = lambda j: pl.BlockSpec((tm, c), lambda i: (i, j))
    vec = pl.BlockSpec((1, c), lambda i: (0, 0))
    w3 = pl.BlockSpec((3, c), lambda i: (0, 0))
    return pl.pallas_call(
        body, name="mixer_bwd", grid=(n_blk,),
        in_specs=[col(0), col(1), col(2), _prev_halo_spec(tm, c, 1), _prev_halo_spec(tm, c, 2),
                  _next_halo_spec(tm, c, 0, s), _next_halo_spec(tm, c, 1, s), _next_halo_spec(tm, c, 2, s),
                  col(0), col(0), _next_halo_spec(tm, c, 0, s), col(1), w3, vec, vec],
        out_specs=[pl.BlockSpec((tm, 3 * c), lambda i: (i, 0)), col(0), col(0), w3, vec, vec],
        out_shape=[jax.ShapeDtypeStruct((s, 3 * c), BF16), jax.ShapeDtypeStruct((s, c), BF16),
                   jax.ShapeDtypeStruct((s, c), F32), jax.ShapeDtypeStruct((3, c), F32),
                   jax.ShapeDtypeStruct((1, c), F32), jax.ShapeDtypeStruct((1, c), F32)],
        compiler_params=_cp(("arbitrary",)))(
            z_a, z_a, z_a, z_a, z_a, z_a, z_a, z_a, o_attn, dmix, dmix, dmix, w_conv, g_conv_out, g_attn_out)


def _gate_fwd(f, b_pad):
    s = f.shape[0]
    tm = min(TQ, s)

    def body(f_ref, b_ref, fb_ref, ft_ref, carry):
        @pl.when(pl.program_id(0) == 0)
        def _():
            carry[...] = jnp.zeros_like(carry)

        z = f_ref[...] + b_ref[...]
        x = jnp.minimum(z, 0.0) - jnp.log1p(jnp.exp(-jnp.abs(z)))
        row = lax.broadcasted_iota(jnp.int32, (tm, LANES), 0)
        sh = 1
        while sh < tm:
            x = x + jnp.where(row >= sh, _shift_down(x, sh), 0.0)
            sh *= 2
        x = x + carry[0:1, :]
        carry[...] = jnp.broadcast_to(x[tm - 1:tm, :], carry.shape)
        head_a = lax.broadcasted_iota(jnp.int32, (tm, LANES), 1) < HEAD_DIM
        for p in range(N_PAIRS):
            fa = jnp.broadcast_to(x[:, 2 * p:2 * p + 1], (tm, LANES))
            fbv = jnp.broadcast_to(x[:, 2 * p + 1:2 * p + 2], (tm, LANES))
            fb_ref[:, p * LANES:(p + 1) * LANES] = jnp.where(head_a, fa, fbv)
        ft_ref[0] = x.T[0:N_HEADS, :]

    return pl.pallas_call(
        body, name="gate_fwd", grid=(s // tm,),
        in_specs=[pl.BlockSpec((tm, LANES), lambda i: (i, 0)), pl.BlockSpec((1, LANES), lambda i: (0, 0))],
        out_specs=[pl.BlockSpec((tm, N_PAIRS * LANES), lambda i: (i, 0)),
                   pl.BlockSpec((1, N_HEADS, tm), lambda i: (i, 0, 0))],
        out_shape=[jax.ShapeDtypeStruct((s, N_PAIRS * LANES), F32),
                   jax.ShapeDtypeStruct((s // tm, N_HEADS, tm), F32)],
        scratch_shapes=[pltpu.VMEM((HALO, LANES), F32)],
        compiler_params=_cp(("arbitrary",)))(f, b_pad)


def _gate_bwd(f, b_pad, d_f):
    s = f.shape[0]
    tm = min(TQ, s)
    n_blk = s // tm

    def body(f_ref, b_ref, d_ref, df_ref, gb_ref, carry):
        @pl.when(pl.program_id(0) == 0)
        def _():
            carry[...] = jnp.zeros_like(carry)
            gb_ref[...] = jnp.zeros_like(gb_ref)

        x = d_ref[...]
        row = lax.broadcasted_iota(jnp.int32, (tm, LANES), 0)
        sh = 1
        while sh < tm:
            x = x + jnp.where(row < tm - sh, _shift_up(x, sh), 0.0)
            sh *= 2
        x = x + carry[0:1, :]
        carry[...] = jnp.broadcast_to(x[0:1, :], carry.shape)
        z = f_ref[...] + b_ref[...]
        d = x * (1.0 / (1.0 + jnp.exp(z)))
        df_ref[...] = d.astype(BF16)
        gb_ref[...] += jnp.sum(d, axis=0, keepdims=True)

    rev = pl.BlockSpec((tm, LANES), lambda i: (n_blk - 1 - i, 0))
    vec = pl.BlockSpec((1, LANES), lambda i: (0, 0))
    return pl.pallas_call(
        body, name="gate_bwd", grid=(n_blk,), in_specs=[rev, vec, rev], out_specs=[rev, vec],
        out_shape=[jax.ShapeDtypeStruct((s, LANES), BF16), jax.ShapeDtypeStruct((1, LANES), F32)],
        scratch_shapes=[pltpu.VMEM((HALO, LANES), F32)],
        compiler_params=_cp(("arbitrary",)))(f, b_pad, d_f)


_NT = (((1,), (1,)), ((), ()))
_NN = (((1,), (0,)), ((), ()))
_TN = (((0,), (0,)), ((), ()))


def _head_masks(shape):
    lane = lax.broadcasted_iota(jnp.int32, shape, len(shape) - 1)
    return lane < HEAD_DIM


def _pick_row(ft, h):
    rows = lax.broadcasted_iota(jnp.int32, ft.shape, 0)
    return jnp.sum(jnp.where(rows == h, ft, 0.0), axis=0, keepdims=True)


def _prune_bounds(qkv, fb):
    s = qkv.shape[0]
    tq = min(TQ, s)

    def body(q_ref, k_ref, fb_ref, out_ref):
        head_a = _head_masks((tq, LANES))
        lane = lax.broadcasted_iota(jnp.int32, (HALO, LANES), 1)
        acc = jnp.zeros((HALO, LANES), F32)
        for p in range(N_PAIRS):
            cols = slice(p * LANES, (p + 1) * LANES)
            q2 = q_ref[:, cols].astype(F32) * Q_SCALE
            k2 = k_ref[:, cols].astype(F32)
            f2 = fb_ref[:, cols]
            for hh in range(2):
                sel = head_a if hh == 0 else jnp.logical_not(head_a)
                qn = jnp.sqrt(jnp.sum(jnp.where(sel, q2 * q2, 0.0), axis=-1, keepdims=True))
                kn = jnp.sqrt(jnp.sum(jnp.where(sel, k2 * k2, 0.0), axis=-1, keepdims=True))
                f = f2[:, hh * HEAD_DIM:hh * HEAD_DIM + 1]
                h = 2 * p + hh
                vals = (jnp.max(qn, axis=0, keepdims=True), jnp.max(kn, axis=0, keepdims=True),
                        jnp.max(qn * kn + f, axis=0, keepdims=True), f[tq - 1:tq, :])
                for slot, v in enumerate(vals):
                    acc = jnp.where(lane == slot * N_HEADS + h, v, acc)
        out_ref[0] = acc

    blk = lambda j: pl.BlockSpec((tq, ATTN_W), lambda i: (i, j))
    return pl.pallas_call(
        body, name="prune_bounds", grid=(s // tq,), in_specs=[blk(0), blk(1), blk(0)],
        out_specs=pl.BlockSpec((1, HALO, LANES), lambda i: (i, 0, 0)),
        out_shape=jax.ShapeDtypeStruct((s // tq, HALO, LANES), F32),
        compiler_params=_cp(("parallel",)))(qkv, qkv, fb)


def _first_key_blocks(qkv, fb):
    t = _prune_bounds(qkv, fb)[:, 0, :]
    nh = N_HEADS
    a, b, c, e = t[:, 0:nh], t[:, nh:2 * nh], t[:, 2 * nh:3 * nh], t[:, 3 * nh:4 * nh]
    bound = a[:, None, :] * b[None, :, :] * 1.001 + c[:, None, :] - e[None, :, :]
    n_q = t.shape[0]
    idx = jnp.arange(n_q)
    need = jnp.logical_not(bound < -(EXP_ZERO + 2.0)) | (idx[None, :, None] >= idx[:, None, None])
    first = jnp.argmax(need, axis=1).astype(jnp.int32)
    return jnp.min(first.reshape(n_q, N_PAIRS, 2), axis=-1).T.reshape(-1)


def _attn_fwd(qkv, fb, ft, first_blk):
    s = qkv.shape[0]
    tq = min(TQ, s)
    n_q = s // tq
    neg = -1e30

    def body(first_ref, q_ref, k_ref, v_ref, fb_ref, ft_ref, o_ref, g_ref):
        p = pl.program_id(0)
        i = pl.program_id(1)
        head_a = _head_masks((tq, LANES))
        q2 = q_ref[...] * Q_SCALE
        zero = jnp.zeros_like(q2)
        q_a, q_b = jnp.where(head_a, q2, zero), jnp.where(head_a, zero, q2)
        fq2 = fb_ref[...]
        fq_a, fq_b = fq2[:, 0:1], fq2[:, HEAD_DIM:HEAD_DIM + 1]
        causal = lax.broadcasted_iota(jnp.int32, (tq, tq), 1) <= lax.broadcasted_iota(jnp.int32, (tq, tq), 0)

        def one_head(q_h, fq_h, fk_h, k2, m, l, masked):
            sc = (lax.dot_general(q_h, k2, _NT, preferred_element_type=F32) + fq_h) - fk_h
            if masked:
                sc = jnp.where(causal, sc, -jnp.inf)
            m_new = jnp.maximum(m, jnp.max(sc, axis=-1, keepdims=True))
            pr = jnp.exp(sc - m_new)
            alpha = jnp.exp(m - m_new)
            return pr.astype(BF16), m_new, alpha * l + jnp.sum(pr, axis=-1, keepdims=True), alpha

        def step(kb, carry, masked):
            m_a, l_a, m_b, l_b, acc = carry
            rows = pl.ds(pl.multiple_of(kb * tq, tq), tq)
            k2, v2 = k_ref[rows, :], v_ref[rows, :]
            ftv = ft_ref[kb]
            p_a, m_a, l_a, al_a = one_head(q_a, fq_a, _pick_row(ftv, 2 * p), k2, m_a, l_a, masked)
            p_b, m_b, l_b, al_b = one_head(q_b, fq_b, _pick_row(ftv, 2 * p + 1), k2, m_b, l_b, masked)
            zv = jnp.zeros_like(v2)
            head_k = _head_masks(v2.shape)
            pv = (lax.dot_general(p_a, jnp.where(head_k, v2, zv), _NN, preferred_element_type=F32)
                  + lax.dot_general(p_b, jnp.where(head_k, zv, v2), _NN, preferred_element_type=F32))
            acc = acc * jnp.where(head_a, al_a, al_b) + pv
            return m_a, l_a, m_b, l_b, acc

        col = lambda v: jnp.full((tq, 1), v, F32)
        carry = (col(neg), col(0.0), col(neg), col(0.0), jnp.zeros((tq, LANES), F32))
        carry = lax.fori_loop(first_ref[p * n_q + i], i, lambda kb, cr: step(kb, cr, False), carry)
        m_a, l_a, m_b, l_b, acc = step(i, carry, True)
        o_ref[...] = acc / jnp.where(head_a, l_a, l_b)
        g_ref[...] = fq2 - jnp.where(head_a, m_a + jnp.log(l_a), m_b + jnp.log(l_b))

    qblk = lambda off: pl.BlockSpec((tq, LANES), lambda p, i, first: (i, off + p))
    full = lambda off: pl.BlockSpec((s, LANES), lambda p, i, first: (0, off + p))
    grid_spec = pltpu.PrefetchScalarGridSpec(
        num_scalar_prefetch=1, grid=(N_PAIRS, n_q),
        in_specs=[qblk(0), full(N_PAIRS), full(2 * N_PAIRS), qblk(0),
                  pl.BlockSpec((n_q, N_HEADS, tq), lambda p, i, first: (0, 0, 0))],
        out_specs=[qblk(0), qblk(0)])
    return pl.pallas_call(
        body, name="attn_fwd", grid_spec=grid_spec,
        out_shape=[jax.ShapeDtypeStruct((s, ATTN_W), F32), jax.ShapeDtypeStruct((s, ATTN_W), F32)],
        compiler_params=_cp(("parallel", "arbitrary")))(first_blk, qkv, qkv, qkv, fb, ft)


def _attn_bwd(qkv, do, g, delta, ft, first_blk):
    s = qkv.shape[0]
    tq = min(TQ, s)
    n_q = s // tq

    def body(first_ref, q_ref, do_ref, g_ref, dl_ref, k_ref, v_ref, ft_ref, dq_ref, dfq_ref, dk_ref, dv_ref,
             dfk_ref):
        p = pl.program_id(0)
        i = pl.program_id(1)

        @pl.when(i == 0)
        def _():
            dk_ref[...] = jnp.zeros_like(dk_ref)
            dv_ref[...] = jnp.zeros_like(dv_ref)
            dfk_ref[...] = jnp.zeros_like(dfk_ref)

        head_a = _head_masks((tq, LANES))
        q2 = q_ref[...] * Q_SCALE
        do2 = do_ref[...]
        zero = jnp.zeros_like(q2)
        q_a, q_b = jnp.where(head_a, q2, zero), jnp.where(head_a, zero, q2)
        do_a, do_b = jnp.where(head_a, do2, zero), jnp.where(head_a, zero, do2)
        g2, dl2 = g_ref[...], dl_ref[...]
        g_a, g_b = g2[:, 0:1], g2[:, HEAD_DIM:HEAD_DIM + 1]
        dl_a, dl_b = dl2[:, 0:1], dl2[:, HEAD_DIM:HEAD_DIM + 1]
        causal = lax.broadcasted_iota(jnp.int32, (tq, tq), 1) <= lax.broadcasted_iota(jnp.int32, (tq, tq), 0)
        rows8 = lax.broadcasted_iota(jnp.int32, (N_HEADS, tq), 0)

        def one_head(q_h, do_h, g_h, dl_h, fk_h, k2, v2, masked):
            sc = (lax.dot_general(q_h, k2, _NT, preferred_element_type=F32) + g_h) - fk_h
            pr = jnp.exp(sc)
            if masked:
                pr = jnp.where(causal, pr, 0.0)
            dp = lax.dot_general(do_h, v2, _NT, preferred_element_type=F32)
            ds = pr * (dp - dl_h)
            return (pr.astype(BF16), ds.astype(BF16), jnp.sum(ds, axis=0, keepdims=True),
                    jnp.sum(ds, axis=1, keepdims=True))

        def step(j, carry, masked):
            dq, r_a, r_b = carry
            rows = pl.ds(pl.multiple_of(j * tq, tq), tq)
            k2, v2 = k_ref[rows, :], v_ref[rows, :]
            k_a, k_b = jnp.where(head_a, k2, zero), jnp.where(head_a, zero, k2)
            ftv = ft_ref[j]
            p_a, ds_a, c_a, s_a = one_head(q_a, do_a, g_a, dl_a, _pick_row(ftv, 2 * p), k2, v2, masked)
            p_b, ds_b, c_b, s_b = one_head(q_b, do_b, g_b, dl_b, _pick_row(ftv, 2 * p + 1), k2, v2, masked)
            dv_ref[rows, :] += (lax.dot_general(p_a, do_a, _TN, preferred_element_type=F32)
                                + lax.dot_general(p_b, do_b, _TN, preferred_element_type=F32))
            dk_ref[rows, :] += (lax.dot_general(ds_a, q_a, _TN, preferred_element_type=F32)
                                + lax.dot_general(ds_b, q_b, _TN, preferred_element_type=F32))
            dfk_ref[0, j] += jnp.where(rows8 == 0, -c_a, jnp.where(rows8 == 1, -c_b, 0.0))
            dq = dq + (lax.dot_general(ds_a, k_a, _NN, preferred_element_type=F32)
                       + lax.dot_general(ds_b, k_b, _NN, preferred_element_type=F32))
            return dq, r_a + s_a, r_b + s_b

        zcol = jnp.zeros((tq, 1), F32)
        carry = lax.fori_loop(first_ref[p * n_q + i], i, lambda j, cr: step(j, cr, False),
                              (jnp.zeros((tq, LANES), F32), zcol, zcol))
        dq, r_a, r_b = step(i, carry, True)
        dq_ref[...] = (dq * Q_SCALE).astype(BF16)
        dfq_ref[...] = jnp.where(head_a, r_a, r_b)

    qblk = lambda off: pl.BlockSpec((tq, LANES), lambda p, i, first: (i, off + p))
    full = lambda off: pl.BlockSpec((s, LANES), lambda p, i, first: (0, off + p))
    grid_spec = pltpu.PrefetchScalarGridSpec(
        num_scalar_prefetch=1, grid=(N_PAIRS, n_q),
        in_specs=[qblk(0), qblk(0), qblk(0), qblk(0), full(N_PAIRS), full(2 * N_PAIRS),
                  pl.BlockSpec((n_q, N_HEADS, tq), lambda p, i, first: (0, 0, 0))],
        out_specs=[qblk(0), qblk(0), full(0), full(0),
                   pl.BlockSpec((1, n_q, N_HEADS, tq), lambda p, i, first: (p, 0, 0, 0))])
    return pl.pallas_call(
        body, name="attn_bwd", grid_spec=grid_spec,
        out_shape=[jax.ShapeDtypeStruct((s, ATTN_W), BF16), jax.ShapeDtypeStruct((s, ATTN_W), F32),
                   jax.ShapeDtypeStruct((s, ATTN_W), F32), jax.ShapeDtypeStruct((s, ATTN_W), F32),
                   jax.ShapeDtypeStruct((N_PAIRS, n_q, N_HEADS, tq), F32)],
        compiler_params=_cp(("parallel", "arbitrary")))(first_blk, qkv, do, g, delta, qkv, qkv, ft)


def _ffn_act_fwd(up, w_ffn):
    s = up.shape[0]
    tm, tn = min(TM_FFN, s), TN_FFN
    nb = D_FF // tn

    def body(a_ref, g_ref, ap_ref, gp_ref, wa_ref, wg_ref, act_ref):
        i = pl.program_id(1)

        def conv(blk_ref, prev_ref, w_ref):
            prev = jnp.where(i > 0, prev_ref[...], 0.0)
            return _conv_taps(jnp.concatenate([prev, blk_ref[...]], axis=0), w_ref[...])[HALO:]

        u_a, u_g = conv(a_ref, ap_ref, wa_ref), conv(g_ref, gp_ref, wg_ref)
        act_ref[...] = (u_g * (1.0 / (1.0 + jnp.exp(-u_g))) * u_a).astype(BF16)

    blk = lambda off: pl.BlockSpec((tm, tn), lambda n, i: (i, off + n))
    prev = lambda off: pl.BlockSpec((HALO, tn), lambda n, i: (jnp.maximum(i * (tm // HALO) - 1, 0), off + n))
    wsp = lambda off: pl.BlockSpec((3, tn), lambda n, i: (0, off + n))
    return pl.pallas_call(
        body, name="ffn_act_fwd", grid=(nb, s // tm),
        in_specs=[blk(0), blk(nb), prev(0), prev(nb), wsp(0), wsp(nb)],
        out_specs=pl.BlockSpec((tm, tn), lambda n, i: (i, n)),
        out_shape=jax.ShapeDtypeStruct((s, D_FF), BF16),
        compiler_params=_cp(("parallel", "parallel")))(up, up, up, up, w_ffn, w_ffn)


def _ffn_act_bwd(up, dact, w_ffn):
    s = up.shape[0]
    tm, tn = min(TM_FFN, s), TN_FFN
    nb = D_FF // tn
    n_blk = s // tm

    def body(a_ref, g_ref, ap_ref, gp_ref, an_ref, gn_ref, d_ref, dn_ref, wa_ref, wg_ref,
             da_ref, dg_ref, gwa_ref, gwg_ref):
        i = pl.program_id(1)

        @pl.when(i == 0)
        def _():
            gwa_ref[...] = jnp.zeros_like(gwa_ref)
            gwg_ref[...] = jnp.zeros_like(gwg_ref)

        def ext(blk_ref, prev_ref, next_ref):
            return jnp.concatenate([jnp.where(i > 0, prev_ref[...], 0.0), blk_ref[...], next_ref[...]], axis=0)

        wa, wg = wa_ref[...], wg_ref[...]
        up_a, up_g = ext(a_ref, ap_ref, an_ref), ext(g_ref, gp_ref, gn_ref)
        u_a, u_g = _conv_taps(up_a, wa), _conv_taps(up_g, wg)
        d_e = jnp.concatenate([jnp.zeros((HALO, tn), F32), d_ref[...],
                               jnp.where(i < n_blk - 1, dn_ref[...], 0.0)], axis=0)
        sig = 1.0 / (1.0 + jnp.exp(-u_g))
        du_a = d_e * (u_g * sig)
        du_g = d_e * u_a * (sig * (1.0 + u_g * (1.0 - sig)))
        blk = slice(HALO, HALO + tm)
        da_ref[...] = _conv_taps_t(du_a, wa)[blk].astype(BF16)
        dg_ref[...] = _conv_taps_t(du_g, wg)[blk].astype(BF16)
        for gw_ref, upv, du in ((gwa_ref, up_a[blk], du_a), (gwg_ref, up_g[blk], du_g)):
            gw_ref[0:1, :] += jnp.sum(upv * _shift_up(du, 2)[blk], axis=0, keepdims=True)
            gw_ref[1:2, :] += jnp.sum(upv * _shift_up(du, 1)[blk], axis=0, keepdims=True)
            gw_ref[2:3, :] += jnp.sum(upv * du[blk], axis=0, keepdims=True)

    blk = lambda off: pl.BlockSpec((tm, tn), lambda n, i: (i, off + n))
    prev = lambda off: pl.BlockSpec((HALO, tn), lambda n, i: (jnp.maximum(i * (tm // HALO) - 1, 0), off + n))
    nxt = lambda off: pl.BlockSpec(
        (HALO, tn), lambda n, i: (jnp.minimum((i + 1) * (tm // HALO), s // HALO - 1), off + n))
    wsp = lambda off: pl.BlockSpec((3, tn), lambda n, i: (0, off + n))
    return pl.pallas_call(
        body, name="ffn_act_bwd", grid=(nb, n_blk),
        in_specs=[blk(0), blk(nb), prev(0), prev(nb), nxt(0), nxt(nb), blk(0), nxt(0), wsp(0), wsp(nb)],
        out_specs=[blk(0), blk(0), wsp(0), wsp(0)],
        out_shape=[jax.ShapeDtypeStruct((s, D_FF), BF16), jax.ShapeDtypeStruct((s, D_FF), BF16),
                   jax.ShapeDtypeStruct((3, D_FF), F32), jax.ShapeDtypeStruct((3, D_FF), F32)],
        compiler_params=_cp(("parallel", "arbitrary")))(up, up, up, up, up, up, dact, dact, w_ffn, w_ffn)


def _adamw(w, g, m, v, name):
    r, c = w.shape
    tr = next((t for t in (512, 352, 256, 128, 64, 32, 16, 8) if r > t and r % t == 0), r)

    def body(w_ref, g_ref, m_ref, v_ref, d_ref, nm_ref, nv_ref):
        gv = g_ref[...]
        m_new = ADAM_B1 * m_ref[...] + (1.0 - ADAM_B1) * gv
        v_new = ADAM_B2 * v_ref[...] + (1.0 - ADAM_B2) * (gv * gv)
        m_hat = m_new / (1.0 - ADAM_B1 ** ADAM_STEP)
        v_hat = v_new / (1.0 - ADAM_B2 ** ADAM_STEP)
        d_ref[...] = -ADAM_LR * (m_hat / (jnp.sqrt(v_hat) + ADAM_EPS) + ADAM_WD * w_ref[...])
        nm_ref[...] = m_new
        nv_ref[...] = v_new

    spec = pl.BlockSpec((tr, c), lambda i: (i, 0))
    shp = jax.ShapeDtypeStruct((r, c), F32)
    return pl.pallas_call(
        body, name=name, grid=(r // tr,), in_specs=[spec] * 4, out_specs=[spec] * 3, out_shape=[shp] * 3,
        compiler_params=_cp(("parallel",)))(w, g, m, v)


def _sum_rows_block(h):
    return h // 4 if h % (4 * HALO) == 0 else h


def _pair_sum(parts, from_sibling, c):
    _, n, h, _ = parts.shape
    tr = _sum_rows_block(h)

    def body(sel_ref, a_ref, b_ref, o_ref):
        o_ref[...] = a_ref[0] + b_ref[...]

    blk = pl.BlockSpec((1, tr, LANES), lambda j, i, sel_ref: (j, i, 0))
    grid_spec = pltpu.PrefetchScalarGridSpec(
        num_scalar_prefetch=1, grid=(n, h // tr),
        in_specs=[pl.BlockSpec((1, 1, tr, LANES), lambda j, i, sel_ref: (sel_ref[0], j, i, 0)), blk], out_specs=blk)
    return pl.pallas_call(
        body, name="pair_sum", grid_spec=grid_spec, out_shape=jax.ShapeDtypeStruct((n, h, LANES), F32),
        compiler_params=_cp(("parallel", "parallel")))(jnp.reshape(c, (1,)).astype(jnp.int32), parts, from_sibling)


def _chip_sum(pair, got, my_chip):
    _, h, _ = pair.shape
    tr = _sum_rows_block(h)

    def body(sel_ref, p_ref, g0_ref, g1_ref, g2_ref, o_ref):
        o_ref[...] = ((p_ref[...] + g0_ref[...]) + g1_ref[...]) + g2_ref[...]

    slot = lambda k: pl.BlockSpec((1, tr, LANES), lambda i, sel_ref: (k, i, 0))
    grid_spec = pltpu.PrefetchScalarGridSpec(
        num_scalar_prefetch=1, grid=(h // tr,),
        in_specs=[pl.BlockSpec((1, tr, LANES), lambda i, sel_ref: (sel_ref[0], i, 0)), slot(0), slot(1), slot(2)],
        out_specs=slot(0))
    return pl.pallas_call(
        body, name="chip_sum", grid_spec=grid_spec, out_shape=jax.ShapeDtypeStruct((1, h, LANES), F32),
        compiler_params=_cp(("parallel",)))(jnp.reshape(my_chip, (1,)).astype(jnp.int32), pair, got, got, got)


def _place():
    return lax.axis_index("x"), lax.axis_index("y"), lax.axis_index("c")


def _other_chips(x, y):
    return [(1 - x, y), (x, 1 - y), (1 - x, 1 - y)]


def _all_gather_chips(shard):
    r = shard.shape[0]
    h = r // 2

    def body(src_ref, out_ref, send_sems, recv_sems, local_sem):
        x, y, c = _place()
        my_chip = 2 * x + y
        chips = _other_chips(x, y)

        def half(chip, which):
            return out_ref.at[chip, pl.ds(which * h, h), :]

        def copy(k, chip, which, to, src=None):
            return pltpu.make_async_remote_copy(
                src_ref=half(chip, which) if src is None else src, dst_ref=half(chip, which),
                send_sem=send_sems.at[k], recv_sem=recv_sems.at[k], device_id=to, device_id_type=MESH)

        mine = pltpu.make_async_copy(src_ref, out_ref.at[my_chip], local_sem)
        mine.start()
        first = [copy(j, my_chip, c, (cx, cy, c), src=src_ref.at[pl.ds(c * h, h), :])
                 for j, (cx, cy) in enumerate(chips)]
        for cp in first:
            cp.start()
        passed = [copy(3 + j, 2 * cx + cy, c, (x, y, 1 - c)) for j, (cx, cy) in enumerate(chips)]
        for j, (cx, cy) in enumerate(chips):
            copy(j, 2 * cx + cy, c, (x, y, c)).wait_recv()
            passed[j].start()
        for j, (cx, cy) in enumerate(chips):
            copy(3 + j, 2 * cx + cy, 1 - c, (x, y, c)).wait_recv()
        for cp in first + passed:
            cp.wait_send()
        mine.wait()

    return pl.pallas_call(
        body, name="all_gather_weights", out_shape=jax.ShapeDtypeStruct((N_CHIPS, r, LANES), shard.dtype),
        in_specs=[pl.BlockSpec(memory_space=pl.ANY)], out_specs=pl.BlockSpec(memory_space=pl.ANY),
        scratch_shapes=[pltpu.SemaphoreType.DMA((6,)), pltpu.SemaphoreType.DMA((6,)), pltpu.SemaphoreType.DMA])(shard)


def _send_other_half(parts):
    _, n, h, _ = parts.shape

    def body(a_ref, out_ref, send_sem, recv_sem):
        x, y, c = _place()
        cp = pltpu.make_async_remote_copy(
            src_ref=a_ref.at[1 - c], dst_ref=out_ref, send_sem=send_sem, recv_sem=recv_sem,
            device_id=(x, y, 1 - c), device_id_type=MESH)
        cp.start()
        cp.wait()

    return pl.pallas_call(
        body, name="pair_exchange", out_shape=jax.ShapeDtypeStruct((n, h, LANES), parts.dtype),
        in_specs=[pl.BlockSpec(memory_space=pl.ANY)], out_specs=pl.BlockSpec(memory_space=pl.ANY),
        scratch_shapes=[pltpu.SemaphoreType.DMA, pltpu.SemaphoreType.DMA])(parts)


def _join_halves(mine):
    _, h, _ = mine.shape

    def body(a_ref, out_ref, send_sem, recv_sem, local_sem):
        x, y, c = _place()
        own = pltpu.make_async_copy(a_ref, out_ref.at[pl.ds(c, 1)], local_sem)
        own.start()
        cp = pltpu.make_async_remote_copy(
            src_ref=a_ref, dst_ref=out_ref.at[pl.ds(c, 1)], send_sem=send_sem, recv_sem=recv_sem,
            device_id=(x, y, 1 - c), device_id_type=MESH)
        cp.start()
        cp.wait_send()
        pltpu.make_async_remote_copy(
            src_ref=a_ref, dst_ref=out_ref.at[pl.ds(1 - c, 1)], send_sem=send_sem, recv_sem=recv_sem,
            device_id=(x, y, 1 - c), device_id_type=MESH).wait_recv()
        own.wait()

    return pl.pallas_call(
        body, name="half_exchange", out_shape=jax.ShapeDtypeStruct((2, h, LANES), mine.dtype),
        in_specs=[pl.BlockSpec(memory_space=pl.ANY)], out_specs=pl.BlockSpec(memory_space=pl.ANY),
        scratch_shapes=[pltpu.SemaphoreType.DMA, pltpu.SemaphoreType.DMA, pltpu.SemaphoreType.DMA])(mine)


def _scatter_to_chips(part):
    _, h, _ = part.shape

    def body(p_ref, out_ref, send_sems, recv_sems):
        x, y, c = _place()
        copies = [pltpu.make_async_remote_copy(
            src_ref=p_ref.at[2 * cx + cy], dst_ref=out_ref.at[k], send_sem=send_sems.at[k], recv_sem=recv_sems.at[k],
            device_id=(cx, cy, c), device_id_type=MESH) for k, (cx, cy) in enumerate(_other_chips(x, y))]
        for cp in copies:
            cp.start()
        for cp in copies:
            cp.wait()

    return pl.pallas_call(
        body, name="scatter_grads", out_shape=jax.ShapeDtypeStruct((3, h, LANES), part.dtype),
        in_specs=[pl.BlockSpec(memory_space=pl.ANY)], out_specs=pl.BlockSpec(memory_space=pl.ANY),
        scratch_shapes=[pltpu.SemaphoreType.DMA((3,)), pltpu.SemaphoreType.DMA((3,))])(part)


def _all_reduce_small(packet):
    rows, width = packet.shape
    n_dev = 8

    def body(x_ref, out_ref, gath, send_sems, recv_sems):
        x, y, c = _place()
        me, sibling = (x, y, c), (x, y, 1 - c)
        chips = _other_chips(x, y)

        def slot(px, py, pc):
            return gath.at[pl.ds((4 * px + 2 * py + pc) * rows, rows), :]

        def copy(k, block, to, src=None):
            return pltpu.make_async_remote_copy(
                src_ref=slot(*block) if src is None else src, dst_ref=slot(*block), send_sem=send_sems.at[k],
                recv_sem=recv_sems.at[k], device_id=to, device_id_type=MESH)

        first = [copy(0, me, sibling, src=x_ref)]
        first += [copy(1 + j, me, (*chip, c), src=x_ref) for j, chip in enumerate(chips)]
        for cp in first:
            cp.start()
        gath[pl.ds((4 * x + 2 * y + c) * rows, rows), :] = x_ref[...]
        passed = [copy(4 + j, (*chip, c), sibling) for j, chip in enumerate(chips)]
        for j, chip in enumerate(chips):
            copy(1 + j, (*chip, c), me).wait_recv()
            passed[j].start()
        copy(0, sibling, me).wait_recv()
        for j, chip in enumerate(chips):
            copy(4 + j, (*chip, 1 - c), me).wait_recv()
        for cp in first + passed:
            cp.wait_send()
        acc = gath[0:rows, :]
        for d in range(1, n_dev):
            acc = acc + gath[d * rows:(d + 1) * rows, :]
        out_ref[...] = acc

    return pl.pallas_call(
        body, name="all_reduce_small", out_shape=jax.ShapeDtypeStruct((rows, width), F32),
        in_specs=[pl.BlockSpec(memory_space=pltpu.VMEM)], out_specs=pl.BlockSpec(memory_space=pltpu.VMEM),
        scratch_shapes=[pltpu.VMEM((n_dev * rows, width), F32), pltpu.SemaphoreType.DMA((7,)),
                        pltpu.SemaphoreType.DMA((7,))])(packet)


def _flat_rows(parts, dtype, row_multiple):
    flat = jnp.concatenate([p.astype(dtype).reshape(-1) for p in parts])
    rows = -(-flat.shape[0] // LANES)
    rows = -(-rows // row_multiple) * row_multiple
    return jnp.pad(flat, (0, rows * LANES - flat.shape[0])).reshape(rows, LANES)


def _unflatten(flat2d, shapes):
    flat = flat2d.reshape(-1)
    out, off = [], 0
    for shp in shapes:
        n = 1
        for dim in shp:
            n *= dim
        out.append(flat[off:off + n].reshape(shp))
        off += n
    return out


def _reduce_scatter_grads(parts):
    x, y, c = _place()
    pair = _pair_sum(parts, _send_other_half(parts), c)
    total = _chip_sum(pair, _scatter_to_chips(pair), 2 * x + y)
    return _join_halves(total).reshape(-1, LANES)


def kernel(x, g_mix, w_in, b_f, w_conv, g_conv_out, g_attn_out, w_o, g_ffn, w_up, w_ffn_conv, w_down, g_final, loss_target, m_g_mix, m_w_in, m_b_f, m_w_conv, m_g_conv_out, m_g_attn_out, m_w_o, m_g_ffn, m_w_up, m_w_ffn_conv, m_w_down, m_g_final, v_g_mix, v_w_in, v_b_f, v_w_conv, v_g_conv_out, v_g_attn_out, v_w_o, v_g_ffn, v_w_up, v_w_ffn_conv, v_w_down, v_g_final):
    s = x.shape[1]
    x0 = x[0]
    target = loss_target[0]
    d = D_MODEL
    sharded = [w_in[0], w_o[0], w_up[0], w_down[0]]
    small_sharded = [w_conv[0], w_ffn_conv[0]]
    shard_shapes = [a.shape for a in sharded]
    small_shapes = [a.shape for a in small_sharded]

    small_bits = lax.bitcast_convert_type(jnp.concatenate([a.reshape(-1) for a in small_sharded]), BF16).reshape(-1)
    n_small = small_bits.shape[0]
    gathered = _all_gather_chips(_flat_rows(sharded + [small_bits], BF16, 32))
    per_chip = [_unflatten(gathered[j], shard_shapes + [(n_small,)]) for j in range(N_CHIPS)]
    w_in_full = jnp.concatenate([pc[0] for pc in per_chip], axis=1)
    w_o_full = jnp.concatenate([pc[1] for pc in per_chip], axis=0)
    w_up_full = jnp.concatenate([pc[2] for pc in per_chip], axis=1)
    w_down_full = jnp.concatenate([pc[3] for pc in per_chip], axis=0)
    smalls = [_unflatten(lax.bitcast_convert_type(pc[4].reshape(-1, 2), F32), small_shapes) for pc in per_chip]
    w_conv_full = jnp.concatenate([sm[0] for sm in smalls], axis=1)
    w_ffn_full = jnp.concatenate([sm[1] for sm in smalls], axis=1)
    c3 = 3 * CONV_CH
    w_a, w_b = w_in_full[:, :c3], w_in_full[:, c3:c3 + 3 * ATTN_W]
    w_c = jnp.pad(w_in_full[:, c3 + 3 * ATTN_W:], ((0, 0), (0, LANES - N_HEADS)))
    w_q, w_k, w_v = (w_b[:, i * ATTN_W:(i + 1) * ATTN_W] for i in range(3))
    w_up_lin, w_up_gate = w_up_full[:, :D_FF], w_up_full[:, D_FF:]
    b_pad = jnp.pad(b_f, ((0, 0), (0, LANES - N_HEADS)))

    h1 = _rms_fwd(x0, g_mix, "rms_mix")
    z_a = _mm("nn", [h1], [w_a], F32, TM_MM, 512, "in_proj_conv")
    qkv = _mm("nn", [h1], [w_b], BF16, TM_MM, 512, "in_proj_qkv")
    f_log = _mm("nn", [h1], [w_c], F32, TM_MM, LANES, "in_proj_gate")
    fb, ft = _gate_fwd(f_log, b_pad)
    first_blk = _first_key_blocks(qkv, fb)
    o_attn, g_att = _attn_fwd(qkv, fb, ft, first_blk)
    mix = _mixer_fwd(z_a, o_attn, w_conv_full, g_conv_out, g_attn_out)
    x2 = _mm("nn", [mix], [w_o_full], F32, TM_MM, 512, "out_proj", add=x0)
    h2 = _rms_fwd(x2, g_ffn, "rms_ffn")
    up = _mm("nn", [h2], [w_up_full], F32, TM_MM, 512, "up_proj")
    act = _ffn_act_fwd(up, w_ffn_full)
    x3 = _mm("nn", [act], [w_down_full], F32, 512, 512, "down_proj", add=x2)

    dx3, dx3_b, loss_row, gg_final = _loss_head(x3, target, g_final.reshape(1, d))
    dact = _mm("nt", [dx3_b], [w_down_full], F32, TM_MM, 1408, "d_act")
    gw_down = _mm_tn(act, dx3_b, 1408, 1024, "gw_down")
    dup_lin, dup_gate, gwf_lin, gwf_gate = _ffn_act_bwd(up, dact, w_ffn_full)
    dh2 = _mm("nt", [dup_lin, dup_gate], [w_up_lin, w_up_gate], F32, 512, 512, "d_h2")
    gw_up_lin = _mm_tn(h2, dup_lin, 1024, 1408, "gw_up_lin")
    gw_up_gate = _mm_tn(h2, dup_gate, 1024, 1408, "gw_up_gate")
    dx2, dx2_b, gg_ffn = _rms_bwd(x2, dh2, g_ffn, dx3, "rms_ffn_bwd", True)
    dmix = _mm("nt", [dx2_b], [w_o_full], F32, TM_MM, 512, "d_mix")
    gw_o = _mm_tn(mix, dx2_b, 1024, 1024, "gw_o")
    dz_a, d_o, delta, gw_conv, gg_conv_out, gg_attn_out = _mixer_bwd(z_a, o_attn, dmix, w_conv_full, g_conv_out,
                                                                     g_attn_out)
    dq, dfq, dk, dv, dfk = _attn_bwd(qkv, d_o, g_att, delta, ft, first_blk)
    d_f = jnp.transpose(dfk[:, :, 0:2, :], (1, 3, 0, 2)).reshape(s, N_HEADS) + dfq[:, ::HEAD_DIM]
    df_b, gb_f = _gate_bwd(f_log, b_pad, jnp.pad(d_f, ((0, 0), (0, LANES - N_HEADS))))
    dh1 = _mm("nt", [dz_a, dq, dk, dv, df_b], [w_a, w_q, w_k, w_v, w_c], F32, TM_MM, 512, "d_h1")
    gw_a = _mm_tn(h1, dz_a, 1024, c3, "gw_in_conv")
    gw_q = _mm_tn(h1, dq, 1024, ATTN_W, "gw_in_q")
    gw_k = _mm_tn(h1, dk, 1024, ATTN_W, "gw_in_k")
    gw_v = _mm_tn(h1, dv, 1024, ATTN_W, "gw_in_v")
    gw_c = _mm_tn(h1, df_b, 1024, LANES, "gw_in_gate")
    grad_x, gg_mix = _rms_bwd(x0, dh1, g_mix, dx2, "rms_mix_bwd", False)

    gw_in = jnp.concatenate([gw_a, gw_q, gw_k, gw_v, gw_c[:, :N_HEADS]], axis=1)
    gw_up = jnp.concatenate([gw_up_lin, gw_up_gate], axis=1)
    gw_ffn = jnp.concatenate([gwf_lin, gwf_gate], axis=1)
    n_in, n_up, n_cv, n_ff = IN_COLS // N_CHIPS, 2 * D_FF // N_CHIPS, CONV_CH // N_CHIPS, 2 * D_FF // N_CHIPS
    r_o, r_dn = d // N_CHIPS, D_FF // N_CHIPS
    chip_parts = [_flat_rows(
        [gw_in[:, j * n_in:(j + 1) * n_in], gw_o[j * r_o:(j + 1) * r_o], gw_up[:, j * n_up:(j + 1) * n_up],
         gw_down[j * r_dn:(j + 1) * r_dn], gw_conv[:, j * n_cv:(j + 1) * n_cv], gw_ffn[:, j * n_ff:(j + 1) * n_ff]],
        F32, 8 * HALO) for j in range(N_CHIPS)]
    reduced = _reduce_scatter_grads(jnp.stack([cp.reshape(2, -1, LANES) for cp in chip_parts], axis=1))
    g_w_in, g_w_o, g_w_up, g_w_down, g_w_conv, g_w_ffn = _unflatten(reduced, shard_shapes + small_shapes)

    half = CONV_CH
    packet = jnp.zeros((8, d), F32)
    packet = packet.at[0].set(gg_mix[0]).at[1, :half].set(gg_conv_out[0]).at[1, half:].set(gg_attn_out[0])
    packet = packet.at[2].set(gg_ffn[0]).at[3].set(gg_final[0]).at[4, :N_HEADS].set(gb_f[0, :N_HEADS])
    packet = packet.at[4, LANES].set(loss_row[0, 0])
    tot = _all_reduce_small(packet)
    loss = tot[4, LANES]
    g_g_mix, g_g_conv_out, g_g_attn_out = tot[0:1], tot[1:2, :half], tot[1:2, half:]
    g_g_ffn, g_g_final, g_b_f = tot[2:3], tot[3], tot[4:5, :N_HEADS]

    def adam_big(w, g, m, v, name):
        dl, nm, nv = _adamw(w[0], g, m[0], v[0], name)
        return dl[None], nm[None], nv[None]

    u_w_in = adam_big(w_in, g_w_in, m_w_in, v_w_in, "adam_w_in")
    u_w_o = adam_big(w_o, g_w_o, m_w_o, v_w_o, "adam_w_o")
    u_w_up = adam_big(w_up, g_w_up, m_w_up, v_w_up, "adam_w_up")
    u_w_down = adam_big(w_down, g_w_down, m_w_down, v_w_down, "adam_w_down")

    small_w = [g_mix, b_f, g_conv_out, g_attn_out, g_ffn, g_final, w_conv, w_ffn_conv]
    small_g = [g_g_mix, g_b_f, g_g_conv_out, g_g_attn_out, g_g_ffn, g_g_final, g_w_conv, g_w_ffn]
    small_m = [m_g_mix, m_b_f, m_g_conv_out, m_g_attn_out, m_g_ffn, m_g_final, m_w_conv, m_w_ffn_conv]
    small_v = [v_g_mix, v_b_f, v_g_conv_out, v_g_attn_out, v_g_ffn, v_g_final, v_w_conv, v_w_ffn_conv]
    shapes = [a.shape for a in small_w]
    pack = lambda arrs: _flat_rows(arrs, F32, 8)
    sd, sm, sv = _adamw(pack(small_w), pack(small_g), pack(small_m), pack(small_v), "adam_small")
    sd, sm, sv = _unflatten(sd, shapes), _unflatten(sm, shapes), _unflatten(sv, shapes)
    (d_g_mix, d_b_f, d_g_conv_out, d_g_attn_out, d_g_ffn, d_g_final, d_w_conv, d_w_ffn) = sd
    (nm_g_mix, nm_b_f, nm_g_conv_out, nm_g_attn_out, nm_g_ffn, nm_g_final, nm_w_conv, nm_w_ffn) = sm
    (nv_g_mix, nv_b_f, nv_g_conv_out, nv_g_attn_out, nv_g_ffn, nv_g_final, nv_w_conv, nv_w_ffn) = sv

    grads = (g_g_mix, g_w_in[None], g_b_f, g_w_conv[None], g_g_conv_out, g_g_attn_out, g_w_o[None], g_g_ffn,
             g_w_up[None], g_w_ffn[None], g_w_down[None], g_g_final)
    deltas = (d_g_mix, u_w_in[0], d_b_f, d_w_conv, d_g_conv_out, d_g_attn_out, u_w_o[0], d_g_ffn, u_w_up[0],
              d_w_ffn, u_w_down[0], d_g_final)
    new_m = (nm_g_mix, u_w_in[1], nm_b_f, nm_w_conv, nm_g_conv_out, nm_g_attn_out, u_w_o[1], nm_g_ffn, u_w_up[1],
             nm_w_ffn, u_w_down[1], nm_g_final)
    new_v = (nv_g_mix, u_w_in[2], nv_b_f, nv_w_conv, nv_g_conv_out, nv_g_attn_out, u_w_o[2], nv_g_ffn, u_w_up[2],
             nv_w_ffn, u_w_down[2], nv_g_final)
    return (loss, grad_x[None], *grads, *deltas, *new_m, *new_v)
```

```python
import functools

import jax
import jax.numpy as jnp
from jax import lax
from jax.experimental import pallas as pl
from jax.experimental.pallas import tpu as pltpu

F32, BF16 = jnp.float32, jnp.bfloat16
MESH = pl.DeviceIdType.MESH

D_MODEL = 1024
CONV_CH = 512
ATTN_W = 512
N_HEADS = 8
HEAD_DIM = 64
N_PAIRS = N_HEADS // 2
D_FF = 2816
IN_COLS = 3 * CONV_CH + 3 * ATTN_W + N_HEADS
EPS = 1e-6
Q_SCALE = 0.125
EXP_ZERO = 104.0
N_CHIPS = 4
LANES = 128
HALO = 8

ADAM_LR, ADAM_B1, ADAM_B2, ADAM_EPS, ADAM_WD, ADAM_STEP = 0.001, 0.9, 0.999, 1e-08, 0.01, 10

TM_ROWS = 512
TM_MM = 1024
TK_TN = 512
TQ = 512
TM_FFN = 1024
TN_FFN = 256
VMEM_LIMIT = 52 * 2**20


def _cp(sem, vmem=VMEM_LIMIT):
    return pltpu.CompilerParams(dimension_semantics=sem, vmem_limit_bytes=vmem)


def _bf(a):
    return a if a.dtype == BF16 else a.astype(BF16)


def _mm(mode, a_list, b_list, out_dtype, tm, tn, name, add=None, b_chips=False):
    n_p = len(a_list)
    a0 = a_list[0]
    m_dim = a0[0].shape[1] if isinstance(a0, tuple) else a0.shape[0]
    b0 = b_list[0]
    if b_chips:
        n_dim = b0.shape[0] * b0.shape[2]
        assert tn == b0.shape[2] and mode == "nn"
    else:
        b0 = b0[0][b0[1]] if isinstance(b0, tuple) else b0
        n_dim = b0.shape[1 if mode == "nn" else 0]
    tm, tn = min(tm, m_dim), min(tn, n_dim)
    assert m_dim % tm == 0 and n_dim % tn == 0
    dims = (((1,), (0,)), ((), ())) if mode == "nn" else (((1,), (1,)), ((), ()))
    in_specs, args = [], []
    for a in a_list:
        if isinstance(a, tuple):
            arr, lead, col, width = a
            in_specs.append(pl.BlockSpec((None, tm, width), lambda m, n, lead=lead, col=col: (lead, m, col)))
        else:
            arr = a
            in_specs.append(pl.BlockSpec((tm, a.shape[1]), lambda m, n: (m, 0)))
        args.append(arr)
    for b in b_list:
        if b_chips:
            arr = b
            in_specs.append(pl.BlockSpec((None, b.shape[1], tn), lambda m, n: (n, 0, 0)))
        elif isinstance(b, tuple):
            arr, lead = b
            if mode == "nn":
                in_specs.append(pl.BlockSpec((None, arr.shape[1], tn), lambda m, n, lead=lead: (lead, 0, n)))
            else:
                in_specs.append(pl.BlockSpec((None, tn, arr.shape[2]), lambda m, n, lead=lead: (lead, n, 0)))
        elif mode == "nn":
            arr = b
            in_specs.append(pl.BlockSpec((b.shape[0], tn), lambda m, n: (0, n)))
        else:
            arr = b
            in_specs.append(pl.BlockSpec((tn, b.shape[1]), lambda m, n: (n, 0)))
        args.append(arr)
    if add is not None:
        in_specs.append(pl.BlockSpec((tm, tn), lambda m, n: (m, n)))
        args.append(add)

    def body(*refs):
        o_ref = refs[-1]
        acc = None
        for i in range(n_p):
            d = lax.dot_general(_bf(refs[i][...]), _bf(refs[n_p + i][...]), dims,
                                preferred_element_type=F32)
            acc = d if acc is None else acc + d
        if add is not None:
            acc = refs[2 * n_p][...] + acc
        o_ref[...] = acc.astype(out_dtype)

    return pl.pallas_call(
        body, name=name, grid=(m_dim // tm, n_dim // tn), in_specs=in_specs,
        out_specs=pl.BlockSpec((tm, tn), lambda m, n: (m, n)),
        out_shape=jax.ShapeDtypeStruct((m_dim, n_dim), out_dtype),
        compiler_params=_cp(("parallel", "parallel")))(*args)


def _mm_tn(a, b, tm, tn, name, out_chips=False):
    k_dim, m_dim = a.shape
    n_dim = b.shape[-1] * (b.shape[0] if b.ndim == 3 else 1)
    tm, tn, tk = min(tm, m_dim), min(tn, b.shape[-1]), min(TK_TN, k_dim)
    assert m_dim % tm == 0 and b.shape[-1] % tn == 0 and k_dim % tk == 0
    per = b.shape[-1] // tn
    if b.ndim == 3:
        b_spec = pl.BlockSpec((None, tk, tn), lambda m, n, k: (n // per, k, n % per))
    else:
        b_spec = pl.BlockSpec((tk, tn), lambda m, n, k: (k, n))

    def body(a_ref, b_ref, o_ref):
        @pl.when(pl.program_id(2) == 0)
        def _():
            o_ref[...] = jnp.zeros_like(o_ref)
        o_ref[...] += lax.dot_general(_bf(a_ref[...]), _bf(b_ref[...]), (((0,), (0,)), ((), ())),
                                      preferred_element_type=F32)

    return pl.pallas_call(
        body, name=name, grid=(m_dim // tm, n_dim // tn, k_dim // tk),
        in_specs=[pl.BlockSpec((tk, tm), lambda m, n, k: (k, m)), b_spec],
        out_specs=(pl.BlockSpec((None, tm, tn), lambda m, n, k: (n, m, 0)) if out_chips
                   else pl.BlockSpec((tm, tn), lambda m, n, k: (m, n))),
        out_shape=jax.ShapeDtypeStruct((n_dim // tn, m_dim, tn) if out_chips else (m_dim, n_dim), F32),
        compiler_params=_cp(("parallel", "parallel", "arbitrary")))(a, b)


def _rstd(x):
    return lax.rsqrt(jnp.mean(x * x, axis=-1, keepdims=True) + EPS)


def _rms_fwd(x, g, name):
    s, d = x.shape
    tm = min(TM_ROWS, s)

    def body(x_ref, g_ref, h_ref):
        xv = x_ref[...]
        h_ref[...] = (xv * _rstd(xv) * g_ref[...]).astype(BF16)

    return pl.pallas_call(
        body, name=name, grid=(s // tm,),
        in_specs=[pl.BlockSpec((tm, d), lambda i: (i, 0)), pl.BlockSpec((1, d), lambda i: (0, 0))],
        out_specs=pl.BlockSpec((tm, d), lambda i: (i, 0)),
        out_shape=jax.ShapeDtypeStruct((s, d), BF16), compiler_params=_cp(("parallel",)))(x, g)


def _rms_bwd(x, dh, g, dres, name, with_bf16):
    s, d = x.shape
    tm = min(TM_ROWS, s)

    def body(x_ref, dh_ref, g_ref, dres_ref, dx_ref, *rest):
        gg_ref = rest[-1]

        @pl.when(pl.program_id(0) == 0)
        def _():
            gg_ref[...] = jnp.zeros_like(gg_ref)

        xv = x_ref[...]
        xn = xv * _rstd(xv)
        dhv = dh_ref[...]
        gg_ref[...] += jnp.sum(dhv * xn, axis=0, keepdims=True)
        t = dhv * g_ref[...]
        dx = dres_ref[...] + _rstd(xv) * (t - xn * jnp.mean(t * xn, axis=-1, keepdims=True))
        dx_ref[...] = dx
        if with_bf16:
            rest[0][...] = dx.astype(BF16)

    row = pl.BlockSpec((tm, d), lambda i: (i, 0))
    vec = pl.BlockSpec((1, d), lambda i: (0, 0))
    out_specs = [row] + ([row] if with_bf16 else []) + [vec]
    out_shape = ([jax.ShapeDtypeStruct((s, d), F32)] + ([jax.ShapeDtypeStruct((s, d), BF16)] if with_bf16 else [])
                 + [jax.ShapeDtypeStruct((1, d), F32)])
    return pl.pallas_call(
        body, name=name, grid=(s // tm,), in_specs=[row, row, vec, row], out_specs=out_specs, out_shape=out_shape,
        compiler_params=_cp(("arbitrary",)))(x, dh, g, dres)


def _loss_head(x3, target, g):
    s, d = x3.shape
    tm = min(TM_ROWS, s)

    def body(x_ref, t_ref, g_ref, dx_ref, dxb_ref, loss_ref, gg_ref):
        @pl.when(pl.program_id(0) == 0)
        def _():
            gg_ref[...] = jnp.zeros_like(gg_ref)
            loss_ref[...] = jnp.zeros_like(loss_ref)

        xv = x_ref[...]
        r = _rstd(xv)
        xn = xv * r
        gv = g_ref[...]
        err = xn * gv - t_ref[...]
        loss_ref[...] += 0.5 * jnp.sum(jnp.mean(err * err, axis=-1, keepdims=True), axis=0, keepdims=True)
        dy = err * (1.0 / d)
        gg_ref[...] += jnp.sum(dy * xn, axis=0, keepdims=True)
        t = dy * gv
        dx = r * (t - xn * jnp.mean(t * xn, axis=-1, keepdims=True))
        dx_ref[...] = dx
        dxb_ref[...] = dx.astype(BF16)

    row = pl.BlockSpec((tm, d), lambda i: (i, 0))
    vec = pl.BlockSpec((1, d), lambda i: (0, 0))
    return pl.pallas_call(
        body, name="loss_head", grid=(s // tm,), in_specs=[row, row, vec],
        out_specs=[row, row, pl.BlockSpec((1, LANES), lambda i: (0, 0)), vec],
        out_shape=[jax.ShapeDtypeStruct((s, d), F32), jax.ShapeDtypeStruct((s, d), BF16),
                   jax.ShapeDtypeStruct((1, LANES), F32), jax.ShapeDtypeStruct((1, d), F32)],
        compiler_params=_cp(("arbitrary",)))(x3, target, g)


def _prev_halo_spec(tm, width, col):
    return pl.BlockSpec((HALO, width), lambda i, *_: (jnp.maximum(i * (tm // HALO) - 1, 0), col))


def _next_halo_spec(tm, width, col, s):
    return pl.BlockSpec((HALO, width), lambda i, *_: (jnp.minimum((i + 1) * (tm // HALO), s // HALO - 1), col))


def _shift_down(x, k):
    return pltpu.roll(x, k, 0)


def _shift_up(x, k):
    return pltpu.roll(x, x.shape[0] - k, 0)


def _conv_taps(x_ext, w):
    return w[0:1, :] * _shift_down(x_ext, 2) + w[1:2, :] * _shift_down(x_ext, 1) + w[2:3, :] * x_ext


def _conv_taps_t(d_ext, w):
    return w[2:3, :] * d_ext + w[1:2, :] * _shift_up(d_ext, 1) + w[0:1, :] * _shift_up(d_ext, 2)


def _mixer_fwd(z_a, o_attn, w_conv, g_conv_out, g_attn_out):
    s = z_a.shape[0]
    c = CONV_CH
    tm = min(TM_ROWS, s)

    def body(gb_ref, gc_ref, xc_ref, gcp_ref, xcp_ref, o_ref, w_ref, gco_ref, gao_ref, mix_ref):
        i = pl.program_id(0)
        cx = gc_ref[...] * xc_ref[...]
        cx_prev = jnp.where(i > 0, gcp_ref[...] * xcp_ref[...], 0.0)
        conv = _conv_taps(jnp.concatenate([cx_prev, cx], axis=0), w_ref[...])[HALO:]
        y = gb_ref[...] * conv
        mix_ref[:, 0:c] = (y * _rstd(y) * gco_ref[...]).astype(BF16)
        o = o_ref[...]
        mix_ref[:, c:2 * c] = (o * _rstd(o) * gao_ref[...]).astype(BF16)

    col = lambda j: pl.BlockSpec((tm, c), lambda i: (i, j))
    vec = pl.BlockSpec((1, c), lambda i: (0, 0))
    return pl.pallas_call(
        body, name="mixer_fwd", grid=(s // tm,),
        in_specs=[col(0), col(1), col(2), _prev_halo_spec(tm, c, 1), _prev_halo_spec(tm, c, 2), col(0),
                  pl.BlockSpec((3, c), lambda i: (0, 0)), vec, vec],
        out_specs=pl.BlockSpec((tm, 2 * c), lambda i: (i, 0)),
        out_shape=jax.ShapeDtypeStruct((s, 2 * c), BF16),
        compiler_params=_cp(("parallel",)))(z_a, z_a, z_a, z_a, z_a, o_attn, w_conv, g_conv_out, g_attn_out)


def _mixer_bwd(z_a, o_attn, dmix, w_conv, g_conv_out, g_attn_out):
    s = z_a.shape[0]
    c = CONV_CH
    tm = min(TM_ROWS, s)
    n_blk = s // tm

    def body(gb_ref, gc_ref, xc_ref, gcp_ref, xcp_ref, gbn_ref, gcn_ref, xcn_ref, o_ref, dnc_ref, dncn_ref, dna_ref,
             w_ref, gco_ref, gao_ref, dz_ref, do_ref, delta_ref, gw_ref, ggco_ref, ggao_ref):
        i = pl.program_id(0)

        @pl.when(i == 0)
        def _():
            gw_ref[...] = jnp.zeros_like(gw_ref)
            ggco_ref[...] = jnp.zeros_like(ggco_ref)
            ggao_ref[...] = jnp.zeros_like(ggao_ref)

        w = w_ref[...]
        zeros = jnp.zeros((HALO, c), F32)
        gb_e = jnp.concatenate([zeros, gb_ref[...], gbn_ref[...]], axis=0)
        cx_prev = jnp.where(i > 0, gcp_ref[...] * xcp_ref[...], 0.0)
        gc_e = jnp.concatenate([zeros, gc_ref[...], gcn_ref[...]], axis=0)
        xc_e = jnp.concatenate([zeros, xc_ref[...], xcn_ref[...]], axis=0)
        cx_e = jnp.concatenate([cx_prev, gc_ref[...] * xc_ref[...], gcn_ref[...] * xcn_ref[...]], axis=0)
        dn_next = jnp.where(i < n_blk - 1, dncn_ref[...], 0.0)
        dn_e = jnp.concatenate([zeros, dnc_ref[...], dn_next], axis=0)

        conv_e = _conv_taps(cx_e, w)
        y_e = gb_e * conv_e
        r_e = _rstd(y_e)
        yn_e = y_e * r_e
        t_e = dn_e * gco_ref[...]
        dy_e = r_e * (t_e - yn_e * jnp.mean(t_e * yn_e, axis=-1, keepdims=True))
        dconv_e = dy_e * gb_e
        dcx_e = _conv_taps_t(dconv_e, w)
        blk = slice(HALO, HALO + tm)
        dz_ref[:, 0:c] = (dy_e * conv_e)[blk].astype(BF16)
        dz_ref[:, c:2 * c] = (dcx_e * xc_e)[blk].astype(BF16)
        dz_ref[:, 2 * c:3 * c] = (dcx_e * gc_e)[blk].astype(BF16)
        ggco_ref[...] += jnp.sum((dn_e * yn_e)[blk], axis=0, keepdims=True)
        dconv = dconv_e[blk]
        gw_ref[0:1, :] += jnp.sum(dconv * _shift_down(cx_e, 2)[blk], axis=0, keepdims=True)
        gw_ref[1:2, :] += jnp.sum(dconv * _shift_down(cx_e, 1)[blk], axis=0, keepdims=True)
        gw_ref[2:3, :] += jnp.sum(dconv * cx_e[blk], axis=0, keepdims=True)

        o = o_ref[...]
        ra = _rstd(o)
        on = o * ra
        dna = dna_ref[...]
        ggao_ref[...] += jnp.sum(dna * on, axis=0, keepdims=True)
        ta = dna * gao_ref[...]
        do = ra * (ta - on * jnp.mean(ta * on, axis=-1, keepdims=True))
        do_ref[...] = do.astype(BF16)
        prod = do * o
        head_a = lax.broadcasted_iota(jnp.int32, (tm, LANES), 1) < HEAD_DIM
        for p in range(N_PAIRS):
            pb = prod[:, p * LANES:(p + 1) * LANES]
            sa = jnp.sum(jnp.where(head_a, pb, 0.0), axis=-1, keepdims=True)
            sb = jnp.sum(jnp.where(head_a, 0.0, pb), axis=-1, keepdims=True)
            delta_ref[:, p * LANES:(p + 1) * LANES] = jnp.where(head_a, sa, sb)

    col = lambda j: pl.BlockSpec((tm, c), lambda i: (i, j))
    vec = pl.BlockSpec((1, c), lambda i: (0, 0))
    w3 = pl.BlockSpec((3, c), lambda i: (0, 0))
    return pl.pallas_call(
        body, name="mixer_bwd", grid=(n_blk,),
        in_specs=[col(0), col(1), col(2), _prev_halo_spec(tm, c, 1), _prev_halo_spec(tm, c, 2),
                  _next_halo_spec(tm, c, 0, s), _next_halo_spec(tm, c, 1, s), _next_halo_spec(tm, c, 2, s),
                  col(0), col(0), _next_halo_spec(tm, c, 0, s), col(1), w3, vec, vec],
        out_specs=[pl.BlockSpec((tm, 3 * c), lambda i: (i, 0)), col(0), col(0), w3, vec, vec],
        out_shape=[jax.ShapeDtypeStruct((s, 3 * c), BF16), jax.ShapeDtypeStruct((s, c), BF16),
                   jax.ShapeDtypeStruct((s, c), F32), jax.ShapeDtypeStruct((3, c), F32),
                   jax.ShapeDtypeStruct((1, c), F32), jax.ShapeDtypeStruct((1, c), F32)],
        compiler_params=_cp(("arbitrary",)))(
            z_a, z_a, z_a, z_a, z_a, z_a, z_a, z_a, o_attn, dmix, dmix, dmix, w_conv, g_conv_out, g_attn_out)


def _gate_fwd(f, b_pad):
    s = f.shape[0]
    tm = min(TQ, s)

    def body(f_ref, b_ref, fb_ref, ft_ref, carry):
        @pl.when(pl.program_id(0) == 0)
        def _():
            carry[...] = jnp.zeros_like(carry)

        z = f_ref[...] + b_ref[...]
        x = jnp.minimum(z, 0.0) - jnp.log1p(jnp.exp(-jnp.abs(z)))
        row = lax.broadcasted_iota(jnp.int32, (tm, LANES), 0)
        sh = 1
        while sh < tm:
            x = x + jnp.where(row >= sh, _shift_down(x, sh), 0.0)
            sh *= 2
        x = x + carry[0:1, :]
        carry[...] = jnp.broadcast_to(x[tm - 1:tm, :], carry.shape)
        head_a = lax.broadcasted_iota(jnp.int32, (tm, LANES), 1) < HEAD_DIM
        for p in range(N_PAIRS):
            fa = jnp.broadcast_to(x[:, 2 * p:2 * p + 1], (tm, LANES))
            fbv = jnp.broadcast_to(x[:, 2 * p + 1:2 * p + 2], (tm, LANES))
            fb_ref[:, p * LANES:(p + 1) * LANES] = jnp.where(head_a, fa, fbv)
        ft_ref[0] = x.T[0:N_HEADS, :]

    return pl.pallas_call(
        body, name="gate_fwd", grid=(s // tm,),
        in_specs=[pl.BlockSpec((tm, LANES), lambda i: (i, 0)), pl.BlockSpec((1, LANES), lambda i: (0, 0))],
        out_specs=[pl.BlockSpec((tm, N_PAIRS * LANES), lambda i: (i, 0)),
                   pl.BlockSpec((1, N_HEADS, tm), lambda i: (i, 0, 0))],
        out_shape=[jax.ShapeDtypeStruct((s, N_PAIRS * LANES), F32),
                   jax.ShapeDtypeStruct((s // tm, N_HEADS, tm), F32)],
        scratch_shapes=[pltpu.VMEM((HALO, LANES), F32)],
        compiler_params=_cp(("arbitrary",)))(f, b_pad)


def _gate_bwd(f, b_pad, d_f):
    s = f.shape[0]
    tm = min(TQ, s)
    n_blk = s // tm

    def body(f_ref, b_ref, d_ref, df_ref, gb_ref, carry):
        @pl.when(pl.program_id(0) == 0)
        def _():
            carry[...] = jnp.zeros_like(carry)
            gb_ref[...] = jnp.zeros_like(gb_ref)

        x = d_ref[...]
        row = lax.broadcasted_iota(jnp.int32, (tm, LANES), 0)
        sh = 1
        while sh < tm:
            x = x + jnp.where(row < tm - sh, _shift_up(x, sh), 0.0)
            sh *= 2
        x = x + carry[0:1, :]
        carry[...] = jnp.broadcast_to(x[0:1, :], carry.shape)
        z = f_ref[...] + b_ref[...]
        d = x * (1.0 / (1.0 + jnp.exp(z)))
        df_ref[...] = d.astype(BF16)
        gb_ref[...] += jnp.sum(d, axis=0, keepdims=True)

    rev = pl.BlockSpec((tm, LANES), lambda i: (n_blk - 1 - i, 0))
    vec = pl.BlockSpec((1, LANES), lambda i: (0, 0))
    return pl.pallas_call(
        body, name="gate_bwd", grid=(n_blk,), in_specs=[rev, vec, rev], out_specs=[rev, vec],
        out_shape=[jax.ShapeDtypeStruct((s, LANES), BF16), jax.ShapeDtypeStruct((1, LANES), F32)],
        scratch_shapes=[pltpu.VMEM((HALO, LANES), F32)],
        compiler_params=_cp(("arbitrary",)))(f, b_pad, d_f)


_NT = (((1,), (1,)), ((), ()))
_NN = (((1,), (0,)), ((), ()))
_TN = (((0,), (0,)), ((), ()))


def _head_masks(shape):
    lane = lax.broadcasted_iota(jnp.int32, shape, len(shape) - 1)
    return lane < HEAD_DIM


def _pick_row(ft, h):
    rows = lax.broadcasted_iota(jnp.int32, ft.shape, 0)
    return jnp.sum(jnp.where(rows == h, ft, 0.0), axis=0, keepdims=True)


def _prune_bounds(qkv, fb):
    s = qkv.shape[0]
    tq = min(TQ, s)

    def body(q_ref, k_ref, fb_ref, out_ref):
        head_a = _head_masks((tq, LANES))
        lane = lax.broadcasted_iota(jnp.int32, (HALO, LANES), 1)
        acc = jnp.zeros((HALO, LANES), F32)
        for p in range(N_PAIRS):
            cols = slice(p * LANES, (p + 1) * LANES)
            q2 = q_ref[:, cols].astype(F32) * Q_SCALE
            k2 = k_ref[:, cols].astype(F32)
            f2 = fb_ref[:, cols]
            for hh in range(2):
                sel = head_a if hh == 0 else jnp.logical_not(head_a)
                qn = jnp.sqrt(jnp.sum(jnp.where(sel, q2 * q2, 0.0), axis=-1, keepdims=True))
                kn = jnp.sqrt(jnp.sum(jnp.where(sel, k2 * k2, 0.0), axis=-1, keepdims=True))
                f = f2[:, hh * HEAD_DIM:hh * HEAD_DIM + 1]
                h = 2 * p + hh
                vals = (jnp.max(qn, axis=0, keepdims=True), jnp.max(kn, axis=0, keepdims=True),
                        jnp.max(qn * kn + f, axis=0, keepdims=True), f[tq - 1:tq, :])
                for slot, v in enumerate(vals):
                    acc = jnp.where(lane == slot * N_HEADS + h, v, acc)
        out_ref[0] = acc

    blk = lambda j: pl.BlockSpec((tq, ATTN_W), lambda i: (i, j))
    return pl.pallas_call(
        body, name="prune_bounds", grid=(s // tq,), in_specs=[blk(0), blk(1), blk(0)],
        out_specs=pl.BlockSpec((1, HALO, LANES), lambda i: (i, 0, 0)),
        out_shape=jax.ShapeDtypeStruct((s // tq, HALO, LANES), F32),
        compiler_params=_cp(("parallel",)))(qkv, qkv, fb)


def _first_key_blocks(qkv, fb):
    t = _prune_bounds(qkv, fb)[:, 0, :]
    nh = N_HEADS
    a, b, c, e = t[:, 0:nh], t[:, nh:2 * nh], t[:, 2 * nh:3 * nh], t[:, 3 * nh:4 * nh]
    bound = a[:, None, :] * b[None, :, :] * 1.001 + c[:, None, :] - e[None, :, :]
    n_q = t.shape[0]
    idx = jnp.arange(n_q)
    need = jnp.logical_not(bound < -(EXP_ZERO + 2.0)) | (idx[None, :, None] >= idx[:, None, None])
    first = jnp.argmax(need, axis=1).astype(jnp.int32)
    return jnp.min(first.reshape(n_q, N_PAIRS, 2), axis=-1).T.reshape(-1)


def _attn_fwd(qkv, fb, ft, first_blk):
    s = qkv.shape[0]
    tq = min(TQ, s)
    n_q = s // tq
    neg = -1e30

    def body(first_ref, q_ref, k_ref, v_ref, fb_ref, ft_ref, o_ref, g_ref):
        p = pl.program_id(0)
        i = pl.program_id(1)
        head_a = _head_masks((tq, LANES))
        q2 = q_ref[...] * Q_SCALE
        zero = jnp.zeros_like(q2)
        q_a, q_b = jnp.where(head_a, q2, zero), jnp.where(head_a, zero, q2)
        fq2 = fb_ref[...]
        fq_a, fq_b = fq2[:, 0:1], fq2[:, HEAD_DIM:HEAD_DIM + 1]
        causal = lax.broadcasted_iota(jnp.int32, (tq, tq), 1) <= lax.broadcasted_iota(jnp.int32, (tq, tq), 0)

        def one_head(q_h, fq_h, fk_h, k2, m, l, masked):
            sc = (lax.dot_general(q_h, k2, _NT, preferred_element_type=F32) + fq_h) - fk_h
            if masked:
                sc = jnp.where(causal, sc, -jnp.inf)
            m_new = jnp.maximum(m, jnp.max(sc, axis=-1, keepdims=True))
            pr = jnp.exp(sc - m_new)
            alpha = jnp.exp(m - m_new)
            return pr.astype(BF16), m_new, alpha * l + jnp.sum(pr, axis=-1, keepdims=True), alpha

        def step(kb, carry, masked):
            m_a, l_a, m_b, l_b, acc = carry
            rows = pl.ds(pl.multiple_of(kb * tq, tq), tq)
            k2, v2 = k_ref[rows, :], v_ref[rows, :]
            ftv = ft_ref[kb]
            p_a, m_a, l_a, al_a = one_head(q_a, fq_a, _pick_row(ftv, 2 * p), k2, m_a, l_a, masked)
            p_b, m_b, l_b, al_b = one_head(q_b, fq_b, _pick_row(ftv, 2 * p + 1), k2, m_b, l_b, masked)
            zv = jnp.zeros_like(v2)
            head_k = _head_masks(v2.shape)
            pv = (lax.dot_general(p_a, jnp.where(head_k, v2, zv), _NN, preferred_element_type=F32)
                  + lax.dot_general(p_b, jnp.where(head_k, zv, v2), _NN, preferred_element_type=F32))
            acc = acc * jnp.where(head_a, al_a, al_b) + pv
            return m_a, l_a, m_b, l_b, acc

        col = lambda v: jnp.full((tq, 1), v, F32)
        carry = (col(neg), col(0.0), col(neg), col(0.0), jnp.zeros((tq, LANES), F32))
        carry = lax.fori_loop(first_ref[p * n_q + i], i, lambda kb, cr: step(kb, cr, False), carry)
        m_a, l_a, m_b, l_b, acc = step(i, carry, True)
        o_ref[...] = acc / jnp.where(head_a, l_a, l_b)
        g_ref[...] = fq2 - jnp.where(head_a, m_a + jnp.log(l_a), m_b + jnp.log(l_b))

    qblk = lambda off: pl.BlockSpec((tq, LANES), lambda p, i, first: (i, off + p))
    full = lambda off: pl.BlockSpec((s, LANES), lambda p, i, first: (0, off + p))
    grid_spec = pltpu.PrefetchScalarGridSpec(
        num_scalar_prefetch=1, grid=(N_PAIRS, n_q),
        in_specs=[qblk(0), full(N_PAIRS), full(2 * N_PAIRS), qblk(0),
                  pl.BlockSpec((n_q, N_HEADS, tq), lambda p, i, first: (0, 0, 0))],
        out_specs=[qblk(0), qblk(0)])
    return pl.pallas_call(
        body, name="attn_fwd", grid_spec=grid_spec,
        out_shape=[jax.ShapeDtypeStruct((s, ATTN_W), F32), jax.ShapeDtypeStruct((s, ATTN_W), F32)],
        compiler_params=_cp(("parallel", "arbitrary")))(first_blk, qkv, qkv, qkv, fb, ft)


def _attn_bwd(qkv, do, g, delta, ft, first_blk):
    s = qkv.shape[0]
    tq = min(TQ, s)
    n_q = s // tq

    def body(first_ref, q_ref, do_ref, g_ref, dl_ref, k_ref, v_ref, ft_ref, dq_ref, dfq_ref, dk_ref, dv_ref,
             dfk_ref):
        p = pl.program_id(0)
        i = pl.program_id(1)

        @pl.when(i == 0)
        def _():
            dk_ref[...] = jnp.zeros_like(dk_ref)
            dv_ref[...] = jnp.zeros_like(dv_ref)
            dfk_ref[...] = jnp.zeros_like(dfk_ref)

        head_a = _head_masks((tq, LANES))
        q2 = q_ref[...] * Q_SCALE
        do2 = do_ref[...]
        zero = jnp.zeros_like(q2)
        q_a, q_b = jnp.where(head_a, q2, zero), jnp.where(head_a, zero, q2)
        do_a, do_b = jnp.where(head_a, do2, zero), jnp.where(head_a, zero, do2)
        g2, dl2 = g_ref[...], dl_ref[...]
        g_a, g_b = g2[:, 0:1], g2[:, HEAD_DIM:HEAD_DIM + 1]
        dl_a, dl_b = dl2[:, 0:1], dl2[:, HEAD_DIM:HEAD_DIM + 1]
        causal = lax.broadcasted_iota(jnp.int32, (tq, tq), 1) <= lax.broadcasted_iota(jnp.int32, (tq, tq), 0)
        rows8 = lax.broadcasted_iota(jnp.int32, (N_HEADS, tq), 0)

        def one_head(q_h, do_h, g_h, dl_h, fk_h, k2, v2, masked):
            sc = (lax.dot_general(q_h, k2, _NT, preferred_element_type=F32) + g_h) - fk_h
            pr = jnp.exp(sc)
            if masked:
                pr = jnp.where(causal, pr, 0.0)
            dp = lax.dot_general(do_h, v2, _NT, preferred_element_type=F32)
            ds = pr * (dp - dl_h)
            return (pr.astype(BF16), ds.astype(BF16), jnp.sum(ds, axis=0, keepdims=True),
                    jnp.sum(ds, axis=1, keepdims=True))

        def step(j, carry, masked):
            dq, r_a, r_b = carry
            rows = pl.ds(pl.multiple_of(j * tq, tq), tq)
            k2, v2 = k_ref[rows, :], v_ref[rows, :]
            k_a, k_b = jnp.where(head_a, k2, zero), jnp.where(head_a, zero, k2)
            ftv = ft_ref[j]
            p_a, ds_a, c_a, s_a = one_head(q_a, do_a, g_a, dl_a, _pick_row(ftv, 2 * p), k2, v2, masked)
            p_b, ds_b, c_b, s_b = one_head(q_b, do_b, g_b, dl_b, _pick_row(ftv, 2 * p + 1), k2, v2, masked)
            dv_ref[rows, :] += (lax.dot_general(p_a, do_a, _TN, preferred_element_type=F32)
                                + lax.dot_general(p_b, do_b, _TN, preferred_element_type=F32))
            dk_ref[rows, :] += (lax.dot_general(ds_a, q_a, _TN, preferred_element_type=F32)
                                + lax.dot_general(ds_b, q_b, _TN, preferred_element_type=F32))
            dfk_ref[0, j] += jnp.where(rows8 == 0, -c_a, jnp.where(rows8 == 1, -c_b, 0.0))
            dq = dq + (lax.dot_general(ds_a, k_a, _NN, preferred_element_type=F32)
                       + lax.dot_general(ds_b, k_b, _NN, preferred_element_type=F32))
            return dq, r_a + s_a, r_b + s_b

        zcol = jnp.zeros((tq, 1), F32)
        carry = lax.fori_loop(first_ref[p * n_q + i], i, lambda j, cr: step(j, cr, False),
                              (jnp.zeros((tq, LANES), F32), zcol, zcol))
        dq, r_a, r_b = step(i, carry, True)
        dq_ref[...] = (dq * Q_SCALE).astype(BF16)
        dfq_ref[...] = jnp.where(head_a, r_a, r_b)

    qblk = lambda off: pl.BlockSpec((tq, LANES), lambda p, i, first: (i, off + p))
    full = lambda off: pl.BlockSpec((s, LANES), lambda p, i, first: (0, off + p))
    grid_spec = pltpu.PrefetchScalarGridSpec(
        num_scalar_prefetch=1, grid=(N_PAIRS, n_q),
        in_specs=[qblk(0), qblk(0), qblk(0), qblk(0), full(N_PAIRS), full(2 * N_PAIRS),
                  pl.BlockSpec((n_q, N_HEADS, tq), lambda p, i, first: (0, 0, 0))],
        out_specs=[qblk(0), qblk(0), full(0), full(0),
                   pl.BlockSpec((1, n_q, N_HEADS, tq), lambda p, i, first: (p, 0, 0, 0))])
    return pl.pallas_call(
        body, name="attn_bwd", grid_spec=grid_spec,
        out_shape=[jax.ShapeDtypeStruct((s, ATTN_W), BF16), jax.ShapeDtypeStruct((s, ATTN_W), F32),
                   jax.ShapeDtypeStruct((s, ATTN_W), F32), jax.ShapeDtypeStruct((s, ATTN_W), F32),
                   jax.ShapeDtypeStruct((N_PAIRS, n_q, N_HEADS, tq), F32)],
        compiler_params=_cp(("parallel", "arbitrary")))(first_blk, qkv, do, g, delta, qkv, qkv, ft)


def _ffn_act_fwd(up, w_ffn):
    s = up.shape[0]
    tm, tn = min(TM_FFN, s), TN_FFN
    nb = D_FF // tn

    def body(a_ref, g_ref, ap_ref, gp_ref, wa_ref, wg_ref, act_ref):
        i = pl.program_id(1)

        def conv(blk_ref, prev_ref, w_ref):
            prev = jnp.where(i > 0, prev_ref[...], 0.0)
            return _conv_taps(jnp.concatenate([prev, blk_ref[...]], axis=0), w_ref[...])[HALO:]

        u_a, u_g = conv(a_ref, ap_ref, wa_ref), conv(g_ref, gp_ref, wg_ref)
        act_ref[...] = (u_g * (1.0 / (1.0 + jnp.exp(-u_g))) * u_a).astype(BF16)

    blk = lambda off: pl.BlockSpec((tm, tn), lambda n, i: (i, off + n))
    prev = lambda off: pl.BlockSpec((HALO, tn), lambda n, i: (jnp.maximum(i * (tm // HALO) - 1, 0), off + n))
    wsp = lambda off: pl.BlockSpec((3, tn), lambda n, i: (0, off + n))
    return pl.pallas_call(
        body, name="ffn_act_fwd", grid=(nb, s // tm),
        in_specs=[blk(0), blk(nb), prev(0), prev(nb), wsp(0), wsp(nb)],
        out_specs=pl.BlockSpec((tm, tn), lambda n, i: (i, n)),
        out_shape=jax.ShapeDtypeStruct((s, D_FF), BF16),
        compiler_params=_cp(("parallel", "parallel")))(up, up, up, up, w_ffn, w_ffn)


def _ffn_act_bwd(up, dact, w_ffn):
    s = up.shape[0]
    tm, tn = min(TM_FFN, s), TN_FFN
    nb = D_FF // tn
    n_blk = s // tm

    def body(a_ref, g_ref, ap_ref, gp_ref, an_ref, gn_ref, d_ref, dn_ref, wa_ref, wg_ref,
             dup_ref, gwa_ref, gwg_ref):
        i = pl.program_id(1)

        @pl.when(i == 0)
        def _():
            gwa_ref[...] = jnp.zeros_like(gwa_ref)
            gwg_ref[...] = jnp.zeros_like(gwg_ref)

        def ext(blk_ref, prev_ref, next_ref):
            return jnp.concatenate([jnp.where(i > 0, prev_ref[...], 0.0), blk_ref[...], next_ref[...]], axis=0)

        wa, wg = wa_ref[...], wg_ref[...]
        up_a, up_g = ext(a_ref, ap_ref, an_ref), ext(g_ref, gp_ref, gn_ref)
        u_a, u_g = _conv_taps(up_a, wa), _conv_taps(up_g, wg)
        d_e = jnp.concatenate([jnp.zeros((HALO, tn), F32), d_ref[...],
                               jnp.where(i < n_blk - 1, dn_ref[...], 0.0)], axis=0)
        sig = 1.0 / (1.0 + jnp.exp(-u_g))
        du_a = d_e * (u_g * sig)
        du_g = d_e * u_a * (sig * (1.0 + u_g * (1.0 - sig)))
        blk = slice(HALO, HALO + tm)
        dup_ref[0] = _conv_taps_t(du_a, wa)[blk].astype(BF16)
        dup_ref[1] = _conv_taps_t(du_g, wg)[blk].astype(BF16)
        for gw_ref, upv, du in ((gwa_ref, up_a[blk], du_a), (gwg_ref, up_g[blk], du_g)):
            gw_ref[0:1, :] += jnp.sum(upv * _shift_up(du, 2)[blk], axis=0, keepdims=True)
            gw_ref[1:2, :] += jnp.sum(upv * _shift_up(du, 1)[blk], axis=0, keepdims=True)
            gw_ref[2:3, :] += jnp.sum(upv * du[blk], axis=0, keepdims=True)

    blk = lambda off: pl.BlockSpec((tm, tn), lambda n, i: (i, off + n))
    prev = lambda off: pl.BlockSpec((HALO, tn), lambda n, i: (jnp.maximum(i * (tm // HALO) - 1, 0), off + n))
    nxt = lambda off: pl.BlockSpec(
        (HALO, tn), lambda n, i: (jnp.minimum((i + 1) * (tm // HALO), s // HALO - 1), off + n))
    wsp = lambda off: pl.BlockSpec((3, tn), lambda n, i: (0, off + n))
    return pl.pallas_call(
        body, name="ffn_act_bwd", grid=(nb, n_blk),
        in_specs=[blk(0), blk(nb), prev(0), prev(nb), nxt(0), nxt(nb), blk(0), nxt(0), wsp(0), wsp(nb)],
        out_specs=[pl.BlockSpec((2, tm, tn), lambda n, i: (0, i, n)), wsp(0), wsp(0)],
        out_shape=[jax.ShapeDtypeStruct((2, s, D_FF), BF16),
                   jax.ShapeDtypeStruct((3, D_FF), F32), jax.ShapeDtypeStruct((3, D_FF), F32)],
        compiler_params=_cp(("parallel", "arbitrary")))(up, up, up, up, up, up, dact, dact, w_ffn, w_ffn)


def _adamw(w, g, m, v, name):
    r, c = w.shape
    tr = next((t for t in (512, 352, 256, 128, 64, 32, 16, 8) if r > t and r % t == 0), r)

    def body(w_ref, g_ref, m_ref, v_ref, d_ref, nm_ref, nv_ref):
        gv = g_ref[...]
        m_new = ADAM_B1 * m_ref[...] + (1.0 - ADAM_B1) * gv
        v_new = ADAM_B2 * v_ref[...] + (1.0 - ADAM_B2) * (gv * gv)
        m_hat = m_new / (1.0 - ADAM_B1 ** ADAM_STEP)
        v_hat = v_new / (1.0 - ADAM_B2 ** ADAM_STEP)
        d_ref[...] = -ADAM_LR * (m_hat / (jnp.sqrt(v_hat) + ADAM_EPS) + ADAM_WD * w_ref[...])
        nm_ref[...] = m_new
        nv_ref[...] = v_new

    spec = pl.BlockSpec((tr, c), lambda i: (i, 0))
    shp = jax.ShapeDtypeStruct((r, c), F32)
    return pl.pallas_call(
        body, name=name, grid=(r // tr,), in_specs=[spec] * 4, out_specs=[spec] * 3, out_shape=[shp] * 3,
        compiler_params=_cp(("parallel",)))(w, g, m, v)


def _sum_rows_block(h):
    return h if h <= 352 else 256


def _pair_sum(view, recv, sel, name):
    n, _, h, c = view.shape
    tr = _sum_rows_block(h)

    def body(sel_ref, a_ref, b_ref, o_ref, ob_ref):
        t = a_ref[...] + b_ref[...]
        o_ref[...] = t
        ob_ref[...] = t.astype(BF16)

    blk = pl.BlockSpec((None, tr, c), lambda j, i, sel_ref: (j, i, 0))
    grid_spec = pltpu.PrefetchScalarGridSpec(
        num_scalar_prefetch=1, grid=(n, h // tr),
        in_specs=[pl.BlockSpec((None, None, tr, c), lambda j, i, sel_ref: (j, sel_ref[0], i, 0)),
                  pl.BlockSpec((None, None, tr, c), lambda j, i, sel_ref: (j, 0, i, 0))],
        out_specs=[blk, blk])
    return pl.pallas_call(
        body, name=name, grid_spec=grid_spec,
        out_shape=[jax.ShapeDtypeStruct((n, h, c), F32), jax.ShapeDtypeStruct((n, h, c), BF16)],
        compiler_params=_cp(("parallel", "parallel")))(sel, view, recv)


def _chip_sum(pair, got, sel, name):
    _, h, c = pair.shape
    tr = _sum_rows_block(h)

    def body(sel_ref, p_ref, g0_ref, g1_ref, g2_ref, o_ref):
        o_ref[...] = ((p_ref[...] + g0_ref[...].astype(F32)) + g1_ref[...].astype(F32)) + g2_ref[...].astype(F32)

    slot = lambda k: pl.BlockSpec((None, tr, c), lambda i, sel_ref: (k, i, 0))
    grid_spec = pltpu.PrefetchScalarGridSpec(
        num_scalar_prefetch=1, grid=(h // tr,),
        in_specs=[pl.BlockSpec((None, tr, c), lambda i, sel_ref: (sel_ref[1], i, 0)), slot(0), slot(1), slot(2)],
        out_specs=pl.BlockSpec((tr, c), lambda i, sel_ref: (i, 0)))
    return pl.pallas_call(
        body, name=name, grid_spec=grid_spec, out_shape=jax.ShapeDtypeStruct((h, c), F32),
        compiler_params=_cp(("parallel",)))(sel, pair, got, got, got)


def _place():
    return lax.axis_index("x"), lax.axis_index("y"), lax.axis_index("c")


def _other_chips(x, y):
    return [(1 - x, y), (x, 1 - y), (1 - x, 1 - y)]


def _hbm_specs(n):
    return [pl.BlockSpec(memory_space=pl.ANY)] * n


def _all_gather_weights(bigs, smalls):
    nb, ns = len(bigs), len(smalls)
    n = nb + ns

    def body(*refs):
        ins, outs = refs[:n], refs[n:2 * n]
        send_sems, recv_sems, local_sems = refs[2 * n:]
        x, y, c = _place()
        my_chip = 2 * x + y
        chips = _other_chips(x, y)
        sibling = (x, y, 1 - c)

        def rows(k, which):
            h = ins[k].shape[0] // 2
            return pl.ds(which * h, h)

        def copy(sem, src, dst, to):
            return pltpu.make_async_remote_copy(src_ref=src, dst_ref=dst, send_sem=send_sems.at[sem],
                                                recv_sem=recv_sems.at[sem], device_id=to, device_id_type=MESH)

        local = [pltpu.make_async_copy(ins[k], outs[k].at[my_chip], local_sems.at[k]) for k in range(n)]
        for cp in local:
            cp.start()
        sent = []
        for k in range(nb):
            for j, (cx, cy) in enumerate(chips):
                sent.append(copy(6 * k + j, ins[k].at[rows(k, c)], outs[k].at[my_chip, rows(k, c)], (cx, cy, c)))
        for k in range(nb, n):
            for j, (cx, cy) in enumerate(chips):
                sent.append(copy(6 * nb + 3 * (k - nb) + j, ins[k], outs[k].at[my_chip], (cx, cy, c)))
        for cp in sent:
            cp.start()
        for j, (cx, cy) in enumerate(chips):
            for k in range(nb):
                landed = outs[k].at[2 * cx + cy, rows(k, c)]
                copy(6 * k + j, landed, landed, (x, y, c)).wait_recv()
                fwd = copy(6 * k + 3 + j, landed, landed, sibling)
                fwd.start()
                sent.append(fwd)
        for j, (cx, cy) in enumerate(chips):
            for k in range(nb):
                landed = outs[k].at[2 * cx + cy, rows(k, 1 - c)]
                copy(6 * k + 3 + j, landed, landed, (x, y, c)).wait_recv()
            for k in range(nb, n):
                landed = outs[k].at[2 * cx + cy]
                copy(6 * nb + 3 * (k - nb) + j, landed, landed, (x, y, c)).wait_recv()
        for cp in sent:
            cp.wait_send()
        for cp in local:
            cp.wait()

    arrays = list(bigs) + list(smalls)
    n_sems = 6 * nb + 3 * ns
    return pl.pallas_call(
        body, name="all_gather_weights",
        out_shape=[jax.ShapeDtypeStruct((N_CHIPS,) + a.shape, a.dtype) for a in arrays],
        in_specs=_hbm_specs(n), out_specs=_hbm_specs(n),
        scratch_shapes=[pltpu.SemaphoreType.DMA((n_sems,)), pltpu.SemaphoreType.DMA((n_sems,)),
                        pltpu.SemaphoreType.DMA((n,))])(*arrays)


def _pair_exchange(views):
    n = len(views)

    def body(*refs):
        ins, outs, send_sems, recv_sems = refs[:n], refs[n:2 * n], refs[2 * n], refs[2 * n + 1]
        x, y, c = _place()
        copies = [pltpu.make_async_remote_copy(
            src_ref=ins[k].at[:, pl.ds(1 - c, 1)], dst_ref=outs[k], send_sem=send_sems.at[k],
            recv_sem=recv_sems.at[k], device_id=(x, y, 1 - c), device_id_type=MESH) for k in range(n)]
        for cp in copies:
            cp.start()
        for cp in copies:
            cp.wait()

    return pl.pallas_call(
        body, name="pair_exchange",
        out_shape=[jax.ShapeDtypeStruct((v.shape[0], 1) + v.shape[2:], v.dtype) for v in views],
        in_specs=_hbm_specs(n), out_specs=_hbm_specs(n),
        scratch_shapes=[pltpu.SemaphoreType.DMA((n,)), pltpu.SemaphoreType.DMA((n,))])(*views)


def _scatter_to_chips(parts):
    n = len(parts)

    def body(*refs):
        ins, outs, send_sems, recv_sems = refs[:n], refs[n:2 * n], refs[2 * n], refs[2 * n + 1]
        x, y, c = _place()
        copies = [pltpu.make_async_remote_copy(
            src_ref=ins[k].at[pl.ds(2 * cx + cy, 1)], dst_ref=outs[k].at[pl.ds(r, 1)], send_sem=send_sems.at[3 * k + r],
            recv_sem=recv_sems.at[3 * k + r], device_id=(cx, cy, c), device_id_type=MESH)
            for k in range(n) for r, (cx, cy) in enumerate(_other_chips(x, y))]
        for cp in copies:
            cp.start()
        for cp in copies:
            cp.wait()

    return pl.pallas_call(
        body, name="scatter_grads", out_shape=[jax.ShapeDtypeStruct((3,) + p.shape[1:], p.dtype) for p in parts],
        in_specs=_hbm_specs(n), out_specs=_hbm_specs(n),
        scratch_shapes=[pltpu.SemaphoreType.DMA((3 * n,)), pltpu.SemaphoreType.DMA((3 * n,))])(*parts)


def _join_halves(halves):
    n = len(halves)

    def body(*refs):
        ins, outs = refs[:n], refs[n:2 * n]
        send_sems, recv_sems, local_sems = refs[2 * n:]
        x, y, c = _place()

        def rows(k, which):
            h = ins[k].shape[0]
            return outs[k].at[pl.ds(which * h, h)]

        own = [pltpu.make_async_copy(ins[k], rows(k, c), local_sems.at[k]) for k in range(n)]
        sent = [pltpu.make_async_remote_copy(
            src_ref=ins[k], dst_ref=rows(k, c), send_sem=send_sems.at[k], recv_sem=recv_sems.at[k],
            device_id=(x, y, 1 - c), device_id_type=MESH) for k in range(n)]
        for cp in own + sent:
            cp.start()
        for k in range(n):
            pltpu.make_async_remote_copy(
                src_ref=ins[k], dst_ref=rows(k, 1 - c), send_sem=send_sems.at[k], recv_sem=recv_sems.at[k],
                device_id=(x, y, 1 - c), device_id_type=MESH).wait_recv()
        for cp in sent:
            cp.wait_send()
        for cp in own:
            cp.wait()

    return pl.pallas_call(
        body, name="half_exchange",
        out_shape=[jax.ShapeDtypeStruct((2 * a.shape[0],) + a.shape[1:], a.dtype) for a in halves],
        in_specs=_hbm_specs(n), out_specs=_hbm_specs(n),
        scratch_shapes=[pltpu.SemaphoreType.DMA((n,)), pltpu.SemaphoreType.DMA((n,)),
                        pltpu.SemaphoreType.DMA((n,))])(*halves)


def _all_reduce_small(packet):
    rows, width = packet.shape
    n_dev = 8

    def body(x_ref, out_ref, gath, send_sems, recv_sems):
        x, y, c = _place()
        me, sibling = (x, y, c), (x, y, 1 - c)
        chips = _other_chips(x, y)

        def slot(px, py, pc):
            return gath.at[pl.ds((4 * px + 2 * py + pc) * rows, rows), :]

        def copy(k, block, to, src=None):
            return pltpu.make_async_remote_copy(
                src_ref=slot(*block) if src is None else src, dst_ref=slot(*block), send_sem=send_sems.at[k],
                recv_sem=recv_sems.at[k], device_id=to, device_id_type=MESH)

        first = [copy(0, me, sibling, src=x_ref)]
        first += [copy(1 + j, me, (*chip, c), src=x_ref) for j, chip in enumerate(chips)]
        for cp in first:
            cp.start()
        gath[pl.ds((4 * x + 2 * y + c) * rows, rows), :] = x_ref[...]
        passed = [copy(4 + j, (*chip, c), sibling) for j, chip in enumerate(chips)]
        for j, chip in enumerate(chips):
            copy(1 + j, (*chip, c), me).wait_recv()
            passed[j].start()
        copy(0, sibling, me).wait_recv()
        for j, chip in enumerate(chips):
            copy(4 + j, (*chip, 1 - c), me).wait_recv()
        for cp in first + passed:
            cp.wait_send()
        acc = gath[0:rows, :]
        for d in range(1, n_dev):
            acc = acc + gath[d * rows:(d + 1) * rows, :]
        out_ref[...] = acc

    return pl.pallas_call(
        body, name="all_reduce_small", out_shape=jax.ShapeDtypeStruct((rows, width), F32),
        in_specs=[pl.BlockSpec(memory_space=pltpu.VMEM)], out_specs=pl.BlockSpec(memory_space=pltpu.VMEM),
        scratch_shapes=[pltpu.VMEM((n_dev * rows, width), F32), pltpu.SemaphoreType.DMA((7,)),
                        pltpu.SemaphoreType.DMA((7,))])(packet)


def _flat_rows(parts, width, row_multiple):
    flat = jnp.concatenate([p.astype(F32).reshape(-1) for p in parts])
    rows = -(-flat.shape[0] // width)
    rows = -(-rows // row_multiple) * row_multiple
    return jnp.pad(flat, (0, rows * width - flat.shape[0])).reshape(rows, width)


def _unflatten(flat2d, shapes):
    flat = flat2d.reshape(-1)
    out, off = [], 0
    for shp in shapes:
        n = 1
        for dim in shp:
            n *= dim
        out.append(flat[off:off + n].reshape(shp))
        off += n
    return out


def _reduce_scatter_grads(chip_major, names):
    x, y, c = _place()
    sel = jnp.stack([c, 2 * x + y]).astype(jnp.int32)
    views = [g.reshape(N_CHIPS, 2, g.shape[1] // 2, g.shape[2]) for g in chip_major]
    recv = _pair_exchange(views)
    pairs = [_pair_sum(v, r, sel, "pair_sum_" + nm) for v, r, nm in zip(views, recv, names)]
    got = _scatter_to_chips([pb for _, pb in pairs])
    return _join_halves([_chip_sum(p, g, sel, "chip_sum_" + nm) for (p, _), g, nm in zip(pairs, got, names)])


def kernel(x, g_mix, w_in, b_f, w_conv, g_conv_out, g_attn_out, w_o, g_ffn, w_up, w_ffn_conv, w_down, g_final, loss_target, m_g_mix, m_w_in, m_b_f, m_w_conv, m_g_conv_out, m_g_attn_out, m_w_o, m_g_ffn, m_w_up, m_w_ffn_conv, m_w_down, m_g_final, v_g_mix, v_w_in, v_b_f, v_w_conv, v_g_conv_out, v_g_attn_out, v_w_o, v_g_ffn, v_w_up, v_w_ffn_conv, v_w_down, v_g_final):
    s = x.shape[1]
    x0 = x[0]
    target = loss_target[0]
    d = D_MODEL
    x_pos, y_pos, _ = _place()
    my_chip = 2 * x_pos + y_pos

    c_in, c_o, c_up, c_down, c_conv, c_ffn = _all_gather_weights(
        [w_in[0].astype(BF16), w_o[0].astype(BF16), w_up[0].astype(BF16), w_down[0].astype(BF16)],
        [w_conv[0], w_ffn_conv[0]])
    w_in_full = jnp.concatenate([c_in[j] for j in range(N_CHIPS)], axis=1)
    w_o_full = c_o.reshape(d, d)
    w_down_full = c_down.reshape(D_FF, d)
    w_conv_full = jnp.concatenate([c_conv[j] for j in range(N_CHIPS)], axis=1)
    w_ffn_full = jnp.concatenate([c_ffn[j] for j in range(N_CHIPS)], axis=1)
    c3 = 3 * CONV_CH
    w_a, w_b = w_in_full[:, :c3], w_in_full[:, c3:c3 + 3 * ATTN_W]
    w_c = jnp.pad(w_in_full[:, c3 + 3 * ATTN_W:], ((0, 0), (0, LANES - N_HEADS)))
    w_q, w_k, w_v = (w_b[:, i * ATTN_W:(i + 1) * ATTN_W] for i in range(3))
    n_up = c_up.shape[2]
    b_pad = jnp.pad(b_f, ((0, 0), (0, LANES - N_HEADS)))

    h1 = _rms_fwd(x0, g_mix, "rms_mix")
    z_a = _mm("nn", [h1], [w_a], F32, TM_MM, 512, "in_proj_conv")
    qkv = _mm("nn", [h1], [w_b], BF16, TM_MM, 512, "in_proj_qkv")
    f_log = _mm("nn", [h1], [w_c], F32, TM_MM, LANES, "in_proj_gate")
    fb, ft = _gate_fwd(f_log, b_pad)
    first_blk = _first_key_blocks(qkv, fb)
    o_attn, g_att = _attn_fwd(qkv, fb, ft, first_blk)
    mix = _mixer_fwd(z_a, o_attn, w_conv_full, g_conv_out, g_attn_out)
    x2 = _mm("nn", [mix], [w_o_full], F32, TM_MM, 512, "out_proj", add=x0)
    h2 = _rms_fwd(x2, g_ffn, "rms_ffn")
    up = _mm("nn", [h2], [c_up], F32, TM_MM, n_up, "up_proj", b_chips=True)
    act = _ffn_act_fwd(up, w_ffn_full)
    x3 = _mm("nn", [act], [w_down_full], F32, 512, 512, "down_proj", add=x2)

    dx3, dx3_b, loss_row, gg_final = _loss_head(x3, target, g_final.reshape(1, d))
    dact = _mm("nt", [dx3_b], [w_down_full], F32, TM_MM, 1408, "d_act")
    gw_down = _mm_tn(act, dx3_b, 1408, 1024, "gw_down")
    dup, gwf_lin, gwf_gate = _ffn_act_bwd(up, dact, w_ffn_full)
    dh2 = _mm("nt", [(dup, j // 2, j % 2, n_up) for j in range(N_CHIPS)], [(c_up, j) for j in range(N_CHIPS)],
              F32, 512, 512, "d_h2")
    gw_up = _mm_tn(h2, dup, 1024, n_up, "gw_up", out_chips=True)
    dx2, dx2_b, gg_ffn = _rms_bwd(x2, dh2, g_ffn, dx3, "rms_ffn_bwd", True)
    dmix = _mm("nt", [dx2_b], [w_o_full], F32, TM_MM, 512, "d_mix")
    gw_o = _mm_tn(mix, dx2_b, 1024, 1024, "gw_o")
    dz_a, d_o, delta, gw_conv, gg_conv_out, gg_attn_out = _mixer_bwd(z_a, o_attn, dmix, w_conv_full, g_conv_out,
                                                                     g_attn_out)
    dq, dfq, dk, dv, dfk = _attn_bwd(qkv, d_o, g_att, delta, ft, first_blk)
    d_f = jnp.transpose(dfk[:, :, 0:2, :], (1, 3, 0, 2)).reshape(s, N_HEADS) + dfq[:, ::HEAD_DIM]
    df_b, gb_f = _gate_bwd(f_log, b_pad, jnp.pad(d_f, ((0, 0), (0, LANES - N_HEADS))))
    dh1 = _mm("nt", [dz_a, dq, dk, dv, df_b], [w_a, w_q, w_k, w_v, w_c], F32, TM_MM, 512, "d_h1")
    gw_a = _mm_tn(h1, dz_a, 1024, c3, "gw_in_conv")
    gw_q = _mm_tn(h1, dq, 1024, ATTN_W, "gw_in_q")
    gw_k = _mm_tn(h1, dk, 1024, ATTN_W, "gw_in_k")
    gw_v = _mm_tn(h1, dv, 1024, ATTN_W, "gw_in_v")
    gw_c = _mm_tn(h1, df_b, 1024, LANES, "gw_in_gate")
    grad_x, gg_mix = _rms_bwd(x0, dh1, g_mix, dx2, "rms_mix_bwd", False)

    gw_in = jnp.concatenate([gw_a, gw_q, gw_k, gw_v, gw_c[:, :N_HEADS]], axis=1)
    n_in = IN_COLS // N_CHIPS
    gw_in = jnp.stack([gw_in[:, j * n_in:(j + 1) * n_in] for j in range(N_CHIPS)])
    g_w_in, g_w_o, g_w_up, g_w_down = _reduce_scatter_grads(
        [gw_in, gw_o.reshape(N_CHIPS, d // N_CHIPS, d), gw_up, gw_down.reshape(N_CHIPS, D_FF // N_CHIPS, d)],
        ["w_in", "w_o", "w_up", "w_down"])

    gw_ffn = jnp.concatenate([gwf_lin, gwf_gate], axis=1)
    small_parts = [gg_mix, gg_conv_out, gg_attn_out, gg_ffn, gg_final, gb_f[:, :N_HEADS], loss_row[:, 0:1], gw_conv,
                   gw_ffn]
    small_shapes = [a.shape for a in small_parts]
    tot = _unflatten(_all_reduce_small(_flat_rows(small_parts, d, 8)), small_shapes)
    g_g_mix, g_g_conv_out, g_g_attn_out, g_g_ffn, g_g_final, g_b_f, loss_sum, g_conv_full, g_ffn_full = tot
    loss = loss_sum[0, 0]
    g_g_final = g_g_final[0]
    g_w_conv = lax.dynamic_slice_in_dim(g_conv_full, my_chip * (CONV_CH // N_CHIPS), CONV_CH // N_CHIPS, axis=1)
    g_w_ffn = lax.dynamic_slice_in_dim(g_ffn_full, my_chip * n_up, n_up, axis=1)

    def adam_big(w, g, m, v, name):
        dl, nm, nv = _adamw(w[0], g, m[0], v[0], name)
        return dl[None], nm[None], nv[None]

    u_w_in = adam_big(w_in, g_w_in, m_w_in, v_w_in, "adam_w_in")
    u_w_o = adam_big(w_o, g_w_o, m_w_o, v_w_o, "adam_w_o")
    u_w_up = adam_big(w_up, g_w_up, m_w_up, v_w_up, "adam_w_up")
    u_w_down = adam_big(w_down, g_w_down, m_w_down, v_w_down, "adam_w_down")

    small_w = [g_mix, b_f, g_conv_out, g_attn_out, g_ffn, g_final, w_conv, w_ffn_conv]
    small_g = [g_g_mix, g_b_f, g_g_conv_out, g_g_attn_out, g_g_ffn, g_g_final, g_w_conv, g_w_ffn]
    small_m = [m_g_mix, m_b_f, m_g_conv_out, m_g_attn_out, m_g_ffn, m_g_final, m_w_conv, m_w_ffn_conv]
    small_v = [v_g_mix, v_b_f, v_g_conv_out, v_g_attn_out, v_g_ffn, v_g_final, v_w_conv, v_w_ffn_conv]
    shapes = [a.shape for a in small_w]
    pack = lambda arrs: _flat_rows(arrs, LANES, 8)
    sd, sm, sv = _adamw(pack(small_w), pack(small_g), pack(small_m), pack(small_v), "adam_small")
    sd, sm, sv = _unflatten(sd, shapes), _unflatten(sm, shapes), _unflatten(sv, shapes)
    (d_g_mix, d_b_f, d_g_conv_out, d_g_attn_out, d_g_ffn, d_g_final, d_w_conv, d_w_ffn) = sd
    (nm_g_mix, nm_b_f, nm_g_conv_out, nm_g_attn_out, nm_g_ffn, nm_g_final, nm_w_conv, nm_w_ffn) = sm
    (nv_g_mix, nv_b_f, nv_g_conv_out, nv_g_attn_out, nv_g_ffn, nv_g_final, nv_w_conv, nv_w_ffn) = sv

    grads = (g_g_mix, g_w_in[None], g_b_f, g_w_conv[None], g_g_conv_out, g_g_attn_out, g_w_o[None], g_g_ffn,
             g_w_up[None], g_w_ffn[None], g_w_down[None], g_g_final)
    deltas = (d_g_mix, u_w_in[0], d_b_f, d_w_conv, d_g_conv_out, d_g_attn_out, u_w_o[0], d_g_ffn, u_w_up[0],
              d_w_ffn, u_w_down[0], d_g_final)
    new_m = (nm_g_mix, u_w_in[1], nm_b_f, nm_w_conv, nm_g_conv_out, nm_g_attn_out, u_w_o[1], nm_g_ffn, u_w_up[1],
             nm_w_ffn, u_w_down[1], nm_g_final)
    new_v = (nv_g_mix, u_w_in[2], nv_b_f, nv_w_conv, nv_g_conv_out, nv_g_attn_out, u_w_o[2], nv_g_ffn, u_w_up[2],
             nv_w_ffn, u_w_down[2], nv_g_final)
    return (loss, grad_x[None], *grads, *deltas, *new_m, *new_v)
```

```python
import functools

import jax
import jax.numpy as jnp
from jax import lax
from jax.experimental import pallas as pl
from jax.experimental.pallas import tpu as pltpu

F32, BF16 = jnp.float32, jnp.bfloat16
MESH = pl.DeviceIdType.MESH

D_MODEL = 1024
CONV_CH = 512
ATTN_W = 512
N_HEADS = 8
HEAD_DIM = 64
N_PAIRS = N_HEADS // 2
D_FF = 2816
IN_COLS = 3 * CONV_CH + 3 * ATTN_W + N_HEADS
EPS = 1e-6
Q_SCALE = 0.125
EXP_ZERO = 104.0
N_CHIPS = 4
LANES = 128
HALO = 8

ADAM_LR, ADAM_B1, ADAM_B2, ADAM_EPS, ADAM_WD, ADAM_STEP = 0.001, 0.9, 0.999, 1e-08, 0.01, 10

TM_ROWS = 512
TM_MM = 1024
TK_TN = 512
TQ = 512
TM_FFN = 1024
TN_FFN = 256
VMEM_LIMIT = 52 * 2**20


def _cp(sem, vmem=VMEM_LIMIT):
    return pltpu.CompilerParams(dimension_semantics=sem, vmem_limit_bytes=vmem)


def _bf(a):
    return a if a.dtype == BF16 else a.astype(BF16)


def _mm(mode, a_list, b_list, out_dtype, tm, tn, name, add=None, b_chips=False):
    n_p = len(a_list)
    a0 = a_list[0]
    m_dim = a0[0].shape[1] if isinstance(a0, tuple) else a0.shape[0]
    b0 = b_list[0]
    if b_chips:
        n_dim = b0.shape[0] * b0.shape[2]
        assert tn == b0.shape[2] and mode == "nn"
    else:
        b0 = b0[0][b0[1]] if isinstance(b0, tuple) else b0
        n_dim = b0.shape[1 if mode == "nn" else 0]
    tm, tn = min(tm, m_dim), min(tn, n_dim)
    assert m_dim % tm == 0 and n_dim % tn == 0
    dims = (((1,), (0,)), ((), ())) if mode == "nn" else (((1,), (1,)), ((), ()))
    in_specs, args = [], []
    for a in a_list:
        if isinstance(a, tuple):
            arr, lead, col, width = a
            in_specs.append(pl.BlockSpec((None, tm, width), lambda m, n, lead=lead, col=col: (lead, m, col)))
        else:
            arr = a
            in_specs.append(pl.BlockSpec((tm, a.shape[1]), lambda m, n: (m, 0)))
        args.append(arr)
    for b in b_list:
        if b_chips:
            arr = b
            in_specs.append(pl.BlockSpec((None, b.shape[1], tn), lambda m, n: (n, 0, 0)))
        elif isinstance(b, tuple):
            arr, lead = b
            if mode == "nn":
                in_specs.append(pl.BlockSpec((None, arr.shape[1], tn), lambda m, n, lead=lead: (lead, 0, n)))
            else:
                in_specs.append(pl.BlockSpec((None, tn, arr.shape[2]), lambda m, n, lead=lead: (lead, n, 0)))
        elif mode == "nn":
            arr = b
            in_specs.append(pl.BlockSpec((b.shape[0], tn), lambda m, n: (0, n)))
        else:
            arr = b
            in_specs.append(pl.BlockSpec((tn, b.shape[1]), lambda m, n: (n, 0)))
        args.append(arr)
    if add is not None:
        in_specs.append(pl.BlockSpec((tm, tn), lambda m, n: (m, n)))
        args.append(add)

    def body(*refs):
        o_ref = refs[-1]
        acc = None
        for i in range(n_p):
            d = lax.dot_general(_bf(refs[i][...]), _bf(refs[n_p + i][...]), dims,
                                preferred_element_type=F32)
            acc = d if acc is None else acc + d
        if add is not None:
            acc = refs[2 * n_p][...] + acc
        o_ref[...] = acc.astype(out_dtype)

    return pl.pallas_call(
        body, name=name, grid=(m_dim // tm, n_dim // tn), in_specs=in_specs,
        out_specs=pl.BlockSpec((tm, tn), lambda m, n: (m, n)),
        out_shape=jax.ShapeDtypeStruct((m_dim, n_dim), out_dtype),
        compiler_params=_cp(("parallel", "parallel")))(*args)


def _mm_tn(a, b, tm, tn, name, out_chips=False):
    k_dim, m_dim = a.shape
    n_dim = b.shape[-1] * (b.shape[0] if b.ndim == 3 else 1)
    tm, tn, tk = min(tm, m_dim), min(tn, b.shape[-1]), min(TK_TN, k_dim)
    assert m_dim % tm == 0 and b.shape[-1] % tn == 0 and k_dim % tk == 0
    per = b.shape[-1] // tn
    if b.ndim == 3:
        b_spec = pl.BlockSpec((None, tk, tn), lambda m, n, k: (n // per, k, n % per))
    else:
        b_spec = pl.BlockSpec((tk, tn), lambda m, n, k: (k, n))

    def body(a_ref, b_ref, o_ref):
        @pl.when(pl.program_id(2) == 0)
        def _():
            o_ref[...] = jnp.zeros_like(o_ref)
        o_ref[...] += lax.dot_general(_bf(a_ref[...]), _bf(b_ref[...]), (((0,), (0,)), ((), ())),
                                      preferred_element_type=F32)

    return pl.pallas_call(
        body, name=name, grid=(m_dim // tm, n_dim // tn, k_dim // tk),
        in_specs=[pl.BlockSpec((tk, tm), lambda m, n, k: (k, m)), b_spec],
        out_specs=(pl.BlockSpec((None, tm, tn), lambda m, n, k: (n, m, 0)) if out_chips
                   else pl.BlockSpec((tm, tn), lambda m, n, k: (m, n))),
        out_shape=jax.ShapeDtypeStruct((n_dim // tn, m_dim, tn) if out_chips else (m_dim, n_dim), F32),
        compiler_params=_cp(("parallel", "parallel", "arbitrary")))(a, b)


def _rstd(x):
    return lax.rsqrt(jnp.mean(x * x, axis=-1, keepdims=True) + EPS)


def _rms_fwd(x, g, name):
    s, d = x.shape
    tm = min(TM_ROWS, s)

    def body(x_ref, g_ref, h_ref):
        xv = x_ref[...]
        h_ref[...] = (xv * _rstd(xv) * g_ref[...]).astype(BF16)

    return pl.pallas_call(
        body, name=name, grid=(s // tm,),
        in_specs=[pl.BlockSpec((tm, d), lambda i: (i, 0)), pl.BlockSpec((1, d), lambda i: (0, 0))],
        out_specs=pl.BlockSpec((tm, d), lambda i: (i, 0)),
        out_shape=jax.ShapeDtypeStruct((s, d), BF16), compiler_params=_cp(("parallel",)))(x, g)


def _rms_bwd(x, dh, g, dres, name, with_bf16):
    s, d = x.shape
    tm = min(TM_ROWS, s)

    def body(x_ref, dh_ref, g_ref, dres_ref, dx_ref, *rest):
        gg_ref = rest[-1]

        @pl.when(pl.program_id(0) == 0)
        def _():
            gg_ref[...] = jnp.zeros_like(gg_ref)

        xv = x_ref[...]
        xn = xv * _rstd(xv)
        dhv = dh_ref[...]
        gg_ref[...] += jnp.sum(dhv * xn, axis=0, keepdims=True)
        t = dhv * g_ref[...]
        dx = dres_ref[...] + _rstd(xv) * (t - xn * jnp.mean(t * xn, axis=-1, keepdims=True))
        dx_ref[...] = dx
        if with_bf16:
            rest[0][...] = dx.astype(BF16)

    row = pl.BlockSpec((tm, d), lambda i: (i, 0))
    vec = pl.BlockSpec((1, d), lambda i: (0, 0))
    out_specs = [row] + ([row] if with_bf16 else []) + [vec]
    out_shape = ([jax.ShapeDtypeStruct((s, d), F32)] + ([jax.ShapeDtypeStruct((s, d), BF16)] if with_bf16 else [])
                 + [jax.ShapeDtypeStruct((1, d), F32)])
    return pl.pallas_call(
        body, name=name, grid=(s // tm,), in_specs=[row, row, vec, row], out_specs=out_specs, out_shape=out_shape,
        compiler_params=_cp(("arbitrary",)))(x, dh, g, dres)


def _loss_head(x3, target, g):
    s, d = x3.shape
    tm = min(TM_ROWS, s)

    def body(x_ref, t_ref, g_ref, dx_ref, dxb_ref, loss_ref, gg_ref):
        @pl.when(pl.program_id(0) == 0)
        def _():
            gg_ref[...] = jnp.zeros_like(gg_ref)
            loss_ref[...] = jnp.zeros_like(loss_ref)

        xv = x_ref[...]
        r = _rstd(xv)
        xn = xv * r
        gv = g_ref[...]
        err = xn * gv - t_ref[...]
        loss_ref[...] += 0.5 * jnp.sum(jnp.mean(err * err, axis=-1, keepdims=True), axis=0, keepdims=True)
        dy = err * (1.0 / d)
        gg_ref[...] += jnp.sum(dy * xn, axis=0, keepdims=True)
        t = dy * gv
        dx = r * (t - xn * jnp.mean(t * xn, axis=-1, keepdims=True))
        dx_ref[...] = dx
        dxb_ref[...] = dx.astype(BF16)

    row = pl.BlockSpec((tm, d), lambda i: (i, 0))
    vec = pl.BlockSpec((1, d), lambda i: (0, 0))
    return pl.pallas_call(
        body, name="loss_head", grid=(s // tm,), in_specs=[row, row, vec],
        out_specs=[row, row, pl.BlockSpec((1, LANES), lambda i: (0, 0)), vec],
        out_shape=[jax.ShapeDtypeStruct((s, d), F32), jax.ShapeDtypeStruct((s, d), BF16),
                   jax.ShapeDtypeStruct((1, LANES), F32), jax.ShapeDtypeStruct((1, d), F32)],
        compiler_params=_cp(("arbitrary",)))(x3, target, g)


def _prev_halo_spec(tm, width, col):
    return pl.BlockSpec((HALO, width), lambda i, *_: (jnp.maximum(i * (tm // HALO) - 1, 0), col))


def _next_halo_spec(tm, width, col, s):
    return pl.BlockSpec((HALO, width), lambda i, *_: (jnp.minimum((i + 1) * (tm // HALO), s // HALO - 1), col))


def _shift_down(x, k):
    return pltpu.roll(x, k, 0)


def _shift_up(x, k):
    return pltpu.roll(x, x.shape[0] - k, 0)


def _conv_taps(x_ext, w):
    return w[0:1, :] * _shift_down(x_ext, 2) + w[1:2, :] * _shift_down(x_ext, 1) + w[2:3, :] * x_ext


def _conv_taps_t(d_ext, w):
    return w[2:3, :] * d_ext + w[1:2, :] * _shift_up(d_ext, 1) + w[0:1, :] * _shift_up(d_ext, 2)


def _mixer_fwd(z_a, o_attn, w_conv, g_conv_out, g_attn_out):
    s = z_a.shape[0]
    c = CONV_CH
    tm = min(TM_ROWS, s)

    def body(gb_ref, gc_ref, xc_ref, gcp_ref, xcp_ref, o_ref, w_ref, gco_ref, gao_ref, mix_ref):
        i = pl.program_id(0)
        cx = gc_ref[...] * xc_ref[...]
        cx_prev = jnp.where(i > 0, gcp_ref[...] * xcp_ref[...], 0.0)
        conv = _conv_taps(jnp.concatenate([cx_prev, cx], axis=0), w_ref[...])[HALO:]
        y = gb_ref[...] * conv
        mix_ref[:, 0:c] = (y * _rstd(y) * gco_ref[...]).astype(BF16)
        o = o_ref[...]
        mix_ref[:, c:2 * c] = (o * _rstd(o) * gao_ref[...]).astype(BF16)

    col = lambda j: pl.BlockSpec((tm, c), lambda i: (i, j))
    vec = pl.BlockSpec((1, c), lambda i: (0, 0))
    return pl.pallas_call(
        body, name="mixer_fwd", grid=(s // tm,),
        in_specs=[col(0), col(1), col(2), _prev_halo_spec(tm, c, 1), _prev_halo_spec(tm, c, 2), col(0),
                  pl.BlockSpec((3, c), lambda i: (0, 0)), vec, vec],
        out_specs=pl.BlockSpec((tm, 2 * c), lambda i: (i, 0)),
        out_shape=jax.ShapeDtypeStruct((s, 2 * c), BF16),
        compiler_params=_cp(("parallel",)))(z_a, z_a, z_a, z_a, z_a, o_attn, w_conv, g_conv_out, g_attn_out)


def _mixer_bwd(z_a, o_attn, dmix, w_conv, g_conv_out, g_attn_out):
    s = z_a.shape[0]
    c = CONV_CH
    tm = min(TM_ROWS, s)
    n_blk = s // tm

    def body(gb_ref, gc_ref, xc_ref, gcp_ref, xcp_ref, gbn_ref, gcn_ref, xcn_ref, o_ref, dnc_ref, dncn_ref, dna_ref,
             w_ref, gco_ref, gao_ref, dz_ref, do_ref, delta_ref, gw_ref, ggco_ref, ggao_ref):
        i = pl.program_id(0)

        @pl.when(i == 0)
        def _():
            gw_ref[...] = jnp.zeros_like(gw_ref)
            ggco_ref[...] = jnp.zeros_like(ggco_ref)
            ggao_ref[...] = jnp.zeros_like(ggao_ref)

        w = w_ref[...]
        zeros = jnp.zeros((HALO, c), F32)
        gb_e = jnp.concatenate([zeros, gb_ref[...], gbn_ref[...]], axis=0)
        cx_prev = jnp.where(i > 0, gcp_ref[...] * xcp_ref[...], 0.0)
        gc_e = jnp.concatenate([zeros, gc_ref[...], gcn_ref[...]], axis=0)
        xc_e = jnp.concatenate([zeros, xc_ref[...], xcn_ref[...]], axis=0)
        cx_e = jnp.concatenate([cx_prev, gc_ref[...] * xc_ref[...], gcn_ref[...] * xcn_ref[...]], axis=0)
        dn_next = jnp.where(i < n_blk - 1, dncn_ref[...], 0.0)
        dn_e = jnp.concatenate([zeros, dnc_ref[...], dn_next], axis=0)

        conv_e = _conv_taps(cx_e, w)
        y_e = gb_e * conv_e
        r_e = _rstd(y_e)
        yn_e = y_e * r_e
        t_e = dn_e * gco_ref[...]
        dy_e = r_e * (t_e - yn_e * jnp.mean(t_e * yn_e, axis=-1, keepdims=True))
        dconv_e = dy_e * gb_e
        dcx_e = _conv_taps_t(dconv_e, w)
        blk = slice(HALO, HALO + tm)
        dz_ref[:, 0:c] = (dy_e * conv_e)[blk].astype(BF16)
        dz_ref[:, c:2 * c] = (dcx_e * xc_e)[blk].astype(BF16)
        dz_ref[:, 2 * c:3 * c] = (dcx_e * gc_e)[blk].astype(BF16)
        ggco_ref[...] += jnp.sum((dn_e * yn_e)[blk], axis=0, keepdims=True)
        dconv = dconv_e[blk]
        gw_ref[0:1, :] += jnp.sum(dconv * _shift_down(cx_e, 2)[blk], axis=0, keepdims=True)
        gw_ref[1:2, :] += jnp.sum(dconv * _shift_down(cx_e, 1)[blk], axis=0, keepdims=True)
        gw_ref[2:3, :] += jnp.sum(dconv * cx_e[blk], axis=0, keepdims=True)

        o = o_ref[...]
        ra = _rstd(o)
        on = o * ra
        dna = dna_ref[...]
        ggao_ref[...] += jnp.sum(dna * on, axis=0, keepdims=True)
        ta = dna * gao_ref[...]
        do = ra * (ta - on * jnp.mean(ta * on, axis=-1, keepdims=True))
        do_ref[...] = do.astype(BF16)
        prod = do * o
        head_a = lax.broadcasted_iota(jnp.int32, (tm, LANES), 1) < HEAD_DIM
        for p in range(N_PAIRS):
            pb = prod[:, p * LANES:(p + 1) * LANES]
            sa = jnp.sum(jnp.where(head_a, pb, 0.0), axis=-1, keepdims=True)
            sb = jnp.sum(jnp.where(head_a, 0.0, pb), axis=-1, keepdims=True)
            delta_ref[:, p * LANES:(p + 1) * LANES] = jnp.where(head_a, sa, sb)

    col = lambda j: pl.BlockSpec((tm, c), lambda i: (i, j))
    vec = pl.BlockSpec((1, c), lambda i: (0, 0))
    w3 = pl.BlockSpec((3, c), lambda i: (0, 0))
    return pl.pallas_call(
        body, name="mixer_bwd", grid=(n_blk,),
        in_specs=[col(0), col(1), col(2), _prev_halo_spec(tm, c, 1), _prev_halo_spec(tm, c, 2),
                  _next_halo_spec(tm, c, 0, s), _next_halo_spec(tm, c, 1, s), _next_halo_spec(tm, c, 2, s),
                  col(0), col(0), _next_halo_spec(tm, c, 0, s), col(1), w3, vec, vec],
        out_specs=[pl.BlockSpec((tm, 3 * c), lambda i: (i, 0)), col(0), col(0), w3, vec, vec],
        out_shape=[jax.ShapeDtypeStruct((s, 3 * c), BF16), jax.ShapeDtypeStruct((s, c), BF16),
                   jax.ShapeDtypeStruct((s, c), F32), jax.ShapeDtypeStruct((3, c), F32),
                   jax.ShapeDtypeStruct((1, c), F32), jax.ShapeDtypeStruct((1, c), F32)],
        compiler_params=_cp(("arbitrary",)))(
            z_a, z_a, z_a, z_a, z_a, z_a, z_a, z_a, o_attn, dmix, dmix, dmix, w_conv, g_conv_out, g_attn_out)


def _gate_fwd(f, b_pad):
    s = f.shape[0]
    tm = min(TQ, s)

    def body(f_ref, b_ref, fb_ref, ft_ref, carry):
        @pl.when(pl.program_id(0) == 0)
        def _():
            carry[...] = jnp.zeros_like(carry)

        z = f_ref[...] + b_ref[...]
        x = jnp.minimum(z, 0.0) - jnp.log1p(jnp.exp(-jnp.abs(z)))
        row = lax.broadcasted_iota(jnp.int32, (tm, LANES), 0)
        sh = 1
        while sh < tm:
            x = x + jnp.where(row >= sh, _shift_down(x, sh), 0.0)
            sh *= 2
        x = x + carry[0:1, :]
        carry[...] = jnp.broadcast_to(x[tm - 1:tm, :], carry.shape)
        head_a = lax.broadcasted_iota(jnp.int32, (tm, LANES), 1) < HEAD_DIM
        for p in range(N_PAIRS):
            fa = jnp.broadcast_to(x[:, 2 * p:2 * p + 1], (tm, LANES))
            fbv = jnp.broadcast_to(x[:, 2 * p + 1:2 * p + 2], (tm, LANES))
            fb_ref[:, p * LANES:(p + 1) * LANES] = jnp.where(head_a, fa, fbv)
        ft_ref[0] = x.T[0:N_HEADS, :]

    return pl.pallas_call(
        body, name="gate_fwd", grid=(s // tm,),
        in_specs=[pl.BlockSpec((tm, LANES), lambda i: (i, 0)), pl.BlockSpec((1, LANES), lambda i: (0, 0))],
        out_specs=[pl.BlockSpec((tm, N_PAIRS * LANES), lambda i: (i, 0)),
                   pl.BlockSpec((1, N_HEADS, tm), lambda i: (i, 0, 0))],
        out_shape=[jax.ShapeDtypeStruct((s, N_PAIRS * LANES), F32),
                   jax.ShapeDtypeStruct((s // tm, N_HEADS, tm), F32)],
        scratch_shapes=[pltpu.VMEM((HALO, LANES), F32)],
        compiler_params=_cp(("arbitrary",)))(f, b_pad)


def _gate_bwd(f, b_pad, d_f):
    s = f.shape[0]
    tm = min(TQ, s)
    n_blk = s // tm

    def body(f_ref, b_ref, d_ref, df_ref, gb_ref, carry):
        @pl.when(pl.program_id(0) == 0)
        def _():
            carry[...] = jnp.zeros_like(carry)
            gb_ref[...] = jnp.zeros_like(gb_ref)

        x = d_ref[...]
        row = lax.broadcasted_iota(jnp.int32, (tm, LANES), 0)
        sh = 1
        while sh < tm:
            x = x + jnp.where(row < tm - sh, _shift_up(x, sh), 0.0)
            sh *= 2
        x = x + carry[0:1, :]
        carry[...] = jnp.broadcast_to(x[0:1, :], carry.shape)
        z = f_ref[...] + b_ref[...]
        d = x * (1.0 / (1.0 + jnp.exp(z)))
        df_ref[...] = d.astype(BF16)
        gb_ref[...] += jnp.sum(d, axis=0, keepdims=True)

    rev = pl.BlockSpec((tm, LANES), lambda i: (n_blk - 1 - i, 0))
    vec = pl.BlockSpec((1, LANES), lambda i: (0, 0))
    return pl.pallas_call(
        body, name="gate_bwd", grid=(n_blk,), in_specs=[rev, vec, rev], out_specs=[rev, vec],
        out_shape=[jax.ShapeDtypeStruct((s, LANES), BF16), jax.ShapeDtypeStruct((1, LANES), F32)],
        scratch_shapes=[pltpu.VMEM((HALO, LANES), F32)],
        compiler_params=_cp(("arbitrary",)))(f, b_pad, d_f)


_NT = (((1,), (1,)), ((), ()))
_NN = (((1,), (0,)), ((), ()))
_TN = (((0,), (0,)), ((), ()))


def _head_masks(shape):
    lane = lax.broadcasted_iota(jnp.int32, shape, len(shape) - 1)
    return lane < HEAD_DIM


def _pick_row(ft, h):
    rows = lax.broadcasted_iota(jnp.int32, ft.shape, 0)
    return jnp.sum(jnp.where(rows == h, ft, 0.0), axis=0, keepdims=True)


def _prune_bounds(qkv, fb):
    s = qkv.shape[0]
    tq = min(TQ, s)

    def body(q_ref, k_ref, fb_ref, out_ref):
        head_a = _head_masks((tq, LANES))
        lane = lax.broadcasted_iota(jnp.int32, (HALO, LANES), 1)
        acc = jnp.zeros((HALO, LANES), F32)
        for p in range(N_PAIRS):
            cols = slice(p * LANES, (p + 1) * LANES)
            q2 = q_ref[:, cols].astype(F32) * Q_SCALE
            k2 = k_ref[:, cols].astype(F32)
            f2 = fb_ref[:, cols]
            for hh in range(2):
                sel = head_a if hh == 0 else jnp.logical_not(head_a)
                qn = jnp.sqrt(jnp.sum(jnp.where(sel, q2 * q2, 0.0), axis=-1, keepdims=True))
                kn = jnp.sqrt(jnp.sum(jnp.where(sel, k2 * k2, 0.0), axis=-1, keepdims=True))
                f = f2[:, hh * HEAD_DIM:hh * HEAD_DIM + 1]
                h = 2 * p + hh
                vals = (jnp.max(qn, axis=0, keepdims=True), jnp.max(kn, axis=0, keepdims=True),
                        jnp.max(qn * kn + f, axis=0, keepdims=True), f[tq - 1:tq, :])
                for slot, v in enumerate(vals):
                    acc = jnp.where(lane == slot * N_HEADS + h, v, acc)
        out_ref[0] = acc

    blk = lambda j: pl.BlockSpec((tq, ATTN_W), lambda i: (i, j))
    return pl.pallas_call(
        body, name="prune_bounds", grid=(s // tq,), in_specs=[blk(0), blk(1), blk(0)],
        out_specs=pl.BlockSpec((1, HALO, LANES), lambda i: (i, 0, 0)),
        out_shape=jax.ShapeDtypeStruct((s // tq, HALO, LANES), F32),
        compiler_params=_cp(("parallel",)))(qkv, qkv, fb)


def _first_key_blocks(qkv, fb):
    t = _prune_bounds(qkv, fb)[:, 0, :]
    nh = N_HEADS
    a, b, c, e = t[:, 0:nh], t[:, nh:2 * nh], t[:, 2 * nh:3 * nh], t[:, 3 * nh:4 * nh]
    bound = a[:, None, :] * b[None, :, :] * 1.001 + c[:, None, :] - e[None, :, :]
    n_q = t.shape[0]
    idx = jnp.arange(n_q)
    need = jnp.logical_not(bound < -(EXP_ZERO + 2.0)) | (idx[None, :, None] >= idx[:, None, None])
    first = jnp.argmax(need, axis=1).astype(jnp.int32)
    return jnp.min(first.reshape(n_q, N_PAIRS, 2), axis=-1).T.reshape(-1)


def _attn_fwd(qkv, fb, ft, first_blk):
    s = qkv.shape[0]
    tq = min(TQ, s)
    n_q = s // tq
    neg = -1e30

    def body(first_ref, q_ref, k_ref, v_ref, fb_ref, ft_ref, o_ref, g_ref):
        p = pl.program_id(0)
        i = pl.program_id(1)
        head_a = _head_masks((tq, LANES))
        q2 = q_ref[...] * Q_SCALE
        zero = jnp.zeros_like(q2)
        q_a, q_b = jnp.where(head_a, q2, zero), jnp.where(head_a, zero, q2)
        fq2 = fb_ref[...]
        fq_a, fq_b = fq2[:, 0:1], fq2[:, HEAD_DIM:HEAD_DIM + 1]
        causal = lax.broadcasted_iota(jnp.int32, (tq, tq), 1) <= lax.broadcasted_iota(jnp.int32, (tq, tq), 0)

        def one_head(q_h, fq_h, fk_h, k2, m, l, masked):
            sc = (lax.dot_general(q_h, k2, _NT, preferred_element_type=F32) + fq_h) - fk_h
            if masked:
                sc = jnp.where(causal, sc, -jnp.inf)
            m_new = jnp.maximum(m, jnp.max(sc, axis=-1, keepdims=True))
            pr = jnp.exp(sc - m_new)
            alpha = jnp.exp(m - m_new)
            return pr.astype(BF16), m_new, alpha * l + jnp.sum(pr, axis=-1, keepdims=True), alpha

        def step(kb, carry, masked):
            m_a, l_a, m_b, l_b, acc = carry
            rows = pl.ds(pl.multiple_of(kb * tq, tq), tq)
            k2, v2 = k_ref[rows, :], v_ref[rows, :]
            ftv = ft_ref[kb]
            p_a, m_a, l_a, al_a = one_head(q_a, fq_a, _pick_row(ftv, 2 * p), k2, m_a, l_a, masked)
            p_b, m_b, l_b, al_b = one_head(q_b, fq_b, _pick_row(ftv, 2 * p + 1), k2, m_b, l_b, masked)
            zv = jnp.zeros_like(v2)
            head_k = _head_masks(v2.shape)
            pv = (lax.dot_general(p_a, jnp.where(head_k, v2, zv), _NN, preferred_element_type=F32)
                  + lax.dot_general(p_b, jnp.where(head_k, zv, v2), _NN, preferred_element_type=F32))
            acc = acc * jnp.where(head_a, al_a, al_b) + pv
            return m_a, l_a, m_b, l_b, acc

        col = lambda v: jnp.full((tq, 1), v, F32)
        carry = (col(neg), col(0.0), col(neg), col(0.0), jnp.zeros((tq, LANES), F32))
        carry = lax.fori_loop(first_ref[p * n_q + i], i, lambda kb, cr: step(kb, cr, False), carry)
        m_a, l_a, m_b, l_b, acc = step(i, carry, True)
        o_ref[...] = acc / jnp.where(head_a, l_a, l_b)
        g_ref[...] = fq2 - jnp.where(head_a, m_a + jnp.log(l_a), m_b + jnp.log(l_b))

    qblk = lambda off: pl.BlockSpec((tq, LANES), lambda p, i, first: (i, off + p))
    full = lambda off: pl.BlockSpec((s, LANES), lambda p, i, first: (0, off + p))
    grid_spec = pltpu.PrefetchScalarGridSpec(
        num_scalar_prefetch=1, grid=(N_PAIRS, n_q),
        in_specs=[qblk(0), full(N_PAIRS), full(2 * N_PAIRS), qblk(0),
                  pl.BlockSpec((n_q, N_HEADS, tq), lambda p, i, first: (0, 0, 0))],
        out_specs=[qblk(0), qblk(0)])
    return pl.pallas_call(
        body, name="attn_fwd", grid_spec=grid_spec,
        out_shape=[jax.ShapeDtypeStruct((s, ATTN_W), F32), jax.ShapeDtypeStruct((s, ATTN_W), F32)],
        compiler_params=_cp(("parallel", "arbitrary")))(first_blk, qkv, qkv, qkv, fb, ft)


def _attn_bwd(qkv, do, g, delta, ft, first_blk):
    s = qkv.shape[0]
    tq = min(TQ, s)
    n_q = s // tq

    def body(first_ref, q_ref, do_ref, g_ref, dl_ref, k_ref, v_ref, ft_ref, dq_ref, dfq_ref, dk_ref, dv_ref,
             dfk_ref):
        p = pl.program_id(0)
        i = pl.program_id(1)

        @pl.when(i == 0)
        def _():
            dk_ref[...] = jnp.zeros_like(dk_ref)
            dv_ref[...] = jnp.zeros_like(dv_ref)
            dfk_ref[...] = jnp.zeros_like(dfk_ref)

        head_a = _head_masks((tq, LANES))
        q2 = q_ref[...] * Q_SCALE
        do2 = do_ref[...]
        zero = jnp.zeros_like(q2)
        q_a, q_b = jnp.where(head_a, q2, zero), jnp.where(head_a, zero, q2)
        do_a, do_b = jnp.where(head_a, do2, zero), jnp.where(head_a, zero, do2)
        g2, dl2 = g_ref[...], dl_ref[...]
        g_a, g_b = g2[:, 0:1], g2[:, HEAD_DIM:HEAD_DIM + 1]
        dl_a, dl_b = dl2[:, 0:1], dl2[:, HEAD_DIM:HEAD_DIM + 1]
        causal = lax.broadcasted_iota(jnp.int32, (tq, tq), 1) <= lax.broadcasted_iota(jnp.int32, (tq, tq), 0)
        rows8 = lax.broadcasted_iota(jnp.int32, (N_HEADS, tq), 0)

        def one_head(q_h, do_h, g_h, dl_h, fk_h, k2, v2, masked):
            sc = (lax.dot_general(q_h, k2, _NT, preferred_element_type=F32) + g_h) - fk_h
            pr = jnp.exp(sc)
            if masked:
                pr = jnp.where(causal, pr, 0.0)
            dp = lax.dot_general(do_h, v2, _NT, preferred_element_type=F32)
            ds = pr * (dp - dl_h)
            return (pr.astype(BF16), ds.astype(BF16), jnp.sum(ds, axis=0, keepdims=True),
                    jnp.sum(ds, axis=1, keepdims=True))

        def step(j, carry, masked):
            dq, r_a, r_b = carry
            rows = pl.ds(pl.multiple_of(j * tq, tq), tq)
            k2, v2 = k_ref[rows, :], v_ref[rows, :]
            k_a, k_b = jnp.where(head_a, k2, zero), jnp.where(head_a, zero, k2)
            ftv = ft_ref[j]
            p_a, ds_a, c_a, s_a = one_head(q_a, do_a, g_a, dl_a, _pick_row(ftv, 2 * p), k2, v2, masked)
            p_b, ds_b, c_b, s_b = one_head(q_b, do_b, g_b, dl_b, _pick_row(ftv, 2 * p + 1), k2, v2, masked)
            dv_ref[rows, :] += (lax.dot_general(p_a, do_a, _TN, preferred_element_type=F32)
                                + lax.dot_general(p_b, do_b, _TN, preferred_element_type=F32))
            dk_ref[rows, :] += (lax.dot_general(ds_a, q_a, _TN, preferred_element_type=F32)
                                + lax.dot_general(ds_b, q_b, _TN, preferred_element_type=F32))
            dfk_ref[0, j] += jnp.where(rows8 == 0, -c_a, jnp.where(rows8 == 1, -c_b, 0.0))
            dq = dq + (lax.dot_general(ds_a, k_a, _NN, preferred_element_type=F32)
                       + lax.dot_general(ds_b, k_b, _NN, preferred_element_type=F32))
            return dq, r_a + s_a, r_b + s_b

        zcol = jnp.zeros((tq, 1), F32)
        carry = lax.fori_loop(first_ref[p * n_q + i], i, lambda j, cr: step(j, cr, False),
                              (jnp.zeros((tq, LANES), F32), zcol, zcol))
        dq, r_a, r_b = step(i, carry, True)
        dq_ref[...] = (dq * Q_SCALE).astype(BF16)
        dfq_ref[...] = jnp.where(head_a, r_a, r_b)

    qblk = lambda off: pl.BlockSpec((tq, LANES), lambda p, i, first: (i, off + p))
    full = lambda off: pl.BlockSpec((s, LANES), lambda p, i, first: (0, off + p))
    grid_spec = pltpu.PrefetchScalarGridSpec(
        num_scalar_prefetch=1, grid=(N_PAIRS, n_q),
        in_specs=[qblk(0), qblk(0), qblk(0), qblk(0), full(N_PAIRS), full(2 * N_PAIRS),
                  pl.BlockSpec((n_q, N_HEADS, tq), lambda p, i, first: (0, 0, 0))],
        out_specs=[qblk(0), qblk(0), full(0), full(0),
                   pl.BlockSpec((1, n_q, N_HEADS, tq), lambda p, i, first: (p, 0, 0, 0))])
    return pl.pallas_call(
        body, name="attn_bwd", grid_spec=grid_spec,
        out_shape=[jax.ShapeDtypeStruct((s, ATTN_W), BF16), jax.ShapeDtypeStruct((s, ATTN_W), F32),
                   jax.ShapeDtypeStruct((s, ATTN_W), F32), jax.ShapeDtypeStruct((s, ATTN_W), F32),
                   jax.ShapeDtypeStruct((N_PAIRS, n_q, N_HEADS, tq), F32)],
        compiler_params=_cp(("parallel", "arbitrary")))(first_blk, qkv, do, g, delta, qkv, qkv, ft)


def _ffn_act_fwd(up, w_ffn):
    s = up.shape[0]
    tm, tn = min(TM_FFN, s), TN_FFN
    nb = D_FF // tn

    def body(a_ref, g_ref, ap_ref, gp_ref, wa_ref, wg_ref, act_ref):
        i = pl.program_id(1)

        def conv(blk_ref, prev_ref, w_ref):
            prev = jnp.where(i > 0, prev_ref[...], 0.0)
            return _conv_taps(jnp.concatenate([prev, blk_ref[...]], axis=0), w_ref[...])[HALO:]

        u_a, u_g = conv(a_ref, ap_ref, wa_ref), conv(g_ref, gp_ref, wg_ref)
        act_ref[...] = (u_g * (1.0 / (1.0 + jnp.exp(-u_g))) * u_a).astype(BF16)

    blk = lambda off: pl.BlockSpec((tm, tn), lambda n, i: (i, off + n))
    prev = lambda off: pl.BlockSpec((HALO, tn), lambda n, i: (jnp.maximum(i * (tm // HALO) - 1, 0), off + n))
    wsp = lambda off: pl.BlockSpec((3, tn), lambda n, i: (0, off + n))
    return pl.pallas_call(
        body, name="ffn_act_fwd", grid=(nb, s // tm),
        in_specs=[blk(0), blk(nb), prev(0), prev(nb), wsp(0), wsp(nb)],
        out_specs=pl.BlockSpec((tm, tn), lambda n, i: (i, n)),
        out_shape=jax.ShapeDtypeStruct((s, D_FF), BF16),
        compiler_params=_cp(("parallel", "parallel")))(up, up, up, up, w_ffn, w_ffn)


def _ffn_act_bwd(up, dact, w_ffn):
    s = up.shape[0]
    tm, tn = min(TM_FFN, s), TN_FFN
    nb = D_FF // tn
    n_blk = s // tm

    def body(a_ref, g_ref, ap_ref, gp_ref, an_ref, gn_ref, d_ref, dn_ref, wa_ref, wg_ref,
             dup_ref, gwa_ref, gwg_ref):
        i = pl.program_id(1)

        @pl.when(i == 0)
        def _():
            gwa_ref[...] = jnp.zeros_like(gwa_ref)
            gwg_ref[...] = jnp.zeros_like(gwg_ref)

        def ext(blk_ref, prev_ref, next_ref):
            return jnp.concatenate([jnp.where(i > 0, prev_ref[...], 0.0), blk_ref[...], next_ref[...]], axis=0)

        wa, wg = wa_ref[...], wg_ref[...]
        up_a, up_g = ext(a_ref, ap_ref, an_ref), ext(g_ref, gp_ref, gn_ref)
        u_a, u_g = _conv_taps(up_a, wa), _conv_taps(up_g, wg)
        d_e = jnp.concatenate([jnp.zeros((HALO, tn), F32), d_ref[...],
                               jnp.where(i < n_blk - 1, dn_ref[...], 0.0)], axis=0)
        sig = 1.0 / (1.0 + jnp.exp(-u_g))
        du_a = d_e * (u_g * sig)
        du_g = d_e * u_a * (sig * (1.0 + u_g * (1.0 - sig)))
        blk = slice(HALO, HALO + tm)
        dup_ref[0] = _conv_taps_t(du_a, wa)[blk].astype(BF16)
        dup_ref[1] = _conv_taps_t(du_g, wg)[blk].astype(BF16)
        for gw_ref, upv, du in ((gwa_ref, up_a[blk], du_a), (gwg_ref, up_g[blk], du_g)):
            gw_ref[0:1, :] += jnp.sum(upv * _shift_up(du, 2)[blk], axis=0, keepdims=True)
            gw_ref[1:2, :] += jnp.sum(upv * _shift_up(du, 1)[blk], axis=0, keepdims=True)
            gw_ref[2:3, :] += jnp.sum(upv * du[blk], axis=0, keepdims=True)

    blk = lambda off: pl.BlockSpec((tm, tn), lambda n, i: (i, off + n))
    prev = lambda off: pl.BlockSpec((HALO, tn), lambda n, i: (jnp.maximum(i * (tm // HALO) - 1, 0), off + n))
    nxt = lambda off: pl.BlockSpec(
        (HALO, tn), lambda n, i: (jnp.minimum((i + 1) * (tm // HALO), s // HALO - 1), off + n))
    wsp = lambda off: pl.BlockSpec((3, tn), lambda n, i: (0, off + n))
    return pl.pallas_call(
        body, name="ffn_act_bwd", grid=(nb, n_blk),
        in_specs=[blk(0), blk(nb), prev(0), prev(nb), nxt(0), nxt(nb), blk(0), nxt(0), wsp(0), wsp(nb)],
        out_specs=[pl.BlockSpec((2, tm, tn), lambda n, i: (0, i, n)), wsp(0), wsp(0)],
        out_shape=[jax.ShapeDtypeStruct((2, s, D_FF), BF16),
                   jax.ShapeDtypeStruct((3, D_FF), F32), jax.ShapeDtypeStruct((3, D_FF), F32)],
        compiler_params=_cp(("parallel", "arbitrary")))(up, up, up, up, up, up, dact, dact, w_ffn, w_ffn)


def _adamw(w, g, m, v, name):
    r, c = w.shape
    tr = next((t for t in (512, 352, 256, 128, 64, 32, 16, 8) if r > t and r % t == 0), r)

    def body(w_ref, g_ref, m_ref, v_ref, d_ref, nm_ref, nv_ref):
        gv = g_ref[...]
        m_new = ADAM_B1 * m_ref[...] + (1.0 - ADAM_B1) * gv
        v_new = ADAM_B2 * v_ref[...] + (1.0 - ADAM_B2) * (gv * gv)
        m_hat = m_new / (1.0 - ADAM_B1 ** ADAM_STEP)
        v_hat = v_new / (1.0 - ADAM_B2 ** ADAM_STEP)
        d_ref[...] = -ADAM_LR * (m_hat / (jnp.sqrt(v_hat) + ADAM_EPS) + ADAM_WD * w_ref[...])
        nm_ref[...] = m_new
        nv_ref[...] = v_new

    spec = pl.BlockSpec((tr, c), lambda i: (i, 0))
    shp = jax.ShapeDtypeStruct((r, c), F32)
    return pl.pallas_call(
        body, name=name, grid=(r // tr,), in_specs=[spec] * 4, out_specs=[spec] * 3, out_shape=[shp] * 3,
        compiler_params=_cp(("parallel",)))(w, g, m, v)


def _sum_rows_block(h):
    return h if h <= 352 else 256


def _pair_sum(view, recv, sel, name):
    n, _, h, c = view.shape
    tr = _sum_rows_block(h)

    def body(sel_ref, a_ref, b_ref, o_ref, ob_ref):
        t = a_ref[...] + b_ref[...]
        o_ref[...] = t
        ob_ref[...] = t.astype(BF16)

    blk = pl.BlockSpec((None, tr, c), lambda j, i, sel_ref: (j, i, 0))
    grid_spec = pltpu.PrefetchScalarGridSpec(
        num_scalar_prefetch=1, grid=(n, h // tr),
        in_specs=[pl.BlockSpec((None, None, tr, c), lambda j, i, sel_ref: (j, sel_ref[0], i, 0)),
                  pl.BlockSpec((None, None, tr, c), lambda j, i, sel_ref: (j, 0, i, 0))],
        out_specs=[blk, blk])
    return pl.pallas_call(
        body, name=name, grid_spec=grid_spec,
        out_shape=[jax.ShapeDtypeStruct((n, h, c), F32), jax.ShapeDtypeStruct((n, h, c), BF16)],
        compiler_params=_cp(("parallel", "parallel")))(sel, view, recv)


def _chip_sum(pair, got, sel, name):
    _, h, c = pair.shape
    tr = _sum_rows_block(h)
    nblk = h // tr

    def body(sel_ref, p_ref, g0_ref, g1_ref, g2_ref, o_ref):
        o_ref[...] = ((p_ref[...] + g0_ref[...].astype(F32)) + g1_ref[...].astype(F32)) + g2_ref[...].astype(F32)

    slot = lambda k: pl.BlockSpec((None, tr, c), lambda i, sel_ref: (k, i, 0))
    grid_spec = pltpu.PrefetchScalarGridSpec(
        num_scalar_prefetch=1, grid=(h // tr,),
        in_specs=[pl.BlockSpec((None, tr, c), lambda i, sel_ref: (sel_ref[1], i, 0)), slot(0), slot(1), slot(2)],
        out_specs=pl.BlockSpec((tr, c), lambda i, sel_ref: (sel_ref[0] * nblk + i, 0)))
    return pl.pallas_call(
        body, name=name, grid_spec=grid_spec, out_shape=jax.ShapeDtypeStruct((2 * h, c), F32),
        compiler_params=_cp(("parallel",)))(sel, pair, got, got, got)


def _place():
    return lax.axis_index("x"), lax.axis_index("y"), lax.axis_index("c")


def _other_chips(x, y):
    return [(1 - x, y), (x, 1 - y), (1 - x, 1 - y)]


def _hbm_specs(n):
    return [pl.BlockSpec(memory_space=pl.ANY)] * n


def _all_gather_weights(bigs, smalls):
    nb, ns = len(bigs), len(smalls)
    n = nb + ns

    def body(*refs):
        ins, outs = refs[:n], refs[2 * n:3 * n]
        send_sems, recv_sems = refs[3 * n:]
        x, y, c = _place()
        my_chip = 2 * x + y
        chips = _other_chips(x, y)
        sibling = (x, y, 1 - c)

        def rows(k, which):
            h = ins[k].shape[0] // 2
            return pl.ds(which * h, h)

        def copy(sem, src, dst, to):
            return pltpu.make_async_remote_copy(src_ref=src, dst_ref=dst, send_sem=send_sems.at[sem],
                                                recv_sem=recv_sems.at[sem], device_id=to, device_id_type=MESH)

        sent = []
        for k in range(nb):
            for j, (cx, cy) in enumerate(chips):
                sent.append(copy(6 * k + j, ins[k].at[rows(k, c)], outs[k].at[my_chip, rows(k, c)], (cx, cy, c)))
        for k in range(nb, n):
            for j, (cx, cy) in enumerate(chips):
                sent.append(copy(6 * nb + 3 * (k - nb) + j, ins[k], outs[k].at[my_chip], (cx, cy, c)))
        for cp in sent:
            cp.start()
        for j, (cx, cy) in enumerate(chips):
            for k in range(nb):
                landed = outs[k].at[2 * cx + cy, rows(k, c)]
                copy(6 * k + j, landed, landed, (x, y, c)).wait_recv()
                fwd = copy(6 * k + 3 + j, landed, landed, sibling)
                fwd.start()
                sent.append(fwd)
        for j, (cx, cy) in enumerate(chips):
            for k in range(nb):
                landed = outs[k].at[2 * cx + cy, rows(k, 1 - c)]
                copy(6 * k + 3 + j, landed, landed, (x, y, c)).wait_recv()
            for k in range(nb, n):
                landed = outs[k].at[2 * cx + cy]
                copy(6 * nb + 3 * (k - nb) + j, landed, landed, (x, y, c)).wait_recv()
        for cp in sent:
            cp.wait_send()

    x, y, _ = _place()
    arrays = list(bigs) + list(smalls)
    landing = [lax.dynamic_update_index_in_dim(lax.empty((N_CHIPS,) + a.shape, a.dtype), a, 2 * x + y, 0)
               for a in arrays]
    n_sems = 6 * nb + 3 * ns
    return pl.pallas_call(
        body, name="all_gather_weights",
        out_shape=[jax.ShapeDtypeStruct(b.shape, b.dtype) for b in landing],
        in_specs=_hbm_specs(2 * n), out_specs=_hbm_specs(n), input_output_aliases={n + k: k for k in range(n)},
        scratch_shapes=[pltpu.SemaphoreType.DMA((n_sems,)), pltpu.SemaphoreType.DMA((n_sems,))])(*arrays, *landing)


def _pair_exchange(views):
    n = len(views)

    def body(*refs):
        ins, outs, send_sems, recv_sems = refs[:n], refs[n:2 * n], refs[2 * n], refs[2 * n + 1]
        x, y, c = _place()
        copies = [pltpu.make_async_remote_copy(
            src_ref=ins[k].at[:, pl.ds(1 - c, 1)], dst_ref=outs[k], send_sem=send_sems.at[k],
            recv_sem=recv_sems.at[k], device_id=(x, y, 1 - c), device_id_type=MESH) for k in range(n)]
        for cp in copies:
            cp.start()
        for cp in copies:
            cp.wait()

    return pl.pallas_call(
        body, name="pair_exchange",
        out_shape=[jax.ShapeDtypeStruct((v.shape[0], 1) + v.shape[2:], v.dtype) for v in views],
        in_specs=_hbm_specs(n), out_specs=_hbm_specs(n),
        scratch_shapes=[pltpu.SemaphoreType.DMA((n,)), pltpu.SemaphoreType.DMA((n,))])(*views)


def _scatter_to_chips(parts):
    n = len(parts)

    def body(*refs):
        ins, outs, send_sems, recv_sems = refs[:n], refs[n:2 * n], refs[2 * n], refs[2 * n + 1]
        x, y, c = _place()
        copies = [pltpu.make_async_remote_copy(
            src_ref=ins[k].at[pl.ds(2 * cx + cy, 1)], dst_ref=outs[k].at[pl.ds(r, 1)], send_sem=send_sems.at[3 * k + r],
            recv_sem=recv_sems.at[3 * k + r], device_id=(cx, cy, c), device_id_type=MESH)
            for k in range(n) for r, (cx, cy) in enumerate(_other_chips(x, y))]
        for cp in copies:
            cp.start()
        for cp in copies:
            cp.wait()

    return pl.pallas_call(
        body, name="scatter_grads", out_shape=[jax.ShapeDtypeStruct((3,) + p.shape[1:], p.dtype) for p in parts],
        in_specs=_hbm_specs(n), out_specs=_hbm_specs(n),
        scratch_shapes=[pltpu.SemaphoreType.DMA((3 * n,)), pltpu.SemaphoreType.DMA((3 * n,))])(*parts)


def _join_halves(shards):
    n = len(shards)

    def body(*refs):
        ins, outs, send_sems, recv_sems = refs[:n], refs[n:2 * n], refs[2 * n], refs[2 * n + 1]
        x, y, c = _place()

        def rows(ref, which):
            h = ref.shape[0] // 2
            return ref.at[pl.ds(which * h, h)]

        sent = [pltpu.make_async_remote_copy(
            src_ref=rows(ins[k], c), dst_ref=rows(outs[k], c), send_sem=send_sems.at[k], recv_sem=recv_sems.at[k],
            device_id=(x, y, 1 - c), device_id_type=MESH) for k in range(n)]
        for cp in sent:
            cp.start()
        for k in range(n):
            pltpu.make_async_remote_copy(
                src_ref=rows(ins[k], 1 - c), dst_ref=rows(outs[k], 1 - c), send_sem=send_sems.at[k],
                recv_sem=recv_sems.at[k], device_id=(x, y, 1 - c), device_id_type=MESH).wait_recv()
        for cp in sent:
            cp.wait_send()

    return pl.pallas_call(
        body, name="half_exchange", out_shape=[jax.ShapeDtypeStruct(a.shape, a.dtype) for a in shards],
        in_specs=_hbm_specs(n), out_specs=_hbm_specs(n), input_output_aliases={k: k for k in range(n)},
        scratch_shapes=[pltpu.SemaphoreType.DMA((n,)), pltpu.SemaphoreType.DMA((n,))])(*shards)


def _all_reduce_small(packet):
    rows, width = packet.shape
    n_dev = 8

    def body(x_ref, out_ref, gath, send_sems, recv_sems):
        x, y, c = _place()
        me, sibling = (x, y, c), (x, y, 1 - c)
        chips = _other_chips(x, y)

        def slot(px, py, pc):
            return gath.at[pl.ds((4 * px + 2 * py + pc) * rows, rows), :]

        def copy(k, block, to, src=None):
            return pltpu.make_async_remote_copy(
                src_ref=slot(*block) if src is None else src, dst_ref=slot(*block), send_sem=send_sems.at[k],
                recv_sem=recv_sems.at[k], device_id=to, device_id_type=MESH)

        first = [copy(0, me, sibling, src=x_ref)]
        first += [copy(1 + j, me, (*chip, c), src=x_ref) for j, chip in enumerate(chips)]
        for cp in first:
            cp.start()
        gath[pl.ds((4 * x + 2 * y + c) * rows, rows), :] = x_ref[...]
        passed = [copy(4 + j, (*chip, c), sibling) for j, chip in enumerate(chips)]
        for j, chip in enumerate(chips):
            copy(1 + j, (*chip, c), me).wait_recv()
            passed[j].start()
        copy(0, sibling, me).wait_recv()
        for j, chip in enumerate(chips):
            copy(4 + j, (*chip, 1 - c), me).wait_recv()
        for cp in first + passed:
            cp.wait_send()
        acc = gath[0:rows, :]
        for d in range(1, n_dev):
            acc = acc + gath[d * rows:(d + 1) * rows, :]
        out_ref[...] = acc

    return pl.pallas_call(
        body, name="all_reduce_small", out_shape=jax.ShapeDtypeStruct((rows, width), F32),
        in_specs=[pl.BlockSpec(memory_space=pltpu.VMEM)], out_specs=pl.BlockSpec(memory_space=pltpu.VMEM),
        scratch_shapes=[pltpu.VMEM((n_dev * rows, width), F32), pltpu.SemaphoreType.DMA((7,)),
                        pltpu.SemaphoreType.DMA((7,))])(packet)


def _flat_rows(parts, width, row_multiple):
    flat = jnp.concatenate([p.astype(F32).reshape(-1) for p in parts])
    rows = -(-flat.shape[0] // width)
    rows = -(-rows // row_multiple) * row_multiple
    return jnp.pad(flat, (0, rows * width - flat.shape[0])).reshape(rows, width)


def _unflatten(flat2d, shapes):
    flat = flat2d.reshape(-1)
    out, off = [], 0
    for shp in shapes:
        n = 1
        for dim in shp:
            n *= dim
        out.append(flat[off:off + n].reshape(shp))
        off += n
    return out


def _reduce_scatter_grads(chip_major, names):
    x, y, c = _place()
    sel = jnp.stack([c, 2 * x + y]).astype(jnp.int32)
    views = [g.reshape(N_CHIPS, 2, g.shape[1] // 2, g.shape[2]) for g in chip_major]
    recv = _pair_exchange(views)
    pairs = [_pair_sum(v, r, sel, "pair_sum_" + nm) for v, r, nm in zip(views, recv, names)]
    got = _scatter_to_chips([pb for _, pb in pairs])
    return _join_halves([_chip_sum(p, g, sel, "chip_sum_" + nm) for (p, _), g, nm in zip(pairs, got, names)])


def kernel(x, g_mix, w_in, b_f, w_conv, g_conv_out, g_attn_out, w_o, g_ffn, w_up, w_ffn_conv, w_down, g_final, loss_target, m_g_mix, m_w_in, m_b_f, m_w_conv, m_g_conv_out, m_g_attn_out, m_w_o, m_g_ffn, m_w_up, m_w_ffn_conv, m_w_down, m_g_final, v_g_mix, v_w_in, v_b_f, v_w_conv, v_g_conv_out, v_g_attn_out, v_w_o, v_g_ffn, v_w_up, v_w_ffn_conv, v_w_down, v_g_final):
    s = x.shape[1]
    x0 = x[0]
    target = loss_target[0]
    d = D_MODEL
    x_pos, y_pos, _ = _place()
    my_chip = 2 * x_pos + y_pos

    c_in, c_o, c_up, c_down, c_conv, c_ffn = _all_gather_weights(
        [w_in[0].astype(BF16), w_o[0].astype(BF16), w_up[0].astype(BF16), w_down[0].astype(BF16)],
        [w_conv[0], w_ffn_conv[0]])
    w_in_full = jnp.concatenate([c_in[j] for j in range(N_CHIPS)], axis=1)
    w_o_full = c_o.reshape(d, d)
    w_down_full = c_down.reshape(D_FF, d)
    w_conv_full = jnp.concatenate([c_conv[j] for j in range(N_CHIPS)], axis=1)
    w_ffn_full = jnp.concatenate([c_ffn[j] for j in range(N_CHIPS)], axis=1)
    c3 = 3 * CONV_CH
    w_a, w_b = w_in_full[:, :c3], w_in_full[:, c3:c3 + 3 * ATTN_W]
    w_c = jnp.pad(w_in_full[:, c3 + 3 * ATTN_W:], ((0, 0), (0, LANES - N_HEADS)))
    w_q, w_k, w_v = (w_b[:, i * ATTN_W:(i + 1) * ATTN_W] for i in range(3))
    n_up = c_up.shape[2]
    b_pad = jnp.pad(b_f, ((0, 0), (0, LANES - N_HEADS)))

    h1 = _rms_fwd(x0, g_mix, "rms_mix")
    z_a = _mm("nn", [h1], [w_a], F32, TM_MM, 512, "in_proj_conv")
    qkv = _mm("nn", [h1], [w_b], BF16, TM_MM, 512, "in_proj_qkv")
    f_log = _mm("nn", [h1], [w_c], F32, TM_MM, LANES, "in_proj_gate")
    fb, ft = _gate_fwd(f_log, b_pad)
    first_blk = _first_key_blocks(qkv, fb)
    o_attn, g_att = _attn_fwd(qkv, fb, ft, first_blk)
    mix = _mixer_fwd(z_a, o_attn, w_conv_full, g_conv_out, g_attn_out)
    x2 = _mm("nn", [mix], [w_o_full], F32, TM_MM, 512, "out_proj", add=x0)
    h2 = _rms_fwd(x2, g_ffn, "rms_ffn")
    up = _mm("nn", [h2], [c_up], F32, TM_MM, n_up, "up_proj", b_chips=True)
    act = _ffn_act_fwd(up, w_ffn_full)
    x3 = _mm("nn", [act], [w_down_full], F32, 512, 512, "down_proj", add=x2)

    dx3, dx3_b, loss_row, gg_final = _loss_head(x3, target, g_final.reshape(1, d))
    dact = _mm("nt", [dx3_b], [w_down_full], F32, TM_MM, 1408, "d_act")
    gw_down = _mm_tn(act, dx3_b, 1408, 1024, "gw_down")
    dup, gwf_lin, gwf_gate = _ffn_act_bwd(up, dact, w_ffn_full)
    dh2 = _mm("nt", [(dup, j // 2, j % 2, n_up) for j in range(N_CHIPS)], [(c_up, j) for j in range(N_CHIPS)],
              F32, 512, 512, "d_h2")
    gw_up = _mm_tn(h2, dup, 1024, n_up, "gw_up", out_chips=True)
    dx2, dx2_b, gg_ffn = _rms_bwd(x2, dh2, g_ffn, dx3, "rms_ffn_bwd", True)
    dmix = _mm("nt", [dx2_b], [w_o_full], F32, TM_MM, 512, "d_mix")
    gw_o = _mm_tn(mix, dx2_b, 1024, 1024, "gw_o")
    dz_a, d_o, delta, gw_conv, gg_conv_out, gg_attn_out = _mixer_bwd(z_a, o_attn, dmix, w_conv_full, g_conv_out,
                                                                     g_attn_out)
    dq, dfq, dk, dv, dfk = _attn_bwd(qkv, d_o, g_att, delta, ft, first_blk)
    d_f = jnp.transpose(dfk[:, :, 0:2, :], (1, 3, 0, 2)).reshape(s, N_HEADS) + dfq[:, ::HEAD_DIM]
    df_b, gb_f = _gate_bwd(f_log, b_pad, jnp.pad(d_f, ((0, 0), (0, LANES - N_HEADS))))
    dh1 = _mm("nt", [dz_a, dq, dk, dv, df_b], [w_a, w_q, w_k, w_v, w_c], F32, TM_MM, 512, "d_h1")
    gw_a = _mm_tn(h1, dz_a, 1024, c3, "gw_in_conv")
    gw_q = _mm_tn(h1, dq, 1024, ATTN_W, "gw_in_q")
    gw_k = _mm_tn(h1, dk, 1024, ATTN_W, "gw_in_k")
    gw_v = _mm_tn(h1, dv, 1024, ATTN_W, "gw_in_v")
    gw_c = _mm_tn(h1, df_b, 1024, LANES, "gw_in_gate")
    grad_x, gg_mix = _rms_bwd(x0, dh1, g_mix, dx2, "rms_mix_bwd", False)

    gw_in = jnp.concatenate([gw_a, gw_q, gw_k, gw_v, gw_c[:, :N_HEADS]], axis=1)
    n_in = IN_COLS // N_CHIPS
    gw_in = jnp.stack([gw_in[:, j * n_in:(j + 1) * n_in] for j in range(N_CHIPS)])
    g_w_in, g_w_o, g_w_up, g_w_down = _reduce_scatter_grads(
        [gw_in, gw_o.reshape(N_CHIPS, d // N_CHIPS, d), gw_up, gw_down.reshape(N_CHIPS, D_FF // N_CHIPS, d)],
        ["w_in", "w_o", "w_up", "w_down"])

    gw_ffn = jnp.concatenate([gwf_lin, gwf_gate], axis=1)
    small_parts = [gg_mix, gg_conv_out, gg_attn_out, gg_ffn, gg_final, gb_f[:, :N_HEADS], loss_row[:, 0:1], gw_conv,
                   gw_ffn]
    small_shapes = [a.shape for a in small_parts]
    tot = _unflatten(_all_reduce_small(_flat_rows(small_parts, d, 8)), small_shapes)
    g_g_mix, g_g_conv_out, g_g_attn_out, g_g_ffn, g_g_final, g_b_f, loss_sum, g_conv_full, g_ffn_full = tot
    loss = loss_sum[0, 0]
    g_g_final = g_g_final[0]
    g_w_conv = lax.dynamic_slice_in_dim(g_conv_full, my_chip * (CONV_CH // N_CHIPS), CONV_CH // N_CHIPS, axis=1)
    g_w_ffn = lax.dynamic_slice_in_dim(g_ffn_full, my_chip * n_up, n_up, axis=1)

    def adam_big(w, g, m, v, name):
        dl, nm, nv = _adamw(w[0], g, m[0], v[0], name)
        return dl[None], nm[None], nv[None]

    u_w_in = adam_big(w_in, g_w_in, m_w_in, v_w_in, "adam_w_in")
    u_w_o = adam_big(w_o, g_w_o, m_w_o, v_w_o, "adam_w_o")
    u_w_up = adam_big(w_up, g_w_up, m_w_up, v_w_up, "adam_w_up")
    u_w_down = adam_big(w_down, g_w_down, m_w_down, v_w_down, "adam_w_down")

    small_w = [g_mix, b_f, g_conv_out, g_attn_out, g_ffn, g_final, w_conv, w_ffn_conv]
    small_g = [g_g_mix, g_b_f, g_g_conv_out, g_g_attn_out, g_g_ffn, g_g_final, g_w_conv, g_w_ffn]
    small_m = [m_g_mix, m_b_f, m_g_conv_out, m_g_attn_out, m_g_ffn, m_g_final, m_w_conv, m_w_ffn_conv]
    small_v = [v_g_mix, v_b_f, v_g_conv_out, v_g_attn_out, v_g_ffn, v_g_final, v_w_conv, v_w_ffn_conv]
    shapes = [a.shape for a in small_w]
    pack = lambda arrs: _flat_rows(arrs, LANES, 8)
    sd, sm, sv = _adamw(pack(small_w), pack(small_g), pack(small_m), pack(small_v), "adam_small")
    sd, sm, sv = _unflatten(sd, shapes), _unflatten(sm, shapes), _unflatten(sv, shapes)
    (d_g_mix, d_b_f, d_g_conv_out, d_g_attn_out, d_g_ffn, d_g_final, d_w_conv, d_w_ffn) = sd
    (nm_g_mix, nm_b_f, nm_g_conv_out, nm_g_attn_out, nm_g_ffn, nm_g_final, nm_w_conv, nm_w_ffn) = sm
    (nv_g_mix, nv_b_f, nv_g_conv_out, nv_g_attn_out, nv_g_ffn, nv_g_final, nv_w_conv, nv_w_ffn) = sv

    grads = (g_g_mix, g_w_in[None], g_b_f, g_w_conv[None], g_g_conv_out, g_g_attn_out, g_w_o[None], g_g_ffn,
             g_w_up[None], g_w_ffn[None], g_w_down[None], g_g_final)
    deltas = (d_g_mix, u_w_in[0], d_b_f, d_w_conv, d_g_conv_out, d_g_attn_out, u_w_o[0], d_g_ffn, u_w_up[0],
              d_w_ffn, u_w_down[0], d_g_final)
    new_m = (nm_g_mix, u_w_in[1], nm_b_f, nm_w_conv, nm_g_conv_out, nm_g_attn_out, u_w_o[1], nm_g_ffn, u_w_up[1],
             nm_w_ffn, u_w_down[1], nm_g_final)
    new_v = (nv_g_mix, u_w_in[2], nv_b_f, nv_w_conv, nv_g_conv_out, nv_g_attn_out, u_w_o[2], nv_g_ffn, u_w_up[2],
             nv_w_ffn, u_w_down[2], nv_g_final)
    return (loss, grad_x[None], *grads, *deltas, *new_m, *new_v)
```

```python
import functools

import jax
import jax.numpy as jnp
from jax import lax
from jax.experimental import pallas as pl
from jax.experimental.pallas import tpu as pltpu

F32, BF16 = jnp.float32, jnp.bfloat16
MESH = pl.DeviceIdType.MESH

D_MODEL = 1024
CONV_CH = 512
ATTN_W = 512
N_HEADS = 8
HEAD_DIM = 64
N_PAIRS = N_HEADS // 2
D_FF = 2816
IN_COLS = 3 * CONV_CH + 3 * ATTN_W + N_HEADS
EPS = 1e-6
Q_SCALE = 0.125
EXP_ZERO = 104.0
N_CHIPS = 4
LANES = 128
HALO = 8

ADAM_LR, ADAM_B1, ADAM_B2, ADAM_EPS, ADAM_WD, ADAM_STEP = 0.001, 0.9, 0.999, 1e-08, 0.01, 10

TM_ROWS = 512
TM_MM = 1024
TK_TN = 512
TQ = 512
ROW_CHUNK = 32
TM_FFN = 1024
TN_FFN = 256
VMEM_LIMIT = 52 * 2**20


def _cp(sem, vmem=VMEM_LIMIT):
    return pltpu.CompilerParams(dimension_semantics=sem, vmem_limit_bytes=vmem)


def _bf(a):
    return a if a.dtype == BF16 else a.astype(BF16)


def _mm(mode, a_list, b_list, out_dtype, tm, tn, name, add=None, b_chips=False):
    n_p = len(a_list)
    a0 = a_list[0]
    m_dim = a0[0].shape[1] if isinstance(a0, tuple) else a0.shape[0]
    b0 = b_list[0]
    if b_chips:
        n_dim = b0.shape[0] * b0.shape[2]
        assert tn == b0.shape[2] and mode == "nn"
    else:
        b0 = b0[0][b0[1]] if isinstance(b0, tuple) else b0
        n_dim = b0.shape[1 if mode == "nn" else 0]
    tm, tn = min(tm, m_dim), min(tn, n_dim)
    assert m_dim % tm == 0 and n_dim % tn == 0
    dims = (((1,), (0,)), ((), ())) if mode == "nn" else (((1,), (1,)), ((), ()))
    in_specs, args = [], []
    for a in a_list:
        if isinstance(a, tuple):
            arr, lead, col, width = a
            in_specs.append(pl.BlockSpec((None, tm, width), lambda m, n, lead=lead, col=col: (lead, m, col)))
        else:
            arr = a
            in_specs.append(pl.BlockSpec((tm, a.shape[1]), lambda m, n: (m, 0)))
        args.append(arr)
    for b in b_list:
        if b_chips:
            arr = b
            in_specs.append(pl.BlockSpec((None, b.shape[1], tn), lambda m, n: (n, 0, 0)))
        elif isinstance(b, tuple):
            arr, lead = b
            if mode == "nn":
                in_specs.append(pl.BlockSpec((None, arr.shape[1], tn), lambda m, n, lead=lead: (lead, 0, n)))
            else:
                in_specs.append(pl.BlockSpec((None, tn, arr.shape[2]), lambda m, n, lead=lead: (lead, n, 0)))
        elif mode == "nn":
            arr = b
            in_specs.append(pl.BlockSpec((b.shape[0], tn), lambda m, n: (0, n)))
        else:
            arr = b
            in_specs.append(pl.BlockSpec((tn, b.shape[1]), lambda m, n: (n, 0)))
        args.append(arr)
    if add is not None:
        in_specs.append(pl.BlockSpec((tm, tn), lambda m, n: (m, n)))
        args.append(add)

    def body(*refs):
        o_ref = refs[-1]
        acc = None
        for i in range(n_p):
            d = lax.dot_general(_bf(refs[i][...]), _bf(refs[n_p + i][...]), dims,
                                preferred_element_type=F32)
            acc = d if acc is None else acc + d
        if add is not None:
            acc = refs[2 * n_p][...] + acc
        o_ref[...] = acc.astype(out_dtype)

    return pl.pallas_call(
        body, name=name, grid=(m_dim // tm, n_dim // tn), in_specs=in_specs,
        out_specs=pl.BlockSpec((tm, tn), lambda m, n: (m, n)),
        out_shape=jax.ShapeDtypeStruct((m_dim, n_dim), out_dtype),
        compiler_params=_cp(("parallel", "parallel")))(*args)


def _mm_tn(a, b, tm, tn, name, out_chips=False):
    k_dim, m_dim = a.shape
    n_dim = b.shape[-1] * (b.shape[0] if b.ndim == 3 else 1)
    tm, tn, tk = min(tm, m_dim), min(tn, b.shape[-1]), min(TK_TN, k_dim)
    assert m_dim % tm == 0 and b.shape[-1] % tn == 0 and k_dim % tk == 0
    per = b.shape[-1] // tn
    if b.ndim == 3:
        b_spec = pl.BlockSpec((None, tk, tn), lambda m, n, k: (n // per, k, n % per))
    else:
        b_spec = pl.BlockSpec((tk, tn), lambda m, n, k: (k, n))

    def body(a_ref, b_ref, o_ref):
        @pl.when(pl.program_id(2) == 0)
        def _():
            o_ref[...] = jnp.zeros_like(o_ref)
        o_ref[...] += lax.dot_general(_bf(a_ref[...]), _bf(b_ref[...]), (((0,), (0,)), ((), ())),
                                      preferred_element_type=F32)

    return pl.pallas_call(
        body, name=name, grid=(m_dim // tm, n_dim // tn, k_dim // tk),
        in_specs=[pl.BlockSpec((tk, tm), lambda m, n, k: (k, m)), b_spec],
        out_specs=(pl.BlockSpec((None, tm, tn), lambda m, n, k: (n, m, 0)) if out_chips
                   else pl.BlockSpec((tm, tn), lambda m, n, k: (m, n))),
        out_shape=jax.ShapeDtypeStruct((n_dim // tn, m_dim, tn) if out_chips else (m_dim, n_dim), F32),
        compiler_params=_cp(("parallel", "parallel", "arbitrary")))(a, b)


def _rstd(x):
    return lax.rsqrt(jnp.mean(x * x, axis=-1, keepdims=True) + EPS)


def _rms_fwd(x, g, name):
    s, d = x.shape
    tm = min(TM_ROWS, s)

    def body(x_ref, g_ref, h_ref):
        xv = x_ref[...]
        h_ref[...] = (xv * _rstd(xv) * g_ref[...]).astype(BF16)

    return pl.pallas_call(
        body, name=name, grid=(s // tm,),
        in_specs=[pl.BlockSpec((tm, d), lambda i: (i, 0)), pl.BlockSpec((1, d), lambda i: (0, 0))],
        out_specs=pl.BlockSpec((tm, d), lambda i: (i, 0)),
        out_shape=jax.ShapeDtypeStruct((s, d), BF16), compiler_params=_cp(("parallel",)))(x, g)


def _rms_bwd(x, dh, g, dres, name, with_bf16):
    s, d = x.shape
    tm = min(TM_ROWS, s)

    def body(x_ref, dh_ref, g_ref, dres_ref, dx_ref, *rest):
        gg_ref = rest[-1]

        @pl.when(pl.program_id(0) == 0)
        def _():
            gg_ref[...] = jnp.zeros_like(gg_ref)

        xv = x_ref[...]
        xn = xv * _rstd(xv)
        dhv = dh_ref[...]
        gg_ref[...] += jnp.sum(dhv * xn, axis=0, keepdims=True)
        t = dhv * g_ref[...]
        dx = dres_ref[...] + _rstd(xv) * (t - xn * jnp.mean(t * xn, axis=-1, keepdims=True))
        dx_ref[...] = dx
        if with_bf16:
            rest[0][...] = dx.astype(BF16)

    row = pl.BlockSpec((tm, d), lambda i: (i, 0))
    vec = pl.BlockSpec((1, d), lambda i: (0, 0))
    out_specs = [row] + ([row] if with_bf16 else []) + [vec]
    out_shape = ([jax.ShapeDtypeStruct((s, d), F32)] + ([jax.ShapeDtypeStruct((s, d), BF16)] if with_bf16 else [])
                 + [jax.ShapeDtypeStruct((1, d), F32)])
    return pl.pallas_call(
        body, name=name, grid=(s // tm,), in_specs=[row, row, vec, row], out_specs=out_specs, out_shape=out_shape,
        compiler_params=_cp(("arbitrary",)))(x, dh, g, dres)


def _loss_head(x3, target, g):
    s, d = x3.shape
    tm = min(TM_ROWS, s)

    def body(x_ref, t_ref, g_ref, dx_ref, dxb_ref, loss_ref, gg_ref):
        @pl.when(pl.program_id(0) == 0)
        def _():
            gg_ref[...] = jnp.zeros_like(gg_ref)
            loss_ref[...] = jnp.zeros_like(loss_ref)

        xv = x_ref[...]
        r = _rstd(xv)
        xn = xv * r
        gv = g_ref[...]
        err = xn * gv - t_ref[...]
        loss_ref[...] += 0.5 * jnp.sum(jnp.mean(err * err, axis=-1, keepdims=True), axis=0, keepdims=True)
        dy = err * (1.0 / d)
        gg_ref[...] += jnp.sum(dy * xn, axis=0, keepdims=True)
        t = dy * gv
        dx = r * (t - xn * jnp.mean(t * xn, axis=-1, keepdims=True))
        dx_ref[...] = dx
        dxb_ref[...] = dx.astype(BF16)

    row = pl.BlockSpec((tm, d), lambda i: (i, 0))
    vec = pl.BlockSpec((1, d), lambda i: (0, 0))
    return pl.pallas_call(
        body, name="loss_head", grid=(s // tm,), in_specs=[row, row, vec],
        out_specs=[row, row, pl.BlockSpec((1, LANES), lambda i: (0, 0)), vec],
        out_shape=[jax.ShapeDtypeStruct((s, d), F32), jax.ShapeDtypeStruct((s, d), BF16),
                   jax.ShapeDtypeStruct((1, LANES), F32), jax.ShapeDtypeStruct((1, d), F32)],
        compiler_params=_cp(("arbitrary",)))(x3, target, g)


def _prev_halo_spec(tm, width, col):
    return pl.BlockSpec((HALO, width), lambda i, *_: (jnp.maximum(i * (tm // HALO) - 1, 0), col))


def _next_halo_spec(tm, width, col, s):
    return pl.BlockSpec((HALO, width), lambda i, *_: (jnp.minimum((i + 1) * (tm // HALO), s // HALO - 1), col))


def _shift_down(x, k):
    return pltpu.roll(x, k, 0)


def _shift_up(x, k):
    return pltpu.roll(x, x.shape[0] - k, 0)


def _conv_taps(x_ext, w):
    return w[0:1, :] * _shift_down(x_ext, 2) + w[1:2, :] * _shift_down(x_ext, 1) + w[2:3, :] * x_ext


def _conv_taps_t(d_ext, w):
    return w[2:3, :] * d_ext + w[1:2, :] * _shift_up(d_ext, 1) + w[0:1, :] * _shift_up(d_ext, 2)


def _mixer_fwd(z_a, o_attn, w_conv, g_conv_out, g_attn_out):
    s = z_a.shape[0]
    c = CONV_CH
    tm = min(TM_ROWS, s)

    def body(gb_ref, gc_ref, xc_ref, gcp_ref, xcp_ref, o_ref, w_ref, gco_ref, gao_ref, mix_ref):
        i = pl.program_id(0)
        cx = gc_ref[...] * xc_ref[...]
        cx_prev = jnp.where(i > 0, gcp_ref[...] * xcp_ref[...], 0.0)
        conv = _conv_taps(jnp.concatenate([cx_prev, cx], axis=0), w_ref[...])[HALO:]
        y = gb_ref[...] * conv
        mix_ref[:, 0:c] = (y * _rstd(y) * gco_ref[...]).astype(BF16)
        o = o_ref[...]
        mix_ref[:, c:2 * c] = (o * _rstd(o) * gao_ref[...]).astype(BF16)

    col = lambda j: pl.BlockSpec((tm, c), lambda i: (i, j))
    vec = pl.BlockSpec((1, c), lambda i: (0, 0))
    return pl.pallas_call(
        body, name="mixer_fwd", grid=(s // tm,),
        in_specs=[col(0), col(1), col(2), _prev_halo_spec(tm, c, 1), _prev_halo_spec(tm, c, 2), col(0),
                  pl.BlockSpec((3, c), lambda i: (0, 0)), vec, vec],
        out_specs=pl.BlockSpec((tm, 2 * c), lambda i: (i, 0)),
        out_shape=jax.ShapeDtypeStruct((s, 2 * c), BF16),
        compiler_params=_cp(("parallel",)))(z_a, z_a, z_a, z_a, z_a, o_attn, w_conv, g_conv_out, g_attn_out)


def _mixer_bwd(z_a, o_attn, dmix, w_conv, g_conv_out, g_attn_out):
    s = z_a.shape[0]
    c = CONV_CH
    tm = min(TM_ROWS, s)
    n_blk = s // tm

    def body(gb_ref, gc_ref, xc_ref, gcp_ref, xcp_ref, gbn_ref, gcn_ref, xcn_ref, o_ref, dnc_ref, dncn_ref, dna_ref,
             w_ref, gco_ref, gao_ref, dz_ref, dox_ref, gw_ref, ggco_ref, ggao_ref):
        i = pl.program_id(0)

        @pl.when(i == 0)
        def _():
            gw_ref[...] = jnp.zeros_like(gw_ref)
            ggco_ref[...] = jnp.zeros_like(ggco_ref)
            ggao_ref[...] = jnp.zeros_like(ggao_ref)

        w = w_ref[...]
        zeros = jnp.zeros((HALO, c), F32)
        gb_e = jnp.concatenate([zeros, gb_ref[...], gbn_ref[...]], axis=0)
        cx_prev = jnp.where(i > 0, gcp_ref[...] * xcp_ref[...], 0.0)
        gc_e = jnp.concatenate([zeros, gc_ref[...], gcn_ref[...]], axis=0)
        xc_e = jnp.concatenate([zeros, xc_ref[...], xcn_ref[...]], axis=0)
        cx_e = jnp.concatenate([cx_prev, gc_ref[...] * xc_ref[...], gcn_ref[...] * xcn_ref[...]], axis=0)
        dn_next = jnp.where(i < n_blk - 1, dncn_ref[...], 0.0)
        dn_e = jnp.concatenate([zeros, dnc_ref[...], dn_next], axis=0)

        conv_e = _conv_taps(cx_e, w)
        y_e = gb_e * conv_e
        r_e = _rstd(y_e)
        yn_e = y_e * r_e
        t_e = dn_e * gco_ref[...]
        dy_e = r_e * (t_e - yn_e * jnp.mean(t_e * yn_e, axis=-1, keepdims=True))
        dconv_e = dy_e * gb_e
        dcx_e = _conv_taps_t(dconv_e, w)
        blk = slice(HALO, HALO + tm)
        dz_ref[:, 0:c] = (dy_e * conv_e)[blk].astype(BF16)
        dz_ref[:, c:2 * c] = (dcx_e * xc_e)[blk].astype(BF16)
        dz_ref[:, 2 * c:3 * c] = (dcx_e * gc_e)[blk].astype(BF16)
        ggco_ref[...] += jnp.sum((dn_e * yn_e)[blk], axis=0, keepdims=True)
        dconv = dconv_e[blk]
        gw_ref[0:1, :] += jnp.sum(dconv * _shift_down(cx_e, 2)[blk], axis=0, keepdims=True)
        gw_ref[1:2, :] += jnp.sum(dconv * _shift_down(cx_e, 1)[blk], axis=0, keepdims=True)
        gw_ref[2:3, :] += jnp.sum(dconv * cx_e[blk], axis=0, keepdims=True)

        o = o_ref[...]
        ra = _rstd(o)
        on = o * ra
        dna = dna_ref[...]
        ggao_ref[...] += jnp.sum(dna * on, axis=0, keepdims=True)
        ta = dna * gao_ref[...]
        do = ra * (ta - on * jnp.mean(ta * on, axis=-1, keepdims=True))
        prod = do * o
        lane = lax.broadcasted_iota(jnp.int32, (tm, LANES), 1)
        head_a = lane < HEAD_DIM
        for p in range(N_PAIRS):
            cols = slice(p * LANES, (p + 1) * LANES)
            pb, dob = prod[:, cols], do[:, cols]
            for hh in range(2):
                sel = head_a if hh == 0 else jnp.logical_not(head_a)
                delta = jnp.sum(jnp.where(sel, pb, 0.0), axis=-1, keepdims=True)
                neg3 = _split3(-delta)
                do_h = pltpu.roll(dob, HEAD_DIM, 1) if hh else dob
                dox_ref[2 * p + hh] = _aug(do_h, lane, neg3).astype(BF16)

    col = lambda j: pl.BlockSpec((tm, c), lambda i: (i, j))
    vec = pl.BlockSpec((1, c), lambda i: (0, 0))
    w3 = pl.BlockSpec((3, c), lambda i: (0, 0))
    return pl.pallas_call(
        body, name="mixer_bwd", grid=(n_blk,),
        in_specs=[col(0), col(1), col(2), _prev_halo_spec(tm, c, 1), _prev_halo_spec(tm, c, 2),
                  _next_halo_spec(tm, c, 0, s), _next_halo_spec(tm, c, 1, s), _next_halo_spec(tm, c, 2, s),
                  col(0), col(0), _next_halo_spec(tm, c, 0, s), col(1), w3, vec, vec],
        out_specs=[pl.BlockSpec((tm, 3 * c), lambda i: (i, 0)),
                   pl.BlockSpec((N_HEADS, tm, LANES), lambda i: (0, i, 0)), w3, vec, vec],
        out_shape=[jax.ShapeDtypeStruct((s, 3 * c), BF16), jax.ShapeDtypeStruct((N_HEADS, s, LANES), BF16),
                   jax.ShapeDtypeStruct((3, c), F32), jax.ShapeDtypeStruct((1, c), F32),
                   jax.ShapeDtypeStruct((1, c), F32)],
        compiler_params=_cp(("arbitrary",)))(
            z_a, z_a, z_a, z_a, z_a, z_a, z_a, z_a, o_attn, dmix, dmix, dmix, w_conv, g_conv_out, g_attn_out)


def _gate_fwd(f, b_pad):
    s = f.shape[0]
    tm = min(TQ, s)

    def body(f_ref, b_ref, fb_ref, carry):
        @pl.when(pl.program_id(0) == 0)
        def _():
            carry[...] = jnp.zeros_like(carry)

        z = f_ref[...] + b_ref[...]
        x = jnp.minimum(z, 0.0) - jnp.log1p(jnp.exp(-jnp.abs(z)))
        row = lax.broadcasted_iota(jnp.int32, (tm, LANES), 0)
        sh = 1
        while sh < tm:
            x = x + jnp.where(row >= sh, _shift_down(x, sh), 0.0)
            sh *= 2
        x = x + carry[0:1, :]
        carry[...] = jnp.broadcast_to(x[tm - 1:tm, :], carry.shape)
        head_a = lax.broadcasted_iota(jnp.int32, (tm, LANES), 1) < HEAD_DIM
        for p in range(N_PAIRS):
            fa = jnp.broadcast_to(x[:, 2 * p:2 * p + 1], (tm, LANES))
            fbv = jnp.broadcast_to(x[:, 2 * p + 1:2 * p + 2], (tm, LANES))
            fb_ref[:, p * LANES:(p + 1) * LANES] = jnp.where(head_a, fa, fbv)

    return pl.pallas_call(
        body, name="gate_fwd", grid=(s // tm,),
        in_specs=[pl.BlockSpec((tm, LANES), lambda i: (i, 0)), pl.BlockSpec((1, LANES), lambda i: (0, 0))],
        out_specs=pl.BlockSpec((tm, N_PAIRS * LANES), lambda i: (i, 0)),
        out_shape=jax.ShapeDtypeStruct((s, N_PAIRS * LANES), F32),
        scratch_shapes=[pltpu.VMEM((HALO, LANES), F32)],
        compiler_params=_cp(("arbitrary",)))(f, b_pad)


def _gate_bwd(f, b_pad, d_f):
    s = f.shape[0]
    tm = min(TQ, s)
    n_blk = s // tm

    def body(f_ref, b_ref, d_ref, df_ref, gb_ref, carry):
        @pl.when(pl.program_id(0) == 0)
        def _():
            carry[...] = jnp.zeros_like(carry)
            gb_ref[...] = jnp.zeros_like(gb_ref)

        x = d_ref[...]
        row = lax.broadcasted_iota(jnp.int32, (tm, LANES), 0)
        sh = 1
        while sh < tm:
            x = x + jnp.where(row < tm - sh, _shift_up(x, sh), 0.0)
            sh *= 2
        x = x + carry[0:1, :]
        carry[...] = jnp.broadcast_to(x[0:1, :], carry.shape)
        z = f_ref[...] + b_ref[...]
        d = x * (1.0 / (1.0 + jnp.exp(z)))
        df_ref[...] = d.astype(BF16)
        gb_ref[...] += jnp.sum(d, axis=0, keepdims=True)

    rev = pl.BlockSpec((tm, LANES), lambda i: (n_blk - 1 - i, 0))
    vec = pl.BlockSpec((1, LANES), lambda i: (0, 0))
    return pl.pallas_call(
        body, name="gate_bwd", grid=(n_blk,), in_specs=[rev, vec, rev], out_specs=[rev, vec],
        out_shape=[jax.ShapeDtypeStruct((s, LANES), BF16), jax.ShapeDtypeStruct((1, LANES), F32)],
        scratch_shapes=[pltpu.VMEM((HALO, LANES), F32)],
        compiler_params=_cp(("arbitrary",)))(f, b_pad, d_f)


_NT = (((1,), (1,)), ((), ()))
_NN = (((1,), (0,)), ((), ()))
_TN = (((0,), (0,)), ((), ()))


def _head_masks(shape):
    lane = lax.broadcasted_iota(jnp.int32, shape, len(shape) - 1)
    return lane < HEAD_DIM


def _pick_row(ft, h):
    rows = lax.broadcasted_iota(jnp.int32, ft.shape, 0)
    return jnp.sum(jnp.where(rows == h, ft, 0.0), axis=0, keepdims=True)


def _prune_bounds(qkv, fb):
    s = qkv.shape[0]
    tq = min(TQ, s)

    def body(q_ref, k_ref, fb_ref, out_ref):
        head_a = _head_masks((tq, LANES))
        lane = lax.broadcasted_iota(jnp.int32, (HALO, LANES), 1)
        acc = jnp.zeros((HALO, LANES), F32)
        for p in range(N_PAIRS):
            cols = slice(p * LANES, (p + 1) * LANES)
            q2 = q_ref[:, cols].astype(F32) * Q_SCALE
            k2 = k_ref[:, cols].astype(F32)
            f2 = fb_ref[:, cols]
            for hh in range(2):
                sel = head_a if hh == 0 else jnp.logical_not(head_a)
                qn = jnp.sqrt(jnp.sum(jnp.where(sel, q2 * q2, 0.0), axis=-1, keepdims=True))
                kn = jnp.sqrt(jnp.sum(jnp.where(sel, k2 * k2, 0.0), axis=-1, keepdims=True))
                f = f2[:, hh * HEAD_DIM:hh * HEAD_DIM + 1]
                h = 2 * p + hh
                vals = (jnp.max(qn, axis=0, keepdims=True), jnp.max(kn, axis=0, keepdims=True),
                        jnp.max(qn * kn + f, axis=0, keepdims=True), f[tq - 1:tq, :])
                for slot, v in enumerate(vals):
                    acc = jnp.where(lane == slot * N_HEADS + h, v, acc)
        out_ref[0] = acc

    blk = lambda j: pl.BlockSpec((tq, ATTN_W), lambda i: (i, j))
    return pl.pallas_call(
        body, name="prune_bounds", grid=(s // tq,), in_specs=[blk(0), blk(1), blk(0)],
        out_specs=pl.BlockSpec((1, HALO, LANES), lambda i: (i, 0, 0)),
        out_shape=jax.ShapeDtypeStruct((s // tq, HALO, LANES), F32),
        compiler_params=_cp(("parallel",)))(qkv, qkv, fb)


def _first_key_blocks(qkv, fb):
    t = _prune_bounds(qkv, fb)[:, 0, :]
    nh = N_HEADS
    a, b, c, e = t[:, 0:nh], t[:, nh:2 * nh], t[:, 2 * nh:3 * nh], t[:, 3 * nh:4 * nh]
    bound = a[:, None, :] * b[None, :, :] * 1.001 + c[:, None, :] - e[None, :, :]
    n_q = t.shape[0]
    idx = jnp.arange(n_q)
    need = jnp.logical_not(bound < -(EXP_ZERO + 2.0)) | (idx[None, :, None] >= idx[:, None, None])
    first = jnp.argmax(need, axis=1).astype(jnp.int32)
    return jnp.min(first.reshape(n_q, N_PAIRS, 2), axis=-1).T.reshape(-1)


def _attn_fwd(qkv, fb, ft, first_blk):
    s = qkv.shape[0]
    tq = min(TQ, s)
    n_q = s // tq
    neg = -1e30

    rc = min(ROW_CHUNK, tq)

    def body(first_ref, q_ref, k_ref, v_ref, fb_ref, ft_ref, o_ref, g_ref,
             s_a, s_b, p_ab, m_a, l_a, m_b, l_b, alpha2, acc, pmax_a, psum_a, pmax_b, psum_b):
        p = pl.program_id(0)
        i = pl.program_id(1)
        head_a = _head_masks((tq, LANES))
        q2 = q_ref[...] * Q_SCALE
        zero = jnp.zeros_like(q2)
        q_a, q_b = jnp.where(head_a, q2, zero), jnp.where(head_a, zero, q2)
        for m_scr, l_scr in ((m_a, l_a), (m_b, l_b)):
            m_scr[...] = jnp.full(m_scr.shape, neg, F32)
            l_scr[...] = jnp.zeros(l_scr.shape, F32)
        acc[...] = jnp.zeros(acc.shape, F32)

        def step(kb, masked):
            rows_k = pl.ds(pl.multiple_of(kb * tq, tq), tq)
            k2, v2 = k_ref[rows_k, :], v_ref[rows_k, :]
            ftv = ft_ref[kb]
            fk = (_pick_row(ftv, 2 * p), _pick_row(ftv, 2 * p + 1))
            s_a[...] = lax.dot_general(q_a, k2, _NT, preferred_element_type=F32)
            s_b[...] = lax.dot_general(q_b, k2, _NT, preferred_element_type=F32)
            alphas = []
            for hh, (s_scr, m_scr, l_scr, pmax, psum) in enumerate(((s_a, m_a, l_a, pmax_a, psum_a),
                                                                     (s_b, m_b, l_b, pmax_b, psum_b))):
                fq = fb_ref[:, hh * HEAD_DIM:hh * HEAD_DIM + 1]

                def shifted(r, ncol, add=None):
                    rows = slice(r * rc, (r + 1) * rc)
                    t = s_scr[rows, 0:ncol]
                    if add is not None:
                        t = t + add[rows, :]
                    t = t - fk[hh][:, 0:ncol]
                    if masked:
                        col_id = lax.broadcasted_iota(jnp.int32, (rc, ncol), 1)
                        row_id = lax.broadcasted_iota(jnp.int32, (rc, ncol), 0) + r * rc
                        t = jnp.where(col_id <= row_id, t, -jnp.inf)
                    return rows, t

                def lane_blocks(t, op):
                    out = t[:, 0:LANES]
                    for cb in range(1, t.shape[1] // LANES):
                        out = op(out, t[:, cb * LANES:(cb + 1) * LANES])
                    return out

                ncols = [min(tq, -(-((r + 1) * rc) // LANES) * LANES) if masked else tq for r in range(tq // rc)]
                for r, ncol in enumerate(ncols):
                    rows, t = shifted(r, ncol)
                    pmax[rows, :] = lane_blocks(t, jnp.maximum)
                m_old = m_scr[...]
                m_new = jnp.maximum(m_old, jnp.max(pmax[...], axis=-1, keepdims=True) + fq)
                alpha = jnp.exp(m_old - m_new)
                m_scr[...] = m_new
                shift = fq - m_new
                for r, ncol in enumerate(ncols):
                    rows, t = shifted(r, ncol, shift)
                    pr = jnp.exp(t)
                    psum[rows, :] = lane_blocks(pr, jnp.add)
                    p_ab[rows, hh * tq:hh * tq + ncol] = pr.astype(BF16)
                    if ncol < tq:
                        p_ab[rows, hh * tq + ncol:(hh + 1) * tq] = jnp.zeros((rc, tq - ncol), BF16)
                l_scr[...] = alpha * l_scr[...] + jnp.sum(psum[...], axis=-1, keepdims=True)
                alphas.append(alpha)
            alpha2[...] = jnp.where(head_a, alphas[0], alphas[1])
            zv = jnp.zeros_like(v2)
            head_k = _head_masks(v2.shape)
            vv = jnp.concatenate([jnp.where(head_k, v2, zv), jnp.where(head_k, zv, v2)], axis=0)
            acc[...] = acc[...] * alpha2[...] + lax.dot_general(p_ab[...], vv, _NN, preferred_element_type=F32)

        def unmasked(kb, carry):
            step(kb, False)
            return carry

        lax.fori_loop(first_ref[p * n_q + i], i, unmasked, 0)
        step(i, True)
        fq2 = fb_ref[...]
        o_ref[...] = acc[...] / jnp.where(head_a, l_a[...], l_b[...])
        g_ref[...] = fq2 - jnp.where(head_a, m_a[...] + jnp.log(l_a[...]), m_b[...] + jnp.log(l_b[...]))

    qblk = lambda off: pl.BlockSpec((tq, LANES), lambda p, i, first: (i, off + p))
    full = lambda off: pl.BlockSpec((s, LANES), lambda p, i, first: (0, off + p))
    col_scr = pltpu.VMEM((tq, 1), F32)
    grid_spec = pltpu.PrefetchScalarGridSpec(
        num_scalar_prefetch=1, grid=(N_PAIRS, n_q),
        in_specs=[qblk(0), full(N_PAIRS), full(2 * N_PAIRS), qblk(0),
                  pl.BlockSpec((n_q, N_HEADS, tq), lambda p, i, first: (0, 0, 0))],
        out_specs=[qblk(0), qblk(0)],
        scratch_shapes=[pltpu.VMEM((tq, tq), F32), pltpu.VMEM((tq, tq), F32), pltpu.VMEM((tq, 2 * tq), BF16),
                        col_scr, col_scr, col_scr, col_scr] + [pltpu.VMEM((tq, LANES), F32)] * 6)
    return pl.pallas_call(
        body, name="attn_fwd", grid_spec=grid_spec,
        out_shape=[jax.ShapeDtypeStruct((s, ATTN_W), F32), jax.ShapeDtypeStruct((s, ATTN_W), F32)],
        compiler_params=_cp(("parallel", "arbitrary")))(first_blk, qkv, qkv, qkv, fb, ft)


def _attn_bwd(qkv, do, g, delta, ft, first_blk):
    s = qkv.shape[0]
    tq = min(TQ, s)
    n_q = s // tq

    def body(first_ref, q_ref, do_ref, g_ref, dl_ref, k_ref, v_ref, ft_ref, dq_ref, dfq_ref, dk_ref, dv_ref,
             dfk_ref):
        p = pl.program_id(0)
        i = pl.program_id(1)

        @pl.when(i == 0)
        def _():
            dk_ref[...] = jnp.zeros_like(dk_ref)
            dv_ref[...] = jnp.zeros_like(dv_ref)
            dfk_ref[...] = jnp.zeros_like(dfk_ref)

        head_a = _head_masks((tq, LANES))
        q2 = q_ref[...] * Q_SCALE
        do2 = do_ref[...]
        zero = jnp.zeros_like(q2)
        q_a, q_b = jnp.where(head_a, q2, zero), jnp.where(head_a, zero, q2)
        do_a, do_b = jnp.where(head_a, do2, zero), jnp.where(head_a, zero, do2)
        g2, dl2 = g_ref[...], dl_ref[...]
        g_a, g_b = g2[:, 0:1], g2[:, HEAD_DIM:HEAD_DIM + 1]
        dl_a, dl_b = dl2[:, 0:1], dl2[:, HEAD_DIM:HEAD_DIM + 1]
        causal = lax.broadcasted_iota(jnp.int32, (tq, tq), 1) <= lax.broadcasted_iota(jnp.int32, (tq, tq), 0)
        rows8 = lax.broadcasted_iota(jnp.int32, (N_HEADS, tq), 0)

        def one_head(q_h, do_h, g_h, dl_h, fk_h, k2, v2, masked):
            sc = (lax.dot_general(q_h, k2, _NT, preferred_element_type=F32) + g_h) - fk_h
            pr = jnp.exp(sc)
            if masked:
                pr = jnp.where(causal, pr, 0.0)
            dp = lax.dot_general(do_h, v2, _NT, preferred_element_type=F32)
            ds = pr * (dp - dl_h)
            return (pr.astype(BF16), ds.astype(BF16), jnp.sum(ds, axis=0, keepdims=True),
                    jnp.sum(ds, axis=1, keepdims=True))

        def step(j, carry, masked):
            dq, r_a, r_b = carry
            rows = pl.ds(pl.multiple_of(j * tq, tq), tq)
            k2, v2 = k_ref[rows, :], v_ref[rows, :]
            k_a, k_b = jnp.where(head_a, k2, zero), jnp.where(head_a, zero, k2)
            ftv = ft_ref[j]
            p_a, ds_a, c_a, s_a = one_head(q_a, do_a, g_a, dl_a, _pick_row(ftv, 2 * p), k2, v2, masked)
            p_b, ds_b, c_b, s_b = one_head(q_b, do_b, g_b, dl_b, _pick_row(ftv, 2 * p + 1), k2, v2, masked)
            dv_ref[rows, :] += (lax.dot_general(p_a, do_a, _TN, preferred_element_type=F32)
                                + lax.dot_general(p_b, do_b, _TN, preferred_element_type=F32))
            dk_ref[rows, :] += (lax.dot_general(ds_a, q_a, _TN, preferred_element_type=F32)
                                + lax.dot_general(ds_b, q_b, _TN, preferred_element_type=F32))
            dfk_ref[0, j] += jnp.where(rows8 == 0, -c_a, jnp.where(rows8 == 1, -c_b, 0.0))
            dq = dq + (lax.dot_general(ds_a, k_a, _NN, preferred_element_type=F32)
                       + lax.dot_general(ds_b, k_b, _NN, preferred_element_type=F32))
            return dq, r_a + s_a, r_b + s_b

        zcol = jnp.zeros((tq, 1), F32)
        carry = lax.fori_loop(first_ref[p * n_q + i], i, lambda j, cr: step(j, cr, False),
                              (jnp.zeros((tq, LANES), F32), zcol, zcol))
        dq, r_a, r_b = step(i, carry, True)
        dq_ref[...] = (dq * Q_SCALE).astype(BF16)
        dfq_ref[...] = jnp.where(head_a, r_a, r_b)

    qblk = lambda off: pl.BlockSpec((tq, LANES), lambda p, i, first: (i, off + p))
    full = lambda off: pl.BlockSpec((s, LANES), lambda p, i, first: (0, off + p))
    grid_spec = pltpu.PrefetchScalarGridSpec(
        num_scalar_prefetch=1, grid=(N_PAIRS, n_q),
        in_specs=[qblk(0), qblk(0), qblk(0), qblk(0), full(N_PAIRS), full(2 * N_PAIRS),
                  pl.BlockSpec((n_q, N_HEADS, tq), lambda p, i, first: (0, 0, 0))],
        out_specs=[qblk(0), qblk(0), full(0), full(0),
                   pl.BlockSpec((1, n_q, N_HEADS, tq), lambda p, i, first: (p, 0, 0, 0))])
    return pl.pallas_call(
        body, name="attn_bwd", grid_spec=grid_spec,
        out_shape=[jax.ShapeDtypeStruct((s, ATTN_W), BF16), jax.ShapeDtypeStruct((s, ATTN_W), F32),
                   jax.ShapeDtypeStruct((s, ATTN_W), F32), jax.ShapeDtypeStruct((s, ATTN_W), F32),
                   jax.ShapeDtypeStruct((N_PAIRS, n_q, N_HEADS, tq), F32)],
        compiler_params=_cp(("parallel", "arbitrary")))(first_blk, qkv, do, g, delta, qkv, qkv, ft)


AUG = HEAD_DIM


def _split3(x):
    hi = x.astype(BF16).astype(F32)
    r = x - hi
    mid = r.astype(BF16).astype(F32)
    lo = (r - mid).astype(BF16).astype(F32)
    return hi, mid, lo


def _aug(base, lane, vals):
    out = jnp.where(lane < AUG, base, 0.0)
    for k, v in enumerate(vals):
        out = jnp.where(lane == AUG + k, v, out)
    return out


def _attn_prep(qkv, fb):
    s = qkv.shape[0]
    tq = min(TQ, s)
    n_q = s // tq

    def body(q_ref, k_ref, v_ref, fb_ref, qx_ref, kx_ref, kxt_ref, vx_ref, vt_ref, b_ref):
        lane = lax.broadcasted_iota(jnp.int32, (tq, LANES), 1)
        lane8 = lax.broadcasted_iota(jnp.int32, (HALO, LANES), 1)
        acc = jnp.zeros((HALO, LANES), F32)
        for h in range(N_HEADS):
            p, hh = divmod(h, 2)
            cols = slice(p * LANES, (p + 1) * LANES)

            def head(ref):
                x = ref[:, cols].astype(F32)
                return pltpu.roll(x, HEAD_DIM, 1) if hh else x

            q, k, v = head(q_ref) * Q_SCALE, head(k_ref), head(v_ref)
            f = fb_ref[:, h * HEAD_DIM:h * HEAD_DIM + 1]
            f3 = _split3(f)
            qx = _aug(q, lane, (-1.0, -1.0, -1.0) + f3)
            kx = _aug(k, lane, f3 + (1.0, 1.0, 1.0))
            vx = _aug(v, lane, (1.0, 1.0, 1.0))
            qx_ref[h] = qx.astype(BF16)
            kx_ref[h] = kx.astype(BF16)
            vx_ref[h] = vx.astype(BF16)
            kxt_ref[h, 0] = kx.T.astype(BF16)
            vt_ref[h, 0] = vx.T.astype(BF16)
            head_lanes = lane < HEAD_DIM
            qn = jnp.sqrt(jnp.sum(jnp.where(head_lanes, q * q, 0.0), axis=-1, keepdims=True))
            kn = jnp.sqrt(jnp.sum(jnp.where(head_lanes, k * k, 0.0), axis=-1, keepdims=True))
            vals = (jnp.max(qn, axis=0, keepdims=True), jnp.max(kn, axis=0, keepdims=True),
                    jnp.max(qn * kn + f, axis=0, keepdims=True), f[tq - 1:tq, :])
            for slot, val in enumerate(vals):
                acc = jnp.where(lane8 == slot * N_HEADS + h, val, acc)
        b_ref[0] = acc

    blk = lambda j: pl.BlockSpec((tq, ATTN_W), lambda i: (i, j))
    rows = pl.BlockSpec((N_HEADS, tq, LANES), lambda i: (0, i, 0))
    cols_t = pl.BlockSpec((N_HEADS, 1, LANES, tq), lambda i: (0, i, 0, 0))
    shp = jax.ShapeDtypeStruct((N_HEADS, s, LANES), BF16)
    shp_t = jax.ShapeDtypeStruct((N_HEADS, n_q, LANES, tq), BF16)
    return pl.pallas_call(
        body, name="attn_prep", grid=(n_q,), in_specs=[blk(0), blk(1), blk(2), blk(0)],
        out_specs=[rows, rows, cols_t, rows, cols_t, pl.BlockSpec((1, HALO, LANES), lambda i: (i, 0, 0))],
        out_shape=[shp, shp, shp_t, shp, shp_t, jax.ShapeDtypeStruct((n_q, HALO, LANES), F32)],
        compiler_params=_cp(("parallel",)))(qkv, qkv, qkv, fb)


def _key_block_ranges(bounds):
    t = bounds[:, 0, :]
    nh = N_HEADS
    a, b, c, e = t[:, 0:nh], t[:, nh:2 * nh], t[:, 2 * nh:3 * nh], t[:, 3 * nh:4 * nh]
    bound = a[:, None, :] * b[None, :, :] * 1.001 + c[:, None, :] - e[None, :, :]
    n_q = t.shape[0]
    idx = jnp.arange(n_q)
    need = jnp.logical_not(bound < -(EXP_ZERO + 2.0)) | (idx[None, :, None] >= idx[:, None, None])
    first = jnp.argmax(need, axis=1).astype(jnp.int32)
    first = jnp.min(first.reshape(n_q, N_PAIRS, 2), axis=-1)
    visits = (first[:, None, :] <= idx[None, :, None]) & (idx[None, :, None] <= idx[:, None, None])
    last = jnp.max(jnp.where(visits, idx[:, None, None], 0), axis=0).astype(jnp.int32)
    return first.T.reshape(-1), last.T.reshape(-1)


def _attn_fwd_t(qx, kx, vt, first_blk):
    _, s, _ = qx.shape
    tq = min(TQ, s)
    n_q = s // tq
    neg = -1e30

    def body(first_ref, qx_ref, kx_ref, vt_ref, o_ref, lse_ref, acc_ref, m_ref):
        p = pl.program_id(0)
        i = pl.program_id(1)
        acc_ref[...] = jnp.zeros(acc_ref.shape, F32)
        m_ref[...] = jnp.full(m_ref.shape, neg, F32)
        key_le_query = (lax.broadcasted_iota(jnp.int32, (tq, tq), 0) <= lax.broadcasted_iota(jnp.int32, (tq, tq), 1))

        def step(kb, masked):
            rows_k = pl.ds(pl.multiple_of(kb * tq, tq), tq)
            for hh in range(2):
                st = lax.dot_general(kx_ref[hh, rows_k, :], qx_ref[hh], _NT, preferred_element_type=F32)
                if masked:
                    st = jnp.where(key_le_query, st, -jnp.inf)
                m_old = m_ref[hh]
                m_new = jnp.maximum(m_old, jnp.max(st, axis=0, keepdims=True))
                m_ref[hh] = m_new
                pt = jnp.exp(st - m_new).astype(BF16)
                acc_ref[hh] = acc_ref[hh] * jnp.exp(m_old - m_new) + lax.dot_general(
                    vt_ref[hh, kb], pt, _NN, preferred_element_type=F32)

        def unmasked(kb, carry):
            step(kb, False)
            return carry

        lax.fori_loop(first_ref[p * n_q + i], i, unmasked, 0)
        step(i, True)
        outs, lses = [], []
        for hh in range(2):
            acc = acc_ref[hh]
            l = acc[AUG:AUG + 1, :]
            outs.append(acc[0:HEAD_DIM, :] / l)
            lses.append(m_ref[hh] + jnp.log(l))
        o_ref[...] = jnp.concatenate(outs, axis=0).T
        rows8 = lax.broadcasted_iota(jnp.int32, (N_HEADS, tq), 0)
        lse_ref[0, 0] = jnp.where(rows8 == 0, lses[0], jnp.where(rows8 == 1, lses[1], 0.0))

    grid_spec = pltpu.PrefetchScalarGridSpec(
        num_scalar_prefetch=1, grid=(N_PAIRS, n_q),
        in_specs=[pl.BlockSpec((2, tq, LANES), lambda p, i, first: (p, i, 0)),
                  pl.BlockSpec((2, s, LANES), lambda p, i, first: (p, 0, 0)),
                  pl.BlockSpec((2, n_q, LANES, tq), lambda p, i, first: (p, 0, 0, 0))],
        out_specs=[pl.BlockSpec((tq, LANES), lambda p, i, first: (i, p)),
                   pl.BlockSpec((1, 1, N_HEADS, tq), lambda p, i, first: (p, i, 0, 0))],
        scratch_shapes=[pltpu.VMEM((2, LANES, tq), F32), pltpu.VMEM((2, 1, tq), F32)])
    return pl.pallas_call(
        body, name="attn_fwd", grid_spec=grid_spec,
        out_shape=[jax.ShapeDtypeStruct((s, ATTN_W), F32), jax.ShapeDtypeStruct((N_PAIRS, n_q, N_HEADS, tq), F32)],
        compiler_params=_cp(("parallel", "arbitrary")))(first_blk, qx, kx, vt)


def _attn_bwd_t(qx, dox, kx, kxt, vx, lse, last_blk):
    _, s, _ = qx.shape
    tq = min(TQ, s)
    n_q = s // tq

    def body(last_ref, qx_ref, dox_ref, lse_ref, kx_ref, kxt_ref, vx_ref, dk_ref, dv_ref, aux_ref, dqt_ref):
        p = pl.program_id(0)
        j = pl.program_id(1)

        @pl.when(j == 0)
        def _():
            dqt_ref[...] = jnp.zeros(dqt_ref.shape, F32)

        key_le_query = (lax.broadcasted_iota(jnp.int32, (tq, tq), 0) <= lax.broadcasted_iota(jnp.int32, (tq, tq), 1))

        def step(i, carry, masked):
            rows_q = pl.ds(pl.multiple_of(i * tq, tq), tq)
            out = []
            for hh in range(2):
                dk, dv = carry[2 * hh], carry[2 * hh + 1]
                q, do = qx_ref[hh, rows_q, :], dox_ref[hh, rows_q, :]
                st = lax.dot_general(kx_ref[hh], q, _NT, preferred_element_type=F32)
                pt = jnp.exp(st - lse_ref[0, i, hh:hh + 1, :])
                if masked:
                    pt = jnp.where(key_le_query, pt, 0.0)
                dst = pt * lax.dot_general(vx_ref[hh], do, _NT, preferred_element_type=F32)
                pb, dsb = pt.astype(BF16), dst.astype(BF16)
                dv = dv + lax.dot_general(pb, do, _NN, preferred_element_type=F32)
                dk = dk + lax.dot_general(dsb, q, _NN, preferred_element_type=F32)
                dqt_ref[hh, i] += lax.dot_general(kxt_ref[hh, 0], dsb, _NN, preferred_element_type=F32)
                out += [dk, dv]
            return tuple(out)

        zero = jnp.zeros((tq, LANES), F32)
        carry = step(j, (zero, zero, zero, zero), True)
        dk_a, dv_a, dk_b, dv_b = lax.fori_loop(j + 1, last_ref[p * n_q + j] + 1,
                                               lambda i, cr: step(i, cr, False), carry)
        head_a = lax.broadcasted_iota(jnp.int32, (tq, LANES), 1) < HEAD_DIM
        dk_ref[...] = jnp.where(head_a, dk_a, pltpu.roll(dk_b, HEAD_DIM, 1)).astype(BF16)
        dv_ref[...] = jnp.where(head_a, dv_a, pltpu.roll(dv_b, HEAD_DIM, 1)).astype(BF16)
        aux_ref[...] = jnp.where(head_a, pltpu.roll(dk_a, HEAD_DIM, 1), dk_b)

    resident = pl.BlockSpec((2, s, LANES), lambda p, j, last: (p, 0, 0))
    key_rows = pl.BlockSpec((2, tq, LANES), lambda p, j, last: (p, j, 0))
    pair_out = pl.BlockSpec((tq, LANES), lambda p, j, last: (j, p))
    grid_spec = pltpu.PrefetchScalarGridSpec(
        num_scalar_prefetch=1, grid=(N_PAIRS, n_q),
        in_specs=[resident, resident, pl.BlockSpec((1, n_q, N_HEADS, tq), lambda p, j, last: (p, 0, 0, 0)),
                  key_rows, pl.BlockSpec((2, 1, LANES, tq), lambda p, j, last: (p, j, 0, 0)), key_rows],
        out_specs=[pair_out, pair_out, pair_out,
                   pl.BlockSpec((2, n_q, LANES, tq), lambda p, j, last: (p, 0, 0, 0))])
    return pl.pallas_call(
        body, name="attn_bwd", grid_spec=grid_spec,
        out_shape=[jax.ShapeDtypeStruct((s, ATTN_W), BF16), jax.ShapeDtypeStruct((s, ATTN_W), BF16),
                   jax.ShapeDtypeStruct((s, ATTN_W), F32), jax.ShapeDtypeStruct((N_HEADS, n_q, LANES, tq), F32)],
        compiler_params=_cp(("parallel", "arbitrary")))(last_blk, qx, dox, lse, kx, kxt, vx)


def _attn_dq_finish(dqt):
    _, n_q, _, tq = dqt.shape

    def body(dqt_ref, dq_ref, dfq_ref):
        a, b = dqt_ref[0, 0], dqt_ref[1, 0]
        dq_ref[...] = (jnp.concatenate([a[0:HEAD_DIM], b[0:HEAD_DIM]], axis=0).T * Q_SCALE).astype(BF16)
        rows8 = lax.broadcasted_iota(jnp.int32, (N_HEADS, tq), 0)
        dfq_ref[0, 0] = jnp.where(rows8 == 0, a[AUG + 3:AUG + 4], jnp.where(rows8 == 1, b[AUG + 3:AUG + 4], 0.0))

    return pl.pallas_call(
        body, name="attn_dq_finish", grid=(N_PAIRS, n_q),
        in_specs=[pl.BlockSpec((2, 1, LANES, tq), lambda p, i: (p, i, 0, 0))],
        out_specs=[pl.BlockSpec((tq, LANES), lambda p, i: (i, p)),
                   pl.BlockSpec((1, 1, N_HEADS, tq), lambda p, i: (p, i, 0, 0))],
        out_shape=[jax.ShapeDtypeStruct((n_q * tq, ATTN_W), BF16),
                   jax.ShapeDtypeStruct((N_PAIRS, n_q, N_HEADS, tq), F32)],
        compiler_params=_cp(("parallel", "parallel")))(dqt)


def _ffn_act_fwd(up, w_ffn):
    s = up.shape[0]
    tm, tn = min(TM_FFN, s), TN_FFN
    nb = D_FF // tn

    def body(a_ref, g_ref, ap_ref, gp_ref, wa_ref, wg_ref, act_ref):
        i = pl.program_id(1)

        def conv(blk_ref, prev_ref, w_ref):
            prev = jnp.where(i > 0, prev_ref[...], 0.0)
            return _conv_taps(jnp.concatenate([prev, blk_ref[...]], axis=0), w_ref[...])[HALO:]

        u_a, u_g = conv(a_ref, ap_ref, wa_ref), conv(g_ref, gp_ref, wg_ref)
        act_ref[...] = (u_g * (1.0 / (1.0 + jnp.exp(-u_g))) * u_a).astype(BF16)

    blk = lambda off: pl.BlockSpec((tm, tn), lambda n, i: (i, off + n))
    prev = lambda off: pl.BlockSpec((HALO, tn), lambda n, i: (jnp.maximum(i * (tm // HALO) - 1, 0), off + n))
    wsp = lambda off: pl.BlockSpec((3, tn), lambda n, i: (0, off + n))
    return pl.pallas_call(
        body, name="ffn_act_fwd", grid=(nb, s // tm),
        in_specs=[blk(0), blk(nb), prev(0), prev(nb), wsp(0), wsp(nb)],
        out_specs=pl.BlockSpec((tm, tn), lambda n, i: (i, n)),
        out_shape=jax.ShapeDtypeStruct((s, D_FF), BF16),
        compiler_params=_cp(("parallel", "parallel")))(up, up, up, up, w_ffn, w_ffn)


def _ffn_act_bwd(up, dact, w_ffn):
    s = up.shape[0]
    tm, tn = min(TM_FFN, s), TN_FFN
    nb = D_FF // tn
    n_blk = s // tm

    def body(a_ref, g_ref, ap_ref, gp_ref, an_ref, gn_ref, d_ref, dn_ref, wa_ref, wg_ref,
             dup_ref, gwa_ref, gwg_ref):
        i = pl.program_id(1)

        @pl.when(i == 0)
        def _():
            gwa_ref[...] = jnp.zeros_like(gwa_ref)
            gwg_ref[...] = jnp.zeros_like(gwg_ref)

        def ext(blk_ref, prev_ref, next_ref):
            return jnp.concatenate([jnp.where(i > 0, prev_ref[...], 0.0), blk_ref[...], next_ref[...]], axis=0)

        wa, wg = wa_ref[...], wg_ref[...]
        up_a, up_g = ext(a_ref, ap_ref, an_ref), ext(g_ref, gp_ref, gn_ref)
        u_a, u_g = _conv_taps(up_a, wa), _conv_taps(up_g, wg)
        d_e = jnp.concatenate([jnp.zeros((HALO, tn), F32), d_ref[...],
                               jnp.where(i < n_blk - 1, dn_ref[...], 0.0)], axis=0)
        sig = 1.0 / (1.0 + jnp.exp(-u_g))
        du_a = d_e * (u_g * sig)
        du_g = d_e * u_a * (sig * (1.0 + u_g * (1.0 - sig)))
        blk = slice(HALO, HALO + tm)
        dup_ref[0] = _conv_taps_t(du_a, wa)[blk].astype(BF16)
        dup_ref[1] = _conv_taps_t(du_g, wg)[blk].astype(BF16)
        for gw_ref, upv, du in ((gwa_ref, up_a[blk], du_a), (gwg_ref, up_g[blk], du_g)):
            gw_ref[0:1, :] += jnp.sum(upv * _shift_up(du, 2)[blk], axis=0, keepdims=True)
            gw_ref[1:2, :] += jnp.sum(upv * _shift_up(du, 1)[blk], axis=0, keepdims=True)
            gw_ref[2:3, :] += jnp.sum(upv * du[blk], axis=0, keepdims=True)

    blk = lambda off: pl.BlockSpec((tm, tn), lambda n, i: (i, off + n))
    prev = lambda off: pl.BlockSpec((HALO, tn), lambda n, i: (jnp.maximum(i * (tm // HALO) - 1, 0), off + n))
    nxt = lambda off: pl.BlockSpec(
        (HALO, tn), lambda n, i: (jnp.minimum((i + 1) * (tm // HALO), s // HALO - 1), off + n))
    wsp = lambda off: pl.BlockSpec((3, tn), lambda n, i: (0, off + n))
    return pl.pallas_call(
        body, name="ffn_act_bwd", grid=(nb, n_blk),
        in_specs=[blk(0), blk(nb), prev(0), prev(nb), nxt(0), nxt(nb), blk(0), nxt(0), wsp(0), wsp(nb)],
        out_specs=[pl.BlockSpec((2, tm, tn), lambda n, i: (0, i, n)), wsp(0), wsp(0)],
        out_shape=[jax.ShapeDtypeStruct((2, s, D_FF), BF16),
                   jax.ShapeDtypeStruct((3, D_FF), F32), jax.ShapeDtypeStruct((3, D_FF), F32)],
        compiler_params=_cp(("parallel", "arbitrary")))(up, up, up, up, up, up, dact, dact, w_ffn, w_ffn)


def _adamw(w, g, m, v, name):
    r, c = w.shape
    tr = next((t for t in (512, 352, 256, 128, 64, 32, 16, 8) if r > t and r % t == 0), r)

    def body(w_ref, g_ref, m_ref, v_ref, d_ref, nm_ref, nv_ref):
        gv = g_ref[...]
        m_new = ADAM_B1 * m_ref[...] + (1.0 - ADAM_B1) * gv
        v_new = ADAM_B2 * v_ref[...] + (1.0 - ADAM_B2) * (gv * gv)
        m_hat = m_new / (1.0 - ADAM_B1 ** ADAM_STEP)
        v_hat = v_new / (1.0 - ADAM_B2 ** ADAM_STEP)
        d_ref[...] = -ADAM_LR * (m_hat / (jnp.sqrt(v_hat) + ADAM_EPS) + ADAM_WD * w_ref[...])
        nm_ref[...] = m_new
        nv_ref[...] = v_new

    spec = pl.BlockSpec((tr, c), lambda i: (i, 0))
    shp = jax.ShapeDtypeStruct((r, c), F32)
    return pl.pallas_call(
        body, name=name, grid=(r // tr,), in_specs=[spec] * 4, out_specs=[spec] * 3, out_shape=[shp] * 3,
        compiler_params=_cp(("parallel",)))(w, g, m, v)


def _sum_rows_block(h):
    return h if h <= 352 else 256


def _pair_sum(view, recv, sel, name):
    n, _, h, c = view.shape
    tr = _sum_rows_block(h)

    def body(sel_ref, a_ref, b_ref, o_ref, ob_ref):
        t = a_ref[...] + b_ref[...]
        o_ref[...] = t
        ob_ref[...] = t.astype(BF16)

    blk = pl.BlockSpec((None, tr, c), lambda j, i, sel_ref: (j, i, 0))
    grid_spec = pltpu.PrefetchScalarGridSpec(
        num_scalar_prefetch=1, grid=(n, h // tr),
        in_specs=[pl.BlockSpec((None, None, tr, c), lambda j, i, sel_ref: (j, sel_ref[0], i, 0)),
                  pl.BlockSpec((None, None, tr, c), lambda j, i, sel_ref: (j, 0, i, 0))],
        out_specs=[blk, blk])
    return pl.pallas_call(
        body, name=name, grid_spec=grid_spec,
        out_shape=[jax.ShapeDtypeStruct((n, h, c), F32), jax.ShapeDtypeStruct((n, h, c), BF16)],
        compiler_params=_cp(("parallel", "parallel")))(sel, view, recv)


def _chip_sum(pair, got, sel, name):
    _, h, c = pair.shape
    tr = _sum_rows_block(h)
    nblk = h // tr

    def body(sel_ref, p_ref, g0_ref, g1_ref, g2_ref, o_ref):
        o_ref[...] = ((p_ref[...] + g0_ref[...].astype(F32)) + g1_ref[...].astype(F32)) + g2_ref[...].astype(F32)

    slot = lambda k: pl.BlockSpec((None, tr, c), lambda i, sel_ref: (k, i, 0))
    grid_spec = pltpu.PrefetchScalarGridSpec(
        num_scalar_prefetch=1, grid=(h // tr,),
        in_specs=[pl.BlockSpec((None, tr, c), lambda i, sel_ref: (sel_ref[1], i, 0)), slot(0), slot(1), slot(2)],
        out_specs=pl.BlockSpec((tr, c), lambda i, sel_ref: (sel_ref[0] * nblk + i, 0)))
    return pl.pallas_call(
        body, name=name, grid_spec=grid_spec, out_shape=jax.ShapeDtypeStruct((2 * h, c), F32),
        compiler_params=_cp(("parallel",)))(sel, pair, got, got, got)


def _place():
    return lax.axis_index("x"), lax.axis_index("y"), lax.axis_index("c")


def _other_chips(x, y):
    return [(1 - x, y), (x, 1 - y), (1 - x, 1 - y)]


def _hbm_specs(n):
    return [pl.BlockSpec(memory_space=pl.ANY)] * n


def _all_gather_weights(bigs, smalls):
    nb, ns = len(bigs), len(smalls)
    n = nb + ns

    def body(*refs):
        ins, outs = refs[:n], refs[2 * n:3 * n]
        send_sems, recv_sems = refs[3 * n:]
        x, y, c = _place()
        my_chip = 2 * x + y
        chips = _other_chips(x, y)
        sibling = (x, y, 1 - c)

        def rows(k, which):
            h = ins[k].shape[0] // 2
            return pl.ds(which * h, h)

        def copy(sem, src, dst, to):
            return pltpu.make_async_remote_copy(src_ref=src, dst_ref=dst, send_sem=send_sems.at[sem],
                                                recv_sem=recv_sems.at[sem], device_id=to, device_id_type=MESH)

        sent = []
        for k in range(nb):
            for j, (cx, cy) in enumerate(chips):
                sent.append(copy(6 * k + j, ins[k].at[rows(k, c)], outs[k].at[my_chip, rows(k, c)], (cx, cy, c)))
        for k in range(nb, n):
            for j, (cx, cy) in enumerate(chips):
                sent.append(copy(6 * nb + 3 * (k - nb) + j, ins[k], outs[k].at[my_chip], (cx, cy, c)))
        for cp in sent:
            cp.start()
        for j, (cx, cy) in enumerate(chips):
            for k in range(nb):
                landed = outs[k].at[2 * cx + cy, rows(k, c)]
                copy(6 * k + j, landed, landed, (x, y, c)).wait_recv()
                fwd = copy(6 * k + 3 + j, landed, landed, sibling)
                fwd.start()
                sent.append(fwd)
        for j, (cx, cy) in enumerate(chips):
            for k in range(nb):
                landed = outs[k].at[2 * cx + cy, rows(k, 1 - c)]
                copy(6 * k + 3 + j, landed, landed, (x, y, c)).wait_recv()
            for k in range(nb, n):
                landed = outs[k].at[2 * cx + cy]
                copy(6 * nb + 3 * (k - nb) + j, landed, landed, (x, y, c)).wait_recv()
        for cp in sent:
            cp.wait_send()

    x, y, _ = _place()
    arrays = list(bigs) + list(smalls)
    landing = [lax.dynamic_update_index_in_dim(lax.empty((N_CHIPS,) + a.shape, a.dtype), a, 2 * x + y, 0)
               for a in arrays]
    n_sems = 6 * nb + 3 * ns
    return pl.pallas_call(
        body, name="all_gather_weights",
        out_shape=[jax.ShapeDtypeStruct(b.shape, b.dtype) for b in landing],
        in_specs=_hbm_specs(2 * n), out_specs=_hbm_specs(n), input_output_aliases={n + k: k for k in range(n)},
        scratch_shapes=[pltpu.SemaphoreType.DMA((n_sems,)), pltpu.SemaphoreType.DMA((n_sems,))])(*arrays, *landing)


def _pair_exchange(views):
    n = len(views)

    def body(*refs):
        ins, outs, send_sems, recv_sems = refs[:n], refs[n:2 * n], refs[2 * n], refs[2 * n + 1]
        x, y, c = _place()
        copies = [pltpu.make_async_remote_copy(
            src_ref=ins[k].at[:, pl.ds(1 - c, 1)], dst_ref=outs[k], send_sem=send_sems.at[k],
            recv_sem=recv_sems.at[k], device_id=(x, y, 1 - c), device_id_type=MESH) for k in range(n)]
        for cp in copies:
            cp.start()
        for cp in copies:
            cp.wait()

    return pl.pallas_call(
        body, name="pair_exchange",
        out_shape=[jax.ShapeDtypeStruct((v.shape[0], 1) + v.shape[2:], v.dtype) for v in views],
        in_specs=_hbm_specs(n), out_specs=_hbm_specs(n),
        scratch_shapes=[pltpu.SemaphoreType.DMA((n,)), pltpu.SemaphoreType.DMA((n,))])(*views)


def _scatter_to_chips(parts):
    n = len(parts)

    def body(*refs):
        ins, outs, send_sems, recv_sems = refs[:n], refs[n:2 * n], refs[2 * n], refs[2 * n + 1]
        x, y, c = _place()
        copies = [pltpu.make_async_remote_copy(
            src_ref=ins[k].at[pl.ds(2 * cx + cy, 1)], dst_ref=outs[k].at[pl.ds(r, 1)], send_sem=send_sems.at[3 * k + r],
            recv_sem=recv_sems.at[3 * k + r], device_id=(cx, cy, c), device_id_type=MESH)
            for k in range(n) for r, (cx, cy) in enumerate(_other_chips(x, y))]
        for cp in copies:
            cp.start()
        for cp in copies:
            cp.wait()

    return pl.pallas_call(
        body, name="scatter_grads", out_shape=[jax.ShapeDtypeStruct((3,) + p.shape[1:], p.dtype) for p in parts],
        in_specs=_hbm_specs(n), out_specs=_hbm_specs(n),
        scratch_shapes=[pltpu.SemaphoreType.DMA((3 * n,)), pltpu.SemaphoreType.DMA((3 * n,))])(*parts)


def _join_halves(shards):
    n = len(shards)

    def body(*refs):
        ins, outs, send_sems, recv_sems = refs[:n], refs[n:2 * n], refs[2 * n], refs[2 * n + 1]
        x, y, c = _place()

        def rows(ref, which):
            h = ref.shape[0] // 2
            return ref.at[pl.ds(which * h, h)]

        sent = [pltpu.make_async_remote_copy(
            src_ref=rows(ins[k], c), dst_ref=rows(outs[k], c), send_sem=send_sems.at[k], recv_sem=recv_sems.at[k],
            device_id=(x, y, 1 - c), device_id_type=MESH) for k in range(n)]
        for cp in sent:
            cp.start()
        for k in range(n):
            pltpu.make_async_remote_copy(
                src_ref=rows(ins[k], 1 - c), dst_ref=rows(outs[k], 1 - c), send_sem=send_sems.at[k],
                recv_sem=recv_sems.at[k], device_id=(x, y, 1 - c), device_id_type=MESH).wait_recv()
        for cp in sent:
            cp.wait_send()

    return pl.pallas_call(
        body, name="half_exchange", out_shape=[jax.ShapeDtypeStruct(a.shape, a.dtype) for a in shards],
        in_specs=_hbm_specs(n), out_specs=_hbm_specs(n), input_output_aliases={k: k for k in range(n)},
        scratch_shapes=[pltpu.SemaphoreType.DMA((n,)), pltpu.SemaphoreType.DMA((n,))])(*shards)


def _all_reduce_small(packet):
    rows, width = packet.shape
    n_dev = 8

    def body(x_ref, out_ref, gath, send_sems, recv_sems):
        x, y, c = _place()
        me, sibling = (x, y, c), (x, y, 1 - c)
        chips = _other_chips(x, y)

        def slot(px, py, pc):
            return gath.at[pl.ds((4 * px + 2 * py + pc) * rows, rows), :]

        def copy(k, block, to, src=None):
            return pltpu.make_async_remote_copy(
                src_ref=slot(*block) if src is None else src, dst_ref=slot(*block), send_sem=send_sems.at[k],
                recv_sem=recv_sems.at[k], device_id=to, device_id_type=MESH)

        first = [copy(0, me, sibling, src=x_ref)]
        first += [copy(1 + j, me, (*chip, c), src=x_ref) for j, chip in enumerate(chips)]
        for cp in first:
            cp.start()
        gath[pl.ds((4 * x + 2 * y + c) * rows, rows), :] = x_ref[...]
        passed = [copy(4 + j, (*chip, c), sibling) for j, chip in enumerate(chips)]
        for j, chip in enumerate(chips):
            copy(1 + j, (*chip, c), me).wait_recv()
            passed[j].start()
        copy(0, sibling, me).wait_recv()
        for j, chip in enumerate(chips):
            copy(4 + j, (*chip, 1 - c), me).wait_recv()
        for cp in first + passed:
            cp.wait_send()
        acc = gath[0:rows, :]
        for d in range(1, n_dev):
            acc = acc + gath[d * rows:(d + 1) * rows, :]
        out_ref[...] = acc

    return pl.pallas_call(
        body, name="all_reduce_small", out_shape=jax.ShapeDtypeStruct((rows, width), F32),
        in_specs=[pl.BlockSpec(memory_space=pltpu.VMEM)], out_specs=pl.BlockSpec(memory_space=pltpu.VMEM),
        scratch_shapes=[pltpu.VMEM((n_dev * rows, width), F32), pltpu.SemaphoreType.DMA((7,)),
                        pltpu.SemaphoreType.DMA((7,))])(packet)


def _flat_rows(parts, width, row_multiple):
    flat = jnp.concatenate([p.astype(F32).reshape(-1) for p in parts])
    rows = -(-flat.shape[0] // width)
    rows = -(-rows // row_multiple) * row_multiple
    return jnp.pad(flat, (0, rows * width - flat.shape[0])).reshape(rows, width)


def _unflatten(flat2d, shapes):
    flat = flat2d.reshape(-1)
    out, off = [], 0
    for shp in shapes:
        n = 1
        for dim in shp:
            n *= dim
        out.append(flat[off:off + n].reshape(shp))
        off += n
    return out


def _reduce_scatter_grads(chip_major, names):
    x, y, c = _place()
    sel = jnp.stack([c, 2 * x + y]).astype(jnp.int32)
    views = [g.reshape(N_CHIPS, 2, g.shape[1] // 2, g.shape[2]) for g in chip_major]
    recv = _pair_exchange(views)
    pairs = [_pair_sum(v, r, sel, "pair_sum_" + nm) for v, r, nm in zip(views, recv, names)]
    got = _scatter_to_chips([pb for _, pb in pairs])
    return _join_halves([_chip_sum(p, g, sel, "chip_sum_" + nm) for (p, _), g, nm in zip(pairs, got, names)])


def kernel(x, g_mix, w_in, b_f, w_conv, g_conv_out, g_attn_out, w_o, g_ffn, w_up, w_ffn_conv, w_down, g_final, loss_target, m_g_mix, m_w_in, m_b_f, m_w_conv, m_g_conv_out, m_g_attn_out, m_w_o, m_g_ffn, m_w_up, m_w_ffn_conv, m_w_down, m_g_final, v_g_mix, v_w_in, v_b_f, v_w_conv, v_g_conv_out, v_g_attn_out, v_w_o, v_g_ffn, v_w_up, v_w_ffn_conv, v_w_down, v_g_final):
    s = x.shape[1]
    x0 = x[0]
    target = loss_target[0]
    d = D_MODEL
    x_pos, y_pos, _ = _place()
    my_chip = 2 * x_pos + y_pos

    c_in, c_o, c_up, c_down, c_conv, c_ffn = _all_gather_weights(
        [w_in[0].astype(BF16), w_o[0].astype(BF16), w_up[0].astype(BF16), w_down[0].astype(BF16)],
        [w_conv[0], w_ffn_conv[0]])
    w_in_full = jnp.concatenate([c_in[j] for j in range(N_CHIPS)], axis=1)
    w_o_full = c_o.reshape(d, d)
    w_down_full = c_down.reshape(D_FF, d)
    w_conv_full = jnp.concatenate([c_conv[j] for j in range(N_CHIPS)], axis=1)
    w_ffn_full = jnp.concatenate([c_ffn[j] for j in range(N_CHIPS)], axis=1)
    c3 = 3 * CONV_CH
    w_a, w_b = w_in_full[:, :c3], w_in_full[:, c3:c3 + 3 * ATTN_W]
    w_c = jnp.pad(w_in_full[:, c3 + 3 * ATTN_W:], ((0, 0), (0, LANES - N_HEADS)))
    w_q, w_k, w_v = (w_b[:, i * ATTN_W:(i + 1) * ATTN_W] for i in range(3))
    n_up = c_up.shape[2]
    b_pad = jnp.pad(b_f, ((0, 0), (0, LANES - N_HEADS)))

    h1 = _rms_fwd(x0, g_mix, "rms_mix")
    z_a = _mm("nn", [h1], [w_a], F32, TM_MM, 512, "in_proj_conv")
    qkv = _mm("nn", [h1], [w_b], BF16, TM_MM, 512, "in_proj_qkv")
    f_log = _mm("nn", [h1], [w_c], F32, TM_MM, LANES, "in_proj_gate")
    fb = _gate_fwd(f_log, b_pad)
    qx, kx, kxt, vx, vt, bounds = _attn_prep(qkv, fb)
    first_blk, last_blk = _key_block_ranges(bounds)
    o_attn, lse = _attn_fwd_t(qx, kx, vt, first_blk)
    mix = _mixer_fwd(z_a, o_attn, w_conv_full, g_conv_out, g_attn_out)
    x2 = _mm("nn", [mix], [w_o_full], F32, TM_MM, 512, "out_proj", add=x0)
    h2 = _rms_fwd(x2, g_ffn, "rms_ffn")
    up = _mm("nn", [h2], [c_up], F32, TM_MM, n_up, "up_proj", b_chips=True)
    act = _ffn_act_fwd(up, w_ffn_full)
    x3 = _mm("nn", [act], [w_down_full], F32, 512, 512, "down_proj", add=x2)

    dx3, dx3_b, loss_row, gg_final = _loss_head(x3, target, g_final.reshape(1, d))
    dact = _mm("nt", [dx3_b], [w_down_full], F32, TM_MM, 1408, "d_act")
    gw_down = _mm_tn(act, dx3_b, 1408, 1024, "gw_down")
    dup, gwf_lin, gwf_gate = _ffn_act_bwd(up, dact, w_ffn_full)
    dh2 = _mm("nt", [(dup, j // 2, j % 2, n_up) for j in range(N_CHIPS)], [(c_up, j) for j in range(N_CHIPS)],
              F32, 512, 512, "d_h2")
    gw_up = _mm_tn(h2, dup, 1024, n_up, "gw_up", out_chips=True)
    dx2, dx2_b, gg_ffn = _rms_bwd(x2, dh2, g_ffn, dx3, "rms_ffn_bwd", True)
    dmix = _mm("nt", [dx2_b], [w_o_full], F32, TM_MM, 512, "d_mix")
    gw_o = _mm_tn(mix, dx2_b, 1024, 1024, "gw_o")
    dz_a, dox, gw_conv, gg_conv_out, gg_attn_out = _mixer_bwd(z_a, o_attn, dmix, w_conv_full, g_conv_out, g_attn_out)
    dk, dv, aux, dqt = _attn_bwd_t(qx, dox, kx, kxt, vx, lse, last_blk)
    dq, dfq = _attn_dq_finish(dqt)
    d_f = aux[:, ::HEAD_DIM] + jnp.transpose(dfq[:, :, 0:2, :], (1, 3, 0, 2)).reshape(s, N_HEADS)
    df_b, gb_f = _gate_bwd(f_log, b_pad, jnp.pad(d_f, ((0, 0), (0, LANES - N_HEADS))))
    dh1 = _mm("nt", [dz_a, dq, dk, dv, df_b], [w_a, w_q, w_k, w_v, w_c], F32, TM_MM, 512, "d_h1")
    gw_a = _mm_tn(h1, dz_a, 1024, c3, "gw_in_conv")
    gw_q = _mm_tn(h1, dq, 1024, ATTN_W, "gw_in_q")
    gw_k = _mm_tn(h1, dk, 1024, ATTN_W, "gw_in_k")
    gw_v = _mm_tn(h1, dv, 1024, ATTN_W, "gw_in_v")
    gw_c = _mm_tn(h1, df_b, 1024, LANES, "gw_in_gate")
    grad_x, gg_mix = _rms_bwd(x0, dh1, g_mix, dx2, "rms_mix_bwd", False)

    gw_in = jnp.concatenate([gw_a, gw_q, gw_k, gw_v, gw_c[:, :N_HEADS]], axis=1)
    n_in = IN_COLS // N_CHIPS
    gw_in = jnp.stack([gw_in[:, j * n_in:(j + 1) * n_in] for j in range(N_CHIPS)])
    g_w_in, g_w_o, g_w_up, g_w_down = _reduce_scatter_grads(
        [gw_in, gw_o.reshape(N_CHIPS, d // N_CHIPS, d), gw_up, gw_down.reshape(N_CHIPS, D_FF // N_CHIPS, d)],
        ["w_in", "w_o", "w_up", "w_down"])

    gw_ffn = jnp.concatenate([gwf_lin, gwf_gate], axis=1)
    small_parts = [gg_mix, gg_conv_out, gg_attn_out, gg_ffn, gg_final, gb_f[:, :N_HEADS], loss_row[:, 0:1], gw_conv,
                   gw_ffn]
    small_shapes = [a.shape for a in small_parts]
    tot = _unflatten(_all_reduce_small(_flat_rows(small_parts, d, 8)), small_shapes)
    g_g_mix, g_g_conv_out, g_g_attn_out, g_g_ffn, g_g_final, g_b_f, loss_sum, g_conv_full, g_ffn_full = tot
    loss = loss_sum[0, 0]
    g_g_final = g_g_final[0]
    g_w_conv = lax.dynamic_slice_in_dim(g_conv_full, my_chip * (CONV_CH // N_CHIPS), CONV_CH // N_CHIPS, axis=1)
    g_w_ffn = lax.dynamic_slice_in_dim(g_ffn_full, my_chip * n_up, n_up, axis=1)

    def adam_big(w, g, m, v, name):
        dl, nm, nv = _adamw(w[0], g, m[0], v[0], name)
        return dl[None], nm[None], nv[None]

    u_w_in = adam_big(w_in, g_w_in, m_w_in, v_w_in, "adam_w_in")
    u_w_o = adam_big(w_o, g_w_o, m_w_o, v_w_o, "adam_w_o")
    u_w_up = adam_big(w_up, g_w_up, m_w_up, v_w_up, "adam_w_up")
    u_w_down = adam_big(w_down, g_w_down, m_w_down, v_w_down, "adam_w_down")

    small_w = [g_mix, b_f, g_conv_out, g_attn_out, g_ffn, g_final, w_conv, w_ffn_conv]
    small_g = [g_g_mix, g_b_f, g_g_conv_out, g_g_attn_out, g_g_ffn, g_g_final, g_w_conv, g_w_ffn]
    small_m = [m_g_mix, m_b_f, m_g_conv_out, m_g_attn_out, m_g_ffn, m_g_final, m_w_conv, m_w_ffn_conv]
    small_v = [v_g_mix, v_b_f, v_g_conv_out, v_g_attn_out, v_g_ffn, v_g_final, v_w_conv, v_w_ffn_conv]
    shapes = [a.shape for a in small_w]
    pack = lambda arrs: _flat_rows(arrs, LANES, 8)
    sd, sm, sv = _adamw(pack(small_w), pack(small_g), pack(small_m), pack(small_v), "adam_small")
    sd, sm, sv = _unflatten(sd, shapes), _unflatten(sm, shapes), _unflatten(sv, shapes)
    (d_g_mix, d_b_f, d_g_conv_out, d_g_attn_out, d_g_ffn, d_g_final, d_w_conv, d_w_ffn) = sd
    (nm_g_mix, nm_b_f, nm_g_conv_out, nm_g_attn_out, nm_g_ffn, nm_g_final, nm_w_conv, nm_w_ffn) = sm
    (nv_g_mix, nv_b_f, nv_g_conv_out, nv_g_attn_out, nv_g_ffn, nv_g_final, nv_w_conv, nv_w_ffn) = sv

    grads = (g_g_mix, g_w_in[None], g_b_f, g_w_conv[None], g_g_conv_out, g_g_attn_out, g_w_o[None], g_g_ffn,
             g_w_up[None], g_w_ffn[None], g_w_down[None], g_g_final)
    deltas = (d_g_mix, u_w_in[0], d_b_f, d_w_conv, d_g_conv_out, d_g_attn_out, u_w_o[0], d_g_ffn, u_w_up[0],
              d_w_ffn, u_w_down[0], d_g_final)
    new_m = (nm_g_mix, u_w_in[1], nm_b_f, nm_w_conv, nm_g_conv_out, nm_g_attn_out, u_w_o[1], nm_g_ffn, u_w_up[1],
             nm_w_ffn, u_w_down[1], nm_g_final)
    new_v = (nv_g_mix, u_w_in[2], nv_b_f, nv_w_conv, nv_g_conv_out, nv_g_attn_out, u_w_o[2], nv_g_ffn, u_w_up[2],
             nv_w_ffn, u_w_down[2], nv_g_final)
    return (loss, grad_x[None], *grads, *deltas, *new_m, *new_v)
```

```python
import functools

import jax
import jax.numpy as jnp
from jax import lax
from jax.experimental import pallas as pl
from jax.experimental.pallas import tpu as pltpu

F32, BF16 = jnp.float32, jnp.bfloat16
MESH = pl.DeviceIdType.MESH

D_MODEL = 1024
CONV_CH = 512
ATTN_W = 512
N_HEADS = 8
HEAD_DIM = 64
N_PAIRS = N_HEADS // 2
D_FF = 2816
IN_COLS = 3 * CONV_CH + 3 * ATTN_W + N_HEADS
EPS = 1e-6
Q_SCALE = 0.125
EXP_ZERO = 104.0
N_CHIPS = 4
LANES = 128
HALO = 8

ADAM_LR, ADAM_B1, ADAM_B2, ADAM_EPS, ADAM_WD, ADAM_STEP = 0.001, 0.9, 0.999, 1e-08, 0.01, 10

TM_ROWS = 512
TM_MM = 1024
TK_TN = 512
TQ = 512
ROW_CHUNK = 32
TM_FFN = 1024
TN_FFN = 256
VMEM_LIMIT = 52 * 2**20


def _cp(sem, vmem=VMEM_LIMIT):
    return pltpu.CompilerParams(dimension_semantics=sem, vmem_limit_bytes=vmem)


def _bf(a):
    return a if a.dtype == BF16 else a.astype(BF16)


def _mm(mode, a_list, b_list, out_dtype, tm, tn, name, add=None, b_chips=False):
    n_p = len(a_list)
    a0 = a_list[0]
    m_dim = a0[0].shape[1] if isinstance(a0, tuple) else a0.shape[0]
    b0 = b_list[0]
    if b_chips:
        n_dim = b0.shape[0] * b0.shape[2]
        assert tn == b0.shape[2] and mode == "nn"
    else:
        b0 = b0[0][b0[1]] if isinstance(b0, tuple) else b0
        n_dim = b0.shape[1 if mode == "nn" else 0]
    tm, tn = min(tm, m_dim), min(tn, n_dim)
    assert m_dim % tm == 0 and n_dim % tn == 0
    dims = (((1,), (0,)), ((), ())) if mode == "nn" else (((1,), (1,)), ((), ()))
    in_specs, args = [], []
    for a in a_list:
        if isinstance(a, tuple):
            arr, lead, col, width = a
            in_specs.append(pl.BlockSpec((None, tm, width), lambda m, n, lead=lead, col=col: (lead, m, col)))
        else:
            arr = a
            in_specs.append(pl.BlockSpec((tm, a.shape[1]), lambda m, n: (m, 0)))
        args.append(arr)
    for b in b_list:
        if b_chips:
            arr = b
            in_specs.append(pl.BlockSpec((None, b.shape[1], tn), lambda m, n: (n, 0, 0)))
        elif isinstance(b, tuple):
            arr, lead = b
            if mode == "nn":
                in_specs.append(pl.BlockSpec((None, arr.shape[1], tn), lambda m, n, lead=lead: (lead, 0, n)))
            else:
                in_specs.append(pl.BlockSpec((None, tn, arr.shape[2]), lambda m, n, lead=lead: (lead, n, 0)))
        elif mode == "nn":
            arr = b
            in_specs.append(pl.BlockSpec((b.shape[0], tn), lambda m, n: (0, n)))
        else:
            arr = b
            in_specs.append(pl.BlockSpec((tn, b.shape[1]), lambda m, n: (n, 0)))
        args.append(arr)
    if add is not None:
        in_specs.append(pl.BlockSpec((tm, tn), lambda m, n: (m, n)))
        args.append(add)

    def body(*refs):
        o_ref = refs[-1]
        acc = None
        for i in range(n_p):
            d = lax.dot_general(_bf(refs[i][...]), _bf(refs[n_p + i][...]), dims,
                                preferred_element_type=F32)
            acc = d if acc is None else acc + d
        if add is not None:
            acc = refs[2 * n_p][...] + acc
        o_ref[...] = acc.astype(out_dtype)

    return pl.pallas_call(
        body, name=name, grid=(m_dim // tm, n_dim // tn), in_specs=in_specs,
        out_specs=pl.BlockSpec((tm, tn), lambda m, n: (m, n)),
        out_shape=jax.ShapeDtypeStruct((m_dim, n_dim), out_dtype),
        compiler_params=_cp(("parallel", "parallel")))(*args)


def _mm_tn(a, b, tm, tn, name, out_chips=False):
    k_dim, m_dim = a.shape
    n_dim = b.shape[-1] * (b.shape[0] if b.ndim == 3 else 1)
    tm, tn, tk = min(tm, m_dim), min(tn, b.shape[-1]), min(TK_TN, k_dim)
    assert m_dim % tm == 0 and b.shape[-1] % tn == 0 and k_dim % tk == 0
    per = b.shape[-1] // tn
    if b.ndim == 3:
        b_spec = pl.BlockSpec((None, tk, tn), lambda m, n, k: (n // per, k, n % per))
    else:
        b_spec = pl.BlockSpec((tk, tn), lambda m, n, k: (k, n))

    def body(a_ref, b_ref, o_ref):
        @pl.when(pl.program_id(2) == 0)
        def _():
            o_ref[...] = jnp.zeros_like(o_ref)
        o_ref[...] += lax.dot_general(_bf(a_ref[...]), _bf(b_ref[...]), (((0,), (0,)), ((), ())),
                                      preferred_element_type=F32)

    return pl.pallas_call(
        body, name=name, grid=(m_dim // tm, n_dim // tn, k_dim // tk),
        in_specs=[pl.BlockSpec((tk, tm), lambda m, n, k: (k, m)), b_spec],
        out_specs=(pl.BlockSpec((None, tm, tn), lambda m, n, k: (n, m, 0)) if out_chips
                   else pl.BlockSpec((tm, tn), lambda m, n, k: (m, n))),
        out_shape=jax.ShapeDtypeStruct((n_dim // tn, m_dim, tn) if out_chips else (m_dim, n_dim), F32),
        compiler_params=_cp(("parallel", "parallel", "arbitrary")))(a, b)


def _rstd(x):
    return lax.rsqrt(jnp.mean(x * x, axis=-1, keepdims=True) + EPS)


def _rms_fwd(x, g, name):
    s, d = x.shape
    tm = min(TM_ROWS, s)

    def body(x_ref, g_ref, h_ref):
        xv = x_ref[...]
        h_ref[...] = (xv * _rstd(xv) * g_ref[...]).astype(BF16)

    return pl.pallas_call(
        body, name=name, grid=(s // tm,),
        in_specs=[pl.BlockSpec((tm, d), lambda i: (i, 0)), pl.BlockSpec((1, d), lambda i: (0, 0))],
        out_specs=pl.BlockSpec((tm, d), lambda i: (i, 0)),
        out_shape=jax.ShapeDtypeStruct((s, d), BF16), compiler_params=_cp(("parallel",)))(x, g)


def _rms_bwd(x, dh, g, dres, name, with_bf16):
    s, d = x.shape
    tm = min(TM_ROWS, s)

    def body(x_ref, dh_ref, g_ref, dres_ref, dx_ref, *rest):
        gg_ref = rest[-1]

        @pl.when(pl.program_id(0) == 0)
        def _():
            gg_ref[...] = jnp.zeros_like(gg_ref)

        xv = x_ref[...]
        xn = xv * _rstd(xv)
        dhv = dh_ref[...]
        gg_ref[...] += jnp.sum(dhv * xn, axis=0, keepdims=True)
        t = dhv * g_ref[...]
        dx = dres_ref[...] + _rstd(xv) * (t - xn * jnp.mean(t * xn, axis=-1, keepdims=True))
        dx_ref[...] = dx
        if with_bf16:
            rest[0][...] = dx.astype(BF16)

    row = pl.BlockSpec((tm, d), lambda i: (i, 0))
    vec = pl.BlockSpec((1, d), lambda i: (0, 0))
    out_specs = [row] + ([row] if with_bf16 else []) + [vec]
    out_shape = ([jax.ShapeDtypeStruct((s, d), F32)] + ([jax.ShapeDtypeStruct((s, d), BF16)] if with_bf16 else [])
                 + [jax.ShapeDtypeStruct((1, d), F32)])
    return pl.pallas_call(
        body, name=name, grid=(s // tm,), in_specs=[row, row, vec, row], out_specs=out_specs, out_shape=out_shape,
        compiler_params=_cp(("arbitrary",)))(x, dh, g, dres)


def _loss_head(x3, target, g):
    s, d = x3.shape
    tm = min(TM_ROWS, s)

    def body(x_ref, t_ref, g_ref, dx_ref, dxb_ref, loss_ref, gg_ref):
        @pl.when(pl.program_id(0) == 0)
        def _():
            gg_ref[...] = jnp.zeros_like(gg_ref)
            loss_ref[...] = jnp.zeros_like(loss_ref)

        xv = x_ref[...]
        r = _rstd(xv)
        xn = xv * r
        gv = g_ref[...]
        err = xn * gv - t_ref[...]
        loss_ref[...] += 0.5 * jnp.sum(jnp.mean(err * err, axis=-1, keepdims=True), axis=0, keepdims=True)
        dy = err * (1.0 / d)
        gg_ref[...] += jnp.sum(dy * xn, axis=0, keepdims=True)
        t = dy * gv
        dx = r * (t - xn * jnp.mean(t * xn, axis=-1, keepdims=True))
        dx_ref[...] = dx
        dxb_ref[...] = dx.astype(BF16)

    row = pl.BlockSpec((tm, d), lambda i: (i, 0))
    vec = pl.BlockSpec((1, d), lambda i: (0, 0))
    return pl.pallas_call(
        body, name="loss_head", grid=(s // tm,), in_specs=[row, row, vec],
        out_specs=[row, row, pl.BlockSpec((1, LANES), lambda i: (0, 0)), vec],
        out_shape=[jax.ShapeDtypeStruct((s, d), F32), jax.ShapeDtypeStruct((s, d), BF16),
                   jax.ShapeDtypeStruct((1, LANES), F32), jax.ShapeDtypeStruct((1, d), F32)],
        compiler_params=_cp(("arbitrary",)))(x3, target, g)


def _prev_halo_spec(tm, width, col):
    return pl.BlockSpec((HALO, width), lambda i, *_: (jnp.maximum(i * (tm // HALO) - 1, 0), col))


def _next_halo_spec(tm, width, col, s):
    return pl.BlockSpec((HALO, width), lambda i, *_: (jnp.minimum((i + 1) * (tm // HALO), s // HALO - 1), col))


def _shift_down(x, k):
    return pltpu.roll(x, k, 0)


def _shift_up(x, k):
    return pltpu.roll(x, x.shape[0] - k, 0)


def _conv_taps(x_ext, w):
    return w[0:1, :] * _shift_down(x_ext, 2) + w[1:2, :] * _shift_down(x_ext, 1) + w[2:3, :] * x_ext


def _conv_taps_t(d_ext, w):
    return w[2:3, :] * d_ext + w[1:2, :] * _shift_up(d_ext, 1) + w[0:1, :] * _shift_up(d_ext, 2)


def _mixer_fwd(z_a, o_attn, w_conv, g_conv_out, g_attn_out):
    s = z_a.shape[0]
    c = CONV_CH
    tm = min(TM_ROWS, s)

    def body(gb_ref, gc_ref, xc_ref, gcp_ref, xcp_ref, o_ref, w_ref, gco_ref, gao_ref, mix_ref):
        i = pl.program_id(0)
        cx = gc_ref[...] * xc_ref[...]
        cx_prev = jnp.where(i > 0, gcp_ref[...] * xcp_ref[...], 0.0)
        conv = _conv_taps(jnp.concatenate([cx_prev, cx], axis=0), w_ref[...])[HALO:]
        y = gb_ref[...] * conv
        mix_ref[:, 0:c] = (y * _rstd(y) * gco_ref[...]).astype(BF16)
        o = o_ref[...]
        mix_ref[:, c:2 * c] = (o * _rstd(o) * gao_ref[...]).astype(BF16)

    col = lambda j: pl.BlockSpec((tm, c), lambda i: (i, j))
    vec = pl.BlockSpec((1, c), lambda i: (0, 0))
    return pl.pallas_call(
        body, name="mixer_fwd", grid=(s // tm,),
        in_specs=[col(0), col(1), col(2), _prev_halo_spec(tm, c, 1), _prev_halo_spec(tm, c, 2), col(0),
                  pl.BlockSpec((3, c), lambda i: (0, 0)), vec, vec],
        out_specs=pl.BlockSpec((tm, 2 * c), lambda i: (i, 0)),
        out_shape=jax.ShapeDtypeStruct((s, 2 * c), BF16),
        compiler_params=_cp(("parallel",)))(z_a, z_a, z_a, z_a, z_a, o_attn, w_conv, g_conv_out, g_attn_out)


def _mixer_bwd(z_a, o_attn, dmix, w_conv, g_conv_out, g_attn_out):
    s = z_a.shape[0]
    c = CONV_CH
    tm = min(TM_ROWS, s)
    n_blk = s // tm

    def body(gb_ref, gc_ref, xc_ref, gcp_ref, xcp_ref, gbn_ref, gcn_ref, xcn_ref, o_ref, dnc_ref, dncn_ref, dna_ref,
             w_ref, gco_ref, gao_ref, dz_ref, dox_ref, gw_ref, ggco_ref, ggao_ref):
        i = pl.program_id(0)

        @pl.when(i == 0)
        def _():
            gw_ref[...] = jnp.zeros_like(gw_ref)
            ggco_ref[...] = jnp.zeros_like(ggco_ref)
            ggao_ref[...] = jnp.zeros_like(ggao_ref)

        w = w_ref[...]
        zeros = jnp.zeros((HALO, c), F32)
        gb_e = jnp.concatenate([zeros, gb_ref[...], gbn_ref[...]], axis=0)
        cx_prev = jnp.where(i > 0, gcp_ref[...] * xcp_ref[...], 0.0)
        gc_e = jnp.concatenate([zeros, gc_ref[...], gcn_ref[...]], axis=0)
        xc_e = jnp.concatenate([zeros, xc_ref[...], xcn_ref[...]], axis=0)
        cx_e = jnp.concatenate([cx_prev, gc_ref[...] * xc_ref[...], gcn_ref[...] * xcn_ref[...]], axis=0)
        dn_next = jnp.where(i < n_blk - 1, dncn_ref[...], 0.0)
        dn_e = jnp.concatenate([zeros, dnc_ref[...], dn_next], axis=0)

        conv_e = _conv_taps(cx_e, w)
        y_e = gb_e * conv_e
        r_e = _rstd(y_e)
        yn_e = y_e * r_e
        t_e = dn_e * gco_ref[...]
        dy_e = r_e * (t_e - yn_e * jnp.mean(t_e * yn_e, axis=-1, keepdims=True))
        dconv_e = dy_e * gb_e
        dcx_e = _conv_taps_t(dconv_e, w)
        blk = slice(HALO, HALO + tm)
        dz_ref[:, 0:c] = (dy_e * conv_e)[blk].astype(BF16)
        dz_ref[:, c:2 * c] = (dcx_e * xc_e)[blk].astype(BF16)
        dz_ref[:, 2 * c:3 * c] = (dcx_e * gc_e)[blk].astype(BF16)
        ggco_ref[...] += jnp.sum((dn_e * yn_e)[blk], axis=0, keepdims=True)
        dconv = dconv_e[blk]
        gw_ref[0:1, :] += jnp.sum(dconv * _shift_down(cx_e, 2)[blk], axis=0, keepdims=True)
        gw_ref[1:2, :] += jnp.sum(dconv * _shift_down(cx_e, 1)[blk], axis=0, keepdims=True)
        gw_ref[2:3, :] += jnp.sum(dconv * cx_e[blk], axis=0, keepdims=True)

        o = o_ref[...]
        ra = _rstd(o)
        on = o * ra
        dna = dna_ref[...]
        ggao_ref[...] += jnp.sum(dna * on, axis=0, keepdims=True)
        ta = dna * gao_ref[...]
        do = ra * (ta - on * jnp.mean(ta * on, axis=-1, keepdims=True))
        prod = do * o
        lane = lax.broadcasted_iota(jnp.int32, (tm, LANES), 1)
        head_a = lane < HEAD_DIM
        for p in range(N_PAIRS):
            cols = slice(p * LANES, (p + 1) * LANES)
            pb, dob = prod[:, cols], do[:, cols]
            for hh in range(2):
                sel = head_a if hh == 0 else jnp.logical_not(head_a)
                delta = jnp.sum(jnp.where(sel, pb, 0.0), axis=-1, keepdims=True)
                neg3 = _split3(-delta)
                do_h = pltpu.roll(dob, HEAD_DIM, 1) if hh else dob
                dox_ref[2 * p + hh] = _aug(do_h, lane, neg3).astype(BF16)

    col = lambda j: pl.BlockSpec((tm, c), lambda i: (i, j))
    vec = pl.BlockSpec((1, c), lambda i: (0, 0))
    w3 = pl.BlockSpec((3, c), lambda i: (0, 0))
    return pl.pallas_call(
        body, name="mixer_bwd", grid=(n_blk,),
        in_specs=[col(0), col(1), col(2), _prev_halo_spec(tm, c, 1), _prev_halo_spec(tm, c, 2),
                  _next_halo_spec(tm, c, 0, s), _next_halo_spec(tm, c, 1, s), _next_halo_spec(tm, c, 2, s),
                  col(0), col(0), _next_halo_spec(tm, c, 0, s), col(1), w3, vec, vec],
        out_specs=[pl.BlockSpec((tm, 3 * c), lambda i: (i, 0)),
                   pl.BlockSpec((N_HEADS, tm, LANES), lambda i: (0, i, 0)), w3, vec, vec],
        out_shape=[jax.ShapeDtypeStruct((s, 3 * c), BF16), jax.ShapeDtypeStruct((N_HEADS, s, LANES), BF16),
                   jax.ShapeDtypeStruct((3, c), F32), jax.ShapeDtypeStruct((1, c), F32),
                   jax.ShapeDtypeStruct((1, c), F32)],
        compiler_params=_cp(("arbitrary",)))(
            z_a, z_a, z_a, z_a, z_a, z_a, z_a, z_a, o_attn, dmix, dmix, dmix, w_conv, g_conv_out, g_attn_out)


def _gate_fwd(f, b_pad):
    s = f.shape[0]
    tm = min(TQ, s)

    def body(f_ref, b_ref, fb_ref, carry):
        @pl.when(pl.program_id(0) == 0)
        def _():
            carry[...] = jnp.zeros_like(carry)

        z = f_ref[...] + b_ref[...]
        x = jnp.minimum(z, 0.0) - jnp.log1p(jnp.exp(-jnp.abs(z)))
        row = lax.broadcasted_iota(jnp.int32, (tm, LANES), 0)
        sh = 1
        while sh < tm:
            x = x + jnp.where(row >= sh, _shift_down(x, sh), 0.0)
            sh *= 2
        x = x + carry[0:1, :]
        carry[...] = jnp.broadcast_to(x[tm - 1:tm, :], carry.shape)
        head_a = lax.broadcasted_iota(jnp.int32, (tm, LANES), 1) < HEAD_DIM
        for p in range(N_PAIRS):
            fa = jnp.broadcast_to(x[:, 2 * p:2 * p + 1], (tm, LANES))
            fbv = jnp.broadcast_to(x[:, 2 * p + 1:2 * p + 2], (tm, LANES))
            fb_ref[:, p * LANES:(p + 1) * LANES] = jnp.where(head_a, fa, fbv)

    return pl.pallas_call(
        body, name="gate_fwd", grid=(s // tm,),
        in_specs=[pl.BlockSpec((tm, LANES), lambda i: (i, 0)), pl.BlockSpec((1, LANES), lambda i: (0, 0))],
        out_specs=pl.BlockSpec((tm, N_PAIRS * LANES), lambda i: (i, 0)),
        out_shape=jax.ShapeDtypeStruct((s, N_PAIRS * LANES), F32),
        scratch_shapes=[pltpu.VMEM((HALO, LANES), F32)],
        compiler_params=_cp(("arbitrary",)))(f, b_pad)


def _gate_bwd(f, b_pad, d_f):
    s = f.shape[0]
    tm = min(TQ, s)
    n_blk = s // tm

    def body(f_ref, b_ref, d_ref, df_ref, gb_ref, carry):
        @pl.when(pl.program_id(0) == 0)
        def _():
            carry[...] = jnp.zeros_like(carry)
            gb_ref[...] = jnp.zeros_like(gb_ref)

        x = d_ref[...]
        row = lax.broadcasted_iota(jnp.int32, (tm, LANES), 0)
        sh = 1
        while sh < tm:
            x = x + jnp.where(row < tm - sh, _shift_up(x, sh), 0.0)
            sh *= 2
        x = x + carry[0:1, :]
        carry[...] = jnp.broadcast_to(x[0:1, :], carry.shape)
        z = f_ref[...] + b_ref[...]
        d = x * (1.0 / (1.0 + jnp.exp(z)))
        df_ref[...] = d.astype(BF16)
        gb_ref[...] += jnp.sum(d, axis=0, keepdims=True)

    rev = pl.BlockSpec((tm, LANES), lambda i: (n_blk - 1 - i, 0))
    vec = pl.BlockSpec((1, LANES), lambda i: (0, 0))
    return pl.pallas_call(
        body, name="gate_bwd", grid=(n_blk,), in_specs=[rev, vec, rev], out_specs=[rev, vec],
        out_shape=[jax.ShapeDtypeStruct((s, LANES), BF16), jax.ShapeDtypeStruct((1, LANES), F32)],
        scratch_shapes=[pltpu.VMEM((HALO, LANES), F32)],
        compiler_params=_cp(("arbitrary",)))(f, b_pad, d_f)


_NT = (((1,), (1,)), ((), ()))
_NN = (((1,), (0,)), ((), ()))
_TN = (((0,), (0,)), ((), ()))


def _head_masks(shape):
    lane = lax.broadcasted_iota(jnp.int32, shape, len(shape) - 1)
    return lane < HEAD_DIM


def _pick_row(ft, h):
    rows = lax.broadcasted_iota(jnp.int32, ft.shape, 0)
    return jnp.sum(jnp.where(rows == h, ft, 0.0), axis=0, keepdims=True)


def _prune_bounds(qkv, fb):
    s = qkv.shape[0]
    tq = min(TQ, s)

    def body(q_ref, k_ref, fb_ref, out_ref):
        head_a = _head_masks((tq, LANES))
        lane = lax.broadcasted_iota(jnp.int32, (HALO, LANES), 1)
        acc = jnp.zeros((HALO, LANES), F32)
        for p in range(N_PAIRS):
            cols = slice(p * LANES, (p + 1) * LANES)
            q2 = q_ref[:, cols].astype(F32) * Q_SCALE
            k2 = k_ref[:, cols].astype(F32)
            f2 = fb_ref[:, cols]
            for hh in range(2):
                sel = head_a if hh == 0 else jnp.logical_not(head_a)
                qn = jnp.sqrt(jnp.sum(jnp.where(sel, q2 * q2, 0.0), axis=-1, keepdims=True))
                kn = jnp.sqrt(jnp.sum(jnp.where(sel, k2 * k2, 0.0), axis=-1, keepdims=True))
                f = f2[:, hh * HEAD_DIM:hh * HEAD_DIM + 1]
                h = 2 * p + hh
                vals = (jnp.max(qn, axis=0, keepdims=True), jnp.max(kn, axis=0, keepdims=True),
                        jnp.max(qn * kn + f, axis=0, keepdims=True), f[tq - 1:tq, :])
                for slot, v in enumerate(vals):
                    acc = jnp.where(lane == slot * N_HEADS + h, v, acc)
        out_ref[0] = acc

    blk = lambda j: pl.BlockSpec((tq, ATTN_W), lambda i: (i, j))
    return pl.pallas_call(
        body, name="prune_bounds", grid=(s // tq,), in_specs=[blk(0), blk(1), blk(0)],
        out_specs=pl.BlockSpec((1, HALO, LANES), lambda i: (i, 0, 0)),
        out_shape=jax.ShapeDtypeStruct((s // tq, HALO, LANES), F32),
        compiler_params=_cp(("parallel",)))(qkv, qkv, fb)


def _first_key_blocks(qkv, fb):
    t = _prune_bounds(qkv, fb)[:, 0, :]
    nh = N_HEADS
    a, b, c, e = t[:, 0:nh], t[:, nh:2 * nh], t[:, 2 * nh:3 * nh], t[:, 3 * nh:4 * nh]
    bound = a[:, None, :] * b[None, :, :] * 1.001 + c[:, None, :] - e[None, :, :]
    n_q = t.shape[0]
    idx = jnp.arange(n_q)
    need = jnp.logical_not(bound < -(EXP_ZERO + 2.0)) | (idx[None, :, None] >= idx[:, None, None])
    first = jnp.argmax(need, axis=1).astype(jnp.int32)
    return jnp.min(first.reshape(n_q, N_PAIRS, 2), axis=-1).T.reshape(-1)


def _attn_fwd(qkv, fb, ft, first_blk):
    s = qkv.shape[0]
    tq = min(TQ, s)
    n_q = s // tq
    neg = -1e30

    rc = min(ROW_CHUNK, tq)

    def body(first_ref, q_ref, k_ref, v_ref, fb_ref, ft_ref, o_ref, g_ref,
             s_a, s_b, p_ab, m_a, l_a, m_b, l_b, alpha2, acc, pmax_a, psum_a, pmax_b, psum_b):
        p = pl.program_id(0)
        i = pl.program_id(1)
        head_a = _head_masks((tq, LANES))
        q2 = q_ref[...] * Q_SCALE
        zero = jnp.zeros_like(q2)
        q_a, q_b = jnp.where(head_a, q2, zero), jnp.where(head_a, zero, q2)
        for m_scr, l_scr in ((m_a, l_a), (m_b, l_b)):
            m_scr[...] = jnp.full(m_scr.shape, neg, F32)
            l_scr[...] = jnp.zeros(l_scr.shape, F32)
        acc[...] = jnp.zeros(acc.shape, F32)

        def step(kb, masked):
            rows_k = pl.ds(pl.multiple_of(kb * tq, tq), tq)
            k2, v2 = k_ref[rows_k, :], v_ref[rows_k, :]
            ftv = ft_ref[kb]
            fk = (_pick_row(ftv, 2 * p), _pick_row(ftv, 2 * p + 1))
            s_a[...] = lax.dot_general(q_a, k2, _NT, preferred_element_type=F32)
            s_b[...] = lax.dot_general(q_b, k2, _NT, preferred_element_type=F32)
            alphas = []
            for hh, (s_scr, m_scr, l_scr, pmax, psum) in enumerate(((s_a, m_a, l_a, pmax_a, psum_a),
                                                                     (s_b, m_b, l_b, pmax_b, psum_b))):
                fq = fb_ref[:, hh * HEAD_DIM:hh * HEAD_DIM + 1]

                def shifted(r, ncol, add=None):
                    rows = slice(r * rc, (r + 1) * rc)
                    t = s_scr[rows, 0:ncol]
                    if add is not None:
                        t = t + add[rows, :]
                    t = t - fk[hh][:, 0:ncol]
                    if masked:
                        col_id = lax.broadcasted_iota(jnp.int32, (rc, ncol), 1)
                        row_id = lax.broadcasted_iota(jnp.int32, (rc, ncol), 0) + r * rc
                        t = jnp.where(col_id <= row_id, t, -jnp.inf)
                    return rows, t

                def lane_blocks(t, op):
                    out = t[:, 0:LANES]
                    for cb in range(1, t.shape[1] // LANES):
                        out = op(out, t[:, cb * LANES:(cb + 1) * LANES])
                    return out

                ncols = [min(tq, -(-((r + 1) * rc) // LANES) * LANES) if masked else tq for r in range(tq // rc)]
                for r, ncol in enumerate(ncols):
                    rows, t = shifted(r, ncol)
                    pmax[rows, :] = lane_blocks(t, jnp.maximum)
                m_old = m_scr[...]
                m_new = jnp.maximum(m_old, jnp.max(pmax[...], axis=-1, keepdims=True) + fq)
                alpha = jnp.exp(m_old - m_new)
                m_scr[...] = m_new
                shift = fq - m_new
                for r, ncol in enumerate(ncols):
                    rows, t = shifted(r, ncol, shift)
                    pr = jnp.exp(t)
                    psum[rows, :] = lane_blocks(pr, jnp.add)
                    p_ab[rows, hh * tq:hh * tq + ncol] = pr.astype(BF16)
                    if ncol < tq:
                        p_ab[rows, hh * tq + ncol:(hh + 1) * tq] = jnp.zeros((rc, tq - ncol), BF16)
                l_scr[...] = alpha * l_scr[...] + jnp.sum(psum[...], axis=-1, keepdims=True)
                alphas.append(alpha)
            alpha2[...] = jnp.where(head_a, alphas[0], alphas[1])
            zv = jnp.zeros_like(v2)
            head_k = _head_masks(v2.shape)
            vv = jnp.concatenate([jnp.where(head_k, v2, zv), jnp.where(head_k, zv, v2)], axis=0)
            acc[...] = acc[...] * alpha2[...] + lax.dot_general(p_ab[...], vv, _NN, preferred_element_type=F32)

        def unmasked(kb, carry):
            step(kb, False)
            return carry

        lax.fori_loop(first_ref[p * n_q + i], i, unmasked, 0)
        step(i, True)
        fq2 = fb_ref[...]
        o_ref[...] = acc[...] / jnp.where(head_a, l_a[...], l_b[...])
        g_ref[...] = fq2 - jnp.where(head_a, m_a[...] + jnp.log(l_a[...]), m_b[...] + jnp.log(l_b[...]))

    qblk = lambda off: pl.BlockSpec((tq, LANES), lambda p, i, first: (i, off + p))
    full = lambda off: pl.BlockSpec((s, LANES), lambda p, i, first: (0, off + p))
    col_scr = pltpu.VMEM((tq, 1), F32)
    grid_spec = pltpu.PrefetchScalarGridSpec(
        num_scalar_prefetch=1, grid=(N_PAIRS, n_q),
        in_specs=[qblk(0), full(N_PAIRS), full(2 * N_PAIRS), qblk(0),
                  pl.BlockSpec((n_q, N_HEADS, tq), lambda p, i, first: (0, 0, 0))],
        out_specs=[qblk(0), qblk(0)],
        scratch_shapes=[pltpu.VMEM((tq, tq), F32), pltpu.VMEM((tq, tq), F32), pltpu.VMEM((tq, 2 * tq), BF16),
                        col_scr, col_scr, col_scr, col_scr] + [pltpu.VMEM((tq, LANES), F32)] * 6)
    return pl.pallas_call(
        body, name="attn_fwd", grid_spec=grid_spec,
        out_shape=[jax.ShapeDtypeStruct((s, ATTN_W), F32), jax.ShapeDtypeStruct((s, ATTN_W), F32)],
        compiler_params=_cp(("parallel", "arbitrary")))(first_blk, qkv, qkv, qkv, fb, ft)


def _attn_bwd(qkv, do, g, delta, ft, first_blk):
    s = qkv.shape[0]
    tq = min(TQ, s)
    n_q = s // tq

    def body(first_ref, q_ref, do_ref, g_ref, dl_ref, k_ref, v_ref, ft_ref, dq_ref, dfq_ref, dk_ref, dv_ref,
             dfk_ref):
        p = pl.program_id(0)
        i = pl.program_id(1)

        @pl.when(i == 0)
        def _():
            dk_ref[...] = jnp.zeros_like(dk_ref)
            dv_ref[...] = jnp.zeros_like(dv_ref)
            dfk_ref[...] = jnp.zeros_like(dfk_ref)

        head_a = _head_masks((tq, LANES))
        q2 = q_ref[...] * Q_SCALE
        do2 = do_ref[...]
        zero = jnp.zeros_like(q2)
        q_a, q_b = jnp.where(head_a, q2, zero), jnp.where(head_a, zero, q2)
        do_a, do_b = jnp.where(head_a, do2, zero), jnp.where(head_a, zero, do2)
        g2, dl2 = g_ref[...], dl_ref[...]
        g_a, g_b = g2[:, 0:1], g2[:, HEAD_DIM:HEAD_DIM + 1]
        dl_a, dl_b = dl2[:, 0:1], dl2[:, HEAD_DIM:HEAD_DIM + 1]
        causal = lax.broadcasted_iota(jnp.int32, (tq, tq), 1) <= lax.broadcasted_iota(jnp.int32, (tq, tq), 0)
        rows8 = lax.broadcasted_iota(jnp.int32, (N_HEADS, tq), 0)

        def one_head(q_h, do_h, g_h, dl_h, fk_h, k2, v2, masked):
            sc = (lax.dot_general(q_h, k2, _NT, preferred_element_type=F32) + g_h) - fk_h
            pr = jnp.exp(sc)
            if masked:
                pr = jnp.where(causal, pr, 0.0)
            dp = lax.dot_general(do_h, v2, _NT, preferred_element_type=F32)
            ds = pr * (dp - dl_h)
            return (pr.astype(BF16), ds.astype(BF16), jnp.sum(ds, axis=0, keepdims=True),
                    jnp.sum(ds, axis=1, keepdims=True))

        def step(j, carry, masked):
            dq, r_a, r_b = carry
            rows = pl.ds(pl.multiple_of(j * tq, tq), tq)
            k2, v2 = k_ref[rows, :], v_ref[rows, :]
            k_a, k_b = jnp.where(head_a, k2, zero), jnp.where(head_a, zero, k2)
            ftv = ft_ref[j]
            p_a, ds_a, c_a, s_a = one_head(q_a, do_a, g_a, dl_a, _pick_row(ftv, 2 * p), k2, v2, masked)
            p_b, ds_b, c_b, s_b = one_head(q_b, do_b, g_b, dl_b, _pick_row(ftv, 2 * p + 1), k2, v2, masked)
            dv_ref[rows, :] += (lax.dot_general(p_a, do_a, _TN, preferred_element_type=F32)
                                + lax.dot_general(p_b, do_b, _TN, preferred_element_type=F32))
            dk_ref[rows, :] += (lax.dot_general(ds_a, q_a, _TN, preferred_element_type=F32)
                                + lax.dot_general(ds_b, q_b, _TN, preferred_element_type=F32))
            dfk_ref[0, j] += jnp.where(rows8 == 0, -c_a, jnp.where(rows8 == 1, -c_b, 0.0))
            dq = dq + (lax.dot_general(ds_a, k_a, _NN, preferred_element_type=F32)
                       + lax.dot_general(ds_b, k_b, _NN, preferred_element_type=F32))
            return dq, r_a + s_a, r_b + s_b

        zcol = jnp.zeros((tq, 1), F32)
        carry = lax.fori_loop(first_ref[p * n_q + i], i, lambda j, cr: step(j, cr, False),
                              (jnp.zeros((tq, LANES), F32), zcol, zcol))
        dq, r_a, r_b = step(i, carry, True)
        dq_ref[...] = (dq * Q_SCALE).astype(BF16)
        dfq_ref[...] = jnp.where(head_a, r_a, r_b)

    qblk = lambda off: pl.BlockSpec((tq, LANES), lambda p, i, first: (i, off + p))
    full = lambda off: pl.BlockSpec((s, LANES), lambda p, i, first: (0, off + p))
    grid_spec = pltpu.PrefetchScalarGridSpec(
        num_scalar_prefetch=1, grid=(N_PAIRS, n_q),
        in_specs=[qblk(0), qblk(0), qblk(0), qblk(0), full(N_PAIRS), full(2 * N_PAIRS),
                  pl.BlockSpec((n_q, N_HEADS, tq), lambda p, i, first: (0, 0, 0))],
        out_specs=[qblk(0), qblk(0), full(0), full(0),
                   pl.BlockSpec((1, n_q, N_HEADS, tq), lambda p, i, first: (p, 0, 0, 0))])
    return pl.pallas_call(
        body, name="attn_bwd", grid_spec=grid_spec,
        out_shape=[jax.ShapeDtypeStruct((s, ATTN_W), BF16), jax.ShapeDtypeStruct((s, ATTN_W), F32),
                   jax.ShapeDtypeStruct((s, ATTN_W), F32), jax.ShapeDtypeStruct((s, ATTN_W), F32),
                   jax.ShapeDtypeStruct((N_PAIRS, n_q, N_HEADS, tq), F32)],
        compiler_params=_cp(("parallel", "arbitrary")))(first_blk, qkv, do, g, delta, qkv, qkv, ft)


AUG = HEAD_DIM
NORM_MARGIN = 1.01


def _split3(x):
    hi = x.astype(BF16).astype(F32)
    r = x - hi
    mid = r.astype(BF16).astype(F32)
    lo = (r - mid).astype(BF16).astype(F32)
    return hi, mid, lo


def _aug(base, lane, vals):
    out = jnp.where(lane < AUG, base, 0.0)
    for k, v in enumerate(vals):
        out = jnp.where(lane == AUG + k, v, out)
    return out


def _attn_prep(qkv, fb):
    s = qkv.shape[0]
    tq = min(TQ, s)
    n_q = s // tq

    def body(q_ref, k_ref, v_ref, fb_ref, qx_ref, kx_ref, kxt_ref, vx_ref, vt_ref, b_ref):
        lane = lax.broadcasted_iota(jnp.int32, (tq, LANES), 1)
        lane8 = lax.broadcasted_iota(jnp.int32, (HALO, LANES), 1)
        head_lanes = lane < AUG
        is_lane = [lane == AUG + k for k in range(6)]
        first3 = (lane >= AUG) & (lane < AUG + 3)
        next3 = (lane >= AUG + 3) & (lane < AUG + 6)
        q_const = jnp.where(first3, -1.0, 0.0)
        k_const = jnp.where(next3, 1.0, 0.0)
        v_const = jnp.where(first3, 1.0, 0.0)
        ones_head = (lax.broadcasted_iota(jnp.int32, (LANES, LANES), 0) < HEAD_DIM).astype(BF16)
        acc = jnp.zeros((HALO, LANES), F32)
        for p in range(N_PAIRS):
            cols = slice(p * LANES, (p + 1) * LANES)
            q2, k2, v2 = (ref[:, cols].astype(F32) for ref in (q_ref, k_ref, v_ref))
            q2 = q2 * Q_SCALE
            f2 = fb_ref[:, cols]
            for hh in range(2):
                h = 2 * p + hh
                q, k, v = ((pltpu.roll(x, HEAD_DIM, 1) if hh else x) for x in (q2, k2, v2))
                f = f2 if hh else pltpu.roll(f2, HEAD_DIM, 1)
                hi, mid, lo = _split3(f)
                q_aug = jnp.where(is_lane[3], hi, jnp.where(is_lane[4], mid, jnp.where(is_lane[5], lo, q_const)))
                k_aug = jnp.where(is_lane[0], hi, jnp.where(is_lane[1], mid, jnp.where(is_lane[2], lo, k_const)))
                kx = jnp.where(head_lanes, k, k_aug)
                vx = jnp.where(head_lanes, v, v_const)
                qx_ref[h] = jnp.where(head_lanes, q, q_aug).astype(BF16)
                kx_ref[h] = kx.astype(BF16)
                vx_ref[h] = vx.astype(BF16)
                kxt_ref[h, 0] = kx.T.astype(BF16)
                vt_ref[h, 0] = vx.T.astype(BF16)
                q_sq = lax.dot_general((q * q).astype(BF16), ones_head, _NN, preferred_element_type=F32)
                k_sq = lax.dot_general((k * k).astype(BF16), ones_head, _NN, preferred_element_type=F32)
                qk = jnp.sqrt(q_sq * k_sq) * NORM_MARGIN + f
                vals = (jnp.sqrt(jnp.max(q_sq, axis=0, keepdims=True)), jnp.sqrt(jnp.max(k_sq, axis=0, keepdims=True)),
                        jnp.max(qk, axis=0, keepdims=True), f[tq - 1:tq, :])
                for slot, val in enumerate(vals):
                    acc = jnp.where(lane8 == slot * N_HEADS + h, val[:, AUG:AUG + 1], acc)
        b_ref[0] = acc

    blk = lambda j: pl.BlockSpec((tq, ATTN_W), lambda i: (i, j))
    rows = pl.BlockSpec((N_HEADS, tq, LANES), lambda i: (0, i, 0))
    cols_t = pl.BlockSpec((N_HEADS, 1, LANES, tq), lambda i: (0, i, 0, 0))
    shp = jax.ShapeDtypeStruct((N_HEADS, s, LANES), BF16)
    shp_t = jax.ShapeDtypeStruct((N_HEADS, n_q, LANES, tq), BF16)
    return pl.pallas_call(
        body, name="attn_prep", grid=(n_q,), in_specs=[blk(0), blk(1), blk(2), blk(0)],
        out_specs=[rows, rows, cols_t, rows, cols_t, pl.BlockSpec((1, HALO, LANES), lambda i: (i, 0, 0))],
        out_shape=[shp, shp, shp_t, shp, shp_t, jax.ShapeDtypeStruct((n_q, HALO, LANES), F32)],
        compiler_params=_cp(("parallel",)))(qkv, qkv, qkv, fb)


def _key_block_ranges(bounds):
    t = bounds[:, 0, :]
    nh = N_HEADS
    a, b, c, e = t[:, 0:nh], t[:, nh:2 * nh], t[:, 2 * nh:3 * nh], t[:, 3 * nh:4 * nh]
    bound = a[:, None, :] * b[None, :, :] * NORM_MARGIN + c[:, None, :] - e[None, :, :]
    n_q = t.shape[0]
    idx = jnp.arange(n_q)
    need = jnp.logical_not(bound < -(EXP_ZERO + 2.0)) | (idx[None, :, None] >= idx[:, None, None])
    first = jnp.argmax(need, axis=1).astype(jnp.int32)
    first = jnp.min(first.reshape(n_q, N_PAIRS, 2), axis=-1)
    visits = (first[:, None, :] <= idx[None, :, None]) & (idx[None, :, None] <= idx[:, None, None])
    last = jnp.max(jnp.where(visits, idx[:, None, None], 0), axis=0).astype(jnp.int32)
    return first.T.reshape(-1), last.T.reshape(-1)


def _attn_fwd_t(qx, kx, vt, first_blk):
    _, s, _ = qx.shape
    tq = min(TQ, s)
    n_q = s // tq
    neg = -1e30

    def body(first_ref, qx_ref, kx_ref, vt_ref, o_ref, lse_ref, acc_ref, m_ref):
        p = pl.program_id(0)
        i = pl.program_id(1)
        acc_ref[...] = jnp.zeros(acc_ref.shape, F32)
        m_ref[...] = jnp.full(m_ref.shape, neg, F32)
        key_le_query = (lax.broadcasted_iota(jnp.int32, (tq, tq), 0) <= lax.broadcasted_iota(jnp.int32, (tq, tq), 1))

        def step(kb, masked):
            rows_k = pl.ds(pl.multiple_of(kb * tq, tq), tq)
            for hh in range(2):
                st = lax.dot_general(kx_ref[hh, rows_k, :], qx_ref[hh], _NT, preferred_element_type=F32)
                if masked:
                    st = jnp.where(key_le_query, st, -jnp.inf)
                m_old = m_ref[hh]
                m_new = jnp.maximum(m_old, jnp.max(st, axis=0, keepdims=True))
                m_ref[hh] = m_new
                pt = jnp.exp(st - m_new).astype(BF16)
                acc_ref[hh] = acc_ref[hh] * jnp.exp(m_old - m_new) + lax.dot_general(
                    vt_ref[hh, kb], pt, _NN, preferred_element_type=F32)

        def unmasked(kb, carry):
            step(kb, False)
            return carry

        lax.fori_loop(first_ref[p * n_q + i], i, unmasked, 0)
        step(i, True)
        outs, lses = [], []
        for hh in range(2):
            acc = acc_ref[hh]
            l = acc[AUG:AUG + 1, :]
            outs.append(acc[0:HEAD_DIM, :] / l)
            lses.append(m_ref[hh] + jnp.log(l))
        o_ref[...] = jnp.concatenate(outs, axis=0).T
        rows8 = lax.broadcasted_iota(jnp.int32, (N_HEADS, tq), 0)
        lse_ref[0, 0] = jnp.where(rows8 == 0, lses[0], jnp.where(rows8 == 1, lses[1], 0.0))

    grid_spec = pltpu.PrefetchScalarGridSpec(
        num_scalar_prefetch=1, grid=(N_PAIRS, n_q),
        in_specs=[pl.BlockSpec((2, tq, LANES), lambda p, i, first: (p, i, 0)),
                  pl.BlockSpec((2, s, LANES), lambda p, i, first: (p, 0, 0)),
                  pl.BlockSpec((2, n_q, LANES, tq), lambda p, i, first: (p, 0, 0, 0))],
        out_specs=[pl.BlockSpec((tq, LANES), lambda p, i, first: (i, p)),
                   pl.BlockSpec((1, 1, N_HEADS, tq), lambda p, i, first: (p, i, 0, 0))],
        scratch_shapes=[pltpu.VMEM((2, LANES, tq), F32), pltpu.VMEM((2, 1, tq), F32)])
    return pl.pallas_call(
        body, name="attn_fwd", grid_spec=grid_spec,
        out_shape=[jax.ShapeDtypeStruct((s, ATTN_W), F32), jax.ShapeDtypeStruct((N_PAIRS, n_q, N_HEADS, tq), F32)],
        compiler_params=_cp(("parallel", "arbitrary")))(first_blk, qx, kx, vt)


def _attn_bwd_t(qx, dox, kx, kxt, vx, lse, last_blk):
    _, s, _ = qx.shape
    tq = min(TQ, s)
    n_q = s // tq

    def body(last_ref, qx_ref, dox_ref, lse_ref, kx_ref, kxt_ref, vx_ref, dk_ref, dv_ref, aux_ref, dqt_ref):
        p = pl.program_id(0)
        j = pl.program_id(1)

        @pl.when(j == 0)
        def _():
            dqt_ref[...] = jnp.zeros(dqt_ref.shape, F32)

        key_le_query = (lax.broadcasted_iota(jnp.int32, (tq, tq), 0) <= lax.broadcasted_iota(jnp.int32, (tq, tq), 1))

        def step(i, carry, masked):
            rows_q = pl.ds(pl.multiple_of(i * tq, tq), tq)
            out = []
            for hh in range(2):
                dk, dv = carry[2 * hh], carry[2 * hh + 1]
                q, do = qx_ref[hh, rows_q, :], dox_ref[hh, rows_q, :]
                st = lax.dot_general(kx_ref[hh], q, _NT, preferred_element_type=F32)
                pt = jnp.exp(st - lse_ref[0, i, hh:hh + 1, :])
                if masked:
                    pt = jnp.where(key_le_query, pt, 0.0)
                dst = pt * lax.dot_general(vx_ref[hh], do, _NT, preferred_element_type=F32)
                pb, dsb = pt.astype(BF16), dst.astype(BF16)
                dv = dv + lax.dot_general(pb, do, _NN, preferred_element_type=F32)
                dk = dk + lax.dot_general(dsb, q, _NN, preferred_element_type=F32)
                dqt_ref[hh, i] += lax.dot_general(kxt_ref[hh, 0], dsb, _NN, preferred_element_type=F32)
                out += [dk, dv]
            return tuple(out)

        zero = jnp.zeros((tq, LANES), F32)
        carry = step(j, (zero, zero, zero, zero), True)
        dk_a, dv_a, dk_b, dv_b = lax.fori_loop(j + 1, last_ref[p * n_q + j] + 1,
                                               lambda i, cr: step(i, cr, False), carry)
        head_a = lax.broadcasted_iota(jnp.int32, (tq, LANES), 1) < HEAD_DIM
        dk_ref[...] = jnp.where(head_a, dk_a, pltpu.roll(dk_b, HEAD_DIM, 1)).astype(BF16)
        dv_ref[...] = jnp.where(head_a, dv_a, pltpu.roll(dv_b, HEAD_DIM, 1)).astype(BF16)
        aux_ref[...] = jnp.where(head_a, pltpu.roll(dk_a, HEAD_DIM, 1), dk_b)

    resident = pl.BlockSpec((2, s, LANES), lambda p, j, last: (p, 0, 0))
    key_rows = pl.BlockSpec((2, tq, LANES), lambda p, j, last: (p, j, 0))
    pair_out = pl.BlockSpec((tq, LANES), lambda p, j, last: (j, p))
    grid_spec = pltpu.PrefetchScalarGridSpec(
        num_scalar_prefetch=1, grid=(N_PAIRS, n_q),
        in_specs=[resident, resident, pl.BlockSpec((1, n_q, N_HEADS, tq), lambda p, j, last: (p, 0, 0, 0)),
                  key_rows, pl.BlockSpec((2, 1, LANES, tq), lambda p, j, last: (p, j, 0, 0)), key_rows],
        out_specs=[pair_out, pair_out, pair_out,
                   pl.BlockSpec((2, n_q, LANES, tq), lambda p, j, last: (p, 0, 0, 0))])
    return pl.pallas_call(
        body, name="attn_bwd", grid_spec=grid_spec,
        out_shape=[jax.ShapeDtypeStruct((s, ATTN_W), BF16), jax.ShapeDtypeStruct((s, ATTN_W), BF16),
                   jax.ShapeDtypeStruct((s, ATTN_W), F32), jax.ShapeDtypeStruct((N_HEADS, n_q, LANES, tq), F32)],
        compiler_params=_cp(("parallel", "arbitrary")))(last_blk, qx, dox, lse, kx, kxt, vx)


def _attn_dq_finish(dqt):
    _, n_q, _, tq = dqt.shape

    def body(dqt_ref, dq_ref, dfq_ref):
        a, b = dqt_ref[0, 0], dqt_ref[1, 0]
        dq_ref[...] = (jnp.concatenate([a[0:HEAD_DIM], b[0:HEAD_DIM]], axis=0).T * Q_SCALE).astype(BF16)
        rows8 = lax.broadcasted_iota(jnp.int32, (N_HEADS, tq), 0)
        dfq_ref[0, 0] = jnp.where(rows8 == 0, a[AUG + 3:AUG + 4], jnp.where(rows8 == 1, b[AUG + 3:AUG + 4], 0.0))

    return pl.pallas_call(
        body, name="attn_dq_finish", grid=(N_PAIRS, n_q),
        in_specs=[pl.BlockSpec((2, 1, LANES, tq), lambda p, i: (p, i, 0, 0))],
        out_specs=[pl.BlockSpec((tq, LANES), lambda p, i: (i, p)),
                   pl.BlockSpec((1, 1, N_HEADS, tq), lambda p, i: (p, i, 0, 0))],
        out_shape=[jax.ShapeDtypeStruct((n_q * tq, ATTN_W), BF16),
                   jax.ShapeDtypeStruct((N_PAIRS, n_q, N_HEADS, tq), F32)],
        compiler_params=_cp(("parallel", "parallel")))(dqt)


def _ffn_act_fwd(up, w_ffn):
    s = up.shape[0]
    tm, tn = min(TM_FFN, s), TN_FFN
    nb = D_FF // tn

    def body(a_ref, g_ref, ap_ref, gp_ref, wa_ref, wg_ref, act_ref):
        i = pl.program_id(1)

        def conv(blk_ref, prev_ref, w_ref):
            prev = jnp.where(i > 0, prev_ref[...], 0.0)
            return _conv_taps(jnp.concatenate([prev, blk_ref[...]], axis=0), w_ref[...])[HALO:]

        u_a, u_g = conv(a_ref, ap_ref, wa_ref), conv(g_ref, gp_ref, wg_ref)
        act_ref[...] = (u_g * (1.0 / (1.0 + jnp.exp(-u_g))) * u_a).astype(BF16)

    blk = lambda off: pl.BlockSpec((tm, tn), lambda n, i: (i, off + n))
    prev = lambda off: pl.BlockSpec((HALO, tn), lambda n, i: (jnp.maximum(i * (tm // HALO) - 1, 0), off + n))
    wsp = lambda off: pl.BlockSpec((3, tn), lambda n, i: (0, off + n))
    return pl.pallas_call(
        body, name="ffn_act_fwd", grid=(nb, s // tm),
        in_specs=[blk(0), blk(nb), prev(0), prev(nb), wsp(0), wsp(nb)],
        out_specs=pl.BlockSpec((tm, tn), lambda n, i: (i, n)),
        out_shape=jax.ShapeDtypeStruct((s, D_FF), BF16),
        compiler_params=_cp(("parallel", "parallel")))(up, up, up, up, w_ffn, w_ffn)


def _ffn_down_loss(up, w_ffn, w_down, x2, target, g_final):
    s, d = x2.shape
    tm, tn = min(TM_ROWS, s), TN_FFN
    nb = D_FF // tn

    def body(a_ref, g_ref, ap_ref, gp_ref, wa_ref, wg_ref, wd_ref, x2_ref, t_ref, gf_ref,
             act_ref, dx_ref, dxb_ref, loss_ref, gg_ref, acc):
        i = pl.program_id(0)
        k = pl.program_id(1)

        @pl.when((i == 0) & (k == 0))
        def _():
            gg_ref[...] = jnp.zeros_like(gg_ref)
            loss_ref[...] = jnp.zeros_like(loss_ref)

        def conv(blk_ref, prev_ref, w_ref):
            prev = jnp.where(i > 0, prev_ref[...], 0.0)
            return _conv_taps(jnp.concatenate([prev, blk_ref[...]], axis=0), w_ref[...])[HALO:]

        u_a, u_g = conv(a_ref, ap_ref, wa_ref), conv(g_ref, gp_ref, wg_ref)
        act = (u_g * (1.0 / (1.0 + jnp.exp(-u_g))) * u_a).astype(BF16)
        act_ref[...] = act
        part = lax.dot_general(act, wd_ref[...], _NN, preferred_element_type=F32)

        @pl.when(k == 0)
        def _():
            acc[...] = part

        @pl.when(k > 0)
        def _():
            acc[...] += part

        @pl.when(k == nb - 1)
        def _():
            xv = x2_ref[...] + acc[...]
            r = _rstd(xv)
            xn = xv * r
            gv = gf_ref[...]
            err = xn * gv - t_ref[...]
            loss_ref[...] += 0.5 * jnp.sum(jnp.mean(err * err, axis=-1, keepdims=True), axis=0, keepdims=True)
            dy = err * (1.0 / d)
            gg_ref[...] += jnp.sum(dy * xn, axis=0, keepdims=True)
            t = dy * gv
            dx = r * (t - xn * jnp.mean(t * xn, axis=-1, keepdims=True))
            dx_ref[...] = dx
            dxb_ref[...] = dx.astype(BF16)

    blk = lambda off: pl.BlockSpec((tm, tn), lambda i, k: (i, off + k))
    prev = lambda off: pl.BlockSpec((HALO, tn), lambda i, k: (jnp.maximum(i * (tm // HALO) - 1, 0), off + k))
    wsp = lambda off: pl.BlockSpec((3, tn), lambda i, k: (0, off + k))
    row = pl.BlockSpec((tm, d), lambda i, k: (i, 0))
    vec = pl.BlockSpec((1, d), lambda i, k: (0, 0))
    return pl.pallas_call(
        body, name="ffn_down_loss", grid=(s // tm, nb),
        in_specs=[blk(0), blk(nb), prev(0), prev(nb), wsp(0), wsp(nb),
                  pl.BlockSpec((tn, d), lambda i, k: (k, 0)), row, row, vec],
        out_specs=[pl.BlockSpec((tm, tn), lambda i, k: (i, k)), row, row,
                   pl.BlockSpec((1, LANES), lambda i, k: (0, 0)), vec],
        out_shape=[jax.ShapeDtypeStruct((s, D_FF), BF16), jax.ShapeDtypeStruct((s, d), F32),
                   jax.ShapeDtypeStruct((s, d), BF16), jax.ShapeDtypeStruct((1, LANES), F32),
                   jax.ShapeDtypeStruct((1, d), F32)],
        scratch_shapes=[pltpu.VMEM((tm, d), F32)],
        compiler_params=_cp(("arbitrary", "arbitrary")))(up, up, up, up, w_ffn, w_ffn, w_down, x2, target, g_final)


def _ffn_act_bwd(up, dact, w_ffn):
    s = up.shape[0]
    tm, tn = min(TM_FFN, s), TN_FFN
    nb = D_FF // tn
    n_blk = s // tm

    def body(a_ref, g_ref, ap_ref, gp_ref, an_ref, gn_ref, d_ref, dn_ref, wa_ref, wg_ref,
             dup_ref, gwa_ref, gwg_ref):
        i = pl.program_id(1)

        @pl.when(i == 0)
        def _():
            gwa_ref[...] = jnp.zeros_like(gwa_ref)
            gwg_ref[...] = jnp.zeros_like(gwg_ref)

        def ext(blk_ref, prev_ref, next_ref):
            return jnp.concatenate([jnp.where(i > 0, prev_ref[...], 0.0), blk_ref[...], next_ref[...]], axis=0)

        wa, wg = wa_ref[...], wg_ref[...]
        up_a, up_g = ext(a_ref, ap_ref, an_ref), ext(g_ref, gp_ref, gn_ref)
        u_a, u_g = _conv_taps(up_a, wa), _conv_taps(up_g, wg)
        d_e = jnp.concatenate([jnp.zeros((HALO, tn), F32), d_ref[...],
                               jnp.where(i < n_blk - 1, dn_ref[...], 0.0)], axis=0)
        sig = 1.0 / (1.0 + jnp.exp(-u_g))
        du_a = d_e * (u_g * sig)
        du_g = d_e * u_a * (sig * (1.0 + u_g * (1.0 - sig)))
        blk = slice(HALO, HALO + tm)
        dup_ref[0] = _conv_taps_t(du_a, wa)[blk].astype(BF16)
        dup_ref[1] = _conv_taps_t(du_g, wg)[blk].astype(BF16)
        for gw_ref, upv, du in ((gwa_ref, up_a[blk], du_a), (gwg_ref, up_g[blk], du_g)):
            gw_ref[0:1, :] += jnp.sum(upv * _shift_up(du, 2)[blk], axis=0, keepdims=True)
            gw_ref[1:2, :] += jnp.sum(upv * _shift_up(du, 1)[blk], axis=0, keepdims=True)
            gw_ref[2:3, :] += jnp.sum(upv * du[blk], axis=0, keepdims=True)

    blk = lambda off: pl.BlockSpec((tm, tn), lambda n, i: (i, off + n))
    prev = lambda off: pl.BlockSpec((HALO, tn), lambda n, i: (jnp.maximum(i * (tm // HALO) - 1, 0), off + n))
    nxt = lambda off: pl.BlockSpec(
        (HALO, tn), lambda n, i: (jnp.minimum((i + 1) * (tm // HALO), s // HALO - 1), off + n))
    wsp = lambda off: pl.BlockSpec((3, tn), lambda n, i: (0, off + n))
    return pl.pallas_call(
        body, name="ffn_act_bwd", grid=(nb, n_blk),
        in_specs=[blk(0), blk(nb), prev(0), prev(nb), nxt(0), nxt(nb), blk(0), nxt(0), wsp(0), wsp(nb)],
        out_specs=[pl.BlockSpec((2, tm, tn), lambda n, i: (0, i, n)), wsp(0), wsp(0)],
        out_shape=[jax.ShapeDtypeStruct((2, s, D_FF), BF16),
                   jax.ShapeDtypeStruct((3, D_FF), F32), jax.ShapeDtypeStruct((3, D_FF), F32)],
        compiler_params=_cp(("parallel", "arbitrary")))(up, up, up, up, up, up, dact, dact, w_ffn, w_ffn)


def _adamw(w, g, m, v, name):
    r, c = w.shape
    tr = next((t for t in (512, 352, 256, 128, 64, 32, 16, 8) if r > t and r % t == 0), r)

    def body(w_ref, g_ref, m_ref, v_ref, d_ref, nm_ref, nv_ref):
        gv = g_ref[...]
        m_new = ADAM_B1 * m_ref[...] + (1.0 - ADAM_B1) * gv
        v_new = ADAM_B2 * v_ref[...] + (1.0 - ADAM_B2) * (gv * gv)
        m_hat = m_new / (1.0 - ADAM_B1 ** ADAM_STEP)
        v_hat = v_new / (1.0 - ADAM_B2 ** ADAM_STEP)
        d_ref[...] = -ADAM_LR * (m_hat / (jnp.sqrt(v_hat) + ADAM_EPS) + ADAM_WD * w_ref[...])
        nm_ref[...] = m_new
        nv_ref[...] = v_new

    spec = pl.BlockSpec((tr, c), lambda i: (i, 0))
    shp = jax.ShapeDtypeStruct((r, c), F32)
    return pl.pallas_call(
        body, name=name, grid=(r // tr,), in_specs=[spec] * 4, out_specs=[spec] * 3, out_shape=[shp] * 3,
        compiler_params=_cp(("parallel",)))(w, g, m, v)


def _sum_rows_block(h):
    return h if h <= 352 else 256


def _pair_sum(view, recv, sel, name):
    n, _, h, c = view.shape
    tr = _sum_rows_block(h)

    def body(sel_ref, a_ref, b_ref, o_ref, ob_ref):
        t = a_ref[...] + b_ref[...]
        o_ref[...] = t
        ob_ref[...] = t.astype(BF16)

    blk = pl.BlockSpec((None, tr, c), lambda j, i, sel_ref: (j, i, 0))
    grid_spec = pltpu.PrefetchScalarGridSpec(
        num_scalar_prefetch=1, grid=(n, h // tr),
        in_specs=[pl.BlockSpec((None, None, tr, c), lambda j, i, sel_ref: (j, sel_ref[0], i, 0)),
                  pl.BlockSpec((None, None, tr, c), lambda j, i, sel_ref: (j, 0, i, 0))],
        out_specs=[blk, blk])
    return pl.pallas_call(
        body, name=name, grid_spec=grid_spec,
        out_shape=[jax.ShapeDtypeStruct((n, h, c), F32), jax.ShapeDtypeStruct((n, h, c), BF16)],
        compiler_params=_cp(("parallel", "parallel")))(sel, view, recv)


def _chip_sum(pair, got, sel, name):
    _, h, c = pair.shape
    tr = _sum_rows_block(h)
    nblk = h // tr

    def body(sel_ref, p_ref, g0_ref, g1_ref, g2_ref, o_ref):
        o_ref[...] = ((p_ref[...] + g0_ref[...].astype(F32)) + g1_ref[...].astype(F32)) + g2_ref[...].astype(F32)

    slot = lambda k: pl.BlockSpec((None, tr, c), lambda i, sel_ref: (k, i, 0))
    grid_spec = pltpu.PrefetchScalarGridSpec(
        num_scalar_prefetch=1, grid=(h // tr,),
        in_specs=[pl.BlockSpec((None, tr, c), lambda i, sel_ref: (sel_ref[1], i, 0)), slot(0), slot(1), slot(2)],
        out_specs=pl.BlockSpec((tr, c), lambda i, sel_ref: (sel_ref[0] * nblk + i, 0)))
    return pl.pallas_call(
        body, name=name, grid_spec=grid_spec, out_shape=jax.ShapeDtypeStruct((2 * h, c), F32),
        compiler_params=_cp(("parallel",)))(sel, pair, got, got, got)


def _place():
    return lax.axis_index("x"), lax.axis_index("y"), lax.axis_index("c")


def _other_chips(x, y):
    return [(1 - x, y), (x, 1 - y), (1 - x, 1 - y)]


def _hbm_specs(n):
    return [pl.BlockSpec(memory_space=pl.ANY)] * n


def _all_gather_weights(bigs, smalls):
    nb, ns = len(bigs), len(smalls)
    n = nb + ns

    def body(*refs):
        ins, outs = refs[:n], refs[2 * n:3 * n]
        send_sems, recv_sems = refs[3 * n:]
        x, y, c = _place()
        my_chip = 2 * x + y
        chips = _other_chips(x, y)
        sibling = (x, y, 1 - c)

        def rows(k, which):
            h = ins[k].shape[0] // 2
            return pl.ds(which * h, h)

        def copy(sem, src, dst, to):
            return pltpu.make_async_remote_copy(src_ref=src, dst_ref=dst, send_sem=send_sems.at[sem],
                                                recv_sem=recv_sems.at[sem], device_id=to, device_id_type=MESH)

        sent = []
        for k in range(nb):
            for j, (cx, cy) in enumerate(chips):
                sent.append(copy(6 * k + j, ins[k].at[rows(k, c)], outs[k].at[my_chip, rows(k, c)], (cx, cy, c)))
        for k in range(nb, n):
            for j, (cx, cy) in enumerate(chips):
                sent.append(copy(6 * nb + 3 * (k - nb) + j, ins[k], outs[k].at[my_chip], (cx, cy, c)))
        for cp in sent:
            cp.start()
        for j, (cx, cy) in enumerate(chips):
            for k in range(nb):
                landed = outs[k].at[2 * cx + cy, rows(k, c)]
                copy(6 * k + j, landed, landed, (x, y, c)).wait_recv()
                fwd = copy(6 * k + 3 + j, landed, landed, sibling)
                fwd.start()
                sent.append(fwd)
        for j, (cx, cy) in enumerate(chips):
            for k in range(nb):
                landed = outs[k].at[2 * cx + cy, rows(k, 1 - c)]
                copy(6 * k + 3 + j, landed, landed, (x, y, c)).wait_recv()
            for k in range(nb, n):
                landed = outs[k].at[2 * cx + cy]
                copy(6 * nb + 3 * (k - nb) + j, landed, landed, (x, y, c)).wait_recv()
        for cp in sent:
            cp.wait_send()

    x, y, _ = _place()
    arrays = list(bigs) + list(smalls)
    landing = [lax.dynamic_update_index_in_dim(lax.empty((N_CHIPS,) + a.shape, a.dtype), a, 2 * x + y, 0)
               for a in arrays]
    n_sems = 6 * nb + 3 * ns
    return pl.pallas_call(
        body, name="all_gather_weights",
        out_shape=[jax.ShapeDtypeStruct(b.shape, b.dtype) for b in landing],
        in_specs=_hbm_specs(2 * n), out_specs=_hbm_specs(n), input_output_aliases={n + k: k for k in range(n)},
        scratch_shapes=[pltpu.SemaphoreType.DMA((n_sems,)), pltpu.SemaphoreType.DMA((n_sems,))])(*arrays, *landing)


def _pair_exchange(views):
    n = len(views)

    def body(*refs):
        ins, outs, send_sems, recv_sems = refs[:n], refs[n:2 * n], refs[2 * n], refs[2 * n + 1]
        x, y, c = _place()
        copies = [pltpu.make_async_remote_copy(
            src_ref=ins[k].at[:, pl.ds(1 - c, 1)], dst_ref=outs[k], send_sem=send_sems.at[k],
            recv_sem=recv_sems.at[k], device_id=(x, y, 1 - c), device_id_type=MESH) for k in range(n)]
        for cp in copies:
            cp.start()
        for cp in copies:
            cp.wait()

    return pl.pallas_call(
        body, name="pair_exchange",
        out_shape=[jax.ShapeDtypeStruct((v.shape[0], 1) + v.shape[2:], v.dtype) for v in views],
        in_specs=_hbm_specs(n), out_specs=_hbm_specs(n),
        scratch_shapes=[pltpu.SemaphoreType.DMA((n,)), pltpu.SemaphoreType.DMA((n,))])(*views)


def _scatter_to_chips(parts):
    n = len(parts)

    def body(*refs):
        ins, outs, send_sems, recv_sems = refs[:n], refs[n:2 * n], refs[2 * n], refs[2 * n + 1]
        x, y, c = _place()
        copies = [pltpu.make_async_remote_copy(
            src_ref=ins[k].at[pl.ds(2 * cx + cy, 1)], dst_ref=outs[k].at[pl.ds(r, 1)], send_sem=send_sems.at[3 * k + r],
            recv_sem=recv_sems.at[3 * k + r], device_id=(cx, cy, c), device_id_type=MESH)
            for k in range(n) for r, (cx, cy) in enumerate(_other_chips(x, y))]
        for cp in copies:
            cp.start()
        for cp in copies:
            cp.wait()

    return pl.pallas_call(
        body, name="scatter_grads", out_shape=[jax.ShapeDtypeStruct((3,) + p.shape[1:], p.dtype) for p in parts],
        in_specs=_hbm_specs(n), out_specs=_hbm_specs(n),
        scratch_shapes=[pltpu.SemaphoreType.DMA((3 * n,)), pltpu.SemaphoreType.DMA((3 * n,))])(*parts)


def _join_halves(shards):
    n = len(shards)

    def body(*refs):
        ins, outs, send_sems, recv_sems = refs[:n], refs[n:2 * n], refs[2 * n], refs[2 * n + 1]
        x, y, c = _place()

        def rows(ref, which):
            h = ref.shape[0] // 2
            return ref.at[pl.ds(which * h, h)]

        sent = [pltpu.make_async_remote_copy(
            src_ref=rows(ins[k], c), dst_ref=rows(outs[k], c), send_sem=send_sems.at[k], recv_sem=recv_sems.at[k],
            device_id=(x, y, 1 - c), device_id_type=MESH) for k in range(n)]
        for cp in sent:
            cp.start()
        for k in range(n):
            pltpu.make_async_remote_copy(
                src_ref=rows(ins[k], 1 - c), dst_ref=rows(outs[k], 1 - c), send_sem=send_sems.at[k],
                recv_sem=recv_sems.at[k], device_id=(x, y, 1 - c), device_id_type=MESH).wait_recv()
        for cp in sent:
            cp.wait_send()

    return pl.pallas_call(
        body, name="half_exchange", out_shape=[jax.ShapeDtypeStruct(a.shape, a.dtype) for a in shards],
        in_specs=_hbm_specs(n), out_specs=_hbm_specs(n), input_output_aliases={k: k for k in range(n)},
        scratch_shapes=[pltpu.SemaphoreType.DMA((n,)), pltpu.SemaphoreType.DMA((n,))])(*shards)


def _all_reduce_small(packet):
    rows, width = packet.shape
    n_dev = 8

    def body(x_ref, out_ref, gath, send_sems, recv_sems):
        x, y, c = _place()
        me, sibling = (x, y, c), (x, y, 1 - c)
        chips = _other_chips(x, y)

        def slot(px, py, pc):
            return gath.at[pl.ds((4 * px + 2 * py + pc) * rows, rows), :]

        def copy(k, block, to, src=None):
            return pltpu.make_async_remote_copy(
                src_ref=slot(*block) if src is None else src, dst_ref=slot(*block), send_sem=send_sems.at[k],
                recv_sem=recv_sems.at[k], device_id=to, device_id_type=MESH)

        first = [copy(0, me, sibling, src=x_ref)]
        first += [copy(1 + j, me, (*chip, c), src=x_ref) for j, chip in enumerate(chips)]
        for cp in first:
            cp.start()
        gath[pl.ds((4 * x + 2 * y + c) * rows, rows), :] = x_ref[...]
        passed = [copy(4 + j, (*chip, c), sibling) for j, chip in enumerate(chips)]
        for j, chip in enumerate(chips):
            copy(1 + j, (*chip, c), me).wait_recv()
            passed[j].start()
        copy(0, sibling, me).wait_recv()
        for j, chip in enumerate(chips):
            copy(4 + j, (*chip, 1 - c), me).wait_recv()
        for cp in first + passed:
            cp.wait_send()
        acc = gath[0:rows, :]
        for d in range(1, n_dev):
            acc = acc + gath[d * rows:(d + 1) * rows, :]
        out_ref[...] = acc

    return pl.pallas_call(
        body, name="all_reduce_small", out_shape=jax.ShapeDtypeStruct((rows, width), F32),
        in_specs=[pl.BlockSpec(memory_space=pltpu.VMEM)], out_specs=pl.BlockSpec(memory_space=pltpu.VMEM),
        scratch_shapes=[pltpu.VMEM((n_dev * rows, width), F32), pltpu.SemaphoreType.DMA((7,)),
                        pltpu.SemaphoreType.DMA((7,))])(packet)


def _flat_rows(parts, width, row_multiple):
    flat = jnp.concatenate([p.astype(F32).reshape(-1) for p in parts])
    rows = -(-flat.shape[0] // width)
    rows = -(-rows // row_multiple) * row_multiple
    return jnp.pad(flat, (0, rows * width - flat.shape[0])).reshape(rows, width)


def _unflatten(flat2d, shapes):
    flat = flat2d.reshape(-1)
    out, off = [], 0
    for shp in shapes:
        n = 1
        for dim in shp:
            n *= dim
        out.append(flat[off:off + n].reshape(shp))
        off += n
    return out


def _reduce_scatter_grads(chip_major, names):
    x, y, c = _place()
    sel = jnp.stack([c, 2 * x + y]).astype(jnp.int32)
    views = [g.reshape(N_CHIPS, 2, g.shape[1] // 2, g.shape[2]) for g in chip_major]
    recv = _pair_exchange(views)
    pairs = [_pair_sum(v, r, sel, "pair_sum_" + nm) for v, r, nm in zip(views, recv, names)]
    got = _scatter_to_chips([pb for _, pb in pairs])
    return _join_halves([_chip_sum(p, g, sel, "chip_sum_" + nm) for (p, _), g, nm in zip(pairs, got, names)])


def kernel(x, g_mix, w_in, b_f, w_conv, g_conv_out, g_attn_out, w_o, g_ffn, w_up, w_ffn_conv, w_down, g_final, loss_target, m_g_mix, m_w_in, m_b_f, m_w_conv, m_g_conv_out, m_g_attn_out, m_w_o, m_g_ffn, m_w_up, m_w_ffn_conv, m_w_down, m_g_final, v_g_mix, v_w_in, v_b_f, v_w_conv, v_g_conv_out, v_g_attn_out, v_w_o, v_g_ffn, v_w_up, v_w_ffn_conv, v_w_down, v_g_final):
    s = x.shape[1]
    x0 = x[0]
    target = loss_target[0]
    d = D_MODEL
    x_pos, y_pos, _ = _place()
    my_chip = 2 * x_pos + y_pos

    c_in, c_o, c_up, c_down, c_conv, c_ffn = _all_gather_weights(
        [w_in[0].astype(BF16), w_o[0].astype(BF16), w_up[0].astype(BF16), w_down[0].astype(BF16)],
        [w_conv[0], w_ffn_conv[0]])
    w_in_full = jnp.concatenate([c_in[j] for j in range(N_CHIPS)], axis=1)
    w_o_full = c_o.reshape(d, d)
    w_down_full = c_down.reshape(D_FF, d)
    w_conv_full = jnp.concatenate([c_conv[j] for j in range(N_CHIPS)], axis=1)
    w_ffn_full = jnp.concatenate([c_ffn[j] for j in range(N_CHIPS)], axis=1)
    c3 = 3 * CONV_CH
    w_a, w_b = w_in_full[:, :c3], w_in_full[:, c3:c3 + 3 * ATTN_W]
    w_c = jnp.pad(w_in_full[:, c3 + 3 * ATTN_W:], ((0, 0), (0, LANES - N_HEADS)))
    w_q, w_k, w_v = (w_b[:, i * ATTN_W:(i + 1) * ATTN_W] for i in range(3))
    n_up = c_up.shape[2]
    b_pad = jnp.pad(b_f, ((0, 0), (0, LANES - N_HEADS)))

    h1 = _rms_fwd(x0, g_mix, "rms_mix")
    z_a = _mm("nn", [h1], [w_a], F32, TM_MM, 512, "in_proj_conv")
    qkv = _mm("nn", [h1], [w_b], BF16, TM_MM, 512, "in_proj_qkv")
    f_log = _mm("nn", [h1], [w_c], F32, TM_MM, LANES, "in_proj_gate")
    fb = _gate_fwd(f_log, b_pad)
    qx, kx, kxt, vx, vt, bounds = _attn_prep(qkv, fb)
    first_blk, last_blk = _key_block_ranges(bounds)
    o_attn, lse = _attn_fwd_t(qx, kx, vt, first_blk)
    mix = _mixer_fwd(z_a, o_attn, w_conv_full, g_conv_out, g_attn_out)
    x2 = _mm("nn", [mix], [w_o_full], F32, TM_MM, 512, "out_proj", add=x0)
    h2 = _rms_fwd(x2, g_ffn, "rms_ffn")
    up = _mm("nn", [h2], [c_up], F32, TM_MM, n_up, "up_proj", b_chips=True)
    act, dx3, dx3_b, loss_row, gg_final = _ffn_down_loss(up, w_ffn_full, w_down_full, x2, target,
                                                         g_final.reshape(1, d))

    dact = _mm("nt", [dx3_b], [w_down_full], F32, TM_MM, 1408, "d_act")
    gw_down = _mm_tn(act, dx3_b, 1408, 1024, "gw_down")
    dup, gwf_lin, gwf_gate = _ffn_act_bwd(up, dact, w_ffn_full)
    dh2 = _mm("nt", [(dup, j // 2, j % 2, n_up) for j in range(N_CHIPS)], [(c_up, j) for j in range(N_CHIPS)],
              F32, 512, 512, "d_h2")
    gw_up = _mm_tn(h2, dup, 1024, n_up, "gw_up", out_chips=True)
    dx2, dx2_b, gg_ffn = _rms_bwd(x2, dh2, g_ffn, dx3, "rms_ffn_bwd", True)
    dmix = _mm("nt", [dx2_b], [w_o_full], F32, TM_MM, 512, "d_mix")
    gw_o = _mm_tn(mix, dx2_b, 1024, 1024, "gw_o")
    dz_a, dox, gw_conv, gg_conv_out, gg_attn_out = _mixer_bwd(z_a, o_attn, dmix, w_conv_full, g_conv_out, g_attn_out)
    dk, dv, aux, dqt = _attn_bwd_t(qx, dox, kx, kxt, vx, lse, last_blk)
    dq, dfq = _attn_dq_finish(dqt)
    d_f = aux[:, ::HEAD_DIM] + jnp.transpose(dfq[:, :, 0:2, :], (1, 3, 0, 2)).reshape(s, N_HEADS)
    df_b, gb_f = _gate_bwd(f_log, b_pad, jnp.pad(d_f, ((0, 0), (0, LANES - N_HEADS))))
    dh1 = _mm("nt", [dz_a, dq, dk, dv, df_b], [w_a, w_q, w_k, w_v, w_c], F32, TM_MM, 512, "d_h1")
    gw_a = _mm_tn(h1, dz_a, 1024, c3, "gw_in_conv")
    gw_q = _mm_tn(h1, dq, 1024, ATTN_W, "gw_in_q")
    gw_k = _mm_tn(h1, dk, 1024, ATTN_W, "gw_in_k")
    gw_v = _mm_tn(h1, dv, 1024, ATTN_W, "gw_in_v")
    gw_c = _mm_tn(h1, df_b, 1024, LANES, "gw_in_gate")
    grad_x, gg_mix = _rms_bwd(x0, dh1, g_mix, dx2, "rms_mix_bwd", False)

    gw_in = jnp.concatenate([gw_a, gw_q, gw_k, gw_v, gw_c[:, :N_HEADS]], axis=1)
    n_in = IN_COLS // N_CHIPS
    gw_in = jnp.stack([gw_in[:, j * n_in:(j + 1) * n_in] for j in range(N_CHIPS)])
    g_w_in, g_w_o, g_w_up, g_w_down = _reduce_scatter_grads(
        [gw_in, gw_o.reshape(N_CHIPS, d // N_CHIPS, d), gw_up, gw_down.reshape(N_CHIPS, D_FF // N_CHIPS, d)],
        ["w_in", "w_o", "w_up", "w_down"])

    gw_ffn = jnp.concatenate([gwf_lin, gwf_gate], axis=1)
    small_parts = [gg_mix, gg_conv_out, gg_attn_out, gg_ffn, gg_final, gb_f[:, :N_HEADS], loss_row[:, 0:1], gw_conv,
                   gw_ffn]
    small_shapes = [a.shape for a in small_parts]
    tot = _unflatten(_all_reduce_small(_flat_rows(small_parts, d, 8)), small_shapes)
    g_g_mix, g_g_conv_out, g_g_attn_out, g_g_ffn, g_g_final, g_b_f, loss_sum, g_conv_full, g_ffn_full = tot
    loss = loss_sum[0, 0]
    g_g_final = g_g_final[0]
    g_w_conv = lax.dynamic_slice_in_dim(g_conv_full, my_chip * (CONV_CH // N_CHIPS), CONV_CH // N_CHIPS, axis=1)
    g_w_ffn = lax.dynamic_slice_in_dim(g_ffn_full, my_chip * n_up, n_up, axis=1)

    def adam_big(w, g, m, v, name):
        dl, nm, nv = _adamw(w[0], g, m[0], v[0], name)
        return dl[None], nm[None], nv[None]

    u_w_in = adam_big(w_in, g_w_in, m_w_in, v_w_in, "adam_w_in")
    u_w_o = adam_big(w_o, g_w_o, m_w_o, v_w_o, "adam_w_o")
    u_w_up = adam_big(w_up, g_w_up, m_w_up, v_w_up, "adam_w_up")
    u_w_down = adam_big(w_down, g_w_down, m_w_down, v_w_down, "adam_w_down")

    small_w = [g_mix, b_f, g_conv_out, g_attn_out, g_ffn, g_final, w_conv, w_ffn_conv]
    small_g = [g_g_mix, g_b_f, g_g_conv_out, g_g_attn_out, g_g_ffn, g_g_final, g_w_conv, g_w_ffn]
    small_m = [m_g_mix, m_b_f, m_g_conv_out, m_g_attn_out, m_g_ffn, m_g_final, m_w_conv, m_w_ffn_conv]
    small_v = [v_g_mix, v_b_f, v_g_conv_out, v_g_attn_out, v_g_ffn, v_g_final, v_w_conv, v_w_ffn_conv]
    shapes = [a.shape for a in small_w]
    pack = lambda arrs: _flat_rows(arrs, LANES, 8)
    sd, sm, sv = _adamw(pack(small_w), pack(small_g), pack(small_m), pack(small_v), "adam_small")
    sd, sm, sv = _unflatten(sd, shapes), _unflatten(sm, shapes), _unflatten(sv, shapes)
    (d_g_mix, d_b_f, d_g_conv_out, d_g_attn_out, d_g_ffn, d_g_final, d_w_conv, d_w_ffn) = sd
    (nm_g_mix, nm_b_f, nm_g_conv_out, nm_g_attn_out, nm_g_ffn, nm_g_final, nm_w_conv, nm_w_ffn) = sm
    (nv_g_mix, nv_b_f, nv_g_conv_out, nv_g_attn_out, nv_g_ffn, nv_g_final, nv_w_conv, nv_w_ffn) = sv

    grads = (g_g_mix, g_w_in[None], g_b_f, g_w_conv[None], g_g_conv_out, g_g_attn_out, g_w_o[None], g_g_ffn,
             g_w_up[None], g_w_ffn[None], g_w_down[None], g_g_final)
    deltas = (d_g_mix, u_w_in[0], d_b_f, d_w_conv, d_g_conv_out, d_g_attn_out, u_w_o[0], d_g_ffn, u_w_up[0],
              d_w_ffn, u_w_down[0], d_g_final)
    new_m = (nm_g_mix, u_w_in[1], nm_b_f, nm_w_conv, nm_g_conv_out, nm_g_attn_out, u_w_o[1], nm_g_ffn, u_w_up[1],
             nm_w_ffn, u_w_down[1], nm_g_final)
    new_v = (nv_g_mix, u_w_in[2], nv_b_f, nv_w_conv, nv_g_conv_out, nv_g_attn_out, u_w_o[2], nv_g_ffn, u_w_up[2],
             nv_w_ffn, u_w_down[2], nv_g_final)
    return (loss, grad_x[None], *grads, *deltas, *new_m, *new_v)
```

```python
import functools

import jax
import jax.numpy as jnp
from jax import lax
from jax.experimental import pallas as pl
from jax.experimental.pallas import tpu as pltpu

F32, BF16 = jnp.float32, jnp.bfloat16
MESH = pl.DeviceIdType.MESH

D_MODEL = 1024
CONV_CH = 512
ATTN_W = 512
N_HEADS = 8
HEAD_DIM = 64
N_PAIRS = N_HEADS // 2
D_FF = 2816
IN_COLS = 3 * CONV_CH + 3 * ATTN_W + N_HEADS
EPS = 1e-6
Q_SCALE = 0.125
EXP_ZERO = 104.0
N_CHIPS = 4
LANES = 128
HALO = 8

ADAM_LR, ADAM_B1, ADAM_B2, ADAM_EPS, ADAM_WD, ADAM_STEP = 0.001, 0.9, 0.999, 1e-08, 0.01, 10

TM_ROWS = 512
TM_MM = 1024
TK_TN = 512
TQ = 512
ROW_CHUNK = 32
TM_FFN = 1024
TN_FFN = 256
VMEM_LIMIT = 52 * 2**20


def _cp(sem, vmem=VMEM_LIMIT):
    return pltpu.CompilerParams(dimension_semantics=sem, vmem_limit_bytes=vmem)


def _bf(a):
    return a if a.dtype == BF16 else a.astype(BF16)


def _mm(mode, a_list, b_list, out_dtype, tm, tn, name, add=None, b_chips=False):
    n_p = len(a_list)
    a0 = a_list[0]
    m_dim = a0[0].shape[1] if isinstance(a0, tuple) else a0.shape[0]
    b0 = b_list[0]
    if b_chips:
        n_dim = b0.shape[0] * b0.shape[2]
        assert tn == b0.shape[2] and mode == "nn"
    else:
        b0 = b0[0][b0[1]] if isinstance(b0, tuple) else b0
        n_dim = b0.shape[1 if mode == "nn" else 0]
    tm, tn = min(tm, m_dim), min(tn, n_dim)
    assert m_dim % tm == 0 and n_dim % tn == 0
    dims = (((1,), (0,)), ((), ())) if mode == "nn" else (((1,), (1,)), ((), ()))
    in_specs, args = [], []
    for a in a_list:
        if isinstance(a, tuple):
            arr, lead, col, width = a
            in_specs.append(pl.BlockSpec((None, tm, width), lambda m, n, lead=lead, col=col: (lead, m, col)))
        else:
            arr = a
            in_specs.append(pl.BlockSpec((tm, a.shape[1]), lambda m, n: (m, 0)))
        args.append(arr)
    for b in b_list:
        if b_chips:
            arr = b
            in_specs.append(pl.BlockSpec((None, b.shape[1], tn), lambda m, n: (n, 0, 0)))
        elif isinstance(b, tuple):
            arr, lead = b
            if mode == "nn":
                in_specs.append(pl.BlockSpec((None, arr.shape[1], tn), lambda m, n, lead=lead: (lead, 0, n)))
            else:
                in_specs.append(pl.BlockSpec((None, tn, arr.shape[2]), lambda m, n, lead=lead: (lead, n, 0)))
        elif mode == "nn":
            arr = b
            in_specs.append(pl.BlockSpec((b.shape[0], tn), lambda m, n: (0, n)))
        else:
            arr = b
            in_specs.append(pl.BlockSpec((tn, b.shape[1]), lambda m, n: (n, 0)))
        args.append(arr)
    if add is not None:
        in_specs.append(pl.BlockSpec((tm, tn), lambda m, n: (m, n)))
        args.append(add)

    def body(*refs):
        o_ref = refs[-1]
        acc = None
        for i in range(n_p):
            d = lax.dot_general(_bf(refs[i][...]), _bf(refs[n_p + i][...]), dims,
                                preferred_element_type=F32)
            acc = d if acc is None else acc + d
        if add is not None:
            acc = refs[2 * n_p][...] + acc
        o_ref[...] = acc.astype(out_dtype)

    return pl.pallas_call(
        body, name=name, grid=(m_dim // tm, n_dim // tn), in_specs=in_specs,
        out_specs=pl.BlockSpec((tm, tn), lambda m, n: (m, n)),
        out_shape=jax.ShapeDtypeStruct((m_dim, n_dim), out_dtype),
        compiler_params=_cp(("parallel", "parallel")))(*args)


def _mm_tn(a, b, tm, tn, name, out_chips=False):
    k_dim, m_dim = a.shape
    n_dim = b.shape[-1] * (b.shape[0] if b.ndim == 3 else 1)
    tm, tn, tk = min(tm, m_dim), min(tn, b.shape[-1]), min(TK_TN, k_dim)
    assert m_dim % tm == 0 and b.shape[-1] % tn == 0 and k_dim % tk == 0
    per = b.shape[-1] // tn
    if b.ndim == 3:
        b_spec = pl.BlockSpec((None, tk, tn), lambda m, n, k: (n // per, k, n % per))
    else:
        b_spec = pl.BlockSpec((tk, tn), lambda m, n, k: (k, n))

    def body(a_ref, b_ref, o_ref):
        @pl.when(pl.program_id(2) == 0)
        def _():
            o_ref[...] = jnp.zeros_like(o_ref)
        o_ref[...] += lax.dot_general(_bf(a_ref[...]), _bf(b_ref[...]), (((0,), (0,)), ((), ())),
                                      preferred_element_type=F32)

    return pl.pallas_call(
        body, name=name, grid=(m_dim // tm, n_dim // tn, k_dim // tk),
        in_specs=[pl.BlockSpec((tk, tm), lambda m, n, k: (k, m)), b_spec],
        out_specs=(pl.BlockSpec((None, tm, tn), lambda m, n, k: (n, m, 0)) if out_chips
                   else pl.BlockSpec((tm, tn), lambda m, n, k: (m, n))),
        out_shape=jax.ShapeDtypeStruct((n_dim // tn, m_dim, tn) if out_chips else (m_dim, n_dim), F32),
        compiler_params=_cp(("parallel", "parallel", "arbitrary")))(a, b)


def _rstd(x):
    return lax.rsqrt(jnp.mean(x * x, axis=-1, keepdims=True) + EPS)


def _rms_fwd(x, g, name):
    s, d = x.shape
    tm = min(TM_ROWS, s)

    def body(x_ref, g_ref, h_ref):
        xv = x_ref[...]
        h_ref[...] = (xv * _rstd(xv) * g_ref[...]).astype(BF16)

    return pl.pallas_call(
        body, name=name, grid=(s // tm,),
        in_specs=[pl.BlockSpec((tm, d), lambda i: (i, 0)), pl.BlockSpec((1, d), lambda i: (0, 0))],
        out_specs=pl.BlockSpec((tm, d), lambda i: (i, 0)),
        out_shape=jax.ShapeDtypeStruct((s, d), BF16), compiler_params=_cp(("parallel",)))(x, g)


def _rms_bwd(x, dh, g, dres, name, with_bf16):
    s, d = x.shape
    tm = min(TM_ROWS, s)

    def body(x_ref, dh_ref, g_ref, dres_ref, dx_ref, *rest):
        gg_ref = rest[-1]

        @pl.when(pl.program_id(0) == 0)
        def _():
            gg_ref[...] = jnp.zeros_like(gg_ref)

        xv = x_ref[...]
        xn = xv * _rstd(xv)
        dhv = dh_ref[...]
        gg_ref[...] += jnp.sum(dhv * xn, axis=0, keepdims=True)
        t = dhv * g_ref[...]
        dx = dres_ref[...] + _rstd(xv) * (t - xn * jnp.mean(t * xn, axis=-1, keepdims=True))
        dx_ref[...] = dx
        if with_bf16:
            rest[0][...] = dx.astype(BF16)

    row = pl.BlockSpec((tm, d), lambda i: (i, 0))
    vec = pl.BlockSpec((1, d), lambda i: (0, 0))
    out_specs = [row] + ([row] if with_bf16 else []) + [vec]
    out_shape = ([jax.ShapeDtypeStruct((s, d), F32)] + ([jax.ShapeDtypeStruct((s, d), BF16)] if with_bf16 else [])
                 + [jax.ShapeDtypeStruct((1, d), F32)])
    return pl.pallas_call(
        body, name=name, grid=(s // tm,), in_specs=[row, row, vec, row], out_specs=out_specs, out_shape=out_shape,
        compiler_params=_cp(("arbitrary",)))(x, dh, g, dres)


def _loss_head(x3, target, g):
    s, d = x3.shape
    tm = min(TM_ROWS, s)

    def body(x_ref, t_ref, g_ref, dx_ref, dxb_ref, loss_ref, gg_ref):
        @pl.when(pl.program_id(0) == 0)
        def _():
            gg_ref[...] = jnp.zeros_like(gg_ref)
            loss_ref[...] = jnp.zeros_like(loss_ref)

        xv = x_ref[...]
        r = _rstd(xv)
        xn = xv * r
        gv = g_ref[...]
        err = xn * gv - t_ref[...]
        loss_ref[...] += 0.5 * jnp.sum(jnp.mean(err * err, axis=-1, keepdims=True), axis=0, keepdims=True)
        dy = err * (1.0 / d)
        gg_ref[...] += jnp.sum(dy * xn, axis=0, keepdims=True)
        t = dy * gv
        dx = r * (t - xn * jnp.mean(t * xn, axis=-1, keepdims=True))
        dx_ref[...] = dx
        dxb_ref[...] = dx.astype(BF16)

    row = pl.BlockSpec((tm, d), lambda i: (i, 0))
    vec = pl.BlockSpec((1, d), lambda i: (0, 0))
    return pl.pallas_call(
        body, name="loss_head", grid=(s // tm,), in_specs=[row, row, vec],
        out_specs=[row, row, pl.BlockSpec((1, LANES), lambda i: (0, 0)), vec],
        out_shape=[jax.ShapeDtypeStruct((s, d), F32), jax.ShapeDtypeStruct((s, d), BF16),
                   jax.ShapeDtypeStruct((1, LANES), F32), jax.ShapeDtypeStruct((1, d), F32)],
        compiler_params=_cp(("arbitrary",)))(x3, target, g)


def _prev_halo_spec(tm, width, col):
    return pl.BlockSpec((HALO, width), lambda i, *_: (jnp.maximum(i * (tm // HALO) - 1, 0), col))


def _next_halo_spec(tm, width, col, s):
    return pl.BlockSpec((HALO, width), lambda i, *_: (jnp.minimum((i + 1) * (tm // HALO), s // HALO - 1), col))


def _shift_down(x, k):
    return pltpu.roll(x, k, 0)


def _shift_up(x, k):
    return pltpu.roll(x, x.shape[0] - k, 0)


def _conv_taps(x_ext, w):
    return w[0:1, :] * _shift_down(x_ext, 2) + w[1:2, :] * _shift_down(x_ext, 1) + w[2:3, :] * x_ext


def _conv_taps_t(d_ext, w):
    return w[2:3, :] * d_ext + w[1:2, :] * _shift_up(d_ext, 1) + w[0:1, :] * _shift_up(d_ext, 2)


def _mixer_fwd(z_a, o_attn, w_conv, g_conv_out, g_attn_out):
    s = z_a.shape[0]
    c = CONV_CH
    tm = min(TM_ROWS, s)

    def body(gb_ref, gc_ref, xc_ref, gcp_ref, xcp_ref, o_ref, w_ref, gco_ref, gao_ref, mix_ref):
        i = pl.program_id(0)
        cx = gc_ref[...] * xc_ref[...]
        cx_prev = jnp.where(i > 0, gcp_ref[...] * xcp_ref[...], 0.0)
        conv = _conv_taps(jnp.concatenate([cx_prev, cx], axis=0), w_ref[...])[HALO:]
        y = gb_ref[...] * conv
        mix_ref[:, 0:c] = (y * _rstd(y) * gco_ref[...]).astype(BF16)
        o = o_ref[...]
        mix_ref[:, c:2 * c] = (o * _rstd(o) * gao_ref[...]).astype(BF16)

    col = lambda j: pl.BlockSpec((tm, c), lambda i: (i, j))
    vec = pl.BlockSpec((1, c), lambda i: (0, 0))
    return pl.pallas_call(
        body, name="mixer_fwd", grid=(s // tm,),
        in_specs=[col(0), col(1), col(2), _prev_halo_spec(tm, c, 1), _prev_halo_spec(tm, c, 2), col(0),
                  pl.BlockSpec((3, c), lambda i: (0, 0)), vec, vec],
        out_specs=pl.BlockSpec((tm, 2 * c), lambda i: (i, 0)),
        out_shape=jax.ShapeDtypeStruct((s, 2 * c), BF16),
        compiler_params=_cp(("parallel",)))(z_a, z_a, z_a, z_a, z_a, o_attn, w_conv, g_conv_out, g_attn_out)


def _mixer_bwd(z_a, o_attn, dmix, w_conv, g_conv_out, g_attn_out):
    s = z_a.shape[0]
    c = CONV_CH
    tm = min(TM_ROWS, s)
    n_blk = s // tm

    def body(gb_ref, gc_ref, xc_ref, gcp_ref, xcp_ref, gbn_ref, gcn_ref, xcn_ref, o_ref, dnc_ref, dncn_ref, dna_ref,
             w_ref, gco_ref, gao_ref, dz_ref, dox_ref, gw_ref, ggco_ref, ggao_ref):
        i = pl.program_id(0)

        @pl.when(i == 0)
        def _():
            gw_ref[...] = jnp.zeros_like(gw_ref)
            ggco_ref[...] = jnp.zeros_like(ggco_ref)
            ggao_ref[...] = jnp.zeros_like(ggao_ref)

        w = w_ref[...]
        zeros = jnp.zeros((HALO, c), F32)
        gb_e = jnp.concatenate([zeros, gb_ref[...], gbn_ref[...]], axis=0)
        cx_prev = jnp.where(i > 0, gcp_ref[...] * xcp_ref[...], 0.0)
        gc_e = jnp.concatenate([zeros, gc_ref[...], gcn_ref[...]], axis=0)
        xc_e = jnp.concatenate([zeros, xc_ref[...], xcn_ref[...]], axis=0)
        cx_e = jnp.concatenate([cx_prev, gc_ref[...] * xc_ref[...], gcn_ref[...] * xcn_ref[...]], axis=0)
        dn_next = jnp.where(i < n_blk - 1, dncn_ref[...], 0.0)
        dn_e = jnp.concatenate([zeros, dnc_ref[...], dn_next], axis=0)

        conv_e = _conv_taps(cx_e, w)
        y_e = gb_e * conv_e
        r_e = _rstd(y_e)
        yn_e = y_e * r_e
        t_e = dn_e * gco_ref[...]
        dy_e = r_e * (t_e - yn_e * jnp.mean(t_e * yn_e, axis=-1, keepdims=True))
        dconv_e = dy_e * gb_e
        dcx_e = _conv_taps_t(dconv_e, w)
        blk = slice(HALO, HALO + tm)
        dz_ref[:, 0:c] = (dy_e * conv_e)[blk].astype(BF16)
        dz_ref[:, c:2 * c] = (dcx_e * xc_e)[blk].astype(BF16)
        dz_ref[:, 2 * c:3 * c] = (dcx_e * gc_e)[blk].astype(BF16)
        ggco_ref[...] += jnp.sum((dn_e * yn_e)[blk], axis=0, keepdims=True)
        dconv = dconv_e[blk]
        gw_ref[0:1, :] += jnp.sum(dconv * _shift_down(cx_e, 2)[blk], axis=0, keepdims=True)
        gw_ref[1:2, :] += jnp.sum(dconv * _shift_down(cx_e, 1)[blk], axis=0, keepdims=True)
        gw_ref[2:3, :] += jnp.sum(dconv * cx_e[blk], axis=0, keepdims=True)

        o = o_ref[...]
        ra = _rstd(o)
        on = o * ra
        dna = dna_ref[...]
        ggao_ref[...] += jnp.sum(dna * on, axis=0, keepdims=True)
        ta = dna * gao_ref[...]
        do = ra * (ta - on * jnp.mean(ta * on, axis=-1, keepdims=True))
        prod = do * o
        lane = lax.broadcasted_iota(jnp.int32, (tm, LANES), 1)
        head_a = lane < HEAD_DIM
        for p in range(N_PAIRS):
            cols = slice(p * LANES, (p + 1) * LANES)
            pb, dob = prod[:, cols], do[:, cols]
            for hh in range(2):
                sel = head_a if hh == 0 else jnp.logical_not(head_a)
                delta = jnp.sum(jnp.where(sel, pb, 0.0), axis=-1, keepdims=True)
                neg3 = _split3(-delta)
                do_h = pltpu.roll(dob, HEAD_DIM, 1) if hh else dob
                dox_ref[2 * p + hh] = _aug(do_h, lane, neg3).astype(BF16)

    col = lambda j: pl.BlockSpec((tm, c), lambda i: (i, j))
    vec = pl.BlockSpec((1, c), lambda i: (0, 0))
    w3 = pl.BlockSpec((3, c), lambda i: (0, 0))
    return pl.pallas_call(
        body, name="mixer_bwd", grid=(n_blk,),
        in_specs=[col(0), col(1), col(2), _prev_halo_spec(tm, c, 1), _prev_halo_spec(tm, c, 2),
                  _next_halo_spec(tm, c, 0, s), _next_halo_spec(tm, c, 1, s), _next_halo_spec(tm, c, 2, s),
                  col(0), col(0), _next_halo_spec(tm, c, 0, s), col(1), w3, vec, vec],
        out_specs=[pl.BlockSpec((tm, 3 * c), lambda i: (i, 0)),
                   pl.BlockSpec((N_HEADS, tm, LANES), lambda i: (0, i, 0)), w3, vec, vec],
        out_shape=[jax.ShapeDtypeStruct((s, 3 * c), BF16), jax.ShapeDtypeStruct((N_HEADS, s, LANES), BF16),
                   jax.ShapeDtypeStruct((3, c), F32), jax.ShapeDtypeStruct((1, c), F32),
                   jax.ShapeDtypeStruct((1, c), F32)],
        compiler_params=_cp(("arbitrary",)))(
            z_a, z_a, z_a, z_a, z_a, z_a, z_a, z_a, o_attn, dmix, dmix, dmix, w_conv, g_conv_out, g_attn_out)


def _gate_fwd(f, b_pad):
    s = f.shape[0]
    tm = min(TQ, s)

    def body(f_ref, b_ref, fb_ref, carry):
        @pl.when(pl.program_id(0) == 0)
        def _():
            carry[...] = jnp.zeros_like(carry)

        z = f_ref[...] + b_ref[...]
        x = jnp.minimum(z, 0.0) - jnp.log1p(jnp.exp(-jnp.abs(z)))
        row = lax.broadcasted_iota(jnp.int32, (tm, LANES), 0)
        sh = 1
        while sh < tm:
            x = x + jnp.where(row >= sh, _shift_down(x, sh), 0.0)
            sh *= 2
        x = x + carry[0:1, :]
        carry[...] = jnp.broadcast_to(x[tm - 1:tm, :], carry.shape)
        head_a = lax.broadcasted_iota(jnp.int32, (tm, LANES), 1) < HEAD_DIM
        for p in range(N_PAIRS):
            fa = jnp.broadcast_to(x[:, 2 * p:2 * p + 1], (tm, LANES))
            fbv = jnp.broadcast_to(x[:, 2 * p + 1:2 * p + 2], (tm, LANES))
            fb_ref[:, p * LANES:(p + 1) * LANES] = jnp.where(head_a, fa, fbv)

    return pl.pallas_call(
        body, name="gate_fwd", grid=(s // tm,),
        in_specs=[pl.BlockSpec((tm, LANES), lambda i: (i, 0)), pl.BlockSpec((1, LANES), lambda i: (0, 0))],
        out_specs=pl.BlockSpec((tm, N_PAIRS * LANES), lambda i: (i, 0)),
        out_shape=jax.ShapeDtypeStruct((s, N_PAIRS * LANES), F32),
        scratch_shapes=[pltpu.VMEM((HALO, LANES), F32)],
        compiler_params=_cp(("arbitrary",)))(f, b_pad)


def _gate_bwd(f, b_pad, d_f):
    s = f.shape[0]
    tm = min(TQ, s)
    n_blk = s // tm

    def body(f_ref, b_ref, d_ref, df_ref, gb_ref, carry):
        @pl.when(pl.program_id(0) == 0)
        def _():
            carry[...] = jnp.zeros_like(carry)
            gb_ref[...] = jnp.zeros_like(gb_ref)

        x = d_ref[...]
        row = lax.broadcasted_iota(jnp.int32, (tm, LANES), 0)
        sh = 1
        while sh < tm:
            x = x + jnp.where(row < tm - sh, _shift_up(x, sh), 0.0)
            sh *= 2
        x = x + carry[0:1, :]
        carry[...] = jnp.broadcast_to(x[0:1, :], carry.shape)
        z = f_ref[...] + b_ref[...]
        d = x * (1.0 / (1.0 + jnp.exp(z)))
        df_ref[...] = d.astype(BF16)
        gb_ref[...] += jnp.sum(d, axis=0, keepdims=True)

    rev = pl.BlockSpec((tm, LANES), lambda i: (n_blk - 1 - i, 0))
    vec = pl.BlockSpec((1, LANES), lambda i: (0, 0))
    return pl.pallas_call(
        body, name="gate_bwd", grid=(n_blk,), in_specs=[rev, vec, rev], out_specs=[rev, vec],
        out_shape=[jax.ShapeDtypeStruct((s, LANES), BF16), jax.ShapeDtypeStruct((1, LANES), F32)],
        scratch_shapes=[pltpu.VMEM((HALO, LANES), F32)],
        compiler_params=_cp(("arbitrary",)))(f, b_pad, d_f)


_NT = (((1,), (1,)), ((), ()))
_NN = (((1,), (0,)), ((), ()))
_TN = (((0,), (0,)), ((), ()))


def _head_masks(shape):
    lane = lax.broadcasted_iota(jnp.int32, shape, len(shape) - 1)
    return lane < HEAD_DIM


def _pick_row(ft, h):
    rows = lax.broadcasted_iota(jnp.int32, ft.shape, 0)
    return jnp.sum(jnp.where(rows == h, ft, 0.0), axis=0, keepdims=True)


def _prune_bounds(qkv, fb):
    s = qkv.shape[0]
    tq = min(TQ, s)

    def body(q_ref, k_ref, fb_ref, out_ref):
        head_a = _head_masks((tq, LANES))
        lane = lax.broadcasted_iota(jnp.int32, (HALO, LANES), 1)
        acc = jnp.zeros((HALO, LANES), F32)
        for p in range(N_PAIRS):
            cols = slice(p * LANES, (p + 1) * LANES)
            q2 = q_ref[:, cols].astype(F32) * Q_SCALE
            k2 = k_ref[:, cols].astype(F32)
            f2 = fb_ref[:, cols]
            for hh in range(2):
                sel = head_a if hh == 0 else jnp.logical_not(head_a)
                qn = jnp.sqrt(jnp.sum(jnp.where(sel, q2 * q2, 0.0), axis=-1, keepdims=True))
                kn = jnp.sqrt(jnp.sum(jnp.where(sel, k2 * k2, 0.0), axis=-1, keepdims=True))
                f = f2[:, hh * HEAD_DIM:hh * HEAD_DIM + 1]
                h = 2 * p + hh
                vals = (jnp.max(qn, axis=0, keepdims=True), jnp.max(kn, axis=0, keepdims=True),
                        jnp.max(qn * kn + f, axis=0, keepdims=True), f[tq - 1:tq, :])
                for slot, v in enumerate(vals):
                    acc = jnp.where(lane == slot * N_HEADS + h, v, acc)
        out_ref[0] = acc

    blk = lambda j: pl.BlockSpec((tq, ATTN_W), lambda i: (i, j))
    return pl.pallas_call(
        body, name="prune_bounds", grid=(s // tq,), in_specs=[blk(0), blk(1), blk(0)],
        out_specs=pl.BlockSpec((1, HALO, LANES), lambda i: (i, 0, 0)),
        out_shape=jax.ShapeDtypeStruct((s // tq, HALO, LANES), F32),
        compiler_params=_cp(("parallel",)))(qkv, qkv, fb)


def _first_key_blocks(qkv, fb):
    t = _prune_bounds(qkv, fb)[:, 0, :]
    nh = N_HEADS
    a, b, c, e = t[:, 0:nh], t[:, nh:2 * nh], t[:, 2 * nh:3 * nh], t[:, 3 * nh:4 * nh]
    bound = a[:, None, :] * b[None, :, :] * 1.001 + c[:, None, :] - e[None, :, :]
    n_q = t.shape[0]
    idx = jnp.arange(n_q)
    need = jnp.logical_not(bound < -(EXP_ZERO + 2.0)) | (idx[None, :, None] >= idx[:, None, None])
    first = jnp.argmax(need, axis=1).astype(jnp.int32)
    return jnp.min(first.reshape(n_q, N_PAIRS, 2), axis=-1).T.reshape(-1)


def _attn_fwd(qkv, fb, ft, first_blk):
    s = qkv.shape[0]
    tq = min(TQ, s)
    n_q = s // tq
    neg = -1e30

    rc = min(ROW_CHUNK, tq)

    def body(first_ref, q_ref, k_ref, v_ref, fb_ref, ft_ref, o_ref, g_ref,
             s_a, s_b, p_ab, m_a, l_a, m_b, l_b, alpha2, acc, pmax_a, psum_a, pmax_b, psum_b):
        p = pl.program_id(0)
        i = pl.program_id(1)
        head_a = _head_masks((tq, LANES))
        q2 = q_ref[...] * Q_SCALE
        zero = jnp.zeros_like(q2)
        q_a, q_b = jnp.where(head_a, q2, zero), jnp.where(head_a, zero, q2)
        for m_scr, l_scr in ((m_a, l_a), (m_b, l_b)):
            m_scr[...] = jnp.full(m_scr.shape, neg, F32)
            l_scr[...] = jnp.zeros(l_scr.shape, F32)
        acc[...] = jnp.zeros(acc.shape, F32)

        def step(kb, masked):
            rows_k = pl.ds(pl.multiple_of(kb * tq, tq), tq)
            k2, v2 = k_ref[rows_k, :], v_ref[rows_k, :]
            ftv = ft_ref[kb]
            fk = (_pick_row(ftv, 2 * p), _pick_row(ftv, 2 * p + 1))
            s_a[...] = lax.dot_general(q_a, k2, _NT, preferred_element_type=F32)
            s_b[...] = lax.dot_general(q_b, k2, _NT, preferred_element_type=F32)
            alphas = []
            for hh, (s_scr, m_scr, l_scr, pmax, psum) in enumerate(((s_a, m_a, l_a, pmax_a, psum_a),
                                                                     (s_b, m_b, l_b, pmax_b, psum_b))):
                fq = fb_ref[:, hh * HEAD_DIM:hh * HEAD_DIM + 1]

                def shifted(r, ncol, add=None):
                    rows = slice(r * rc, (r + 1) * rc)
                    t = s_scr[rows, 0:ncol]
                    if add is not None:
                        t = t + add[rows, :]
                    t = t - fk[hh][:, 0:ncol]
                    if masked:
                        col_id = lax.broadcasted_iota(jnp.int32, (rc, ncol), 1)
                        row_id = lax.broadcasted_iota(jnp.int32, (rc, ncol), 0) + r * rc
                        t = jnp.where(col_id <= row_id, t, -jnp.inf)
                    return rows, t

                def lane_blocks(t, op):
                    out = t[:, 0:LANES]
                    for cb in range(1, t.shape[1] // LANES):
                        out = op(out, t[:, cb * LANES:(cb + 1) * LANES])
                    return out

                ncols = [min(tq, -(-((r + 1) * rc) // LANES) * LANES) if masked else tq for r in range(tq // rc)]
                for r, ncol in enumerate(ncols):
                    rows, t = shifted(r, ncol)
                    pmax[rows, :] = lane_blocks(t, jnp.maximum)
                m_old = m_scr[...]
                m_new = jnp.maximum(m_old, jnp.max(pmax[...], axis=-1, keepdims=True) + fq)
                alpha = jnp.exp(m_old - m_new)
                m_scr[...] = m_new
                shift = fq - m_new
                for r, ncol in enumerate(ncols):
                    rows, t = shifted(r, ncol, shift)
                    pr = jnp.exp(t)
                    psum[rows, :] = lane_blocks(pr, jnp.add)
                    p_ab[rows, hh * tq:hh * tq + ncol] = pr.astype(BF16)
                    if ncol < tq:
                        p_ab[rows, hh * tq + ncol:(hh + 1) * tq] = jnp.zeros((rc, tq - ncol), BF16)
                l_scr[...] = alpha * l_scr[...] + jnp.sum(psum[...], axis=-1, keepdims=True)
                alphas.append(alpha)
            alpha2[...] = jnp.where(head_a, alphas[0], alphas[1])
            zv = jnp.zeros_like(v2)
            head_k = _head_masks(v2.shape)
            vv = jnp.concatenate([jnp.where(head_k, v2, zv), jnp.where(head_k, zv, v2)], axis=0)
            acc[...] = acc[...] * alpha2[...] + lax.dot_general(p_ab[...], vv, _NN, preferred_element_type=F32)

        def unmasked(kb, carry):
            step(kb, False)
            return carry

        lax.fori_loop(first_ref[p * n_q + i], i, unmasked, 0)
        step(i, True)
        fq2 = fb_ref[...]
        o_ref[...] = acc[...] / jnp.where(head_a, l_a[...], l_b[...])
        g_ref[...] = fq2 - jnp.where(head_a, m_a[...] + jnp.log(l_a[...]), m_b[...] + jnp.log(l_b[...]))

    qblk = lambda off: pl.BlockSpec((tq, LANES), lambda p, i, first: (i, off + p))
    full = lambda off: pl.BlockSpec((s, LANES), lambda p, i, first: (0, off + p))
    col_scr = pltpu.VMEM((tq, 1), F32)
    grid_spec = pltpu.PrefetchScalarGridSpec(
        num_scalar_prefetch=1, grid=(N_PAIRS, n_q),
        in_specs=[qblk(0), full(N_PAIRS), full(2 * N_PAIRS), qblk(0),
                  pl.BlockSpec((n_q, N_HEADS, tq), lambda p, i, first: (0, 0, 0))],
        out_specs=[qblk(0), qblk(0)],
        scratch_shapes=[pltpu.VMEM((tq, tq), F32), pltpu.VMEM((tq, tq), F32), pltpu.VMEM((tq, 2 * tq), BF16),
                        col_scr, col_scr, col_scr, col_scr] + [pltpu.VMEM((tq, LANES), F32)] * 6)
    return pl.pallas_call(
        body, name="attn_fwd", grid_spec=grid_spec,
        out_shape=[jax.ShapeDtypeStruct((s, ATTN_W), F32), jax.ShapeDtypeStruct((s, ATTN_W), F32)],
        compiler_params=_cp(("parallel", "arbitrary")))(first_blk, qkv, qkv, qkv, fb, ft)


def _attn_bwd(qkv, do, g, delta, ft, first_blk):
    s = qkv.shape[0]
    tq = min(TQ, s)
    n_q = s // tq

    def body(first_ref, q_ref, do_ref, g_ref, dl_ref, k_ref, v_ref, ft_ref, dq_ref, dfq_ref, dk_ref, dv_ref,
             dfk_ref):
        p = pl.program_id(0)
        i = pl.program_id(1)

        @pl.when(i == 0)
        def _():
            dk_ref[...] = jnp.zeros_like(dk_ref)
            dv_ref[...] = jnp.zeros_like(dv_ref)
            dfk_ref[...] = jnp.zeros_like(dfk_ref)

        head_a = _head_masks((tq, LANES))
        q2 = q_ref[...] * Q_SCALE
        do2 = do_ref[...]
        zero = jnp.zeros_like(q2)
        q_a, q_b = jnp.where(head_a, q2, zero), jnp.where(head_a, zero, q2)
        do_a, do_b = jnp.where(head_a, do2, zero), jnp.where(head_a, zero, do2)
        g2, dl2 = g_ref[...], dl_ref[...]
        g_a, g_b = g2[:, 0:1], g2[:, HEAD_DIM:HEAD_DIM + 1]
        dl_a, dl_b = dl2[:, 0:1], dl2[:, HEAD_DIM:HEAD_DIM + 1]
        causal = lax.broadcasted_iota(jnp.int32, (tq, tq), 1) <= lax.broadcasted_iota(jnp.int32, (tq, tq), 0)
        rows8 = lax.broadcasted_iota(jnp.int32, (N_HEADS, tq), 0)

        def one_head(q_h, do_h, g_h, dl_h, fk_h, k2, v2, masked):
            sc = (lax.dot_general(q_h, k2, _NT, preferred_element_type=F32) + g_h) - fk_h
            pr = jnp.exp(sc)
            if masked:
                pr = jnp.where(causal, pr, 0.0)
            dp = lax.dot_general(do_h, v2, _NT, preferred_element_type=F32)
            ds = pr * (dp - dl_h)
            return (pr.astype(BF16), ds.astype(BF16), jnp.sum(ds, axis=0, keepdims=True),
                    jnp.sum(ds, axis=1, keepdims=True))

        def step(j, carry, masked):
            dq, r_a, r_b = carry
            rows = pl.ds(pl.multiple_of(j * tq, tq), tq)
            k2, v2 = k_ref[rows, :], v_ref[rows, :]
            k_a, k_b = jnp.where(head_a, k2, zero), jnp.where(head_a, zero, k2)
            ftv = ft_ref[j]
            p_a, ds_a, c_a, s_a = one_head(q_a, do_a, g_a, dl_a, _pick_row(ftv, 2 * p), k2, v2, masked)
            p_b, ds_b, c_b, s_b = one_head(q_b, do_b, g_b, dl_b, _pick_row(ftv, 2 * p + 1), k2, v2, masked)
            dv_ref[rows, :] += (lax.dot_general(p_a, do_a, _TN, preferred_element_type=F32)
                                + lax.dot_general(p_b, do_b, _TN, preferred_element_type=F32))
            dk_ref[rows, :] += (lax.dot_general(ds_a, q_a, _TN, preferred_element_type=F32)
                                + lax.dot_general(ds_b, q_b, _TN, preferred_element_type=F32))
            dfk_ref[0, j] += jnp.where(rows8 == 0, -c_a, jnp.where(rows8 == 1, -c_b, 0.0))
            dq = dq + (lax.dot_general(ds_a, k_a, _NN, preferred_element_type=F32)
                       + lax.dot_general(ds_b, k_b, _NN, preferred_element_type=F32))
            return dq, r_a + s_a, r_b + s_b

        zcol = jnp.zeros((tq, 1), F32)
        carry = lax.fori_loop(first_ref[p * n_q + i], i, lambda j, cr: step(j, cr, False),
                              (jnp.zeros((tq, LANES), F32), zcol, zcol))
        dq, r_a, r_b = step(i, carry, True)
        dq_ref[...] = (dq * Q_SCALE).astype(BF16)
        dfq_ref[...] = jnp.where(head_a, r_a, r_b)

    qblk = lambda off: pl.BlockSpec((tq, LANES), lambda p, i, first: (i, off + p))
    full = lambda off: pl.BlockSpec((s, LANES), lambda p, i, first: (0, off + p))
    grid_spec = pltpu.PrefetchScalarGridSpec(
        num_scalar_prefetch=1, grid=(N_PAIRS, n_q),
        in_specs=[qblk(0), qblk(0), qblk(0), qblk(0), full(N_PAIRS), full(2 * N_PAIRS),
                  pl.BlockSpec((n_q, N_HEADS, tq), lambda p, i, first: (0, 0, 0))],
        out_specs=[qblk(0), qblk(0), full(0), full(0),
                   pl.BlockSpec((1, n_q, N_HEADS, tq), lambda p, i, first: (p, 0, 0, 0))])
    return pl.pallas_call(
        body, name="attn_bwd", grid_spec=grid_spec,
        out_shape=[jax.ShapeDtypeStruct((s, ATTN_W), BF16), jax.ShapeDtypeStruct((s, ATTN_W), F32),
                   jax.ShapeDtypeStruct((s, ATTN_W), F32), jax.ShapeDtypeStruct((s, ATTN_W), F32),
                   jax.ShapeDtypeStruct((N_PAIRS, n_q, N_HEADS, tq), F32)],
        compiler_params=_cp(("parallel", "arbitrary")))(first_blk, qkv, do, g, delta, qkv, qkv, ft)


AUG = HEAD_DIM
NORM_MARGIN = 1.01


def _split3(x):
    hi = x.astype(BF16).astype(F32)
    r = x - hi
    mid = r.astype(BF16).astype(F32)
    lo = (r - mid).astype(BF16).astype(F32)
    return hi, mid, lo


def _aug(base, lane, vals):
    out = jnp.where(lane < AUG, base, 0.0)
    for k, v in enumerate(vals):
        out = jnp.where(lane == AUG + k, v, out)
    return out


def _attn_prep(qkv, fb, bigs):
    s = qkv.shape[0]
    tq = min(TQ, s)
    n_q = s // tq

    n = len(bigs)
    arrays, landing, sems = _gather_operands(bigs, [])

    def body(q_ref, k_ref, v_ref, fb_ref, *rest):
        qx_ref, kx_ref, kxt_ref, vx_ref, vt_ref, b_ref = rest[2 * n:2 * n + 6]
        finish = _hosted_gather((rest[:n], rest[2 * n + 6:3 * n + 6]) + tuple(rest[3 * n + 6:]), n, n,
                                pl.program_id(0), n_q)
        lane = lax.broadcasted_iota(jnp.int32, (tq, LANES), 1)
        lane8 = lax.broadcasted_iota(jnp.int32, (HALO, LANES), 1)
        head_lanes = lane < AUG
        is_lane = [lane == AUG + k for k in range(6)]
        first3 = (lane >= AUG) & (lane < AUG + 3)
        next3 = (lane >= AUG + 3) & (lane < AUG + 6)
        q_const = jnp.where(first3, -1.0, 0.0)
        k_const = jnp.where(next3, 1.0, 0.0)
        v_const = jnp.where(first3, 1.0, 0.0)
        ones_head = (lax.broadcasted_iota(jnp.int32, (LANES, LANES), 0) < HEAD_DIM).astype(BF16)
        acc = jnp.zeros((HALO, LANES), F32)
        for p in range(N_PAIRS):
            cols = slice(p * LANES, (p + 1) * LANES)
            q2, k2, v2 = (ref[:, cols].astype(F32) for ref in (q_ref, k_ref, v_ref))
            q2 = q2 * Q_SCALE
            f2 = fb_ref[:, cols]
            for hh in range(2):
                h = 2 * p + hh
                q, k, v = ((pltpu.roll(x, HEAD_DIM, 1) if hh else x) for x in (q2, k2, v2))
                f = f2 if hh else pltpu.roll(f2, HEAD_DIM, 1)
                hi, mid, lo = _split3(f)
                q_aug = jnp.where(is_lane[3], hi, jnp.where(is_lane[4], mid, jnp.where(is_lane[5], lo, q_const)))
                k_aug = jnp.where(is_lane[0], hi, jnp.where(is_lane[1], mid, jnp.where(is_lane[2], lo, k_const)))
                kx = jnp.where(head_lanes, k, k_aug)
                vx = jnp.where(head_lanes, v, v_const)
                qx_ref[h] = jnp.where(head_lanes, q, q_aug).astype(BF16)
                kx_ref[h] = kx.astype(BF16)
                vx_ref[h] = vx.astype(BF16)
                kxt_ref[h, 0] = kx.T.astype(BF16)
                vt_ref[h, 0] = vx.T.astype(BF16)
                q_sq = lax.dot_general((q * q).astype(BF16), ones_head, _NN, preferred_element_type=F32)
                k_sq = lax.dot_general((k * k).astype(BF16), ones_head, _NN, preferred_element_type=F32)
                qk = jnp.sqrt(q_sq * k_sq) * NORM_MARGIN + f
                vals = (jnp.sqrt(jnp.max(q_sq, axis=0, keepdims=True)), jnp.sqrt(jnp.max(k_sq, axis=0, keepdims=True)),
                        jnp.max(qk, axis=0, keepdims=True), f[tq - 1:tq, :])
                for slot, val in enumerate(vals):
                    acc = jnp.where(lane8 == slot * N_HEADS + h, val[:, AUG:AUG + 1], acc)
        b_ref[0] = acc
        finish()

    blk = lambda j: pl.BlockSpec((tq, ATTN_W), lambda i: (i, j))
    rows = pl.BlockSpec((N_HEADS, tq, LANES), lambda i: (0, i, 0))
    cols_t = pl.BlockSpec((N_HEADS, 1, LANES, tq), lambda i: (0, i, 0, 0))
    shp = jax.ShapeDtypeStruct((N_HEADS, s, LANES), BF16)
    shp_t = jax.ShapeDtypeStruct((N_HEADS, n_q, LANES, tq), BF16)
    outs = pl.pallas_call(
        body, name="attn_prep", grid=(n_q,), in_specs=[blk(0), blk(1), blk(2), blk(0)] + _hbm_specs(2 * n),
        out_specs=[rows, rows, cols_t, rows, cols_t,
                   pl.BlockSpec((1, HALO, LANES), lambda i: (i, 0, 0))] + _hbm_specs(n),
        out_shape=[shp, shp, shp_t, shp, shp_t, jax.ShapeDtypeStruct((n_q, HALO, LANES), F32)]
        + [jax.ShapeDtypeStruct(b.shape, b.dtype) for b in landing],
        input_output_aliases={4 + n + k: 6 + k for k in range(n)}, scratch_shapes=sems,
        compiler_params=_cp(("arbitrary",)))(qkv, qkv, qkv, fb, *arrays, *landing)
    return tuple(outs[:6]) + (outs[6:],)


def _key_block_ranges(bounds):
    t = bounds[:, 0, :]
    nh = N_HEADS
    a, b, c, e = t[:, 0:nh], t[:, nh:2 * nh], t[:, 2 * nh:3 * nh], t[:, 3 * nh:4 * nh]
    bound = a[:, None, :] * b[None, :, :] * NORM_MARGIN + c[:, None, :] - e[None, :, :]
    n_q = t.shape[0]
    idx = jnp.arange(n_q)
    need = jnp.logical_not(bound < -(EXP_ZERO + 2.0)) | (idx[None, :, None] >= idx[:, None, None])
    first = jnp.argmax(need, axis=1).astype(jnp.int32)
    first = jnp.min(first.reshape(n_q, N_PAIRS, 2), axis=-1)
    visits = (first[:, None, :] <= idx[None, :, None]) & (idx[None, :, None] <= idx[:, None, None])
    last = jnp.max(jnp.where(visits, idx[:, None, None], 0), axis=0).astype(jnp.int32)
    return first.T.reshape(-1), last.T.reshape(-1)


def _attn_fwd_t(qx, kx, vt, first_blk):
    _, s, _ = qx.shape
    tq = min(TQ, s)
    n_q = s // tq
    neg = -1e30

    def body(first_ref, qx_ref, kx_ref, vt_ref, o_ref, lse_ref, acc_ref, m_ref):
        p = pl.program_id(0)
        i = pl.program_id(1)
        acc_ref[...] = jnp.zeros(acc_ref.shape, F32)
        m_ref[...] = jnp.full(m_ref.shape, neg, F32)
        key_le_query = (lax.broadcasted_iota(jnp.int32, (tq, tq), 0) <= lax.broadcasted_iota(jnp.int32, (tq, tq), 1))

        def step(kb, masked):
            rows_k = pl.ds(pl.multiple_of(kb * tq, tq), tq)
            for hh in range(2):
                st = lax.dot_general(kx_ref[hh, rows_k, :], qx_ref[hh], _NT, preferred_element_type=F32)
                if masked:
                    st = jnp.where(key_le_query, st, -jnp.inf)
                m_old = m_ref[hh]
                m_new = jnp.maximum(m_old, jnp.max(st, axis=0, keepdims=True))
                m_ref[hh] = m_new
                pt = jnp.exp(st - m_new).astype(BF16)
                acc_ref[hh] = acc_ref[hh] * jnp.exp(m_old - m_new) + lax.dot_general(
                    vt_ref[hh, kb], pt, _NN, preferred_element_type=F32)

        def unmasked(kb, carry):
            step(kb, False)
            return carry

        lax.fori_loop(first_ref[p * n_q + i], i, unmasked, 0)
        step(i, True)
        outs, lses = [], []
        for hh in range(2):
            acc = acc_ref[hh]
            l = acc[AUG:AUG + 1, :]
            outs.append(acc[0:HEAD_DIM, :] / l)
            lses.append(m_ref[hh] + jnp.log(l))
        o_ref[...] = jnp.concatenate(outs, axis=0).T
        rows8 = lax.broadcasted_iota(jnp.int32, (N_HEADS, tq), 0)
        lse_ref[0, 0] = jnp.where(rows8 == 0, lses[0], jnp.where(rows8 == 1, lses[1], 0.0))

    grid_spec = pltpu.PrefetchScalarGridSpec(
        num_scalar_prefetch=1, grid=(N_PAIRS, n_q),
        in_specs=[pl.BlockSpec((2, tq, LANES), lambda p, i, first: (p, i, 0)),
                  pl.BlockSpec((2, s, LANES), lambda p, i, first: (p, 0, 0)),
                  pl.BlockSpec((2, n_q, LANES, tq), lambda p, i, first: (p, 0, 0, 0))],
        out_specs=[pl.BlockSpec((tq, LANES), lambda p, i, first: (i, p)),
                   pl.BlockSpec((1, 1, N_HEADS, tq), lambda p, i, first: (p, i, 0, 0))],
        scratch_shapes=[pltpu.VMEM((2, LANES, tq), F32), pltpu.VMEM((2, 1, tq), F32)])
    return pl.pallas_call(
        body, name="attn_fwd", grid_spec=grid_spec,
        out_shape=[jax.ShapeDtypeStruct((s, ATTN_W), F32), jax.ShapeDtypeStruct((N_PAIRS, n_q, N_HEADS, tq), F32)],
        compiler_params=_cp(("parallel", "arbitrary")))(first_blk, qx, kx, vt)


def _attn_bwd_t(qx, dox, kx, kxt, vx, lse, last_blk):
    _, s, _ = qx.shape
    tq = min(TQ, s)
    n_q = s // tq

    def body(last_ref, qx_ref, dox_ref, lse_ref, kx_ref, kxt_ref, vx_ref, dk_ref, dv_ref, aux_ref, dqt_ref):
        p = pl.program_id(0)
        j = pl.program_id(1)

        @pl.when(j == 0)
        def _():
            dqt_ref[...] = jnp.zeros(dqt_ref.shape, F32)

        key_le_query = (lax.broadcasted_iota(jnp.int32, (tq, tq), 0) <= lax.broadcasted_iota(jnp.int32, (tq, tq), 1))

        def step(i, carry, masked):
            rows_q = pl.ds(pl.multiple_of(i * tq, tq), tq)
            out = []
            for hh in range(2):
                dk, dv = carry[2 * hh], carry[2 * hh + 1]
                q, do = qx_ref[hh, rows_q, :], dox_ref[hh, rows_q, :]
                st = lax.dot_general(kx_ref[hh], q, _NT, preferred_element_type=F32)
                pt = jnp.exp(st - lse_ref[0, i, hh:hh + 1, :])
                if masked:
                    pt = jnp.where(key_le_query, pt, 0.0)
                dst = pt * lax.dot_general(vx_ref[hh], do, _NT, preferred_element_type=F32)
                pb, dsb = pt.astype(BF16), dst.astype(BF16)
                dv = dv + lax.dot_general(pb, do, _NN, preferred_element_type=F32)
                dk = dk + lax.dot_general(dsb, q, _NN, preferred_element_type=F32)
                dqt_ref[hh, i] += lax.dot_general(kxt_ref[hh, 0], dsb, _NN, preferred_element_type=F32)
                out += [dk, dv]
            return tuple(out)

        zero = jnp.zeros((tq, LANES), F32)
        carry = step(j, (zero, zero, zero, zero), True)
        dk_a, dv_a, dk_b, dv_b = lax.fori_loop(j + 1, last_ref[p * n_q + j] + 1,
                                               lambda i, cr: step(i, cr, False), carry)
        head_a = lax.broadcasted_iota(jnp.int32, (tq, LANES), 1) < HEAD_DIM
        dk_ref[...] = jnp.where(head_a, dk_a, pltpu.roll(dk_b, HEAD_DIM, 1)).astype(BF16)
        dv_ref[...] = jnp.where(head_a, dv_a, pltpu.roll(dv_b, HEAD_DIM, 1)).astype(BF16)
        aux_ref[...] = jnp.where(head_a, pltpu.roll(dk_a, HEAD_DIM, 1), dk_b)

    resident = pl.BlockSpec((2, s, LANES), lambda p, j, last: (p, 0, 0))
    key_rows = pl.BlockSpec((2, tq, LANES), lambda p, j, last: (p, j, 0))
    pair_out = pl.BlockSpec((tq, LANES), lambda p, j, last: (j, p))
    grid_spec = pltpu.PrefetchScalarGridSpec(
        num_scalar_prefetch=1, grid=(N_PAIRS, n_q),
        in_specs=[resident, resident, pl.BlockSpec((1, n_q, N_HEADS, tq), lambda p, j, last: (p, 0, 0, 0)),
                  key_rows, pl.BlockSpec((2, 1, LANES, tq), lambda p, j, last: (p, j, 0, 0)), key_rows],
        out_specs=[pair_out, pair_out, pair_out,
                   pl.BlockSpec((2, n_q, LANES, tq), lambda p, j, last: (p, 0, 0, 0))])
    return pl.pallas_call(
        body, name="attn_bwd", grid_spec=grid_spec,
        out_shape=[jax.ShapeDtypeStruct((s, ATTN_W), BF16), jax.ShapeDtypeStruct((s, ATTN_W), BF16),
                   jax.ShapeDtypeStruct((s, ATTN_W), F32), jax.ShapeDtypeStruct((N_HEADS, n_q, LANES, tq), F32)],
        compiler_params=_cp(("parallel", "arbitrary")))(last_blk, qx, dox, lse, kx, kxt, vx)


def _attn_dq_finish(dqt):
    _, n_q, _, tq = dqt.shape

    def body(dqt_ref, dq_ref, dfq_ref):
        a, b = dqt_ref[0, 0], dqt_ref[1, 0]
        dq_ref[...] = (jnp.concatenate([a[0:HEAD_DIM], b[0:HEAD_DIM]], axis=0).T * Q_SCALE).astype(BF16)
        rows8 = lax.broadcasted_iota(jnp.int32, (N_HEADS, tq), 0)
        dfq_ref[0, 0] = jnp.where(rows8 == 0, a[AUG + 3:AUG + 4], jnp.where(rows8 == 1, b[AUG + 3:AUG + 4], 0.0))

    return pl.pallas_call(
        body, name="attn_dq_finish", grid=(N_PAIRS, n_q),
        in_specs=[pl.BlockSpec((2, 1, LANES, tq), lambda p, i: (p, i, 0, 0))],
        out_specs=[pl.BlockSpec((tq, LANES), lambda p, i: (i, p)),
                   pl.BlockSpec((1, 1, N_HEADS, tq), lambda p, i: (p, i, 0, 0))],
        out_shape=[jax.ShapeDtypeStruct((n_q * tq, ATTN_W), BF16),
                   jax.ShapeDtypeStruct((N_PAIRS, n_q, N_HEADS, tq), F32)],
        compiler_params=_cp(("parallel", "parallel")))(dqt)


def _ffn_act_fwd(up, w_ffn):
    s = up.shape[0]
    tm, tn = min(TM_FFN, s), TN_FFN
    nb = D_FF // tn

    def body(a_ref, g_ref, ap_ref, gp_ref, wa_ref, wg_ref, act_ref):
        i = pl.program_id(1)

        def conv(blk_ref, prev_ref, w_ref):
            prev = jnp.where(i > 0, prev_ref[...], 0.0)
            return _conv_taps(jnp.concatenate([prev, blk_ref[...]], axis=0), w_ref[...])[HALO:]

        u_a, u_g = conv(a_ref, ap_ref, wa_ref), conv(g_ref, gp_ref, wg_ref)
        act_ref[...] = (u_g * (1.0 / (1.0 + jnp.exp(-u_g))) * u_a).astype(BF16)

    blk = lambda off: pl.BlockSpec((tm, tn), lambda n, i: (i, off + n))
    prev = lambda off: pl.BlockSpec((HALO, tn), lambda n, i: (jnp.maximum(i * (tm // HALO) - 1, 0), off + n))
    wsp = lambda off: pl.BlockSpec((3, tn), lambda n, i: (0, off + n))
    return pl.pallas_call(
        body, name="ffn_act_fwd", grid=(nb, s // tm),
        in_specs=[blk(0), blk(nb), prev(0), prev(nb), wsp(0), wsp(nb)],
        out_specs=pl.BlockSpec((tm, tn), lambda n, i: (i, n)),
        out_shape=jax.ShapeDtypeStruct((s, D_FF), BF16),
        compiler_params=_cp(("parallel", "parallel")))(up, up, up, up, w_ffn, w_ffn)


def _ffn_down_loss(up, w_ffn, w_down, x2, target, g_final):
    s, d = x2.shape
    tm, tn = min(TM_ROWS, s), TN_FFN
    nb = D_FF // tn

    def body(a_ref, g_ref, ap_ref, gp_ref, wa_ref, wg_ref, wd_ref, x2_ref, t_ref, gf_ref,
             act_ref, dx_ref, dxb_ref, loss_ref, gg_ref, acc, act_even, act_odd):
        i = pl.program_id(0)
        k = pl.program_id(1)

        @pl.when((i == 0) & (k == 0))
        def _():
            gg_ref[...] = jnp.zeros_like(gg_ref)
            loss_ref[...] = jnp.zeros_like(loss_ref)

        def conv(blk_ref, prev_ref, w_ref):
            prev = jnp.where(i > 0, prev_ref[...], 0.0)
            return _conv_taps(jnp.concatenate([prev, blk_ref[...]], axis=0), w_ref[...])[HALO:]

        def activation(dst):
            u_a, u_g = conv(a_ref, ap_ref, wa_ref), conv(g_ref, gp_ref, wg_ref)
            act = (u_g * (1.0 / (1.0 + jnp.exp(-u_g))) * u_a).astype(BF16)
            act_ref[...] = act
            dst[...] = act

        def project(src, first):
            part = lax.dot_general(src[...], wd_ref[...], _NN, preferred_element_type=F32)
            acc[...] = part if first else acc[...] + part

        @pl.when(k == 0)
        def _():
            activation(act_even)

        @pl.when(k == 1)
        def _():
            project(act_even, True)
            activation(act_odd)

        @pl.when((k > 1) & (k < nb) & (k % 2 == 0))
        def _():
            project(act_odd, False)
            activation(act_even)

        @pl.when((k > 1) & (k < nb) & (k % 2 == 1))
        def _():
            project(act_even, False)
            activation(act_odd)

        @pl.when(k == nb)
        def _():
            project(act_even if nb % 2 == 1 else act_odd, False)
            xv = x2_ref[...] + acc[...]
            r = _rstd(xv)
            xn = xv * r
            gv = gf_ref[...]
            err = xn * gv - t_ref[...]
            loss_ref[...] += 0.5 * jnp.sum(jnp.mean(err * err, axis=-1, keepdims=True), axis=0, keepdims=True)
            dy = err * (1.0 / d)
            gg_ref[...] += jnp.sum(dy * xn, axis=0, keepdims=True)
            t = dy * gv
            dx = r * (t - xn * jnp.mean(t * xn, axis=-1, keepdims=True))
            dx_ref[...] = dx
            dxb_ref[...] = dx.astype(BF16)

    chunk = lambda k: jnp.minimum(k, nb - 1)
    blk = lambda off: pl.BlockSpec((tm, tn), lambda i, k: (i, off + chunk(k)))
    prev = lambda off: pl.BlockSpec((HALO, tn),
                                    lambda i, k: (jnp.maximum(i * (tm // HALO) - 1, 0), off + chunk(k)))
    wsp = lambda off: pl.BlockSpec((3, tn), lambda i, k: (0, off + chunk(k)))
    row = pl.BlockSpec((tm, d), lambda i, k: (i, 0))
    vec = pl.BlockSpec((1, d), lambda i, k: (0, 0))
    return pl.pallas_call(
        body, name="ffn_down_loss", grid=(s // tm, nb + 1),
        in_specs=[blk(0), blk(nb), prev(0), prev(nb), wsp(0), wsp(nb),
                  pl.BlockSpec((tn, d), lambda i, k: (jnp.maximum(k - 1, 0), 0)), row, row, vec],
        out_specs=[pl.BlockSpec((tm, tn), lambda i, k: (i, chunk(k))), row, row,
                   pl.BlockSpec((1, LANES), lambda i, k: (0, 0)), vec],
        out_shape=[jax.ShapeDtypeStruct((s, D_FF), BF16), jax.ShapeDtypeStruct((s, d), F32),
                   jax.ShapeDtypeStruct((s, d), BF16), jax.ShapeDtypeStruct((1, LANES), F32),
                   jax.ShapeDtypeStruct((1, d), F32)],
        scratch_shapes=[pltpu.VMEM((tm, d), F32), pltpu.VMEM((tm, tn), BF16), pltpu.VMEM((tm, tn), BF16)],
        compiler_params=_cp(("arbitrary", "arbitrary")))(up, up, up, up, w_ffn, w_ffn, w_down, x2, target, g_final)


def _ffn_act_bwd(up, dact, w_ffn):
    s = up.shape[0]
    tm, tn = min(TM_FFN, s), TN_FFN
    nb = D_FF // tn
    n_blk = s // tm

    def body(a_ref, g_ref, ap_ref, gp_ref, an_ref, gn_ref, d_ref, dn_ref, wa_ref, wg_ref,
             dup_ref, gwa_ref, gwg_ref):
        i = pl.program_id(1)

        @pl.when(i == 0)
        def _():
            gwa_ref[...] = jnp.zeros_like(gwa_ref)
            gwg_ref[...] = jnp.zeros_like(gwg_ref)

        def ext(blk_ref, prev_ref, next_ref):
            return jnp.concatenate([jnp.where(i > 0, prev_ref[...], 0.0), blk_ref[...], next_ref[...]], axis=0)

        wa, wg = wa_ref[...], wg_ref[...]
        up_a, up_g = ext(a_ref, ap_ref, an_ref), ext(g_ref, gp_ref, gn_ref)
        u_a, u_g = _conv_taps(up_a, wa), _conv_taps(up_g, wg)
        d_e = jnp.concatenate([jnp.zeros((HALO, tn), F32), d_ref[...],
                               jnp.where(i < n_blk - 1, dn_ref[...], 0.0)], axis=0)
        sig = 1.0 / (1.0 + jnp.exp(-u_g))
        du_a = d_e * (u_g * sig)
        du_g = d_e * u_a * (sig * (1.0 + u_g * (1.0 - sig)))
        blk = slice(HALO, HALO + tm)
        dup_ref[0] = _conv_taps_t(du_a, wa)[blk].astype(BF16)
        dup_ref[1] = _conv_taps_t(du_g, wg)[blk].astype(BF16)
        for gw_ref, upv, du in ((gwa_ref, up_a[blk], du_a), (gwg_ref, up_g[blk], du_g)):
            gw_ref[0:1, :] += jnp.sum(upv * _shift_up(du, 2)[blk], axis=0, keepdims=True)
            gw_ref[1:2, :] += jnp.sum(upv * _shift_up(du, 1)[blk], axis=0, keepdims=True)
            gw_ref[2:3, :] += jnp.sum(upv * du[blk], axis=0, keepdims=True)

    blk = lambda off: pl.BlockSpec((tm, tn), lambda n, i: (i, off + n))
    prev = lambda off: pl.BlockSpec((HALO, tn), lambda n, i: (jnp.maximum(i * (tm // HALO) - 1, 0), off + n))
    nxt = lambda off: pl.BlockSpec(
        (HALO, tn), lambda n, i: (jnp.minimum((i + 1) * (tm // HALO), s // HALO - 1), off + n))
    wsp = lambda off: pl.BlockSpec((3, tn), lambda n, i: (0, off + n))
    return pl.pallas_call(
        body, name="ffn_act_bwd", grid=(nb, n_blk),
        in_specs=[blk(0), blk(nb), prev(0), prev(nb), nxt(0), nxt(nb), blk(0), nxt(0), wsp(0), wsp(nb)],
        out_specs=[pl.BlockSpec((2, tm, tn), lambda n, i: (0, i, n)), wsp(0), wsp(0)],
        out_shape=[jax.ShapeDtypeStruct((2, s, D_FF), BF16),
                   jax.ShapeDtypeStruct((3, D_FF), F32), jax.ShapeDtypeStruct((3, D_FF), F32)],
        compiler_params=_cp(("parallel", "arbitrary")))(up, up, up, up, up, up, dact, dact, w_ffn, w_ffn)


def _adamw(w, g, m, v, name):
    r, c = w.shape
    tr = next((t for t in (512, 352, 256, 128, 64, 32, 16, 8) if r > t and r % t == 0), r)

    def body(w_ref, g_ref, m_ref, v_ref, d_ref, nm_ref, nv_ref):
        gv = g_ref[...]
        m_new = ADAM_B1 * m_ref[...] + (1.0 - ADAM_B1) * gv
        v_new = ADAM_B2 * v_ref[...] + (1.0 - ADAM_B2) * (gv * gv)
        m_hat = m_new / (1.0 - ADAM_B1 ** ADAM_STEP)
        v_hat = v_new / (1.0 - ADAM_B2 ** ADAM_STEP)
        d_ref[...] = -ADAM_LR * (m_hat / (jnp.sqrt(v_hat) + ADAM_EPS) + ADAM_WD * w_ref[...])
        nm_ref[...] = m_new
        nv_ref[...] = v_new

    spec = pl.BlockSpec((tr, c), lambda i: (i, 0))
    shp = jax.ShapeDtypeStruct((r, c), F32)
    return pl.pallas_call(
        body, name=name, grid=(r // tr,), in_specs=[spec] * 4, out_specs=[spec] * 3, out_shape=[shp] * 3,
        compiler_params=_cp(("parallel",)))(w, g, m, v)


def _sum_rows_block(h):
    return h if h <= 352 else 256


def _pair_sum(view, recv, sel, name):
    n, _, h, c = view.shape
    tr = _sum_rows_block(h)

    def body(sel_ref, a_ref, b_ref, o_ref, ob_ref):
        t = a_ref[...] + b_ref[...]
        o_ref[...] = t
        ob_ref[...] = t.astype(BF16)

    blk = pl.BlockSpec((None, tr, c), lambda j, i, sel_ref: (j, i, 0))
    grid_spec = pltpu.PrefetchScalarGridSpec(
        num_scalar_prefetch=1, grid=(n, h // tr),
        in_specs=[pl.BlockSpec((None, None, tr, c), lambda j, i, sel_ref: (j, sel_ref[0], i, 0)),
                  pl.BlockSpec((None, None, tr, c), lambda j, i, sel_ref: (j, 0, i, 0))],
        out_specs=[blk, blk])
    return pl.pallas_call(
        body, name=name, grid_spec=grid_spec,
        out_shape=[jax.ShapeDtypeStruct((n, h, c), F32), jax.ShapeDtypeStruct((n, h, c), BF16)],
        compiler_params=_cp(("parallel", "parallel")))(sel, view, recv)


def _chip_sum(pair, got, sel, name):
    _, h, c = pair.shape
    tr = _sum_rows_block(h)
    nblk = h // tr

    def body(sel_ref, p_ref, g0_ref, g1_ref, g2_ref, o_ref):
        o_ref[...] = ((p_ref[...] + g0_ref[...].astype(F32)) + g1_ref[...].astype(F32)) + g2_ref[...].astype(F32)

    slot = lambda k: pl.BlockSpec((None, tr, c), lambda i, sel_ref: (k, i, 0))
    grid_spec = pltpu.PrefetchScalarGridSpec(
        num_scalar_prefetch=1, grid=(h // tr,),
        in_specs=[pl.BlockSpec((None, tr, c), lambda i, sel_ref: (sel_ref[1], i, 0)), slot(0), slot(1), slot(2)],
        out_specs=pl.BlockSpec((tr, c), lambda i, sel_ref: (sel_ref[0] * nblk + i, 0)))
    return pl.pallas_call(
        body, name=name, grid_spec=grid_spec, out_shape=jax.ShapeDtypeStruct((2 * h, c), F32),
        compiler_params=_cp(("parallel",)))(sel, pair, got, got, got)


def _place():
    return lax.axis_index("x"), lax.axis_index("y"), lax.axis_index("c")


def _other_chips(x, y):
    return [(1 - x, y), (x, 1 - y), (1 - x, 1 - y)]


def _hbm_specs(n):
    return [pl.BlockSpec(memory_space=pl.ANY)] * n


def _all_gather_weights(bigs, smalls):
    nb, ns = len(bigs), len(smalls)
    n = nb + ns

    def body(*refs):
        start, forward, finish = _gather_phases(refs[:n], refs[2 * n:3 * n], nb, *refs[3 * n:])
        start()
        forward()
        finish()

    arrays, landing, sems = _gather_operands(bigs, smalls)
    return pl.pallas_call(
        body, name="all_gather_weights",
        out_shape=[jax.ShapeDtypeStruct(b.shape, b.dtype) for b in landing],
        in_specs=_hbm_specs(2 * n), out_specs=_hbm_specs(n), input_output_aliases={n + k: k for k in range(n)},
        scratch_shapes=sems)(*arrays, *landing)


def _hosted_gather(refs, n, nb, step, total):
    ins, outs, send_sems, recv_sems = refs
    start, forward, finish = _gather_phases(ins, outs, nb, send_sems, recv_sems)
    pl.when(step == 0)(start)
    pl.when(step == (3 * total) // 4)(forward)
    return lambda: pl.when(step == total - 1)(finish)


def _in_proj(h1, w_a, w_b, w_c, bigs, smalls):
    s, d = h1.shape
    tm, tn = min(TM_MM, s), ATTN_W
    na, nq = w_a.shape[1] // tn, w_b.shape[1] // tn
    steps = na + nq + 1
    total = (s // tm) * steps
    nb, n = len(bigs), len(bigs) + len(smalls)
    arrays, landing, sems = _gather_operands(bigs, smalls)

    def body(h_ref, wa_ref, wb_ref, wc_ref, *rest):
        z_ref, qkv_ref, f_ref = rest[2 * n:2 * n + 3]
        m, j = pl.program_id(0), pl.program_id(1)
        finish = _hosted_gather((rest[:n], rest[2 * n + 3:3 * n + 3]) + tuple(rest[3 * n + 3:]), n, nb,
                                m * steps + j, total)
        h = h_ref[...]

        @pl.when(j < na)
        def _():
            z_ref[...] = lax.dot_general(h, wa_ref[...], _NN, preferred_element_type=F32)

        @pl.when((j >= na) & (j < na + nq))
        def _():
            qkv_ref[...] = lax.dot_general(h, wb_ref[...], _NN, preferred_element_type=F32).astype(BF16)

        @pl.when(j == na + nq)
        def _():
            f_ref[...] = lax.dot_general(h, wc_ref[...], _NN, preferred_element_type=F32)

        finish()

    blk_a = lambda m, j: (m, jnp.minimum(j, na - 1))
    blk_b = lambda m, j: (m, jnp.clip(j - na, 0, nq - 1))
    outs = pl.pallas_call(
        body, name="in_proj", grid=(s // tm, steps),
        in_specs=[pl.BlockSpec((tm, d), lambda m, j: (m, 0)),
                  pl.BlockSpec((d, tn), lambda m, j: (0, jnp.minimum(j, na - 1))),
                  pl.BlockSpec((d, tn), lambda m, j: (0, jnp.clip(j - na, 0, nq - 1))),
                  pl.BlockSpec((d, LANES), lambda m, j: (0, 0))] + _hbm_specs(2 * n),
        out_specs=[pl.BlockSpec((tm, tn), blk_a), pl.BlockSpec((tm, tn), blk_b),
                   pl.BlockSpec((tm, LANES), lambda m, j: (m, 0))] + _hbm_specs(n),
        out_shape=[jax.ShapeDtypeStruct((s, w_a.shape[1]), F32), jax.ShapeDtypeStruct((s, w_b.shape[1]), BF16),
                   jax.ShapeDtypeStruct((s, LANES), F32)] + [jax.ShapeDtypeStruct(b.shape, b.dtype) for b in landing],
        input_output_aliases={4 + n + k: 3 + k for k in range(n)}, scratch_shapes=sems,
        compiler_params=_cp(("arbitrary", "arbitrary")))(h1, w_a, w_b, w_c, *arrays, *landing)
    return outs[0], outs[1], outs[2], outs[3:]


def _gather_operands(bigs, smalls):
    x, y, _ = _place()
    arrays = list(bigs) + list(smalls)
    landing = [lax.dynamic_update_index_in_dim(lax.empty((N_CHIPS,) + a.shape, a.dtype), a, 2 * x + y, 0)
               for a in arrays]
    n_sems = 6 * len(bigs) + 3 * len(smalls)
    return arrays, landing, [pltpu.SemaphoreType.DMA((n_sems,)), pltpu.SemaphoreType.DMA((n_sems,))]


def _gather_phases(ins, outs, nb, send_sems, recv_sems):
    n = len(ins)
    x, y, c = _place()
    my_chip = 2 * x + y
    chips = _other_chips(x, y)
    sibling = (x, y, 1 - c)

    def rows(k, which):
        h = ins[k].shape[0] // 2
        return pl.ds(which * h, h)

    def copy(sem, src, dst, to):
        return pltpu.make_async_remote_copy(src_ref=src, dst_ref=dst, send_sem=send_sems.at[sem],
                                            recv_sem=recv_sems.at[sem], device_id=to, device_id_type=MESH)

    def sends():
        out = [copy(6 * k + j, ins[k].at[rows(k, c)], outs[k].at[my_chip, rows(k, c)], (cx, cy, c))
               for k in range(nb) for j, (cx, cy) in enumerate(chips)]
        return out + [copy(6 * nb + 3 * (k - nb) + j, ins[k], outs[k].at[my_chip], (cx, cy, c))
                      for k in range(nb, n) for j, (cx, cy) in enumerate(chips)]

    def landed(k, j, which):
        cx, cy = chips[j]
        return outs[k].at[2 * cx + cy, rows(k, which)]

    def forwards():
        return [copy(6 * k + 3 + j, landed(k, j, c), landed(k, j, c), sibling)
                for j in range(3) for k in range(nb)]

    def start():
        for cp in sends():
            cp.start()

    def forward():
        for j in range(3):
            for k in range(nb):
                copy(6 * k + j, landed(k, j, c), landed(k, j, c), (x, y, c)).wait_recv()
                copy(6 * k + 3 + j, landed(k, j, c), landed(k, j, c), sibling).start()

    def finish():
        for j, (cx, cy) in enumerate(chips):
            for k in range(nb):
                copy(6 * k + 3 + j, landed(k, j, 1 - c), landed(k, j, 1 - c), (x, y, c)).wait_recv()
            for k in range(nb, n):
                arrived = outs[k].at[2 * cx + cy]
                copy(6 * nb + 3 * (k - nb) + j, arrived, arrived, (x, y, c)).wait_recv()
        for cp in sends() + forwards():
            cp.wait_send()

    return start, forward, finish


def _pair_exchange(views):
    n = len(views)

    def body(*refs):
        ins, outs, send_sems, recv_sems = refs[:n], refs[n:2 * n], refs[2 * n], refs[2 * n + 1]
        x, y, c = _place()
        copies = [pltpu.make_async_remote_copy(
            src_ref=ins[k].at[:, pl.ds(1 - c, 1)], dst_ref=outs[k], send_sem=send_sems.at[k],
            recv_sem=recv_sems.at[k], device_id=(x, y, 1 - c), device_id_type=MESH) for k in range(n)]
        for cp in copies:
            cp.start()
        for cp in copies:
            cp.wait()

    return pl.pallas_call(
        body, name="pair_exchange",
        out_shape=[jax.ShapeDtypeStruct((v.shape[0], 1) + v.shape[2:], v.dtype) for v in views],
        in_specs=_hbm_specs(n), out_specs=_hbm_specs(n),
        scratch_shapes=[pltpu.SemaphoreType.DMA((n,)), pltpu.SemaphoreType.DMA((n,))])(*views)


def _scatter_to_chips(parts):
    n = len(parts)

    def body(*refs):
        ins, outs, send_sems, recv_sems = refs[:n], refs[n:2 * n], refs[2 * n], refs[2 * n + 1]
        x, y, c = _place()
        copies = [pltpu.make_async_remote_copy(
            src_ref=ins[k].at[pl.ds(2 * cx + cy, 1)], dst_ref=outs[k].at[pl.ds(r, 1)], send_sem=send_sems.at[3 * k + r],
            recv_sem=recv_sems.at[3 * k + r], device_id=(cx, cy, c), device_id_type=MESH)
            for k in range(n) for r, (cx, cy) in enumerate(_other_chips(x, y))]
        for cp in copies:
            cp.start()
        for cp in copies:
            cp.wait()

    return pl.pallas_call(
        body, name="scatter_grads", out_shape=[jax.ShapeDtypeStruct((3,) + p.shape[1:], p.dtype) for p in parts],
        in_specs=_hbm_specs(n), out_specs=_hbm_specs(n),
        scratch_shapes=[pltpu.SemaphoreType.DMA((3 * n,)), pltpu.SemaphoreType.DMA((3 * n,))])(*parts)


def _join_halves(shards):
    n = len(shards)

    def body(*refs):
        ins, outs, send_sems, recv_sems = refs[:n], refs[n:2 * n], refs[2 * n], refs[2 * n + 1]
        x, y, c = _place()

        def rows(ref, which):
            h = ref.shape[0] // 2
            return ref.at[pl.ds(which * h, h)]

        sent = [pltpu.make_async_remote_copy(
            src_ref=rows(ins[k], c), dst_ref=rows(outs[k], c), send_sem=send_sems.at[k], recv_sem=recv_sems.at[k],
            device_id=(x, y, 1 - c), device_id_type=MESH) for k in range(n)]
        for cp in sent:
            cp.start()
        for k in range(n):
            pltpu.make_async_remote_copy(
                src_ref=rows(ins[k], 1 - c), dst_ref=rows(outs[k], 1 - c), send_sem=send_sems.at[k],
                recv_sem=recv_sems.at[k], device_id=(x, y, 1 - c), device_id_type=MESH).wait_recv()
        for cp in sent:
            cp.wait_send()

    return pl.pallas_call(
        body, name="half_exchange", out_shape=[jax.ShapeDtypeStruct(a.shape, a.dtype) for a in shards],
        in_specs=_hbm_specs(n), out_specs=_hbm_specs(n), input_output_aliases={k: k for k in range(n)},
        scratch_shapes=[pltpu.SemaphoreType.DMA((n,)), pltpu.SemaphoreType.DMA((n,))])(*shards)


def _all_reduce_small(packet):
    rows, width = packet.shape
    n_dev = 8

    def body(x_ref, out_ref, gath, send_sems, recv_sems):
        x, y, c = _place()
        me, sibling = (x, y, c), (x, y, 1 - c)
        chips = _other_chips(x, y)

        def slot(px, py, pc):
            return gath.at[pl.ds((4 * px + 2 * py + pc) * rows, rows), :]

        def copy(k, block, to, src=None):
            return pltpu.make_async_remote_copy(
                src_ref=slot(*block) if src is None else src, dst_ref=slot(*block), send_sem=send_sems.at[k],
                recv_sem=recv_sems.at[k], device_id=to, device_id_type=MESH)

        first = [copy(0, me, sibling, src=x_ref)]
        first += [copy(1 + j, me, (*chip, c), src=x_ref) for j, chip in enumerate(chips)]
        for cp in first:
            cp.start()
        gath[pl.ds((4 * x + 2 * y + c) * rows, rows), :] = x_ref[...]
        passed = [copy(4 + j, (*chip, c), sibling) for j, chip in enumerate(chips)]
        for j, chip in enumerate(chips):
            copy(1 + j, (*chip, c), me).wait_recv()
            passed[j].start()
        copy(0, sibling, me).wait_recv()
        for j, chip in enumerate(chips):
            copy(4 + j, (*chip, 1 - c), me).wait_recv()
        for cp in first + passed:
            cp.wait_send()
        acc = gath[0:rows, :]
        for d in range(1, n_dev):
            acc = acc + gath[d * rows:(d + 1) * rows, :]
        out_ref[...] = acc

    return pl.pallas_call(
        body, name="all_reduce_small", out_shape=jax.ShapeDtypeStruct((rows, width), F32),
        in_specs=[pl.BlockSpec(memory_space=pltpu.VMEM)], out_specs=pl.BlockSpec(memory_space=pltpu.VMEM),
        scratch_shapes=[pltpu.VMEM((n_dev * rows, width), F32), pltpu.SemaphoreType.DMA((7,)),
                        pltpu.SemaphoreType.DMA((7,))])(packet)


def _flat_rows(parts, width, row_multiple):
    flat = jnp.concatenate([p.astype(F32).reshape(-1) for p in parts])
    rows = -(-flat.shape[0] // width)
    rows = -(-rows // row_multiple) * row_multiple
    return jnp.pad(flat, (0, rows * width - flat.shape[0])).reshape(rows, width)


def _unflatten(flat2d, shapes):
    flat = flat2d.reshape(-1)
    out, off = [], 0
    for shp in shapes:
        n = 1
        for dim in shp:
            n *= dim
        out.append(flat[off:off + n].reshape(shp))
        off += n
    return out


def _reduce_scatter_grads(chip_major, names):
    x, y, c = _place()
    sel = jnp.stack([c, 2 * x + y]).astype(jnp.int32)
    views = [g.reshape(N_CHIPS, 2, g.shape[1] // 2, g.shape[2]) for g in chip_major]
    recv = _pair_exchange(views)
    pairs = [_pair_sum(v, r, sel, "pair_sum_" + nm) for v, r, nm in zip(views, recv, names)]
    got = _scatter_to_chips([pb for _, pb in pairs])
    return _join_halves([_chip_sum(p, g, sel, "chip_sum_" + nm) for (p, _), g, nm in zip(pairs, got, names)])


def kernel(x, g_mix, w_in, b_f, w_conv, g_conv_out, g_attn_out, w_o, g_ffn, w_up, w_ffn_conv, w_down, g_final, loss_target, m_g_mix, m_w_in, m_b_f, m_w_conv, m_g_conv_out, m_g_attn_out, m_w_o, m_g_ffn, m_w_up, m_w_ffn_conv, m_w_down, m_g_final, v_g_mix, v_w_in, v_b_f, v_w_conv, v_g_conv_out, v_g_attn_out, v_w_o, v_g_ffn, v_w_up, v_w_ffn_conv, v_w_down, v_g_final):
    s = x.shape[1]
    x0 = x[0]
    target = loss_target[0]
    d = D_MODEL
    x_pos, y_pos, _ = _place()
    my_chip = 2 * x_pos + y_pos

    (c_in,) = _all_gather_weights([w_in[0].astype(BF16)], [])
    w_in_full = jnp.concatenate([c_in[j] for j in range(N_CHIPS)], axis=1)
    c3 = 3 * CONV_CH
    w_a, w_b = w_in_full[:, :c3], w_in_full[:, c3:c3 + 3 * ATTN_W]
    w_c = jnp.pad(w_in_full[:, c3 + 3 * ATTN_W:], ((0, 0), (0, LANES - N_HEADS)))
    w_q, w_k, w_v = (w_b[:, i * ATTN_W:(i + 1) * ATTN_W] for i in range(3))
    b_pad = jnp.pad(b_f, ((0, 0), (0, LANES - N_HEADS)))

    h1 = _rms_fwd(x0, g_mix, "rms_mix")
    z_a, qkv, f_log, (c_o, c_up, c_conv, c_ffn) = _in_proj(
        h1, w_a, w_b, w_c, [w_o[0].astype(BF16), w_up[0].astype(BF16)], [w_conv[0], w_ffn_conv[0]])
    fb = _gate_fwd(f_log, b_pad)
    qx, kx, kxt, vx, vt, bounds, (c_down,) = _attn_prep(qkv, fb, [w_down[0].astype(BF16)])
    w_o_full = c_o.reshape(d, d)
    w_down_full = c_down.reshape(D_FF, d)
    w_conv_full = jnp.concatenate([c_conv[j] for j in range(N_CHIPS)], axis=1)
    w_ffn_full = jnp.concatenate([c_ffn[j] for j in range(N_CHIPS)], axis=1)
    n_up = c_up.shape[2]
    first_blk, last_blk = _key_block_ranges(bounds)
    o_attn, lse = _attn_fwd_t(qx, kx, vt, first_blk)
    mix = _mixer_fwd(z_a, o_attn, w_conv_full, g_conv_out, g_attn_out)
    x2 = _mm("nn", [mix], [w_o_full], F32, TM_MM, 512, "out_proj", add=x0)
    h2 = _rms_fwd(x2, g_ffn, "rms_ffn")
    up = _mm("nn", [h2], [c_up], F32, TM_MM, n_up, "up_proj", b_chips=True)
    act, dx3, dx3_b, loss_row, gg_final = _ffn_down_loss(up, w_ffn_full, w_down_full, x2, target,
                                                         g_final.reshape(1, d))

    dact = _mm("nt", [dx3_b], [w_down_full], F32, TM_MM, 1408, "d_act")
    gw_down = _mm_tn(act, dx3_b, 1408, 1024, "gw_down")
    dup, gwf_lin, gwf_gate = _ffn_act_bwd(up, dact, w_ffn_full)
    dh2 = _mm("nt", [(dup, j // 2, j % 2, n_up) for j in range(N_CHIPS)], [(c_up, j) for j in range(N_CHIPS)],
              F32, 512, 512, "d_h2")
    gw_up = _mm_tn(h2, dup, 1024, n_up, "gw_up", out_chips=True)
    dx2, dx2_b, gg_ffn = _rms_bwd(x2, dh2, g_ffn, dx3, "rms_ffn_bwd", True)
    dmix = _mm("nt", [dx2_b], [w_o_full], F32, TM_MM, 512, "d_mix")
    gw_o = _mm_tn(mix, dx2_b, 1024, 1024, "gw_o")
    dz_a, dox, gw_conv, gg_conv_out, gg_attn_out = _mixer_bwd(z_a, o_attn, dmix, w_conv_full, g_conv_out, g_attn_out)
    dk, dv, aux, dqt = _attn_bwd_t(qx, dox, kx, kxt, vx, lse, last_blk)
    dq, dfq = _attn_dq_finish(dqt)
    d_f = aux[:, ::HEAD_DIM] + jnp.transpose(dfq[:, :, 0:2, :], (1, 3, 0, 2)).reshape(s, N_HEADS)
    df_b, gb_f = _gate_bwd(f_log, b_pad, jnp.pad(d_f, ((0, 0), (0, LANES - N_HEADS))))
    dh1 = _mm("nt", [dz_a, dq, dk, dv, df_b], [w_a, w_q, w_k, w_v, w_c], F32, TM_MM, 512, "d_h1")
    gw_a = _mm_tn(h1, dz_a, 1024, c3, "gw_in_conv")
    gw_q = _mm_tn(h1, dq, 1024, ATTN_W, "gw_in_q")
    gw_k = _mm_tn(h1, dk, 1024, ATTN_W, "gw_in_k")
    gw_v = _mm_tn(h1, dv, 1024, ATTN_W, "gw_in_v")
    gw_c = _mm_tn(h1, df_b, 1024, LANES, "gw_in_gate")
    grad_x, gg_mix = _rms_bwd(x0, dh1, g_mix, dx2, "rms_mix_bwd", False)

    gw_in = jnp.concatenate([gw_a, gw_q, gw_k, gw_v, gw_c[:, :N_HEADS]], axis=1)
    n_in = IN_COLS // N_CHIPS
    gw_in = jnp.stack([gw_in[:, j * n_in:(j + 1) * n_in] for j in range(N_CHIPS)])
    g_w_in, g_w_o, g_w_up, g_w_down = _reduce_scatter_grads(
        [gw_in, gw_o.reshape(N_CHIPS, d // N_CHIPS, d), gw_up, gw_down.reshape(N_CHIPS, D_FF // N_CHIPS, d)],
        ["w_in", "w_o", "w_up", "w_down"])

    gw_ffn = jnp.concatenate([gwf_lin, gwf_gate], axis=1)
    small_parts = [gg_mix, gg_conv_out, gg_attn_out, gg_ffn, gg_final, gb_f[:, :N_HEADS], loss_row[:, 0:1], gw_conv,
                   gw_ffn]
    small_shapes = [a.shape for a in small_parts]
    tot = _unflatten(_all_reduce_small(_flat_rows(small_parts, d, 8)), small_shapes)
    g_g_mix, g_g_conv_out, g_g_attn_out, g_g_ffn, g_g_final, g_b_f, loss_sum, g_conv_full, g_ffn_full = tot
    loss = loss_sum[0, 0]
    g_g_final = g_g_final[0]
    g_w_conv = lax.dynamic_slice_in_dim(g_conv_full, my_chip * (CONV_CH // N_CHIPS), CONV_CH // N_CHIPS, axis=1)
    g_w_ffn = lax.dynamic_slice_in_dim(g_ffn_full, my_chip * n_up, n_up, axis=1)

    def adam_big(w, g, m, v, name):
        dl, nm, nv = _adamw(w[0], g, m[0], v[0], name)
        return dl[None], nm[None], nv[None]

    u_w_in = adam_big(w_in, g_w_in, m_w_in, v_w_in, "adam_w_in")
    u_w_o = adam_big(w_o, g_w_o, m_w_o, v_w_o, "adam_w_o")
    u_w_up = adam_big(w_up, g_w_up, m_w_up, v_w_up, "adam_w_up")
    u_w_down = adam_big(w_down, g_w_down, m_w_down, v_w_down, "adam_w_down")

    small_w = [g_mix, b_f, g_conv_out, g_attn_out, g_ffn, g_final, w_conv, w_ffn_conv]
    small_g = [g_g_mix, g_b_f, g_g_conv_out, g_g_attn_out, g_g_ffn, g_g_final, g_w_conv, g_w_ffn]
    small_m = [m_g_mix, m_b_f, m_g_conv_out, m_g_attn_out, m_g_ffn, m_g_final, m_w_conv, m_w_ffn_conv]
    small_v = [v_g_mix, v_b_f, v_g_conv_out, v_g_attn_out, v_g_ffn, v_g_final, v_w_conv, v_w_ffn_conv]
    shapes = [a.shape for a in small_w]
    pack = lambda arrs: _flat_rows(arrs, LANES, 8)
    sd, sm, sv = _adamw(pack(small_w), pack(small_g), pack(small_m), pack(small_v), "adam_small")
    sd, sm, sv = _unflatten(sd, shapes), _unflatten(sm, shapes), _unflatten(sv, shapes)
    (d_g_mix, d_b_f, d_g_conv_out, d_g_attn_out, d_g_ffn, d_g_final, d_w_conv, d_w_ffn) = sd
    (nm_g_mix, nm_b_f, nm_g_conv_out, nm_g_attn_out, nm_g_ffn, nm_g_final, nm_w_conv, nm_w_ffn) = sm
    (nv_g_mix, nv_b_f, nv_g_conv_out, nv_g_attn_out, nv_g_ffn, nv_g_final, nv_w_conv, nv_w_ffn) = sv

    grads = (g_g_mix, g_w_in[None], g_b_f, g_w_conv[None], g_g_conv_out, g_g_attn_out, g_w_o[None], g_g_ffn,
             g_w_up[None], g_w_ffn[None], g_w_down[None], g_g_final)
    deltas = (d_g_mix, u_w_in[0], d_b_f, d_w_conv, d_g_conv_out, d_g_attn_out, u_w_o[0], d_g_ffn, u_w_up[0],
              d_w_ffn, u_w_down[0], d_g_final)
    new_m = (nm_g_mix, u_w_in[1], nm_b_f, nm_w_conv, nm_g_conv_out, nm_g_attn_out, u_w_o[1], nm_g_ffn, u_w_up[1],
             nm_w_ffn, u_w_down[1], nm_g_final)
    new_v = (nv_g_mix, u_w_in[2], nv_b_f, nv_w_conv, nv_g_conv_out, nv_g_attn_out, u_w_o[2], nv_g_ffn, u_w_up[2],
             nv_w_ffn, u_w_down[2], nv_g_final)
    return (loss, grad_x[None], *grads, *deltas, *new_m, *new_v)
```

```python
import functools

import jax
import jax.numpy as jnp
from jax import lax
from jax.experimental import pallas as pl
from jax.experimental.pallas import tpu as pltpu

F32, BF16 = jnp.float32, jnp.bfloat16
MESH = pl.DeviceIdType.MESH

D_MODEL = 1024
CONV_CH = 512
ATTN_W = 512
N_HEADS = 8
HEAD_DIM = 64
N_PAIRS = N_HEADS // 2
D_FF = 2816
IN_COLS = 3 * CONV_CH + 3 * ATTN_W + N_HEADS
EPS = 1e-6
Q_SCALE = 0.125
EXP_ZERO = 104.0
N_CHIPS = 4
LANES = 128
HALO = 8

ADAM_LR, ADAM_B1, ADAM_B2, ADAM_EPS, ADAM_WD, ADAM_STEP = 0.001, 0.9, 0.999, 1e-08, 0.01, 10

TM_ROWS = 512
TM_MM = 1024
TK_TN = 512
TQ = 512
ROW_CHUNK = 32
TM_FFN = 1024
TN_FFN = 256
VMEM_LIMIT = 52 * 2**20


def _cp(sem, vmem=VMEM_LIMIT):
    return pltpu.CompilerParams(dimension_semantics=sem, vmem_limit_bytes=vmem)


def _bf(a):
    return a if a.dtype == BF16 else a.astype(BF16)


def _mm(mode, a_list, b_list, out_dtype, tm, tn, name, add=None, b_chips=False, scatter=()):
    n_p = len(a_list)
    a0 = a_list[0]
    m_dim = a0[0].shape[1] if isinstance(a0, tuple) else a0.shape[0]
    b0 = b_list[0]
    if b_chips:
        n_dim = b0.shape[0] * b0.shape[2]
        assert tn == b0.shape[2] and mode == "nn"
    else:
        b0 = b0[0][b0[1]] if isinstance(b0, tuple) else b0
        n_dim = b0.shape[1 if mode == "nn" else 0]
    tm, tn = min(tm, m_dim), min(tn, n_dim)
    assert m_dim % tm == 0 and n_dim % tn == 0
    dims = (((1,), (0,)), ((), ())) if mode == "nn" else (((1,), (1,)), ((), ()))
    in_specs, args = [], []
    for a in a_list:
        if isinstance(a, tuple):
            arr, lead, col, width = a
            in_specs.append(pl.BlockSpec((None, tm, width), lambda m, n, lead=lead, col=col: (lead, m, col)))
        else:
            arr = a
            in_specs.append(pl.BlockSpec((tm, a.shape[1]), lambda m, n: (m, 0)))
        args.append(arr)
    for b in b_list:
        if b_chips:
            arr = b
            in_specs.append(pl.BlockSpec((None, b.shape[1], tn), lambda m, n: (n, 0, 0)))
        elif isinstance(b, tuple):
            arr, lead = b
            if mode == "nn":
                in_specs.append(pl.BlockSpec((None, arr.shape[1], tn), lambda m, n, lead=lead: (lead, 0, n)))
            else:
                in_specs.append(pl.BlockSpec((None, tn, arr.shape[2]), lambda m, n, lead=lead: (lead, n, 0)))
        elif mode == "nn":
            arr = b
            in_specs.append(pl.BlockSpec((b.shape[0], tn), lambda m, n: (0, n)))
        else:
            arr = b
            in_specs.append(pl.BlockSpec((tn, b.shape[1]), lambda m, n: (n, 0)))
        args.append(arr)
    if add is not None:
        in_specs.append(pl.BlockSpec((tm, tn), lambda m, n: (m, n)))
        args.append(add)

    n_in = len(args)
    n_sc = len(scatter)
    grid = (m_dim // tm, n_dim // tn)

    def body(*refs):
        o_ref = refs[n_in + n_sc]
        if n_sc:
            finish = _hosted_scatter(refs[n_in:n_in + n_sc], refs[n_in + n_sc + 1:n_in + 2 * n_sc + 1],
                                     refs[n_in + 2 * n_sc + 1:], pl.program_id(0) * grid[1] + pl.program_id(1),
                                     grid[0] * grid[1])
        acc = None
        for i in range(n_p):
            d = lax.dot_general(_bf(refs[i][...]), _bf(refs[n_p + i][...]), dims,
                                preferred_element_type=F32)
            acc = d if acc is None else acc + d
        if add is not None:
            acc = refs[2 * n_p][...] + acc
        o_ref[...] = acc.astype(out_dtype)
        if n_sc:
            finish()

    main_spec = pl.BlockSpec((tm, tn), lambda m, n: (m, n))
    main_shape = jax.ShapeDtypeStruct((m_dim, n_dim), out_dtype)
    if not n_sc:
        return pl.pallas_call(body, name=name, grid=grid, in_specs=in_specs, out_specs=main_spec,
                              out_shape=main_shape, compiler_params=_cp(("parallel", "parallel")))(*args)
    got_shapes, sems = _scatter_operands(scatter)
    outs = pl.pallas_call(
        body, name=name, grid=grid, in_specs=in_specs + _hbm_specs(n_sc), out_specs=[main_spec] + _hbm_specs(n_sc),
        out_shape=[main_shape] + got_shapes, scratch_shapes=sems,
        compiler_params=_cp(("arbitrary", "arbitrary")))(*args, *scatter)
    return outs[0], outs[1:]


def _mm_tn(a, b, tm, tn, name, out_chips=False):
    k_dim, m_dim = a.shape
    n_dim = b.shape[-1] * (b.shape[0] if b.ndim == 3 else 1)
    tm, tn, tk = min(tm, m_dim), min(tn, b.shape[-1]), min(TK_TN, k_dim)
    assert m_dim % tm == 0 and b.shape[-1] % tn == 0 and k_dim % tk == 0
    per = b.shape[-1] // tn
    if b.ndim == 3:
        b_spec = pl.BlockSpec((None, tk, tn), lambda m, n, k: (n // per, k, n % per))
    else:
        b_spec = pl.BlockSpec((tk, tn), lambda m, n, k: (k, n))

    def body(a_ref, b_ref, o_ref):
        @pl.when(pl.program_id(2) == 0)
        def _():
            o_ref[...] = jnp.zeros_like(o_ref)
        o_ref[...] += lax.dot_general(_bf(a_ref[...]), _bf(b_ref[...]), (((0,), (0,)), ((), ())),
                                      preferred_element_type=F32)

    return pl.pallas_call(
        body, name=name, grid=(m_dim // tm, n_dim // tn, k_dim // tk),
        in_specs=[pl.BlockSpec((tk, tm), lambda m, n, k: (k, m)), b_spec],
        out_specs=(pl.BlockSpec((None, tm, tn), lambda m, n, k: (n, m, 0)) if out_chips
                   else pl.BlockSpec((tm, tn), lambda m, n, k: (m, n))),
        out_shape=jax.ShapeDtypeStruct((n_dim // tn, m_dim, tn) if out_chips else (m_dim, n_dim), F32),
        compiler_params=_cp(("parallel", "parallel", "arbitrary")))(a, b)


def _rstd(x):
    return lax.rsqrt(jnp.mean(x * x, axis=-1, keepdims=True) + EPS)


def _rms_fwd(x, g, name):
    s, d = x.shape
    tm = min(TM_ROWS, s)

    def body(x_ref, g_ref, h_ref):
        xv = x_ref[...]
        h_ref[...] = (xv * _rstd(xv) * g_ref[...]).astype(BF16)

    return pl.pallas_call(
        body, name=name, grid=(s // tm,),
        in_specs=[pl.BlockSpec((tm, d), lambda i: (i, 0)), pl.BlockSpec((1, d), lambda i: (0, 0))],
        out_specs=pl.BlockSpec((tm, d), lambda i: (i, 0)),
        out_shape=jax.ShapeDtypeStruct((s, d), BF16), compiler_params=_cp(("parallel",)))(x, g)


def _rms_bwd(x, dh, g, dres, name, with_bf16):
    s, d = x.shape
    tm = min(TM_ROWS, s)

    def body(x_ref, dh_ref, g_ref, dres_ref, dx_ref, *rest):
        gg_ref = rest[-1]

        @pl.when(pl.program_id(0) == 0)
        def _():
            gg_ref[...] = jnp.zeros_like(gg_ref)

        xv = x_ref[...]
        xn = xv * _rstd(xv)
        dhv = dh_ref[...]
        gg_ref[...] += jnp.sum(dhv * xn, axis=0, keepdims=True)
        t = dhv * g_ref[...]
        dx = dres_ref[...] + _rstd(xv) * (t - xn * jnp.mean(t * xn, axis=-1, keepdims=True))
        dx_ref[...] = dx
        if with_bf16:
            rest[0][...] = dx.astype(BF16)

    row = pl.BlockSpec((tm, d), lambda i: (i, 0))
    vec = pl.BlockSpec((1, d), lambda i: (0, 0))
    out_specs = [row] + ([row] if with_bf16 else []) + [vec]
    out_shape = ([jax.ShapeDtypeStruct((s, d), F32)] + ([jax.ShapeDtypeStruct((s, d), BF16)] if with_bf16 else [])
                 + [jax.ShapeDtypeStruct((1, d), F32)])
    return pl.pallas_call(
        body, name=name, grid=(s // tm,), in_specs=[row, row, vec, row], out_specs=out_specs, out_shape=out_shape,
        compiler_params=_cp(("arbitrary",)))(x, dh, g, dres)


def _loss_head(x3, target, g):
    s, d = x3.shape
    tm = min(TM_ROWS, s)

    def body(x_ref, t_ref, g_ref, dx_ref, dxb_ref, loss_ref, gg_ref):
        @pl.when(pl.program_id(0) == 0)
        def _():
            gg_ref[...] = jnp.zeros_like(gg_ref)
            loss_ref[...] = jnp.zeros_like(loss_ref)

        xv = x_ref[...]
        r = _rstd(xv)
        xn = xv * r
        gv = g_ref[...]
        err = xn * gv - t_ref[...]
        loss_ref[...] += 0.5 * jnp.sum(jnp.mean(err * err, axis=-1, keepdims=True), axis=0, keepdims=True)
        dy = err * (1.0 / d)
        gg_ref[...] += jnp.sum(dy * xn, axis=0, keepdims=True)
        t = dy * gv
        dx = r * (t - xn * jnp.mean(t * xn, axis=-1, keepdims=True))
        dx_ref[...] = dx
        dxb_ref[...] = dx.astype(BF16)

    row = pl.BlockSpec((tm, d), lambda i: (i, 0))
    vec = pl.BlockSpec((1, d), lambda i: (0, 0))
    return pl.pallas_call(
        body, name="loss_head", grid=(s // tm,), in_specs=[row, row, vec],
        out_specs=[row, row, pl.BlockSpec((1, LANES), lambda i: (0, 0)), vec],
        out_shape=[jax.ShapeDtypeStruct((s, d), F32), jax.ShapeDtypeStruct((s, d), BF16),
                   jax.ShapeDtypeStruct((1, LANES), F32), jax.ShapeDtypeStruct((1, d), F32)],
        compiler_params=_cp(("arbitrary",)))(x3, target, g)


def _prev_halo_spec(tm, width, col):
    return pl.BlockSpec((HALO, width), lambda i, *_: (jnp.maximum(i * (tm // HALO) - 1, 0), col))


def _next_halo_spec(tm, width, col, s):
    return pl.BlockSpec((HALO, width), lambda i, *_: (jnp.minimum((i + 1) * (tm // HALO), s // HALO - 1), col))


def _shift_down(x, k):
    return pltpu.roll(x, k, 0)


def _shift_up(x, k):
    return pltpu.roll(x, x.shape[0] - k, 0)


def _conv_taps(x_ext, w):
    return w[0:1, :] * _shift_down(x_ext, 2) + w[1:2, :] * _shift_down(x_ext, 1) + w[2:3, :] * x_ext


def _conv_taps_t(d_ext, w):
    return w[2:3, :] * d_ext + w[1:2, :] * _shift_up(d_ext, 1) + w[0:1, :] * _shift_up(d_ext, 2)


def _mixer_fwd(z_a, o_attn, w_conv, g_conv_out, g_attn_out):
    s = z_a.shape[0]
    c = CONV_CH
    tm = min(TM_ROWS, s)

    def body(gb_ref, gc_ref, xc_ref, gcp_ref, xcp_ref, o_ref, w_ref, gco_ref, gao_ref, mix_ref):
        i = pl.program_id(0)
        cx = gc_ref[...] * xc_ref[...]
        cx_prev = jnp.where(i > 0, gcp_ref[...] * xcp_ref[...], 0.0)
        conv = _conv_taps(jnp.concatenate([cx_prev, cx], axis=0), w_ref[...])[HALO:]
        y = gb_ref[...] * conv
        mix_ref[:, 0:c] = (y * _rstd(y) * gco_ref[...]).astype(BF16)
        o = o_ref[...]
        mix_ref[:, c:2 * c] = (o * _rstd(o) * gao_ref[...]).astype(BF16)

    col = lambda j: pl.BlockSpec((tm, c), lambda i: (i, j))
    vec = pl.BlockSpec((1, c), lambda i: (0, 0))
    return pl.pallas_call(
        body, name="mixer_fwd", grid=(s // tm,),
        in_specs=[col(0), col(1), col(2), _prev_halo_spec(tm, c, 1), _prev_halo_spec(tm, c, 2), col(0),
                  pl.BlockSpec((3, c), lambda i: (0, 0)), vec, vec],
        out_specs=pl.BlockSpec((tm, 2 * c), lambda i: (i, 0)),
        out_shape=jax.ShapeDtypeStruct((s, 2 * c), BF16),
        compiler_params=_cp(("parallel",)))(z_a, z_a, z_a, z_a, z_a, o_attn, w_conv, g_conv_out, g_attn_out)


def _mixer_bwd(z_a, o_attn, dmix, w_conv, g_conv_out, g_attn_out, scatter):
    s = z_a.shape[0]
    c = CONV_CH
    tm = min(TM_ROWS, s)
    n_blk = s // tm
    n_sc = len(scatter)
    got_shapes, sems = _scatter_operands(scatter)

    def body(gb_ref, gc_ref, xc_ref, gcp_ref, xcp_ref, gbn_ref, gcn_ref, xcn_ref, o_ref, dnc_ref, dncn_ref, dna_ref,
             w_ref, gco_ref, gao_ref, *rest):
        dz_ref, dox_ref, gw_ref, ggco_ref, ggao_ref = rest[n_sc:n_sc + 5]
        i = pl.program_id(0)
        finish = _hosted_scatter(rest[:n_sc], rest[n_sc + 5:2 * n_sc + 5], rest[2 * n_sc + 5:], i, n_blk)

        @pl.when(i == 0)
        def _():
            gw_ref[...] = jnp.zeros_like(gw_ref)
            ggco_ref[...] = jnp.zeros_like(ggco_ref)
            ggao_ref[...] = jnp.zeros_like(ggao_ref)

        w = w_ref[...]
        zeros = jnp.zeros((HALO, c), F32)
        gb_e = jnp.concatenate([zeros, gb_ref[...], gbn_ref[...]], axis=0)
        cx_prev = jnp.where(i > 0, gcp_ref[...] * xcp_ref[...], 0.0)
        gc_e = jnp.concatenate([zeros, gc_ref[...], gcn_ref[...]], axis=0)
        xc_e = jnp.concatenate([zeros, xc_ref[...], xcn_ref[...]], axis=0)
        cx_e = jnp.concatenate([cx_prev, gc_ref[...] * xc_ref[...], gcn_ref[...] * xcn_ref[...]], axis=0)
        dn_next = jnp.where(i < n_blk - 1, dncn_ref[...], 0.0)
        dn_e = jnp.concatenate([zeros, dnc_ref[...], dn_next], axis=0)

        conv_e = _conv_taps(cx_e, w)
        y_e = gb_e * conv_e
        r_e = _rstd(y_e)
        yn_e = y_e * r_e
        t_e = dn_e * gco_ref[...]
        dy_e = r_e * (t_e - yn_e * jnp.mean(t_e * yn_e, axis=-1, keepdims=True))
        dconv_e = dy_e * gb_e
        dcx_e = _conv_taps_t(dconv_e, w)
        blk = slice(HALO, HALO + tm)
        dz_ref[:, 0:c] = (dy_e * conv_e)[blk].astype(BF16)
        dz_ref[:, c:2 * c] = (dcx_e * xc_e)[blk].astype(BF16)
        dz_ref[:, 2 * c:3 * c] = (dcx_e * gc_e)[blk].astype(BF16)
        ggco_ref[...] += jnp.sum((dn_e * yn_e)[blk], axis=0, keepdims=True)
        dconv = dconv_e[blk]
        gw_ref[0:1, :] += jnp.sum(dconv * _shift_down(cx_e, 2)[blk], axis=0, keepdims=True)
        gw_ref[1:2, :] += jnp.sum(dconv * _shift_down(cx_e, 1)[blk], axis=0, keepdims=True)
        gw_ref[2:3, :] += jnp.sum(dconv * cx_e[blk], axis=0, keepdims=True)

        o = o_ref[...]
        ra = _rstd(o)
        on = o * ra
        dna = dna_ref[...]
        ggao_ref[...] += jnp.sum(dna * on, axis=0, keepdims=True)
        ta = dna * gao_ref[...]
        do = ra * (ta - on * jnp.mean(ta * on, axis=-1, keepdims=True))
        prod = do * o
        lane = lax.broadcasted_iota(jnp.int32, (tm, LANES), 1)
        head_a = lane < HEAD_DIM
        for p in range(N_PAIRS):
            cols = slice(p * LANES, (p + 1) * LANES)
            pb, dob = prod[:, cols], do[:, cols]
            for hh in range(2):
                sel = head_a if hh == 0 else jnp.logical_not(head_a)
                delta = jnp.sum(jnp.where(sel, pb, 0.0), axis=-1, keepdims=True)
                neg3 = _split3(-delta)
                do_h = pltpu.roll(dob, HEAD_DIM, 1) if hh else dob
                dox_ref[2 * p + hh] = _aug(do_h, lane, neg3).astype(BF16)
        finish()

    col = lambda j: pl.BlockSpec((tm, c), lambda i: (i, j))
    vec = pl.BlockSpec((1, c), lambda i: (0, 0))
    w3 = pl.BlockSpec((3, c), lambda i: (0, 0))
    outs = pl.pallas_call(
        body, name="mixer_bwd", grid=(n_blk,),
        in_specs=[col(0), col(1), col(2), _prev_halo_spec(tm, c, 1), _prev_halo_spec(tm, c, 2),
                  _next_halo_spec(tm, c, 0, s), _next_halo_spec(tm, c, 1, s), _next_halo_spec(tm, c, 2, s),
                  col(0), col(0), _next_halo_spec(tm, c, 0, s), col(1), w3, vec, vec] + _hbm_specs(n_sc),
        out_specs=[pl.BlockSpec((tm, 3 * c), lambda i: (i, 0)),
                   pl.BlockSpec((N_HEADS, tm, LANES), lambda i: (0, i, 0)), w3, vec, vec] + _hbm_specs(n_sc),
        out_shape=[jax.ShapeDtypeStruct((s, 3 * c), BF16), jax.ShapeDtypeStruct((N_HEADS, s, LANES), BF16),
                   jax.ShapeDtypeStruct((3, c), F32), jax.ShapeDtypeStruct((1, c), F32),
                   jax.ShapeDtypeStruct((1, c), F32)] + got_shapes,
        scratch_shapes=sems, compiler_params=_cp(("arbitrary",)))(
            z_a, z_a, z_a, z_a, z_a, z_a, z_a, z_a, o_attn, dmix, dmix, dmix, w_conv, g_conv_out, g_attn_out,
            *scatter)
    return tuple(outs[:5]) + (outs[5:],)


def _gate_fwd(f, b_pad):
    s = f.shape[0]
    tm = min(TQ, s)

    def body(f_ref, b_ref, fb_ref, carry):
        @pl.when(pl.program_id(0) == 0)
        def _():
            carry[...] = jnp.zeros_like(carry)

        z = f_ref[...] + b_ref[...]
        x = jnp.minimum(z, 0.0) - jnp.log1p(jnp.exp(-jnp.abs(z)))
        row = lax.broadcasted_iota(jnp.int32, (tm, LANES), 0)
        sh = 1
        while sh < tm:
            x = x + jnp.where(row >= sh, _shift_down(x, sh), 0.0)
            sh *= 2
        x = x + carry[0:1, :]
        carry[...] = jnp.broadcast_to(x[tm - 1:tm, :], carry.shape)
        head_a = lax.broadcasted_iota(jnp.int32, (tm, LANES), 1) < HEAD_DIM
        for p in range(N_PAIRS):
            fa = jnp.broadcast_to(x[:, 2 * p:2 * p + 1], (tm, LANES))
            fbv = jnp.broadcast_to(x[:, 2 * p + 1:2 * p + 2], (tm, LANES))
            fb_ref[:, p * LANES:(p + 1) * LANES] = jnp.where(head_a, fa, fbv)

    return pl.pallas_call(
        body, name="gate_fwd", grid=(s // tm,),
        in_specs=[pl.BlockSpec((tm, LANES), lambda i: (i, 0)), pl.BlockSpec((1, LANES), lambda i: (0, 0))],
        out_specs=pl.BlockSpec((tm, N_PAIRS * LANES), lambda i: (i, 0)),
        out_shape=jax.ShapeDtypeStruct((s, N_PAIRS * LANES), F32),
        scratch_shapes=[pltpu.VMEM((HALO, LANES), F32)],
        compiler_params=_cp(("arbitrary",)))(f, b_pad)


def _gate_bwd(f, b_pad, d_f):
    s = f.shape[0]
    tm = min(TQ, s)
    n_blk = s // tm

    def body(f_ref, b_ref, d_ref, df_ref, gb_ref, carry):
        @pl.when(pl.program_id(0) == 0)
        def _():
            carry[...] = jnp.zeros_like(carry)
            gb_ref[...] = jnp.zeros_like(gb_ref)

        x = d_ref[...]
        row = lax.broadcasted_iota(jnp.int32, (tm, LANES), 0)
        sh = 1
        while sh < tm:
            x = x + jnp.where(row < tm - sh, _shift_up(x, sh), 0.0)
            sh *= 2
        x = x + carry[0:1, :]
        carry[...] = jnp.broadcast_to(x[0:1, :], carry.shape)
        z = f_ref[...] + b_ref[...]
        d = x * (1.0 / (1.0 + jnp.exp(z)))
        df_ref[...] = d.astype(BF16)
        gb_ref[...] += jnp.sum(d, axis=0, keepdims=True)

    rev = pl.BlockSpec((tm, LANES), lambda i: (n_blk - 1 - i, 0))
    vec = pl.BlockSpec((1, LANES), lambda i: (0, 0))
    return pl.pallas_call(
        body, name="gate_bwd", grid=(n_blk,), in_specs=[rev, vec, rev], out_specs=[rev, vec],
        out_shape=[jax.ShapeDtypeStruct((s, LANES), BF16), jax.ShapeDtypeStruct((1, LANES), F32)],
        scratch_shapes=[pltpu.VMEM((HALO, LANES), F32)],
        compiler_params=_cp(("arbitrary",)))(f, b_pad, d_f)


_NT = (((1,), (1,)), ((), ()))
_NN = (((1,), (0,)), ((), ()))
_TN = (((0,), (0,)), ((), ()))


def _head_masks(shape):
    lane = lax.broadcasted_iota(jnp.int32, shape, len(shape) - 1)
    return lane < HEAD_DIM


def _pick_row(ft, h):
    rows = lax.broadcasted_iota(jnp.int32, ft.shape, 0)
    return jnp.sum(jnp.where(rows == h, ft, 0.0), axis=0, keepdims=True)


def _prune_bounds(qkv, fb):
    s = qkv.shape[0]
    tq = min(TQ, s)

    def body(q_ref, k_ref, fb_ref, out_ref):
        head_a = _head_masks((tq, LANES))
        lane = lax.broadcasted_iota(jnp.int32, (HALO, LANES), 1)
        acc = jnp.zeros((HALO, LANES), F32)
        for p in range(N_PAIRS):
            cols = slice(p * LANES, (p + 1) * LANES)
            q2 = q_ref[:, cols].astype(F32) * Q_SCALE
            k2 = k_ref[:, cols].astype(F32)
            f2 = fb_ref[:, cols]
            for hh in range(2):
                sel = head_a if hh == 0 else jnp.logical_not(head_a)
                qn = jnp.sqrt(jnp.sum(jnp.where(sel, q2 * q2, 0.0), axis=-1, keepdims=True))
                kn = jnp.sqrt(jnp.sum(jnp.where(sel, k2 * k2, 0.0), axis=-1, keepdims=True))
                f = f2[:, hh * HEAD_DIM:hh * HEAD_DIM + 1]
                h = 2 * p + hh
                vals = (jnp.max(qn, axis=0, keepdims=True), jnp.max(kn, axis=0, keepdims=True),
                        jnp.max(qn * kn + f, axis=0, keepdims=True), f[tq - 1:tq, :])
                for slot, v in enumerate(vals):
                    acc = jnp.where(lane == slot * N_HEADS + h, v, acc)
        out_ref[0] = acc

    blk = lambda j: pl.BlockSpec((tq, ATTN_W), lambda i: (i, j))
    return pl.pallas_call(
        body, name="prune_bounds", grid=(s // tq,), in_specs=[blk(0), blk(1), blk(0)],
        out_specs=pl.BlockSpec((1, HALO, LANES), lambda i: (i, 0, 0)),
        out_shape=jax.ShapeDtypeStruct((s // tq, HALO, LANES), F32),
        compiler_params=_cp(("parallel",)))(qkv, qkv, fb)


def _first_key_blocks(qkv, fb):
    t = _prune_bounds(qkv, fb)[:, 0, :]
    nh = N_HEADS
    a, b, c, e = t[:, 0:nh], t[:, nh:2 * nh], t[:, 2 * nh:3 * nh], t[:, 3 * nh:4 * nh]
    bound = a[:, None, :] * b[None, :, :] * 1.001 + c[:, None, :] - e[None, :, :]
    n_q = t.shape[0]
    idx = jnp.arange(n_q)
    need = jnp.logical_not(bound < -(EXP_ZERO + 2.0)) | (idx[None, :, None] >= idx[:, None, None])
    first = jnp.argmax(need, axis=1).astype(jnp.int32)
    return jnp.min(first.reshape(n_q, N_PAIRS, 2), axis=-1).T.reshape(-1)


def _attn_fwd(qkv, fb, ft, first_blk):
    s = qkv.shape[0]
    tq = min(TQ, s)
    n_q = s // tq
    neg = -1e30

    rc = min(ROW_CHUNK, tq)

    def body(first_ref, q_ref, k_ref, v_ref, fb_ref, ft_ref, o_ref, g_ref,
             s_a, s_b, p_ab, m_a, l_a, m_b, l_b, alpha2, acc, pmax_a, psum_a, pmax_b, psum_b):
        p = pl.program_id(0)
        i = pl.program_id(1)
        head_a = _head_masks((tq, LANES))
        q2 = q_ref[...] * Q_SCALE
        zero = jnp.zeros_like(q2)
        q_a, q_b = jnp.where(head_a, q2, zero), jnp.where(head_a, zero, q2)
        for m_scr, l_scr in ((m_a, l_a), (m_b, l_b)):
            m_scr[...] = jnp.full(m_scr.shape, neg, F32)
            l_scr[...] = jnp.zeros(l_scr.shape, F32)
        acc[...] = jnp.zeros(acc.shape, F32)

        def step(kb, masked):
            rows_k = pl.ds(pl.multiple_of(kb * tq, tq), tq)
            k2, v2 = k_ref[rows_k, :], v_ref[rows_k, :]
            ftv = ft_ref[kb]
            fk = (_pick_row(ftv, 2 * p), _pick_row(ftv, 2 * p + 1))
            s_a[...] = lax.dot_general(q_a, k2, _NT, preferred_element_type=F32)
            s_b[...] = lax.dot_general(q_b, k2, _NT, preferred_element_type=F32)
            alphas = []
            for hh, (s_scr, m_scr, l_scr, pmax, psum) in enumerate(((s_a, m_a, l_a, pmax_a, psum_a),
                                                                     (s_b, m_b, l_b, pmax_b, psum_b))):
                fq = fb_ref[:, hh * HEAD_DIM:hh * HEAD_DIM + 1]

                def shifted(r, ncol, add=None):
                    rows = slice(r * rc, (r + 1) * rc)
                    t = s_scr[rows, 0:ncol]
                    if add is not None:
                        t = t + add[rows, :]
                    t = t - fk[hh][:, 0:ncol]
                    if masked:
                        col_id = lax.broadcasted_iota(jnp.int32, (rc, ncol), 1)
                        row_id = lax.broadcasted_iota(jnp.int32, (rc, ncol), 0) + r * rc
                        t = jnp.where(col_id <= row_id, t, -jnp.inf)
                    return rows, t

                def lane_blocks(t, op):
                    out = t[:, 0:LANES]
                    for cb in range(1, t.shape[1] // LANES):
                        out = op(out, t[:, cb * LANES:(cb + 1) * LANES])
                    return out

                ncols = [min(tq, -(-((r + 1) * rc) // LANES) * LANES) if masked else tq for r in range(tq // rc)]
                for r, ncol in enumerate(ncols):
                    rows, t = shifted(r, ncol)
                    pmax[rows, :] = lane_blocks(t, jnp.maximum)
                m_old = m_scr[...]
                m_new = jnp.maximum(m_old, jnp.max(pmax[...], axis=-1, keepdims=True) + fq)
                alpha = jnp.exp(m_old - m_new)
                m_scr[...] = m_new
                shift = fq - m_new
                for r, ncol in enumerate(ncols):
                    rows, t = shifted(r, ncol, shift)
                    pr = jnp.exp(t)
                    psum[rows, :] = lane_blocks(pr, jnp.add)
                    p_ab[rows, hh * tq:hh * tq + ncol] = pr.astype(BF16)
                    if ncol < tq:
                        p_ab[rows, hh * tq + ncol:(hh + 1) * tq] = jnp.zeros((rc, tq - ncol), BF16)
                l_scr[...] = alpha * l_scr[...] + jnp.sum(psum[...], axis=-1, keepdims=True)
                alphas.append(alpha)
            alpha2[...] = jnp.where(head_a, alphas[0], alphas[1])
            zv = jnp.zeros_like(v2)
            head_k = _head_masks(v2.shape)
            vv = jnp.concatenate([jnp.where(head_k, v2, zv), jnp.where(head_k, zv, v2)], axis=0)
            acc[...] = acc[...] * alpha2[...] + lax.dot_general(p_ab[...], vv, _NN, preferred_element_type=F32)

        def unmasked(kb, carry):
            step(kb, False)
            return carry

        lax.fori_loop(first_ref[p * n_q + i], i, unmasked, 0)
        step(i, True)
        fq2 = fb_ref[...]
        o_ref[...] = acc[...] / jnp.where(head_a, l_a[...], l_b[...])
        g_ref[...] = fq2 - jnp.where(head_a, m_a[...] + jnp.log(l_a[...]), m_b[...] + jnp.log(l_b[...]))

    qblk = lambda off: pl.BlockSpec((tq, LANES), lambda p, i, first: (i, off + p))
    full = lambda off: pl.BlockSpec((s, LANES), lambda p, i, first: (0, off + p))
    col_scr = pltpu.VMEM((tq, 1), F32)
    grid_spec = pltpu.PrefetchScalarGridSpec(
        num_scalar_prefetch=1, grid=(N_PAIRS, n_q),
        in_specs=[qblk(0), full(N_PAIRS), full(2 * N_PAIRS), qblk(0),
                  pl.BlockSpec((n_q, N_HEADS, tq), lambda p, i, first: (0, 0, 0))],
        out_specs=[qblk(0), qblk(0)],
        scratch_shapes=[pltpu.VMEM((tq, tq), F32), pltpu.VMEM((tq, tq), F32), pltpu.VMEM((tq, 2 * tq), BF16),
                        col_scr, col_scr, col_scr, col_scr] + [pltpu.VMEM((tq, LANES), F32)] * 6)
    return pl.pallas_call(
        body, name="attn_fwd", grid_spec=grid_spec,
        out_shape=[jax.ShapeDtypeStruct((s, ATTN_W), F32), jax.ShapeDtypeStruct((s, ATTN_W), F32)],
        compiler_params=_cp(("parallel", "arbitrary")))(first_blk, qkv, qkv, qkv, fb, ft)


def _attn_bwd(qkv, do, g, delta, ft, first_blk):
    s = qkv.shape[0]
    tq = min(TQ, s)
    n_q = s // tq

    def body(first_ref, q_ref, do_ref, g_ref, dl_ref, k_ref, v_ref, ft_ref, dq_ref, dfq_ref, dk_ref, dv_ref,
             dfk_ref):
        p = pl.program_id(0)
        i = pl.program_id(1)

        @pl.when(i == 0)
        def _():
            dk_ref[...] = jnp.zeros_like(dk_ref)
            dv_ref[...] = jnp.zeros_like(dv_ref)
            dfk_ref[...] = jnp.zeros_like(dfk_ref)

        head_a = _head_masks((tq, LANES))
        q2 = q_ref[...] * Q_SCALE
        do2 = do_ref[...]
        zero = jnp.zeros_like(q2)
        q_a, q_b = jnp.where(head_a, q2, zero), jnp.where(head_a, zero, q2)
        do_a, do_b = jnp.where(head_a, do2, zero), jnp.where(head_a, zero, do2)
        g2, dl2 = g_ref[...], dl_ref[...]
        g_a, g_b = g2[:, 0:1], g2[:, HEAD_DIM:HEAD_DIM + 1]
        dl_a, dl_b = dl2[:, 0:1], dl2[:, HEAD_DIM:HEAD_DIM + 1]
        causal = lax.broadcasted_iota(jnp.int32, (tq, tq), 1) <= lax.broadcasted_iota(jnp.int32, (tq, tq), 0)
        rows8 = lax.broadcasted_iota(jnp.int32, (N_HEADS, tq), 0)

        def one_head(q_h, do_h, g_h, dl_h, fk_h, k2, v2, masked):
            sc = (lax.dot_general(q_h, k2, _NT, preferred_element_type=F32) + g_h) - fk_h
            pr = jnp.exp(sc)
            if masked:
                pr = jnp.where(causal, pr, 0.0)
            dp = lax.dot_general(do_h, v2, _NT, preferred_element_type=F32)
            ds = pr * (dp - dl_h)
            return (pr.astype(BF16), ds.astype(BF16), jnp.sum(ds, axis=0, keepdims=True),
                    jnp.sum(ds, axis=1, keepdims=True))

        def step(j, carry, masked):
            dq, r_a, r_b = carry
            rows = pl.ds(pl.multiple_of(j * tq, tq), tq)
            k2, v2 = k_ref[rows, :], v_ref[rows, :]
            k_a, k_b = jnp.where(head_a, k2, zero), jnp.where(head_a, zero, k2)
            ftv = ft_ref[j]
            p_a, ds_a, c_a, s_a = one_head(q_a, do_a, g_a, dl_a, _pick_row(ftv, 2 * p), k2, v2, masked)
            p_b, ds_b, c_b, s_b = one_head(q_b, do_b, g_b, dl_b, _pick_row(ftv, 2 * p + 1), k2, v2, masked)
            dv_ref[rows, :] += (lax.dot_general(p_a, do_a, _TN, preferred_element_type=F32)
                                + lax.dot_general(p_b, do_b, _TN, preferred_element_type=F32))
            dk_ref[rows, :] += (lax.dot_general(ds_a, q_a, _TN, preferred_element_type=F32)
                                + lax.dot_general(ds_b, q_b, _TN, preferred_element_type=F32))
            dfk_ref[0, j] += jnp.where(rows8 == 0, -c_a, jnp.where(rows8 == 1, -c_b, 0.0))
            dq = dq + (lax.dot_general(ds_a, k_a, _NN, preferred_element_type=F32)
                       + lax.dot_general(ds_b, k_b, _NN, preferred_element_type=F32))
            return dq, r_a + s_a, r_b + s_b

        zcol = jnp.zeros((tq, 1), F32)
        carry = lax.fori_loop(first_ref[p * n_q + i], i, lambda j, cr: step(j, cr, False),
                              (jnp.zeros((tq, LANES), F32), zcol, zcol))
        dq, r_a, r_b = step(i, carry, True)
        dq_ref[...] = (dq * Q_SCALE).astype(BF16)
        dfq_ref[...] = jnp.where(head_a, r_a, r_b)

    qblk = lambda off: pl.BlockSpec((tq, LANES), lambda p, i, first: (i, off + p))
    full = lambda off: pl.BlockSpec((s, LANES), lambda p, i, first: (0, off + p))
    grid_spec = pltpu.PrefetchScalarGridSpec(
        num_scalar_prefetch=1, grid=(N_PAIRS, n_q),
        in_specs=[qblk(0), qblk(0), qblk(0), qblk(0), full(N_PAIRS), full(2 * N_PAIRS),
                  pl.BlockSpec((n_q, N_HEADS, tq), lambda p, i, first: (0, 0, 0))],
        out_specs=[qblk(0), qblk(0), full(0), full(0),
                   pl.BlockSpec((1, n_q, N_HEADS, tq), lambda p, i, first: (p, 0, 0, 0))])
    return pl.pallas_call(
        body, name="attn_bwd", grid_spec=grid_spec,
        out_shape=[jax.ShapeDtypeStruct((s, ATTN_W), BF16), jax.ShapeDtypeStruct((s, ATTN_W), F32),
                   jax.ShapeDtypeStruct((s, ATTN_W), F32), jax.ShapeDtypeStruct((s, ATTN_W), F32),
                   jax.ShapeDtypeStruct((N_PAIRS, n_q, N_HEADS, tq), F32)],
        compiler_params=_cp(("parallel", "arbitrary")))(first_blk, qkv, do, g, delta, qkv, qkv, ft)


AUG = HEAD_DIM
NORM_MARGIN = 1.01


def _split3(x):
    hi = x.astype(BF16).astype(F32)
    r = x - hi
    mid = r.astype(BF16).astype(F32)
    lo = (r - mid).astype(BF16).astype(F32)
    return hi, mid, lo


def _aug(base, lane, vals):
    out = jnp.where(lane < AUG, base, 0.0)
    for k, v in enumerate(vals):
        out = jnp.where(lane == AUG + k, v, out)
    return out


def _attn_prep(qkv, fb, bigs):
    s = qkv.shape[0]
    tq = min(TQ, s)
    n_q = s // tq

    n = len(bigs)
    arrays, landing, sems = _gather_operands(bigs, [])

    def body(q_ref, k_ref, v_ref, fb_ref, *rest):
        qx_ref, kx_ref, kxt_ref, vx_ref, vt_ref, b_ref = rest[2 * n:2 * n + 6]
        finish = _hosted_gather((rest[:n], rest[2 * n + 6:3 * n + 6]) + tuple(rest[3 * n + 6:]), n, n,
                                pl.program_id(0), n_q)
        lane = lax.broadcasted_iota(jnp.int32, (tq, LANES), 1)
        lane8 = lax.broadcasted_iota(jnp.int32, (HALO, LANES), 1)
        head_lanes = lane < AUG
        is_lane = [lane == AUG + k for k in range(6)]
        first3 = (lane >= AUG) & (lane < AUG + 3)
        next3 = (lane >= AUG + 3) & (lane < AUG + 6)
        q_const = jnp.where(first3, -1.0, 0.0)
        k_const = jnp.where(next3, 1.0, 0.0)
        v_const = jnp.where(first3, 1.0, 0.0)
        ones_head = (lax.broadcasted_iota(jnp.int32, (LANES, LANES), 0) < HEAD_DIM).astype(BF16)
        acc = jnp.zeros((HALO, LANES), F32)
        for p in range(N_PAIRS):
            cols = slice(p * LANES, (p + 1) * LANES)
            q2, k2, v2 = (ref[:, cols].astype(F32) for ref in (q_ref, k_ref, v_ref))
            q2 = q2 * Q_SCALE
            f2 = fb_ref[:, cols]
            for hh in range(2):
                h = 2 * p + hh
                q, k, v = ((pltpu.roll(x, HEAD_DIM, 1) if hh else x) for x in (q2, k2, v2))
                f = f2 if hh else pltpu.roll(f2, HEAD_DIM, 1)
                hi, mid, lo = _split3(f)
                q_aug = jnp.where(is_lane[3], hi, jnp.where(is_lane[4], mid, jnp.where(is_lane[5], lo, q_const)))
                k_aug = jnp.where(is_lane[0], hi, jnp.where(is_lane[1], mid, jnp.where(is_lane[2], lo, k_const)))
                kx = jnp.where(head_lanes, k, k_aug)
                vx = jnp.where(head_lanes, v, v_const)
                qx_ref[h] = jnp.where(head_lanes, q, q_aug).astype(BF16)
                kx_ref[h] = kx.astype(BF16)
                vx_ref[h] = vx.astype(BF16)
                kxt_ref[h, 0] = kx.T.astype(BF16)
                vt_ref[h, 0] = vx.T.astype(BF16)
                q_sq = lax.dot_general((q * q).astype(BF16), ones_head, _NN, preferred_element_type=F32)
                k_sq = lax.dot_general((k * k).astype(BF16), ones_head, _NN, preferred_element_type=F32)
                qk = jnp.sqrt(q_sq * k_sq) * NORM_MARGIN + f
                vals = (jnp.sqrt(jnp.max(q_sq, axis=0, keepdims=True)), jnp.sqrt(jnp.max(k_sq, axis=0, keepdims=True)),
                        jnp.max(qk, axis=0, keepdims=True), f[tq - 1:tq, :])
                for slot, val in enumerate(vals):
                    acc = jnp.where(lane8 == slot * N_HEADS + h, val[:, AUG:AUG + 1], acc)
        b_ref[0] = acc
        finish()

    blk = lambda j: pl.BlockSpec((tq, ATTN_W), lambda i: (i, j))
    rows = pl.BlockSpec((N_HEADS, tq, LANES), lambda i: (0, i, 0))
    cols_t = pl.BlockSpec((N_HEADS, 1, LANES, tq), lambda i: (0, i, 0, 0))
    shp = jax.ShapeDtypeStruct((N_HEADS, s, LANES), BF16)
    shp_t = jax.ShapeDtypeStruct((N_HEADS, n_q, LANES, tq), BF16)
    outs = pl.pallas_call(
        body, name="attn_prep", grid=(n_q,), in_specs=[blk(0), blk(1), blk(2), blk(0)] + _hbm_specs(2 * n),
        out_specs=[rows, rows, cols_t, rows, cols_t,
                   pl.BlockSpec((1, HALO, LANES), lambda i: (i, 0, 0))] + _hbm_specs(n),
        out_shape=[shp, shp, shp_t, shp, shp_t, jax.ShapeDtypeStruct((n_q, HALO, LANES), F32)]
        + [jax.ShapeDtypeStruct(b.shape, b.dtype) for b in landing],
        input_output_aliases={4 + n + k: 6 + k for k in range(n)}, scratch_shapes=sems,
        compiler_params=_cp(("arbitrary",)))(qkv, qkv, qkv, fb, *arrays, *landing)
    return tuple(outs[:6]) + (outs[6:],)


def _key_block_ranges(bounds):
    t = bounds[:, 0, :]
    nh = N_HEADS
    a, b, c, e = t[:, 0:nh], t[:, nh:2 * nh], t[:, 2 * nh:3 * nh], t[:, 3 * nh:4 * nh]
    bound = a[:, None, :] * b[None, :, :] * NORM_MARGIN + c[:, None, :] - e[None, :, :]
    n_q = t.shape[0]
    idx = jnp.arange(n_q)
    need = jnp.logical_not(bound < -(EXP_ZERO + 2.0)) | (idx[None, :, None] >= idx[:, None, None])
    first = jnp.argmax(need, axis=1).astype(jnp.int32)
    first = jnp.min(first.reshape(n_q, N_PAIRS, 2), axis=-1)
    visits = (first[:, None, :] <= idx[None, :, None]) & (idx[None, :, None] <= idx[:, None, None])
    last = jnp.max(jnp.where(visits, idx[:, None, None], 0), axis=0).astype(jnp.int32)
    return first.T.reshape(-1), last.T.reshape(-1)


def _attn_fwd_t(qx, kx, vt, first_blk):
    _, s, _ = qx.shape
    tq = min(TQ, s)
    n_q = s // tq
    neg = -1e30

    def body(first_ref, qx_ref, kx_ref, vt_ref, o_ref, lse_ref, acc_ref, m_ref):
        p = pl.program_id(0)
        i = pl.program_id(1)
        acc_ref[...] = jnp.zeros(acc_ref.shape, F32)
        m_ref[...] = jnp.full(m_ref.shape, neg, F32)
        key_le_query = (lax.broadcasted_iota(jnp.int32, (tq, tq), 0) <= lax.broadcasted_iota(jnp.int32, (tq, tq), 1))

        def step(kb, masked):
            rows_k = pl.ds(pl.multiple_of(kb * tq, tq), tq)
            for hh in range(2):
                st = lax.dot_general(kx_ref[hh, rows_k, :], qx_ref[hh], _NT, preferred_element_type=F32)
                if masked:
                    st = jnp.where(key_le_query, st, -jnp.inf)
                m_old = m_ref[hh]
                m_new = jnp.maximum(m_old, jnp.max(st, axis=0, keepdims=True))
                m_ref[hh] = m_new
                pt = jnp.exp(st - m_new).astype(BF16)
                acc_ref[hh] = acc_ref[hh] * jnp.exp(m_old - m_new) + lax.dot_general(
                    vt_ref[hh, kb], pt, _NN, preferred_element_type=F32)

        def unmasked(kb, carry):
            step(kb, False)
            return carry

        lax.fori_loop(first_ref[p * n_q + i], i, unmasked, 0)
        step(i, True)
        outs, lses = [], []
        for hh in range(2):
            acc = acc_ref[hh]
            l = acc[AUG:AUG + 1, :]
            outs.append(acc[0:HEAD_DIM, :] / l)
            lses.append(m_ref[hh] + jnp.log(l))
        o_ref[...] = jnp.concatenate(outs, axis=0).T
        rows8 = lax.broadcasted_iota(jnp.int32, (N_HEADS, tq), 0)
        lse_ref[0, 0] = jnp.where(rows8 == 0, lses[0], jnp.where(rows8 == 1, lses[1], 0.0))

    grid_spec = pltpu.PrefetchScalarGridSpec(
        num_scalar_prefetch=1, grid=(N_PAIRS, n_q),
        in_specs=[pl.BlockSpec((2, tq, LANES), lambda p, i, first: (p, i, 0)),
                  pl.BlockSpec((2, s, LANES), lambda p, i, first: (p, 0, 0)),
                  pl.BlockSpec((2, n_q, LANES, tq), lambda p, i, first: (p, 0, 0, 0))],
        out_specs=[pl.BlockSpec((tq, LANES), lambda p, i, first: (i, p)),
                   pl.BlockSpec((1, 1, N_HEADS, tq), lambda p, i, first: (p, i, 0, 0))],
        scratch_shapes=[pltpu.VMEM((2, LANES, tq), F32), pltpu.VMEM((2, 1, tq), F32)])
    return pl.pallas_call(
        body, name="attn_fwd", grid_spec=grid_spec,
        out_shape=[jax.ShapeDtypeStruct((s, ATTN_W), F32), jax.ShapeDtypeStruct((N_PAIRS, n_q, N_HEADS, tq), F32)],
        compiler_params=_cp(("parallel", "arbitrary")))(first_blk, qx, kx, vt)


def _attn_bwd_t(qx, dox, kx, kxt, vx, lse, last_blk):
    _, s, _ = qx.shape
    tq = min(TQ, s)
    n_q = s // tq

    def body(last_ref, qx_ref, dox_ref, lse_ref, kx_ref, kxt_ref, vx_ref, dk_ref, dv_ref, dfk_ref, dqt_ref):
        p = pl.program_id(0)
        j = pl.program_id(1)

        @pl.when(j == 0)
        def _():
            dqt_ref[...] = jnp.zeros(dqt_ref.shape, F32)

        key_le_query = (lax.broadcasted_iota(jnp.int32, (tq, tq), 0) <= lax.broadcasted_iota(jnp.int32, (tq, tq), 1))

        def step(i, carry, masked):
            rows_q = pl.ds(pl.multiple_of(i * tq, tq), tq)
            out = []
            for hh in range(2):
                dk, dv = carry[2 * hh], carry[2 * hh + 1]
                q, do = qx_ref[hh, rows_q, :], dox_ref[hh, rows_q, :]
                st = lax.dot_general(kx_ref[hh], q, _NT, preferred_element_type=F32)
                pt = jnp.exp(st - lse_ref[0, i, hh:hh + 1, :])
                if masked:
                    pt = jnp.where(key_le_query, pt, 0.0)
                dst = pt * lax.dot_general(vx_ref[hh], do, _NT, preferred_element_type=F32)
                pb, dsb = pt.astype(BF16), dst.astype(BF16)
                dv = dv + lax.dot_general(pb, do, _NN, preferred_element_type=F32)
                dk = dk + lax.dot_general(dsb, q, _NN, preferred_element_type=F32)
                dqt_ref[hh, i] += lax.dot_general(kxt_ref[hh, 0], dsb, _NN, preferred_element_type=F32)
                out += [dk, dv]
            return tuple(out)

        zero = jnp.zeros((tq, LANES), F32)
        carry = step(j, (zero, zero, zero, zero), True)
        dk_a, dv_a, dk_b, dv_b = lax.fori_loop(j + 1, last_ref[p * n_q + j] + 1,
                                               lambda i, cr: step(i, cr, False), carry)
        head_a = lax.broadcasted_iota(jnp.int32, (tq, LANES), 1) < HEAD_DIM
        dk_ref[...] = jnp.where(head_a, dk_a, pltpu.roll(dk_b, HEAD_DIM, 1)).astype(BF16)
        dv_ref[...] = jnp.where(head_a, dv_a, pltpu.roll(dv_b, HEAD_DIM, 1)).astype(BF16)
        aux_t = jnp.where(head_a, pltpu.roll(dk_a, HEAD_DIM, 1), dk_b).T
        rows8 = lax.broadcasted_iota(jnp.int32, (N_HEADS, tq), 0)
        dfk_ref[0, 0] = jnp.where(rows8 == 0, aux_t[0:1], jnp.where(rows8 == 1, aux_t[HEAD_DIM:HEAD_DIM + 1], 0.0))

    resident = pl.BlockSpec((2, s, LANES), lambda p, j, last: (p, 0, 0))
    key_rows = pl.BlockSpec((2, tq, LANES), lambda p, j, last: (p, j, 0))
    pair_out = pl.BlockSpec((tq, LANES), lambda p, j, last: (j, p))
    grid_spec = pltpu.PrefetchScalarGridSpec(
        num_scalar_prefetch=1, grid=(N_PAIRS, n_q),
        in_specs=[resident, resident, pl.BlockSpec((1, n_q, N_HEADS, tq), lambda p, j, last: (p, 0, 0, 0)),
                  key_rows, pl.BlockSpec((2, 1, LANES, tq), lambda p, j, last: (p, j, 0, 0)), key_rows],
        out_specs=[pair_out, pair_out, pl.BlockSpec((1, 1, N_HEADS, tq), lambda p, j, last: (p, j, 0, 0)),
                   pl.BlockSpec((2, n_q, LANES, tq), lambda p, j, last: (p, 0, 0, 0))])
    return pl.pallas_call(
        body, name="attn_bwd", grid_spec=grid_spec,
        out_shape=[jax.ShapeDtypeStruct((s, ATTN_W), BF16), jax.ShapeDtypeStruct((s, ATTN_W), BF16),
                   jax.ShapeDtypeStruct((N_PAIRS, n_q, N_HEADS, tq), F32),
                   jax.ShapeDtypeStruct((N_HEADS, n_q, LANES, tq), F32)],
        compiler_params=_cp(("parallel", "arbitrary")))(last_blk, qx, dox, lse, kx, kxt, vx)


def _attn_dq_finish(dqt):
    _, n_q, _, tq = dqt.shape
    per = 4 if n_q % 4 == 0 else 1

    def body(dqt_ref, dq_ref, dfq_ref):
        rows8 = lax.broadcasted_iota(jnp.int32, (N_HEADS, tq), 0)
        for b in range(per):
            a, bb = dqt_ref[0, b], dqt_ref[1, b]
            dq_ref[b * tq:(b + 1) * tq, :] = (
                jnp.concatenate([a[0:HEAD_DIM], bb[0:HEAD_DIM]], axis=0).T * Q_SCALE).astype(BF16)
            dfq_ref[0, b] = jnp.where(rows8 == 0, a[AUG + 3:AUG + 4],
                                      jnp.where(rows8 == 1, bb[AUG + 3:AUG + 4], 0.0))

    return pl.pallas_call(
        body, name="attn_dq_finish", grid=(N_PAIRS, n_q // per),
        in_specs=[pl.BlockSpec((2, per, LANES, tq), lambda p, i: (p, i, 0, 0))],
        out_specs=[pl.BlockSpec((per * tq, LANES), lambda p, i: (i, p)),
                   pl.BlockSpec((1, per, N_HEADS, tq), lambda p, i: (p, i, 0, 0))],
        out_shape=[jax.ShapeDtypeStruct((n_q * tq, ATTN_W), BF16),
                   jax.ShapeDtypeStruct((N_PAIRS, n_q, N_HEADS, tq), F32)],
        compiler_params=_cp(("parallel", "parallel")))(dqt)


def _ffn_act_fwd(up, w_ffn):
    s = up.shape[0]
    tm, tn = min(TM_FFN, s), TN_FFN
    nb = D_FF // tn

    def body(a_ref, g_ref, ap_ref, gp_ref, wa_ref, wg_ref, act_ref):
        i = pl.program_id(1)

        def conv(blk_ref, prev_ref, w_ref):
            prev = jnp.where(i > 0, prev_ref[...], 0.0)
            return _conv_taps(jnp.concatenate([prev, blk_ref[...]], axis=0), w_ref[...])[HALO:]

        u_a, u_g = conv(a_ref, ap_ref, wa_ref), conv(g_ref, gp_ref, wg_ref)
        act_ref[...] = (u_g * (1.0 / (1.0 + jnp.exp(-u_g))) * u_a).astype(BF16)

    blk = lambda off: pl.BlockSpec((tm, tn), lambda n, i: (i, off + n))
    prev = lambda off: pl.BlockSpec((HALO, tn), lambda n, i: (jnp.maximum(i * (tm // HALO) - 1, 0), off + n))
    wsp = lambda off: pl.BlockSpec((3, tn), lambda n, i: (0, off + n))
    return pl.pallas_call(
        body, name="ffn_act_fwd", grid=(nb, s // tm),
        in_specs=[blk(0), blk(nb), prev(0), prev(nb), wsp(0), wsp(nb)],
        out_specs=pl.BlockSpec((tm, tn), lambda n, i: (i, n)),
        out_shape=jax.ShapeDtypeStruct((s, D_FF), BF16),
        compiler_params=_cp(("parallel", "parallel")))(up, up, up, up, w_ffn, w_ffn)


def _ffn_down_loss(up, w_ffn, w_down, x2, target, g_final):
    s, d = x2.shape
    tm, tn = min(TM_ROWS, s), TN_FFN
    nb = D_FF // tn

    def body(a_ref, g_ref, ap_ref, gp_ref, wa_ref, wg_ref, wd_ref, x2_ref, t_ref, gf_ref,
             act_ref, dx_ref, dxb_ref, loss_ref, gg_ref, acc, act_even, act_odd):
        i = pl.program_id(0)
        k = pl.program_id(1)

        @pl.when((i == 0) & (k == 0))
        def _():
            gg_ref[...] = jnp.zeros_like(gg_ref)
            loss_ref[...] = jnp.zeros_like(loss_ref)

        def conv(blk_ref, prev_ref, w_ref):
            prev = jnp.where(i > 0, prev_ref[...], 0.0)
            return _conv_taps(jnp.concatenate([prev, blk_ref[...]], axis=0), w_ref[...])[HALO:]

        def activation(dst):
            u_a, u_g = conv(a_ref, ap_ref, wa_ref), conv(g_ref, gp_ref, wg_ref)
            act = (u_g * (1.0 / (1.0 + jnp.exp(-u_g))) * u_a).astype(BF16)
            act_ref[...] = act
            dst[...] = act

        def project(src, first):
            part = lax.dot_general(src[...], wd_ref[...], _NN, preferred_element_type=F32)
            acc[...] = part if first else acc[...] + part

        @pl.when(k == 0)
        def _():
            activation(act_even)

        @pl.when(k == 1)
        def _():
            project(act_even, True)
            activation(act_odd)

        @pl.when((k > 1) & (k < nb) & (k % 2 == 0))
        def _():
            project(act_odd, False)
            activation(act_even)

        @pl.when((k > 1) & (k < nb) & (k % 2 == 1))
        def _():
            project(act_even, False)
            activation(act_odd)

        @pl.when(k == nb)
        def _():
            project(act_even if nb % 2 == 1 else act_odd, False)
            xv = x2_ref[...] + acc[...]
            r = _rstd(xv)
            xn = xv * r
            gv = gf_ref[...]
            err = xn * gv - t_ref[...]
            loss_ref[...] += 0.5 * jnp.sum(jnp.mean(err * err, axis=-1, keepdims=True), axis=0, keepdims=True)
            dy = err * (1.0 / d)
            gg_ref[...] += jnp.sum(dy * xn, axis=0, keepdims=True)
            t = dy * gv
            dx = r * (t - xn * jnp.mean(t * xn, axis=-1, keepdims=True))
            dx_ref[...] = dx
            dxb_ref[...] = dx.astype(BF16)

    chunk = lambda k: jnp.minimum(k, nb - 1)
    blk = lambda off: pl.BlockSpec((tm, tn), lambda i, k: (i, off + chunk(k)))
    prev = lambda off: pl.BlockSpec((HALO, tn),
                                    lambda i, k: (jnp.maximum(i * (tm // HALO) - 1, 0), off + chunk(k)))
    wsp = lambda off: pl.BlockSpec((3, tn), lambda i, k: (0, off + chunk(k)))
    row = pl.BlockSpec((tm, d), lambda i, k: (i, 0))
    vec = pl.BlockSpec((1, d), lambda i, k: (0, 0))
    return pl.pallas_call(
        body, name="ffn_down_loss", grid=(s // tm, nb + 1),
        in_specs=[blk(0), blk(nb), prev(0), prev(nb), wsp(0), wsp(nb),
                  pl.BlockSpec((tn, d), lambda i, k: (jnp.maximum(k - 1, 0), 0)), row, row, vec],
        out_specs=[pl.BlockSpec((tm, tn), lambda i, k: (i, chunk(k))), row, row,
                   pl.BlockSpec((1, LANES), lambda i, k: (0, 0)), vec],
        out_shape=[jax.ShapeDtypeStruct((s, D_FF), BF16), jax.ShapeDtypeStruct((s, d), F32),
                   jax.ShapeDtypeStruct((s, d), BF16), jax.ShapeDtypeStruct((1, LANES), F32),
                   jax.ShapeDtypeStruct((1, d), F32)],
        scratch_shapes=[pltpu.VMEM((tm, d), F32), pltpu.VMEM((tm, tn), BF16), pltpu.VMEM((tm, tn), BF16)],
        compiler_params=_cp(("arbitrary", "arbitrary")))(up, up, up, up, w_ffn, w_ffn, w_down, x2, target, g_final)


def _ffn_act_bwd(up, dact, w_ffn):
    s = up.shape[0]
    tm, tn = min(TM_FFN, s), TN_FFN
    nb = D_FF // tn
    n_blk = s // tm

    def body(a_ref, g_ref, ap_ref, gp_ref, an_ref, gn_ref, d_ref, dn_ref, wa_ref, wg_ref,
             dup_ref, gwa_ref, gwg_ref):
        i = pl.program_id(1)

        @pl.when(i == 0)
        def _():
            gwa_ref[...] = jnp.zeros_like(gwa_ref)
            gwg_ref[...] = jnp.zeros_like(gwg_ref)

        def ext(blk_ref, prev_ref, next_ref):
            return jnp.concatenate([jnp.where(i > 0, prev_ref[...], 0.0), blk_ref[...], next_ref[...]], axis=0)

        wa, wg = wa_ref[...], wg_ref[...]
        up_a, up_g = ext(a_ref, ap_ref, an_ref), ext(g_ref, gp_ref, gn_ref)
        u_a, u_g = _conv_taps(up_a, wa), _conv_taps(up_g, wg)
        d_e = jnp.concatenate([jnp.zeros((HALO, tn), F32), d_ref[...],
                               jnp.where(i < n_blk - 1, dn_ref[...], 0.0)], axis=0)
        sig = 1.0 / (1.0 + jnp.exp(-u_g))
        du_a = d_e * (u_g * sig)
        du_g = d_e * u_a * (sig * (1.0 + u_g * (1.0 - sig)))
        blk = slice(HALO, HALO + tm)
        dup_ref[0] = _conv_taps_t(du_a, wa)[blk].astype(BF16)
        dup_ref[1] = _conv_taps_t(du_g, wg)[blk].astype(BF16)
        for gw_ref, upv, du in ((gwa_ref, up_a[blk], du_a), (gwg_ref, up_g[blk], du_g)):
            gw_ref[0:1, :] += jnp.sum(upv * _shift_up(du, 2)[blk], axis=0, keepdims=True)
            gw_ref[1:2, :] += jnp.sum(upv * _shift_up(du, 1)[blk], axis=0, keepdims=True)
            gw_ref[2:3, :] += jnp.sum(upv * du[blk], axis=0, keepdims=True)

    blk = lambda off: pl.BlockSpec((tm, tn), lambda n, i: (i, off + n))
    prev = lambda off: pl.BlockSpec((HALO, tn), lambda n, i: (jnp.maximum(i * (tm // HALO) - 1, 0), off + n))
    nxt = lambda off: pl.BlockSpec(
        (HALO, tn), lambda n, i: (jnp.minimum((i + 1) * (tm // HALO), s // HALO - 1), off + n))
    wsp = lambda off: pl.BlockSpec((3, tn), lambda n, i: (0, off + n))
    return pl.pallas_call(
        body, name="ffn_act_bwd", grid=(nb, n_blk),
        in_specs=[blk(0), blk(nb), prev(0), prev(nb), nxt(0), nxt(nb), blk(0), nxt(0), wsp(0), wsp(nb)],
        out_specs=[pl.BlockSpec((2, tm, tn), lambda n, i: (0, i, n)), wsp(0), wsp(0)],
        out_shape=[jax.ShapeDtypeStruct((2, s, D_FF), BF16),
                   jax.ShapeDtypeStruct((3, D_FF), F32), jax.ShapeDtypeStruct((3, D_FF), F32)],
        compiler_params=_cp(("parallel", "arbitrary")))(up, up, up, up, up, up, dact, dact, w_ffn, w_ffn)


def _adamw(w, g, m, v, name):
    r, c = w.shape
    tr = next((t for t in (512, 352, 256, 128, 64, 32, 16, 8) if r > t and r % t == 0), r)

    def body(w_ref, g_ref, m_ref, v_ref, d_ref, nm_ref, nv_ref):
        gv = g_ref[...]
        m_new = ADAM_B1 * m_ref[...] + (1.0 - ADAM_B1) * gv
        v_new = ADAM_B2 * v_ref[...] + (1.0 - ADAM_B2) * (gv * gv)
        m_hat = m_new / (1.0 - ADAM_B1 ** ADAM_STEP)
        v_hat = v_new / (1.0 - ADAM_B2 ** ADAM_STEP)
        d_ref[...] = -ADAM_LR * (m_hat / (jnp.sqrt(v_hat) + ADAM_EPS) + ADAM_WD * w_ref[...])
        nm_ref[...] = m_new
        nv_ref[...] = v_new

    spec = pl.BlockSpec((tr, c), lambda i: (i, 0))
    shp = jax.ShapeDtypeStruct((r, c), F32)
    return pl.pallas_call(
        body, name=name, grid=(r // tr,), in_specs=[spec] * 4, out_specs=[spec] * 3, out_shape=[shp] * 3,
        compiler_params=_cp(("parallel",)))(w, g, m, v)


def _sum_rows_block(h):
    return h if h <= 352 else 256


def _pair_sum(view, recv, sel, name):
    n, _, h, c = view.shape
    tr = _sum_rows_block(h)

    def body(sel_ref, a_ref, b_ref, o_ref, ob_ref):
        t = a_ref[...] + b_ref[...]
        o_ref[...] = t
        ob_ref[...] = t.astype(BF16)

    blk = pl.BlockSpec((None, tr, c), lambda j, i, sel_ref: (j, i, 0))
    grid_spec = pltpu.PrefetchScalarGridSpec(
        num_scalar_prefetch=1, grid=(n, h // tr),
        in_specs=[pl.BlockSpec((None, None, tr, c), lambda j, i, sel_ref: (j, sel_ref[0], i, 0)),
                  pl.BlockSpec((None, None, tr, c), lambda j, i, sel_ref: (j, 0, i, 0))],
        out_specs=[blk, blk])
    return pl.pallas_call(
        body, name=name, grid_spec=grid_spec,
        out_shape=[jax.ShapeDtypeStruct((n, h, c), F32), jax.ShapeDtypeStruct((n, h, c), BF16)],
        compiler_params=_cp(("parallel", "parallel")))(sel, view, recv)


def _chip_sum(pair, got, sel, name):
    _, h, c = pair.shape
    tr = _sum_rows_block(h)
    nblk = h // tr

    def body(sel_ref, p_ref, g0_ref, g1_ref, g2_ref, o_ref):
        o_ref[...] = ((p_ref[...] + g0_ref[...].astype(F32)) + g1_ref[...].astype(F32)) + g2_ref[...].astype(F32)

    slot = lambda k: pl.BlockSpec((None, tr, c), lambda i, sel_ref: (k, i, 0))
    grid_spec = pltpu.PrefetchScalarGridSpec(
        num_scalar_prefetch=1, grid=(h // tr,),
        in_specs=[pl.BlockSpec((None, tr, c), lambda i, sel_ref: (sel_ref[1], i, 0)), slot(0), slot(1), slot(2)],
        out_specs=pl.BlockSpec((tr, c), lambda i, sel_ref: (sel_ref[0] * nblk + i, 0)))
    return pl.pallas_call(
        body, name=name, grid_spec=grid_spec, out_shape=jax.ShapeDtypeStruct((2 * h, c), F32),
        compiler_params=_cp(("parallel",)))(sel, pair, got, got, got)


def _place():
    return lax.axis_index("x"), lax.axis_index("y"), lax.axis_index("c")


def _other_chips(x, y):
    return [(1 - x, y), (x, 1 - y), (1 - x, 1 - y)]


def _hbm_specs(n):
    return [pl.BlockSpec(memory_space=pl.ANY)] * n


def _all_gather_weights(bigs, smalls):
    nb, ns = len(bigs), len(smalls)
    n = nb + ns

    def body(*refs):
        start, forward, finish = _gather_phases(refs[:n], refs[2 * n:3 * n], nb, *refs[3 * n:])
        start()
        forward()
        finish()

    arrays, landing, sems = _gather_operands(bigs, smalls)
    return pl.pallas_call(
        body, name="all_gather_weights",
        out_shape=[jax.ShapeDtypeStruct(b.shape, b.dtype) for b in landing],
        in_specs=_hbm_specs(2 * n), out_specs=_hbm_specs(n), input_output_aliases={n + k: k for k in range(n)},
        scratch_shapes=sems)(*arrays, *landing)


def _hosted_gather(refs, n, nb, step, total):
    ins, outs, send_sems, recv_sems = refs
    start, forward, finish = _gather_phases(ins, outs, nb, send_sems, recv_sems)
    pl.when(step == 0)(start)
    pl.when(step == (3 * total) // 4)(forward)
    return lambda: pl.when(step == total - 1)(finish)


def _in_proj(h1, w_a, w_b, w_c, bigs, smalls):
    s, d = h1.shape
    tm, tn = min(TM_MM, s), ATTN_W
    na, nq = w_a.shape[1] // tn, w_b.shape[1] // tn
    steps = na + nq + 1
    total = (s // tm) * steps
    nb, n = len(bigs), len(bigs) + len(smalls)
    arrays, landing, sems = _gather_operands(bigs, smalls)

    def body(h_ref, wa_ref, wb_ref, wc_ref, *rest):
        z_ref, qkv_ref, f_ref = rest[2 * n:2 * n + 3]
        m, j = pl.program_id(0), pl.program_id(1)
        finish = _hosted_gather((rest[:n], rest[2 * n + 3:3 * n + 3]) + tuple(rest[3 * n + 3:]), n, nb,
                                m * steps + j, total)
        h = h_ref[...]

        @pl.when(j < na)
        def _():
            z_ref[...] = lax.dot_general(h, wa_ref[...], _NN, preferred_element_type=F32)

        @pl.when((j >= na) & (j < na + nq))
        def _():
            qkv_ref[...] = lax.dot_general(h, wb_ref[...], _NN, preferred_element_type=F32).astype(BF16)

        @pl.when(j == na + nq)
        def _():
            f_ref[...] = lax.dot_general(h, wc_ref[...], _NN, preferred_element_type=F32)

        finish()

    blk_a = lambda m, j: (m, jnp.minimum(j, na - 1))
    blk_b = lambda m, j: (m, jnp.clip(j - na, 0, nq - 1))
    outs = pl.pallas_call(
        body, name="in_proj", grid=(s // tm, steps),
        in_specs=[pl.BlockSpec((tm, d), lambda m, j: (m, 0)),
                  pl.BlockSpec((d, tn), lambda m, j: (0, jnp.minimum(j, na - 1))),
                  pl.BlockSpec((d, tn), lambda m, j: (0, jnp.clip(j - na, 0, nq - 1))),
                  pl.BlockSpec((d, LANES), lambda m, j: (0, 0))] + _hbm_specs(2 * n),
        out_specs=[pl.BlockSpec((tm, tn), blk_a), pl.BlockSpec((tm, tn), blk_b),
                   pl.BlockSpec((tm, LANES), lambda m, j: (m, 0))] + _hbm_specs(n),
        out_shape=[jax.ShapeDtypeStruct((s, w_a.shape[1]), F32), jax.ShapeDtypeStruct((s, w_b.shape[1]), BF16),
                   jax.ShapeDtypeStruct((s, LANES), F32)] + [jax.ShapeDtypeStruct(b.shape, b.dtype) for b in landing],
        input_output_aliases={4 + n + k: 3 + k for k in range(n)}, scratch_shapes=sems,
        compiler_params=_cp(("arbitrary", "arbitrary")))(h1, w_a, w_b, w_c, *arrays, *landing)
    return outs[0], outs[1], outs[2], outs[3:]


def _gather_operands(bigs, smalls):
    x, y, _ = _place()
    arrays = list(bigs) + list(smalls)
    landing = [lax.dynamic_update_index_in_dim(lax.empty((N_CHIPS,) + a.shape, a.dtype), a, 2 * x + y, 0)
               for a in arrays]
    n_sems = 6 * len(bigs) + 3 * len(smalls)
    return arrays, landing, [pltpu.SemaphoreType.DMA((n_sems,)), pltpu.SemaphoreType.DMA((n_sems,))]


def _gather_phases(ins, outs, nb, send_sems, recv_sems):
    n = len(ins)
    x, y, c = _place()
    my_chip = 2 * x + y
    chips = _other_chips(x, y)
    sibling = (x, y, 1 - c)

    def rows(k, which):
        h = ins[k].shape[0] // 2
        return pl.ds(which * h, h)

    def copy(sem, src, dst, to):
        return pltpu.make_async_remote_copy(src_ref=src, dst_ref=dst, send_sem=send_sems.at[sem],
                                            recv_sem=recv_sems.at[sem], device_id=to, device_id_type=MESH)

    def sends():
        out = [copy(6 * k + j, ins[k].at[rows(k, c)], outs[k].at[my_chip, rows(k, c)], (cx, cy, c))
               for k in range(nb) for j, (cx, cy) in enumerate(chips)]
        return out + [copy(6 * nb + 3 * (k - nb) + j, ins[k], outs[k].at[my_chip], (cx, cy, c))
                      for k in range(nb, n) for j, (cx, cy) in enumerate(chips)]

    def landed(k, j, which):
        cx, cy = chips[j]
        return outs[k].at[2 * cx + cy, rows(k, which)]

    def forwards():
        return [copy(6 * k + 3 + j, landed(k, j, c), landed(k, j, c), sibling)
                for j in range(3) for k in range(nb)]

    def start():
        for cp in sends():
            cp.start()

    def forward():
        for j in range(3):
            for k in range(nb):
                copy(6 * k + j, landed(k, j, c), landed(k, j, c), (x, y, c)).wait_recv()
                copy(6 * k + 3 + j, landed(k, j, c), landed(k, j, c), sibling).start()

    def finish():
        for j, (cx, cy) in enumerate(chips):
            for k in range(nb):
                copy(6 * k + 3 + j, landed(k, j, 1 - c), landed(k, j, 1 - c), (x, y, c)).wait_recv()
            for k in range(nb, n):
                arrived = outs[k].at[2 * cx + cy]
                copy(6 * nb + 3 * (k - nb) + j, arrived, arrived, (x, y, c)).wait_recv()
        for cp in sends() + forwards():
            cp.wait_send()

    return start, forward, finish


def _pair_exchange(views, name):
    n = len(views)

    def body(*refs):
        ins, outs, send_sems, recv_sems = refs[:n], refs[n:2 * n], refs[2 * n], refs[2 * n + 1]
        x, y, c = _place()
        copies = [pltpu.make_async_remote_copy(
            src_ref=ins[k].at[:, pl.ds(1 - c, 1)], dst_ref=outs[k], send_sem=send_sems.at[k],
            recv_sem=recv_sems.at[k], device_id=(x, y, 1 - c), device_id_type=MESH) for k in range(n)]
        for cp in copies:
            cp.start()
        for cp in copies:
            cp.wait()

    return pl.pallas_call(
        body, name=name,
        out_shape=[jax.ShapeDtypeStruct((v.shape[0], 1) + v.shape[2:], v.dtype) for v in views],
        in_specs=_hbm_specs(n), out_specs=_hbm_specs(n),
        scratch_shapes=[pltpu.SemaphoreType.DMA((n,)), pltpu.SemaphoreType.DMA((n,))])(*views)


def _scatter_to_chips(parts):
    n = len(parts)

    def body(*refs):
        start, finish = _scatter_phases(refs[:n], refs[n:2 * n], refs[2 * n], refs[2 * n + 1])
        start()
        finish()

    shapes, sems = _scatter_operands(parts)
    return pl.pallas_call(
        body, name="scatter_grads", out_shape=shapes, in_specs=_hbm_specs(n), out_specs=_hbm_specs(n),
        scratch_shapes=sems)(*parts)


def _scatter_operands(parts):
    n = len(parts)
    return ([jax.ShapeDtypeStruct((3,) + p.shape[1:], p.dtype) for p in parts],
            [pltpu.SemaphoreType.DMA((3 * n,)), pltpu.SemaphoreType.DMA((3 * n,))])


def _scatter_phases(ins, outs, send_sems, recv_sems):
    x, y, c = _place()

    def copies():
        return [pltpu.make_async_remote_copy(
            src_ref=ins[k].at[pl.ds(2 * cx + cy, 1)], dst_ref=outs[k].at[pl.ds(r, 1)], send_sem=send_sems.at[3 * k + r],
            recv_sem=recv_sems.at[3 * k + r], device_id=(cx, cy, c), device_id_type=MESH)
            for k in range(len(ins)) for r, (cx, cy) in enumerate(_other_chips(x, y))]

    def start():
        for cp in copies():
            cp.start()

    def finish():
        for cp in copies():
            cp.wait()

    return start, finish


def _hosted_scatter(ins, outs, sems, step, total):
    start, finish = _scatter_phases(ins, outs, *sems)
    pl.when(step == 0)(start)
    return lambda: pl.when(step == total - 1)(finish)


def _join_halves(shards):
    n = len(shards)

    def body(*refs):
        ins, outs, send_sems, recv_sems = refs[:n], refs[n:2 * n], refs[2 * n], refs[2 * n + 1]
        x, y, c = _place()

        def rows(ref, which):
            h = ref.shape[0] // 2
            return ref.at[pl.ds(which * h, h)]

        sent = [pltpu.make_async_remote_copy(
            src_ref=rows(ins[k], c), dst_ref=rows(outs[k], c), send_sem=send_sems.at[k], recv_sem=recv_sems.at[k],
            device_id=(x, y, 1 - c), device_id_type=MESH) for k in range(n)]
        for cp in sent:
            cp.start()
        for k in range(n):
            pltpu.make_async_remote_copy(
                src_ref=rows(ins[k], 1 - c), dst_ref=rows(outs[k], 1 - c), send_sem=send_sems.at[k],
                recv_sem=recv_sems.at[k], device_id=(x, y, 1 - c), device_id_type=MESH).wait_recv()
        for cp in sent:
            cp.wait_send()

    return pl.pallas_call(
        body, name="half_exchange", out_shape=[jax.ShapeDtypeStruct(a.shape, a.dtype) for a in shards],
        in_specs=_hbm_specs(n), out_specs=_hbm_specs(n), input_output_aliases={k: k for k in range(n)},
        scratch_shapes=[pltpu.SemaphoreType.DMA((n,)), pltpu.SemaphoreType.DMA((n,))])(*shards)


def _all_reduce_small(packet):
    rows, width = packet.shape
    n_dev = 8

    def body(x_ref, out_ref, gath, send_sems, recv_sems):
        x, y, c = _place()
        me, sibling = (x, y, c), (x, y, 1 - c)
        chips = _other_chips(x, y)

        def slot(px, py, pc):
            return gath.at[pl.ds((4 * px + 2 * py + pc) * rows, rows), :]

        def copy(k, block, to, src=None):
            return pltpu.make_async_remote_copy(
                src_ref=slot(*block) if src is None else src, dst_ref=slot(*block), send_sem=send_sems.at[k],
                recv_sem=recv_sems.at[k], device_id=to, device_id_type=MESH)

        first = [copy(0, me, sibling, src=x_ref)]
        first += [copy(1 + j, me, (*chip, c), src=x_ref) for j, chip in enumerate(chips)]
        for cp in first:
            cp.start()
        gath[pl.ds((4 * x + 2 * y + c) * rows, rows), :] = x_ref[...]
        passed = [copy(4 + j, (*chip, c), sibling) for j, chip in enumerate(chips)]
        for j, chip in enumerate(chips):
            copy(1 + j, (*chip, c), me).wait_recv()
            passed[j].start()
        copy(0, sibling, me).wait_recv()
        for j, chip in enumerate(chips):
            copy(4 + j, (*chip, 1 - c), me).wait_recv()
        for cp in first + passed:
            cp.wait_send()
        acc = gath[0:rows, :]
        for d in range(1, n_dev):
            acc = acc + gath[d * rows:(d + 1) * rows, :]
        out_ref[...] = acc

    return pl.pallas_call(
        body, name="all_reduce_small", out_shape=jax.ShapeDtypeStruct((rows, width), F32),
        in_specs=[pl.BlockSpec(memory_space=pltpu.VMEM)], out_specs=pl.BlockSpec(memory_space=pltpu.VMEM),
        scratch_shapes=[pltpu.VMEM((n_dev * rows, width), F32), pltpu.SemaphoreType.DMA((7,)),
                        pltpu.SemaphoreType.DMA((7,))])(packet)


def _flat_rows(parts, width, row_multiple):
    flat = jnp.concatenate([p.astype(F32).reshape(-1) for p in parts])
    rows = -(-flat.shape[0] // width)
    rows = -(-rows // row_multiple) * row_multiple
    return jnp.pad(flat, (0, rows * width - flat.shape[0])).reshape(rows, width)


def _unflatten(flat2d, shapes):
    flat = flat2d.reshape(-1)
    out, off = [], 0
    for shp in shapes:
        n = 1
        for dim in shp:
            n *= dim
        out.append(flat[off:off + n].reshape(shp))
        off += n
    return out


def _core_and_chip():
    x, y, c = _place()
    return jnp.stack([c, 2 * x + y]).astype(jnp.int32)


def _pair_sums(chip_major, names, call_name):
    views = [g.reshape(N_CHIPS, 2, g.shape[1] // 2, g.shape[2]) for g in chip_major]
    recv = _pair_exchange(views, call_name)
    sel = _core_and_chip()
    return [_pair_sum(v, r, sel, "pair_sum_" + nm) for v, r, nm in zip(views, recv, names)]


def _finish_grads(pairs, got, names):
    sel = _core_and_chip()
    return _join_halves([_chip_sum(p, g, sel, "chip_sum_" + nm) for (p, _), g, nm in zip(pairs, got, names)])


def kernel(x, g_mix, w_in, b_f, w_conv, g_conv_out, g_attn_out, w_o, g_ffn, w_up, w_ffn_conv, w_down, g_final, loss_target, m_g_mix, m_w_in, m_b_f, m_w_conv, m_g_conv_out, m_g_attn_out, m_w_o, m_g_ffn, m_w_up, m_w_ffn_conv, m_w_down, m_g_final, v_g_mix, v_w_in, v_b_f, v_w_conv, v_g_conv_out, v_g_attn_out, v_w_o, v_g_ffn, v_w_up, v_w_ffn_conv, v_w_down, v_g_final):
    s = x.shape[1]
    x0 = x[0]
    target = loss_target[0]
    d = D_MODEL
    x_pos, y_pos, _ = _place()
    my_chip = 2 * x_pos + y_pos

    (c_in,) = _all_gather_weights([w_in[0].astype(BF16)], [])
    w_in_full = jnp.concatenate([c_in[j] for j in range(N_CHIPS)], axis=1)
    c3 = 3 * CONV_CH
    w_a, w_b = w_in_full[:, :c3], w_in_full[:, c3:c3 + 3 * ATTN_W]
    w_c = jnp.pad(w_in_full[:, c3 + 3 * ATTN_W:], ((0, 0), (0, LANES - N_HEADS)))
    w_q, w_k, w_v = (w_b[:, i * ATTN_W:(i + 1) * ATTN_W] for i in range(3))
    b_pad = jnp.pad(b_f, ((0, 0), (0, LANES - N_HEADS)))

    h1 = _rms_fwd(x0, g_mix, "rms_mix")
    z_a, qkv, f_log, (c_o, c_up, c_conv, c_ffn) = _in_proj(
        h1, w_a, w_b, w_c, [w_o[0].astype(BF16), w_up[0].astype(BF16)], [w_conv[0], w_ffn_conv[0]])
    fb = _gate_fwd(f_log, b_pad)
    qx, kx, kxt, vx, vt, bounds, (c_down,) = _attn_prep(qkv, fb, [w_down[0].astype(BF16)])
    w_o_full = c_o.reshape(d, d)
    w_down_full = c_down.reshape(D_FF, d)
    w_conv_full = jnp.concatenate([c_conv[j] for j in range(N_CHIPS)], axis=1)
    w_ffn_full = jnp.concatenate([c_ffn[j] for j in range(N_CHIPS)], axis=1)
    n_up = c_up.shape[2]
    first_blk, last_blk = _key_block_ranges(bounds)
    o_attn, lse = _attn_fwd_t(qx, kx, vt, first_blk)
    mix = _mixer_fwd(z_a, o_attn, w_conv_full, g_conv_out, g_attn_out)
    x2 = _mm("nn", [mix], [w_o_full], F32, TM_MM, 512, "out_proj", add=x0)
    h2 = _rms_fwd(x2, g_ffn, "rms_ffn")
    up = _mm("nn", [h2], [c_up], F32, TM_MM, n_up, "up_proj", b_chips=True)
    act = _ffn_act_fwd(up, w_ffn_full)
    x3 = _mm("nn", [act], [w_down_full], F32, 512, 512, "down_proj", add=x2)
    dx3, dx3_b, loss_row, gg_final = _loss_head(x3, target, g_final.reshape(1, d))

    dact = _mm("nt", [dx3_b], [w_down_full], F32, TM_MM, 1408, "d_act")
    gw_down = _mm_tn(act, dx3_b, 1408, 1024, "gw_down")
    dup, gwf_lin, gwf_gate = _ffn_act_bwd(up, dact, w_ffn_full)
    dh2 = _mm("nt", [(dup, j // 2, j % 2, n_up) for j in range(N_CHIPS)], [(c_up, j) for j in range(N_CHIPS)],
              F32, 512, 512, "d_h2")
    gw_up = _mm_tn(h2, dup, 1024, n_up, "gw_up", out_chips=True)
    dx2, dx2_b, gg_ffn = _rms_bwd(x2, dh2, g_ffn, dx3, "rms_ffn_bwd", True)
    dmix = _mm("nt", [dx2_b], [w_o_full], F32, TM_MM, 512, "d_mix")
    gw_o = _mm_tn(mix, dx2_b, 1024, 1024, "gw_o")
    early = _pair_sums([gw_o.reshape(N_CHIPS, d // N_CHIPS, d), gw_up, gw_down.reshape(N_CHIPS, D_FF // N_CHIPS, d)],
                       ["w_o", "w_up", "w_down"], "pair_exchange")
    dz_a, dox, gw_conv, gg_conv_out, gg_attn_out, (got_o, got_down) = _mixer_bwd(
        z_a, o_attn, dmix, w_conv_full, g_conv_out, g_attn_out, [early[0][1], early[2][1]])
    dk, dv, dfk, dqt = _attn_bwd_t(qx, dox, kx, kxt, vx, lse, last_blk)
    dq, dfq = _attn_dq_finish(dqt)
    d_f = jnp.transpose((dfk + dfq)[:, :, 0:2, :], (1, 3, 0, 2)).reshape(s, N_HEADS)
    df_b, gb_f = _gate_bwd(f_log, b_pad, jnp.pad(d_f, ((0, 0), (0, LANES - N_HEADS))))
    dh1, (got_up,) = _mm("nt", [dz_a, dq, dk, dv, df_b], [w_a, w_q, w_k, w_v, w_c], F32, TM_MM, 512, "d_h1",
                         scatter=[early[1][1]])
    gw_a = _mm_tn(h1, dz_a, 1024, c3, "gw_in_conv")
    gw_q = _mm_tn(h1, dq, 1024, ATTN_W, "gw_in_q")
    gw_k = _mm_tn(h1, dk, 1024, ATTN_W, "gw_in_k")
    gw_v = _mm_tn(h1, dv, 1024, ATTN_W, "gw_in_v")
    gw_c = _mm_tn(h1, df_b, 1024, LANES, "gw_in_gate")
    grad_x, gg_mix = _rms_bwd(x0, dh1, g_mix, dx2, "rms_mix_bwd", False)

    gw_in = jnp.concatenate([gw_a, gw_q, gw_k, gw_v, gw_c[:, :N_HEADS]], axis=1)
    n_in = IN_COLS // N_CHIPS
    gw_in = jnp.stack([gw_in[:, j * n_in:(j + 1) * n_in] for j in range(N_CHIPS)])
    late = _pair_sums([gw_in], ["w_in"], "pair_exchange_w_in")
    got_in = _scatter_to_chips([late[0][1]])
    g_w_in, g_w_o, g_w_up, g_w_down = _finish_grads(late + early, [got_in[0], got_o, got_up, got_down],
                                                    ["w_in", "w_o", "w_up", "w_down"])

    gw_ffn = jnp.concatenate([gwf_lin, gwf_gate], axis=1)
    small_parts = [gg_mix, gg_conv_out, gg_attn_out, gg_ffn, gg_final, gb_f[:, :N_HEADS], loss_row[:, 0:1], gw_conv,
                   gw_ffn]
    small_shapes = [a.shape for a in small_parts]
    tot = _unflatten(_all_reduce_small(_flat_rows(small_parts, d, 8)), small_shapes)
    g_g_mix, g_g_conv_out, g_g_attn_out, g_g_ffn, g_g_final, g_b_f, loss_sum, g_conv_full, g_ffn_full = tot
    loss = loss_sum[0, 0]
    g_g_final = g_g_final[0]
    g_w_conv = lax.dynamic_slice_in_dim(g_conv_full, my_chip * (CONV_CH // N_CHIPS), CONV_CH // N_CHIPS, axis=1)
    g_w_ffn = lax.dynamic_slice_in_dim(g_ffn_full, my_chip * n_up, n_up, axis=1)

    def adam_big(w, g, m, v, name):
        dl, nm, nv = _adamw(w[0], g, m[0], v[0], name)
        return dl[None], nm[None], nv[None]

    u_w_in = adam_big(w_in, g_w_in, m_w_in, v_w_in, "adam_w_in")
    u_w_o = adam_big(w_o, g_w_o, m_w_o, v_w_o, "adam_w_o")
    u_w_up = adam_big(w_up, g_w_up, m_w_up, v_w_up, "adam_w_up")
    u_w_down = adam_big(w_down, g_w_down, m_w_down, v_w_down, "adam_w_down")

    small_w = [g_mix, b_f, g_conv_out, g_attn_out, g_ffn, g_final, w_conv, w_ffn_conv]
    small_g = [g_g_mix, g_b_f, g_g_conv_out, g_g_attn_out, g_g_ffn, g_g_final, g_w_conv, g_w_ffn]
    small_m = [m_g_mix, m_b_f, m_g_conv_out, m_g_attn_out, m_g_ffn, m_g_final, m_w_conv, m_w_ffn_conv]
    small_v = [v_g_mix, v_b_f, v_g_conv_out, v_g_attn_out, v_g_ffn, v_g_final, v_w_conv, v_w_ffn_conv]
    shapes = [a.shape for a in small_w]
    pack = lambda arrs: _flat_rows(arrs, LANES, 8)
    sd, sm, sv = _adamw(pack(small_w), pack(small_g), pack(small_m), pack(small_v), "adam_small")
    sd, sm, sv = _unflatten(sd, shapes), _unflatten(sm, shapes), _unflatten(sv, shapes)
    (d_g_mix, d_b_f, d_g_conv_out, d_g_attn_out, d_g_ffn, d_g_final, d_w_conv, d_w_ffn) = sd
    (nm_g_mix, nm_b_f, nm_g_conv_out, nm_g_attn_out, nm_g_ffn, nm_g_final, nm_w_conv, nm_w_ffn) = sm
    (nv_g_mix, nv_b_f, nv_g_conv_out, nv_g_attn_out, nv_g_ffn, nv_g_final, nv_w_conv, nv_w_ffn) = sv

    grads = (g_g_mix, g_w_in[None], g_b_f, g_w_conv[None], g_g_conv_out, g_g_attn_out, g_w_o[None], g_g_ffn,
             g_w_up[None], g_w_ffn[None], g_w_down[None], g_g_final)
    deltas = (d_g_mix, u_w_in[0], d_b_f, d_w_conv, d_g_conv_out, d_g_attn_out, u_w_o[0], d_g_ffn, u_w_up[0],
              d_w_ffn, u_w_down[0], d_g_final)
    new_m = (nm_g_mix, u_w_in[1], nm_b_f, nm_w_conv, nm_g_conv_out, nm_g_attn_out, u_w_o[1], nm_g_ffn, u_w_up[1],
             nm_w_ffn, u_w_down[1], nm_g_final)
    new_v = (nv_g_mix, u_w_in[2], nv_b_f, nv_w_conv, nv_g_conv_out, nv_g_attn_out, u_w_o[2], nv_g_ffn, u_w_up[2],
             nv_w_ffn, u_w_down[2], nv_g_final)
    return (loss, grad_x[None], *grads, *deltas, *new_m, *new_v)
```

```python
import functools

import jax
import jax.numpy as jnp
from jax import lax
from jax.experimental import pallas as pl
from jax.experimental.pallas import tpu as pltpu

F32, BF16 = jnp.float32, jnp.bfloat16
MESH = pl.DeviceIdType.MESH

D_MODEL = 1024
CONV_CH = 512
ATTN_W = 512
N_HEADS = 8
HEAD_DIM = 64
N_PAIRS = N_HEADS // 2
D_FF = 2816
IN_COLS = 3 * CONV_CH + 3 * ATTN_W + N_HEADS
EPS = 1e-6
Q_SCALE = 0.125
EXP_ZERO = 104.0
N_CHIPS = 4
LANES = 128
HALO = 8

ADAM_LR, ADAM_B1, ADAM_B2, ADAM_EPS, ADAM_WD, ADAM_STEP = 0.001, 0.9, 0.999, 1e-08, 0.01, 10

TM_ROWS = 512
TM_MM = 1024
TK_TN = 512
TQ = 512
ROW_CHUNK = 32
TM_FFN = 1024
TN_FFN = 256
VMEM_LIMIT = 52 * 2**20


def _cp(sem, vmem=VMEM_LIMIT):
    return pltpu.CompilerParams(dimension_semantics=sem, vmem_limit_bytes=vmem)


def _bf(a):
    return a if a.dtype == BF16 else a.astype(BF16)


def _mm(mode, a_list, b_list, out_dtype, tm, tn, name, add=None, b_chips=False, scatter=()):
    n_p = len(a_list)
    a0 = a_list[0]
    m_dim = a0[0].shape[1] if isinstance(a0, tuple) else a0.shape[0]
    b0 = b_list[0]
    if b_chips:
        n_dim = b0.shape[0] * b0.shape[2]
        assert tn == b0.shape[2] and mode == "nn"
    else:
        b0 = b0[0][b0[1]] if isinstance(b0, tuple) else b0
        n_dim = b0.shape[1 if mode == "nn" else 0]
    tm, tn = min(tm, m_dim), min(tn, n_dim)
    assert m_dim % tm == 0 and n_dim % tn == 0
    dims = (((1,), (0,)), ((), ())) if mode == "nn" else (((1,), (1,)), ((), ()))
    in_specs, args = [], []
    for a in a_list:
        if isinstance(a, tuple):
            arr, lead, col, width = a
            in_specs.append(pl.BlockSpec((None, tm, width), lambda m, n, lead=lead, col=col: (lead, m, col)))
        else:
            arr = a
            in_specs.append(pl.BlockSpec((tm, a.shape[1]), lambda m, n: (m, 0)))
        args.append(arr)
    for b in b_list:
        if b_chips:
            arr = b
            in_specs.append(pl.BlockSpec((None, b.shape[1], tn), lambda m, n: (n, 0, 0)))
        elif isinstance(b, tuple):
            arr, lead = b
            if mode == "nn":
                in_specs.append(pl.BlockSpec((None, arr.shape[1], tn), lambda m, n, lead=lead: (lead, 0, n)))
            else:
                in_specs.append(pl.BlockSpec((None, tn, arr.shape[2]), lambda m, n, lead=lead: (lead, n, 0)))
        elif mode == "nn":
            arr = b
            in_specs.append(pl.BlockSpec((b.shape[0], tn), lambda m, n: (0, n)))
        else:
            arr = b
            in_specs.append(pl.BlockSpec((tn, b.shape[1]), lambda m, n: (n, 0)))
        args.append(arr)
    if add is not None:
        in_specs.append(pl.BlockSpec((tm, tn), lambda m, n: (m, n)))
        args.append(add)

    n_in = len(args)
    n_sc = len(scatter)
    grid = (m_dim // tm, n_dim // tn)

    def body(*refs):
        o_ref = refs[n_in + n_sc]
        if n_sc:
            finish = _hosted_scatter(refs[n_in:n_in + n_sc], refs[n_in + n_sc + 1:n_in + 2 * n_sc + 1],
                                     refs[n_in + 2 * n_sc + 1:], pl.program_id(0) * grid[1] + pl.program_id(1),
                                     grid[0] * grid[1])
        acc = None
        for i in range(n_p):
            d = lax.dot_general(_bf(refs[i][...]), _bf(refs[n_p + i][...]), dims,
                                preferred_element_type=F32)
            acc = d if acc is None else acc + d
        if add is not None:
            acc = refs[2 * n_p][...] + acc
        o_ref[...] = acc.astype(out_dtype)
        if n_sc:
            finish()

    main_spec = pl.BlockSpec((tm, tn), lambda m, n: (m, n))
    main_shape = jax.ShapeDtypeStruct((m_dim, n_dim), out_dtype)
    if not n_sc:
        return pl.pallas_call(body, name=name, grid=grid, in_specs=in_specs, out_specs=main_spec,
                              out_shape=main_shape, compiler_params=_cp(("parallel", "parallel")))(*args)
    got_shapes, sems = _scatter_operands(scatter)
    outs = pl.pallas_call(
        body, name=name, grid=grid, in_specs=in_specs + _hbm_specs(n_sc), out_specs=[main_spec] + _hbm_specs(n_sc),
        out_shape=[main_shape] + got_shapes, scratch_shapes=sems,
        compiler_params=_cp(("arbitrary", "arbitrary")))(*args, *scatter)
    return outs[0], outs[1:]


def _mm_tn(a, b, tm, tn, name, out_chips=False):
    k_dim, m_dim = a.shape
    n_dim = b.shape[-1] * (b.shape[0] if b.ndim == 3 else 1)
    tm, tn, tk = min(tm, m_dim), min(tn, b.shape[-1]), min(TK_TN, k_dim)
    assert m_dim % tm == 0 and b.shape[-1] % tn == 0 and k_dim % tk == 0
    per = b.shape[-1] // tn
    if b.ndim == 3:
        b_spec = pl.BlockSpec((None, tk, tn), lambda m, n, k: (n // per, k, n % per))
    else:
        b_spec = pl.BlockSpec((tk, tn), lambda m, n, k: (k, n))

    def body(a_ref, b_ref, o_ref):
        @pl.when(pl.program_id(2) == 0)
        def _():
            o_ref[...] = jnp.zeros_like(o_ref)
        o_ref[...] += lax.dot_general(_bf(a_ref[...]), _bf(b_ref[...]), (((0,), (0,)), ((), ())),
                                      preferred_element_type=F32)

    return pl.pallas_call(
        body, name=name, grid=(m_dim // tm, n_dim // tn, k_dim // tk),
        in_specs=[pl.BlockSpec((tk, tm), lambda m, n, k: (k, m)), b_spec],
        out_specs=(pl.BlockSpec((None, tm, tn), lambda m, n, k: (n, m, 0)) if out_chips
                   else pl.BlockSpec((tm, tn), lambda m, n, k: (m, n))),
        out_shape=jax.ShapeDtypeStruct((n_dim // tn, m_dim, tn) if out_chips else (m_dim, n_dim), F32),
        compiler_params=_cp(("parallel", "parallel", "arbitrary")))(a, b)


def _rstd(x):
    return lax.rsqrt(jnp.mean(x * x, axis=-1, keepdims=True) + EPS)


def _rms_fwd(x, g, name):
    s, d = x.shape
    tm = min(TM_ROWS, s)

    def body(x_ref, g_ref, h_ref):
        xv = x_ref[...]
        h_ref[...] = (xv * _rstd(xv) * g_ref[...]).astype(BF16)

    return pl.pallas_call(
        body, name=name, grid=(s // tm,),
        in_specs=[pl.BlockSpec((tm, d), lambda i: (i, 0)), pl.BlockSpec((1, d), lambda i: (0, 0))],
        out_specs=pl.BlockSpec((tm, d), lambda i: (i, 0)),
        out_shape=jax.ShapeDtypeStruct((s, d), BF16), compiler_params=_cp(("parallel",)))(x, g)


def _rms_bwd(x, dh, g, dres, name, with_bf16, scatter=()):
    s, d = x.shape
    tm = min(TM_ROWS, s)
    n_sc = len(scatter)
    n_out = 3 if with_bf16 else 2
    got_shapes, sems = _scatter_operands(scatter) if n_sc else ([], [])

    def body(x_ref, dh_ref, g_ref, dres_ref, *rest):
        dx_ref, gg_ref = rest[n_sc], rest[n_sc + n_out - 1]
        i = pl.program_id(0)
        if n_sc:
            finish = _hosted_scatter(rest[:n_sc], rest[n_sc + n_out:2 * n_sc + n_out], rest[2 * n_sc + n_out:], i,
                                     s // tm)

        @pl.when(i == 0)
        def _():
            gg_ref[...] = jnp.zeros_like(gg_ref)

        xv = x_ref[...]
        xn = xv * _rstd(xv)
        dhv = dh_ref[...]
        gg_ref[...] += jnp.sum(dhv * xn, axis=0, keepdims=True)
        t = dhv * g_ref[...]
        dx = dres_ref[...] + _rstd(xv) * (t - xn * jnp.mean(t * xn, axis=-1, keepdims=True))
        dx_ref[...] = dx
        if with_bf16:
            rest[n_sc + 1][...] = dx.astype(BF16)
        if n_sc:
            finish()

    row = pl.BlockSpec((tm, d), lambda i: (i, 0))
    vec = pl.BlockSpec((1, d), lambda i: (0, 0))
    out_specs = [row] + ([row] if with_bf16 else []) + [vec] + _hbm_specs(n_sc)
    out_shape = ([jax.ShapeDtypeStruct((s, d), F32)] + ([jax.ShapeDtypeStruct((s, d), BF16)] if with_bf16 else [])
                 + [jax.ShapeDtypeStruct((1, d), F32)] + got_shapes)
    outs = pl.pallas_call(
        body, name=name, grid=(s // tm,), in_specs=[row, row, vec, row] + _hbm_specs(n_sc), out_specs=out_specs,
        out_shape=out_shape, scratch_shapes=sems, compiler_params=_cp(("arbitrary",)))(x, dh, g, dres, *scatter)
    return tuple(outs[:n_out]) + ((outs[n_out:],) if n_sc else ())


def _loss_head(x3, target, g):
    s, d = x3.shape
    tm = min(TM_ROWS, s)

    def body(x_ref, t_ref, g_ref, dx_ref, dxb_ref, loss_ref, gg_ref):
        @pl.when(pl.program_id(0) == 0)
        def _():
            gg_ref[...] = jnp.zeros_like(gg_ref)
            loss_ref[...] = jnp.zeros_like(loss_ref)

        xv = x_ref[...]
        r = _rstd(xv)
        xn = xv * r
        gv = g_ref[...]
        err = xn * gv - t_ref[...]
        loss_ref[...] += 0.5 * jnp.sum(jnp.mean(err * err, axis=-1, keepdims=True), axis=0, keepdims=True)
        dy = err * (1.0 / d)
        gg_ref[...] += jnp.sum(dy * xn, axis=0, keepdims=True)
        t = dy * gv
        dx = r * (t - xn * jnp.mean(t * xn, axis=-1, keepdims=True))
        dx_ref[...] = dx
        dxb_ref[...] = dx.astype(BF16)

    row = pl.BlockSpec((tm, d), lambda i: (i, 0))
    vec = pl.BlockSpec((1, d), lambda i: (0, 0))
    return pl.pallas_call(
        body, name="loss_head", grid=(s // tm,), in_specs=[row, row, vec],
        out_specs=[row, row, pl.BlockSpec((1, LANES), lambda i: (0, 0)), vec],
        out_shape=[jax.ShapeDtypeStruct((s, d), F32), jax.ShapeDtypeStruct((s, d), BF16),
                   jax.ShapeDtypeStruct((1, LANES), F32), jax.ShapeDtypeStruct((1, d), F32)],
        compiler_params=_cp(("arbitrary",)))(x3, target, g)


def _prev_halo_spec(tm, width, col):
    return pl.BlockSpec((HALO, width), lambda i, *_: (jnp.maximum(i * (tm // HALO) - 1, 0), col))


def _next_halo_spec(tm, width, col, s):
    return pl.BlockSpec((HALO, width), lambda i, *_: (jnp.minimum((i + 1) * (tm // HALO), s // HALO - 1), col))


def _shift_down(x, k):
    return pltpu.roll(x, k, 0)


def _shift_up(x, k):
    return pltpu.roll(x, x.shape[0] - k, 0)


def _conv_taps(x_ext, w):
    return w[0:1, :] * _shift_down(x_ext, 2) + w[1:2, :] * _shift_down(x_ext, 1) + w[2:3, :] * x_ext


def _conv_taps_t(d_ext, w):
    return w[2:3, :] * d_ext + w[1:2, :] * _shift_up(d_ext, 1) + w[0:1, :] * _shift_up(d_ext, 2)


def _mixer_fwd(z_a, o_attn, w_conv, g_conv_out, g_attn_out):
    s = z_a.shape[0]
    c = CONV_CH
    tm = min(TM_ROWS, s)

    def body(gb_ref, gc_ref, xc_ref, gcp_ref, xcp_ref, o_ref, w_ref, gco_ref, gao_ref, mix_ref):
        i = pl.program_id(0)
        cx = gc_ref[...] * xc_ref[...]
        cx_prev = jnp.where(i > 0, gcp_ref[...] * xcp_ref[...], 0.0)
        conv = _conv_taps(jnp.concatenate([cx_prev, cx], axis=0), w_ref[...])[HALO:]
        y = gb_ref[...] * conv
        mix_ref[:, 0:c] = (y * _rstd(y) * gco_ref[...]).astype(BF16)
        o = o_ref[...]
        mix_ref[:, c:2 * c] = (o * _rstd(o) * gao_ref[...]).astype(BF16)

    col = lambda j: pl.BlockSpec((tm, c), lambda i: (i, j))
    vec = pl.BlockSpec((1, c), lambda i: (0, 0))
    return pl.pallas_call(
        body, name="mixer_fwd", grid=(s // tm,),
        in_specs=[col(0), col(1), col(2), _prev_halo_spec(tm, c, 1), _prev_halo_spec(tm, c, 2), col(0),
                  pl.BlockSpec((3, c), lambda i: (0, 0)), vec, vec],
        out_specs=pl.BlockSpec((tm, 2 * c), lambda i: (i, 0)),
        out_shape=jax.ShapeDtypeStruct((s, 2 * c), BF16),
        compiler_params=_cp(("parallel",)))(z_a, z_a, z_a, z_a, z_a, o_attn, w_conv, g_conv_out, g_attn_out)


def _mixer_bwd(z_a, o_attn, dmix, w_conv, g_conv_out, g_attn_out, scatter):
    s = z_a.shape[0]
    c = CONV_CH
    tm = min(TM_ROWS, s)
    n_blk = s // tm
    n_sc = len(scatter)
    got_shapes, sems = _scatter_operands(scatter)

    def body(gb_ref, gc_ref, xc_ref, gcp_ref, xcp_ref, gbn_ref, gcn_ref, xcn_ref, o_ref, dnc_ref, dncn_ref, dna_ref,
             w_ref, gco_ref, gao_ref, *rest):
        dz_ref, dox_ref, gw_ref, ggco_ref, ggao_ref = rest[n_sc:n_sc + 5]
        i = pl.program_id(0)
        finish = _hosted_scatter(rest[:n_sc], rest[n_sc + 5:2 * n_sc + 5], rest[2 * n_sc + 5:], i, n_blk)

        @pl.when(i == 0)
        def _():
            gw_ref[...] = jnp.zeros_like(gw_ref)
            ggco_ref[...] = jnp.zeros_like(ggco_ref)
            ggao_ref[...] = jnp.zeros_like(ggao_ref)

        w = w_ref[...]
        zeros = jnp.zeros((HALO, c), F32)
        gb_e = jnp.concatenate([zeros, gb_ref[...], gbn_ref[...]], axis=0)
        cx_prev = jnp.where(i > 0, gcp_ref[...] * xcp_ref[...], 0.0)
        gc_e = jnp.concatenate([zeros, gc_ref[...], gcn_ref[...]], axis=0)
        xc_e = jnp.concatenate([zeros, xc_ref[...], xcn_ref[...]], axis=0)
        cx_e = jnp.concatenate([cx_prev, gc_ref[...] * xc_ref[...], gcn_ref[...] * xcn_ref[...]], axis=0)
        dn_next = jnp.where(i < n_blk - 1, dncn_ref[...], 0.0)
        dn_e = jnp.concatenate([zeros, dnc_ref[...], dn_next], axis=0)

        conv_e = _conv_taps(cx_e, w)
        y_e = gb_e * conv_e
        r_e = _rstd(y_e)
        yn_e = y_e * r_e
        t_e = dn_e * gco_ref[...]
        dy_e = r_e * (t_e - yn_e * jnp.mean(t_e * yn_e, axis=-1, keepdims=True))
        dconv_e = dy_e * gb_e
        dcx_e = _conv_taps_t(dconv_e, w)
        blk = slice(HALO, HALO + tm)
        dz_ref[:, 0:c] = (dy_e * conv_e)[blk].astype(BF16)
        dz_ref[:, c:2 * c] = (dcx_e * xc_e)[blk].astype(BF16)
        dz_ref[:, 2 * c:3 * c] = (dcx_e * gc_e)[blk].astype(BF16)
        ggco_ref[...] += jnp.sum((dn_e * yn_e)[blk], axis=0, keepdims=True)
        dconv = dconv_e[blk]
        gw_ref[0:1, :] += jnp.sum(dconv * _shift_down(cx_e, 2)[blk], axis=0, keepdims=True)
        gw_ref[1:2, :] += jnp.sum(dconv * _shift_down(cx_e, 1)[blk], axis=0, keepdims=True)
        gw_ref[2:3, :] += jnp.sum(dconv * cx_e[blk], axis=0, keepdims=True)

        o = o_ref[...]
        ra = _rstd(o)
        on = o * ra
        dna = dna_ref[...]
        ggao_ref[...] += jnp.sum(dna * on, axis=0, keepdims=True)
        ta = dna * gao_ref[...]
        do = ra * (ta - on * jnp.mean(ta * on, axis=-1, keepdims=True))
        prod = do * o
        lane = lax.broadcasted_iota(jnp.int32, (tm, LANES), 1)
        head_a = lane < HEAD_DIM
        for p in range(N_PAIRS):
            cols = slice(p * LANES, (p + 1) * LANES)
            pb, dob = prod[:, cols], do[:, cols]
            for hh in range(2):
                sel = head_a if hh == 0 else jnp.logical_not(head_a)
                delta = jnp.sum(jnp.where(sel, pb, 0.0), axis=-1, keepdims=True)
                neg3 = _split3(-delta)
                do_h = pltpu.roll(dob, HEAD_DIM, 1) if hh else dob
                dox_ref[2 * p + hh] = _aug(do_h, lane, neg3).astype(BF16)
        finish()

    col = lambda j: pl.BlockSpec((tm, c), lambda i: (i, j))
    vec = pl.BlockSpec((1, c), lambda i: (0, 0))
    w3 = pl.BlockSpec((3, c), lambda i: (0, 0))
    outs = pl.pallas_call(
        body, name="mixer_bwd", grid=(n_blk,),
        in_specs=[col(0), col(1), col(2), _prev_halo_spec(tm, c, 1), _prev_halo_spec(tm, c, 2),
                  _next_halo_spec(tm, c, 0, s), _next_halo_spec(tm, c, 1, s), _next_halo_spec(tm, c, 2, s),
                  col(0), col(0), _next_halo_spec(tm, c, 0, s), col(1), w3, vec, vec] + _hbm_specs(n_sc),
        out_specs=[pl.BlockSpec((tm, 3 * c), lambda i: (i, 0)),
                   pl.BlockSpec((N_HEADS, tm, LANES), lambda i: (0, i, 0)), w3, vec, vec] + _hbm_specs(n_sc),
        out_shape=[jax.ShapeDtypeStruct((s, 3 * c), BF16), jax.ShapeDtypeStruct((N_HEADS, s, LANES), BF16),
                   jax.ShapeDtypeStruct((3, c), F32), jax.ShapeDtypeStruct((1, c), F32),
                   jax.ShapeDtypeStruct((1, c), F32)] + got_shapes,
        scratch_shapes=sems, compiler_params=_cp(("arbitrary",)))(
            z_a, z_a, z_a, z_a, z_a, z_a, z_a, z_a, o_attn, dmix, dmix, dmix, w_conv, g_conv_out, g_attn_out,
            *scatter)
    return tuple(outs[:5]) + (outs[5:],)


def _gate_fwd(f, b_pad):
    s = f.shape[0]
    tm = min(TQ, s)

    def body(f_ref, b_ref, fb_ref, carry):
        @pl.when(pl.program_id(0) == 0)
        def _():
            carry[...] = jnp.zeros_like(carry)

        z = f_ref[...] + b_ref[...]
        x = jnp.minimum(z, 0.0) - jnp.log1p(jnp.exp(-jnp.abs(z)))
        row = lax.broadcasted_iota(jnp.int32, (tm, LANES), 0)
        sh = 1
        while sh < tm:
            x = x + jnp.where(row >= sh, _shift_down(x, sh), 0.0)
            sh *= 2
        x = x + carry[0:1, :]
        carry[...] = jnp.broadcast_to(x[tm - 1:tm, :], carry.shape)
        head_a = lax.broadcasted_iota(jnp.int32, (tm, LANES), 1) < HEAD_DIM
        for p in range(N_PAIRS):
            fa = jnp.broadcast_to(x[:, 2 * p:2 * p + 1], (tm, LANES))
            fbv = jnp.broadcast_to(x[:, 2 * p + 1:2 * p + 2], (tm, LANES))
            fb_ref[:, p * LANES:(p + 1) * LANES] = jnp.where(head_a, fa, fbv)

    return pl.pallas_call(
        body, name="gate_fwd", grid=(s // tm,),
        in_specs=[pl.BlockSpec((tm, LANES), lambda i: (i, 0)), pl.BlockSpec((1, LANES), lambda i: (0, 0))],
        out_specs=pl.BlockSpec((tm, N_PAIRS * LANES), lambda i: (i, 0)),
        out_shape=jax.ShapeDtypeStruct((s, N_PAIRS * LANES), F32),
        scratch_shapes=[pltpu.VMEM((HALO, LANES), F32)],
        compiler_params=_cp(("arbitrary",)))(f, b_pad)


def _gate_bwd(f, b_pad, d_f):
    s = f.shape[0]
    tm = min(TQ, s)
    n_blk = s // tm

    def body(f_ref, b_ref, d_ref, df_ref, gb_ref, carry):
        @pl.when(pl.program_id(0) == 0)
        def _():
            carry[...] = jnp.zeros_like(carry)
            gb_ref[...] = jnp.zeros_like(gb_ref)

        x = d_ref[...]
        row = lax.broadcasted_iota(jnp.int32, (tm, LANES), 0)
        sh = 1
        while sh < tm:
            x = x + jnp.where(row < tm - sh, _shift_up(x, sh), 0.0)
            sh *= 2
        x = x + carry[0:1, :]
        carry[...] = jnp.broadcast_to(x[0:1, :], carry.shape)
        z = f_ref[...] + b_ref[...]
        d = x * (1.0 / (1.0 + jnp.exp(z)))
        df_ref[...] = d.astype(BF16)
        gb_ref[...] += jnp.sum(d, axis=0, keepdims=True)

    rev = pl.BlockSpec((tm, LANES), lambda i: (n_blk - 1 - i, 0))
    vec = pl.BlockSpec((1, LANES), lambda i: (0, 0))
    return pl.pallas_call(
        body, name="gate_bwd", grid=(n_blk,), in_specs=[rev, vec, rev], out_specs=[rev, vec],
        out_shape=[jax.ShapeDtypeStruct((s, LANES), BF16), jax.ShapeDtypeStruct((1, LANES), F32)],
        scratch_shapes=[pltpu.VMEM((HALO, LANES), F32)],
        compiler_params=_cp(("arbitrary",)))(f, b_pad, d_f)


_NT = (((1,), (1,)), ((), ()))
_NN = (((1,), (0,)), ((), ()))
_TN = (((0,), (0,)), ((), ()))


def _head_masks(shape):
    lane = lax.broadcasted_iota(jnp.int32, shape, len(shape) - 1)
    return lane < HEAD_DIM


def _pick_row(ft, h):
    rows = lax.broadcasted_iota(jnp.int32, ft.shape, 0)
    return jnp.sum(jnp.where(rows == h, ft, 0.0), axis=0, keepdims=True)


def _prune_bounds(qkv, fb):
    s = qkv.shape[0]
    tq = min(TQ, s)

    def body(q_ref, k_ref, fb_ref, out_ref):
        head_a = _head_masks((tq, LANES))
        lane = lax.broadcasted_iota(jnp.int32, (HALO, LANES), 1)
        acc = jnp.zeros((HALO, LANES), F32)
        for p in range(N_PAIRS):
            cols = slice(p * LANES, (p + 1) * LANES)
            q2 = q_ref[:, cols].astype(F32) * Q_SCALE
            k2 = k_ref[:, cols].astype(F32)
            f2 = fb_ref[:, cols]
            for hh in range(2):
                sel = head_a if hh == 0 else jnp.logical_not(head_a)
                qn = jnp.sqrt(jnp.sum(jnp.where(sel, q2 * q2, 0.0), axis=-1, keepdims=True))
                kn = jnp.sqrt(jnp.sum(jnp.where(sel, k2 * k2, 0.0), axis=-1, keepdims=True))
                f = f2[:, hh * HEAD_DIM:hh * HEAD_DIM + 1]
                h = 2 * p + hh
                vals = (jnp.max(qn, axis=0, keepdims=True), jnp.max(kn, axis=0, keepdims=True),
                        jnp.max(qn * kn + f, axis=0, keepdims=True), f[tq - 1:tq, :])
                for slot, v in enumerate(vals):
                    acc = jnp.where(lane == slot * N_HEADS + h, v, acc)
        out_ref[0] = acc

    blk = lambda j: pl.BlockSpec((tq, ATTN_W), lambda i: (i, j))
    return pl.pallas_call(
        body, name="prune_bounds", grid=(s // tq,), in_specs=[blk(0), blk(1), blk(0)],
        out_specs=pl.BlockSpec((1, HALO, LANES), lambda i: (i, 0, 0)),
        out_shape=jax.ShapeDtypeStruct((s // tq, HALO, LANES), F32),
        compiler_params=_cp(("parallel",)))(qkv, qkv, fb)


def _first_key_blocks(qkv, fb):
    t = _prune_bounds(qkv, fb)[:, 0, :]
    nh = N_HEADS
    a, b, c, e = t[:, 0:nh], t[:, nh:2 * nh], t[:, 2 * nh:3 * nh], t[:, 3 * nh:4 * nh]
    bound = a[:, None, :] * b[None, :, :] * 1.001 + c[:, None, :] - e[None, :, :]
    n_q = t.shape[0]
    idx = jnp.arange(n_q)
    need = jnp.logical_not(bound < -(EXP_ZERO + 2.0)) | (idx[None, :, None] >= idx[:, None, None])
    first = jnp.argmax(need, axis=1).astype(jnp.int32)
    return jnp.min(first.reshape(n_q, N_PAIRS, 2), axis=-1).T.reshape(-1)


def _attn_fwd(qkv, fb, ft, first_blk):
    s = qkv.shape[0]
    tq = min(TQ, s)
    n_q = s // tq
    neg = -1e30

    rc = min(ROW_CHUNK, tq)

    def body(first_ref, q_ref, k_ref, v_ref, fb_ref, ft_ref, o_ref, g_ref,
             s_a, s_b, p_ab, m_a, l_a, m_b, l_b, alpha2, acc, pmax_a, psum_a, pmax_b, psum_b):
        p = pl.program_id(0)
        i = pl.program_id(1)
        head_a = _head_masks((tq, LANES))
        q2 = q_ref[...] * Q_SCALE
        zero = jnp.zeros_like(q2)
        q_a, q_b = jnp.where(head_a, q2, zero), jnp.where(head_a, zero, q2)
        for m_scr, l_scr in ((m_a, l_a), (m_b, l_b)):
            m_scr[...] = jnp.full(m_scr.shape, neg, F32)
            l_scr[...] = jnp.zeros(l_scr.shape, F32)
        acc[...] = jnp.zeros(acc.shape, F32)

        def step(kb, masked):
            rows_k = pl.ds(pl.multiple_of(kb * tq, tq), tq)
            k2, v2 = k_ref[rows_k, :], v_ref[rows_k, :]
            ftv = ft_ref[kb]
            fk = (_pick_row(ftv, 2 * p), _pick_row(ftv, 2 * p + 1))
            s_a[...] = lax.dot_general(q_a, k2, _NT, preferred_element_type=F32)
            s_b[...] = lax.dot_general(q_b, k2, _NT, preferred_element_type=F32)
            alphas = []
            for hh, (s_scr, m_scr, l_scr, pmax, psum) in enumerate(((s_a, m_a, l_a, pmax_a, psum_a),
                                                                     (s_b, m_b, l_b, pmax_b, psum_b))):
                fq = fb_ref[:, hh * HEAD_DIM:hh * HEAD_DIM + 1]

                def shifted(r, ncol, add=None):
                    rows = slice(r * rc, (r + 1) * rc)
                    t = s_scr[rows, 0:ncol]
                    if add is not None:
                        t = t + add[rows, :]
                    t = t - fk[hh][:, 0:ncol]
                    if masked:
                        col_id = lax.broadcasted_iota(jnp.int32, (rc, ncol), 1)
                        row_id = lax.broadcasted_iota(jnp.int32, (rc, ncol), 0) + r * rc
                        t = jnp.where(col_id <= row_id, t, -jnp.inf)
                    return rows, t

                def lane_blocks(t, op):
                    out = t[:, 0:LANES]
                    for cb in range(1, t.shape[1] // LANES):
                        out = op(out, t[:, cb * LANES:(cb + 1) * LANES])
                    return out

                ncols = [min(tq, -(-((r + 1) * rc) // LANES) * LANES) if masked else tq for r in range(tq // rc)]
                for r, ncol in enumerate(ncols):
                    rows, t = shifted(r, ncol)
                    pmax[rows, :] = lane_blocks(t, jnp.maximum)
                m_old = m_scr[...]
                m_new = jnp.maximum(m_old, jnp.max(pmax[...], axis=-1, keepdims=True) + fq)
                alpha = jnp.exp(m_old - m_new)
                m_scr[...] = m_new
                shift = fq - m_new
                for r, ncol in enumerate(ncols):
                    rows, t = shifted(r, ncol, shift)
                    pr = jnp.exp(t)
                    psum[rows, :] = lane_blocks(pr, jnp.add)
                    p_ab[rows, hh * tq:hh * tq + ncol] = pr.astype(BF16)
                    if ncol < tq:
                        p_ab[rows, hh * tq + ncol:(hh + 1) * tq] = jnp.zeros((rc, tq - ncol), BF16)
                l_scr[...] = alpha * l_scr[...] + jnp.sum(psum[...], axis=-1, keepdims=True)
                alphas.append(alpha)
            alpha2[...] = jnp.where(head_a, alphas[0], alphas[1])
            zv = jnp.zeros_like(v2)
            head_k = _head_masks(v2.shape)
            vv = jnp.concatenate([jnp.where(head_k, v2, zv), jnp.where(head_k, zv, v2)], axis=0)
            acc[...] = acc[...] * alpha2[...] + lax.dot_general(p_ab[...], vv, _NN, preferred_element_type=F32)

        def unmasked(kb, carry):
            step(kb, False)
            return carry

        lax.fori_loop(first_ref[p * n_q + i], i, unmasked, 0)
        step(i, True)
        fq2 = fb_ref[...]
        o_ref[...] = acc[...] / jnp.where(head_a, l_a[...], l_b[...])
        g_ref[...] = fq2 - jnp.where(head_a, m_a[...] + jnp.log(l_a[...]), m_b[...] + jnp.log(l_b[...]))

    qblk = lambda off: pl.BlockSpec((tq, LANES), lambda p, i, first: (i, off + p))
    full = lambda off: pl.BlockSpec((s, LANES), lambda p, i, first: (0, off + p))
    col_scr = pltpu.VMEM((tq, 1), F32)
    grid_spec = pltpu.PrefetchScalarGridSpec(
        num_scalar_prefetch=1, grid=(N_PAIRS, n_q),
        in_specs=[qblk(0), full(N_PAIRS), full(2 * N_PAIRS), qblk(0),
                  pl.BlockSpec((n_q, N_HEADS, tq), lambda p, i, first: (0, 0, 0))],
        out_specs=[qblk(0), qblk(0)],
        scratch_shapes=[pltpu.VMEM((tq, tq), F32), pltpu.VMEM((tq, tq), F32), pltpu.VMEM((tq, 2 * tq), BF16),
                        col_scr, col_scr, col_scr, col_scr] + [pltpu.VMEM((tq, LANES), F32)] * 6)
    return pl.pallas_call(
        body, name="attn_fwd", grid_spec=grid_spec,
        out_shape=[jax.ShapeDtypeStruct((s, ATTN_W), F32), jax.ShapeDtypeStruct((s, ATTN_W), F32)],
        compiler_params=_cp(("parallel", "arbitrary")))(first_blk, qkv, qkv, qkv, fb, ft)


def _attn_bwd(qkv, do, g, delta, ft, first_blk):
    s = qkv.shape[0]
    tq = min(TQ, s)
    n_q = s // tq

    def body(first_ref, q_ref, do_ref, g_ref, dl_ref, k_ref, v_ref, ft_ref, dq_ref, dfq_ref, dk_ref, dv_ref,
             dfk_ref):
        p = pl.program_id(0)
        i = pl.program_id(1)

        @pl.when(i == 0)
        def _():
            dk_ref[...] = jnp.zeros_like(dk_ref)
            dv_ref[...] = jnp.zeros_like(dv_ref)
            dfk_ref[...] = jnp.zeros_like(dfk_ref)

        head_a = _head_masks((tq, LANES))
        q2 = q_ref[...] * Q_SCALE
        do2 = do_ref[...]
        zero = jnp.zeros_like(q2)
        q_a, q_b = jnp.where(head_a, q2, zero), jnp.where(head_a, zero, q2)
        do_a, do_b = jnp.where(head_a, do2, zero), jnp.where(head_a, zero, do2)
        g2, dl2 = g_ref[...], dl_ref[...]
        g_a, g_b = g2[:, 0:1], g2[:, HEAD_DIM:HEAD_DIM + 1]
        dl_a, dl_b = dl2[:, 0:1], dl2[:, HEAD_DIM:HEAD_DIM + 1]
        causal = lax.broadcasted_iota(jnp.int32, (tq, tq), 1) <= lax.broadcasted_iota(jnp.int32, (tq, tq), 0)
        rows8 = lax.broadcasted_iota(jnp.int32, (N_HEADS, tq), 0)

        def one_head(q_h, do_h, g_h, dl_h, fk_h, k2, v2, masked):
            sc = (lax.dot_general(q_h, k2, _NT, preferred_element_type=F32) + g_h) - fk_h
            pr = jnp.exp(sc)
            if masked:
                pr = jnp.where(causal, pr, 0.0)
            dp = lax.dot_general(do_h, v2, _NT, preferred_element_type=F32)
            ds = pr * (dp - dl_h)
            return (pr.astype(BF16), ds.astype(BF16), jnp.sum(ds, axis=0, keepdims=True),
                    jnp.sum(ds, axis=1, keepdims=True))

        def step(j, carry, masked):
            dq, r_a, r_b = carry
            rows = pl.ds(pl.multiple_of(j * tq, tq), tq)
            k2, v2 = k_ref[rows, :], v_ref[rows, :]
            k_a, k_b = jnp.where(head_a, k2, zero), jnp.where(head_a, zero, k2)
            ftv = ft_ref[j]
            p_a, ds_a, c_a, s_a = one_head(q_a, do_a, g_a, dl_a, _pick_row(ftv, 2 * p), k2, v2, masked)
            p_b, ds_b, c_b, s_b = one_head(q_b, do_b, g_b, dl_b, _pick_row(ftv, 2 * p + 1), k2, v2, masked)
            dv_ref[rows, :] += (lax.dot_general(p_a, do_a, _TN, preferred_element_type=F32)
                                + lax.dot_general(p_b, do_b, _TN, preferred_element_type=F32))
            dk_ref[rows, :] += (lax.dot_general(ds_a, q_a, _TN, preferred_element_type=F32)
                                + lax.dot_general(ds_b, q_b, _TN, preferred_element_type=F32))
            dfk_ref[0, j] += jnp.where(rows8 == 0, -c_a, jnp.where(rows8 == 1, -c_b, 0.0))
            dq = dq + (lax.dot_general(ds_a, k_a, _NN, preferred_element_type=F32)
                       + lax.dot_general(ds_b, k_b, _NN, preferred_element_type=F32))
            return dq, r_a + s_a, r_b + s_b

        zcol = jnp.zeros((tq, 1), F32)
        carry = lax.fori_loop(first_ref[p * n_q + i], i, lambda j, cr: step(j, cr, False),
                              (jnp.zeros((tq, LANES), F32), zcol, zcol))
        dq, r_a, r_b = step(i, carry, True)
        dq_ref[...] = (dq * Q_SCALE).astype(BF16)
        dfq_ref[...] = jnp.where(head_a, r_a, r_b)

    qblk = lambda off: pl.BlockSpec((tq, LANES), lambda p, i, first: (i, off + p))
    full = lambda off: pl.BlockSpec((s, LANES), lambda p, i, first: (0, off + p))
    grid_spec = pltpu.PrefetchScalarGridSpec(
        num_scalar_prefetch=1, grid=(N_PAIRS, n_q),
        in_specs=[qblk(0), qblk(0), qblk(0), qblk(0), full(N_PAIRS), full(2 * N_PAIRS),
                  pl.BlockSpec((n_q, N_HEADS, tq), lambda p, i, first: (0, 0, 0))],
        out_specs=[qblk(0), qblk(0), full(0), full(0),
                   pl.BlockSpec((1, n_q, N_HEADS, tq), lambda p, i, first: (p, 0, 0, 0))])
    return pl.pallas_call(
        body, name="attn_bwd", grid_spec=grid_spec,
        out_shape=[jax.ShapeDtypeStruct((s, ATTN_W), BF16), jax.ShapeDtypeStruct((s, ATTN_W), F32),
                   jax.ShapeDtypeStruct((s, ATTN_W), F32), jax.ShapeDtypeStruct((s, ATTN_W), F32),
                   jax.ShapeDtypeStruct((N_PAIRS, n_q, N_HEADS, tq), F32)],
        compiler_params=_cp(("parallel", "arbitrary")))(first_blk, qkv, do, g, delta, qkv, qkv, ft)


AUG = HEAD_DIM
NORM_MARGIN = 1.01


def _split3(x):
    hi = x.astype(BF16).astype(F32)
    r = x - hi
    mid = r.astype(BF16).astype(F32)
    lo = (r - mid).astype(BF16).astype(F32)
    return hi, mid, lo


def _aug(base, lane, vals):
    out = jnp.where(lane < AUG, base, 0.0)
    for k, v in enumerate(vals):
        out = jnp.where(lane == AUG + k, v, out)
    return out


def _attn_prep(qkv, fb, bigs):
    s = qkv.shape[0]
    tq = min(TQ, s)
    n_q = s // tq

    n = len(bigs)
    arrays, landing, sems = _gather_operands(bigs, [])

    def body(q_ref, k_ref, v_ref, fb_ref, *rest):
        qx_ref, kx_ref, kxt_ref, vx_ref, vt_ref, b_ref = rest[2 * n:2 * n + 6]
        finish = _hosted_gather((rest[:n], rest[2 * n + 6:3 * n + 6]) + tuple(rest[3 * n + 6:]), n, n,
                                pl.program_id(0), n_q)
        lane = lax.broadcasted_iota(jnp.int32, (tq, LANES), 1)
        lane8 = lax.broadcasted_iota(jnp.int32, (HALO, LANES), 1)
        head_lanes = lane < AUG
        is_lane = [lane == AUG + k for k in range(6)]
        first3 = (lane >= AUG) & (lane < AUG + 3)
        next3 = (lane >= AUG + 3) & (lane < AUG + 6)
        q_const = jnp.where(first3, -1.0, 0.0)
        k_const = jnp.where(next3, 1.0, 0.0)
        v_const = jnp.where(first3, 1.0, 0.0)
        ones_head = (lax.broadcasted_iota(jnp.int32, (LANES, LANES), 0) < HEAD_DIM).astype(BF16)
        acc = jnp.zeros((HALO, LANES), F32)
        for p in range(N_PAIRS):
            cols = slice(p * LANES, (p + 1) * LANES)
            q2, k2, v2 = (ref[:, cols].astype(F32) for ref in (q_ref, k_ref, v_ref))
            q2 = q2 * Q_SCALE
            f2 = fb_ref[:, cols]
            for hh in range(2):
                h = 2 * p + hh
                q, k, v = ((pltpu.roll(x, HEAD_DIM, 1) if hh else x) for x in (q2, k2, v2))
                f = f2 if hh else pltpu.roll(f2, HEAD_DIM, 1)
                hi, mid, lo = _split3(f)
                q_aug = jnp.where(is_lane[3], hi, jnp.where(is_lane[4], mid, jnp.where(is_lane[5], lo, q_const)))
                k_aug = jnp.where(is_lane[0], hi, jnp.where(is_lane[1], mid, jnp.where(is_lane[2], lo, k_const)))
                kx = jnp.where(head_lanes, k, k_aug)
                vx = jnp.where(head_lanes, v, v_const)
                qx_ref[h] = jnp.where(head_lanes, q, q_aug).astype(BF16)
                kx_ref[h] = kx.astype(BF16)
                vx_ref[h] = vx.astype(BF16)
                kxt_ref[h, 0] = kx.T.astype(BF16)
                vt_ref[h, 0] = vx.T.astype(BF16)
                q_sq = lax.dot_general((q * q).astype(BF16), ones_head, _NN, preferred_element_type=F32)
                k_sq = lax.dot_general((k * k).astype(BF16), ones_head, _NN, preferred_element_type=F32)
                qk = jnp.sqrt(q_sq * k_sq) * NORM_MARGIN + f
                vals = (jnp.sqrt(jnp.max(q_sq, axis=0, keepdims=True)), jnp.sqrt(jnp.max(k_sq, axis=0, keepdims=True)),
                        jnp.max(qk, axis=0, keepdims=True), f[tq - 1:tq, :])
                for slot, val in enumerate(vals):
                    acc = jnp.where(lane8 == slot * N_HEADS + h, val[:, AUG:AUG + 1], acc)
        b_ref[0] = acc
        finish()

    blk = lambda j: pl.BlockSpec((tq, ATTN_W), lambda i: (i, j))
    rows = pl.BlockSpec((N_HEADS, tq, LANES), lambda i: (0, i, 0))
    cols_t = pl.BlockSpec((N_HEADS, 1, LANES, tq), lambda i: (0, i, 0, 0))
    shp = jax.ShapeDtypeStruct((N_HEADS, s, LANES), BF16)
    shp_t = jax.ShapeDtypeStruct((N_HEADS, n_q, LANES, tq), BF16)
    outs = pl.pallas_call(
        body, name="attn_prep", grid=(n_q,), in_specs=[blk(0), blk(1), blk(2), blk(0)] + _hbm_specs(2 * n),
        out_specs=[rows, rows, cols_t, rows, cols_t,
                   pl.BlockSpec((1, HALO, LANES), lambda i: (i, 0, 0))] + _hbm_specs(n),
        out_shape=[shp, shp, shp_t, shp, shp_t, jax.ShapeDtypeStruct((n_q, HALO, LANES), F32)]
        + [jax.ShapeDtypeStruct(b.shape, b.dtype) for b in landing],
        input_output_aliases={4 + n + k: 6 + k for k in range(n)}, scratch_shapes=sems,
        compiler_params=_cp(("arbitrary",)))(qkv, qkv, qkv, fb, *arrays, *landing)
    return tuple(outs[:6]) + (outs[6:],)


def _key_block_ranges(bounds):
    t = bounds[:, 0, :]
    nh = N_HEADS
    a, b, c, e = t[:, 0:nh], t[:, nh:2 * nh], t[:, 2 * nh:3 * nh], t[:, 3 * nh:4 * nh]
    bound = a[:, None, :] * b[None, :, :] * NORM_MARGIN + c[:, None, :] - e[None, :, :]
    n_q = t.shape[0]
    idx = jnp.arange(n_q)
    need = jnp.logical_not(bound < -(EXP_ZERO + 2.0)) | (idx[None, :, None] >= idx[:, None, None])
    first = jnp.argmax(need, axis=1).astype(jnp.int32)
    first = jnp.min(first.reshape(n_q, N_PAIRS, 2), axis=-1)
    visits = (first[:, None, :] <= idx[None, :, None]) & (idx[None, :, None] <= idx[:, None, None])
    last = jnp.max(jnp.where(visits, idx[:, None, None], 0), axis=0).astype(jnp.int32)
    return first.T.reshape(-1), last.T.reshape(-1)


def _attn_fwd_t(qx, kx, vt, first_blk):
    _, s, _ = qx.shape
    tq = min(TQ, s)
    n_q = s // tq
    neg = -1e30

    def body(first_ref, qx_ref, kx_ref, vt_ref, o_ref, lse_ref, acc_ref, m_ref, s_even, s_odd):
        p = pl.program_id(0)
        i = pl.program_id(1)
        acc_ref[...] = jnp.zeros(acc_ref.shape, F32)
        m_ref[...] = jnp.full(m_ref.shape, neg, F32)
        key_le_query = (lax.broadcasted_iota(jnp.int32, (tq, tq), 0) <= lax.broadcasted_iota(jnp.int32, (tq, tq), 1))
        first = first_ref[p * n_q + i]

        def scores(kb, hh, dst):
            rows_k = pl.ds(pl.multiple_of(kb * tq, tq), tq)
            dst[hh] = lax.dot_general(kx_ref[hh, rows_k, :], qx_ref[hh], _NT, preferred_element_type=F32)

        def step(kb, src, nxt):
            for hh in range(2):
                st = src[hh]
                if nxt is None:
                    st = jnp.where(key_le_query, st, -jnp.inf)
                else:
                    scores(kb + 1, hh, nxt)
                m_old = m_ref[hh]
                m_new = jnp.maximum(m_old, jnp.max(st, axis=0, keepdims=True))
                m_ref[hh] = m_new
                pt = jnp.exp(st - m_new).astype(BF16)
                acc_ref[hh] = acc_ref[hh] * jnp.exp(m_old - m_new) + lax.dot_general(
                    vt_ref[hh, kb], pt, _NN, preferred_element_type=F32)

        def by_parity(kb, fn):
            @pl.when(kb % 2 == 0)
            def _():
                fn(s_even, s_odd)

            @pl.when(kb % 2 == 1)
            def _():
                fn(s_odd, s_even)

        def first_scores(src, nxt):
            scores(first, 0, src)
            scores(first, 1, src)

        def unmasked(kb, carry):
            by_parity(kb, lambda src, nxt: step(kb, src, nxt))
            return carry

        by_parity(first, first_scores)
        lax.fori_loop(first, i, unmasked, 0)
        by_parity(i, lambda src, nxt: step(i, src, None))
        outs, lses = [], []
        for hh in range(2):
            acc = acc_ref[hh]
            l = acc[AUG:AUG + 1, :]
            outs.append(acc[0:HEAD_DIM, :] / l)
            lses.append(m_ref[hh] + jnp.log(l))
        o_ref[...] = jnp.concatenate(outs, axis=0).T
        rows8 = lax.broadcasted_iota(jnp.int32, (N_HEADS, tq), 0)
        lse_ref[0, 0] = jnp.where(rows8 == 0, lses[0], jnp.where(rows8 == 1, lses[1], 0.0))

    grid_spec = pltpu.PrefetchScalarGridSpec(
        num_scalar_prefetch=1, grid=(N_PAIRS, n_q),
        in_specs=[pl.BlockSpec((2, tq, LANES), lambda p, i, first: (p, i, 0)),
                  pl.BlockSpec((2, s, LANES), lambda p, i, first: (p, 0, 0)),
                  pl.BlockSpec((2, n_q, LANES, tq), lambda p, i, first: (p, 0, 0, 0))],
        out_specs=[pl.BlockSpec((tq, LANES), lambda p, i, first: (i, p)),
                   pl.BlockSpec((1, 1, N_HEADS, tq), lambda p, i, first: (p, i, 0, 0))],
        scratch_shapes=[pltpu.VMEM((2, LANES, tq), F32), pltpu.VMEM((2, 1, tq), F32),
                        pltpu.VMEM((2, tq, tq), F32), pltpu.VMEM((2, tq, tq), F32)])
    return pl.pallas_call(
        body, name="attn_fwd", grid_spec=grid_spec,
        out_shape=[jax.ShapeDtypeStruct((s, ATTN_W), F32), jax.ShapeDtypeStruct((N_PAIRS, n_q, N_HEADS, tq), F32)],
        compiler_params=_cp(("parallel", "arbitrary")))(first_blk, qx, kx, vt)


def _attn_bwd_t(qx, dox, kx, kxt, vx, lse, last_blk):
    _, s, _ = qx.shape
    tq = min(TQ, s)
    n_q = s // tq

    def body(last_ref, qx_ref, dox_ref, lse_ref, kx_ref, kxt_ref, vx_ref, dk_ref, dv_ref, dfk_ref, dqt_ref):
        p = pl.program_id(0)
        j = pl.program_id(1)

        @pl.when(j == 0)
        def _():
            dqt_ref[...] = jnp.zeros(dqt_ref.shape, F32)

        key_le_query = (lax.broadcasted_iota(jnp.int32, (tq, tq), 0) <= lax.broadcasted_iota(jnp.int32, (tq, tq), 1))

        def step(i, carry, masked):
            rows_q = pl.ds(pl.multiple_of(i * tq, tq), tq)
            out = []
            for hh in range(2):
                dk, dv = carry[2 * hh], carry[2 * hh + 1]
                q, do = qx_ref[hh, rows_q, :], dox_ref[hh, rows_q, :]
                st = lax.dot_general(kx_ref[hh], q, _NT, preferred_element_type=F32)
                pt = jnp.exp(st - lse_ref[0, i, hh:hh + 1, :])
                if masked:
                    pt = jnp.where(key_le_query, pt, 0.0)
                dst = pt * lax.dot_general(vx_ref[hh], do, _NT, preferred_element_type=F32)
                pb, dsb = pt.astype(BF16), dst.astype(BF16)
                dv = dv + lax.dot_general(pb, do, _NN, preferred_element_type=F32)
                dk = dk + lax.dot_general(dsb, q, _NN, preferred_element_type=F32)
                dqt_ref[hh, i] += lax.dot_general(kxt_ref[hh, 0], dsb, _NN, preferred_element_type=F32)
                out += [dk, dv]
            return tuple(out)

        zero = jnp.zeros((tq, LANES), F32)
        carry = step(j, (zero, zero, zero, zero), True)
        dk_a, dv_a, dk_b, dv_b = lax.fori_loop(j + 1, last_ref[p * n_q + j] + 1,
                                               lambda i, cr: step(i, cr, False), carry)
        head_a = lax.broadcasted_iota(jnp.int32, (tq, LANES), 1) < HEAD_DIM
        dk_ref[...] = jnp.where(head_a, dk_a, pltpu.roll(dk_b, HEAD_DIM, 1)).astype(BF16)
        dv_ref[...] = jnp.where(head_a, dv_a, pltpu.roll(dv_b, HEAD_DIM, 1)).astype(BF16)
        aux_t = jnp.where(head_a, pltpu.roll(dk_a, HEAD_DIM, 1), dk_b).T
        rows8 = lax.broadcasted_iota(jnp.int32, (N_HEADS, tq), 0)
        dfk_ref[0, 0] = jnp.where(rows8 == 0, aux_t[0:1], jnp.where(rows8 == 1, aux_t[HEAD_DIM:HEAD_DIM + 1], 0.0))

    resident = pl.BlockSpec((2, s, LANES), lambda p, j, last: (p, 0, 0))
    key_rows = pl.BlockSpec((2, tq, LANES), lambda p, j, last: (p, j, 0))
    pair_out = pl.BlockSpec((tq, LANES), lambda p, j, last: (j, p))
    grid_spec = pltpu.PrefetchScalarGridSpec(
        num_scalar_prefetch=1, grid=(N_PAIRS, n_q),
        in_specs=[resident, resident, pl.BlockSpec((1, n_q, N_HEADS, tq), lambda p, j, last: (p, 0, 0, 0)),
                  key_rows, pl.BlockSpec((2, 1, LANES, tq), lambda p, j, last: (p, j, 0, 0)), key_rows],
        out_specs=[pair_out, pair_out, pl.BlockSpec((1, 1, N_HEADS, tq), lambda p, j, last: (p, j, 0, 0)),
                   pl.BlockSpec((2, n_q, LANES, tq), lambda p, j, last: (p, 0, 0, 0))])
    return pl.pallas_call(
        body, name="attn_bwd", grid_spec=grid_spec,
        out_shape=[jax.ShapeDtypeStruct((s, ATTN_W), BF16), jax.ShapeDtypeStruct((s, ATTN_W), BF16),
                   jax.ShapeDtypeStruct((N_PAIRS, n_q, N_HEADS, tq), F32),
                   jax.ShapeDtypeStruct((N_HEADS, n_q, LANES, tq), F32)],
        compiler_params=_cp(("parallel", "arbitrary")))(last_blk, qx, dox, lse, kx, kxt, vx)


def _attn_dq_finish(dqt):
    _, n_q, _, tq = dqt.shape
    per = 4 if n_q % 4 == 0 else 1

    def body(dqt_ref, dq_ref, dfq_ref):
        rows8 = lax.broadcasted_iota(jnp.int32, (N_HEADS, tq), 0)
        for b in range(per):
            a, bb = dqt_ref[0, b], dqt_ref[1, b]
            dq_ref[b * tq:(b + 1) * tq, :] = (
                jnp.concatenate([a[0:HEAD_DIM], bb[0:HEAD_DIM]], axis=0).T * Q_SCALE).astype(BF16)
            dfq_ref[0, b] = jnp.where(rows8 == 0, a[AUG + 3:AUG + 4],
                                      jnp.where(rows8 == 1, bb[AUG + 3:AUG + 4], 0.0))

    return pl.pallas_call(
        body, name="attn_dq_finish", grid=(N_PAIRS, n_q // per),
        in_specs=[pl.BlockSpec((2, per, LANES, tq), lambda p, i: (p, i, 0, 0))],
        out_specs=[pl.BlockSpec((per * tq, LANES), lambda p, i: (i, p)),
                   pl.BlockSpec((1, per, N_HEADS, tq), lambda p, i: (p, i, 0, 0))],
        out_shape=[jax.ShapeDtypeStruct((n_q * tq, ATTN_W), BF16),
                   jax.ShapeDtypeStruct((N_PAIRS, n_q, N_HEADS, tq), F32)],
        compiler_params=_cp(("parallel", "parallel")))(dqt)


def _ffn_act_fwd(up, w_ffn):
    s = up.shape[0]
    tm, tn = min(TM_FFN, s), TN_FFN
    nb = D_FF // tn

    def body(a_ref, g_ref, ap_ref, gp_ref, wa_ref, wg_ref, act_ref):
        i = pl.program_id(1)

        def conv(blk_ref, prev_ref, w_ref):
            prev = jnp.where(i > 0, prev_ref[...], 0.0)
            return _conv_taps(jnp.concatenate([prev, blk_ref[...]], axis=0), w_ref[...])[HALO:]

        u_a, u_g = conv(a_ref, ap_ref, wa_ref), conv(g_ref, gp_ref, wg_ref)
        act_ref[...] = (u_g * (1.0 / (1.0 + jnp.exp(-u_g))) * u_a).astype(BF16)

    blk = lambda off: pl.BlockSpec((tm, tn), lambda n, i: (i, off + n))
    prev = lambda off: pl.BlockSpec((HALO, tn), lambda n, i: (jnp.maximum(i * (tm // HALO) - 1, 0), off + n))
    wsp = lambda off: pl.BlockSpec((3, tn), lambda n, i: (0, off + n))
    return pl.pallas_call(
        body, name="ffn_act_fwd", grid=(nb, s // tm),
        in_specs=[blk(0), blk(nb), prev(0), prev(nb), wsp(0), wsp(nb)],
        out_specs=pl.BlockSpec((tm, tn), lambda n, i: (i, n)),
        out_shape=jax.ShapeDtypeStruct((s, D_FF), BF16),
        compiler_params=_cp(("parallel", "parallel")))(up, up, up, up, w_ffn, w_ffn)


def _ffn_down_loss(up, w_ffn, w_down, x2, target, g_final):
    s, d = x2.shape
    tm, tn = min(TM_ROWS, s), TN_FFN
    nb = D_FF // tn

    def body(a_ref, g_ref, ap_ref, gp_ref, wa_ref, wg_ref, wd_ref, x2_ref, t_ref, gf_ref,
             act_ref, dx_ref, dxb_ref, loss_ref, gg_ref, acc, act_even, act_odd):
        i = pl.program_id(0)
        k = pl.program_id(1)

        @pl.when((i == 0) & (k == 0))
        def _():
            gg_ref[...] = jnp.zeros_like(gg_ref)
            loss_ref[...] = jnp.zeros_like(loss_ref)

        def conv(blk_ref, prev_ref, w_ref):
            prev = jnp.where(i > 0, prev_ref[...], 0.0)
            return _conv_taps(jnp.concatenate([prev, blk_ref[...]], axis=0), w_ref[...])[HALO:]

        def activation(dst):
            u_a, u_g = conv(a_ref, ap_ref, wa_ref), conv(g_ref, gp_ref, wg_ref)
            act = (u_g * (1.0 / (1.0 + jnp.exp(-u_g))) * u_a).astype(BF16)
            act_ref[...] = act
            dst[...] = act

        def project(src, first):
            part = lax.dot_general(src[...], wd_ref[...], _NN, preferred_element_type=F32)
            acc[...] = part if first else acc[...] + part

        @pl.when(k == 0)
        def _():
            activation(act_even)

        @pl.when(k == 1)
        def _():
            project(act_even, True)
            activation(act_odd)

        @pl.when((k > 1) & (k < nb) & (k % 2 == 0))
        def _():
            project(act_odd, False)
            activation(act_even)

        @pl.when((k > 1) & (k < nb) & (k % 2 == 1))
        def _():
            project(act_even, False)
            activation(act_odd)

        @pl.when(k == nb)
        def _():
            project(act_even if nb % 2 == 1 else act_odd, False)
            xv = x2_ref[...] + acc[...]
            r = _rstd(xv)
            xn = xv * r
            gv = gf_ref[...]
            err = xn * gv - t_ref[...]
            loss_ref[...] += 0.5 * jnp.sum(jnp.mean(err * err, axis=-1, keepdims=True), axis=0, keepdims=True)
            dy = err * (1.0 / d)
            gg_ref[...] += jnp.sum(dy * xn, axis=0, keepdims=True)
            t = dy * gv
            dx = r * (t - xn * jnp.mean(t * xn, axis=-1, keepdims=True))
            dx_ref[...] = dx
            dxb_ref[...] = dx.astype(BF16)

    chunk = lambda k: jnp.minimum(k, nb - 1)
    blk = lambda off: pl.BlockSpec((tm, tn), lambda i, k: (i, off + chunk(k)))
    prev = lambda off: pl.BlockSpec((HALO, tn),
                                    lambda i, k: (jnp.maximum(i * (tm // HALO) - 1, 0), off + chunk(k)))
    wsp = lambda off: pl.BlockSpec((3, tn), lambda i, k: (0, off + chunk(k)))
    row = pl.BlockSpec((tm, d), lambda i, k: (i, 0))
    vec = pl.BlockSpec((1, d), lambda i, k: (0, 0))
    return pl.pallas_call(
        body, name="ffn_down_loss", grid=(s // tm, nb + 1),
        in_specs=[blk(0), blk(nb), prev(0), prev(nb), wsp(0), wsp(nb),
                  pl.BlockSpec((tn, d), lambda i, k: (jnp.maximum(k - 1, 0), 0)), row, row, vec],
        out_specs=[pl.BlockSpec((tm, tn), lambda i, k: (i, chunk(k))), row, row,
                   pl.BlockSpec((1, LANES), lambda i, k: (0, 0)), vec],
        out_shape=[jax.ShapeDtypeStruct((s, D_FF), BF16), jax.ShapeDtypeStruct((s, d), F32),
                   jax.ShapeDtypeStruct((s, d), BF16), jax.ShapeDtypeStruct((1, LANES), F32),
                   jax.ShapeDtypeStruct((1, d), F32)],
        scratch_shapes=[pltpu.VMEM((tm, d), F32), pltpu.VMEM((tm, tn), BF16), pltpu.VMEM((tm, tn), BF16)],
        compiler_params=_cp(("arbitrary", "arbitrary")))(up, up, up, up, w_ffn, w_ffn, w_down, x2, target, g_final)


def _ffn_act_bwd(up, dact, w_ffn):
    s = up.shape[0]
    tm, tn = min(TM_FFN, s), TN_FFN
    nb = D_FF // tn
    n_blk = s // tm

    def body(a_ref, g_ref, ap_ref, gp_ref, an_ref, gn_ref, d_ref, dn_ref, wa_ref, wg_ref,
             dup_ref, gwa_ref, gwg_ref):
        i = pl.program_id(1)

        @pl.when(i == 0)
        def _():
            gwa_ref[...] = jnp.zeros_like(gwa_ref)
            gwg_ref[...] = jnp.zeros_like(gwg_ref)

        def ext(blk_ref, prev_ref, next_ref):
            return jnp.concatenate([jnp.where(i > 0, prev_ref[...], 0.0), blk_ref[...], next_ref[...]], axis=0)

        wa, wg = wa_ref[...], wg_ref[...]
        up_a, up_g = ext(a_ref, ap_ref, an_ref), ext(g_ref, gp_ref, gn_ref)
        u_a, u_g = _conv_taps(up_a, wa), _conv_taps(up_g, wg)
        d_e = jnp.concatenate([jnp.zeros((HALO, tn), F32), d_ref[...],
                               jnp.where(i < n_blk - 1, dn_ref[...], 0.0)], axis=0)
        sig = 1.0 / (1.0 + jnp.exp(-u_g))
        du_a = d_e * (u_g * sig)
        du_g = d_e * u_a * (sig * (1.0 + u_g * (1.0 - sig)))
        blk = slice(HALO, HALO + tm)
        dup_ref[0] = _conv_taps_t(du_a, wa)[blk].astype(BF16)
        dup_ref[1] = _conv_taps_t(du_g, wg)[blk].astype(BF16)
        for gw_ref, upv, du in ((gwa_ref, up_a[blk], du_a), (gwg_ref, up_g[blk], du_g)):
            gw_ref[0:1, :] += jnp.sum(upv * _shift_up(du, 2)[blk], axis=0, keepdims=True)
            gw_ref[1:2, :] += jnp.sum(upv * _shift_up(du, 1)[blk], axis=0, keepdims=True)
            gw_ref[2:3, :] += jnp.sum(upv * du[blk], axis=0, keepdims=True)

    blk = lambda off: pl.BlockSpec((tm, tn), lambda n, i: (i, off + n))
    prev = lambda off: pl.BlockSpec((HALO, tn), lambda n, i: (jnp.maximum(i * (tm // HALO) - 1, 0), off + n))
    nxt = lambda off: pl.BlockSpec(
        (HALO, tn), lambda n, i: (jnp.minimum((i + 1) * (tm // HALO), s // HALO - 1), off + n))
    wsp = lambda off: pl.BlockSpec((3, tn), lambda n, i: (0, off + n))
    return pl.pallas_call(
        body, name="ffn_act_bwd", grid=(nb, n_blk),
        in_specs=[blk(0), blk(nb), prev(0), prev(nb), nxt(0), nxt(nb), blk(0), nxt(0), wsp(0), wsp(nb)],
        out_specs=[pl.BlockSpec((2, tm, tn), lambda n, i: (0, i, n)), wsp(0), wsp(0)],
        out_shape=[jax.ShapeDtypeStruct((2, s, D_FF), BF16),
                   jax.ShapeDtypeStruct((3, D_FF), F32), jax.ShapeDtypeStruct((3, D_FF), F32)],
        compiler_params=_cp(("parallel", "arbitrary")))(up, up, up, up, up, up, dact, dact, w_ffn, w_ffn)


def _adamw(w, g, m, v, name):
    r, c = w.shape
    tr = next((t for t in (512, 352, 256, 128, 64, 32, 16, 8) if r > t and r % t == 0), r)

    def body(w_ref, g_ref, m_ref, v_ref, d_ref, nm_ref, nv_ref):
        gv = g_ref[...]
        m_new = ADAM_B1 * m_ref[...] + (1.0 - ADAM_B1) * gv
        v_new = ADAM_B2 * v_ref[...] + (1.0 - ADAM_B2) * (gv * gv)
        m_hat = m_new / (1.0 - ADAM_B1 ** ADAM_STEP)
        v_hat = v_new / (1.0 - ADAM_B2 ** ADAM_STEP)
        d_ref[...] = -ADAM_LR * (m_hat / (jnp.sqrt(v_hat) + ADAM_EPS) + ADAM_WD * w_ref[...])
        nm_ref[...] = m_new
        nv_ref[...] = v_new

    spec = pl.BlockSpec((tr, c), lambda i: (i, 0))
    shp = jax.ShapeDtypeStruct((r, c), F32)
    return pl.pallas_call(
        body, name=name, grid=(r // tr,), in_specs=[spec] * 4, out_specs=[spec] * 3, out_shape=[shp] * 3,
        compiler_params=_cp(("parallel",)))(w, g, m, v)


def _sum_rows_block(h):
    return h if h <= 352 else 256


def _pair_sum(view, recv, sel, name):
    n, _, h, c = view.shape
    tr = _sum_rows_block(h)

    def body(sel_ref, a_ref, b_ref, o_ref, ob_ref):
        t = a_ref[...] + b_ref[...]
        o_ref[...] = t
        ob_ref[...] = t.astype(BF16)

    blk = pl.BlockSpec((None, tr, c), lambda j, i, sel_ref: (j, i, 0))
    grid_spec = pltpu.PrefetchScalarGridSpec(
        num_scalar_prefetch=1, grid=(n, h // tr),
        in_specs=[pl.BlockSpec((None, None, tr, c), lambda j, i, sel_ref: (j, sel_ref[0], i, 0)),
                  pl.BlockSpec((None, None, tr, c), lambda j, i, sel_ref: (j, 0, i, 0))],
        out_specs=[blk, blk])
    return pl.pallas_call(
        body, name=name, grid_spec=grid_spec,
        out_shape=[jax.ShapeDtypeStruct((n, h, c), F32), jax.ShapeDtypeStruct((n, h, c), BF16)],
        compiler_params=_cp(("parallel", "parallel")))(sel, view, recv)


def _chip_sum(pair, got, sel, name):
    _, h, c = pair.shape
    tr = _sum_rows_block(h)
    nblk = h // tr

    def body(sel_ref, p_ref, g0_ref, g1_ref, g2_ref, o_ref):
        o_ref[...] = ((p_ref[...] + g0_ref[...].astype(F32)) + g1_ref[...].astype(F32)) + g2_ref[...].astype(F32)

    slot = lambda k: pl.BlockSpec((None, tr, c), lambda i, sel_ref: (k, i, 0))
    grid_spec = pltpu.PrefetchScalarGridSpec(
        num_scalar_prefetch=1, grid=(h // tr,),
        in_specs=[pl.BlockSpec((None, tr, c), lambda i, sel_ref: (sel_ref[1], i, 0)), slot(0), slot(1), slot(2)],
        out_specs=pl.BlockSpec((tr, c), lambda i, sel_ref: (sel_ref[0] * nblk + i, 0)))
    return pl.pallas_call(
        body, name=name, grid_spec=grid_spec, out_shape=jax.ShapeDtypeStruct((2 * h, c), F32),
        compiler_params=_cp(("parallel",)))(sel, pair, got, got, got)


def _place():
    return lax.axis_index("x"), lax.axis_index("y"), lax.axis_index("c")


def _other_chips(x, y):
    return [(1 - x, y), (x, 1 - y), (1 - x, 1 - y)]


def _hbm_specs(n):
    return [pl.BlockSpec(memory_space=pl.ANY)] * n


def _all_gather_weights(bigs, smalls):
    nb, ns = len(bigs), len(smalls)
    n = nb + ns

    def body(*refs):
        start, forward, finish = _gather_phases(refs[:n], refs[2 * n:3 * n], nb, *refs[3 * n:])
        start()
        forward()
        finish()

    arrays, landing, sems = _gather_operands(bigs, smalls)
    return pl.pallas_call(
        body, name="all_gather_weights",
        out_shape=[jax.ShapeDtypeStruct(b.shape, b.dtype) for b in landing],
        in_specs=_hbm_specs(2 * n), out_specs=_hbm_specs(n), input_output_aliases={n + k: k for k in range(n)},
        scratch_shapes=sems)(*arrays, *landing)


def _hosted_gather(refs, n, nb, step, total):
    ins, outs, send_sems, recv_sems = refs
    start, forward, finish = _gather_phases(ins, outs, nb, send_sems, recv_sems)
    pl.when(step == 0)(start)
    pl.when(step == (3 * total) // 4)(forward)
    return lambda: pl.when(step == total - 1)(finish)


def _in_proj(h1, w_a, w_b, w_c, bigs, smalls):
    s, d = h1.shape
    tm, tn = min(TM_MM, s), ATTN_W
    na, nq = w_a.shape[1] // tn, w_b.shape[1] // tn
    steps = na + nq + 1
    total = (s // tm) * steps
    nb, n = len(bigs), len(bigs) + len(smalls)
    arrays, landing, sems = _gather_operands(bigs, smalls)

    def body(h_ref, wa_ref, wb_ref, wc_ref, *rest):
        z_ref, qkv_ref, f_ref = rest[2 * n:2 * n + 3]
        m, j = pl.program_id(0), pl.program_id(1)
        finish = _hosted_gather((rest[:n], rest[2 * n + 3:3 * n + 3]) + tuple(rest[3 * n + 3:]), n, nb,
                                m * steps + j, total)
        h = h_ref[...]

        @pl.when(j < na)
        def _():
            z_ref[...] = lax.dot_general(h, wa_ref[...], _NN, preferred_element_type=F32)

        @pl.when((j >= na) & (j < na + nq))
        def _():
            qkv_ref[...] = lax.dot_general(h, wb_ref[...], _NN, preferred_element_type=F32).astype(BF16)

        @pl.when(j == na + nq)
        def _():
            f_ref[...] = lax.dot_general(h, wc_ref[...], _NN, preferred_element_type=F32)

        finish()

    blk_a = lambda m, j: (m, jnp.minimum(j, na - 1))
    blk_b = lambda m, j: (m, jnp.clip(j - na, 0, nq - 1))
    outs = pl.pallas_call(
        body, name="in_proj", grid=(s // tm, steps),
        in_specs=[pl.BlockSpec((tm, d), lambda m, j: (m, 0)),
                  pl.BlockSpec((d, tn), lambda m, j: (0, jnp.minimum(j, na - 1))),
                  pl.BlockSpec((d, tn), lambda m, j: (0, jnp.clip(j - na, 0, nq - 1))),
                  pl.BlockSpec((d, LANES), lambda m, j: (0, 0))] + _hbm_specs(2 * n),
        out_specs=[pl.BlockSpec((tm, tn), blk_a), pl.BlockSpec((tm, tn), blk_b),
                   pl.BlockSpec((tm, LANES), lambda m, j: (m, 0))] + _hbm_specs(n),
        out_shape=[jax.ShapeDtypeStruct((s, w_a.shape[1]), F32), jax.ShapeDtypeStruct((s, w_b.shape[1]), BF16),
                   jax.ShapeDtypeStruct((s, LANES), F32)] + [jax.ShapeDtypeStruct(b.shape, b.dtype) for b in landing],
        input_output_aliases={4 + n + k: 3 + k for k in range(n)}, scratch_shapes=sems,
        compiler_params=_cp(("arbitrary", "arbitrary")))(h1, w_a, w_b, w_c, *arrays, *landing)
    return outs[0], outs[1], outs[2], outs[3:]


def _gather_operands(bigs, smalls):
    x, y, _ = _place()
    arrays = list(bigs) + list(smalls)
    landing = [lax.dynamic_update_index_in_dim(lax.empty((N_CHIPS,) + a.shape, a.dtype), a, 2 * x + y, 0)
               for a in arrays]
    n_sems = 6 * len(bigs) + 3 * len(smalls)
    return arrays, landing, [pltpu.SemaphoreType.DMA((n_sems,)), pltpu.SemaphoreType.DMA((n_sems,))]


def _gather_phases(ins, outs, nb, send_sems, recv_sems):
    n = len(ins)
    x, y, c = _place()
    my_chip = 2 * x + y
    chips = _other_chips(x, y)
    sibling = (x, y, 1 - c)

    def rows(k, which):
        h = ins[k].shape[0] // 2
        return pl.ds(which * h, h)

    def copy(sem, src, dst, to):
        return pltpu.make_async_remote_copy(src_ref=src, dst_ref=dst, send_sem=send_sems.at[sem],
                                            recv_sem=recv_sems.at[sem], device_id=to, device_id_type=MESH)

    def sends():
        out = [copy(6 * k + j, ins[k].at[rows(k, c)], outs[k].at[my_chip, rows(k, c)], (cx, cy, c))
               for k in range(nb) for j, (cx, cy) in enumerate(chips)]
        return out + [copy(6 * nb + 3 * (k - nb) + j, ins[k], outs[k].at[my_chip], (cx, cy, c))
                      for k in range(nb, n) for j, (cx, cy) in enumerate(chips)]

    def landed(k, j, which):
        cx, cy = chips[j]
        return outs[k].at[2 * cx + cy, rows(k, which)]

    def forwards():
        return [copy(6 * k + 3 + j, landed(k, j, c), landed(k, j, c), sibling)
                for j in range(3) for k in range(nb)]

    def start():
        for cp in sends():
            cp.start()

    def forward():
        for j in range(3):
            for k in range(nb):
                copy(6 * k + j, landed(k, j, c), landed(k, j, c), (x, y, c)).wait_recv()
                copy(6 * k + 3 + j, landed(k, j, c), landed(k, j, c), sibling).start()

    def finish():
        for j, (cx, cy) in enumerate(chips):
            for k in range(nb):
                copy(6 * k + 3 + j, landed(k, j, 1 - c), landed(k, j, 1 - c), (x, y, c)).wait_recv()
            for k in range(nb, n):
                arrived = outs[k].at[2 * cx + cy]
                copy(6 * nb + 3 * (k - nb) + j, arrived, arrived, (x, y, c)).wait_recv()
        for cp in sends() + forwards():
            cp.wait_send()

    return start, forward, finish


def _pair_exchange(views, name):
    n = len(views)

    def body(*refs):
        ins, outs, send_sems, recv_sems = refs[:n], refs[n:2 * n], refs[2 * n], refs[2 * n + 1]
        x, y, c = _place()
        copies = [pltpu.make_async_remote_copy(
            src_ref=ins[k].at[:, pl.ds(1 - c, 1)], dst_ref=outs[k], send_sem=send_sems.at[k],
            recv_sem=recv_sems.at[k], device_id=(x, y, 1 - c), device_id_type=MESH) for k in range(n)]
        for cp in copies:
            cp.start()
        for cp in copies:
            cp.wait()

    return pl.pallas_call(
        body, name=name,
        out_shape=[jax.ShapeDtypeStruct((v.shape[0], 1) + v.shape[2:], v.dtype) for v in views],
        in_specs=_hbm_specs(n), out_specs=_hbm_specs(n),
        scratch_shapes=[pltpu.SemaphoreType.DMA((n,)), pltpu.SemaphoreType.DMA((n,))])(*views)


def _scatter_to_chips(parts):
    n = len(parts)

    def body(*refs):
        start, finish = _scatter_phases(refs[:n], refs[n:2 * n], refs[2 * n], refs[2 * n + 1])
        start()
        finish()

    shapes, sems = _scatter_operands(parts)
    return pl.pallas_call(
        body, name="scatter_grads", out_shape=shapes, in_specs=_hbm_specs(n), out_specs=_hbm_specs(n),
        scratch_shapes=sems)(*parts)


def _scatter_operands(parts):
    n = len(parts)
    return ([jax.ShapeDtypeStruct((3,) + p.shape[1:], p.dtype) for p in parts],
            [pltpu.SemaphoreType.DMA((3 * n,)), pltpu.SemaphoreType.DMA((3 * n,))])


def _scatter_phases(ins, outs, send_sems, recv_sems):
    x, y, c = _place()

    def copies():
        return [pltpu.make_async_remote_copy(
            src_ref=ins[k].at[pl.ds(2 * cx + cy, 1)], dst_ref=outs[k].at[pl.ds(r, 1)], send_sem=send_sems.at[3 * k + r],
            recv_sem=recv_sems.at[3 * k + r], device_id=(cx, cy, c), device_id_type=MESH)
            for k in range(len(ins)) for r, (cx, cy) in enumerate(_other_chips(x, y))]

    def start():
        for cp in copies():
            cp.start()

    def finish():
        for cp in copies():
            cp.wait()

    return start, finish


def _hosted_scatter(ins, outs, sems, step, total):
    start, finish = _scatter_phases(ins, outs, *sems)
    pl.when(step == 0)(start)
    return lambda: pl.when(step == total - 1)(finish)


def _join_halves(shards):
    n = len(shards)

    def body(*refs):
        ins, outs, send_sems, recv_sems = refs[:n], refs[n:2 * n], refs[2 * n], refs[2 * n + 1]
        x, y, c = _place()

        def rows(ref, which):
            h = ref.shape[0] // 2
            return ref.at[pl.ds(which * h, h)]

        sent = [pltpu.make_async_remote_copy(
            src_ref=rows(ins[k], c), dst_ref=rows(outs[k], c), send_sem=send_sems.at[k], recv_sem=recv_sems.at[k],
            device_id=(x, y, 1 - c), device_id_type=MESH) for k in range(n)]
        for cp in sent:
            cp.start()
        for k in range(n):
            pltpu.make_async_remote_copy(
                src_ref=rows(ins[k], 1 - c), dst_ref=rows(outs[k], 1 - c), send_sem=send_sems.at[k],
                recv_sem=recv_sems.at[k], device_id=(x, y, 1 - c), device_id_type=MESH).wait_recv()
        for cp in sent:
            cp.wait_send()

    return pl.pallas_call(
        body, name="half_exchange", out_shape=[jax.ShapeDtypeStruct(a.shape, a.dtype) for a in shards],
        in_specs=_hbm_specs(n), out_specs=_hbm_specs(n), input_output_aliases={k: k for k in range(n)},
        scratch_shapes=[pltpu.SemaphoreType.DMA((n,)), pltpu.SemaphoreType.DMA((n,))])(*shards)


def _all_reduce_small(packet):
    rows, width = packet.shape
    n_dev = 8

    def body(x_ref, out_ref, gath, send_sems, recv_sems):
        x, y, c = _place()
        me, sibling = (x, y, c), (x, y, 1 - c)
        chips = _other_chips(x, y)

        def slot(px, py, pc):
            return gath.at[pl.ds((4 * px + 2 * py + pc) * rows, rows), :]

        def copy(k, block, to, src=None):
            return pltpu.make_async_remote_copy(
                src_ref=slot(*block) if src is None else src, dst_ref=slot(*block), send_sem=send_sems.at[k],
                recv_sem=recv_sems.at[k], device_id=to, device_id_type=MESH)

        first = [copy(0, me, sibling, src=x_ref)]
        first += [copy(1 + j, me, (*chip, c), src=x_ref) for j, chip in enumerate(chips)]
        for cp in first:
            cp.start()
        gath[pl.ds((4 * x + 2 * y + c) * rows, rows), :] = x_ref[...]
        passed = [copy(4 + j, (*chip, c), sibling) for j, chip in enumerate(chips)]
        for j, chip in enumerate(chips):
            copy(1 + j, (*chip, c), me).wait_recv()
            passed[j].start()
        copy(0, sibling, me).wait_recv()
        for j, chip in enumerate(chips):
            copy(4 + j, (*chip, 1 - c), me).wait_recv()
        for cp in first + passed:
            cp.wait_send()
        acc = gath[0:rows, :]
        for d in range(1, n_dev):
            acc = acc + gath[d * rows:(d + 1) * rows, :]
        out_ref[...] = acc

    return pl.pallas_call(
        body, name="all_reduce_small", out_shape=jax.ShapeDtypeStruct((rows, width), F32),
        in_specs=[pl.BlockSpec(memory_space=pltpu.VMEM)], out_specs=pl.BlockSpec(memory_space=pltpu.VMEM),
        scratch_shapes=[pltpu.VMEM((n_dev * rows, width), F32), pltpu.SemaphoreType.DMA((7,)),
                        pltpu.SemaphoreType.DMA((7,))])(packet)


def _flat_rows(parts, width, row_multiple):
    flat = jnp.concatenate([p.astype(F32).reshape(-1) for p in parts])
    rows = -(-flat.shape[0] // width)
    rows = -(-rows // row_multiple) * row_multiple
    return jnp.pad(flat, (0, rows * width - flat.shape[0])).reshape(rows, width)


def _unflatten(flat2d, shapes):
    flat = flat2d.reshape(-1)
    out, off = [], 0
    for shp in shapes:
        n = 1
        for dim in shp:
            n *= dim
        out.append(flat[off:off + n].reshape(shp))
        off += n
    return out


def _core_and_chip():
    x, y, c = _place()
    return jnp.stack([c, 2 * x + y]).astype(jnp.int32)


def _pair_sums(chip_major, names, call_name):
    views = [g.reshape(N_CHIPS, 2, g.shape[1] // 2, g.shape[2]) for g in chip_major]
    recv = _pair_exchange(views, call_name)
    sel = _core_and_chip()
    return [_pair_sum(v, r, sel, "pair_sum_" + nm) for v, r, nm in zip(views, recv, names)]


def _finish_grads(pairs, got, names):
    sel = _core_and_chip()
    return _join_halves([_chip_sum(p, g, sel, "chip_sum_" + nm) for (p, _), g, nm in zip(pairs, got, names)])


def kernel(x, g_mix, w_in, b_f, w_conv, g_conv_out, g_attn_out, w_o, g_ffn, w_up, w_ffn_conv, w_down, g_final, loss_target, m_g_mix, m_w_in, m_b_f, m_w_conv, m_g_conv_out, m_g_attn_out, m_w_o, m_g_ffn, m_w_up, m_w_ffn_conv, m_w_down, m_g_final, v_g_mix, v_w_in, v_b_f, v_w_conv, v_g_conv_out, v_g_attn_out, v_w_o, v_g_ffn, v_w_up, v_w_ffn_conv, v_w_down, v_g_final):
    s = x.shape[1]
    x0 = x[0]
    target = loss_target[0]
    d = D_MODEL
    x_pos, y_pos, _ = _place()
    my_chip = 2 * x_pos + y_pos

    (c_in,) = _all_gather_weights([w_in[0].astype(BF16)], [])
    w_in_full = jnp.concatenate([c_in[j] for j in range(N_CHIPS)], axis=1)
    c3 = 3 * CONV_CH
    w_a, w_b = w_in_full[:, :c3], w_in_full[:, c3:c3 + 3 * ATTN_W]
    w_c = jnp.pad(w_in_full[:, c3 + 3 * ATTN_W:], ((0, 0), (0, LANES - N_HEADS)))
    w_q, w_k, w_v = (w_b[:, i * ATTN_W:(i + 1) * ATTN_W] for i in range(3))
    b_pad = jnp.pad(b_f, ((0, 0), (0, LANES - N_HEADS)))

    h1 = _rms_fwd(x0, g_mix, "rms_mix")
    z_a, qkv, f_log, (c_o, c_up, c_conv, c_ffn) = _in_proj(
        h1, w_a, w_b, w_c, [w_o[0].astype(BF16), w_up[0].astype(BF16)], [w_conv[0], w_ffn_conv[0]])
    fb = _gate_fwd(f_log, b_pad)
    qx, kx, kxt, vx, vt, bounds, (c_down,) = _attn_prep(qkv, fb, [w_down[0].astype(BF16)])
    w_o_full = c_o.reshape(d, d)
    w_down_full = c_down.reshape(D_FF, d)
    w_conv_full = jnp.concatenate([c_conv[j] for j in range(N_CHIPS)], axis=1)
    w_ffn_full = jnp.concatenate([c_ffn[j] for j in range(N_CHIPS)], axis=1)
    n_up = c_up.shape[2]
    first_blk, last_blk = _key_block_ranges(bounds)
    o_attn, lse = _attn_fwd_t(qx, kx, vt, first_blk)
    mix = _mixer_fwd(z_a, o_attn, w_conv_full, g_conv_out, g_attn_out)
    x2 = _mm("nn", [mix], [w_o_full], F32, TM_MM, 512, "out_proj", add=x0)
    h2 = _rms_fwd(x2, g_ffn, "rms_ffn")
    up = _mm("nn", [h2], [c_up], F32, TM_MM, n_up, "up_proj", b_chips=True)
    act = _ffn_act_fwd(up, w_ffn_full)
    x3 = _mm("nn", [act], [w_down_full], F32, 512, 512, "down_proj", add=x2)
    dx3, dx3_b, loss_row, gg_final = _loss_head(x3, target, g_final.reshape(1, d))

    dact = _mm("nt", [dx3_b], [w_down_full], F32, TM_MM, 1408, "d_act")
    gw_down = _mm_tn(act, dx3_b, 1408, 1024, "gw_down")
    dup, gwf_lin, gwf_gate = _ffn_act_bwd(up, dact, w_ffn_full)
    dh2 = _mm("nt", [(dup, j // 2, j % 2, n_up) for j in range(N_CHIPS)], [(c_up, j) for j in range(N_CHIPS)],
              F32, 512, 512, "d_h2")
    gw_up = _mm_tn(h2, dup, 1024, n_up, "gw_up", out_chips=True)
    dx2, dx2_b, gg_ffn = _rms_bwd(x2, dh2, g_ffn, dx3, "rms_ffn_bwd", True)
    dmix = _mm("nt", [dx2_b], [w_o_full], F32, TM_MM, 512, "d_mix")
    gw_o = _mm_tn(mix, dx2_b, 1024, 1024, "gw_o")
    early = _pair_sums([gw_o.reshape(N_CHIPS, d // N_CHIPS, d), gw_up, gw_down.reshape(N_CHIPS, D_FF // N_CHIPS, d)],
                       ["w_o", "w_up", "w_down"], "pair_exchange")
    dz_a, dox, gw_conv, gg_conv_out, gg_attn_out, (got_o, got_down) = _mixer_bwd(
        z_a, o_attn, dmix, w_conv_full, g_conv_out, g_attn_out, [early[0][1], early[2][1]])
    dk, dv, dfk, dqt = _attn_bwd_t(qx, dox, kx, kxt, vx, lse, last_blk)
    dq, dfq = _attn_dq_finish(dqt)
    d_f = jnp.transpose((dfk + dfq)[:, :, 0:2, :], (1, 3, 0, 2)).reshape(s, N_HEADS)
    df_b, gb_f = _gate_bwd(f_log, b_pad, jnp.pad(d_f, ((0, 0), (0, LANES - N_HEADS))))
    dh1, (got_up,) = _mm("nt", [dz_a, dq, dk, dv, df_b], [w_a, w_q, w_k, w_v, w_c], F32, TM_MM, 512, "d_h1",
                         scatter=[early[1][1]])
    gw_a = _mm_tn(h1, dz_a, 1024, c3, "gw_in_conv")
    gw_q = _mm_tn(h1, dq, 1024, ATTN_W, "gw_in_q")
    gw_k = _mm_tn(h1, dk, 1024, ATTN_W, "gw_in_k")
    gw_v = _mm_tn(h1, dv, 1024, ATTN_W, "gw_in_v")
    gw_c = _mm_tn(h1, df_b, 1024, LANES, "gw_in_gate")
    gw_in = jnp.concatenate([gw_a, gw_q, gw_k, gw_v, gw_c[:, :N_HEADS]], axis=1)
    n_in = IN_COLS // N_CHIPS
    gw_in = jnp.stack([gw_in[:, j * n_in:(j + 1) * n_in] for j in range(N_CHIPS)])
    late = _pair_sums([gw_in], ["w_in"], "pair_exchange_w_in")
    grad_x, gg_mix, (got_in,) = _rms_bwd(x0, dh1, g_mix, dx2, "rms_mix_bwd", False, scatter=[late[0][1]])
    g_w_in, g_w_o, g_w_up, g_w_down = _finish_grads(late + early, [got_in, got_o, got_up, got_down],
                                                    ["w_in", "w_o", "w_up", "w_down"])

    gw_ffn = jnp.concatenate([gwf_lin, gwf_gate], axis=1)
    small_parts = [gg_mix, gg_conv_out, gg_attn_out, gg_ffn, gg_final, gb_f[:, :N_HEADS], loss_row[:, 0:1], gw_conv,
                   gw_ffn]
    small_shapes = [a.shape for a in small_parts]
    tot = _unflatten(_all_reduce_small(_flat_rows(small_parts, d, 8)), small_shapes)
    g_g_mix, g_g_conv_out, g_g_attn_out, g_g_ffn, g_g_final, g_b_f, loss_sum, g_conv_full, g_ffn_full = tot
    loss = loss_sum[0, 0]
    g_g_final = g_g_final[0]
    g_w_conv = lax.dynamic_slice_in_dim(g_conv_full, my_chip * (CONV_CH // N_CHIPS), CONV_CH // N_CHIPS, axis=1)
    g_w_ffn = lax.dynamic_slice_in_dim(g_ffn_full, my_chip * n_up, n_up, axis=1)

    def adam_big(w, g, m, v, name):
        dl, nm, nv = _adamw(w[0], g, m[0], v[0], name)
        return dl[None], nm[None], nv[None]

    u_w_in = adam_big(w_in, g_w_in, m_w_in, v_w_in, "adam_w_in")
    u_w_o = adam_big(w_o, g_w_o, m_w_o, v_w_o, "adam_w_o")
    u_w_up = adam_big(w_up, g_w_up, m_w_up, v_w_up, "adam_w_up")
    u_w_down = adam_big(w_down, g_w_down, m_w_down, v_w_down, "adam_w_down")

    small_w = [g_mix, b_f, g_conv_out, g_attn_out, g_ffn, g_final, w_conv, w_ffn_conv]
    small_g = [g_g_mix, g_b_f, g_g_conv_out, g_g_attn_out, g_g_ffn, g_g_final, g_w_conv, g_w_ffn]
    small_m = [m_g_mix, m_b_f, m_g_conv_out, m_g_attn_out, m_g_ffn, m_g_final, m_w_conv, m_w_ffn_conv]
    small_v = [v_g_mix, v_b_f, v_g_conv_out, v_g_attn_out, v_g_ffn, v_g_final, v_w_conv, v_w_ffn_conv]
    shapes = [a.shape for a in small_w]
    pack = lambda arrs: _flat_rows(arrs, LANES, 8)
    sd, sm, sv = _adamw(pack(small_w), pack(small_g), pack(small_m), pack(small_v), "adam_small")
    sd, sm, sv = _unflatten(sd, shapes), _unflatten(sm, shapes), _unflatten(sv, shapes)
    (d_g_mix, d_b_f, d_g_conv_out, d_g_attn_out, d_g_ffn, d_g_final, d_w_conv, d_w_ffn) = sd
    (nm_g_mix, nm_b_f, nm_g_conv_out, nm_g_attn_out, nm_g_ffn, nm_g_final, nm_w_conv, nm_w_ffn) = sm
    (nv_g_mix, nv_b_f, nv_g_conv_out, nv_g_attn_out, nv_g_ffn, nv_g_final, nv_w_conv, nv_w_ffn) = sv

    grads = (g_g_mix, g_w_in[None], g_b_f, g_w_conv[None], g_g_conv_out, g_g_attn_out, g_w_o[None], g_g_ffn,
             g_w_up[None], g_w_ffn[None], g_w_down[None], g_g_final)
    deltas = (d_g_mix, u_w_in[0], d_b_f, d_w_conv, d_g_conv_out, d_g_attn_out, u_w_o[0], d_g_ffn, u_w_up[0],
              d_w_ffn, u_w_down[0], d_g_final)
    new_m = (nm_g_mix, u_w_in[1], nm_b_f, nm_w_conv, nm_g_conv_out, nm_g_attn_out, u_w_o[1], nm_g_ffn, u_w_up[1],
             nm_w_ffn, u_w_down[1], nm_g_final)
    new_v = (nv_g_mix, u_w_in[2], nv_b_f, nv_w_conv, nv_g_conv_out, nv_g_attn_out, u_w_o[2], nv_g_ffn, u_w_up[2],
             nv_w_ffn, u_w_down[2], nv_g_final)
    return (loss, grad_x[None], *grads, *deltas, *new_m, *new_v)
```

```python
import functools

import jax
import jax.numpy as jnp
from jax import lax
from jax.experimental import pallas as pl
from jax.experimental.pallas import tpu as pltpu

F32, BF16 = jnp.float32, jnp.bfloat16
MESH = pl.DeviceIdType.MESH

D_MODEL = 1024
CONV_CH = 512
ATTN_W = 512
N_HEADS = 8
HEAD_DIM = 64
N_PAIRS = N_HEADS // 2
D_FF = 2816
IN_COLS = 3 * CONV_CH + 3 * ATTN_W + N_HEADS
EPS = 1e-6
Q_SCALE = 0.125
EXP_ZERO = 104.0
N_CHIPS = 4
LANES = 128
HALO = 8

ADAM_LR, ADAM_B1, ADAM_B2, ADAM_EPS, ADAM_WD, ADAM_STEP = 0.001, 0.9, 0.999, 1e-08, 0.01, 10

TM_ROWS = 512
TM_MM = 1024
TK_TN = 1024
TQ = 512
ROW_CHUNK = 32
TM_FFN = 1024
TN_FFN = 256
VMEM_LIMIT = 52 * 2**20


def _cp(sem, vmem=VMEM_LIMIT):
    return pltpu.CompilerParams(dimension_semantics=sem, vmem_limit_bytes=vmem)


def _bf(a):
    return a if a.dtype == BF16 else a.astype(BF16)


def _mm(mode, a_list, b_list, out_dtype, tm, tn, name, add=None, b_chips=False, scatter=()):
    n_p = len(a_list)
    a0 = a_list[0]
    m_dim = a0[0].shape[1] if isinstance(a0, tuple) else a0.shape[0]
    b0 = b_list[0]
    if b_chips:
        n_dim = b0.shape[0] * b0.shape[2]
        assert tn == b0.shape[2] and mode == "nn"
    else:
        b0 = b0[0][b0[1]] if isinstance(b0, tuple) else b0
        n_dim = b0.shape[1 if mode == "nn" else 0]
    tm, tn = min(tm, m_dim), min(tn, n_dim)
    assert m_dim % tm == 0 and n_dim % tn == 0
    dims = (((1,), (0,)), ((), ())) if mode == "nn" else (((1,), (1,)), ((), ()))
    in_specs, args = [], []
    for a in a_list:
        if isinstance(a, tuple):
            arr, lead, col, width = a
            in_specs.append(pl.BlockSpec((None, tm, width), lambda m, n, lead=lead, col=col: (lead, m, col)))
        else:
            arr = a
            in_specs.append(pl.BlockSpec((tm, a.shape[1]), lambda m, n: (m, 0)))
        args.append(arr)
    for b in b_list:
        if b_chips:
            arr = b
            in_specs.append(pl.BlockSpec((None, b.shape[1], tn), lambda m, n: (n, 0, 0)))
        elif isinstance(b, tuple):
            arr, lead = b
            if mode == "nn":
                in_specs.append(pl.BlockSpec((None, arr.shape[1], tn), lambda m, n, lead=lead: (lead, 0, n)))
            else:
                in_specs.append(pl.BlockSpec((None, tn, arr.shape[2]), lambda m, n, lead=lead: (lead, n, 0)))
        elif mode == "nn":
            arr = b
            in_specs.append(pl.BlockSpec((b.shape[0], tn), lambda m, n: (0, n)))
        else:
            arr = b
            in_specs.append(pl.BlockSpec((tn, b.shape[1]), lambda m, n: (n, 0)))
        args.append(arr)
    if add is not None:
        in_specs.append(pl.BlockSpec((tm, tn), lambda m, n: (m, n)))
        args.append(add)

    n_in = len(args)
    n_sc = len(scatter)
    grid = (m_dim // tm, n_dim // tn)

    def body(*refs):
        o_ref = refs[n_in + n_sc]
        if n_sc:
            finish = _hosted_scatter(refs[n_in:n_in + n_sc], refs[n_in + n_sc + 1:n_in + 2 * n_sc + 1],
                                     refs[n_in + 2 * n_sc + 1:], pl.program_id(0) * grid[1] + pl.program_id(1),
                                     grid[0] * grid[1])
        acc = None
        for i in range(n_p):
            d = lax.dot_general(_bf(refs[i][...]), _bf(refs[n_p + i][...]), dims,
                                preferred_element_type=F32)
            acc = d if acc is None else acc + d
        if add is not None:
            acc = refs[2 * n_p][...] + acc
        o_ref[...] = acc.astype(out_dtype)
        if n_sc:
            finish()

    main_spec = pl.BlockSpec((tm, tn), lambda m, n: (m, n))
    main_shape = jax.ShapeDtypeStruct((m_dim, n_dim), out_dtype)
    if not n_sc:
        return pl.pallas_call(body, name=name, grid=grid, in_specs=in_specs, out_specs=main_spec,
                              out_shape=main_shape, compiler_params=_cp(("parallel", "parallel")))(*args)
    got_shapes, sems = _scatter_operands(scatter)
    outs = pl.pallas_call(
        body, name=name, grid=grid, in_specs=in_specs + _hbm_specs(n_sc), out_specs=[main_spec] + _hbm_specs(n_sc),
        out_shape=[main_shape] + got_shapes, scratch_shapes=sems,
        compiler_params=_cp(("arbitrary", "arbitrary")))(*args, *scatter)
    return outs[0], outs[1:]


def _mm_tn(a, b, tm, tn, name, out_chips=False):
    k_dim, m_dim = a.shape
    n_dim = b.shape[-1] * (b.shape[0] if b.ndim == 3 else 1)
    tm, tn, tk = min(tm, m_dim), min(tn, b.shape[-1]), min(TK_TN, k_dim)
    assert m_dim % tm == 0 and b.shape[-1] % tn == 0 and k_dim % tk == 0
    per = b.shape[-1] // tn
    if b.ndim == 3:
        b_spec = pl.BlockSpec((None, tk, tn), lambda m, n, k: (n // per, k, n % per))
    else:
        b_spec = pl.BlockSpec((tk, tn), lambda m, n, k: (k, n))

    def body(a_ref, b_ref, o_ref):
        @pl.when(pl.program_id(2) == 0)
        def _():
            o_ref[...] = jnp.zeros_like(o_ref)
        o_ref[...] += lax.dot_general(_bf(a_ref[...]), _bf(b_ref[...]), (((0,), (0,)), ((), ())),
                                      preferred_element_type=F32)

    return pl.pallas_call(
        body, name=name, grid=(m_dim // tm, n_dim // tn, k_dim // tk),
        in_specs=[pl.BlockSpec((tk, tm), lambda m, n, k: (k, m)), b_spec],
        out_specs=(pl.BlockSpec((None, tm, tn), lambda m, n, k: (n, m, 0)) if out_chips
                   else pl.BlockSpec((tm, tn), lambda m, n, k: (m, n))),
        out_shape=jax.ShapeDtypeStruct((n_dim // tn, m_dim, tn) if out_chips else (m_dim, n_dim), F32),
        compiler_params=_cp(("parallel", "parallel", "arbitrary")))(a, b)


def _rstd(x):
    return lax.rsqrt(jnp.mean(x * x, axis=-1, keepdims=True) + EPS)


def _rms_fwd(x, g, name):
    s, d = x.shape
    tm = min(TM_ROWS, s)

    def body(x_ref, g_ref, h_ref):
        xv = x_ref[...]
        h_ref[...] = (xv * _rstd(xv) * g_ref[...]).astype(BF16)

    return pl.pallas_call(
        body, name=name, grid=(s // tm,),
        in_specs=[pl.BlockSpec((tm, d), lambda i: (i, 0)), pl.BlockSpec((1, d), lambda i: (0, 0))],
        out_specs=pl.BlockSpec((tm, d), lambda i: (i, 0)),
        out_shape=jax.ShapeDtypeStruct((s, d), BF16), compiler_params=_cp(("parallel",)))(x, g)


def _rms_bwd(x, dh, g, dres, name, with_bf16, scatter=()):
    s, d = x.shape
    tm = min(TM_ROWS, s)
    n_sc = len(scatter)
    n_out = 3 if with_bf16 else 2
    got_shapes, sems = _scatter_operands(scatter) if n_sc else ([], [])

    def body(x_ref, dh_ref, g_ref, dres_ref, *rest):
        dx_ref, gg_ref = rest[n_sc], rest[n_sc + n_out - 1]
        i = pl.program_id(0)
        if n_sc:
            finish = _hosted_scatter(rest[:n_sc], rest[n_sc + n_out:2 * n_sc + n_out], rest[2 * n_sc + n_out:], i,
                                     s // tm)

        @pl.when(i == 0)
        def _():
            gg_ref[...] = jnp.zeros_like(gg_ref)

        xv = x_ref[...]
        xn = xv * _rstd(xv)
        dhv = dh_ref[...]
        gg_ref[...] += jnp.sum(dhv * xn, axis=0, keepdims=True)
        t = dhv * g_ref[...]
        dx = dres_ref[...] + _rstd(xv) * (t - xn * jnp.mean(t * xn, axis=-1, keepdims=True))
        dx_ref[...] = dx
        if with_bf16:
            rest[n_sc + 1][...] = dx.astype(BF16)
        if n_sc:
            finish()

    row = pl.BlockSpec((tm, d), lambda i: (i, 0))
    vec = pl.BlockSpec((1, d), lambda i: (0, 0))
    out_specs = [row] + ([row] if with_bf16 else []) + [vec] + _hbm_specs(n_sc)
    out_shape = ([jax.ShapeDtypeStruct((s, d), F32)] + ([jax.ShapeDtypeStruct((s, d), BF16)] if with_bf16 else [])
                 + [jax.ShapeDtypeStruct((1, d), F32)] + got_shapes)
    outs = pl.pallas_call(
        body, name=name, grid=(s // tm,), in_specs=[row, row, vec, row] + _hbm_specs(n_sc), out_specs=out_specs,
        out_shape=out_shape, scratch_shapes=sems, compiler_params=_cp(("arbitrary",)))(x, dh, g, dres, *scatter)
    return tuple(outs[:n_out]) + ((outs[n_out:],) if n_sc else ())


def _loss_head(x3, target, g):
    s, d = x3.shape
    tm = min(TM_ROWS, s)

    def body(x_ref, t_ref, g_ref, dx_ref, dxb_ref, loss_ref, gg_ref):
        @pl.when(pl.program_id(0) == 0)
        def _():
            gg_ref[...] = jnp.zeros_like(gg_ref)
            loss_ref[...] = jnp.zeros_like(loss_ref)

        xv = x_ref[...]
        r = _rstd(xv)
        xn = xv * r
        gv = g_ref[...]
        err = xn * gv - t_ref[...]
        loss_ref[...] += 0.5 * jnp.sum(jnp.mean(err * err, axis=-1, keepdims=True), axis=0, keepdims=True)
        dy = err * (1.0 / d)
        gg_ref[...] += jnp.sum(dy * xn, axis=0, keepdims=True)
        t = dy * gv
        dx = r * (t - xn * jnp.mean(t * xn, axis=-1, keepdims=True))
        dx_ref[...] = dx
        dxb_ref[...] = dx.astype(BF16)

    row = pl.BlockSpec((tm, d), lambda i: (i, 0))
    vec = pl.BlockSpec((1, d), lambda i: (0, 0))
    return pl.pallas_call(
        body, name="loss_head", grid=(s // tm,), in_specs=[row, row, vec],
        out_specs=[row, row, pl.BlockSpec((1, LANES), lambda i: (0, 0)), vec],
        out_shape=[jax.ShapeDtypeStruct((s, d), F32), jax.ShapeDtypeStruct((s, d), BF16),
                   jax.ShapeDtypeStruct((1, LANES), F32), jax.ShapeDtypeStruct((1, d), F32)],
        compiler_params=_cp(("arbitrary",)))(x3, target, g)


def _prev_halo_spec(tm, width, col):
    return pl.BlockSpec((HALO, width), lambda i, *_: (jnp.maximum(i * (tm // HALO) - 1, 0), col))


def _next_halo_spec(tm, width, col, s):
    return pl.BlockSpec((HALO, width), lambda i, *_: (jnp.minimum((i + 1) * (tm // HALO), s // HALO - 1), col))


def _shift_down(x, k):
    return pltpu.roll(x, k, 0)


def _shift_up(x, k):
    return pltpu.roll(x, x.shape[0] - k, 0)


def _conv_taps(x_ext, w):
    return w[0:1, :] * _shift_down(x_ext, 2) + w[1:2, :] * _shift_down(x_ext, 1) + w[2:3, :] * x_ext


def _conv_taps_t(d_ext, w):
    return w[2:3, :] * d_ext + w[1:2, :] * _shift_up(d_ext, 1) + w[0:1, :] * _shift_up(d_ext, 2)


def _mixer_fwd(z_a, o_attn, w_conv, g_conv_out, g_attn_out):
    s = z_a.shape[0]
    c = CONV_CH
    tm = min(TM_ROWS, s)

    def body(gb_ref, gc_ref, xc_ref, gcp_ref, xcp_ref, o_ref, w_ref, gco_ref, gao_ref, mix_ref):
        i = pl.program_id(0)
        cx = gc_ref[...] * xc_ref[...]
        cx_prev = jnp.where(i > 0, gcp_ref[...] * xcp_ref[...], 0.0)
        conv = _conv_taps(jnp.concatenate([cx_prev, cx], axis=0), w_ref[...])[HALO:]
        y = gb_ref[...] * conv
        mix_ref[:, 0:c] = (y * _rstd(y) * gco_ref[...]).astype(BF16)
        o = o_ref[...]
        mix_ref[:, c:2 * c] = (o * _rstd(o) * gao_ref[...]).astype(BF16)

    col = lambda j: pl.BlockSpec((tm, c), lambda i: (i, j))
    vec = pl.BlockSpec((1, c), lambda i: (0, 0))
    return pl.pallas_call(
        body, name="mixer_fwd", grid=(s // tm,),
        in_specs=[col(0), col(1), col(2), _prev_halo_spec(tm, c, 1), _prev_halo_spec(tm, c, 2), col(0),
                  pl.BlockSpec((3, c), lambda i: (0, 0)), vec, vec],
        out_specs=pl.BlockSpec((tm, 2 * c), lambda i: (i, 0)),
        out_shape=jax.ShapeDtypeStruct((s, 2 * c), BF16),
        compiler_params=_cp(("parallel",)))(z_a, z_a, z_a, z_a, z_a, o_attn, w_conv, g_conv_out, g_attn_out)


def _mixer_bwd(z_a, o_attn, dmix, w_conv, g_conv_out, g_attn_out, scatter):
    s = z_a.shape[0]
    c = CONV_CH
    tm = min(TM_ROWS, s)
    n_blk = s // tm
    n_sc = len(scatter)
    got_shapes, sems = _scatter_operands(scatter)

    def body(gb_ref, gc_ref, xc_ref, gcp_ref, xcp_ref, gbn_ref, gcn_ref, xcn_ref, o_ref, dnc_ref, dncn_ref, dna_ref,
             w_ref, gco_ref, gao_ref, *rest):
        dz_ref, dox_ref, gw_ref, ggco_ref, ggao_ref = rest[n_sc:n_sc + 5]
        i = pl.program_id(0)
        finish = _hosted_scatter(rest[:n_sc], rest[n_sc + 5:2 * n_sc + 5], rest[2 * n_sc + 5:], i, n_blk)

        @pl.when(i == 0)
        def _():
            gw_ref[...] = jnp.zeros_like(gw_ref)
            ggco_ref[...] = jnp.zeros_like(ggco_ref)
            ggao_ref[...] = jnp.zeros_like(ggao_ref)

        w = w_ref[...]
        zeros = jnp.zeros((HALO, c), F32)
        gb_e = jnp.concatenate([zeros, gb_ref[...], gbn_ref[...]], axis=0)
        cx_prev = jnp.where(i > 0, gcp_ref[...] * xcp_ref[...], 0.0)
        gc_e = jnp.concatenate([zeros, gc_ref[...], gcn_ref[...]], axis=0)
        xc_e = jnp.concatenate([zeros, xc_ref[...], xcn_ref[...]], axis=0)
        cx_e = jnp.concatenate([cx_prev, gc_ref[...] * xc_ref[...], gcn_ref[...] * xcn_ref[...]], axis=0)
        dn_next = jnp.where(i < n_blk - 1, dncn_ref[...], 0.0)
        dn_e = jnp.concatenate([zeros, dnc_ref[...], dn_next], axis=0)

        conv_e = _conv_taps(cx_e, w)
        y_e = gb_e * conv_e
        r_e = _rstd(y_e)
        yn_e = y_e * r_e
        t_e = dn_e * gco_ref[...]
        dy_e = r_e * (t_e - yn_e * jnp.mean(t_e * yn_e, axis=-1, keepdims=True))
        dconv_e = dy_e * gb_e
        dcx_e = _conv_taps_t(dconv_e, w)
        blk = slice(HALO, HALO + tm)
        dz_ref[:, 0:c] = (dy_e * conv_e)[blk].astype(BF16)
        dz_ref[:, c:2 * c] = (dcx_e * xc_e)[blk].astype(BF16)
        dz_ref[:, 2 * c:3 * c] = (dcx_e * gc_e)[blk].astype(BF16)
        ggco_ref[...] += jnp.sum((dn_e * yn_e)[blk], axis=0, keepdims=True)
        dconv = dconv_e[blk]
        gw_ref[0:1, :] += jnp.sum(dconv * _shift_down(cx_e, 2)[blk], axis=0, keepdims=True)
        gw_ref[1:2, :] += jnp.sum(dconv * _shift_down(cx_e, 1)[blk], axis=0, keepdims=True)
        gw_ref[2:3, :] += jnp.sum(dconv * cx_e[blk], axis=0, keepdims=True)

        o = o_ref[...]
        ra = _rstd(o)
        on = o * ra
        dna = dna_ref[...]
        ggao_ref[...] += jnp.sum(dna * on, axis=0, keepdims=True)
        ta = dna * gao_ref[...]
        do = ra * (ta - on * jnp.mean(ta * on, axis=-1, keepdims=True))
        prod = do * o
        lane = lax.broadcasted_iota(jnp.int32, (tm, LANES), 1)
        head_a = lane < HEAD_DIM
        for p in range(N_PAIRS):
            cols = slice(p * LANES, (p + 1) * LANES)
            pb, dob = prod[:, cols], do[:, cols]
            for hh in range(2):
                sel = head_a if hh == 0 else jnp.logical_not(head_a)
                delta = jnp.sum(jnp.where(sel, pb, 0.0), axis=-1, keepdims=True)
                neg3 = _split3(-delta)
                do_h = pltpu.roll(dob, HEAD_DIM, 1) if hh else dob
                dox_ref[2 * p + hh] = _aug(do_h, lane, neg3).astype(BF16)
        finish()

    col = lambda j: pl.BlockSpec((tm, c), lambda i: (i, j))
    vec = pl.BlockSpec((1, c), lambda i: (0, 0))
    w3 = pl.BlockSpec((3, c), lambda i: (0, 0))
    outs = pl.pallas_call(
        body, name="mixer_bwd", grid=(n_blk,),
        in_specs=[col(0), col(1), col(2), _prev_halo_spec(tm, c, 1), _prev_halo_spec(tm, c, 2),
                  _next_halo_spec(tm, c, 0, s), _next_halo_spec(tm, c, 1, s), _next_halo_spec(tm, c, 2, s),
                  col(0), col(0), _next_halo_spec(tm, c, 0, s), col(1), w3, vec, vec] + _hbm_specs(n_sc),
        out_specs=[pl.BlockSpec((tm, 3 * c), lambda i: (i, 0)),
                   pl.BlockSpec((N_HEADS, tm, LANES), lambda i: (0, i, 0)), w3, vec, vec] + _hbm_specs(n_sc),
        out_shape=[jax.ShapeDtypeStruct((s, 3 * c), BF16), jax.ShapeDtypeStruct((N_HEADS, s, LANES), BF16),
                   jax.ShapeDtypeStruct((3, c), F32), jax.ShapeDtypeStruct((1, c), F32),
                   jax.ShapeDtypeStruct((1, c), F32)] + got_shapes,
        scratch_shapes=sems, compiler_params=_cp(("arbitrary",)))(
            z_a, z_a, z_a, z_a, z_a, z_a, z_a, z_a, o_attn, dmix, dmix, dmix, w_conv, g_conv_out, g_attn_out,
            *scatter)
    return tuple(outs[:5]) + (outs[5:],)


def _gate_fwd(f, b_pad):
    s = f.shape[0]
    tm = min(TQ, s)

    def body(f_ref, b_ref, fb_ref, carry):
        @pl.when(pl.program_id(0) == 0)
        def _():
            carry[...] = jnp.zeros_like(carry)

        z = f_ref[...] + b_ref[...]
        x = jnp.minimum(z, 0.0) - jnp.log1p(jnp.exp(-jnp.abs(z)))
        row = lax.broadcasted_iota(jnp.int32, (tm, LANES), 0)
        sh = 1
        while sh < tm:
            x = x + jnp.where(row >= sh, _shift_down(x, sh), 0.0)
            sh *= 2
        x = x + carry[0:1, :]
        carry[...] = jnp.broadcast_to(x[tm - 1:tm, :], carry.shape)
        head_a = lax.broadcasted_iota(jnp.int32, (tm, LANES), 1) < HEAD_DIM
        for p in range(N_PAIRS):
            fa = jnp.broadcast_to(x[:, 2 * p:2 * p + 1], (tm, LANES))
            fbv = jnp.broadcast_to(x[:, 2 * p + 1:2 * p + 2], (tm, LANES))
            fb_ref[:, p * LANES:(p + 1) * LANES] = jnp.where(head_a, fa, fbv)

    return pl.pallas_call(
        body, name="gate_fwd", grid=(s // tm,),
        in_specs=[pl.BlockSpec((tm, LANES), lambda i: (i, 0)), pl.BlockSpec((1, LANES), lambda i: (0, 0))],
        out_specs=pl.BlockSpec((tm, N_PAIRS * LANES), lambda i: (i, 0)),
        out_shape=jax.ShapeDtypeStruct((s, N_PAIRS * LANES), F32),
        scratch_shapes=[pltpu.VMEM((HALO, LANES), F32)],
        compiler_params=_cp(("arbitrary",)))(f, b_pad)


def _gate_bwd(f, b_pad, d_f):
    s = f.shape[0]
    tm = min(TQ, s)
    n_blk = s // tm

    def body(f_ref, b_ref, d_ref, df_ref, gb_ref, carry):
        @pl.when(pl.program_id(0) == 0)
        def _():
            carry[...] = jnp.zeros_like(carry)
            gb_ref[...] = jnp.zeros_like(gb_ref)

        x = d_ref[...]
        row = lax.broadcasted_iota(jnp.int32, (tm, LANES), 0)
        sh = 1
        while sh < tm:
            x = x + jnp.where(row < tm - sh, _shift_up(x, sh), 0.0)
            sh *= 2
        x = x + carry[0:1, :]
        carry[...] = jnp.broadcast_to(x[0:1, :], carry.shape)
        z = f_ref[...] + b_ref[...]
        d = x * (1.0 / (1.0 + jnp.exp(z)))
        df_ref[...] = d.astype(BF16)
        gb_ref[...] += jnp.sum(d, axis=0, keepdims=True)

    rev = pl.BlockSpec((tm, LANES), lambda i: (n_blk - 1 - i, 0))
    vec = pl.BlockSpec((1, LANES), lambda i: (0, 0))
    return pl.pallas_call(
        body, name="gate_bwd", grid=(n_blk,), in_specs=[rev, vec, rev], out_specs=[rev, vec],
        out_shape=[jax.ShapeDtypeStruct((s, LANES), BF16), jax.ShapeDtypeStruct((1, LANES), F32)],
        scratch_shapes=[pltpu.VMEM((HALO, LANES), F32)],
        compiler_params=_cp(("arbitrary",)))(f, b_pad, d_f)


_NT = (((1,), (1,)), ((), ()))
_NN = (((1,), (0,)), ((), ()))
_TN = (((0,), (0,)), ((), ()))


def _head_masks(shape):
    lane = lax.broadcasted_iota(jnp.int32, shape, len(shape) - 1)
    return lane < HEAD_DIM


def _pick_row(ft, h):
    rows = lax.broadcasted_iota(jnp.int32, ft.shape, 0)
    return jnp.sum(jnp.where(rows == h, ft, 0.0), axis=0, keepdims=True)


def _prune_bounds(qkv, fb):
    s = qkv.shape[0]
    tq = min(TQ, s)

    def body(q_ref, k_ref, fb_ref, out_ref):
        head_a = _head_masks((tq, LANES))
        lane = lax.broadcasted_iota(jnp.int32, (HALO, LANES), 1)
        acc = jnp.zeros((HALO, LANES), F32)
        for p in range(N_PAIRS):
            cols = slice(p * LANES, (p + 1) * LANES)
            q2 = q_ref[:, cols].astype(F32) * Q_SCALE
            k2 = k_ref[:, cols].astype(F32)
            f2 = fb_ref[:, cols]
            for hh in range(2):
                sel = head_a if hh == 0 else jnp.logical_not(head_a)
                qn = jnp.sqrt(jnp.sum(jnp.where(sel, q2 * q2, 0.0), axis=-1, keepdims=True))
                kn = jnp.sqrt(jnp.sum(jnp.where(sel, k2 * k2, 0.0), axis=-1, keepdims=True))
                f = f2[:, hh * HEAD_DIM:hh * HEAD_DIM + 1]
                h = 2 * p + hh
                vals = (jnp.max(qn, axis=0, keepdims=True), jnp.max(kn, axis=0, keepdims=True),
                        jnp.max(qn * kn + f, axis=0, keepdims=True), f[tq - 1:tq, :])
                for slot, v in enumerate(vals):
                    acc = jnp.where(lane == slot * N_HEADS + h, v, acc)
        out_ref[0] = acc

    blk = lambda j: pl.BlockSpec((tq, ATTN_W), lambda i: (i, j))
    return pl.pallas_call(
        body, name="prune_bounds", grid=(s // tq,), in_specs=[blk(0), blk(1), blk(0)],
        out_specs=pl.BlockSpec((1, HALO, LANES), lambda i: (i, 0, 0)),
        out_shape=jax.ShapeDtypeStruct((s // tq, HALO, LANES), F32),
        compiler_params=_cp(("parallel",)))(qkv, qkv, fb)


def _first_key_blocks(qkv, fb):
    t = _prune_bounds(qkv, fb)[:, 0, :]
    nh = N_HEADS
    a, b, c, e = t[:, 0:nh], t[:, nh:2 * nh], t[:, 2 * nh:3 * nh], t[:, 3 * nh:4 * nh]
    bound = a[:, None, :] * b[None, :, :] * 1.001 + c[:, None, :] - e[None, :, :]
    n_q = t.shape[0]
    idx = jnp.arange(n_q)
    need = jnp.logical_not(bound < -(EXP_ZERO + 2.0)) | (idx[None, :, None] >= idx[:, None, None])
    first = jnp.argmax(need, axis=1).astype(jnp.int32)
    return jnp.min(first.reshape(n_q, N_PAIRS, 2), axis=-1).T.reshape(-1)


def _attn_fwd(qkv, fb, ft, first_blk):
    s = qkv.shape[0]
    tq = min(TQ, s)
    n_q = s // tq
    neg = -1e30

    rc = min(ROW_CHUNK, tq)

    def body(first_ref, q_ref, k_ref, v_ref, fb_ref, ft_ref, o_ref, g_ref,
             s_a, s_b, p_ab, m_a, l_a, m_b, l_b, alpha2, acc, pmax_a, psum_a, pmax_b, psum_b):
        p = pl.program_id(0)
        i = pl.program_id(1)
        head_a = _head_masks((tq, LANES))
        q2 = q_ref[...] * Q_SCALE
        zero = jnp.zeros_like(q2)
        q_a, q_b = jnp.where(head_a, q2, zero), jnp.where(head_a, zero, q2)
        for m_scr, l_scr in ((m_a, l_a), (m_b, l_b)):
            m_scr[...] = jnp.full(m_scr.shape, neg, F32)
            l_scr[...] = jnp.zeros(l_scr.shape, F32)
        acc[...] = jnp.zeros(acc.shape, F32)

        def step(kb, masked):
            rows_k = pl.ds(pl.multiple_of(kb * tq, tq), tq)
            k2, v2 = k_ref[rows_k, :], v_ref[rows_k, :]
            ftv = ft_ref[kb]
            fk = (_pick_row(ftv, 2 * p), _pick_row(ftv, 2 * p + 1))
            s_a[...] = lax.dot_general(q_a, k2, _NT, preferred_element_type=F32)
            s_b[...] = lax.dot_general(q_b, k2, _NT, preferred_element_type=F32)
            alphas = []
            for hh, (s_scr, m_scr, l_scr, pmax, psum) in enumerate(((s_a, m_a, l_a, pmax_a, psum_a),
                                                                     (s_b, m_b, l_b, pmax_b, psum_b))):
                fq = fb_ref[:, hh * HEAD_DIM:hh * HEAD_DIM + 1]

                def shifted(r, ncol, add=None):
                    rows = slice(r * rc, (r + 1) * rc)
                    t = s_scr[rows, 0:ncol]
                    if add is not None:
                        t = t + add[rows, :]
                    t = t - fk[hh][:, 0:ncol]
                    if masked:
                        col_id = lax.broadcasted_iota(jnp.int32, (rc, ncol), 1)
                        row_id = lax.broadcasted_iota(jnp.int32, (rc, ncol), 0) + r * rc
                        t = jnp.where(col_id <= row_id, t, -jnp.inf)
                    return rows, t

                def lane_blocks(t, op):
                    out = t[:, 0:LANES]
                    for cb in range(1, t.shape[1] // LANES):
                        out = op(out, t[:, cb * LANES:(cb + 1) * LANES])
                    return out

                ncols = [min(tq, -(-((r + 1) * rc) // LANES) * LANES) if masked else tq for r in range(tq // rc)]
                for r, ncol in enumerate(ncols):
                    rows, t = shifted(r, ncol)
                    pmax[rows, :] = lane_blocks(t, jnp.maximum)
                m_old = m_scr[...]
                m_new = jnp.maximum(m_old, jnp.max(pmax[...], axis=-1, keepdims=True) + fq)
                alpha = jnp.exp(m_old - m_new)
                m_scr[...] = m_new
                shift = fq - m_new
                for r, ncol in enumerate(ncols):
                    rows, t = shifted(r, ncol, shift)
                    pr = jnp.exp(t)
                    psum[rows, :] = lane_blocks(pr, jnp.add)
                    p_ab[rows, hh * tq:hh * tq + ncol] = pr.astype(BF16)
                    if ncol < tq:
                        p_ab[rows, hh * tq + ncol:(hh + 1) * tq] = jnp.zeros((rc, tq - ncol), BF16)
                l_scr[...] = alpha * l_scr[...] + jnp.sum(psum[...], axis=-1, keepdims=True)
                alphas.append(alpha)
            alpha2[...] = jnp.where(head_a, alphas[0], alphas[1])
            zv = jnp.zeros_like(v2)
            head_k = _head_masks(v2.shape)
            vv = jnp.concatenate([jnp.where(head_k, v2, zv), jnp.where(head_k, zv, v2)], axis=0)
            acc[...] = acc[...] * alpha2[...] + lax.dot_general(p_ab[...], vv, _NN, preferred_element_type=F32)

        def unmasked(kb, carry):
            step(kb, False)
            return carry

        lax.fori_loop(first_ref[p * n_q + i], i, unmasked, 0)
        step(i, True)
        fq2 = fb_ref[...]
        o_ref[...] = acc[...] / jnp.where(head_a, l_a[...], l_b[...])
        g_ref[...] = fq2 - jnp.where(head_a, m_a[...] + jnp.log(l_a[...]), m_b[...] + jnp.log(l_b[...]))

    qblk = lambda off: pl.BlockSpec((tq, LANES), lambda p, i, first: (i, off + p))
    full = lambda off: pl.BlockSpec((s, LANES), lambda p, i, first: (0, off + p))
    col_scr = pltpu.VMEM((tq, 1), F32)
    grid_spec = pltpu.PrefetchScalarGridSpec(
        num_scalar_prefetch=1, grid=(N_PAIRS, n_q),
        in_specs=[qblk(0), full(N_PAIRS), full(2 * N_PAIRS), qblk(0),
                  pl.BlockSpec((n_q, N_HEADS, tq), lambda p, i, first: (0, 0, 0))],
        out_specs=[qblk(0), qblk(0)],
        scratch_shapes=[pltpu.VMEM((tq, tq), F32), pltpu.VMEM((tq, tq), F32), pltpu.VMEM((tq, 2 * tq), BF16),
                        col_scr, col_scr, col_scr, col_scr] + [pltpu.VMEM((tq, LANES), F32)] * 6)
    return pl.pallas_call(
        body, name="attn_fwd", grid_spec=grid_spec,
        out_shape=[jax.ShapeDtypeStruct((s, ATTN_W), F32), jax.ShapeDtypeStruct((s, ATTN_W), F32)],
        compiler_params=_cp(("parallel", "arbitrary")))(first_blk, qkv, qkv, qkv, fb, ft)


def _attn_bwd(qkv, do, g, delta, ft, first_blk):
    s = qkv.shape[0]
    tq = min(TQ, s)
    n_q = s // tq

    def body(first_ref, q_ref, do_ref, g_ref, dl_ref, k_ref, v_ref, ft_ref, dq_ref, dfq_ref, dk_ref, dv_ref,
             dfk_ref):
        p = pl.program_id(0)
        i = pl.program_id(1)

        @pl.when(i == 0)
        def _():
            dk_ref[...] = jnp.zeros_like(dk_ref)
            dv_ref[...] = jnp.zeros_like(dv_ref)
            dfk_ref[...] = jnp.zeros_like(dfk_ref)

        head_a = _head_masks((tq, LANES))
        q2 = q_ref[...] * Q_SCALE
        do2 = do_ref[...]
        zero = jnp.zeros_like(q2)
        q_a, q_b = jnp.where(head_a, q2, zero), jnp.where(head_a, zero, q2)
        do_a, do_b = jnp.where(head_a, do2, zero), jnp.where(head_a, zero, do2)
        g2, dl2 = g_ref[...], dl_ref[...]
        g_a, g_b = g2[:, 0:1], g2[:, HEAD_DIM:HEAD_DIM + 1]
        dl_a, dl_b = dl2[:, 0:1], dl2[:, HEAD_DIM:HEAD_DIM + 1]
        causal = lax.broadcasted_iota(jnp.int32, (tq, tq), 1) <= lax.broadcasted_iota(jnp.int32, (tq, tq), 0)
        rows8 = lax.broadcasted_iota(jnp.int32, (N_HEADS, tq), 0)

        def one_head(q_h, do_h, g_h, dl_h, fk_h, k2, v2, masked):
            sc = (lax.dot_general(q_h, k2, _NT, preferred_element_type=F32) + g_h) - fk_h
            pr = jnp.exp(sc)
            if masked:
                pr = jnp.where(causal, pr, 0.0)
            dp = lax.dot_general(do_h, v2, _NT, preferred_element_type=F32)
            ds = pr * (dp - dl_h)
            return (pr.astype(BF16), ds.astype(BF16), jnp.sum(ds, axis=0, keepdims=True),
                    jnp.sum(ds, axis=1, keepdims=True))

        def step(j, carry, masked):
            dq, r_a, r_b = carry
            rows = pl.ds(pl.multiple_of(j * tq, tq), tq)
            k2, v2 = k_ref[rows, :], v_ref[rows, :]
            k_a, k_b = jnp.where(head_a, k2, zero), jnp.where(head_a, zero, k2)
            ftv = ft_ref[j]
            p_a, ds_a, c_a, s_a = one_head(q_a, do_a, g_a, dl_a, _pick_row(ftv, 2 * p), k2, v2, masked)
            p_b, ds_b, c_b, s_b = one_head(q_b, do_b, g_b, dl_b, _pick_row(ftv, 2 * p + 1), k2, v2, masked)
            dv_ref[rows, :] += (lax.dot_general(p_a, do_a, _TN, preferred_element_type=F32)
                                + lax.dot_general(p_b, do_b, _TN, preferred_element_type=F32))
            dk_ref[rows, :] += (lax.dot_general(ds_a, q_a, _TN, preferred_element_type=F32)
                                + lax.dot_general(ds_b, q_b, _TN, preferred_element_type=F32))
            dfk_ref[0, j] += jnp.where(rows8 == 0, -c_a, jnp.where(rows8 == 1, -c_b, 0.0))
            dq = dq + (lax.dot_general(ds_a, k_a, _NN, preferred_element_type=F32)
                       + lax.dot_general(ds_b, k_b, _NN, preferred_element_type=F32))
            return dq, r_a + s_a, r_b + s_b

        zcol = jnp.zeros((tq, 1), F32)
        carry = lax.fori_loop(first_ref[p * n_q + i], i, lambda j, cr: step(j, cr, False),
                              (jnp.zeros((tq, LANES), F32), zcol, zcol))
        dq, r_a, r_b = step(i, carry, True)
        dq_ref[...] = (dq * Q_SCALE).astype(BF16)
        dfq_ref[...] = jnp.where(head_a, r_a, r_b)

    qblk = lambda off: pl.BlockSpec((tq, LANES), lambda p, i, first: (i, off + p))
    full = lambda off: pl.BlockSpec((s, LANES), lambda p, i, first: (0, off + p))
    grid_spec = pltpu.PrefetchScalarGridSpec(
        num_scalar_prefetch=1, grid=(N_PAIRS, n_q),
        in_specs=[qblk(0), qblk(0), qblk(0), qblk(0), full(N_PAIRS), full(2 * N_PAIRS),
                  pl.BlockSpec((n_q, N_HEADS, tq), lambda p, i, first: (0, 0, 0))],
        out_specs=[qblk(0), qblk(0), full(0), full(0),
                   pl.BlockSpec((1, n_q, N_HEADS, tq), lambda p, i, first: (p, 0, 0, 0))])
    return pl.pallas_call(
        body, name="attn_bwd", grid_spec=grid_spec,
        out_shape=[jax.ShapeDtypeStruct((s, ATTN_W), BF16), jax.ShapeDtypeStruct((s, ATTN_W), F32),
                   jax.ShapeDtypeStruct((s, ATTN_W), F32), jax.ShapeDtypeStruct((s, ATTN_W), F32),
                   jax.ShapeDtypeStruct((N_PAIRS, n_q, N_HEADS, tq), F32)],
        compiler_params=_cp(("parallel", "arbitrary")))(first_blk, qkv, do, g, delta, qkv, qkv, ft)


AUG = HEAD_DIM
NORM_MARGIN = 1.01


def _split3(x):
    hi = x.astype(BF16).astype(F32)
    r = x - hi
    mid = r.astype(BF16).astype(F32)
    lo = (r - mid).astype(BF16).astype(F32)
    return hi, mid, lo


def _aug(base, lane, vals):
    out = jnp.where(lane < AUG, base, 0.0)
    for k, v in enumerate(vals):
        out = jnp.where(lane == AUG + k, v, out)
    return out


def _attn_prep(qkv, fb, bigs):
    s = qkv.shape[0]
    tq = min(TQ, s)
    n_q = s // tq

    n = len(bigs)
    arrays, landing, sems = _gather_operands(bigs, [])

    def body(q_ref, k_ref, v_ref, fb_ref, *rest):
        qx_ref, kx_ref, kxt_ref, vx_ref, vt_ref, b_ref = rest[2 * n:2 * n + 6]
        finish = _hosted_gather((rest[:n], rest[2 * n + 6:3 * n + 6]) + tuple(rest[3 * n + 6:]), n, n,
                                pl.program_id(0), n_q)
        lane = lax.broadcasted_iota(jnp.int32, (tq, LANES), 1)
        lane8 = lax.broadcasted_iota(jnp.int32, (HALO, LANES), 1)
        head_lanes = lane < AUG
        is_lane = [lane == AUG + k for k in range(6)]
        first3 = (lane >= AUG) & (lane < AUG + 3)
        next3 = (lane >= AUG + 3) & (lane < AUG + 6)
        q_const = jnp.where(first3, -1.0, 0.0)
        k_const = jnp.where(next3, 1.0, 0.0)
        v_const = jnp.where(first3, 1.0, 0.0)
        ones_head = (lax.broadcasted_iota(jnp.int32, (LANES, LANES), 0) < HEAD_DIM).astype(BF16)
        acc = jnp.zeros((HALO, LANES), F32)
        for p in range(N_PAIRS):
            cols = slice(p * LANES, (p + 1) * LANES)
            q2, k2, v2 = (ref[:, cols].astype(F32) for ref in (q_ref, k_ref, v_ref))
            q2 = q2 * Q_SCALE
            f2 = fb_ref[:, cols]
            for hh in range(2):
                h = 2 * p + hh
                q, k, v = ((pltpu.roll(x, HEAD_DIM, 1) if hh else x) for x in (q2, k2, v2))
                f = f2 if hh else pltpu.roll(f2, HEAD_DIM, 1)
                hi, mid, lo = _split3(f)
                q_aug = jnp.where(is_lane[3], hi, jnp.where(is_lane[4], mid, jnp.where(is_lane[5], lo, q_const)))
                k_aug = jnp.where(is_lane[0], hi, jnp.where(is_lane[1], mid, jnp.where(is_lane[2], lo, k_const)))
                kx = jnp.where(head_lanes, k, k_aug)
                vx = jnp.where(head_lanes, v, v_const)
                qx_ref[h] = jnp.where(head_lanes, q, q_aug).astype(BF16)
                kx_ref[h] = kx.astype(BF16)
                vx_ref[h] = vx.astype(BF16)
                kxt_ref[h, 0] = kx.T.astype(BF16)
                vt_ref[h, 0] = vx.T.astype(BF16)
                q_sq = lax.dot_general((q * q).astype(BF16), ones_head, _NN, preferred_element_type=F32)
                k_sq = lax.dot_general((k * k).astype(BF16), ones_head, _NN, preferred_element_type=F32)
                qk = jnp.sqrt(q_sq * k_sq) * NORM_MARGIN + f
                vals = (jnp.sqrt(jnp.max(q_sq, axis=0, keepdims=True)), jnp.sqrt(jnp.max(k_sq, axis=0, keepdims=True)),
                        jnp.max(qk, axis=0, keepdims=True), f[tq - 1:tq, :])
                for slot, val in enumerate(vals):
                    acc = jnp.where(lane8 == slot * N_HEADS + h, val[:, AUG:AUG + 1], acc)
        b_ref[0] = acc
        finish()

    blk = lambda j: pl.BlockSpec((tq, ATTN_W), lambda i: (i, j))
    rows = pl.BlockSpec((N_HEADS, tq, LANES), lambda i: (0, i, 0))
    cols_t = pl.BlockSpec((N_HEADS, 1, LANES, tq), lambda i: (0, i, 0, 0))
    shp = jax.ShapeDtypeStruct((N_HEADS, s, LANES), BF16)
    shp_t = jax.ShapeDtypeStruct((N_HEADS, n_q, LANES, tq), BF16)
    outs = pl.pallas_call(
        body, name="attn_prep", grid=(n_q,), in_specs=[blk(0), blk(1), blk(2), blk(0)] + _hbm_specs(2 * n),
        out_specs=[rows, rows, cols_t, rows, cols_t,
                   pl.BlockSpec((1, HALO, LANES), lambda i: (i, 0, 0))] + _hbm_specs(n),
        out_shape=[shp, shp, shp_t, shp, shp_t, jax.ShapeDtypeStruct((n_q, HALO, LANES), F32)]
        + [jax.ShapeDtypeStruct(b.shape, b.dtype) for b in landing],
        input_output_aliases={4 + n + k: 6 + k for k in range(n)}, scratch_shapes=sems,
        compiler_params=_cp(("arbitrary",)))(qkv, qkv, qkv, fb, *arrays, *landing)
    return tuple(outs[:6]) + (outs[6:],)


def _key_block_ranges(bounds):
    t = bounds[:, 0, :]
    nh = N_HEADS
    a, b, c, e = t[:, 0:nh], t[:, nh:2 * nh], t[:, 2 * nh:3 * nh], t[:, 3 * nh:4 * nh]
    bound = a[:, None, :] * b[None, :, :] * NORM_MARGIN + c[:, None, :] - e[None, :, :]
    n_q = t.shape[0]
    idx = jnp.arange(n_q)
    need = jnp.logical_not(bound < -(EXP_ZERO + 2.0)) | (idx[None, :, None] >= idx[:, None, None])
    first = jnp.argmax(need, axis=1).astype(jnp.int32)
    first = jnp.min(first.reshape(n_q, N_PAIRS, 2), axis=-1)
    visits = (first[:, None, :] <= idx[None, :, None]) & (idx[None, :, None] <= idx[:, None, None])
    last = jnp.max(jnp.where(visits, idx[:, None, None], 0), axis=0).astype(jnp.int32)
    return first.T.reshape(-1), last.T.reshape(-1)


def _attn_fwd_t(qx, kx, vt, first_blk):
    _, s, _ = qx.shape
    tq = min(TQ, s)
    n_q = s // tq
    neg = -1e30

    def body(first_ref, qx_ref, kx_ref, vt_ref, o_ref, lse_ref, acc_ref, m_ref, s_even, s_odd):
        p = pl.program_id(0)
        i = pl.program_id(1)
        acc_ref[...] = jnp.zeros(acc_ref.shape, F32)
        m_ref[...] = jnp.full(m_ref.shape, neg, F32)
        key_le_query = (lax.broadcasted_iota(jnp.int32, (tq, tq), 0) <= lax.broadcasted_iota(jnp.int32, (tq, tq), 1))
        first = first_ref[p * n_q + i]

        def scores(kb, hh, dst):
            rows_k = pl.ds(pl.multiple_of(kb * tq, tq), tq)
            dst[hh] = lax.dot_general(kx_ref[hh, rows_k, :], qx_ref[hh], _NT, preferred_element_type=F32)

        def step(kb, src, nxt):
            for hh in range(2):
                st = src[hh]
                if nxt is None:
                    st = jnp.where(key_le_query, st, -jnp.inf)
                else:
                    scores(kb + 1, hh, nxt)
                m_old = m_ref[hh]
                m_new = jnp.maximum(m_old, jnp.max(st, axis=0, keepdims=True))
                m_ref[hh] = m_new
                pt = jnp.exp(st - m_new).astype(BF16)
                acc_ref[hh] = acc_ref[hh] * jnp.exp(m_old - m_new) + lax.dot_general(
                    vt_ref[hh, kb], pt, _NN, preferred_element_type=F32)

        def by_parity(kb, fn):
            @pl.when(kb % 2 == 0)
            def _():
                fn(s_even, s_odd)

            @pl.when(kb % 2 == 1)
            def _():
                fn(s_odd, s_even)

        def first_scores(src, nxt):
            scores(first, 0, src)
            scores(first, 1, src)

        def unmasked(kb, carry):
            by_parity(kb, lambda src, nxt: step(kb, src, nxt))
            return carry

        by_parity(first, first_scores)
        lax.fori_loop(first, i, unmasked, 0)
        by_parity(i, lambda src, nxt: step(i, src, None))
        outs, lses = [], []
        for hh in range(2):
            acc = acc_ref[hh]
            l = acc[AUG:AUG + 1, :]
            outs.append(acc[0:HEAD_DIM, :] / l)
            lses.append(m_ref[hh] + jnp.log(l))
        o_ref[...] = jnp.concatenate(outs, axis=0).T
        rows8 = lax.broadcasted_iota(jnp.int32, (N_HEADS, tq), 0)
        lse_ref[0, 0] = jnp.where(rows8 == 0, lses[0], jnp.where(rows8 == 1, lses[1], 0.0))

    grid_spec = pltpu.PrefetchScalarGridSpec(
        num_scalar_prefetch=1, grid=(N_PAIRS, n_q),
        in_specs=[pl.BlockSpec((2, tq, LANES), lambda p, i, first: (p, i, 0)),
                  pl.BlockSpec((2, s, LANES), lambda p, i, first: (p, 0, 0)),
                  pl.BlockSpec((2, n_q, LANES, tq), lambda p, i, first: (p, 0, 0, 0))],
        out_specs=[pl.BlockSpec((tq, LANES), lambda p, i, first: (i, p)),
                   pl.BlockSpec((1, 1, N_HEADS, tq), lambda p, i, first: (p, i, 0, 0))],
        scratch_shapes=[pltpu.VMEM((2, LANES, tq), F32), pltpu.VMEM((2, 1, tq), F32),
                        pltpu.VMEM((2, tq, tq), F32), pltpu.VMEM((2, tq, tq), F32)])
    return pl.pallas_call(
        body, name="attn_fwd", grid_spec=grid_spec,
        out_shape=[jax.ShapeDtypeStruct((s, ATTN_W), F32), jax.ShapeDtypeStruct((N_PAIRS, n_q, N_HEADS, tq), F32)],
        compiler_params=_cp(("parallel", "arbitrary")))(first_blk, qx, kx, vt)


def _attn_bwd_t(qx, dox, kx, kxt, vx, lse, last_blk):
    _, s, _ = qx.shape
    tq = min(TQ, s)
    n_q = s // tq

    def body(last_ref, qx_ref, dox_ref, lse_ref, kx_ref, kxt_ref, vx_ref, dk_ref, dv_ref, dfk_ref, dqt_ref):
        p = pl.program_id(0)
        j = pl.program_id(1)

        @pl.when(j == 0)
        def _():
            dqt_ref[...] = jnp.zeros(dqt_ref.shape, F32)

        key_le_query = (lax.broadcasted_iota(jnp.int32, (tq, tq), 0) <= lax.broadcasted_iota(jnp.int32, (tq, tq), 1))

        def step(i, carry, masked):
            rows_q = pl.ds(pl.multiple_of(i * tq, tq), tq)
            out = []
            for hh in range(2):
                dk, dv = carry[2 * hh], carry[2 * hh + 1]
                q, do = qx_ref[hh, rows_q, :], dox_ref[hh, rows_q, :]
                st = lax.dot_general(kx_ref[hh], q, _NT, preferred_element_type=F32)
                pt = jnp.exp(st - lse_ref[0, i, hh:hh + 1, :])
                if masked:
                    pt = jnp.where(key_le_query, pt, 0.0)
                dst = pt * lax.dot_general(vx_ref[hh], do, _NT, preferred_element_type=F32)
                pb, dsb = pt.astype(BF16), dst.astype(BF16)
                dv = dv + lax.dot_general(pb, do, _NN, preferred_element_type=F32)
                dk = dk + lax.dot_general(dsb, q, _NN, preferred_element_type=F32)
                dqt_ref[hh, i] += lax.dot_general(kxt_ref[hh, 0], dsb, _NN, preferred_element_type=F32)
                out += [dk, dv]
            return tuple(out)

        zero = jnp.zeros((tq, LANES), F32)
        carry = step(j, (zero, zero, zero, zero), True)
        dk_a, dv_a, dk_b, dv_b = lax.fori_loop(j + 1, last_ref[p * n_q + j] + 1,
                                               lambda i, cr: step(i, cr, False), carry)
        head_a = lax.broadcasted_iota(jnp.int32, (tq, LANES), 1) < HEAD_DIM
        dk_ref[...] = jnp.where(head_a, dk_a, pltpu.roll(dk_b, HEAD_DIM, 1)).astype(BF16)
        dv_ref[...] = jnp.where(head_a, dv_a, pltpu.roll(dv_b, HEAD_DIM, 1)).astype(BF16)
        aux_t = jnp.where(head_a, pltpu.roll(dk_a, HEAD_DIM, 1), dk_b).T
        rows8 = lax.broadcasted_iota(jnp.int32, (N_HEADS, tq), 0)
        dfk_ref[0, 0] = jnp.where(rows8 == 0, aux_t[0:1], jnp.where(rows8 == 1, aux_t[HEAD_DIM:HEAD_DIM + 1], 0.0))

    resident = pl.BlockSpec((2, s, LANES), lambda p, j, last: (p, 0, 0))
    key_rows = pl.BlockSpec((2, tq, LANES), lambda p, j, last: (p, j, 0))
    pair_out = pl.BlockSpec((tq, LANES), lambda p, j, last: (j, p))
    grid_spec = pltpu.PrefetchScalarGridSpec(
        num_scalar_prefetch=1, grid=(N_PAIRS, n_q),
        in_specs=[resident, resident, pl.BlockSpec((1, n_q, N_HEADS, tq), lambda p, j, last: (p, 0, 0, 0)),
                  key_rows, pl.BlockSpec((2, 1, LANES, tq), lambda p, j, last: (p, j, 0, 0)), key_rows],
        out_specs=[pair_out, pair_out, pl.BlockSpec((1, 1, N_HEADS, tq), lambda p, j, last: (p, j, 0, 0)),
                   pl.BlockSpec((2, n_q, LANES, tq), lambda p, j, last: (p, 0, 0, 0))])
    return pl.pallas_call(
        body, name="attn_bwd", grid_spec=grid_spec,
        out_shape=[jax.ShapeDtypeStruct((s, ATTN_W), BF16), jax.ShapeDtypeStruct((s, ATTN_W), BF16),
                   jax.ShapeDtypeStruct((N_PAIRS, n_q, N_HEADS, tq), F32),
                   jax.ShapeDtypeStruct((N_HEADS, n_q, LANES, tq), F32)],
        compiler_params=_cp(("parallel", "arbitrary")))(last_blk, qx, dox, lse, kx, kxt, vx)


def _attn_dq_finish(dqt):
    _, n_q, _, tq = dqt.shape
    per = 4 if n_q % 4 == 0 else 1

    def body(dqt_ref, dq_ref, dfq_ref):
        rows8 = lax.broadcasted_iota(jnp.int32, (N_HEADS, tq), 0)
        for b in range(per):
            a, bb = dqt_ref[0, b], dqt_ref[1, b]
            dq_ref[b * tq:(b + 1) * tq, :] = (
                jnp.concatenate([a[0:HEAD_DIM], bb[0:HEAD_DIM]], axis=0).T * Q_SCALE).astype(BF16)
            dfq_ref[0, b] = jnp.where(rows8 == 0, a[AUG + 3:AUG + 4],
                                      jnp.where(rows8 == 1, bb[AUG + 3:AUG + 4], 0.0))

    return pl.pallas_call(
        body, name="attn_dq_finish", grid=(N_PAIRS, n_q // per),
        in_specs=[pl.BlockSpec((2, per, LANES, tq), lambda p, i: (p, i, 0, 0))],
        out_specs=[pl.BlockSpec((per * tq, LANES), lambda p, i: (i, p)),
                   pl.BlockSpec((1, per, N_HEADS, tq), lambda p, i: (p, i, 0, 0))],
        out_shape=[jax.ShapeDtypeStruct((n_q * tq, ATTN_W), BF16),
                   jax.ShapeDtypeStruct((N_PAIRS, n_q, N_HEADS, tq), F32)],
        compiler_params=_cp(("parallel", "parallel")))(dqt)


def _ffn_act_fwd(up, w_ffn):
    s = up.shape[0]
    tm, tn = min(TM_FFN, s), TN_FFN
    nb = D_FF // tn

    def body(a_ref, g_ref, ap_ref, gp_ref, wa_ref, wg_ref, act_ref):
        i = pl.program_id(1)

        def conv(blk_ref, prev_ref, w_ref):
            prev = jnp.where(i > 0, prev_ref[...], 0.0)
            return _conv_taps(jnp.concatenate([prev, blk_ref[...]], axis=0), w_ref[...])[HALO:]

        u_a, u_g = conv(a_ref, ap_ref, wa_ref), conv(g_ref, gp_ref, wg_ref)
        act_ref[...] = (u_g * jax.nn.sigmoid(u_g) * u_a).astype(BF16)

    blk = lambda off: pl.BlockSpec((tm, tn), lambda n, i: (i, off + n))
    prev = lambda off: pl.BlockSpec((HALO, tn), lambda n, i: (jnp.maximum(i * (tm // HALO) - 1, 0), off + n))
    wsp = lambda off: pl.BlockSpec((3, tn), lambda n, i: (0, off + n))
    return pl.pallas_call(
        body, name="ffn_act_fwd", grid=(nb, s // tm),
        in_specs=[blk(0), blk(nb), prev(0), prev(nb), wsp(0), wsp(nb)],
        out_specs=pl.BlockSpec((tm, tn), lambda n, i: (i, n)),
        out_shape=jax.ShapeDtypeStruct((s, D_FF), BF16),
        compiler_params=_cp(("parallel", "parallel")))(up, up, up, up, w_ffn, w_ffn)


def _ffn_down_loss(up, w_ffn, w_down, x2, target, g_final):
    s, d = x2.shape
    tm, tn = min(TM_ROWS, s), TN_FFN
    nb = D_FF // tn

    def body(a_ref, g_ref, ap_ref, gp_ref, wa_ref, wg_ref, wd_ref, x2_ref, t_ref, gf_ref,
             act_ref, dx_ref, dxb_ref, loss_ref, gg_ref, acc, act_even, act_odd):
        i = pl.program_id(0)
        k = pl.program_id(1)

        @pl.when((i == 0) & (k == 0))
        def _():
            gg_ref[...] = jnp.zeros_like(gg_ref)
            loss_ref[...] = jnp.zeros_like(loss_ref)

        def conv(blk_ref, prev_ref, w_ref):
            prev = jnp.where(i > 0, prev_ref[...], 0.0)
            return _conv_taps(jnp.concatenate([prev, blk_ref[...]], axis=0), w_ref[...])[HALO:]

        def activation(dst):
            u_a, u_g = conv(a_ref, ap_ref, wa_ref), conv(g_ref, gp_ref, wg_ref)
            act = (u_g * (1.0 / (1.0 + jnp.exp(-u_g))) * u_a).astype(BF16)
            act_ref[...] = act
            dst[...] = act

        def project(src, first):
            part = lax.dot_general(src[...], wd_ref[...], _NN, preferred_element_type=F32)
            acc[...] = part if first else acc[...] + part

        @pl.when(k == 0)
        def _():
            activation(act_even)

        @pl.when(k == 1)
        def _():
            project(act_even, True)
            activation(act_odd)

        @pl.when((k > 1) & (k < nb) & (k % 2 == 0))
        def _():
            project(act_odd, False)
            activation(act_even)

        @pl.when((k > 1) & (k < nb) & (k % 2 == 1))
        def _():
            project(act_even, False)
            activation(act_odd)

        @pl.when(k == nb)
        def _():
            project(act_even if nb % 2 == 1 else act_odd, False)
            xv = x2_ref[...] + acc[...]
            r = _rstd(xv)
            xn = xv * r
            gv = gf_ref[...]
            err = xn * gv - t_ref[...]
            loss_ref[...] += 0.5 * jnp.sum(jnp.mean(err * err, axis=-1, keepdims=True), axis=0, keepdims=True)
            dy = err * (1.0 / d)
            gg_ref[...] += jnp.sum(dy * xn, axis=0, keepdims=True)
            t = dy * gv
            dx = r * (t - xn * jnp.mean(t * xn, axis=-1, keepdims=True))
            dx_ref[...] = dx
            dxb_ref[...] = dx.astype(BF16)

    chunk = lambda k: jnp.minimum(k, nb - 1)
    blk = lambda off: pl.BlockSpec((tm, tn), lambda i, k: (i, off + chunk(k)))
    prev = lambda off: pl.BlockSpec((HALO, tn),
                                    lambda i, k: (jnp.maximum(i * (tm // HALO) - 1, 0), off + chunk(k)))
    wsp = lambda off: pl.BlockSpec((3, tn), lambda i, k: (0, off + chunk(k)))
    row = pl.BlockSpec((tm, d), lambda i, k: (i, 0))
    vec = pl.BlockSpec((1, d), lambda i, k: (0, 0))
    return pl.pallas_call(
        body, name="ffn_down_loss", grid=(s // tm, nb + 1),
        in_specs=[blk(0), blk(nb), prev(0), prev(nb), wsp(0), wsp(nb),
                  pl.BlockSpec((tn, d), lambda i, k: (jnp.maximum(k - 1, 0), 0)), row, row, vec],
        out_specs=[pl.BlockSpec((tm, tn), lambda i, k: (i, chunk(k))), row, row,
                   pl.BlockSpec((1, LANES), lambda i, k: (0, 0)), vec],
        out_shape=[jax.ShapeDtypeStruct((s, D_FF), BF16), jax.ShapeDtypeStruct((s, d), F32),
                   jax.ShapeDtypeStruct((s, d), BF16), jax.ShapeDtypeStruct((1, LANES), F32),
                   jax.ShapeDtypeStruct((1, d), F32)],
        scratch_shapes=[pltpu.VMEM((tm, d), F32), pltpu.VMEM((tm, tn), BF16), pltpu.VMEM((tm, tn), BF16)],
        compiler_params=_cp(("arbitrary", "arbitrary")))(up, up, up, up, w_ffn, w_ffn, w_down, x2, target, g_final)


def _ffn_act_bwd(up, dact, w_ffn):
    s = up.shape[0]
    tm, tn = min(TM_FFN, s), TN_FFN
    nb = D_FF // tn
    n_blk = s // tm

    def body(a_ref, g_ref, ap_ref, gp_ref, an_ref, gn_ref, d_ref, dn_ref, wa_ref, wg_ref,
             dup_ref, gwa_ref, gwg_ref):
        i = pl.program_id(1)

        @pl.when(i == 0)
        def _():
            gwa_ref[...] = jnp.zeros_like(gwa_ref)
            gwg_ref[...] = jnp.zeros_like(gwg_ref)

        def ext(blk_ref, prev_ref, next_ref):
            return jnp.concatenate([jnp.where(i > 0, prev_ref[...], 0.0), blk_ref[...], next_ref[...]], axis=0)

        wa, wg = wa_ref[...], wg_ref[...]
        up_a, up_g = ext(a_ref, ap_ref, an_ref), ext(g_ref, gp_ref, gn_ref)
        u_a, u_g = _conv_taps(up_a, wa), _conv_taps(up_g, wg)
        d_e = jnp.concatenate([jnp.zeros((HALO, tn), F32), d_ref[...],
                               jnp.where(i < n_blk - 1, dn_ref[...], 0.0)], axis=0)
        sig = jax.nn.sigmoid(u_g)
        du_a = d_e * (u_g * sig)
        du_g = d_e * u_a * (sig * (1.0 + u_g * (1.0 - sig)))
        blk = slice(HALO, HALO + tm)
        dup_ref[0] = _conv_taps_t(du_a, wa)[blk].astype(BF16)
        dup_ref[1] = _conv_taps_t(du_g, wg)[blk].astype(BF16)
        for gw_ref, upv, du in ((gwa_ref, up_a[blk], du_a), (gwg_ref, up_g[blk], du_g)):
            gw_ref[0:1, :] += jnp.sum(upv * _shift_up(du, 2)[blk], axis=0, keepdims=True)
            gw_ref[1:2, :] += jnp.sum(upv * _shift_up(du, 1)[blk], axis=0, keepdims=True)
            gw_ref[2:3, :] += jnp.sum(upv * du[blk], axis=0, keepdims=True)

    blk = lambda off: pl.BlockSpec((tm, tn), lambda n, i: (i, off + n))
    prev = lambda off: pl.BlockSpec((HALO, tn), lambda n, i: (jnp.maximum(i * (tm // HALO) - 1, 0), off + n))
    nxt = lambda off: pl.BlockSpec(
        (HALO, tn), lambda n, i: (jnp.minimum((i + 1) * (tm // HALO), s // HALO - 1), off + n))
    wsp = lambda off: pl.BlockSpec((3, tn), lambda n, i: (0, off + n))
    return pl.pallas_call(
        body, name="ffn_act_bwd", grid=(nb, n_blk),
        in_specs=[blk(0), blk(nb), prev(0), prev(nb), nxt(0), nxt(nb), blk(0), nxt(0), wsp(0), wsp(nb)],
        out_specs=[pl.BlockSpec((2, tm, tn), lambda n, i: (0, i, n)), wsp(0), wsp(0)],
        out_shape=[jax.ShapeDtypeStruct((2, s, D_FF), BF16),
                   jax.ShapeDtypeStruct((3, D_FF), F32), jax.ShapeDtypeStruct((3, D_FF), F32)],
        compiler_params=_cp(("parallel", "arbitrary")))(up, up, up, up, up, up, dact, dact, w_ffn, w_ffn)


def _adamw(w, g, m, v, name):
    r, c = w.shape
    tr = next((t for t in (512, 352, 256, 128, 64, 32, 16, 8) if r > t and r % t == 0), r)

    def body(w_ref, g_ref, m_ref, v_ref, d_ref, nm_ref, nv_ref):
        gv = g_ref[...]
        m_new = ADAM_B1 * m_ref[...] + (1.0 - ADAM_B1) * gv
        v_new = ADAM_B2 * v_ref[...] + (1.0 - ADAM_B2) * (gv * gv)
        m_hat = m_new / (1.0 - ADAM_B1 ** ADAM_STEP)
        v_hat = v_new / (1.0 - ADAM_B2 ** ADAM_STEP)
        d_ref[...] = -ADAM_LR * (m_hat / (jnp.sqrt(v_hat) + ADAM_EPS) + ADAM_WD * w_ref[...])
        nm_ref[...] = m_new
        nv_ref[...] = v_new

    spec = pl.BlockSpec((tr, c), lambda i: (i, 0))
    shp = jax.ShapeDtypeStruct((r, c), F32)
    return pl.pallas_call(
        body, name=name, grid=(r // tr,), in_specs=[spec] * 4, out_specs=[spec] * 3, out_shape=[shp] * 3,
        compiler_params=_cp(("parallel",)))(w, g, m, v)


def _sum_rows_block(h):
    return h if h <= 352 else 256


def _pair_sum(view, recv, sel, name):
    n, _, h, c = view.shape
    tr = _sum_rows_block(h)

    def body(sel_ref, a_ref, b_ref, o_ref, ob_ref):
        t = a_ref[...] + b_ref[...]
        o_ref[...] = t
        ob_ref[...] = t.astype(BF16)

    blk = pl.BlockSpec((None, tr, c), lambda j, i, sel_ref: (j, i, 0))
    grid_spec = pltpu.PrefetchScalarGridSpec(
        num_scalar_prefetch=1, grid=(n, h // tr),
        in_specs=[pl.BlockSpec((None, None, tr, c), lambda j, i, sel_ref: (j, sel_ref[0], i, 0)),
                  pl.BlockSpec((None, None, tr, c), lambda j, i, sel_ref: (j, 0, i, 0))],
        out_specs=[blk, blk])
    return pl.pallas_call(
        body, name=name, grid_spec=grid_spec,
        out_shape=[jax.ShapeDtypeStruct((n, h, c), F32), jax.ShapeDtypeStruct((n, h, c), BF16)],
        compiler_params=_cp(("parallel", "parallel")))(sel, view, recv)


def _chip_sum(pair, got, sel, name):
    _, h, c = pair.shape
    tr = _sum_rows_block(h)
    nblk = h // tr

    def body(sel_ref, p_ref, g0_ref, g1_ref, g2_ref, o_ref):
        o_ref[...] = ((p_ref[...] + g0_ref[...].astype(F32)) + g1_ref[...].astype(F32)) + g2_ref[...].astype(F32)

    slot = lambda k: pl.BlockSpec((None, tr, c), lambda i, sel_ref: (k, i, 0))
    grid_spec = pltpu.PrefetchScalarGridSpec(
        num_scalar_prefetch=1, grid=(h // tr,),
        in_specs=[pl.BlockSpec((None, tr, c), lambda i, sel_ref: (sel_ref[1], i, 0)), slot(0), slot(1), slot(2)],
        out_specs=pl.BlockSpec((tr, c), lambda i, sel_ref: (sel_ref[0] * nblk + i, 0)))
    return pl.pallas_call(
        body, name=name, grid_spec=grid_spec, out_shape=jax.ShapeDtypeStruct((2 * h, c), F32),
        compiler_params=_cp(("parallel",)))(sel, pair, got, got, got)


def _place():
    return lax.axis_index("x"), lax.axis_index("y"), lax.axis_index("c")


def _other_chips(x, y):
    return [(1 - x, y), (x, 1 - y), (1 - x, 1 - y)]


def _hbm_specs(n):
    return [pl.BlockSpec(memory_space=pl.ANY)] * n


def _all_gather_weights(bigs, smalls):
    nb, ns = len(bigs), len(smalls)
    n = nb + ns

    def body(*refs):
        start, forward, finish = _gather_phases(refs[:n], refs[2 * n:3 * n], nb, *refs[3 * n:])
        start()
        forward()
        finish()

    arrays, landing, sems = _gather_operands(bigs, smalls)
    return pl.pallas_call(
        body, name="all_gather_weights",
        out_shape=[jax.ShapeDtypeStruct(b.shape, b.dtype) for b in landing],
        in_specs=_hbm_specs(2 * n), out_specs=_hbm_specs(n), input_output_aliases={n + k: k for k in range(n)},
        scratch_shapes=sems)(*arrays, *landing)


def _hosted_gather(refs, n, nb, step, total):
    ins, outs, send_sems, recv_sems = refs
    start, forward, finish = _gather_phases(ins, outs, nb, send_sems, recv_sems)
    pl.when(step == 0)(start)
    pl.when(step == (3 * total) // 4)(forward)
    return lambda: pl.when(step == total - 1)(finish)


def _in_proj(h1, w_a, w_b, w_c, bigs, smalls):
    s, d = h1.shape
    tm, tn = min(TM_MM, s), ATTN_W
    na, nq = w_a.shape[1] // tn, w_b.shape[1] // tn
    steps = na + nq + 1
    total = (s // tm) * steps
    nb, n = len(bigs), len(bigs) + len(smalls)
    arrays, landing, sems = _gather_operands(bigs, smalls)

    def body(h_ref, wa_ref, wb_ref, wc_ref, *rest):
        z_ref, qkv_ref, f_ref = rest[2 * n:2 * n + 3]
        m, j = pl.program_id(0), pl.program_id(1)
        finish = _hosted_gather((rest[:n], rest[2 * n + 3:3 * n + 3]) + tuple(rest[3 * n + 3:]), n, nb,
                                m * steps + j, total)
        h = h_ref[...]

        @pl.when(j < na)
        def _():
            z_ref[...] = lax.dot_general(h, wa_ref[...], _NN, preferred_element_type=F32)

        @pl.when((j >= na) & (j < na + nq))
        def _():
            qkv_ref[...] = lax.dot_general(h, wb_ref[...], _NN, preferred_element_type=F32).astype(BF16)

        @pl.when(j == na + nq)
        def _():
            f_ref[...] = lax.dot_general(h, wc_ref[...], _NN, preferred_element_type=F32)

        finish()

    blk_a = lambda m, j: (m, jnp.minimum(j, na - 1))
    blk_b = lambda m, j: (m, jnp.clip(j - na, 0, nq - 1))
    outs = pl.pallas_call(
        body, name="in_proj", grid=(s // tm, steps),
        in_specs=[pl.BlockSpec((tm, d), lambda m, j: (m, 0)),
                  pl.BlockSpec((d, tn), lambda m, j: (0, jnp.minimum(j, na - 1))),
                  pl.BlockSpec((d, tn), lambda m, j: (0, jnp.clip(j - na, 0, nq - 1))),
                  pl.BlockSpec((d, LANES), lambda m, j: (0, 0))] + _hbm_specs(2 * n),
        out_specs=[pl.BlockSpec((tm, tn), blk_a), pl.BlockSpec((tm, tn), blk_b),
                   pl.BlockSpec((tm, LANES), lambda m, j: (m, 0))] + _hbm_specs(n),
        out_shape=[jax.ShapeDtypeStruct((s, w_a.shape[1]), F32), jax.ShapeDtypeStruct((s, w_b.shape[1]), BF16),
                   jax.ShapeDtypeStruct((s, LANES), F32)] + [jax.ShapeDtypeStruct(b.shape, b.dtype) for b in landing],
        input_output_aliases={4 + n + k: 3 + k for k in range(n)}, scratch_shapes=sems,
        compiler_params=_cp(("arbitrary", "arbitrary")))(h1, w_a, w_b, w_c, *arrays, *landing)
    return outs[0], outs[1], outs[2], outs[3:]


def _gather_operands(bigs, smalls):
    x, y, _ = _place()
    arrays = list(bigs) + list(smalls)
    landing = [lax.dynamic_update_index_in_dim(lax.empty((N_CHIPS,) + a.shape, a.dtype), a, 2 * x + y, 0)
               for a in arrays]
    n_sems = 6 * len(bigs) + 3 * len(smalls)
    return arrays, landing, [pltpu.SemaphoreType.DMA((n_sems,)), pltpu.SemaphoreType.DMA((n_sems,))]


def _gather_phases(ins, outs, nb, send_sems, recv_sems):
    n = len(ins)
    x, y, c = _place()
    my_chip = 2 * x + y
    chips = _other_chips(x, y)
    sibling = (x, y, 1 - c)

    def rows(k, which):
        h = ins[k].shape[0] // 2
        return pl.ds(which * h, h)

    def copy(sem, src, dst, to):
        return pltpu.make_async_remote_copy(src_ref=src, dst_ref=dst, send_sem=send_sems.at[sem],
                                            recv_sem=recv_sems.at[sem], device_id=to, device_id_type=MESH)

    def sends():
        out = [copy(6 * k + j, ins[k].at[rows(k, c)], outs[k].at[my_chip, rows(k, c)], (cx, cy, c))
               for k in range(nb) for j, (cx, cy) in enumerate(chips)]
        return out + [copy(6 * nb + 3 * (k - nb) + j, ins[k], outs[k].at[my_chip], (cx, cy, c))
                      for k in range(nb, n) for j, (cx, cy) in enumerate(chips)]

    def landed(k, j, which):
        cx, cy = chips[j]
        return outs[k].at[2 * cx + cy, rows(k, which)]

    def forwards():
        return [copy(6 * k + 3 + j, landed(k, j, c), landed(k, j, c), sibling)
                for j in range(3) for k in range(nb)]

    def start():
        for cp in sends():
            cp.start()

    def forward():
        for j in range(3):
            for k in range(nb):
                copy(6 * k + j, landed(k, j, c), landed(k, j, c), (x, y, c)).wait_recv()
                copy(6 * k + 3 + j, landed(k, j, c), landed(k, j, c), sibling).start()

    def finish():
        for j, (cx, cy) in enumerate(chips):
            for k in range(nb):
                copy(6 * k + 3 + j, landed(k, j, 1 - c), landed(k, j, 1 - c), (x, y, c)).wait_recv()
            for k in range(nb, n):
                arrived = outs[k].at[2 * cx + cy]
                copy(6 * nb + 3 * (k - nb) + j, arrived, arrived, (x, y, c)).wait_recv()
        for cp in sends() + forwards():
            cp.wait_send()

    return start, forward, finish


def _pair_exchange(views, name):
    n = len(views)

    def body(*refs):
        ins, outs, send_sems, recv_sems = refs[:n], refs[n:2 * n], refs[2 * n], refs[2 * n + 1]
        x, y, c = _place()
        copies = [pltpu.make_async_remote_copy(
            src_ref=ins[k].at[:, pl.ds(1 - c, 1)], dst_ref=outs[k], send_sem=send_sems.at[k],
            recv_sem=recv_sems.at[k], device_id=(x, y, 1 - c), device_id_type=MESH) for k in range(n)]
        for cp in copies:
            cp.start()
        for cp in copies:
            cp.wait()

    return pl.pallas_call(
        body, name=name,
        out_shape=[jax.ShapeDtypeStruct((v.shape[0], 1) + v.shape[2:], v.dtype) for v in views],
        in_specs=_hbm_specs(n), out_specs=_hbm_specs(n),
        scratch_shapes=[pltpu.SemaphoreType.DMA((n,)), pltpu.SemaphoreType.DMA((n,))])(*views)


def _scatter_to_chips(parts):
    n = len(parts)

    def body(*refs):
        start, finish = _scatter_phases(refs[:n], refs[n:2 * n], refs[2 * n], refs[2 * n + 1])
        start()
        finish()

    shapes, sems = _scatter_operands(parts)
    return pl.pallas_call(
        body, name="scatter_grads", out_shape=shapes, in_specs=_hbm_specs(n), out_specs=_hbm_specs(n),
        scratch_shapes=sems)(*parts)


def _scatter_operands(parts):
    n = len(parts)
    return ([jax.ShapeDtypeStruct((3,) + p.shape[1:], p.dtype) for p in parts],
            [pltpu.SemaphoreType.DMA((3 * n,)), pltpu.SemaphoreType.DMA((3 * n,))])


def _scatter_phases(ins, outs, send_sems, recv_sems):
    x, y, c = _place()

    def copies():
        return [pltpu.make_async_remote_copy(
            src_ref=ins[k].at[pl.ds(2 * cx + cy, 1)], dst_ref=outs[k].at[pl.ds(r, 1)], send_sem=send_sems.at[3 * k + r],
            recv_sem=recv_sems.at[3 * k + r], device_id=(cx, cy, c), device_id_type=MESH)
            for k in range(len(ins)) for r, (cx, cy) in enumerate(_other_chips(x, y))]

    def start():
        for cp in copies():
            cp.start()

    def finish():
        for cp in copies():
            cp.wait()

    return start, finish


def _hosted_scatter(ins, outs, sems, step, total):
    start, finish = _scatter_phases(ins, outs, *sems)
    pl.when(step == 0)(start)
    return lambda: pl.when(step == total - 1)(finish)


def _join_halves(shards):
    n = len(shards)

    def body(*refs):
        ins, outs, send_sems, recv_sems = refs[:n], refs[n:2 * n], refs[2 * n], refs[2 * n + 1]
        x, y, c = _place()

        def rows(ref, which):
            h = ref.shape[0] // 2
            return ref.at[pl.ds(which * h, h)]

        sent = [pltpu.make_async_remote_copy(
            src_ref=rows(ins[k], c), dst_ref=rows(outs[k], c), send_sem=send_sems.at[k], recv_sem=recv_sems.at[k],
            device_id=(x, y, 1 - c), device_id_type=MESH) for k in range(n)]
        for cp in sent:
            cp.start()
        for k in range(n):
            pltpu.make_async_remote_copy(
                src_ref=rows(ins[k], 1 - c), dst_ref=rows(outs[k], 1 - c), send_sem=send_sems.at[k],
                recv_sem=recv_sems.at[k], device_id=(x, y, 1 - c), device_id_type=MESH).wait_recv()
        for cp in sent:
            cp.wait_send()

    return pl.pallas_call(
        body, name="half_exchange", out_shape=[jax.ShapeDtypeStruct(a.shape, a.dtype) for a in shards],
        in_specs=_hbm_specs(n), out_specs=_hbm_specs(n), input_output_aliases={k: k for k in range(n)},
        scratch_shapes=[pltpu.SemaphoreType.DMA((n,)), pltpu.SemaphoreType.DMA((n,))])(*shards)


def _all_reduce_small(packet):
    rows, width = packet.shape
    n_dev = 8

    def body(x_ref, out_ref, gath, send_sems, recv_sems):
        x, y, c = _place()
        me, sibling = (x, y, c), (x, y, 1 - c)
        chips = _other_chips(x, y)

        def slot(px, py, pc):
            return gath.at[pl.ds((4 * px + 2 * py + pc) * rows, rows), :]

        def copy(k, block, to, src=None):
            return pltpu.make_async_remote_copy(
                src_ref=slot(*block) if src is None else src, dst_ref=slot(*block), send_sem=send_sems.at[k],
                recv_sem=recv_sems.at[k], device_id=to, device_id_type=MESH)

        first = [copy(0, me, sibling, src=x_ref)]
        first += [copy(1 + j, me, (*chip, c), src=x_ref) for j, chip in enumerate(chips)]
        for cp in first:
            cp.start()
        gath[pl.ds((4 * x + 2 * y + c) * rows, rows), :] = x_ref[...]
        passed = [copy(4 + j, (*chip, c), sibling) for j, chip in enumerate(chips)]
        for j, chip in enumerate(chips):
            copy(1 + j, (*chip, c), me).wait_recv()
            passed[j].start()
        copy(0, sibling, me).wait_recv()
        for j, chip in enumerate(chips):
            copy(4 + j, (*chip, 1 - c), me).wait_recv()
        for cp in first + passed:
            cp.wait_send()
        acc = gath[0:rows, :]
        for d in range(1, n_dev):
            acc = acc + gath[d * rows:(d + 1) * rows, :]
        out_ref[...] = acc

    return pl.pallas_call(
        body, name="all_reduce_small", out_shape=jax.ShapeDtypeStruct((rows, width), F32),
        in_specs=[pl.BlockSpec(memory_space=pltpu.VMEM)], out_specs=pl.BlockSpec(memory_space=pltpu.VMEM),
        scratch_shapes=[pltpu.VMEM((n_dev * rows, width), F32), pltpu.SemaphoreType.DMA((7,)),
                        pltpu.SemaphoreType.DMA((7,))])(packet)


def _flat_rows(parts, width, row_multiple):
    flat = jnp.concatenate([p.astype(F32).reshape(-1) for p in parts])
    rows = -(-flat.shape[0] // width)
    rows = -(-rows // row_multiple) * row_multiple
    return jnp.pad(flat, (0, rows * width - flat.shape[0])).reshape(rows, width)


def _unflatten(flat2d, shapes):
    flat = flat2d.reshape(-1)
    out, off = [], 0
    for shp in shapes:
        n = 1
        for dim in shp:
            n *= dim
        out.append(flat[off:off + n].reshape(shp))
        off += n
    return out


def _core_and_chip():
    x, y, c = _place()
    return jnp.stack([c, 2 * x + y]).astype(jnp.int32)


def _pair_sums(chip_major, names, call_name):
    views = [g.reshape(N_CHIPS, 2, g.shape[1] // 2, g.shape[2]) for g in chip_major]
    recv = _pair_exchange(views, call_name)
    sel = _core_and_chip()
    return [_pair_sum(v, r, sel, "pair_sum_" + nm) for v, r, nm in zip(views, recv, names)]


def _finish_grads(pairs, got, names):
    sel = _core_and_chip()
    return _join_halves([_chip_sum(p, g, sel, "chip_sum_" + nm) for (p, _), g, nm in zip(pairs, got, names)])


def kernel(x, g_mix, w_in, b_f, w_conv, g_conv_out, g_attn_out, w_o, g_ffn, w_up, w_ffn_conv, w_down, g_final, loss_target, m_g_mix, m_w_in, m_b_f, m_w_conv, m_g_conv_out, m_g_attn_out, m_w_o, m_g_ffn, m_w_up, m_w_ffn_conv, m_w_down, m_g_final, v_g_mix, v_w_in, v_b_f, v_w_conv, v_g_conv_out, v_g_attn_out, v_w_o, v_g_ffn, v_w_up, v_w_ffn_conv, v_w_down, v_g_final):
    s = x.shape[1]
    x0 = x[0]
    target = loss_target[0]
    d = D_MODEL
    x_pos, y_pos, _ = _place()
    my_chip = 2 * x_pos + y_pos

    (c_in,) = _all_gather_weights([w_in[0].astype(BF16)], [])
    w_in_full = jnp.concatenate([c_in[j] for j in range(N_CHIPS)], axis=1)
    c3 = 3 * CONV_CH
    w_a, w_b = w_in_full[:, :c3], w_in_full[:, c3:c3 + 3 * ATTN_W]
    w_c = jnp.pad(w_in_full[:, c3 + 3 * ATTN_W:], ((0, 0), (0, LANES - N_HEADS)))
    w_q, w_k, w_v = (w_b[:, i * ATTN_W:(i + 1) * ATTN_W] for i in range(3))
    b_pad = jnp.pad(b_f, ((0, 0), (0, LANES - N_HEADS)))

    h1 = _rms_fwd(x0, g_mix, "rms_mix")
    z_a, qkv, f_log, (c_o, c_up, c_conv, c_ffn) = _in_proj(
        h1, w_a, w_b, w_c, [w_o[0].astype(BF16), w_up[0].astype(BF16)], [w_conv[0], w_ffn_conv[0]])
    fb = _gate_fwd(f_log, b_pad)
    qx, kx, kxt, vx, vt, bounds, (c_down,) = _attn_prep(qkv, fb, [w_down[0].astype(BF16)])
    w_o_full = c_o.reshape(d, d)
    w_down_full = c_down.reshape(D_FF, d)
    w_conv_full = jnp.concatenate([c_conv[j] for j in range(N_CHIPS)], axis=1)
    w_ffn_full = jnp.concatenate([c_ffn[j] for j in range(N_CHIPS)], axis=1)
    n_up = c_up.shape[2]
    first_blk, last_blk = _key_block_ranges(bounds)
    o_attn, lse = _attn_fwd_t(qx, kx, vt, first_blk)
    mix = _mixer_fwd(z_a, o_attn, w_conv_full, g_conv_out, g_attn_out)
    x2 = _mm("nn", [mix], [w_o_full], F32, TM_MM, 512, "out_proj", add=x0)
    h2 = _rms_fwd(x2, g_ffn, "rms_ffn")
    up = _mm("nn", [h2], [c_up], F32, TM_MM, n_up, "up_proj", b_chips=True)
    act = _ffn_act_fwd(up, w_ffn_full)
    x3 = _mm("nn", [act], [w_down_full], F32, TM_MM, 512, "down_proj", add=x2)
    dx3, dx3_b, loss_row, gg_final = _loss_head(x3, target, g_final.reshape(1, d))

    dact = _mm("nt", [dx3_b], [w_down_full], F32, TM_MM, 1408, "d_act")
    gw_down = _mm_tn(act, dx3_b, 1408, 1024, "gw_down")
    dup, gwf_lin, gwf_gate = _ffn_act_bwd(up, dact, w_ffn_full)
    dh2 = _mm("nt", [(dup, j // 2, j % 2, n_up) for j in range(N_CHIPS)], [(c_up, j) for j in range(N_CHIPS)],
              F32, TM_MM, 512, "d_h2")
    gw_up = _mm_tn(h2, dup, 1024, n_up, "gw_up", out_chips=True)
    dx2, dx2_b, gg_ffn = _rms_bwd(x2, dh2, g_ffn, dx3, "rms_ffn_bwd", True)
    dmix = _mm("nt", [dx2_b], [w_o_full], F32, TM_MM, 512, "d_mix")
    gw_o = _mm_tn(mix, dx2_b, 1024, 1024, "gw_o")
    early = _pair_sums([gw_o.reshape(N_CHIPS, d // N_CHIPS, d), gw_up, gw_down.reshape(N_CHIPS, D_FF // N_CHIPS, d)],
                       ["w_o", "w_up", "w_down"], "pair_exchange")
    dz_a, dox, gw_conv, gg_conv_out, gg_attn_out, (got_o, got_down) = _mixer_bwd(
        z_a, o_attn, dmix, w_conv_full, g_conv_out, g_attn_out, [early[0][1], early[2][1]])
    dk, dv, dfk, dqt = _attn_bwd_t(qx, dox, kx, kxt, vx, lse, last_blk)
    dq, dfq = _attn_dq_finish(dqt)
    d_f = jnp.transpose((dfk + dfq)[:, :, 0:2, :], (1, 3, 0, 2)).reshape(s, N_HEADS)
    df_b, gb_f = _gate_bwd(f_log, b_pad, jnp.pad(d_f, ((0, 0), (0, LANES - N_HEADS))))
    dh1, (got_up,) = _mm("nt", [dz_a, dq, dk, dv, df_b], [w_a, w_q, w_k, w_v, w_c], F32, TM_MM, 512, "d_h1",
                         scatter=[early[1][1]])
    gw_a = _mm_tn(h1, dz_a, 1024, c3, "gw_in_conv")
    gw_q = _mm_tn(h1, dq, 1024, ATTN_W, "gw_in_q")
    gw_k = _mm_tn(h1, dk, 1024, ATTN_W, "gw_in_k")
    gw_v = _mm_tn(h1, dv, 1024, ATTN_W, "gw_in_v")
    gw_c = _mm_tn(h1, df_b, 1024, LANES, "gw_in_gate")
    gw_in = jnp.concatenate([gw_a, gw_q, gw_k, gw_v, gw_c[:, :N_HEADS]], axis=1)
    n_in = IN_COLS // N_CHIPS
    gw_in = jnp.stack([gw_in[:, j * n_in:(j + 1) * n_in] for j in range(N_CHIPS)])
    late = _pair_sums([gw_in], ["w_in"], "pair_exchange_w_in")
    grad_x, gg_mix, (got_in,) = _rms_bwd(x0, dh1, g_mix, dx2, "rms_mix_bwd", False, scatter=[late[0][1]])
    g_w_in, g_w_o, g_w_up, g_w_down = _finish_grads(late + early, [got_in, got_o, got_up, got_down],
                                                    ["w_in", "w_o", "w_up", "w_down"])

    gw_ffn = jnp.concatenate([gwf_lin, gwf_gate], axis=1)
    small_parts = [gg_mix, gg_conv_out, gg_attn_out, gg_ffn, gg_final, gb_f[:, :N_HEADS], loss_row[:, 0:1], gw_conv,
                   gw_ffn]
    small_shapes = [a.shape for a in small_parts]
    tot = _unflatten(_all_reduce_small(_flat_rows(small_parts, d, 8)), small_shapes)
    g_g_mix, g_g_conv_out, g_g_attn_out, g_g_ffn, g_g_final, g_b_f, loss_sum, g_conv_full, g_ffn_full = tot
    loss = loss_sum[0, 0]
    g_g_final = g_g_final[0]
    g_w_conv = lax.dynamic_slice_in_dim(g_conv_full, my_chip * (CONV_CH // N_CHIPS), CONV_CH // N_CHIPS, axis=1)
    g_w_ffn = lax.dynamic_slice_in_dim(g_ffn_full, my_chip * n_up, n_up, axis=1)

    def adam_big(w, g, m, v, name):
        dl, nm, nv = _adamw(w[0], g, m[0], v[0], name)
        return dl[None], nm[None], nv[None]

    u_w_in = adam_big(w_in, g_w_in, m_w_in, v_w_in, "adam_w_in")
    u_w_o = adam_big(w_o, g_w_o, m_w_o, v_w_o, "adam_w_o")
    u_w_up = adam_big(w_up, g_w_up, m_w_up, v_w_up, "adam_w_up")
    u_w_down = adam_big(w_down, g_w_down, m_w_down, v_w_down, "adam_w_down")

    small_w = [g_mix, b_f, g_conv_out, g_attn_out, g_ffn, g_final, w_conv, w_ffn_conv]
    small_g = [g_g_mix, g_b_f, g_g_conv_out, g_g_attn_out, g_g_ffn, g_g_final, g_w_conv, g_w_ffn]
    small_m = [m_g_mix, m_b_f, m_g_conv_out, m_g_attn_out, m_g_ffn, m_g_final, m_w_conv, m_w_ffn_conv]
    small_v = [v_g_mix, v_b_f, v_g_conv_out, v_g_attn_out, v_g_ffn, v_g_final, v_w_conv, v_w_ffn_conv]
    shapes = [a.shape for a in small_w]
    pack = lambda arrs: _flat_rows(arrs, LANES, 8)
    sd, sm, sv = _adamw(pack(small_w), pack(small_g), pack(small_m), pack(small_v), "adam_small")
    sd, sm, sv = _unflatten(sd, shapes), _unflatten(sm, shapes), _unflatten(sv, shapes)
    (d_g_mix, d_b_f, d_g_conv_out, d_g_attn_out, d_g_ffn, d_g_final, d_w_conv, d_w_ffn) = sd
    (nm_g_mix, nm_b_f, nm_g_conv_out, nm_g_attn_out, nm_g_ffn, nm_g_final, nm_w_conv, nm_w_ffn) = sm
    (nv_g_mix, nv_b_f, nv_g_conv_out, nv_g_attn_out, nv_g_ffn, nv_g_final, nv_w_conv, nv_w_ffn) = sv

    grads = (g_g_mix, g_w_in[None], g_b_f, g_w_conv[None], g_g_conv_out, g_g_attn_out, g_w_o[None], g_g_ffn,
             g_w_up[None], g_w_ffn[None], g_w_down[None], g_g_final)
    deltas = (d_g_mix, u_w_in[0], d_b_f, d_w_conv, d_g_conv_out, d_g_attn_out, u_w_o[0], d_g_ffn, u_w_up[0],
              d_w_ffn, u_w_down[0], d_g_final)
    new_m = (nm_g_mix, u_w_in[1], nm_b_f, nm_w_conv, nm_g_conv_out, nm_g_attn_out, u_w_o[1], nm_g_ffn, u_w_up[1],
             nm_w_ffn, u_w_down[1], nm_g_final)
    new_v = (nv_g_mix, u_w_in[2], nv_b_f, nv_w_conv, nv_g_conv_out, nv_g_attn_out, u_w_o[2], nv_g_ffn, u_w_up[2],
             nv_w_ffn, u_w_down[2], nv_g_final)
    return (loss, grad_x[None], *grads, *deltas, *new_m, *new_v)
```

```python
import functools

import jax
import jax.numpy as jnp
from jax import lax
from jax.experimental import pallas as pl
from jax.experimental.pallas import tpu as pltpu

F32, BF16 = jnp.float32, jnp.bfloat16
MESH = pl.DeviceIdType.MESH

D_MODEL = 1024
CONV_CH = 512
ATTN_W = 512
N_HEADS = 8
HEAD_DIM = 64
N_PAIRS = N_HEADS // 2
D_FF = 2816
IN_COLS = 3 * CONV_CH + 3 * ATTN_W + N_HEADS
EPS = 1e-6
Q_SCALE = 0.125
EXP_ZERO = 104.0
N_CHIPS = 4
LANES = 128
HALO = 8

ADAM_LR, ADAM_B1, ADAM_B2, ADAM_EPS, ADAM_WD, ADAM_STEP = 0.001, 0.9, 0.999, 1e-08, 0.01, 10

TM_ROWS = 512
TM_MM = 1024
TK_TN = 1024
TQ = 512
ROW_CHUNK = 32
TM_FFN = 1024
TN_FFN = 256
VMEM_LIMIT = 52 * 2**20


def _cp(sem, vmem=VMEM_LIMIT):
    return pltpu.CompilerParams(dimension_semantics=sem, vmem_limit_bytes=vmem)


def _bf(a):
    return a if a.dtype == BF16 else a.astype(BF16)


def _mm(mode, a_list, b_list, out_dtype, tm, tn, name, add=None, b_chips=False, scatter=()):
    n_p = len(a_list)
    a0 = a_list[0]
    m_dim = a0[0].shape[1] if isinstance(a0, tuple) else a0.shape[0]
    b0 = b_list[0]
    if b_chips:
        n_dim = b0.shape[0] * b0.shape[2]
        assert tn == b0.shape[2] and mode == "nn"
    else:
        b0 = b0[0][b0[1]] if isinstance(b0, tuple) else b0
        n_dim = b0.shape[1 if mode == "nn" else 0]
    tm, tn = min(tm, m_dim), min(tn, n_dim)
    assert m_dim % tm == 0 and n_dim % tn == 0
    dims = (((1,), (0,)), ((), ())) if mode == "nn" else (((1,), (1,)), ((), ()))
    in_specs, args = [], []
    for a in a_list:
        if isinstance(a, tuple):
            arr, lead, col, width = a
            in_specs.append(pl.BlockSpec((None, tm, width), lambda m, n, lead=lead, col=col: (lead, m, col)))
        else:
            arr = a
            in_specs.append(pl.BlockSpec((tm, a.shape[1]), lambda m, n: (m, 0)))
        args.append(arr)
    for b in b_list:
        if b_chips:
            arr = b
            in_specs.append(pl.BlockSpec((None, b.shape[1], tn), lambda m, n: (n, 0, 0)))
        elif isinstance(b, tuple):
            arr, lead = b
            if mode == "nn":
                in_specs.append(pl.BlockSpec((None, arr.shape[1], tn), lambda m, n, lead=lead: (lead, 0, n)))
            else:
                in_specs.append(pl.BlockSpec((None, tn, arr.shape[2]), lambda m, n, lead=lead: (lead, n, 0)))
        elif mode == "nn":
            arr = b
            in_specs.append(pl.BlockSpec((b.shape[0], tn), lambda m, n: (0, n)))
        else:
            arr = b
            in_specs.append(pl.BlockSpec((tn, b.shape[1]), lambda m, n: (n, 0)))
        args.append(arr)
    if add is not None:
        in_specs.append(pl.BlockSpec((tm, tn), lambda m, n: (m, n)))
        args.append(add)

    n_in = len(args)
    n_sc = len(scatter)
    grid = (m_dim // tm, n_dim // tn)

    def body(*refs):
        o_ref = refs[n_in + n_sc]
        if n_sc:
            finish = _hosted_scatter(refs[n_in:n_in + n_sc], refs[n_in + n_sc + 1:n_in + 2 * n_sc + 1],
                                     refs[n_in + 2 * n_sc + 1:], pl.program_id(0) * grid[1] + pl.program_id(1),
                                     grid[0] * grid[1])
        acc = None
        for i in range(n_p):
            d = lax.dot_general(_bf(refs[i][...]), _bf(refs[n_p + i][...]), dims,
                                preferred_element_type=F32)
            acc = d if acc is None else acc + d
        if add is not None:
            acc = refs[2 * n_p][...] + acc
        o_ref[...] = acc.astype(out_dtype)
        if n_sc:
            finish()

    main_spec = pl.BlockSpec((tm, tn), lambda m, n: (m, n))
    main_shape = jax.ShapeDtypeStruct((m_dim, n_dim), out_dtype)
    if not n_sc:
        return pl.pallas_call(body, name=name, grid=grid, in_specs=in_specs, out_specs=main_spec,
                              out_shape=main_shape, compiler_params=_cp(("parallel", "parallel")))(*args)
    got_shapes, sems = _scatter_operands(scatter)
    outs = pl.pallas_call(
        body, name=name, grid=grid, in_specs=in_specs + _hbm_specs(n_sc), out_specs=[main_spec] + _hbm_specs(n_sc),
        out_shape=[main_shape] + got_shapes, scratch_shapes=sems,
        compiler_params=_cp(("arbitrary", "arbitrary")))(*args, *scatter)
    return outs[0], outs[1:]


def _mm_tn(a, b, tm, tn, name, out_chips=False):
    k_dim, m_dim = a.shape
    n_dim = b.shape[-1] * (b.shape[0] if b.ndim == 3 else 1)
    tm, tn, tk = min(tm, m_dim), min(tn, b.shape[-1]), min(TK_TN, k_dim)
    assert m_dim % tm == 0 and b.shape[-1] % tn == 0 and k_dim % tk == 0
    per = b.shape[-1] // tn
    if b.ndim == 3:
        b_spec = pl.BlockSpec((None, tk, tn), lambda m, n, k: (n // per, k, n % per))
    else:
        b_spec = pl.BlockSpec((tk, tn), lambda m, n, k: (k, n))

    def body(a_ref, b_ref, o_ref):
        @pl.when(pl.program_id(2) == 0)
        def _():
            o_ref[...] = jnp.zeros_like(o_ref)
        o_ref[...] += lax.dot_general(_bf(a_ref[...]), _bf(b_ref[...]), (((0,), (0,)), ((), ())),
                                      preferred_element_type=F32)

    return pl.pallas_call(
        body, name=name, grid=(m_dim // tm, n_dim // tn, k_dim // tk),
        in_specs=[pl.BlockSpec((tk, tm), lambda m, n, k: (k, m)), b_spec],
        out_specs=(pl.BlockSpec((None, tm, tn), lambda m, n, k: (n, m, 0)) if out_chips
                   else pl.BlockSpec((tm, tn), lambda m, n, k: (m, n))),
        out_shape=jax.ShapeDtypeStruct((n_dim // tn, m_dim, tn) if out_chips else (m_dim, n_dim), F32),
        compiler_params=_cp(("parallel", "parallel", "arbitrary")))(a, b)


def _rstd(x):
    return lax.rsqrt(jnp.mean(x * x, axis=-1, keepdims=True) + EPS)


def _rms_fwd(x, g, name):
    s, d = x.shape
    tm = min(TM_ROWS, s)

    def body(x_ref, g_ref, h_ref):
        xv = x_ref[...]
        h_ref[...] = (xv * _rstd(xv) * g_ref[...]).astype(BF16)

    return pl.pallas_call(
        body, name=name, grid=(s // tm,),
        in_specs=[pl.BlockSpec((tm, d), lambda i: (i, 0)), pl.BlockSpec((1, d), lambda i: (0, 0))],
        out_specs=pl.BlockSpec((tm, d), lambda i: (i, 0)),
        out_shape=jax.ShapeDtypeStruct((s, d), BF16), compiler_params=_cp(("parallel",)))(x, g)


def _rms_bwd(x, dh, g, dres, name, with_bf16, scatter=()):
    s, d = x.shape
    tm = min(TM_ROWS, s)
    n_sc = len(scatter)
    n_out = 3 if with_bf16 else 2
    got_shapes, sems = _scatter_operands(scatter) if n_sc else ([], [])

    def body(x_ref, dh_ref, g_ref, dres_ref, *rest):
        dx_ref, gg_ref = rest[n_sc], rest[n_sc + n_out - 1]
        i = pl.program_id(0)
        if n_sc:
            finish = _hosted_scatter(rest[:n_sc], rest[n_sc + n_out:2 * n_sc + n_out], rest[2 * n_sc + n_out:], i,
                                     s // tm)

        @pl.when(i == 0)
        def _():
            gg_ref[...] = jnp.zeros_like(gg_ref)

        xv = x_ref[...]
        xn = xv * _rstd(xv)
        dhv = dh_ref[...]
        gg_ref[...] += jnp.sum(dhv * xn, axis=0, keepdims=True)
        t = dhv * g_ref[...]
        dx = dres_ref[...] + _rstd(xv) * (t - xn * jnp.mean(t * xn, axis=-1, keepdims=True))
        dx_ref[...] = dx
        if with_bf16:
            rest[n_sc + 1][...] = dx.astype(BF16)
        if n_sc:
            finish()

    row = pl.BlockSpec((tm, d), lambda i: (i, 0))
    vec = pl.BlockSpec((1, d), lambda i: (0, 0))
    out_specs = [row] + ([row] if with_bf16 else []) + [vec] + _hbm_specs(n_sc)
    out_shape = ([jax.ShapeDtypeStruct((s, d), F32)] + ([jax.ShapeDtypeStruct((s, d), BF16)] if with_bf16 else [])
                 + [jax.ShapeDtypeStruct((1, d), F32)] + got_shapes)
    outs = pl.pallas_call(
        body, name=name, grid=(s // tm,), in_specs=[row, row, vec, row] + _hbm_specs(n_sc), out_specs=out_specs,
        out_shape=out_shape, scratch_shapes=sems, compiler_params=_cp(("arbitrary",)))(x, dh, g, dres, *scatter)
    return tuple(outs[:n_out]) + ((outs[n_out:],) if n_sc else ())


def _loss_head(x3, target, g):
    s, d = x3.shape
    tm = min(TM_ROWS, s)

    def body(x_ref, t_ref, g_ref, dx_ref, dxb_ref, loss_ref, gg_ref):
        @pl.when(pl.program_id(0) == 0)
        def _():
            gg_ref[...] = jnp.zeros_like(gg_ref)
            loss_ref[...] = jnp.zeros_like(loss_ref)

        xv = x_ref[...]
        r = _rstd(xv)
        xn = xv * r
        gv = g_ref[...]
        err = xn * gv - t_ref[...]
        loss_ref[...] += 0.5 * jnp.sum(jnp.mean(err * err, axis=-1, keepdims=True), axis=0, keepdims=True)
        dy = err * (1.0 / d)
        gg_ref[...] += jnp.sum(dy * xn, axis=0, keepdims=True)
        t = dy * gv
        dx = r * (t - xn * jnp.mean(t * xn, axis=-1, keepdims=True))
        dx_ref[...] = dx
        dxb_ref[...] = dx.astype(BF16)

    row = pl.BlockSpec((tm, d), lambda i: (i, 0))
    vec = pl.BlockSpec((1, d), lambda i: (0, 0))
    return pl.pallas_call(
        body, name="loss_head", grid=(s // tm,), in_specs=[row, row, vec],
        out_specs=[row, row, pl.BlockSpec((1, LANES), lambda i: (0, 0)), vec],
        out_shape=[jax.ShapeDtypeStruct((s, d), F32), jax.ShapeDtypeStruct((s, d), BF16),
                   jax.ShapeDtypeStruct((1, LANES), F32), jax.ShapeDtypeStruct((1, d), F32)],
        compiler_params=_cp(("arbitrary",)))(x3, target, g)


def _prev_halo_spec(tm, width, col):
    return pl.BlockSpec((HALO, width), lambda i, *_: (jnp.maximum(i * (tm // HALO) - 1, 0), col))


def _next_halo_spec(tm, width, col, s):
    return pl.BlockSpec((HALO, width), lambda i, *_: (jnp.minimum((i + 1) * (tm // HALO), s // HALO - 1), col))


def _shift_down(x, k):
    return pltpu.roll(x, k, 0)


def _shift_up(x, k):
    return pltpu.roll(x, x.shape[0] - k, 0)


def _conv_taps(x_ext, w):
    return w[0:1, :] * _shift_down(x_ext, 2) + w[1:2, :] * _shift_down(x_ext, 1) + w[2:3, :] * x_ext


def _conv_taps_t(d_ext, w):
    return w[2:3, :] * d_ext + w[1:2, :] * _shift_up(d_ext, 1) + w[0:1, :] * _shift_up(d_ext, 2)


def _mixer_fwd(z_a, o_attn, w_conv, g_conv_out, g_attn_out):
    s = z_a.shape[0]
    c = CONV_CH
    tm = min(TM_ROWS, s)

    def body(gb_ref, gc_ref, xc_ref, gcp_ref, xcp_ref, o_ref, w_ref, gco_ref, gao_ref, mix_ref):
        i = pl.program_id(0)
        cx = gc_ref[...] * xc_ref[...]
        cx_prev = jnp.where(i > 0, gcp_ref[...] * xcp_ref[...], 0.0)
        conv = _conv_taps(jnp.concatenate([cx_prev, cx], axis=0), w_ref[...])[HALO:]
        y = gb_ref[...] * conv
        mix_ref[:, 0:c] = (y * _rstd(y) * gco_ref[...]).astype(BF16)
        o = o_ref[...]
        mix_ref[:, c:2 * c] = (o * _rstd(o) * gao_ref[...]).astype(BF16)

    col = lambda j: pl.BlockSpec((tm, c), lambda i: (i, j))
    vec = pl.BlockSpec((1, c), lambda i: (0, 0))
    return pl.pallas_call(
        body, name="mixer_fwd", grid=(s // tm,),
        in_specs=[col(0), col(1), col(2), _prev_halo_spec(tm, c, 1), _prev_halo_spec(tm, c, 2), col(0),
                  pl.BlockSpec((3, c), lambda i: (0, 0)), vec, vec],
        out_specs=pl.BlockSpec((tm, 2 * c), lambda i: (i, 0)),
        out_shape=jax.ShapeDtypeStruct((s, 2 * c), BF16),
        compiler_params=_cp(("parallel",)))(z_a, z_a, z_a, z_a, z_a, o_attn, w_conv, g_conv_out, g_attn_out)


def _mixer_bwd(z_a, o_attn, dmix, w_conv, g_conv_out, g_attn_out, scatter):
    s = z_a.shape[0]
    c = CONV_CH
    tm = min(TM_ROWS, s)
    n_blk = s // tm
    n_sc = len(scatter)
    got_shapes, sems = _scatter_operands(scatter)

    def body(gb_ref, gc_ref, xc_ref, gcp_ref, xcp_ref, gbn_ref, gcn_ref, xcn_ref, o_ref, dnc_ref, dncn_ref, dna_ref,
             w_ref, gco_ref, gao_ref, *rest):
        dz_ref, dox_ref, gw_ref, ggco_ref, ggao_ref = rest[n_sc:n_sc + 5]
        i = pl.program_id(0)
        finish = _hosted_scatter(rest[:n_sc], rest[n_sc + 5:2 * n_sc + 5], rest[2 * n_sc + 5:], i, n_blk)

        @pl.when(i == 0)
        def _():
            gw_ref[...] = jnp.zeros_like(gw_ref)
            ggco_ref[...] = jnp.zeros_like(ggco_ref)
            ggao_ref[...] = jnp.zeros_like(ggao_ref)

        w = w_ref[...]
        zeros = jnp.zeros((HALO, c), F32)
        gb_e = jnp.concatenate([zeros, gb_ref[...], gbn_ref[...]], axis=0)
        cx_prev = jnp.where(i > 0, gcp_ref[...] * xcp_ref[...], 0.0)
        gc_e = jnp.concatenate([zeros, gc_ref[...], gcn_ref[...]], axis=0)
        xc_e = jnp.concatenate([zeros, xc_ref[...], xcn_ref[...]], axis=0)
        cx_e = jnp.concatenate([cx_prev, gc_ref[...] * xc_ref[...], gcn_ref[...] * xcn_ref[...]], axis=0)
        dn_next = jnp.where(i < n_blk - 1, dncn_ref[...], 0.0)
        dn_e = jnp.concatenate([zeros, dnc_ref[...], dn_next], axis=0)

        conv_e = _conv_taps(cx_e, w)
        y_e = gb_e * conv_e
        r_e = _rstd(y_e)
        yn_e = y_e * r_e
        t_e = dn_e * gco_ref[...]
        dy_e = r_e * (t_e - yn_e * jnp.mean(t_e * yn_e, axis=-1, keepdims=True))
        dconv_e = dy_e * gb_e
        dcx_e = _conv_taps_t(dconv_e, w)
        blk = slice(HALO, HALO + tm)
        dz_ref[:, 0:c] = (dy_e * conv_e)[blk].astype(BF16)
        dz_ref[:, c:2 * c] = (dcx_e * xc_e)[blk].astype(BF16)
        dz_ref[:, 2 * c:3 * c] = (dcx_e * gc_e)[blk].astype(BF16)
        ggco_ref[...] += jnp.sum((dn_e * yn_e)[blk], axis=0, keepdims=True)
        dconv = dconv_e[blk]
        gw_ref[0:1, :] += jnp.sum(dconv * _shift_down(cx_e, 2)[blk], axis=0, keepdims=True)
        gw_ref[1:2, :] += jnp.sum(dconv * _shift_down(cx_e, 1)[blk], axis=0, keepdims=True)
        gw_ref[2:3, :] += jnp.sum(dconv * cx_e[blk], axis=0, keepdims=True)

        o = o_ref[...]
        ra = _rstd(o)
        on = o * ra
        dna = dna_ref[...]
        ggao_ref[...] += jnp.sum(dna * on, axis=0, keepdims=True)
        ta = dna * gao_ref[...]
        do = ra * (ta - on * jnp.mean(ta * on, axis=-1, keepdims=True))
        prod = do * o
        lane = lax.broadcasted_iota(jnp.int32, (tm, LANES), 1)
        head_a = lane < HEAD_DIM
        for p in range(N_PAIRS):
            cols = slice(p * LANES, (p + 1) * LANES)
            pb, dob = prod[:, cols], do[:, cols]
            for hh in range(2):
                sel = head_a if hh == 0 else jnp.logical_not(head_a)
                delta = jnp.sum(jnp.where(sel, pb, 0.0), axis=-1, keepdims=True)
                neg3 = _split3(-delta)
                do_h = pltpu.roll(dob, HEAD_DIM, 1) if hh else dob
                dox_ref[2 * p + hh] = _aug(do_h, lane, neg3).astype(BF16)
        finish()

    col = lambda j: pl.BlockSpec((tm, c), lambda i: (i, j))
    vec = pl.BlockSpec((1, c), lambda i: (0, 0))
    w3 = pl.BlockSpec((3, c), lambda i: (0, 0))
    outs = pl.pallas_call(
        body, name="mixer_bwd", grid=(n_blk,),
        in_specs=[col(0), col(1), col(2), _prev_halo_spec(tm, c, 1), _prev_halo_spec(tm, c, 2),
                  _next_halo_spec(tm, c, 0, s), _next_halo_spec(tm, c, 1, s), _next_halo_spec(tm, c, 2, s),
                  col(0), col(0), _next_halo_spec(tm, c, 0, s), col(1), w3, vec, vec] + _hbm_specs(n_sc),
        out_specs=[pl.BlockSpec((tm, 3 * c), lambda i: (i, 0)),
                   pl.BlockSpec((N_HEADS, tm, LANES), lambda i: (0, i, 0)), w3, vec, vec] + _hbm_specs(n_sc),
        out_shape=[jax.ShapeDtypeStruct((s, 3 * c), BF16), jax.ShapeDtypeStruct((N_HEADS, s, LANES), BF16),
                   jax.ShapeDtypeStruct((3, c), F32), jax.ShapeDtypeStruct((1, c), F32),
                   jax.ShapeDtypeStruct((1, c), F32)] + got_shapes,
        scratch_shapes=sems, compiler_params=_cp(("arbitrary",)))(
            z_a, z_a, z_a, z_a, z_a, z_a, z_a, z_a, o_attn, dmix, dmix, dmix, w_conv, g_conv_out, g_attn_out,
            *scatter)
    return tuple(outs[:5]) + (outs[5:],)


def _gate_fwd(f, b_pad):
    s = f.shape[0]
    tm = min(TQ, s)

    def body(f_ref, b_ref, fb_ref, carry):
        @pl.when(pl.program_id(0) == 0)
        def _():
            carry[...] = jnp.zeros_like(carry)

        z = f_ref[...] + b_ref[...]
        x = jnp.minimum(z, 0.0) - jnp.log1p(jnp.exp(-jnp.abs(z)))
        row = lax.broadcasted_iota(jnp.int32, (tm, LANES), 0)
        sh = 1
        while sh < tm:
            x = x + jnp.where(row >= sh, _shift_down(x, sh), 0.0)
            sh *= 2
        x = x + carry[0:1, :]
        carry[...] = jnp.broadcast_to(x[tm - 1:tm, :], carry.shape)
        head_a = lax.broadcasted_iota(jnp.int32, (tm, LANES), 1) < HEAD_DIM
        for p in range(N_PAIRS):
            fa = jnp.broadcast_to(x[:, 2 * p:2 * p + 1], (tm, LANES))
            fbv = jnp.broadcast_to(x[:, 2 * p + 1:2 * p + 2], (tm, LANES))
            fb_ref[:, p * LANES:(p + 1) * LANES] = jnp.where(head_a, fa, fbv)

    return pl.pallas_call(
        body, name="gate_fwd", grid=(s // tm,),
        in_specs=[pl.BlockSpec((tm, LANES), lambda i: (i, 0)), pl.BlockSpec((1, LANES), lambda i: (0, 0))],
        out_specs=pl.BlockSpec((tm, N_PAIRS * LANES), lambda i: (i, 0)),
        out_shape=jax.ShapeDtypeStruct((s, N_PAIRS * LANES), F32),
        scratch_shapes=[pltpu.VMEM((HALO, LANES), F32)],
        compiler_params=_cp(("arbitrary",)))(f, b_pad)


def _gate_bwd(f, b_pad, d_f):
    s = f.shape[0]
    tm = min(TQ, s)
    n_blk = s // tm

    def body(f_ref, b_ref, d_ref, df_ref, gb_ref, carry):
        @pl.when(pl.program_id(0) == 0)
        def _():
            carry[...] = jnp.zeros_like(carry)
            gb_ref[...] = jnp.zeros_like(gb_ref)

        x = d_ref[...]
        row = lax.broadcasted_iota(jnp.int32, (tm, LANES), 0)
        sh = 1
        while sh < tm:
            x = x + jnp.where(row < tm - sh, _shift_up(x, sh), 0.0)
            sh *= 2
        x = x + carry[0:1, :]
        carry[...] = jnp.broadcast_to(x[0:1, :], carry.shape)
        z = f_ref[...] + b_ref[...]
        d = x * (1.0 / (1.0 + jnp.exp(z)))
        df_ref[...] = d.astype(BF16)
        gb_ref[...] += jnp.sum(d, axis=0, keepdims=True)

    rev = pl.BlockSpec((tm, LANES), lambda i: (n_blk - 1 - i, 0))
    vec = pl.BlockSpec((1, LANES), lambda i: (0, 0))
    return pl.pallas_call(
        body, name="gate_bwd", grid=(n_blk,), in_specs=[rev, vec, rev], out_specs=[rev, vec],
        out_shape=[jax.ShapeDtypeStruct((s, LANES), BF16), jax.ShapeDtypeStruct((1, LANES), F32)],
        scratch_shapes=[pltpu.VMEM((HALO, LANES), F32)],
        compiler_params=_cp(("arbitrary",)))(f, b_pad, d_f)


_NT = (((1,), (1,)), ((), ()))
_NN = (((1,), (0,)), ((), ()))
_TN = (((0,), (0,)), ((), ()))


def _head_masks(shape):
    lane = lax.broadcasted_iota(jnp.int32, shape, len(shape) - 1)
    return lane < HEAD_DIM


def _pick_row(ft, h):
    rows = lax.broadcasted_iota(jnp.int32, ft.shape, 0)
    return jnp.sum(jnp.where(rows == h, ft, 0.0), axis=0, keepdims=True)


def _prune_bounds(qkv, fb):
    s = qkv.shape[0]
    tq = min(TQ, s)

    def body(q_ref, k_ref, fb_ref, out_ref):
        head_a = _head_masks((tq, LANES))
        lane = lax.broadcasted_iota(jnp.int32, (HALO, LANES), 1)
        acc = jnp.zeros((HALO, LANES), F32)
        for p in range(N_PAIRS):
            cols = slice(p * LANES, (p + 1) * LANES)
            q2 = q_ref[:, cols].astype(F32) * Q_SCALE
            k2 = k_ref[:, cols].astype(F32)
            f2 = fb_ref[:, cols]
            for hh in range(2):
                sel = head_a if hh == 0 else jnp.logical_not(head_a)
                qn = jnp.sqrt(jnp.sum(jnp.where(sel, q2 * q2, 0.0), axis=-1, keepdims=True))
                kn = jnp.sqrt(jnp.sum(jnp.where(sel, k2 * k2, 0.0), axis=-1, keepdims=True))
                f = f2[:, hh * HEAD_DIM:hh * HEAD_DIM + 1]
                h = 2 * p + hh
                vals = (jnp.max(qn, axis=0, keepdims=True), jnp.max(kn, axis=0, keepdims=True),
                        jnp.max(qn * kn + f, axis=0, keepdims=True), f[tq - 1:tq, :])
                for slot, v in enumerate(vals):
                    acc = jnp.where(lane == slot * N_HEADS + h, v, acc)
        out_ref[0] = acc

    blk = lambda j: pl.BlockSpec((tq, ATTN_W), lambda i: (i, j))
    return pl.pallas_call(
        body, name="prune_bounds", grid=(s // tq,), in_specs=[blk(0), blk(1), blk(0)],
        out_specs=pl.BlockSpec((1, HALO, LANES), lambda i: (i, 0, 0)),
        out_shape=jax.ShapeDtypeStruct((s // tq, HALO, LANES), F32),
        compiler_params=_cp(("parallel",)))(qkv, qkv, fb)


def _first_key_blocks(qkv, fb):
    t = _prune_bounds(qkv, fb)[:, 0, :]
    nh = N_HEADS
    a, b, c, e = t[:, 0:nh], t[:, nh:2 * nh], t[:, 2 * nh:3 * nh], t[:, 3 * nh:4 * nh]
    bound = a[:, None, :] * b[None, :, :] * 1.001 + c[:, None, :] - e[None, :, :]
    n_q = t.shape[0]
    idx = jnp.arange(n_q)
    need = jnp.logical_not(bound < -(EXP_ZERO + 2.0)) | (idx[None, :, None] >= idx[:, None, None])
    first = jnp.argmax(need, axis=1).astype(jnp.int32)
    return jnp.min(first.reshape(n_q, N_PAIRS, 2), axis=-1).T.reshape(-1)


def _attn_fwd(qkv, fb, ft, first_blk):
    s = qkv.shape[0]
    tq = min(TQ, s)
    n_q = s // tq
    neg = -1e30

    rc = min(ROW_CHUNK, tq)

    def body(first_ref, q_ref, k_ref, v_ref, fb_ref, ft_ref, o_ref, g_ref,
             s_a, s_b, p_ab, m_a, l_a, m_b, l_b, alpha2, acc, pmax_a, psum_a, pmax_b, psum_b):
        p = pl.program_id(0)
        i = pl.program_id(1)
        head_a = _head_masks((tq, LANES))
        q2 = q_ref[...] * Q_SCALE
        zero = jnp.zeros_like(q2)
        q_a, q_b = jnp.where(head_a, q2, zero), jnp.where(head_a, zero, q2)
        for m_scr, l_scr in ((m_a, l_a), (m_b, l_b)):
            m_scr[...] = jnp.full(m_scr.shape, neg, F32)
            l_scr[...] = jnp.zeros(l_scr.shape, F32)
        acc[...] = jnp.zeros(acc.shape, F32)

        def step(kb, masked):
            rows_k = pl.ds(pl.multiple_of(kb * tq, tq), tq)
            k2, v2 = k_ref[rows_k, :], v_ref[rows_k, :]
            ftv = ft_ref[kb]
            fk = (_pick_row(ftv, 2 * p), _pick_row(ftv, 2 * p + 1))
            s_a[...] = lax.dot_general(q_a, k2, _NT, preferred_element_type=F32)
            s_b[...] = lax.dot_general(q_b, k2, _NT, preferred_element_type=F32)
            alphas = []
            for hh, (s_scr, m_scr, l_scr, pmax, psum) in enumerate(((s_a, m_a, l_a, pmax_a, psum_a),
                                                                     (s_b, m_b, l_b, pmax_b, psum_b))):
                fq = fb_ref[:, hh * HEAD_DIM:hh * HEAD_DIM + 1]

                def shifted(r, ncol, add=None):
                    rows = slice(r * rc, (r + 1) * rc)
                    t = s_scr[rows, 0:ncol]
                    if add is not None:
                        t = t + add[rows, :]
                    t = t - fk[hh][:, 0:ncol]
                    if masked:
                        col_id = lax.broadcasted_iota(jnp.int32, (rc, ncol), 1)
                        row_id = lax.broadcasted_iota(jnp.int32, (rc, ncol), 0) + r * rc
                        t = jnp.where(col_id <= row_id, t, -jnp.inf)
                    return rows, t

                def lane_blocks(t, op):
                    out = t[:, 0:LANES]
                    for cb in range(1, t.shape[1] // LANES):
                        out = op(out, t[:, cb * LANES:(cb + 1) * LANES])
                    return out

                ncols = [min(tq, -(-((r + 1) * rc) // LANES) * LANES) if masked else tq for r in range(tq // rc)]
                for r, ncol in enumerate(ncols):
                    rows, t = shifted(r, ncol)
                    pmax[rows, :] = lane_blocks(t, jnp.maximum)
                m_old = m_scr[...]
                m_new = jnp.maximum(m_old, jnp.max(pmax[...], axis=-1, keepdims=True) + fq)
                alpha = jnp.exp(m_old - m_new)
                m_scr[...] = m_new
                shift = fq - m_new
                for r, ncol in enumerate(ncols):
                    rows, t = shifted(r, ncol, shift)
                    pr = jnp.exp(t)
                    psum[rows, :] = lane_blocks(pr, jnp.add)
                    p_ab[rows, hh * tq:hh * tq + ncol] = pr.astype(BF16)
                    if ncol < tq:
                        p_ab[rows, hh * tq + ncol:(hh + 1) * tq] = jnp.zeros((rc, tq - ncol), BF16)
                l_scr[...] = alpha * l_scr[...] + jnp.sum(psum[...], axis=-1, keepdims=True)
                alphas.append(alpha)
            alpha2[...] = jnp.where(head_a, alphas[0], alphas[1])
            zv = jnp.zeros_like(v2)
            head_k = _head_masks(v2.shape)
            vv = jnp.concatenate([jnp.where(head_k, v2, zv), jnp.where(head_k, zv, v2)], axis=0)
            acc[...] = acc[...] * alpha2[...] + lax.dot_general(p_ab[...], vv, _NN, preferred_element_type=F32)

        def unmasked(kb, carry):
            step(kb, False)
            return carry

        lax.fori_loop(first_ref[p * n_q + i], i, unmasked, 0)
        step(i, True)
        fq2 = fb_ref[...]
        o_ref[...] = acc[...] / jnp.where(head_a, l_a[...], l_b[...])
        g_ref[...] = fq2 - jnp.where(head_a, m_a[...] + jnp.log(l_a[...]), m_b[...] + jnp.log(l_b[...]))

    qblk = lambda off: pl.BlockSpec((tq, LANES), lambda p, i, first: (i, off + p))
    full = lambda off: pl.BlockSpec((s, LANES), lambda p, i, first: (0, off + p))
    col_scr = pltpu.VMEM((tq, 1), F32)
    grid_spec = pltpu.PrefetchScalarGridSpec(
        num_scalar_prefetch=1, grid=(N_PAIRS, n_q),
        in_specs=[qblk(0), full(N_PAIRS), full(2 * N_PAIRS), qblk(0),
                  pl.BlockSpec((n_q, N_HEADS, tq), lambda p, i, first: (0, 0, 0))],
        out_specs=[qblk(0), qblk(0)],
        scratch_shapes=[pltpu.VMEM((tq, tq), F32), pltpu.VMEM((tq, tq), F32), pltpu.VMEM((tq, 2 * tq), BF16),
                        col_scr, col_scr, col_scr, col_scr] + [pltpu.VMEM((tq, LANES), F32)] * 6)
    return pl.pallas_call(
        body, name="attn_fwd", grid_spec=grid_spec,
        out_shape=[jax.ShapeDtypeStruct((s, ATTN_W), F32), jax.ShapeDtypeStruct((s, ATTN_W), F32)],
        compiler_params=_cp(("parallel", "arbitrary")))(first_blk, qkv, qkv, qkv, fb, ft)


def _attn_bwd(qkv, do, g, delta, ft, first_blk):
    s = qkv.shape[0]
    tq = min(TQ, s)
    n_q = s // tq

    def body(first_ref, q_ref, do_ref, g_ref, dl_ref, k_ref, v_ref, ft_ref, dq_ref, dfq_ref, dk_ref, dv_ref,
             dfk_ref):
        p = pl.program_id(0)
        i = pl.program_id(1)

        @pl.when(i == 0)
        def _():
            dk_ref[...] = jnp.zeros_like(dk_ref)
            dv_ref[...] = jnp.zeros_like(dv_ref)
            dfk_ref[...] = jnp.zeros_like(dfk_ref)

        head_a = _head_masks((tq, LANES))
        q2 = q_ref[...] * Q_SCALE
        do2 = do_ref[...]
        zero = jnp.zeros_like(q2)
        q_a, q_b = jnp.where(head_a, q2, zero), jnp.where(head_a, zero, q2)
        do_a, do_b = jnp.where(head_a, do2, zero), jnp.where(head_a, zero, do2)
        g2, dl2 = g_ref[...], dl_ref[...]
        g_a, g_b = g2[:, 0:1], g2[:, HEAD_DIM:HEAD_DIM + 1]
        dl_a, dl_b = dl2[:, 0:1], dl2[:, HEAD_DIM:HEAD_DIM + 1]
        causal = lax.broadcasted_iota(jnp.int32, (tq, tq), 1) <= lax.broadcasted_iota(jnp.int32, (tq, tq), 0)
        rows8 = lax.broadcasted_iota(jnp.int32, (N_HEADS, tq), 0)

        def one_head(q_h, do_h, g_h, dl_h, fk_h, k2, v2, masked):
            sc = (lax.dot_general(q_h, k2, _NT, preferred_element_type=F32) + g_h) - fk_h
            pr = jnp.exp(sc)
            if masked:
                pr = jnp.where(causal, pr, 0.0)
            dp = lax.dot_general(do_h, v2, _NT, preferred_element_type=F32)
            ds = pr * (dp - dl_h)
            return (pr.astype(BF16), ds.astype(BF16), jnp.sum(ds, axis=0, keepdims=True),
                    jnp.sum(ds, axis=1, keepdims=True))

        def step(j, carry, masked):
            dq, r_a, r_b = carry
            rows = pl.ds(pl.multiple_of(j * tq, tq), tq)
            k2, v2 = k_ref[rows, :], v_ref[rows, :]
            k_a, k_b = jnp.where(head_a, k2, zero), jnp.where(head_a, zero, k2)
            ftv = ft_ref[j]
            p_a, ds_a, c_a, s_a = one_head(q_a, do_a, g_a, dl_a, _pick_row(ftv, 2 * p), k2, v2, masked)
            p_b, ds_b, c_b, s_b = one_head(q_b, do_b, g_b, dl_b, _pick_row(ftv, 2 * p + 1), k2, v2, masked)
            dv_ref[rows, :] += (lax.dot_general(p_a, do_a, _TN, preferred_element_type=F32)
                                + lax.dot_general(p_b, do_b, _TN, preferred_element_type=F32))
            dk_ref[rows, :] += (lax.dot_general(ds_a, q_a, _TN, preferred_element_type=F32)
                                + lax.dot_general(ds_b, q_b, _TN, preferred_element_type=F32))
            dfk_ref[0, j] += jnp.where(rows8 == 0, -c_a, jnp.where(rows8 == 1, -c_b, 0.0))
            dq = dq + (lax.dot_general(ds_a, k_a, _NN, preferred_element_type=F32)
                       + lax.dot_general(ds_b, k_b, _NN, preferred_element_type=F32))
            return dq, r_a + s_a, r_b + s_b

        zcol = jnp.zeros((tq, 1), F32)
        carry = lax.fori_loop(first_ref[p * n_q + i], i, lambda j, cr: step(j, cr, False),
                              (jnp.zeros((tq, LANES), F32), zcol, zcol))
        dq, r_a, r_b = step(i, carry, True)
        dq_ref[...] = (dq * Q_SCALE).astype(BF16)
        dfq_ref[...] = jnp.where(head_a, r_a, r_b)

    qblk = lambda off: pl.BlockSpec((tq, LANES), lambda p, i, first: (i, off + p))
    full = lambda off: pl.BlockSpec((s, LANES), lambda p, i, first: (0, off + p))
    grid_spec = pltpu.PrefetchScalarGridSpec(
        num_scalar_prefetch=1, grid=(N_PAIRS, n_q),
        in_specs=[qblk(0), qblk(0), qblk(0), qblk(0), full(N_PAIRS), full(2 * N_PAIRS),
                  pl.BlockSpec((n_q, N_HEADS, tq), lambda p, i, first: (0, 0, 0))],
        out_specs=[qblk(0), qblk(0), full(0), full(0),
                   pl.BlockSpec((1, n_q, N_HEADS, tq), lambda p, i, first: (p, 0, 0, 0))])
    return pl.pallas_call(
        body, name="attn_bwd", grid_spec=grid_spec,
        out_shape=[jax.ShapeDtypeStruct((s, ATTN_W), BF16), jax.ShapeDtypeStruct((s, ATTN_W), F32),
                   jax.ShapeDtypeStruct((s, ATTN_W), F32), jax.ShapeDtypeStruct((s, ATTN_W), F32),
                   jax.ShapeDtypeStruct((N_PAIRS, n_q, N_HEADS, tq), F32)],
        compiler_params=_cp(("parallel", "arbitrary")))(first_blk, qkv, do, g, delta, qkv, qkv, ft)


AUG = HEAD_DIM
NORM_MARGIN = 1.01


def _split3(x):
    hi = x.astype(BF16).astype(F32)
    r = x - hi
    mid = r.astype(BF16).astype(F32)
    lo = (r - mid).astype(BF16).astype(F32)
    return hi, mid, lo


def _aug(base, lane, vals):
    out = jnp.where(lane < AUG, base, 0.0)
    for k, v in enumerate(vals):
        out = jnp.where(lane == AUG + k, v, out)
    return out


def _attn_prep(qkv, fb, bigs):
    s = qkv.shape[0]
    tq = min(TQ, s)
    n_q = s // tq

    n = len(bigs)
    arrays, landing, sems = _gather_operands(bigs, [])

    def body(q_ref, k_ref, v_ref, fb_ref, *rest):
        qx_ref, kx_ref, kxt_ref, vx_ref, vt_ref, b_ref = rest[2 * n:2 * n + 6]
        finish = _hosted_gather((rest[:n], rest[2 * n + 6:3 * n + 6]) + tuple(rest[3 * n + 6:]), n, n,
                                pl.program_id(0), n_q)
        lane = lax.broadcasted_iota(jnp.int32, (tq, LANES), 1)
        lane8 = lax.broadcasted_iota(jnp.int32, (HALO, LANES), 1)
        head_lanes = lane < AUG
        is_lane = [lane == AUG + k for k in range(6)]
        first3 = (lane >= AUG) & (lane < AUG + 3)
        next3 = (lane >= AUG + 3) & (lane < AUG + 6)
        q_const = jnp.where(first3, -1.0, 0.0)
        k_const = jnp.where(next3, 1.0, 0.0)
        v_const = jnp.where(first3, 1.0, 0.0)
        ones_head = (lax.broadcasted_iota(jnp.int32, (LANES, LANES), 0) < HEAD_DIM).astype(BF16)
        acc = jnp.zeros((HALO, LANES), F32)
        for p in range(N_PAIRS):
            cols = slice(p * LANES, (p + 1) * LANES)
            q2, k2, v2 = (ref[:, cols].astype(F32) for ref in (q_ref, k_ref, v_ref))
            q2 = q2 * Q_SCALE
            f2 = fb_ref[:, cols]
            for hh in range(2):
                h = 2 * p + hh
                q, k, v = ((pltpu.roll(x, HEAD_DIM, 1) if hh else x) for x in (q2, k2, v2))
                f = f2 if hh else pltpu.roll(f2, HEAD_DIM, 1)
                hi, mid, lo = _split3(f)
                q_aug = jnp.where(is_lane[3], hi, jnp.where(is_lane[4], mid, jnp.where(is_lane[5], lo, q_const)))
                k_aug = jnp.where(is_lane[0], hi, jnp.where(is_lane[1], mid, jnp.where(is_lane[2], lo, k_const)))
                kx = jnp.where(head_lanes, k, k_aug)
                vx = jnp.where(head_lanes, v, v_const)
                qx_ref[h] = jnp.where(head_lanes, q, q_aug).astype(BF16)
                kx_ref[h] = kx.astype(BF16)
                vx_ref[h] = vx.astype(BF16)
                kxt_ref[h, 0] = kx.T.astype(BF16)
                vt_ref[h, 0] = vx.T.astype(BF16)
                q_sq = lax.dot_general((q * q).astype(BF16), ones_head, _NN, preferred_element_type=F32)
                k_sq = lax.dot_general((k * k).astype(BF16), ones_head, _NN, preferred_element_type=F32)
                qk = jnp.sqrt(q_sq * k_sq) * NORM_MARGIN + f
                vals = (jnp.sqrt(jnp.max(q_sq, axis=0, keepdims=True)), jnp.sqrt(jnp.max(k_sq, axis=0, keepdims=True)),
                        jnp.max(qk, axis=0, keepdims=True), f[tq - 1:tq, :])
                for slot, val in enumerate(vals):
                    acc = jnp.where(lane8 == slot * N_HEADS + h, val[:, AUG:AUG + 1], acc)
        b_ref[0] = acc
        finish()

    blk = lambda j: pl.BlockSpec((tq, ATTN_W), lambda i: (i, j))
    rows = pl.BlockSpec((N_HEADS, tq, LANES), lambda i: (0, i, 0))
    cols_t = pl.BlockSpec((N_HEADS, 1, LANES, tq), lambda i: (0, i, 0, 0))
    shp = jax.ShapeDtypeStruct((N_HEADS, s, LANES), BF16)
    shp_t = jax.ShapeDtypeStruct((N_HEADS, n_q, LANES, tq), BF16)
    outs = pl.pallas_call(
        body, name="attn_prep", grid=(n_q,), in_specs=[blk(0), blk(1), blk(2), blk(0)] + _hbm_specs(2 * n),
        out_specs=[rows, rows, cols_t, rows, cols_t,
                   pl.BlockSpec((1, HALO, LANES), lambda i: (i, 0, 0))] + _hbm_specs(n),
        out_shape=[shp, shp, shp_t, shp, shp_t, jax.ShapeDtypeStruct((n_q, HALO, LANES), F32)]
        + [jax.ShapeDtypeStruct(b.shape, b.dtype) for b in landing],
        input_output_aliases={4 + n + k: 6 + k for k in range(n)}, scratch_shapes=sems,
        compiler_params=_cp(("arbitrary",)))(qkv, qkv, qkv, fb, *arrays, *landing)
    return tuple(outs[:6]) + (outs[6:],)


def _key_block_ranges(bounds):
    t = bounds[:, 0, :]
    nh = N_HEADS
    a, b, c, e = t[:, 0:nh], t[:, nh:2 * nh], t[:, 2 * nh:3 * nh], t[:, 3 * nh:4 * nh]
    bound = a[:, None, :] * b[None, :, :] * NORM_MARGIN + c[:, None, :] - e[None, :, :]
    n_q = t.shape[0]
    idx = jnp.arange(n_q)
    need = jnp.logical_not(bound < -(EXP_ZERO + 2.0)) | (idx[None, :, None] >= idx[:, None, None])
    first = jnp.argmax(need, axis=1).astype(jnp.int32)
    first = jnp.min(first.reshape(n_q, N_PAIRS, 2), axis=-1)
    visits = (first[:, None, :] <= idx[None, :, None]) & (idx[None, :, None] <= idx[:, None, None])
    last = jnp.max(jnp.where(visits, idx[:, None, None], 0), axis=0).astype(jnp.int32)
    return first.T.reshape(-1), last.T.reshape(-1)


def _attn_fwd_t(qx, kx, vt, first_blk):
    _, s, _ = qx.shape
    tq = min(TQ, s)
    n_q = s // tq
    neg = -1e30

    def body(first_ref, qx_ref, kx_ref, vt_ref, o_ref, lse_ref, acc_ref, m_ref, s_even, s_odd):
        p = pl.program_id(0)
        i = pl.program_id(1)
        acc_ref[...] = jnp.zeros(acc_ref.shape, F32)
        m_ref[...] = jnp.full(m_ref.shape, neg, F32)
        key_le_query = (lax.broadcasted_iota(jnp.int32, (tq, tq), 0) <= lax.broadcasted_iota(jnp.int32, (tq, tq), 1))
        first = first_ref[p * n_q + i]

        def scores(kb, hh, dst):
            rows_k = pl.ds(pl.multiple_of(kb * tq, tq), tq)
            dst[hh] = lax.dot_general(kx_ref[hh, rows_k, :], qx_ref[hh], _NT, preferred_element_type=F32)

        def step(kb, src, nxt):
            for hh in range(2):
                st = src[hh]
                if nxt is None:
                    st = jnp.where(key_le_query, st, -jnp.inf)
                else:
                    scores(kb + 1, hh, nxt)
                m_old = m_ref[hh]
                m_new = jnp.maximum(m_old, jnp.max(st, axis=0, keepdims=True))
                m_ref[hh] = m_new
                pt = jnp.exp(st - m_new).astype(BF16)
                acc_ref[hh] = acc_ref[hh] * jnp.exp(m_old - m_new) + lax.dot_general(
                    vt_ref[hh, kb], pt, _NN, preferred_element_type=F32)

        def by_parity(kb, fn):
            @pl.when(kb % 2 == 0)
            def _():
                fn(s_even, s_odd)

            @pl.when(kb % 2 == 1)
            def _():
                fn(s_odd, s_even)

        def first_scores(src, nxt):
            scores(first, 0, src)
            scores(first, 1, src)

        def unmasked(kb, carry):
            by_parity(kb, lambda src, nxt: step(kb, src, nxt))
            return carry

        by_parity(first, first_scores)
        lax.fori_loop(first, i, unmasked, 0)
        by_parity(i, lambda src, nxt: step(i, src, None))
        outs, lses = [], []
        for hh in range(2):
            acc = acc_ref[hh]
            l = acc[AUG:AUG + 1, :]
            outs.append(acc[0:HEAD_DIM, :] / l)
            lses.append(m_ref[hh] + jnp.log(l))
        o_ref[...] = jnp.concatenate(outs, axis=0).T
        rows8 = lax.broadcasted_iota(jnp.int32, (N_HEADS, tq), 0)
        lse_ref[0, 0] = jnp.where(rows8 == 0, lses[0], jnp.where(rows8 == 1, lses[1], 0.0))

    grid_spec = pltpu.PrefetchScalarGridSpec(
        num_scalar_prefetch=1, grid=(N_PAIRS, n_q),
        in_specs=[pl.BlockSpec((2, tq, LANES), lambda p, i, first: (p, i, 0)),
                  pl.BlockSpec((2, s, LANES), lambda p, i, first: (p, 0, 0)),
                  pl.BlockSpec((2, n_q, LANES, tq), lambda p, i, first: (p, 0, 0, 0))],
        out_specs=[pl.BlockSpec((tq, LANES), lambda p, i, first: (i, p)),
                   pl.BlockSpec((1, 1, N_HEADS, tq), lambda p, i, first: (p, i, 0, 0))],
        scratch_shapes=[pltpu.VMEM((2, LANES, tq), F32), pltpu.VMEM((2, 1, tq), F32),
                        pltpu.VMEM((2, tq, tq), F32), pltpu.VMEM((2, tq, tq), F32)])
    return pl.pallas_call(
        body, name="attn_fwd", grid_spec=grid_spec,
        out_shape=[jax.ShapeDtypeStruct((s, ATTN_W), F32), jax.ShapeDtypeStruct((N_PAIRS, n_q, N_HEADS, tq), F32)],
        compiler_params=_cp(("parallel", "arbitrary")))(first_blk, qx, kx, vt)


def _attn_bwd_t(qx, dox, kx, kxt, vx, lse, last_blk):
    _, s, _ = qx.shape
    tq = min(TQ, s)
    n_q = s // tq

    def body(last_ref, qx_ref, dox_ref, lse_ref, kx_ref, kxt_ref, vx_ref, dk_ref, dv_ref, dfk_ref, dqt_ref, dfq_ref):
        p = pl.program_id(0)
        j = pl.program_id(1)

        @pl.when(j == 0)
        def _():
            dqt_ref[...] = jnp.zeros(dqt_ref.shape, F32)
            dfq_ref[...] = jnp.zeros(dfq_ref.shape, F32)

        key_le_query = (lax.broadcasted_iota(jnp.int32, (tq, tq), 0) <= lax.broadcasted_iota(jnp.int32, (tq, tq), 1))

        def step(i, carry, masked):
            rows_q = pl.ds(pl.multiple_of(i * tq, tq), tq)
            out = []
            for hh in range(2):
                dk, dv, col = carry[3 * hh:3 * hh + 3]
                q, do = qx_ref[hh, rows_q, :], dox_ref[hh, rows_q, :]
                st = lax.dot_general(kx_ref[hh], q, _NT, preferred_element_type=F32)
                pt = jnp.exp(st - lse_ref[0, i, hh:hh + 1, :])
                if masked:
                    pt = jnp.where(key_le_query, pt, 0.0)
                dst = pt * lax.dot_general(vx_ref[hh], do, _NT, preferred_element_type=F32)
                pb, dsb = pt.astype(BF16), dst.astype(BF16)
                dv = dv + lax.dot_general(pb, do, _NN, preferred_element_type=F32)
                dk = dk + lax.dot_general(dsb, q, _NN, preferred_element_type=F32)
                dqt_ref[hh, i] += lax.dot_general(kxt_ref[hh, 0], dsb, _NN, preferred_element_type=F32)
                for cb in range(tq // LANES):
                    col = col + dst[:, cb * LANES:(cb + 1) * LANES]
                dfq_ref[hh, i] += jnp.sum(dst.reshape(tq // HALO, HALO, tq), axis=0)
                out += [dk, dv, col]
            return tuple(out)

        zero = jnp.zeros((tq, LANES), F32)
        carry = step(j, (zero,) * 6, True)
        dk_a, dv_a, col_a, dk_b, dv_b, col_b = lax.fori_loop(j + 1, last_ref[p * n_q + j] + 1,
                                                             lambda i, cr: step(i, cr, False), carry)
        head_a = lax.broadcasted_iota(jnp.int32, (tq, LANES), 1) < HEAD_DIM
        dk_ref[...] = jnp.where(head_a, dk_a, pltpu.roll(dk_b, HEAD_DIM, 1)).astype(BF16)
        dv_ref[...] = jnp.where(head_a, dv_a, pltpu.roll(dv_b, HEAD_DIM, 1)).astype(BF16)
        rows8 = lax.broadcasted_iota(jnp.int32, (N_HEADS, tq), 0)
        dfk_a, dfk_b = (-jnp.sum(c.T, axis=0, keepdims=True) for c in (col_a, col_b))
        dfk_ref[0, 0] = jnp.where(rows8 == 0, dfk_a, jnp.where(rows8 == 1, dfk_b, 0.0))

    resident = pl.BlockSpec((2, s, LANES), lambda p, j, last: (p, 0, 0))
    key_rows = pl.BlockSpec((2, tq, LANES), lambda p, j, last: (p, j, 0))
    pair_out = pl.BlockSpec((tq, LANES), lambda p, j, last: (j, p))
    grid_spec = pltpu.PrefetchScalarGridSpec(
        num_scalar_prefetch=1, grid=(N_PAIRS, n_q),
        in_specs=[resident, resident, pl.BlockSpec((1, n_q, N_HEADS, tq), lambda p, j, last: (p, 0, 0, 0)),
                  key_rows, pl.BlockSpec((2, 1, LANES, tq), lambda p, j, last: (p, j, 0, 0)), key_rows],
        out_specs=[pair_out, pair_out, pl.BlockSpec((1, 1, N_HEADS, tq), lambda p, j, last: (p, j, 0, 0)),
                   pl.BlockSpec((2, n_q, LANES, tq), lambda p, j, last: (p, 0, 0, 0)),
                   pl.BlockSpec((2, n_q, HALO, tq), lambda p, j, last: (p, 0, 0, 0))])
    return pl.pallas_call(
        body, name="attn_bwd", grid_spec=grid_spec,
        out_shape=[jax.ShapeDtypeStruct((s, ATTN_W), BF16), jax.ShapeDtypeStruct((s, ATTN_W), BF16),
                   jax.ShapeDtypeStruct((N_PAIRS, n_q, N_HEADS, tq), F32),
                   jax.ShapeDtypeStruct((N_HEADS, n_q, LANES, tq), F32),
                   jax.ShapeDtypeStruct((N_HEADS, n_q, HALO, tq), F32)],
        compiler_params=_cp(("parallel", "arbitrary")))(last_blk, qx, dox, lse, kx, kxt, vx)


def _attn_dq_finish(dqt):
    _, n_q, _, tq = dqt.shape
    per = 4 if n_q % 4 == 0 else 1

    def body(dqt_ref, dq_ref):
        for b in range(per):
            a, bb = dqt_ref[0, b], dqt_ref[1, b]
            dq_ref[b * tq:(b + 1) * tq, :] = (
                jnp.concatenate([a[0:HEAD_DIM], bb[0:HEAD_DIM]], axis=0).T * Q_SCALE).astype(BF16)

    return pl.pallas_call(
        body, name="attn_dq_finish", grid=(N_PAIRS, n_q // per),
        in_specs=[pl.BlockSpec((2, per, LANES, tq), lambda p, i: (p, i, 0, 0))],
        out_specs=pl.BlockSpec((per * tq, LANES), lambda p, i: (i, p)),
        out_shape=jax.ShapeDtypeStruct((n_q * tq, ATTN_W), BF16),
        compiler_params=_cp(("parallel", "parallel")))(dqt)


def _ffn_act_fwd(up, w_ffn):
    s = up.shape[0]
    tm, tn = min(TM_FFN, s), TN_FFN
    nb = D_FF // tn

    def body(a_ref, g_ref, ap_ref, gp_ref, wa_ref, wg_ref, act_ref):
        i = pl.program_id(1)

        def conv(blk_ref, prev_ref, w_ref):
            prev = jnp.where(i > 0, prev_ref[...], 0.0)
            return _conv_taps(jnp.concatenate([prev, blk_ref[...]], axis=0), w_ref[...])[HALO:]

        u_a, u_g = conv(a_ref, ap_ref, wa_ref), conv(g_ref, gp_ref, wg_ref)
        act_ref[...] = (u_g * jax.nn.sigmoid(u_g) * u_a).astype(BF16)

    blk = lambda off: pl.BlockSpec((tm, tn), lambda n, i: (i, off + n))
    prev = lambda off: pl.BlockSpec((HALO, tn), lambda n, i: (jnp.maximum(i * (tm // HALO) - 1, 0), off + n))
    wsp = lambda off: pl.BlockSpec((3, tn), lambda n, i: (0, off + n))
    return pl.pallas_call(
        body, name="ffn_act_fwd", grid=(nb, s // tm),
        in_specs=[blk(0), blk(nb), prev(0), prev(nb), wsp(0), wsp(nb)],
        out_specs=pl.BlockSpec((tm, tn), lambda n, i: (i, n)),
        out_shape=jax.ShapeDtypeStruct((s, D_FF), BF16),
        compiler_params=_cp(("parallel", "parallel")))(up, up, up, up, w_ffn, w_ffn)


def _ffn_down_loss(up, w_ffn, w_down, x2, target, g_final):
    s, d = x2.shape
    tm, tn = min(TM_ROWS, s), TN_FFN
    nb = D_FF // tn

    def body(a_ref, g_ref, ap_ref, gp_ref, wa_ref, wg_ref, wd_ref, x2_ref, t_ref, gf_ref,
             act_ref, dx_ref, dxb_ref, loss_ref, gg_ref, acc, act_even, act_odd):
        i = pl.program_id(0)
        k = pl.program_id(1)

        @pl.when((i == 0) & (k == 0))
        def _():
            gg_ref[...] = jnp.zeros_like(gg_ref)
            loss_ref[...] = jnp.zeros_like(loss_ref)

        def conv(blk_ref, prev_ref, w_ref):
            prev = jnp.where(i > 0, prev_ref[...], 0.0)
            return _conv_taps(jnp.concatenate([prev, blk_ref[...]], axis=0), w_ref[...])[HALO:]

        def activation(dst):
            u_a, u_g = conv(a_ref, ap_ref, wa_ref), conv(g_ref, gp_ref, wg_ref)
            act = (u_g * (1.0 / (1.0 + jnp.exp(-u_g))) * u_a).astype(BF16)
            act_ref[...] = act
            dst[...] = act

        def project(src, first):
            part = lax.dot_general(src[...], wd_ref[...], _NN, preferred_element_type=F32)
            acc[...] = part if first else acc[...] + part

        @pl.when(k == 0)
        def _():
            activation(act_even)

        @pl.when(k == 1)
        def _():
            project(act_even, True)
            activation(act_odd)

        @pl.when((k > 1) & (k < nb) & (k % 2 == 0))
        def _():
            project(act_odd, False)
            activation(act_even)

        @pl.when((k > 1) & (k < nb) & (k % 2 == 1))
        def _():
            project(act_even, False)
            activation(act_odd)

        @pl.when(k == nb)
        def _():
            project(act_even if nb % 2 == 1 else act_odd, False)
            xv = x2_ref[...] + acc[...]
            r = _rstd(xv)
            xn = xv * r
            gv = gf_ref[...]
            err = xn * gv - t_ref[...]
            loss_ref[...] += 0.5 * jnp.sum(jnp.mean(err * err, axis=-1, keepdims=True), axis=0, keepdims=True)
            dy = err * (1.0 / d)
            gg_ref[...] += jnp.sum(dy * xn, axis=0, keepdims=True)
            t = dy * gv
            dx = r * (t - xn * jnp.mean(t * xn, axis=-1, keepdims=True))
            dx_ref[...] = dx
            dxb_ref[...] = dx.astype(BF16)

    chunk = lambda k: jnp.minimum(k, nb - 1)
    blk = lambda off: pl.BlockSpec((tm, tn), lambda i, k: (i, off + chunk(k)))
    prev = lambda off: pl.BlockSpec((HALO, tn),
                                    lambda i, k: (jnp.maximum(i * (tm // HALO) - 1, 0), off + chunk(k)))
    wsp = lambda off: pl.BlockSpec((3, tn), lambda i, k: (0, off + chunk(k)))
    row = pl.BlockSpec((tm, d), lambda i, k: (i, 0))
    vec = pl.BlockSpec((1, d), lambda i, k: (0, 0))
    return pl.pallas_call(
        body, name="ffn_down_loss", grid=(s // tm, nb + 1),
        in_specs=[blk(0), blk(nb), prev(0), prev(nb), wsp(0), wsp(nb),
                  pl.BlockSpec((tn, d), lambda i, k: (jnp.maximum(k - 1, 0), 0)), row, row, vec],
        out_specs=[pl.BlockSpec((tm, tn), lambda i, k: (i, chunk(k))), row, row,
                   pl.BlockSpec((1, LANES), lambda i, k: (0, 0)), vec],
        out_shape=[jax.ShapeDtypeStruct((s, D_FF), BF16), jax.ShapeDtypeStruct((s, d), F32),
                   jax.ShapeDtypeStruct((s, d), BF16), jax.ShapeDtypeStruct((1, LANES), F32),
                   jax.ShapeDtypeStruct((1, d), F32)],
        scratch_shapes=[pltpu.VMEM((tm, d), F32), pltpu.VMEM((tm, tn), BF16), pltpu.VMEM((tm, tn), BF16)],
        compiler_params=_cp(("arbitrary", "arbitrary")))(up, up, up, up, w_ffn, w_ffn, w_down, x2, target, g_final)


def _ffn_act_bwd(up, dact, w_ffn):
    s = up.shape[0]
    tm, tn = min(TM_FFN, s), TN_FFN
    nb = D_FF // tn
    n_blk = s // tm

    def body(a_ref, g_ref, ap_ref, gp_ref, an_ref, gn_ref, d_ref, dn_ref, wa_ref, wg_ref,
             dup_ref, gwa_ref, gwg_ref):
        i = pl.program_id(1)

        @pl.when(i == 0)
        def _():
            gwa_ref[...] = jnp.zeros_like(gwa_ref)
            gwg_ref[...] = jnp.zeros_like(gwg_ref)

        def ext(blk_ref, prev_ref, next_ref):
            return jnp.concatenate([jnp.where(i > 0, prev_ref[...], 0.0), blk_ref[...], next_ref[...]], axis=0)

        wa, wg = wa_ref[...], wg_ref[...]
        up_a, up_g = ext(a_ref, ap_ref, an_ref), ext(g_ref, gp_ref, gn_ref)
        u_a, u_g = _conv_taps(up_a, wa), _conv_taps(up_g, wg)
        d_e = jnp.concatenate([jnp.zeros((HALO, tn), F32), d_ref[...],
                               jnp.where(i < n_blk - 1, dn_ref[...], 0.0)], axis=0)
        sig = jax.nn.sigmoid(u_g)
        du_a = d_e * (u_g * sig)
        du_g = d_e * u_a * (sig * (1.0 + u_g * (1.0 - sig)))
        blk = slice(HALO, HALO + tm)
        dup_ref[0] = _conv_taps_t(du_a, wa)[blk].astype(BF16)
        dup_ref[1] = _conv_taps_t(du_g, wg)[blk].astype(BF16)
        for gw_ref, upv, du in ((gwa_ref, up_a[blk], du_a), (gwg_ref, up_g[blk], du_g)):
            gw_ref[0:1, :] += jnp.sum(upv * _shift_up(du, 2)[blk], axis=0, keepdims=True)
            gw_ref[1:2, :] += jnp.sum(upv * _shift_up(du, 1)[blk], axis=0, keepdims=True)
            gw_ref[2:3, :] += jnp.sum(upv * du[blk], axis=0, keepdims=True)

    blk = lambda off: pl.BlockSpec((tm, tn), lambda n, i: (i, off + n))
    prev = lambda off: pl.BlockSpec((HALO, tn), lambda n, i: (jnp.maximum(i * (tm // HALO) - 1, 0), off + n))
    nxt = lambda off: pl.BlockSpec(
        (HALO, tn), lambda n, i: (jnp.minimum((i + 1) * (tm // HALO), s // HALO - 1), off + n))
    wsp = lambda off: pl.BlockSpec((3, tn), lambda n, i: (0, off + n))
    return pl.pallas_call(
        body, name="ffn_act_bwd", grid=(nb, n_blk),
        in_specs=[blk(0), blk(nb), prev(0), prev(nb), nxt(0), nxt(nb), blk(0), nxt(0), wsp(0), wsp(nb)],
        out_specs=[pl.BlockSpec((2, tm, tn), lambda n, i: (0, i, n)), wsp(0), wsp(0)],
        out_shape=[jax.ShapeDtypeStruct((2, s, D_FF), BF16),
                   jax.ShapeDtypeStruct((3, D_FF), F32), jax.ShapeDtypeStruct((3, D_FF), F32)],
        compiler_params=_cp(("parallel", "arbitrary")))(up, up, up, up, up, up, dact, dact, w_ffn, w_ffn)


def _adamw(w, g, m, v, name):
    r, c = w.shape
    tr = next((t for t in (512, 352, 256, 128, 64, 32, 16, 8) if r > t and r % t == 0), r)

    def body(w_ref, g_ref, m_ref, v_ref, d_ref, nm_ref, nv_ref):
        gv = g_ref[...]
        m_new = ADAM_B1 * m_ref[...] + (1.0 - ADAM_B1) * gv
        v_new = ADAM_B2 * v_ref[...] + (1.0 - ADAM_B2) * (gv * gv)
        m_hat = m_new / (1.0 - ADAM_B1 ** ADAM_STEP)
        v_hat = v_new / (1.0 - ADAM_B2 ** ADAM_STEP)
        d_ref[...] = -ADAM_LR * (m_hat / (jnp.sqrt(v_hat) + ADAM_EPS) + ADAM_WD * w_ref[...])
        nm_ref[...] = m_new
        nv_ref[...] = v_new

    spec = pl.BlockSpec((tr, c), lambda i: (i, 0))
    shp = jax.ShapeDtypeStruct((r, c), F32)
    return pl.pallas_call(
        body, name=name, grid=(r // tr,), in_specs=[spec] * 4, out_specs=[spec] * 3, out_shape=[shp] * 3,
        compiler_params=_cp(("parallel",)))(w, g, m, v)


def _sum_rows_block(h):
    return h if h <= 352 else 256


def _pair_sum(view, recv, sel, name):
    n, _, h, c = view.shape
    tr = _sum_rows_block(h)

    def body(sel_ref, a_ref, b_ref, o_ref, ob_ref):
        t = a_ref[...] + b_ref[...]
        o_ref[...] = t
        ob_ref[...] = t.astype(BF16)

    blk = pl.BlockSpec((None, tr, c), lambda j, i, sel_ref: (j, i, 0))
    grid_spec = pltpu.PrefetchScalarGridSpec(
        num_scalar_prefetch=1, grid=(n, h // tr),
        in_specs=[pl.BlockSpec((None, None, tr, c), lambda j, i, sel_ref: (j, sel_ref[0], i, 0)),
                  pl.BlockSpec((None, None, tr, c), lambda j, i, sel_ref: (j, 0, i, 0))],
        out_specs=[blk, blk])
    return pl.pallas_call(
        body, name=name, grid_spec=grid_spec,
        out_shape=[jax.ShapeDtypeStruct((n, h, c), F32), jax.ShapeDtypeStruct((n, h, c), BF16)],
        compiler_params=_cp(("parallel", "parallel")))(sel, view, recv)


def _chip_sum(pair, got, sel, name):
    _, h, c = pair.shape
    tr = _sum_rows_block(h)
    nblk = h // tr

    def body(sel_ref, p_ref, g0_ref, g1_ref, g2_ref, o_ref):
        o_ref[...] = ((p_ref[...] + g0_ref[...].astype(F32)) + g1_ref[...].astype(F32)) + g2_ref[...].astype(F32)

    slot = lambda k: pl.BlockSpec((None, tr, c), lambda i, sel_ref: (k, i, 0))
    grid_spec = pltpu.PrefetchScalarGridSpec(
        num_scalar_prefetch=1, grid=(h // tr,),
        in_specs=[pl.BlockSpec((None, tr, c), lambda i, sel_ref: (sel_ref[1], i, 0)), slot(0), slot(1), slot(2)],
        out_specs=pl.BlockSpec((tr, c), lambda i, sel_ref: (sel_ref[0] * nblk + i, 0)))
    return pl.pallas_call(
        body, name=name, grid_spec=grid_spec, out_shape=jax.ShapeDtypeStruct((2 * h, c), F32),
        compiler_params=_cp(("parallel",)))(sel, pair, got, got, got)


def _place():
    return lax.axis_index("x"), lax.axis_index("y"), lax.axis_index("c")


def _other_chips(x, y):
    return [(1 - x, y), (x, 1 - y), (1 - x, 1 - y)]


def _hbm_specs(n):
    return [pl.BlockSpec(memory_space=pl.ANY)] * n


def _all_gather_weights(bigs, smalls):
    nb, ns = len(bigs), len(smalls)
    n = nb + ns

    def body(*refs):
        start, forward, finish = _gather_phases(refs[:n], refs[2 * n:3 * n], nb, *refs[3 * n:])
        start()
        forward()
        finish()

    arrays, landing, sems = _gather_operands(bigs, smalls)
    return pl.pallas_call(
        body, name="all_gather_weights",
        out_shape=[jax.ShapeDtypeStruct(b.shape, b.dtype) for b in landing],
        in_specs=_hbm_specs(2 * n), out_specs=_hbm_specs(n), input_output_aliases={n + k: k for k in range(n)},
        scratch_shapes=sems)(*arrays, *landing)


def _hosted_gather(refs, n, nb, step, total):
    ins, outs, send_sems, recv_sems = refs
    start, forward, finish = _gather_phases(ins, outs, nb, send_sems, recv_sems)
    pl.when(step == 0)(start)
    pl.when(step == (3 * total) // 4)(forward)
    return lambda: pl.when(step == total - 1)(finish)


def _in_proj(h1, w_a, w_b, w_c, bigs, smalls):
    s, d = h1.shape
    tm, tn = min(TM_MM, s), ATTN_W
    na, nq = w_a.shape[1] // tn, w_b.shape[1] // tn
    steps = na + nq + 1
    total = (s // tm) * steps
    nb, n = len(bigs), len(bigs) + len(smalls)
    arrays, landing, sems = _gather_operands(bigs, smalls)

    def body(h_ref, wa_ref, wb_ref, wc_ref, *rest):
        z_ref, qkv_ref, f_ref = rest[2 * n:2 * n + 3]
        m, j = pl.program_id(0), pl.program_id(1)
        finish = _hosted_gather((rest[:n], rest[2 * n + 3:3 * n + 3]) + tuple(rest[3 * n + 3:]), n, nb,
                                m * steps + j, total)
        h = h_ref[...]

        @pl.when(j < na)
        def _():
            z_ref[...] = lax.dot_general(h, wa_ref[...], _NN, preferred_element_type=F32)

        @pl.when((j >= na) & (j < na + nq))
        def _():
            qkv_ref[...] = lax.dot_general(h, wb_ref[...], _NN, preferred_element_type=F32).astype(BF16)

        @pl.when(j == na + nq)
        def _():
            f_ref[...] = lax.dot_general(h, wc_ref[...], _NN, preferred_element_type=F32)

        finish()

    blk_a = lambda m, j: (m, jnp.minimum(j, na - 1))
    blk_b = lambda m, j: (m, jnp.clip(j - na, 0, nq - 1))
    outs = pl.pallas_call(
        body, name="in_proj", grid=(s // tm, steps),
        in_specs=[pl.BlockSpec((tm, d), lambda m, j: (m, 0)),
                  pl.BlockSpec((d, tn), lambda m, j: (0, jnp.minimum(j, na - 1))),
                  pl.BlockSpec((d, tn), lambda m, j: (0, jnp.clip(j - na, 0, nq - 1))),
                  pl.BlockSpec((d, LANES), lambda m, j: (0, 0))] + _hbm_specs(2 * n),
        out_specs=[pl.BlockSpec((tm, tn), blk_a), pl.BlockSpec((tm, tn), blk_b),
                   pl.BlockSpec((tm, LANES), lambda m, j: (m, 0))] + _hbm_specs(n),
        out_shape=[jax.ShapeDtypeStruct((s, w_a.shape[1]), F32), jax.ShapeDtypeStruct((s, w_b.shape[1]), BF16),
                   jax.ShapeDtypeStruct((s, LANES), F32)] + [jax.ShapeDtypeStruct(b.shape, b.dtype) for b in landing],
        input_output_aliases={4 + n + k: 3 + k for k in range(n)}, scratch_shapes=sems,
        compiler_params=_cp(("arbitrary", "arbitrary")))(h1, w_a, w_b, w_c, *arrays, *landing)
    return outs[0], outs[1], outs[2], outs[3:]


def _gather_operands(bigs, smalls):
    x, y, _ = _place()
    arrays = list(bigs) + list(smalls)
    landing = [lax.dynamic_update_index_in_dim(lax.empty((N_CHIPS,) + a.shape, a.dtype), a, 2 * x + y, 0)
               for a in arrays]
    n_sems = 6 * len(bigs) + 3 * len(smalls)
    return arrays, landing, [pltpu.SemaphoreType.DMA((n_sems,)), pltpu.SemaphoreType.DMA((n_sems,))]


def _gather_phases(ins, outs, nb, send_sems, recv_sems):
    n = len(ins)
    x, y, c = _place()
    my_chip = 2 * x + y
    chips = _other_chips(x, y)
    sibling = (x, y, 1 - c)

    def rows(k, which):
        h = ins[k].shape[0] // 2
        return pl.ds(which * h, h)

    def copy(sem, src, dst, to):
        return pltpu.make_async_remote_copy(src_ref=src, dst_ref=dst, send_sem=send_sems.at[sem],
                                            recv_sem=recv_sems.at[sem], device_id=to, device_id_type=MESH)

    def sends():
        out = [copy(6 * k + j, ins[k].at[rows(k, c)], outs[k].at[my_chip, rows(k, c)], (cx, cy, c))
               for k in range(nb) for j, (cx, cy) in enumerate(chips)]
        return out + [copy(6 * nb + 3 * (k - nb) + j, ins[k], outs[k].at[my_chip], (cx, cy, c))
                      for k in range(nb, n) for j, (cx, cy) in enumerate(chips)]

    def landed(k, j, which):
        cx, cy = chips[j]
        return outs[k].at[2 * cx + cy, rows(k, which)]

    def forwards():
        return [copy(6 * k + 3 + j, landed(k, j, c), landed(k, j, c), sibling)
                for j in range(3) for k in range(nb)]

    def start():
        for cp in sends():
            cp.start()

    def forward():
        for j in range(3):
            for k in range(nb):
                copy(6 * k + j, landed(k, j, c), landed(k, j, c), (x, y, c)).wait_recv()
                copy(6 * k + 3 + j, landed(k, j, c), landed(k, j, c), sibling).start()

    def finish():
        for j, (cx, cy) in enumerate(chips):
            for k in range(nb):
                copy(6 * k + 3 + j, landed(k, j, 1 - c), landed(k, j, 1 - c), (x, y, c)).wait_recv()
            for k in range(nb, n):
                arrived = outs[k].at[2 * cx + cy]
                copy(6 * nb + 3 * (k - nb) + j, arrived, arrived, (x, y, c)).wait_recv()
        for cp in sends() + forwards():
            cp.wait_send()

    return start, forward, finish


def _pair_exchange(views, name):
    n = len(views)

    def body(*refs):
        ins, outs, send_sems, recv_sems = refs[:n], refs[n:2 * n], refs[2 * n], refs[2 * n + 1]
        x, y, c = _place()
        copies = [pltpu.make_async_remote_copy(
            src_ref=ins[k].at[:, pl.ds(1 - c, 1)], dst_ref=outs[k], send_sem=send_sems.at[k],
            recv_sem=recv_sems.at[k], device_id=(x, y, 1 - c), device_id_type=MESH) for k in range(n)]
        for cp in copies:
            cp.start()
        for cp in copies:
            cp.wait()

    return pl.pallas_call(
        body, name=name,
        out_shape=[jax.ShapeDtypeStruct((v.shape[0], 1) + v.shape[2:], v.dtype) for v in views],
        in_specs=_hbm_specs(n), out_specs=_hbm_specs(n),
        scratch_shapes=[pltpu.SemaphoreType.DMA((n,)), pltpu.SemaphoreType.DMA((n,))])(*views)


def _scatter_to_chips(parts):
    n = len(parts)

    def body(*refs):
        start, finish = _scatter_phases(refs[:n], refs[n:2 * n], refs[2 * n], refs[2 * n + 1])
        start()
        finish()

    shapes, sems = _scatter_operands(parts)
    return pl.pallas_call(
        body, name="scatter_grads", out_shape=shapes, in_specs=_hbm_specs(n), out_specs=_hbm_specs(n),
        scratch_shapes=sems)(*parts)


def _scatter_operands(parts):
    n = len(parts)
    return ([jax.ShapeDtypeStruct((3,) + p.shape[1:], p.dtype) for p in parts],
            [pltpu.SemaphoreType.DMA((3 * n,)), pltpu.SemaphoreType.DMA((3 * n,))])


def _scatter_phases(ins, outs, send_sems, recv_sems):
    x, y, c = _place()

    def copies():
        return [pltpu.make_async_remote_copy(
            src_ref=ins[k].at[pl.ds(2 * cx + cy, 1)], dst_ref=outs[k].at[pl.ds(r, 1)], send_sem=send_sems.at[3 * k + r],
            recv_sem=recv_sems.at[3 * k + r], device_id=(cx, cy, c), device_id_type=MESH)
            for k in range(len(ins)) for r, (cx, cy) in enumerate(_other_chips(x, y))]

    def start():
        for cp in copies():
            cp.start()

    def finish():
        for cp in copies():
            cp.wait()

    return start, finish


def _hosted_scatter(ins, outs, sems, step, total):
    start, finish = _scatter_phases(ins, outs, *sems)
    pl.when(step == 0)(start)
    return lambda: pl.when(step == total - 1)(finish)


def _join_halves(shards):
    n = len(shards)

    def body(*refs):
        ins, outs, send_sems, recv_sems = refs[:n], refs[n:2 * n], refs[2 * n], refs[2 * n + 1]
        x, y, c = _place()

        def rows(ref, which):
            h = ref.shape[0] // 2
            return ref.at[pl.ds(which * h, h)]

        sent = [pltpu.make_async_remote_copy(
            src_ref=rows(ins[k], c), dst_ref=rows(outs[k], c), send_sem=send_sems.at[k], recv_sem=recv_sems.at[k],
            device_id=(x, y, 1 - c), device_id_type=MESH) for k in range(n)]
        for cp in sent:
            cp.start()
        for k in range(n):
            pltpu.make_async_remote_copy(
                src_ref=rows(ins[k], 1 - c), dst_ref=rows(outs[k], 1 - c), send_sem=send_sems.at[k],
                recv_sem=recv_sems.at[k], device_id=(x, y, 1 - c), device_id_type=MESH).wait_recv()
        for cp in sent:
            cp.wait_send()

    return pl.pallas_call(
        body, name="half_exchange", out_shape=[jax.ShapeDtypeStruct(a.shape, a.dtype) for a in shards],
        in_specs=_hbm_specs(n), out_specs=_hbm_specs(n), input_output_aliases={k: k for k in range(n)},
        scratch_shapes=[pltpu.SemaphoreType.DMA((n,)), pltpu.SemaphoreType.DMA((n,))])(*shards)


def _all_reduce_small(packet):
    rows, width = packet.shape
    n_dev = 8

    def body(x_ref, out_ref, gath, send_sems, recv_sems):
        x, y, c = _place()
        me, sibling = (x, y, c), (x, y, 1 - c)
        chips = _other_chips(x, y)

        def slot(px, py, pc):
            return gath.at[pl.ds((4 * px + 2 * py + pc) * rows, rows), :]

        def copy(k, block, to, src=None):
            return pltpu.make_async_remote_copy(
                src_ref=slot(*block) if src is None else src, dst_ref=slot(*block), send_sem=send_sems.at[k],
                recv_sem=recv_sems.at[k], device_id=to, device_id_type=MESH)

        first = [copy(0, me, sibling, src=x_ref)]
        first += [copy(1 + j, me, (*chip, c), src=x_ref) for j, chip in enumerate(chips)]
        for cp in first:
            cp.start()
        gath[pl.ds((4 * x + 2 * y + c) * rows, rows), :] = x_ref[...]
        passed = [copy(4 + j, (*chip, c), sibling) for j, chip in enumerate(chips)]
        for j, chip in enumerate(chips):
            copy(1 + j, (*chip, c), me).wait_recv()
            passed[j].start()
        copy(0, sibling, me).wait_recv()
        for j, chip in enumerate(chips):
            copy(4 + j, (*chip, 1 - c), me).wait_recv()
        for cp in first + passed:
            cp.wait_send()
        acc = gath[0:rows, :]
        for d in range(1, n_dev):
            acc = acc + gath[d * rows:(d + 1) * rows, :]
        out_ref[...] = acc

    return pl.pallas_call(
        body, name="all_reduce_small", out_shape=jax.ShapeDtypeStruct((rows, width), F32),
        in_specs=[pl.BlockSpec(memory_space=pltpu.VMEM)], out_specs=pl.BlockSpec(memory_space=pltpu.VMEM),
        scratch_shapes=[pltpu.VMEM((n_dev * rows, width), F32), pltpu.SemaphoreType.DMA((7,)),
                        pltpu.SemaphoreType.DMA((7,))])(packet)


def _flat_rows(parts, width, row_multiple):
    flat = jnp.concatenate([p.astype(F32).reshape(-1) for p in parts])
    rows = -(-flat.shape[0] // width)
    rows = -(-rows // row_multiple) * row_multiple
    return jnp.pad(flat, (0, rows * width - flat.shape[0])).reshape(rows, width)


def _unflatten(flat2d, shapes):
    flat = flat2d.reshape(-1)
    out, off = [], 0
    for shp in shapes:
        n = 1
        for dim in shp:
            n *= dim
        out.append(flat[off:off + n].reshape(shp))
        off += n
    return out


def _core_and_chip():
    x, y, c = _place()
    return jnp.stack([c, 2 * x + y]).astype(jnp.int32)


def _pair_sums(chip_major, names, call_name):
    views = [g.reshape(N_CHIPS, 2, g.shape[1] // 2, g.shape[2]) for g in chip_major]
    recv = _pair_exchange(views, call_name)
    sel = _core_and_chip()
    return [_pair_sum(v, r, sel, "pair_sum_" + nm) for v, r, nm in zip(views, recv, names)]


def _finish_grads(pairs, got, names):
    sel = _core_and_chip()
    return _join_halves([_chip_sum(p, g, sel, "chip_sum_" + nm) for (p, _), g, nm in zip(pairs, got, names)])


def kernel(x, g_mix, w_in, b_f, w_conv, g_conv_out, g_attn_out, w_o, g_ffn, w_up, w_ffn_conv, w_down, g_final, loss_target, m_g_mix, m_w_in, m_b_f, m_w_conv, m_g_conv_out, m_g_attn_out, m_w_o, m_g_ffn, m_w_up, m_w_ffn_conv, m_w_down, m_g_final, v_g_mix, v_w_in, v_b_f, v_w_conv, v_g_conv_out, v_g_attn_out, v_w_o, v_g_ffn, v_w_up, v_w_ffn_conv, v_w_down, v_g_final):
    s = x.shape[1]
    x0 = x[0]
    target = loss_target[0]
    d = D_MODEL
    x_pos, y_pos, _ = _place()
    my_chip = 2 * x_pos + y_pos

    (c_in,) = _all_gather_weights([w_in[0].astype(BF16)], [])
    w_in_full = jnp.concatenate([c_in[j] for j in range(N_CHIPS)], axis=1)
    c3 = 3 * CONV_CH
    w_a, w_b = w_in_full[:, :c3], w_in_full[:, c3:c3 + 3 * ATTN_W]
    w_c = jnp.pad(w_in_full[:, c3 + 3 * ATTN_W:], ((0, 0), (0, LANES - N_HEADS)))
    w_q, w_k, w_v = (w_b[:, i * ATTN_W:(i + 1) * ATTN_W] for i in range(3))
    b_pad = jnp.pad(b_f, ((0, 0), (0, LANES - N_HEADS)))

    h1 = _rms_fwd(x0, g_mix, "rms_mix")
    z_a, qkv, f_log, (c_o, c_up, c_conv, c_ffn) = _in_proj(
        h1, w_a, w_b, w_c, [w_o[0].astype(BF16), w_up[0].astype(BF16)], [w_conv[0], w_ffn_conv[0]])
    fb = _gate_fwd(f_log, b_pad)
    qx, kx, kxt, vx, vt, bounds, (c_down,) = _attn_prep(qkv, fb, [w_down[0].astype(BF16)])
    w_o_full = c_o.reshape(d, d)
    w_down_full = c_down.reshape(D_FF, d)
    w_conv_full = jnp.concatenate([c_conv[j] for j in range(N_CHIPS)], axis=1)
    w_ffn_full = jnp.concatenate([c_ffn[j] for j in range(N_CHIPS)], axis=1)
    n_up = c_up.shape[2]
    first_blk, last_blk = _key_block_ranges(bounds)
    o_attn, lse = _attn_fwd_t(qx, kx, vt, first_blk)
    mix = _mixer_fwd(z_a, o_attn, w_conv_full, g_conv_out, g_attn_out)
    x2 = _mm("nn", [mix], [w_o_full], F32, TM_MM, 512, "out_proj", add=x0)
    h2 = _rms_fwd(x2, g_ffn, "rms_ffn")
    up = _mm("nn", [h2], [c_up], F32, TM_MM, n_up, "up_proj", b_chips=True)
    act = _ffn_act_fwd(up, w_ffn_full)
    x3 = _mm("nn", [act], [w_down_full], F32, TM_MM, 512, "down_proj", add=x2)
    dx3, dx3_b, loss_row, gg_final = _loss_head(x3, target, g_final.reshape(1, d))

    dact = _mm("nt", [dx3_b], [w_down_full], F32, TM_MM, 1408, "d_act")
    gw_down = _mm_tn(act, dx3_b, 1408, 1024, "gw_down")
    dup, gwf_lin, gwf_gate = _ffn_act_bwd(up, dact, w_ffn_full)
    dh2 = _mm("nt", [(dup, j // 2, j % 2, n_up) for j in range(N_CHIPS)], [(c_up, j) for j in range(N_CHIPS)],
              F32, TM_MM, 512, "d_h2")
    gw_up = _mm_tn(h2, dup, 1024, n_up, "gw_up", out_chips=True)
    dx2, dx2_b, gg_ffn = _rms_bwd(x2, dh2, g_ffn, dx3, "rms_ffn_bwd", True)
    dmix = _mm("nt", [dx2_b], [w_o_full], F32, TM_MM, 512, "d_mix")
    gw_o = _mm_tn(mix, dx2_b, 1024, 1024, "gw_o")
    early = _pair_sums([gw_o.reshape(N_CHIPS, d // N_CHIPS, d), gw_up, gw_down.reshape(N_CHIPS, D_FF // N_CHIPS, d)],
                       ["w_o", "w_up", "w_down"], "pair_exchange")
    dz_a, dox, gw_conv, gg_conv_out, gg_attn_out, (got_o, got_down) = _mixer_bwd(
        z_a, o_attn, dmix, w_conv_full, g_conv_out, g_attn_out, [early[0][1], early[2][1]])
    dk, dv, dfk, dqt, dfq = _attn_bwd_t(qx, dox, kx, kxt, vx, lse, last_blk)
    dq = _attn_dq_finish(dqt)
    d_f = (jnp.transpose(dfk[:, :, 0:2, :], (1, 3, 0, 2)).reshape(s, N_HEADS)
           + jnp.transpose(jnp.sum(dfq, axis=2), (1, 2, 0)).reshape(s, N_HEADS))
    df_b, gb_f = _gate_bwd(f_log, b_pad, jnp.pad(d_f, ((0, 0), (0, LANES - N_HEADS))))
    dh1, (got_up,) = _mm("nt", [dz_a, dq, dk, dv, df_b], [w_a, w_q, w_k, w_v, w_c], F32, TM_MM, 512, "d_h1",
                         scatter=[early[1][1]])
    gw_a = _mm_tn(h1, dz_a, 1024, c3, "gw_in_conv")
    gw_q = _mm_tn(h1, dq, 1024, ATTN_W, "gw_in_q")
    gw_k = _mm_tn(h1, dk, 1024, ATTN_W, "gw_in_k")
    gw_v = _mm_tn(h1, dv, 1024, ATTN_W, "gw_in_v")
    gw_c = _mm_tn(h1, df_b, 1024, LANES, "gw_in_gate")
    gw_in = jnp.concatenate([gw_a, gw_q, gw_k, gw_v, gw_c[:, :N_HEADS]], axis=1)
    n_in = IN_COLS // N_CHIPS
    gw_in = jnp.stack([gw_in[:, j * n_in:(j + 1) * n_in] for j in range(N_CHIPS)])
    late = _pair_sums([gw_in], ["w_in"], "pair_exchange_w_in")
    grad_x, gg_mix, (got_in,) = _rms_bwd(x0, dh1, g_mix, dx2, "rms_mix_bwd", False, scatter=[late[0][1]])
    g_w_in, g_w_o, g_w_up, g_w_down = _finish_grads(late + early, [got_in, got_o, got_up, got_down],
                                                    ["w_in", "w_o", "w_up", "w_down"])

    gw_ffn = jnp.concatenate([gwf_lin, gwf_gate], axis=1)
    small_parts = [gg_mix, gg_conv_out, gg_attn_out, gg_ffn, gg_final, gb_f[:, :N_HEADS], loss_row[:, 0:1], gw_conv,
                   gw_ffn]
    small_shapes = [a.shape for a in small_parts]
    tot = _unflatten(_all_reduce_small(_flat_rows(small_parts, d, 8)), small_shapes)
    g_g_mix, g_g_conv_out, g_g_attn_out, g_g_ffn, g_g_final, g_b_f, loss_sum, g_conv_full, g_ffn_full = tot
    loss = loss_sum[0, 0]
    g_g_final = g_g_final[0]
    g_w_conv = lax.dynamic_slice_in_dim(g_conv_full, my_chip * (CONV_CH // N_CHIPS), CONV_CH // N_CHIPS, axis=1)
    g_w_ffn = lax.dynamic_slice_in_dim(g_ffn_full, my_chip * n_up, n_up, axis=1)

    def adam_big(w, g, m, v, name):
        dl, nm, nv = _adamw(w[0], g, m[0], v[0], name)
        return dl[None], nm[None], nv[None]

    u_w_in = adam_big(w_in, g_w_in, m_w_in, v_w_in, "adam_w_in")
    u_w_o = adam_big(w_o, g_w_o, m_w_o, v_w_o, "adam_w_o")
    u_w_up = adam_big(w_up, g_w_up, m_w_up, v_w_up, "adam_w_up")
    u_w_down = adam_big(w_down, g_w_down, m_w_down, v_w_down, "adam_w_down")

    small_w = [g_mix, b_f, g_conv_out, g_attn_out, g_ffn, g_final, w_conv, w_ffn_conv]
    small_g = [g_g_mix, g_b_f, g_g_conv_out, g_g_attn_out, g_g_ffn, g_g_final, g_w_conv, g_w_ffn]
    small_m = [m_g_mix, m_b_f, m_g_conv_out, m_g_attn_out, m_g_ffn, m_g_final, m_w_conv, m_w_ffn_conv]
    small_v = [v_g_mix, v_b_f, v_g_conv_out, v_g_attn_out, v_g_ffn, v_g_final, v_w_conv, v_w_ffn_conv]
    shapes = [a.shape for a in small_w]
    pack = lambda arrs: _flat_rows(arrs, LANES, 8)
    sd, sm, sv = _adamw(pack(small_w), pack(small_g), pack(small_m), pack(small_v), "adam_small")
    sd, sm, sv = _unflatten(sd, shapes), _unflatten(sm, shapes), _unflatten(sv, shapes)
    (d_g_mix, d_b_f, d_g_conv_out, d_g_attn_out, d_g_ffn, d_g_final, d_w_conv, d_w_ffn) = sd
    (nm_g_mix, nm_b_f, nm_g_conv_out, nm_g_attn_out, nm_g_ffn, nm_g_final, nm_w_conv, nm_w_ffn) = sm
    (nv_g_mix, nv_b_f, nv_g_conv_out, nv_g_attn_out, nv_g_ffn, nv_g_final, nv_w_conv, nv_w_ffn) = sv

    grads = (g_g_mix, g_w_in[None], g_b_f, g_w_conv[None], g_g_conv_out, g_g_attn_out, g_w_o[None], g_g_ffn,
             g_w_up[None], g_w_ffn[None], g_w_down[None], g_g_final)
    deltas = (d_g_mix, u_w_in[0], d_b_f, d_w_conv, d_g_conv_out, d_g_attn_out, u_w_o[0], d_g_ffn, u_w_up[0],
              d_w_ffn, u_w_down[0], d_g_final)
    new_m = (nm_g_mix, u_w_in[1], nm_b_f, nm_w_conv, nm_g_conv_out, nm_g_attn_out, u_w_o[1], nm_g_ffn, u_w_up[1],
             nm_w_ffn, u_w_down[1], nm_g_final)
    new_v = (nv_g_mix, u_w_in[2], nv_b_f, nv_w_conv, nv_g_conv_out, nv_g_attn_out, u_w_o[2], nv_g_ffn, u_w_up[2],
             nv_w_ffn, u_w_down[2], nv_g_final)
    return (loss, grad_x[None], *grads, *deltas, *new_m, *new_v)
```

```python
import jax
import jax.numpy as jnp
from jax import lax
from jax.experimental import pallas as pl
from jax.experimental.pallas import tpu as pltpu

F32, BF16 = jnp.float32, jnp.bfloat16
MESH = pl.DeviceIdType.MESH

D_MODEL = 1024
CONV_CH = 512
ATTN_W = 512
N_HEADS = 8
HEAD_DIM = 64
N_PAIRS = N_HEADS // 2
D_FF = 2816
IN_COLS = 3 * CONV_CH + 3 * ATTN_W + N_HEADS
EPS = 1e-6
Q_SCALE = 0.125
EXP_ZERO = 104.0
N_CHIPS = 4
LANES = 128
HALO = 8

ADAM_LR, ADAM_B1, ADAM_B2, ADAM_EPS, ADAM_WD, ADAM_STEP = 0.001, 0.9, 0.999, 1e-08, 0.01, 10

TM_ROWS = 512
TM_MM = 1024
TK_TN = 1024
TQ = 512
TM_FFN = 1024
TN_FFN = 256
VMEM_LIMIT = 52 * 2**20


def _cp(sem, vmem=VMEM_LIMIT):
    return pltpu.CompilerParams(dimension_semantics=sem, vmem_limit_bytes=vmem)


def _bf(a):
    return a if a.dtype == BF16 else a.astype(BF16)


def _mm(mode, a_list, b_list, out_dtype, tm, tn, name, add=None, b_chips=False, scatter=(), norm_g=None):
    n_p = len(a_list)
    a0 = a_list[0]
    m_dim = a0[0].shape[1] if isinstance(a0, tuple) else a0.shape[0]
    b0 = b_list[0]
    if b_chips:
        n_dim = b0.shape[0] * b0.shape[2]
        assert tn == b0.shape[2] and mode == "nn"
    else:
        b0 = b0[0][b0[1]] if isinstance(b0, tuple) else b0
        n_dim = b0.shape[1 if mode == "nn" else 0]
    tm, tn = min(tm, m_dim), min(tn, n_dim)
    assert m_dim % tm == 0 and n_dim % tn == 0
    dims = (((1,), (0,)), ((), ())) if mode == "nn" else (((1,), (1,)), ((), ()))
    in_specs, args = [], []
    for a in a_list:
        if isinstance(a, tuple):
            arr, lead, col, width = a
            in_specs.append(pl.BlockSpec((None, tm, width), lambda m, n, lead=lead, col=col: (lead, m, col)))
        else:
            arr = a
            in_specs.append(pl.BlockSpec((tm, a.shape[1]), lambda m, n: (m, 0)))
        args.append(arr)
    for b in b_list:
        if b_chips:
            arr = b
            in_specs.append(pl.BlockSpec((None, b.shape[1], tn), lambda m, n: (n, 0, 0)))
        elif isinstance(b, tuple):
            arr, lead = b
            if mode == "nn":
                in_specs.append(pl.BlockSpec((None, arr.shape[1], tn), lambda m, n, lead=lead: (lead, 0, n)))
            else:
                in_specs.append(pl.BlockSpec((None, tn, arr.shape[2]), lambda m, n, lead=lead: (lead, n, 0)))
        elif mode == "nn":
            arr = b
            in_specs.append(pl.BlockSpec((b.shape[0], tn), lambda m, n: (0, n)))
        else:
            arr = b
            in_specs.append(pl.BlockSpec((tn, b.shape[1]), lambda m, n: (n, 0)))
        args.append(arr)
    if add is not None:
        in_specs.append(pl.BlockSpec((tm, tn), lambda m, n: (m, n)))
        args.append(add)
    if norm_g is not None:
        assert tn == n_dim and not scatter
        in_specs.append(pl.BlockSpec((1, tn), lambda m, n: (0, 0)))
        args.append(norm_g)

    n_in = len(args)
    n_sc = len(scatter)
    grid = (m_dim // tm, n_dim // tn)

    def body(*refs):
        o_ref = refs[n_in + n_sc]
        if n_sc:
            finish = _hosted_scatter(refs[n_in:n_in + n_sc], refs[n_in + n_sc + 1:n_in + 2 * n_sc + 1],
                                     refs[n_in + 2 * n_sc + 1:], pl.program_id(0) * grid[1] + pl.program_id(1),
                                     grid[0] * grid[1])
        acc = None
        for i in range(n_p):
            d = lax.dot_general(_bf(refs[i][...]), _bf(refs[n_p + i][...]), dims,
                                preferred_element_type=F32)
            acc = d if acc is None else acc + d
        if add is not None:
            acc = refs[2 * n_p][...] + acc
        o_ref[...] = acc.astype(out_dtype)
        if norm_g is not None:
            refs[n_in + 1][...] = (acc * _rstd(acc) * refs[n_in - 1][...]).astype(BF16)
        if n_sc:
            finish()

    main_spec = pl.BlockSpec((tm, tn), lambda m, n: (m, n))
    main_shape = jax.ShapeDtypeStruct((m_dim, n_dim), out_dtype)
    if norm_g is not None:
        return pl.pallas_call(body, name=name, grid=grid, in_specs=in_specs, out_specs=[main_spec, main_spec],
                              out_shape=[main_shape, jax.ShapeDtypeStruct((m_dim, n_dim), BF16)],
                              compiler_params=_cp(("parallel", "parallel")))(*args)
    if not n_sc:
        return pl.pallas_call(body, name=name, grid=grid, in_specs=in_specs, out_specs=main_spec,
                              out_shape=main_shape, compiler_params=_cp(("parallel", "parallel")))(*args)
    got_shapes, sems = _scatter_operands(scatter)
    outs = pl.pallas_call(
        body, name=name, grid=grid, in_specs=in_specs + _hbm_specs(n_sc), out_specs=[main_spec] + _hbm_specs(n_sc),
        out_shape=[main_shape] + got_shapes, scratch_shapes=sems,
        compiler_params=_cp(("arbitrary", "arbitrary")))(*args, *scatter)
    return outs[0], outs[1:]


def _mm_tn(a, b, tm, tn, name, out_chips=False):
    k_dim, m_dim = a.shape
    n_dim = b.shape[-1] * (b.shape[0] if b.ndim == 3 else 1)
    tm, tn, tk = min(tm, m_dim), min(tn, b.shape[-1]), min(TK_TN, k_dim)
    assert m_dim % tm == 0 and b.shape[-1] % tn == 0 and k_dim % tk == 0
    per = b.shape[-1] // tn
    if b.ndim == 3:
        b_spec = pl.BlockSpec((None, tk, tn), lambda m, n, k: (n // per, k, n % per))
    else:
        b_spec = pl.BlockSpec((tk, tn), lambda m, n, k: (k, n))

    def body(a_ref, b_ref, o_ref):
        @pl.when(pl.program_id(2) == 0)
        def _():
            o_ref[...] = jnp.zeros_like(o_ref)
        o_ref[...] += lax.dot_general(_bf(a_ref[...]), _bf(b_ref[...]), (((0,), (0,)), ((), ())),
                                      preferred_element_type=F32)

    return pl.pallas_call(
        body, name=name, grid=(m_dim // tm, n_dim // tn, k_dim // tk),
        in_specs=[pl.BlockSpec((tk, tm), lambda m, n, k: (k, m)), b_spec],
        out_specs=(pl.BlockSpec((None, tm, tn), lambda m, n, k: (n, m, 0)) if out_chips
                   else pl.BlockSpec((tm, tn), lambda m, n, k: (m, n))),
        out_shape=jax.ShapeDtypeStruct((n_dim // tn, m_dim, tn) if out_chips else (m_dim, n_dim), F32),
        compiler_params=_cp(("parallel", "parallel", "arbitrary")))(a, b)


def _rstd(x):
    return lax.rsqrt(jnp.mean(x * x, axis=-1, keepdims=True) + EPS)


def _rms_bwd(x, dh, g, dres, name, with_bf16, scatter=()):
    s, d = x.shape
    tm = min(TM_ROWS, s)
    n_sc = len(scatter)
    n_out = 3 if with_bf16 else 2
    got_shapes, sems = _scatter_operands(scatter) if n_sc else ([], [])

    def body(x_ref, dh_ref, g_ref, dres_ref, *rest):
        dx_ref, gg_ref = rest[n_sc], rest[n_sc + n_out - 1]
        i = pl.program_id(0)
        if n_sc:
            finish = _hosted_scatter(rest[:n_sc], rest[n_sc + n_out:2 * n_sc + n_out], rest[2 * n_sc + n_out:], i,
                                     s // tm)

        @pl.when(i == 0)
        def _():
            gg_ref[...] = jnp.zeros_like(gg_ref)

        xv = x_ref[...]
        xn = xv * _rstd(xv)
        dhv = dh_ref[...]
        gg_ref[...] += jnp.sum(dhv * xn, axis=0, keepdims=True)
        t = dhv * g_ref[...]
        dx = dres_ref[...] + _rstd(xv) * (t - xn * jnp.mean(t * xn, axis=-1, keepdims=True))
        dx_ref[...] = dx
        if with_bf16:
            rest[n_sc + 1][...] = dx.astype(BF16)
        if n_sc:
            finish()

    row = pl.BlockSpec((tm, d), lambda i: (i, 0))
    vec = pl.BlockSpec((1, d), lambda i: (0, 0))
    out_specs = [row] + ([row] if with_bf16 else []) + [vec] + _hbm_specs(n_sc)
    out_shape = ([jax.ShapeDtypeStruct((s, d), F32)] + ([jax.ShapeDtypeStruct((s, d), BF16)] if with_bf16 else [])
                 + [jax.ShapeDtypeStruct((1, d), F32)] + got_shapes)
    outs = pl.pallas_call(
        body, name=name, grid=(s // tm,), in_specs=[row, row, vec, row] + _hbm_specs(n_sc), out_specs=out_specs,
        out_shape=out_shape, scratch_shapes=sems, compiler_params=_cp(("arbitrary",)))(x, dh, g, dres, *scatter)
    return tuple(outs[:n_out]) + ((outs[n_out:],) if n_sc else ())


def _loss_head(x3, target, g):
    s, d = x3.shape
    tm = min(TM_ROWS, s)

    def body(x_ref, t_ref, g_ref, dx_ref, dxb_ref, loss_ref, gg_ref):
        @pl.when(pl.program_id(0) == 0)
        def _():
            gg_ref[...] = jnp.zeros_like(gg_ref)
            loss_ref[...] = jnp.zeros_like(loss_ref)

        xv = x_ref[...]
        r = _rstd(xv)
        xn = xv * r
        gv = g_ref[...]
        err = xn * gv - t_ref[...]
        loss_ref[...] += 0.5 * jnp.sum(jnp.mean(err * err, axis=-1, keepdims=True), axis=0, keepdims=True)
        dy = err * (1.0 / d)
        gg_ref[...] += jnp.sum(dy * xn, axis=0, keepdims=True)
        t = dy * gv
        dx = r * (t - xn * jnp.mean(t * xn, axis=-1, keepdims=True))
        dx_ref[...] = dx
        dxb_ref[...] = dx.astype(BF16)

    row = pl.BlockSpec((tm, d), lambda i: (i, 0))
    vec = pl.BlockSpec((1, d), lambda i: (0, 0))
    return pl.pallas_call(
        body, name="loss_head", grid=(s // tm,), in_specs=[row, row, vec],
        out_specs=[row, row, pl.BlockSpec((1, LANES), lambda i: (0, 0)), vec],
        out_shape=[jax.ShapeDtypeStruct((s, d), F32), jax.ShapeDtypeStruct((s, d), BF16),
                   jax.ShapeDtypeStruct((1, LANES), F32), jax.ShapeDtypeStruct((1, d), F32)],
        compiler_params=_cp(("arbitrary",)))(x3, target, g)


def _prev_halo_spec(tm, width, col):
    return pl.BlockSpec((HALO, width), lambda i, *_: (jnp.maximum(i * (tm // HALO) - 1, 0), col))


def _next_halo_spec(tm, width, col, s):
    return pl.BlockSpec((HALO, width), lambda i, *_: (jnp.minimum((i + 1) * (tm // HALO), s // HALO - 1), col))


def _shift_down(x, k):
    return pltpu.roll(x, k, 0)


def _shift_up(x, k):
    return pltpu.roll(x, x.shape[0] - k, 0)


def _conv_taps(x_ext, w):
    return w[0:1, :] * _shift_down(x_ext, 2) + w[1:2, :] * _shift_down(x_ext, 1) + w[2:3, :] * x_ext


def _conv_taps_t(d_ext, w):
    return w[2:3, :] * d_ext + w[1:2, :] * _shift_up(d_ext, 1) + w[0:1, :] * _shift_up(d_ext, 2)


def _mixer_fwd(z_a, o_attn, w_conv, g_conv_out, g_attn_out):
    s = z_a.shape[0]
    c = CONV_CH
    tm = min(TM_ROWS, s)

    def body(gb_ref, gc_ref, xc_ref, gcp_ref, xcp_ref, o_ref, w_ref, gco_ref, gao_ref, mix_ref):
        i = pl.program_id(0)
        cx = gc_ref[...] * xc_ref[...]
        cx_prev = jnp.where(i > 0, gcp_ref[...] * xcp_ref[...], 0.0)
        conv = _conv_taps(jnp.concatenate([cx_prev, cx], axis=0), w_ref[...])[HALO:]
        y = gb_ref[...] * conv
        mix_ref[:, 0:c] = (y * _rstd(y) * gco_ref[...]).astype(BF16)
        o = o_ref[...]
        mix_ref[:, c:2 * c] = (o * _rstd(o) * gao_ref[...]).astype(BF16)

    col = lambda j: pl.BlockSpec((tm, c), lambda i: (i, j))
    vec = pl.BlockSpec((1, c), lambda i: (0, 0))
    return pl.pallas_call(
        body, name="mixer_fwd", grid=(s // tm,),
        in_specs=[col(0), col(1), col(2), _prev_halo_spec(tm, c, 1), _prev_halo_spec(tm, c, 2), col(0),
                  pl.BlockSpec((3, c), lambda i: (0, 0)), vec, vec],
        out_specs=pl.BlockSpec((tm, 2 * c), lambda i: (i, 0)),
        out_shape=jax.ShapeDtypeStruct((s, 2 * c), BF16),
        compiler_params=_cp(("parallel",)))(z_a, z_a, z_a, z_a, z_a, o_attn, w_conv, g_conv_out, g_attn_out)


def _mixer_bwd(z_a, o_attn, dmix, w_conv, g_conv_out, g_attn_out, scatter):
    s = z_a.shape[0]
    c = CONV_CH
    tm = min(TM_ROWS, s)
    n_blk = s // tm
    n_sc = len(scatter)
    got_shapes, sems = _scatter_operands(scatter)

    def body(gb_ref, gc_ref, xc_ref, gcp_ref, xcp_ref, gbn_ref, gcn_ref, xcn_ref, o_ref, dnc_ref, dncn_ref, dna_ref,
             w_ref, gco_ref, gao_ref, *rest):
        dz_ref, dox_ref, gw_ref, ggco_ref, ggao_ref = rest[n_sc:n_sc + 5]
        i = pl.program_id(0)
        finish = _hosted_scatter(rest[:n_sc], rest[n_sc + 5:2 * n_sc + 5], rest[2 * n_sc + 5:], i, n_blk)

        @pl.when(i == 0)
        def _():
            gw_ref[...] = jnp.zeros_like(gw_ref)
            ggco_ref[...] = jnp.zeros_like(ggco_ref)
            ggao_ref[...] = jnp.zeros_like(ggao_ref)

        w = w_ref[...]
        zeros = jnp.zeros((HALO, c), F32)
        gb_e = jnp.concatenate([zeros, gb_ref[...], gbn_ref[...]], axis=0)
        cx_prev = jnp.where(i > 0, gcp_ref[...] * xcp_ref[...], 0.0)
        gc_e = jnp.concatenate([zeros, gc_ref[...], gcn_ref[...]], axis=0)
        xc_e = jnp.concatenate([zeros, xc_ref[...], xcn_ref[...]], axis=0)
        cx_e = jnp.concatenate([cx_prev, gc_ref[...] * xc_ref[...], gcn_ref[...] * xcn_ref[...]], axis=0)
        dn_next = jnp.where(i < n_blk - 1, dncn_ref[...], 0.0)
        dn_e = jnp.concatenate([zeros, dnc_ref[...], dn_next], axis=0)

        conv_e = _conv_taps(cx_e, w)
        y_e = gb_e * conv_e
        r_e = _rstd(y_e)
        yn_e = y_e * r_e
        t_e = dn_e * gco_ref[...]
        dy_e = r_e * (t_e - yn_e * jnp.mean(t_e * yn_e, axis=-1, keepdims=True))
        dconv_e = dy_e * gb_e
        dcx_e = _conv_taps_t(dconv_e, w)
        blk = slice(HALO, HALO + tm)
        dz_ref[:, 0:c] = (dy_e * conv_e)[blk].astype(BF16)
        dz_ref[:, c:2 * c] = (dcx_e * xc_e)[blk].astype(BF16)
        dz_ref[:, 2 * c:3 * c] = (dcx_e * gc_e)[blk].astype(BF16)
        ggco_ref[...] += jnp.sum((dn_e * yn_e)[blk], axis=0, keepdims=True)
        dconv = dconv_e[blk]
        gw_ref[0:1, :] += jnp.sum(dconv * _shift_down(cx_e, 2)[blk], axis=0, keepdims=True)
        gw_ref[1:2, :] += jnp.sum(dconv * _shift_down(cx_e, 1)[blk], axis=0, keepdims=True)
        gw_ref[2:3, :] += jnp.sum(dconv * cx_e[blk], axis=0, keepdims=True)

        o = o_ref[...]
        ra = _rstd(o)
        on = o * ra
        dna = dna_ref[...]
        ggao_ref[...] += jnp.sum(dna * on, axis=0, keepdims=True)
        ta = dna * gao_ref[...]
        do = ra * (ta - on * jnp.mean(ta * on, axis=-1, keepdims=True))
        prod = do * o
        lane = lax.broadcasted_iota(jnp.int32, (tm, LANES), 1)
        head_a = lane < HEAD_DIM
        for p in range(N_PAIRS):
            cols = slice(p * LANES, (p + 1) * LANES)
            pb, dob = prod[:, cols], do[:, cols]
            for hh in range(2):
                sel = head_a if hh == 0 else jnp.logical_not(head_a)
                delta = jnp.sum(jnp.where(sel, pb, 0.0), axis=-1, keepdims=True)
                neg3 = _split3(-delta)
                do_h = pltpu.roll(dob, HEAD_DIM, 1) if hh else dob
                dox_ref[2 * p + hh] = _aug(do_h, lane, neg3).astype(BF16)
        finish()

    col = lambda j: pl.BlockSpec((tm, c), lambda i: (i, j))
    vec = pl.BlockSpec((1, c), lambda i: (0, 0))
    w3 = pl.BlockSpec((3, c), lambda i: (0, 0))
    outs = pl.pallas_call(
        body, name="mixer_bwd", grid=(n_blk,),
        in_specs=[col(0), col(1), col(2), _prev_halo_spec(tm, c, 1), _prev_halo_spec(tm, c, 2),
                  _next_halo_spec(tm, c, 0, s), _next_halo_spec(tm, c, 1, s), _next_halo_spec(tm, c, 2, s),
                  col(0), col(0), _next_halo_spec(tm, c, 0, s), col(1), w3, vec, vec] + _hbm_specs(n_sc),
        out_specs=[pl.BlockSpec((tm, 3 * c), lambda i: (i, 0)),
                   pl.BlockSpec((N_HEADS, tm, LANES), lambda i: (0, i, 0)), w3, vec, vec] + _hbm_specs(n_sc),
        out_shape=[jax.ShapeDtypeStruct((s, 3 * c), BF16), jax.ShapeDtypeStruct((N_HEADS, s, LANES), BF16),
                   jax.ShapeDtypeStruct((3, c), F32), jax.ShapeDtypeStruct((1, c), F32),
                   jax.ShapeDtypeStruct((1, c), F32)] + got_shapes,
        scratch_shapes=sems, compiler_params=_cp(("arbitrary",)))(
            z_a, z_a, z_a, z_a, z_a, z_a, z_a, z_a, o_attn, dmix, dmix, dmix, w_conv, g_conv_out, g_attn_out,
            *scatter)
    return tuple(outs[:5]) + (outs[5:],)


def _gate_fwd(f, b_pad):
    s = f.shape[0]
    tm = min(TQ, s)

    def body(f_ref, b_ref, fb_ref, carry):
        @pl.when(pl.program_id(0) == 0)
        def _():
            carry[...] = jnp.zeros_like(carry)

        z = f_ref[...] + b_ref[...]
        x = jnp.minimum(z, 0.0) - jnp.log1p(jnp.exp(-jnp.abs(z)))
        row = lax.broadcasted_iota(jnp.int32, (tm, LANES), 0)
        sh = 1
        while sh < tm:
            x = x + jnp.where(row >= sh, _shift_down(x, sh), 0.0)
            sh *= 2
        x = x + carry[0:1, :]
        carry[...] = jnp.broadcast_to(x[tm - 1:tm, :], carry.shape)
        head_a = lax.broadcasted_iota(jnp.int32, (tm, LANES), 1) < HEAD_DIM
        for p in range(N_PAIRS):
            fa = jnp.broadcast_to(x[:, 2 * p:2 * p + 1], (tm, LANES))
            fbv = jnp.broadcast_to(x[:, 2 * p + 1:2 * p + 2], (tm, LANES))
            fb_ref[:, p * LANES:(p + 1) * LANES] = jnp.where(head_a, fa, fbv)

    return pl.pallas_call(
        body, name="gate_fwd", grid=(s // tm,),
        in_specs=[pl.BlockSpec((tm, LANES), lambda i: (i, 0)), pl.BlockSpec((1, LANES), lambda i: (0, 0))],
        out_specs=pl.BlockSpec((tm, N_PAIRS * LANES), lambda i: (i, 0)),
        out_shape=jax.ShapeDtypeStruct((s, N_PAIRS * LANES), F32),
        scratch_shapes=[pltpu.VMEM((HALO, LANES), F32)],
        compiler_params=_cp(("arbitrary",)))(f, b_pad)


def _gate_bwd(f, b_pad, d_f):
    s = f.shape[0]
    tm = min(TQ, s)
    n_blk = s // tm

    def body(f_ref, b_ref, d_ref, df_ref, gb_ref, carry):
        @pl.when(pl.program_id(0) == 0)
        def _():
            carry[...] = jnp.zeros_like(carry)
            gb_ref[...] = jnp.zeros_like(gb_ref)

        x = d_ref[...]
        row = lax.broadcasted_iota(jnp.int32, (tm, LANES), 0)
        sh = 1
        while sh < tm:
            x = x + jnp.where(row < tm - sh, _shift_up(x, sh), 0.0)
            sh *= 2
        x = x + carry[0:1, :]
        carry[...] = jnp.broadcast_to(x[0:1, :], carry.shape)
        z = f_ref[...] + b_ref[...]
        d = x * (1.0 / (1.0 + jnp.exp(z)))
        df_ref[...] = d.astype(BF16)
        gb_ref[...] += jnp.sum(d, axis=0, keepdims=True)

    rev = pl.BlockSpec((tm, LANES), lambda i: (n_blk - 1 - i, 0))
    vec = pl.BlockSpec((1, LANES), lambda i: (0, 0))
    return pl.pallas_call(
        body, name="gate_bwd", grid=(n_blk,), in_specs=[rev, vec, rev], out_specs=[rev, vec],
        out_shape=[jax.ShapeDtypeStruct((s, LANES), BF16), jax.ShapeDtypeStruct((1, LANES), F32)],
        scratch_shapes=[pltpu.VMEM((HALO, LANES), F32)],
        compiler_params=_cp(("arbitrary",)))(f, b_pad, d_f)


_NT = (((1,), (1,)), ((), ()))
_NN = (((1,), (0,)), ((), ()))


AUG = HEAD_DIM
NORM_MARGIN = 1.01


def _split3(x):
    hi = x.astype(BF16).astype(F32)
    r = x - hi
    mid = r.astype(BF16).astype(F32)
    lo = (r - mid).astype(BF16).astype(F32)
    return hi, mid, lo


def _aug(base, lane, vals):
    out = jnp.where(lane < AUG, base, 0.0)
    for k, v in enumerate(vals):
        out = jnp.where(lane == AUG + k, v, out)
    return out


def _attn_prep(qkv, fb, bigs):
    s = qkv.shape[0]
    tq = min(TQ, s)
    n_q = s // tq

    n = len(bigs)
    arrays, landing, sems = _gather_operands(bigs, [])

    def body(q_ref, k_ref, v_ref, fb_ref, *rest):
        qx_ref, kx_ref, kxt_ref, vx_ref, vt_ref, b_ref = rest[2 * n:2 * n + 6]
        finish = _hosted_gather((rest[:n], rest[2 * n + 6:3 * n + 6]) + tuple(rest[3 * n + 6:]), n, n,
                                pl.program_id(0), n_q)
        lane = lax.broadcasted_iota(jnp.int32, (tq, LANES), 1)
        lane8 = lax.broadcasted_iota(jnp.int32, (HALO, LANES), 1)
        head_lanes = lane < AUG
        is_lane = [lane == AUG + k for k in range(6)]
        first3 = (lane >= AUG) & (lane < AUG + 3)
        next3 = (lane >= AUG + 3) & (lane < AUG + 6)
        q_const = jnp.where(first3, -1.0, 0.0)
        k_const = jnp.where(next3, 1.0, 0.0)
        v_const = jnp.where(first3, 1.0, 0.0)
        ones_head = (lax.broadcasted_iota(jnp.int32, (LANES, LANES), 0) < HEAD_DIM).astype(BF16)
        acc = jnp.zeros((HALO, LANES), F32)
        for p in range(N_PAIRS):
            cols = slice(p * LANES, (p + 1) * LANES)
            q2, k2, v2 = (ref[:, cols].astype(F32) for ref in (q_ref, k_ref, v_ref))
            q2 = q2 * Q_SCALE
            f2 = fb_ref[:, cols]
            for hh in range(2):
                h = 2 * p + hh
                q, k, v = ((pltpu.roll(x, HEAD_DIM, 1) if hh else x) for x in (q2, k2, v2))
                f = f2 if hh else pltpu.roll(f2, HEAD_DIM, 1)
                hi, mid, lo = _split3(f)
                q_aug = jnp.where(is_lane[3], hi, jnp.where(is_lane[4], mid, jnp.where(is_lane[5], lo, q_const)))
                k_aug = jnp.where(is_lane[0], hi, jnp.where(is_lane[1], mid, jnp.where(is_lane[2], lo, k_const)))
                kx = jnp.where(head_lanes, k, k_aug)
                vx = jnp.where(head_lanes, v, v_const)
                qx_ref[h] = jnp.where(head_lanes, q, q_aug).astype(BF16)
                kx_ref[h] = kx.astype(BF16)
                vx_ref[h] = vx.astype(BF16)
                kxt_ref[h, 0] = kx.T.astype(BF16)
                vt_ref[h, 0] = vx.T.astype(BF16)
                q_sq = lax.dot_general((q * q).astype(BF16), ones_head, _NN, preferred_element_type=F32)
                k_sq = lax.dot_general((k * k).astype(BF16), ones_head, _NN, preferred_element_type=F32)
                qk = jnp.sqrt(q_sq * k_sq) * NORM_MARGIN + f
                vals = (jnp.sqrt(jnp.max(q_sq, axis=0, keepdims=True)), jnp.sqrt(jnp.max(k_sq, axis=0, keepdims=True)),
                        jnp.max(qk, axis=0, keepdims=True), f[tq - 1:tq, :])
                for slot, val in enumerate(vals):
                    acc = jnp.where(lane8 == slot * N_HEADS + h, val[:, AUG:AUG + 1], acc)
        b_ref[0] = acc
        finish()

    blk = lambda j: pl.BlockSpec((tq, ATTN_W), lambda i: (i, j))
    rows = pl.BlockSpec((N_HEADS, tq, LANES), lambda i: (0, i, 0))
    cols_t = pl.BlockSpec((N_HEADS, 1, LANES, tq), lambda i: (0, i, 0, 0))
    shp = jax.ShapeDtypeStruct((N_HEADS, s, LANES), BF16)
    shp_t = jax.ShapeDtypeStruct((N_HEADS, n_q, LANES, tq), BF16)
    outs = pl.pallas_call(
        body, name="attn_prep", grid=(n_q,), in_specs=[blk(0), blk(1), blk(2), blk(0)] + _hbm_specs(2 * n),
        out_specs=[rows, rows, cols_t, rows, cols_t,
                   pl.BlockSpec((1, HALO, LANES), lambda i: (i, 0, 0))] + _hbm_specs(n),
        out_shape=[shp, shp, shp_t, shp, shp_t, jax.ShapeDtypeStruct((n_q, HALO, LANES), F32)]
        + [jax.ShapeDtypeStruct(b.shape, b.dtype) for b in landing],
        input_output_aliases={4 + n + k: 6 + k for k in range(n)}, scratch_shapes=sems,
        compiler_params=_cp(("arbitrary",)))(qkv, qkv, qkv, fb, *arrays, *landing)
    return tuple(outs[:6]) + (outs[6:],)


def _key_block_ranges(bounds):
    t = bounds[:, 0, :]
    nh = N_HEADS
    a, b, c, e = t[:, 0:nh], t[:, nh:2 * nh], t[:, 2 * nh:3 * nh], t[:, 3 * nh:4 * nh]
    bound = a[:, None, :] * b[None, :, :] * NORM_MARGIN + c[:, None, :] - e[None, :, :]
    n_q = t.shape[0]
    idx = jnp.arange(n_q)
    need = jnp.logical_not(bound < -(EXP_ZERO + 2.0)) | (idx[None, :, None] >= idx[:, None, None])
    first = jnp.argmax(need, axis=1).astype(jnp.int32)
    first = jnp.min(first.reshape(n_q, N_PAIRS, 2), axis=-1)
    visits = (first[:, None, :] <= idx[None, :, None]) & (idx[None, :, None] <= idx[:, None, None])
    last = jnp.max(jnp.where(visits, idx[:, None, None], 0), axis=0).astype(jnp.int32)
    return first.T.reshape(-1), last.T.reshape(-1)


def _attn_fwd_t(qx, kx, vt, first_blk):
    _, s, _ = qx.shape
    tq = min(TQ, s)
    n_q = s // tq
    neg = -1e30

    def body(first_ref, qx_ref, kx_ref, vt_ref, o_ref, lse_ref, acc_ref, m_ref, s_even, s_odd):
        p = pl.program_id(0)
        i = pl.program_id(1)
        acc_ref[...] = jnp.zeros(acc_ref.shape, F32)
        m_ref[...] = jnp.full(m_ref.shape, neg, F32)
        key_le_query = (lax.broadcasted_iota(jnp.int32, (tq, tq), 0) <= lax.broadcasted_iota(jnp.int32, (tq, tq), 1))
        first = first_ref[p * n_q + i]

        def scores(kb, hh, dst):
            rows_k = pl.ds(pl.multiple_of(kb * tq, tq), tq)
            dst[hh] = lax.dot_general(kx_ref[hh, rows_k, :], qx_ref[hh], _NT, preferred_element_type=F32)

        def step(kb, src, nxt):
            for hh in range(2):
                st = src[hh]
                if nxt is None:
                    st = jnp.where(key_le_query, st, -jnp.inf)
                else:
                    scores(kb + 1, hh, nxt)
                m_old = m_ref[hh]
                m_new = jnp.maximum(m_old, jnp.max(st, axis=0, keepdims=True))
                m_ref[hh] = m_new
                pt = jnp.exp(st - m_new).astype(BF16)
                acc_ref[hh] = acc_ref[hh] * jnp.exp(m_old - m_new) + lax.dot_general(
                    vt_ref[hh, kb], pt, _NN, preferred_element_type=F32)

        def by_parity(kb, fn):
            @pl.when(kb % 2 == 0)
            def _():
                fn(s_even, s_odd)

            @pl.when(kb % 2 == 1)
            def _():
                fn(s_odd, s_even)

        def first_scores(src, nxt):
            scores(first, 0, src)
            scores(first, 1, src)

        def unmasked(kb, carry):
            by_parity(kb, lambda src, nxt: step(kb, src, nxt))
            return carry

        by_parity(first, first_scores)
        lax.fori_loop(first, i, unmasked, 0)
        by_parity(i, lambda src, nxt: step(i, src, None))
        outs, lses = [], []
        for hh in range(2):
            acc = acc_ref[hh]
            l = acc[AUG:AUG + 1, :]
            outs.append(acc[0:HEAD_DIM, :] / l)
            lses.append(m_ref[hh] + jnp.log(l))
        o_ref[...] = jnp.concatenate(outs, axis=0).T
        rows8 = lax.broadcasted_iota(jnp.int32, (N_HEADS, tq), 0)
        lse_ref[0, 0] = jnp.where(rows8 == 0, lses[0], jnp.where(rows8 == 1, lses[1], 0.0))

    grid_spec = pltpu.PrefetchScalarGridSpec(
        num_scalar_prefetch=1, grid=(N_PAIRS, n_q),
        in_specs=[pl.BlockSpec((2, tq, LANES), lambda p, i, first: (p, i, 0)),
                  pl.BlockSpec((2, s, LANES), lambda p, i, first: (p, 0, 0)),
                  pl.BlockSpec((2, n_q, LANES, tq), lambda p, i, first: (p, 0, 0, 0))],
        out_specs=[pl.BlockSpec((tq, LANES), lambda p, i, first: (i, p)),
                   pl.BlockSpec((1, 1, N_HEADS, tq), lambda p, i, first: (p, i, 0, 0))],
        scratch_shapes=[pltpu.VMEM((2, LANES, tq), F32), pltpu.VMEM((2, 1, tq), F32),
                        pltpu.VMEM((2, tq, tq), F32), pltpu.VMEM((2, tq, tq), F32)])
    return pl.pallas_call(
        body, name="attn_fwd", grid_spec=grid_spec,
        out_shape=[jax.ShapeDtypeStruct((s, ATTN_W), F32), jax.ShapeDtypeStruct((N_PAIRS, n_q, N_HEADS, tq), F32)],
        compiler_params=_cp(("parallel", "arbitrary")))(first_blk, qx, kx, vt)


def _attn_bwd_t(qx, dox, kx, kxt, vx, lse, last_blk):
    _, s, _ = qx.shape
    tq = min(TQ, s)
    n_q = s // tq

    def body(last_ref, qx_ref, dox_ref, lse_ref, kx_ref, kxt_ref, vx_ref, dk_ref, dv_ref, dfk_ref, dqt_ref, dfq_ref):
        p = pl.program_id(0)
        j = pl.program_id(1)

        @pl.when(j == 0)
        def _():
            dqt_ref[...] = jnp.zeros(dqt_ref.shape, F32)
            dfq_ref[...] = jnp.zeros(dfq_ref.shape, F32)

        key_le_query = (lax.broadcasted_iota(jnp.int32, (tq, tq), 0) <= lax.broadcasted_iota(jnp.int32, (tq, tq), 1))

        def step(i, carry, masked):
            rows_q = pl.ds(pl.multiple_of(i * tq, tq), tq)
            out = []
            for hh in range(2):
                dk, dv, col = carry[3 * hh:3 * hh + 3]
                q, do = qx_ref[hh, rows_q, :], dox_ref[hh, rows_q, :]
                st = lax.dot_general(kx_ref[hh], q, _NT, preferred_element_type=F32)
                pt = jnp.exp(st - lse_ref[0, i, hh:hh + 1, :])
                if masked:
                    pt = jnp.where(key_le_query, pt, 0.0)
                dst = pt * lax.dot_general(vx_ref[hh], do, _NT, preferred_element_type=F32)
                pb, dsb = pt.astype(BF16), dst.astype(BF16)
                dv = dv + lax.dot_general(pb, do, _NN, preferred_element_type=F32)
                dk = dk + lax.dot_general(dsb, q, _NN, preferred_element_type=F32)
                dqt_ref[hh, i] += lax.dot_general(kxt_ref[hh, 0], dsb, _NN, preferred_element_type=F32)
                for cb in range(tq // LANES):
                    col = col + dst[:, cb * LANES:(cb + 1) * LANES]
                dfq_ref[hh, i] += jnp.sum(dst.reshape(tq // HALO, HALO, tq), axis=0)
                out += [dk, dv, col]
            return tuple(out)

        zero = jnp.zeros((tq, LANES), F32)
        carry = step(j, (zero,) * 6, True)
        dk_a, dv_a, col_a, dk_b, dv_b, col_b = lax.fori_loop(j + 1, last_ref[p * n_q + j] + 1,
                                                             lambda i, cr: step(i, cr, False), carry)
        head_a = lax.broadcasted_iota(jnp.int32, (tq, LANES), 1) < HEAD_DIM
        dk_ref[...] = jnp.where(head_a, dk_a, pltpu.roll(dk_b, HEAD_DIM, 1)).astype(BF16)
        dv_ref[...] = jnp.where(head_a, dv_a, pltpu.roll(dv_b, HEAD_DIM, 1)).astype(BF16)
        rows8 = lax.broadcasted_iota(jnp.int32, (N_HEADS, tq), 0)
        dfk_a, dfk_b = (-jnp.sum(c.T, axis=0, keepdims=True) for c in (col_a, col_b))
        dfk_ref[0, 0] = jnp.where(rows8 == 0, dfk_a, jnp.where(rows8 == 1, dfk_b, 0.0))

    resident = pl.BlockSpec((2, s, LANES), lambda p, j, last: (p, 0, 0))
    key_rows = pl.BlockSpec((2, tq, LANES), lambda p, j, last: (p, j, 0))
    pair_out = pl.BlockSpec((tq, LANES), lambda p, j, last: (j, p))
    grid_spec = pltpu.PrefetchScalarGridSpec(
        num_scalar_prefetch=1, grid=(N_PAIRS, n_q),
        in_specs=[resident, resident, pl.BlockSpec((1, n_q, N_HEADS, tq), lambda p, j, last: (p, 0, 0, 0)),
                  key_rows, pl.BlockSpec((2, 1, LANES, tq), lambda p, j, last: (p, j, 0, 0)), key_rows],
        out_specs=[pair_out, pair_out, pl.BlockSpec((1, 1, N_HEADS, tq), lambda p, j, last: (p, j, 0, 0)),
                   pl.BlockSpec((2, n_q, LANES, tq), lambda p, j, last: (p, 0, 0, 0)),
                   pl.BlockSpec((2, n_q, HALO, tq), lambda p, j, last: (p, 0, 0, 0))])
    return pl.pallas_call(
        body, name="attn_bwd", grid_spec=grid_spec,
        out_shape=[jax.ShapeDtypeStruct((s, ATTN_W), BF16), jax.ShapeDtypeStruct((s, ATTN_W), BF16),
                   jax.ShapeDtypeStruct((N_PAIRS, n_q, N_HEADS, tq), F32),
                   jax.ShapeDtypeStruct((N_HEADS, n_q, LANES, tq), F32),
                   jax.ShapeDtypeStruct((N_HEADS, n_q, HALO, tq), F32)],
        compiler_params=_cp(("parallel", "arbitrary")))(last_blk, qx, dox, lse, kx, kxt, vx)


def _attn_dq_finish(dqt):
    _, n_q, _, tq = dqt.shape
    per = 4 if n_q % 4 == 0 else 1

    def body(dqt_ref, dq_ref):
        for b in range(per):
            a, bb = dqt_ref[0, b], dqt_ref[1, b]
            dq_ref[b * tq:(b + 1) * tq, :] = (
                jnp.concatenate([a[0:HEAD_DIM], bb[0:HEAD_DIM]], axis=0).T * Q_SCALE).astype(BF16)

    return pl.pallas_call(
        body, name="attn_dq_finish", grid=(N_PAIRS, n_q // per),
        in_specs=[pl.BlockSpec((2, per, LANES, tq), lambda p, i: (p, i, 0, 0))],
        out_specs=pl.BlockSpec((per * tq, LANES), lambda p, i: (i, p)),
        out_shape=jax.ShapeDtypeStruct((n_q * tq, ATTN_W), BF16),
        compiler_params=_cp(("parallel", "parallel")))(dqt)


def _ffn_act_fwd(up, w_ffn):
    s = up.shape[0]
    tm, tn = min(TM_FFN, s), TN_FFN
    nb = D_FF // tn

    def body(a_ref, g_ref, ap_ref, gp_ref, wa_ref, wg_ref, act_ref):
        i = pl.program_id(1)

        def conv(blk_ref, prev_ref, w_ref):
            prev = jnp.where(i > 0, prev_ref[...], 0.0)
            return _conv_taps(jnp.concatenate([prev, blk_ref[...]], axis=0), w_ref[...])[HALO:]

        u_a, u_g = conv(a_ref, ap_ref, wa_ref), conv(g_ref, gp_ref, wg_ref)
        act_ref[...] = (u_g * jax.nn.sigmoid(u_g) * u_a).astype(BF16)

    blk = lambda off: pl.BlockSpec((tm, tn), lambda n, i: (i, off + n))
    prev = lambda off: pl.BlockSpec((HALO, tn), lambda n, i: (jnp.maximum(i * (tm // HALO) - 1, 0), off + n))
    wsp = lambda off: pl.BlockSpec((3, tn), lambda n, i: (0, off + n))
    return pl.pallas_call(
        body, name="ffn_act_fwd", grid=(nb, s // tm),
        in_specs=[blk(0), blk(nb), prev(0), prev(nb), wsp(0), wsp(nb)],
        out_specs=pl.BlockSpec((tm, tn), lambda n, i: (i, n)),
        out_shape=jax.ShapeDtypeStruct((s, D_FF), BF16),
        compiler_params=_cp(("parallel", "parallel")))(up, up, up, up, w_ffn, w_ffn)


def _ffn_act_bwd(up, dact, w_ffn):
    s = up.shape[0]
    tm, tn = min(TM_FFN, s), TN_FFN
    nb = D_FF // tn
    n_blk = s // tm

    def body(a_ref, g_ref, ap_ref, gp_ref, an_ref, gn_ref, d_ref, dn_ref, wa_ref, wg_ref,
             dup_ref, gwa_ref, gwg_ref):
        i = pl.program_id(1)

        @pl.when(i == 0)
        def _():
            gwa_ref[...] = jnp.zeros_like(gwa_ref)
            gwg_ref[...] = jnp.zeros_like(gwg_ref)

        def ext(blk_ref, prev_ref, next_ref):
            return jnp.concatenate([jnp.where(i > 0, prev_ref[...], 0.0), blk_ref[...], next_ref[...]], axis=0)

        wa, wg = wa_ref[...], wg_ref[...]
        up_a, up_g = ext(a_ref, ap_ref, an_ref), ext(g_ref, gp_ref, gn_ref)
        u_a, u_g = _conv_taps(up_a, wa), _conv_taps(up_g, wg)
        d_e = jnp.concatenate([jnp.zeros((HALO, tn), F32), d_ref[...],
                               jnp.where(i < n_blk - 1, dn_ref[...], 0.0)], axis=0)
        sig = jax.nn.sigmoid(u_g)
        du_a = d_e * (u_g * sig)
        du_g = d_e * u_a * (sig * (1.0 + u_g * (1.0 - sig)))
        blk = slice(HALO, HALO + tm)
        dup_ref[0] = _conv_taps_t(du_a, wa)[blk].astype(BF16)
        dup_ref[1] = _conv_taps_t(du_g, wg)[blk].astype(BF16)
        for gw_ref, upv, du in ((gwa_ref, up_a[blk], du_a), (gwg_ref, up_g[blk], du_g)):
            gw_ref[0:1, :] += jnp.sum(upv * _shift_up(du, 2)[blk], axis=0, keepdims=True)
            gw_ref[1:2, :] += jnp.sum(upv * _shift_up(du, 1)[blk], axis=0, keepdims=True)
            gw_ref[2:3, :] += jnp.sum(upv * du[blk], axis=0, keepdims=True)

    blk = lambda off: pl.BlockSpec((tm, tn), lambda n, i: (i, off + n))
    prev = lambda off: pl.BlockSpec((HALO, tn), lambda n, i: (jnp.maximum(i * (tm // HALO) - 1, 0), off + n))
    nxt = lambda off: pl.BlockSpec(
        (HALO, tn), lambda n, i: (jnp.minimum((i + 1) * (tm // HALO), s // HALO - 1), off + n))
    wsp = lambda off: pl.BlockSpec((3, tn), lambda n, i: (0, off + n))
    return pl.pallas_call(
        body, name="ffn_act_bwd", grid=(nb, n_blk),
        in_specs=[blk(0), blk(nb), prev(0), prev(nb), nxt(0), nxt(nb), blk(0), nxt(0), wsp(0), wsp(nb)],
        out_specs=[pl.BlockSpec((2, tm, tn), lambda n, i: (0, i, n)), wsp(0), wsp(0)],
        out_shape=[jax.ShapeDtypeStruct((2, s, D_FF), BF16),
                   jax.ShapeDtypeStruct((3, D_FF), F32), jax.ShapeDtypeStruct((3, D_FF), F32)],
        compiler_params=_cp(("parallel", "arbitrary")))(up, up, up, up, up, up, dact, dact, w_ffn, w_ffn)


def _adamw(w, g, m, v, name):
    r, c = w.shape
    tr = next((t for t in (512, 352, 256, 128, 64, 32, 16, 8) if r > t and r % t == 0), r)

    def body(w_ref, g_ref, m_ref, v_ref, d_ref, nm_ref, nv_ref):
        gv = g_ref[...]
        m_new = ADAM_B1 * m_ref[...] + (1.0 - ADAM_B1) * gv
        v_new = ADAM_B2 * v_ref[...] + (1.0 - ADAM_B2) * (gv * gv)
        m_hat = m_new / (1.0 - ADAM_B1 ** ADAM_STEP)
        v_hat = v_new / (1.0 - ADAM_B2 ** ADAM_STEP)
        d_ref[...] = -ADAM_LR * (m_hat / (jnp.sqrt(v_hat) + ADAM_EPS) + ADAM_WD * w_ref[...])
        nm_ref[...] = m_new
        nv_ref[...] = v_new

    spec = pl.BlockSpec((tr, c), lambda i: (i, 0))
    shp = jax.ShapeDtypeStruct((r, c), F32)
    return pl.pallas_call(
        body, name=name, grid=(r // tr,), in_specs=[spec] * 4, out_specs=[spec] * 3, out_shape=[shp] * 3,
        compiler_params=_cp(("parallel",)))(w, g, m, v)


def _sum_rows_block(h):
    return h if h <= 352 else 256


def _pair_sum(view, recv, sel, name):
    n, _, h, c = view.shape
    tr = _sum_rows_block(h)

    def body(sel_ref, a_ref, b_ref, o_ref, ob_ref):
        t = a_ref[...] + b_ref[...]
        o_ref[...] = t
        ob_ref[...] = t.astype(BF16)

    blk = pl.BlockSpec((None, tr, c), lambda j, i, sel_ref: (j, i, 0))
    grid_spec = pltpu.PrefetchScalarGridSpec(
        num_scalar_prefetch=1, grid=(n, h // tr),
        in_specs=[pl.BlockSpec((None, None, tr, c), lambda j, i, sel_ref: (j, sel_ref[0], i, 0)),
                  pl.BlockSpec((None, None, tr, c), lambda j, i, sel_ref: (j, 0, i, 0))],
        out_specs=[blk, blk])
    return pl.pallas_call(
        body, name=name, grid_spec=grid_spec,
        out_shape=[jax.ShapeDtypeStruct((n, h, c), F32), jax.ShapeDtypeStruct((n, h, c), BF16)],
        compiler_params=_cp(("parallel", "parallel")))(sel, view, recv)


def _chip_sum(pair, got, sel, name):
    _, h, c = pair.shape
    tr = _sum_rows_block(h)
    nblk = h // tr

    def body(sel_ref, p_ref, g0_ref, g1_ref, g2_ref, o_ref):
        o_ref[...] = ((p_ref[...] + g0_ref[...].astype(F32)) + g1_ref[...].astype(F32)) + g2_ref[...].astype(F32)

    slot = lambda k: pl.BlockSpec((None, tr, c), lambda i, sel_ref: (k, i, 0))
    grid_spec = pltpu.PrefetchScalarGridSpec(
        num_scalar_prefetch=1, grid=(h // tr,),
        in_specs=[pl.BlockSpec((None, tr, c), lambda i, sel_ref: (sel_ref[1], i, 0)), slot(0), slot(1), slot(2)],
        out_specs=pl.BlockSpec((tr, c), lambda i, sel_ref: (sel_ref[0] * nblk + i, 0)))
    return pl.pallas_call(
        body, name=name, grid_spec=grid_spec, out_shape=jax.ShapeDtypeStruct((2 * h, c), F32),
        compiler_params=_cp(("parallel",)))(sel, pair, got, got, got)


def _place():
    return lax.axis_index("x"), lax.axis_index("y"), lax.axis_index("c")


def _other_chips(x, y):
    return [(1 - x, y), (x, 1 - y), (1 - x, 1 - y)]


def _hbm_specs(n):
    return [pl.BlockSpec(memory_space=pl.ANY)] * n


def _all_gather_weights(bigs, smalls):
    nb, ns = len(bigs), len(smalls)
    n = nb + ns

    def body(*refs):
        start, forward, finish = _gather_phases(refs[:n], refs[2 * n:3 * n], nb, *refs[3 * n:])
        start()
        forward()
        finish()

    arrays, landing, sems = _gather_operands(bigs, smalls)
    return pl.pallas_call(
        body, name="all_gather_weights",
        out_shape=[jax.ShapeDtypeStruct(b.shape, b.dtype) for b in landing],
        in_specs=_hbm_specs(2 * n), out_specs=_hbm_specs(n), input_output_aliases={n + k: k for k in range(n)},
        scratch_shapes=sems)(*arrays, *landing)


def _hosted_gather(refs, n, nb, step, total):
    ins, outs, send_sems, recv_sems = refs
    start, forward, finish = _gather_phases(ins, outs, nb, send_sems, recv_sems)
    pl.when(step == 0)(start)
    pl.when(step == (3 * total) // 4)(forward)
    return lambda: pl.when(step == total - 1)(finish)


def _in_proj(x, g, w_a, w_b, w_c, bigs, smalls):
    s, d = x.shape
    tm, tn = min(TM_MM, s), ATTN_W
    na, nq = w_a.shape[1] // tn, w_b.shape[1] // tn
    steps = na + nq + 1
    total = (s // tm) * steps
    nb, n = len(bigs), len(bigs) + len(smalls)
    arrays, landing, sems = _gather_operands(bigs, smalls)

    def body(x_ref, g_ref, wa_ref, wb_ref, wc_ref, *rest):
        z_ref, qkv_ref, f_ref, h_ref = rest[2 * n:2 * n + 4]
        h_scr = rest[-1]
        m, j = pl.program_id(0), pl.program_id(1)
        finish = _hosted_gather((rest[:n], rest[2 * n + 4:3 * n + 4]) + tuple(rest[3 * n + 4:3 * n + 6]), n, nb,
                                m * steps + j, total)

        @pl.when(j == 0)
        def _():
            xv = x_ref[...]
            hv = (xv * _rstd(xv) * g_ref[...]).astype(BF16)
            h_scr[...] = hv
            h_ref[...] = hv

        h = h_scr[...]

        @pl.when(j < na)
        def _():
            z_ref[...] = lax.dot_general(h, wa_ref[...], _NN, preferred_element_type=F32)

        @pl.when((j >= na) & (j < na + nq))
        def _():
            qkv_ref[...] = lax.dot_general(h, wb_ref[...], _NN, preferred_element_type=F32).astype(BF16)

        @pl.when(j == na + nq)
        def _():
            f_ref[...] = lax.dot_general(h, wc_ref[...], _NN, preferred_element_type=F32)

        finish()

    blk_a = lambda m, j: (m, jnp.minimum(j, na - 1))
    blk_b = lambda m, j: (m, jnp.clip(j - na, 0, nq - 1))
    outs = pl.pallas_call(
        body, name="in_proj", grid=(s // tm, steps),
        in_specs=[pl.BlockSpec((tm, d), lambda m, j: (m, 0)), pl.BlockSpec((1, d), lambda m, j: (0, 0)),
                  pl.BlockSpec((d, tn), lambda m, j: (0, jnp.minimum(j, na - 1))),
                  pl.BlockSpec((d, tn), lambda m, j: (0, jnp.clip(j - na, 0, nq - 1))),
                  pl.BlockSpec((d, LANES), lambda m, j: (0, 0))] + _hbm_specs(2 * n),
        out_specs=[pl.BlockSpec((tm, tn), blk_a), pl.BlockSpec((tm, tn), blk_b),
                   pl.BlockSpec((tm, LANES), lambda m, j: (m, 0)), pl.BlockSpec((tm, d), lambda m, j: (m, 0))]
        + _hbm_specs(n),
        out_shape=[jax.ShapeDtypeStruct((s, w_a.shape[1]), F32), jax.ShapeDtypeStruct((s, w_b.shape[1]), BF16),
                   jax.ShapeDtypeStruct((s, LANES), F32), jax.ShapeDtypeStruct((s, d), BF16)]
        + [jax.ShapeDtypeStruct(b.shape, b.dtype) for b in landing],
        input_output_aliases={5 + n + k: 4 + k for k in range(n)},
        scratch_shapes=sems + [pltpu.VMEM((tm, d), BF16)],
        compiler_params=_cp(("arbitrary", "arbitrary")))(x, g, w_a, w_b, w_c, *arrays, *landing)
    return outs[0], outs[1], outs[2], outs[3], outs[4:]


def _gather_operands(bigs, smalls):
    x, y, _ = _place()
    arrays = list(bigs) + list(smalls)
    landing = [lax.dynamic_update_index_in_dim(lax.empty((N_CHIPS,) + a.shape, a.dtype), a, 2 * x + y, 0)
               for a in arrays]
    n_sems = 6 * len(bigs) + 3 * len(smalls)
    return arrays, landing, [pltpu.SemaphoreType.DMA((n_sems,)), pltpu.SemaphoreType.DMA((n_sems,))]


def _gather_phases(ins, outs, nb, send_sems, recv_sems):
    n = len(ins)
    x, y, c = _place()
    my_chip = 2 * x + y
    chips = _other_chips(x, y)
    sibling = (x, y, 1 - c)

    def rows(k, which):
        h = ins[k].shape[0] // 2
        return pl.ds(which * h, h)

    def copy(sem, src, dst, to):
        return pltpu.make_async_remote_copy(src_ref=src, dst_ref=dst, send_sem=send_sems.at[sem],
                                            recv_sem=recv_sems.at[sem], device_id=to, device_id_type=MESH)

    def sends():
        out = [copy(6 * k + j, ins[k].at[rows(k, c)], outs[k].at[my_chip, rows(k, c)], (cx, cy, c))
               for k in range(nb) for j, (cx, cy) in enumerate(chips)]
        return out + [copy(6 * nb + 3 * (k - nb) + j, ins[k], outs[k].at[my_chip], (cx, cy, c))
                      for k in range(nb, n) for j, (cx, cy) in enumerate(chips)]

    def landed(k, j, which):
        cx, cy = chips[j]
        return outs[k].at[2 * cx + cy, rows(k, which)]

    def forwards():
        return [copy(6 * k + 3 + j, landed(k, j, c), landed(k, j, c), sibling)
                for j in range(3) for k in range(nb)]

    def start():
        for cp in sends():
            cp.start()

    def forward():
        for j in range(3):
            for k in range(nb):
                copy(6 * k + j, landed(k, j, c), landed(k, j, c), (x, y, c)).wait_recv()
                copy(6 * k + 3 + j, landed(k, j, c), landed(k, j, c), sibling).start()

    def finish():
        for j, (cx, cy) in enumerate(chips):
            for k in range(nb):
                copy(6 * k + 3 + j, landed(k, j, 1 - c), landed(k, j, 1 - c), (x, y, c)).wait_recv()
            for k in range(nb, n):
                arrived = outs[k].at[2 * cx + cy]
                copy(6 * nb + 3 * (k - nb) + j, arrived, arrived, (x, y, c)).wait_recv()
        for cp in sends() + forwards():
            cp.wait_send()

    return start, forward, finish


def _pair_exchange(views, name):
    n = len(views)

    def body(*refs):
        ins, outs, send_sems, recv_sems = refs[:n], refs[n:2 * n], refs[2 * n], refs[2 * n + 1]
        x, y, c = _place()
        copies = [pltpu.make_async_remote_copy(
            src_ref=ins[k].at[:, pl.ds(1 - c, 1)], dst_ref=outs[k], send_sem=send_sems.at[k],
            recv_sem=recv_sems.at[k], device_id=(x, y, 1 - c), device_id_type=MESH) for k in range(n)]
        for cp in copies:
            cp.start()
        for cp in copies:
            cp.wait()

    return pl.pallas_call(
        body, name=name,
        out_shape=[jax.ShapeDtypeStruct((v.shape[0], 1) + v.shape[2:], v.dtype) for v in views],
        in_specs=_hbm_specs(n), out_specs=_hbm_specs(n),
        scratch_shapes=[pltpu.SemaphoreType.DMA((n,)), pltpu.SemaphoreType.DMA((n,))])(*views)


def _scatter_operands(parts):
    n = len(parts)
    return ([jax.ShapeDtypeStruct((3,) + p.shape[1:], p.dtype) for p in parts],
            [pltpu.SemaphoreType.DMA((3 * n,)), pltpu.SemaphoreType.DMA((3 * n,))])


def _scatter_phases(ins, outs, send_sems, recv_sems):
    x, y, c = _place()

    def copies():
        return [pltpu.make_async_remote_copy(
            src_ref=ins[k].at[pl.ds(2 * cx + cy, 1)], dst_ref=outs[k].at[pl.ds(r, 1)], send_sem=send_sems.at[3 * k + r],
            recv_sem=recv_sems.at[3 * k + r], device_id=(cx, cy, c), device_id_type=MESH)
            for k in range(len(ins)) for r, (cx, cy) in enumerate(_other_chips(x, y))]

    def start():
        for cp in copies():
            cp.start()

    def finish():
        for cp in copies():
            cp.wait()

    return start, finish


def _hosted_scatter(ins, outs, sems, step, total):
    start, finish = _scatter_phases(ins, outs, *sems)
    pl.when(step == 0)(start)
    return lambda: pl.when(step == total - 1)(finish)


def _join_halves(shards):
    n = len(shards)

    def body(*refs):
        ins, outs, send_sems, recv_sems = refs[:n], refs[n:2 * n], refs[2 * n], refs[2 * n + 1]
        x, y, c = _place()

        def rows(ref, which):
            h = ref.shape[0] // 2
            return ref.at[pl.ds(which * h, h)]

        sent = [pltpu.make_async_remote_copy(
            src_ref=rows(ins[k], c), dst_ref=rows(outs[k], c), send_sem=send_sems.at[k], recv_sem=recv_sems.at[k],
            device_id=(x, y, 1 - c), device_id_type=MESH) for k in range(n)]
        for cp in sent:
            cp.start()
        for k in range(n):
            pltpu.make_async_remote_copy(
                src_ref=rows(ins[k], 1 - c), dst_ref=rows(outs[k], 1 - c), send_sem=send_sems.at[k],
                recv_sem=recv_sems.at[k], device_id=(x, y, 1 - c), device_id_type=MESH).wait_recv()
        for cp in sent:
            cp.wait_send()

    return pl.pallas_call(
        body, name="half_exchange", out_shape=[jax.ShapeDtypeStruct(a.shape, a.dtype) for a in shards],
        in_specs=_hbm_specs(n), out_specs=_hbm_specs(n), input_output_aliases={k: k for k in range(n)},
        scratch_shapes=[pltpu.SemaphoreType.DMA((n,)), pltpu.SemaphoreType.DMA((n,))])(*shards)


def _all_reduce_small(packet):
    rows, width = packet.shape
    n_dev = 8

    def body(x_ref, out_ref, gath, send_sems, recv_sems):
        x, y, c = _place()
        me, sibling = (x, y, c), (x, y, 1 - c)
        chips = _other_chips(x, y)

        def slot(px, py, pc):
            return gath.at[pl.ds((4 * px + 2 * py + pc) * rows, rows), :]

        def copy(k, block, to, src=None):
            return pltpu.make_async_remote_copy(
                src_ref=slot(*block) if src is None else src, dst_ref=slot(*block), send_sem=send_sems.at[k],
                recv_sem=recv_sems.at[k], device_id=to, device_id_type=MESH)

        first = [copy(0, me, sibling, src=x_ref)]
        first += [copy(1 + j, me, (*chip, c), src=x_ref) for j, chip in enumerate(chips)]
        for cp in first:
            cp.start()
        gath[pl.ds((4 * x + 2 * y + c) * rows, rows), :] = x_ref[...]
        passed = [copy(4 + j, (*chip, c), sibling) for j, chip in enumerate(chips)]
        for j, chip in enumerate(chips):
            copy(1 + j, (*chip, c), me).wait_recv()
            passed[j].start()
        copy(0, sibling, me).wait_recv()
        for j, chip in enumerate(chips):
            copy(4 + j, (*chip, 1 - c), me).wait_recv()
        for cp in first + passed:
            cp.wait_send()
        acc = gath[0:rows, :]
        for d in range(1, n_dev):
            acc = acc + gath[d * rows:(d + 1) * rows, :]
        out_ref[...] = acc

    return pl.pallas_call(
        body, name="all_reduce_small", out_shape=jax.ShapeDtypeStruct((rows, width), F32),
        in_specs=[pl.BlockSpec(memory_space=pltpu.VMEM)], out_specs=pl.BlockSpec(memory_space=pltpu.VMEM),
        scratch_shapes=[pltpu.VMEM((n_dev * rows, width), F32), pltpu.SemaphoreType.DMA((7,)),
                        pltpu.SemaphoreType.DMA((7,))])(packet)


def _flat_rows(parts, width, row_multiple):
    flat = jnp.concatenate([p.astype(F32).reshape(-1) for p in parts])
    rows = -(-flat.shape[0] // width)
    rows = -(-rows // row_multiple) * row_multiple
    return jnp.pad(flat, (0, rows * width - flat.shape[0])).reshape(rows, width)


def _unflatten(flat2d, shapes):
    flat = flat2d.reshape(-1)
    out, off = [], 0
    for shp in shapes:
        n = 1
        for dim in shp:
            n *= dim
        out.append(flat[off:off + n].reshape(shp))
        off += n
    return out


def _core_and_chip():
    x, y, c = _place()
    return jnp.stack([c, 2 * x + y]).astype(jnp.int32)


def _pair_sums(chip_major, names, call_name):
    views = [g.reshape(N_CHIPS, 2, g.shape[1] // 2, g.shape[2]) for g in chip_major]
    recv = _pair_exchange(views, call_name)
    sel = _core_and_chip()
    return [_pair_sum(v, r, sel, "pair_sum_" + nm) for v, r, nm in zip(views, recv, names)]


def _finish_grads(pairs, got, names):
    sel = _core_and_chip()
    return _join_halves([_chip_sum(p, g, sel, "chip_sum_" + nm) for (p, _), g, nm in zip(pairs, got, names)])


def kernel(x, g_mix, w_in, b_f, w_conv, g_conv_out, g_attn_out, w_o, g_ffn, w_up, w_ffn_conv, w_down, g_final, loss_target, m_g_mix, m_w_in, m_b_f, m_w_conv, m_g_conv_out, m_g_attn_out, m_w_o, m_g_ffn, m_w_up, m_w_ffn_conv, m_w_down, m_g_final, v_g_mix, v_w_in, v_b_f, v_w_conv, v_g_conv_out, v_g_attn_out, v_w_o, v_g_ffn, v_w_up, v_w_ffn_conv, v_w_down, v_g_final):
    s = x.shape[1]
    x0 = x[0]
    target = loss_target[0]
    d = D_MODEL
    x_pos, y_pos, _ = _place()
    my_chip = 2 * x_pos + y_pos

    (c_in,) = _all_gather_weights([w_in[0].astype(BF16)], [])
    w_in_full = jnp.concatenate([c_in[j] for j in range(N_CHIPS)], axis=1)
    c3 = 3 * CONV_CH
    w_a, w_b = w_in_full[:, :c3], w_in_full[:, c3:c3 + 3 * ATTN_W]
    w_c = jnp.pad(w_in_full[:, c3 + 3 * ATTN_W:], ((0, 0), (0, LANES - N_HEADS)))
    w_q, w_k, w_v = (w_b[:, i * ATTN_W:(i + 1) * ATTN_W] for i in range(3))
    b_pad = jnp.pad(b_f, ((0, 0), (0, LANES - N_HEADS)))

    z_a, qkv, f_log, h1, (c_o, c_up, c_conv, c_ffn) = _in_proj(
        x0, g_mix, w_a, w_b, w_c, [w_o[0].astype(BF16), w_up[0].astype(BF16)], [w_conv[0], w_ffn_conv[0]])
    fb = _gate_fwd(f_log, b_pad)
    qx, kx, kxt, vx, vt, bounds, (c_down,) = _attn_prep(qkv, fb, [w_down[0].astype(BF16)])
    w_o_full = c_o.reshape(d, d)
    w_down_full = c_down.reshape(D_FF, d)
    w_conv_full = jnp.concatenate([c_conv[j] for j in range(N_CHIPS)], axis=1)
    w_ffn_full = jnp.concatenate([c_ffn[j] for j in range(N_CHIPS)], axis=1)
    n_up = c_up.shape[2]
    first_blk, last_blk = _key_block_ranges(bounds)
    o_attn, lse = _attn_fwd_t(qx, kx, vt, first_blk)
    mix = _mixer_fwd(z_a, o_attn, w_conv_full, g_conv_out, g_attn_out)
    x2, h2 = _mm("nn", [mix], [w_o_full], F32, 512, d, "out_proj", add=x0, norm_g=g_ffn)
    up = _mm("nn", [h2], [c_up], F32, TM_MM, n_up, "up_proj", b_chips=True)
    act = _ffn_act_fwd(up, w_ffn_full)
    x3 = _mm("nn", [act], [w_down_full], F32, TM_MM, 512, "down_proj", add=x2)
    dx3, dx3_b, loss_row, gg_final = _loss_head(x3, target, g_final.reshape(1, d))

    dact = _mm("nt", [dx3_b], [w_down_full], F32, TM_MM, 1408, "d_act")
    gw_down = _mm_tn(act, dx3_b, 1408, 1024, "gw_down")
    dup, gwf_lin, gwf_gate = _ffn_act_bwd(up, dact, w_ffn_full)
    dh2 = _mm("nt", [(dup, j // 2, j % 2, n_up) for j in range(N_CHIPS)], [(c_up, j) for j in range(N_CHIPS)],
              F32, TM_MM, 512, "d_h2")
    gw_up = _mm_tn(h2, dup, 1024, n_up, "gw_up", out_chips=True)
    dx2, dx2_b, gg_ffn = _rms_bwd(x2, dh2, g_ffn, dx3, "rms_ffn_bwd", True)
    dmix = _mm("nt", [dx2_b], [w_o_full], F32, TM_MM, 512, "d_mix")
    gw_o = _mm_tn(mix, dx2_b, 1024, 1024, "gw_o")
    early = _pair_sums([gw_o.reshape(N_CHIPS, d // N_CHIPS, d), gw_up, gw_down.reshape(N_CHIPS, D_FF // N_CHIPS, d)],
                       ["w_o", "w_up", "w_down"], "pair_exchange")
    dz_a, dox, gw_conv, gg_conv_out, gg_attn_out, (got_o, got_down) = _mixer_bwd(
        z_a, o_attn, dmix, w_conv_full, g_conv_out, g_attn_out, [early[0][1], early[2][1]])
    dk, dv, dfk, dqt, dfq = _attn_bwd_t(qx, dox, kx, kxt, vx, lse, last_blk)
    dq = _attn_dq_finish(dqt)
    d_f = (jnp.transpose(dfk[:, :, 0:2, :], (1, 3, 0, 2)).reshape(s, N_HEADS)
           + jnp.transpose(jnp.sum(dfq, axis=2), (1, 2, 0)).reshape(s, N_HEADS))
    df_b, gb_f = _gate_bwd(f_log, b_pad, jnp.pad(d_f, ((0, 0), (0, LANES - N_HEADS))))
    dh1, (got_up,) = _mm("nt", [dz_a, dq, dk, dv, df_b], [w_a, w_q, w_k, w_v, w_c], F32, TM_MM, 512, "d_h1",
                         scatter=[early[1][1]])
    gw_a = _mm_tn(h1, dz_a, 1024, c3, "gw_in_conv")
    gw_q = _mm_tn(h1, dq, 1024, ATTN_W, "gw_in_q")
    gw_k = _mm_tn(h1, dk, 1024, ATTN_W, "gw_in_k")
    gw_v = _mm_tn(h1, dv, 1024, ATTN_W, "gw_in_v")
    gw_c = _mm_tn(h1, df_b, 1024, LANES, "gw_in_gate")
    gw_in = jnp.concatenate([gw_a, gw_q, gw_k, gw_v, gw_c[:, :N_HEADS]], axis=1)
    n_in = IN_COLS // N_CHIPS
    gw_in = jnp.stack([gw_in[:, j * n_in:(j + 1) * n_in] for j in range(N_CHIPS)])
    late = _pair_sums([gw_in], ["w_in"], "pair_exchange_w_in")
    grad_x, gg_mix, (got_in,) = _rms_bwd(x0, dh1, g_mix, dx2, "rms_mix_bwd", False, scatter=[late[0][1]])
    g_w_in, g_w_o, g_w_up, g_w_down = _finish_grads(late + early, [got_in, got_o, got_up, got_down],
                                                    ["w_in", "w_o", "w_up", "w_down"])

    gw_ffn = jnp.concatenate([gwf_lin, gwf_gate], axis=1)
    small_parts = [gg_mix, gg_conv_out, gg_attn_out, gg_ffn, gg_final, gb_f[:, :N_HEADS], loss_row[:, 0:1], gw_conv,
                   gw_ffn]
    small_shapes = [a.shape for a in small_parts]
    tot = _unflatten(_all_reduce_small(_flat_rows(small_parts, d, 8)), small_shapes)
    g_g_mix, g_g_conv_out, g_g_attn_out, g_g_ffn, g_g_final, g_b_f, loss_sum, g_conv_full, g_ffn_full = tot
    loss = loss_sum[0, 0]
    g_g_final = g_g_final[0]
    g_w_conv = lax.dynamic_slice_in_dim(g_conv_full, my_chip * (CONV_CH // N_CHIPS), CONV_CH // N_CHIPS, axis=1)
    g_w_ffn = lax.dynamic_slice_in_dim(g_ffn_full, my_chip * n_up, n_up, axis=1)

    def adam_big(w, g, m, v, name):
        dl, nm, nv = _adamw(w[0], g, m[0], v[0], name)
        return dl[None], nm[None], nv[None]

    u_w_in = adam_big(w_in, g_w_in, m_w_in, v_w_in, "adam_w_in")
    u_w_o = adam_big(w_o, g_w_o, m_w_o, v_w_o, "adam_w_o")
    u_w_up = adam_big(w_up, g_w_up, m_w_up, v_w_up, "adam_w_up")
    u_w_down = adam_big(w_down, g_w_down, m_w_down, v_w_down, "adam_w_down")

    small_w = [g_mix, b_f, g_conv_out, g_attn_out, g_ffn, g_final, w_conv, w_ffn_conv]
    small_g = [g_g_mix, g_b_f, g_g_conv_out, g_g_attn_out, g_g_ffn, g_g_final, g_w_conv, g_w_ffn]
    small_m = [m_g_mix, m_b_f, m_g_conv_out, m_g_attn_out, m_g_ffn, m_g_final, m_w_conv, m_w_ffn_conv]
    small_v = [v_g_mix, v_b_f, v_g_conv_out, v_g_attn_out, v_g_ffn, v_g_final, v_w_conv, v_w_ffn_conv]
    shapes = [a.shape for a in small_w]
    pack = lambda arrs: _flat_rows(arrs, LANES, 8)
    sd, sm, sv = _adamw(pack(small_w), pack(small_g), pack(small_m), pack(small_v), "adam_small")
    sd, sm, sv = _unflatten(sd, shapes), _unflatten(sm, shapes), _unflatten(sv, shapes)
    (d_g_mix, d_b_f, d_g_conv_out, d_g_attn_out, d_g_ffn, d_g_final, d_w_conv, d_w_ffn) = sd
    (nm_g_mix, nm_b_f, nm_g_conv_out, nm_g_attn_out, nm_g_ffn, nm_g_final, nm_w_conv, nm_w_ffn) = sm
    (nv_g_mix, nv_b_f, nv_g_conv_out, nv_g_attn_out, nv_g_ffn, nv_g_final, nv_w_conv, nv_w_ffn) = sv

    grads = (g_g_mix, g_w_in[None], g_b_f, g_w_conv[None], g_g_conv_out, g_g_attn_out, g_w_o[None], g_g_ffn,
             g_w_up[None], g_w_ffn[None], g_w_down[None], g_g_final)
    deltas = (d_g_mix, u_w_in[0], d_b_f, d_w_conv, d_g_conv_out, d_g_attn_out, u_w_o[0], d_g_ffn, u_w_up[0],
              d_w_ffn, u_w_down[0], d_g_final)
    new_m = (nm_g_mix, u_w_in[1], nm_b_f, nm_w_conv, nm_g_conv_out, nm_g_attn_out, u_w_o[1], nm_g_ffn, u_w_up[1],
             nm_w_ffn, u_w_down[1], nm_g_final)
    new_v = (nv_g_mix, u_w_in[2], nv_b_f, nv_w_conv, nv_g_conv_out, nv_g_attn_out, u_w_o[2], nv_g_ffn, u_w_up[2],
             nv_w_ffn, u_w_down[2], nv_g_final)
    return (loss, grad_x[None], *grads, *deltas, *new_m, *new_v)
```

```python
import jax
import jax.numpy as jnp
from jax import lax
from jax.experimental import pallas as pl
from jax.experimental.pallas import tpu as pltpu

F32, BF16 = jnp.float32, jnp.bfloat16
MESH = pl.DeviceIdType.MESH

D_MODEL = 1024
CONV_CH = 512
ATTN_W = 512
N_HEADS = 8
HEAD_DIM = 64
N_PAIRS = N_HEADS // 2
D_FF = 2816
IN_COLS = 3 * CONV_CH + 3 * ATTN_W + N_HEADS
EPS = 1e-6
Q_SCALE = 0.125
EXP_ZERO = 88.0
N_CHIPS = 4
LANES = 128
HALO = 8

ADAM_LR, ADAM_B1, ADAM_B2, ADAM_EPS, ADAM_WD, ADAM_STEP = 0.001, 0.9, 0.999, 1e-08, 0.01, 10

TM_ROWS = 512
TM_MM = 1024
TK_TN = 1024
TQ = 512
TM_FFN = 1024
TN_FFN = 256
VMEM_LIMIT = 52 * 2**20


def _cp(sem, vmem=VMEM_LIMIT):
    return pltpu.CompilerParams(dimension_semantics=sem, vmem_limit_bytes=vmem)


def _bf(a):
    return a if a.dtype == BF16 else a.astype(BF16)


def _mm(mode, a_list, b_list, out_dtype, tm, tn, name, add=None, b_chips=False, scatter=(), norm_g=None):
    n_p = len(a_list)
    a0 = a_list[0]
    m_dim = a0[0].shape[1] if isinstance(a0, tuple) else a0.shape[0]
    b0 = b_list[0]
    if b_chips:
        n_dim = b0.shape[0] * b0.shape[2]
        assert tn == b0.shape[2] and mode == "nn"
    else:
        b0 = b0[0][b0[1]] if isinstance(b0, tuple) else b0
        n_dim = b0.shape[1 if mode == "nn" else 0]
    tm, tn = min(tm, m_dim), min(tn, n_dim)
    assert m_dim % tm == 0 and n_dim % tn == 0
    dims = (((1,), (0,)), ((), ())) if mode == "nn" else (((1,), (1,)), ((), ()))
    in_specs, args = [], []
    for a in a_list:
        if isinstance(a, tuple):
            arr, lead, col, width = a
            in_specs.append(pl.BlockSpec((None, tm, width), lambda m, n, lead=lead, col=col: (lead, m, col)))
        else:
            arr = a
            in_specs.append(pl.BlockSpec((tm, a.shape[1]), lambda m, n: (m, 0)))
        args.append(arr)
    for b in b_list:
        if b_chips:
            arr = b
            in_specs.append(pl.BlockSpec((None, b.shape[1], tn), lambda m, n: (n, 0, 0)))
        elif isinstance(b, tuple):
            arr, lead = b
            if mode == "nn":
                in_specs.append(pl.BlockSpec((None, arr.shape[1], tn), lambda m, n, lead=lead: (lead, 0, n)))
            else:
                in_specs.append(pl.BlockSpec((None, tn, arr.shape[2]), lambda m, n, lead=lead: (lead, n, 0)))
        elif mode == "nn":
            arr = b
            in_specs.append(pl.BlockSpec((b.shape[0], tn), lambda m, n: (0, n)))
        else:
            arr = b
            in_specs.append(pl.BlockSpec((tn, b.shape[1]), lambda m, n: (n, 0)))
        args.append(arr)
    if add is not None:
        in_specs.append(pl.BlockSpec((tm, tn), lambda m, n: (m, n)))
        args.append(add)
    if norm_g is not None:
        assert tn == n_dim and not scatter
        in_specs.append(pl.BlockSpec((1, tn), lambda m, n: (0, 0)))
        args.append(norm_g)

    n_in = len(args)
    n_sc = len(scatter)
    grid = (m_dim // tm, n_dim // tn)

    def body(*refs):
        o_ref = refs[n_in + n_sc]
        if n_sc:
            finish = _hosted_scatter(refs[n_in:n_in + n_sc], refs[n_in + n_sc + 1:n_in + 2 * n_sc + 1],
                                     refs[n_in + 2 * n_sc + 1:], pl.program_id(0) * grid[1] + pl.program_id(1),
                                     grid[0] * grid[1])
        acc = None
        for i in range(n_p):
            d = lax.dot_general(_bf(refs[i][...]), _bf(refs[n_p + i][...]), dims,
                                preferred_element_type=F32)
            acc = d if acc is None else acc + d
        if add is not None:
            acc = refs[2 * n_p][...] + acc
        o_ref[...] = acc.astype(out_dtype)
        if norm_g is not None:
            refs[n_in + 1][...] = (acc * _rstd(acc) * refs[n_in - 1][...]).astype(BF16)
        if n_sc:
            finish()

    main_spec = pl.BlockSpec((tm, tn), lambda m, n: (m, n))
    main_shape = jax.ShapeDtypeStruct((m_dim, n_dim), out_dtype)
    if norm_g is not None:
        return pl.pallas_call(body, name=name, grid=grid, in_specs=in_specs, out_specs=[main_spec, main_spec],
                              out_shape=[main_shape, jax.ShapeDtypeStruct((m_dim, n_dim), BF16)],
                              compiler_params=_cp(("parallel", "parallel")))(*args)
    if not n_sc:
        return pl.pallas_call(body, name=name, grid=grid, in_specs=in_specs, out_specs=main_spec,
                              out_shape=main_shape, compiler_params=_cp(("parallel", "parallel")))(*args)
    got_shapes, sems = _scatter_operands(scatter)
    outs = pl.pallas_call(
        body, name=name, grid=grid, in_specs=in_specs + _hbm_specs(n_sc), out_specs=[main_spec] + _hbm_specs(n_sc),
        out_shape=[main_shape] + got_shapes, scratch_shapes=sems,
        compiler_params=_cp(("arbitrary", "arbitrary")))(*args, *scatter)
    return outs[0], outs[1:]


def _mm_tn(a, b, tm, tn, name, out_chips=False):
    k_dim, m_dim = a.shape
    n_dim = b.shape[-1] * (b.shape[0] if b.ndim == 3 else 1)
    tm, tn, tk = min(tm, m_dim), min(tn, b.shape[-1]), min(TK_TN, k_dim)
    assert m_dim % tm == 0 and b.shape[-1] % tn == 0 and k_dim % tk == 0
    per = b.shape[-1] // tn
    if b.ndim == 3:
        b_spec = pl.BlockSpec((None, tk, tn), lambda m, n, k: (n // per, k, n % per))
    else:
        b_spec = pl.BlockSpec((tk, tn), lambda m, n, k: (k, n))

    def body(a_ref, b_ref, o_ref):
        @pl.when(pl.program_id(2) == 0)
        def _():
            o_ref[...] = jnp.zeros_like(o_ref)
        o_ref[...] += lax.dot_general(_bf(a_ref[...]), _bf(b_ref[...]), (((0,), (0,)), ((), ())),
                                      preferred_element_type=F32)

    return pl.pallas_call(
        body, name=name, grid=(m_dim // tm, n_dim // tn, k_dim // tk),
        in_specs=[pl.BlockSpec((tk, tm), lambda m, n, k: (k, m)), b_spec],
        out_specs=(pl.BlockSpec((None, tm, tn), lambda m, n, k: (n, m, 0)) if out_chips
                   else pl.BlockSpec((tm, tn), lambda m, n, k: (m, n))),
        out_shape=jax.ShapeDtypeStruct((n_dim // tn, m_dim, tn) if out_chips else (m_dim, n_dim), F32),
        compiler_params=_cp(("parallel", "parallel", "arbitrary")))(a, b)


def _rstd(x):
    return lax.rsqrt(jnp.mean(x * x, axis=-1, keepdims=True) + EPS)


def _rms_bwd(x, dh, g, dres, name, with_bf16, scatter=()):
    s, d = x.shape
    tm = min(TM_ROWS, s)
    n_sc = len(scatter)
    n_out = 3 if with_bf16 else 2
    got_shapes, sems = _scatter_operands(scatter) if n_sc else ([], [])

    def body(x_ref, dh_ref, g_ref, dres_ref, *rest):
        dx_ref, gg_ref = rest[n_sc], rest[n_sc + n_out - 1]
        i = pl.program_id(0)
        if n_sc:
            finish = _hosted_scatter(rest[:n_sc], rest[n_sc + n_out:2 * n_sc + n_out], rest[2 * n_sc + n_out:], i,
                                     s // tm)

        @pl.when(i == 0)
        def _():
            gg_ref[...] = jnp.zeros_like(gg_ref)

        xv = x_ref[...]
        xn = xv * _rstd(xv)
        dhv = dh_ref[...]
        gg_ref[...] += jnp.sum(dhv * xn, axis=0, keepdims=True)
        t = dhv * g_ref[...]
        dx = dres_ref[...] + _rstd(xv) * (t - xn * jnp.mean(t * xn, axis=-1, keepdims=True))
        dx_ref[...] = dx
        if with_bf16:
            rest[n_sc + 1][...] = dx.astype(BF16)
        if n_sc:
            finish()

    row = pl.BlockSpec((tm, d), lambda i: (i, 0))
    vec = pl.BlockSpec((1, d), lambda i: (0, 0))
    out_specs = [row] + ([row] if with_bf16 else []) + [vec] + _hbm_specs(n_sc)
    out_shape = ([jax.ShapeDtypeStruct((s, d), F32)] + ([jax.ShapeDtypeStruct((s, d), BF16)] if with_bf16 else [])
                 + [jax.ShapeDtypeStruct((1, d), F32)] + got_shapes)
    outs = pl.pallas_call(
        body, name=name, grid=(s // tm,), in_specs=[row, row, vec, row] + _hbm_specs(n_sc), out_specs=out_specs,
        out_shape=out_shape, scratch_shapes=sems, compiler_params=_cp(("arbitrary",)))(x, dh, g, dres, *scatter)
    return tuple(outs[:n_out]) + ((outs[n_out:],) if n_sc else ())


def _loss_head(x3, target, g):
    s, d = x3.shape
    tm = min(TM_ROWS, s)

    def body(x_ref, t_ref, g_ref, dx_ref, dxb_ref, loss_ref, gg_ref):
        @pl.when(pl.program_id(0) == 0)
        def _():
            gg_ref[...] = jnp.zeros_like(gg_ref)
            loss_ref[...] = jnp.zeros_like(loss_ref)

        xv = x_ref[...]
        r = _rstd(xv)
        xn = xv * r
        gv = g_ref[...]
        err = xn * gv - t_ref[...]
        loss_ref[...] += 0.5 * jnp.sum(jnp.mean(err * err, axis=-1, keepdims=True), axis=0, keepdims=True)
        dy = err * (1.0 / d)
        gg_ref[...] += jnp.sum(dy * xn, axis=0, keepdims=True)
        t = dy * gv
        dx = r * (t - xn * jnp.mean(t * xn, axis=-1, keepdims=True))
        dx_ref[...] = dx
        dxb_ref[...] = dx.astype(BF16)

    row = pl.BlockSpec((tm, d), lambda i: (i, 0))
    vec = pl.BlockSpec((1, d), lambda i: (0, 0))
    return pl.pallas_call(
        body, name="loss_head", grid=(s // tm,), in_specs=[row, row, vec],
        out_specs=[row, row, pl.BlockSpec((1, LANES), lambda i: (0, 0)), vec],
        out_shape=[jax.ShapeDtypeStruct((s, d), F32), jax.ShapeDtypeStruct((s, d), BF16),
                   jax.ShapeDtypeStruct((1, LANES), F32), jax.ShapeDtypeStruct((1, d), F32)],
        compiler_params=_cp(("arbitrary",)))(x3, target, g)


def _prev_halo_spec(tm, width, col):
    return pl.BlockSpec((HALO, width), lambda i, *_: (jnp.maximum(i * (tm // HALO) - 1, 0), col))


def _next_halo_spec(tm, width, col, s):
    return pl.BlockSpec((HALO, width), lambda i, *_: (jnp.minimum((i + 1) * (tm // HALO), s // HALO - 1), col))


def _shift_down(x, k):
    return pltpu.roll(x, k, 0)


def _shift_up(x, k):
    return pltpu.roll(x, x.shape[0] - k, 0)


def _conv_taps(x_ext, w):
    return w[0:1, :] * _shift_down(x_ext, 2) + w[1:2, :] * _shift_down(x_ext, 1) + w[2:3, :] * x_ext


def _conv_taps_t(d_ext, w):
    return w[2:3, :] * d_ext + w[1:2, :] * _shift_up(d_ext, 1) + w[0:1, :] * _shift_up(d_ext, 2)


def _mixer_fwd(z_a, o_attn, w_conv, g_conv_out, g_attn_out):
    s = z_a.shape[0]
    c = CONV_CH
    tm = min(TM_ROWS, s)

    def body(gb_ref, gc_ref, xc_ref, gcp_ref, xcp_ref, o_ref, w_ref, gco_ref, gao_ref, mix_ref):
        i = pl.program_id(0)
        cx = gc_ref[...] * xc_ref[...]
        cx_prev = jnp.where(i > 0, gcp_ref[...] * xcp_ref[...], 0.0)
        conv = _conv_taps(jnp.concatenate([cx_prev, cx], axis=0), w_ref[...])[HALO:]
        y = gb_ref[...] * conv
        mix_ref[:, 0:c] = (y * _rstd(y) * gco_ref[...]).astype(BF16)
        o = o_ref[...]
        mix_ref[:, c:2 * c] = (o * _rstd(o) * gao_ref[...]).astype(BF16)

    col = lambda j: pl.BlockSpec((tm, c), lambda i: (i, j))
    vec = pl.BlockSpec((1, c), lambda i: (0, 0))
    return pl.pallas_call(
        body, name="mixer_fwd", grid=(s // tm,),
        in_specs=[col(0), col(1), col(2), _prev_halo_spec(tm, c, 1), _prev_halo_spec(tm, c, 2), col(0),
                  pl.BlockSpec((3, c), lambda i: (0, 0)), vec, vec],
        out_specs=pl.BlockSpec((tm, 2 * c), lambda i: (i, 0)),
        out_shape=jax.ShapeDtypeStruct((s, 2 * c), BF16),
        compiler_params=_cp(("parallel",)))(z_a, z_a, z_a, z_a, z_a, o_attn, w_conv, g_conv_out, g_attn_out)


def _mixer_bwd(z_a, o_attn, dmix, w_conv, g_conv_out, g_attn_out, scatter):
    s = z_a.shape[0]
    c = CONV_CH
    tm = min(TM_ROWS, s)
    n_blk = s // tm
    n_sc = len(scatter)
    got_shapes, sems = _scatter_operands(scatter)

    def body(gb_ref, gc_ref, xc_ref, gcp_ref, xcp_ref, gbn_ref, gcn_ref, xcn_ref, o_ref, dnc_ref, dncn_ref, dna_ref,
             w_ref, gco_ref, gao_ref, *rest):
        dz_ref, dox_ref, gw_ref, ggco_ref, ggao_ref = rest[n_sc:n_sc + 5]
        i = pl.program_id(0)
        finish = _hosted_scatter(rest[:n_sc], rest[n_sc + 5:2 * n_sc + 5], rest[2 * n_sc + 5:], i, n_blk)

        @pl.when(i == 0)
        def _():
            gw_ref[...] = jnp.zeros_like(gw_ref)
            ggco_ref[...] = jnp.zeros_like(ggco_ref)
            ggao_ref[...] = jnp.zeros_like(ggao_ref)

        w = w_ref[...]
        zeros = jnp.zeros((HALO, c), F32)
        gb_e = jnp.concatenate([zeros, gb_ref[...], gbn_ref[...]], axis=0)
        cx_prev = jnp.where(i > 0, gcp_ref[...] * xcp_ref[...], 0.0)
        gc_e = jnp.concatenate([zeros, gc_ref[...], gcn_ref[...]], axis=0)
        xc_e = jnp.concatenate([zeros, xc_ref[...], xcn_ref[...]], axis=0)
        cx_e = jnp.concatenate([cx_prev, gc_ref[...] * xc_ref[...], gcn_ref[...] * xcn_ref[...]], axis=0)
        dn_next = jnp.where(i < n_blk - 1, dncn_ref[...], 0.0)
        dn_e = jnp.concatenate([zeros, dnc_ref[...], dn_next], axis=0)

        conv_e = _conv_taps(cx_e, w)
        y_e = gb_e * conv_e
        r_e = _rstd(y_e)
        yn_e = y_e * r_e
        t_e = dn_e * gco_ref[...]
        dy_e = r_e * (t_e - yn_e * jnp.mean(t_e * yn_e, axis=-1, keepdims=True))
        dconv_e = dy_e * gb_e
        dcx_e = _conv_taps_t(dconv_e, w)
        blk = slice(HALO, HALO + tm)
        dz_ref[:, 0:c] = (dy_e * conv_e)[blk].astype(BF16)
        dz_ref[:, c:2 * c] = (dcx_e * xc_e)[blk].astype(BF16)
        dz_ref[:, 2 * c:3 * c] = (dcx_e * gc_e)[blk].astype(BF16)
        ggco_ref[...] += jnp.sum((dn_e * yn_e)[blk], axis=0, keepdims=True)
        dconv = dconv_e[blk]
        gw_ref[0:1, :] += jnp.sum(dconv * _shift_down(cx_e, 2)[blk], axis=0, keepdims=True)
        gw_ref[1:2, :] += jnp.sum(dconv * _shift_down(cx_e, 1)[blk], axis=0, keepdims=True)
        gw_ref[2:3, :] += jnp.sum(dconv * cx_e[blk], axis=0, keepdims=True)

        o = o_ref[...]
        ra = _rstd(o)
        on = o * ra
        dna = dna_ref[...]
        ggao_ref[...] += jnp.sum(dna * on, axis=0, keepdims=True)
        ta = dna * gao_ref[...]
        do = ra * (ta - on * jnp.mean(ta * on, axis=-1, keepdims=True))
        prod = do * o
        lane = lax.broadcasted_iota(jnp.int32, (tm, LANES), 1)
        head_a = lane < HEAD_DIM
        for p in range(N_PAIRS):
            cols = slice(p * LANES, (p + 1) * LANES)
            pb, dob = prod[:, cols], do[:, cols]
            for hh in range(2):
                sel = head_a if hh == 0 else jnp.logical_not(head_a)
                delta = jnp.sum(jnp.where(sel, pb, 0.0), axis=-1, keepdims=True)
                neg3 = _split3(-delta)
                do_h = pltpu.roll(dob, HEAD_DIM, 1) if hh else dob
                dox_ref[2 * p + hh] = _aug(do_h, lane, neg3).astype(BF16)
        finish()

    col = lambda j: pl.BlockSpec((tm, c), lambda i: (i, j))
    vec = pl.BlockSpec((1, c), lambda i: (0, 0))
    w3 = pl.BlockSpec((3, c), lambda i: (0, 0))
    outs = pl.pallas_call(
        body, name="mixer_bwd", grid=(n_blk,),
        in_specs=[col(0), col(1), col(2), _prev_halo_spec(tm, c, 1), _prev_halo_spec(tm, c, 2),
                  _next_halo_spec(tm, c, 0, s), _next_halo_spec(tm, c, 1, s), _next_halo_spec(tm, c, 2, s),
                  col(0), col(0), _next_halo_spec(tm, c, 0, s), col(1), w3, vec, vec] + _hbm_specs(n_sc),
        out_specs=[pl.BlockSpec((tm, 3 * c), lambda i: (i, 0)),
                   pl.BlockSpec((N_HEADS, tm, LANES), lambda i: (0, i, 0)), w3, vec, vec] + _hbm_specs(n_sc),
        out_shape=[jax.ShapeDtypeStruct((s, 3 * c), BF16), jax.ShapeDtypeStruct((N_HEADS, s, LANES), BF16),
                   jax.ShapeDtypeStruct((3, c), F32), jax.ShapeDtypeStruct((1, c), F32),
                   jax.ShapeDtypeStruct((1, c), F32)] + got_shapes,
        scratch_shapes=sems, compiler_params=_cp(("arbitrary",)))(
            z_a, z_a, z_a, z_a, z_a, z_a, z_a, z_a, o_attn, dmix, dmix, dmix, w_conv, g_conv_out, g_attn_out,
            *scatter)
    return tuple(outs[:5]) + (outs[5:],)


def _gate_fwd(f, b_pad):
    s = f.shape[0]
    tm = min(TQ, s)

    def body(f_ref, b_ref, fb_ref, carry):
        @pl.when(pl.program_id(0) == 0)
        def _():
            carry[...] = jnp.zeros_like(carry)

        z = f_ref[...] + b_ref[...]
        x = jnp.minimum(z, 0.0) - jnp.log1p(jnp.exp(-jnp.abs(z)))
        row = lax.broadcasted_iota(jnp.int32, (tm, LANES), 0)
        sh = 1
        while sh < tm:
            x = x + jnp.where(row >= sh, _shift_down(x, sh), 0.0)
            sh *= 2
        x = x + carry[0:1, :]
        carry[...] = jnp.broadcast_to(x[tm - 1:tm, :], carry.shape)
        head_a = lax.broadcasted_iota(jnp.int32, (tm, LANES), 1) < HEAD_DIM
        for p in range(N_PAIRS):
            fa = jnp.broadcast_to(x[:, 2 * p:2 * p + 1], (tm, LANES))
            fbv = jnp.broadcast_to(x[:, 2 * p + 1:2 * p + 2], (tm, LANES))
            fb_ref[:, p * LANES:(p + 1) * LANES] = jnp.where(head_a, fa, fbv)

    return pl.pallas_call(
        body, name="gate_fwd", grid=(s // tm,),
        in_specs=[pl.BlockSpec((tm, LANES), lambda i: (i, 0)), pl.BlockSpec((1, LANES), lambda i: (0, 0))],
        out_specs=pl.BlockSpec((tm, N_PAIRS * LANES), lambda i: (i, 0)),
        out_shape=jax.ShapeDtypeStruct((s, N_PAIRS * LANES), F32),
        scratch_shapes=[pltpu.VMEM((HALO, LANES), F32)],
        compiler_params=_cp(("arbitrary",)))(f, b_pad)


def _gate_bwd(f, b_pad, d_f):
    s = f.shape[0]
    tm = min(TQ, s)
    n_blk = s // tm

    def body(f_ref, b_ref, d_ref, df_ref, gb_ref, carry):
        @pl.when(pl.program_id(0) == 0)
        def _():
            carry[...] = jnp.zeros_like(carry)
            gb_ref[...] = jnp.zeros_like(gb_ref)

        x = d_ref[...]
        row = lax.broadcasted_iota(jnp.int32, (tm, LANES), 0)
        sh = 1
        while sh < tm:
            x = x + jnp.where(row < tm - sh, _shift_up(x, sh), 0.0)
            sh *= 2
        x = x + carry[0:1, :]
        carry[...] = jnp.broadcast_to(x[0:1, :], carry.shape)
        z = f_ref[...] + b_ref[...]
        d = x * (1.0 / (1.0 + jnp.exp(z)))
        df_ref[...] = d.astype(BF16)
        gb_ref[...] += jnp.sum(d, axis=0, keepdims=True)

    rev = pl.BlockSpec((tm, LANES), lambda i: (n_blk - 1 - i, 0))
    vec = pl.BlockSpec((1, LANES), lambda i: (0, 0))
    return pl.pallas_call(
        body, name="gate_bwd", grid=(n_blk,), in_specs=[rev, vec, rev], out_specs=[rev, vec],
        out_shape=[jax.ShapeDtypeStruct((s, LANES), BF16), jax.ShapeDtypeStruct((1, LANES), F32)],
        scratch_shapes=[pltpu.VMEM((HALO, LANES), F32)],
        compiler_params=_cp(("arbitrary",)))(f, b_pad, d_f)


_NT = (((1,), (1,)), ((), ()))
_NN = (((1,), (0,)), ((), ()))


AUG = HEAD_DIM
NORM_MARGIN = 1.01


def _split3(x):
    hi = x.astype(BF16).astype(F32)
    r = x - hi
    mid = r.astype(BF16).astype(F32)
    lo = (r - mid).astype(BF16).astype(F32)
    return hi, mid, lo


def _aug(base, lane, vals):
    out = jnp.where(lane < AUG, base, 0.0)
    for k, v in enumerate(vals):
        out = jnp.where(lane == AUG + k, v, out)
    return out


def _attn_prep(qkv, fb, bigs):
    s = qkv.shape[0]
    tq = min(TQ, s)
    n_q = s // tq

    n = len(bigs)
    arrays, landing, sems = _gather_operands(bigs, [])

    def body(q_ref, k_ref, v_ref, fb_ref, *rest):
        qx_ref, kx_ref, kxt_ref, vx_ref, vt_ref, b_ref = rest[2 * n:2 * n + 6]
        finish = _hosted_gather((rest[:n], rest[2 * n + 6:3 * n + 6]) + tuple(rest[3 * n + 6:]), n, n,
                                pl.program_id(0), n_q)
        lane = lax.broadcasted_iota(jnp.int32, (tq, LANES), 1)
        lane8 = lax.broadcasted_iota(jnp.int32, (HALO, LANES), 1)
        head_lanes = lane < AUG
        is_lane = [lane == AUG + k for k in range(6)]
        first3 = (lane >= AUG) & (lane < AUG + 3)
        next3 = (lane >= AUG + 3) & (lane < AUG + 6)
        q_const = jnp.where(first3, -1.0, 0.0)
        k_const = jnp.where(next3, 1.0, 0.0)
        v_const = jnp.where(first3, 1.0, 0.0)
        ones_head = (lax.broadcasted_iota(jnp.int32, (LANES, LANES), 0) < HEAD_DIM).astype(BF16)
        acc = jnp.zeros((HALO, LANES), F32)
        for p in range(N_PAIRS):
            cols = slice(p * LANES, (p + 1) * LANES)
            q2, k2, v2 = (ref[:, cols].astype(F32) for ref in (q_ref, k_ref, v_ref))
            q2 = q2 * Q_SCALE
            f2 = fb_ref[:, cols]
            for hh in range(2):
                h = 2 * p + hh
                q, k, v = ((pltpu.roll(x, HEAD_DIM, 1) if hh else x) for x in (q2, k2, v2))
                f = f2 if hh else pltpu.roll(f2, HEAD_DIM, 1)
                hi, mid, lo = _split3(f)
                q_aug = jnp.where(is_lane[3], hi, jnp.where(is_lane[4], mid, jnp.where(is_lane[5], lo, q_const)))
                k_aug = jnp.where(is_lane[0], hi, jnp.where(is_lane[1], mid, jnp.where(is_lane[2], lo, k_const)))
                kx = jnp.where(head_lanes, k, k_aug)
                vx = jnp.where(head_lanes, v, v_const)
                qx_ref[h] = jnp.where(head_lanes, q, q_aug).astype(BF16)
                kx_ref[h] = kx.astype(BF16)
                vx_ref[h] = vx.astype(BF16)
                kxt_ref[h, 0] = kx.T.astype(BF16)
                vt_ref[h, 0] = vx.T.astype(BF16)
                q_sq = lax.dot_general((q * q).astype(BF16), ones_head, _NN, preferred_element_type=F32)
                k_sq = lax.dot_general((k * k).astype(BF16), ones_head, _NN, preferred_element_type=F32)
                qk = jnp.sqrt(q_sq * k_sq) * NORM_MARGIN + f
                vals = (jnp.sqrt(jnp.max(q_sq, axis=0, keepdims=True)), jnp.sqrt(jnp.max(k_sq, axis=0, keepdims=True)),
                        jnp.max(qk, axis=0, keepdims=True), f[tq - 1:tq, :])
                for slot, val in enumerate(vals):
                    acc = jnp.where(lane8 == slot * N_HEADS + h, val[:, AUG:AUG + 1], acc)
        b_ref[0] = acc
        finish()

    blk = lambda j: pl.BlockSpec((tq, ATTN_W), lambda i: (i, j))
    rows = pl.BlockSpec((N_HEADS, tq, LANES), lambda i: (0, i, 0))
    cols_t = pl.BlockSpec((N_HEADS, 1, LANES, tq), lambda i: (0, i, 0, 0))
    shp = jax.ShapeDtypeStruct((N_HEADS, s, LANES), BF16)
    shp_t = jax.ShapeDtypeStruct((N_HEADS, n_q, LANES, tq), BF16)
    outs = pl.pallas_call(
        body, name="attn_prep", grid=(n_q,), in_specs=[blk(0), blk(1), blk(2), blk(0)] + _hbm_specs(2 * n),
        out_specs=[rows, rows, cols_t, rows, cols_t,
                   pl.BlockSpec((1, HALO, LANES), lambda i: (i, 0, 0))] + _hbm_specs(n),
        out_shape=[shp, shp, shp_t, shp, shp_t, jax.ShapeDtypeStruct((n_q, HALO, LANES), F32)]
        + [jax.ShapeDtypeStruct(b.shape, b.dtype) for b in landing],
        input_output_aliases={4 + n + k: 6 + k for k in range(n)}, scratch_shapes=sems,
        compiler_params=_cp(("arbitrary",)))(qkv, qkv, qkv, fb, *arrays, *landing)
    return tuple(outs[:6]) + (outs[6:],)


def _key_block_ranges(bounds):
    t = bounds[:, 0, :]
    nh = N_HEADS
    a, b, c, e = t[:, 0:nh], t[:, nh:2 * nh], t[:, 2 * nh:3 * nh], t[:, 3 * nh:4 * nh]
    bound = a[:, None, :] * b[None, :, :] * NORM_MARGIN + c[:, None, :] - e[None, :, :]
    n_q = t.shape[0]
    idx = jnp.arange(n_q)
    need = jnp.logical_not(bound < -(EXP_ZERO + 2.0)) | (idx[None, :, None] >= idx[:, None, None])
    first = jnp.argmax(need, axis=1).astype(jnp.int32)
    first = jnp.min(first.reshape(n_q, N_PAIRS, 2), axis=-1)
    visits = (first[:, None, :] <= idx[None, :, None]) & (idx[None, :, None] <= idx[:, None, None])
    last = jnp.max(jnp.where(visits, idx[:, None, None], 0), axis=0).astype(jnp.int32)
    return first.T.reshape(-1), last.T.reshape(-1)


def _attn_fwd_t(qx, kx, vt, first_blk):
    _, s, _ = qx.shape
    tq = min(TQ, s)
    n_q = s // tq
    neg = -1e30

    def body(first_ref, qx_ref, kx_ref, vt_ref, o_ref, lse_ref, acc_ref, m_ref, s_even, s_odd):
        p = pl.program_id(0)
        i = pl.program_id(1)
        acc_ref[...] = jnp.zeros(acc_ref.shape, F32)
        m_ref[...] = jnp.full(m_ref.shape, neg, F32)
        key_le_query = (lax.broadcasted_iota(jnp.int32, (tq, tq), 0) <= lax.broadcasted_iota(jnp.int32, (tq, tq), 1))
        first = first_ref[p * n_q + i]

        def scores(kb, hh, dst):
            rows_k = pl.ds(pl.multiple_of(kb * tq, tq), tq)
            dst[hh] = lax.dot_general(kx_ref[hh, rows_k, :], qx_ref[hh], _NT, preferred_element_type=F32)

        def step(kb, src, nxt):
            for hh in range(2):
                st = src[hh]
                if nxt is None:
                    st = jnp.where(key_le_query, st, -jnp.inf)
                else:
                    scores(kb + 1, hh, nxt)
                m_old = m_ref[hh]
                m_new = jnp.maximum(m_old, jnp.max(st, axis=0, keepdims=True))
                m_ref[hh] = m_new
                pt = jnp.exp(st - m_new).astype(BF16)
                acc_ref[hh] = acc_ref[hh] * jnp.exp(m_old - m_new) + lax.dot_general(
                    vt_ref[hh, kb], pt, _NN, preferred_element_type=F32)

        def by_parity(kb, fn):
            @pl.when(kb % 2 == 0)
            def _():
                fn(s_even, s_odd)

            @pl.when(kb % 2 == 1)
            def _():
                fn(s_odd, s_even)

        def first_scores(src, nxt):
            scores(first, 0, src)
            scores(first, 1, src)

        def unmasked(kb, carry):
            by_parity(kb, lambda src, nxt: step(kb, src, nxt))
            return carry

        by_parity(first, first_scores)
        lax.fori_loop(first, i, unmasked, 0)
        by_parity(i, lambda src, nxt: step(i, src, None))
        outs, lses = [], []
        for hh in range(2):
            acc = acc_ref[hh]
            l = acc[AUG:AUG + 1, :]
            outs.append(acc[0:HEAD_DIM, :] / l)
            lses.append(m_ref[hh] + jnp.log(l))
        o_ref[...] = jnp.concatenate(outs, axis=0).T
        rows8 = lax.broadcasted_iota(jnp.int32, (N_HEADS, tq), 0)
        lse_ref[0, 0] = jnp.where(rows8 == 0, lses[0], jnp.where(rows8 == 1, lses[1], 0.0))

    grid_spec = pltpu.PrefetchScalarGridSpec(
        num_scalar_prefetch=1, grid=(N_PAIRS, n_q),
        in_specs=[pl.BlockSpec((2, tq, LANES), lambda p, i, first: (p, i, 0)),
                  pl.BlockSpec((2, s, LANES), lambda p, i, first: (p, 0, 0)),
                  pl.BlockSpec((2, n_q, LANES, tq), lambda p, i, first: (p, 0, 0, 0))],
        out_specs=[pl.BlockSpec((tq, LANES), lambda p, i, first: (i, p)),
                   pl.BlockSpec((1, 1, N_HEADS, tq), lambda p, i, first: (p, i, 0, 0))],
        scratch_shapes=[pltpu.VMEM((2, LANES, tq), F32), pltpu.VMEM((2, 1, tq), F32),
                        pltpu.VMEM((2, tq, tq), F32), pltpu.VMEM((2, tq, tq), F32)])
    return pl.pallas_call(
        body, name="attn_fwd", grid_spec=grid_spec,
        out_shape=[jax.ShapeDtypeStruct((s, ATTN_W), F32), jax.ShapeDtypeStruct((N_PAIRS, n_q, N_HEADS, tq), F32)],
        compiler_params=_cp(("parallel", "arbitrary")))(first_blk, qx, kx, vt)


def _attn_bwd_t(qx, dox, kx, kxt, vx, lse, last_blk):
    _, s, _ = qx.shape
    tq = min(TQ, s)
    n_q = s // tq

    def body(last_ref, qx_ref, dox_ref, lse_ref, kx_ref, kxt_ref, vx_ref, dk_ref, dv_ref, dfk_ref, dqt_ref, dfq_ref):
        p = pl.program_id(0)
        j = pl.program_id(1)

        @pl.when(j == 0)
        def _():
            dqt_ref[...] = jnp.zeros(dqt_ref.shape, F32)
            dfq_ref[...] = jnp.zeros(dfq_ref.shape, F32)

        key_le_query = (lax.broadcasted_iota(jnp.int32, (tq, tq), 0) <= lax.broadcasted_iota(jnp.int32, (tq, tq), 1))

        def step(i, carry, masked):
            rows_q = pl.ds(pl.multiple_of(i * tq, tq), tq)
            out = []
            for hh in range(2):
                dk, dv, col = carry[3 * hh:3 * hh + 3]
                q, do = qx_ref[hh, rows_q, :], dox_ref[hh, rows_q, :]
                st = lax.dot_general(kx_ref[hh], q, _NT, preferred_element_type=F32)
                pt = jnp.exp(st - lse_ref[0, i, hh:hh + 1, :])
                if masked:
                    pt = jnp.where(key_le_query, pt, 0.0)
                dst = pt * lax.dot_general(vx_ref[hh], do, _NT, preferred_element_type=F32)
                pb, dsb = pt.astype(BF16), dst.astype(BF16)
                dv = dv + lax.dot_general(pb, do, _NN, preferred_element_type=F32)
                dk = dk + lax.dot_general(dsb, q, _NN, preferred_element_type=F32)
                dqt_ref[hh, i] += lax.dot_general(kxt_ref[hh, 0], dsb, _NN, preferred_element_type=F32)
                for cb in range(tq // LANES):
                    col = col + dst[:, cb * LANES:(cb + 1) * LANES]
                dfq_ref[hh, i] += jnp.sum(dst.reshape(tq // HALO, HALO, tq), axis=0)
                out += [dk, dv, col]
            return tuple(out)

        zero = jnp.zeros((tq, LANES), F32)
        carry = step(j, (zero,) * 6, True)
        dk_a, dv_a, col_a, dk_b, dv_b, col_b = lax.fori_loop(j + 1, last_ref[p * n_q + j] + 1,
                                                             lambda i, cr: step(i, cr, False), carry)
        head_a = lax.broadcasted_iota(jnp.int32, (tq, LANES), 1) < HEAD_DIM
        dk_ref[...] = jnp.where(head_a, dk_a, pltpu.roll(dk_b, HEAD_DIM, 1)).astype(BF16)
        dv_ref[...] = jnp.where(head_a, dv_a, pltpu.roll(dv_b, HEAD_DIM, 1)).astype(BF16)
        rows8 = lax.broadcasted_iota(jnp.int32, (N_HEADS, tq), 0)
        dfk_a, dfk_b = (-jnp.sum(c.T, axis=0, keepdims=True) for c in (col_a, col_b))
        dfk_ref[0, 0] = jnp.where(rows8 == 0, dfk_a, jnp.where(rows8 == 1, dfk_b, 0.0))

    resident = pl.BlockSpec((2, s, LANES), lambda p, j, last: (p, 0, 0))
    key_rows = pl.BlockSpec((2, tq, LANES), lambda p, j, last: (p, j, 0))
    pair_out = pl.BlockSpec((tq, LANES), lambda p, j, last: (j, p))
    grid_spec = pltpu.PrefetchScalarGridSpec(
        num_scalar_prefetch=1, grid=(N_PAIRS, n_q),
        in_specs=[resident, resident, pl.BlockSpec((1, n_q, N_HEADS, tq), lambda p, j, last: (p, 0, 0, 0)),
                  key_rows, pl.BlockSpec((2, 1, LANES, tq), lambda p, j, last: (p, j, 0, 0)), key_rows],
        out_specs=[pair_out, pair_out, pl.BlockSpec((1, 1, N_HEADS, tq), lambda p, j, last: (p, j, 0, 0)),
                   pl.BlockSpec((2, n_q, LANES, tq), lambda p, j, last: (p, 0, 0, 0)),
                   pl.BlockSpec((2, n_q, HALO, tq), lambda p, j, last: (p, 0, 0, 0))])
    return pl.pallas_call(
        body, name="attn_bwd", grid_spec=grid_spec,
        out_shape=[jax.ShapeDtypeStruct((s, ATTN_W), BF16), jax.ShapeDtypeStruct((s, ATTN_W), BF16),
                   jax.ShapeDtypeStruct((N_PAIRS, n_q, N_HEADS, tq), F32),
                   jax.ShapeDtypeStruct((N_HEADS, n_q, LANES, tq), F32),
                   jax.ShapeDtypeStruct((N_HEADS, n_q, HALO, tq), F32)],
        compiler_params=_cp(("parallel", "arbitrary")))(last_blk, qx, dox, lse, kx, kxt, vx)


def _attn_dq_finish(dqt):
    _, n_q, _, tq = dqt.shape
    per = 4 if n_q % 4 == 0 else 1

    def body(dqt_ref, dq_ref):
        for b in range(per):
            a, bb = dqt_ref[0, b], dqt_ref[1, b]
            dq_ref[b * tq:(b + 1) * tq, :] = (
                jnp.concatenate([a[0:HEAD_DIM], bb[0:HEAD_DIM]], axis=0).T * Q_SCALE).astype(BF16)

    return pl.pallas_call(
        body, name="attn_dq_finish", grid=(N_PAIRS, n_q // per),
        in_specs=[pl.BlockSpec((2, per, LANES, tq), lambda p, i: (p, i, 0, 0))],
        out_specs=pl.BlockSpec((per * tq, LANES), lambda p, i: (i, p)),
        out_shape=jax.ShapeDtypeStruct((n_q * tq, ATTN_W), BF16),
        compiler_params=_cp(("parallel", "parallel")))(dqt)


def _ffn_act_fwd(up, w_ffn):
    s = up.shape[0]
    tm, tn = min(TM_FFN, s), TN_FFN
    nb = D_FF // tn

    def body(a_ref, g_ref, ap_ref, gp_ref, wa_ref, wg_ref, act_ref):
        i = pl.program_id(1)

        def conv(blk_ref, prev_ref, w_ref):
            prev = jnp.where(i > 0, prev_ref[...], 0.0)
            return _conv_taps(jnp.concatenate([prev, blk_ref[...]], axis=0), w_ref[...])[HALO:]

        u_a, u_g = conv(a_ref, ap_ref, wa_ref), conv(g_ref, gp_ref, wg_ref)
        act_ref[...] = (u_g * jax.nn.sigmoid(u_g) * u_a).astype(BF16)

    blk = lambda off: pl.BlockSpec((tm, tn), lambda n, i: (i, off + n))
    prev = lambda off: pl.BlockSpec((HALO, tn), lambda n, i: (jnp.maximum(i * (tm // HALO) - 1, 0), off + n))
    wsp = lambda off: pl.BlockSpec((3, tn), lambda n, i: (0, off + n))
    return pl.pallas_call(
        body, name="ffn_act_fwd", grid=(nb, s // tm),
        in_specs=[blk(0), blk(nb), prev(0), prev(nb), wsp(0), wsp(nb)],
        out_specs=pl.BlockSpec((tm, tn), lambda n, i: (i, n)),
        out_shape=jax.ShapeDtypeStruct((s, D_FF), BF16),
        compiler_params=_cp(("parallel", "parallel")))(up, up, up, up, w_ffn, w_ffn)


def _ffn_act_bwd(up, dact, w_ffn):
    s = up.shape[0]
    tm, tn = min(TM_FFN, s), TN_FFN
    nb = D_FF // tn
    n_blk = s // tm

    def body(a_ref, g_ref, ap_ref, gp_ref, an_ref, gn_ref, d_ref, dn_ref, wa_ref, wg_ref,
             dup_ref, gwa_ref, gwg_ref):
        i = pl.program_id(1)

        @pl.when(i == 0)
        def _():
            gwa_ref[...] = jnp.zeros_like(gwa_ref)
            gwg_ref[...] = jnp.zeros_like(gwg_ref)

        def ext(blk_ref, prev_ref, next_ref):
            return jnp.concatenate([jnp.where(i > 0, prev_ref[...], 0.0), blk_ref[...], next_ref[...]], axis=0)

        wa, wg = wa_ref[...], wg_ref[...]
        up_a, up_g = ext(a_ref, ap_ref, an_ref), ext(g_ref, gp_ref, gn_ref)
        u_a, u_g = _conv_taps(up_a, wa), _conv_taps(up_g, wg)
        d_e = jnp.concatenate([jnp.zeros((HALO, tn), F32), d_ref[...],
                               jnp.where(i < n_blk - 1, dn_ref[...], 0.0)], axis=0)
        sig = jax.nn.sigmoid(u_g)
        du_a = d_e * (u_g * sig)
        du_g = d_e * u_a * (sig * (1.0 + u_g * (1.0 - sig)))
        blk = slice(HALO, HALO + tm)
        dup_ref[0] = _conv_taps_t(du_a, wa)[blk].astype(BF16)
        dup_ref[1] = _conv_taps_t(du_g, wg)[blk].astype(BF16)
        for gw_ref, upv, du in ((gwa_ref, up_a[blk], du_a), (gwg_ref, up_g[blk], du_g)):
            gw_ref[0:1, :] += jnp.sum(upv * _shift_up(du, 2)[blk], axis=0, keepdims=True)
            gw_ref[1:2, :] += jnp.sum(upv * _shift_up(du, 1)[blk], axis=0, keepdims=True)
            gw_ref[2:3, :] += jnp.sum(upv * du[blk], axis=0, keepdims=True)

    blk = lambda off: pl.BlockSpec((tm, tn), lambda n, i: (i, off + n))
    prev = lambda off: pl.BlockSpec((HALO, tn), lambda n, i: (jnp.maximum(i * (tm // HALO) - 1, 0), off + n))
    nxt = lambda off: pl.BlockSpec(
        (HALO, tn), lambda n, i: (jnp.minimum((i + 1) * (tm // HALO), s // HALO - 1), off + n))
    wsp = lambda off: pl.BlockSpec((3, tn), lambda n, i: (0, off + n))
    return pl.pallas_call(
        body, name="ffn_act_bwd", grid=(nb, n_blk),
        in_specs=[blk(0), blk(nb), prev(0), prev(nb), nxt(0), nxt(nb), blk(0), nxt(0), wsp(0), wsp(nb)],
        out_specs=[pl.BlockSpec((2, tm, tn), lambda n, i: (0, i, n)), wsp(0), wsp(0)],
        out_shape=[jax.ShapeDtypeStruct((2, s, D_FF), BF16),
                   jax.ShapeDtypeStruct((3, D_FF), F32), jax.ShapeDtypeStruct((3, D_FF), F32)],
        compiler_params=_cp(("parallel", "arbitrary")))(up, up, up, up, up, up, dact, dact, w_ffn, w_ffn)


def _adamw(w, g, m, v, name):
    r, c = w.shape
    tr = next((t for t in (512, 352, 256, 128, 64, 32, 16, 8) if r > t and r % t == 0), r)

    def body(w_ref, g_ref, m_ref, v_ref, d_ref, nm_ref, nv_ref):
        gv = g_ref[...]
        m_new = ADAM_B1 * m_ref[...] + (1.0 - ADAM_B1) * gv
        v_new = ADAM_B2 * v_ref[...] + (1.0 - ADAM_B2) * (gv * gv)
        m_hat = m_new / (1.0 - ADAM_B1 ** ADAM_STEP)
        v_hat = v_new / (1.0 - ADAM_B2 ** ADAM_STEP)
        d_ref[...] = -ADAM_LR * (m_hat / (jnp.sqrt(v_hat) + ADAM_EPS) + ADAM_WD * w_ref[...])
        nm_ref[...] = m_new
        nv_ref[...] = v_new

    spec = pl.BlockSpec((tr, c), lambda i: (i, 0))
    shp = jax.ShapeDtypeStruct((r, c), F32)
    return pl.pallas_call(
        body, name=name, grid=(r // tr,), in_specs=[spec] * 4, out_specs=[spec] * 3, out_shape=[shp] * 3,
        compiler_params=_cp(("parallel",)))(w, g, m, v)


def _sum_rows_block(h):
    return h if h <= 352 else 256


def _pair_sum(view, recv, sel, name):
    n, _, h, c = view.shape
    tr = _sum_rows_block(h)

    def body(sel_ref, a_ref, b_ref, o_ref, ob_ref):
        t = a_ref[...] + b_ref[...]
        o_ref[...] = t
        ob_ref[...] = t.astype(BF16)

    blk = pl.BlockSpec((None, tr, c), lambda j, i, sel_ref: (j, i, 0))
    grid_spec = pltpu.PrefetchScalarGridSpec(
        num_scalar_prefetch=1, grid=(n, h // tr),
        in_specs=[pl.BlockSpec((None, None, tr, c), lambda j, i, sel_ref: (j, sel_ref[0], i, 0)),
                  pl.BlockSpec((None, None, tr, c), lambda j, i, sel_ref: (j, 0, i, 0))],
        out_specs=[blk, blk])
    return pl.pallas_call(
        body, name=name, grid_spec=grid_spec,
        out_shape=[jax.ShapeDtypeStruct((n, h, c), F32), jax.ShapeDtypeStruct((n, h, c), BF16)],
        compiler_params=_cp(("parallel", "parallel")))(sel, view, recv)


def _chip_sum(pair, got, sel, name):
    _, h, c = pair.shape
    tr = _sum_rows_block(h)
    nblk = h // tr

    def body(sel_ref, p_ref, g0_ref, g1_ref, g2_ref, o_ref):
        o_ref[...] = ((p_ref[...] + g0_ref[...].astype(F32)) + g1_ref[...].astype(F32)) + g2_ref[...].astype(F32)

    slot = lambda k: pl.BlockSpec((None, tr, c), lambda i, sel_ref: (k, i, 0))
    grid_spec = pltpu.PrefetchScalarGridSpec(
        num_scalar_prefetch=1, grid=(h // tr,),
        in_specs=[pl.BlockSpec((None, tr, c), lambda i, sel_ref: (sel_ref[1], i, 0)), slot(0), slot(1), slot(2)],
        out_specs=pl.BlockSpec((tr, c), lambda i, sel_ref: (sel_ref[0] * nblk + i, 0)))
    return pl.pallas_call(
        body, name=name, grid_spec=grid_spec, out_shape=jax.ShapeDtypeStruct((2 * h, c), F32),
        compiler_params=_cp(("parallel",)))(sel, pair, got, got, got)


def _place():
    return lax.axis_index("x"), lax.axis_index("y"), lax.axis_index("c")


def _other_chips(x, y):
    return [(1 - x, y), (x, 1 - y), (1 - x, 1 - y)]


def _hbm_specs(n):
    return [pl.BlockSpec(memory_space=pl.ANY)] * n


def _all_gather_weights(bigs, smalls):
    nb, ns = len(bigs), len(smalls)
    n = nb + ns

    def body(*refs):
        start, forward, finish = _gather_phases(refs[:n], refs[2 * n:3 * n], nb, *refs[3 * n:])
        start()
        forward()
        finish()

    arrays, landing, sems = _gather_operands(bigs, smalls)
    return pl.pallas_call(
        body, name="all_gather_weights",
        out_shape=[jax.ShapeDtypeStruct(b.shape, b.dtype) for b in landing],
        in_specs=_hbm_specs(2 * n), out_specs=_hbm_specs(n), input_output_aliases={n + k: k for k in range(n)},
        scratch_shapes=sems)(*arrays, *landing)


def _hosted_gather(refs, n, nb, step, total):
    ins, outs, send_sems, recv_sems = refs
    start, forward, finish = _gather_phases(ins, outs, nb, send_sems, recv_sems)
    pl.when(step == 0)(start)
    pl.when(step == (3 * total) // 4)(forward)
    return lambda: pl.when(step == total - 1)(finish)


def _in_proj(x, g, w_a, w_b, w_c, bigs, smalls):
    s, d = x.shape
    tm, tn = min(TM_MM, s), ATTN_W
    na, nq = w_a.shape[1] // tn, w_b.shape[1] // tn
    steps = na + nq + 1
    total = (s // tm) * steps
    nb, n = len(bigs), len(bigs) + len(smalls)
    arrays, landing, sems = _gather_operands(bigs, smalls)

    def body(x_ref, g_ref, wa_ref, wb_ref, wc_ref, *rest):
        z_ref, qkv_ref, f_ref, h_ref = rest[2 * n:2 * n + 4]
        h_scr = rest[-1]
        m, j = pl.program_id(0), pl.program_id(1)
        finish = _hosted_gather((rest[:n], rest[2 * n + 4:3 * n + 4]) + tuple(rest[3 * n + 4:3 * n + 6]), n, nb,
                                m * steps + j, total)

        @pl.when(j == 0)
        def _():
            xv = x_ref[...]
            hv = (xv * _rstd(xv) * g_ref[...]).astype(BF16)
            h_scr[...] = hv
            h_ref[...] = hv

        h = h_scr[...]

        @pl.when(j < na)
        def _():
            z_ref[...] = lax.dot_general(h, wa_ref[...], _NN, preferred_element_type=F32)

        @pl.when((j >= na) & (j < na + nq))
        def _():
            qkv_ref[...] = lax.dot_general(h, wb_ref[...], _NN, preferred_element_type=F32).astype(BF16)

        @pl.when(j == na + nq)
        def _():
            f_ref[...] = lax.dot_general(h, wc_ref[...], _NN, preferred_element_type=F32)

        finish()

    blk_a = lambda m, j: (m, jnp.minimum(j, na - 1))
    blk_b = lambda m, j: (m, jnp.clip(j - na, 0, nq - 1))
    outs = pl.pallas_call(
        body, name="in_proj", grid=(s // tm, steps),
        in_specs=[pl.BlockSpec((tm, d), lambda m, j: (m, 0)), pl.BlockSpec((1, d), lambda m, j: (0, 0)),
                  pl.BlockSpec((d, tn), lambda m, j: (0, jnp.minimum(j, na - 1))),
                  pl.BlockSpec((d, tn), lambda m, j: (0, jnp.clip(j - na, 0, nq - 1))),
                  pl.BlockSpec((d, LANES), lambda m, j: (0, 0))] + _hbm_specs(2 * n),
        out_specs=[pl.BlockSpec((tm, tn), blk_a), pl.BlockSpec((tm, tn), blk_b),
                   pl.BlockSpec((tm, LANES), lambda m, j: (m, 0)), pl.BlockSpec((tm, d), lambda m, j: (m, 0))]
        + _hbm_specs(n),
        out_shape=[jax.ShapeDtypeStruct((s, w_a.shape[1]), F32), jax.ShapeDtypeStruct((s, w_b.shape[1]), BF16),
                   jax.ShapeDtypeStruct((s, LANES), F32), jax.ShapeDtypeStruct((s, d), BF16)]
        + [jax.ShapeDtypeStruct(b.shape, b.dtype) for b in landing],
        input_output_aliases={5 + n + k: 4 + k for k in range(n)},
        scratch_shapes=sems + [pltpu.VMEM((tm, d), BF16)],
        compiler_params=_cp(("arbitrary", "arbitrary")))(x, g, w_a, w_b, w_c, *arrays, *landing)
    return outs[0], outs[1], outs[2], outs[3], outs[4:]


def _gather_operands(bigs, smalls):
    x, y, _ = _place()
    arrays = list(bigs) + list(smalls)
    landing = [lax.dynamic_update_index_in_dim(lax.empty((N_CHIPS,) + a.shape, a.dtype), a, 2 * x + y, 0)
               for a in arrays]
    n_sems = 6 * len(bigs) + 3 * len(smalls)
    return arrays, landing, [pltpu.SemaphoreType.DMA((n_sems,)), pltpu.SemaphoreType.DMA((n_sems,))]


def _gather_phases(ins, outs, nb, send_sems, recv_sems):
    n = len(ins)
    x, y, c = _place()
    my_chip = 2 * x + y
    chips = _other_chips(x, y)
    sibling = (x, y, 1 - c)

    def rows(k, which):
        h = ins[k].shape[0] // 2
        return pl.ds(which * h, h)

    def copy(sem, src, dst, to):
        return pltpu.make_async_remote_copy(src_ref=src, dst_ref=dst, send_sem=send_sems.at[sem],
                                            recv_sem=recv_sems.at[sem], device_id=to, device_id_type=MESH)

    def sends():
        out = [copy(6 * k + j, ins[k].at[rows(k, c)], outs[k].at[my_chip, rows(k, c)], (cx, cy, c))
               for k in range(nb) for j, (cx, cy) in enumerate(chips)]
        return out + [copy(6 * nb + 3 * (k - nb) + j, ins[k], outs[k].at[my_chip], (cx, cy, c))
                      for k in range(nb, n) for j, (cx, cy) in enumerate(chips)]

    def landed(k, j, which):
        cx, cy = chips[j]
        return outs[k].at[2 * cx + cy, rows(k, which)]

    def forwards():
        return [copy(6 * k + 3 + j, landed(k, j, c), landed(k, j, c), sibling)
                for j in range(3) for k in range(nb)]

    def start():
        for cp in sends():
            cp.start()

    def forward():
        for j in range(3):
            for k in range(nb):
                copy(6 * k + j, landed(k, j, c), landed(k, j, c), (x, y, c)).wait_recv()
                copy(6 * k + 3 + j, landed(k, j, c), landed(k, j, c), sibling).start()

    def finish():
        for j, (cx, cy) in enumerate(chips):
            for k in range(nb):
                copy(6 * k + 3 + j, landed(k, j, 1 - c), landed(k, j, 1 - c), (x, y, c)).wait_recv()
            for k in range(nb, n):
                arrived = outs[k].at[2 * cx + cy]
                copy(6 * nb + 3 * (k - nb) + j, arrived, arrived, (x, y, c)).wait_recv()
        for cp in sends() + forwards():
            cp.wait_send()

    return start, forward, finish


def _pair_exchange(views, name):
    n = len(views)

    def body(*refs):
        ins, outs, send_sems, recv_sems = refs[:n], refs[n:2 * n], refs[2 * n], refs[2 * n + 1]
        x, y, c = _place()
        copies = [pltpu.make_async_remote_copy(
            src_ref=ins[k].at[:, pl.ds(1 - c, 1)], dst_ref=outs[k], send_sem=send_sems.at[k],
            recv_sem=recv_sems.at[k], device_id=(x, y, 1 - c), device_id_type=MESH) for k in range(n)]
        for cp in copies:
            cp.start()
        for cp in copies:
            cp.wait()

    return pl.pallas_call(
        body, name=name,
        out_shape=[jax.ShapeDtypeStruct((v.shape[0], 1) + v.shape[2:], v.dtype) for v in views],
        in_specs=_hbm_specs(n), out_specs=_hbm_specs(n),
        scratch_shapes=[pltpu.SemaphoreType.DMA((n,)), pltpu.SemaphoreType.DMA((n,))])(*views)


def _scatter_operands(parts):
    n = len(parts)
    return ([jax.ShapeDtypeStruct((3,) + p.shape[1:], p.dtype) for p in parts],
            [pltpu.SemaphoreType.DMA((3 * n,)), pltpu.SemaphoreType.DMA((3 * n,))])


def _scatter_phases(ins, outs, send_sems, recv_sems):
    x, y, c = _place()

    def copies():
        return [pltpu.make_async_remote_copy(
            src_ref=ins[k].at[pl.ds(2 * cx + cy, 1)], dst_ref=outs[k].at[pl.ds(r, 1)], send_sem=send_sems.at[3 * k + r],
            recv_sem=recv_sems.at[3 * k + r], device_id=(cx, cy, c), device_id_type=MESH)
            for k in range(len(ins)) for r, (cx, cy) in enumerate(_other_chips(x, y))]

    def start():
        for cp in copies():
            cp.start()

    def finish():
        for cp in copies():
            cp.wait()

    return start, finish


def _hosted_scatter(ins, outs, sems, step, total):
    start, finish = _scatter_phases(ins, outs, *sems)
    pl.when(step == 0)(start)
    return lambda: pl.when(step == total - 1)(finish)


def _join_halves(shards):
    n = len(shards)

    def body(*refs):
        ins, outs, send_sems, recv_sems = refs[:n], refs[n:2 * n], refs[2 * n], refs[2 * n + 1]
        x, y, c = _place()

        def rows(ref, which):
            h = ref.shape[0] // 2
            return ref.at[pl.ds(which * h, h)]

        sent = [pltpu.make_async_remote_copy(
            src_ref=rows(ins[k], c), dst_ref=rows(outs[k], c), send_sem=send_sems.at[k], recv_sem=recv_sems.at[k],
            device_id=(x, y, 1 - c), device_id_type=MESH) for k in range(n)]
        for cp in sent:
            cp.start()
        for k in range(n):
            pltpu.make_async_remote_copy(
                src_ref=rows(ins[k], 1 - c), dst_ref=rows(outs[k], 1 - c), send_sem=send_sems.at[k],
                recv_sem=recv_sems.at[k], device_id=(x, y, 1 - c), device_id_type=MESH).wait_recv()
        for cp in sent:
            cp.wait_send()

    return pl.pallas_call(
        body, name="half_exchange", out_shape=[jax.ShapeDtypeStruct(a.shape, a.dtype) for a in shards],
        in_specs=_hbm_specs(n), out_specs=_hbm_specs(n), input_output_aliases={k: k for k in range(n)},
        scratch_shapes=[pltpu.SemaphoreType.DMA((n,)), pltpu.SemaphoreType.DMA((n,))])(*shards)


def _all_reduce_small(packet):
    rows, width = packet.shape
    n_dev = 8

    def body(x_ref, out_ref, gath, send_sems, recv_sems):
        x, y, c = _place()
        me, sibling = (x, y, c), (x, y, 1 - c)
        chips = _other_chips(x, y)

        def slot(px, py, pc):
            return gath.at[pl.ds((4 * px + 2 * py + pc) * rows, rows), :]

        def copy(k, block, to, src=None):
            return pltpu.make_async_remote_copy(
                src_ref=slot(*block) if src is None else src, dst_ref=slot(*block), send_sem=send_sems.at[k],
                recv_sem=recv_sems.at[k], device_id=to, device_id_type=MESH)

        first = [copy(0, me, sibling, src=x_ref)]
        first += [copy(1 + j, me, (*chip, c), src=x_ref) for j, chip in enumerate(chips)]
        for cp in first:
            cp.start()
        gath[pl.ds((4 * x + 2 * y + c) * rows, rows), :] = x_ref[...]
        passed = [copy(4 + j, (*chip, c), sibling) for j, chip in enumerate(chips)]
        for j, chip in enumerate(chips):
            copy(1 + j, (*chip, c), me).wait_recv()
            passed[j].start()
        copy(0, sibling, me).wait_recv()
        for j, chip in enumerate(chips):
            copy(4 + j, (*chip, 1 - c), me).wait_recv()
        for cp in first + passed:
            cp.wait_send()
        acc = gath[0:rows, :]
        for d in range(1, n_dev):
            acc = acc + gath[d * rows:(d + 1) * rows, :]
        out_ref[...] = acc

    return pl.pallas_call(
        body, name="all_reduce_small", out_shape=jax.ShapeDtypeStruct((rows, width), F32),
        in_specs=[pl.BlockSpec(memory_space=pltpu.VMEM)], out_specs=pl.BlockSpec(memory_space=pltpu.VMEM),
        scratch_shapes=[pltpu.VMEM((n_dev * rows, width), F32), pltpu.SemaphoreType.DMA((7,)),
                        pltpu.SemaphoreType.DMA((7,))])(packet)


def _flat_rows(parts, width, row_multiple):
    flat = jnp.concatenate([p.astype(F32).reshape(-1) for p in parts])
    rows = -(-flat.shape[0] // width)
    rows = -(-rows // row_multiple) * row_multiple
    return jnp.pad(flat, (0, rows * width - flat.shape[0])).reshape(rows, width)


def _unflatten(flat2d, shapes):
    flat = flat2d.reshape(-1)
    out, off = [], 0
    for shp in shapes:
        n = 1
        for dim in shp:
            n *= dim
        out.append(flat[off:off + n].reshape(shp))
        off += n
    return out


def _core_and_chip():
    x, y, c = _place()
    return jnp.stack([c, 2 * x + y]).astype(jnp.int32)


def _pair_sums(chip_major, names, call_name):
    views = [g.reshape(N_CHIPS, 2, g.shape[1] // 2, g.shape[2]) for g in chip_major]
    recv = _pair_exchange(views, call_name)
    sel = _core_and_chip()
    return [_pair_sum(v, r, sel, "pair_sum_" + nm) for v, r, nm in zip(views, recv, names)]


def _finish_grads(pairs, got, names):
    sel = _core_and_chip()
    return _join_halves([_chip_sum(p, g, sel, "chip_sum_" + nm) for (p, _), g, nm in zip(pairs, got, names)])


def kernel(x, g_mix, w_in, b_f, w_conv, g_conv_out, g_attn_out, w_o, g_ffn, w_up, w_ffn_conv, w_down, g_final, loss_target, m_g_mix, m_w_in, m_b_f, m_w_conv, m_g_conv_out, m_g_attn_out, m_w_o, m_g_ffn, m_w_up, m_w_ffn_conv, m_w_down, m_g_final, v_g_mix, v_w_in, v_b_f, v_w_conv, v_g_conv_out, v_g_attn_out, v_w_o, v_g_ffn, v_w_up, v_w_ffn_conv, v_w_down, v_g_final):
    s = x.shape[1]
    x0 = x[0]
    target = loss_target[0]
    d = D_MODEL
    x_pos, y_pos, _ = _place()
    my_chip = 2 * x_pos + y_pos

    (c_in,) = _all_gather_weights([w_in[0].astype(BF16)], [])
    w_in_full = jnp.concatenate([c_in[j] for j in range(N_CHIPS)], axis=1)
    c3 = 3 * CONV_CH
    w_a, w_b = w_in_full[:, :c3], w_in_full[:, c3:c3 + 3 * ATTN_W]
    w_c = jnp.pad(w_in_full[:, c3 + 3 * ATTN_W:], ((0, 0), (0, LANES - N_HEADS)))
    w_q, w_k, w_v = (w_b[:, i * ATTN_W:(i + 1) * ATTN_W] for i in range(3))
    b_pad = jnp.pad(b_f, ((0, 0), (0, LANES - N_HEADS)))

    z_a, qkv, f_log, h1, (c_o, c_up, c_conv, c_ffn) = _in_proj(
        x0, g_mix, w_a, w_b, w_c, [w_o[0].astype(BF16), w_up[0].astype(BF16)], [w_conv[0], w_ffn_conv[0]])
    fb = _gate_fwd(f_log, b_pad)
    qx, kx, kxt, vx, vt, bounds, (c_down,) = _attn_prep(qkv, fb, [w_down[0].astype(BF16)])
    w_o_full = c_o.reshape(d, d)
    w_down_full = c_down.reshape(D_FF, d)
    w_conv_full = jnp.concatenate([c_conv[j] for j in range(N_CHIPS)], axis=1)
    w_ffn_full = jnp.concatenate([c_ffn[j] for j in range(N_CHIPS)], axis=1)
    n_up = c_up.shape[2]
    first_blk, last_blk = _key_block_ranges(bounds)
    o_attn, lse = _attn_fwd_t(qx, kx, vt, first_blk)
    mix = _mixer_fwd(z_a, o_attn, w_conv_full, g_conv_out, g_attn_out)
    x2, h2 = _mm("nn", [mix], [w_o_full], F32, 512, d, "out_proj", add=x0, norm_g=g_ffn)
    up = _mm("nn", [h2], [c_up], F32, TM_MM, n_up, "up_proj", b_chips=True)
    act = _ffn_act_fwd(up, w_ffn_full)
    x3 = _mm("nn", [act], [w_down_full], F32, TM_MM, 512, "down_proj", add=x2)
    dx3, dx3_b, loss_row, gg_final = _loss_head(x3, target, g_final.reshape(1, d))

    dact = _mm("nt", [dx3_b], [w_down_full], F32, TM_MM, 1408, "d_act")
    gw_down = _mm_tn(act, dx3_b, 1408, 1024, "gw_down")
    dup, gwf_lin, gwf_gate = _ffn_act_bwd(up, dact, w_ffn_full)
    dh2 = _mm("nt", [(dup, j // 2, j % 2, n_up) for j in range(N_CHIPS)], [(c_up, j) for j in range(N_CHIPS)],
              F32, TM_MM, 512, "d_h2")
    gw_up = _mm_tn(h2, dup, 1024, n_up, "gw_up", out_chips=True)
    dx2, dx2_b, gg_ffn = _rms_bwd(x2, dh2, g_ffn, dx3, "rms_ffn_bwd", True)
    dmix = _mm("nt", [dx2_b], [w_o_full], F32, TM_MM, 512, "d_mix")
    gw_o = _mm_tn(mix, dx2_b, 1024, 1024, "gw_o")
    early = _pair_sums([gw_o.reshape(N_CHIPS, d // N_CHIPS, d), gw_up, gw_down.reshape(N_CHIPS, D_FF // N_CHIPS, d)],
                       ["w_o", "w_up", "w_down"], "pair_exchange")
    dz_a, dox, gw_conv, gg_conv_out, gg_attn_out, (got_o, got_down) = _mixer_bwd(
        z_a, o_attn, dmix, w_conv_full, g_conv_out, g_attn_out, [early[0][1], early[2][1]])
    dk, dv, dfk, dqt, dfq = _attn_bwd_t(qx, dox, kx, kxt, vx, lse, last_blk)
    dq = _attn_dq_finish(dqt)
    d_f = (jnp.transpose(dfk[:, :, 0:2, :], (1, 3, 0, 2)).reshape(s, N_HEADS)
           + jnp.transpose(jnp.sum(dfq, axis=2), (1, 2, 0)).reshape(s, N_HEADS))
    df_b, gb_f = _gate_bwd(f_log, b_pad, jnp.pad(d_f, ((0, 0), (0, LANES - N_HEADS))))
    gw_a = _mm_tn(h1, dz_a, 1024, c3, "gw_in_conv")
    gw_q = _mm_tn(h1, dq, 1024, ATTN_W, "gw_in_q")
    gw_k = _mm_tn(h1, dk, 1024, ATTN_W, "gw_in_k")
    gw_v = _mm_tn(h1, dv, 1024, ATTN_W, "gw_in_v")
    gw_c = _mm_tn(h1, df_b, 1024, LANES, "gw_in_gate")
    gw_in = jnp.concatenate([gw_a, gw_q, gw_k, gw_v, gw_c[:, :N_HEADS]], axis=1)
    n_in = IN_COLS // N_CHIPS
    gw_in = jnp.stack([gw_in[:, j * n_in:(j + 1) * n_in] for j in range(N_CHIPS)])
    late = _pair_sums([gw_in], ["w_in"], "pair_exchange_w_in")
    dh1, (got_up, got_in) = _mm("nt", [dz_a, dq, dk, dv, df_b], [w_a, w_q, w_k, w_v, w_c], F32, TM_MM, 512, "d_h1",
                                scatter=[early[1][1], late[0][1]])
    grad_x, gg_mix = _rms_bwd(x0, dh1, g_mix, dx2, "rms_mix_bwd", False)
    g_w_in, g_w_o, g_w_up, g_w_down = _finish_grads(late + early, [got_in, got_o, got_up, got_down],
                                                    ["w_in", "w_o", "w_up", "w_down"])

    gw_ffn = jnp.concatenate([gwf_lin, gwf_gate], axis=1)
    small_parts = [gg_mix, gg_conv_out, gg_attn_out, gg_ffn, gg_final, gb_f[:, :N_HEADS], loss_row[:, 0:1], gw_conv,
                   gw_ffn]
    small_shapes = [a.shape for a in small_parts]
    tot = _unflatten(_all_reduce_small(_flat_rows(small_parts, d, 8)), small_shapes)
    g_g_mix, g_g_conv_out, g_g_attn_out, g_g_ffn, g_g_final, g_b_f, loss_sum, g_conv_full, g_ffn_full = tot
    loss = loss_sum[0, 0]
    g_g_final = g_g_final[0]
    g_w_conv = lax.dynamic_slice_in_dim(g_conv_full, my_chip * (CONV_CH // N_CHIPS), CONV_CH // N_CHIPS, axis=1)
    g_w_ffn = lax.dynamic_slice_in_dim(g_ffn_full, my_chip * n_up, n_up, axis=1)

    def adam_big(w, g, m, v, name):
        dl, nm, nv = _adamw(w[0], g, m[0], v[0], name)
        return dl[None], nm[None], nv[None]

    u_w_in = adam_big(w_in, g_w_in, m_w_in, v_w_in, "adam_w_in")
    u_w_o = adam_big(w_o, g_w_o, m_w_o, v_w_o, "adam_w_o")
    u_w_up = adam_big(w_up, g_w_up, m_w_up, v_w_up, "adam_w_up")
    u_w_down = adam_big(w_down, g_w_down, m_w_down, v_w_down, "adam_w_down")

    small_w = [g_mix, b_f, g_conv_out, g_attn_out, g_ffn, g_final, w_conv, w_ffn_conv]
    small_g = [g_g_mix, g_b_f, g_g_conv_out, g_g_attn_out, g_g_ffn, g_g_final, g_w_conv, g_w_ffn]
    small_m = [m_g_mix, m_b_f, m_g_conv_out, m_g_attn_out, m_g_ffn, m_g_final, m_w_conv, m_w_ffn_conv]
    small_v = [v_g_mix, v_b_f, v_g_conv_out, v_g_attn_out, v_g_ffn, v_g_final, v_w_conv, v_w_ffn_conv]
    shapes = [a.shape for a in small_w]
    pack = lambda arrs: _flat_rows(arrs, LANES, 8)
    sd, sm, sv = _adamw(pack(small_w), pack(small_g), pack(small_m), pack(small_v), "adam_small")
    sd, sm, sv = _unflatten(sd, shapes), _unflatten(sm, shapes), _unflatten(sv, shapes)
    (d_g_mix, d_b_f, d_g_conv_out, d_g_attn_out, d_g_ffn, d_g_final, d_w_conv, d_w_ffn) = sd
    (nm_g_mix, nm_b_f, nm_g_conv_out, nm_g_attn_out, nm_g_ffn, nm_g_final, nm_w_conv, nm_w_ffn) = sm
    (nv_g_mix, nv_b_f, nv_g_conv_out, nv_g_attn_out, nv_g_ffn, nv_g_final, nv_w_conv, nv_w_ffn) = sv

    grads = (g_g_mix, g_w_in[None], g_b_f, g_w_conv[None], g_g_conv_out, g_g_attn_out, g_w_o[None], g_g_ffn,
             g_w_up[None], g_w_ffn[None], g_w_down[None], g_g_final)
    deltas = (d_g_mix, u_w_in[0], d_b_f, d_w_conv, d_g_conv_out, d_g_attn_out, u_w_o[0], d_g_ffn, u_w_up[0],
              d_w_ffn, u_w_down[0], d_g_final)
    new_m = (nm_g_mix, u_w_in[1], nm_b_f, nm_w_conv, nm_g_conv_out, nm_g_attn_out, u_w_o[1], nm_g_ffn, u_w_up[1],
             nm_w_ffn, u_w_down[1], nm_g_final)
    new_v = (nv_g_mix, u_w_in[2], nv_b_f, nv_w_conv, nv_g_conv_out, nv_g_attn_out, u_w_o[2], nv_g_ffn, u_w_up[2],
             nv_w_ffn, u_w_down[2], nv_g_final)
    return (loss, grad_x[None], *grads, *deltas, *new_m, *new_v)
```

```python
import jax
import jax.numpy as jnp
from jax import lax
from jax.experimental import pallas as pl
from jax.experimental.pallas import tpu as pltpu

F32, BF16 = jnp.float32, jnp.bfloat16
MESH = pl.DeviceIdType.MESH

D_MODEL = 1024
CONV_CH = 512
ATTN_W = 512
N_HEADS = 8
HEAD_DIM = 64
N_PAIRS = N_HEADS // 2
D_FF = 2816
IN_COLS = 3 * CONV_CH + 3 * ATTN_W + N_HEADS
EPS = 1e-6
Q_SCALE = 0.125
EXP_ZERO = 88.0
N_CHIPS = 4
LANES = 128
HALO = 8

ADAM_LR, ADAM_B1, ADAM_B2, ADAM_EPS, ADAM_WD, ADAM_STEP = 0.001, 0.9, 0.999, 1e-08, 0.01, 10

TM_ROWS = 512
TM_MM = 1024
TK_TN = 1024
TQ = 512
TM_FFN = 1024
TN_FFN = 256
VMEM_LIMIT = 52 * 2**20


def _cp(sem, vmem=VMEM_LIMIT):
    return pltpu.CompilerParams(dimension_semantics=sem, vmem_limit_bytes=vmem)


def _bf(a):
    return a if a.dtype == BF16 else a.astype(BF16)


def _mm(mode, a_list, b_list, out_dtype, tm, tn, name, add=None, b_chips=False, scatter=(), norm_g=None):
    n_p = len(a_list)
    a0 = a_list[0]
    m_dim = a0[0].shape[1] if isinstance(a0, tuple) else a0.shape[0]
    b0 = b_list[0]
    if b_chips:
        n_dim = b0.shape[0] * b0.shape[2]
        assert tn == b0.shape[2] and mode == "nn"
    else:
        b0 = b0[0][b0[1]] if isinstance(b0, tuple) else b0
        n_dim = b0.shape[1 if mode == "nn" else 0]
    tm, tn = min(tm, m_dim), min(tn, n_dim)
    assert m_dim % tm == 0 and n_dim % tn == 0
    dims = (((1,), (0,)), ((), ())) if mode == "nn" else (((1,), (1,)), ((), ()))
    in_specs, args = [], []
    for a in a_list:
        if isinstance(a, tuple):
            arr, lead, col, width = a
            in_specs.append(pl.BlockSpec((None, tm, width), lambda m, n, lead=lead, col=col: (lead, m, col)))
        else:
            arr = a
            in_specs.append(pl.BlockSpec((tm, a.shape[1]), lambda m, n: (m, 0)))
        args.append(arr)
    for b in b_list:
        if b_chips:
            arr = b
            in_specs.append(pl.BlockSpec((None, b.shape[1], tn), lambda m, n: (n, 0, 0)))
        elif isinstance(b, tuple):
            arr, lead = b
            if mode == "nn":
                in_specs.append(pl.BlockSpec((None, arr.shape[1], tn), lambda m, n, lead=lead: (lead, 0, n)))
            else:
                in_specs.append(pl.BlockSpec((None, tn, arr.shape[2]), lambda m, n, lead=lead: (lead, n, 0)))
        elif mode == "nn":
            arr = b
            in_specs.append(pl.BlockSpec((b.shape[0], tn), lambda m, n: (0, n)))
        else:
            arr = b
            in_specs.append(pl.BlockSpec((tn, b.shape[1]), lambda m, n: (n, 0)))
        args.append(arr)
    if add is not None:
        in_specs.append(pl.BlockSpec((tm, tn), lambda m, n: (m, n)))
        args.append(add)
    if norm_g is not None:
        assert tn == n_dim and not scatter
        in_specs.append(pl.BlockSpec((1, tn), lambda m, n: (0, 0)))
        args.append(norm_g)

    n_in = len(args)
    n_sc = len(scatter)
    grid = (m_dim // tm, n_dim // tn)

    def body(*refs):
        o_ref = refs[n_in + n_sc]
        if n_sc:
            finish = _hosted_scatter(refs[n_in:n_in + n_sc], refs[n_in + n_sc + 1:n_in + 2 * n_sc + 1],
                                     refs[n_in + 2 * n_sc + 1:], pl.program_id(0) * grid[1] + pl.program_id(1),
                                     grid[0] * grid[1])
        acc = None
        for i in range(n_p):
            d = lax.dot_general(_bf(refs[i][...]), _bf(refs[n_p + i][...]), dims,
                                preferred_element_type=F32)
            acc = d if acc is None else acc + d
        if add is not None:
            acc = refs[2 * n_p][...] + acc
        o_ref[...] = acc.astype(out_dtype)
        if norm_g is not None:
            refs[n_in + 1][...] = (acc * _rstd(acc) * refs[n_in - 1][...]).astype(BF16)
        if n_sc:
            finish()

    main_spec = pl.BlockSpec((tm, tn), lambda m, n: (m, n))
    main_shape = jax.ShapeDtypeStruct((m_dim, n_dim), out_dtype)
    if norm_g is not None:
        return pl.pallas_call(body, name=name, grid=grid, in_specs=in_specs, out_specs=[main_spec, main_spec],
                              out_shape=[main_shape, jax.ShapeDtypeStruct((m_dim, n_dim), BF16)],
                              compiler_params=_cp(("parallel", "parallel")))(*args)
    if not n_sc:
        return pl.pallas_call(body, name=name, grid=grid, in_specs=in_specs, out_specs=main_spec,
                              out_shape=main_shape, compiler_params=_cp(("parallel", "parallel")))(*args)
    got_shapes, sems = _scatter_operands(scatter)
    outs = pl.pallas_call(
        body, name=name, grid=grid, in_specs=in_specs + _hbm_specs(n_sc), out_specs=[main_spec] + _hbm_specs(n_sc),
        out_shape=[main_shape] + got_shapes, scratch_shapes=sems,
        compiler_params=_cp(("arbitrary", "arbitrary")))(*args, *scatter)
    return outs[0], outs[1:]


def _mm_tn(a, b, tm, tn, name, out_chips=False):
    k_dim, m_dim = a.shape
    n_dim = b.shape[-1] * (b.shape[0] if b.ndim == 3 else 1)
    tm, tn, tk = min(tm, m_dim), min(tn, b.shape[-1]), min(TK_TN, k_dim)
    assert m_dim % tm == 0 and b.shape[-1] % tn == 0 and k_dim % tk == 0
    per = b.shape[-1] // tn
    if b.ndim == 3:
        b_spec = pl.BlockSpec((None, tk, tn), lambda m, n, k: (n // per, k, n % per))
    else:
        b_spec = pl.BlockSpec((tk, tn), lambda m, n, k: (k, n))

    def body(a_ref, b_ref, o_ref):
        @pl.when(pl.program_id(2) == 0)
        def _():
            o_ref[...] = jnp.zeros_like(o_ref)
        o_ref[...] += lax.dot_general(_bf(a_ref[...]), _bf(b_ref[...]), (((0,), (0,)), ((), ())),
                                      preferred_element_type=F32)

    return pl.pallas_call(
        body, name=name, grid=(m_dim // tm, n_dim // tn, k_dim // tk),
        in_specs=[pl.BlockSpec((tk, tm), lambda m, n, k: (k, m)), b_spec],
        out_specs=(pl.BlockSpec((None, tm, tn), lambda m, n, k: (n, m, 0)) if out_chips
                   else pl.BlockSpec((tm, tn), lambda m, n, k: (m, n))),
        out_shape=jax.ShapeDtypeStruct((n_dim // tn, m_dim, tn) if out_chips else (m_dim, n_dim), F32),
        compiler_params=_cp(("parallel", "parallel", "arbitrary")))(a, b)


def _rstd(x):
    return lax.rsqrt(jnp.mean(x * x, axis=-1, keepdims=True) + EPS)


def _rms_bwd(x, dh, g, dres, name, with_bf16, scatter=()):
    s, d = x.shape
    tm = min(TM_ROWS, s)
    n_sc = len(scatter)
    n_out = 3 if with_bf16 else 2
    got_shapes, sems = _scatter_operands(scatter) if n_sc else ([], [])

    def body(x_ref, dh_ref, g_ref, dres_ref, *rest):
        dx_ref, gg_ref = rest[n_sc], rest[n_sc + n_out - 1]
        i = pl.program_id(0)
        if n_sc:
            finish = _hosted_scatter(rest[:n_sc], rest[n_sc + n_out:2 * n_sc + n_out], rest[2 * n_sc + n_out:], i,
                                     s // tm)

        @pl.when(i == 0)
        def _():
            gg_ref[...] = jnp.zeros_like(gg_ref)

        xv = x_ref[...]
        xn = xv * _rstd(xv)
        dhv = dh_ref[...]
        gg_ref[...] += jnp.sum(dhv * xn, axis=0, keepdims=True)
        t = dhv * g_ref[...]
        dx = dres_ref[...] + _rstd(xv) * (t - xn * jnp.mean(t * xn, axis=-1, keepdims=True))
        dx_ref[...] = dx
        if with_bf16:
            rest[n_sc + 1][...] = dx.astype(BF16)
        if n_sc:
            finish()

    row = pl.BlockSpec((tm, d), lambda i: (i, 0))
    vec = pl.BlockSpec((1, d), lambda i: (0, 0))
    out_specs = [row] + ([row] if with_bf16 else []) + [vec] + _hbm_specs(n_sc)
    out_shape = ([jax.ShapeDtypeStruct((s, d), F32)] + ([jax.ShapeDtypeStruct((s, d), BF16)] if with_bf16 else [])
                 + [jax.ShapeDtypeStruct((1, d), F32)] + got_shapes)
    outs = pl.pallas_call(
        body, name=name, grid=(s // tm,), in_specs=[row, row, vec, row] + _hbm_specs(n_sc), out_specs=out_specs,
        out_shape=out_shape, scratch_shapes=sems, compiler_params=_cp(("arbitrary",)))(x, dh, g, dres, *scatter)
    return tuple(outs[:n_out]) + ((outs[n_out:],) if n_sc else ())


def _down_proj_loss(act, w_down, x2, target, g):
    s, d = x2.shape
    tm = min(TM_ROWS, s)
    k_dim = act.shape[1]

    def body(a_ref, w_ref, x_ref, t_ref, g_ref, dx_ref, dxb_ref, loss_ref, gg_ref):
        @pl.when(pl.program_id(0) == 0)
        def _():
            gg_ref[...] = jnp.zeros_like(gg_ref)
            loss_ref[...] = jnp.zeros_like(loss_ref)

        xv = x_ref[...] + lax.dot_general(a_ref[...], w_ref[...], (((1,), (0,)), ((), ())),
                                          preferred_element_type=F32)
        r = _rstd(xv)
        xn = xv * r
        gv = g_ref[...]
        err = xn * gv - t_ref[...]
        loss_ref[...] += 0.5 * jnp.sum(jnp.mean(err * err, axis=-1, keepdims=True), axis=0, keepdims=True)
        dy = err * (1.0 / d)
        gg_ref[...] += jnp.sum(dy * xn, axis=0, keepdims=True)
        t = dy * gv
        dx = r * (t - xn * jnp.mean(t * xn, axis=-1, keepdims=True))
        dx_ref[...] = dx
        dxb_ref[...] = dx.astype(BF16)

    row = pl.BlockSpec((tm, d), lambda i: (i, 0))
    vec = pl.BlockSpec((1, d), lambda i: (0, 0))
    return pl.pallas_call(
        body, name="down_proj_loss", grid=(s // tm,),
        in_specs=[pl.BlockSpec((tm, k_dim), lambda i: (i, 0)), pl.BlockSpec((k_dim, d), lambda i: (0, 0)), row, row,
                  vec],
        out_specs=[row, row, pl.BlockSpec((1, LANES), lambda i: (0, 0)), vec],
        out_shape=[jax.ShapeDtypeStruct((s, d), F32), jax.ShapeDtypeStruct((s, d), BF16),
                   jax.ShapeDtypeStruct((1, LANES), F32), jax.ShapeDtypeStruct((1, d), F32)],
        compiler_params=_cp(("arbitrary",)))(act, w_down, x2, target, g)


def _prev_halo_spec(tm, width, col):
    return pl.BlockSpec((HALO, width), lambda i, *_: (jnp.maximum(i * (tm // HALO) - 1, 0), col))


def _next_halo_spec(tm, width, col, s):
    return pl.BlockSpec((HALO, width), lambda i, *_: (jnp.minimum((i + 1) * (tm // HALO), s // HALO - 1), col))


def _shift_down(x, k):
    return pltpu.roll(x, k, 0)


def _shift_up(x, k):
    return pltpu.roll(x, x.shape[0] - k, 0)


def _conv_taps(x_ext, w):
    return w[0:1, :] * _shift_down(x_ext, 2) + w[1:2, :] * _shift_down(x_ext, 1) + w[2:3, :] * x_ext


def _conv_taps_t(d_ext, w):
    return w[2:3, :] * d_ext + w[1:2, :] * _shift_up(d_ext, 1) + w[0:1, :] * _shift_up(d_ext, 2)


def _mixer_fwd(z_a, o_attn, w_conv, g_conv_out, g_attn_out):
    s = z_a.shape[0]
    c = CONV_CH
    tm = min(TM_ROWS, s)

    def body(gb_ref, gc_ref, xc_ref, gcp_ref, xcp_ref, o_ref, w_ref, gco_ref, gao_ref, mix_ref):
        i = pl.program_id(0)
        cx = gc_ref[...] * xc_ref[...]
        cx_prev = jnp.where(i > 0, gcp_ref[...] * xcp_ref[...], 0.0)
        conv = _conv_taps(jnp.concatenate([cx_prev, cx], axis=0), w_ref[...])[HALO:]
        y = gb_ref[...] * conv
        mix_ref[:, 0:c] = (y * _rstd(y) * gco_ref[...]).astype(BF16)
        o = o_ref[...]
        mix_ref[:, c:2 * c] = (o * _rstd(o) * gao_ref[...]).astype(BF16)

    col = lambda j: pl.BlockSpec((tm, c), lambda i: (i, j))
    vec = pl.BlockSpec((1, c), lambda i: (0, 0))
    return pl.pallas_call(
        body, name="mixer_fwd", grid=(s // tm,),
        in_specs=[col(0), col(1), col(2), _prev_halo_spec(tm, c, 1), _prev_halo_spec(tm, c, 2), col(0),
                  pl.BlockSpec((3, c), lambda i: (0, 0)), vec, vec],
        out_specs=pl.BlockSpec((tm, 2 * c), lambda i: (i, 0)),
        out_shape=jax.ShapeDtypeStruct((s, 2 * c), BF16),
        compiler_params=_cp(("parallel",)))(z_a, z_a, z_a, z_a, z_a, o_attn, w_conv, g_conv_out, g_attn_out)


def _mixer_bwd(z_a, o_attn, dmix, w_conv, g_conv_out, g_attn_out, scatter):
    s = z_a.shape[0]
    c = CONV_CH
    tm = min(TM_ROWS, s)
    n_blk = s // tm
    n_sc = len(scatter)
    got_shapes, sems = _scatter_operands(scatter)

    def body(gb_ref, gc_ref, xc_ref, gcp_ref, xcp_ref, gbn_ref, gcn_ref, xcn_ref, o_ref, dnc_ref, dncn_ref, dna_ref,
             w_ref, gco_ref, gao_ref, *rest):
        dz_ref, dox_ref, gw_ref, ggco_ref, ggao_ref = rest[n_sc:n_sc + 5]
        i = pl.program_id(0)
        finish = _hosted_scatter(rest[:n_sc], rest[n_sc + 5:2 * n_sc + 5], rest[2 * n_sc + 5:], i, n_blk)

        @pl.when(i == 0)
        def _():
            gw_ref[...] = jnp.zeros_like(gw_ref)
            ggco_ref[...] = jnp.zeros_like(ggco_ref)
            ggao_ref[...] = jnp.zeros_like(ggao_ref)

        w = w_ref[...]
        zeros = jnp.zeros((HALO, c), F32)
        gb_e = jnp.concatenate([zeros, gb_ref[...], gbn_ref[...]], axis=0)
        cx_prev = jnp.where(i > 0, gcp_ref[...] * xcp_ref[...], 0.0)
        gc_e = jnp.concatenate([zeros, gc_ref[...], gcn_ref[...]], axis=0)
        xc_e = jnp.concatenate([zeros, xc_ref[...], xcn_ref[...]], axis=0)
        cx_e = jnp.concatenate([cx_prev, gc_ref[...] * xc_ref[...], gcn_ref[...] * xcn_ref[...]], axis=0)
        dn_next = jnp.where(i < n_blk - 1, dncn_ref[...], 0.0)
        dn_e = jnp.concatenate([zeros, dnc_ref[...], dn_next], axis=0)

        conv_e = _conv_taps(cx_e, w)
        y_e = gb_e * conv_e
        r_e = _rstd(y_e)
        yn_e = y_e * r_e
        t_e = dn_e * gco_ref[...]
        dy_e = r_e * (t_e - yn_e * jnp.mean(t_e * yn_e, axis=-1, keepdims=True))
        dconv_e = dy_e * gb_e
        dcx_e = _conv_taps_t(dconv_e, w)
        blk = slice(HALO, HALO + tm)
        dz_ref[:, 0:c] = (dy_e * conv_e)[blk].astype(BF16)
        dz_ref[:, c:2 * c] = (dcx_e * xc_e)[blk].astype(BF16)
        dz_ref[:, 2 * c:3 * c] = (dcx_e * gc_e)[blk].astype(BF16)
        ggco_ref[...] += jnp.sum((dn_e * yn_e)[blk], axis=0, keepdims=True)
        dconv = dconv_e[blk]
        gw_ref[0:1, :] += jnp.sum(dconv * _shift_down(cx_e, 2)[blk], axis=0, keepdims=True)
        gw_ref[1:2, :] += jnp.sum(dconv * _shift_down(cx_e, 1)[blk], axis=0, keepdims=True)
        gw_ref[2:3, :] += jnp.sum(dconv * cx_e[blk], axis=0, keepdims=True)

        o = o_ref[...]
        ra = _rstd(o)
        on = o * ra
        dna = dna_ref[...]
        ggao_ref[...] += jnp.sum(dna * on, axis=0, keepdims=True)
        ta = dna * gao_ref[...]
        do = ra * (ta - on * jnp.mean(ta * on, axis=-1, keepdims=True))
        prod = do * o
        lane = lax.broadcasted_iota(jnp.int32, (tm, LANES), 1)
        head_a = lane < HEAD_DIM
        for p in range(N_PAIRS):
            cols = slice(p * LANES, (p + 1) * LANES)
            pb, dob = prod[:, cols], do[:, cols]
            for hh in range(2):
                sel = head_a if hh == 0 else jnp.logical_not(head_a)
                delta = jnp.sum(jnp.where(sel, pb, 0.0), axis=-1, keepdims=True)
                neg3 = _split3(-delta)
                do_h = pltpu.roll(dob, HEAD_DIM, 1) if hh else dob
                dox_ref[2 * p + hh] = _aug(do_h, lane, neg3).astype(BF16)
        finish()

    col = lambda j: pl.BlockSpec((tm, c), lambda i: (i, j))
    vec = pl.BlockSpec((1, c), lambda i: (0, 0))
    w3 = pl.BlockSpec((3, c), lambda i: (0, 0))
    outs = pl.pallas_call(
        body, name="mixer_bwd", grid=(n_blk,),
        in_specs=[col(0), col(1), col(2), _prev_halo_spec(tm, c, 1), _prev_halo_spec(tm, c, 2),
                  _next_halo_spec(tm, c, 0, s), _next_halo_spec(tm, c, 1, s), _next_halo_spec(tm, c, 2, s),
                  col(0), col(0), _next_halo_spec(tm, c, 0, s), col(1), w3, vec, vec] + _hbm_specs(n_sc),
        out_specs=[pl.BlockSpec((tm, 3 * c), lambda i: (i, 0)),
                   pl.BlockSpec((N_HEADS, tm, LANES), lambda i: (0, i, 0)), w3, vec, vec] + _hbm_specs(n_sc),
        out_shape=[jax.ShapeDtypeStruct((s, 3 * c), BF16), jax.ShapeDtypeStruct((N_HEADS, s, LANES), BF16),
                   jax.ShapeDtypeStruct((3, c), F32), jax.ShapeDtypeStruct((1, c), F32),
                   jax.ShapeDtypeStruct((1, c), F32)] + got_shapes,
        scratch_shapes=sems, compiler_params=_cp(("arbitrary",)))(
            z_a, z_a, z_a, z_a, z_a, z_a, z_a, z_a, o_attn, dmix, dmix, dmix, w_conv, g_conv_out, g_attn_out,
            *scatter)
    return tuple(outs[:5]) + (outs[5:],)


def _gate_fwd(f, b_pad):
    s = f.shape[0]
    tm = min(TQ, s)

    def body(f_ref, b_ref, fb_ref, carry):
        @pl.when(pl.program_id(0) == 0)
        def _():
            carry[...] = jnp.zeros_like(carry)

        z = f_ref[...] + b_ref[...]
        x = jnp.minimum(z, 0.0) - jnp.log1p(jnp.exp(-jnp.abs(z)))
        row = lax.broadcasted_iota(jnp.int32, (tm, LANES), 0)
        sh = 1
        while sh < tm:
            x = x + jnp.where(row >= sh, _shift_down(x, sh), 0.0)
            sh *= 2
        x = x + carry[0:1, :]
        carry[...] = jnp.broadcast_to(x[tm - 1:tm, :], carry.shape)
        head_a = lax.broadcasted_iota(jnp.int32, (tm, LANES), 1) < HEAD_DIM
        for p in range(N_PAIRS):
            fa = jnp.broadcast_to(x[:, 2 * p:2 * p + 1], (tm, LANES))
            fbv = jnp.broadcast_to(x[:, 2 * p + 1:2 * p + 2], (tm, LANES))
            fb_ref[:, p * LANES:(p + 1) * LANES] = jnp.where(head_a, fa, fbv)

    return pl.pallas_call(
        body, name="gate_fwd", grid=(s // tm,),
        in_specs=[pl.BlockSpec((tm, LANES), lambda i: (i, 0)), pl.BlockSpec((1, LANES), lambda i: (0, 0))],
        out_specs=pl.BlockSpec((tm, N_PAIRS * LANES), lambda i: (i, 0)),
        out_shape=jax.ShapeDtypeStruct((s, N_PAIRS * LANES), F32),
        scratch_shapes=[pltpu.VMEM((HALO, LANES), F32)],
        compiler_params=_cp(("arbitrary",)))(f, b_pad)


def _gate_bwd(f, b_pad, d_f):
    s = f.shape[0]
    tm = min(TQ, s)
    n_blk = s // tm

    def body(f_ref, b_ref, d_ref, df_ref, gb_ref, carry):
        @pl.when(pl.program_id(0) == 0)
        def _():
            carry[...] = jnp.zeros_like(carry)
            gb_ref[...] = jnp.zeros_like(gb_ref)

        x = d_ref[...]
        row = lax.broadcasted_iota(jnp.int32, (tm, LANES), 0)
        sh = 1
        while sh < tm:
            x = x + jnp.where(row < tm - sh, _shift_up(x, sh), 0.0)
            sh *= 2
        x = x + carry[0:1, :]
        carry[...] = jnp.broadcast_to(x[0:1, :], carry.shape)
        z = f_ref[...] + b_ref[...]
        d = x * (1.0 / (1.0 + jnp.exp(z)))
        df_ref[...] = d.astype(BF16)
        gb_ref[...] += jnp.sum(d, axis=0, keepdims=True)

    rev = pl.BlockSpec((tm, LANES), lambda i: (n_blk - 1 - i, 0))
    vec = pl.BlockSpec((1, LANES), lambda i: (0, 0))
    return pl.pallas_call(
        body, name="gate_bwd", grid=(n_blk,), in_specs=[rev, vec, rev], out_specs=[rev, vec],
        out_shape=[jax.ShapeDtypeStruct((s, LANES), BF16), jax.ShapeDtypeStruct((1, LANES), F32)],
        scratch_shapes=[pltpu.VMEM((HALO, LANES), F32)],
        compiler_params=_cp(("arbitrary",)))(f, b_pad, d_f)


_NT = (((1,), (1,)), ((), ()))
_NN = (((1,), (0,)), ((), ()))


AUG = HEAD_DIM
NORM_MARGIN = 1.01


def _split3(x):
    hi = x.astype(BF16).astype(F32)
    r = x - hi
    mid = r.astype(BF16).astype(F32)
    lo = (r - mid).astype(BF16).astype(F32)
    return hi, mid, lo


def _aug(base, lane, vals):
    out = jnp.where(lane < AUG, base, 0.0)
    for k, v in enumerate(vals):
        out = jnp.where(lane == AUG + k, v, out)
    return out


def _attn_prep(qkv, fb, bigs):
    s = qkv.shape[0]
    tq = min(TQ, s)
    n_q = s // tq

    n = len(bigs)
    arrays, landing, sems = _gather_operands(bigs, [])

    def body(q_ref, k_ref, v_ref, fb_ref, *rest):
        qx_ref, kx_ref, kxt_ref, vx_ref, vt_ref, b_ref = rest[2 * n:2 * n + 6]
        finish = _hosted_gather((rest[:n], rest[2 * n + 6:3 * n + 6]) + tuple(rest[3 * n + 6:]), n, n,
                                pl.program_id(0), n_q)
        lane = lax.broadcasted_iota(jnp.int32, (tq, LANES), 1)
        lane8 = lax.broadcasted_iota(jnp.int32, (HALO, LANES), 1)
        head_lanes = lane < AUG
        is_lane = [lane == AUG + k for k in range(6)]
        first3 = (lane >= AUG) & (lane < AUG + 3)
        next3 = (lane >= AUG + 3) & (lane < AUG + 6)
        q_const = jnp.where(first3, -1.0, 0.0)
        k_const = jnp.where(next3, 1.0, 0.0)
        v_const = jnp.where(first3, 1.0, 0.0)
        ones_head = (lax.broadcasted_iota(jnp.int32, (LANES, LANES), 0) < HEAD_DIM).astype(BF16)
        acc = jnp.zeros((HALO, LANES), F32)
        for p in range(N_PAIRS):
            cols = slice(p * LANES, (p + 1) * LANES)
            q2, k2, v2 = (ref[:, cols].astype(F32) for ref in (q_ref, k_ref, v_ref))
            q2 = q2 * Q_SCALE
            f2 = fb_ref[:, cols]
            for hh in range(2):
                h = 2 * p + hh
                q, k, v = ((pltpu.roll(x, HEAD_DIM, 1) if hh else x) for x in (q2, k2, v2))
                f = f2 if hh else pltpu.roll(f2, HEAD_DIM, 1)
                hi, mid, lo = _split3(f)
                q_aug = jnp.where(is_lane[3], hi, jnp.where(is_lane[4], mid, jnp.where(is_lane[5], lo, q_const)))
                k_aug = jnp.where(is_lane[0], hi, jnp.where(is_lane[1], mid, jnp.where(is_lane[2], lo, k_const)))
                kx = jnp.where(head_lanes, k, k_aug)
                vx = jnp.where(head_lanes, v, v_const)
                qx_ref[h] = jnp.where(head_lanes, q, q_aug).astype(BF16)
                kx_ref[h] = kx.astype(BF16)
                vx_ref[h] = vx.astype(BF16)
                kxt_ref[h, 0] = kx.T.astype(BF16)
                vt_ref[h, 0] = vx.T.astype(BF16)
                q_sq = lax.dot_general((q * q).astype(BF16), ones_head, _NN, preferred_element_type=F32)
                k_sq = lax.dot_general((k * k).astype(BF16), ones_head, _NN, preferred_element_type=F32)
                diag = lax.dot_general((q * k).astype(BF16), ones_head, _NN, preferred_element_type=F32)
                diag = diag - jnp.sqrt(q_sq * k_sq) * (NORM_MARGIN - 1.0)
                vals = (jnp.sqrt(jnp.max(q_sq, axis=0, keepdims=True)), jnp.sqrt(jnp.max(k_sq, axis=0, keepdims=True)),
                        jnp.max(f - diag, axis=0, keepdims=True), f[tq - 1:tq, :])
                for slot, val in enumerate(vals):
                    acc = jnp.where(lane8 == slot * N_HEADS + h, val[:, AUG:AUG + 1], acc)
        b_ref[0] = acc
        finish()

    blk = lambda j: pl.BlockSpec((tq, ATTN_W), lambda i: (i, j))
    rows = pl.BlockSpec((N_HEADS, tq, LANES), lambda i: (0, i, 0))
    cols_t = pl.BlockSpec((N_HEADS, 1, LANES, tq), lambda i: (0, i, 0, 0))
    shp = jax.ShapeDtypeStruct((N_HEADS, s, LANES), BF16)
    shp_t = jax.ShapeDtypeStruct((N_HEADS, n_q, LANES, tq), BF16)
    outs = pl.pallas_call(
        body, name="attn_prep", grid=(n_q,), in_specs=[blk(0), blk(1), blk(2), blk(0)] + _hbm_specs(2 * n),
        out_specs=[rows, rows, cols_t, rows, cols_t,
                   pl.BlockSpec((1, HALO, LANES), lambda i: (i, 0, 0))] + _hbm_specs(n),
        out_shape=[shp, shp, shp_t, shp, shp_t, jax.ShapeDtypeStruct((n_q, HALO, LANES), F32)]
        + [jax.ShapeDtypeStruct(b.shape, b.dtype) for b in landing],
        input_output_aliases={4 + n + k: 6 + k for k in range(n)}, scratch_shapes=sems,
        compiler_params=_cp(("arbitrary",)))(qkv, qkv, qkv, fb, *arrays, *landing)
    return tuple(outs[:6]) + (outs[6:],)


def _key_block_ranges(bounds):
    t = bounds[:, 0, :]
    nh = N_HEADS
    a, b, c, e = t[:, 0:nh], t[:, nh:2 * nh], t[:, 2 * nh:3 * nh], t[:, 3 * nh:4 * nh]
    bound = a[:, None, :] * b[None, :, :] * NORM_MARGIN + c[:, None, :] - e[None, :, :]
    n_q = t.shape[0]
    idx = jnp.arange(n_q)
    need = jnp.logical_not(bound < -(EXP_ZERO + 2.0)) | (idx[None, :, None] >= idx[:, None, None])
    first = jnp.argmax(need, axis=1).astype(jnp.int32)
    first = jnp.min(first.reshape(n_q, N_PAIRS, 2), axis=-1)
    visits = (first[:, None, :] <= idx[None, :, None]) & (idx[None, :, None] <= idx[:, None, None])
    last = jnp.max(jnp.where(visits, idx[:, None, None], 0), axis=0).astype(jnp.int32)
    return first.T.reshape(-1), last.T.reshape(-1)


def _attn_fwd_t(qx, kx, vt, first_blk):
    _, s, _ = qx.shape
    tq = min(TQ, s)
    n_q = s // tq
    neg = -1e30

    def body(first_ref, qx_ref, kx_ref, vt_ref, o_ref, lse_ref, acc_ref, m_ref, s_even, s_odd):
        p = pl.program_id(0)
        i = pl.program_id(1)
        acc_ref[...] = jnp.zeros(acc_ref.shape, F32)
        m_ref[...] = jnp.full(m_ref.shape, neg, F32)
        key_le_query = (lax.broadcasted_iota(jnp.int32, (tq, tq), 0) <= lax.broadcasted_iota(jnp.int32, (tq, tq), 1))
        first = first_ref[p * n_q + i]

        def scores(kb, hh, dst):
            rows_k = pl.ds(pl.multiple_of(kb * tq, tq), tq)
            dst[hh] = lax.dot_general(kx_ref[hh, rows_k, :], qx_ref[hh], _NT, preferred_element_type=F32)

        def step(kb, src, nxt):
            for hh in range(2):
                st = src[hh]
                if nxt is None:
                    st = jnp.where(key_le_query, st, -jnp.inf)
                else:
                    scores(kb + 1, hh, nxt)
                m_old = m_ref[hh]
                m_new = jnp.maximum(m_old, jnp.max(st, axis=0, keepdims=True))
                m_ref[hh] = m_new
                pt = jnp.exp(st - m_new).astype(BF16)
                acc_ref[hh] = acc_ref[hh] * jnp.exp(m_old - m_new) + lax.dot_general(
                    vt_ref[hh, kb], pt, _NN, preferred_element_type=F32)

        def by_parity(kb, fn):
            @pl.when(kb % 2 == 0)
            def _():
                fn(s_even, s_odd)

            @pl.when(kb % 2 == 1)
            def _():
                fn(s_odd, s_even)

        def first_scores(src, nxt):
            scores(first, 0, src)
            scores(first, 1, src)

        def unmasked(kb, carry):
            by_parity(kb, lambda src, nxt: step(kb, src, nxt))
            return carry

        by_parity(first, first_scores)
        lax.fori_loop(first, i, unmasked, 0)
        by_parity(i, lambda src, nxt: step(i, src, None))
        outs, lses = [], []
        for hh in range(2):
            acc = acc_ref[hh]
            l = acc[AUG:AUG + 1, :]
            outs.append(acc[0:HEAD_DIM, :] / l)
            lses.append(m_ref[hh] + jnp.log(l))
        o_ref[...] = jnp.concatenate(outs, axis=0).T
        rows8 = lax.broadcasted_iota(jnp.int32, (N_HEADS, tq), 0)
        lse_ref[0, 0] = jnp.where(rows8 == 0, lses[0], jnp.where(rows8 == 1, lses[1], 0.0))

    grid_spec = pltpu.PrefetchScalarGridSpec(
        num_scalar_prefetch=1, grid=(N_PAIRS, n_q),
        in_specs=[pl.BlockSpec((2, tq, LANES), lambda p, i, first: (p, i, 0)),
                  pl.BlockSpec((2, s, LANES), lambda p, i, first: (p, 0, 0)),
                  pl.BlockSpec((2, n_q, LANES, tq), lambda p, i, first: (p, 0, 0, 0))],
        out_specs=[pl.BlockSpec((tq, LANES), lambda p, i, first: (i, p)),
                   pl.BlockSpec((1, 1, N_HEADS, tq), lambda p, i, first: (p, i, 0, 0))],
        scratch_shapes=[pltpu.VMEM((2, LANES, tq), F32), pltpu.VMEM((2, 1, tq), F32),
                        pltpu.VMEM((2, tq, tq), F32), pltpu.VMEM((2, tq, tq), F32)])
    return pl.pallas_call(
        body, name="attn_fwd", grid_spec=grid_spec,
        out_shape=[jax.ShapeDtypeStruct((s, ATTN_W), F32), jax.ShapeDtypeStruct((N_PAIRS, n_q, N_HEADS, tq), F32)],
        compiler_params=_cp(("parallel", "arbitrary")))(first_blk, qx, kx, vt)


def _attn_bwd_t(qx, dox, kx, kxt, vx, lse, last_blk):
    _, s, _ = qx.shape
    tq = min(TQ, s)
    n_q = s // tq

    def body(last_ref, qx_ref, dox_ref, lse_ref, kx_ref, kxt_ref, vx_ref, dk_ref, dv_ref, dfk_ref, dqt_ref, dfq_ref):
        p = pl.program_id(0)
        j = pl.program_id(1)

        @pl.when(j == 0)
        def _():
            dqt_ref[...] = jnp.zeros(dqt_ref.shape, F32)
            dfq_ref[...] = jnp.zeros(dfq_ref.shape, F32)

        key_le_query = (lax.broadcasted_iota(jnp.int32, (tq, tq), 0) <= lax.broadcasted_iota(jnp.int32, (tq, tq), 1))

        def step(i, carry, masked):
            rows_q = pl.ds(pl.multiple_of(i * tq, tq), tq)
            out = []
            for hh in range(2):
                dk, dv, col = carry[3 * hh:3 * hh + 3]
                q, do = qx_ref[hh, rows_q, :], dox_ref[hh, rows_q, :]
                st = lax.dot_general(kx_ref[hh], q, _NT, preferred_element_type=F32)
                pt = jnp.exp(st - lse_ref[0, i, hh:hh + 1, :])
                if masked:
                    pt = jnp.where(key_le_query, pt, 0.0)
                dst = pt * lax.dot_general(vx_ref[hh], do, _NT, preferred_element_type=F32)
                pb, dsb = pt.astype(BF16), dst.astype(BF16)
                dv = dv + lax.dot_general(pb, do, _NN, preferred_element_type=F32)
                dk = dk + lax.dot_general(dsb, q, _NN, preferred_element_type=F32)
                dqt_ref[hh, i] += lax.dot_general(kxt_ref[hh, 0], dsb, _NN, preferred_element_type=F32)
                for cb in range(tq // LANES):
                    col = col + dst[:, cb * LANES:(cb + 1) * LANES]
                dfq_ref[hh, i] += jnp.sum(dst.reshape(tq // HALO, HALO, tq), axis=0)
                out += [dk, dv, col]
            return tuple(out)

        zero = jnp.zeros((tq, LANES), F32)
        carry = step(j, (zero,) * 6, True)
        dk_a, dv_a, col_a, dk_b, dv_b, col_b = lax.fori_loop(j + 1, last_ref[p * n_q + j] + 1,
                                                             lambda i, cr: step(i, cr, False), carry)
        head_a = lax.broadcasted_iota(jnp.int32, (tq, LANES), 1) < HEAD_DIM
        dk_ref[...] = jnp.where(head_a, dk_a, pltpu.roll(dk_b, HEAD_DIM, 1)).astype(BF16)
        dv_ref[...] = jnp.where(head_a, dv_a, pltpu.roll(dv_b, HEAD_DIM, 1)).astype(BF16)
        rows8 = lax.broadcasted_iota(jnp.int32, (N_HEADS, tq), 0)
        dfk_a, dfk_b = (-jnp.sum(c.T, axis=0, keepdims=True) for c in (col_a, col_b))
        dfk_ref[0, 0] = jnp.where(rows8 == 0, dfk_a, jnp.where(rows8 == 1, dfk_b, 0.0))

    resident = pl.BlockSpec((2, s, LANES), lambda p, j, last: (p, 0, 0))
    key_rows = pl.BlockSpec((2, tq, LANES), lambda p, j, last: (p, j, 0))
    pair_out = pl.BlockSpec((tq, LANES), lambda p, j, last: (j, p))
    grid_spec = pltpu.PrefetchScalarGridSpec(
        num_scalar_prefetch=1, grid=(N_PAIRS, n_q),
        in_specs=[resident, resident, pl.BlockSpec((1, n_q, N_HEADS, tq), lambda p, j, last: (p, 0, 0, 0)),
                  key_rows, pl.BlockSpec((2, 1, LANES, tq), lambda p, j, last: (p, j, 0, 0)), key_rows],
        out_specs=[pair_out, pair_out, pl.BlockSpec((1, 1, N_HEADS, tq), lambda p, j, last: (p, j, 0, 0)),
                   pl.BlockSpec((2, n_q, LANES, tq), lambda p, j, last: (p, 0, 0, 0)),
                   pl.BlockSpec((2, n_q, HALO, tq), lambda p, j, last: (p, 0, 0, 0))])
    return pl.pallas_call(
        body, name="attn_bwd", grid_spec=grid_spec,
        out_shape=[jax.ShapeDtypeStruct((s, ATTN_W), BF16), jax.ShapeDtypeStruct((s, ATTN_W), BF16),
                   jax.ShapeDtypeStruct((N_PAIRS, n_q, N_HEADS, tq), F32),
                   jax.ShapeDtypeStruct((N_HEADS, n_q, LANES, tq), F32),
                   jax.ShapeDtypeStruct((N_HEADS, n_q, HALO, tq), F32)],
        compiler_params=_cp(("parallel", "arbitrary")))(last_blk, qx, dox, lse, kx, kxt, vx)


def _attn_dq_finish(dqt):
    _, n_q, _, tq = dqt.shape
    per = 4 if n_q % 4 == 0 else 1

    def body(dqt_ref, dq_ref):
        for b in range(per):
            a, bb = dqt_ref[0, b], dqt_ref[1, b]
            dq_ref[b * tq:(b + 1) * tq, :] = (
                jnp.concatenate([a[0:HEAD_DIM], bb[0:HEAD_DIM]], axis=0).T * Q_SCALE).astype(BF16)

    return pl.pallas_call(
        body, name="attn_dq_finish", grid=(N_PAIRS, n_q // per),
        in_specs=[pl.BlockSpec((2, per, LANES, tq), lambda p, i: (p, i, 0, 0))],
        out_specs=pl.BlockSpec((per * tq, LANES), lambda p, i: (i, p)),
        out_shape=jax.ShapeDtypeStruct((n_q * tq, ATTN_W), BF16),
        compiler_params=_cp(("parallel", "parallel")))(dqt)


def _ffn_act_fwd(up, w_ffn):
    s = up.shape[0]
    tm, tn = min(TM_FFN, s), TN_FFN
    nb = D_FF // tn

    def body(a_ref, g_ref, ap_ref, gp_ref, wa_ref, wg_ref, act_ref):
        i = pl.program_id(1)

        def conv(blk_ref, prev_ref, w_ref):
            prev = jnp.where(i > 0, prev_ref[...], 0.0)
            return _conv_taps(jnp.concatenate([prev, blk_ref[...]], axis=0), w_ref[...])[HALO:]

        u_a, u_g = conv(a_ref, ap_ref, wa_ref), conv(g_ref, gp_ref, wg_ref)
        act_ref[...] = (u_g * jax.nn.sigmoid(u_g) * u_a).astype(BF16)

    blk = lambda off: pl.BlockSpec((tm, tn), lambda n, i: (i, off + n))
    prev = lambda off: pl.BlockSpec((HALO, tn), lambda n, i: (jnp.maximum(i * (tm // HALO) - 1, 0), off + n))
    wsp = lambda off: pl.BlockSpec((3, tn), lambda n, i: (0, off + n))
    return pl.pallas_call(
        body, name="ffn_act_fwd", grid=(nb, s // tm),
        in_specs=[blk(0), blk(nb), prev(0), prev(nb), wsp(0), wsp(nb)],
        out_specs=pl.BlockSpec((tm, tn), lambda n, i: (i, n)),
        out_shape=jax.ShapeDtypeStruct((s, D_FF), BF16),
        compiler_params=_cp(("parallel", "parallel")))(up, up, up, up, w_ffn, w_ffn)


def _ffn_act_bwd(up, dact, w_ffn):
    s = up.shape[0]
    tm, tn = min(TM_FFN, s), TN_FFN
    nb = D_FF // tn
    n_blk = s // tm

    def body(a_ref, g_ref, ap_ref, gp_ref, an_ref, gn_ref, d_ref, dn_ref, wa_ref, wg_ref,
             dup_ref, gwa_ref, gwg_ref):
        i = pl.program_id(1)

        @pl.when(i == 0)
        def _():
            gwa_ref[...] = jnp.zeros_like(gwa_ref)
            gwg_ref[...] = jnp.zeros_like(gwg_ref)

        def ext(blk_ref, prev_ref, next_ref):
            return jnp.concatenate([jnp.where(i > 0, prev_ref[...], 0.0), blk_ref[...], next_ref[...]], axis=0)

        wa, wg = wa_ref[...], wg_ref[...]
        up_a, up_g = ext(a_ref, ap_ref, an_ref), ext(g_ref, gp_ref, gn_ref)
        u_a, u_g = _conv_taps(up_a, wa), _conv_taps(up_g, wg)
        d_e = jnp.concatenate([jnp.zeros((HALO, tn), F32), d_ref[...],
                               jnp.where(i < n_blk - 1, dn_ref[...], 0.0)], axis=0)
        sig = jax.nn.sigmoid(u_g)
        du_a = d_e * (u_g * sig)
        du_g = d_e * u_a * (sig * (1.0 + u_g * (1.0 - sig)))
        blk = slice(HALO, HALO + tm)
        dup_ref[0] = _conv_taps_t(du_a, wa)[blk].astype(BF16)
        dup_ref[1] = _conv_taps_t(du_g, wg)[blk].astype(BF16)
        for gw_ref, upv, du in ((gwa_ref, up_a[blk], du_a), (gwg_ref, up_g[blk], du_g)):
            gw_ref[0:1, :] += jnp.sum(upv * _shift_up(du, 2)[blk], axis=0, keepdims=True)
            gw_ref[1:2, :] += jnp.sum(upv * _shift_up(du, 1)[blk], axis=0, keepdims=True)
            gw_ref[2:3, :] += jnp.sum(upv * du[blk], axis=0, keepdims=True)

    blk = lambda off: pl.BlockSpec((tm, tn), lambda n, i: (i, off + n))
    prev = lambda off: pl.BlockSpec((HALO, tn), lambda n, i: (jnp.maximum(i * (tm // HALO) - 1, 0), off + n))
    nxt = lambda off: pl.BlockSpec(
        (HALO, tn), lambda n, i: (jnp.minimum((i + 1) * (tm // HALO), s // HALO - 1), off + n))
    wsp = lambda off: pl.BlockSpec((3, tn), lambda n, i: (0, off + n))
    return pl.pallas_call(
        body, name="ffn_act_bwd", grid=(nb, n_blk),
        in_specs=[blk(0), blk(nb), prev(0), prev(nb), nxt(0), nxt(nb), blk(0), nxt(0), wsp(0), wsp(nb)],
        out_specs=[pl.BlockSpec((2, tm, tn), lambda n, i: (0, i, n)), wsp(0), wsp(0)],
        out_shape=[jax.ShapeDtypeStruct((2, s, D_FF), BF16),
                   jax.ShapeDtypeStruct((3, D_FF), F32), jax.ShapeDtypeStruct((3, D_FF), F32)],
        compiler_params=_cp(("parallel", "arbitrary")))(up, up, up, up, up, up, dact, dact, w_ffn, w_ffn)


def _adamw(w, g, m, v, name):
    r, c = w.shape
    tr = next((t for t in (512, 352, 256, 128, 64, 32, 16, 8) if r > t and r % t == 0), r)

    def body(w_ref, g_ref, m_ref, v_ref, d_ref, nm_ref, nv_ref):
        gv = g_ref[...]
        m_new = ADAM_B1 * m_ref[...] + (1.0 - ADAM_B1) * gv
        v_new = ADAM_B2 * v_ref[...] + (1.0 - ADAM_B2) * (gv * gv)
        m_hat = m_new / (1.0 - ADAM_B1 ** ADAM_STEP)
        v_hat = v_new / (1.0 - ADAM_B2 ** ADAM_STEP)
        d_ref[...] = -ADAM_LR * (m_hat / (jnp.sqrt(v_hat) + ADAM_EPS) + ADAM_WD * w_ref[...])
        nm_ref[...] = m_new
        nv_ref[...] = v_new

    spec = pl.BlockSpec((tr, c), lambda i: (i, 0))
    shp = jax.ShapeDtypeStruct((r, c), F32)
    return pl.pallas_call(
        body, name=name, grid=(r // tr,), in_specs=[spec] * 4, out_specs=[spec] * 3, out_shape=[shp] * 3,
        compiler_params=_cp(("parallel",)))(w, g, m, v)


def _sum_rows_block(h):
    return h if h <= 352 else 256


def _pair_sum(view, recv, sel, name):
    n, _, h, c = view.shape
    tr = _sum_rows_block(h)

    def body(sel_ref, a_ref, b_ref, o_ref, ob_ref):
        t = a_ref[...] + b_ref[...]
        o_ref[...] = t
        ob_ref[...] = t.astype(BF16)

    blk = pl.BlockSpec((None, tr, c), lambda j, i, sel_ref: (j, i, 0))
    grid_spec = pltpu.PrefetchScalarGridSpec(
        num_scalar_prefetch=1, grid=(n, h // tr),
        in_specs=[pl.BlockSpec((None, None, tr, c), lambda j, i, sel_ref: (j, sel_ref[0], i, 0)),
                  pl.BlockSpec((None, None, tr, c), lambda j, i, sel_ref: (j, 0, i, 0))],
        out_specs=[blk, blk])
    return pl.pallas_call(
        body, name=name, grid_spec=grid_spec,
        out_shape=[jax.ShapeDtypeStruct((n, h, c), F32), jax.ShapeDtypeStruct((n, h, c), BF16)],
        compiler_params=_cp(("parallel", "parallel")))(sel, view, recv)


def _chip_sum(pair, got, sel, name):
    _, h, c = pair.shape
    tr = _sum_rows_block(h)
    nblk = h // tr

    def body(sel_ref, p_ref, g0_ref, g1_ref, g2_ref, o_ref):
        o_ref[...] = ((p_ref[...] + g0_ref[...].astype(F32)) + g1_ref[...].astype(F32)) + g2_ref[...].astype(F32)

    slot = lambda k: pl.BlockSpec((None, tr, c), lambda i, sel_ref: (k, i, 0))
    grid_spec = pltpu.PrefetchScalarGridSpec(
        num_scalar_prefetch=1, grid=(h // tr,),
        in_specs=[pl.BlockSpec((None, tr, c), lambda i, sel_ref: (sel_ref[1], i, 0)), slot(0), slot(1), slot(2)],
        out_specs=pl.BlockSpec((tr, c), lambda i, sel_ref: (sel_ref[0] * nblk + i, 0)))
    return pl.pallas_call(
        body, name=name, grid_spec=grid_spec, out_shape=jax.ShapeDtypeStruct((2 * h, c), F32),
        compiler_params=_cp(("parallel",)))(sel, pair, got, got, got)


def _place():
    return lax.axis_index("x"), lax.axis_index("y"), lax.axis_index("c")


def _other_chips(x, y):
    return [(1 - x, y), (x, 1 - y), (1 - x, 1 - y)]


def _hbm_specs(n):
    return [pl.BlockSpec(memory_space=pl.ANY)] * n


def _all_gather_weights(bigs, smalls):
    nb, ns = len(bigs), len(smalls)
    n = nb + ns

    def body(*refs):
        start, forward, finish = _gather_phases(refs[:n], refs[2 * n:3 * n], nb, *refs[3 * n:])
        start()
        forward()
        finish()

    arrays, landing, sems = _gather_operands(bigs, smalls)
    return pl.pallas_call(
        body, name="all_gather_weights",
        out_shape=[jax.ShapeDtypeStruct(b.shape, b.dtype) for b in landing],
        in_specs=_hbm_specs(2 * n), out_specs=_hbm_specs(n), input_output_aliases={n + k: k for k in range(n)},
        scratch_shapes=sems)(*arrays, *landing)


def _hosted_gather(refs, n, nb, step, total):
    ins, outs, send_sems, recv_sems = refs
    start, forward, finish = _gather_phases(ins, outs, nb, send_sems, recv_sems)
    pl.when(step == 0)(start)
    pl.when(step == (3 * total) // 4)(forward)
    return lambda: pl.when(step == total - 1)(finish)


def _in_proj(x, g, w_a, w_b, w_c, bigs, smalls):
    s, d = x.shape
    tm, tn = min(TM_MM, s), ATTN_W
    na, nq = w_a.shape[1] // tn, w_b.shape[1] // tn
    steps = na + nq + 1
    total = (s // tm) * steps
    nb, n = len(bigs), len(bigs) + len(smalls)
    arrays, landing, sems = _gather_operands(bigs, smalls)

    def body(x_ref, g_ref, wa_ref, wb_ref, wc_ref, *rest):
        z_ref, qkv_ref, f_ref, h_ref = rest[2 * n:2 * n + 4]
        h_scr = rest[-1]
        m, j = pl.program_id(0), pl.program_id(1)
        finish = _hosted_gather((rest[:n], rest[2 * n + 4:3 * n + 4]) + tuple(rest[3 * n + 4:3 * n + 6]), n, nb,
                                m * steps + j, total)

        @pl.when(j == 0)
        def _():
            xv = x_ref[...]
            hv = (xv * _rstd(xv) * g_ref[...]).astype(BF16)
            h_scr[...] = hv
            h_ref[...] = hv

        h = h_scr[...]

        @pl.when(j < na)
        def _():
            z_ref[...] = lax.dot_general(h, wa_ref[...], _NN, preferred_element_type=F32)

        @pl.when((j >= na) & (j < na + nq))
        def _():
            qkv_ref[...] = lax.dot_general(h, wb_ref[...], _NN, preferred_element_type=F32).astype(BF16)

        @pl.when(j == na + nq)
        def _():
            f_ref[...] = lax.dot_general(h, wc_ref[...], _NN, preferred_element_type=F32)

        finish()

    blk_a = lambda m, j: (m, jnp.minimum(j, na - 1))
    blk_b = lambda m, j: (m, jnp.clip(j - na, 0, nq - 1))
    outs = pl.pallas_call(
        body, name="in_proj", grid=(s // tm, steps),
        in_specs=[pl.BlockSpec((tm, d), lambda m, j: (m, 0)), pl.BlockSpec((1, d), lambda m, j: (0, 0)),
                  pl.BlockSpec((d, tn), lambda m, j: (0, jnp.minimum(j, na - 1))),
                  pl.BlockSpec((d, tn), lambda m, j: (0, jnp.clip(j - na, 0, nq - 1))),
                  pl.BlockSpec((d, LANES), lambda m, j: (0, 0))] + _hbm_specs(2 * n),
        out_specs=[pl.BlockSpec((tm, tn), blk_a), pl.BlockSpec((tm, tn), blk_b),
                   pl.BlockSpec((tm, LANES), lambda m, j: (m, 0)), pl.BlockSpec((tm, d), lambda m, j: (m, 0))]
        + _hbm_specs(n),
        out_shape=[jax.ShapeDtypeStruct((s, w_a.shape[1]), F32), jax.ShapeDtypeStruct((s, w_b.shape[1]), BF16),
                   jax.ShapeDtypeStruct((s, LANES), F32), jax.ShapeDtypeStruct((s, d), BF16)]
        + [jax.ShapeDtypeStruct(b.shape, b.dtype) for b in landing],
        input_output_aliases={5 + n + k: 4 + k for k in range(n)},
        scratch_shapes=sems + [pltpu.VMEM((tm, d), BF16)],
        compiler_params=_cp(("arbitrary", "arbitrary")))(x, g, w_a, w_b, w_c, *arrays, *landing)
    return outs[0], outs[1], outs[2], outs[3], outs[4:]


def _gather_operands(bigs, smalls):
    x, y, _ = _place()
    arrays = list(bigs) + list(smalls)
    landing = [lax.dynamic_update_index_in_dim(lax.empty((N_CHIPS,) + a.shape, a.dtype), a, 2 * x + y, 0)
               for a in arrays]
    n_sems = 6 * len(bigs) + 3 * len(smalls)
    return arrays, landing, [pltpu.SemaphoreType.DMA((n_sems,)), pltpu.SemaphoreType.DMA((n_sems,))]


def _gather_phases(ins, outs, nb, send_sems, recv_sems):
    n = len(ins)
    x, y, c = _place()
    my_chip = 2 * x + y
    chips = _other_chips(x, y)
    sibling = (x, y, 1 - c)

    def rows(k, which):
        h = ins[k].shape[0] // 2
        return pl.ds(which * h, h)

    def copy(sem, src, dst, to):
        return pltpu.make_async_remote_copy(src_ref=src, dst_ref=dst, send_sem=send_sems.at[sem],
                                            recv_sem=recv_sems.at[sem], device_id=to, device_id_type=MESH)

    def sends():
        out = [copy(6 * k + j, ins[k].at[rows(k, c)], outs[k].at[my_chip, rows(k, c)], (cx, cy, c))
               for k in range(nb) for j, (cx, cy) in enumerate(chips)]
        return out + [copy(6 * nb + 3 * (k - nb) + j, ins[k], outs[k].at[my_chip], (cx, cy, c))
                      for k in range(nb, n) for j, (cx, cy) in enumerate(chips)]

    def landed(k, j, which):
        cx, cy = chips[j]
        return outs[k].at[2 * cx + cy, rows(k, which)]

    def forwards():
        return [copy(6 * k + 3 + j, landed(k, j, c), landed(k, j, c), sibling)
                for j in range(3) for k in range(nb)]

    def start():
        for cp in sends():
            cp.start()

    def forward():
        for j in range(3):
            for k in range(nb):
                copy(6 * k + j, landed(k, j, c), landed(k, j, c), (x, y, c)).wait_recv()
                copy(6 * k + 3 + j, landed(k, j, c), landed(k, j, c), sibling).start()

    def finish():
        for j, (cx, cy) in enumerate(chips):
            for k in range(nb):
                copy(6 * k + 3 + j, landed(k, j, 1 - c), landed(k, j, 1 - c), (x, y, c)).wait_recv()
            for k in range(nb, n):
                arrived = outs[k].at[2 * cx + cy]
                copy(6 * nb + 3 * (k - nb) + j, arrived, arrived, (x, y, c)).wait_recv()
        for cp in sends() + forwards():
            cp.wait_send()

    return start, forward, finish


def _pair_exchange(views, name):
    n = len(views)

    def body(*refs):
        ins, outs, send_sems, recv_sems = refs[:n], refs[n:2 * n], refs[2 * n], refs[2 * n + 1]
        x, y, c = _place()
        copies = [pltpu.make_async_remote_copy(
            src_ref=ins[k].at[:, pl.ds(1 - c, 1)], dst_ref=outs[k], send_sem=send_sems.at[k],
            recv_sem=recv_sems.at[k], device_id=(x, y, 1 - c), device_id_type=MESH) for k in range(n)]
        for cp in copies:
            cp.start()
        for cp in copies:
            cp.wait()

    return pl.pallas_call(
        body, name=name,
        out_shape=[jax.ShapeDtypeStruct((v.shape[0], 1) + v.shape[2:], v.dtype) for v in views],
        in_specs=_hbm_specs(n), out_specs=_hbm_specs(n),
        scratch_shapes=[pltpu.SemaphoreType.DMA((n,)), pltpu.SemaphoreType.DMA((n,))])(*views)


def _scatter_operands(parts):
    n = len(parts)
    return ([jax.ShapeDtypeStruct((3,) + p.shape[1:], p.dtype) for p in parts],
            [pltpu.SemaphoreType.DMA((3 * n,)), pltpu.SemaphoreType.DMA((3 * n,))])


def _scatter_phases(ins, outs, send_sems, recv_sems):
    x, y, c = _place()

    def copies():
        return [pltpu.make_async_remote_copy(
            src_ref=ins[k].at[pl.ds(2 * cx + cy, 1)], dst_ref=outs[k].at[pl.ds(r, 1)], send_sem=send_sems.at[3 * k + r],
            recv_sem=recv_sems.at[3 * k + r], device_id=(cx, cy, c), device_id_type=MESH)
            for k in range(len(ins)) for r, (cx, cy) in enumerate(_other_chips(x, y))]

    def start():
        for cp in copies():
            cp.start()

    def finish():
        for cp in copies():
            cp.wait()

    return start, finish


def _hosted_scatter(ins, outs, sems, step, total):
    start, finish = _scatter_phases(ins, outs, *sems)
    pl.when(step == 0)(start)
    return lambda: pl.when(step == total - 1)(finish)


def _join_halves(shards):
    n = len(shards)

    def body(*refs):
        ins, outs, send_sems, recv_sems = refs[:n], refs[n:2 * n], refs[2 * n], refs[2 * n + 1]
        x, y, c = _place()

        def rows(ref, which):
            h = ref.shape[0] // 2
            return ref.at[pl.ds(which * h, h)]

        sent = [pltpu.make_async_remote_copy(
            src_ref=rows(ins[k], c), dst_ref=rows(outs[k], c), send_sem=send_sems.at[k], recv_sem=recv_sems.at[k],
            device_id=(x, y, 1 - c), device_id_type=MESH) for k in range(n)]
        for cp in sent:
            cp.start()
        for k in range(n):
            pltpu.make_async_remote_copy(
                src_ref=rows(ins[k], 1 - c), dst_ref=rows(outs[k], 1 - c), send_sem=send_sems.at[k],
                recv_sem=recv_sems.at[k], device_id=(x, y, 1 - c), device_id_type=MESH).wait_recv()
        for cp in sent:
            cp.wait_send()

    return pl.pallas_call(
        body, name="half_exchange", out_shape=[jax.ShapeDtypeStruct(a.shape, a.dtype) for a in shards],
        in_specs=_hbm_specs(n), out_specs=_hbm_specs(n), input_output_aliases={k: k for k in range(n)},
        scratch_shapes=[pltpu.SemaphoreType.DMA((n,)), pltpu.SemaphoreType.DMA((n,))])(*shards)


def _all_reduce_small(packet):
    rows, width = packet.shape
    n_dev = 8

    def body(x_ref, out_ref, gath, send_sems, recv_sems):
        x, y, c = _place()
        me, sibling = (x, y, c), (x, y, 1 - c)
        chips = _other_chips(x, y)

        def slot(px, py, pc):
            return gath.at[pl.ds((4 * px + 2 * py + pc) * rows, rows), :]

        def copy(k, block, to, src=None):
            return pltpu.make_async_remote_copy(
                src_ref=slot(*block) if src is None else src, dst_ref=slot(*block), send_sem=send_sems.at[k],
                recv_sem=recv_sems.at[k], device_id=to, device_id_type=MESH)

        first = [copy(0, me, sibling, src=x_ref)]
        first += [copy(1 + j, me, (*chip, c), src=x_ref) for j, chip in enumerate(chips)]
        for cp in first:
            cp.start()
        gath[pl.ds((4 * x + 2 * y + c) * rows, rows), :] = x_ref[...]
        passed = [copy(4 + j, (*chip, c), sibling) for j, chip in enumerate(chips)]
        for j, chip in enumerate(chips):
            copy(1 + j, (*chip, c), me).wait_recv()
            passed[j].start()
        copy(0, sibling, me).wait_recv()
        for j, chip in enumerate(chips):
            copy(4 + j, (*chip, 1 - c), me).wait_recv()
        for cp in first + passed:
            cp.wait_send()
        acc = gath[0:rows, :]
        for d in range(1, n_dev):
            acc = acc + gath[d * rows:(d + 1) * rows, :]
        out_ref[...] = acc

    return pl.pallas_call(
        body, name="all_reduce_small", out_shape=jax.ShapeDtypeStruct((rows, width), F32),
        in_specs=[pl.BlockSpec(memory_space=pltpu.VMEM)], out_specs=pl.BlockSpec(memory_space=pltpu.VMEM),
        scratch_shapes=[pltpu.VMEM((n_dev * rows, width), F32), pltpu.SemaphoreType.DMA((7,)),
                        pltpu.SemaphoreType.DMA((7,))])(packet)


def _flat_rows(parts, width, row_multiple):
    flat = jnp.concatenate([p.astype(F32).reshape(-1) for p in parts])
    rows = -(-flat.shape[0] // width)
    rows = -(-rows // row_multiple) * row_multiple
    return jnp.pad(flat, (0, rows * width - flat.shape[0])).reshape(rows, width)


def _unflatten(flat2d, shapes):
    flat = flat2d.reshape(-1)
    out, off = [], 0
    for shp in shapes:
        n = 1
        for dim in shp:
            n *= dim
        out.append(flat[off:off + n].reshape(shp))
        off += n
    return out


def _core_and_chip():
    x, y, c = _place()
    return jnp.stack([c, 2 * x + y]).astype(jnp.int32)


def _pair_sums(chip_major, names, call_name):
    views = [g.reshape(N_CHIPS, 2, g.shape[1] // 2, g.shape[2]) for g in chip_major]
    recv = _pair_exchange(views, call_name)
    sel = _core_and_chip()
    return [_pair_sum(v, r, sel, "pair_sum_" + nm) for v, r, nm in zip(views, recv, names)]


def _finish_grads(pairs, got, names):
    sel = _core_and_chip()
    return _join_halves([_chip_sum(p, g, sel, "chip_sum_" + nm) for (p, _), g, nm in zip(pairs, got, names)])


def kernel(x, g_mix, w_in, b_f, w_conv, g_conv_out, g_attn_out, w_o, g_ffn, w_up, w_ffn_conv, w_down, g_final, loss_target, m_g_mix, m_w_in, m_b_f, m_w_conv, m_g_conv_out, m_g_attn_out, m_w_o, m_g_ffn, m_w_up, m_w_ffn_conv, m_w_down, m_g_final, v_g_mix, v_w_in, v_b_f, v_w_conv, v_g_conv_out, v_g_attn_out, v_w_o, v_g_ffn, v_w_up, v_w_ffn_conv, v_w_down, v_g_final):
    s = x.shape[1]
    x0 = x[0]
    target = loss_target[0]
    d = D_MODEL
    x_pos, y_pos, _ = _place()
    my_chip = 2 * x_pos + y_pos

    (c_in,) = _all_gather_weights([w_in[0].astype(BF16)], [])
    w_in_full = jnp.concatenate([c_in[j] for j in range(N_CHIPS)], axis=1)
    c3 = 3 * CONV_CH
    w_a, w_b = w_in_full[:, :c3], w_in_full[:, c3:c3 + 3 * ATTN_W]
    w_c = jnp.pad(w_in_full[:, c3 + 3 * ATTN_W:], ((0, 0), (0, LANES - N_HEADS)))
    w_q, w_k, w_v = (w_b[:, i * ATTN_W:(i + 1) * ATTN_W] for i in range(3))
    b_pad = jnp.pad(b_f, ((0, 0), (0, LANES - N_HEADS)))

    z_a, qkv, f_log, h1, (c_o, c_up, c_conv, c_ffn) = _in_proj(
        x0, g_mix, w_a, w_b, w_c, [w_o[0].astype(BF16), w_up[0].astype(BF16)], [w_conv[0], w_ffn_conv[0]])
    fb = _gate_fwd(f_log, b_pad)
    qx, kx, kxt, vx, vt, bounds, (c_down,) = _attn_prep(qkv, fb, [w_down[0].astype(BF16)])
    w_o_full = c_o.reshape(d, d)
    w_down_full = c_down.reshape(D_FF, d)
    w_conv_full = jnp.concatenate([c_conv[j] for j in range(N_CHIPS)], axis=1)
    w_ffn_full = jnp.concatenate([c_ffn[j] for j in range(N_CHIPS)], axis=1)
    n_up = c_up.shape[2]
    first_blk, last_blk = _key_block_ranges(bounds)
    o_attn, lse = _attn_fwd_t(qx, kx, vt, first_blk)
    mix = _mixer_fwd(z_a, o_attn, w_conv_full, g_conv_out, g_attn_out)
    x2, h2 = _mm("nn", [mix], [w_o_full], F32, 512, d, "out_proj", add=x0, norm_g=g_ffn)
    up = _mm("nn", [h2], [c_up], F32, TM_MM, n_up, "up_proj", b_chips=True)
    act = _ffn_act_fwd(up, w_ffn_full)
    dx3, dx3_b, loss_row, gg_final = _down_proj_loss(act, w_down_full, x2, target, g_final.reshape(1, d))

    dact = _mm("nt", [dx3_b], [w_down_full], F32, TM_MM, 1408, "d_act")
    gw_down = _mm_tn(act, dx3_b, 1408, 1024, "gw_down")
    dup, gwf_lin, gwf_gate = _ffn_act_bwd(up, dact, w_ffn_full)
    dh2 = _mm("nt", [(dup, j // 2, j % 2, n_up) for j in range(N_CHIPS)], [(c_up, j) for j in range(N_CHIPS)],
              F32, TM_MM, 512, "d_h2")
    gw_up = _mm_tn(h2, dup, 1024, n_up, "gw_up", out_chips=True)
    dx2, dx2_b, gg_ffn = _rms_bwd(x2, dh2, g_ffn, dx3, "rms_ffn_bwd", True)
    dmix = _mm("nt", [dx2_b], [w_o_full], F32, TM_MM, 512, "d_mix")
    gw_o = _mm_tn(mix, dx2_b, 1024, 1024, "gw_o")
    early = _pair_sums([gw_o.reshape(N_CHIPS, d // N_CHIPS, d), gw_up, gw_down.reshape(N_CHIPS, D_FF // N_CHIPS, d)],
                       ["w_o", "w_up", "w_down"], "pair_exchange")
    dz_a, dox, gw_conv, gg_conv_out, gg_attn_out, (got_o, got_down) = _mixer_bwd(
        z_a, o_attn, dmix, w_conv_full, g_conv_out, g_attn_out, [early[0][1], early[2][1]])
    dk, dv, dfk, dqt, dfq = _attn_bwd_t(qx, dox, kx, kxt, vx, lse, last_blk)
    dq = _attn_dq_finish(dqt)
    d_f = (jnp.transpose(dfk[:, :, 0:2, :], (1, 3, 0, 2)).reshape(s, N_HEADS)
           + jnp.transpose(jnp.sum(dfq, axis=2), (1, 2, 0)).reshape(s, N_HEADS))
    df_b, gb_f = _gate_bwd(f_log, b_pad, jnp.pad(d_f, ((0, 0), (0, LANES - N_HEADS))))
    gw_a = _mm_tn(h1, dz_a, 1024, c3, "gw_in_conv")
    gw_q = _mm_tn(h1, dq, 1024, ATTN_W, "gw_in_q")
    gw_k = _mm_tn(h1, dk, 1024, ATTN_W, "gw_in_k")
    gw_v = _mm_tn(h1, dv, 1024, ATTN_W, "gw_in_v")
    gw_c = _mm_tn(h1, df_b, 1024, LANES, "gw_in_gate")
    gw_in = jnp.concatenate([gw_a, gw_q, gw_k, gw_v, gw_c[:, :N_HEADS]], axis=1)
    n_in = IN_COLS // N_CHIPS
    gw_in = jnp.stack([gw_in[:, j * n_in:(j + 1) * n_in] for j in range(N_CHIPS)])
    late = _pair_sums([gw_in], ["w_in"], "pair_exchange_w_in")
    dh1, (got_up, got_in) = _mm("nt", [dz_a, dq, dk, dv, df_b], [w_a, w_q, w_k, w_v, w_c], F32, TM_MM, 512, "d_h1",
                                scatter=[early[1][1], late[0][1]])
    grad_x, gg_mix = _rms_bwd(x0, dh1, g_mix, dx2, "rms_mix_bwd", False)
    g_w_in, g_w_o, g_w_up, g_w_down = _finish_grads(late + early, [got_in, got_o, got_up, got_down],
                                                    ["w_in", "w_o", "w_up", "w_down"])

    gw_ffn = jnp.concatenate([gwf_lin, gwf_gate], axis=1)
    small_parts = [gg_mix, gg_conv_out, gg_attn_out, gg_ffn, gg_final, gb_f[:, :N_HEADS], loss_row[:, 0:1], gw_conv,
                   gw_ffn]
    small_shapes = [a.shape for a in small_parts]
    tot = _unflatten(_all_reduce_small(_flat_rows(small_parts, d, 8)), small_shapes)
    g_g_mix, g_g_conv_out, g_g_attn_out, g_g_ffn, g_g_final, g_b_f, loss_sum, g_conv_full, g_ffn_full = tot
    loss = loss_sum[0, 0]
    g_g_final = g_g_final[0]
    g_w_conv = lax.dynamic_slice_in_dim(g_conv_full, my_chip * (CONV_CH // N_CHIPS), CONV_CH // N_CHIPS, axis=1)
    g_w_ffn = lax.dynamic_slice_in_dim(g_ffn_full, my_chip * n_up, n_up, axis=1)

    def adam_big(w, g, m, v, name):
        dl, nm, nv = _adamw(w[0], g, m[0], v[0], name)
        return dl[None], nm[None], nv[None]

    u_w_in = adam_big(w_in, g_w_in, m_w_in, v_w_in, "adam_w_in")
    u_w_o = adam_big(w_o, g_w_o, m_w_o, v_w_o, "adam_w_o")
    u_w_up = adam_big(w_up, g_w_up, m_w_up, v_w_up, "adam_w_up")
    u_w_down = adam_big(w_down, g_w_down, m_w_down, v_w_down, "adam_w_down")

    small_w = [g_mix, b_f, g_conv_out, g_attn_out, g_ffn, g_final, w_conv, w_ffn_conv]
    small_g = [g_g_mix, g_b_f, g_g_conv_out, g_g_attn_out, g_g_ffn, g_g_final, g_w_conv, g_w_ffn]
    small_m = [m_g_mix, m_b_f, m_g_conv_out, m_g_attn_out, m_g_ffn, m_g_final, m_w_conv, m_w_ffn_conv]
    small_v = [v_g_mix, v_b_f, v_g_conv_out, v_g_attn_out, v_g_ffn, v_g_final, v_w_conv, v_w_ffn_conv]
    shapes = [a.shape for a in small_w]
    pack = lambda arrs: _flat_rows(arrs, LANES, 8)
    sd, sm, sv = _adamw(pack(small_w), pack(small_g), pack(small_m), pack(small_v), "adam_small")
    sd, sm, sv = _unflatten(sd, shapes), _unflatten(sm, shapes), _unflatten(sv, shapes)
    (d_g_mix, d_b_f, d_g_conv_out, d_g_attn_out, d_g_ffn, d_g_final, d_w_conv, d_w_ffn) = sd
    (nm_g_mix, nm_b_f, nm_g_conv_out, nm_g_attn_out, nm_g_ffn, nm_g_final, nm_w_conv, nm_w_ffn) = sm
    (nv_g_mix, nv_b_f, nv_g_conv_out, nv_g_attn_out, nv_g_ffn, nv_g_final, nv_w_conv, nv_w_ffn) = sv

    grads = (g_g_mix, g_w_in[None], g_b_f, g_w_conv[None], g_g_conv_out, g_g_attn_out, g_w_o[None], g_g_ffn,
             g_w_up[None], g_w_ffn[None], g_w_down[None], g_g_final)
    deltas = (d_g_mix, u_w_in[0], d_b_f, d_w_conv, d_g_conv_out, d_g_attn_out, u_w_o[0], d_g_ffn, u_w_up[0],
              d_w_ffn, u_w_down[0], d_g_final)
    new_m = (nm_g_mix, u_w_in[1], nm_b_f, nm_w_conv, nm_g_conv_out, nm_g_attn_out, u_w_o[1], nm_g_ffn, u_w_up[1],
             nm_w_ffn, u_w_down[1], nm_g_final)
    new_v = (nv_g_mix, u_w_in[2], nv_b_f, nv_w_conv, nv_g_conv_out, nv_g_attn_out, u_w_o[2], nv_g_ffn, u_w_up[2],
             nv_w_ffn, u_w_down[2], nv_g_final)
    return (loss, grad_x[None], *grads, *deltas, *new_m, *new_v)
```

```python
import jax
import jax.numpy as jnp
from jax import lax
from jax.experimental import pallas as pl
from jax.experimental.pallas import tpu as pltpu

F32, BF16 = jnp.float32, jnp.bfloat16
MESH = pl.DeviceIdType.MESH

D_MODEL = 1024
CONV_CH = 512
ATTN_W = 512
N_HEADS = 8
HEAD_DIM = 64
N_PAIRS = N_HEADS // 2
D_FF = 2816
IN_COLS = 3 * CONV_CH + 3 * ATTN_W + N_HEADS
EPS = 1e-6
Q_SCALE = 0.125
EXP_ZERO = 88.0
N_CHIPS = 4
LANES = 128
HALO = 8

ADAM_LR, ADAM_B1, ADAM_B2, ADAM_EPS, ADAM_WD, ADAM_STEP = 0.001, 0.9, 0.999, 1e-08, 0.01, 10

TM_ROWS = 512
TM_MM = 1024
TK_TN = 1024
TQ = 512
TM_FFN = 1024
TN_FFN = 256
VMEM_LIMIT = 52 * 2**20


def _cp(sem, vmem=VMEM_LIMIT):
    return pltpu.CompilerParams(dimension_semantics=sem, vmem_limit_bytes=vmem)


def _bf(a):
    return a if a.dtype == BF16 else a.astype(BF16)


def _mm(mode, a_list, b_list, out_dtype, tm, tn, name, add=None, b_chips=False, scatter=(), norm_g=None):
    n_p = len(a_list)
    a0 = a_list[0]
    m_dim = a0[0].shape[1] if isinstance(a0, tuple) else a0.shape[0]
    b0 = b_list[0]
    if b_chips:
        n_dim = b0.shape[0] * b0.shape[2]
        assert tn == b0.shape[2] and mode == "nn"
    else:
        b0 = b0[0][b0[1]] if isinstance(b0, tuple) else b0
        n_dim = b0.shape[1 if mode == "nn" else 0]
    tm, tn = min(tm, m_dim), min(tn, n_dim)
    assert m_dim % tm == 0 and n_dim % tn == 0
    dims = (((1,), (0,)), ((), ())) if mode == "nn" else (((1,), (1,)), ((), ()))
    in_specs, args = [], []
    for a in a_list:
        if isinstance(a, tuple):
            arr, lead, col, width = a
            in_specs.append(pl.BlockSpec((None, tm, width), lambda m, n, lead=lead, col=col: (lead, m, col)))
        else:
            arr = a
            in_specs.append(pl.BlockSpec((tm, a.shape[1]), lambda m, n: (m, 0)))
        args.append(arr)
    for b in b_list:
        if b_chips:
            arr = b
            in_specs.append(pl.BlockSpec((None, b.shape[1], tn), lambda m, n: (n, 0, 0)))
        elif isinstance(b, tuple):
            arr, lead = b
            if mode == "nn":
                in_specs.append(pl.BlockSpec((None, arr.shape[1], tn), lambda m, n, lead=lead: (lead, 0, n)))
            else:
                in_specs.append(pl.BlockSpec((None, tn, arr.shape[2]), lambda m, n, lead=lead: (lead, n, 0)))
        elif mode == "nn":
            arr = b
            in_specs.append(pl.BlockSpec((b.shape[0], tn), lambda m, n: (0, n)))
        else:
            arr = b
            in_specs.append(pl.BlockSpec((tn, b.shape[1]), lambda m, n: (n, 0)))
        args.append(arr)
    if add is not None:
        in_specs.append(pl.BlockSpec((tm, tn), lambda m, n: (m, n)))
        args.append(add)
    if norm_g is not None:
        assert tn == n_dim and not scatter
        in_specs.append(pl.BlockSpec((1, tn), lambda m, n: (0, 0)))
        args.append(norm_g)

    n_in = len(args)
    n_sc = len(scatter)
    grid = (m_dim // tm, n_dim // tn)

    def body(*refs):
        o_ref = refs[n_in + n_sc]
        if n_sc:
            finish = _hosted_scatter(refs[n_in:n_in + n_sc], refs[n_in + n_sc + 1:n_in + 2 * n_sc + 1],
                                     refs[n_in + 2 * n_sc + 1:], pl.program_id(0) * grid[1] + pl.program_id(1),
                                     grid[0] * grid[1])
        acc = None
        for i in range(n_p):
            d = lax.dot_general(_bf(refs[i][...]), _bf(refs[n_p + i][...]), dims,
                                preferred_element_type=F32)
            acc = d if acc is None else acc + d
        if add is not None:
            acc = refs[2 * n_p][...] + acc
        o_ref[...] = acc.astype(out_dtype)
        if norm_g is not None:
            refs[n_in + 1][...] = (acc * _rstd(acc) * refs[n_in - 1][...]).astype(BF16)
        if n_sc:
            finish()

    main_spec = pl.BlockSpec((tm, tn), lambda m, n: (m, n))
    main_shape = jax.ShapeDtypeStruct((m_dim, n_dim), out_dtype)
    if norm_g is not None:
        return pl.pallas_call(body, name=name, grid=grid, in_specs=in_specs, out_specs=[main_spec, main_spec],
                              out_shape=[main_shape, jax.ShapeDtypeStruct((m_dim, n_dim), BF16)],
                              compiler_params=_cp(("parallel", "parallel")))(*args)
    if not n_sc:
        return pl.pallas_call(body, name=name, grid=grid, in_specs=in_specs, out_specs=main_spec,
                              out_shape=main_shape, compiler_params=_cp(("parallel", "parallel")))(*args)
    got_shapes, sems = _scatter_operands(scatter)
    outs = pl.pallas_call(
        body, name=name, grid=grid, in_specs=in_specs + _hbm_specs(n_sc), out_specs=[main_spec] + _hbm_specs(n_sc),
        out_shape=[main_shape] + got_shapes, scratch_shapes=sems,
        compiler_params=_cp(("arbitrary", "arbitrary")))(*args, *scatter)
    return outs[0], outs[1:]


def _mm_tn(a, b, tm, tn, name, out_chips=False):
    k_dim, m_dim = a.shape
    n_dim = b.shape[-1] * (b.shape[0] if b.ndim == 3 else 1)
    tm, tn, tk = min(tm, m_dim), min(tn, b.shape[-1]), min(TK_TN, k_dim)
    assert m_dim % tm == 0 and b.shape[-1] % tn == 0 and k_dim % tk == 0
    per = b.shape[-1] // tn
    if b.ndim == 3:
        b_spec = pl.BlockSpec((None, tk, tn), lambda m, n, k: (n // per, k, n % per))
    else:
        b_spec = pl.BlockSpec((tk, tn), lambda m, n, k: (k, n))

    def body(a_ref, b_ref, o_ref):
        @pl.when(pl.program_id(2) == 0)
        def _():
            o_ref[...] = jnp.zeros_like(o_ref)
        o_ref[...] += lax.dot_general(_bf(a_ref[...]), _bf(b_ref[...]), (((0,), (0,)), ((), ())),
                                      preferred_element_type=F32)

    return pl.pallas_call(
        body, name=name, grid=(m_dim // tm, n_dim // tn, k_dim // tk),
        in_specs=[pl.BlockSpec((tk, tm), lambda m, n, k: (k, m)), b_spec],
        out_specs=(pl.BlockSpec((None, tm, tn), lambda m, n, k: (n, m, 0)) if out_chips
                   else pl.BlockSpec((tm, tn), lambda m, n, k: (m, n))),
        out_shape=jax.ShapeDtypeStruct((n_dim // tn, m_dim, tn) if out_chips else (m_dim, n_dim), F32),
        compiler_params=_cp(("parallel", "parallel", "arbitrary")))(a, b)


def _rstd(x):
    return lax.rsqrt(jnp.mean(x * x, axis=-1, keepdims=True) + EPS)


def _rms_bwd(x, dh, g, dres, name, with_bf16, scatter=()):
    s, d = x.shape
    tm = min(TM_ROWS, s)
    n_sc = len(scatter)
    n_out = 3 if with_bf16 else 2
    got_shapes, sems = _scatter_operands(scatter) if n_sc else ([], [])

    def body(x_ref, dh_ref, g_ref, dres_ref, *rest):
        dx_ref, gg_ref = rest[n_sc], rest[n_sc + n_out - 1]
        i = pl.program_id(0)
        if n_sc:
            finish = _hosted_scatter(rest[:n_sc], rest[n_sc + n_out:2 * n_sc + n_out], rest[2 * n_sc + n_out:], i,
                                     s // tm)

        @pl.when(i == 0)
        def _():
            gg_ref[...] = jnp.zeros_like(gg_ref)

        xv = x_ref[...]
        xn = xv * _rstd(xv)
        dhv = dh_ref[...]
        gg_ref[...] += jnp.sum(dhv * xn, axis=0, keepdims=True)
        t = dhv * g_ref[...]
        dx = dres_ref[...] + _rstd(xv) * (t - xn * jnp.mean(t * xn, axis=-1, keepdims=True))
        dx_ref[...] = dx
        if with_bf16:
            rest[n_sc + 1][...] = dx.astype(BF16)
        if n_sc:
            finish()

    row = pl.BlockSpec((tm, d), lambda i: (i, 0))
    vec = pl.BlockSpec((1, d), lambda i: (0, 0))
    out_specs = [row] + ([row] if with_bf16 else []) + [vec] + _hbm_specs(n_sc)
    out_shape = ([jax.ShapeDtypeStruct((s, d), F32)] + ([jax.ShapeDtypeStruct((s, d), BF16)] if with_bf16 else [])
                 + [jax.ShapeDtypeStruct((1, d), F32)] + got_shapes)
    outs = pl.pallas_call(
        body, name=name, grid=(s // tm,), in_specs=[row, row, vec, row] + _hbm_specs(n_sc), out_specs=out_specs,
        out_shape=out_shape, scratch_shapes=sems, compiler_params=_cp(("arbitrary",)))(x, dh, g, dres, *scatter)
    return tuple(outs[:n_out]) + ((outs[n_out:],) if n_sc else ())


def _down_proj_loss(act, w_down, x2, target, g):
    s, d = x2.shape
    tm = min(TM_ROWS, s)
    k_dim = act.shape[1]

    def body(a_ref, w_ref, x_ref, t_ref, g_ref, dx_ref, dxb_ref, loss_ref, gg_ref):
        @pl.when(pl.program_id(0) == 0)
        def _():
            gg_ref[...] = jnp.zeros_like(gg_ref)
            loss_ref[...] = jnp.zeros_like(loss_ref)

        xv = x_ref[...] + lax.dot_general(a_ref[...], w_ref[...], (((1,), (0,)), ((), ())),
                                          preferred_element_type=F32)
        r = _rstd(xv)
        xn = xv * r
        gv = g_ref[...]
        err = xn * gv - t_ref[...]
        loss_ref[...] += 0.5 * jnp.sum(jnp.mean(err * err, axis=-1, keepdims=True), axis=0, keepdims=True)
        dy = err * (1.0 / d)
        gg_ref[...] += jnp.sum(dy * xn, axis=0, keepdims=True)
        t = dy * gv
        dx = r * (t - xn * jnp.mean(t * xn, axis=-1, keepdims=True))
        dx_ref[...] = dx
        dxb_ref[...] = dx.astype(BF16)

    row = pl.BlockSpec((tm, d), lambda i: (i, 0))
    vec = pl.BlockSpec((1, d), lambda i: (0, 0))
    return pl.pallas_call(
        body, name="down_proj_loss", grid=(s // tm,),
        in_specs=[pl.BlockSpec((tm, k_dim), lambda i: (i, 0)), pl.BlockSpec((k_dim, d), lambda i: (0, 0)), row, row,
                  vec],
        out_specs=[row, row, pl.BlockSpec((1, LANES), lambda i: (0, 0)), vec],
        out_shape=[jax.ShapeDtypeStruct((s, d), F32), jax.ShapeDtypeStruct((s, d), BF16),
                   jax.ShapeDtypeStruct((1, LANES), F32), jax.ShapeDtypeStruct((1, d), F32)],
        compiler_params=_cp(("arbitrary",)))(act, w_down, x2, target, g)


def _prev_halo_spec(tm, width, col):
    return pl.BlockSpec((HALO, width), lambda i, *_: (jnp.maximum(i * (tm // HALO) - 1, 0), col))


def _next_halo_spec(tm, width, col, s):
    return pl.BlockSpec((HALO, width), lambda i, *_: (jnp.minimum((i + 1) * (tm // HALO), s // HALO - 1), col))


def _shift_down(x, k):
    return pltpu.roll(x, k, 0)


def _shift_up(x, k):
    return pltpu.roll(x, x.shape[0] - k, 0)


def _conv_taps(x_ext, w):
    return w[0:1, :] * _shift_down(x_ext, 2) + w[1:2, :] * _shift_down(x_ext, 1) + w[2:3, :] * x_ext


def _conv_taps_t(d_ext, w):
    return w[2:3, :] * d_ext + w[1:2, :] * _shift_up(d_ext, 1) + w[0:1, :] * _shift_up(d_ext, 2)


def _mixer_fwd(z_a, o_attn, w_conv, g_conv_out, g_attn_out):
    s = z_a.shape[0]
    c = CONV_CH
    tm = min(TM_ROWS, s)

    def body(gb_ref, gc_ref, xc_ref, gcp_ref, xcp_ref, o_ref, w_ref, gco_ref, gao_ref, mix_ref):
        i = pl.program_id(0)
        cx = gc_ref[...] * xc_ref[...]
        cx_prev = jnp.where(i > 0, gcp_ref[...] * xcp_ref[...], 0.0)
        conv = _conv_taps(jnp.concatenate([cx_prev, cx], axis=0), w_ref[...])[HALO:]
        y = gb_ref[...] * conv
        mix_ref[:, 0:c] = (y * _rstd(y) * gco_ref[...]).astype(BF16)
        o = o_ref[...]
        mix_ref[:, c:2 * c] = (o * _rstd(o) * gao_ref[...]).astype(BF16)

    col = lambda j: pl.BlockSpec((tm, c), lambda i: (i, j))
    vec = pl.BlockSpec((1, c), lambda i: (0, 0))
    return pl.pallas_call(
        body, name="mixer_fwd", grid=(s // tm,),
        in_specs=[col(0), col(1), col(2), _prev_halo_spec(tm, c, 1), _prev_halo_spec(tm, c, 2), col(0),
                  pl.BlockSpec((3, c), lambda i: (0, 0)), vec, vec],
        out_specs=pl.BlockSpec((tm, 2 * c), lambda i: (i, 0)),
        out_shape=jax.ShapeDtypeStruct((s, 2 * c), BF16),
        compiler_params=_cp(("parallel",)))(z_a, z_a, z_a, z_a, z_a, o_attn, w_conv, g_conv_out, g_attn_out)


def _mixer_bwd(z_a, o_attn, dmix, w_conv, g_conv_out, g_attn_out, scatter):
    s = z_a.shape[0]
    c = CONV_CH
    tm = min(TM_ROWS, s)
    n_blk = s // tm
    n_sc = len(scatter)
    got_shapes, sems = _scatter_operands(scatter)

    def body(gb_ref, gc_ref, xc_ref, gcp_ref, xcp_ref, gbn_ref, gcn_ref, xcn_ref, o_ref, dnc_ref, dncn_ref, dna_ref,
             w_ref, gco_ref, gao_ref, *rest):
        dz_ref, dox_ref, gw_ref, ggco_ref, ggao_ref = rest[n_sc:n_sc + 5]
        i = pl.program_id(0)
        finish = _hosted_scatter(rest[:n_sc], rest[n_sc + 5:2 * n_sc + 5], rest[2 * n_sc + 5:], i, n_blk)

        @pl.when(i == 0)
        def _():
            gw_ref[...] = jnp.zeros_like(gw_ref)
            ggco_ref[...] = jnp.zeros_like(ggco_ref)
            ggao_ref[...] = jnp.zeros_like(ggao_ref)

        w = w_ref[...]
        zeros = jnp.zeros((HALO, c), F32)
        gb_e = jnp.concatenate([zeros, gb_ref[...], gbn_ref[...]], axis=0)
        cx_prev = jnp.where(i > 0, gcp_ref[...] * xcp_ref[...], 0.0)
        gc_e = jnp.concatenate([zeros, gc_ref[...], gcn_ref[...]], axis=0)
        xc_e = jnp.concatenate([zeros, xc_ref[...], xcn_ref[...]], axis=0)
        cx_e = jnp.concatenate([cx_prev, gc_ref[...] * xc_ref[...], gcn_ref[...] * xcn_ref[...]], axis=0)
        dn_next = jnp.where(i < n_blk - 1, dncn_ref[...], 0.0)
        dn_e = jnp.concatenate([zeros, dnc_ref[...], dn_next], axis=0)

        cx_1, cx_2 = _shift_down(cx_e, 1), _shift_down(cx_e, 2)
        conv_e = w[0:1, :] * cx_2 + w[1:2, :] * cx_1 + w[2:3, :] * cx_e
        y_e = gb_e * conv_e
        r_e = _rstd(y_e)
        yn_e = y_e * r_e
        t_e = dn_e * gco_ref[...]
        dy_e = r_e * (t_e - yn_e * jnp.mean(t_e * yn_e, axis=-1, keepdims=True))
        dconv_e = dy_e * gb_e
        dcx_e = _conv_taps_t(dconv_e, w)
        blk = slice(HALO, HALO + tm)
        dz_ref[:, 0:c] = (dy_e * conv_e)[blk].astype(BF16)
        dz_ref[:, c:2 * c] = (dcx_e * xc_e)[blk].astype(BF16)
        dz_ref[:, 2 * c:3 * c] = (dcx_e * gc_e)[blk].astype(BF16)
        ggco_ref[...] += jnp.sum((dn_e * yn_e)[blk], axis=0, keepdims=True)
        dconv = dconv_e[blk]
        gw_ref[0:1, :] += jnp.sum(dconv * cx_2[blk], axis=0, keepdims=True)
        gw_ref[1:2, :] += jnp.sum(dconv * cx_1[blk], axis=0, keepdims=True)
        gw_ref[2:3, :] += jnp.sum(dconv * cx_e[blk], axis=0, keepdims=True)

        o = o_ref[...]
        ra = _rstd(o)
        on = o * ra
        dna = dna_ref[...]
        ggao_ref[...] += jnp.sum(dna * on, axis=0, keepdims=True)
        ta = dna * gao_ref[...]
        do = ra * (ta - on * jnp.mean(ta * on, axis=-1, keepdims=True))
        prod = do * o
        lane = lax.broadcasted_iota(jnp.int32, (tm, LANES), 1)
        head_a = lane < HEAD_DIM
        for p in range(N_PAIRS):
            cols = slice(p * LANES, (p + 1) * LANES)
            pb, dob = prod[:, cols], do[:, cols]
            for hh in range(2):
                sel = head_a if hh == 0 else jnp.logical_not(head_a)
                delta = jnp.sum(jnp.where(sel, pb, 0.0), axis=-1, keepdims=True)
                neg3 = _split3(-delta)
                do_h = pltpu.roll(dob, HEAD_DIM, 1) if hh else dob
                dox_ref[2 * p + hh] = _aug(do_h, lane, neg3).astype(BF16)
        finish()

    col = lambda j: pl.BlockSpec((tm, c), lambda i: (i, j))
    vec = pl.BlockSpec((1, c), lambda i: (0, 0))
    w3 = pl.BlockSpec((3, c), lambda i: (0, 0))
    outs = pl.pallas_call(
        body, name="mixer_bwd", grid=(n_blk,),
        in_specs=[col(0), col(1), col(2), _prev_halo_spec(tm, c, 1), _prev_halo_spec(tm, c, 2),
                  _next_halo_spec(tm, c, 0, s), _next_halo_spec(tm, c, 1, s), _next_halo_spec(tm, c, 2, s),
                  col(0), col(0), _next_halo_spec(tm, c, 0, s), col(1), w3, vec, vec] + _hbm_specs(n_sc),
        out_specs=[pl.BlockSpec((tm, 3 * c), lambda i: (i, 0)),
                   pl.BlockSpec((N_HEADS, tm, LANES), lambda i: (0, i, 0)), w3, vec, vec] + _hbm_specs(n_sc),
        out_shape=[jax.ShapeDtypeStruct((s, 3 * c), BF16), jax.ShapeDtypeStruct((N_HEADS, s, LANES), BF16),
                   jax.ShapeDtypeStruct((3, c), F32), jax.ShapeDtypeStruct((1, c), F32),
                   jax.ShapeDtypeStruct((1, c), F32)] + got_shapes,
        scratch_shapes=sems, compiler_params=_cp(("arbitrary",)))(
            z_a, z_a, z_a, z_a, z_a, z_a, z_a, z_a, o_attn, dmix, dmix, dmix, w_conv, g_conv_out, g_attn_out,
            *scatter)
    return tuple(outs[:5]) + (outs[5:],)


def _gate_fwd(f, b_pad):
    s = f.shape[0]
    tm = min(TQ, s)

    def body(f_ref, b_ref, fb_ref, carry):
        @pl.when(pl.program_id(0) == 0)
        def _():
            carry[...] = jnp.zeros_like(carry)

        z = f_ref[...] + b_ref[...]
        x = jnp.minimum(z, 0.0) - jnp.log1p(jnp.exp(-jnp.abs(z)))
        row = lax.broadcasted_iota(jnp.int32, (tm, LANES), 0)
        sh = 1
        while sh < tm:
            x = x + jnp.where(row >= sh, _shift_down(x, sh), 0.0)
            sh *= 2
        x = x + carry[0:1, :]
        carry[...] = jnp.broadcast_to(x[tm - 1:tm, :], carry.shape)
        head_a = lax.broadcasted_iota(jnp.int32, (tm, LANES), 1) < HEAD_DIM
        for p in range(N_PAIRS):
            fa = jnp.broadcast_to(x[:, 2 * p:2 * p + 1], (tm, LANES))
            fbv = jnp.broadcast_to(x[:, 2 * p + 1:2 * p + 2], (tm, LANES))
            fb_ref[:, p * LANES:(p + 1) * LANES] = jnp.where(head_a, fa, fbv)

    return pl.pallas_call(
        body, name="gate_fwd", grid=(s // tm,),
        in_specs=[pl.BlockSpec((tm, LANES), lambda i: (i, 0)), pl.BlockSpec((1, LANES), lambda i: (0, 0))],
        out_specs=pl.BlockSpec((tm, N_PAIRS * LANES), lambda i: (i, 0)),
        out_shape=jax.ShapeDtypeStruct((s, N_PAIRS * LANES), F32),
        scratch_shapes=[pltpu.VMEM((HALO, LANES), F32)],
        compiler_params=_cp(("arbitrary",)))(f, b_pad)


def _gate_bwd(f, b_pad, d_f):
    s = f.shape[0]
    tm = min(TQ, s)
    n_blk = s // tm

    def body(f_ref, b_ref, d_ref, df_ref, gb_ref, carry):
        @pl.when(pl.program_id(0) == 0)
        def _():
            carry[...] = jnp.zeros_like(carry)
            gb_ref[...] = jnp.zeros_like(gb_ref)

        x = d_ref[...]
        row = lax.broadcasted_iota(jnp.int32, (tm, LANES), 0)
        sh = 1
        while sh < tm:
            x = x + jnp.where(row < tm - sh, _shift_up(x, sh), 0.0)
            sh *= 2
        x = x + carry[0:1, :]
        carry[...] = jnp.broadcast_to(x[0:1, :], carry.shape)
        z = f_ref[...] + b_ref[...]
        d = x * (1.0 / (1.0 + jnp.exp(z)))
        df_ref[...] = d.astype(BF16)
        gb_ref[...] += jnp.sum(d, axis=0, keepdims=True)

    rev = pl.BlockSpec((tm, LANES), lambda i: (n_blk - 1 - i, 0))
    vec = pl.BlockSpec((1, LANES), lambda i: (0, 0))
    return pl.pallas_call(
        body, name="gate_bwd", grid=(n_blk,), in_specs=[rev, vec, rev], out_specs=[rev, vec],
        out_shape=[jax.ShapeDtypeStruct((s, LANES), BF16), jax.ShapeDtypeStruct((1, LANES), F32)],
        scratch_shapes=[pltpu.VMEM((HALO, LANES), F32)],
        compiler_params=_cp(("arbitrary",)))(f, b_pad, d_f)


_NT = (((1,), (1,)), ((), ()))
_NN = (((1,), (0,)), ((), ()))


AUG = HEAD_DIM
NORM_MARGIN = 1.01


def _split3(x):
    hi = x.astype(BF16).astype(F32)
    r = x - hi
    mid = r.astype(BF16).astype(F32)
    lo = (r - mid).astype(BF16).astype(F32)
    return hi, mid, lo


def _aug(base, lane, vals):
    out = jnp.where(lane < AUG, base, 0.0)
    for k, v in enumerate(vals):
        out = jnp.where(lane == AUG + k, v, out)
    return out


def _attn_prep(qkv, fb, bigs):
    s = qkv.shape[0]
    tq = min(TQ, s)
    n_q = s // tq

    n = len(bigs)
    arrays, landing, sems = _gather_operands(bigs, [])

    def body(q_ref, k_ref, v_ref, fb_ref, *rest):
        qx_ref, kx_ref, kxt_ref, vx_ref, vt_ref, b_ref = rest[2 * n:2 * n + 6]
        finish = _hosted_gather((rest[:n], rest[2 * n + 6:3 * n + 6]) + tuple(rest[3 * n + 6:]), n, n,
                                pl.program_id(0), n_q)
        lane = lax.broadcasted_iota(jnp.int32, (tq, LANES), 1)
        lane8 = lax.broadcasted_iota(jnp.int32, (HALO, LANES), 1)
        head_lanes = lane < AUG
        is_lane = [lane == AUG + k for k in range(6)]
        first3 = (lane >= AUG) & (lane < AUG + 3)
        next3 = (lane >= AUG + 3) & (lane < AUG + 6)
        q_const = jnp.where(first3, -1.0, 0.0)
        k_const = jnp.where(next3, 1.0, 0.0)
        v_const = jnp.where(first3, 1.0, 0.0)
        ones_head = (lax.broadcasted_iota(jnp.int32, (LANES, LANES), 0) < HEAD_DIM).astype(BF16)
        acc = jnp.zeros((HALO, LANES), F32)
        for p in range(N_PAIRS):
            cols = slice(p * LANES, (p + 1) * LANES)
            q2, k2, v2 = (ref[:, cols].astype(F32) for ref in (q_ref, k_ref, v_ref))
            q2 = q2 * Q_SCALE
            f2 = fb_ref[:, cols]
            for hh in range(2):
                h = 2 * p + hh
                q, k, v = ((pltpu.roll(x, HEAD_DIM, 1) if hh else x) for x in (q2, k2, v2))
                f = f2 if hh else pltpu.roll(f2, HEAD_DIM, 1)
                hi, mid, lo = _split3(f)
                q_aug = jnp.where(is_lane[3], hi, jnp.where(is_lane[4], mid, jnp.where(is_lane[5], lo, q_const)))
                k_aug = jnp.where(is_lane[0], hi, jnp.where(is_lane[1], mid, jnp.where(is_lane[2], lo, k_const)))
                kx = jnp.where(head_lanes, k, k_aug)
                vx = jnp.where(head_lanes, v, v_const)
                qx_ref[h] = jnp.where(head_lanes, q, q_aug).astype(BF16)
                kx_ref[h] = kx.astype(BF16)
                vx_ref[h] = vx.astype(BF16)
                kxt_ref[h, 0] = kx.T.astype(BF16)
                vt_ref[h, 0] = vx.T.astype(BF16)
                q_sq = lax.dot_general((q * q).astype(BF16), ones_head, _NN, preferred_element_type=F32)
                k_sq = lax.dot_general((k * k).astype(BF16), ones_head, _NN, preferred_element_type=F32)
                diag = lax.dot_general((q * k).astype(BF16), ones_head, _NN, preferred_element_type=F32)
                diag = diag - jnp.sqrt(q_sq * k_sq) * (NORM_MARGIN - 1.0)
                vals = (jnp.sqrt(jnp.max(q_sq, axis=0, keepdims=True)), jnp.sqrt(jnp.max(k_sq, axis=0, keepdims=True)),
                        jnp.max(f - diag, axis=0, keepdims=True), f[tq - 1:tq, :])
                for slot, val in enumerate(vals):
                    acc = jnp.where(lane8 == slot * N_HEADS + h, val[:, AUG:AUG + 1], acc)
        b_ref[0] = acc
        finish()

    blk = lambda j: pl.BlockSpec((tq, ATTN_W), lambda i: (i, j))
    rows = pl.BlockSpec((N_HEADS, tq, LANES), lambda i: (0, i, 0))
    cols_t = pl.BlockSpec((N_HEADS, 1, LANES, tq), lambda i: (0, i, 0, 0))
    shp = jax.ShapeDtypeStruct((N_HEADS, s, LANES), BF16)
    shp_t = jax.ShapeDtypeStruct((N_HEADS, n_q, LANES, tq), BF16)
    outs = pl.pallas_call(
        body, name="attn_prep", grid=(n_q,), in_specs=[blk(0), blk(1), blk(2), blk(0)] + _hbm_specs(2 * n),
        out_specs=[rows, rows, cols_t, rows, cols_t,
                   pl.BlockSpec((1, HALO, LANES), lambda i: (i, 0, 0))] + _hbm_specs(n),
        out_shape=[shp, shp, shp_t, shp, shp_t, jax.ShapeDtypeStruct((n_q, HALO, LANES), F32)]
        + [jax.ShapeDtypeStruct(b.shape, b.dtype) for b in landing],
        input_output_aliases={4 + n + k: 6 + k for k in range(n)}, scratch_shapes=sems,
        compiler_params=_cp(("arbitrary",)))(qkv, qkv, qkv, fb, *arrays, *landing)
    return tuple(outs[:6]) + (outs[6:],)


def _key_block_ranges(bounds):
    t = bounds[:, 0, :]
    nh = N_HEADS
    a, b, c, e = t[:, 0:nh], t[:, nh:2 * nh], t[:, 2 * nh:3 * nh], t[:, 3 * nh:4 * nh]
    bound = a[:, None, :] * b[None, :, :] * NORM_MARGIN + c[:, None, :] - e[None, :, :]
    n_q = t.shape[0]
    idx = jnp.arange(n_q)
    need = jnp.logical_not(bound < -(EXP_ZERO + 2.0)) | (idx[None, :, None] >= idx[:, None, None])
    first = jnp.argmax(need, axis=1).astype(jnp.int32)
    first = jnp.min(first.reshape(n_q, N_PAIRS, 2), axis=-1)
    visits = (first[:, None, :] <= idx[None, :, None]) & (idx[None, :, None] <= idx[:, None, None])
    last = jnp.max(jnp.where(visits, idx[:, None, None], 0), axis=0).astype(jnp.int32)
    return first.T.reshape(-1), last.T.reshape(-1)


def _attn_fwd_t(qx, kx, vt, first_blk):
    _, s, _ = qx.shape
    tq = min(TQ, s)
    n_q = s // tq
    neg = -1e30

    def body(first_ref, qx_ref, kx_ref, vt_ref, o_ref, lse_ref, acc_ref, m_ref, s_even, s_odd):
        p = pl.program_id(0)
        i = pl.program_id(1)
        acc_ref[...] = jnp.zeros(acc_ref.shape, F32)
        m_ref[...] = jnp.full(m_ref.shape, neg, F32)
        key_le_query = (lax.broadcasted_iota(jnp.int32, (tq, tq), 0) <= lax.broadcasted_iota(jnp.int32, (tq, tq), 1))
        first = first_ref[p * n_q + i]

        def scores(kb, hh, dst):
            rows_k = pl.ds(pl.multiple_of(kb * tq, tq), tq)
            dst[hh] = lax.dot_general(kx_ref[hh, rows_k, :], qx_ref[hh], _NT, preferred_element_type=F32)

        def step(kb, src, nxt):
            for hh in range(2):
                st = src[hh]
                if nxt is None:
                    st = jnp.where(key_le_query, st, -jnp.inf)
                else:
                    scores(kb + 1, hh, nxt)
                m_old = m_ref[hh]
                m_new = jnp.maximum(m_old, jnp.max(st, axis=0, keepdims=True))
                m_ref[hh] = m_new
                pt = jnp.exp(st - m_new).astype(BF16)
                acc_ref[hh] = acc_ref[hh] * jnp.exp(m_old - m_new) + lax.dot_general(
                    vt_ref[hh, kb], pt, _NN, preferred_element_type=F32)

        def by_parity(kb, fn):
            @pl.when(kb % 2 == 0)
            def _():
                fn(s_even, s_odd)

            @pl.when(kb % 2 == 1)
            def _():
                fn(s_odd, s_even)

        def first_scores(src, nxt):
            scores(first, 0, src)
            scores(first, 1, src)

        def unmasked(kb, carry):
            by_parity(kb, lambda src, nxt: step(kb, src, nxt))
            return carry

        by_parity(first, first_scores)
        lax.fori_loop(first, i, unmasked, 0)
        by_parity(i, lambda src, nxt: step(i, src, None))
        outs, lses = [], []
        for hh in range(2):
            acc = acc_ref[hh]
            l = acc[AUG:AUG + 1, :]
            outs.append(acc[0:HEAD_DIM, :] / l)
            lses.append(m_ref[hh] + jnp.log(l))
        o_ref[...] = jnp.concatenate(outs, axis=0).T
        rows8 = lax.broadcasted_iota(jnp.int32, (N_HEADS, tq), 0)
        lse_ref[0, 0] = jnp.where(rows8 == 0, lses[0], jnp.where(rows8 == 1, lses[1], 0.0))

    grid_spec = pltpu.PrefetchScalarGridSpec(
        num_scalar_prefetch=1, grid=(N_PAIRS, n_q),
        in_specs=[pl.BlockSpec((2, tq, LANES), lambda p, i, first: (p, i, 0)),
                  pl.BlockSpec((2, s, LANES), lambda p, i, first: (p, 0, 0)),
                  pl.BlockSpec((2, n_q, LANES, tq), lambda p, i, first: (p, 0, 0, 0))],
        out_specs=[pl.BlockSpec((tq, LANES), lambda p, i, first: (i, p)),
                   pl.BlockSpec((1, 1, N_HEADS, tq), lambda p, i, first: (p, i, 0, 0))],
        scratch_shapes=[pltpu.VMEM((2, LANES, tq), F32), pltpu.VMEM((2, 1, tq), F32),
                        pltpu.VMEM((2, tq, tq), F32), pltpu.VMEM((2, tq, tq), F32)])
    return pl.pallas_call(
        body, name="attn_fwd", grid_spec=grid_spec,
        out_shape=[jax.ShapeDtypeStruct((s, ATTN_W), F32), jax.ShapeDtypeStruct((N_PAIRS, n_q, N_HEADS, tq), F32)],
        compiler_params=_cp(("parallel", "arbitrary")))(first_blk, qx, kx, vt)


def _attn_bwd_t(qx, dox, kx, kxt, vx, lse, last_blk):
    _, s, _ = qx.shape
    tq = min(TQ, s)
    n_q = s // tq

    def body(last_ref, qx_ref, dox_ref, lse_ref, kx_ref, kxt_ref, vx_ref, dk_ref, dv_ref, dfk_ref, dqt_ref, dfq_ref):
        p = pl.program_id(0)
        j = pl.program_id(1)

        @pl.when(j == 0)
        def _():
            dqt_ref[...] = jnp.zeros(dqt_ref.shape, F32)
            dfq_ref[...] = jnp.zeros(dfq_ref.shape, F32)

        key_le_query = (lax.broadcasted_iota(jnp.int32, (tq, tq), 0) <= lax.broadcasted_iota(jnp.int32, (tq, tq), 1))

        def step(i, carry, masked):
            rows_q = pl.ds(pl.multiple_of(i * tq, tq), tq)
            out = []
            for hh in range(2):
                dk, dv, col = carry[3 * hh:3 * hh + 3]
                q, do = qx_ref[hh, rows_q, :], dox_ref[hh, rows_q, :]
                st = lax.dot_general(kx_ref[hh], q, _NT, preferred_element_type=F32)
                pt = jnp.exp(st - lse_ref[0, i, hh:hh + 1, :])
                if masked:
                    pt = jnp.where(key_le_query, pt, 0.0)
                dst = pt * lax.dot_general(vx_ref[hh], do, _NT, preferred_element_type=F32)
                pb, dsb = pt.astype(BF16), dst.astype(BF16)
                dv = dv + lax.dot_general(pb, do, _NN, preferred_element_type=F32)
                dk = dk + lax.dot_general(dsb, q, _NN, preferred_element_type=F32)
                dqt_ref[hh, i] += lax.dot_general(kxt_ref[hh, 0], dsb, _NN, preferred_element_type=F32)
                for cb in range(tq // LANES):
                    col = col + dst[:, cb * LANES:(cb + 1) * LANES]
                dfq_ref[hh, i] += jnp.sum(dst.reshape(tq // HALO, HALO, tq), axis=0)
                out += [dk, dv, col]
            return tuple(out)

        zero = jnp.zeros((tq, LANES), F32)
        carry = step(j, (zero,) * 6, True)
        dk_a, dv_a, col_a, dk_b, dv_b, col_b = lax.fori_loop(j + 1, last_ref[p * n_q + j] + 1,
                                                             lambda i, cr: step(i, cr, False), carry)
        head_a = lax.broadcasted_iota(jnp.int32, (tq, LANES), 1) < HEAD_DIM
        dk_ref[...] = jnp.where(head_a, dk_a, pltpu.roll(dk_b, HEAD_DIM, 1)).astype(BF16)
        dv_ref[...] = jnp.where(head_a, dv_a, pltpu.roll(dv_b, HEAD_DIM, 1)).astype(BF16)
        rows8 = lax.broadcasted_iota(jnp.int32, (N_HEADS, tq), 0)
        dfk_a, dfk_b = (-jnp.sum(c.T, axis=0, keepdims=True) for c in (col_a, col_b))
        dfk_ref[0, 0] = jnp.where(rows8 == 0, dfk_a, jnp.where(rows8 == 1, dfk_b, 0.0))

    resident = pl.BlockSpec((2, s, LANES), lambda p, j, last: (p, 0, 0))
    key_rows = pl.BlockSpec((2, tq, LANES), lambda p, j, last: (p, j, 0))
    pair_out = pl.BlockSpec((tq, LANES), lambda p, j, last: (j, p))
    grid_spec = pltpu.PrefetchScalarGridSpec(
        num_scalar_prefetch=1, grid=(N_PAIRS, n_q),
        in_specs=[resident, resident, pl.BlockSpec((1, n_q, N_HEADS, tq), lambda p, j, last: (p, 0, 0, 0)),
                  key_rows, pl.BlockSpec((2, 1, LANES, tq), lambda p, j, last: (p, j, 0, 0)), key_rows],
        out_specs=[pair_out, pair_out, pl.BlockSpec((1, 1, N_HEADS, tq), lambda p, j, last: (p, j, 0, 0)),
                   pl.BlockSpec((2, n_q, LANES, tq), lambda p, j, last: (p, 0, 0, 0)),
                   pl.BlockSpec((2, n_q, HALO, tq), lambda p, j, last: (p, 0, 0, 0))])
    return pl.pallas_call(
        body, name="attn_bwd", grid_spec=grid_spec,
        out_shape=[jax.ShapeDtypeStruct((s, ATTN_W), BF16), jax.ShapeDtypeStruct((s, ATTN_W), BF16),
                   jax.ShapeDtypeStruct((N_PAIRS, n_q, N_HEADS, tq), F32),
                   jax.ShapeDtypeStruct((N_HEADS, n_q, LANES, tq), F32),
                   jax.ShapeDtypeStruct((N_HEADS, n_q, HALO, tq), F32)],
        compiler_params=_cp(("parallel", "arbitrary")))(last_blk, qx, dox, lse, kx, kxt, vx)


def _attn_dq_finish(dqt):
    _, n_q, _, tq = dqt.shape
    per = 4 if n_q % 4 == 0 else 1

    def body(dqt_ref, dq_ref):
        for b in range(per):
            a, bb = dqt_ref[0, b], dqt_ref[1, b]
            dq_ref[b * tq:(b + 1) * tq, :] = (
                jnp.concatenate([a[0:HEAD_DIM], bb[0:HEAD_DIM]], axis=0).T * Q_SCALE).astype(BF16)

    return pl.pallas_call(
        body, name="attn_dq_finish", grid=(N_PAIRS, n_q // per),
        in_specs=[pl.BlockSpec((2, per, LANES, tq), lambda p, i: (p, i, 0, 0))],
        out_specs=pl.BlockSpec((per * tq, LANES), lambda p, i: (i, p)),
        out_shape=jax.ShapeDtypeStruct((n_q * tq, ATTN_W), BF16),
        compiler_params=_cp(("parallel", "parallel")))(dqt)


def _ffn_act_fwd(up, w_ffn):
    s = up.shape[0]
    tm, tn = min(TM_FFN, s), TN_FFN
    nb = D_FF // tn

    def body(a_ref, g_ref, ap_ref, gp_ref, wa_ref, wg_ref, act_ref, u_ref):
        i = pl.program_id(1)

        def conv(blk_ref, prev_ref, w_ref):
            prev = jnp.where(i > 0, prev_ref[...], 0.0)
            return _conv_taps(jnp.concatenate([prev, blk_ref[...]], axis=0), w_ref[...])[HALO:]

        u_a, u_g = conv(a_ref, ap_ref, wa_ref), conv(g_ref, gp_ref, wg_ref)
        u_ref[0], u_ref[1] = u_a, u_g
        act_ref[...] = (u_g * jax.nn.sigmoid(u_g) * u_a).astype(BF16)

    blk = lambda off: pl.BlockSpec((tm, tn), lambda n, i: (i, off + n))
    prev = lambda off: pl.BlockSpec((HALO, tn), lambda n, i: (jnp.maximum(i * (tm // HALO) - 1, 0), off + n))
    wsp = lambda off: pl.BlockSpec((3, tn), lambda n, i: (0, off + n))
    return pl.pallas_call(
        body, name="ffn_act_fwd", grid=(nb, s // tm),
        in_specs=[blk(0), blk(nb), prev(0), prev(nb), wsp(0), wsp(nb)],
        out_specs=[pl.BlockSpec((tm, tn), lambda n, i: (i, n)), pl.BlockSpec((2, tm, tn), lambda n, i: (0, i, n))],
        out_shape=[jax.ShapeDtypeStruct((s, D_FF), BF16), jax.ShapeDtypeStruct((2, s, D_FF), F32)],
        compiler_params=_cp(("parallel", "parallel")))(up, up, up, up, w_ffn, w_ffn)


def _ffn_act_bwd(up, u, dact, w_ffn):
    s = up.shape[0]
    tm, tn = min(TM_FFN, s), TN_FFN
    nb = D_FF // tn
    n_blk = s // tm

    def body(u_ref, un_ref, a_ref, g_ref, d_ref, dn_ref, wa_ref, wg_ref, dup_ref, gwa_ref, gwg_ref):
        i = pl.program_id(1)

        @pl.when(i == 0)
        def _():
            gwa_ref[...] = jnp.zeros_like(gwa_ref)
            gwg_ref[...] = jnp.zeros_like(gwg_ref)

        ext = lambda rows, next_rows: jnp.concatenate([rows, next_rows], axis=0)
        u_a, u_g = ext(u_ref[0], un_ref[0]), ext(u_ref[1], un_ref[1])
        d_e = ext(d_ref[...], jnp.where(i < n_blk - 1, dn_ref[...], 0.0))
        sig = jax.nn.sigmoid(u_g)
        du_a = d_e * (u_g * sig)
        du_g = d_e * u_a * (sig * (1.0 + u_g * (1.0 - sig)))
        halves = ((gwa_ref, wa_ref[...], a_ref[...], du_a), (gwg_ref, wg_ref[...], g_ref[...], du_g))
        for half, (gw_ref, w, upv, du) in enumerate(halves):
            du0, du1, du2 = du[:tm], _shift_up(du, 1)[:tm], _shift_up(du, 2)[:tm]
            dup_ref[half] = (w[2:3, :] * du0 + w[1:2, :] * du1 + w[0:1, :] * du2).astype(BF16)
            gw_ref[0:1, :] += jnp.sum(upv * du2, axis=0, keepdims=True)
            gw_ref[1:2, :] += jnp.sum(upv * du1, axis=0, keepdims=True)
            gw_ref[2:3, :] += jnp.sum(upv * du0, axis=0, keepdims=True)

    next_row = lambda i: jnp.minimum((i + 1) * (tm // HALO), s // HALO - 1)
    blk = lambda off: pl.BlockSpec((tm, tn), lambda n, i: (i, off + n))
    wsp = lambda off: pl.BlockSpec((3, tn), lambda n, i: (0, off + n))
    pair = lambda rows, row_of: pl.BlockSpec((2, rows, tn), lambda n, i: (0, row_of(i), n))
    return pl.pallas_call(
        body, name="ffn_act_bwd", grid=(nb, n_blk),
        in_specs=[pair(tm, lambda i: i), pair(HALO, next_row), blk(0), blk(nb), blk(0),
                  pl.BlockSpec((HALO, tn), lambda n, i: (next_row(i), n)), wsp(0), wsp(nb)],
        out_specs=[pair(tm, lambda i: i), wsp(0), wsp(0)],
        out_shape=[jax.ShapeDtypeStruct((2, s, D_FF), BF16),
                   jax.ShapeDtypeStruct((3, D_FF), F32), jax.ShapeDtypeStruct((3, D_FF), F32)],
        compiler_params=_cp(("parallel", "arbitrary")))(u, u, up, up, dact, dact, w_ffn, w_ffn)


def _adamw(w, g, m, v, name):
    r, c = w.shape
    tr = next((t for t in (512, 352, 256, 128, 64, 32, 16, 8) if r > t and r % t == 0), r)

    def body(w_ref, g_ref, m_ref, v_ref, d_ref, nm_ref, nv_ref):
        gv = g_ref[...]
        m_new = ADAM_B1 * m_ref[...] + (1.0 - ADAM_B1) * gv
        v_new = ADAM_B2 * v_ref[...] + (1.0 - ADAM_B2) * (gv * gv)
        m_hat = m_new / (1.0 - ADAM_B1 ** ADAM_STEP)
        v_hat = v_new / (1.0 - ADAM_B2 ** ADAM_STEP)
        d_ref[...] = -ADAM_LR * (m_hat / (jnp.sqrt(v_hat) + ADAM_EPS) + ADAM_WD * w_ref[...])
        nm_ref[...] = m_new
        nv_ref[...] = v_new

    spec = pl.BlockSpec((tr, c), lambda i: (i, 0))
    shp = jax.ShapeDtypeStruct((r, c), F32)
    return pl.pallas_call(
        body, name=name, grid=(r // tr,), in_specs=[spec] * 4, out_specs=[spec] * 3, out_shape=[shp] * 3,
        compiler_params=_cp(("parallel",)))(w, g, m, v)


def _sum_rows_block(h):
    return h if h <= 352 else 256


def _pair_sum(view, recv, sel, name):
    n, _, h, c = view.shape
    tr = _sum_rows_block(h)

    def body(sel_ref, a_ref, b_ref, o_ref, ob_ref):
        t = a_ref[...] + b_ref[...]
        o_ref[...] = t
        ob_ref[...] = t.astype(BF16)

    blk = pl.BlockSpec((None, tr, c), lambda j, i, sel_ref: (j, i, 0))
    grid_spec = pltpu.PrefetchScalarGridSpec(
        num_scalar_prefetch=1, grid=(n, h // tr),
        in_specs=[pl.BlockSpec((None, None, tr, c), lambda j, i, sel_ref: (j, sel_ref[0], i, 0)),
                  pl.BlockSpec((None, None, tr, c), lambda j, i, sel_ref: (j, 0, i, 0))],
        out_specs=[blk, blk])
    return pl.pallas_call(
        body, name=name, grid_spec=grid_spec,
        out_shape=[jax.ShapeDtypeStruct((n, h, c), F32), jax.ShapeDtypeStruct((n, h, c), BF16)],
        compiler_params=_cp(("parallel", "parallel")))(sel, view, recv)


def _chip_sum(pair, got, sel, name):
    _, h, c = pair.shape
    tr = _sum_rows_block(h)
    nblk = h // tr

    def body(sel_ref, p_ref, g0_ref, g1_ref, g2_ref, o_ref):
        o_ref[...] = ((p_ref[...] + g0_ref[...].astype(F32)) + g1_ref[...].astype(F32)) + g2_ref[...].astype(F32)

    slot = lambda k: pl.BlockSpec((None, tr, c), lambda i, sel_ref: (k, i, 0))
    grid_spec = pltpu.PrefetchScalarGridSpec(
        num_scalar_prefetch=1, grid=(h // tr,),
        in_specs=[pl.BlockSpec((None, tr, c), lambda i, sel_ref: (sel_ref[1], i, 0)), slot(0), slot(1), slot(2)],
        out_specs=pl.BlockSpec((tr, c), lambda i, sel_ref: (sel_ref[0] * nblk + i, 0)))
    return pl.pallas_call(
        body, name=name, grid_spec=grid_spec, out_shape=jax.ShapeDtypeStruct((2 * h, c), F32),
        compiler_params=_cp(("parallel",)))(sel, pair, got, got, got)


def _place():
    return lax.axis_index("x"), lax.axis_index("y"), lax.axis_index("c")


def _other_chips(x, y):
    return [(1 - x, y), (x, 1 - y), (1 - x, 1 - y)]


def _hbm_specs(n):
    return [pl.BlockSpec(memory_space=pl.ANY)] * n


def _all_gather_weights(bigs, smalls):
    nb, ns = len(bigs), len(smalls)
    n = nb + ns

    def body(*refs):
        start, forward, finish = _gather_phases(refs[:n], refs[2 * n:3 * n], nb, *refs[3 * n:])
        start()
        forward()
        finish()

    arrays, landing, sems = _gather_operands(bigs, smalls)
    return pl.pallas_call(
        body, name="all_gather_weights",
        out_shape=[jax.ShapeDtypeStruct(b.shape, b.dtype) for b in landing],
        in_specs=_hbm_specs(2 * n), out_specs=_hbm_specs(n), input_output_aliases={n + k: k for k in range(n)},
        scratch_shapes=sems)(*arrays, *landing)


def _hosted_gather(refs, n, nb, step, total):
    ins, outs, send_sems, recv_sems = refs
    start, forward, finish = _gather_phases(ins, outs, nb, send_sems, recv_sems)
    pl.when(step == 0)(start)
    pl.when(step == (3 * total) // 4)(forward)
    return lambda: pl.when(step == total - 1)(finish)


def _in_proj(x, g, w_a, w_b, w_c, bigs, smalls):
    s, d = x.shape
    tm, tn = min(TM_MM, s), ATTN_W
    na, nq = w_a.shape[1] // tn, w_b.shape[1] // tn
    steps = na + nq + 1
    total = (s // tm) * steps
    nb, n = len(bigs), len(bigs) + len(smalls)
    arrays, landing, sems = _gather_operands(bigs, smalls)

    def body(x_ref, g_ref, wa_ref, wb_ref, wc_ref, *rest):
        z_ref, qkv_ref, f_ref, h_ref = rest[2 * n:2 * n + 4]
        h_scr = rest[-1]
        m, j = pl.program_id(0), pl.program_id(1)
        finish = _hosted_gather((rest[:n], rest[2 * n + 4:3 * n + 4]) + tuple(rest[3 * n + 4:3 * n + 6]), n, nb,
                                m * steps + j, total)

        @pl.when(j == 0)
        def _():
            xv = x_ref[...]
            hv = (xv * _rstd(xv) * g_ref[...]).astype(BF16)
            h_scr[...] = hv
            h_ref[...] = hv

        h = h_scr[...]

        @pl.when(j < na)
        def _():
            z_ref[...] = lax.dot_general(h, wa_ref[...], _NN, preferred_element_type=F32)

        @pl.when((j >= na) & (j < na + nq))
        def _():
            qkv_ref[...] = lax.dot_general(h, wb_ref[...], _NN, preferred_element_type=F32).astype(BF16)

        @pl.when(j == na + nq)
        def _():
            f_ref[...] = lax.dot_general(h, wc_ref[...], _NN, preferred_element_type=F32)

        finish()

    blk_a = lambda m, j: (m, jnp.minimum(j, na - 1))
    blk_b = lambda m, j: (m, jnp.clip(j - na, 0, nq - 1))
    outs = pl.pallas_call(
        body, name="in_proj", grid=(s // tm, steps),
        in_specs=[pl.BlockSpec((tm, d), lambda m, j: (m, 0)), pl.BlockSpec((1, d), lambda m, j: (0, 0)),
                  pl.BlockSpec((d, tn), lambda m, j: (0, jnp.minimum(j, na - 1))),
                  pl.BlockSpec((d, tn), lambda m, j: (0, jnp.clip(j - na, 0, nq - 1))),
                  pl.BlockSpec((d, LANES), lambda m, j: (0, 0))] + _hbm_specs(2 * n),
        out_specs=[pl.BlockSpec((tm, tn), blk_a), pl.BlockSpec((tm, tn), blk_b),
                   pl.BlockSpec((tm, LANES), lambda m, j: (m, 0)), pl.BlockSpec((tm, d), lambda m, j: (m, 0))]
        + _hbm_specs(n),
        out_shape=[jax.ShapeDtypeStruct((s, w_a.shape[1]), F32), jax.ShapeDtypeStruct((s, w_b.shape[1]), BF16),
                   jax.ShapeDtypeStruct((s, LANES), F32), jax.ShapeDtypeStruct((s, d), BF16)]
        + [jax.ShapeDtypeStruct(b.shape, b.dtype) for b in landing],
        input_output_aliases={5 + n + k: 4 + k for k in range(n)},
        scratch_shapes=sems + [pltpu.VMEM((tm, d), BF16)],
        compiler_params=_cp(("arbitrary", "arbitrary")))(x, g, w_a, w_b, w_c, *arrays, *landing)
    return outs[0], outs[1], outs[2], outs[3], outs[4:]


def _gather_operands(bigs, smalls):
    x, y, _ = _place()
    arrays = list(bigs) + list(smalls)
    landing = [lax.dynamic_update_index_in_dim(lax.empty((N_CHIPS,) + a.shape, a.dtype), a, 2 * x + y, 0)
               for a in arrays]
    n_sems = 6 * len(bigs) + 3 * len(smalls)
    return arrays, landing, [pltpu.SemaphoreType.DMA((n_sems,)), pltpu.SemaphoreType.DMA((n_sems,))]


def _gather_phases(ins, outs, nb, send_sems, recv_sems):
    n = len(ins)
    x, y, c = _place()
    my_chip = 2 * x + y
    chips = _other_chips(x, y)
    sibling = (x, y, 1 - c)

    def rows(k, which):
        h = ins[k].shape[0] // 2
        return pl.ds(which * h, h)

    def copy(sem, src, dst, to):
        return pltpu.make_async_remote_copy(src_ref=src, dst_ref=dst, send_sem=send_sems.at[sem],
                                            recv_sem=recv_sems.at[sem], device_id=to, device_id_type=MESH)

    def sends():
        out = [copy(6 * k + j, ins[k].at[rows(k, c)], outs[k].at[my_chip, rows(k, c)], (cx, cy, c))
               for k in range(nb) for j, (cx, cy) in enumerate(chips)]
        return out + [copy(6 * nb + 3 * (k - nb) + j, ins[k], outs[k].at[my_chip], (cx, cy, c))
                      for k in range(nb, n) for j, (cx, cy) in enumerate(chips)]

    def landed(k, j, which):
        cx, cy = chips[j]
        return outs[k].at[2 * cx + cy, rows(k, which)]

    def forwards():
        return [copy(6 * k + 3 + j, landed(k, j, c), landed(k, j, c), sibling)
                for j in range(3) for k in range(nb)]

    def start():
        for cp in sends():
            cp.start()

    def forward():
        for j in range(3):
            for k in range(nb):
                copy(6 * k + j, landed(k, j, c), landed(k, j, c), (x, y, c)).wait_recv()
                copy(6 * k + 3 + j, landed(k, j, c), landed(k, j, c), sibling).start()

    def finish():
        for j, (cx, cy) in enumerate(chips):
            for k in range(nb):
                copy(6 * k + 3 + j, landed(k, j, 1 - c), landed(k, j, 1 - c), (x, y, c)).wait_recv()
            for k in range(nb, n):
                arrived = outs[k].at[2 * cx + cy]
                copy(6 * nb + 3 * (k - nb) + j, arrived, arrived, (x, y, c)).wait_recv()
        for cp in sends() + forwards():
            cp.wait_send()

    return start, forward, finish


def _pair_exchange(views, name):
    n = len(views)

    def body(*refs):
        ins, outs, send_sems, recv_sems = refs[:n], refs[n:2 * n], refs[2 * n], refs[2 * n + 1]
        x, y, c = _place()
        copies = [pltpu.make_async_remote_copy(
            src_ref=ins[k].at[:, pl.ds(1 - c, 1)], dst_ref=outs[k], send_sem=send_sems.at[k],
            recv_sem=recv_sems.at[k], device_id=(x, y, 1 - c), device_id_type=MESH) for k in range(n)]
        for cp in copies:
            cp.start()
        for cp in copies:
            cp.wait()

    return pl.pallas_call(
        body, name=name,
        out_shape=[jax.ShapeDtypeStruct((v.shape[0], 1) + v.shape[2:], v.dtype) for v in views],
        in_specs=_hbm_specs(n), out_specs=_hbm_specs(n),
        scratch_shapes=[pltpu.SemaphoreType.DMA((n,)), pltpu.SemaphoreType.DMA((n,))])(*views)


def _scatter_operands(parts):
    n = len(parts)
    return ([jax.ShapeDtypeStruct((3,) + p.shape[1:], p.dtype) for p in parts],
            [pltpu.SemaphoreType.DMA((3 * n,)), pltpu.SemaphoreType.DMA((3 * n,))])


def _scatter_phases(ins, outs, send_sems, recv_sems):
    x, y, c = _place()

    def copies():
        return [pltpu.make_async_remote_copy(
            src_ref=ins[k].at[pl.ds(2 * cx + cy, 1)], dst_ref=outs[k].at[pl.ds(r, 1)], send_sem=send_sems.at[3 * k + r],
            recv_sem=recv_sems.at[3 * k + r], device_id=(cx, cy, c), device_id_type=MESH)
            for k in range(len(ins)) for r, (cx, cy) in enumerate(_other_chips(x, y))]

    def start():
        for cp in copies():
            cp.start()

    def finish():
        for cp in copies():
            cp.wait()

    return start, finish


def _hosted_scatter(ins, outs, sems, step, total):
    start, finish = _scatter_phases(ins, outs, *sems)
    pl.when(step == 0)(start)
    return lambda: pl.when(step == total - 1)(finish)


def _join_halves(shards):
    n = len(shards)

    def body(*refs):
        ins, outs, send_sems, recv_sems = refs[:n], refs[n:2 * n], refs[2 * n], refs[2 * n + 1]
        x, y, c = _place()

        def rows(ref, which):
            h = ref.shape[0] // 2
            return ref.at[pl.ds(which * h, h)]

        sent = [pltpu.make_async_remote_copy(
            src_ref=rows(ins[k], c), dst_ref=rows(outs[k], c), send_sem=send_sems.at[k], recv_sem=recv_sems.at[k],
            device_id=(x, y, 1 - c), device_id_type=MESH) for k in range(n)]
        for cp in sent:
            cp.start()
        for k in range(n):
            pltpu.make_async_remote_copy(
                src_ref=rows(ins[k], 1 - c), dst_ref=rows(outs[k], 1 - c), send_sem=send_sems.at[k],
                recv_sem=recv_sems.at[k], device_id=(x, y, 1 - c), device_id_type=MESH).wait_recv()
        for cp in sent:
            cp.wait_send()

    return pl.pallas_call(
        body, name="half_exchange", out_shape=[jax.ShapeDtypeStruct(a.shape, a.dtype) for a in shards],
        in_specs=_hbm_specs(n), out_specs=_hbm_specs(n), input_output_aliases={k: k for k in range(n)},
        scratch_shapes=[pltpu.SemaphoreType.DMA((n,)), pltpu.SemaphoreType.DMA((n,))])(*shards)


def _all_reduce_small(packet):
    rows, width = packet.shape
    n_dev = 8

    def body(x_ref, out_ref, gath, send_sems, recv_sems):
        x, y, c = _place()
        me, sibling = (x, y, c), (x, y, 1 - c)
        chips = _other_chips(x, y)

        def slot(px, py, pc):
            return gath.at[pl.ds((4 * px + 2 * py + pc) * rows, rows), :]

        def copy(k, block, to, src=None):
            return pltpu.make_async_remote_copy(
                src_ref=slot(*block) if src is None else src, dst_ref=slot(*block), send_sem=send_sems.at[k],
                recv_sem=recv_sems.at[k], device_id=to, device_id_type=MESH)

        first = [copy(0, me, sibling, src=x_ref)]
        first += [copy(1 + j, me, (*chip, c), src=x_ref) for j, chip in enumerate(chips)]
        for cp in first:
            cp.start()
        gath[pl.ds((4 * x + 2 * y + c) * rows, rows), :] = x_ref[...]
        passed = [copy(4 + j, (*chip, c), sibling) for j, chip in enumerate(chips)]
        for j, chip in enumerate(chips):
            copy(1 + j, (*chip, c), me).wait_recv()
            passed[j].start()
        copy(0, sibling, me).wait_recv()
        for j, chip in enumerate(chips):
            copy(4 + j, (*chip, 1 - c), me).wait_recv()
        for cp in first + passed:
            cp.wait_send()
        acc = gath[0:rows, :]
        for d in range(1, n_dev):
            acc = acc + gath[d * rows:(d + 1) * rows, :]
        out_ref[...] = acc

    return pl.pallas_call(
        body, name="all_reduce_small", out_shape=jax.ShapeDtypeStruct((rows, width), F32),
        in_specs=[pl.BlockSpec(memory_space=pltpu.VMEM)], out_specs=pl.BlockSpec(memory_space=pltpu.VMEM),
        scratch_shapes=[pltpu.VMEM((n_dev * rows, width), F32), pltpu.SemaphoreType.DMA((7,)),
                        pltpu.SemaphoreType.DMA((7,))])(packet)


def _flat_rows(parts, width, row_multiple):
    flat = jnp.concatenate([p.astype(F32).reshape(-1) for p in parts])
    rows = -(-flat.shape[0] // width)
    rows = -(-rows // row_multiple) * row_multiple
    return jnp.pad(flat, (0, rows * width - flat.shape[0])).reshape(rows, width)


def _unflatten(flat2d, shapes):
    flat = flat2d.reshape(-1)
    out, off = [], 0
    for shp in shapes:
        n = 1
        for dim in shp:
            n *= dim
        out.append(flat[off:off + n].reshape(shp))
        off += n
    return out


def _core_and_chip():
    x, y, c = _place()
    return jnp.stack([c, 2 * x + y]).astype(jnp.int32)


def _pair_sums(chip_major, names, call_name):
    views = [g.reshape(N_CHIPS, 2, g.shape[1] // 2, g.shape[2]) for g in chip_major]
    recv = _pair_exchange(views, call_name)
    sel = _core_and_chip()
    return [_pair_sum(v, r, sel, "pair_sum_" + nm) for v, r, nm in zip(views, recv, names)]


def _finish_grads(pairs, got, names):
    sel = _core_and_chip()
    return _join_halves([_chip_sum(p, g, sel, "chip_sum_" + nm) for (p, _), g, nm in zip(pairs, got, names)])


def kernel(x, g_mix, w_in, b_f, w_conv, g_conv_out, g_attn_out, w_o, g_ffn, w_up, w_ffn_conv, w_down, g_final, loss_target, m_g_mix, m_w_in, m_b_f, m_w_conv, m_g_conv_out, m_g_attn_out, m_w_o, m_g_ffn, m_w_up, m_w_ffn_conv, m_w_down, m_g_final, v_g_mix, v_w_in, v_b_f, v_w_conv, v_g_conv_out, v_g_attn_out, v_w_o, v_g_ffn, v_w_up, v_w_ffn_conv, v_w_down, v_g_final):
    s = x.shape[1]
    x0 = x[0]
    target = loss_target[0]
    d = D_MODEL
    x_pos, y_pos, _ = _place()
    my_chip = 2 * x_pos + y_pos

    (c_in,) = _all_gather_weights([w_in[0].astype(BF16)], [])
    w_in_full = jnp.concatenate([c_in[j] for j in range(N_CHIPS)], axis=1)
    c3 = 3 * CONV_CH
    w_a, w_b = w_in_full[:, :c3], w_in_full[:, c3:c3 + 3 * ATTN_W]
    w_c = jnp.pad(w_in_full[:, c3 + 3 * ATTN_W:], ((0, 0), (0, LANES - N_HEADS)))
    w_q, w_k, w_v = (w_b[:, i * ATTN_W:(i + 1) * ATTN_W] for i in range(3))
    b_pad = jnp.pad(b_f, ((0, 0), (0, LANES - N_HEADS)))

    z_a, qkv, f_log, h1, (c_o, c_up, c_conv, c_ffn) = _in_proj(
        x0, g_mix, w_a, w_b, w_c, [w_o[0].astype(BF16), w_up[0].astype(BF16)], [w_conv[0], w_ffn_conv[0]])
    fb = _gate_fwd(f_log, b_pad)
    qx, kx, kxt, vx, vt, bounds, (c_down,) = _attn_prep(qkv, fb, [w_down[0].astype(BF16)])
    w_o_full = c_o.reshape(d, d)
    w_down_full = c_down.reshape(D_FF, d)
    w_conv_full = jnp.concatenate([c_conv[j] for j in range(N_CHIPS)], axis=1)
    w_ffn_full = jnp.concatenate([c_ffn[j] for j in range(N_CHIPS)], axis=1)
    n_up = c_up.shape[2]
    first_blk, last_blk = _key_block_ranges(bounds)
    o_attn, lse = _attn_fwd_t(qx, kx, vt, first_blk)
    mix = _mixer_fwd(z_a, o_attn, w_conv_full, g_conv_out, g_attn_out)
    x2, h2 = _mm("nn", [mix], [w_o_full], F32, 512, d, "out_proj", add=x0, norm_g=g_ffn)
    up = _mm("nn", [h2], [c_up], F32, TM_MM, n_up, "up_proj", b_chips=True)
    act, u_conv = _ffn_act_fwd(up, w_ffn_full)
    dx3, dx3_b, loss_row, gg_final = _down_proj_loss(act, w_down_full, x2, target, g_final.reshape(1, d))

    dact = _mm("nt", [dx3_b], [w_down_full], F32, TM_MM, 1408, "d_act")
    gw_down = _mm_tn(act, dx3_b, 1408, 1024, "gw_down")
    dup, gwf_lin, gwf_gate = _ffn_act_bwd(up, u_conv, dact, w_ffn_full)
    dh2 = _mm("nt", [(dup, j // 2, j % 2, n_up) for j in range(N_CHIPS)], [(c_up, j) for j in range(N_CHIPS)],
              F32, TM_MM, 512, "d_h2")
    gw_up = _mm_tn(h2, dup, 1024, n_up, "gw_up", out_chips=True)
    dx2, dx2_b, gg_ffn = _rms_bwd(x2, dh2, g_ffn, dx3, "rms_ffn_bwd", True)
    dmix = _mm("nt", [dx2_b], [w_o_full], F32, TM_MM, 512, "d_mix")
    gw_o = _mm_tn(mix, dx2_b, 1024, 1024, "gw_o")
    early = _pair_sums([gw_o.reshape(N_CHIPS, d // N_CHIPS, d), gw_up, gw_down.reshape(N_CHIPS, D_FF // N_CHIPS, d)],
                       ["w_o", "w_up", "w_down"], "pair_exchange")
    dz_a, dox, gw_conv, gg_conv_out, gg_attn_out, (got_o, got_down) = _mixer_bwd(
        z_a, o_attn, dmix, w_conv_full, g_conv_out, g_attn_out, [early[0][1], early[2][1]])
    dk, dv, dfk, dqt, dfq = _attn_bwd_t(qx, dox, kx, kxt, vx, lse, last_blk)
    dq = _attn_dq_finish(dqt)
    d_f = (jnp.transpose(dfk[:, :, 0:2, :], (1, 3, 0, 2)).reshape(s, N_HEADS)
           + jnp.transpose(jnp.sum(dfq, axis=2), (1, 2, 0)).reshape(s, N_HEADS))
    df_b, gb_f = _gate_bwd(f_log, b_pad, jnp.pad(d_f, ((0, 0), (0, LANES - N_HEADS))))
    gw_a = _mm_tn(h1, dz_a, 1024, c3, "gw_in_conv")
    gw_q = _mm_tn(h1, dq, 1024, ATTN_W, "gw_in_q")
    gw_k = _mm_tn(h1, dk, 1024, ATTN_W, "gw_in_k")
    gw_v = _mm_tn(h1, dv, 1024, ATTN_W, "gw_in_v")
    gw_c = _mm_tn(h1, df_b, 1024, LANES, "gw_in_gate")
    gw_in = jnp.concatenate([gw_a, gw_q, gw_k, gw_v, gw_c[:, :N_HEADS]], axis=1)
    n_in = IN_COLS // N_CHIPS
    gw_in = jnp.stack([gw_in[:, j * n_in:(j + 1) * n_in] for j in range(N_CHIPS)])
    late = _pair_sums([gw_in], ["w_in"], "pair_exchange_w_in")
    dh1, (got_up, got_in) = _mm("nt", [dz_a, dq, dk, dv, df_b], [w_a, w_q, w_k, w_v, w_c], F32, TM_MM, 512, "d_h1",
                                scatter=[early[1][1], late[0][1]])
    grad_x, gg_mix = _rms_bwd(x0, dh1, g_mix, dx2, "rms_mix_bwd", False)
    g_w_in, g_w_o, g_w_up, g_w_down = _finish_grads(late + early, [got_in, got_o, got_up, got_down],
                                                    ["w_in", "w_o", "w_up", "w_down"])

    gw_ffn = jnp.concatenate([gwf_lin, gwf_gate], axis=1)
    small_parts = [gg_mix, gg_conv_out, gg_attn_out, gg_ffn, gg_final, gb_f[:, :N_HEADS], loss_row[:, 0:1], gw_conv,
                   gw_ffn]
    small_shapes = [a.shape for a in small_parts]
    tot = _unflatten(_all_reduce_small(_flat_rows(small_parts, d, 8)), small_shapes)
    g_g_mix, g_g_conv_out, g_g_attn_out, g_g_ffn, g_g_final, g_b_f, loss_sum, g_conv_full, g_ffn_full = tot
    loss = loss_sum[0, 0]
    g_g_final = g_g_final[0]
    g_w_conv = lax.dynamic_slice_in_dim(g_conv_full, my_chip * (CONV_CH // N_CHIPS), CONV_CH // N_CHIPS, axis=1)
    g_w_ffn = lax.dynamic_slice_in_dim(g_ffn_full, my_chip * n_up, n_up, axis=1)

    def adam_big(w, g, m, v, name):
        dl, nm, nv = _adamw(w[0], g, m[0], v[0], name)
        return dl[None], nm[None], nv[None]

    u_w_in = adam_big(w_in, g_w_in, m_w_in, v_w_in, "adam_w_in")
    u_w_o = adam_big(w_o, g_w_o, m_w_o, v_w_o, "adam_w_o")
    u_w_up = adam_big(w_up, g_w_up, m_w_up, v_w_up, "adam_w_up")
    u_w_down = adam_big(w_down, g_w_down, m_w_down, v_w_down, "adam_w_down")

    small_w = [g_mix, b_f, g_conv_out, g_attn_out, g_ffn, g_final, w_conv, w_ffn_conv]
    small_g = [g_g_mix, g_b_f, g_g_conv_out, g_g_attn_out, g_g_ffn, g_g_final, g_w_conv, g_w_ffn]
    small_m = [m_g_mix, m_b_f, m_g_conv_out, m_g_attn_out, m_g_ffn, m_g_final, m_w_conv, m_w_ffn_conv]
    small_v = [v_g_mix, v_b_f, v_g_conv_out, v_g_attn_out, v_g_ffn, v_g_final, v_w_conv, v_w_ffn_conv]
    shapes = [a.shape for a in small_w]
    pack = lambda arrs: _flat_rows(arrs, LANES, 8)
    sd, sm, sv = _adamw(pack(small_w), pack(small_g), pack(small_m), pack(small_v), "adam_small")
    sd, sm, sv = _unflatten(sd, shapes), _unflatten(sm, shapes), _unflatten(sv, shapes)
    (d_g_mix, d_b_f, d_g_conv_out, d_g_attn_out, d_g_ffn, d_g_final, d_w_conv, d_w_ffn) = sd
    (nm_g_mix, nm_b_f, nm_g_conv_out, nm_g_attn_out, nm_g_ffn, nm_g_final, nm_w_conv, nm_w_ffn) = sm
    (nv_g_mix, nv_b_f, nv_g_conv_out, nv_g_attn_out, nv_g_ffn, nv_g_final, nv_w_conv, nv_w_ffn) = sv

    grads = (g_g_mix, g_w_in[None], g_b_f, g_w_conv[None], g_g_conv_out, g_g_attn_out, g_w_o[None], g_g_ffn,
             g_w_up[None], g_w_ffn[None], g_w_down[None], g_g_final)
    deltas = (d_g_mix, u_w_in[0], d_b_f, d_w_conv, d_g_conv_out, d_g_attn_out, u_w_o[0], d_g_ffn, u_w_up[0],
              d_w_ffn, u_w_down[0], d_g_final)
    new_m = (nm_g_mix, u_w_in[1], nm_b_f, nm_w_conv, nm_g_conv_out, nm_g_attn_out, u_w_o[1], nm_g_ffn, u_w_up[1],
             nm_w_ffn, u_w_down[1], nm_g_final)
    new_v = (nv_g_mix, u_w_in[2], nv_b_f, nv_w_conv, nv_g_conv_out, nv_g_attn_out, u_w_o[2], nv_g_ffn, u_w_up[2],
             nv_w_ffn, u_w_down[2], nv_g_final)
    return (loss, grad_x[None], *grads, *deltas, *new_m, *new_v)
```

```python
import jax
import jax.numpy as jnp
from jax import lax
from jax.experimental import pallas as pl
from jax.experimental.pallas import tpu as pltpu

F32, BF16 = jnp.float32, jnp.bfloat16
MESH = pl.DeviceIdType.MESH

D_MODEL = 1024
CONV_CH = 512
ATTN_W = 512
N_HEADS = 8
HEAD_DIM = 64
N_PAIRS = N_HEADS // 2
D_FF = 2816
IN_COLS = 3 * CONV_CH + 3 * ATTN_W + N_HEADS
EPS = 1e-6
Q_SCALE = 0.125
EXP_ZERO = 88.0
N_CHIPS = 4
LANES = 128
HALO = 8
HALO_BF16 = 2 * HALO

ADAM_LR, ADAM_B1, ADAM_B2, ADAM_EPS, ADAM_WD, ADAM_STEP = 0.001, 0.9, 0.999, 1e-08, 0.01, 10

TM_ROWS = 512
TM_MM = 1024
TK_TN = 1024
TQ = 512
TM_FFN = 1024
TN_FFN = 256
VMEM_LIMIT = 52 * 2**20


def _cp(sem, vmem=VMEM_LIMIT):
    return pltpu.CompilerParams(dimension_semantics=sem, vmem_limit_bytes=vmem)


def _bf(a):
    return a if a.dtype == BF16 else a.astype(BF16)


def _mm(mode, a_list, b_list, out_dtype, tm, tn, name, add=None, b_chips=False, scatter=(), norm_g=None):
    n_p = len(a_list)
    a0 = a_list[0]
    m_dim = a0[0].shape[1] if isinstance(a0, tuple) else a0.shape[0]
    b0 = b_list[0]
    if b_chips:
        n_dim = b0.shape[0] * b0.shape[2]
        assert tn == b0.shape[2] and mode == "nn"
    else:
        b0 = b0[0][b0[1]] if isinstance(b0, tuple) else b0
        n_dim = b0.shape[1 if mode == "nn" else 0]
    tm, tn = min(tm, m_dim), min(tn, n_dim)
    assert m_dim % tm == 0 and n_dim % tn == 0
    dims = (((1,), (0,)), ((), ())) if mode == "nn" else (((1,), (1,)), ((), ()))
    in_specs, args = [], []
    for a in a_list:
        if isinstance(a, tuple):
            arr, lead, col, width = a
            in_specs.append(pl.BlockSpec((None, tm, width), lambda m, n, lead=lead, col=col: (lead, m, col)))
        else:
            arr = a
            in_specs.append(pl.BlockSpec((tm, a.shape[1]), lambda m, n: (m, 0)))
        args.append(arr)
    for b in b_list:
        if b_chips:
            arr = b
            in_specs.append(pl.BlockSpec((None, b.shape[1], tn), lambda m, n: (n, 0, 0)))
        elif isinstance(b, tuple):
            arr, lead = b
            if mode == "nn":
                in_specs.append(pl.BlockSpec((None, arr.shape[1], tn), lambda m, n, lead=lead: (lead, 0, n)))
            else:
                in_specs.append(pl.BlockSpec((None, tn, arr.shape[2]), lambda m, n, lead=lead: (lead, n, 0)))
        elif mode == "nn":
            arr = b
            in_specs.append(pl.BlockSpec((b.shape[0], tn), lambda m, n: (0, n)))
        else:
            arr = b
            in_specs.append(pl.BlockSpec((tn, b.shape[1]), lambda m, n: (n, 0)))
        args.append(arr)
    if add is not None:
        in_specs.append(pl.BlockSpec((tm, tn), lambda m, n: (m, n)))
        args.append(add)
    if norm_g is not None:
        assert tn == n_dim and not scatter
        in_specs.append(pl.BlockSpec((1, tn), lambda m, n: (0, 0)))
        args.append(norm_g)

    n_in = len(args)
    n_sc = len(scatter)
    grid = (m_dim // tm, n_dim // tn)

    def body(*refs):
        o_ref = refs[n_in + n_sc]
        if n_sc:
            finish = _hosted_scatter(refs[n_in:n_in + n_sc], refs[n_in + n_sc + 1:n_in + 2 * n_sc + 1],
                                     refs[n_in + 2 * n_sc + 1:], pl.program_id(0) * grid[1] + pl.program_id(1),
                                     grid[0] * grid[1])
        acc = None
        for i in range(n_p):
            d = lax.dot_general(_bf(refs[i][...]), _bf(refs[n_p + i][...]), dims,
                                preferred_element_type=F32)
            acc = d if acc is None else acc + d
        if add is not None:
            acc = refs[2 * n_p][...] + acc
        o_ref[...] = acc.astype(out_dtype)
        if norm_g is not None:
            refs[n_in + 1][...] = (acc * _rstd(acc) * refs[n_in - 1][...]).astype(BF16)
        if n_sc:
            finish()

    main_spec = pl.BlockSpec((tm, tn), lambda m, n: (m, n))
    main_shape = jax.ShapeDtypeStruct((m_dim, n_dim), out_dtype)
    if norm_g is not None:
        return pl.pallas_call(body, name=name, grid=grid, in_specs=in_specs, out_specs=[main_spec, main_spec],
                              out_shape=[main_shape, jax.ShapeDtypeStruct((m_dim, n_dim), BF16)],
                              compiler_params=_cp(("parallel", "parallel")))(*args)
    if not n_sc:
        return pl.pallas_call(body, name=name, grid=grid, in_specs=in_specs, out_specs=main_spec,
                              out_shape=main_shape, compiler_params=_cp(("parallel", "parallel")))(*args)
    got_shapes, sems = _scatter_operands(scatter)
    outs = pl.pallas_call(
        body, name=name, grid=grid, in_specs=in_specs + _hbm_specs(n_sc), out_specs=[main_spec] + _hbm_specs(n_sc),
        out_shape=[main_shape] + got_shapes, scratch_shapes=sems,
        compiler_params=_cp(("arbitrary", "arbitrary")))(*args, *scatter)
    return outs[0], outs[1:]


def _mm_tn(a, b, tm, tn, name, out_chips=False):
    k_dim, m_dim = a.shape
    n_dim = b.shape[-1] * (b.shape[0] if b.ndim == 3 else 1)
    tm, tn, tk = min(tm, m_dim), min(tn, b.shape[-1]), min(TK_TN, k_dim)
    assert m_dim % tm == 0 and b.shape[-1] % tn == 0 and k_dim % tk == 0
    per = b.shape[-1] // tn
    if b.ndim == 3:
        b_spec = pl.BlockSpec((None, tk, tn), lambda m, n, k: (n // per, k, n % per))
    else:
        b_spec = pl.BlockSpec((tk, tn), lambda m, n, k: (k, n))

    def body(a_ref, b_ref, o_ref):
        @pl.when(pl.program_id(2) == 0)
        def _():
            o_ref[...] = jnp.zeros_like(o_ref)
        o_ref[...] += lax.dot_general(_bf(a_ref[...]), _bf(b_ref[...]), (((0,), (0,)), ((), ())),
                                      preferred_element_type=F32)

    return pl.pallas_call(
        body, name=name, grid=(m_dim // tm, n_dim // tn, k_dim // tk),
        in_specs=[pl.BlockSpec((tk, tm), lambda m, n, k: (k, m)), b_spec],
        out_specs=(pl.BlockSpec((None, tm, tn), lambda m, n, k: (n, m, 0)) if out_chips
                   else pl.BlockSpec((tm, tn), lambda m, n, k: (m, n))),
        out_shape=jax.ShapeDtypeStruct((n_dim // tn, m_dim, tn) if out_chips else (m_dim, n_dim), F32),
        compiler_params=_cp(("parallel", "parallel", "arbitrary")))(a, b)


def _rstd(x):
    return lax.rsqrt(jnp.mean(x * x, axis=-1, keepdims=True) + EPS)


def _rms_bwd(x, dh, g, dres, name, with_bf16, scatter=()):
    s, d = x.shape
    tm = min(TM_ROWS, s)
    n_sc = len(scatter)
    n_out = 3 if with_bf16 else 2
    got_shapes, sems = _scatter_operands(scatter) if n_sc else ([], [])

    def body(x_ref, dh_ref, g_ref, dres_ref, *rest):
        dx_ref, gg_ref = rest[n_sc], rest[n_sc + n_out - 1]
        i = pl.program_id(0)
        if n_sc:
            finish = _hosted_scatter(rest[:n_sc], rest[n_sc + n_out:2 * n_sc + n_out], rest[2 * n_sc + n_out:], i,
                                     s // tm)

        @pl.when(i == 0)
        def _():
            gg_ref[...] = jnp.zeros_like(gg_ref)

        xv = x_ref[...]
        xn = xv * _rstd(xv)
        dhv = dh_ref[...]
        gg_ref[...] += jnp.sum(dhv * xn, axis=0, keepdims=True)
        t = dhv * g_ref[...]
        dx = dres_ref[...] + _rstd(xv) * (t - xn * jnp.mean(t * xn, axis=-1, keepdims=True))
        dx_ref[...] = dx
        if with_bf16:
            rest[n_sc + 1][...] = dx.astype(BF16)
        if n_sc:
            finish()

    row = pl.BlockSpec((tm, d), lambda i: (i, 0))
    vec = pl.BlockSpec((1, d), lambda i: (0, 0))
    out_specs = [row] + ([row] if with_bf16 else []) + [vec] + _hbm_specs(n_sc)
    out_shape = ([jax.ShapeDtypeStruct((s, d), F32)] + ([jax.ShapeDtypeStruct((s, d), BF16)] if with_bf16 else [])
                 + [jax.ShapeDtypeStruct((1, d), F32)] + got_shapes)
    outs = pl.pallas_call(
        body, name=name, grid=(s // tm,), in_specs=[row, row, vec, row] + _hbm_specs(n_sc), out_specs=out_specs,
        out_shape=out_shape, scratch_shapes=sems, compiler_params=_cp(("arbitrary",)))(x, dh, g, dres, *scatter)
    return tuple(outs[:n_out]) + ((outs[n_out:],) if n_sc else ())


def _down_proj_loss(act, w_down, x2, target, g):
    s, d = x2.shape
    tm = min(TM_ROWS, s)
    k_dim = act.shape[1]

    def body(a_ref, w_ref, x_ref, t_ref, g_ref, dx_ref, dxb_ref, loss_ref, gg_ref):
        @pl.when(pl.program_id(0) == 0)
        def _():
            gg_ref[...] = jnp.zeros_like(gg_ref)
            loss_ref[...] = jnp.zeros_like(loss_ref)

        xv = x_ref[...] + lax.dot_general(a_ref[...], w_ref[...], (((1,), (0,)), ((), ())),
                                          preferred_element_type=F32)
        r = _rstd(xv)
        xn = xv * r
        gv = g_ref[...]
        err = xn * gv - t_ref[...]
        loss_ref[...] += 0.5 * jnp.sum(jnp.mean(err * err, axis=-1, keepdims=True), axis=0, keepdims=True)
        dy = err * (1.0 / d)
        gg_ref[...] += jnp.sum(dy * xn, axis=0, keepdims=True)
        t = dy * gv
        dx = r * (t - xn * jnp.mean(t * xn, axis=-1, keepdims=True))
        dx_ref[...] = dx
        dxb_ref[...] = dx.astype(BF16)

    row = pl.BlockSpec((tm, d), lambda i: (i, 0))
    vec = pl.BlockSpec((1, d), lambda i: (0, 0))
    return pl.pallas_call(
        body, name="down_proj_loss", grid=(s // tm,),
        in_specs=[pl.BlockSpec((tm, k_dim), lambda i: (i, 0)), pl.BlockSpec((k_dim, d), lambda i: (0, 0)), row, row,
                  vec],
        out_specs=[row, row, pl.BlockSpec((1, LANES), lambda i: (0, 0)), vec],
        out_shape=[jax.ShapeDtypeStruct((s, d), F32), jax.ShapeDtypeStruct((s, d), BF16),
                   jax.ShapeDtypeStruct((1, LANES), F32), jax.ShapeDtypeStruct((1, d), F32)],
        compiler_params=_cp(("arbitrary",)))(act, w_down, x2, target, g)


def _prev_halo_spec(tm, width, col):
    return pl.BlockSpec((HALO, width), lambda i, *_: (jnp.maximum(i * (tm // HALO) - 1, 0), col))


def _next_halo_spec(tm, width, col, s):
    return pl.BlockSpec((HALO, width), lambda i, *_: (jnp.minimum((i + 1) * (tm // HALO), s // HALO - 1), col))


def _shift_down(x, k):
    return pltpu.roll(x, k, 0)


def _shift_up(x, k):
    return pltpu.roll(x, x.shape[0] - k, 0)


def _conv_taps(x_ext, w):
    return w[0:1, :] * _shift_down(x_ext, 2) + w[1:2, :] * _shift_down(x_ext, 1) + w[2:3, :] * x_ext


def _conv_taps_t(d_ext, w):
    return w[2:3, :] * d_ext + w[1:2, :] * _shift_up(d_ext, 1) + w[0:1, :] * _shift_up(d_ext, 2)


def _mixer_fwd(z_a, o_attn, w_conv, g_conv_out, g_attn_out):
    s = z_a.shape[0]
    c = CONV_CH
    tm = min(TM_ROWS, s)

    def body(gb_ref, gc_ref, xc_ref, gcp_ref, xcp_ref, o_ref, w_ref, gco_ref, gao_ref, mix_ref):
        i = pl.program_id(0)
        cx = gc_ref[...] * xc_ref[...]
        cx_prev = jnp.where(i > 0, gcp_ref[...] * xcp_ref[...], 0.0)
        conv = _conv_taps(jnp.concatenate([cx_prev, cx], axis=0), w_ref[...])[HALO:]
        y = gb_ref[...] * conv
        mix_ref[:, 0:c] = (y * _rstd(y) * gco_ref[...]).astype(BF16)
        o = o_ref[...]
        mix_ref[:, c:2 * c] = (o * _rstd(o) * gao_ref[...]).astype(BF16)

    col = lambda j: pl.BlockSpec((tm, c), lambda i: (i, j))
    vec = pl.BlockSpec((1, c), lambda i: (0, 0))
    return pl.pallas_call(
        body, name="mixer_fwd", grid=(s // tm,),
        in_specs=[col(0), col(1), col(2), _prev_halo_spec(tm, c, 1), _prev_halo_spec(tm, c, 2), col(0),
                  pl.BlockSpec((3, c), lambda i: (0, 0)), vec, vec],
        out_specs=pl.BlockSpec((tm, 2 * c), lambda i: (i, 0)),
        out_shape=jax.ShapeDtypeStruct((s, 2 * c), BF16),
        compiler_params=_cp(("parallel",)))(z_a, z_a, z_a, z_a, z_a, o_attn, w_conv, g_conv_out, g_attn_out)


def _mixer_bwd(z_a, o_attn, dmix, w_conv, g_conv_out, g_attn_out, scatter):
    s = z_a.shape[0]
    c = CONV_CH
    tm = min(TM_ROWS, s)
    n_blk = s // tm
    n_sc = len(scatter)
    got_shapes, sems = _scatter_operands(scatter)

    def body(gb_ref, gc_ref, xc_ref, gcp_ref, xcp_ref, gbn_ref, gcn_ref, xcn_ref, o_ref, dnc_ref, dncn_ref, dna_ref,
             w_ref, gco_ref, gao_ref, *rest):
        dz_ref, dox_ref, gw_ref, ggco_ref, ggao_ref = rest[n_sc:n_sc + 5]
        i = pl.program_id(0)
        finish = _hosted_scatter(rest[:n_sc], rest[n_sc + 5:2 * n_sc + 5], rest[2 * n_sc + 5:], i, n_blk)

        @pl.when(i == 0)
        def _():
            gw_ref[...] = jnp.zeros_like(gw_ref)
            ggco_ref[...] = jnp.zeros_like(ggco_ref)
            ggao_ref[...] = jnp.zeros_like(ggao_ref)

        w = w_ref[...]
        zeros = jnp.zeros((HALO, c), F32)
        gb_e = jnp.concatenate([zeros, gb_ref[...], gbn_ref[...]], axis=0)
        cx_prev = jnp.where(i > 0, gcp_ref[...] * xcp_ref[...], 0.0)
        gc_e = jnp.concatenate([zeros, gc_ref[...], gcn_ref[...]], axis=0)
        xc_e = jnp.concatenate([zeros, xc_ref[...], xcn_ref[...]], axis=0)
        cx_e = jnp.concatenate([cx_prev, gc_ref[...] * xc_ref[...], gcn_ref[...] * xcn_ref[...]], axis=0)
        dn_next = jnp.where(i < n_blk - 1, dncn_ref[...], 0.0)
        dn_e = jnp.concatenate([zeros, dnc_ref[...], dn_next], axis=0)

        cx_1, cx_2 = _shift_down(cx_e, 1), _shift_down(cx_e, 2)
        conv_e = w[0:1, :] * cx_2 + w[1:2, :] * cx_1 + w[2:3, :] * cx_e
        y_e = gb_e * conv_e
        r_e = _rstd(y_e)
        yn_e = y_e * r_e
        t_e = dn_e * gco_ref[...]
        dy_e = r_e * (t_e - yn_e * jnp.mean(t_e * yn_e, axis=-1, keepdims=True))
        dconv_e = dy_e * gb_e
        dcx_e = _conv_taps_t(dconv_e, w)
        blk = slice(HALO, HALO + tm)
        dz_ref[:, 0:c] = (dy_e * conv_e)[blk].astype(BF16)
        dz_ref[:, c:2 * c] = (dcx_e * xc_e)[blk].astype(BF16)
        dz_ref[:, 2 * c:3 * c] = (dcx_e * gc_e)[blk].astype(BF16)
        ggco_ref[...] += jnp.sum((dn_e * yn_e)[blk], axis=0, keepdims=True)
        dconv = dconv_e[blk]
        gw_ref[0:1, :] += jnp.sum(dconv * cx_2[blk], axis=0, keepdims=True)
        gw_ref[1:2, :] += jnp.sum(dconv * cx_1[blk], axis=0, keepdims=True)
        gw_ref[2:3, :] += jnp.sum(dconv * cx_e[blk], axis=0, keepdims=True)

        o = o_ref[...]
        ra = _rstd(o)
        on = o * ra
        dna = dna_ref[...]
        ggao_ref[...] += jnp.sum(dna * on, axis=0, keepdims=True)
        ta = dna * gao_ref[...]
        do = ra * (ta - on * jnp.mean(ta * on, axis=-1, keepdims=True))
        prod = do * o
        lane = lax.broadcasted_iota(jnp.int32, (tm, LANES), 1)
        head_a = lane < HEAD_DIM
        for p in range(N_PAIRS):
            cols = slice(p * LANES, (p + 1) * LANES)
            pb, dob = prod[:, cols], do[:, cols]
            for hh in range(2):
                sel = head_a if hh == 0 else jnp.logical_not(head_a)
                delta = jnp.sum(jnp.where(sel, pb, 0.0), axis=-1, keepdims=True)
                neg3 = _split3(-delta)
                do_h = pltpu.roll(dob, HEAD_DIM, 1) if hh else dob
                dox_ref[2 * p + hh] = _aug(do_h, lane, neg3).astype(BF16)
        finish()

    col = lambda j: pl.BlockSpec((tm, c), lambda i: (i, j))
    vec = pl.BlockSpec((1, c), lambda i: (0, 0))
    w3 = pl.BlockSpec((3, c), lambda i: (0, 0))
    outs = pl.pallas_call(
        body, name="mixer_bwd", grid=(n_blk,),
        in_specs=[col(0), col(1), col(2), _prev_halo_spec(tm, c, 1), _prev_halo_spec(tm, c, 2),
                  _next_halo_spec(tm, c, 0, s), _next_halo_spec(tm, c, 1, s), _next_halo_spec(tm, c, 2, s),
                  col(0), col(0), _next_halo_spec(tm, c, 0, s), col(1), w3, vec, vec] + _hbm_specs(n_sc),
        out_specs=[pl.BlockSpec((tm, 3 * c), lambda i: (i, 0)),
                   pl.BlockSpec((N_HEADS, tm, LANES), lambda i: (0, i, 0)), w3, vec, vec] + _hbm_specs(n_sc),
        out_shape=[jax.ShapeDtypeStruct((s, 3 * c), BF16), jax.ShapeDtypeStruct((N_HEADS, s, LANES), BF16),
                   jax.ShapeDtypeStruct((3, c), F32), jax.ShapeDtypeStruct((1, c), F32),
                   jax.ShapeDtypeStruct((1, c), F32)] + got_shapes,
        scratch_shapes=sems, compiler_params=_cp(("arbitrary",)))(
            z_a, z_a, z_a, z_a, z_a, z_a, z_a, z_a, o_attn, dmix, dmix, dmix, w_conv, g_conv_out, g_attn_out,
            *scatter)
    return tuple(outs[:5]) + (outs[5:],)


def _gate_fwd(f, b_pad):
    s = f.shape[0]
    tm = min(TQ, s)

    def body(f_ref, b_ref, fb_ref, carry):
        @pl.when(pl.program_id(0) == 0)
        def _():
            carry[...] = jnp.zeros_like(carry)

        z = f_ref[...] + b_ref[...]
        x = jnp.minimum(z, 0.0) - jnp.log1p(jnp.exp(-jnp.abs(z)))
        row = lax.broadcasted_iota(jnp.int32, (tm, LANES), 0)
        sh = 1
        while sh < tm:
            x = x + jnp.where(row >= sh, _shift_down(x, sh), 0.0)
            sh *= 2
        x = x + carry[0:1, :]
        carry[...] = jnp.broadcast_to(x[tm - 1:tm, :], carry.shape)
        head_a = lax.broadcasted_iota(jnp.int32, (tm, LANES), 1) < HEAD_DIM
        for p in range(N_PAIRS):
            fa = jnp.broadcast_to(x[:, 2 * p:2 * p + 1], (tm, LANES))
            fbv = jnp.broadcast_to(x[:, 2 * p + 1:2 * p + 2], (tm, LANES))
            fb_ref[:, p * LANES:(p + 1) * LANES] = jnp.where(head_a, fa, fbv)

    return pl.pallas_call(
        body, name="gate_fwd", grid=(s // tm,),
        in_specs=[pl.BlockSpec((tm, LANES), lambda i: (i, 0)), pl.BlockSpec((1, LANES), lambda i: (0, 0))],
        out_specs=pl.BlockSpec((tm, N_PAIRS * LANES), lambda i: (i, 0)),
        out_shape=jax.ShapeDtypeStruct((s, N_PAIRS * LANES), F32),
        scratch_shapes=[pltpu.VMEM((HALO, LANES), F32)],
        compiler_params=_cp(("arbitrary",)))(f, b_pad)


def _gate_bwd(f, b_pad, d_f):
    s = f.shape[0]
    tm = min(TQ, s)
    n_blk = s // tm

    def body(f_ref, b_ref, d_ref, df_ref, gb_ref, carry):
        @pl.when(pl.program_id(0) == 0)
        def _():
            carry[...] = jnp.zeros_like(carry)
            gb_ref[...] = jnp.zeros_like(gb_ref)

        x = d_ref[...]
        row = lax.broadcasted_iota(jnp.int32, (tm, LANES), 0)
        sh = 1
        while sh < tm:
            x = x + jnp.where(row < tm - sh, _shift_up(x, sh), 0.0)
            sh *= 2
        x = x + carry[0:1, :]
        carry[...] = jnp.broadcast_to(x[0:1, :], carry.shape)
        z = f_ref[...] + b_ref[...]
        d = x * (1.0 / (1.0 + jnp.exp(z)))
        df_ref[...] = d.astype(BF16)
        gb_ref[...] += jnp.sum(d, axis=0, keepdims=True)

    rev = pl.BlockSpec((tm, LANES), lambda i: (n_blk - 1 - i, 0))
    vec = pl.BlockSpec((1, LANES), lambda i: (0, 0))
    return pl.pallas_call(
        body, name="gate_bwd", grid=(n_blk,), in_specs=[rev, vec, rev], out_specs=[rev, vec],
        out_shape=[jax.ShapeDtypeStruct((s, LANES), BF16), jax.ShapeDtypeStruct((1, LANES), F32)],
        scratch_shapes=[pltpu.VMEM((HALO, LANES), F32)],
        compiler_params=_cp(("arbitrary",)))(f, b_pad, d_f)


_NT = (((1,), (1,)), ((), ()))
_NN = (((1,), (0,)), ((), ()))


AUG = HEAD_DIM
NORM_MARGIN = 1.01


def _split3(x):
    hi = x.astype(BF16).astype(F32)
    r = x - hi
    mid = r.astype(BF16).astype(F32)
    lo = (r - mid).astype(BF16).astype(F32)
    return hi, mid, lo


def _aug(base, lane, vals):
    out = jnp.where(lane < AUG, base, 0.0)
    for k, v in enumerate(vals):
        out = jnp.where(lane == AUG + k, v, out)
    return out


def _attn_prep(qkv, fb, bigs):
    s = qkv.shape[0]
    tq = min(TQ, s)
    n_q = s // tq

    n = len(bigs)
    arrays, landing, sems = _gather_operands(bigs, [])

    def body(q_ref, k_ref, v_ref, fb_ref, *rest):
        qx_ref, kx_ref, kxt_ref, vx_ref, vt_ref, b_ref = rest[2 * n:2 * n + 6]
        finish = _hosted_gather((rest[:n], rest[2 * n + 6:3 * n + 6]) + tuple(rest[3 * n + 6:]), n, n,
                                pl.program_id(0), n_q)
        lane = lax.broadcasted_iota(jnp.int32, (tq, LANES), 1)
        lane8 = lax.broadcasted_iota(jnp.int32, (HALO, LANES), 1)
        head_lanes = lane < AUG
        is_lane = [lane == AUG + k for k in range(6)]
        first3 = (lane >= AUG) & (lane < AUG + 3)
        next3 = (lane >= AUG + 3) & (lane < AUG + 6)
        q_const = jnp.where(first3, -1.0, 0.0)
        k_const = jnp.where(next3, 1.0, 0.0)
        v_const = jnp.where(first3, 1.0, 0.0)
        ones_head = (lax.broadcasted_iota(jnp.int32, (LANES, LANES), 0) < HEAD_DIM).astype(BF16)
        acc = jnp.zeros((HALO, LANES), F32)
        for p in range(N_PAIRS):
            cols = slice(p * LANES, (p + 1) * LANES)
            q2, k2, v2 = (ref[:, cols].astype(F32) for ref in (q_ref, k_ref, v_ref))
            q2 = q2 * Q_SCALE
            f2 = fb_ref[:, cols]
            for hh in range(2):
                h = 2 * p + hh
                q, k, v = ((pltpu.roll(x, HEAD_DIM, 1) if hh else x) for x in (q2, k2, v2))
                f = f2 if hh else pltpu.roll(f2, HEAD_DIM, 1)
                hi, mid, lo = _split3(f)
                q_aug = jnp.where(is_lane[3], hi, jnp.where(is_lane[4], mid, jnp.where(is_lane[5], lo, q_const)))
                k_aug = jnp.where(is_lane[0], hi, jnp.where(is_lane[1], mid, jnp.where(is_lane[2], lo, k_const)))
                kx = jnp.where(head_lanes, k, k_aug)
                vx = jnp.where(head_lanes, v, v_const)
                qx_ref[h] = jnp.where(head_lanes, q, q_aug).astype(BF16)
                kx_ref[h] = kx.astype(BF16)
                vx_ref[h] = vx.astype(BF16)
                kxt_ref[h, 0] = kx.T.astype(BF16)
                vt_ref[h, 0] = vx.T.astype(BF16)
                q_sq = lax.dot_general((q * q).astype(BF16), ones_head, _NN, preferred_element_type=F32)
                k_sq = lax.dot_general((k * k).astype(BF16), ones_head, _NN, preferred_element_type=F32)
                diag = lax.dot_general((q * k).astype(BF16), ones_head, _NN, preferred_element_type=F32)
                diag = diag - jnp.sqrt(q_sq * k_sq) * (NORM_MARGIN - 1.0)
                vals = (jnp.sqrt(jnp.max(q_sq, axis=0, keepdims=True)), jnp.sqrt(jnp.max(k_sq, axis=0, keepdims=True)),
                        jnp.max(f - diag, axis=0, keepdims=True), f[tq - 1:tq, :])
                for slot, val in enumerate(vals):
                    acc = jnp.where(lane8 == slot * N_HEADS + h, val[:, AUG:AUG + 1], acc)
        b_ref[0] = acc
        finish()

    blk = lambda j: pl.BlockSpec((tq, ATTN_W), lambda i: (i, j))
    rows = pl.BlockSpec((N_HEADS, tq, LANES), lambda i: (0, i, 0))
    cols_t = pl.BlockSpec((N_HEADS, 1, LANES, tq), lambda i: (0, i, 0, 0))
    shp = jax.ShapeDtypeStruct((N_HEADS, s, LANES), BF16)
    shp_t = jax.ShapeDtypeStruct((N_HEADS, n_q, LANES, tq), BF16)
    outs = pl.pallas_call(
        body, name="attn_prep", grid=(n_q,), in_specs=[blk(0), blk(1), blk(2), blk(0)] + _hbm_specs(2 * n),
        out_specs=[rows, rows, cols_t, rows, cols_t,
                   pl.BlockSpec((1, HALO, LANES), lambda i: (i, 0, 0))] + _hbm_specs(n),
        out_shape=[shp, shp, shp_t, shp, shp_t, jax.ShapeDtypeStruct((n_q, HALO, LANES), F32)]
        + [jax.ShapeDtypeStruct(b.shape, b.dtype) for b in landing],
        input_output_aliases={4 + n + k: 6 + k for k in range(n)}, scratch_shapes=sems,
        compiler_params=_cp(("arbitrary",)))(qkv, qkv, qkv, fb, *arrays, *landing)
    return tuple(outs[:6]) + (outs[6:],)


def _key_block_ranges(bounds):
    t = bounds[:, 0, :]
    nh = N_HEADS
    a, b, c, e = t[:, 0:nh], t[:, nh:2 * nh], t[:, 2 * nh:3 * nh], t[:, 3 * nh:4 * nh]
    bound = a[:, None, :] * b[None, :, :] * NORM_MARGIN + c[:, None, :] - e[None, :, :]
    n_q = t.shape[0]
    idx = jnp.arange(n_q)
    need = jnp.logical_not(bound < -(EXP_ZERO + 2.0)) | (idx[None, :, None] >= idx[:, None, None])
    first = jnp.argmax(need, axis=1).astype(jnp.int32)
    first = jnp.min(first.reshape(n_q, N_PAIRS, 2), axis=-1)
    visits = (first[:, None, :] <= idx[None, :, None]) & (idx[None, :, None] <= idx[:, None, None])
    last = jnp.max(jnp.where(visits, idx[:, None, None], 0), axis=0).astype(jnp.int32)
    return first.T.reshape(-1), last.T.reshape(-1)


def _attn_fwd_t(qx, kx, vt, first_blk):
    _, s, _ = qx.shape
    tq = min(TQ, s)
    n_q = s // tq
    neg = -1e30

    def body(first_ref, qx_ref, kx_ref, vt_ref, o_ref, lse_ref, acc_ref, m_ref, s_even, s_odd):
        p = pl.program_id(0)
        i = pl.program_id(1)
        acc_ref[...] = jnp.zeros(acc_ref.shape, F32)
        m_ref[...] = jnp.full(m_ref.shape, neg, F32)
        key_le_query = (lax.broadcasted_iota(jnp.int32, (tq, tq), 0) <= lax.broadcasted_iota(jnp.int32, (tq, tq), 1))
        first = first_ref[p * n_q + i]

        def scores(kb, hh, dst):
            rows_k = pl.ds(pl.multiple_of(kb * tq, tq), tq)
            dst[hh] = lax.dot_general(kx_ref[hh, rows_k, :], qx_ref[hh], _NT, preferred_element_type=F32)

        def step(kb, src, nxt):
            for hh in range(2):
                st = src[hh]
                if nxt is None:
                    st = jnp.where(key_le_query, st, -jnp.inf)
                else:
                    scores(kb + 1, hh, nxt)
                m_old = m_ref[hh]
                m_new = jnp.maximum(m_old, jnp.max(st, axis=0, keepdims=True))
                m_ref[hh] = m_new
                pt = jnp.exp(st - m_new).astype(BF16)
                acc_ref[hh] = acc_ref[hh] * jnp.exp(m_old - m_new) + lax.dot_general(
                    vt_ref[hh, kb], pt, _NN, preferred_element_type=F32)

        def by_parity(kb, fn):
            @pl.when(kb % 2 == 0)
            def _():
                fn(s_even, s_odd)

            @pl.when(kb % 2 == 1)
            def _():
                fn(s_odd, s_even)

        def first_scores(src, nxt):
            scores(first, 0, src)
            scores(first, 1, src)

        def unmasked(kb, carry):
            by_parity(kb, lambda src, nxt: step(kb, src, nxt))
            return carry

        by_parity(first, first_scores)
        lax.fori_loop(first, i, unmasked, 0)
        by_parity(i, lambda src, nxt: step(i, src, None))
        outs, lses = [], []
        for hh in range(2):
            acc = acc_ref[hh]
            l = acc[AUG:AUG + 1, :]
            outs.append(acc[0:HEAD_DIM, :] / l)
            lses.append(m_ref[hh] + jnp.log(l))
        o_ref[...] = jnp.concatenate(outs, axis=0).T
        rows8 = lax.broadcasted_iota(jnp.int32, (N_HEADS, tq), 0)
        lse_ref[0, 0] = jnp.where(rows8 == 0, lses[0], jnp.where(rows8 == 1, lses[1], 0.0))

    grid_spec = pltpu.PrefetchScalarGridSpec(
        num_scalar_prefetch=1, grid=(N_PAIRS, n_q),
        in_specs=[pl.BlockSpec((2, tq, LANES), lambda p, i, first: (p, i, 0)),
                  pl.BlockSpec((2, s, LANES), lambda p, i, first: (p, 0, 0)),
                  pl.BlockSpec((2, n_q, LANES, tq), lambda p, i, first: (p, 0, 0, 0))],
        out_specs=[pl.BlockSpec((tq, LANES), lambda p, i, first: (i, p)),
                   pl.BlockSpec((1, 1, N_HEADS, tq), lambda p, i, first: (p, i, 0, 0))],
        scratch_shapes=[pltpu.VMEM((2, LANES, tq), F32), pltpu.VMEM((2, 1, tq), F32),
                        pltpu.VMEM((2, tq, tq), F32), pltpu.VMEM((2, tq, tq), F32)])
    return pl.pallas_call(
        body, name="attn_fwd", grid_spec=grid_spec,
        out_shape=[jax.ShapeDtypeStruct((s, ATTN_W), F32), jax.ShapeDtypeStruct((N_PAIRS, n_q, N_HEADS, tq), F32)],
        compiler_params=_cp(("parallel", "arbitrary")))(first_blk, qx, kx, vt)


def _attn_bwd_t(qx, dox, kx, kxt, vx, lse, last_blk):
    _, s, _ = qx.shape
    tq = min(TQ, s)
    n_q = s // tq

    def body(last_ref, qx_ref, dox_ref, lse_ref, kx_ref, kxt_ref, vx_ref, dk_ref, dv_ref, dfk_ref, dqt_ref, dfq_ref):
        p = pl.program_id(0)
        j = pl.program_id(1)

        @pl.when(j == 0)
        def _():
            dqt_ref[...] = jnp.zeros(dqt_ref.shape, F32)
            dfq_ref[...] = jnp.zeros(dfq_ref.shape, F32)

        key_le_query = (lax.broadcasted_iota(jnp.int32, (tq, tq), 0) <= lax.broadcasted_iota(jnp.int32, (tq, tq), 1))

        def step(i, carry, masked):
            rows_q = pl.ds(pl.multiple_of(i * tq, tq), tq)
            out = []
            for hh in range(2):
                dk, dv, col = carry[3 * hh:3 * hh + 3]
                q, do = qx_ref[hh, rows_q, :], dox_ref[hh, rows_q, :]
                st = lax.dot_general(kx_ref[hh], q, _NT, preferred_element_type=F32)
                pt = jnp.exp(st - lse_ref[0, i, hh:hh + 1, :])
                if masked:
                    pt = jnp.where(key_le_query, pt, 0.0)
                dst = pt * lax.dot_general(vx_ref[hh], do, _NT, preferred_element_type=F32)
                pb, dsb = pt.astype(BF16), dst.astype(BF16)
                dv = dv + lax.dot_general(pb, do, _NN, preferred_element_type=F32)
                dk = dk + lax.dot_general(dsb, q, _NN, preferred_element_type=F32)
                dqt_ref[hh, i] += lax.dot_general(kxt_ref[hh, 0], dsb, _NN, preferred_element_type=F32)
                for cb in range(tq // LANES):
                    col = col + dst[:, cb * LANES:(cb + 1) * LANES]
                dfq_ref[hh, i] += jnp.sum(dst.reshape(tq // HALO, HALO, tq), axis=0)
                out += [dk, dv, col]
            return tuple(out)

        zero = jnp.zeros((tq, LANES), F32)
        carry = step(j, (zero,) * 6, True)
        dk_a, dv_a, col_a, dk_b, dv_b, col_b = lax.fori_loop(j + 1, last_ref[p * n_q + j] + 1,
                                                             lambda i, cr: step(i, cr, False), carry)
        head_a = lax.broadcasted_iota(jnp.int32, (tq, LANES), 1) < HEAD_DIM
        dk_ref[...] = jnp.where(head_a, dk_a, pltpu.roll(dk_b, HEAD_DIM, 1)).astype(BF16)
        dv_ref[...] = jnp.where(head_a, dv_a, pltpu.roll(dv_b, HEAD_DIM, 1)).astype(BF16)
        rows8 = lax.broadcasted_iota(jnp.int32, (N_HEADS, tq), 0)
        dfk_a, dfk_b = (-jnp.sum(c.T, axis=0, keepdims=True) for c in (col_a, col_b))
        dfk_ref[0, 0] = jnp.where(rows8 == 0, dfk_a, jnp.where(rows8 == 1, dfk_b, 0.0))

    resident = pl.BlockSpec((2, s, LANES), lambda p, j, last: (p, 0, 0))
    key_rows = pl.BlockSpec((2, tq, LANES), lambda p, j, last: (p, j, 0))
    pair_out = pl.BlockSpec((tq, LANES), lambda p, j, last: (j, p))
    grid_spec = pltpu.PrefetchScalarGridSpec(
        num_scalar_prefetch=1, grid=(N_PAIRS, n_q),
        in_specs=[resident, resident, pl.BlockSpec((1, n_q, N_HEADS, tq), lambda p, j, last: (p, 0, 0, 0)),
                  key_rows, pl.BlockSpec((2, 1, LANES, tq), lambda p, j, last: (p, j, 0, 0)), key_rows],
        out_specs=[pair_out, pair_out, pl.BlockSpec((1, 1, N_HEADS, tq), lambda p, j, last: (p, j, 0, 0)),
                   pl.BlockSpec((2, n_q, LANES, tq), lambda p, j, last: (p, 0, 0, 0)),
                   pl.BlockSpec((2, n_q, HALO, tq), lambda p, j, last: (p, 0, 0, 0))])
    return pl.pallas_call(
        body, name="attn_bwd", grid_spec=grid_spec,
        out_shape=[jax.ShapeDtypeStruct((s, ATTN_W), BF16), jax.ShapeDtypeStruct((s, ATTN_W), BF16),
                   jax.ShapeDtypeStruct((N_PAIRS, n_q, N_HEADS, tq), F32),
                   jax.ShapeDtypeStruct((N_HEADS, n_q, LANES, tq), F32),
                   jax.ShapeDtypeStruct((N_HEADS, n_q, HALO, tq), F32)],
        compiler_params=_cp(("parallel", "arbitrary")))(last_blk, qx, dox, lse, kx, kxt, vx)


def _attn_dq_finish(dqt):
    _, n_q, _, tq = dqt.shape
    per = 4 if n_q % 4 == 0 else 1

    def body(dqt_ref, dq_ref):
        for b in range(per):
            a, bb = dqt_ref[0, b], dqt_ref[1, b]
            dq_ref[b * tq:(b + 1) * tq, :] = (
                jnp.concatenate([a[0:HEAD_DIM], bb[0:HEAD_DIM]], axis=0).T * Q_SCALE).astype(BF16)

    return pl.pallas_call(
        body, name="attn_dq_finish", grid=(N_PAIRS, n_q // per),
        in_specs=[pl.BlockSpec((2, per, LANES, tq), lambda p, i: (p, i, 0, 0))],
        out_specs=pl.BlockSpec((per * tq, LANES), lambda p, i: (i, p)),
        out_shape=jax.ShapeDtypeStruct((n_q * tq, ATTN_W), BF16),
        compiler_params=_cp(("parallel", "parallel")))(dqt)


def _ffn_act_fwd(up, w_ffn):
    s = up.shape[0]
    tm, tn = min(TM_FFN, s), TN_FFN
    nb = D_FF // tn

    def body(a_ref, g_ref, ap_ref, gp_ref, wa_ref, wg_ref, act_ref, u_ref):
        i = pl.program_id(1)

        def conv(blk_ref, prev_ref, w_ref):
            prev = jnp.where(i > 0, prev_ref[...], 0.0)
            return _conv_taps(jnp.concatenate([prev, blk_ref[...]], axis=0), w_ref[...])[HALO:]

        u_a, u_g = conv(a_ref, ap_ref, wa_ref), conv(g_ref, gp_ref, wg_ref)
        u_ref[0], u_ref[1] = u_a.astype(BF16), u_g.astype(BF16)
        act_ref[...] = (u_g * jax.nn.sigmoid(u_g) * u_a).astype(BF16)

    blk = lambda off: pl.BlockSpec((tm, tn), lambda n, i: (i, off + n))
    prev = lambda off: pl.BlockSpec((HALO, tn), lambda n, i: (jnp.maximum(i * (tm // HALO) - 1, 0), off + n))
    wsp = lambda off: pl.BlockSpec((3, tn), lambda n, i: (0, off + n))
    return pl.pallas_call(
        body, name="ffn_act_fwd", grid=(nb, s // tm),
        in_specs=[blk(0), blk(nb), prev(0), prev(nb), wsp(0), wsp(nb)],
        out_specs=[pl.BlockSpec((tm, tn), lambda n, i: (i, n)), pl.BlockSpec((2, tm, tn), lambda n, i: (0, i, n))],
        out_shape=[jax.ShapeDtypeStruct((s, D_FF), BF16), jax.ShapeDtypeStruct((2, s, D_FF), BF16)],
        compiler_params=_cp(("parallel", "parallel")))(up, up, up, up, w_ffn, w_ffn)


def _ffn_act_bwd(up, u, dact, w_ffn):
    s = up.shape[0]
    tm, tn = min(TM_FFN, s), TN_FFN
    nb = D_FF // tn
    n_blk = s // tm

    def body(u_ref, un_ref, a_ref, g_ref, d_ref, dn_ref, wa_ref, wg_ref, dup_ref, gwa_ref, gwg_ref):
        i = pl.program_id(1)

        @pl.when(i == 0)
        def _():
            gwa_ref[...] = jnp.zeros_like(gwa_ref)
            gwg_ref[...] = jnp.zeros_like(gwg_ref)

        ext = lambda rows, next_rows: jnp.concatenate([rows, next_rows], axis=0)
        u_a, u_g = (ext(u_ref[h].astype(F32), un_ref[h].astype(F32)[:HALO]) for h in range(2))
        d_e = ext(d_ref[...], jnp.where(i < n_blk - 1, dn_ref[...], 0.0))
        sig = jax.nn.sigmoid(u_g)
        du_a = d_e * (u_g * sig)
        du_g = d_e * u_a * (sig * (1.0 + u_g * (1.0 - sig)))
        halves = ((gwa_ref, wa_ref[...], a_ref[...], du_a), (gwg_ref, wg_ref[...], g_ref[...], du_g))
        for half, (gw_ref, w, upv, du) in enumerate(halves):
            du0, du1, du2 = du[:tm], _shift_up(du, 1)[:tm], _shift_up(du, 2)[:tm]
            dup_ref[half] = (w[2:3, :] * du0 + w[1:2, :] * du1 + w[0:1, :] * du2).astype(BF16)
            gw_ref[0:1, :] += jnp.sum(upv * du2, axis=0, keepdims=True)
            gw_ref[1:2, :] += jnp.sum(upv * du1, axis=0, keepdims=True)
            gw_ref[2:3, :] += jnp.sum(upv * du0, axis=0, keepdims=True)

    next_row = lambda halo: lambda i: jnp.minimum((i + 1) * (tm // halo), s // halo - 1)
    blk = lambda off: pl.BlockSpec((tm, tn), lambda n, i: (i, off + n))
    wsp = lambda off: pl.BlockSpec((3, tn), lambda n, i: (0, off + n))
    pair = lambda rows, row_of: pl.BlockSpec((2, rows, tn), lambda n, i: (0, row_of(i), n))
    return pl.pallas_call(
        body, name="ffn_act_bwd", grid=(nb, n_blk),
        in_specs=[pair(tm, lambda i: i), pair(HALO_BF16, next_row(HALO_BF16)), blk(0), blk(nb), blk(0),
                  pl.BlockSpec((HALO, tn), lambda n, i: (next_row(HALO)(i), n)), wsp(0), wsp(nb)],
        out_specs=[pair(tm, lambda i: i), wsp(0), wsp(0)],
        out_shape=[jax.ShapeDtypeStruct((2, s, D_FF), BF16),
                   jax.ShapeDtypeStruct((3, D_FF), F32), jax.ShapeDtypeStruct((3, D_FF), F32)],
        compiler_params=_cp(("parallel", "arbitrary")))(u, u, up, up, dact, dact, w_ffn, w_ffn)


def _adamw(w, g, m, v, name):
    r, c = w.shape
    tr = next((t for t in (512, 352, 256, 128, 64, 32, 16, 8) if r > t and r % t == 0), r)

    def body(w_ref, g_ref, m_ref, v_ref, d_ref, nm_ref, nv_ref):
        gv = g_ref[...]
        m_new = ADAM_B1 * m_ref[...] + (1.0 - ADAM_B1) * gv
        v_new = ADAM_B2 * v_ref[...] + (1.0 - ADAM_B2) * (gv * gv)
        m_hat = m_new / (1.0 - ADAM_B1 ** ADAM_STEP)
        v_hat = v_new / (1.0 - ADAM_B2 ** ADAM_STEP)
        d_ref[...] = -ADAM_LR * (m_hat / (jnp.sqrt(v_hat) + ADAM_EPS) + ADAM_WD * w_ref[...])
        nm_ref[...] = m_new
        nv_ref[...] = v_new

    spec = pl.BlockSpec((tr, c), lambda i: (i, 0))
    shp = jax.ShapeDtypeStruct((r, c), F32)
    return pl.pallas_call(
        body, name=name, grid=(r // tr,), in_specs=[spec] * 4, out_specs=[spec] * 3, out_shape=[shp] * 3,
        compiler_params=_cp(("parallel",)))(w, g, m, v)


def _sum_rows_block(h):
    return h if h <= 352 else 256


def _pair_sum(view, recv, sel, name):
    n, _, h, c = view.shape
    tr = _sum_rows_block(h)

    def body(sel_ref, a_ref, b_ref, o_ref, ob_ref):
        t = a_ref[...] + b_ref[...]
        o_ref[...] = t
        ob_ref[...] = t.astype(BF16)

    blk = pl.BlockSpec((None, tr, c), lambda j, i, sel_ref: (j, i, 0))
    grid_spec = pltpu.PrefetchScalarGridSpec(
        num_scalar_prefetch=1, grid=(n, h // tr),
        in_specs=[pl.BlockSpec((None, None, tr, c), lambda j, i, sel_ref: (j, sel_ref[0], i, 0)),
                  pl.BlockSpec((None, None, tr, c), lambda j, i, sel_ref: (j, 0, i, 0))],
        out_specs=[blk, blk])
    return pl.pallas_call(
        body, name=name, grid_spec=grid_spec,
        out_shape=[jax.ShapeDtypeStruct((n, h, c), F32), jax.ShapeDtypeStruct((n, h, c), BF16)],
        compiler_params=_cp(("parallel", "parallel")))(sel, view, recv)


def _chip_sum(pair, got, sel, name):
    _, h, c = pair.shape
    tr = _sum_rows_block(h)
    nblk = h // tr

    def body(sel_ref, p_ref, g0_ref, g1_ref, g2_ref, o_ref):
        o_ref[...] = ((p_ref[...] + g0_ref[...].astype(F32)) + g1_ref[...].astype(F32)) + g2_ref[...].astype(F32)

    slot = lambda k: pl.BlockSpec((None, tr, c), lambda i, sel_ref: (k, i, 0))
    grid_spec = pltpu.PrefetchScalarGridSpec(
        num_scalar_prefetch=1, grid=(h // tr,),
        in_specs=[pl.BlockSpec((None, tr, c), lambda i, sel_ref: (sel_ref[1], i, 0)), slot(0), slot(1), slot(2)],
        out_specs=pl.BlockSpec((tr, c), lambda i, sel_ref: (sel_ref[0] * nblk + i, 0)))
    return pl.pallas_call(
        body, name=name, grid_spec=grid_spec, out_shape=jax.ShapeDtypeStruct((2 * h, c), F32),
        compiler_params=_cp(("parallel",)))(sel, pair, got, got, got)


def _place():
    return lax.axis_index("x"), lax.axis_index("y"), lax.axis_index("c")


def _other_chips(x, y):
    return [(1 - x, y), (x, 1 - y), (1 - x, 1 - y)]


def _hbm_specs(n):
    return [pl.BlockSpec(memory_space=pl.ANY)] * n


def _all_gather_weights(bigs, smalls):
    nb, ns = len(bigs), len(smalls)
    n = nb + ns

    def body(*refs):
        start, forward, finish = _gather_phases(refs[:n], refs[2 * n:3 * n], nb, *refs[3 * n:])
        start()
        forward()
        finish()

    arrays, landing, sems = _gather_operands(bigs, smalls)
    return pl.pallas_call(
        body, name="all_gather_weights",
        out_shape=[jax.ShapeDtypeStruct(b.shape, b.dtype) for b in landing],
        in_specs=_hbm_specs(2 * n), out_specs=_hbm_specs(n), input_output_aliases={n + k: k for k in range(n)},
        scratch_shapes=sems)(*arrays, *landing)


def _hosted_gather(refs, n, nb, step, total):
    ins, outs, send_sems, recv_sems = refs
    start, forward, finish = _gather_phases(ins, outs, nb, send_sems, recv_sems)
    pl.when(step == 0)(start)
    pl.when(step == (3 * total) // 4)(forward)
    return lambda: pl.when(step == total - 1)(finish)


def _in_proj(x, g, w_a, w_b, w_c, bigs, smalls):
    s, d = x.shape
    tm, tn = min(TM_MM, s), ATTN_W
    na, nq = w_a.shape[1] // tn, w_b.shape[1] // tn
    steps = na + nq + 1
    total = (s // tm) * steps
    nb, n = len(bigs), len(bigs) + len(smalls)
    arrays, landing, sems = _gather_operands(bigs, smalls)

    def body(x_ref, g_ref, wa_ref, wb_ref, wc_ref, *rest):
        z_ref, qkv_ref, f_ref, h_ref = rest[2 * n:2 * n + 4]
        h_scr = rest[-1]
        m, j = pl.program_id(0), pl.program_id(1)
        finish = _hosted_gather((rest[:n], rest[2 * n + 4:3 * n + 4]) + tuple(rest[3 * n + 4:3 * n + 6]), n, nb,
                                m * steps + j, total)

        @pl.when(j == 0)
        def _():
            xv = x_ref[...]
            hv = (xv * _rstd(xv) * g_ref[...]).astype(BF16)
            h_scr[...] = hv
            h_ref[...] = hv

        h = h_scr[...]

        @pl.when(j < na)
        def _():
            z_ref[...] = lax.dot_general(h, wa_ref[...], _NN, preferred_element_type=F32)

        @pl.when((j >= na) & (j < na + nq))
        def _():
            qkv_ref[...] = lax.dot_general(h, wb_ref[...], _NN, preferred_element_type=F32).astype(BF16)

        @pl.when(j == na + nq)
        def _():
            f_ref[...] = lax.dot_general(h, wc_ref[...], _NN, preferred_element_type=F32)

        finish()

    blk_a = lambda m, j: (m, jnp.minimum(j, na - 1))
    blk_b = lambda m, j: (m, jnp.clip(j - na, 0, nq - 1))
    outs = pl.pallas_call(
        body, name="in_proj", grid=(s // tm, steps),
        in_specs=[pl.BlockSpec((tm, d), lambda m, j: (m, 0)), pl.BlockSpec((1, d), lambda m, j: (0, 0)),
                  pl.BlockSpec((d, tn), lambda m, j: (0, jnp.minimum(j, na - 1))),
                  pl.BlockSpec((d, tn), lambda m, j: (0, jnp.clip(j - na, 0, nq - 1))),
                  pl.BlockSpec((d, LANES), lambda m, j: (0, 0))] + _hbm_specs(2 * n),
        out_specs=[pl.BlockSpec((tm, tn), blk_a), pl.BlockSpec((tm, tn), blk_b),
                   pl.BlockSpec((tm, LANES), lambda m, j: (m, 0)), pl.BlockSpec((tm, d), lambda m, j: (m, 0))]
        + _hbm_specs(n),
        out_shape=[jax.ShapeDtypeStruct((s, w_a.shape[1]), F32), jax.ShapeDtypeStruct((s, w_b.shape[1]), BF16),
                   jax.ShapeDtypeStruct((s, LANES), F32), jax.ShapeDtypeStruct((s, d), BF16)]
        + [jax.ShapeDtypeStruct(b.shape, b.dtype) for b in landing],
        input_output_aliases={5 + n + k: 4 + k for k in range(n)},
        scratch_shapes=sems + [pltpu.VMEM((tm, d), BF16)],
        compiler_params=_cp(("arbitrary", "arbitrary")))(x, g, w_a, w_b, w_c, *arrays, *landing)
    return outs[0], outs[1], outs[2], outs[3], outs[4:]


def _gather_operands(bigs, smalls):
    x, y, _ = _place()
    arrays = list(bigs) + list(smalls)
    landing = [lax.dynamic_update_index_in_dim(lax.empty((N_CHIPS,) + a.shape, a.dtype), a, 2 * x + y, 0)
               for a in arrays]
    n_sems = 6 * len(bigs) + 3 * len(smalls)
    return arrays, landing, [pltpu.SemaphoreType.DMA((n_sems,)), pltpu.SemaphoreType.DMA((n_sems,))]


def _gather_phases(ins, outs, nb, send_sems, recv_sems):
    n = len(ins)
    x, y, c = _place()
    my_chip = 2 * x + y
    chips = _other_chips(x, y)
    sibling = (x, y, 1 - c)

    def rows(k, which):
        h = ins[k].shape[0] // 2
        return pl.ds(which * h, h)

    def copy(sem, src, dst, to):
        return pltpu.make_async_remote_copy(src_ref=src, dst_ref=dst, send_sem=send_sems.at[sem],
                                            recv_sem=recv_sems.at[sem], device_id=to, device_id_type=MESH)

    def sends():
        out = [copy(6 * k + j, ins[k].at[rows(k, c)], outs[k].at[my_chip, rows(k, c)], (cx, cy, c))
               for k in range(nb) for j, (cx, cy) in enumerate(chips)]
        return out + [copy(6 * nb + 3 * (k - nb) + j, ins[k], outs[k].at[my_chip], (cx, cy, c))
                      for k in range(nb, n) for j, (cx, cy) in enumerate(chips)]

    def landed(k, j, which):
        cx, cy = chips[j]
        return outs[k].at[2 * cx + cy, rows(k, which)]

    def forwards():
        return [copy(6 * k + 3 + j, landed(k, j, c), landed(k, j, c), sibling)
                for j in range(3) for k in range(nb)]

    def start():
        for cp in sends():
            cp.start()

    def forward():
        for j in range(3):
            for k in range(nb):
                copy(6 * k + j, landed(k, j, c), landed(k, j, c), (x, y, c)).wait_recv()
                copy(6 * k + 3 + j, landed(k, j, c), landed(k, j, c), sibling).start()

    def finish():
        for j, (cx, cy) in enumerate(chips):
            for k in range(nb):
                copy(6 * k + 3 + j, landed(k, j, 1 - c), landed(k, j, 1 - c), (x, y, c)).wait_recv()
            for k in range(nb, n):
                arrived = outs[k].at[2 * cx + cy]
                copy(6 * nb + 3 * (k - nb) + j, arrived, arrived, (x, y, c)).wait_recv()
        for cp in sends() + forwards():
            cp.wait_send()

    return start, forward, finish


def _pair_exchange(views, name):
    n = len(views)

    def body(*refs):
        ins, outs, send_sems, recv_sems = refs[:n], refs[n:2 * n], refs[2 * n], refs[2 * n + 1]
        x, y, c = _place()
        copies = [pltpu.make_async_remote_copy(
            src_ref=ins[k].at[:, pl.ds(1 - c, 1)], dst_ref=outs[k], send_sem=send_sems.at[k],
            recv_sem=recv_sems.at[k], device_id=(x, y, 1 - c), device_id_type=MESH) for k in range(n)]
        for cp in copies:
            cp.start()
        for cp in copies:
            cp.wait()

    return pl.pallas_call(
        body, name=name,
        out_shape=[jax.ShapeDtypeStruct((v.shape[0], 1) + v.shape[2:], v.dtype) for v in views],
        in_specs=_hbm_specs(n), out_specs=_hbm_specs(n),
        scratch_shapes=[pltpu.SemaphoreType.DMA((n,)), pltpu.SemaphoreType.DMA((n,))])(*views)


def _scatter_operands(parts):
    n = len(parts)
    return ([jax.ShapeDtypeStruct((3,) + p.shape[1:], p.dtype) for p in parts],
            [pltpu.SemaphoreType.DMA((3 * n,)), pltpu.SemaphoreType.DMA((3 * n,))])


def _scatter_phases(ins, outs, send_sems, recv_sems):
    x, y, c = _place()

    def copies():
        return [pltpu.make_async_remote_copy(
            src_ref=ins[k].at[pl.ds(2 * cx + cy, 1)], dst_ref=outs[k].at[pl.ds(r, 1)], send_sem=send_sems.at[3 * k + r],
            recv_sem=recv_sems.at[3 * k + r], device_id=(cx, cy, c), device_id_type=MESH)
            for k in range(len(ins)) for r, (cx, cy) in enumerate(_other_chips(x, y))]

    def start():
        for cp in copies():
            cp.start()

    def finish():
        for cp in copies():
            cp.wait()

    return start, finish


def _hosted_scatter(ins, outs, sems, step, total):
    start, finish = _scatter_phases(ins, outs, *sems)
    pl.when(step == 0)(start)
    return lambda: pl.when(step == total - 1)(finish)


def _join_halves(shards):
    n = len(shards)

    def body(*refs):
        ins, outs, send_sems, recv_sems = refs[:n], refs[n:2 * n], refs[2 * n], refs[2 * n + 1]
        x, y, c = _place()

        def rows(ref, which):
            h = ref.shape[0] // 2
            return ref.at[pl.ds(which * h, h)]

        sent = [pltpu.make_async_remote_copy(
            src_ref=rows(ins[k], c), dst_ref=rows(outs[k], c), send_sem=send_sems.at[k], recv_sem=recv_sems.at[k],
            device_id=(x, y, 1 - c), device_id_type=MESH) for k in range(n)]
        for cp in sent:
            cp.start()
        for k in range(n):
            pltpu.make_async_remote_copy(
                src_ref=rows(ins[k], 1 - c), dst_ref=rows(outs[k], 1 - c), send_sem=send_sems.at[k],
                recv_sem=recv_sems.at[k], device_id=(x, y, 1 - c), device_id_type=MESH).wait_recv()
        for cp in sent:
            cp.wait_send()

    return pl.pallas_call(
        body, name="half_exchange", out_shape=[jax.ShapeDtypeStruct(a.shape, a.dtype) for a in shards],
        in_specs=_hbm_specs(n), out_specs=_hbm_specs(n), input_output_aliases={k: k for k in range(n)},
        scratch_shapes=[pltpu.SemaphoreType.DMA((n,)), pltpu.SemaphoreType.DMA((n,))])(*shards)


def _all_reduce_small(packet):
    rows, width = packet.shape
    n_dev = 8

    def body(x_ref, out_ref, gath, send_sems, recv_sems):
        x, y, c = _place()
        me, sibling = (x, y, c), (x, y, 1 - c)
        chips = _other_chips(x, y)

        def slot(px, py, pc):
            return gath.at[pl.ds((4 * px + 2 * py + pc) * rows, rows), :]

        def copy(k, block, to, src=None):
            return pltpu.make_async_remote_copy(
                src_ref=slot(*block) if src is None else src, dst_ref=slot(*block), send_sem=send_sems.at[k],
                recv_sem=recv_sems.at[k], device_id=to, device_id_type=MESH)

        first = [copy(0, me, sibling, src=x_ref)]
        first += [copy(1 + j, me, (*chip, c), src=x_ref) for j, chip in enumerate(chips)]
        for cp in first:
            cp.start()
        gath[pl.ds((4 * x + 2 * y + c) * rows, rows), :] = x_ref[...]
        passed = [copy(4 + j, (*chip, c), sibling) for j, chip in enumerate(chips)]
        for j, chip in enumerate(chips):
            copy(1 + j, (*chip, c), me).wait_recv()
            passed[j].start()
        copy(0, sibling, me).wait_recv()
        for j, chip in enumerate(chips):
            copy(4 + j, (*chip, 1 - c), me).wait_recv()
        for cp in first + passed:
            cp.wait_send()
        acc = gath[0:rows, :]
        for d in range(1, n_dev):
            acc = acc + gath[d * rows:(d + 1) * rows, :]
        out_ref[...] = acc

    return pl.pallas_call(
        body, name="all_reduce_small", out_shape=jax.ShapeDtypeStruct((rows, width), F32),
        in_specs=[pl.BlockSpec(memory_space=pltpu.VMEM)], out_specs=pl.BlockSpec(memory_space=pltpu.VMEM),
        scratch_shapes=[pltpu.VMEM((n_dev * rows, width), F32), pltpu.SemaphoreType.DMA((7,)),
                        pltpu.SemaphoreType.DMA((7,))])(packet)


def _flat_rows(parts, width, row_multiple):
    flat = jnp.concatenate([p.astype(F32).reshape(-1) for p in parts])
    rows = -(-flat.shape[0] // width)
    rows = -(-rows // row_multiple) * row_multiple
    return jnp.pad(flat, (0, rows * width - flat.shape[0])).reshape(rows, width)


def _unflatten(flat2d, shapes):
    flat = flat2d.reshape(-1)
    out, off = [], 0
    for shp in shapes:
        n = 1
        for dim in shp:
            n *= dim
        out.append(flat[off:off + n].reshape(shp))
        off += n
    return out


def _core_and_chip():
    x, y, c = _place()
    return jnp.stack([c, 2 * x + y]).astype(jnp.int32)


def _pair_sums(chip_major, names, call_name):
    views = [g.reshape(N_CHIPS, 2, g.shape[1] // 2, g.shape[2]) for g in chip_major]
    recv = _pair_exchange(views, call_name)
    sel = _core_and_chip()
    return [_pair_sum(v, r, sel, "pair_sum_" + nm) for v, r, nm in zip(views, recv, names)]


def _finish_grads(pairs, got, names):
    sel = _core_and_chip()
    return _join_halves([_chip_sum(p, g, sel, "chip_sum_" + nm) for (p, _), g, nm in zip(pairs, got, names)])


def kernel(x, g_mix, w_in, b_f, w_conv, g_conv_out, g_attn_out, w_o, g_ffn, w_up, w_ffn_conv, w_down, g_final, loss_target, m_g_mix, m_w_in, m_b_f, m_w_conv, m_g_conv_out, m_g_attn_out, m_w_o, m_g_ffn, m_w_up, m_w_ffn_conv, m_w_down, m_g_final, v_g_mix, v_w_in, v_b_f, v_w_conv, v_g_conv_out, v_g_attn_out, v_w_o, v_g_ffn, v_w_up, v_w_ffn_conv, v_w_down, v_g_final):
    s = x.shape[1]
    x0 = x[0]
    target = loss_target[0]
    d = D_MODEL
    x_pos, y_pos, _ = _place()
    my_chip = 2 * x_pos + y_pos

    (c_in,) = _all_gather_weights([w_in[0].astype(BF16)], [])
    w_in_full = jnp.concatenate([c_in[j] for j in range(N_CHIPS)], axis=1)
    c3 = 3 * CONV_CH
    w_a, w_b = w_in_full[:, :c3], w_in_full[:, c3:c3 + 3 * ATTN_W]
    w_c = jnp.pad(w_in_full[:, c3 + 3 * ATTN_W:], ((0, 0), (0, LANES - N_HEADS)))
    w_q, w_k, w_v = (w_b[:, i * ATTN_W:(i + 1) * ATTN_W] for i in range(3))
    b_pad = jnp.pad(b_f, ((0, 0), (0, LANES - N_HEADS)))

    z_a, qkv, f_log, h1, (c_o, c_up, c_conv, c_ffn) = _in_proj(
        x0, g_mix, w_a, w_b, w_c, [w_o[0].astype(BF16), w_up[0].astype(BF16)], [w_conv[0], w_ffn_conv[0]])
    fb = _gate_fwd(f_log, b_pad)
    qx, kx, kxt, vx, vt, bounds, (c_down,) = _attn_prep(qkv, fb, [w_down[0].astype(BF16)])
    w_o_full = c_o.reshape(d, d)
    w_down_full = c_down.reshape(D_FF, d)
    w_conv_full = jnp.concatenate([c_conv[j] for j in range(N_CHIPS)], axis=1)
    w_ffn_full = jnp.concatenate([c_ffn[j] for j in range(N_CHIPS)], axis=1)
    n_up = c_up.shape[2]
    first_blk, last_blk = _key_block_ranges(bounds)
    o_attn, lse = _attn_fwd_t(qx, kx, vt, first_blk)
    mix = _mixer_fwd(z_a, o_attn, w_conv_full, g_conv_out, g_attn_out)
    x2, h2 = _mm("nn", [mix], [w_o_full], F32, 512, d, "out_proj", add=x0, norm_g=g_ffn)
    up = _mm("nn", [h2], [c_up], F32, TM_MM, n_up, "up_proj", b_chips=True)
    act, u_conv = _ffn_act_fwd(up, w_ffn_full)
    dx3, dx3_b, loss_row, gg_final = _down_proj_loss(act, w_down_full, x2, target, g_final.reshape(1, d))

    dact = _mm("nt", [dx3_b], [w_down_full], F32, TM_MM, 1408, "d_act")
    gw_down = _mm_tn(act, dx3_b, 1408, 1024, "gw_down")
    dup, gwf_lin, gwf_gate = _ffn_act_bwd(up, u_conv, dact, w_ffn_full)
    dh2 = _mm("nt", [(dup, j // 2, j % 2, n_up) for j in range(N_CHIPS)], [(c_up, j) for j in range(N_CHIPS)],
              F32, TM_MM, 512, "d_h2")
    gw_up = _mm_tn(h2, dup, 1024, n_up, "gw_up", out_chips=True)
    dx2, dx2_b, gg_ffn = _rms_bwd(x2, dh2, g_ffn, dx3, "rms_ffn_bwd", True)
    dmix = _mm("nt", [dx2_b], [w_o_full], F32, TM_MM, 512, "d_mix")
    gw_o = _mm_tn(mix, dx2_b, 1024, 1024, "gw_o")
    early = _pair_sums([gw_o.reshape(N_CHIPS, d // N_CHIPS, d), gw_up, gw_down.reshape(N_CHIPS, D_FF // N_CHIPS, d)],
                       ["w_o", "w_up", "w_down"], "pair_exchange")
    dz_a, dox, gw_conv, gg_conv_out, gg_attn_out, (got_o, got_down) = _mixer_bwd(
        z_a, o_attn, dmix, w_conv_full, g_conv_out, g_attn_out, [early[0][1], early[2][1]])
    dk, dv, dfk, dqt, dfq = _attn_bwd_t(qx, dox, kx, kxt, vx, lse, last_blk)
    dq = _attn_dq_finish(dqt)
    d_f = (jnp.transpose(dfk[:, :, 0:2, :], (1, 3, 0, 2)).reshape(s, N_HEADS)
           + jnp.transpose(jnp.sum(dfq, axis=2), (1, 2, 0)).reshape(s, N_HEADS))
    df_b, gb_f = _gate_bwd(f_log, b_pad, jnp.pad(d_f, ((0, 0), (0, LANES - N_HEADS))))
    gw_a = _mm_tn(h1, dz_a, 1024, c3, "gw_in_conv")
    gw_q = _mm_tn(h1, dq, 1024, ATTN_W, "gw_in_q")
    gw_k = _mm_tn(h1, dk, 1024, ATTN_W, "gw_in_k")
    gw_v = _mm_tn(h1, dv, 1024, ATTN_W, "gw_in_v")
    gw_c = _mm_tn(h1, df_b, 1024, LANES, "gw_in_gate")
    gw_in = jnp.concatenate([gw_a, gw_q, gw_k, gw_v, gw_c[:, :N_HEADS]], axis=1)
    n_in = IN_COLS // N_CHIPS
    gw_in = jnp.stack([gw_in[:, j * n_in:(j + 1) * n_in] for j in range(N_CHIPS)])
    late = _pair_sums([gw_in], ["w_in"], "pair_exchange_w_in")
    dh1, (got_up, got_in) = _mm("nt", [dz_a, dq, dk, dv, df_b], [w_a, w_q, w_k, w_v, w_c], F32, TM_MM, 512, "d_h1",
                                scatter=[early[1][1], late[0][1]])
    grad_x, gg_mix = _rms_bwd(x0, dh1, g_mix, dx2, "rms_mix_bwd", False)
    g_w_in, g_w_o, g_w_up, g_w_down = _finish_grads(late + early, [got_in, got_o, got_up, got_down],
                                                    ["w_in", "w_o", "w_up", "w_down"])

    gw_ffn = jnp.concatenate([gwf_lin, gwf_gate], axis=1)
    small_parts = [gg_mix, gg_conv_out, gg_attn_out, gg_ffn, gg_final, gb_f[:, :N_HEADS], loss_row[:, 0:1], gw_conv,
                   gw_ffn]
    small_shapes = [a.shape for a in small_parts]
    tot = _unflatten(_all_reduce_small(_flat_rows(small_parts, d, 8)), small_shapes)
    g_g_mix, g_g_conv_out, g_g_attn_out, g_g_ffn, g_g_final, g_b_f, loss_sum, g_conv_full, g_ffn_full = tot
    loss = loss_sum[0, 0]
    g_g_final = g_g_final[0]
    g_w_conv = lax.dynamic_slice_in_dim(g_conv_full, my_chip * (CONV_CH // N_CHIPS), CONV_CH // N_CHIPS, axis=1)
    g_w_ffn = lax.dynamic_slice_in_dim(g_ffn_full, my_chip * n_up, n_up, axis=1)

    def adam_big(w, g, m, v, name):
        dl, nm, nv = _adamw(w[0], g, m[0], v[0], name)
        return dl[None], nm[None], nv[None]

    u_w_in = adam_big(w_in, g_w_in, m_w_in, v_w_in, "adam_w_in")
    u_w_o = adam_big(w_o, g_w_o, m_w_o, v_w_o, "adam_w_o")
    u_w_up = adam_big(w_up, g_w_up, m_w_up, v_w_up, "adam_w_up")
    u_w_down = adam_big(w_down, g_w_down, m_w_down, v_w_down, "adam_w_down")

    small_w = [g_mix, b_f, g_conv_out, g_attn_out, g_ffn, g_final, w_conv, w_ffn_conv]
    small_g = [g_g_mix, g_b_f, g_g_conv_out, g_g_attn_out, g_g_ffn, g_g_final, g_w_conv, g_w_ffn]
    small_m = [m_g_mix, m_b_f, m_g_conv_out, m_g_attn_out, m_g_ffn, m_g_final, m_w_conv, m_w_ffn_conv]
    small_v = [v_g_mix, v_b_f, v_g_conv_out, v_g_attn_out, v_g_ffn, v_g_final, v_w_conv, v_w_ffn_conv]
    shapes = [a.shape for a in small_w]
    pack = lambda arrs: _flat_rows(arrs, LANES, 8)
    sd, sm, sv = _adamw(pack(small_w), pack(small_g), pack(small_m), pack(small_v), "adam_small")
    sd, sm, sv = _unflatten(sd, shapes), _unflatten(sm, shapes), _unflatten(sv, shapes)
    (d_g_mix, d_b_f, d_g_conv_out, d_g_attn_out, d_g_ffn, d_g_final, d_w_conv, d_w_ffn) = sd
    (nm_g_mix, nm_b_f, nm_g_conv_out, nm_g_attn_out, nm_g_ffn, nm_g_final, nm_w_conv, nm_w_ffn) = sm
    (nv_g_mix, nv_b_f, nv_g_conv_out, nv_g_attn_out, nv_g_ffn, nv_g_final, nv_w_conv, nv_w_ffn) = sv

    grads = (g_g_mix, g_w_in[None], g_b_f, g_w_conv[None], g_g_conv_out, g_g_attn_out, g_w_o[None], g_g_ffn,
             g_w_up[None], g_w_ffn[None], g_w_down[None], g_g_final)
    deltas = (d_g_mix, u_w_in[0], d_b_f, d_w_conv, d_g_conv_out, d_g_attn_out, u_w_o[0], d_g_ffn, u_w_up[0],
              d_w_ffn, u_w_down[0], d_g_final)
    new_m = (nm_g_mix, u_w_in[1], nm_b_f, nm_w_conv, nm_g_conv_out, nm_g_attn_out, u_w_o[1], nm_g_ffn, u_w_up[1],
             nm_w_ffn, u_w_down[1], nm_g_final)
    new_v = (nv_g_mix, u_w_in[2], nv_b_f, nv_w_conv, nv_g_conv_out, nv_g_attn_out, u_w_o[2], nv_g_ffn, u_w_up[2],
             nv_w_ffn, u_w_down[2], nv_g_final)
    return (loss, grad_x[None], *grads, *deltas, *new_m, *new_v)
```

```python
import jax
import jax.numpy as jnp
from jax import lax
from jax.experimental import pallas as pl
from jax.experimental.pallas import tpu as pltpu

F32, BF16 = jnp.float32, jnp.bfloat16
MESH = pl.DeviceIdType.MESH

D_MODEL = 1024
CONV_CH = 512
ATTN_W = 512
N_HEADS = 8
HEAD_DIM = 64
N_PAIRS = N_HEADS // 2
D_FF = 2816
IN_COLS = 3 * CONV_CH + 3 * ATTN_W + N_HEADS
EPS = 1e-6
Q_SCALE = 0.125
EXP_ZERO = 88.0
N_CHIPS = 4
LANES = 128
HALO = 8
HALO_BF16 = 2 * HALO

ADAM_LR, ADAM_B1, ADAM_B2, ADAM_EPS, ADAM_WD, ADAM_STEP = 0.001, 0.9, 0.999, 1e-08, 0.01, 10

TM_ROWS = 512
TM_MM = 1024
TK_TN = 1024
TQ = 512
TM_FFN = 256
TN_FFN = 1408
VMEM_LIMIT = 52 * 2**20


def _cp(sem, vmem=VMEM_LIMIT):
    return pltpu.CompilerParams(dimension_semantics=sem, vmem_limit_bytes=vmem)


def _bf(a):
    return a if a.dtype == BF16 else a.astype(BF16)


def _mm(mode, a_list, b_list, out_dtype, tm, tn, name, add=None, b_chips=False, scatter=(), norm_g=None):
    n_p = len(a_list)
    a0 = a_list[0]
    m_dim = a0[0].shape[1] if isinstance(a0, tuple) else a0.shape[0]
    b0 = b_list[0]
    if b_chips:
        n_dim = b0.shape[0] * b0.shape[2]
        assert tn == b0.shape[2] and mode == "nn"
    else:
        b0 = b0[0][b0[1]] if isinstance(b0, tuple) else b0
        n_dim = b0.shape[1 if mode == "nn" else 0]
    tm, tn = min(tm, m_dim), min(tn, n_dim)
    assert m_dim % tm == 0 and n_dim % tn == 0
    dims = (((1,), (0,)), ((), ())) if mode == "nn" else (((1,), (1,)), ((), ()))
    in_specs, args = [], []
    for a in a_list:
        if isinstance(a, tuple):
            arr, lead, col, width = a
            in_specs.append(pl.BlockSpec((None, tm, width), lambda m, n, lead=lead, col=col: (lead, m, col)))
        else:
            arr = a
            in_specs.append(pl.BlockSpec((tm, a.shape[1]), lambda m, n: (m, 0)))
        args.append(arr)
    for b in b_list:
        if b_chips:
            arr = b
            in_specs.append(pl.BlockSpec((None, b.shape[1], tn), lambda m, n: (n, 0, 0)))
        elif isinstance(b, tuple):
            arr, lead = b
            if mode == "nn":
                in_specs.append(pl.BlockSpec((None, arr.shape[1], tn), lambda m, n, lead=lead: (lead, 0, n)))
            else:
                in_specs.append(pl.BlockSpec((None, tn, arr.shape[2]), lambda m, n, lead=lead: (lead, n, 0)))
        elif mode == "nn":
            arr = b
            in_specs.append(pl.BlockSpec((b.shape[0], tn), lambda m, n: (0, n)))
        else:
            arr = b
            in_specs.append(pl.BlockSpec((tn, b.shape[1]), lambda m, n: (n, 0)))
        args.append(arr)
    if add is not None:
        in_specs.append(pl.BlockSpec((tm, tn), lambda m, n: (m, n)))
        args.append(add)
    if norm_g is not None:
        assert tn == n_dim and not scatter
        in_specs.append(pl.BlockSpec((1, tn), lambda m, n: (0, 0)))
        args.append(norm_g)

    n_in = len(args)
    n_sc = len(scatter)
    grid = (m_dim // tm, n_dim // tn)

    def body(*refs):
        o_ref = refs[n_in + n_sc]
        if n_sc:
            finish = _hosted_scatter(refs[n_in:n_in + n_sc], refs[n_in + n_sc + 1:n_in + 2 * n_sc + 1],
                                     refs[n_in + 2 * n_sc + 1:], pl.program_id(0) * grid[1] + pl.program_id(1),
                                     grid[0] * grid[1])
        acc = None
        for i in range(n_p):
            d = lax.dot_general(_bf(refs[i][...]), _bf(refs[n_p + i][...]), dims,
                                preferred_element_type=F32)
            acc = d if acc is None else acc + d
        if add is not None:
            acc = refs[2 * n_p][...] + acc
        o_ref[...] = acc.astype(out_dtype)
        if norm_g is not None:
            refs[n_in + 1][...] = (acc * _rstd(acc) * refs[n_in - 1][...]).astype(BF16)
        if n_sc:
            finish()

    main_spec = pl.BlockSpec((tm, tn), lambda m, n: (m, n))
    main_shape = jax.ShapeDtypeStruct((m_dim, n_dim), out_dtype)
    if norm_g is not None:
        return pl.pallas_call(body, name=name, grid=grid, in_specs=in_specs, out_specs=[main_spec, main_spec],
                              out_shape=[main_shape, jax.ShapeDtypeStruct((m_dim, n_dim), BF16)],
                              compiler_params=_cp(("parallel", "parallel")))(*args)
    if not n_sc:
        return pl.pallas_call(body, name=name, grid=grid, in_specs=in_specs, out_specs=main_spec,
                              out_shape=main_shape, compiler_params=_cp(("parallel", "parallel")))(*args)
    got_shapes, sems = _scatter_operands(scatter)
    outs = pl.pallas_call(
        body, name=name, grid=grid, in_specs=in_specs + _hbm_specs(n_sc), out_specs=[main_spec] + _hbm_specs(n_sc),
        out_shape=[main_shape] + got_shapes, scratch_shapes=sems,
        compiler_params=_cp(("arbitrary", "arbitrary")))(*args, *scatter)
    return outs[0], outs[1:]


def _mm_tn(a, b, tm, tn, name, out_chips=False):
    k_dim, m_dim = a.shape
    n_dim = b.shape[-1] * (b.shape[0] if b.ndim == 3 else 1)
    tm, tn, tk = min(tm, m_dim), min(tn, b.shape[-1]), min(TK_TN, k_dim)
    assert m_dim % tm == 0 and b.shape[-1] % tn == 0 and k_dim % tk == 0
    per = b.shape[-1] // tn
    if b.ndim == 3:
        b_spec = pl.BlockSpec((None, tk, tn), lambda m, n, k: (n // per, k, n % per))
    else:
        b_spec = pl.BlockSpec((tk, tn), lambda m, n, k: (k, n))

    def body(a_ref, b_ref, o_ref):
        @pl.when(pl.program_id(2) == 0)
        def _():
            o_ref[...] = jnp.zeros_like(o_ref)
        o_ref[...] += lax.dot_general(_bf(a_ref[...]), _bf(b_ref[...]), (((0,), (0,)), ((), ())),
                                      preferred_element_type=F32)

    return pl.pallas_call(
        body, name=name, grid=(m_dim // tm, n_dim // tn, k_dim // tk),
        in_specs=[pl.BlockSpec((tk, tm), lambda m, n, k: (k, m)), b_spec],
        out_specs=(pl.BlockSpec((None, tm, tn), lambda m, n, k: (n, m, 0)) if out_chips
                   else pl.BlockSpec((tm, tn), lambda m, n, k: (m, n))),
        out_shape=jax.ShapeDtypeStruct((n_dim // tn, m_dim, tn) if out_chips else (m_dim, n_dim), F32),
        compiler_params=_cp(("parallel", "parallel", "arbitrary")))(a, b)


def _rstd(x):
    return lax.rsqrt(jnp.mean(x * x, axis=-1, keepdims=True) + EPS)


def _rms_bwd(x, dh, g, dres, name, with_bf16, scatter=()):
    s, d = x.shape
    tm = min(TM_ROWS, s)
    n_sc = len(scatter)
    n_out = 3 if with_bf16 else 2
    got_shapes, sems = _scatter_operands(scatter) if n_sc else ([], [])

    def body(x_ref, dh_ref, g_ref, dres_ref, *rest):
        dx_ref, gg_ref = rest[n_sc], rest[n_sc + n_out - 1]
        i = pl.program_id(0)
        if n_sc:
            finish = _hosted_scatter(rest[:n_sc], rest[n_sc + n_out:2 * n_sc + n_out], rest[2 * n_sc + n_out:], i,
                                     s // tm)

        @pl.when(i == 0)
        def _():
            gg_ref[...] = jnp.zeros_like(gg_ref)

        xv = x_ref[...]
        xn = xv * _rstd(xv)
        dhv = dh_ref[...]
        gg_ref[...] += jnp.sum(dhv * xn, axis=0, keepdims=True)
        t = dhv * g_ref[...]
        dx = dres_ref[...] + _rstd(xv) * (t - xn * jnp.mean(t * xn, axis=-1, keepdims=True))
        dx_ref[...] = dx
        if with_bf16:
            rest[n_sc + 1][...] = dx.astype(BF16)
        if n_sc:
            finish()

    row = pl.BlockSpec((tm, d), lambda i: (i, 0))
    vec = pl.BlockSpec((1, d), lambda i: (0, 0))
    out_specs = [row] + ([row] if with_bf16 else []) + [vec] + _hbm_specs(n_sc)
    out_shape = ([jax.ShapeDtypeStruct((s, d), F32)] + ([jax.ShapeDtypeStruct((s, d), BF16)] if with_bf16 else [])
                 + [jax.ShapeDtypeStruct((1, d), F32)] + got_shapes)
    outs = pl.pallas_call(
        body, name=name, grid=(s // tm,), in_specs=[row, row, vec, row] + _hbm_specs(n_sc), out_specs=out_specs,
        out_shape=out_shape, scratch_shapes=sems, compiler_params=_cp(("arbitrary",)))(x, dh, g, dres, *scatter)
    return tuple(outs[:n_out]) + ((outs[n_out:],) if n_sc else ())


def _down_proj_loss(act, w_down, x2, target, g):
    s, d = x2.shape
    tm = min(TM_ROWS, s)
    k_dim = act.shape[1]

    def body(a_ref, w_ref, x_ref, t_ref, g_ref, dx_ref, dxb_ref, loss_ref, gg_ref):
        @pl.when(pl.program_id(0) == 0)
        def _():
            gg_ref[...] = jnp.zeros_like(gg_ref)
            loss_ref[...] = jnp.zeros_like(loss_ref)

        xv = x_ref[...] + lax.dot_general(a_ref[...], w_ref[...], (((1,), (0,)), ((), ())),
                                          preferred_element_type=F32)
        r = _rstd(xv)
        xn = xv * r
        gv = g_ref[...]
        err = xn * gv - t_ref[...]
        loss_ref[...] += 0.5 * jnp.sum(jnp.mean(err * err, axis=-1, keepdims=True), axis=0, keepdims=True)
        dy = err * (1.0 / d)
        gg_ref[...] += jnp.sum(dy * xn, axis=0, keepdims=True)
        t = dy * gv
        dx = r * (t - xn * jnp.mean(t * xn, axis=-1, keepdims=True))
        dx_ref[...] = dx
        dxb_ref[...] = dx.astype(BF16)

    row = pl.BlockSpec((tm, d), lambda i: (i, 0))
    vec = pl.BlockSpec((1, d), lambda i: (0, 0))
    return pl.pallas_call(
        body, name="down_proj_loss", grid=(s // tm,),
        in_specs=[pl.BlockSpec((tm, k_dim), lambda i: (i, 0)), pl.BlockSpec((k_dim, d), lambda i: (0, 0)), row, row,
                  vec],
        out_specs=[row, row, pl.BlockSpec((1, LANES), lambda i: (0, 0)), vec],
        out_shape=[jax.ShapeDtypeStruct((s, d), F32), jax.ShapeDtypeStruct((s, d), BF16),
                   jax.ShapeDtypeStruct((1, LANES), F32), jax.ShapeDtypeStruct((1, d), F32)],
        compiler_params=_cp(("arbitrary",)))(act, w_down, x2, target, g)


def _prev_halo_spec(tm, width, col):
    return pl.BlockSpec((HALO, width), lambda i, *_: (jnp.maximum(i * (tm // HALO) - 1, 0), col))


def _next_halo_spec(tm, width, col, s):
    return pl.BlockSpec((HALO, width), lambda i, *_: (jnp.minimum((i + 1) * (tm // HALO), s // HALO - 1), col))


def _shift_down(x, k):
    return pltpu.roll(x, k, 0)


def _shift_up(x, k):
    return pltpu.roll(x, x.shape[0] - k, 0)


def _conv_taps(x_ext, w):
    return w[0:1, :] * _shift_down(x_ext, 2) + w[1:2, :] * _shift_down(x_ext, 1) + w[2:3, :] * x_ext


def _conv_taps_t(d_ext, w):
    return w[2:3, :] * d_ext + w[1:2, :] * _shift_up(d_ext, 1) + w[0:1, :] * _shift_up(d_ext, 2)


def _mixer_fwd(z_a, o_attn, w_conv, g_conv_out, g_attn_out):
    s = z_a.shape[0]
    c = CONV_CH
    tm = min(TM_ROWS, s)

    def body(gb_ref, gc_ref, xc_ref, gcp_ref, xcp_ref, o_ref, w_ref, gco_ref, gao_ref, mix_ref):
        i = pl.program_id(0)
        cx = gc_ref[...] * xc_ref[...]
        cx_prev = jnp.where(i > 0, gcp_ref[...] * xcp_ref[...], 0.0)
        conv = _conv_taps(jnp.concatenate([cx_prev, cx], axis=0), w_ref[...])[HALO:]
        y = gb_ref[...] * conv
        mix_ref[:, 0:c] = (y * _rstd(y) * gco_ref[...]).astype(BF16)
        o = o_ref[...]
        mix_ref[:, c:2 * c] = (o * _rstd(o) * gao_ref[...]).astype(BF16)

    col = lambda j: pl.BlockSpec((tm, c), lambda i: (i, j))
    vec = pl.BlockSpec((1, c), lambda i: (0, 0))
    return pl.pallas_call(
        body, name="mixer_fwd", grid=(s // tm,),
        in_specs=[col(0), col(1), col(2), _prev_halo_spec(tm, c, 1), _prev_halo_spec(tm, c, 2), col(0),
                  pl.BlockSpec((3, c), lambda i: (0, 0)), vec, vec],
        out_specs=pl.BlockSpec((tm, 2 * c), lambda i: (i, 0)),
        out_shape=jax.ShapeDtypeStruct((s, 2 * c), BF16),
        compiler_params=_cp(("parallel",)))(z_a, z_a, z_a, z_a, z_a, o_attn, w_conv, g_conv_out, g_attn_out)


def _mixer_bwd(z_a, o_attn, dmix, w_conv, g_conv_out, g_attn_out, scatter):
    s = z_a.shape[0]
    c = CONV_CH
    tm = min(TM_ROWS, s)
    n_blk = s // tm
    n_sc = len(scatter)
    got_shapes, sems = _scatter_operands(scatter)

    def body(gb_ref, gc_ref, xc_ref, gcp_ref, xcp_ref, gbn_ref, gcn_ref, xcn_ref, o_ref, dnc_ref, dncn_ref, dna_ref,
             w_ref, gco_ref, gao_ref, *rest):
        dz_ref, dox_ref, gw_ref, ggco_ref, ggao_ref = rest[n_sc:n_sc + 5]
        i = pl.program_id(0)
        finish = _hosted_scatter(rest[:n_sc], rest[n_sc + 5:2 * n_sc + 5], rest[2 * n_sc + 5:], i, n_blk)

        @pl.when(i == 0)
        def _():
            gw_ref[...] = jnp.zeros_like(gw_ref)
            ggco_ref[...] = jnp.zeros_like(ggco_ref)
            ggao_ref[...] = jnp.zeros_like(ggao_ref)

        w = w_ref[...]
        zeros = jnp.zeros((HALO, c), F32)
        gb_e = jnp.concatenate([zeros, gb_ref[...], gbn_ref[...]], axis=0)
        cx_prev = jnp.where(i > 0, gcp_ref[...] * xcp_ref[...], 0.0)
        gc_e = jnp.concatenate([zeros, gc_ref[...], gcn_ref[...]], axis=0)
        xc_e = jnp.concatenate([zeros, xc_ref[...], xcn_ref[...]], axis=0)
        cx_e = jnp.concatenate([cx_prev, gc_ref[...] * xc_ref[...], gcn_ref[...] * xcn_ref[...]], axis=0)
        dn_next = jnp.where(i < n_blk - 1, dncn_ref[...], 0.0)
        dn_e = jnp.concatenate([zeros, dnc_ref[...], dn_next], axis=0)

        cx_1, cx_2 = _shift_down(cx_e, 1), _shift_down(cx_e, 2)
        conv_e = w[0:1, :] * cx_2 + w[1:2, :] * cx_1 + w[2:3, :] * cx_e
        y_e = gb_e * conv_e
        r_e = _rstd(y_e)
        yn_e = y_e * r_e
        t_e = dn_e * gco_ref[...]
        dy_e = r_e * (t_e - yn_e * jnp.mean(t_e * yn_e, axis=-1, keepdims=True))
        dconv_e = dy_e * gb_e
        dcx_e = _conv_taps_t(dconv_e, w)
        blk = slice(HALO, HALO + tm)
        dz_ref[:, 0:c] = (dy_e * conv_e)[blk].astype(BF16)
        dz_ref[:, c:2 * c] = (dcx_e * xc_e)[blk].astype(BF16)
        dz_ref[:, 2 * c:3 * c] = (dcx_e * gc_e)[blk].astype(BF16)
        ggco_ref[...] += jnp.sum((dn_e * yn_e)[blk], axis=0, keepdims=True)
        dconv = dconv_e[blk]
        gw_ref[0:1, :] += jnp.sum(dconv * cx_2[blk], axis=0, keepdims=True)
        gw_ref[1:2, :] += jnp.sum(dconv * cx_1[blk], axis=0, keepdims=True)
        gw_ref[2:3, :] += jnp.sum(dconv * cx_e[blk], axis=0, keepdims=True)

        o = o_ref[...]
        ra = _rstd(o)
        on = o * ra
        dna = dna_ref[...]
        ggao_ref[...] += jnp.sum(dna * on, axis=0, keepdims=True)
        ta = dna * gao_ref[...]
        do = ra * (ta - on * jnp.mean(ta * on, axis=-1, keepdims=True))
        prod = do * o
        lane = lax.broadcasted_iota(jnp.int32, (tm, LANES), 1)
        head_a = lane < HEAD_DIM
        for p in range(N_PAIRS):
            cols = slice(p * LANES, (p + 1) * LANES)
            pb, dob = prod[:, cols], do[:, cols]
            for hh in range(2):
                sel = head_a if hh == 0 else jnp.logical_not(head_a)
                delta = jnp.sum(jnp.where(sel, pb, 0.0), axis=-1, keepdims=True)
                neg3 = _split3(-delta)
                do_h = pltpu.roll(dob, HEAD_DIM, 1) if hh else dob
                dox_ref[2 * p + hh] = _aug(do_h, lane, neg3).astype(BF16)
        finish()

    col = lambda j: pl.BlockSpec((tm, c), lambda i: (i, j))
    vec = pl.BlockSpec((1, c), lambda i: (0, 0))
    w3 = pl.BlockSpec((3, c), lambda i: (0, 0))
    outs = pl.pallas_call(
        body, name="mixer_bwd", grid=(n_blk,),
        in_specs=[col(0), col(1), col(2), _prev_halo_spec(tm, c, 1), _prev_halo_spec(tm, c, 2),
                  _next_halo_spec(tm, c, 0, s), _next_halo_spec(tm, c, 1, s), _next_halo_spec(tm, c, 2, s),
                  col(0), col(0), _next_halo_spec(tm, c, 0, s), col(1), w3, vec, vec] + _hbm_specs(n_sc),
        out_specs=[pl.BlockSpec((tm, 3 * c), lambda i: (i, 0)),
                   pl.BlockSpec((N_HEADS, tm, LANES), lambda i: (0, i, 0)), w3, vec, vec] + _hbm_specs(n_sc),
        out_shape=[jax.ShapeDtypeStruct((s, 3 * c), BF16), jax.ShapeDtypeStruct((N_HEADS, s, LANES), BF16),
                   jax.ShapeDtypeStruct((3, c), F32), jax.ShapeDtypeStruct((1, c), F32),
                   jax.ShapeDtypeStruct((1, c), F32)] + got_shapes,
        scratch_shapes=sems, compiler_params=_cp(("arbitrary",)))(
            z_a, z_a, z_a, z_a, z_a, z_a, z_a, z_a, o_attn, dmix, dmix, dmix, w_conv, g_conv_out, g_attn_out,
            *scatter)
    return tuple(outs[:5]) + (outs[5:],)


def _gate_fwd(f, b_pad):
    s = f.shape[0]
    tm = min(TQ, s)

    def body(f_ref, b_ref, fb_ref, carry):
        @pl.when(pl.program_id(0) == 0)
        def _():
            carry[...] = jnp.zeros_like(carry)

        z = f_ref[...] + b_ref[...]
        x = jnp.minimum(z, 0.0) - jnp.log1p(jnp.exp(-jnp.abs(z)))
        row = lax.broadcasted_iota(jnp.int32, (tm, LANES), 0)
        sh = 1
        while sh < tm:
            x = x + jnp.where(row >= sh, _shift_down(x, sh), 0.0)
            sh *= 2
        x = x + carry[0:1, :]
        carry[...] = jnp.broadcast_to(x[tm - 1:tm, :], carry.shape)
        head_a = lax.broadcasted_iota(jnp.int32, (tm, LANES), 1) < HEAD_DIM
        for p in range(N_PAIRS):
            fa = jnp.broadcast_to(x[:, 2 * p:2 * p + 1], (tm, LANES))
            fbv = jnp.broadcast_to(x[:, 2 * p + 1:2 * p + 2], (tm, LANES))
            fb_ref[:, p * LANES:(p + 1) * LANES] = jnp.where(head_a, fa, fbv)

    return pl.pallas_call(
        body, name="gate_fwd", grid=(s // tm,),
        in_specs=[pl.BlockSpec((tm, LANES), lambda i: (i, 0)), pl.BlockSpec((1, LANES), lambda i: (0, 0))],
        out_specs=pl.BlockSpec((tm, N_PAIRS * LANES), lambda i: (i, 0)),
        out_shape=jax.ShapeDtypeStruct((s, N_PAIRS * LANES), F32),
        scratch_shapes=[pltpu.VMEM((HALO, LANES), F32)],
        compiler_params=_cp(("arbitrary",)))(f, b_pad)


def _gate_bwd(f, b_pad, d_f):
    s = f.shape[0]
    tm = min(TQ, s)
    n_blk = s // tm

    def body(f_ref, b_ref, d_ref, df_ref, gb_ref, carry):
        @pl.when(pl.program_id(0) == 0)
        def _():
            carry[...] = jnp.zeros_like(carry)
            gb_ref[...] = jnp.zeros_like(gb_ref)

        x = d_ref[...]
        row = lax.broadcasted_iota(jnp.int32, (tm, LANES), 0)
        sh = 1
        while sh < tm:
            x = x + jnp.where(row < tm - sh, _shift_up(x, sh), 0.0)
            sh *= 2
        x = x + carry[0:1, :]
        carry[...] = jnp.broadcast_to(x[0:1, :], carry.shape)
        z = f_ref[...] + b_ref[...]
        d = x * (1.0 / (1.0 + jnp.exp(z)))
        df_ref[...] = d.astype(BF16)
        gb_ref[...] += jnp.sum(d, axis=0, keepdims=True)

    rev = pl.BlockSpec((tm, LANES), lambda i: (n_blk - 1 - i, 0))
    vec = pl.BlockSpec((1, LANES), lambda i: (0, 0))
    return pl.pallas_call(
        body, name="gate_bwd", grid=(n_blk,), in_specs=[rev, vec, rev], out_specs=[rev, vec],
        out_shape=[jax.ShapeDtypeStruct((s, LANES), BF16), jax.ShapeDtypeStruct((1, LANES), F32)],
        scratch_shapes=[pltpu.VMEM((HALO, LANES), F32)],
        compiler_params=_cp(("arbitrary",)))(f, b_pad, d_f)


_NT = (((1,), (1,)), ((), ()))
_NN = (((1,), (0,)), ((), ()))


AUG = HEAD_DIM
NORM_MARGIN = 1.01


def _split3(x):
    hi = x.astype(BF16).astype(F32)
    r = x - hi
    mid = r.astype(BF16).astype(F32)
    lo = (r - mid).astype(BF16).astype(F32)
    return hi, mid, lo


def _aug(base, lane, vals):
    out = jnp.where(lane < AUG, base, 0.0)
    for k, v in enumerate(vals):
        out = jnp.where(lane == AUG + k, v, out)
    return out


def _attn_prep(qkv, fb, bigs):
    s = qkv.shape[0]
    tq = min(TQ, s)
    n_q = s // tq

    n = len(bigs)
    arrays, landing, sems = _gather_operands(bigs, [])

    def body(q_ref, k_ref, v_ref, fb_ref, *rest):
        qx_ref, kx_ref, kxt_ref, vx_ref, vt_ref, b_ref = rest[2 * n:2 * n + 6]
        finish = _hosted_gather((rest[:n], rest[2 * n + 6:3 * n + 6]) + tuple(rest[3 * n + 6:]), n, n,
                                pl.program_id(0), n_q)
        lane = lax.broadcasted_iota(jnp.int32, (tq, LANES), 1)
        lane8 = lax.broadcasted_iota(jnp.int32, (HALO, LANES), 1)
        head_lanes = lane < AUG
        is_lane = [lane == AUG + k for k in range(6)]
        first3 = (lane >= AUG) & (lane < AUG + 3)
        next3 = (lane >= AUG + 3) & (lane < AUG + 6)
        q_const = jnp.where(first3, -1.0, 0.0)
        k_const = jnp.where(next3, 1.0, 0.0)
        v_const = jnp.where(first3, 1.0, 0.0)
        ones_head = (lax.broadcasted_iota(jnp.int32, (LANES, LANES), 0) < HEAD_DIM).astype(BF16)
        acc = jnp.zeros((HALO, LANES), F32)
        for p in range(N_PAIRS):
            cols = slice(p * LANES, (p + 1) * LANES)
            q2, k2, v2 = (ref[:, cols].astype(F32) for ref in (q_ref, k_ref, v_ref))
            q2 = q2 * Q_SCALE
            f2 = fb_ref[:, cols]
            for hh in range(2):
                h = 2 * p + hh
                q, k, v = ((pltpu.roll(x, HEAD_DIM, 1) if hh else x) for x in (q2, k2, v2))
                f = f2 if hh else pltpu.roll(f2, HEAD_DIM, 1)
                hi, mid, lo = _split3(f)
                q_aug = jnp.where(is_lane[3], hi, jnp.where(is_lane[4], mid, jnp.where(is_lane[5], lo, q_const)))
                k_aug = jnp.where(is_lane[0], hi, jnp.where(is_lane[1], mid, jnp.where(is_lane[2], lo, k_const)))
                kx = jnp.where(head_lanes, k, k_aug)
                vx = jnp.where(head_lanes, v, v_const)
                qx_ref[h] = jnp.where(head_lanes, q, q_aug).astype(BF16)
                kx_ref[h] = kx.astype(BF16)
                vx_ref[h] = vx.astype(BF16)
                kxt_ref[h, 0] = kx.T.astype(BF16)
                vt_ref[h, 0] = vx.T.astype(BF16)
                q_sq = lax.dot_general((q * q).astype(BF16), ones_head, _NN, preferred_element_type=F32)
                k_sq = lax.dot_general((k * k).astype(BF16), ones_head, _NN, preferred_element_type=F32)
                diag = lax.dot_general((q * k).astype(BF16), ones_head, _NN, preferred_element_type=F32)
                diag = diag - jnp.sqrt(q_sq * k_sq) * (NORM_MARGIN - 1.0)
                vals = (jnp.sqrt(jnp.max(q_sq, axis=0, keepdims=True)), jnp.sqrt(jnp.max(k_sq, axis=0, keepdims=True)),
                        jnp.max(f - diag, axis=0, keepdims=True), f[tq - 1:tq, :])
                for slot, val in enumerate(vals):
                    acc = jnp.where(lane8 == slot * N_HEADS + h, val[:, AUG:AUG + 1], acc)
        b_ref[0] = acc
        finish()

    blk = lambda j: pl.BlockSpec((tq, ATTN_W), lambda i: (i, j))
    rows = pl.BlockSpec((N_HEADS, tq, LANES), lambda i: (0, i, 0))
    cols_t = pl.BlockSpec((N_HEADS, 1, LANES, tq), lambda i: (0, i, 0, 0))
    shp = jax.ShapeDtypeStruct((N_HEADS, s, LANES), BF16)
    shp_t = jax.ShapeDtypeStruct((N_HEADS, n_q, LANES, tq), BF16)
    outs = pl.pallas_call(
        body, name="attn_prep", grid=(n_q,), in_specs=[blk(0), blk(1), blk(2), blk(0)] + _hbm_specs(2 * n),
        out_specs=[rows, rows, cols_t, rows, cols_t,
                   pl.BlockSpec((1, HALO, LANES), lambda i: (i, 0, 0))] + _hbm_specs(n),
        out_shape=[shp, shp, shp_t, shp, shp_t, jax.ShapeDtypeStruct((n_q, HALO, LANES), F32)]
        + [jax.ShapeDtypeStruct(b.shape, b.dtype) for b in landing],
        input_output_aliases={4 + n + k: 6 + k for k in range(n)}, scratch_shapes=sems,
        compiler_params=_cp(("arbitrary",)))(qkv, qkv, qkv, fb, *arrays, *landing)
    return tuple(outs[:6]) + (outs[6:],)


def _key_block_ranges(bounds):
    t = bounds[:, 0, :]
    nh = N_HEADS
    a, b, c, e = t[:, 0:nh], t[:, nh:2 * nh], t[:, 2 * nh:3 * nh], t[:, 3 * nh:4 * nh]
    bound = a[:, None, :] * b[None, :, :] * NORM_MARGIN + c[:, None, :] - e[None, :, :]
    n_q = t.shape[0]
    idx = jnp.arange(n_q)
    need = jnp.logical_not(bound < -(EXP_ZERO + 2.0)) | (idx[None, :, None] >= idx[:, None, None])
    first = jnp.argmax(need, axis=1).astype(jnp.int32)
    first = jnp.min(first.reshape(n_q, N_PAIRS, 2), axis=-1)
    visits = (first[:, None, :] <= idx[None, :, None]) & (idx[None, :, None] <= idx[:, None, None])
    last = jnp.max(jnp.where(visits, idx[:, None, None], 0), axis=0).astype(jnp.int32)
    return first.T.reshape(-1), last.T.reshape(-1)


def _attn_fwd_t(qx, kx, vt, first_blk):
    _, s, _ = qx.shape
    tq = min(TQ, s)
    n_q = s // tq
    neg = -1e30

    def body(first_ref, qx_ref, kx_ref, vt_ref, o_ref, lse_ref, acc_ref, m_ref, s_even, s_odd):
        p = pl.program_id(0)
        i = pl.program_id(1)
        acc_ref[...] = jnp.zeros(acc_ref.shape, F32)
        m_ref[...] = jnp.full(m_ref.shape, neg, F32)
        key_le_query = (lax.broadcasted_iota(jnp.int32, (tq, tq), 0) <= lax.broadcasted_iota(jnp.int32, (tq, tq), 1))
        first = first_ref[p * n_q + i]

        def scores(kb, hh, dst):
            rows_k = pl.ds(pl.multiple_of(kb * tq, tq), tq)
            dst[hh] = lax.dot_general(kx_ref[hh, rows_k, :], qx_ref[hh], _NT, preferred_element_type=F32)

        def step(kb, src, nxt):
            for hh in range(2):
                st = src[hh]
                if nxt is None:
                    st = jnp.where(key_le_query, st, -jnp.inf)
                else:
                    scores(kb + 1, hh, nxt)
                m_old = m_ref[hh]
                m_new = jnp.maximum(m_old, jnp.max(st, axis=0, keepdims=True))
                m_ref[hh] = m_new
                pt = jnp.exp(st - m_new).astype(BF16)
                acc_ref[hh] = acc_ref[hh] * jnp.exp(m_old - m_new) + lax.dot_general(
                    vt_ref[hh, kb], pt, _NN, preferred_element_type=F32)

        def by_parity(kb, fn):
            @pl.when(kb % 2 == 0)
            def _():
                fn(s_even, s_odd)

            @pl.when(kb % 2 == 1)
            def _():
                fn(s_odd, s_even)

        def first_scores(src, nxt):
            scores(first, 0, src)
            scores(first, 1, src)

        def unmasked(kb, carry):
            by_parity(kb, lambda src, nxt: step(kb, src, nxt))
            return carry

        by_parity(first, first_scores)
        lax.fori_loop(first, i, unmasked, 0)
        by_parity(i, lambda src, nxt: step(i, src, None))
        outs, lses = [], []
        for hh in range(2):
            acc = acc_ref[hh]
            l = acc[AUG:AUG + 1, :]
            outs.append(acc[0:HEAD_DIM, :] / l)
            lses.append(m_ref[hh] + jnp.log(l))
        o_ref[...] = jnp.concatenate(outs, axis=0).T
        rows8 = lax.broadcasted_iota(jnp.int32, (N_HEADS, tq), 0)
        lse_ref[0, 0] = jnp.where(rows8 == 0, lses[0], jnp.where(rows8 == 1, lses[1], 0.0))

    grid_spec = pltpu.PrefetchScalarGridSpec(
        num_scalar_prefetch=1, grid=(N_PAIRS, n_q),
        in_specs=[pl.BlockSpec((2, tq, LANES), lambda p, i, first: (p, i, 0)),
                  pl.BlockSpec((2, s, LANES), lambda p, i, first: (p, 0, 0)),
                  pl.BlockSpec((2, n_q, LANES, tq), lambda p, i, first: (p, 0, 0, 0))],
        out_specs=[pl.BlockSpec((tq, LANES), lambda p, i, first: (i, p)),
                   pl.BlockSpec((1, 1, N_HEADS, tq), lambda p, i, first: (p, i, 0, 0))],
        scratch_shapes=[pltpu.VMEM((2, LANES, tq), F32), pltpu.VMEM((2, 1, tq), F32),
                        pltpu.VMEM((2, tq, tq), F32), pltpu.VMEM((2, tq, tq), F32)])
    return pl.pallas_call(
        body, name="attn_fwd", grid_spec=grid_spec,
        out_shape=[jax.ShapeDtypeStruct((s, ATTN_W), F32), jax.ShapeDtypeStruct((N_PAIRS, n_q, N_HEADS, tq), F32)],
        compiler_params=_cp(("parallel", "arbitrary")))(first_blk, qx, kx, vt)


def _attn_bwd_t(qx, dox, kx, kxt, vx, lse, last_blk):
    _, s, _ = qx.shape
    tq = min(TQ, s)
    n_q = s // tq

    def body(last_ref, qx_ref, dox_ref, lse_ref, kx_ref, kxt_ref, vx_ref, dk_ref, dv_ref, dfk_ref, dqt_ref, dfq_ref):
        p = pl.program_id(0)
        j = pl.program_id(1)

        @pl.when(j == 0)
        def _():
            dqt_ref[...] = jnp.zeros(dqt_ref.shape, F32)
            dfq_ref[...] = jnp.zeros(dfq_ref.shape, F32)

        key_le_query = (lax.broadcasted_iota(jnp.int32, (tq, tq), 0) <= lax.broadcasted_iota(jnp.int32, (tq, tq), 1))

        def step(i, carry, masked):
            rows_q = pl.ds(pl.multiple_of(i * tq, tq), tq)
            out = []
            for hh in range(2):
                dk, dv, col = carry[3 * hh:3 * hh + 3]
                q, do = qx_ref[hh, rows_q, :], dox_ref[hh, rows_q, :]
                st = lax.dot_general(kx_ref[hh], q, _NT, preferred_element_type=F32)
                pt = jnp.exp(st - lse_ref[0, i, hh:hh + 1, :])
                if masked:
                    pt = jnp.where(key_le_query, pt, 0.0)
                dst = pt * lax.dot_general(vx_ref[hh], do, _NT, preferred_element_type=F32)
                pb, dsb = pt.astype(BF16), dst.astype(BF16)
                dv = dv + lax.dot_general(pb, do, _NN, preferred_element_type=F32)
                dk = dk + lax.dot_general(dsb, q, _NN, preferred_element_type=F32)
                dqt_ref[hh, i] += lax.dot_general(kxt_ref[hh, 0], dsb, _NN, preferred_element_type=F32)
                for cb in range(tq // LANES):
                    col = col + dst[:, cb * LANES:(cb + 1) * LANES]
                dfq_ref[hh, i] += jnp.sum(dst.reshape(tq // HALO, HALO, tq), axis=0)
                out += [dk, dv, col]
            return tuple(out)

        zero = jnp.zeros((tq, LANES), F32)
        carry = step(j, (zero,) * 6, True)
        dk_a, dv_a, col_a, dk_b, dv_b, col_b = lax.fori_loop(j + 1, last_ref[p * n_q + j] + 1,
                                                             lambda i, cr: step(i, cr, False), carry)
        head_a = lax.broadcasted_iota(jnp.int32, (tq, LANES), 1) < HEAD_DIM
        dk_ref[...] = jnp.where(head_a, dk_a, pltpu.roll(dk_b, HEAD_DIM, 1)).astype(BF16)
        dv_ref[...] = jnp.where(head_a, dv_a, pltpu.roll(dv_b, HEAD_DIM, 1)).astype(BF16)
        rows8 = lax.broadcasted_iota(jnp.int32, (N_HEADS, tq), 0)
        dfk_a, dfk_b = (-jnp.sum(c.T, axis=0, keepdims=True) for c in (col_a, col_b))
        dfk_ref[0, 0] = jnp.where(rows8 == 0, dfk_a, jnp.where(rows8 == 1, dfk_b, 0.0))

    resident = pl.BlockSpec((2, s, LANES), lambda p, j, last: (p, 0, 0))
    key_rows = pl.BlockSpec((2, tq, LANES), lambda p, j, last: (p, j, 0))
    pair_out = pl.BlockSpec((tq, LANES), lambda p, j, last: (j, p))
    grid_spec = pltpu.PrefetchScalarGridSpec(
        num_scalar_prefetch=1, grid=(N_PAIRS, n_q),
        in_specs=[resident, resident, pl.BlockSpec((1, n_q, N_HEADS, tq), lambda p, j, last: (p, 0, 0, 0)),
                  key_rows, pl.BlockSpec((2, 1, LANES, tq), lambda p, j, last: (p, j, 0, 0)), key_rows],
        out_specs=[pair_out, pair_out, pl.BlockSpec((1, 1, N_HEADS, tq), lambda p, j, last: (p, j, 0, 0)),
                   pl.BlockSpec((2, n_q, LANES, tq), lambda p, j, last: (p, 0, 0, 0)),
                   pl.BlockSpec((2, n_q, HALO, tq), lambda p, j, last: (p, 0, 0, 0))])
    return pl.pallas_call(
        body, name="attn_bwd", grid_spec=grid_spec,
        out_shape=[jax.ShapeDtypeStruct((s, ATTN_W), BF16), jax.ShapeDtypeStruct((s, ATTN_W), BF16),
                   jax.ShapeDtypeStruct((N_PAIRS, n_q, N_HEADS, tq), F32),
                   jax.ShapeDtypeStruct((N_HEADS, n_q, LANES, tq), F32),
                   jax.ShapeDtypeStruct((N_HEADS, n_q, HALO, tq), F32)],
        compiler_params=_cp(("parallel", "arbitrary")))(last_blk, qx, dox, lse, kx, kxt, vx)


def _attn_dq_finish(dqt):
    _, n_q, _, tq = dqt.shape
    per = 4 if n_q % 4 == 0 else 1

    def body(dqt_ref, dq_ref):
        for b in range(per):
            a, bb = dqt_ref[0, b], dqt_ref[1, b]
            dq_ref[b * tq:(b + 1) * tq, :] = (
                jnp.concatenate([a[0:HEAD_DIM], bb[0:HEAD_DIM]], axis=0).T * Q_SCALE).astype(BF16)

    return pl.pallas_call(
        body, name="attn_dq_finish", grid=(N_PAIRS, n_q // per),
        in_specs=[pl.BlockSpec((2, per, LANES, tq), lambda p, i: (p, i, 0, 0))],
        out_specs=pl.BlockSpec((per * tq, LANES), lambda p, i: (i, p)),
        out_shape=jax.ShapeDtypeStruct((n_q * tq, ATTN_W), BF16),
        compiler_params=_cp(("parallel", "parallel")))(dqt)


def _ffn_act_fwd(up, w_ffn):
    s = up.shape[0]
    tm, tn = min(TM_FFN, s), TN_FFN
    nb = D_FF // tn

    def body(a_ref, g_ref, ap_ref, gp_ref, wa_ref, wg_ref, act_ref, u_ref):
        i = pl.program_id(1)

        def conv(blk_ref, prev_ref, w_ref):
            prev = jnp.where(i > 0, prev_ref[...], 0.0)
            return _conv_taps(jnp.concatenate([prev, blk_ref[...]], axis=0), w_ref[...])[HALO:]

        u_a, u_g = conv(a_ref, ap_ref, wa_ref), conv(g_ref, gp_ref, wg_ref)
        u_ref[0], u_ref[1] = u_a.astype(BF16), u_g.astype(BF16)
        act_ref[...] = (u_g * jax.nn.sigmoid(u_g) * u_a).astype(BF16)

    blk = lambda off: pl.BlockSpec((tm, tn), lambda n, i: (i, off + n))
    prev = lambda off: pl.BlockSpec((HALO, tn), lambda n, i: (jnp.maximum(i * (tm // HALO) - 1, 0), off + n))
    wsp = lambda off: pl.BlockSpec((3, tn), lambda n, i: (0, off + n))
    return pl.pallas_call(
        body, name="ffn_act_fwd", grid=(nb, s // tm),
        in_specs=[blk(0), blk(nb), prev(0), prev(nb), wsp(0), wsp(nb)],
        out_specs=[pl.BlockSpec((tm, tn), lambda n, i: (i, n)), pl.BlockSpec((2, tm, tn), lambda n, i: (0, i, n))],
        out_shape=[jax.ShapeDtypeStruct((s, D_FF), BF16), jax.ShapeDtypeStruct((2, s, D_FF), BF16)],
        compiler_params=_cp(("parallel", "parallel")))(up, up, up, up, w_ffn, w_ffn)


def _ffn_act_bwd(up, u, dact, w_ffn):
    s = up.shape[0]
    tm, tn = min(TM_FFN, s), TN_FFN
    nb = D_FF // tn
    n_blk = s // tm

    def body(u_ref, un_ref, a_ref, g_ref, d_ref, dn_ref, wa_ref, wg_ref, dup_ref, gwa_ref, gwg_ref):
        i = pl.program_id(1)

        @pl.when(i == 0)
        def _():
            gwa_ref[...] = jnp.zeros_like(gwa_ref)
            gwg_ref[...] = jnp.zeros_like(gwg_ref)

        ext = lambda rows, next_rows: jnp.concatenate([rows, next_rows], axis=0)
        u_a, u_g = (ext(u_ref[h].astype(F32), un_ref[h].astype(F32)[:HALO]) for h in range(2))
        d_e = ext(d_ref[...], jnp.where(i < n_blk - 1, dn_ref[...], 0.0))
        sig = jax.nn.sigmoid(u_g)
        du_a = d_e * (u_g * sig)
        du_g = d_e * u_a * (sig * (1.0 + u_g * (1.0 - sig)))
        halves = ((gwa_ref, wa_ref[...], a_ref[...], du_a), (gwg_ref, wg_ref[...], g_ref[...], du_g))
        for half, (gw_ref, w, upv, du) in enumerate(halves):
            du0, du1, du2 = du[:tm], _shift_up(du, 1)[:tm], _shift_up(du, 2)[:tm]
            dup_ref[half] = (w[2:3, :] * du0 + w[1:2, :] * du1 + w[0:1, :] * du2).astype(BF16)
            gw_ref[0:1, :] += jnp.sum(upv * du2, axis=0, keepdims=True)
            gw_ref[1:2, :] += jnp.sum(upv * du1, axis=0, keepdims=True)
            gw_ref[2:3, :] += jnp.sum(upv * du0, axis=0, keepdims=True)

    next_row = lambda halo: lambda i: jnp.minimum((i + 1) * (tm // halo), s // halo - 1)
    blk = lambda off: pl.BlockSpec((tm, tn), lambda n, i: (i, off + n))
    wsp = lambda off: pl.BlockSpec((3, tn), lambda n, i: (0, off + n))
    pair = lambda rows, row_of: pl.BlockSpec((2, rows, tn), lambda n, i: (0, row_of(i), n))
    return pl.pallas_call(
        body, name="ffn_act_bwd", grid=(nb, n_blk),
        in_specs=[pair(tm, lambda i: i), pair(HALO_BF16, next_row(HALO_BF16)), blk(0), blk(nb), blk(0),
                  pl.BlockSpec((HALO, tn), lambda n, i: (next_row(HALO)(i), n)), wsp(0), wsp(nb)],
        out_specs=[pair(tm, lambda i: i), wsp(0), wsp(0)],
        out_shape=[jax.ShapeDtypeStruct((2, s, D_FF), BF16),
                   jax.ShapeDtypeStruct((3, D_FF), F32), jax.ShapeDtypeStruct((3, D_FF), F32)],
        compiler_params=_cp(("parallel", "arbitrary")))(u, u, up, up, dact, dact, w_ffn, w_ffn)


def _adamw(w, g, m, v, name):
    r, c = w.shape
    tr = next((t for t in (512, 352, 256, 128, 64, 32, 16, 8) if r > t and r % t == 0), r)

    def body(w_ref, g_ref, m_ref, v_ref, d_ref, nm_ref, nv_ref):
        gv = g_ref[...]
        m_new = ADAM_B1 * m_ref[...] + (1.0 - ADAM_B1) * gv
        v_new = ADAM_B2 * v_ref[...] + (1.0 - ADAM_B2) * (gv * gv)
        m_hat = m_new / (1.0 - ADAM_B1 ** ADAM_STEP)
        v_hat = v_new / (1.0 - ADAM_B2 ** ADAM_STEP)
        d_ref[...] = -ADAM_LR * (m_hat / (jnp.sqrt(v_hat) + ADAM_EPS) + ADAM_WD * w_ref[...])
        nm_ref[...] = m_new
        nv_ref[...] = v_new

    spec = pl.BlockSpec((tr, c), lambda i: (i, 0))
    shp = jax.ShapeDtypeStruct((r, c), F32)
    return pl.pallas_call(
        body, name=name, grid=(r // tr,), in_specs=[spec] * 4, out_specs=[spec] * 3, out_shape=[shp] * 3,
        compiler_params=_cp(("parallel",)))(w, g, m, v)


def _sum_rows_block(h):
    return h if h <= 352 else 256


def _pair_sum(view, recv, sel, name):
    n, _, h, c = view.shape
    tr = _sum_rows_block(h)

    def body(sel_ref, a_ref, b_ref, o_ref, ob_ref):
        t = a_ref[...] + b_ref[...]
        o_ref[...] = t
        ob_ref[...] = t.astype(BF16)

    blk = pl.BlockSpec((None, tr, c), lambda j, i, sel_ref: (j, i, 0))
    grid_spec = pltpu.PrefetchScalarGridSpec(
        num_scalar_prefetch=1, grid=(n, h // tr),
        in_specs=[pl.BlockSpec((None, None, tr, c), lambda j, i, sel_ref: (j, sel_ref[0], i, 0)),
                  pl.BlockSpec((None, None, tr, c), lambda j, i, sel_ref: (j, 0, i, 0))],
        out_specs=[blk, blk])
    return pl.pallas_call(
        body, name=name, grid_spec=grid_spec,
        out_shape=[jax.ShapeDtypeStruct((n, h, c), F32), jax.ShapeDtypeStruct((n, h, c), BF16)],
        compiler_params=_cp(("parallel", "parallel")))(sel, view, recv)


def _chip_sum(pair, got, sel, name):
    _, h, c = pair.shape
    tr = _sum_rows_block(h)
    nblk = h // tr

    def body(sel_ref, p_ref, g0_ref, g1_ref, g2_ref, o_ref):
        o_ref[...] = ((p_ref[...] + g0_ref[...].astype(F32)) + g1_ref[...].astype(F32)) + g2_ref[...].astype(F32)

    slot = lambda k: pl.BlockSpec((None, tr, c), lambda i, sel_ref: (k, i, 0))
    grid_spec = pltpu.PrefetchScalarGridSpec(
        num_scalar_prefetch=1, grid=(h // tr,),
        in_specs=[pl.BlockSpec((None, tr, c), lambda i, sel_ref: (sel_ref[1], i, 0)), slot(0), slot(1), slot(2)],
        out_specs=pl.BlockSpec((tr, c), lambda i, sel_ref: (sel_ref[0] * nblk + i, 0)))
    return pl.pallas_call(
        body, name=name, grid_spec=grid_spec, out_shape=jax.ShapeDtypeStruct((2 * h, c), F32),
        compiler_params=_cp(("parallel",)))(sel, pair, got, got, got)


def _place():
    return lax.axis_index("x"), lax.axis_index("y"), lax.axis_index("c")


def _other_chips(x, y):
    return [(1 - x, y), (x, 1 - y), (1 - x, 1 - y)]


def _hbm_specs(n):
    return [pl.BlockSpec(memory_space=pl.ANY)] * n


def _all_gather_weights(bigs, smalls):
    nb, ns = len(bigs), len(smalls)
    n = nb + ns

    def body(*refs):
        start, forward, finish = _gather_phases(refs[:n], refs[2 * n:3 * n], nb, *refs[3 * n:])
        start()
        forward()
        finish()

    arrays, landing, sems = _gather_operands(bigs, smalls)
    return pl.pallas_call(
        body, name="all_gather_weights",
        out_shape=[jax.ShapeDtypeStruct(b.shape, b.dtype) for b in landing],
        in_specs=_hbm_specs(2 * n), out_specs=_hbm_specs(n), input_output_aliases={n + k: k for k in range(n)},
        scratch_shapes=sems)(*arrays, *landing)


def _hosted_gather(refs, n, nb, step, total):
    ins, outs, send_sems, recv_sems = refs
    start, forward, finish = _gather_phases(ins, outs, nb, send_sems, recv_sems)
    pl.when(step == 0)(start)
    pl.when(step == (3 * total) // 4)(forward)
    return lambda: pl.when(step == total - 1)(finish)


def _in_proj(x, g, w_a, w_b, w_c, bigs, smalls):
    s, d = x.shape
    tm, tn = min(TM_MM, s), ATTN_W
    na, nq = w_a.shape[1] // tn, w_b.shape[1] // tn
    steps = na + nq + 1
    total = (s // tm) * steps
    nb, n = len(bigs), len(bigs) + len(smalls)
    arrays, landing, sems = _gather_operands(bigs, smalls)

    def body(x_ref, g_ref, wa_ref, wb_ref, wc_ref, *rest):
        z_ref, qkv_ref, f_ref, h_ref = rest[2 * n:2 * n + 4]
        h_scr = rest[-1]
        m, j = pl.program_id(0), pl.program_id(1)
        finish = _hosted_gather((rest[:n], rest[2 * n + 4:3 * n + 4]) + tuple(rest[3 * n + 4:3 * n + 6]), n, nb,
                                m * steps + j, total)

        @pl.when(j == 0)
        def _():
            xv = x_ref[...]
            hv = (xv * _rstd(xv) * g_ref[...]).astype(BF16)
            h_scr[...] = hv
            h_ref[...] = hv

        h = h_scr[...]

        @pl.when(j < na)
        def _():
            z_ref[...] = lax.dot_general(h, wa_ref[...], _NN, preferred_element_type=F32)

        @pl.when((j >= na) & (j < na + nq))
        def _():
            qkv_ref[...] = lax.dot_general(h, wb_ref[...], _NN, preferred_element_type=F32).astype(BF16)

        @pl.when(j == na + nq)
        def _():
            f_ref[...] = lax.dot_general(h, wc_ref[...], _NN, preferred_element_type=F32)

        finish()

    blk_a = lambda m, j: (m, jnp.minimum(j, na - 1))
    blk_b = lambda m, j: (m, jnp.clip(j - na, 0, nq - 1))
    outs = pl.pallas_call(
        body, name="in_proj", grid=(s // tm, steps),
        in_specs=[pl.BlockSpec((tm, d), lambda m, j: (m, 0)), pl.BlockSpec((1, d), lambda m, j: (0, 0)),
                  pl.BlockSpec((d, tn), lambda m, j: (0, jnp.minimum(j, na - 1))),
                  pl.BlockSpec((d, tn), lambda m, j: (0, jnp.clip(j - na, 0, nq - 1))),
                  pl.BlockSpec((d, LANES), lambda m, j: (0, 0))] + _hbm_specs(2 * n),
        out_specs=[pl.BlockSpec((tm, tn), blk_a), pl.BlockSpec((tm, tn), blk_b),
                   pl.BlockSpec((tm, LANES), lambda m, j: (m, 0)), pl.BlockSpec((tm, d), lambda m, j: (m, 0))]
        + _hbm_specs(n),
        out_shape=[jax.ShapeDtypeStruct((s, w_a.shape[1]), F32), jax.ShapeDtypeStruct((s, w_b.shape[1]), BF16),
                   jax.ShapeDtypeStruct((s, LANES), F32), jax.ShapeDtypeStruct((s, d), BF16)]
        + [jax.ShapeDtypeStruct(b.shape, b.dtype) for b in landing],
        input_output_aliases={5 + n + k: 4 + k for k in range(n)},
        scratch_shapes=sems + [pltpu.VMEM((tm, d), BF16)],
        compiler_params=_cp(("arbitrary", "arbitrary")))(x, g, w_a, w_b, w_c, *arrays, *landing)
    return outs[0], outs[1], outs[2], outs[3], outs[4:]


def _gather_operands(bigs, smalls):
    x, y, _ = _place()
    arrays = list(bigs) + list(smalls)
    landing = [lax.dynamic_update_index_in_dim(lax.empty((N_CHIPS,) + a.shape, a.dtype), a, 2 * x + y, 0)
               for a in arrays]
    n_sems = 6 * len(bigs) + 3 * len(smalls)
    return arrays, landing, [pltpu.SemaphoreType.DMA((n_sems,)), pltpu.SemaphoreType.DMA((n_sems,))]


def _gather_phases(ins, outs, nb, send_sems, recv_sems):
    n = len(ins)
    x, y, c = _place()
    my_chip = 2 * x + y
    chips = _other_chips(x, y)
    sibling = (x, y, 1 - c)

    def rows(k, which):
        h = ins[k].shape[0] // 2
        return pl.ds(which * h, h)

    def copy(sem, src, dst, to):
        return pltpu.make_async_remote_copy(src_ref=src, dst_ref=dst, send_sem=send_sems.at[sem],
                                            recv_sem=recv_sems.at[sem], device_id=to, device_id_type=MESH)

    def sends():
        out = [copy(6 * k + j, ins[k].at[rows(k, c)], outs[k].at[my_chip, rows(k, c)], (cx, cy, c))
               for k in range(nb) for j, (cx, cy) in enumerate(chips)]
        return out + [copy(6 * nb + 3 * (k - nb) + j, ins[k], outs[k].at[my_chip], (cx, cy, c))
                      for k in range(nb, n) for j, (cx, cy) in enumerate(chips)]

    def landed(k, j, which):
        cx, cy = chips[j]
        return outs[k].at[2 * cx + cy, rows(k, which)]

    def forwards():
        return [copy(6 * k + 3 + j, landed(k, j, c), landed(k, j, c), sibling)
                for j in range(3) for k in range(nb)]

    def start():
        for cp in sends():
            cp.start()

    def forward():
        for j in range(3):
            for k in range(nb):
                copy(6 * k + j, landed(k, j, c), landed(k, j, c), (x, y, c)).wait_recv()
                copy(6 * k + 3 + j, landed(k, j, c), landed(k, j, c), sibling).start()

    def finish():
        for j, (cx, cy) in enumerate(chips):
            for k in range(nb):
                copy(6 * k + 3 + j, landed(k, j, 1 - c), landed(k, j, 1 - c), (x, y, c)).wait_recv()
            for k in range(nb, n):
                arrived = outs[k].at[2 * cx + cy]
                copy(6 * nb + 3 * (k - nb) + j, arrived, arrived, (x, y, c)).wait_recv()
        for cp in sends() + forwards():
            cp.wait_send()

    return start, forward, finish


def _pair_exchange(views, name):
    n = len(views)

    def body(*refs):
        ins, outs, send_sems, recv_sems = refs[:n], refs[n:2 * n], refs[2 * n], refs[2 * n + 1]
        x, y, c = _place()
        copies = [pltpu.make_async_remote_copy(
            src_ref=ins[k].at[:, pl.ds(1 - c, 1)], dst_ref=outs[k], send_sem=send_sems.at[k],
            recv_sem=recv_sems.at[k], device_id=(x, y, 1 - c), device_id_type=MESH) for k in range(n)]
        for cp in copies:
            cp.start()
        for cp in copies:
            cp.wait()

    return pl.pallas_call(
        body, name=name,
        out_shape=[jax.ShapeDtypeStruct((v.shape[0], 1) + v.shape[2:], v.dtype) for v in views],
        in_specs=_hbm_specs(n), out_specs=_hbm_specs(n),
        scratch_shapes=[pltpu.SemaphoreType.DMA((n,)), pltpu.SemaphoreType.DMA((n,))])(*views)


def _scatter_operands(parts):
    n = len(parts)
    return ([jax.ShapeDtypeStruct((3,) + p.shape[1:], p.dtype) for p in parts],
            [pltpu.SemaphoreType.DMA((3 * n,)), pltpu.SemaphoreType.DMA((3 * n,))])


def _scatter_phases(ins, outs, send_sems, recv_sems):
    x, y, c = _place()

    def copies():
        return [pltpu.make_async_remote_copy(
            src_ref=ins[k].at[pl.ds(2 * cx + cy, 1)], dst_ref=outs[k].at[pl.ds(r, 1)], send_sem=send_sems.at[3 * k + r],
            recv_sem=recv_sems.at[3 * k + r], device_id=(cx, cy, c), device_id_type=MESH)
            for k in range(len(ins)) for r, (cx, cy) in enumerate(_other_chips(x, y))]

    def start():
        for cp in copies():
            cp.start()

    def finish():
        for cp in copies():
            cp.wait()

    return start, finish


def _hosted_scatter(ins, outs, sems, step, total):
    start, finish = _scatter_phases(ins, outs, *sems)
    pl.when(step == 0)(start)
    return lambda: pl.when(step == total - 1)(finish)


def _join_halves(shards):
    n = len(shards)

    def body(*refs):
        ins, outs, send_sems, recv_sems = refs[:n], refs[n:2 * n], refs[2 * n], refs[2 * n + 1]
        x, y, c = _place()

        def rows(ref, which):
            h = ref.shape[0] // 2
            return ref.at[pl.ds(which * h, h)]

        sent = [pltpu.make_async_remote_copy(
            src_ref=rows(ins[k], c), dst_ref=rows(outs[k], c), send_sem=send_sems.at[k], recv_sem=recv_sems.at[k],
            device_id=(x, y, 1 - c), device_id_type=MESH) for k in range(n)]
        for cp in sent:
            cp.start()
        for k in range(n):
            pltpu.make_async_remote_copy(
                src_ref=rows(ins[k], 1 - c), dst_ref=rows(outs[k], 1 - c), send_sem=send_sems.at[k],
                recv_sem=recv_sems.at[k], device_id=(x, y, 1 - c), device_id_type=MESH).wait_recv()
        for cp in sent:
            cp.wait_send()

    return pl.pallas_call(
        body, name="half_exchange", out_shape=[jax.ShapeDtypeStruct(a.shape, a.dtype) for a in shards],
        in_specs=_hbm_specs(n), out_specs=_hbm_specs(n), input_output_aliases={k: k for k in range(n)},
        scratch_shapes=[pltpu.SemaphoreType.DMA((n,)), pltpu.SemaphoreType.DMA((n,))])(*shards)


def _all_reduce_small(packet):
    rows, width = packet.shape
    n_dev = 8

    def body(x_ref, out_ref, gath, send_sems, recv_sems):
        x, y, c = _place()
        me, sibling = (x, y, c), (x, y, 1 - c)
        chips = _other_chips(x, y)

        def slot(px, py, pc):
            return gath.at[pl.ds((4 * px + 2 * py + pc) * rows, rows), :]

        def copy(k, block, to, src=None):
            return pltpu.make_async_remote_copy(
                src_ref=slot(*block) if src is None else src, dst_ref=slot(*block), send_sem=send_sems.at[k],
                recv_sem=recv_sems.at[k], device_id=to, device_id_type=MESH)

        first = [copy(0, me, sibling, src=x_ref)]
        first += [copy(1 + j, me, (*chip, c), src=x_ref) for j, chip in enumerate(chips)]
        for cp in first:
            cp.start()
        gath[pl.ds((4 * x + 2 * y + c) * rows, rows), :] = x_ref[...]
        passed = [copy(4 + j, (*chip, c), sibling) for j, chip in enumerate(chips)]
        for j, chip in enumerate(chips):
            copy(1 + j, (*chip, c), me).wait_recv()
            passed[j].start()
        copy(0, sibling, me).wait_recv()
        for j, chip in enumerate(chips):
            copy(4 + j, (*chip, 1 - c), me).wait_recv()
        for cp in first + passed:
            cp.wait_send()
        acc = gath[0:rows, :]
        for d in range(1, n_dev):
            acc = acc + gath[d * rows:(d + 1) * rows, :]
        out_ref[...] = acc

    return pl.pallas_call(
        body, name="all_reduce_small", out_shape=jax.ShapeDtypeStruct((rows, width), F32),
        in_specs=[pl.BlockSpec(memory_space=pltpu.VMEM)], out_specs=pl.BlockSpec(memory_space=pltpu.VMEM),
        scratch_shapes=[pltpu.VMEM((n_dev * rows, width), F32), pltpu.SemaphoreType.DMA((7,)),
                        pltpu.SemaphoreType.DMA((7,))])(packet)


def _flat_rows(parts, width, row_multiple):
    flat = jnp.concatenate([p.astype(F32).reshape(-1) for p in parts])
    rows = -(-flat.shape[0] // width)
    rows = -(-rows // row_multiple) * row_multiple
    return jnp.pad(flat, (0, rows * width - flat.shape[0])).reshape(rows, width)


def _unflatten(flat2d, shapes):
    flat = flat2d.reshape(-1)
    out, off = [], 0
    for shp in shapes:
        n = 1
        for dim in shp:
            n *= dim
        out.append(flat[off:off + n].reshape(shp))
        off += n
    return out


def _core_and_chip():
    x, y, c = _place()
    return jnp.stack([c, 2 * x + y]).astype(jnp.int32)


def _pair_sums(chip_major, names, call_name):
    views = [g.reshape(N_CHIPS, 2, g.shape[1] // 2, g.shape[2]) for g in chip_major]
    recv = _pair_exchange(views, call_name)
    sel = _core_and_chip()
    return [_pair_sum(v, r, sel, "pair_sum_" + nm) for v, r, nm in zip(views, recv, names)]


def _finish_grads(pairs, got, names):
    sel = _core_and_chip()
    return _join_halves([_chip_sum(p, g, sel, "chip_sum_" + nm) for (p, _), g, nm in zip(pairs, got, names)])


def kernel(x, g_mix, w_in, b_f, w_conv, g_conv_out, g_attn_out, w_o, g_ffn, w_up, w_ffn_conv, w_down, g_final, loss_target, m_g_mix, m_w_in, m_b_f, m_w_conv, m_g_conv_out, m_g_attn_out, m_w_o, m_g_ffn, m_w_up, m_w_ffn_conv, m_w_down, m_g_final, v_g_mix, v_w_in, v_b_f, v_w_conv, v_g_conv_out, v_g_attn_out, v_w_o, v_g_ffn, v_w_up, v_w_ffn_conv, v_w_down, v_g_final):
    s = x.shape[1]
    x0 = x[0]
    target = loss_target[0]
    d = D_MODEL
    x_pos, y_pos, _ = _place()
    my_chip = 2 * x_pos + y_pos

    (c_in,) = _all_gather_weights([w_in[0].astype(BF16)], [])
    w_in_full = jnp.concatenate([c_in[j] for j in range(N_CHIPS)], axis=1)
    c3 = 3 * CONV_CH
    w_a, w_b = w_in_full[:, :c3], w_in_full[:, c3:c3 + 3 * ATTN_W]
    w_c = jnp.pad(w_in_full[:, c3 + 3 * ATTN_W:], ((0, 0), (0, LANES - N_HEADS)))
    w_q, w_k, w_v = (w_b[:, i * ATTN_W:(i + 1) * ATTN_W] for i in range(3))
    b_pad = jnp.pad(b_f, ((0, 0), (0, LANES - N_HEADS)))

    z_a, qkv, f_log, h1, (c_o, c_up, c_conv, c_ffn) = _in_proj(
        x0, g_mix, w_a, w_b, w_c, [w_o[0].astype(BF16), w_up[0].astype(BF16)], [w_conv[0], w_ffn_conv[0]])
    fb = _gate_fwd(f_log, b_pad)
    qx, kx, kxt, vx, vt, bounds, (c_down,) = _attn_prep(qkv, fb, [w_down[0].astype(BF16)])
    w_o_full = c_o.reshape(d, d)
    w_down_full = c_down.reshape(D_FF, d)
    w_conv_full = jnp.concatenate([c_conv[j] for j in range(N_CHIPS)], axis=1)
    w_ffn_full = jnp.concatenate([c_ffn[j] for j in range(N_CHIPS)], axis=1)
    n_up = c_up.shape[2]
    first_blk, last_blk = _key_block_ranges(bounds)
    o_attn, lse = _attn_fwd_t(qx, kx, vt, first_blk)
    mix = _mixer_fwd(z_a, o_attn, w_conv_full, g_conv_out, g_attn_out)
    x2, h2 = _mm("nn", [mix], [w_o_full], F32, 512, d, "out_proj", add=x0, norm_g=g_ffn)
    up = _mm("nn", [h2], [c_up], F32, TM_MM, n_up, "up_proj", b_chips=True)
    act, u_conv = _ffn_act_fwd(up, w_ffn_full)
    dx3, dx3_b, loss_row, gg_final = _down_proj_loss(act, w_down_full, x2, target, g_final.reshape(1, d))

    dact = _mm("nt", [dx3_b], [w_down_full], F32, TM_MM, 1408, "d_act")
    gw_down = _mm_tn(act, dx3_b, 1408, 1024, "gw_down")
    dup, gwf_lin, gwf_gate = _ffn_act_bwd(up, u_conv, dact, w_ffn_full)
    dh2 = _mm("nt", [(dup, j // 2, j % 2, n_up) for j in range(N_CHIPS)], [(c_up, j) for j in range(N_CHIPS)],
              F32, TM_MM, 512, "d_h2")
    gw_up = _mm_tn(h2, dup, 1024, n_up, "gw_up", out_chips=True)
    dx2, dx2_b, gg_ffn = _rms_bwd(x2, dh2, g_ffn, dx3, "rms_ffn_bwd", True)
    dmix = _mm("nt", [dx2_b], [w_o_full], F32, TM_MM, 512, "d_mix")
    gw_o = _mm_tn(mix, dx2_b, 1024, 1024, "gw_o")
    early = _pair_sums([gw_o.reshape(N_CHIPS, d // N_CHIPS, d), gw_up, gw_down.reshape(N_CHIPS, D_FF // N_CHIPS, d)],
                       ["w_o", "w_up", "w_down"], "pair_exchange")
    dz_a, dox, gw_conv, gg_conv_out, gg_attn_out, (got_o, got_down) = _mixer_bwd(
        z_a, o_attn, dmix, w_conv_full, g_conv_out, g_attn_out, [early[0][1], early[2][1]])
    dk, dv, dfk, dqt, dfq = _attn_bwd_t(qx, dox, kx, kxt, vx, lse, last_blk)
    dq = _attn_dq_finish(dqt)
    d_f = (jnp.transpose(dfk[:, :, 0:2, :], (1, 3, 0, 2)).reshape(s, N_HEADS)
           + jnp.transpose(jnp.sum(dfq, axis=2), (1, 2, 0)).reshape(s, N_HEADS))
    df_b, gb_f = _gate_bwd(f_log, b_pad, jnp.pad(d_f, ((0, 0), (0, LANES - N_HEADS))))
    gw_a = _mm_tn(h1, dz_a, 1024, c3, "gw_in_conv")
    gw_q = _mm_tn(h1, dq, 1024, ATTN_W, "gw_in_q")
    gw_k = _mm_tn(h1, dk, 1024, ATTN_W, "gw_in_k")
    gw_v = _mm_tn(h1, dv, 1024, ATTN_W, "gw_in_v")
    gw_c = _mm_tn(h1, df_b, 1024, LANES, "gw_in_gate")
    gw_in = jnp.concatenate([gw_a, gw_q, gw_k, gw_v, gw_c[:, :N_HEADS]], axis=1)
    n_in = IN_COLS // N_CHIPS
    gw_in = jnp.stack([gw_in[:, j * n_in:(j + 1) * n_in] for j in range(N_CHIPS)])
    late = _pair_sums([gw_in], ["w_in"], "pair_exchange_w_in")
    dh1, (got_up, got_in) = _mm("nt", [dz_a, dq, dk, dv, df_b], [w_a, w_q, w_k, w_v, w_c], F32, TM_MM, 512, "d_h1",
                                scatter=[early[1][1], late[0][1]])
    grad_x, gg_mix = _rms_bwd(x0, dh1, g_mix, dx2, "rms_mix_bwd", False)
    g_w_in, g_w_o, g_w_up, g_w_down = _finish_grads(late + early, [got_in, got_o, got_up, got_down],
                                                    ["w_in", "w_o", "w_up", "w_down"])

    gw_ffn = jnp.concatenate([gwf_lin, gwf_gate], axis=1)
    small_parts = [gg_mix, gg_conv_out, gg_attn_out, gg_ffn, gg_final, gb_f[:, :N_HEADS], loss_row[:, 0:1], gw_conv,
                   gw_ffn]
    small_shapes = [a.shape for a in small_parts]
    tot = _unflatten(_all_reduce_small(_flat_rows(small_parts, d, 8)), small_shapes)
    g_g_mix, g_g_conv_out, g_g_attn_out, g_g_ffn, g_g_final, g_b_f, loss_sum, g_conv_full, g_ffn_full = tot
    loss = loss_sum[0, 0]
    g_g_final = g_g_final[0]
    g_w_conv = lax.dynamic_slice_in_dim(g_conv_full, my_chip * (CONV_CH // N_CHIPS), CONV_CH // N_CHIPS, axis=1)
    g_w_ffn = lax.dynamic_slice_in_dim(g_ffn_full, my_chip * n_up, n_up, axis=1)

    def adam_big(w, g, m, v, name):
        dl, nm, nv = _adamw(w[0], g, m[0], v[0], name)
        return dl[None], nm[None], nv[None]

    u_w_in = adam_big(w_in, g_w_in, m_w_in, v_w_in, "adam_w_in")
    u_w_o = adam_big(w_o, g_w_o, m_w_o, v_w_o, "adam_w_o")
    u_w_up = adam_big(w_up, g_w_up, m_w_up, v_w_up, "adam_w_up")
    u_w_down = adam_big(w_down, g_w_down, m_w_down, v_w_down, "adam_w_down")

    small_w = [g_mix, b_f, g_conv_out, g_attn_out, g_ffn, g_final, w_conv, w_ffn_conv]
    small_g = [g_g_mix, g_b_f, g_g_conv_out, g_g_attn_out, g_g_ffn, g_g_final, g_w_conv, g_w_ffn]
    small_m = [m_g_mix, m_b_f, m_g_conv_out, m_g_attn_out, m_g_ffn, m_g_final, m_w_conv, m_w_ffn_conv]
    small_v = [v_g_mix, v_b_f, v_g_conv_out, v_g_attn_out, v_g_ffn, v_g_final, v_w_conv, v_w_ffn_conv]
    shapes = [a.shape for a in small_w]
    pack = lambda arrs: _flat_rows(arrs, LANES, 8)
    sd, sm, sv = _adamw(pack(small_w), pack(small_g), pack(small_m), pack(small_v), "adam_small")
    sd, sm, sv = _unflatten(sd, shapes), _unflatten(sm, shapes), _unflatten(sv, shapes)
    (d_g_mix, d_b_f, d_g_conv_out, d_g_attn_out, d_g_ffn, d_g_final, d_w_conv, d_w_ffn) = sd
    (nm_g_mix, nm_b_f, nm_g_conv_out, nm_g_attn_out, nm_g_ffn, nm_g_final, nm_w_conv, nm_w_ffn) = sm
    (nv_g_mix, nv_b_f, nv_g_conv_out, nv_g_attn_out, nv_g_ffn, nv_g_final, nv_w_conv, nv_w_ffn) = sv

    grads = (g_g_mix, g_w_in[None], g_b_f, g_w_conv[None], g_g_conv_out, g_g_attn_out, g_w_o[None], g_g_ffn,
             g_w_up[None], g_w_ffn[None], g_w_down[None], g_g_final)
    deltas = (d_g_mix, u_w_in[0], d_b_f, d_w_conv, d_g_conv_out, d_g_attn_out, u_w_o[0], d_g_ffn, u_w_up[0],
              d_w_ffn, u_w_down[0], d_g_final)
    new_m = (nm_g_mix, u_w_in[1], nm_b_f, nm_w_conv, nm_g_conv_out, nm_g_attn_out, u_w_o[1], nm_g_ffn, u_w_up[1],
             nm_w_ffn, u_w_down[1], nm_g_final)
    new_v = (nv_g_mix, u_w_in[2], nv_b_f, nv_w_conv, nv_g_conv_out, nv_g_attn_out, u_w_o[2], nv_g_ffn, u_w_up[2],
             nv_w_ffn, u_w_down[2], nv_g_final)
    return (loss, grad_x[None], *grads, *deltas, *new_m, *new_v)
```

```python
import jax
import jax.numpy as jnp
from jax import lax
from jax.experimental import pallas as pl
from jax.experimental.pallas import tpu as pltpu

F32, BF16 = jnp.float32, jnp.bfloat16
MESH = pl.DeviceIdType.MESH

D_MODEL = 1024
CONV_CH = 512
ATTN_W = 512
N_HEADS = 8
HEAD_DIM = 64
N_PAIRS = N_HEADS // 2
D_FF = 2816
IN_COLS = 3 * CONV_CH + 3 * ATTN_W + N_HEADS
EPS = 1e-6
Q_SCALE = 0.125
EXP_ZERO = 88.0
N_CHIPS = 4
LANES = 128
HALO = 8
HALO_BF16 = 2 * HALO

ADAM_LR, ADAM_B1, ADAM_B2, ADAM_EPS, ADAM_WD, ADAM_STEP = 0.001, 0.9, 0.999, 1e-08, 0.01, 10

TM_ROWS = 512
TM_MM = 1024
TK_TN = 1024
TQ = 512
TM_FFN = 256
TN_FFN = 1408
VMEM_LIMIT = 52 * 2**20


def _cp(sem, vmem=VMEM_LIMIT):
    return pltpu.CompilerParams(dimension_semantics=sem, vmem_limit_bytes=vmem)


def _bf(a):
    return a if a.dtype == BF16 else a.astype(BF16)


def _mm(mode, a_list, b_list, out_dtype, tm, tn, name, add=None, b_chips=False, scatter=(), norm_g=None):
    n_p = len(a_list)
    a0 = a_list[0]
    m_dim = a0[0].shape[1] if isinstance(a0, tuple) else a0.shape[0]
    b0 = b_list[0]
    if b_chips:
        n_dim = b0.shape[0] * b0.shape[2]
        assert tn == b0.shape[2] and mode == "nn"
    else:
        b0 = b0[0][b0[1]] if isinstance(b0, tuple) else b0
        n_dim = b0.shape[1 if mode == "nn" else 0]
    tm, tn = min(tm, m_dim), min(tn, n_dim)
    assert m_dim % tm == 0 and n_dim % tn == 0
    dims = (((1,), (0,)), ((), ())) if mode == "nn" else (((1,), (1,)), ((), ()))
    in_specs, args = [], []
    for a in a_list:
        if isinstance(a, tuple):
            arr, lead, col, width = a
            in_specs.append(pl.BlockSpec((None, tm, width), lambda m, n, lead=lead, col=col: (lead, m, col)))
        else:
            arr = a
            in_specs.append(pl.BlockSpec((tm, a.shape[1]), lambda m, n: (m, 0)))
        args.append(arr)
    for b in b_list:
        if b_chips:
            arr = b
            in_specs.append(pl.BlockSpec((None, b.shape[1], tn), lambda m, n: (n, 0, 0)))
        elif isinstance(b, tuple):
            arr, lead = b
            if mode == "nn":
                in_specs.append(pl.BlockSpec((None, arr.shape[1], tn), lambda m, n, lead=lead: (lead, 0, n)))
            else:
                in_specs.append(pl.BlockSpec((None, tn, arr.shape[2]), lambda m, n, lead=lead: (lead, n, 0)))
        elif mode == "nn":
            arr = b
            in_specs.append(pl.BlockSpec((b.shape[0], tn), lambda m, n: (0, n)))
        else:
            arr = b
            in_specs.append(pl.BlockSpec((tn, b.shape[1]), lambda m, n: (n, 0)))
        args.append(arr)
    if add is not None:
        in_specs.append(pl.BlockSpec((tm, tn), lambda m, n: (m, n)))
        args.append(add)
    if norm_g is not None:
        assert tn == n_dim and not scatter
        in_specs.append(pl.BlockSpec((1, tn), lambda m, n: (0, 0)))
        args.append(norm_g)

    n_in = len(args)
    n_sc = len(scatter)
    grid = (m_dim // tm, n_dim // tn)

    def body(*refs):
        o_ref = refs[n_in + n_sc]
        if n_sc:
            finish = _hosted_scatter(refs[n_in:n_in + n_sc], refs[n_in + n_sc + 1:n_in + 2 * n_sc + 1],
                                     refs[n_in + 2 * n_sc + 1:], pl.program_id(0) * grid[1] + pl.program_id(1),
                                     grid[0] * grid[1])
        acc = None
        for i in range(n_p):
            d = lax.dot_general(_bf(refs[i][...]), _bf(refs[n_p + i][...]), dims,
                                preferred_element_type=F32)
            acc = d if acc is None else acc + d
        if add is not None:
            acc = refs[2 * n_p][...] + acc
        o_ref[...] = acc.astype(out_dtype)
        if norm_g is not None:
            refs[n_in + 1][...] = (acc * _rstd(acc) * refs[n_in - 1][...]).astype(BF16)
        if n_sc:
            finish()

    main_spec = pl.BlockSpec((tm, tn), lambda m, n: (m, n))
    main_shape = jax.ShapeDtypeStruct((m_dim, n_dim), out_dtype)
    if norm_g is not None:
        return pl.pallas_call(body, name=name, grid=grid, in_specs=in_specs, out_specs=[main_spec, main_spec],
                              out_shape=[main_shape, jax.ShapeDtypeStruct((m_dim, n_dim), BF16)],
                              compiler_params=_cp(("parallel", "parallel")))(*args)
    if not n_sc:
        return pl.pallas_call(body, name=name, grid=grid, in_specs=in_specs, out_specs=main_spec,
                              out_shape=main_shape, compiler_params=_cp(("parallel", "parallel")))(*args)
    got_shapes, sems = _scatter_operands(scatter)
    outs = pl.pallas_call(
        body, name=name, grid=grid, in_specs=in_specs + _hbm_specs(n_sc), out_specs=[main_spec] + _hbm_specs(n_sc),
        out_shape=[main_shape] + got_shapes, scratch_shapes=sems,
        compiler_params=_cp(("arbitrary", "arbitrary")))(*args, *scatter)
    return outs[0], outs[1:]


def _mm_tn(a, b, tm, tn, name, out_chips=False):
    k_dim, m_dim = a.shape
    n_dim = b.shape[-1] * (b.shape[0] if b.ndim == 3 else 1)
    tm, tn, tk = min(tm, m_dim), min(tn, b.shape[-1]), min(TK_TN, k_dim)
    assert m_dim % tm == 0 and b.shape[-1] % tn == 0 and k_dim % tk == 0
    per = b.shape[-1] // tn
    if b.ndim == 3:
        b_spec = pl.BlockSpec((None, tk, tn), lambda m, n, k: (n // per, k, n % per))
    else:
        b_spec = pl.BlockSpec((tk, tn), lambda m, n, k: (k, n))

    def body(a_ref, b_ref, o_ref):
        @pl.when(pl.program_id(2) == 0)
        def _():
            o_ref[...] = jnp.zeros_like(o_ref)
        o_ref[...] += lax.dot_general(_bf(a_ref[...]), _bf(b_ref[...]), (((0,), (0,)), ((), ())),
                                      preferred_element_type=F32)

    return pl.pallas_call(
        body, name=name, grid=(m_dim // tm, n_dim // tn, k_dim // tk),
        in_specs=[pl.BlockSpec((tk, tm), lambda m, n, k: (k, m)), b_spec],
        out_specs=(pl.BlockSpec((None, tm, tn), lambda m, n, k: (n, m, 0)) if out_chips
                   else pl.BlockSpec((tm, tn), lambda m, n, k: (m, n))),
        out_shape=jax.ShapeDtypeStruct((n_dim // tn, m_dim, tn) if out_chips else (m_dim, n_dim), F32),
        compiler_params=_cp(("parallel", "parallel", "arbitrary")))(a, b)


def _rstd(x):
    return lax.rsqrt(jnp.mean(x * x, axis=-1, keepdims=True) + EPS)


def _rms_bwd(x, dh, g, dres, name, with_bf16, scatter=()):
    s, d = x.shape
    tm = min(TM_ROWS, s)
    n_sc = len(scatter)
    n_out = 3 if with_bf16 else 2
    got_shapes, sems = _scatter_operands(scatter) if n_sc else ([], [])

    def body(x_ref, dh_ref, g_ref, dres_ref, *rest):
        dx_ref, gg_ref = rest[n_sc], rest[n_sc + n_out - 1]
        i = pl.program_id(0)
        if n_sc:
            finish = _hosted_scatter(rest[:n_sc], rest[n_sc + n_out:2 * n_sc + n_out], rest[2 * n_sc + n_out:], i,
                                     s // tm)

        @pl.when(i == 0)
        def _():
            gg_ref[...] = jnp.zeros_like(gg_ref)

        xv = x_ref[...]
        xn = xv * _rstd(xv)
        dhv = dh_ref[...]
        gg_ref[...] += jnp.sum(dhv * xn, axis=0, keepdims=True)
        t = dhv * g_ref[...]
        dx = dres_ref[...] + _rstd(xv) * (t - xn * jnp.mean(t * xn, axis=-1, keepdims=True))
        dx_ref[...] = dx
        if with_bf16:
            rest[n_sc + 1][...] = dx.astype(BF16)
        if n_sc:
            finish()

    row = pl.BlockSpec((tm, d), lambda i: (i, 0))
    vec = pl.BlockSpec((1, d), lambda i: (0, 0))
    out_specs = [row] + ([row] if with_bf16 else []) + [vec] + _hbm_specs(n_sc)
    out_shape = ([jax.ShapeDtypeStruct((s, d), F32)] + ([jax.ShapeDtypeStruct((s, d), BF16)] if with_bf16 else [])
                 + [jax.ShapeDtypeStruct((1, d), F32)] + got_shapes)
    outs = pl.pallas_call(
        body, name=name, grid=(s // tm,), in_specs=[row, row, vec, row] + _hbm_specs(n_sc), out_specs=out_specs,
        out_shape=out_shape, scratch_shapes=sems, compiler_params=_cp(("arbitrary",)))(x, dh, g, dres, *scatter)
    return tuple(outs[:n_out]) + ((outs[n_out:],) if n_sc else ())


def _down_proj_loss(act, w_down, x2, target, g):
    s, d = x2.shape
    tm = min(TM_ROWS, s)
    k_dim = act.shape[1]

    def body(a_ref, w_ref, x_ref, t_ref, g_ref, dx_ref, dxb_ref, loss_ref, gg_ref):
        @pl.when(pl.program_id(0) == 0)
        def _():
            gg_ref[...] = jnp.zeros_like(gg_ref)
            loss_ref[...] = jnp.zeros_like(loss_ref)

        xv = x_ref[...] + lax.dot_general(a_ref[...], w_ref[...], (((1,), (0,)), ((), ())),
                                          preferred_element_type=F32)
        r = _rstd(xv)
        xn = xv * r
        gv = g_ref[...]
        err = xn * gv - t_ref[...]
        loss_ref[...] += 0.5 * jnp.sum(jnp.mean(err * err, axis=-1, keepdims=True), axis=0, keepdims=True)
        dy = err * (1.0 / d)
        gg_ref[...] += jnp.sum(dy * xn, axis=0, keepdims=True)
        t = dy * gv
        dx = r * (t - xn * jnp.mean(t * xn, axis=-1, keepdims=True))
        dx_ref[...] = dx
        dxb_ref[...] = dx.astype(BF16)

    row = pl.BlockSpec((tm, d), lambda i: (i, 0))
    vec = pl.BlockSpec((1, d), lambda i: (0, 0))
    return pl.pallas_call(
        body, name="down_proj_loss", grid=(s // tm,),
        in_specs=[pl.BlockSpec((tm, k_dim), lambda i: (i, 0)), pl.BlockSpec((k_dim, d), lambda i: (0, 0)), row, row,
                  vec],
        out_specs=[row, row, pl.BlockSpec((1, LANES), lambda i: (0, 0)), vec],
        out_shape=[jax.ShapeDtypeStruct((s, d), F32), jax.ShapeDtypeStruct((s, d), BF16),
                   jax.ShapeDtypeStruct((1, LANES), F32), jax.ShapeDtypeStruct((1, d), F32)],
        compiler_params=_cp(("arbitrary",)))(act, w_down, x2, target, g)


def _prev_halo_spec(tm, width, col):
    return pl.BlockSpec((HALO, width), lambda i, *_: (jnp.maximum(i * (tm // HALO) - 1, 0), col))


def _next_halo_spec(tm, width, col, s):
    return pl.BlockSpec((HALO, width), lambda i, *_: (jnp.minimum((i + 1) * (tm // HALO), s // HALO - 1), col))


def _shift_down(x, k):
    return pltpu.roll(x, k, 0)


def _shift_up(x, k):
    return pltpu.roll(x, x.shape[0] - k, 0)


def _conv_taps(x_ext, w):
    return w[0:1, :] * _shift_down(x_ext, 2) + w[1:2, :] * _shift_down(x_ext, 1) + w[2:3, :] * x_ext


def _conv_taps_t(d_ext, w):
    return w[2:3, :] * d_ext + w[1:2, :] * _shift_up(d_ext, 1) + w[0:1, :] * _shift_up(d_ext, 2)


def _mixer_fwd(z_a, o_attn, w_conv, g_conv_out, g_attn_out):
    s = z_a.shape[0]
    c = CONV_CH
    tm = min(TM_ROWS, s)

    def body(gb_ref, gc_ref, xc_ref, gcp_ref, xcp_ref, o_ref, w_ref, gco_ref, gao_ref, mix_ref):
        i = pl.program_id(0)
        cx = gc_ref[...] * xc_ref[...]
        cx_prev = jnp.where(i > 0, gcp_ref[...] * xcp_ref[...], 0.0)
        conv = _conv_taps(jnp.concatenate([cx_prev, cx], axis=0), w_ref[...])[HALO:]
        y = gb_ref[...] * conv
        mix_ref[:, 0:c] = (y * _rstd(y) * gco_ref[...]).astype(BF16)
        o = o_ref[...]
        mix_ref[:, c:2 * c] = (o * _rstd(o) * gao_ref[...]).astype(BF16)

    col = lambda j: pl.BlockSpec((tm, c), lambda i: (i, j))
    vec = pl.BlockSpec((1, c), lambda i: (0, 0))
    return pl.pallas_call(
        body, name="mixer_fwd", grid=(s // tm,),
        in_specs=[col(0), col(1), col(2), _prev_halo_spec(tm, c, 1), _prev_halo_spec(tm, c, 2), col(0),
                  pl.BlockSpec((3, c), lambda i: (0, 0)), vec, vec],
        out_specs=pl.BlockSpec((tm, 2 * c), lambda i: (i, 0)),
        out_shape=jax.ShapeDtypeStruct((s, 2 * c), BF16),
        compiler_params=_cp(("parallel",)))(z_a, z_a, z_a, z_a, z_a, o_attn, w_conv, g_conv_out, g_attn_out)


def _mixer_bwd(z_a, o_attn, dmix, w_conv, g_conv_out, g_attn_out, scatter):
    s = z_a.shape[0]
    c = CONV_CH
    tm = min(TM_ROWS, s)
    n_blk = s // tm
    n_sc = len(scatter)
    got_shapes, sems = _scatter_operands(scatter)

    def body(gb_ref, gc_ref, xc_ref, gcp_ref, xcp_ref, gbn_ref, gcn_ref, xcn_ref, o_ref, dnc_ref, dncn_ref, dna_ref,
             w_ref, gco_ref, gao_ref, *rest):
        dz_ref, dox_ref, gw_ref, ggco_ref, ggao_ref = rest[n_sc:n_sc + 5]
        i = pl.program_id(0)
        finish = _hosted_scatter(rest[:n_sc], rest[n_sc + 5:2 * n_sc + 5], rest[2 * n_sc + 5:], i, n_blk)

        @pl.when(i == 0)
        def _():
            gw_ref[...] = jnp.zeros_like(gw_ref)
            ggco_ref[...] = jnp.zeros_like(ggco_ref)
            ggao_ref[...] = jnp.zeros_like(ggao_ref)

        w = w_ref[...]
        zeros = jnp.zeros((HALO, c), F32)
        gb_e = jnp.concatenate([zeros, gb_ref[...], gbn_ref[...]], axis=0)
        cx_prev = jnp.where(i > 0, gcp_ref[...] * xcp_ref[...], 0.0)
        gc_e = jnp.concatenate([zeros, gc_ref[...], gcn_ref[...]], axis=0)
        xc_e = jnp.concatenate([zeros, xc_ref[...], xcn_ref[...]], axis=0)
        cx_e = jnp.concatenate([cx_prev, gc_ref[...] * xc_ref[...], gcn_ref[...] * xcn_ref[...]], axis=0)
        dn_next = jnp.where(i < n_blk - 1, dncn_ref[...], 0.0)
        dn_e = jnp.concatenate([zeros, dnc_ref[...], dn_next], axis=0)

        cx_1, cx_2 = _shift_down(cx_e, 1), _shift_down(cx_e, 2)
        conv_e = w[0:1, :] * cx_2 + w[1:2, :] * cx_1 + w[2:3, :] * cx_e
        y_e = gb_e * conv_e
        r_e = _rstd(y_e)
        yn_e = y_e * r_e
        t_e = dn_e * gco_ref[...]
        dy_e = r_e * (t_e - yn_e * jnp.mean(t_e * yn_e, axis=-1, keepdims=True))
        dconv_e = dy_e * gb_e
        dcx_e = _conv_taps_t(dconv_e, w)
        blk = slice(HALO, HALO + tm)
        dz_ref[:, 0:c] = (dy_e * conv_e)[blk].astype(BF16)
        dz_ref[:, c:2 * c] = (dcx_e * xc_e)[blk].astype(BF16)
        dz_ref[:, 2 * c:3 * c] = (dcx_e * gc_e)[blk].astype(BF16)
        ggco_ref[...] += jnp.sum((dn_e * yn_e)[blk], axis=0, keepdims=True)
        dconv = dconv_e[blk]
        gw_ref[0:1, :] += jnp.sum(dconv * cx_2[blk], axis=0, keepdims=True)
        gw_ref[1:2, :] += jnp.sum(dconv * cx_1[blk], axis=0, keepdims=True)
        gw_ref[2:3, :] += jnp.sum(dconv * cx_e[blk], axis=0, keepdims=True)

        o = o_ref[...]
        ra = _rstd(o)
        on = o * ra
        dna = dna_ref[...]
        ggao_ref[...] += jnp.sum(dna * on, axis=0, keepdims=True)
        ta = dna * gao_ref[...]
        do = ra * (ta - on * jnp.mean(ta * on, axis=-1, keepdims=True))
        prod = do * o
        lane = lax.broadcasted_iota(jnp.int32, (tm, LANES), 1)
        head_a = lane < HEAD_DIM
        for p in range(N_PAIRS):
            cols = slice(p * LANES, (p + 1) * LANES)
            pb, dob = prod[:, cols], do[:, cols]
            for hh in range(2):
                sel = head_a if hh == 0 else jnp.logical_not(head_a)
                delta = jnp.sum(jnp.where(sel, pb, 0.0), axis=-1, keepdims=True)
                neg3 = _split3(-delta)
                do_h = pltpu.roll(dob, HEAD_DIM, 1) if hh else dob
                dox_ref[2 * p + hh] = _aug(do_h, lane, neg3).astype(BF16)
        finish()

    col = lambda j: pl.BlockSpec((tm, c), lambda i: (i, j))
    vec = pl.BlockSpec((1, c), lambda i: (0, 0))
    w3 = pl.BlockSpec((3, c), lambda i: (0, 0))
    outs = pl.pallas_call(
        body, name="mixer_bwd", grid=(n_blk,),
        in_specs=[col(0), col(1), col(2), _prev_halo_spec(tm, c, 1), _prev_halo_spec(tm, c, 2),
                  _next_halo_spec(tm, c, 0, s), _next_halo_spec(tm, c, 1, s), _next_halo_spec(tm, c, 2, s),
                  col(0), col(0), _next_halo_spec(tm, c, 0, s), col(1), w3, vec, vec] + _hbm_specs(n_sc),
        out_specs=[pl.BlockSpec((tm, 3 * c), lambda i: (i, 0)),
                   pl.BlockSpec((N_HEADS, tm, LANES), lambda i: (0, i, 0)), w3, vec, vec] + _hbm_specs(n_sc),
        out_shape=[jax.ShapeDtypeStruct((s, 3 * c), BF16), jax.ShapeDtypeStruct((N_HEADS, s, LANES), BF16),
                   jax.ShapeDtypeStruct((3, c), F32), jax.ShapeDtypeStruct((1, c), F32),
                   jax.ShapeDtypeStruct((1, c), F32)] + got_shapes,
        scratch_shapes=sems, compiler_params=_cp(("arbitrary",)))(
            z_a, z_a, z_a, z_a, z_a, z_a, z_a, z_a, o_attn, dmix, dmix, dmix, w_conv, g_conv_out, g_attn_out,
            *scatter)
    return tuple(outs[:5]) + (outs[5:],)


def _gate_fwd(f, b_pad):
    s = f.shape[0]
    tm = min(TQ, s)

    def body(f_ref, b_ref, fb_ref, carry):
        @pl.when(pl.program_id(0) == 0)
        def _():
            carry[...] = jnp.zeros_like(carry)

        z = f_ref[...] + b_ref[...]
        x = jnp.minimum(z, 0.0) - jnp.log1p(jnp.exp(-jnp.abs(z)))
        row = lax.broadcasted_iota(jnp.int32, (tm, LANES), 0)
        sh = 1
        while sh < tm:
            x = x + jnp.where(row >= sh, _shift_down(x, sh), 0.0)
            sh *= 2
        x = x + carry[0:1, :]
        carry[...] = jnp.broadcast_to(x[tm - 1:tm, :], carry.shape)
        head_a = lax.broadcasted_iota(jnp.int32, (tm, LANES), 1) < HEAD_DIM
        for p in range(N_PAIRS):
            fa = jnp.broadcast_to(x[:, 2 * p:2 * p + 1], (tm, LANES))
            fbv = jnp.broadcast_to(x[:, 2 * p + 1:2 * p + 2], (tm, LANES))
            fb_ref[:, p * LANES:(p + 1) * LANES] = jnp.where(head_a, fa, fbv)

    return pl.pallas_call(
        body, name="gate_fwd", grid=(s // tm,),
        in_specs=[pl.BlockSpec((tm, LANES), lambda i: (i, 0)), pl.BlockSpec((1, LANES), lambda i: (0, 0))],
        out_specs=pl.BlockSpec((tm, N_PAIRS * LANES), lambda i: (i, 0)),
        out_shape=jax.ShapeDtypeStruct((s, N_PAIRS * LANES), F32),
        scratch_shapes=[pltpu.VMEM((HALO, LANES), F32)],
        compiler_params=_cp(("arbitrary",)))(f, b_pad)


def _gate_bwd(f, b_pad, d_f):
    s = f.shape[0]
    tm = min(TQ, s)
    n_blk = s // tm

    def body(f_ref, b_ref, d_ref, df_ref, gb_ref, carry):
        @pl.when(pl.program_id(0) == 0)
        def _():
            carry[...] = jnp.zeros_like(carry)
            gb_ref[...] = jnp.zeros_like(gb_ref)

        x = d_ref[...]
        row = lax.broadcasted_iota(jnp.int32, (tm, LANES), 0)
        sh = 1
        while sh < tm:
            x = x + jnp.where(row < tm - sh, _shift_up(x, sh), 0.0)
            sh *= 2
        x = x + carry[0:1, :]
        carry[...] = jnp.broadcast_to(x[0:1, :], carry.shape)
        z = f_ref[...] + b_ref[...]
        d = x * (1.0 / (1.0 + jnp.exp(z)))
        df_ref[...] = d.astype(BF16)
        gb_ref[...] += jnp.sum(d, axis=0, keepdims=True)

    rev = pl.BlockSpec((tm, LANES), lambda i: (n_blk - 1 - i, 0))
    vec = pl.BlockSpec((1, LANES), lambda i: (0, 0))
    return pl.pallas_call(
        body, name="gate_bwd", grid=(n_blk,), in_specs=[rev, vec, rev], out_specs=[rev, vec],
        out_shape=[jax.ShapeDtypeStruct((s, LANES), BF16), jax.ShapeDtypeStruct((1, LANES), F32)],
        scratch_shapes=[pltpu.VMEM((HALO, LANES), F32)],
        compiler_params=_cp(("arbitrary",)))(f, b_pad, d_f)


_NT = (((1,), (1,)), ((), ()))
_NN = (((1,), (0,)), ((), ()))


AUG = HEAD_DIM
NORM_MARGIN = 1.01


def _split3(x):
    hi = x.astype(BF16).astype(F32)
    r = x - hi
    mid = r.astype(BF16).astype(F32)
    lo = (r - mid).astype(BF16).astype(F32)
    return hi, mid, lo


def _aug(base, lane, vals):
    out = jnp.where(lane < AUG, base, 0.0)
    for k, v in enumerate(vals):
        out = jnp.where(lane == AUG + k, v, out)
    return out


def _attn_prep(qkv, fb, bigs):
    s = qkv.shape[0]
    tq = min(TQ, s)
    n_q = s // tq

    n = len(bigs)
    arrays, landing, sems = _gather_operands(bigs, [])

    def body(q_ref, k_ref, v_ref, fb_ref, *rest):
        qx_ref, kx_ref, kxt_ref, vx_ref, vt_ref, b_ref = rest[2 * n:2 * n + 6]
        finish = _hosted_gather((rest[:n], rest[2 * n + 6:3 * n + 6]) + tuple(rest[3 * n + 6:]), n, n,
                                pl.program_id(0), n_q)
        lane = lax.broadcasted_iota(jnp.int32, (tq, LANES), 1)
        lane8 = lax.broadcasted_iota(jnp.int32, (HALO, LANES), 1)
        head_lanes = lane < AUG
        is_lane = [lane == AUG + k for k in range(6)]
        first3 = (lane >= AUG) & (lane < AUG + 3)
        next3 = (lane >= AUG + 3) & (lane < AUG + 6)
        q_const = jnp.where(first3, -1.0, 0.0)
        k_const = jnp.where(next3, 1.0, 0.0)
        v_const = jnp.where(first3, 1.0, 0.0)
        ones_head = (lax.broadcasted_iota(jnp.int32, (LANES, LANES), 0) < HEAD_DIM).astype(BF16)
        acc = jnp.zeros((HALO, LANES), F32)
        for p in range(N_PAIRS):
            cols = slice(p * LANES, (p + 1) * LANES)
            q2, k2, v2 = (ref[:, cols].astype(F32) for ref in (q_ref, k_ref, v_ref))
            q2 = q2 * Q_SCALE
            f2 = fb_ref[:, cols]
            for hh in range(2):
                h = 2 * p + hh
                q, k, v = ((pltpu.roll(x, HEAD_DIM, 1) if hh else x) for x in (q2, k2, v2))
                f = f2 if hh else pltpu.roll(f2, HEAD_DIM, 1)
                hi, mid, lo = _split3(f)
                q_aug = jnp.where(is_lane[3], hi, jnp.where(is_lane[4], mid, jnp.where(is_lane[5], lo, q_const)))
                k_aug = jnp.where(is_lane[0], hi, jnp.where(is_lane[1], mid, jnp.where(is_lane[2], lo, k_const)))
                kx = jnp.where(head_lanes, k, k_aug)
                vx = jnp.where(head_lanes, v, v_const)
                qx_ref[h] = jnp.where(head_lanes, q, q_aug).astype(BF16)
                kx_ref[h] = kx.astype(BF16)
                vx_ref[h] = vx.astype(BF16)
                kxt_ref[h, 0] = kx.T.astype(BF16)
                vt_ref[h, 0] = vx.T.astype(BF16)
                q_sq = lax.dot_general((q * q).astype(BF16), ones_head, _NN, preferred_element_type=F32)
                k_sq = lax.dot_general((k * k).astype(BF16), ones_head, _NN, preferred_element_type=F32)
                diag = lax.dot_general((q * k).astype(BF16), ones_head, _NN, preferred_element_type=F32)
                diag = diag - jnp.sqrt(q_sq * k_sq) * (NORM_MARGIN - 1.0)
                vals = (jnp.sqrt(jnp.max(q_sq, axis=0, keepdims=True)), jnp.sqrt(jnp.max(k_sq, axis=0, keepdims=True)),
                        jnp.max(f - diag, axis=0, keepdims=True), f[tq - 1:tq, :])
                for slot, val in enumerate(vals):
                    acc = jnp.where(lane8 == slot * N_HEADS + h, val[:, AUG:AUG + 1], acc)
        b_ref[0] = acc
        finish()

    blk = lambda j: pl.BlockSpec((tq, ATTN_W), lambda i: (i, j))
    rows = pl.BlockSpec((N_HEADS, tq, LANES), lambda i: (0, i, 0))
    cols_t = pl.BlockSpec((N_HEADS, 1, LANES, tq), lambda i: (0, i, 0, 0))
    shp = jax.ShapeDtypeStruct((N_HEADS, s, LANES), BF16)
    shp_t = jax.ShapeDtypeStruct((N_HEADS, n_q, LANES, tq), BF16)
    outs = pl.pallas_call(
        body, name="attn_prep", grid=(n_q,), in_specs=[blk(0), blk(1), blk(2), blk(0)] + _hbm_specs(2 * n),
        out_specs=[rows, rows, cols_t, rows, cols_t,
                   pl.BlockSpec((1, HALO, LANES), lambda i: (i, 0, 0))] + _hbm_specs(n),
        out_shape=[shp, shp, shp_t, shp, shp_t, jax.ShapeDtypeStruct((n_q, HALO, LANES), F32)]
        + [jax.ShapeDtypeStruct(b.shape, b.dtype) for b in landing],
        input_output_aliases={4 + n + k: 6 + k for k in range(n)}, scratch_shapes=sems,
        compiler_params=_cp(("arbitrary",)))(qkv, qkv, qkv, fb, *arrays, *landing)
    return tuple(outs[:6]) + (outs[6:],)


def _key_block_ranges(bounds):
    t = bounds[:, 0, :]
    nh = N_HEADS
    a, b, c, e = t[:, 0:nh], t[:, nh:2 * nh], t[:, 2 * nh:3 * nh], t[:, 3 * nh:4 * nh]
    bound = a[:, None, :] * b[None, :, :] * NORM_MARGIN + c[:, None, :] - e[None, :, :]
    n_q = t.shape[0]
    idx = jnp.arange(n_q)
    need = jnp.logical_not(bound < -(EXP_ZERO + 2.0)) | (idx[None, :, None] >= idx[:, None, None])
    first = jnp.argmax(need, axis=1).astype(jnp.int32)
    first = jnp.min(first.reshape(n_q, N_PAIRS, 2), axis=-1)
    visits = (first[:, None, :] <= idx[None, :, None]) & (idx[None, :, None] <= idx[:, None, None])
    last = jnp.max(jnp.where(visits, idx[:, None, None], 0), axis=0).astype(jnp.int32)
    return first.T.reshape(-1), last.T.reshape(-1)


def _attn_fwd_t(qx, kx, vt, first_blk):
    _, s, _ = qx.shape
    tq = min(TQ, s)
    n_q = s // tq
    neg = -1e30

    def body(first_ref, qx_ref, kx_ref, vt_ref, o_ref, lse_ref, acc_ref, m_ref, s_even, s_odd):
        p = pl.program_id(0)
        i = pl.program_id(1)
        acc_ref[...] = jnp.zeros(acc_ref.shape, F32)
        m_ref[...] = jnp.full(m_ref.shape, neg, F32)
        key_le_query = (lax.broadcasted_iota(jnp.int32, (tq, tq), 0) <= lax.broadcasted_iota(jnp.int32, (tq, tq), 1))
        first = first_ref[p * n_q + i]

        def scores(kb, hh, dst):
            rows_k = pl.ds(pl.multiple_of(kb * tq, tq), tq)
            dst[hh] = lax.dot_general(kx_ref[hh, rows_k, :], qx_ref[hh], _NT, preferred_element_type=F32)

        def step(kb, src, nxt):
            for hh in range(2):
                st = src[hh]
                if nxt is None:
                    st = jnp.where(key_le_query, st, -jnp.inf)
                else:
                    scores(kb + 1, hh, nxt)
                m_old = m_ref[hh]
                m_new = jnp.maximum(m_old, jnp.max(st, axis=0, keepdims=True))
                m_ref[hh] = m_new
                pt = jnp.exp(st - m_new).astype(BF16)
                acc_ref[hh] = acc_ref[hh] * jnp.exp(m_old - m_new) + lax.dot_general(
                    vt_ref[hh, kb], pt, _NN, preferred_element_type=F32)

        def by_parity(kb, fn):
            @pl.when(kb % 2 == 0)
            def _():
                fn(s_even, s_odd)

            @pl.when(kb % 2 == 1)
            def _():
                fn(s_odd, s_even)

        def first_scores(src, nxt):
            scores(first, 0, src)
            scores(first, 1, src)

        def unmasked(kb, carry):
            by_parity(kb, lambda src, nxt: step(kb, src, nxt))
            return carry

        by_parity(first, first_scores)
        lax.fori_loop(first, i, unmasked, 0)
        by_parity(i, lambda src, nxt: step(i, src, None))
        outs, lses = [], []
        for hh in range(2):
            acc = acc_ref[hh]
            l = acc[AUG:AUG + 1, :]
            outs.append(acc[0:HEAD_DIM, :] / l)
            lses.append(m_ref[hh] + jnp.log(l))
        o_ref[...] = jnp.concatenate(outs, axis=0).T
        rows8 = lax.broadcasted_iota(jnp.int32, (N_HEADS, tq), 0)
        lse_ref[0, 0] = jnp.where(rows8 == 0, lses[0], jnp.where(rows8 == 1, lses[1], 0.0))

    grid_spec = pltpu.PrefetchScalarGridSpec(
        num_scalar_prefetch=1, grid=(N_PAIRS, n_q),
        in_specs=[pl.BlockSpec((2, tq, LANES), lambda p, i, first: (p, i, 0)),
                  pl.BlockSpec((2, s, LANES), lambda p, i, first: (p, 0, 0)),
                  pl.BlockSpec((2, n_q, LANES, tq), lambda p, i, first: (p, 0, 0, 0))],
        out_specs=[pl.BlockSpec((tq, LANES), lambda p, i, first: (i, p)),
                   pl.BlockSpec((1, 1, N_HEADS, tq), lambda p, i, first: (p, i, 0, 0))],
        scratch_shapes=[pltpu.VMEM((2, LANES, tq), F32), pltpu.VMEM((2, 1, tq), F32),
                        pltpu.VMEM((2, tq, tq), F32), pltpu.VMEM((2, tq, tq), F32)])
    return pl.pallas_call(
        body, name="attn_fwd", grid_spec=grid_spec,
        out_shape=[jax.ShapeDtypeStruct((s, ATTN_W), F32), jax.ShapeDtypeStruct((N_PAIRS, n_q, N_HEADS, tq), F32)],
        compiler_params=_cp(("parallel", "arbitrary")))(first_blk, qx, kx, vt)


def _attn_bwd_t(qx, dox, kx, kxt, vx, lse, last_blk):
    _, s, _ = qx.shape
    tq = min(TQ, s)
    n_q = s // tq

    def body(last_ref, qx_ref, dox_ref, lse_ref, kx_ref, kxt_ref, vx_ref, dk_ref, dv_ref, dfk_ref, dqt_ref, dfq_ref):
        p = pl.program_id(0)
        j = pl.program_id(1)

        @pl.when(j == 0)
        def _():
            dqt_ref[...] = jnp.zeros(dqt_ref.shape, F32)
            dfq_ref[...] = jnp.zeros(dfq_ref.shape, F32)

        key_le_query = (lax.broadcasted_iota(jnp.int32, (tq, tq), 0) <= lax.broadcasted_iota(jnp.int32, (tq, tq), 1))

        def step(i, carry, masked):
            rows_q = pl.ds(pl.multiple_of(i * tq, tq), tq)
            out = []
            for hh in range(2):
                dk, dv, col = carry[3 * hh:3 * hh + 3]
                q, do = qx_ref[hh, rows_q, :], dox_ref[hh, rows_q, :]
                st = lax.dot_general(kx_ref[hh], q, _NT, preferred_element_type=F32)
                pt = jnp.exp(st - lse_ref[0, i, hh:hh + 1, :])
                if masked:
                    pt = jnp.where(key_le_query, pt, 0.0)
                dst = pt * lax.dot_general(vx_ref[hh], do, _NT, preferred_element_type=F32)
                pb, dsb = pt.astype(BF16), dst.astype(BF16)
                dv = dv + lax.dot_general(pb, do, _NN, preferred_element_type=F32)
                dk = dk + lax.dot_general(dsb, q, _NN, preferred_element_type=F32)
                dqt_ref[hh, i] += lax.dot_general(kxt_ref[hh, 0], dsb, _NN, preferred_element_type=F32)
                for cb in range(tq // LANES):
                    col = col + dst[:, cb * LANES:(cb + 1) * LANES]
                dfq_ref[hh, i] += jnp.sum(dst.reshape(tq // HALO, HALO, tq), axis=0)
                out += [dk, dv, col]
            return tuple(out)

        zero = jnp.zeros((tq, LANES), F32)
        carry = step(j, (zero,) * 6, True)
        dk_a, dv_a, col_a, dk_b, dv_b, col_b = lax.fori_loop(j + 1, last_ref[p * n_q + j] + 1,
                                                             lambda i, cr: step(i, cr, False), carry)
        head_a = lax.broadcasted_iota(jnp.int32, (tq, LANES), 1) < HEAD_DIM
        dk_ref[...] = jnp.where(head_a, dk_a, pltpu.roll(dk_b, HEAD_DIM, 1)).astype(BF16)
        dv_ref[...] = jnp.where(head_a, dv_a, pltpu.roll(dv_b, HEAD_DIM, 1)).astype(BF16)
        rows8 = lax.broadcasted_iota(jnp.int32, (N_HEADS, tq), 0)
        dfk_a, dfk_b = (-jnp.sum(c.T, axis=0, keepdims=True) for c in (col_a, col_b))
        dfk_ref[0, 0] = jnp.where(rows8 == 0, dfk_a, jnp.where(rows8 == 1, dfk_b, 0.0))

    resident = pl.BlockSpec((2, s, LANES), lambda p, j, last: (p, 0, 0))
    key_rows = pl.BlockSpec((2, tq, LANES), lambda p, j, last: (p, j, 0))
    pair_out = pl.BlockSpec((tq, LANES), lambda p, j, last: (j, p))
    grid_spec = pltpu.PrefetchScalarGridSpec(
        num_scalar_prefetch=1, grid=(N_PAIRS, n_q),
        in_specs=[resident, resident, pl.BlockSpec((1, n_q, N_HEADS, tq), lambda p, j, last: (p, 0, 0, 0)),
                  key_rows, pl.BlockSpec((2, 1, LANES, tq), lambda p, j, last: (p, j, 0, 0)), key_rows],
        out_specs=[pair_out, pair_out, pl.BlockSpec((1, 1, N_HEADS, tq), lambda p, j, last: (p, j, 0, 0)),
                   pl.BlockSpec((2, n_q, LANES, tq), lambda p, j, last: (p, 0, 0, 0)),
                   pl.BlockSpec((2, n_q, HALO, tq), lambda p, j, last: (p, 0, 0, 0))])
    return pl.pallas_call(
        body, name="attn_bwd", grid_spec=grid_spec,
        out_shape=[jax.ShapeDtypeStruct((s, ATTN_W), BF16), jax.ShapeDtypeStruct((s, ATTN_W), BF16),
                   jax.ShapeDtypeStruct((N_PAIRS, n_q, N_HEADS, tq), F32),
                   jax.ShapeDtypeStruct((N_HEADS, n_q, LANES, tq), F32),
                   jax.ShapeDtypeStruct((N_HEADS, n_q, HALO, tq), F32)],
        compiler_params=_cp(("parallel", "arbitrary")))(last_blk, qx, dox, lse, kx, kxt, vx)


def _attn_dq_finish(dqt):
    _, n_q, _, tq = dqt.shape
    per = 4 if n_q % 4 == 0 else 1

    def body(dqt_ref, dq_ref):
        for b in range(per):
            a, bb = dqt_ref[0, b], dqt_ref[1, b]
            dq_ref[b * tq:(b + 1) * tq, :] = (
                jnp.concatenate([a[0:HEAD_DIM], bb[0:HEAD_DIM]], axis=0).T * Q_SCALE).astype(BF16)

    return pl.pallas_call(
        body, name="attn_dq_finish", grid=(N_PAIRS, n_q // per),
        in_specs=[pl.BlockSpec((2, per, LANES, tq), lambda p, i: (p, i, 0, 0))],
        out_specs=pl.BlockSpec((per * tq, LANES), lambda p, i: (i, p)),
        out_shape=jax.ShapeDtypeStruct((n_q * tq, ATTN_W), BF16),
        compiler_params=_cp(("parallel", "parallel")))(dqt)


def _ffn_act_fwd(up, w_ffn):
    s = up.shape[0]
    tm, tn = min(TM_FFN, s), TN_FFN
    nb = D_FF // tn

    def body(a_ref, g_ref, ap_ref, gp_ref, wa_ref, wg_ref, act_ref, u_ref):
        i = pl.program_id(1)

        def conv(blk_ref, prev_ref, w_ref):
            prev = jnp.where(i > 0, prev_ref[...].astype(F32)[HALO:], 0.0)
            return _conv_taps(jnp.concatenate([prev, blk_ref[...].astype(F32)], axis=0), w_ref[...])[HALO:]

        u_a, u_g = conv(a_ref, ap_ref, wa_ref), conv(g_ref, gp_ref, wg_ref)
        u_ref[0], u_ref[1] = u_a.astype(BF16), u_g.astype(BF16)
        act_ref[...] = (u_g * jax.nn.sigmoid(u_g) * u_a).astype(BF16)

    blk = lambda off: pl.BlockSpec((tm, tn), lambda n, i: (i, off + n))
    prev = lambda off: pl.BlockSpec(
        (HALO_BF16, tn), lambda n, i: (jnp.maximum(i * (tm // HALO_BF16) - 1, 0), off + n))
    wsp = lambda off: pl.BlockSpec((3, tn), lambda n, i: (0, off + n))
    return pl.pallas_call(
        body, name="ffn_act_fwd", grid=(nb, s // tm),
        in_specs=[blk(0), blk(nb), prev(0), prev(nb), wsp(0), wsp(nb)],
        out_specs=[pl.BlockSpec((tm, tn), lambda n, i: (i, n)), pl.BlockSpec((2, tm, tn), lambda n, i: (0, i, n))],
        out_shape=[jax.ShapeDtypeStruct((s, D_FF), BF16), jax.ShapeDtypeStruct((2, s, D_FF), BF16)],
        compiler_params=_cp(("parallel", "parallel")))(up, up, up, up, w_ffn, w_ffn)


def _ffn_act_bwd(up, u, dact, w_ffn):
    s = up.shape[0]
    tm, tn = min(TM_FFN, s), TN_FFN
    nb = D_FF // tn
    n_blk = s // tm

    def body(u_ref, un_ref, a_ref, g_ref, d_ref, dn_ref, wa_ref, wg_ref, dup_ref, gwa_ref, gwg_ref):
        i = pl.program_id(1)

        @pl.when(i == 0)
        def _():
            gwa_ref[...] = jnp.zeros_like(gwa_ref)
            gwg_ref[...] = jnp.zeros_like(gwg_ref)

        ext = lambda rows, next_rows: jnp.concatenate([rows, next_rows], axis=0)
        u_a, u_g = (ext(u_ref[h].astype(F32), un_ref[h].astype(F32)[:HALO]) for h in range(2))
        d_e = ext(d_ref[...], jnp.where(i < n_blk - 1, dn_ref[...], 0.0))
        sig = jax.nn.sigmoid(u_g)
        du_a = d_e * (u_g * sig)
        du_g = d_e * u_a * (sig * (1.0 + u_g * (1.0 - sig)))
        halves = ((gwa_ref, wa_ref[...], a_ref[...].astype(F32), du_a),
                  (gwg_ref, wg_ref[...], g_ref[...].astype(F32), du_g))
        for half, (gw_ref, w, upv, du) in enumerate(halves):
            du0, du1, du2 = du[:tm], _shift_up(du, 1)[:tm], _shift_up(du, 2)[:tm]
            dup_ref[half] = (w[2:3, :] * du0 + w[1:2, :] * du1 + w[0:1, :] * du2).astype(BF16)
            gw_ref[0:1, :] += jnp.sum(upv * du2, axis=0, keepdims=True)
            gw_ref[1:2, :] += jnp.sum(upv * du1, axis=0, keepdims=True)
            gw_ref[2:3, :] += jnp.sum(upv * du0, axis=0, keepdims=True)

    next_row = lambda halo: lambda i: jnp.minimum((i + 1) * (tm // halo), s // halo - 1)
    blk = lambda off: pl.BlockSpec((tm, tn), lambda n, i: (i, off + n))
    wsp = lambda off: pl.BlockSpec((3, tn), lambda n, i: (0, off + n))
    pair = lambda rows, row_of: pl.BlockSpec((2, rows, tn), lambda n, i: (0, row_of(i), n))
    return pl.pallas_call(
        body, name="ffn_act_bwd", grid=(nb, n_blk),
        in_specs=[pair(tm, lambda i: i), pair(HALO_BF16, next_row(HALO_BF16)), blk(0), blk(nb), blk(0),
                  pl.BlockSpec((HALO, tn), lambda n, i: (next_row(HALO)(i), n)), wsp(0), wsp(nb)],
        out_specs=[pair(tm, lambda i: i), wsp(0), wsp(0)],
        out_shape=[jax.ShapeDtypeStruct((2, s, D_FF), BF16),
                   jax.ShapeDtypeStruct((3, D_FF), F32), jax.ShapeDtypeStruct((3, D_FF), F32)],
        compiler_params=_cp(("parallel", "arbitrary")))(u, u, up, up, dact, dact, w_ffn, w_ffn)


def _adamw(w, g, m, v, name):
    r, c = w.shape
    tr = next((t for t in (512, 352, 256, 128, 64, 32, 16, 8) if r > t and r % t == 0), r)

    def body(w_ref, g_ref, m_ref, v_ref, d_ref, nm_ref, nv_ref):
        gv = g_ref[...]
        m_new = ADAM_B1 * m_ref[...] + (1.0 - ADAM_B1) * gv
        v_new = ADAM_B2 * v_ref[...] + (1.0 - ADAM_B2) * (gv * gv)
        m_hat = m_new / (1.0 - ADAM_B1 ** ADAM_STEP)
        v_hat = v_new / (1.0 - ADAM_B2 ** ADAM_STEP)
        d_ref[...] = -ADAM_LR * (m_hat / (jnp.sqrt(v_hat) + ADAM_EPS) + ADAM_WD * w_ref[...])
        nm_ref[...] = m_new
        nv_ref[...] = v_new

    spec = pl.BlockSpec((tr, c), lambda i: (i, 0))
    shp = jax.ShapeDtypeStruct((r, c), F32)
    return pl.pallas_call(
        body, name=name, grid=(r // tr,), in_specs=[spec] * 4, out_specs=[spec] * 3, out_shape=[shp] * 3,
        compiler_params=_cp(("parallel",)))(w, g, m, v)


def _sum_rows_block(h):
    return h if h <= 352 else 256


def _pair_sum(view, recv, sel, name):
    n, _, h, c = view.shape
    tr = _sum_rows_block(h)

    def body(sel_ref, a_ref, b_ref, o_ref, ob_ref):
        t = a_ref[...] + b_ref[...]
        o_ref[...] = t
        ob_ref[...] = t.astype(BF16)

    blk = pl.BlockSpec((None, tr, c), lambda j, i, sel_ref: (j, i, 0))
    grid_spec = pltpu.PrefetchScalarGridSpec(
        num_scalar_prefetch=1, grid=(n, h // tr),
        in_specs=[pl.BlockSpec((None, None, tr, c), lambda j, i, sel_ref: (j, sel_ref[0], i, 0)),
                  pl.BlockSpec((None, None, tr, c), lambda j, i, sel_ref: (j, 0, i, 0))],
        out_specs=[blk, blk])
    return pl.pallas_call(
        body, name=name, grid_spec=grid_spec,
        out_shape=[jax.ShapeDtypeStruct((n, h, c), F32), jax.ShapeDtypeStruct((n, h, c), BF16)],
        compiler_params=_cp(("parallel", "parallel")))(sel, view, recv)


def _chip_sum(pair, got, sel, name):
    _, h, c = pair.shape
    tr = _sum_rows_block(h)
    nblk = h // tr

    def body(sel_ref, p_ref, g0_ref, g1_ref, g2_ref, o_ref):
        o_ref[...] = ((p_ref[...] + g0_ref[...].astype(F32)) + g1_ref[...].astype(F32)) + g2_ref[...].astype(F32)

    slot = lambda k: pl.BlockSpec((None, tr, c), lambda i, sel_ref: (k, i, 0))
    grid_spec = pltpu.PrefetchScalarGridSpec(
        num_scalar_prefetch=1, grid=(h // tr,),
        in_specs=[pl.BlockSpec((None, tr, c), lambda i, sel_ref: (sel_ref[1], i, 0)), slot(0), slot(1), slot(2)],
        out_specs=pl.BlockSpec((tr, c), lambda i, sel_ref: (sel_ref[0] * nblk + i, 0)))
    return pl.pallas_call(
        body, name=name, grid_spec=grid_spec, out_shape=jax.ShapeDtypeStruct((2 * h, c), F32),
        compiler_params=_cp(("parallel",)))(sel, pair, got, got, got)


def _place():
    return lax.axis_index("x"), lax.axis_index("y"), lax.axis_index("c")


def _other_chips(x, y):
    return [(1 - x, y), (x, 1 - y), (1 - x, 1 - y)]


def _hbm_specs(n):
    return [pl.BlockSpec(memory_space=pl.ANY)] * n


def _all_gather_weights(bigs, smalls):
    nb, ns = len(bigs), len(smalls)
    n = nb + ns

    def body(*refs):
        start, forward, finish = _gather_phases(refs[:n], refs[2 * n:3 * n], nb, *refs[3 * n:])
        start()
        forward()
        finish()

    arrays, landing, sems = _gather_operands(bigs, smalls)
    return pl.pallas_call(
        body, name="all_gather_weights",
        out_shape=[jax.ShapeDtypeStruct(b.shape, b.dtype) for b in landing],
        in_specs=_hbm_specs(2 * n), out_specs=_hbm_specs(n), input_output_aliases={n + k: k for k in range(n)},
        scratch_shapes=sems)(*arrays, *landing)


def _hosted_gather(refs, n, nb, step, total):
    ins, outs, send_sems, recv_sems = refs
    start, forward, finish = _gather_phases(ins, outs, nb, send_sems, recv_sems)
    pl.when(step == 0)(start)
    pl.when(step == (3 * total) // 4)(forward)
    return lambda: pl.when(step == total - 1)(finish)


def _in_proj(x, g, w_a, w_b, w_c, bigs, smalls):
    s, d = x.shape
    tm, tn = min(TM_MM, s), ATTN_W
    na, nq = w_a.shape[1] // tn, w_b.shape[1] // tn
    steps = na + nq + 1
    total = (s // tm) * steps
    nb, n = len(bigs), len(bigs) + len(smalls)
    arrays, landing, sems = _gather_operands(bigs, smalls)

    def body(x_ref, g_ref, wa_ref, wb_ref, wc_ref, *rest):
        z_ref, qkv_ref, f_ref, h_ref = rest[2 * n:2 * n + 4]
        h_scr = rest[-1]
        m, j = pl.program_id(0), pl.program_id(1)
        finish = _hosted_gather((rest[:n], rest[2 * n + 4:3 * n + 4]) + tuple(rest[3 * n + 4:3 * n + 6]), n, nb,
                                m * steps + j, total)

        @pl.when(j == 0)
        def _():
            xv = x_ref[...]
            hv = (xv * _rstd(xv) * g_ref[...]).astype(BF16)
            h_scr[...] = hv
            h_ref[...] = hv

        h = h_scr[...]

        @pl.when(j < na)
        def _():
            z_ref[...] = lax.dot_general(h, wa_ref[...], _NN, preferred_element_type=F32)

        @pl.when((j >= na) & (j < na + nq))
        def _():
            qkv_ref[...] = lax.dot_general(h, wb_ref[...], _NN, preferred_element_type=F32).astype(BF16)

        @pl.when(j == na + nq)
        def _():
            f_ref[...] = lax.dot_general(h, wc_ref[...], _NN, preferred_element_type=F32)

        finish()

    blk_a = lambda m, j: (m, jnp.minimum(j, na - 1))
    blk_b = lambda m, j: (m, jnp.clip(j - na, 0, nq - 1))
    outs = pl.pallas_call(
        body, name="in_proj", grid=(s // tm, steps),
        in_specs=[pl.BlockSpec((tm, d), lambda m, j: (m, 0)), pl.BlockSpec((1, d), lambda m, j: (0, 0)),
                  pl.BlockSpec((d, tn), lambda m, j: (0, jnp.minimum(j, na - 1))),
                  pl.BlockSpec((d, tn), lambda m, j: (0, jnp.clip(j - na, 0, nq - 1))),
                  pl.BlockSpec((d, LANES), lambda m, j: (0, 0))] + _hbm_specs(2 * n),
        out_specs=[pl.BlockSpec((tm, tn), blk_a), pl.BlockSpec((tm, tn), blk_b),
                   pl.BlockSpec((tm, LANES), lambda m, j: (m, 0)), pl.BlockSpec((tm, d), lambda m, j: (m, 0))]
        + _hbm_specs(n),
        out_shape=[jax.ShapeDtypeStruct((s, w_a.shape[1]), F32), jax.ShapeDtypeStruct((s, w_b.shape[1]), BF16),
                   jax.ShapeDtypeStruct((s, LANES), F32), jax.ShapeDtypeStruct((s, d), BF16)]
        + [jax.ShapeDtypeStruct(b.shape, b.dtype) for b in landing],
        input_output_aliases={5 + n + k: 4 + k for k in range(n)},
        scratch_shapes=sems + [pltpu.VMEM((tm, d), BF16)],
        compiler_params=_cp(("arbitrary", "arbitrary")))(x, g, w_a, w_b, w_c, *arrays, *landing)
    return outs[0], outs[1], outs[2], outs[3], outs[4:]


def _gather_operands(bigs, smalls):
    x, y, _ = _place()
    arrays = list(bigs) + list(smalls)
    landing = [lax.dynamic_update_index_in_dim(lax.empty((N_CHIPS,) + a.shape, a.dtype), a, 2 * x + y, 0)
               for a in arrays]
    n_sems = 6 * len(bigs) + 3 * len(smalls)
    return arrays, landing, [pltpu.SemaphoreType.DMA((n_sems,)), pltpu.SemaphoreType.DMA((n_sems,))]


def _gather_phases(ins, outs, nb, send_sems, recv_sems):
    n = len(ins)
    x, y, c = _place()
    my_chip = 2 * x + y
    chips = _other_chips(x, y)
    sibling = (x, y, 1 - c)

    def rows(k, which):
        h = ins[k].shape[0] // 2
        return pl.ds(which * h, h)

    def copy(sem, src, dst, to):
        return pltpu.make_async_remote_copy(src_ref=src, dst_ref=dst, send_sem=send_sems.at[sem],
                                            recv_sem=recv_sems.at[sem], device_id=to, device_id_type=MESH)

    def sends():
        out = [copy(6 * k + j, ins[k].at[rows(k, c)], outs[k].at[my_chip, rows(k, c)], (cx, cy, c))
               for k in range(nb) for j, (cx, cy) in enumerate(chips)]
        return out + [copy(6 * nb + 3 * (k - nb) + j, ins[k], outs[k].at[my_chip], (cx, cy, c))
                      for k in range(nb, n) for j, (cx, cy) in enumerate(chips)]

    def landed(k, j, which):
        cx, cy = chips[j]
        return outs[k].at[2 * cx + cy, rows(k, which)]

    def forwards():
        return [copy(6 * k + 3 + j, landed(k, j, c), landed(k, j, c), sibling)
                for j in range(3) for k in range(nb)]

    def start():
        for cp in sends():
            cp.start()

    def forward():
        for j in range(3):
            for k in range(nb):
                copy(6 * k + j, landed(k, j, c), landed(k, j, c), (x, y, c)).wait_recv()
                copy(6 * k + 3 + j, landed(k, j, c), landed(k, j, c), sibling).start()

    def finish():
        for j, (cx, cy) in enumerate(chips):
            for k in range(nb):
                copy(6 * k + 3 + j, landed(k, j, 1 - c), landed(k, j, 1 - c), (x, y, c)).wait_recv()
            for k in range(nb, n):
                arrived = outs[k].at[2 * cx + cy]
                copy(6 * nb + 3 * (k - nb) + j, arrived, arrived, (x, y, c)).wait_recv()
        for cp in sends() + forwards():
            cp.wait_send()

    return start, forward, finish


def _pair_exchange(views, name):
    n = len(views)

    def body(*refs):
        ins, outs, send_sems, recv_sems = refs[:n], refs[n:2 * n], refs[2 * n], refs[2 * n + 1]
        x, y, c = _place()
        copies = [pltpu.make_async_remote_copy(
            src_ref=ins[k].at[:, pl.ds(1 - c, 1)], dst_ref=outs[k], send_sem=send_sems.at[k],
            recv_sem=recv_sems.at[k], device_id=(x, y, 1 - c), device_id_type=MESH) for k in range(n)]
        for cp in copies:
            cp.start()
        for cp in copies:
            cp.wait()

    return pl.pallas_call(
        body, name=name,
        out_shape=[jax.ShapeDtypeStruct((v.shape[0], 1) + v.shape[2:], v.dtype) for v in views],
        in_specs=_hbm_specs(n), out_specs=_hbm_specs(n),
        scratch_shapes=[pltpu.SemaphoreType.DMA((n,)), pltpu.SemaphoreType.DMA((n,))])(*views)


def _scatter_operands(parts):
    n = len(parts)
    return ([jax.ShapeDtypeStruct((3,) + p.shape[1:], p.dtype) for p in parts],
            [pltpu.SemaphoreType.DMA((3 * n,)), pltpu.SemaphoreType.DMA((3 * n,))])


def _scatter_phases(ins, outs, send_sems, recv_sems):
    x, y, c = _place()

    def copies():
        return [pltpu.make_async_remote_copy(
            src_ref=ins[k].at[pl.ds(2 * cx + cy, 1)], dst_ref=outs[k].at[pl.ds(r, 1)], send_sem=send_sems.at[3 * k + r],
            recv_sem=recv_sems.at[3 * k + r], device_id=(cx, cy, c), device_id_type=MESH)
            for k in range(len(ins)) for r, (cx, cy) in enumerate(_other_chips(x, y))]

    def start():
        for cp in copies():
            cp.start()

    def finish():
        for cp in copies():
            cp.wait()

    return start, finish


def _hosted_scatter(ins, outs, sems, step, total):
    start, finish = _scatter_phases(ins, outs, *sems)
    pl.when(step == 0)(start)
    return lambda: pl.when(step == total - 1)(finish)


def _join_halves(shards):
    n = len(shards)

    def body(*refs):
        ins, outs, send_sems, recv_sems = refs[:n], refs[n:2 * n], refs[2 * n], refs[2 * n + 1]
        x, y, c = _place()

        def rows(ref, which):
            h = ref.shape[0] // 2
            return ref.at[pl.ds(which * h, h)]

        sent = [pltpu.make_async_remote_copy(
            src_ref=rows(ins[k], c), dst_ref=rows(outs[k], c), send_sem=send_sems.at[k], recv_sem=recv_sems.at[k],
            device_id=(x, y, 1 - c), device_id_type=MESH) for k in range(n)]
        for cp in sent:
            cp.start()
        for k in range(n):
            pltpu.make_async_remote_copy(
                src_ref=rows(ins[k], 1 - c), dst_ref=rows(outs[k], 1 - c), send_sem=send_sems.at[k],
                recv_sem=recv_sems.at[k], device_id=(x, y, 1 - c), device_id_type=MESH).wait_recv()
        for cp in sent:
            cp.wait_send()

    return pl.pallas_call(
        body, name="half_exchange", out_shape=[jax.ShapeDtypeStruct(a.shape, a.dtype) for a in shards],
        in_specs=_hbm_specs(n), out_specs=_hbm_specs(n), input_output_aliases={k: k for k in range(n)},
        scratch_shapes=[pltpu.SemaphoreType.DMA((n,)), pltpu.SemaphoreType.DMA((n,))])(*shards)


def _all_reduce_small(packet):
    rows, width = packet.shape
    n_dev = 8

    def body(x_ref, out_ref, gath, send_sems, recv_sems):
        x, y, c = _place()
        me, sibling = (x, y, c), (x, y, 1 - c)
        chips = _other_chips(x, y)

        def slot(px, py, pc):
            return gath.at[pl.ds((4 * px + 2 * py + pc) * rows, rows), :]

        def copy(k, block, to, src=None):
            return pltpu.make_async_remote_copy(
                src_ref=slot(*block) if src is None else src, dst_ref=slot(*block), send_sem=send_sems.at[k],
                recv_sem=recv_sems.at[k], device_id=to, device_id_type=MESH)

        first = [copy(0, me, sibling, src=x_ref)]
        first += [copy(1 + j, me, (*chip, c), src=x_ref) for j, chip in enumerate(chips)]
        for cp in first:
            cp.start()
        gath[pl.ds((4 * x + 2 * y + c) * rows, rows), :] = x_ref[...]
        passed = [copy(4 + j, (*chip, c), sibling) for j, chip in enumerate(chips)]
        for j, chip in enumerate(chips):
            copy(1 + j, (*chip, c), me).wait_recv()
            passed[j].start()
        copy(0, sibling, me).wait_recv()
        for j, chip in enumerate(chips):
            copy(4 + j, (*chip, 1 - c), me).wait_recv()
        for cp in first + passed:
            cp.wait_send()
        acc = gath[0:rows, :]
        for d in range(1, n_dev):
            acc = acc + gath[d * rows:(d + 1) * rows, :]
        out_ref[...] = acc

    return pl.pallas_call(
        body, name="all_reduce_small", out_shape=jax.ShapeDtypeStruct((rows, width), F32),
        in_specs=[pl.BlockSpec(memory_space=pltpu.VMEM)], out_specs=pl.BlockSpec(memory_space=pltpu.VMEM),
        scratch_shapes=[pltpu.VMEM((n_dev * rows, width), F32), pltpu.SemaphoreType.DMA((7,)),
                        pltpu.SemaphoreType.DMA((7,))])(packet)


def _flat_rows(parts, width, row_multiple):
    flat = jnp.concatenate([p.astype(F32).reshape(-1) for p in parts])
    rows = -(-flat.shape[0] // width)
    rows = -(-rows // row_multiple) * row_multiple
    return jnp.pad(flat, (0, rows * width - flat.shape[0])).reshape(rows, width)


def _unflatten(flat2d, shapes):
    flat = flat2d.reshape(-1)
    out, off = [], 0
    for shp in shapes:
        n = 1
        for dim in shp:
            n *= dim
        out.append(flat[off:off + n].reshape(shp))
        off += n
    return out


def _core_and_chip():
    x, y, c = _place()
    return jnp.stack([c, 2 * x + y]).astype(jnp.int32)


def _pair_sums(chip_major, names, call_name):
    views = [g.reshape(N_CHIPS, 2, g.shape[1] // 2, g.shape[2]) for g in chip_major]
    recv = _pair_exchange(views, call_name)
    sel = _core_and_chip()
    return [_pair_sum(v, r, sel, "pair_sum_" + nm) for v, r, nm in zip(views, recv, names)]


def _finish_grads(pairs, got, names):
    sel = _core_and_chip()
    return _join_halves([_chip_sum(p, g, sel, "chip_sum_" + nm) for (p, _), g, nm in zip(pairs, got, names)])


def kernel(x, g_mix, w_in, b_f, w_conv, g_conv_out, g_attn_out, w_o, g_ffn, w_up, w_ffn_conv, w_down, g_final, loss_target, m_g_mix, m_w_in, m_b_f, m_w_conv, m_g_conv_out, m_g_attn_out, m_w_o, m_g_ffn, m_w_up, m_w_ffn_conv, m_w_down, m_g_final, v_g_mix, v_w_in, v_b_f, v_w_conv, v_g_conv_out, v_g_attn_out, v_w_o, v_g_ffn, v_w_up, v_w_ffn_conv, v_w_down, v_g_final):
    s = x.shape[1]
    x0 = x[0]
    target = loss_target[0]
    d = D_MODEL
    x_pos, y_pos, _ = _place()
    my_chip = 2 * x_pos + y_pos

    (c_in,) = _all_gather_weights([w_in[0].astype(BF16)], [])
    w_in_full = jnp.concatenate([c_in[j] for j in range(N_CHIPS)], axis=1)
    c3 = 3 * CONV_CH
    w_a, w_b = w_in_full[:, :c3], w_in_full[:, c3:c3 + 3 * ATTN_W]
    w_c = jnp.pad(w_in_full[:, c3 + 3 * ATTN_W:], ((0, 0), (0, LANES - N_HEADS)))
    w_q, w_k, w_v = (w_b[:, i * ATTN_W:(i + 1) * ATTN_W] for i in range(3))
    b_pad = jnp.pad(b_f, ((0, 0), (0, LANES - N_HEADS)))

    z_a, qkv, f_log, h1, (c_o, c_up, c_conv, c_ffn) = _in_proj(
        x0, g_mix, w_a, w_b, w_c, [w_o[0].astype(BF16), w_up[0].astype(BF16)], [w_conv[0], w_ffn_conv[0]])
    fb = _gate_fwd(f_log, b_pad)
    qx, kx, kxt, vx, vt, bounds, (c_down,) = _attn_prep(qkv, fb, [w_down[0].astype(BF16)])
    w_o_full = c_o.reshape(d, d)
    w_down_full = c_down.reshape(D_FF, d)
    w_conv_full = jnp.concatenate([c_conv[j] for j in range(N_CHIPS)], axis=1)
    w_ffn_full = jnp.concatenate([c_ffn[j] for j in range(N_CHIPS)], axis=1)
    n_up = c_up.shape[2]
    first_blk, last_blk = _key_block_ranges(bounds)
    o_attn, lse = _attn_fwd_t(qx, kx, vt, first_blk)
    mix = _mixer_fwd(z_a, o_attn, w_conv_full, g_conv_out, g_attn_out)
    x2, h2 = _mm("nn", [mix], [w_o_full], F32, 512, d, "out_proj", add=x0, norm_g=g_ffn)
    up = _mm("nn", [h2], [c_up], BF16, TM_MM, n_up, "up_proj", b_chips=True)
    act, u_conv = _ffn_act_fwd(up, w_ffn_full)
    dx3, dx3_b, loss_row, gg_final = _down_proj_loss(act, w_down_full, x2, target, g_final.reshape(1, d))

    dact = _mm("nt", [dx3_b], [w_down_full], F32, TM_MM, 1408, "d_act")
    gw_down = _mm_tn(act, dx3_b, 1408, 1024, "gw_down")
    dup, gwf_lin, gwf_gate = _ffn_act_bwd(up, u_conv, dact, w_ffn_full)
    dh2 = _mm("nt", [(dup, j // 2, j % 2, n_up) for j in range(N_CHIPS)], [(c_up, j) for j in range(N_CHIPS)],
              F32, TM_MM, 512, "d_h2")
    gw_up = _mm_tn(h2, dup, 1024, n_up, "gw_up", out_chips=True)
    dx2, dx2_b, gg_ffn = _rms_bwd(x2, dh2, g_ffn, dx3, "rms_ffn_bwd", True)
    dmix = _mm("nt", [dx2_b], [w_o_full], F32, TM_MM, 512, "d_mix")
    gw_o = _mm_tn(mix, dx2_b, 1024, 1024, "gw_o")
    early = _pair_sums([gw_o.reshape(N_CHIPS, d // N_CHIPS, d), gw_up, gw_down.reshape(N_CHIPS, D_FF // N_CHIPS, d)],
                       ["w_o", "w_up", "w_down"], "pair_exchange")
    dz_a, dox, gw_conv, gg_conv_out, gg_attn_out, (got_o, got_down) = _mixer_bwd(
        z_a, o_attn, dmix, w_conv_full, g_conv_out, g_attn_out, [early[0][1], early[2][1]])
    dk, dv, dfk, dqt, dfq = _attn_bwd_t(qx, dox, kx, kxt, vx, lse, last_blk)
    dq = _attn_dq_finish(dqt)
    d_f = (jnp.transpose(dfk[:, :, 0:2, :], (1, 3, 0, 2)).reshape(s, N_HEADS)
           + jnp.transpose(jnp.sum(dfq, axis=2), (1, 2, 0)).reshape(s, N_HEADS))
    df_b, gb_f = _gate_bwd(f_log, b_pad, jnp.pad(d_f, ((0, 0), (0, LANES - N_HEADS))))
    gw_a = _mm_tn(h1, dz_a, 1024, c3, "gw_in_conv")
    gw_q = _mm_tn(h1, dq, 1024, ATTN_W, "gw_in_q")
    gw_k = _mm_tn(h1, dk, 1024, ATTN_W, "gw_in_k")
    gw_v = _mm_tn(h1, dv, 1024, ATTN_W, "gw_in_v")
    gw_c = _mm_tn(h1, df_b, 1024, LANES, "gw_in_gate")
    gw_in = jnp.concatenate([gw_a, gw_q, gw_k, gw_v, gw_c[:, :N_HEADS]], axis=1)
    n_in = IN_COLS // N_CHIPS
    gw_in = jnp.stack([gw_in[:, j * n_in:(j + 1) * n_in] for j in range(N_CHIPS)])
    late = _pair_sums([gw_in], ["w_in"], "pair_exchange_w_in")
    dh1, (got_up, got_in) = _mm("nt", [dz_a, dq, dk, dv, df_b], [w_a, w_q, w_k, w_v, w_c], F32, TM_MM, 512, "d_h1",
                                scatter=[early[1][1], late[0][1]])
    grad_x, gg_mix = _rms_bwd(x0, dh1, g_mix, dx2, "rms_mix_bwd", False)
    g_w_in, g_w_o, g_w_up, g_w_down = _finish_grads(late + early, [got_in, got_o, got_up, got_down],
                                                    ["w_in", "w_o", "w_up", "w_down"])

    gw_ffn = jnp.concatenate([gwf_lin, gwf_gate], axis=1)
    small_parts = [gg_mix, gg_conv_out, gg_attn_out, gg_ffn, gg_final, gb_f[:, :N_HEADS], loss_row[:, 0:1], gw_conv,
                   gw_ffn]
    small_shapes = [a.shape for a in small_parts]
    tot = _unflatten(_all_reduce_small(_flat_rows(small_parts, d, 8)), small_shapes)
    g_g_mix, g_g_conv_out, g_g_attn_out, g_g_ffn, g_g_final, g_b_f, loss_sum, g_conv_full, g_ffn_full = tot
    loss = loss_sum[0, 0]
    g_g_final = g_g_final[0]
    g_w_conv = lax.dynamic_slice_in_dim(g_conv_full, my_chip * (CONV_CH // N_CHIPS), CONV_CH // N_CHIPS, axis=1)
    g_w_ffn = lax.dynamic_slice_in_dim(g_ffn_full, my_chip * n_up, n_up, axis=1)

    def adam_big(w, g, m, v, name):
        dl, nm, nv = _adamw(w[0], g, m[0], v[0], name)
        return dl[None], nm[None], nv[None]

    u_w_in = adam_big(w_in, g_w_in, m_w_in, v_w_in, "adam_w_in")
    u_w_o = adam_big(w_o, g_w_o, m_w_o, v_w_o, "adam_w_o")
    u_w_up = adam_big(w_up, g_w_up, m_w_up, v_w_up, "adam_w_up")
    u_w_down = adam_big(w_down, g_w_down, m_w_down, v_w_down, "adam_w_down")

    small_w = [g_mix, b_f, g_conv_out, g_attn_out, g_ffn, g_final, w_conv, w_ffn_conv]
    small_g = [g_g_mix, g_b_f, g_g_conv_out, g_g_attn_out, g_g_ffn, g_g_final, g_w_conv, g_w_ffn]
    small_m = [m_g_mix, m_b_f, m_g_conv_out, m_g_attn_out, m_g_ffn, m_g_final, m_w_conv, m_w_ffn_conv]
    small_v = [v_g_mix, v_b_f, v_g_conv_out, v_g_attn_out, v_g_ffn, v_g_final, v_w_conv, v_w_ffn_conv]
    shapes = [a.shape for a in small_w]
    pack = lambda arrs: _flat_rows(arrs, LANES, 8)
    sd, sm, sv = _adamw(pack(small_w), pack(small_g), pack(small_m), pack(small_v), "adam_small")
    sd, sm, sv = _unflatten(sd, shapes), _unflatten(sm, shapes), _unflatten(sv, shapes)
    (d_g_mix, d_b_f, d_g_conv_out, d_g_attn_out, d_g_ffn, d_g_final, d_w_conv, d_w_ffn) = sd
    (nm_g_mix, nm_b_f, nm_g_conv_out, nm_g_attn_out, nm_g_ffn, nm_g_final, nm_w_conv, nm_w_ffn) = sm
    (nv_g_mix, nv_b_f, nv_g_conv_out, nv_g_attn_out, nv_g_ffn, nv_g_final, nv_w_conv, nv_w_ffn) = sv

    grads = (g_g_mix, g_w_in[None], g_b_f, g_w_conv[None], g_g_conv_out, g_g_attn_out, g_w_o[None], g_g_ffn,
             g_w_up[None], g_w_ffn[None], g_w_down[None], g_g_final)
    deltas = (d_g_mix, u_w_in[0], d_b_f, d_w_conv, d_g_conv_out, d_g_attn_out, u_w_o[0], d_g_ffn, u_w_up[0],
              d_w_ffn, u_w_down[0], d_g_final)
    new_m = (nm_g_mix, u_w_in[1], nm_b_f, nm_w_conv, nm_g_conv_out, nm_g_attn_out, u_w_o[1], nm_g_ffn, u_w_up[1],
             nm_w_ffn, u_w_down[1], nm_g_final)
    new_v = (nv_g_mix, u_w_in[2], nv_b_f, nv_w_conv, nv_g_conv_out, nv_g_attn_out, u_w_o[2], nv_g_ffn, u_w_up[2],
             nv_w_ffn, u_w_down[2], nv_g_final)
    return (loss, grad_x[None], *grads, *deltas, *new_m, *new_v)
```

```python
import jax
import jax.numpy as jnp
from jax import lax
from jax.experimental import pallas as pl
from jax.experimental.pallas import tpu as pltpu

F32, BF16 = jnp.float32, jnp.bfloat16
MESH = pl.DeviceIdType.MESH

D_MODEL = 1024
CONV_CH = 512
ATTN_W = 512
N_HEADS = 8
HEAD_DIM = 64
N_PAIRS = N_HEADS // 2
D_FF = 2816
IN_COLS = 3 * CONV_CH + 3 * ATTN_W + N_HEADS
EPS = 1e-6
Q_SCALE = 0.125
EXP_ZERO = 88.0
N_CHIPS = 4
LANES = 128
HALO = 8
HALO_BF16 = 2 * HALO

ADAM_LR, ADAM_B1, ADAM_B2, ADAM_EPS, ADAM_WD, ADAM_STEP = 0.001, 0.9, 0.999, 1e-08, 0.01, 10

TM_ROWS = 512
TM_MM = 1024
TK_TN = 1024
TQ = 512
TM_FFN = 256
TN_FFN = 1408
VMEM_LIMIT = 52 * 2**20


def _cp(sem, vmem=VMEM_LIMIT):
    return pltpu.CompilerParams(dimension_semantics=sem, vmem_limit_bytes=vmem)


def _bf(a):
    return a if a.dtype == BF16 else a.astype(BF16)


def _mm(mode, a_list, b_list, out_dtype, tm, tn, name, add=None, b_chips=False, scatter=(), norm_g=None):
    n_p = len(a_list)
    a0 = a_list[0]
    m_dim = a0[0].shape[1] if isinstance(a0, tuple) else a0.shape[0]
    b0 = b_list[0]
    if b_chips:
        n_dim = b0.shape[0] * b0.shape[2]
        assert tn == b0.shape[2] and mode == "nn"
    else:
        b0 = b0[0][b0[1]] if isinstance(b0, tuple) else b0
        n_dim = b0.shape[1 if mode == "nn" else 0]
    tm, tn = min(tm, m_dim), min(tn, n_dim)
    assert m_dim % tm == 0 and n_dim % tn == 0
    dims = (((1,), (0,)), ((), ())) if mode == "nn" else (((1,), (1,)), ((), ()))
    in_specs, args = [], []
    for a in a_list:
        if isinstance(a, tuple):
            arr, lead, col, width = a
            in_specs.append(pl.BlockSpec((None, tm, width), lambda m, n, lead=lead, col=col: (lead, m, col)))
        else:
            arr = a
            in_specs.append(pl.BlockSpec((tm, a.shape[1]), lambda m, n: (m, 0)))
        args.append(arr)
    for b in b_list:
        if b_chips:
            arr = b
            in_specs.append(pl.BlockSpec((None, b.shape[1], tn), lambda m, n: (n, 0, 0)))
        elif isinstance(b, tuple):
            arr, lead = b
            if mode == "nn":
                in_specs.append(pl.BlockSpec((None, arr.shape[1], tn), lambda m, n, lead=lead: (lead, 0, n)))
            else:
                in_specs.append(pl.BlockSpec((None, tn, arr.shape[2]), lambda m, n, lead=lead: (lead, n, 0)))
        elif mode == "nn":
            arr = b
            in_specs.append(pl.BlockSpec((b.shape[0], tn), lambda m, n: (0, n)))
        else:
            arr = b
            in_specs.append(pl.BlockSpec((tn, b.shape[1]), lambda m, n: (n, 0)))
        args.append(arr)
    if add is not None:
        in_specs.append(pl.BlockSpec((tm, tn), lambda m, n: (m, n)))
        args.append(add)
    if norm_g is not None:
        assert tn == n_dim and not scatter
        in_specs.append(pl.BlockSpec((1, tn), lambda m, n: (0, 0)))
        args.append(norm_g)

    n_in = len(args)
    n_sc = len(scatter)
    grid = (m_dim // tm, n_dim // tn)

    def body(*refs):
        o_ref = refs[n_in + n_sc]
        if n_sc:
            finish = _hosted_scatter(refs[n_in:n_in + n_sc], refs[n_in + n_sc + 1:n_in + 2 * n_sc + 1],
                                     refs[n_in + 2 * n_sc + 1:], pl.program_id(0) * grid[1] + pl.program_id(1),
                                     grid[0] * grid[1])
        acc = None
        for i in range(n_p):
            d = lax.dot_general(_bf(refs[i][...]), _bf(refs[n_p + i][...]), dims,
                                preferred_element_type=F32)
            acc = d if acc is None else acc + d
        if add is not None:
            acc = refs[2 * n_p][...] + acc
        o_ref[...] = acc.astype(out_dtype)
        if norm_g is not None:
            refs[n_in + 1][...] = (acc * _rstd(acc) * refs[n_in - 1][...]).astype(BF16)
        if n_sc:
            finish()

    main_spec = pl.BlockSpec((tm, tn), lambda m, n: (m, n))
    main_shape = jax.ShapeDtypeStruct((m_dim, n_dim), out_dtype)
    if norm_g is not None:
        return pl.pallas_call(body, name=name, grid=grid, in_specs=in_specs, out_specs=[main_spec, main_spec],
                              out_shape=[main_shape, jax.ShapeDtypeStruct((m_dim, n_dim), BF16)],
                              compiler_params=_cp(("parallel", "parallel")))(*args)
    if not n_sc:
        return pl.pallas_call(body, name=name, grid=grid, in_specs=in_specs, out_specs=main_spec,
                              out_shape=main_shape, compiler_params=_cp(("parallel", "parallel")))(*args)
    got_shapes, sems = _scatter_operands(scatter)
    outs = pl.pallas_call(
        body, name=name, grid=grid, in_specs=in_specs + _hbm_specs(n_sc), out_specs=[main_spec] + _hbm_specs(n_sc),
        out_shape=[main_shape] + got_shapes, scratch_shapes=sems,
        compiler_params=_cp(("arbitrary", "arbitrary")))(*args, *scatter)
    return outs[0], outs[1:]


def _mm_tn(a, b, tm, tn, name, out_chips=False):
    k_dim, m_dim = a.shape
    n_dim = b.shape[-1] * (b.shape[0] if b.ndim == 3 else 1)
    tm, tn, tk = min(tm, m_dim), min(tn, b.shape[-1]), min(TK_TN, k_dim)
    assert m_dim % tm == 0 and b.shape[-1] % tn == 0 and k_dim % tk == 0
    per = b.shape[-1] // tn
    if b.ndim == 3:
        b_spec = pl.BlockSpec((None, tk, tn), lambda m, n, k: (n // per, k, n % per))
    else:
        b_spec = pl.BlockSpec((tk, tn), lambda m, n, k: (k, n))

    def body(a_ref, b_ref, o_ref):
        @pl.when(pl.program_id(2) == 0)
        def _():
            o_ref[...] = jnp.zeros_like(o_ref)
        o_ref[...] += lax.dot_general(_bf(a_ref[...]), _bf(b_ref[...]), (((0,), (0,)), ((), ())),
                                      preferred_element_type=F32)

    return pl.pallas_call(
        body, name=name, grid=(m_dim // tm, n_dim // tn, k_dim // tk),
        in_specs=[pl.BlockSpec((tk, tm), lambda m, n, k: (k, m)), b_spec],
        out_specs=(pl.BlockSpec((None, tm, tn), lambda m, n, k: (n, m, 0)) if out_chips
                   else pl.BlockSpec((tm, tn), lambda m, n, k: (m, n))),
        out_shape=jax.ShapeDtypeStruct((n_dim // tn, m_dim, tn) if out_chips else (m_dim, n_dim), F32),
        compiler_params=_cp(("parallel", "parallel", "arbitrary")))(a, b)


def _rstd(x):
    return lax.rsqrt(jnp.mean(x * x, axis=-1, keepdims=True) + EPS)


def _rms_bwd(x, dh, g, dres, name, with_bf16, scatter=()):
    s, d = x.shape
    tm = min(TM_ROWS, s)
    n_sc = len(scatter)
    n_out = 3 if with_bf16 else 2
    got_shapes, sems = _scatter_operands(scatter) if n_sc else ([], [])

    def body(x_ref, dh_ref, g_ref, dres_ref, *rest):
        dx_ref, gg_ref = rest[n_sc], rest[n_sc + n_out - 1]
        i = pl.program_id(0)
        if n_sc:
            finish = _hosted_scatter(rest[:n_sc], rest[n_sc + n_out:2 * n_sc + n_out], rest[2 * n_sc + n_out:], i,
                                     s // tm)

        @pl.when(i == 0)
        def _():
            gg_ref[...] = jnp.zeros_like(gg_ref)

        xv = x_ref[...]
        xn = xv * _rstd(xv)
        dhv = dh_ref[...]
        gg_ref[...] += jnp.sum(dhv * xn, axis=0, keepdims=True)
        t = dhv * g_ref[...]
        dx = dres_ref[...] + _rstd(xv) * (t - xn * jnp.mean(t * xn, axis=-1, keepdims=True))
        dx_ref[...] = dx
        if with_bf16:
            rest[n_sc + 1][...] = dx.astype(BF16)
        if n_sc:
            finish()

    row = pl.BlockSpec((tm, d), lambda i: (i, 0))
    vec = pl.BlockSpec((1, d), lambda i: (0, 0))
    out_specs = [row] + ([row] if with_bf16 else []) + [vec] + _hbm_specs(n_sc)
    out_shape = ([jax.ShapeDtypeStruct((s, d), F32)] + ([jax.ShapeDtypeStruct((s, d), BF16)] if with_bf16 else [])
                 + [jax.ShapeDtypeStruct((1, d), F32)] + got_shapes)
    outs = pl.pallas_call(
        body, name=name, grid=(s // tm,), in_specs=[row, row, vec, row] + _hbm_specs(n_sc), out_specs=out_specs,
        out_shape=out_shape, scratch_shapes=sems, compiler_params=_cp(("arbitrary",)))(x, dh, g, dres, *scatter)
    return tuple(outs[:n_out]) + ((outs[n_out:],) if n_sc else ())


def _down_proj_loss(act, w_down, x2, target, g):
    s, d = x2.shape
    tm = min(TM_ROWS, s)
    k_dim = act.shape[1]

    def body(a_ref, w_ref, x_ref, t_ref, g_ref, dx_ref, dxb_ref, loss_ref, gg_ref):
        @pl.when(pl.program_id(0) == 0)
        def _():
            gg_ref[...] = jnp.zeros_like(gg_ref)
            loss_ref[...] = jnp.zeros_like(loss_ref)

        xv = x_ref[...] + lax.dot_general(a_ref[...], w_ref[...], (((1,), (0,)), ((), ())),
                                          preferred_element_type=F32)
        r = _rstd(xv)
        xn = xv * r
        gv = g_ref[...]
        err = xn * gv - t_ref[...]
        loss_ref[...] += 0.5 * jnp.sum(jnp.mean(err * err, axis=-1, keepdims=True), axis=0, keepdims=True)
        dy = err * (1.0 / d)
        gg_ref[...] += jnp.sum(dy * xn, axis=0, keepdims=True)
        t = dy * gv
        dx = r * (t - xn * jnp.mean(t * xn, axis=-1, keepdims=True))
        dx_ref[...] = dx
        dxb_ref[...] = dx.astype(BF16)

    row = pl.BlockSpec((tm, d), lambda i: (i, 0))
    vec = pl.BlockSpec((1, d), lambda i: (0, 0))
    return pl.pallas_call(
        body, name="down_proj_loss", grid=(s // tm,),
        in_specs=[pl.BlockSpec((tm, k_dim), lambda i: (i, 0)), pl.BlockSpec((k_dim, d), lambda i: (0, 0)), row, row,
                  vec],
        out_specs=[row, row, pl.BlockSpec((1, LANES), lambda i: (0, 0)), vec],
        out_shape=[jax.ShapeDtypeStruct((s, d), F32), jax.ShapeDtypeStruct((s, d), BF16),
                   jax.ShapeDtypeStruct((1, LANES), F32), jax.ShapeDtypeStruct((1, d), F32)],
        compiler_params=_cp(("arbitrary",)))(act, w_down, x2, target, g)


def _prev_halo_spec(tm, width, col):
    return pl.BlockSpec((HALO, width), lambda i, *_: (jnp.maximum(i * (tm // HALO) - 1, 0), col))


def _next_halo_spec(tm, width, col, s):
    return pl.BlockSpec((HALO, width), lambda i, *_: (jnp.minimum((i + 1) * (tm // HALO), s // HALO - 1), col))


def _shift_down(x, k):
    return pltpu.roll(x, k, 0)


def _shift_up(x, k):
    return pltpu.roll(x, x.shape[0] - k, 0)


def _conv_taps(x_ext, w):
    return w[0:1, :] * _shift_down(x_ext, 2) + w[1:2, :] * _shift_down(x_ext, 1) + w[2:3, :] * x_ext


def _conv_taps_t(d_ext, w):
    return w[2:3, :] * d_ext + w[1:2, :] * _shift_up(d_ext, 1) + w[0:1, :] * _shift_up(d_ext, 2)


def _mixer_fwd(z_a, o_attn, w_conv, g_conv_out, g_attn_out):
    s = z_a.shape[0]
    c = CONV_CH
    tm = min(TM_ROWS, s)

    def body(gb_ref, gc_ref, xc_ref, gcp_ref, xcp_ref, o_ref, w_ref, gco_ref, gao_ref, mix_ref):
        i = pl.program_id(0)
        cx = gc_ref[...] * xc_ref[...]
        cx_prev = jnp.where(i > 0, gcp_ref[...] * xcp_ref[...], 0.0)
        conv = _conv_taps(jnp.concatenate([cx_prev, cx], axis=0), w_ref[...])[HALO:]
        y = gb_ref[...] * conv
        mix_ref[:, 0:c] = (y * _rstd(y) * gco_ref[...]).astype(BF16)
        o = o_ref[...]
        mix_ref[:, c:2 * c] = (o * _rstd(o) * gao_ref[...]).astype(BF16)

    col = lambda j: pl.BlockSpec((tm, c), lambda i: (i, j))
    vec = pl.BlockSpec((1, c), lambda i: (0, 0))
    return pl.pallas_call(
        body, name="mixer_fwd", grid=(s // tm,),
        in_specs=[col(0), col(1), col(2), _prev_halo_spec(tm, c, 1), _prev_halo_spec(tm, c, 2), col(0),
                  pl.BlockSpec((3, c), lambda i: (0, 0)), vec, vec],
        out_specs=pl.BlockSpec((tm, 2 * c), lambda i: (i, 0)),
        out_shape=jax.ShapeDtypeStruct((s, 2 * c), BF16),
        compiler_params=_cp(("parallel",)))(z_a, z_a, z_a, z_a, z_a, o_attn, w_conv, g_conv_out, g_attn_out)


def _mixer_bwd(z_a, o_attn, dmix, w_conv, g_conv_out, g_attn_out, scatter):
    s = z_a.shape[0]
    c = CONV_CH
    tm = min(TM_ROWS, s)
    n_blk = s // tm
    n_sc = len(scatter)
    got_shapes, sems = _scatter_operands(scatter)

    def body(gb_ref, gc_ref, xc_ref, gcp_ref, xcp_ref, gbn_ref, gcn_ref, xcn_ref, o_ref, dnc_ref, dncn_ref, dna_ref,
             w_ref, gco_ref, gao_ref, *rest):
        dz_ref, dox_ref, gw_ref, ggco_ref, ggao_ref = rest[n_sc:n_sc + 5]
        i = pl.program_id(0)
        finish = _hosted_scatter(rest[:n_sc], rest[n_sc + 5:2 * n_sc + 5], rest[2 * n_sc + 5:], i, n_blk)

        @pl.when(i == 0)
        def _():
            gw_ref[...] = jnp.zeros_like(gw_ref)
            ggco_ref[...] = jnp.zeros_like(ggco_ref)
            ggao_ref[...] = jnp.zeros_like(ggao_ref)

        w = w_ref[...]
        zeros = jnp.zeros((HALO, c), F32)
        gb_e = jnp.concatenate([zeros, gb_ref[...], gbn_ref[...]], axis=0)
        cx_prev = jnp.where(i > 0, gcp_ref[...] * xcp_ref[...], 0.0)
        gc_e = jnp.concatenate([zeros, gc_ref[...], gcn_ref[...]], axis=0)
        xc_e = jnp.concatenate([zeros, xc_ref[...], xcn_ref[...]], axis=0)
        cx_e = jnp.concatenate([cx_prev, gc_ref[...] * xc_ref[...], gcn_ref[...] * xcn_ref[...]], axis=0)
        dn_next = jnp.where(i < n_blk - 1, dncn_ref[...], 0.0)
        dn_e = jnp.concatenate([zeros, dnc_ref[...], dn_next], axis=0)

        cx_1, cx_2 = _shift_down(cx_e, 1), _shift_down(cx_e, 2)
        conv_e = w[0:1, :] * cx_2 + w[1:2, :] * cx_1 + w[2:3, :] * cx_e
        y_e = gb_e * conv_e
        r_e = _rstd(y_e)
        yn_e = y_e * r_e
        t_e = dn_e * gco_ref[...]
        dy_e = r_e * (t_e - yn_e * jnp.mean(t_e * yn_e, axis=-1, keepdims=True))
        dconv_e = dy_e * gb_e
        dcx_e = _conv_taps_t(dconv_e, w)
        blk = slice(HALO, HALO + tm)
        dz_ref[:, 0:c] = (dy_e * conv_e)[blk].astype(BF16)
        dz_ref[:, c:2 * c] = (dcx_e * xc_e)[blk].astype(BF16)
        dz_ref[:, 2 * c:3 * c] = (dcx_e * gc_e)[blk].astype(BF16)
        ggco_ref[...] += jnp.sum((dn_e * yn_e)[blk], axis=0, keepdims=True)
        dconv = dconv_e[blk]
        gw_ref[0:1, :] += jnp.sum(dconv * cx_2[blk], axis=0, keepdims=True)
        gw_ref[1:2, :] += jnp.sum(dconv * cx_1[blk], axis=0, keepdims=True)
        gw_ref[2:3, :] += jnp.sum(dconv * cx_e[blk], axis=0, keepdims=True)

        o = o_ref[...]
        ra = _rstd(o)
        on = o * ra
        dna = dna_ref[...]
        ggao_ref[...] += jnp.sum(dna * on, axis=0, keepdims=True)
        ta = dna * gao_ref[...]
        do = ra * (ta - on * jnp.mean(ta * on, axis=-1, keepdims=True))
        prod = do * o
        lane = lax.broadcasted_iota(jnp.int32, (tm, LANES), 1)
        head_a = lane < HEAD_DIM
        for p in range(N_PAIRS):
            cols = slice(p * LANES, (p + 1) * LANES)
            pb, dob = prod[:, cols], do[:, cols]
            for hh in range(2):
                sel = head_a if hh == 0 else jnp.logical_not(head_a)
                delta = jnp.sum(jnp.where(sel, pb, 0.0), axis=-1, keepdims=True)
                neg3 = _split3(-delta)
                do_h = pltpu.roll(dob, HEAD_DIM, 1) if hh else dob
                dox_ref[2 * p + hh] = _aug(do_h, lane, neg3).astype(BF16)
        finish()

    col = lambda j: pl.BlockSpec((tm, c), lambda i: (i, j))
    vec = pl.BlockSpec((1, c), lambda i: (0, 0))
    w3 = pl.BlockSpec((3, c), lambda i: (0, 0))
    outs = pl.pallas_call(
        body, name="mixer_bwd", grid=(n_blk,),
        in_specs=[col(0), col(1), col(2), _prev_halo_spec(tm, c, 1), _prev_halo_spec(tm, c, 2),
                  _next_halo_spec(tm, c, 0, s), _next_halo_spec(tm, c, 1, s), _next_halo_spec(tm, c, 2, s),
                  col(0), col(0), _next_halo_spec(tm, c, 0, s), col(1), w3, vec, vec] + _hbm_specs(n_sc),
        out_specs=[pl.BlockSpec((tm, 3 * c), lambda i: (i, 0)),
                   pl.BlockSpec((N_HEADS, tm, LANES), lambda i: (0, i, 0)), w3, vec, vec] + _hbm_specs(n_sc),
        out_shape=[jax.ShapeDtypeStruct((s, 3 * c), BF16), jax.ShapeDtypeStruct((N_HEADS, s, LANES), BF16),
                   jax.ShapeDtypeStruct((3, c), F32), jax.ShapeDtypeStruct((1, c), F32),
                   jax.ShapeDtypeStruct((1, c), F32)] + got_shapes,
        scratch_shapes=sems, compiler_params=_cp(("arbitrary",)))(
            z_a, z_a, z_a, z_a, z_a, z_a, z_a, z_a, o_attn, dmix, dmix, dmix, w_conv, g_conv_out, g_attn_out,
            *scatter)
    return tuple(outs[:5]) + (outs[5:],)


def _gate_fwd(f, b_pad):
    s = f.shape[0]
    tm = min(TQ, s)

    def body(f_ref, b_ref, fb_ref, carry):
        @pl.when(pl.program_id(0) == 0)
        def _():
            carry[...] = jnp.zeros_like(carry)

        z = f_ref[...] + b_ref[...]
        x = jnp.minimum(z, 0.0) - jnp.log1p(jnp.exp(-jnp.abs(z)))
        row = lax.broadcasted_iota(jnp.int32, (tm, LANES), 0)
        sh = 1
        while sh < tm:
            x = x + jnp.where(row >= sh, _shift_down(x, sh), 0.0)
            sh *= 2
        x = x + carry[0:1, :]
        carry[...] = jnp.broadcast_to(x[tm - 1:tm, :], carry.shape)
        head_a = lax.broadcasted_iota(jnp.int32, (tm, LANES), 1) < HEAD_DIM
        for p in range(N_PAIRS):
            fa = jnp.broadcast_to(x[:, 2 * p:2 * p + 1], (tm, LANES))
            fbv = jnp.broadcast_to(x[:, 2 * p + 1:2 * p + 2], (tm, LANES))
            fb_ref[:, p * LANES:(p + 1) * LANES] = jnp.where(head_a, fa, fbv)

    return pl.pallas_call(
        body, name="gate_fwd", grid=(s // tm,),
        in_specs=[pl.BlockSpec((tm, LANES), lambda i: (i, 0)), pl.BlockSpec((1, LANES), lambda i: (0, 0))],
        out_specs=pl.BlockSpec((tm, N_PAIRS * LANES), lambda i: (i, 0)),
        out_shape=jax.ShapeDtypeStruct((s, N_PAIRS * LANES), F32),
        scratch_shapes=[pltpu.VMEM((HALO, LANES), F32)],
        compiler_params=_cp(("arbitrary",)))(f, b_pad)


def _gate_bwd(f, b_pad, d_f):
    s = f.shape[0]
    tm = min(TQ, s)
    n_blk = s // tm

    def body(f_ref, b_ref, d_ref, df_ref, gb_ref, carry):
        @pl.when(pl.program_id(0) == 0)
        def _():
            carry[...] = jnp.zeros_like(carry)
            gb_ref[...] = jnp.zeros_like(gb_ref)

        x = d_ref[...]
        row = lax.broadcasted_iota(jnp.int32, (tm, LANES), 0)
        sh = 1
        while sh < tm:
            x = x + jnp.where(row < tm - sh, _shift_up(x, sh), 0.0)
            sh *= 2
        x = x + carry[0:1, :]
        carry[...] = jnp.broadcast_to(x[0:1, :], carry.shape)
        z = f_ref[...] + b_ref[...]
        d = x * (1.0 / (1.0 + jnp.exp(z)))
        df_ref[...] = d.astype(BF16)
        gb_ref[...] += jnp.sum(d, axis=0, keepdims=True)

    rev = pl.BlockSpec((tm, LANES), lambda i: (n_blk - 1 - i, 0))
    vec = pl.BlockSpec((1, LANES), lambda i: (0, 0))
    return pl.pallas_call(
        body, name="gate_bwd", grid=(n_blk,), in_specs=[rev, vec, rev], out_specs=[rev, vec],
        out_shape=[jax.ShapeDtypeStruct((s, LANES), BF16), jax.ShapeDtypeStruct((1, LANES), F32)],
        scratch_shapes=[pltpu.VMEM((HALO, LANES), F32)],
        compiler_params=_cp(("arbitrary",)))(f, b_pad, d_f)


_NT = (((1,), (1,)), ((), ()))
_NN = (((1,), (0,)), ((), ()))


AUG = HEAD_DIM
NORM_MARGIN = 1.01


def _split3(x):
    hi = x.astype(BF16).astype(F32)
    r = x - hi
    mid = r.astype(BF16).astype(F32)
    lo = (r - mid).astype(BF16).astype(F32)
    return hi, mid, lo


def _aug(base, lane, vals):
    out = jnp.where(lane < AUG, base, 0.0)
    for k, v in enumerate(vals):
        out = jnp.where(lane == AUG + k, v, out)
    return out


def _attn_prep(qkv, fb, bigs):
    s = qkv.shape[0]
    tq = min(TQ, s)
    n_q = s // tq

    n = len(bigs)
    arrays, landing, sems = _gather_operands(bigs, [])

    def body(q_ref, k_ref, v_ref, fb_ref, *rest):
        qx_ref, kx_ref, kxt_ref, vx_ref, vt_ref, b_ref = rest[2 * n:2 * n + 6]
        finish = _hosted_gather((rest[:n], rest[2 * n + 6:3 * n + 6]) + tuple(rest[3 * n + 6:]), n, n,
                                pl.program_id(0), n_q)
        lane = lax.broadcasted_iota(jnp.int32, (tq, LANES), 1)
        lane8 = lax.broadcasted_iota(jnp.int32, (HALO, LANES), 1)
        head_lanes = lane < AUG
        is_lane = [lane == AUG + k for k in range(6)]
        first3 = (lane >= AUG) & (lane < AUG + 3)
        next3 = (lane >= AUG + 3) & (lane < AUG + 6)
        q_const = jnp.where(first3, -1.0, 0.0)
        k_const = jnp.where(next3, 1.0, 0.0)
        v_const = jnp.where(first3, 1.0, 0.0)
        ones_head = (lax.broadcasted_iota(jnp.int32, (LANES, LANES), 0) < HEAD_DIM).astype(BF16)
        acc = jnp.zeros((HALO, LANES), F32)
        for p in range(N_PAIRS):
            cols = slice(p * LANES, (p + 1) * LANES)
            q2, k2, v2 = (ref[:, cols].astype(F32) for ref in (q_ref, k_ref, v_ref))
            q2 = q2 * Q_SCALE
            f2 = fb_ref[:, cols]
            for hh in range(2):
                h = 2 * p + hh
                q, k, v = ((pltpu.roll(x, HEAD_DIM, 1) if hh else x) for x in (q2, k2, v2))
                f = f2 if hh else pltpu.roll(f2, HEAD_DIM, 1)
                hi, mid, lo = _split3(f)
                q_aug = jnp.where(is_lane[3], hi, jnp.where(is_lane[4], mid, jnp.where(is_lane[5], lo, q_const)))
                k_aug = jnp.where(is_lane[0], hi, jnp.where(is_lane[1], mid, jnp.where(is_lane[2], lo, k_const)))
                kx = jnp.where(head_lanes, k, k_aug)
                vx = jnp.where(head_lanes, v, v_const)
                qx_ref[h] = jnp.where(head_lanes, q, q_aug).astype(BF16)
                kx_ref[h] = kx.astype(BF16)
                vx_ref[h] = vx.astype(BF16)
                kxt_ref[h, 0] = kx.T.astype(BF16)
                vt_ref[h, 0] = vx.T.astype(BF16)
                q_sq = lax.dot_general((q * q).astype(BF16), ones_head, _NN, preferred_element_type=F32)
                k_sq = lax.dot_general((k * k).astype(BF16), ones_head, _NN, preferred_element_type=F32)
                diag = lax.dot_general((q * k).astype(BF16), ones_head, _NN, preferred_element_type=F32)
                diag = diag - jnp.sqrt(q_sq * k_sq) * (NORM_MARGIN - 1.0)
                vals = (jnp.sqrt(jnp.max(q_sq, axis=0, keepdims=True)), jnp.sqrt(jnp.max(k_sq, axis=0, keepdims=True)),
                        jnp.max(f - diag, axis=0, keepdims=True), f[tq - 1:tq, :])
                for slot, val in enumerate(vals):
                    acc = jnp.where(lane8 == slot * N_HEADS + h, val[:, AUG:AUG + 1], acc)
        b_ref[0] = acc
        finish()

    blk = lambda j: pl.BlockSpec((tq, ATTN_W), lambda i: (i, j))
    rows = pl.BlockSpec((N_HEADS, tq, LANES), lambda i: (0, i, 0))
    cols_t = pl.BlockSpec((N_HEADS, 1, LANES, tq), lambda i: (0, i, 0, 0))
    shp = jax.ShapeDtypeStruct((N_HEADS, s, LANES), BF16)
    shp_t = jax.ShapeDtypeStruct((N_HEADS, n_q, LANES, tq), BF16)
    outs = pl.pallas_call(
        body, name="attn_prep", grid=(n_q,), in_specs=[blk(0), blk(1), blk(2), blk(0)] + _hbm_specs(2 * n),
        out_specs=[rows, rows, cols_t, rows, cols_t,
                   pl.BlockSpec((1, HALO, LANES), lambda i: (i, 0, 0))] + _hbm_specs(n),
        out_shape=[shp, shp, shp_t, shp, shp_t, jax.ShapeDtypeStruct((n_q, HALO, LANES), F32)]
        + [jax.ShapeDtypeStruct(b.shape, b.dtype) for b in landing],
        input_output_aliases={4 + n + k: 6 + k for k in range(n)}, scratch_shapes=sems,
        compiler_params=_cp(("arbitrary",)))(qkv, qkv, qkv, fb, *arrays, *landing)
    return tuple(outs[:6]) + (outs[6:],)


def _key_block_ranges(bounds):
    t = bounds[:, 0, :]
    nh = N_HEADS
    a, b, c, e = t[:, 0:nh], t[:, nh:2 * nh], t[:, 2 * nh:3 * nh], t[:, 3 * nh:4 * nh]
    bound = a[:, None, :] * b[None, :, :] * NORM_MARGIN + c[:, None, :] - e[None, :, :]
    n_q = t.shape[0]
    idx = jnp.arange(n_q)
    need = jnp.logical_not(bound < -(EXP_ZERO + 2.0)) | (idx[None, :, None] >= idx[:, None, None])
    first = jnp.argmax(need, axis=1).astype(jnp.int32)
    first = jnp.min(first.reshape(n_q, N_PAIRS, 2), axis=-1)
    visits = (first[:, None, :] <= idx[None, :, None]) & (idx[None, :, None] <= idx[:, None, None])
    last = jnp.max(jnp.where(visits, idx[:, None, None], 0), axis=0).astype(jnp.int32)
    return first.T.reshape(-1), last.T.reshape(-1)


def _attn_fwd_t(qx, kx, vt, first_blk):
    _, s, _ = qx.shape
    tq = min(TQ, s)
    n_q = s // tq
    neg = -1e30

    def body(first_ref, qx_ref, kx_ref, vt_ref, o_ref, lse_ref, acc_ref, m_ref, s_even, s_odd):
        p = pl.program_id(0)
        i = pl.program_id(1)
        acc_ref[...] = jnp.zeros(acc_ref.shape, F32)
        m_ref[...] = jnp.full(m_ref.shape, neg, F32)
        key_le_query = (lax.broadcasted_iota(jnp.int32, (tq, tq), 0) <= lax.broadcasted_iota(jnp.int32, (tq, tq), 1))
        first = first_ref[p * n_q + i]

        def scores(kb, hh, dst):
            rows_k = pl.ds(pl.multiple_of(kb * tq, tq), tq)
            dst[hh] = lax.dot_general(kx_ref[hh, rows_k, :], qx_ref[hh], _NT, preferred_element_type=F32)

        def step(kb, src, nxt):
            for hh in range(2):
                st = src[hh]
                if nxt is None:
                    st = jnp.where(key_le_query, st, -jnp.inf)
                else:
                    scores(kb + 1, hh, nxt)
                m_old = m_ref[hh]
                m_new = jnp.maximum(m_old, jnp.max(st, axis=0, keepdims=True))
                m_ref[hh] = m_new
                pt = jnp.exp(st - m_new).astype(BF16)
                acc_ref[hh] = acc_ref[hh] * jnp.exp(m_old - m_new) + lax.dot_general(
                    vt_ref[hh, kb], pt, _NN, preferred_element_type=F32)

        def by_parity(kb, fn):
            @pl.when(kb % 2 == 0)
            def _():
                fn(s_even, s_odd)

            @pl.when(kb % 2 == 1)
            def _():
                fn(s_odd, s_even)

        def first_scores(src, nxt):
            scores(first, 0, src)
            scores(first, 1, src)

        def unmasked(kb, carry):
            by_parity(kb, lambda src, nxt: step(kb, src, nxt))
            return carry

        by_parity(first, first_scores)
        lax.fori_loop(first, i, unmasked, 0)
        by_parity(i, lambda src, nxt: step(i, src, None))
        outs, lses = [], []
        for hh in range(2):
            acc = acc_ref[hh]
            l = acc[AUG:AUG + 1, :]
            outs.append(acc[0:HEAD_DIM, :] / l)
            lses.append(m_ref[hh] + jnp.log(l))
        o_ref[...] = jnp.concatenate(outs, axis=0).T
        rows8 = lax.broadcasted_iota(jnp.int32, (N_HEADS, tq), 0)
        lse_ref[0, 0] = jnp.where(rows8 == 0, lses[0], jnp.where(rows8 == 1, lses[1], 0.0))

    grid_spec = pltpu.PrefetchScalarGridSpec(
        num_scalar_prefetch=1, grid=(N_PAIRS, n_q),
        in_specs=[pl.BlockSpec((2, tq, LANES), lambda p, i, first: (p, i, 0)),
                  pl.BlockSpec((2, s, LANES), lambda p, i, first: (p, 0, 0)),
                  pl.BlockSpec((2, n_q, LANES, tq), lambda p, i, first: (p, 0, 0, 0))],
        out_specs=[pl.BlockSpec((tq, LANES), lambda p, i, first: (i, p)),
                   pl.BlockSpec((1, 1, N_HEADS, tq), lambda p, i, first: (p, i, 0, 0))],
        scratch_shapes=[pltpu.VMEM((2, LANES, tq), F32), pltpu.VMEM((2, 1, tq), F32),
                        pltpu.VMEM((2, tq, tq), F32), pltpu.VMEM((2, tq, tq), F32)])
    return pl.pallas_call(
        body, name="attn_fwd", grid_spec=grid_spec,
        out_shape=[jax.ShapeDtypeStruct((s, ATTN_W), F32), jax.ShapeDtypeStruct((N_PAIRS, n_q, N_HEADS, tq), F32)],
        compiler_params=_cp(("parallel", "arbitrary")))(first_blk, qx, kx, vt)


def _attn_bwd_t(qx, dox, kx, kxt, vx, lse, last_blk):
    _, s, _ = qx.shape
    tq = min(TQ, s)
    n_q = s // tq

    def body(last_ref, qx_ref, dox_ref, lse_ref, kx_ref, kxt_ref, vx_ref, dk_ref, dv_ref, dfk_ref, dqt_ref, dfq_ref):
        p = pl.program_id(0)
        j = pl.program_id(1)

        @pl.when(j == 0)
        def _():
            dqt_ref[...] = jnp.zeros(dqt_ref.shape, F32)
            dfq_ref[...] = jnp.zeros(dfq_ref.shape, F32)

        key_le_query = (lax.broadcasted_iota(jnp.int32, (tq, tq), 0) <= lax.broadcasted_iota(jnp.int32, (tq, tq), 1))

        def step(i, carry, masked):
            rows_q = pl.ds(pl.multiple_of(i * tq, tq), tq)
            out = []
            for hh in range(2):
                dk, dv, col = carry[3 * hh:3 * hh + 3]
                q, do = qx_ref[hh, rows_q, :], dox_ref[hh, rows_q, :]
                st = lax.dot_general(kx_ref[hh], q, _NT, preferred_element_type=F32)
                pt = jnp.exp(st - lse_ref[0, i, hh:hh + 1, :])
                if masked:
                    pt = jnp.where(key_le_query, pt, 0.0)
                dst = pt * lax.dot_general(vx_ref[hh], do, _NT, preferred_element_type=F32)
                pb, dsb = pt.astype(BF16), dst.astype(BF16)
                dv = dv + lax.dot_general(pb, do, _NN, preferred_element_type=F32)
                dk = dk + lax.dot_general(dsb, q, _NN, preferred_element_type=F32)
                dqt_ref[hh, i] += lax.dot_general(kxt_ref[hh, 0], dsb, _NN, preferred_element_type=F32)
                for cb in range(tq // LANES):
                    col = col + dst[:, cb * LANES:(cb + 1) * LANES]
                dfq_ref[hh, i] += jnp.sum(dst.reshape(tq // HALO, HALO, tq), axis=0)
                out += [dk, dv, col]
            return tuple(out)

        zero = jnp.zeros((tq, LANES), F32)
        carry = step(j, (zero,) * 6, True)
        dk_a, dv_a, col_a, dk_b, dv_b, col_b = lax.fori_loop(j + 1, last_ref[p * n_q + j] + 1,
                                                             lambda i, cr: step(i, cr, False), carry)
        head_a = lax.broadcasted_iota(jnp.int32, (tq, LANES), 1) < HEAD_DIM
        dk_ref[...] = jnp.where(head_a, dk_a, pltpu.roll(dk_b, HEAD_DIM, 1)).astype(BF16)
        dv_ref[...] = jnp.where(head_a, dv_a, pltpu.roll(dv_b, HEAD_DIM, 1)).astype(BF16)
        rows8 = lax.broadcasted_iota(jnp.int32, (N_HEADS, tq), 0)
        dfk_a, dfk_b = (-jnp.sum(c.T, axis=0, keepdims=True) for c in (col_a, col_b))
        dfk_ref[0, 0] = jnp.where(rows8 == 0, dfk_a, jnp.where(rows8 == 1, dfk_b, 0.0))

    resident = pl.BlockSpec((2, s, LANES), lambda p, j, last: (p, 0, 0))
    key_rows = pl.BlockSpec((2, tq, LANES), lambda p, j, last: (p, j, 0))
    pair_out = pl.BlockSpec((tq, LANES), lambda p, j, last: (j, p))
    grid_spec = pltpu.PrefetchScalarGridSpec(
        num_scalar_prefetch=1, grid=(N_PAIRS, n_q),
        in_specs=[resident, resident, pl.BlockSpec((1, n_q, N_HEADS, tq), lambda p, j, last: (p, 0, 0, 0)),
                  key_rows, pl.BlockSpec((2, 1, LANES, tq), lambda p, j, last: (p, j, 0, 0)), key_rows],
        out_specs=[pair_out, pair_out, pl.BlockSpec((1, 1, N_HEADS, tq), lambda p, j, last: (p, j, 0, 0)),
                   pl.BlockSpec((2, n_q, LANES, tq), lambda p, j, last: (p, 0, 0, 0)),
                   pl.BlockSpec((2, n_q, HALO, tq), lambda p, j, last: (p, 0, 0, 0))])
    return pl.pallas_call(
        body, name="attn_bwd", grid_spec=grid_spec,
        out_shape=[jax.ShapeDtypeStruct((s, ATTN_W), BF16), jax.ShapeDtypeStruct((s, ATTN_W), BF16),
                   jax.ShapeDtypeStruct((N_PAIRS, n_q, N_HEADS, tq), F32),
                   jax.ShapeDtypeStruct((N_HEADS, n_q, LANES, tq), F32),
                   jax.ShapeDtypeStruct((N_HEADS, n_q, HALO, tq), F32)],
        compiler_params=_cp(("parallel", "arbitrary")))(last_blk, qx, dox, lse, kx, kxt, vx)


def _attn_dq_finish(dqt):
    _, n_q, _, tq = dqt.shape
    per = 4 if n_q % 4 == 0 else 1

    def body(dqt_ref, dq_ref):
        for b in range(per):
            a, bb = dqt_ref[0, b], dqt_ref[1, b]
            dq_ref[b * tq:(b + 1) * tq, :] = (
                jnp.concatenate([a[0:HEAD_DIM], bb[0:HEAD_DIM]], axis=0).T * Q_SCALE).astype(BF16)

    return pl.pallas_call(
        body, name="attn_dq_finish", grid=(N_PAIRS, n_q // per),
        in_specs=[pl.BlockSpec((2, per, LANES, tq), lambda p, i: (p, i, 0, 0))],
        out_specs=pl.BlockSpec((per * tq, LANES), lambda p, i: (i, p)),
        out_shape=jax.ShapeDtypeStruct((n_q * tq, ATTN_W), BF16),
        compiler_params=_cp(("parallel", "parallel")))(dqt)


def _ffn_act_fwd(up, w_ffn):
    s = up.shape[0]
    tm, tn = min(TM_FFN, s), TN_FFN
    nb = D_FF // tn

    def body(a_ref, g_ref, ap_ref, gp_ref, wa_ref, wg_ref, act_ref, u_ref):
        i = pl.program_id(1)

        def conv(blk_ref, prev_ref, w_ref):
            prev = jnp.where(i > 0, prev_ref[...].astype(F32)[HALO:], 0.0)
            return _conv_taps(jnp.concatenate([prev, blk_ref[...].astype(F32)], axis=0), w_ref[...])[HALO:]

        u_a, u_g = conv(a_ref, ap_ref, wa_ref), conv(g_ref, gp_ref, wg_ref)
        u_ref[0], u_ref[1] = u_a.astype(BF16), u_g.astype(BF16)
        act_ref[...] = (u_g * jax.nn.sigmoid(u_g) * u_a).astype(BF16)

    blk = lambda off: pl.BlockSpec((tm, tn), lambda n, i: (i, off + n))
    prev = lambda off: pl.BlockSpec(
        (HALO_BF16, tn), lambda n, i: (jnp.maximum(i * (tm // HALO_BF16) - 1, 0), off + n))
    wsp = lambda off: pl.BlockSpec((3, tn), lambda n, i: (0, off + n))
    return pl.pallas_call(
        body, name="ffn_act_fwd", grid=(nb, s // tm),
        in_specs=[blk(0), blk(nb), prev(0), prev(nb), wsp(0), wsp(nb)],
        out_specs=[pl.BlockSpec((tm, tn), lambda n, i: (i, n)), pl.BlockSpec((2, tm, tn), lambda n, i: (0, i, n))],
        out_shape=[jax.ShapeDtypeStruct((s, D_FF), BF16), jax.ShapeDtypeStruct((2, s, D_FF), BF16)],
        compiler_params=_cp(("parallel", "parallel")))(up, up, up, up, w_ffn, w_ffn)


def _ffn_act_bwd(up, u, dact, w_ffn):
    s = up.shape[0]
    tm, tn = min(TM_FFN, s), TN_FFN
    nb = D_FF // tn
    n_blk = s // tm

    def body(u_ref, un_ref, a_ref, g_ref, d_ref, dn_ref, wa_ref, wg_ref, dup_ref, gwa_ref, gwg_ref):
        i = pl.program_id(1)

        @pl.when(i == 0)
        def _():
            gwa_ref[...] = jnp.zeros_like(gwa_ref)
            gwg_ref[...] = jnp.zeros_like(gwg_ref)

        ext = lambda rows, next_rows: jnp.concatenate([rows, next_rows], axis=0)
        u_a, u_g = (ext(u_ref[h].astype(F32), un_ref[h].astype(F32)[:HALO]) for h in range(2))
        d_e = ext(d_ref[...], jnp.where(i < n_blk - 1, dn_ref[...], 0.0))
        sig = jax.nn.sigmoid(u_g)
        du_a = d_e * (u_g * sig)
        du_g = d_e * u_a * (sig * (1.0 + u_g * (1.0 - sig)))
        halves = ((gwa_ref, wa_ref[...], a_ref[...].astype(F32), du_a),
                  (gwg_ref, wg_ref[...], g_ref[...].astype(F32), du_g))
        for half, (gw_ref, w, upv, du) in enumerate(halves):
            du0, du1, du2 = du[:tm], _shift_up(du, 1)[:tm], _shift_up(du, 2)[:tm]
            dup_ref[half] = (w[2:3, :] * du0 + w[1:2, :] * du1 + w[0:1, :] * du2).astype(BF16)
            gw_ref[0:1, :] += jnp.sum(upv * du2, axis=0, keepdims=True)
            gw_ref[1:2, :] += jnp.sum(upv * du1, axis=0, keepdims=True)
            gw_ref[2:3, :] += jnp.sum(upv * du0, axis=0, keepdims=True)

    next_row = lambda halo: lambda i: jnp.minimum((i + 1) * (tm // halo), s // halo - 1)
    blk = lambda off: pl.BlockSpec((tm, tn), lambda n, i: (i, off + n))
    wsp = lambda off: pl.BlockSpec((3, tn), lambda n, i: (0, off + n))
    pair = lambda rows, row_of: pl.BlockSpec((2, rows, tn), lambda n, i: (0, row_of(i), n))
    return pl.pallas_call(
        body, name="ffn_act_bwd", grid=(nb, n_blk),
        in_specs=[pair(tm, lambda i: i), pair(HALO_BF16, next_row(HALO_BF16)), blk(0), blk(nb), blk(0),
                  pl.BlockSpec((HALO, tn), lambda n, i: (next_row(HALO)(i), n)), wsp(0), wsp(nb)],
        out_specs=[pair(tm, lambda i: i), wsp(0), wsp(0)],
        out_shape=[jax.ShapeDtypeStruct((2, s, D_FF), BF16),
                   jax.ShapeDtypeStruct((3, D_FF), F32), jax.ShapeDtypeStruct((3, D_FF), F32)],
        compiler_params=_cp(("parallel", "arbitrary")))(u, u, up, up, dact, dact, w_ffn, w_ffn)


def _adamw(w, g, m, v, name):
    r, c = w.shape
    tr = next((t for t in (512, 352, 256, 128, 64, 32, 16, 8) if r > t and r % t == 0), r)

    def body(w_ref, g_ref, m_ref, v_ref, d_ref, nm_ref, nv_ref):
        gv = g_ref[...]
        m_new = ADAM_B1 * m_ref[...] + (1.0 - ADAM_B1) * gv
        v_new = ADAM_B2 * v_ref[...] + (1.0 - ADAM_B2) * (gv * gv)
        m_hat = m_new / (1.0 - ADAM_B1 ** ADAM_STEP)
        v_hat = v_new / (1.0 - ADAM_B2 ** ADAM_STEP)
        d_ref[...] = -ADAM_LR * (m_hat / (jnp.sqrt(v_hat) + ADAM_EPS) + ADAM_WD * w_ref[...])
        nm_ref[...] = m_new
        nv_ref[...] = v_new

    spec = pl.BlockSpec((tr, c), lambda i: (i, 0))
    shp = jax.ShapeDtypeStruct((r, c), F32)
    return pl.pallas_call(
        body, name=name, grid=(r // tr,), in_specs=[spec] * 4, out_specs=[spec] * 3, out_shape=[shp] * 3,
        compiler_params=_cp(("parallel",)))(w, g, m, v)


def _sum_rows_block(h):
    return h if h <= 352 else 256


def _pair_sum(view, recv, sel, name):
    n, _, h, c = view.shape
    tr = _sum_rows_block(h)

    def body(sel_ref, a_ref, b_ref, o_ref, ob_ref):
        t = a_ref[...] + b_ref[...]
        o_ref[...] = t
        ob_ref[...] = t.astype(BF16)

    blk = pl.BlockSpec((None, tr, c), lambda j, i, sel_ref: (j, i, 0))
    grid_spec = pltpu.PrefetchScalarGridSpec(
        num_scalar_prefetch=1, grid=(n, h // tr),
        in_specs=[pl.BlockSpec((None, None, tr, c), lambda j, i, sel_ref: (j, sel_ref[0], i, 0)),
                  pl.BlockSpec((None, None, tr, c), lambda j, i, sel_ref: (j, 0, i, 0))],
        out_specs=[blk, blk])
    return pl.pallas_call(
        body, name=name, grid_spec=grid_spec,
        out_shape=[jax.ShapeDtypeStruct((n, h, c), F32), jax.ShapeDtypeStruct((n, h, c), BF16)],
        compiler_params=_cp(("parallel", "parallel")))(sel, view, recv)


def _chip_sum(pair, got, sel, name):
    _, h, c = pair.shape
    tr = _sum_rows_block(h)
    nblk = h // tr

    def body(sel_ref, p_ref, g0_ref, g1_ref, g2_ref, o_ref):
        o_ref[...] = ((p_ref[...] + g0_ref[...].astype(F32)) + g1_ref[...].astype(F32)) + g2_ref[...].astype(F32)

    slot = lambda k: pl.BlockSpec((None, tr, c), lambda i, sel_ref: (k, i, 0))
    grid_spec = pltpu.PrefetchScalarGridSpec(
        num_scalar_prefetch=1, grid=(h // tr,),
        in_specs=[pl.BlockSpec((None, tr, c), lambda i, sel_ref: (sel_ref[1], i, 0)), slot(0), slot(1), slot(2)],
        out_specs=pl.BlockSpec((tr, c), lambda i, sel_ref: (sel_ref[0] * nblk + i, 0)))
    return pl.pallas_call(
        body, name=name, grid_spec=grid_spec, out_shape=jax.ShapeDtypeStruct((2 * h, c), F32),
        compiler_params=_cp(("parallel",)))(sel, pair, got, got, got)


def _place():
    return lax.axis_index("x"), lax.axis_index("y"), lax.axis_index("c")


def _other_chips(x, y):
    return [(1 - x, y), (x, 1 - y), (1 - x, 1 - y)]


def _hbm_specs(n):
    return [pl.BlockSpec(memory_space=pl.ANY)] * n


def _all_gather_weights(bigs, smalls):
    nb, ns = len(bigs), len(smalls)
    n = nb + ns

    def body(*refs):
        start, forward, finish = _gather_phases(refs[:n], refs[2 * n:3 * n], nb, *refs[3 * n:])
        start()
        forward()
        finish()

    arrays, landing, sems = _gather_operands(bigs, smalls)
    return pl.pallas_call(
        body, name="all_gather_weights",
        out_shape=[jax.ShapeDtypeStruct(b.shape, b.dtype) for b in landing],
        in_specs=_hbm_specs(2 * n), out_specs=_hbm_specs(n), input_output_aliases={n + k: k for k in range(n)},
        scratch_shapes=sems)(*arrays, *landing)


def _hosted_gather(refs, n, nb, step, total):
    ins, outs, send_sems, recv_sems = refs
    start, forward, finish = _gather_phases(ins, outs, nb, send_sems, recv_sems)
    pl.when(step == 0)(start)
    pl.when(step == (3 * total) // 4)(forward)
    return lambda: pl.when(step == total - 1)(finish)


def _in_proj(x, g, w_a, w_b, w_c, bigs, smalls):
    s, d = x.shape
    tm, tn = min(TM_MM, s), ATTN_W
    na, nq = w_a.shape[1] // tn, w_b.shape[1] // tn
    steps = na + nq + 1
    total = (s // tm) * steps
    nb, n = len(bigs), len(bigs) + len(smalls)
    arrays, landing, sems = _gather_operands(bigs, smalls)

    def body(x_ref, g_ref, wa_ref, wb_ref, wc_ref, *rest):
        z_ref, qkv_ref, f_ref, h_ref = rest[2 * n:2 * n + 4]
        h_scr = rest[-1]
        m, j = pl.program_id(0), pl.program_id(1)
        finish = _hosted_gather((rest[:n], rest[2 * n + 4:3 * n + 4]) + tuple(rest[3 * n + 4:3 * n + 6]), n, nb,
                                m * steps + j, total)

        @pl.when(j == 0)
        def _():
            xv = x_ref[...]
            hv = (xv * _rstd(xv) * g_ref[...]).astype(BF16)
            h_scr[...] = hv
            h_ref[...] = hv

        h = h_scr[...]

        @pl.when(j < na)
        def _():
            z_ref[...] = lax.dot_general(h, wa_ref[...], _NN, preferred_element_type=F32)

        @pl.when((j >= na) & (j < na + nq))
        def _():
            qkv_ref[...] = lax.dot_general(h, wb_ref[...], _NN, preferred_element_type=F32).astype(BF16)

        @pl.when(j == na + nq)
        def _():
            f_ref[...] = lax.dot_general(h, wc_ref[...], _NN, preferred_element_type=F32)

        finish()

    blk_a = lambda m, j: (m, jnp.minimum(j, na - 1))
    blk_b = lambda m, j: (m, jnp.clip(j - na, 0, nq - 1))
    outs = pl.pallas_call(
        body, name="in_proj", grid=(s // tm, steps),
        in_specs=[pl.BlockSpec((tm, d), lambda m, j: (m, 0)), pl.BlockSpec((1, d), lambda m, j: (0, 0)),
                  pl.BlockSpec((d, tn), lambda m, j: (0, jnp.minimum(j, na - 1))),
                  pl.BlockSpec((d, tn), lambda m, j: (0, jnp.clip(j - na, 0, nq - 1))),
                  pl.BlockSpec((d, LANES), lambda m, j: (0, 0))] + _hbm_specs(2 * n),
        out_specs=[pl.BlockSpec((tm, tn), blk_a), pl.BlockSpec((tm, tn), blk_b),
                   pl.BlockSpec((tm, LANES), lambda m, j: (m, 0)), pl.BlockSpec((tm, d), lambda m, j: (m, 0))]
        + _hbm_specs(n),
        out_shape=[jax.ShapeDtypeStruct((s, w_a.shape[1]), F32), jax.ShapeDtypeStruct((s, w_b.shape[1]), BF16),
                   jax.ShapeDtypeStruct((s, LANES), F32), jax.ShapeDtypeStruct((s, d), BF16)]
        + [jax.ShapeDtypeStruct(b.shape, b.dtype) for b in landing],
        input_output_aliases={5 + n + k: 4 + k for k in range(n)},
        scratch_shapes=sems + [pltpu.VMEM((tm, d), BF16)],
        compiler_params=_cp(("arbitrary", "arbitrary")))(x, g, w_a, w_b, w_c, *arrays, *landing)
    return outs[0], outs[1], outs[2], outs[3], outs[4:]


def _gather_operands(bigs, smalls):
    x, y, _ = _place()
    arrays = list(bigs) + list(smalls)
    landing = [lax.dynamic_update_index_in_dim(lax.empty((N_CHIPS,) + a.shape, a.dtype), a, 2 * x + y, 0)
               for a in arrays]
    n_sems = 6 * len(bigs) + 3 * len(smalls)
    return arrays, landing, [pltpu.SemaphoreType.DMA((n_sems,)), pltpu.SemaphoreType.DMA((n_sems,))]


def _gather_phases(ins, outs, nb, send_sems, recv_sems):
    n = len(ins)
    x, y, c = _place()
    my_chip = 2 * x + y
    chips = _other_chips(x, y)
    sibling = (x, y, 1 - c)

    def rows(k, which):
        h = ins[k].shape[0] // 2
        return pl.ds(which * h, h)

    def copy(sem, src, dst, to):
        return pltpu.make_async_remote_copy(src_ref=src, dst_ref=dst, send_sem=send_sems.at[sem],
                                            recv_sem=recv_sems.at[sem], device_id=to, device_id_type=MESH)

    def sends():
        out = [copy(6 * k + j, ins[k].at[rows(k, c)], outs[k].at[my_chip, rows(k, c)], (cx, cy, c))
               for k in range(nb) for j, (cx, cy) in enumerate(chips)]
        return out + [copy(6 * nb + 3 * (k - nb) + j, ins[k], outs[k].at[my_chip], (cx, cy, c))
                      for k in range(nb, n) for j, (cx, cy) in enumerate(chips)]

    def landed(k, j, which):
        cx, cy = chips[j]
        return outs[k].at[2 * cx + cy, rows(k, which)]

    def forwards():
        return [copy(6 * k + 3 + j, landed(k, j, c), landed(k, j, c), sibling)
                for j in range(3) for k in range(nb)]

    def start():
        for cp in sends():
            cp.start()

    def forward():
        for j in range(3):
            for k in range(nb):
                copy(6 * k + j, landed(k, j, c), landed(k, j, c), (x, y, c)).wait_recv()
                copy(6 * k + 3 + j, landed(k, j, c), landed(k, j, c), sibling).start()

    def finish():
        for j, (cx, cy) in enumerate(chips):
            for k in range(nb):
                copy(6 * k + 3 + j, landed(k, j, 1 - c), landed(k, j, 1 - c), (x, y, c)).wait_recv()
            for k in range(nb, n):
                arrived = outs[k].at[2 * cx + cy]
                copy(6 * nb + 3 * (k - nb) + j, arrived, arrived, (x, y, c)).wait_recv()
        for cp in sends() + forwards():
            cp.wait_send()

    return start, forward, finish


def _pair_exchange(views, name):
    n = len(views)

    def body(*refs):
        ins, outs, send_sems, recv_sems = refs[:n], refs[n:2 * n], refs[2 * n], refs[2 * n + 1]
        x, y, c = _place()
        copies = [pltpu.make_async_remote_copy(
            src_ref=ins[k].at[:, pl.ds(1 - c, 1)], dst_ref=outs[k], send_sem=send_sems.at[k],
            recv_sem=recv_sems.at[k], device_id=(x, y, 1 - c), device_id_type=MESH) for k in range(n)]
        for cp in copies:
            cp.start()
        for cp in copies:
            cp.wait()

    return pl.pallas_call(
        body, name=name,
        out_shape=[jax.ShapeDtypeStruct((v.shape[0], 1) + v.shape[2:], v.dtype) for v in views],
        in_specs=_hbm_specs(n), out_specs=_hbm_specs(n),
        scratch_shapes=[pltpu.SemaphoreType.DMA((n,)), pltpu.SemaphoreType.DMA((n,))])(*views)


def _scatter_operands(parts):
    n = len(parts)
    return ([jax.ShapeDtypeStruct((3,) + p.shape[1:], p.dtype) for p in parts],
            [pltpu.SemaphoreType.DMA((3 * n,)), pltpu.SemaphoreType.DMA((3 * n,))])


def _scatter_phases(ins, outs, send_sems, recv_sems):
    x, y, c = _place()

    def copies():
        return [pltpu.make_async_remote_copy(
            src_ref=ins[k].at[pl.ds(2 * cx + cy, 1)], dst_ref=outs[k].at[pl.ds(r, 1)], send_sem=send_sems.at[3 * k + r],
            recv_sem=recv_sems.at[3 * k + r], device_id=(cx, cy, c), device_id_type=MESH)
            for k in range(len(ins)) for r, (cx, cy) in enumerate(_other_chips(x, y))]

    def start():
        for cp in copies():
            cp.start()

    def finish():
        for cp in copies():
            cp.wait()

    return start, finish


def _hosted_scatter(ins, outs, sems, step, total):
    start, finish = _scatter_phases(ins, outs, *sems)
    pl.when(step == 0)(start)
    return lambda: pl.when(step == total - 1)(finish)


def _join_halves(shards):
    n = len(shards)

    def body(*refs):
        ins, outs, send_sems, recv_sems = refs[:n], refs[n:2 * n], refs[2 * n], refs[2 * n + 1]
        x, y, c = _place()

        def rows(ref, which):
            h = ref.shape[0] // 2
            return ref.at[pl.ds(which * h, h)]

        sent = [pltpu.make_async_remote_copy(
            src_ref=rows(ins[k], c), dst_ref=rows(outs[k], c), send_sem=send_sems.at[k], recv_sem=recv_sems.at[k],
            device_id=(x, y, 1 - c), device_id_type=MESH) for k in range(n)]
        for cp in sent:
            cp.start()
        for k in range(n):
            pltpu.make_async_remote_copy(
                src_ref=rows(ins[k], 1 - c), dst_ref=rows(outs[k], 1 - c), send_sem=send_sems.at[k],
                recv_sem=recv_sems.at[k], device_id=(x, y, 1 - c), device_id_type=MESH).wait_recv()
        for cp in sent:
            cp.wait_send()

    return pl.pallas_call(
        body, name="half_exchange", out_shape=[jax.ShapeDtypeStruct(a.shape, a.dtype) for a in shards],
        in_specs=_hbm_specs(n), out_specs=_hbm_specs(n), input_output_aliases={k: k for k in range(n)},
        scratch_shapes=[pltpu.SemaphoreType.DMA((n,)), pltpu.SemaphoreType.DMA((n,))])(*shards)


def _all_reduce_small(packet):
    rows, width = packet.shape
    n_dev = 8

    def body(x_ref, out_ref, gath, send_sems, recv_sems):
        x, y, c = _place()
        me, sibling = (x, y, c), (x, y, 1 - c)
        chips = _other_chips(x, y)

        def slot(px, py, pc):
            return gath.at[pl.ds((4 * px + 2 * py + pc) * rows, rows), :]

        def copy(k, block, to, src=None):
            return pltpu.make_async_remote_copy(
                src_ref=slot(*block) if src is None else src, dst_ref=slot(*block), send_sem=send_sems.at[k],
                recv_sem=recv_sems.at[k], device_id=to, device_id_type=MESH)

        first = [copy(0, me, sibling, src=x_ref)]
        first += [copy(1 + j, me, (*chip, c), src=x_ref) for j, chip in enumerate(chips)]
        for cp in first:
            cp.start()
        gath[pl.ds((4 * x + 2 * y + c) * rows, rows), :] = x_ref[...]
        passed = [copy(4 + j, (*chip, c), sibling) for j, chip in enumerate(chips)]
        for j, chip in enumerate(chips):
            copy(1 + j, (*chip, c), me).wait_recv()
            passed[j].start()
        copy(0, sibling, me).wait_recv()
        for j, chip in enumerate(chips):
            copy(4 + j, (*chip, 1 - c), me).wait_recv()
        for cp in first + passed:
            cp.wait_send()
        acc = gath[0:rows, :]
        for d in range(1, n_dev):
            acc = acc + gath[d * rows:(d + 1) * rows, :]
        out_ref[...] = acc

    return pl.pallas_call(
        body, name="all_reduce_small", out_shape=jax.ShapeDtypeStruct((rows, width), F32),
        in_specs=[pl.BlockSpec(memory_space=pltpu.VMEM)], out_specs=pl.BlockSpec(memory_space=pltpu.VMEM),
        scratch_shapes=[pltpu.VMEM((n_dev * rows, width), F32), pltpu.SemaphoreType.DMA((7,)),
                        pltpu.SemaphoreType.DMA((7,))])(packet)


def _flat_rows(parts, width, row_multiple):
    flat = jnp.concatenate([p.astype(F32).reshape(-1) for p in parts])
    rows = -(-flat.shape[0] // width)
    rows = -(-rows // row_multiple) * row_multiple
    return jnp.pad(flat, (0, rows * width - flat.shape[0])).reshape(rows, width)


def _unflatten(flat2d, shapes):
    flat = flat2d.reshape(-1)
    out, off = [], 0
    for shp in shapes:
        n = 1
        for dim in shp:
            n *= dim
        out.append(flat[off:off + n].reshape(shp))
        off += n
    return out


def _core_and_chip():
    x, y, c = _place()
    return jnp.stack([c, 2 * x + y]).astype(jnp.int32)


def _pair_sums(chip_major, names, call_name):
    views = [g.reshape(N_CHIPS, 2, g.shape[1] // 2, g.shape[2]) for g in chip_major]
    recv = _pair_exchange(views, call_name)
    sel = _core_and_chip()
    return [_pair_sum(v, r, sel, "pair_sum_" + nm) for v, r, nm in zip(views, recv, names)]


def _finish_grads(pairs, got, names):
    sel = _core_and_chip()
    return _join_halves([_chip_sum(p, g, sel, "chip_sum_" + nm) for (p, _), g, nm in zip(pairs, got, names)])


def kernel(x, g_mix, w_in, b_f, w_conv, g_conv_out, g_attn_out, w_o, g_ffn, w_up, w_ffn_conv, w_down, g_final, loss_target, m_g_mix, m_w_in, m_b_f, m_w_conv, m_g_conv_out, m_g_attn_out, m_w_o, m_g_ffn, m_w_up, m_w_ffn_conv, m_w_down, m_g_final, v_g_mix, v_w_in, v_b_f, v_w_conv, v_g_conv_out, v_g_attn_out, v_w_o, v_g_ffn, v_w_up, v_w_ffn_conv, v_w_down, v_g_final):
    s = x.shape[1]
    x0 = x[0]
    target = loss_target[0]
    d = D_MODEL
    x_pos, y_pos, _ = _place()
    my_chip = 2 * x_pos + y_pos

    (c_in,) = _all_gather_weights([w_in[0].astype(BF16)], [])
    w_in_full = jnp.concatenate([c_in[j] for j in range(N_CHIPS)], axis=1)
    c3 = 3 * CONV_CH
    w_a, w_b = w_in_full[:, :c3], w_in_full[:, c3:c3 + 3 * ATTN_W]
    w_c = jnp.pad(w_in_full[:, c3 + 3 * ATTN_W:], ((0, 0), (0, LANES - N_HEADS)))
    w_q, w_k, w_v = (w_b[:, i * ATTN_W:(i + 1) * ATTN_W] for i in range(3))
    b_pad = jnp.pad(b_f, ((0, 0), (0, LANES - N_HEADS)))

    z_a, qkv, f_log, h1, (c_o, c_up, c_conv, c_ffn) = _in_proj(
        x0, g_mix, w_a, w_b, w_c, [w_o[0].astype(BF16), w_up[0].astype(BF16)], [w_conv[0], w_ffn_conv[0]])
    fb = _gate_fwd(f_log, b_pad)
    qx, kx, kxt, vx, vt, bounds, (c_down,) = _attn_prep(qkv, fb, [w_down[0].astype(BF16)])
    w_o_full = c_o.reshape(d, d)
    w_down_full = c_down.reshape(D_FF, d)
    w_conv_full = jnp.concatenate([c_conv[j] for j in range(N_CHIPS)], axis=1)
    w_ffn_full = jnp.concatenate([c_ffn[j] for j in range(N_CHIPS)], axis=1)
    n_up = c_up.shape[2]
    first_blk, last_blk = _key_block_ranges(bounds)
    o_attn, lse = _attn_fwd_t(qx, kx, vt, first_blk)
    mix = _mixer_fwd(z_a, o_attn, w_conv_full, g_conv_out, g_attn_out)
    x2, h2 = _mm("nn", [mix], [w_o_full], F32, 512, d, "out_proj", add=x0, norm_g=g_ffn)
    up = _mm("nn", [h2], [c_up], BF16, TM_MM, n_up, "up_proj", b_chips=True)
    act, u_conv = _ffn_act_fwd(up, w_ffn_full)
    dx3, dx3_b, loss_row, gg_final = _down_proj_loss(act, w_down_full, x2, target, g_final.reshape(1, d))

    dact = _mm("nt", [dx3_b], [w_down_full], F32, TM_MM, 1408, "d_act")
    gw_down = _mm_tn(act, dx3_b, 1408, 1024, "gw_down")
    dup, gwf_lin, gwf_gate = _ffn_act_bwd(up, u_conv, dact, w_ffn_full)
    dh2 = _mm("nt", [(dup, j // 2, j % 2, n_up) for j in range(N_CHIPS)], [(c_up, j) for j in range(N_CHIPS)],
              F32, 512, d, "d_h2")
    gw_up = _mm_tn(h2, dup, 1024, n_up, "gw_up", out_chips=True)
    dx2, dx2_b, gg_ffn = _rms_bwd(x2, dh2, g_ffn, dx3, "rms_ffn_bwd", True)
    dmix = _mm("nt", [dx2_b], [w_o_full], F32, TM_MM, 512, "d_mix")
    gw_o = _mm_tn(mix, dx2_b, 1024, 1024, "gw_o")
    early = _pair_sums([gw_o.reshape(N_CHIPS, d // N_CHIPS, d), gw_up, gw_down.reshape(N_CHIPS, D_FF // N_CHIPS, d)],
                       ["w_o", "w_up", "w_down"], "pair_exchange")
    dz_a, dox, gw_conv, gg_conv_out, gg_attn_out, (got_o, got_down) = _mixer_bwd(
        z_a, o_attn, dmix, w_conv_full, g_conv_out, g_attn_out, [early[0][1], early[2][1]])
    dk, dv, dfk, dqt, dfq = _attn_bwd_t(qx, dox, kx, kxt, vx, lse, last_blk)
    dq = _attn_dq_finish(dqt)
    d_f = (jnp.transpose(dfk[:, :, 0:2, :], (1, 3, 0, 2)).reshape(s, N_HEADS)
           + jnp.transpose(jnp.sum(dfq, axis=2), (1, 2, 0)).reshape(s, N_HEADS))
    df_b, gb_f = _gate_bwd(f_log, b_pad, jnp.pad(d_f, ((0, 0), (0, LANES - N_HEADS))))
    gw_a = _mm_tn(h1, dz_a, 1024, c3, "gw_in_conv")
    gw_q = _mm_tn(h1, dq, 1024, ATTN_W, "gw_in_q")
    gw_k = _mm_tn(h1, dk, 1024, ATTN_W, "gw_in_k")
    gw_v = _mm_tn(h1, dv, 1024, ATTN_W, "gw_in_v")
    gw_c = _mm_tn(h1, df_b, 1024, LANES, "gw_in_gate")
    gw_in = jnp.concatenate([gw_a, gw_q, gw_k, gw_v, gw_c[:, :N_HEADS]], axis=1)
    n_in = IN_COLS // N_CHIPS
    gw_in = jnp.stack([gw_in[:, j * n_in:(j + 1) * n_in] for j in range(N_CHIPS)])
    late = _pair_sums([gw_in], ["w_in"], "pair_exchange_w_in")
    dh1, (got_up, got_in) = _mm("nt", [dz_a, dq, dk, dv, df_b], [w_a, w_q, w_k, w_v, w_c], F32, TM_MM, 512, "d_h1",
                                scatter=[early[1][1], late[0][1]])
    grad_x, gg_mix = _rms_bwd(x0, dh1, g_mix, dx2, "rms_mix_bwd", False)
    g_w_in, g_w_o, g_w_up, g_w_down = _finish_grads(late + early, [got_in, got_o, got_up, got_down],
                                                    ["w_in", "w_o", "w_up", "w_down"])

    gw_ffn = jnp.concatenate([gwf_lin, gwf_gate], axis=1)
    small_parts = [gg_mix, gg_conv_out, gg_attn_out, gg_ffn, gg_final, gb_f[:, :N_HEADS], loss_row[:, 0:1], gw_conv,
                   gw_ffn]
    small_shapes = [a.shape for a in small_parts]
    tot = _unflatten(_all_reduce_small(_flat_rows(small_parts, d, 8)), small_shapes)
    g_g_mix, g_g_conv_out, g_g_attn_out, g_g_ffn, g_g_final, g_b_f, loss_sum, g_conv_full, g_ffn_full = tot
    loss = loss_sum[0, 0]
    g_g_final = g_g_final[0]
    g_w_conv = lax.dynamic_slice_in_dim(g_conv_full, my_chip * (CONV_CH // N_CHIPS), CONV_CH // N_CHIPS, axis=1)
    g_w_ffn = lax.dynamic_slice_in_dim(g_ffn_full, my_chip * n_up, n_up, axis=1)

    def adam_big(w, g, m, v, name):
        dl, nm, nv = _adamw(w[0], g, m[0], v[0], name)
        return dl[None], nm[None], nv[None]

    u_w_in = adam_big(w_in, g_w_in, m_w_in, v_w_in, "adam_w_in")
    u_w_o = adam_big(w_o, g_w_o, m_w_o, v_w_o, "adam_w_o")
    u_w_up = adam_big(w_up, g_w_up, m_w_up, v_w_up, "adam_w_up")
    u_w_down = adam_big(w_down, g_w_down, m_w_down, v_w_down, "adam_w_down")

    small_w = [g_mix, b_f, g_conv_out, g_attn_out, g_ffn, g_final, w_conv, w_ffn_conv]
    small_g = [g_g_mix, g_b_f, g_g_conv_out, g_g_attn_out, g_g_ffn, g_g_final, g_w_conv, g_w_ffn]
    small_m = [m_g_mix, m_b_f, m_g_conv_out, m_g_attn_out, m_g_ffn, m_g_final, m_w_conv, m_w_ffn_conv]
    small_v = [v_g_mix, v_b_f, v_g_conv_out, v_g_attn_out, v_g_ffn, v_g_final, v_w_conv, v_w_ffn_conv]
    shapes = [a.shape for a in small_w]
    pack = lambda arrs: _flat_rows(arrs, LANES, 8)
    sd, sm, sv = _adamw(pack(small_w), pack(small_g), pack(small_m), pack(small_v), "adam_small")
    sd, sm, sv = _unflatten(sd, shapes), _unflatten(sm, shapes), _unflatten(sv, shapes)
    (d_g_mix, d_b_f, d_g_conv_out, d_g_attn_out, d_g_ffn, d_g_final, d_w_conv, d_w_ffn) = sd
    (nm_g_mix, nm_b_f, nm_g_conv_out, nm_g_attn_out, nm_g_ffn, nm_g_final, nm_w_conv, nm_w_ffn) = sm
    (nv_g_mix, nv_b_f, nv_g_conv_out, nv_g_attn_out, nv_g_ffn, nv_g_final, nv_w_conv, nv_w_ffn) = sv

    grads = (g_g_mix, g_w_in[None], g_b_f, g_w_conv[None], g_g_conv_out, g_g_attn_out, g_w_o[None], g_g_ffn,
             g_w_up[None], g_w_ffn[None], g_w_down[None], g_g_final)
    deltas = (d_g_mix, u_w_in[0], d_b_f, d_w_conv, d_g_conv_out, d_g_attn_out, u_w_o[0], d_g_ffn, u_w_up[0],
              d_w_ffn, u_w_down[0], d_g_final)
    new_m = (nm_g_mix, u_w_in[1], nm_b_f, nm_w_conv, nm_g_conv_out, nm_g_attn_out, u_w_o[1], nm_g_ffn, u_w_up[1],
             nm_w_ffn, u_w_down[1], nm_g_final)
    new_v = (nv_g_mix, u_w_in[2], nv_b_f, nv_w_conv, nv_g_conv_out, nv_g_attn_out, u_w_o[2], nv_g_ffn, u_w_up[2],
             nv_w_ffn, u_w_down[2], nv_g_final)
    return (loss, grad_x[None], *grads, *deltas, *new_m, *new_v)
```

```python
import jax
import jax.numpy as jnp
from jax import lax
from jax.experimental import pallas as pl
from jax.experimental.pallas import tpu as pltpu

F32, BF16 = jnp.float32, jnp.bfloat16
MESH = pl.DeviceIdType.MESH

D_MODEL = 1024
CONV_CH = 512
ATTN_W = 512
N_HEADS = 8
HEAD_DIM = 64
N_PAIRS = N_HEADS // 2
D_FF = 2816
IN_COLS = 3 * CONV_CH + 3 * ATTN_W + N_HEADS
EPS = 1e-6
Q_SCALE = 0.125
EXP_ZERO = 88.0
N_CHIPS = 4
LANES = 128
HALO = 8
HALO_BF16 = 2 * HALO

ADAM_LR, ADAM_B1, ADAM_B2, ADAM_EPS, ADAM_WD, ADAM_STEP = 0.001, 0.9, 0.999, 1e-08, 0.01, 10

TM_ROWS = 512
TM_MM = 1024
TK_TN = 1024
TQ = 512
TM_FFN = 256
TN_FFN = 1408
VMEM_LIMIT = 52 * 2**20


def _cp(sem, vmem=VMEM_LIMIT):
    return pltpu.CompilerParams(dimension_semantics=sem, vmem_limit_bytes=vmem)


def _bf(a):
    return a if a.dtype == BF16 else a.astype(BF16)


def _mm(mode, a_list, b_list, out_dtype, tm, tn, name, add=None, b_chips=False, scatter=(), norm_g=None):
    n_p = len(a_list)
    a0 = a_list[0]
    m_dim = a0[0].shape[1] if isinstance(a0, tuple) else a0.shape[0]
    b0 = b_list[0]
    if b_chips:
        n_dim = b0.shape[0] * b0.shape[2]
        assert tn == b0.shape[2] and mode == "nn"
    else:
        b0 = b0[0][b0[1]] if isinstance(b0, tuple) else b0
        n_dim = b0.shape[1 if mode == "nn" else 0]
    tm, tn = min(tm, m_dim), min(tn, n_dim)
    assert m_dim % tm == 0 and n_dim % tn == 0
    dims = (((1,), (0,)), ((), ())) if mode == "nn" else (((1,), (1,)), ((), ()))
    in_specs, args = [], []
    for a in a_list:
        if isinstance(a, tuple):
            arr, lead, col, width = a
            in_specs.append(pl.BlockSpec((None, tm, width), lambda m, n, lead=lead, col=col: (lead, m, col)))
        else:
            arr = a
            in_specs.append(pl.BlockSpec((tm, a.shape[1]), lambda m, n: (m, 0)))
        args.append(arr)
    for b in b_list:
        if b_chips:
            arr = b
            in_specs.append(pl.BlockSpec((None, b.shape[1], tn), lambda m, n: (n, 0, 0)))
        elif isinstance(b, tuple):
            arr, lead = b
            if mode == "nn":
                in_specs.append(pl.BlockSpec((None, arr.shape[1], tn), lambda m, n, lead=lead: (lead, 0, n)))
            else:
                in_specs.append(pl.BlockSpec((None, tn, arr.shape[2]), lambda m, n, lead=lead: (lead, n, 0)))
        elif mode == "nn":
            arr = b
            in_specs.append(pl.BlockSpec((b.shape[0], tn), lambda m, n: (0, n)))
        else:
            arr = b
            in_specs.append(pl.BlockSpec((tn, b.shape[1]), lambda m, n: (n, 0)))
        args.append(arr)
    if add is not None:
        in_specs.append(pl.BlockSpec((tm, tn), lambda m, n: (m, n)))
        args.append(add)
    if norm_g is not None:
        assert tn == n_dim and not scatter
        in_specs.append(pl.BlockSpec((1, tn), lambda m, n: (0, 0)))
        args.append(norm_g)

    n_in = len(args)
    n_sc = len(scatter)
    grid = (m_dim // tm, n_dim // tn)

    def body(*refs):
        o_ref = refs[n_in + n_sc]
        if n_sc:
            finish = _hosted_scatter(refs[n_in:n_in + n_sc], refs[n_in + n_sc + 1:n_in + 2 * n_sc + 1],
                                     refs[n_in + 2 * n_sc + 1:], pl.program_id(0) * grid[1] + pl.program_id(1),
                                     grid[0] * grid[1])
        acc = None
        for i in range(n_p):
            d = lax.dot_general(_bf(refs[i][...]), _bf(refs[n_p + i][...]), dims,
                                preferred_element_type=F32)
            acc = d if acc is None else acc + d
        if add is not None:
            acc = refs[2 * n_p][...] + acc
        o_ref[...] = acc.astype(out_dtype)
        if norm_g is not None:
            refs[n_in + 1][...] = (acc * _rstd(acc) * refs[n_in - 1][...]).astype(BF16)
        if n_sc:
            finish()

    main_spec = pl.BlockSpec((tm, tn), lambda m, n: (m, n))
    main_shape = jax.ShapeDtypeStruct((m_dim, n_dim), out_dtype)
    if norm_g is not None:
        return pl.pallas_call(body, name=name, grid=grid, in_specs=in_specs, out_specs=[main_spec, main_spec],
                              out_shape=[main_shape, jax.ShapeDtypeStruct((m_dim, n_dim), BF16)],
                              compiler_params=_cp(("parallel", "parallel")))(*args)
    if not n_sc:
        return pl.pallas_call(body, name=name, grid=grid, in_specs=in_specs, out_specs=main_spec,
                              out_shape=main_shape, compiler_params=_cp(("parallel", "parallel")))(*args)
    got_shapes, sems = _scatter_operands(scatter)
    outs = pl.pallas_call(
        body, name=name, grid=grid, in_specs=in_specs + _hbm_specs(n_sc), out_specs=[main_spec] + _hbm_specs(n_sc),
        out_shape=[main_shape] + got_shapes, scratch_shapes=sems,
        compiler_params=_cp(("arbitrary", "arbitrary")))(*args, *scatter)
    return outs[0], outs[1:]


def _mm_tn(a, b, tm, tn, name, out_chips=False):
    k_dim, m_dim = a.shape
    n_dim = b.shape[-1] * (b.shape[0] if b.ndim == 3 else 1)
    tm, tn, tk = min(tm, m_dim), min(tn, b.shape[-1]), min(TK_TN, k_dim)
    assert m_dim % tm == 0 and b.shape[-1] % tn == 0 and k_dim % tk == 0
    per = b.shape[-1] // tn
    if b.ndim == 3:
        b_spec = pl.BlockSpec((None, tk, tn), lambda m, n, k: (n // per, k, n % per))
    else:
        b_spec = pl.BlockSpec((tk, tn), lambda m, n, k: (k, n))

    def body(a_ref, b_ref, o_ref):
        @pl.when(pl.program_id(2) == 0)
        def _():
            o_ref[...] = jnp.zeros_like(o_ref)
        o_ref[...] += lax.dot_general(_bf(a_ref[...]), _bf(b_ref[...]), (((0,), (0,)), ((), ())),
                                      preferred_element_type=F32)

    return pl.pallas_call(
        body, name=name, grid=(m_dim // tm, n_dim // tn, k_dim // tk),
        in_specs=[pl.BlockSpec((tk, tm), lambda m, n, k: (k, m)), b_spec],
        out_specs=(pl.BlockSpec((None, tm, tn), lambda m, n, k: (n, m, 0)) if out_chips
                   else pl.BlockSpec((tm, tn), lambda m, n, k: (m, n))),
        out_shape=jax.ShapeDtypeStruct((n_dim // tn, m_dim, tn) if out_chips else (m_dim, n_dim), F32),
        compiler_params=_cp(("parallel", "parallel", "arbitrary")))(a, b)


def _rstd(x):
    return lax.rsqrt(jnp.mean(x * x, axis=-1, keepdims=True) + EPS)


def _rms_bwd(x, dh, g, dres, name, with_bf16, scatter=()):
    s, d = x.shape
    tm = min(TM_ROWS, s)
    n_sc = len(scatter)
    n_out = 3 if with_bf16 else 2
    got_shapes, sems = _scatter_operands(scatter) if n_sc else ([], [])

    def body(x_ref, dh_ref, g_ref, dres_ref, *rest):
        dx_ref, gg_ref = rest[n_sc], rest[n_sc + n_out - 1]
        i = pl.program_id(0)
        if n_sc:
            finish = _hosted_scatter(rest[:n_sc], rest[n_sc + n_out:2 * n_sc + n_out], rest[2 * n_sc + n_out:], i,
                                     s // tm)

        @pl.when(i == 0)
        def _():
            gg_ref[...] = jnp.zeros_like(gg_ref)

        xv = x_ref[...]
        xn = xv * _rstd(xv)
        dhv = dh_ref[...]
        gg_ref[...] += jnp.sum(dhv * xn, axis=0, keepdims=True)
        t = dhv * g_ref[...]
        dx = dres_ref[...] + _rstd(xv) * (t - xn * jnp.mean(t * xn, axis=-1, keepdims=True))
        dx_ref[...] = dx
        if with_bf16:
            rest[n_sc + 1][...] = dx.astype(BF16)
        if n_sc:
            finish()

    row = pl.BlockSpec((tm, d), lambda i: (i, 0))
    vec = pl.BlockSpec((1, d), lambda i: (0, 0))
    out_specs = [row] + ([row] if with_bf16 else []) + [vec] + _hbm_specs(n_sc)
    out_shape = ([jax.ShapeDtypeStruct((s, d), F32)] + ([jax.ShapeDtypeStruct((s, d), BF16)] if with_bf16 else [])
                 + [jax.ShapeDtypeStruct((1, d), F32)] + got_shapes)
    outs = pl.pallas_call(
        body, name=name, grid=(s // tm,), in_specs=[row, row, vec, row] + _hbm_specs(n_sc), out_specs=out_specs,
        out_shape=out_shape, scratch_shapes=sems, compiler_params=_cp(("arbitrary",)))(x, dh, g, dres, *scatter)
    return tuple(outs[:n_out]) + ((outs[n_out:],) if n_sc else ())


def _down_proj_loss(act, w_down, x2, target, g):
    s, d = x2.shape
    tm = min(TM_ROWS, s)
    k_dim = act.shape[1]

    def body(a_ref, w_ref, x_ref, t_ref, g_ref, dx_ref, dxb_ref, loss_ref, gg_ref):
        @pl.when(pl.program_id(0) == 0)
        def _():
            gg_ref[...] = jnp.zeros_like(gg_ref)
            loss_ref[...] = jnp.zeros_like(loss_ref)

        xv = x_ref[...] + lax.dot_general(a_ref[...], w_ref[...], (((1,), (0,)), ((), ())),
                                          preferred_element_type=F32)
        r = _rstd(xv)
        xn = xv * r
        gv = g_ref[...]
        err = xn * gv - t_ref[...]
        loss_ref[...] += 0.5 * jnp.sum(jnp.mean(err * err, axis=-1, keepdims=True), axis=0, keepdims=True)
        dy = err * (1.0 / d)
        gg_ref[...] += jnp.sum(dy * xn, axis=0, keepdims=True)
        t = dy * gv
        dx = r * (t - xn * jnp.mean(t * xn, axis=-1, keepdims=True))
        dx_ref[...] = dx
        dxb_ref[...] = dx.astype(BF16)

    row = pl.BlockSpec((tm, d), lambda i: (i, 0))
    vec = pl.BlockSpec((1, d), lambda i: (0, 0))
    return pl.pallas_call(
        body, name="down_proj_loss", grid=(s // tm,),
        in_specs=[pl.BlockSpec((tm, k_dim), lambda i: (i, 0)), pl.BlockSpec((k_dim, d), lambda i: (0, 0)), row, row,
                  vec],
        out_specs=[row, row, pl.BlockSpec((1, LANES), lambda i: (0, 0)), vec],
        out_shape=[jax.ShapeDtypeStruct((s, d), F32), jax.ShapeDtypeStruct((s, d), BF16),
                   jax.ShapeDtypeStruct((1, LANES), F32), jax.ShapeDtypeStruct((1, d), F32)],
        compiler_params=_cp(("arbitrary",)))(act, w_down, x2, target, g)


def _prev_halo_spec(tm, width, col):
    return pl.BlockSpec((HALO, width), lambda i, *_: (jnp.maximum(i * (tm // HALO) - 1, 0), col))


def _next_halo_spec(tm, width, col, s):
    return pl.BlockSpec((HALO, width), lambda i, *_: (jnp.minimum((i + 1) * (tm // HALO), s // HALO - 1), col))


def _shift_down(x, k):
    return pltpu.roll(x, k, 0)


def _shift_up(x, k):
    return pltpu.roll(x, x.shape[0] - k, 0)


def _conv_taps(x_ext, w):
    return w[0:1, :] * _shift_down(x_ext, 2) + w[1:2, :] * _shift_down(x_ext, 1) + w[2:3, :] * x_ext


def _conv_taps_t(d_ext, w):
    return w[2:3, :] * d_ext + w[1:2, :] * _shift_up(d_ext, 1) + w[0:1, :] * _shift_up(d_ext, 2)


def _mixer_fwd(z_a, o_attn, w_conv, g_conv_out, g_attn_out):
    s = z_a.shape[0]
    c = CONV_CH
    tm = min(TM_ROWS, s)

    def body(gb_ref, gc_ref, xc_ref, gcp_ref, xcp_ref, o_ref, w_ref, gco_ref, gao_ref, mix_ref):
        i = pl.program_id(0)
        cx = gc_ref[...] * xc_ref[...]
        cx_prev = jnp.where(i > 0, gcp_ref[...] * xcp_ref[...], 0.0)
        conv = _conv_taps(jnp.concatenate([cx_prev, cx], axis=0), w_ref[...])[HALO:]
        y = gb_ref[...] * conv
        mix_ref[:, 0:c] = (y * _rstd(y) * gco_ref[...]).astype(BF16)
        o = o_ref[...]
        mix_ref[:, c:2 * c] = (o * _rstd(o) * gao_ref[...]).astype(BF16)

    col = lambda j: pl.BlockSpec((tm, c), lambda i: (i, j))
    vec = pl.BlockSpec((1, c), lambda i: (0, 0))
    return pl.pallas_call(
        body, name="mixer_fwd", grid=(s // tm,),
        in_specs=[col(0), col(1), col(2), _prev_halo_spec(tm, c, 1), _prev_halo_spec(tm, c, 2), col(0),
                  pl.BlockSpec((3, c), lambda i: (0, 0)), vec, vec],
        out_specs=pl.BlockSpec((tm, 2 * c), lambda i: (i, 0)),
        out_shape=jax.ShapeDtypeStruct((s, 2 * c), BF16),
        compiler_params=_cp(("parallel",)))(z_a, z_a, z_a, z_a, z_a, o_attn, w_conv, g_conv_out, g_attn_out)


def _mixer_bwd(z_a, o_attn, dmix, w_conv, g_conv_out, g_attn_out, scatter):
    s = z_a.shape[0]
    c = CONV_CH
    tm = min(TM_ROWS, s)
    n_blk = s // tm
    n_sc = len(scatter)
    got_shapes, sems = _scatter_operands(scatter)

    def body(gb_ref, gc_ref, xc_ref, gcp_ref, xcp_ref, gbn_ref, gcn_ref, xcn_ref, o_ref, dnc_ref, dncn_ref, dna_ref,
             w_ref, gco_ref, gao_ref, *rest):
        dz_ref, dox_ref, gw_ref, ggco_ref, ggao_ref = rest[n_sc:n_sc + 5]
        i = pl.program_id(0)
        finish = _hosted_scatter(rest[:n_sc], rest[n_sc + 5:2 * n_sc + 5], rest[2 * n_sc + 5:], i, n_blk)

        @pl.when(i == 0)
        def _():
            gw_ref[...] = jnp.zeros_like(gw_ref)
            ggco_ref[...] = jnp.zeros_like(ggco_ref)
            ggao_ref[...] = jnp.zeros_like(ggao_ref)

        w = w_ref[...]
        zeros = jnp.zeros((HALO, c), F32)
        gb_e = jnp.concatenate([zeros, gb_ref[...], gbn_ref[...]], axis=0)
        cx_prev = jnp.where(i > 0, gcp_ref[...] * xcp_ref[...], 0.0)
        gc_e = jnp.concatenate([zeros, gc_ref[...], gcn_ref[...]], axis=0)
        xc_e = jnp.concatenate([zeros, xc_ref[...], xcn_ref[...]], axis=0)
        cx_e = jnp.concatenate([cx_prev, gc_ref[...] * xc_ref[...], gcn_ref[...] * xcn_ref[...]], axis=0)
        dn_next = jnp.where(i < n_blk - 1, dncn_ref[...], 0.0)
        dn_e = jnp.concatenate([zeros, dnc_ref[...], dn_next], axis=0)

        cx_1, cx_2 = _shift_down(cx_e, 1), _shift_down(cx_e, 2)
        conv_e = w[0:1, :] * cx_2 + w[1:2, :] * cx_1 + w[2:3, :] * cx_e
        y_e = gb_e * conv_e
        r_e = _rstd(y_e)
        yn_e = y_e * r_e
        t_e = dn_e * gco_ref[...]
        dy_e = r_e * (t_e - yn_e * jnp.mean(t_e * yn_e, axis=-1, keepdims=True))
        dconv_e = dy_e * gb_e
        dcx_e = _conv_taps_t(dconv_e, w)
        blk = slice(HALO, HALO + tm)
        dz_ref[:, 0:c] = (dy_e * conv_e)[blk].astype(BF16)
        dz_ref[:, c:2 * c] = (dcx_e * xc_e)[blk].astype(BF16)
        dz_ref[:, 2 * c:3 * c] = (dcx_e * gc_e)[blk].astype(BF16)
        ggco_ref[...] += jnp.sum((dn_e * yn_e)[blk], axis=0, keepdims=True)
        dconv = dconv_e[blk]
        gw_ref[0:1, :] += jnp.sum(dconv * cx_2[blk], axis=0, keepdims=True)
        gw_ref[1:2, :] += jnp.sum(dconv * cx_1[blk], axis=0, keepdims=True)
        gw_ref[2:3, :] += jnp.sum(dconv * cx_e[blk], axis=0, keepdims=True)

        o = o_ref[...]
        ra = _rstd(o)
        on = o * ra
        dna = dna_ref[...]
        ggao_ref[...] += jnp.sum(dna * on, axis=0, keepdims=True)
        ta = dna * gao_ref[...]
        do = ra * (ta - on * jnp.mean(ta * on, axis=-1, keepdims=True))
        prod = do * o
        lane = lax.broadcasted_iota(jnp.int32, (tm, LANES), 1)
        head_a = lane < HEAD_DIM
        for p in range(N_PAIRS):
            cols = slice(p * LANES, (p + 1) * LANES)
            pb, dob = prod[:, cols], do[:, cols]
            for hh in range(2):
                sel = head_a if hh == 0 else jnp.logical_not(head_a)
                delta = jnp.sum(jnp.where(sel, pb, 0.0), axis=-1, keepdims=True)
                neg3 = _split3(-delta)
                do_h = pltpu.roll(dob, HEAD_DIM, 1) if hh else dob
                dox_ref[2 * p + hh] = _aug(do_h, lane, neg3).astype(BF16)
        finish()

    col = lambda j: pl.BlockSpec((tm, c), lambda i: (i, j))
    vec = pl.BlockSpec((1, c), lambda i: (0, 0))
    w3 = pl.BlockSpec((3, c), lambda i: (0, 0))
    outs = pl.pallas_call(
        body, name="mixer_bwd", grid=(n_blk,),
        in_specs=[col(0), col(1), col(2), _prev_halo_spec(tm, c, 1), _prev_halo_spec(tm, c, 2),
                  _next_halo_spec(tm, c, 0, s), _next_halo_spec(tm, c, 1, s), _next_halo_spec(tm, c, 2, s),
                  col(0), col(0), _next_halo_spec(tm, c, 0, s), col(1), w3, vec, vec] + _hbm_specs(n_sc),
        out_specs=[pl.BlockSpec((tm, 3 * c), lambda i: (i, 0)),
                   pl.BlockSpec((N_HEADS, tm, LANES), lambda i: (0, i, 0)), w3, vec, vec] + _hbm_specs(n_sc),
        out_shape=[jax.ShapeDtypeStruct((s, 3 * c), BF16), jax.ShapeDtypeStruct((N_HEADS, s, LANES), BF16),
                   jax.ShapeDtypeStruct((3, c), F32), jax.ShapeDtypeStruct((1, c), F32),
                   jax.ShapeDtypeStruct((1, c), F32)] + got_shapes,
        scratch_shapes=sems, compiler_params=_cp(("arbitrary",)))(
            z_a, z_a, z_a, z_a, z_a, z_a, z_a, z_a, o_attn, dmix, dmix, dmix, w_conv, g_conv_out, g_attn_out,
            *scatter)
    return tuple(outs[:5]) + (outs[5:],)


def _gate_fwd(f, b_pad):
    s = f.shape[0]
    tm = min(TQ, s)

    def body(f_ref, b_ref, fb_ref, carry):
        @pl.when(pl.program_id(0) == 0)
        def _():
            carry[...] = jnp.zeros_like(carry)

        z = f_ref[...] + b_ref[...]
        x = jnp.minimum(z, 0.0) - jnp.log1p(jnp.exp(-jnp.abs(z)))
        row = lax.broadcasted_iota(jnp.int32, (tm, LANES), 0)
        sh = 1
        while sh < tm:
            x = x + jnp.where(row >= sh, _shift_down(x, sh), 0.0)
            sh *= 2
        x = x + carry[0:1, :]
        carry[...] = jnp.broadcast_to(x[tm - 1:tm, :], carry.shape)
        head_a = lax.broadcasted_iota(jnp.int32, (tm, LANES), 1) < HEAD_DIM
        for p in range(N_PAIRS):
            fa = jnp.broadcast_to(x[:, 2 * p:2 * p + 1], (tm, LANES))
            fbv = jnp.broadcast_to(x[:, 2 * p + 1:2 * p + 2], (tm, LANES))
            fb_ref[:, p * LANES:(p + 1) * LANES] = jnp.where(head_a, fa, fbv)

    return pl.pallas_call(
        body, name="gate_fwd", grid=(s // tm,),
        in_specs=[pl.BlockSpec((tm, LANES), lambda i: (i, 0)), pl.BlockSpec((1, LANES), lambda i: (0, 0))],
        out_specs=pl.BlockSpec((tm, N_PAIRS * LANES), lambda i: (i, 0)),
        out_shape=jax.ShapeDtypeStruct((s, N_PAIRS * LANES), F32),
        scratch_shapes=[pltpu.VMEM((HALO, LANES), F32)],
        compiler_params=_cp(("arbitrary",)))(f, b_pad)


def _gate_bwd(f, b_pad, d_f):
    s = f.shape[0]
    tm = min(TQ, s)
    n_blk = s // tm

    def body(f_ref, b_ref, d_ref, df_ref, gb_ref, carry):
        @pl.when(pl.program_id(0) == 0)
        def _():
            carry[...] = jnp.zeros_like(carry)
            gb_ref[...] = jnp.zeros_like(gb_ref)

        x = d_ref[...]
        row = lax.broadcasted_iota(jnp.int32, (tm, LANES), 0)
        sh = 1
        while sh < tm:
            x = x + jnp.where(row < tm - sh, _shift_up(x, sh), 0.0)
            sh *= 2
        x = x + carry[0:1, :]
        carry[...] = jnp.broadcast_to(x[0:1, :], carry.shape)
        z = f_ref[...] + b_ref[...]
        d = x * (1.0 / (1.0 + jnp.exp(z)))
        df_ref[...] = d.astype(BF16)
        gb_ref[...] += jnp.sum(d, axis=0, keepdims=True)

    rev = pl.BlockSpec((tm, LANES), lambda i: (n_blk - 1 - i, 0))
    vec = pl.BlockSpec((1, LANES), lambda i: (0, 0))
    return pl.pallas_call(
        body, name="gate_bwd", grid=(n_blk,), in_specs=[rev, vec, rev], out_specs=[rev, vec],
        out_shape=[jax.ShapeDtypeStruct((s, LANES), BF16), jax.ShapeDtypeStruct((1, LANES), F32)],
        scratch_shapes=[pltpu.VMEM((HALO, LANES), F32)],
        compiler_params=_cp(("arbitrary",)))(f, b_pad, d_f)


_NT = (((1,), (1,)), ((), ()))
_NN = (((1,), (0,)), ((), ()))


AUG = HEAD_DIM
NORM_MARGIN = 1.01


def _split3(x):
    hi = x.astype(BF16).astype(F32)
    r = x - hi
    mid = r.astype(BF16).astype(F32)
    lo = (r - mid).astype(BF16).astype(F32)
    return hi, mid, lo


def _aug(base, lane, vals):
    out = jnp.where(lane < AUG, base, 0.0)
    for k, v in enumerate(vals):
        out = jnp.where(lane == AUG + k, v, out)
    return out


def _attn_prep(qkv, fb, bigs):
    s = qkv.shape[0]
    tq = min(TQ, s)
    n_q = s // tq

    n = len(bigs)
    arrays, landing, sems = _gather_operands(bigs, [])

    def body(q_ref, k_ref, v_ref, fb_ref, *rest):
        qx_ref, kx_ref, kxt_ref, vx_ref, vt_ref, b_ref = rest[2 * n:2 * n + 6]
        finish = _hosted_gather((rest[:n], rest[2 * n + 6:3 * n + 6]) + tuple(rest[3 * n + 6:]), n, n,
                                pl.program_id(0), n_q)
        lane = lax.broadcasted_iota(jnp.int32, (tq, LANES), 1)
        lane8 = lax.broadcasted_iota(jnp.int32, (HALO, LANES), 1)
        head_lanes = lane < AUG
        is_lane = [lane == AUG + k for k in range(6)]
        first3 = (lane >= AUG) & (lane < AUG + 3)
        next3 = (lane >= AUG + 3) & (lane < AUG + 6)
        q_const = jnp.where(first3, -1.0, 0.0)
        k_const = jnp.where(next3, 1.0, 0.0)
        v_const = jnp.where(first3, 1.0, 0.0)
        ones_head = (lax.broadcasted_iota(jnp.int32, (LANES, LANES), 0) < HEAD_DIM).astype(BF16)
        acc = jnp.zeros((HALO, LANES), F32)
        for p in range(N_PAIRS):
            cols = slice(p * LANES, (p + 1) * LANES)
            q2, k2, v2 = (ref[:, cols].astype(F32) for ref in (q_ref, k_ref, v_ref))
            q2 = q2 * Q_SCALE
            f2 = fb_ref[:, cols]
            for hh in range(2):
                h = 2 * p + hh
                q, k, v = ((pltpu.roll(x, HEAD_DIM, 1) if hh else x) for x in (q2, k2, v2))
                f = f2 if hh else pltpu.roll(f2, HEAD_DIM, 1)
                hi, mid, lo = _split3(f)
                q_aug = jnp.where(is_lane[3], hi, jnp.where(is_lane[4], mid, jnp.where(is_lane[5], lo, q_const)))
                k_aug = jnp.where(is_lane[0], hi, jnp.where(is_lane[1], mid, jnp.where(is_lane[2], lo, k_const)))
                kx = jnp.where(head_lanes, k, k_aug)
                vx = jnp.where(head_lanes, v, v_const)
                qx_ref[h] = jnp.where(head_lanes, q, q_aug).astype(BF16)
                kx_ref[h] = kx.astype(BF16)
                vx_ref[h] = vx.astype(BF16)
                kxt_ref[h, 0] = kx.T.astype(BF16)
                vt_ref[h, 0] = vx.T.astype(BF16)
                q_sq = lax.dot_general((q * q).astype(BF16), ones_head, _NN, preferred_element_type=F32)
                k_sq = lax.dot_general((k * k).astype(BF16), ones_head, _NN, preferred_element_type=F32)
                diag = lax.dot_general((q * k).astype(BF16), ones_head, _NN, preferred_element_type=F32)
                diag = diag - jnp.sqrt(q_sq * k_sq) * (NORM_MARGIN - 1.0)
                vals = (jnp.sqrt(jnp.max(q_sq, axis=0, keepdims=True)), jnp.sqrt(jnp.max(k_sq, axis=0, keepdims=True)),
                        jnp.max(f - diag, axis=0, keepdims=True), f[tq - 1:tq, :])
                for slot, val in enumerate(vals):
                    acc = jnp.where(lane8 == slot * N_HEADS + h, val[:, AUG:AUG + 1], acc)
        b_ref[0] = acc
        finish()

    blk = lambda j: pl.BlockSpec((tq, ATTN_W), lambda i: (i, j))
    rows = pl.BlockSpec((N_HEADS, tq, LANES), lambda i: (0, i, 0))
    cols_t = pl.BlockSpec((N_HEADS, 1, LANES, tq), lambda i: (0, i, 0, 0))
    shp = jax.ShapeDtypeStruct((N_HEADS, s, LANES), BF16)
    shp_t = jax.ShapeDtypeStruct((N_HEADS, n_q, LANES, tq), BF16)
    outs = pl.pallas_call(
        body, name="attn_prep", grid=(n_q,), in_specs=[blk(0), blk(1), blk(2), blk(0)] + _hbm_specs(2 * n),
        out_specs=[rows, rows, cols_t, rows, cols_t,
                   pl.BlockSpec((1, HALO, LANES), lambda i: (i, 0, 0))] + _hbm_specs(n),
        out_shape=[shp, shp, shp_t, shp, shp_t, jax.ShapeDtypeStruct((n_q, HALO, LANES), F32)]
        + [jax.ShapeDtypeStruct(b.shape, b.dtype) for b in landing],
        input_output_aliases={4 + n + k: 6 + k for k in range(n)}, scratch_shapes=sems,
        compiler_params=_cp(("arbitrary",)))(qkv, qkv, qkv, fb, *arrays, *landing)
    return tuple(outs[:6]) + (outs[6:],)


def _key_block_ranges(bounds):
    t = bounds[:, 0, :]
    nh = N_HEADS
    a, b, c, e = t[:, 0:nh], t[:, nh:2 * nh], t[:, 2 * nh:3 * nh], t[:, 3 * nh:4 * nh]
    bound = a[:, None, :] * b[None, :, :] * NORM_MARGIN + c[:, None, :] - e[None, :, :]
    n_q = t.shape[0]
    idx = jnp.arange(n_q)
    need = jnp.logical_not(bound < -(EXP_ZERO + 2.0)) | (idx[None, :, None] >= idx[:, None, None])
    first = jnp.argmax(need, axis=1).astype(jnp.int32)
    first = jnp.min(first.reshape(n_q, N_PAIRS, 2), axis=-1)
    visits = (first[:, None, :] <= idx[None, :, None]) & (idx[None, :, None] <= idx[:, None, None])
    last = jnp.max(jnp.where(visits, idx[:, None, None], 0), axis=0).astype(jnp.int32)
    return first.T.reshape(-1), last.T.reshape(-1)


def _attn_fwd_t(qx, kx, vt, first_blk):
    _, s, _ = qx.shape
    tq = min(TQ, s)
    n_q = s // tq
    neg = -1e30

    def body(first_ref, qx_ref, kx_ref, vt_ref, o_ref, lse_ref, acc_ref, m_ref, s_even, s_odd):
        p = pl.program_id(0)
        i = pl.program_id(1)
        acc_ref[...] = jnp.zeros(acc_ref.shape, F32)
        m_ref[...] = jnp.full(m_ref.shape, neg, F32)
        key_le_query = (lax.broadcasted_iota(jnp.int32, (tq, tq), 0) <= lax.broadcasted_iota(jnp.int32, (tq, tq), 1))
        first = first_ref[p * n_q + i]

        def scores(kb, hh, dst):
            rows_k = pl.ds(pl.multiple_of(kb * tq, tq), tq)
            dst[hh] = lax.dot_general(kx_ref[hh, rows_k, :], qx_ref[hh], _NT, preferred_element_type=F32)

        def step(kb, src, nxt):
            for hh in range(2):
                st = src[hh]
                if nxt is None:
                    st = jnp.where(key_le_query, st, -jnp.inf)
                else:
                    scores(kb + 1, hh, nxt)
                m_old = m_ref[hh]
                m_new = jnp.maximum(m_old, jnp.max(st, axis=0, keepdims=True))
                m_ref[hh] = m_new
                pt = jnp.exp(st - m_new).astype(BF16)
                acc_ref[hh] = acc_ref[hh] * jnp.exp(m_old - m_new) + lax.dot_general(
                    vt_ref[hh, kb], pt, _NN, preferred_element_type=F32)

        def by_parity(kb, fn):
            @pl.when(kb % 2 == 0)
            def _():
                fn(s_even, s_odd)

            @pl.when(kb % 2 == 1)
            def _():
                fn(s_odd, s_even)

        def first_scores(src, nxt):
            scores(first, 0, src)
            scores(first, 1, src)

        def unmasked(kb, carry):
            by_parity(kb, lambda src, nxt: step(kb, src, nxt))
            return carry

        by_parity(first, first_scores)
        lax.fori_loop(first, i, unmasked, 0)
        by_parity(i, lambda src, nxt: step(i, src, None))
        outs, lses = [], []
        for hh in range(2):
            acc = acc_ref[hh]
            l = acc[AUG:AUG + 1, :]
            outs.append(acc[0:HEAD_DIM, :] / l)
            lses.append(m_ref[hh] + jnp.log(l))
        o_ref[...] = jnp.concatenate(outs, axis=0).T
        rows8 = lax.broadcasted_iota(jnp.int32, (N_HEADS, tq), 0)
        lse_ref[0, 0] = jnp.where(rows8 == 0, lses[0], jnp.where(rows8 == 1, lses[1], 0.0))

    grid_spec = pltpu.PrefetchScalarGridSpec(
        num_scalar_prefetch=1, grid=(N_PAIRS, n_q),
        in_specs=[pl.BlockSpec((2, tq, LANES), lambda p, i, first: (p, i, 0)),
                  pl.BlockSpec((2, s, LANES), lambda p, i, first: (p, 0, 0)),
                  pl.BlockSpec((2, n_q, LANES, tq), lambda p, i, first: (p, 0, 0, 0))],
        out_specs=[pl.BlockSpec((tq, LANES), lambda p, i, first: (i, p)),
                   pl.BlockSpec((1, 1, N_HEADS, tq), lambda p, i, first: (p, i, 0, 0))],
        scratch_shapes=[pltpu.VMEM((2, LANES, tq), F32), pltpu.VMEM((2, 1, tq), F32),
                        pltpu.VMEM((2, tq, tq), F32), pltpu.VMEM((2, tq, tq), F32)])
    return pl.pallas_call(
        body, name="attn_fwd", grid_spec=grid_spec,
        out_shape=[jax.ShapeDtypeStruct((s, ATTN_W), F32), jax.ShapeDtypeStruct((N_PAIRS, n_q, N_HEADS, tq), F32)],
        compiler_params=_cp(("parallel", "arbitrary")))(first_blk, qx, kx, vt)


def _attn_bwd_t(qx, dox, kx, kxt, vx, lse, last_blk):
    _, s, _ = qx.shape
    tq = min(TQ, s)
    n_q = s // tq

    def body(last_ref, qx_ref, dox_ref, lse_ref, kx_ref, kxt_ref, vx_ref, dk_ref, dv_ref, dfk_ref, dqt_ref, dfq_ref):
        p = pl.program_id(0)
        j = pl.program_id(1)

        @pl.when(j == 0)
        def _():
            dqt_ref[...] = jnp.zeros(dqt_ref.shape, F32)
            dfq_ref[...] = jnp.zeros(dfq_ref.shape, F32)

        key_le_query = (lax.broadcasted_iota(jnp.int32, (tq, tq), 0) <= lax.broadcasted_iota(jnp.int32, (tq, tq), 1))

        def step(i, carry, masked):
            rows_q = pl.ds(pl.multiple_of(i * tq, tq), tq)
            out = []
            for hh in range(2):
                dk, dv, col = carry[3 * hh:3 * hh + 3]
                q, do = qx_ref[hh, rows_q, :], dox_ref[hh, rows_q, :]
                st = lax.dot_general(kx_ref[hh], q, _NT, preferred_element_type=F32)
                pt = jnp.exp(st - lse_ref[0, i, hh:hh + 1, :])
                if masked:
                    pt = jnp.where(key_le_query, pt, 0.0)
                dst = pt * lax.dot_general(vx_ref[hh], do, _NT, preferred_element_type=F32)
                pb, dsb = pt.astype(BF16), dst.astype(BF16)
                dv = dv + lax.dot_general(pb, do, _NN, preferred_element_type=F32)
                dk = dk + lax.dot_general(dsb, q, _NN, preferred_element_type=F32)
                dqt_ref[hh, i] += lax.dot_general(kxt_ref[hh, 0], dsb, _NN, preferred_element_type=F32)
                for cb in range(tq // LANES):
                    col = col + dst[:, cb * LANES:(cb + 1) * LANES]
                dfq_ref[hh, i] += jnp.sum(dst.reshape(tq // HALO, HALO, tq), axis=0)
                out += [dk, dv, col]
            return tuple(out)

        zero = jnp.zeros((tq, LANES), F32)
        carry = step(j, (zero,) * 6, True)
        dk_a, dv_a, col_a, dk_b, dv_b, col_b = lax.fori_loop(j + 1, last_ref[p * n_q + j] + 1,
                                                             lambda i, cr: step(i, cr, False), carry)
        head_a = lax.broadcasted_iota(jnp.int32, (tq, LANES), 1) < HEAD_DIM
        dk_ref[...] = jnp.where(head_a, dk_a, pltpu.roll(dk_b, HEAD_DIM, 1)).astype(BF16)
        dv_ref[...] = jnp.where(head_a, dv_a, pltpu.roll(dv_b, HEAD_DIM, 1)).astype(BF16)
        rows8 = lax.broadcasted_iota(jnp.int32, (N_HEADS, tq), 0)
        dfk_a, dfk_b = (-jnp.sum(c.T, axis=0, keepdims=True) for c in (col_a, col_b))
        dfk_ref[0, 0] = jnp.where(rows8 == 0, dfk_a, jnp.where(rows8 == 1, dfk_b, 0.0))

    resident = pl.BlockSpec((2, s, LANES), lambda p, j, last: (p, 0, 0))
    key_rows = pl.BlockSpec((2, tq, LANES), lambda p, j, last: (p, j, 0))
    pair_out = pl.BlockSpec((tq, LANES), lambda p, j, last: (j, p))
    grid_spec = pltpu.PrefetchScalarGridSpec(
        num_scalar_prefetch=1, grid=(N_PAIRS, n_q),
        in_specs=[resident, resident, pl.BlockSpec((1, n_q, N_HEADS, tq), lambda p, j, last: (p, 0, 0, 0)),
                  key_rows, pl.BlockSpec((2, 1, LANES, tq), lambda p, j, last: (p, j, 0, 0)), key_rows],
        out_specs=[pair_out, pair_out, pl.BlockSpec((1, 1, N_HEADS, tq), lambda p, j, last: (p, j, 0, 0)),
                   pl.BlockSpec((2, n_q, LANES, tq), lambda p, j, last: (p, 0, 0, 0)),
                   pl.BlockSpec((2, n_q, HALO, tq), lambda p, j, last: (p, 0, 0, 0))])
    return pl.pallas_call(
        body, name="attn_bwd", grid_spec=grid_spec,
        out_shape=[jax.ShapeDtypeStruct((s, ATTN_W), BF16), jax.ShapeDtypeStruct((s, ATTN_W), BF16),
                   jax.ShapeDtypeStruct((N_PAIRS, n_q, N_HEADS, tq), F32),
                   jax.ShapeDtypeStruct((N_HEADS, n_q, LANES, tq), F32),
                   jax.ShapeDtypeStruct((N_HEADS, n_q, HALO, tq), F32)],
        compiler_params=_cp(("parallel", "arbitrary")))(last_blk, qx, dox, lse, kx, kxt, vx)


def _attn_dq_finish(dqt):
    _, n_q, _, tq = dqt.shape
    per = 4 if n_q % 4 == 0 else 1

    def body(dqt_ref, dq_ref):
        for b in range(per):
            a, bb = dqt_ref[0, b], dqt_ref[1, b]
            dq_ref[b * tq:(b + 1) * tq, :] = (
                jnp.concatenate([a[0:HEAD_DIM], bb[0:HEAD_DIM]], axis=0).T * Q_SCALE).astype(BF16)

    return pl.pallas_call(
        body, name="attn_dq_finish", grid=(N_PAIRS, n_q // per),
        in_specs=[pl.BlockSpec((2, per, LANES, tq), lambda p, i: (p, i, 0, 0))],
        out_specs=pl.BlockSpec((per * tq, LANES), lambda p, i: (i, p)),
        out_shape=jax.ShapeDtypeStruct((n_q * tq, ATTN_W), BF16),
        compiler_params=_cp(("parallel", "parallel")))(dqt)


def _ffn_act_fwd(up, w_ffn):
    s = up.shape[0]
    tm, tn = min(TM_FFN, s), TN_FFN
    nb = D_FF // tn

    def body(a_ref, g_ref, ap_ref, gp_ref, wa_ref, wg_ref, act_ref, u_ref):
        i = pl.program_id(1)

        def conv(blk_ref, prev_ref, w_ref):
            prev = jnp.where(i > 0, prev_ref[...].astype(F32)[HALO:], 0.0)
            return _conv_taps(jnp.concatenate([prev, blk_ref[...].astype(F32)], axis=0), w_ref[...])[HALO:]

        u_a, u_g = conv(a_ref, ap_ref, wa_ref), conv(g_ref, gp_ref, wg_ref)
        u_ref[0], u_ref[1] = u_a.astype(BF16), u_g.astype(BF16)
        act_ref[...] = (u_g * jax.nn.sigmoid(u_g) * u_a).astype(BF16)

    blk = lambda off: pl.BlockSpec((tm, tn), lambda n, i: (i, off + n))
    prev = lambda off: pl.BlockSpec(
        (HALO_BF16, tn), lambda n, i: (jnp.maximum(i * (tm // HALO_BF16) - 1, 0), off + n))
    wsp = lambda off: pl.BlockSpec((3, tn), lambda n, i: (0, off + n))
    return pl.pallas_call(
        body, name="ffn_act_fwd", grid=(nb, s // tm),
        in_specs=[blk(0), blk(nb), prev(0), prev(nb), wsp(0), wsp(nb)],
        out_specs=[pl.BlockSpec((tm, tn), lambda n, i: (i, n)), pl.BlockSpec((2, tm, tn), lambda n, i: (0, i, n))],
        out_shape=[jax.ShapeDtypeStruct((s, D_FF), BF16), jax.ShapeDtypeStruct((2, s, D_FF), BF16)],
        compiler_params=_cp(("parallel", "parallel")))(up, up, up, up, w_ffn, w_ffn)


def _ffn_act_bwd(up, u, dact, w_ffn):
    s = up.shape[0]
    tm, tn = min(TM_FFN, s), TN_FFN
    nb = D_FF // tn
    n_blk = s // tm

    def body(u_ref, un_ref, a_ref, g_ref, d_ref, dn_ref, wa_ref, wg_ref, dup_ref, gwa_ref, gwg_ref):
        i = pl.program_id(1)

        @pl.when(i == 0)
        def _():
            gwa_ref[...] = jnp.zeros_like(gwa_ref)
            gwg_ref[...] = jnp.zeros_like(gwg_ref)

        ext = lambda rows, next_rows: jnp.concatenate([rows, next_rows], axis=0)
        u_a, u_g = (ext(u_ref[h].astype(F32), un_ref[h].astype(F32)[:HALO]) for h in range(2))
        d_e = ext(d_ref[...], jnp.where(i < n_blk - 1, dn_ref[...], 0.0))
        sig = jax.nn.sigmoid(u_g)
        du_a = d_e * (u_g * sig)
        du_g = d_e * u_a * (sig * (1.0 + u_g * (1.0 - sig)))
        halves = ((gwa_ref, wa_ref[...], a_ref[...].astype(F32), du_a),
                  (gwg_ref, wg_ref[...], g_ref[...].astype(F32), du_g))
        for half, (gw_ref, w, upv, du) in enumerate(halves):
            du0, du1, du2 = du[:tm], _shift_up(du, 1)[:tm], _shift_up(du, 2)[:tm]
            dup_ref[half] = (w[2:3, :] * du0 + w[1:2, :] * du1 + w[0:1, :] * du2).astype(BF16)
            gw_ref[0:1, :] += jnp.sum(upv * du2, axis=0, keepdims=True)
            gw_ref[1:2, :] += jnp.sum(upv * du1, axis=0, keepdims=True)
            gw_ref[2:3, :] += jnp.sum(upv * du0, axis=0, keepdims=True)

    next_row = lambda halo: lambda i: jnp.minimum((i + 1) * (tm // halo), s // halo - 1)
    blk = lambda off: pl.BlockSpec((tm, tn), lambda n, i: (i, off + n))
    wsp = lambda off: pl.BlockSpec((3, tn), lambda n, i: (0, off + n))
    pair = lambda rows, row_of: pl.BlockSpec((2, rows, tn), lambda n, i: (0, row_of(i), n))
    return pl.pallas_call(
        body, name="ffn_act_bwd", grid=(nb, n_blk),
        in_specs=[pair(tm, lambda i: i), pair(HALO_BF16, next_row(HALO_BF16)), blk(0), blk(nb), blk(0),
                  pl.BlockSpec((HALO, tn), lambda n, i: (next_row(HALO)(i), n)), wsp(0), wsp(nb)],
        out_specs=[pair(tm, lambda i: i), wsp(0), wsp(0)],
        out_shape=[jax.ShapeDtypeStruct((2, s, D_FF), BF16),
                   jax.ShapeDtypeStruct((3, D_FF), F32), jax.ShapeDtypeStruct((3, D_FF), F32)],
        compiler_params=_cp(("parallel", "arbitrary")))(u, u, up, up, dact, dact, w_ffn, w_ffn)


def _adamw(w, g, m, v, name):
    r, c = g.shape
    tr = next((t for t in (512, 352, 256, 128, 64, 32, 16, 8) if r > t and r % t == 0), r)

    def body(w_ref, g_ref, m_ref, v_ref, d_ref, nm_ref, nv_ref):
        gv = g_ref[...]
        m_new = ADAM_B1 * m_ref[...] + (1.0 - ADAM_B1) * gv
        v_new = ADAM_B2 * v_ref[...] + (1.0 - ADAM_B2) * (gv * gv)
        m_hat = m_new / (1.0 - ADAM_B1 ** ADAM_STEP)
        v_hat = v_new / (1.0 - ADAM_B2 ** ADAM_STEP)
        d_ref[...] = -ADAM_LR * (m_hat / (jnp.sqrt(v_hat) + ADAM_EPS) + ADAM_WD * w_ref[...])
        nm_ref[...] = m_new
        nv_ref[...] = v_new

    spec = pl.BlockSpec((tr, c), lambda i: (i, 0))
    w_spec = spec if w.ndim == 2 else pl.BlockSpec((None, tr, c), lambda i: (0, i, 0))
    shp = jax.ShapeDtypeStruct(w.shape, F32)
    return pl.pallas_call(
        body, name=name, grid=(r // tr,), in_specs=[w_spec, spec, w_spec, w_spec], out_specs=[w_spec] * 3,
        out_shape=[shp] * 3, compiler_params=_cp(("parallel",)))(w, g, m, v)


def _sum_rows_block(h):
    return h if h <= 352 else 256


def _pair_sum(view, recv, sel, name):
    n, _, h, c = view.shape
    tr = _sum_rows_block(h)

    def body(sel_ref, a_ref, b_ref, o_ref, ob_ref):
        t = a_ref[...] + b_ref[...]
        o_ref[...] = t
        ob_ref[...] = t.astype(BF16)

    blk = pl.BlockSpec((None, tr, c), lambda j, i, sel_ref: (j, i, 0))
    grid_spec = pltpu.PrefetchScalarGridSpec(
        num_scalar_prefetch=1, grid=(n, h // tr),
        in_specs=[pl.BlockSpec((None, None, tr, c), lambda j, i, sel_ref: (j, sel_ref[0], i, 0)),
                  pl.BlockSpec((None, None, tr, c), lambda j, i, sel_ref: (j, 0, i, 0))],
        out_specs=[blk, blk])
    return pl.pallas_call(
        body, name=name, grid_spec=grid_spec,
        out_shape=[jax.ShapeDtypeStruct((n, h, c), F32), jax.ShapeDtypeStruct((n, h, c), BF16)],
        compiler_params=_cp(("parallel", "parallel")))(sel, view, recv)


def _chip_sum(pair, got, sel, name):
    _, h, c = pair.shape
    tr = _sum_rows_block(h)
    nblk = h // tr

    def body(sel_ref, p_ref, g0_ref, g1_ref, g2_ref, o_ref):
        o_ref[...] = ((p_ref[...] + g0_ref[...].astype(F32)) + g1_ref[...].astype(F32)) + g2_ref[...].astype(F32)

    slot = lambda k: pl.BlockSpec((None, tr, c), lambda i, sel_ref: (k, i, 0))
    grid_spec = pltpu.PrefetchScalarGridSpec(
        num_scalar_prefetch=1, grid=(h // tr,),
        in_specs=[pl.BlockSpec((None, tr, c), lambda i, sel_ref: (sel_ref[1], i, 0)), slot(0), slot(1), slot(2)],
        out_specs=pl.BlockSpec((tr, c), lambda i, sel_ref: (sel_ref[0] * nblk + i, 0)))
    return pl.pallas_call(
        body, name=name, grid_spec=grid_spec, out_shape=jax.ShapeDtypeStruct((2 * h, c), F32),
        compiler_params=_cp(("parallel",)))(sel, pair, got, got, got)


def _place():
    return lax.axis_index("x"), lax.axis_index("y"), lax.axis_index("c")


def _other_chips(x, y):
    return [(1 - x, y), (x, 1 - y), (1 - x, 1 - y)]


def _hbm_specs(n):
    return [pl.BlockSpec(memory_space=pl.ANY)] * n


def _all_gather_weights(bigs, smalls):
    nb, ns = len(bigs), len(smalls)
    n = nb + ns

    def body(*refs):
        start, forward, finish = _gather_phases(refs[:n], refs[2 * n:3 * n], nb, *refs[3 * n:])
        start()
        forward()
        finish()

    arrays, landing, sems = _gather_operands(bigs, smalls)
    return pl.pallas_call(
        body, name="all_gather_weights",
        out_shape=[jax.ShapeDtypeStruct(b.shape, b.dtype) for b in landing],
        in_specs=_hbm_specs(2 * n), out_specs=_hbm_specs(n), input_output_aliases={n + k: k for k in range(n)},
        scratch_shapes=sems)(*arrays, *landing)


def _hosted_gather(refs, n, nb, step, total):
    ins, outs, send_sems, recv_sems = refs
    start, forward, finish = _gather_phases(ins, outs, nb, send_sems, recv_sems)
    pl.when(step == 0)(start)
    pl.when(step == (3 * total) // 4)(forward)
    return lambda: pl.when(step == total - 1)(finish)


def _in_proj(x, g, w_a, w_b, w_c, bigs, smalls):
    s, d = x.shape
    tm, tn = min(TM_MM, s), ATTN_W
    na, nq = w_a.shape[1] // tn, w_b.shape[1] // tn
    steps = na + nq + 1
    total = (s // tm) * steps
    nb, n = len(bigs), len(bigs) + len(smalls)
    arrays, landing, sems = _gather_operands(bigs, smalls)

    def body(x_ref, g_ref, wa_ref, wb_ref, wc_ref, *rest):
        z_ref, qkv_ref, f_ref, h_ref = rest[2 * n:2 * n + 4]
        h_scr = rest[-1]
        m, j = pl.program_id(0), pl.program_id(1)
        finish = _hosted_gather((rest[:n], rest[2 * n + 4:3 * n + 4]) + tuple(rest[3 * n + 4:3 * n + 6]), n, nb,
                                m * steps + j, total)

        @pl.when(j == 0)
        def _():
            xv = x_ref[...]
            hv = (xv * _rstd(xv) * g_ref[...]).astype(BF16)
            h_scr[...] = hv
            h_ref[...] = hv

        h = h_scr[...]

        @pl.when(j < na)
        def _():
            z_ref[...] = lax.dot_general(h, wa_ref[...], _NN, preferred_element_type=F32)

        @pl.when((j >= na) & (j < na + nq))
        def _():
            qkv_ref[...] = lax.dot_general(h, wb_ref[...], _NN, preferred_element_type=F32).astype(BF16)

        @pl.when(j == na + nq)
        def _():
            f_ref[...] = lax.dot_general(h, wc_ref[...], _NN, preferred_element_type=F32)

        finish()

    blk_a = lambda m, j: (m, jnp.minimum(j, na - 1))
    blk_b = lambda m, j: (m, jnp.clip(j - na, 0, nq - 1))
    outs = pl.pallas_call(
        body, name="in_proj", grid=(s // tm, steps),
        in_specs=[pl.BlockSpec((tm, d), lambda m, j: (m, 0)), pl.BlockSpec((1, d), lambda m, j: (0, 0)),
                  pl.BlockSpec((d, tn), lambda m, j: (0, jnp.minimum(j, na - 1))),
                  pl.BlockSpec((d, tn), lambda m, j: (0, jnp.clip(j - na, 0, nq - 1))),
                  pl.BlockSpec((d, LANES), lambda m, j: (0, 0))] + _hbm_specs(2 * n),
        out_specs=[pl.BlockSpec((tm, tn), blk_a), pl.BlockSpec((tm, tn), blk_b),
                   pl.BlockSpec((tm, LANES), lambda m, j: (m, 0)), pl.BlockSpec((tm, d), lambda m, j: (m, 0))]
        + _hbm_specs(n),
        out_shape=[jax.ShapeDtypeStruct((s, w_a.shape[1]), F32), jax.ShapeDtypeStruct((s, w_b.shape[1]), BF16),
                   jax.ShapeDtypeStruct((s, LANES), F32), jax.ShapeDtypeStruct((s, d), BF16)]
        + [jax.ShapeDtypeStruct(b.shape, b.dtype) for b in landing],
        input_output_aliases={5 + n + k: 4 + k for k in range(n)},
        scratch_shapes=sems + [pltpu.VMEM((tm, d), BF16)],
        compiler_params=_cp(("arbitrary", "arbitrary")))(x, g, w_a, w_b, w_c, *arrays, *landing)
    return outs[0], outs[1], outs[2], outs[3], outs[4:]


def _gather_operands(bigs, smalls):
    x, y, _ = _place()
    arrays = list(bigs) + list(smalls)
    landing = [lax.dynamic_update_index_in_dim(lax.empty((N_CHIPS,) + a.shape, a.dtype), a, 2 * x + y, 0)
               for a in arrays]
    n_sems = 6 * len(bigs) + 3 * len(smalls)
    return arrays, landing, [pltpu.SemaphoreType.DMA((n_sems,)), pltpu.SemaphoreType.DMA((n_sems,))]


def _gather_phases(ins, outs, nb, send_sems, recv_sems):
    n = len(ins)
    x, y, c = _place()
    my_chip = 2 * x + y
    chips = _other_chips(x, y)
    sibling = (x, y, 1 - c)

    def rows(k, which):
        h = ins[k].shape[0] // 2
        return pl.ds(which * h, h)

    def copy(sem, src, dst, to):
        return pltpu.make_async_remote_copy(src_ref=src, dst_ref=dst, send_sem=send_sems.at[sem],
                                            recv_sem=recv_sems.at[sem], device_id=to, device_id_type=MESH)

    def sends():
        out = [copy(6 * k + j, ins[k].at[rows(k, c)], outs[k].at[my_chip, rows(k, c)], (cx, cy, c))
               for k in range(nb) for j, (cx, cy) in enumerate(chips)]
        return out + [copy(6 * nb + 3 * (k - nb) + j, ins[k], outs[k].at[my_chip], (cx, cy, c))
                      for k in range(nb, n) for j, (cx, cy) in enumerate(chips)]

    def landed(k, j, which):
        cx, cy = chips[j]
        return outs[k].at[2 * cx + cy, rows(k, which)]

    def forwards():
        return [copy(6 * k + 3 + j, landed(k, j, c), landed(k, j, c), sibling)
                for j in range(3) for k in range(nb)]

    def start():
        for cp in sends():
            cp.start()

    def forward():
        for j in range(3):
            for k in range(nb):
                copy(6 * k + j, landed(k, j, c), landed(k, j, c), (x, y, c)).wait_recv()
                copy(6 * k + 3 + j, landed(k, j, c), landed(k, j, c), sibling).start()

    def finish():
        for j, (cx, cy) in enumerate(chips):
            for k in range(nb):
                copy(6 * k + 3 + j, landed(k, j, 1 - c), landed(k, j, 1 - c), (x, y, c)).wait_recv()
            for k in range(nb, n):
                arrived = outs[k].at[2 * cx + cy]
                copy(6 * nb + 3 * (k - nb) + j, arrived, arrived, (x, y, c)).wait_recv()
        for cp in sends() + forwards():
            cp.wait_send()

    return start, forward, finish


def _pair_exchange(views, name):
    n = len(views)

    def body(*refs):
        ins, outs, send_sems, recv_sems = refs[:n], refs[n:2 * n], refs[2 * n], refs[2 * n + 1]
        x, y, c = _place()
        copies = [pltpu.make_async_remote_copy(
            src_ref=ins[k].at[:, pl.ds(1 - c, 1)], dst_ref=outs[k], send_sem=send_sems.at[k],
            recv_sem=recv_sems.at[k], device_id=(x, y, 1 - c), device_id_type=MESH) for k in range(n)]
        for cp in copies:
            cp.start()
        for cp in copies:
            cp.wait()

    return pl.pallas_call(
        body, name=name,
        out_shape=[jax.ShapeDtypeStruct((v.shape[0], 1) + v.shape[2:], v.dtype) for v in views],
        in_specs=_hbm_specs(n), out_specs=_hbm_specs(n),
        scratch_shapes=[pltpu.SemaphoreType.DMA((n,)), pltpu.SemaphoreType.DMA((n,))])(*views)


def _scatter_operands(parts):
    n = len(parts)
    return ([jax.ShapeDtypeStruct((3,) + p.shape[1:], p.dtype) for p in parts],
            [pltpu.SemaphoreType.DMA((3 * n,)), pltpu.SemaphoreType.DMA((3 * n,))])


def _scatter_phases(ins, outs, send_sems, recv_sems):
    x, y, c = _place()

    def copies():
        return [pltpu.make_async_remote_copy(
            src_ref=ins[k].at[pl.ds(2 * cx + cy, 1)], dst_ref=outs[k].at[pl.ds(r, 1)], send_sem=send_sems.at[3 * k + r],
            recv_sem=recv_sems.at[3 * k + r], device_id=(cx, cy, c), device_id_type=MESH)
            for k in range(len(ins)) for r, (cx, cy) in enumerate(_other_chips(x, y))]

    def start():
        for cp in copies():
            cp.start()

    def finish():
        for cp in copies():
            cp.wait()

    return start, finish


def _hosted_scatter(ins, outs, sems, step, total):
    start, finish = _scatter_phases(ins, outs, *sems)
    pl.when(step == 0)(start)
    return lambda: pl.when(step == total - 1)(finish)


def _join_halves(shards):
    n = len(shards)

    def body(*refs):
        ins, outs, send_sems, recv_sems = refs[:n], refs[n:2 * n], refs[2 * n], refs[2 * n + 1]
        x, y, c = _place()

        def rows(ref, which):
            h = ref.shape[0] // 2
            return ref.at[pl.ds(which * h, h)]

        sent = [pltpu.make_async_remote_copy(
            src_ref=rows(ins[k], c), dst_ref=rows(outs[k], c), send_sem=send_sems.at[k], recv_sem=recv_sems.at[k],
            device_id=(x, y, 1 - c), device_id_type=MESH) for k in range(n)]
        for cp in sent:
            cp.start()
        for k in range(n):
            pltpu.make_async_remote_copy(
                src_ref=rows(ins[k], 1 - c), dst_ref=rows(outs[k], 1 - c), send_sem=send_sems.at[k],
                recv_sem=recv_sems.at[k], device_id=(x, y, 1 - c), device_id_type=MESH).wait_recv()
        for cp in sent:
            cp.wait_send()

    return pl.pallas_call(
        body, name="half_exchange", out_shape=[jax.ShapeDtypeStruct(a.shape, a.dtype) for a in shards],
        in_specs=_hbm_specs(n), out_specs=_hbm_specs(n), input_output_aliases={k: k for k in range(n)},
        scratch_shapes=[pltpu.SemaphoreType.DMA((n,)), pltpu.SemaphoreType.DMA((n,))])(*shards)


def _all_reduce_small(packet):
    rows, width = packet.shape
    n_dev = 8

    def body(x_ref, out_ref, gath, send_sems, recv_sems):
        x, y, c = _place()
        me, sibling = (x, y, c), (x, y, 1 - c)
        chips = _other_chips(x, y)

        def slot(px, py, pc):
            return gath.at[pl.ds((4 * px + 2 * py + pc) * rows, rows), :]

        def copy(k, block, to, src=None):
            return pltpu.make_async_remote_copy(
                src_ref=slot(*block) if src is None else src, dst_ref=slot(*block), send_sem=send_sems.at[k],
                recv_sem=recv_sems.at[k], device_id=to, device_id_type=MESH)

        first = [copy(0, me, sibling, src=x_ref)]
        first += [copy(1 + j, me, (*chip, c), src=x_ref) for j, chip in enumerate(chips)]
        for cp in first:
            cp.start()
        gath[pl.ds((4 * x + 2 * y + c) * rows, rows), :] = x_ref[...]
        passed = [copy(4 + j, (*chip, c), sibling) for j, chip in enumerate(chips)]
        for j, chip in enumerate(chips):
            copy(1 + j, (*chip, c), me).wait_recv()
            passed[j].start()
        copy(0, sibling, me).wait_recv()
        for j, chip in enumerate(chips):
            copy(4 + j, (*chip, 1 - c), me).wait_recv()
        for cp in first + passed:
            cp.wait_send()
        acc = gath[0:rows, :]
        for d in range(1, n_dev):
            acc = acc + gath[d * rows:(d + 1) * rows, :]
        out_ref[...] = acc

    return pl.pallas_call(
        body, name="all_reduce_small", out_shape=jax.ShapeDtypeStruct((rows, width), F32),
        in_specs=[pl.BlockSpec(memory_space=pltpu.VMEM)], out_specs=pl.BlockSpec(memory_space=pltpu.VMEM),
        scratch_shapes=[pltpu.VMEM((n_dev * rows, width), F32), pltpu.SemaphoreType.DMA((7,)),
                        pltpu.SemaphoreType.DMA((7,))])(packet)


def _flat_rows(parts, width, row_multiple):
    flat = jnp.concatenate([p.astype(F32).reshape(-1) for p in parts])
    rows = -(-flat.shape[0] // width)
    rows = -(-rows // row_multiple) * row_multiple
    return jnp.pad(flat, (0, rows * width - flat.shape[0])).reshape(rows, width)


def _unflatten(flat2d, shapes):
    flat = flat2d.reshape(-1)
    out, off = [], 0
    for shp in shapes:
        n = 1
        for dim in shp:
            n *= dim
        out.append(flat[off:off + n].reshape(shp))
        off += n
    return out


def _core_and_chip():
    x, y, c = _place()
    return jnp.stack([c, 2 * x + y]).astype(jnp.int32)


def _pair_sums(chip_major, names, call_name):
    views = [g.reshape(N_CHIPS, 2, g.shape[1] // 2, g.shape[2]) for g in chip_major]
    recv = _pair_exchange(views, call_name)
    sel = _core_and_chip()
    return [_pair_sum(v, r, sel, "pair_sum_" + nm) for v, r, nm in zip(views, recv, names)]


def _finish_grads(pairs, got, names):
    sel = _core_and_chip()
    return _join_halves([_chip_sum(p, g, sel, "chip_sum_" + nm) for (p, _), g, nm in zip(pairs, got, names)])


def kernel(x, g_mix, w_in, b_f, w_conv, g_conv_out, g_attn_out, w_o, g_ffn, w_up, w_ffn_conv, w_down, g_final, loss_target, m_g_mix, m_w_in, m_b_f, m_w_conv, m_g_conv_out, m_g_attn_out, m_w_o, m_g_ffn, m_w_up, m_w_ffn_conv, m_w_down, m_g_final, v_g_mix, v_w_in, v_b_f, v_w_conv, v_g_conv_out, v_g_attn_out, v_w_o, v_g_ffn, v_w_up, v_w_ffn_conv, v_w_down, v_g_final):
    s = x.shape[1]
    x0 = x[0]
    target = loss_target[0]
    d = D_MODEL
    x_pos, y_pos, _ = _place()
    my_chip = 2 * x_pos + y_pos

    (c_in,) = _all_gather_weights([w_in[0].astype(BF16)], [])
    w_in_full = jnp.concatenate([c_in[j] for j in range(N_CHIPS)], axis=1)
    c3 = 3 * CONV_CH
    w_a, w_b = w_in_full[:, :c3], w_in_full[:, c3:c3 + 3 * ATTN_W]
    w_c = jnp.pad(w_in_full[:, c3 + 3 * ATTN_W:], ((0, 0), (0, LANES - N_HEADS)))
    w_q, w_k, w_v = (w_b[:, i * ATTN_W:(i + 1) * ATTN_W] for i in range(3))
    b_pad = jnp.pad(b_f, ((0, 0), (0, LANES - N_HEADS)))

    z_a, qkv, f_log, h1, (c_o, c_up, c_conv, c_ffn) = _in_proj(
        x0, g_mix, w_a, w_b, w_c, [w_o[0].astype(BF16), w_up[0].astype(BF16)], [w_conv[0], w_ffn_conv[0]])
    fb = _gate_fwd(f_log, b_pad)
    qx, kx, kxt, vx, vt, bounds, (c_down,) = _attn_prep(qkv, fb, [w_down[0].astype(BF16)])
    w_o_full = c_o.reshape(d, d)
    w_down_full = c_down.reshape(D_FF, d)
    w_conv_full = jnp.concatenate([c_conv[j] for j in range(N_CHIPS)], axis=1)
    w_ffn_full = jnp.concatenate([c_ffn[j] for j in range(N_CHIPS)], axis=1)
    n_up = c_up.shape[2]
    first_blk, last_blk = _key_block_ranges(bounds)
    o_attn, lse = _attn_fwd_t(qx, kx, vt, first_blk)
    mix = _mixer_fwd(z_a, o_attn, w_conv_full, g_conv_out, g_attn_out)
    x2, h2 = _mm("nn", [mix], [w_o_full], F32, 512, d, "out_proj", add=x0, norm_g=g_ffn)
    up = _mm("nn", [h2], [c_up], BF16, TM_MM, n_up, "up_proj", b_chips=True)
    act, u_conv = _ffn_act_fwd(up, w_ffn_full)
    dx3, dx3_b, loss_row, gg_final = _down_proj_loss(act, w_down_full, x2, target, g_final.reshape(1, d))

    dact = _mm("nt", [dx3_b], [w_down_full], F32, TM_MM, 1408, "d_act")
    gw_down = _mm_tn(act, dx3_b, 1408, 1024, "gw_down")
    dup, gwf_lin, gwf_gate = _ffn_act_bwd(up, u_conv, dact, w_ffn_full)
    dh2 = _mm("nt", [(dup, j // 2, j % 2, n_up) for j in range(N_CHIPS)], [(c_up, j) for j in range(N_CHIPS)],
              F32, TM_MM, 512, "d_h2")
    gw_up = _mm_tn(h2, dup, 1024, n_up, "gw_up", out_chips=True)
    dx2, dx2_b, gg_ffn = _rms_bwd(x2, dh2, g_ffn, dx3, "rms_ffn_bwd", True)
    dmix = _mm("nt", [dx2_b], [w_o_full], F32, TM_MM, 512, "d_mix")
    gw_o = _mm_tn(mix, dx2_b, 1024, 1024, "gw_o")
    early = _pair_sums([gw_o.reshape(N_CHIPS, d // N_CHIPS, d), gw_up, gw_down.reshape(N_CHIPS, D_FF // N_CHIPS, d)],
                       ["w_o", "w_up", "w_down"], "pair_exchange")
    dz_a, dox, gw_conv, gg_conv_out, gg_attn_out, (got_o, got_down) = _mixer_bwd(
        z_a, o_attn, dmix, w_conv_full, g_conv_out, g_attn_out, [early[0][1], early[2][1]])
    dk, dv, dfk, dqt, dfq = _attn_bwd_t(qx, dox, kx, kxt, vx, lse, last_blk)
    dq = _attn_dq_finish(dqt)
    d_f = (jnp.transpose(dfk[:, :, 0:2, :], (1, 3, 0, 2)).reshape(s, N_HEADS)
           + jnp.transpose(jnp.sum(dfq, axis=2), (1, 2, 0)).reshape(s, N_HEADS))
    df_b, gb_f = _gate_bwd(f_log, b_pad, jnp.pad(d_f, ((0, 0), (0, LANES - N_HEADS))))
    gw_a = _mm_tn(h1, dz_a, 1024, c3, "gw_in_conv")
    gw_q = _mm_tn(h1, dq, 1024, ATTN_W, "gw_in_q")
    gw_k = _mm_tn(h1, dk, 1024, ATTN_W, "gw_in_k")
    gw_v = _mm_tn(h1, dv, 1024, ATTN_W, "gw_in_v")
    gw_c = _mm_tn(h1, df_b, 1024, LANES, "gw_in_gate")
    gw_in = jnp.concatenate([gw_a, gw_q, gw_k, gw_v, gw_c[:, :N_HEADS]], axis=1)
    n_in = IN_COLS // N_CHIPS
    gw_in = jnp.stack([gw_in[:, j * n_in:(j + 1) * n_in] for j in range(N_CHIPS)])
    late = _pair_sums([gw_in], ["w_in"], "pair_exchange_w_in")
    dh1, (got_up, got_in) = _mm("nt", [dz_a, dq, dk, dv, df_b], [w_a, w_q, w_k, w_v, w_c], F32, TM_MM, 512, "d_h1",
                                scatter=[early[1][1], late[0][1]])
    grad_x, gg_mix = _rms_bwd(x0, dh1, g_mix, dx2, "rms_mix_bwd", False)
    g_w_in, g_w_o, g_w_up, g_w_down = _finish_grads(late + early, [got_in, got_o, got_up, got_down],
                                                    ["w_in", "w_o", "w_up", "w_down"])

    gw_ffn = jnp.concatenate([gwf_lin, gwf_gate], axis=1)
    small_parts = [gg_mix, gg_conv_out, gg_attn_out, gg_ffn, gg_final, gb_f[:, :N_HEADS], loss_row[:, 0:1], gw_conv,
                   gw_ffn]
    small_shapes = [a.shape for a in small_parts]
    tot = _unflatten(_all_reduce_small(_flat_rows(small_parts, d, 8)), small_shapes)
    g_g_mix, g_g_conv_out, g_g_attn_out, g_g_ffn, g_g_final, g_b_f, loss_sum, g_conv_full, g_ffn_full = tot
    loss = loss_sum[0, 0]
    g_g_final = g_g_final[0]
    g_w_conv = lax.dynamic_slice_in_dim(g_conv_full, my_chip * (CONV_CH // N_CHIPS), CONV_CH // N_CHIPS, axis=1)
    g_w_ffn = lax.dynamic_slice_in_dim(g_ffn_full, my_chip * n_up, n_up, axis=1)

    u_w_in = _adamw(w_in, g_w_in, m_w_in, v_w_in, "adam_w_in")
    u_w_o = _adamw(w_o, g_w_o, m_w_o, v_w_o, "adam_w_o")
    u_w_up = _adamw(w_up, g_w_up, m_w_up, v_w_up, "adam_w_up")
    u_w_down = _adamw(w_down, g_w_down, m_w_down, v_w_down, "adam_w_down")

    small_w = [g_mix, b_f, g_conv_out, g_attn_out, g_ffn, g_final, w_conv, w_ffn_conv]
    small_g = [g_g_mix, g_b_f, g_g_conv_out, g_g_attn_out, g_g_ffn, g_g_final, g_w_conv, g_w_ffn]
    small_m = [m_g_mix, m_b_f, m_g_conv_out, m_g_attn_out, m_g_ffn, m_g_final, m_w_conv, m_w_ffn_conv]
    small_v = [v_g_mix, v_b_f, v_g_conv_out, v_g_attn_out, v_g_ffn, v_g_final, v_w_conv, v_w_ffn_conv]
    shapes = [a.shape for a in small_w]
    pack = lambda arrs: _flat_rows(arrs, LANES, 8)
    sd, sm, sv = _adamw(pack(small_w), pack(small_g), pack(small_m), pack(small_v), "adam_small")
    sd, sm, sv = _unflatten(sd, shapes), _unflatten(sm, shapes), _unflatten(sv, shapes)
    (d_g_mix, d_b_f, d_g_conv_out, d_g_attn_out, d_g_ffn, d_g_final, d_w_conv, d_w_ffn) = sd
    (nm_g_mix, nm_b_f, nm_g_conv_out, nm_g_attn_out, nm_g_ffn, nm_g_final, nm_w_conv, nm_w_ffn) = sm
    (nv_g_mix, nv_b_f, nv_g_conv_out, nv_g_attn_out, nv_g_ffn, nv_g_final, nv_w_conv, nv_w_ffn) = sv

    grads = (g_g_mix, g_w_in[None], g_b_f, g_w_conv[None], g_g_conv_out, g_g_attn_out, g_w_o[None], g_g_ffn,
             g_w_up[None], g_w_ffn[None], g_w_down[None], g_g_final)
    deltas = (d_g_mix, u_w_in[0], d_b_f, d_w_conv, d_g_conv_out, d_g_attn_out, u_w_o[0], d_g_ffn, u_w_up[0],
              d_w_ffn, u_w_down[0], d_g_final)
    new_m = (nm_g_mix, u_w_in[1], nm_b_f, nm_w_conv, nm_g_conv_out, nm_g_attn_out, u_w_o[1], nm_g_ffn, u_w_up[1],
             nm_w_ffn, u_w_down[1], nm_g_final)
    new_v = (nv_g_mix, u_w_in[2], nv_b_f, nv_w_conv, nv_g_conv_out, nv_g_attn_out, u_w_o[2], nv_g_ffn, u_w_up[2],
             nv_w_ffn, u_w_down[2], nv_g_final)
    return (loss, grad_x[None], *grads, *deltas, *new_m, *new_v)
```

```python
import jax
import jax.numpy as jnp
from jax import lax
from jax.experimental import pallas as pl
from jax.experimental.pallas import tpu as pltpu

F32, BF16 = jnp.float32, jnp.bfloat16
MESH = pl.DeviceIdType.MESH

D_MODEL = 1024
CONV_CH = 512
ATTN_W = 512
N_HEADS = 8
HEAD_DIM = 64
N_PAIRS = N_HEADS // 2
D_FF = 2816
IN_COLS = 3 * CONV_CH + 3 * ATTN_W + N_HEADS
EPS = 1e-6
Q_SCALE = 0.125
EXP_ZERO = 88.0
N_CHIPS = 4
LANES = 128
HALO = 8
HALO_BF16 = 2 * HALO

ADAM_LR, ADAM_B1, ADAM_B2, ADAM_EPS, ADAM_WD, ADAM_STEP = 0.001, 0.9, 0.999, 1e-08, 0.01, 10

TM_ROWS = 512
TM_MM = 1024
TK_TN = 1024
TQ = 512
TM_FFN = 256
TN_FFN = 1408
VMEM_LIMIT = 52 * 2**20


def _cp(sem, vmem=VMEM_LIMIT):
    return pltpu.CompilerParams(dimension_semantics=sem, vmem_limit_bytes=vmem)


def _bf(a):
    return a if a.dtype == BF16 else a.astype(BF16)


def _mm(mode, a_list, b_list, out_dtype, tm, tn, name, add=None, b_chips=False, scatter=(), norm_g=None):
    n_p = len(a_list)
    a0 = a_list[0]
    m_dim = a0[0].shape[1] if isinstance(a0, tuple) else a0.shape[0]
    b0 = b_list[0]
    if b_chips:
        n_dim = b0.shape[0] * b0.shape[2]
        assert tn == b0.shape[2] and mode == "nn"
    else:
        b0 = b0[0][b0[1]] if isinstance(b0, tuple) else b0
        n_dim = b0.shape[1 if mode == "nn" else 0]
    tm, tn = min(tm, m_dim), min(tn, n_dim)
    assert m_dim % tm == 0 and n_dim % tn == 0
    dims = (((1,), (0,)), ((), ())) if mode == "nn" else (((1,), (1,)), ((), ()))
    in_specs, args = [], []
    for a in a_list:
        if isinstance(a, tuple):
            arr, lead, col, width = a
            in_specs.append(pl.BlockSpec((None, tm, width), lambda m, n, lead=lead, col=col: (lead, m, col)))
        else:
            arr = a
            in_specs.append(pl.BlockSpec((tm, a.shape[1]), lambda m, n: (m, 0)))
        args.append(arr)
    for b in b_list:
        if b_chips:
            arr = b
            in_specs.append(pl.BlockSpec((None, b.shape[1], tn), lambda m, n: (n, 0, 0)))
        elif isinstance(b, tuple):
            arr, lead = b
            if mode == "nn":
                in_specs.append(pl.BlockSpec((None, arr.shape[1], tn), lambda m, n, lead=lead: (lead, 0, n)))
            else:
                in_specs.append(pl.BlockSpec((None, tn, arr.shape[2]), lambda m, n, lead=lead: (lead, n, 0)))
        elif mode == "nn":
            arr = b
            in_specs.append(pl.BlockSpec((b.shape[0], tn), lambda m, n: (0, n)))
        else:
            arr = b
            in_specs.append(pl.BlockSpec((tn, b.shape[1]), lambda m, n: (n, 0)))
        args.append(arr)
    if add is not None:
        in_specs.append(pl.BlockSpec((tm, tn), lambda m, n: (m, n)))
        args.append(add)
    if norm_g is not None:
        assert tn == n_dim and not scatter
        in_specs.append(pl.BlockSpec((1, tn), lambda m, n: (0, 0)))
        args.append(norm_g)

    n_in = len(args)
    n_sc = len(scatter)
    grid = (m_dim // tm, n_dim // tn)

    def body(*refs):
        o_ref = refs[n_in + n_sc]
        if n_sc:
            finish = _hosted_scatter(refs[n_in:n_in + n_sc], refs[n_in + n_sc + 1:n_in + 2 * n_sc + 1],
                                     refs[n_in + 2 * n_sc + 1:], pl.program_id(0) * grid[1] + pl.program_id(1),
                                     grid[0] * grid[1])
        acc = None
        for i in range(n_p):
            d = lax.dot_general(_bf(refs[i][...]), _bf(refs[n_p + i][...]), dims,
                                preferred_element_type=F32)
            acc = d if acc is None else acc + d
        if add is not None:
            acc = refs[2 * n_p][...] + acc
        o_ref[...] = acc.astype(out_dtype)
        if norm_g is not None:
            refs[n_in + 1][...] = (acc * _rstd(acc) * refs[n_in - 1][...]).astype(BF16)
        if n_sc:
            finish()

    main_spec = pl.BlockSpec((tm, tn), lambda m, n: (m, n))
    main_shape = jax.ShapeDtypeStruct((m_dim, n_dim), out_dtype)
    if norm_g is not None:
        return pl.pallas_call(body, name=name, grid=grid, in_specs=in_specs, out_specs=[main_spec, main_spec],
                              out_shape=[main_shape, jax.ShapeDtypeStruct((m_dim, n_dim), BF16)],
                              compiler_params=_cp(("parallel", "parallel")))(*args)
    if not n_sc:
        return pl.pallas_call(body, name=name, grid=grid, in_specs=in_specs, out_specs=main_spec,
                              out_shape=main_shape, compiler_params=_cp(("parallel", "parallel")))(*args)
    got_shapes, sems = _scatter_operands(scatter)
    outs = pl.pallas_call(
        body, name=name, grid=grid, in_specs=in_specs + _hbm_specs(n_sc), out_specs=[main_spec] + _hbm_specs(n_sc),
        out_shape=[main_shape] + got_shapes, scratch_shapes=sems,
        compiler_params=_cp(("arbitrary", "arbitrary")))(*args, *scatter)
    return outs[0], outs[1:]


def _mm_tn(a, b, tm, tn, name, out_chips=False):
    k_dim, m_dim = a.shape
    n_dim = b.shape[-1] * (b.shape[0] if b.ndim == 3 else 1)
    tm, tn, tk = min(tm, m_dim), min(tn, b.shape[-1]), min(TK_TN, k_dim)
    assert m_dim % tm == 0 and b.shape[-1] % tn == 0 and k_dim % tk == 0
    per = b.shape[-1] // tn
    if b.ndim == 3:
        b_spec = pl.BlockSpec((None, tk, tn), lambda m, n, k: (n // per, k, n % per))
    else:
        b_spec = pl.BlockSpec((tk, tn), lambda m, n, k: (k, n))

    def body(a_ref, b_ref, o_ref):
        @pl.when(pl.program_id(2) == 0)
        def _():
            o_ref[...] = jnp.zeros_like(o_ref)
        o_ref[...] += lax.dot_general(_bf(a_ref[...]), _bf(b_ref[...]), (((0,), (0,)), ((), ())),
                                      preferred_element_type=F32)

    return pl.pallas_call(
        body, name=name, grid=(m_dim // tm, n_dim // tn, k_dim // tk),
        in_specs=[pl.BlockSpec((tk, tm), lambda m, n, k: (k, m)), b_spec],
        out_specs=(pl.BlockSpec((None, tm, tn), lambda m, n, k: (n, m, 0)) if out_chips
                   else pl.BlockSpec((tm, tn), lambda m, n, k: (m, n))),
        out_shape=jax.ShapeDtypeStruct((n_dim // tn, m_dim, tn) if out_chips else (m_dim, n_dim), F32),
        compiler_params=_cp(("parallel", "parallel", "arbitrary")))(a, b)


def _rstd(x):
    return lax.rsqrt(jnp.mean(x * x, axis=-1, keepdims=True) + EPS)


def _rms_bwd(x, dh, g, dres, name, with_bf16, scatter=()):
    s, d = x.shape
    tm = min(TM_ROWS, s)
    n_sc = len(scatter)
    n_out = 3 if with_bf16 else 2
    got_shapes, sems = _scatter_operands(scatter) if n_sc else ([], [])

    def body(x_ref, dh_ref, g_ref, dres_ref, *rest):
        dx_ref, gg_ref = rest[n_sc], rest[n_sc + n_out - 1]
        i = pl.program_id(0)
        if n_sc:
            finish = _hosted_scatter(rest[:n_sc], rest[n_sc + n_out:2 * n_sc + n_out], rest[2 * n_sc + n_out:], i,
                                     s // tm)

        @pl.when(i == 0)
        def _():
            gg_ref[...] = jnp.zeros_like(gg_ref)

        xv = x_ref[...]
        xn = xv * _rstd(xv)
        dhv = dh_ref[...]
        gg_ref[...] += jnp.sum(dhv * xn, axis=0, keepdims=True)
        t = dhv * g_ref[...]
        dx = dres_ref[...] + _rstd(xv) * (t - xn * jnp.mean(t * xn, axis=-1, keepdims=True))
        dx_ref[...] = dx
        if with_bf16:
            rest[n_sc + 1][...] = dx.astype(BF16)
        if n_sc:
            finish()

    row = pl.BlockSpec((tm, d), lambda i: (i, 0))
    vec = pl.BlockSpec((1, d), lambda i: (0, 0))
    out_specs = [row] + ([row] if with_bf16 else []) + [vec] + _hbm_specs(n_sc)
    out_shape = ([jax.ShapeDtypeStruct((s, d), F32)] + ([jax.ShapeDtypeStruct((s, d), BF16)] if with_bf16 else [])
                 + [jax.ShapeDtypeStruct((1, d), F32)] + got_shapes)
    outs = pl.pallas_call(
        body, name=name, grid=(s // tm,), in_specs=[row, row, vec, row] + _hbm_specs(n_sc), out_specs=out_specs,
        out_shape=out_shape, scratch_shapes=sems, compiler_params=_cp(("arbitrary",)))(x, dh, g, dres, *scatter)
    return tuple(outs[:n_out]) + ((outs[n_out:],) if n_sc else ())


def _down_proj_loss(act, w_down, x2, target, g):
    s, d = x2.shape
    tm = min(TM_ROWS, s)
    k_dim = act.shape[1]

    def body(a_ref, w_ref, x_ref, t_ref, g_ref, dx_ref, dxb_ref, loss_ref, gg_ref):
        @pl.when(pl.program_id(0) == 0)
        def _():
            gg_ref[...] = jnp.zeros_like(gg_ref)
            loss_ref[...] = jnp.zeros_like(loss_ref)

        xv = x_ref[...] + lax.dot_general(a_ref[...], w_ref[...], (((1,), (0,)), ((), ())),
                                          preferred_element_type=F32)
        r = _rstd(xv)
        xn = xv * r
        gv = g_ref[...]
        err = xn * gv - t_ref[...]
        loss_ref[...] += 0.5 * jnp.sum(jnp.mean(err * err, axis=-1, keepdims=True), axis=0, keepdims=True)
        dy = err * (1.0 / d)
        gg_ref[...] += jnp.sum(dy * xn, axis=0, keepdims=True)
        t = dy * gv
        dx = r * (t - xn * jnp.mean(t * xn, axis=-1, keepdims=True))
        dx_ref[...] = dx
        dxb_ref[...] = dx.astype(BF16)

    row = pl.BlockSpec((tm, d), lambda i: (i, 0))
    vec = pl.BlockSpec((1, d), lambda i: (0, 0))
    return pl.pallas_call(
        body, name="down_proj_loss", grid=(s // tm,),
        in_specs=[pl.BlockSpec((tm, k_dim), lambda i: (i, 0)), pl.BlockSpec((k_dim, d), lambda i: (0, 0)), row, row,
                  vec],
        out_specs=[row, row, pl.BlockSpec((1, LANES), lambda i: (0, 0)), vec],
        out_shape=[jax.ShapeDtypeStruct((s, d), F32), jax.ShapeDtypeStruct((s, d), BF16),
                   jax.ShapeDtypeStruct((1, LANES), F32), jax.ShapeDtypeStruct((1, d), F32)],
        compiler_params=_cp(("arbitrary",)))(act, w_down, x2, target, g)


def _prev_halo_spec(tm, width, col):
    return pl.BlockSpec((HALO, width), lambda i, *_: (jnp.maximum(i * (tm // HALO) - 1, 0), col))


def _next_halo_spec(tm, width, col, s):
    return pl.BlockSpec((HALO, width), lambda i, *_: (jnp.minimum((i + 1) * (tm // HALO), s // HALO - 1), col))


def _shift_down(x, k):
    return pltpu.roll(x, k, 0)


def _shift_up(x, k):
    return pltpu.roll(x, x.shape[0] - k, 0)


def _conv_taps(x_ext, w):
    return w[0:1, :] * _shift_down(x_ext, 2) + w[1:2, :] * _shift_down(x_ext, 1) + w[2:3, :] * x_ext


def _conv_taps_t(d_ext, w):
    return w[2:3, :] * d_ext + w[1:2, :] * _shift_up(d_ext, 1) + w[0:1, :] * _shift_up(d_ext, 2)


def _mixer_fwd(z_a, o_attn, w_conv, g_conv_out, g_attn_out):
    s = z_a.shape[0]
    c = CONV_CH
    tm = min(TM_ROWS, s)

    def body(gb_ref, gc_ref, xc_ref, gcp_ref, xcp_ref, o_ref, w_ref, gco_ref, gao_ref, mix_ref):
        i = pl.program_id(0)
        cx = gc_ref[...] * xc_ref[...]
        cx_prev = jnp.where(i > 0, gcp_ref[...] * xcp_ref[...], 0.0)
        conv = _conv_taps(jnp.concatenate([cx_prev, cx], axis=0), w_ref[...])[HALO:]
        y = gb_ref[...] * conv
        mix_ref[:, 0:c] = (y * _rstd(y) * gco_ref[...]).astype(BF16)
        o = o_ref[...]
        mix_ref[:, c:2 * c] = (o * _rstd(o) * gao_ref[...]).astype(BF16)

    col = lambda j: pl.BlockSpec((tm, c), lambda i: (i, j))
    vec = pl.BlockSpec((1, c), lambda i: (0, 0))
    return pl.pallas_call(
        body, name="mixer_fwd", grid=(s // tm,),
        in_specs=[col(0), col(1), col(2), _prev_halo_spec(tm, c, 1), _prev_halo_spec(tm, c, 2), col(0),
                  pl.BlockSpec((3, c), lambda i: (0, 0)), vec, vec],
        out_specs=pl.BlockSpec((tm, 2 * c), lambda i: (i, 0)),
        out_shape=jax.ShapeDtypeStruct((s, 2 * c), BF16),
        compiler_params=_cp(("parallel",)))(z_a, z_a, z_a, z_a, z_a, o_attn, w_conv, g_conv_out, g_attn_out)


def _mixer_bwd(z_a, o_attn, dmix, w_conv, g_conv_out, g_attn_out, scatter):
    s = z_a.shape[0]
    c = CONV_CH
    tm = min(TM_ROWS, s)
    n_blk = s // tm
    n_sc = len(scatter)
    got_shapes, sems = _scatter_operands(scatter)

    def body(gb_ref, gc_ref, xc_ref, gcp_ref, xcp_ref, gbn_ref, gcn_ref, xcn_ref, o_ref, dnc_ref, dncn_ref, dna_ref,
             w_ref, gco_ref, gao_ref, *rest):
        dz_ref, dox_ref, gw_ref, ggco_ref, ggao_ref = rest[n_sc:n_sc + 5]
        i = pl.program_id(0)
        finish = _hosted_scatter(rest[:n_sc], rest[n_sc + 5:2 * n_sc + 5], rest[2 * n_sc + 5:], i, n_blk)

        @pl.when(i == 0)
        def _():
            gw_ref[...] = jnp.zeros_like(gw_ref)
            ggco_ref[...] = jnp.zeros_like(ggco_ref)
            ggao_ref[...] = jnp.zeros_like(ggao_ref)

        w = w_ref[...]
        zeros = jnp.zeros((HALO, c), F32)
        gb_e = jnp.concatenate([zeros, gb_ref[...], gbn_ref[...]], axis=0)
        cx_prev = jnp.where(i > 0, gcp_ref[...] * xcp_ref[...], 0.0)
        gc_e = jnp.concatenate([zeros, gc_ref[...], gcn_ref[...]], axis=0)
        xc_e = jnp.concatenate([zeros, xc_ref[...], xcn_ref[...]], axis=0)
        cx_e = jnp.concatenate([cx_prev, gc_ref[...] * xc_ref[...], gcn_ref[...] * xcn_ref[...]], axis=0)
        dn_next = jnp.where(i < n_blk - 1, dncn_ref[...], 0.0)
        dn_e = jnp.concatenate([zeros, dnc_ref[...], dn_next], axis=0)

        cx_1, cx_2 = _shift_down(cx_e, 1), _shift_down(cx_e, 2)
        conv_e = w[0:1, :] * cx_2 + w[1:2, :] * cx_1 + w[2:3, :] * cx_e
        y_e = gb_e * conv_e
        r_e = _rstd(y_e)
        yn_e = y_e * r_e
        t_e = dn_e * gco_ref[...]
        dy_e = r_e * (t_e - yn_e * jnp.mean(t_e * yn_e, axis=-1, keepdims=True))
        dconv_e = dy_e * gb_e
        dcx_e = _conv_taps_t(dconv_e, w)
        blk = slice(HALO, HALO + tm)
        dz_ref[:, 0:c] = (dy_e * conv_e)[blk].astype(BF16)
        dz_ref[:, c:2 * c] = (dcx_e * xc_e)[blk].astype(BF16)
        dz_ref[:, 2 * c:3 * c] = (dcx_e * gc_e)[blk].astype(BF16)
        ggco_ref[...] += jnp.sum((dn_e * yn_e)[blk], axis=0, keepdims=True)
        dconv = dconv_e[blk]
        gw_ref[0:1, :] += jnp.sum(dconv * cx_2[blk], axis=0, keepdims=True)
        gw_ref[1:2, :] += jnp.sum(dconv * cx_1[blk], axis=0, keepdims=True)
        gw_ref[2:3, :] += jnp.sum(dconv * cx_e[blk], axis=0, keepdims=True)

        o = o_ref[...]
        ra = _rstd(o)
        on = o * ra
        dna = dna_ref[...]
        ggao_ref[...] += jnp.sum(dna * on, axis=0, keepdims=True)
        ta = dna * gao_ref[...]
        do = ra * (ta - on * jnp.mean(ta * on, axis=-1, keepdims=True))
        prod = do * o
        lane = lax.broadcasted_iota(jnp.int32, (tm, LANES), 1)
        head_a = lane < HEAD_DIM
        for p in range(N_PAIRS):
            cols = slice(p * LANES, (p + 1) * LANES)
            pb, dob = prod[:, cols], do[:, cols]
            for hh in range(2):
                sel = head_a if hh == 0 else jnp.logical_not(head_a)
                delta = jnp.sum(jnp.where(sel, pb, 0.0), axis=-1, keepdims=True)
                neg3 = _split3(-delta)
                do_h = pltpu.roll(dob, HEAD_DIM, 1) if hh else dob
                dox_ref[2 * p + hh] = _aug(do_h, lane, neg3).astype(BF16)
        finish()

    col = lambda j: pl.BlockSpec((tm, c), lambda i: (i, j))
    vec = pl.BlockSpec((1, c), lambda i: (0, 0))
    w3 = pl.BlockSpec((3, c), lambda i: (0, 0))
    outs = pl.pallas_call(
        body, name="mixer_bwd", grid=(n_blk,),
        in_specs=[col(0), col(1), col(2), _prev_halo_spec(tm, c, 1), _prev_halo_spec(tm, c, 2),
                  _next_halo_spec(tm, c, 0, s), _next_halo_spec(tm, c, 1, s), _next_halo_spec(tm, c, 2, s),
                  col(0), col(0), _next_halo_spec(tm, c, 0, s), col(1), w3, vec, vec] + _hbm_specs(n_sc),
        out_specs=[pl.BlockSpec((tm, 3 * c), lambda i: (i, 0)),
                   pl.BlockSpec((N_HEADS, tm, LANES), lambda i: (0, i, 0)), w3, vec, vec] + _hbm_specs(n_sc),
        out_shape=[jax.ShapeDtypeStruct((s, 3 * c), BF16), jax.ShapeDtypeStruct((N_HEADS, s, LANES), BF16),
                   jax.ShapeDtypeStruct((3, c), F32), jax.ShapeDtypeStruct((1, c), F32),
                   jax.ShapeDtypeStruct((1, c), F32)] + got_shapes,
        scratch_shapes=sems, compiler_params=_cp(("arbitrary",)))(
            z_a, z_a, z_a, z_a, z_a, z_a, z_a, z_a, o_attn, dmix, dmix, dmix, w_conv, g_conv_out, g_attn_out,
            *scatter)
    return tuple(outs[:5]) + (outs[5:],)


def _gate_fwd(f, b_pad):
    s = f.shape[0]
    tm = min(TQ, s)

    def body(f_ref, b_ref, fb_ref, carry):
        @pl.when(pl.program_id(0) == 0)
        def _():
            carry[...] = jnp.zeros_like(carry)

        z = f_ref[...] + b_ref[...]
        x = jnp.minimum(z, 0.0) - jnp.log1p(jnp.exp(-jnp.abs(z)))
        row = lax.broadcasted_iota(jnp.int32, (tm, LANES), 0)
        sh = 1
        while sh < tm:
            x = x + jnp.where(row >= sh, _shift_down(x, sh), 0.0)
            sh *= 2
        x = x + carry[0:1, :]
        carry[...] = jnp.broadcast_to(x[tm - 1:tm, :], carry.shape)
        head_a = lax.broadcasted_iota(jnp.int32, (tm, LANES), 1) < HEAD_DIM
        for p in range(N_PAIRS):
            fa = jnp.broadcast_to(x[:, 2 * p:2 * p + 1], (tm, LANES))
            fbv = jnp.broadcast_to(x[:, 2 * p + 1:2 * p + 2], (tm, LANES))
            fb_ref[:, p * LANES:(p + 1) * LANES] = jnp.where(head_a, fa, fbv)

    return pl.pallas_call(
        body, name="gate_fwd", grid=(s // tm,),
        in_specs=[pl.BlockSpec((tm, LANES), lambda i: (i, 0)), pl.BlockSpec((1, LANES), lambda i: (0, 0))],
        out_specs=pl.BlockSpec((tm, N_PAIRS * LANES), lambda i: (i, 0)),
        out_shape=jax.ShapeDtypeStruct((s, N_PAIRS * LANES), F32),
        scratch_shapes=[pltpu.VMEM((HALO, LANES), F32)],
        compiler_params=_cp(("arbitrary",)))(f, b_pad)


def _gate_bwd(f, b_pad, d_f):
    s = f.shape[0]
    tm = min(TQ, s)
    n_blk = s // tm

    def body(f_ref, b_ref, d_ref, df_ref, gb_ref, carry):
        @pl.when(pl.program_id(0) == 0)
        def _():
            carry[...] = jnp.zeros_like(carry)
            gb_ref[...] = jnp.zeros_like(gb_ref)

        x = d_ref[...]
        row = lax.broadcasted_iota(jnp.int32, (tm, LANES), 0)
        sh = 1
        while sh < tm:
            x = x + jnp.where(row < tm - sh, _shift_up(x, sh), 0.0)
            sh *= 2
        x = x + carry[0:1, :]
        carry[...] = jnp.broadcast_to(x[0:1, :], carry.shape)
        z = f_ref[...] + b_ref[...]
        d = x * (1.0 / (1.0 + jnp.exp(z)))
        df_ref[...] = d.astype(BF16)
        gb_ref[...] += jnp.sum(d, axis=0, keepdims=True)

    rev = pl.BlockSpec((tm, LANES), lambda i: (n_blk - 1 - i, 0))
    vec = pl.BlockSpec((1, LANES), lambda i: (0, 0))
    return pl.pallas_call(
        body, name="gate_bwd", grid=(n_blk,), in_specs=[rev, vec, rev], out_specs=[rev, vec],
        out_shape=[jax.ShapeDtypeStruct((s, LANES), BF16), jax.ShapeDtypeStruct((1, LANES), F32)],
        scratch_shapes=[pltpu.VMEM((HALO, LANES), F32)],
        compiler_params=_cp(("arbitrary",)))(f, b_pad, d_f)


_NT = (((1,), (1,)), ((), ()))
_NN = (((1,), (0,)), ((), ()))


AUG = HEAD_DIM
NORM_MARGIN = 1.01


def _split3(x):
    hi = x.astype(BF16).astype(F32)
    r = x - hi
    mid = r.astype(BF16).astype(F32)
    lo = (r - mid).astype(BF16).astype(F32)
    return hi, mid, lo


def _aug(base, lane, vals):
    out = jnp.where(lane < AUG, base, 0.0)
    for k, v in enumerate(vals):
        out = jnp.where(lane == AUG + k, v, out)
    return out


def _attn_prep(qkv, fb, bigs):
    s = qkv.shape[0]
    tq = min(TQ, s)
    n_q = s // tq

    n = len(bigs)
    arrays, landing, sems = _gather_operands(bigs, [])

    def body(q_ref, k_ref, v_ref, fb_ref, *rest):
        qx_ref, kx_ref, kxt_ref, vx_ref, vt_ref, b_ref = rest[2 * n:2 * n + 6]
        finish = _hosted_gather((rest[:n], rest[2 * n + 6:3 * n + 6]) + tuple(rest[3 * n + 6:]), n, n,
                                pl.program_id(0), n_q)
        lane = lax.broadcasted_iota(jnp.int32, (tq, LANES), 1)
        lane8 = lax.broadcasted_iota(jnp.int32, (HALO, LANES), 1)
        head_lanes = lane < AUG
        is_lane = [lane == AUG + k for k in range(6)]
        first3 = (lane >= AUG) & (lane < AUG + 3)
        next3 = (lane >= AUG + 3) & (lane < AUG + 6)
        q_const = jnp.where(first3, -1.0, 0.0)
        k_const = jnp.where(next3, 1.0, 0.0)
        v_const = jnp.where(first3, 1.0, 0.0)
        ones_head = (lax.broadcasted_iota(jnp.int32, (LANES, LANES), 0) < HEAD_DIM).astype(BF16)
        acc = jnp.zeros((HALO, LANES), F32)
        for p in range(N_PAIRS):
            cols = slice(p * LANES, (p + 1) * LANES)
            q2, k2, v2 = (ref[:, cols].astype(F32) for ref in (q_ref, k_ref, v_ref))
            q2 = q2 * Q_SCALE
            f2 = fb_ref[:, cols]
            for hh in range(2):
                h = 2 * p + hh
                q, k, v = ((pltpu.roll(x, HEAD_DIM, 1) if hh else x) for x in (q2, k2, v2))
                f = f2 if hh else pltpu.roll(f2, HEAD_DIM, 1)
                hi, mid, lo = _split3(f)
                q_aug = jnp.where(is_lane[3], hi, jnp.where(is_lane[4], mid, jnp.where(is_lane[5], lo, q_const)))
                k_aug = jnp.where(is_lane[0], hi, jnp.where(is_lane[1], mid, jnp.where(is_lane[2], lo, k_const)))
                kx = jnp.where(head_lanes, k, k_aug)
                vx = jnp.where(head_lanes, v, v_const)
                qx_ref[h] = jnp.where(head_lanes, q, q_aug).astype(BF16)
                kx_ref[h] = kx.astype(BF16)
                vx_ref[h] = vx.astype(BF16)
                kxt_ref[h, 0] = kx.T.astype(BF16)
                vt_ref[h, 0] = vx.T.astype(BF16)
                q_sq = lax.dot_general((q * q).astype(BF16), ones_head, _NN, preferred_element_type=F32)
                k_sq = lax.dot_general((k * k).astype(BF16), ones_head, _NN, preferred_element_type=F32)
                diag = lax.dot_general((q * k).astype(BF16), ones_head, _NN, preferred_element_type=F32)
                diag = diag - jnp.sqrt(q_sq * k_sq) * (NORM_MARGIN - 1.0)
                vals = (jnp.sqrt(jnp.max(q_sq, axis=0, keepdims=True)), jnp.sqrt(jnp.max(k_sq, axis=0, keepdims=True)),
                        jnp.max(f - diag, axis=0, keepdims=True), f[tq - 1:tq, :])
                for slot, val in enumerate(vals):
                    acc = jnp.where(lane8 == slot * N_HEADS + h, val[:, AUG:AUG + 1], acc)
        b_ref[0] = acc
        finish()

    blk = lambda j: pl.BlockSpec((tq, ATTN_W), lambda i: (i, j))
    rows = pl.BlockSpec((N_HEADS, tq, LANES), lambda i: (0, i, 0))
    cols_t = pl.BlockSpec((N_HEADS, 1, LANES, tq), lambda i: (0, i, 0, 0))
    shp = jax.ShapeDtypeStruct((N_HEADS, s, LANES), BF16)
    shp_t = jax.ShapeDtypeStruct((N_HEADS, n_q, LANES, tq), BF16)
    outs = pl.pallas_call(
        body, name="attn_prep", grid=(n_q,), in_specs=[blk(0), blk(1), blk(2), blk(0)] + _hbm_specs(2 * n),
        out_specs=[rows, rows, cols_t, rows, cols_t,
                   pl.BlockSpec((1, HALO, LANES), lambda i: (i, 0, 0))] + _hbm_specs(n),
        out_shape=[shp, shp, shp_t, shp, shp_t, jax.ShapeDtypeStruct((n_q, HALO, LANES), F32)]
        + [jax.ShapeDtypeStruct(b.shape, b.dtype) for b in landing],
        input_output_aliases={4 + n + k: 6 + k for k in range(n)}, scratch_shapes=sems,
        compiler_params=_cp(("arbitrary",)))(qkv, qkv, qkv, fb, *arrays, *landing)
    return tuple(outs[:6]) + (outs[6:],)


def _key_block_ranges(bounds):
    t = bounds[:, 0, :]
    nh = N_HEADS
    a, b, c, e = t[:, 0:nh], t[:, nh:2 * nh], t[:, 2 * nh:3 * nh], t[:, 3 * nh:4 * nh]
    bound = a[:, None, :] * b[None, :, :] * NORM_MARGIN + c[:, None, :] - e[None, :, :]
    n_q = t.shape[0]
    idx = jnp.arange(n_q)
    need = jnp.logical_not(bound < -(EXP_ZERO + 2.0)) | (idx[None, :, None] >= idx[:, None, None])
    first = jnp.argmax(need, axis=1).astype(jnp.int32)
    first = jnp.min(first.reshape(n_q, N_PAIRS, 2), axis=-1)
    visits = (first[:, None, :] <= idx[None, :, None]) & (idx[None, :, None] <= idx[:, None, None])
    last = jnp.max(jnp.where(visits, idx[:, None, None], 0), axis=0).astype(jnp.int32)
    return first.T.reshape(-1), last.T.reshape(-1)


def _attn_fwd_t(qx, kx, vt, first_blk):
    _, s, _ = qx.shape
    tq = min(TQ, s)
    n_q = s // tq
    neg = -1e30

    def body(first_ref, qx_ref, kx_ref, vt_ref, o_ref, lse_ref, acc_ref, m_ref, s_even, s_odd):
        p = pl.program_id(0)
        i = pl.program_id(1)
        acc_ref[...] = jnp.zeros(acc_ref.shape, F32)
        m_ref[...] = jnp.full(m_ref.shape, neg, F32)
        key_le_query = (lax.broadcasted_iota(jnp.int32, (tq, tq), 0) <= lax.broadcasted_iota(jnp.int32, (tq, tq), 1))
        first = first_ref[p * n_q + i]

        def scores(kb, hh, dst):
            rows_k = pl.ds(pl.multiple_of(kb * tq, tq), tq)
            dst[hh] = lax.dot_general(kx_ref[hh, rows_k, :], qx_ref[hh], _NT, preferred_element_type=F32)

        def step(kb, src, nxt):
            for hh in range(2):
                st = src[hh]
                if nxt is None:
                    st = jnp.where(key_le_query, st, -jnp.inf)
                else:
                    scores(kb + 1, hh, nxt)
                m_old = m_ref[hh]
                m_new = jnp.maximum(m_old, jnp.max(st, axis=0, keepdims=True))
                m_ref[hh] = m_new
                pt = jnp.exp(st - m_new).astype(BF16)
                acc_ref[hh] = acc_ref[hh] * jnp.exp(m_old - m_new) + lax.dot_general(
                    vt_ref[hh, kb], pt, _NN, preferred_element_type=F32)

        def by_parity(kb, fn):
            @pl.when(kb % 2 == 0)
            def _():
                fn(s_even, s_odd)

            @pl.when(kb % 2 == 1)
            def _():
                fn(s_odd, s_even)

        def first_scores(src, nxt):
            scores(first, 0, src)
            scores(first, 1, src)

        def unmasked(kb, carry):
            by_parity(kb, lambda src, nxt: step(kb, src, nxt))
            return carry

        by_parity(first, first_scores)
        lax.fori_loop(first, i, unmasked, 0)
        by_parity(i, lambda src, nxt: step(i, src, None))
        outs, lses = [], []
        for hh in range(2):
            acc = acc_ref[hh]
            l = acc[AUG:AUG + 1, :]
            outs.append(acc[0:HEAD_DIM, :] / l)
            lses.append(m_ref[hh] + jnp.log(l))
        o_ref[...] = jnp.concatenate(outs, axis=0).T
        rows8 = lax.broadcasted_iota(jnp.int32, (N_HEADS, tq), 0)
        lse_ref[0, 0] = jnp.where(rows8 == 0, lses[0], jnp.where(rows8 == 1, lses[1], 0.0))

    grid_spec = pltpu.PrefetchScalarGridSpec(
        num_scalar_prefetch=1, grid=(N_PAIRS, n_q),
        in_specs=[pl.BlockSpec((2, tq, LANES), lambda p, i, first: (p, i, 0)),
                  pl.BlockSpec((2, s, LANES), lambda p, i, first: (p, 0, 0)),
                  pl.BlockSpec((2, n_q, LANES, tq), lambda p, i, first: (p, 0, 0, 0))],
        out_specs=[pl.BlockSpec((tq, LANES), lambda p, i, first: (i, p)),
                   pl.BlockSpec((1, 1, N_HEADS, tq), lambda p, i, first: (p, i, 0, 0))],
        scratch_shapes=[pltpu.VMEM((2, LANES, tq), F32), pltpu.VMEM((2, 1, tq), F32),
                        pltpu.VMEM((2, tq, tq), F32), pltpu.VMEM((2, tq, tq), F32)])
    return pl.pallas_call(
        body, name="attn_fwd", grid_spec=grid_spec,
        out_shape=[jax.ShapeDtypeStruct((s, ATTN_W), F32), jax.ShapeDtypeStruct((N_PAIRS, n_q, N_HEADS, tq), F32)],
        compiler_params=_cp(("parallel", "arbitrary")))(first_blk, qx, kx, vt)


def _attn_bwd_t(qx, dox, kx, kxt, vx, lse, last_blk):
    _, s, _ = qx.shape
    tq = min(TQ, s)
    n_q = s // tq

    def body(last_ref, qx_ref, dox_ref, lse_ref, kx_ref, kxt_ref, vx_ref, dk_ref, dv_ref, dfk_ref, dqt_ref, dfq_ref):
        p = pl.program_id(0)
        j = pl.program_id(1)

        @pl.when(j == 0)
        def _():
            dqt_ref[...] = jnp.zeros(dqt_ref.shape, F32)
            dfq_ref[...] = jnp.zeros(dfq_ref.shape, F32)

        key_le_query = (lax.broadcasted_iota(jnp.int32, (tq, tq), 0) <= lax.broadcasted_iota(jnp.int32, (tq, tq), 1))

        def step(i, carry, masked):
            rows_q = pl.ds(pl.multiple_of(i * tq, tq), tq)
            out = []
            for hh in range(2):
                dk, dv, col = carry[3 * hh:3 * hh + 3]
                q, do = qx_ref[hh, rows_q, :], dox_ref[hh, rows_q, :]
                st = lax.dot_general(kx_ref[hh], q, _NT, preferred_element_type=F32)
                pt = jnp.exp(st - lse_ref[0, i, hh:hh + 1, :])
                if masked:
                    pt = jnp.where(key_le_query, pt, 0.0)
                dst = pt * lax.dot_general(vx_ref[hh], do, _NT, preferred_element_type=F32)
                pb, dsb = pt.astype(BF16), dst.astype(BF16)
                dv = dv + lax.dot_general(pb, do, _NN, preferred_element_type=F32)
                dk = dk + lax.dot_general(dsb, q, _NN, preferred_element_type=F32)
                dqt_ref[hh, i] += lax.dot_general(kxt_ref[hh, 0], dsb, _NN, preferred_element_type=F32)
                for cb in range(tq // LANES):
                    col = col + dst[:, cb * LANES:(cb + 1) * LANES]
                dfq_ref[hh, i] += jnp.sum(dst.reshape(tq // HALO, HALO, tq), axis=0)
                out += [dk, dv, col]
            return tuple(out)

        zero = jnp.zeros((tq, LANES), F32)
        carry = step(j, (zero,) * 6, True)
        dk_a, dv_a, col_a, dk_b, dv_b, col_b = lax.fori_loop(j + 1, last_ref[p * n_q + j] + 1,
                                                             lambda i, cr: step(i, cr, False), carry)
        head_a = lax.broadcasted_iota(jnp.int32, (tq, LANES), 1) < HEAD_DIM
        dk_ref[...] = jnp.where(head_a, dk_a, pltpu.roll(dk_b, HEAD_DIM, 1)).astype(BF16)
        dv_ref[...] = jnp.where(head_a, dv_a, pltpu.roll(dv_b, HEAD_DIM, 1)).astype(BF16)
        rows8 = lax.broadcasted_iota(jnp.int32, (N_HEADS, tq), 0)
        dfk_a, dfk_b = (-jnp.sum(c.T, axis=0, keepdims=True) for c in (col_a, col_b))
        dfk_ref[0, 0] = jnp.where(rows8 == 0, dfk_a, jnp.where(rows8 == 1, dfk_b, 0.0))

    resident = pl.BlockSpec((2, s, LANES), lambda p, j, last: (p, 0, 0))
    key_rows = pl.BlockSpec((2, tq, LANES), lambda p, j, last: (p, j, 0))
    pair_out = pl.BlockSpec((tq, LANES), lambda p, j, last: (j, p))
    grid_spec = pltpu.PrefetchScalarGridSpec(
        num_scalar_prefetch=1, grid=(N_PAIRS, n_q),
        in_specs=[resident, resident, pl.BlockSpec((1, n_q, N_HEADS, tq), lambda p, j, last: (p, 0, 0, 0)),
                  key_rows, pl.BlockSpec((2, 1, LANES, tq), lambda p, j, last: (p, j, 0, 0)), key_rows],
        out_specs=[pair_out, pair_out, pl.BlockSpec((1, 1, N_HEADS, tq), lambda p, j, last: (p, j, 0, 0)),
                   pl.BlockSpec((2, n_q, LANES, tq), lambda p, j, last: (p, 0, 0, 0)),
                   pl.BlockSpec((2, n_q, HALO, tq), lambda p, j, last: (p, 0, 0, 0))])
    return pl.pallas_call(
        body, name="attn_bwd", grid_spec=grid_spec,
        out_shape=[jax.ShapeDtypeStruct((s, ATTN_W), BF16), jax.ShapeDtypeStruct((s, ATTN_W), BF16),
                   jax.ShapeDtypeStruct((N_PAIRS, n_q, N_HEADS, tq), F32),
                   jax.ShapeDtypeStruct((N_HEADS, n_q, LANES, tq), F32),
                   jax.ShapeDtypeStruct((N_HEADS, n_q, HALO, tq), F32)],
        compiler_params=_cp(("parallel", "arbitrary")))(last_blk, qx, dox, lse, kx, kxt, vx)


def _attn_dq_finish(dqt):
    _, n_q, _, tq = dqt.shape
    per = 4 if n_q % 4 == 0 else 1

    def body(dqt_ref, dq_ref):
        for b in range(per):
            a, bb = dqt_ref[0, b], dqt_ref[1, b]
            dq_ref[b * tq:(b + 1) * tq, :] = (
                jnp.concatenate([a[0:HEAD_DIM], bb[0:HEAD_DIM]], axis=0).T * Q_SCALE).astype(BF16)

    return pl.pallas_call(
        body, name="attn_dq_finish", grid=(N_PAIRS, n_q // per),
        in_specs=[pl.BlockSpec((2, per, LANES, tq), lambda p, i: (p, i, 0, 0))],
        out_specs=pl.BlockSpec((per * tq, LANES), lambda p, i: (i, p)),
        out_shape=jax.ShapeDtypeStruct((n_q * tq, ATTN_W), BF16),
        compiler_params=_cp(("parallel", "parallel")))(dqt)


def _ffn_act_fwd(up, w_ffn):
    s = up.shape[0]
    tm, tn = min(TM_FFN, s), TN_FFN
    nb = D_FF // tn

    def body(a_ref, g_ref, ap_ref, gp_ref, wa_ref, wg_ref, act_ref, u_ref):
        i = pl.program_id(1)

        def conv(blk_ref, prev_ref, w_ref):
            prev = jnp.where(i > 0, prev_ref[...].astype(F32)[HALO:], 0.0)
            return _conv_taps(jnp.concatenate([prev, blk_ref[...].astype(F32)], axis=0), w_ref[...])[HALO:]

        u_a, u_g = conv(a_ref, ap_ref, wa_ref), conv(g_ref, gp_ref, wg_ref)
        u_ref[0], u_ref[1] = u_a.astype(BF16), u_g.astype(BF16)
        act_ref[...] = (u_g * jax.nn.sigmoid(u_g) * u_a).astype(BF16)

    blk = lambda off: pl.BlockSpec((tm, tn), lambda n, i: (i, off + n))
    prev = lambda off: pl.BlockSpec(
        (HALO_BF16, tn), lambda n, i: (jnp.maximum(i * (tm // HALO_BF16) - 1, 0), off + n))
    wsp = lambda off: pl.BlockSpec((3, tn), lambda n, i: (0, off + n))
    return pl.pallas_call(
        body, name="ffn_act_fwd", grid=(nb, s // tm),
        in_specs=[blk(0), blk(nb), prev(0), prev(nb), wsp(0), wsp(nb)],
        out_specs=[pl.BlockSpec((tm, tn), lambda n, i: (i, n)), pl.BlockSpec((2, tm, tn), lambda n, i: (0, i, n))],
        out_shape=[jax.ShapeDtypeStruct((s, D_FF), BF16), jax.ShapeDtypeStruct((2, s, D_FF), BF16)],
        compiler_params=_cp(("parallel", "parallel")))(up, up, up, up, w_ffn, w_ffn)


def _ffn_act_bwd(up, u, dact, w_ffn):
    s = up.shape[0]
    tm, tn = min(TM_FFN, s), TN_FFN
    nb = D_FF // tn
    n_blk = s // tm

    def body(u_ref, un_ref, a_ref, g_ref, d_ref, dn_ref, wa_ref, wg_ref, dup_ref, gwa_ref, gwg_ref):
        i = pl.program_id(1)

        @pl.when(i == 0)
        def _():
            gwa_ref[...] = jnp.zeros_like(gwa_ref)
            gwg_ref[...] = jnp.zeros_like(gwg_ref)

        ext = lambda rows, next_rows: jnp.concatenate([rows, next_rows], axis=0)
        u_a, u_g = (ext(u_ref[h].astype(F32), un_ref[h].astype(F32)[:HALO]) for h in range(2))
        d_e = ext(d_ref[...], jnp.where(i < n_blk - 1, dn_ref[...], 0.0))
        sig = jax.nn.sigmoid(u_g)
        du_a = d_e * (u_g * sig)
        du_g = d_e * u_a * (sig * (1.0 + u_g * (1.0 - sig)))
        halves = ((gwa_ref, wa_ref[...], a_ref[...].astype(F32), du_a),
                  (gwg_ref, wg_ref[...], g_ref[...].astype(F32), du_g))
        for half, (gw_ref, w, upv, du) in enumerate(halves):
            du0, du1, du2 = du[:tm], _shift_up(du, 1)[:tm], _shift_up(du, 2)[:tm]
            dup_ref[half] = (w[2:3, :] * du0 + w[1:2, :] * du1 + w[0:1, :] * du2).astype(BF16)
            gw_ref[0:1, :] += jnp.sum(upv * du2, axis=0, keepdims=True)
            gw_ref[1:2, :] += jnp.sum(upv * du1, axis=0, keepdims=True)
            gw_ref[2:3, :] += jnp.sum(upv * du0, axis=0, keepdims=True)

    next_row = lambda halo: lambda i: jnp.minimum((i + 1) * (tm // halo), s // halo - 1)
    blk = lambda off: pl.BlockSpec((tm, tn), lambda n, i: (i, off + n))
    wsp = lambda off: pl.BlockSpec((3, tn), lambda n, i: (0, off + n))
    pair = lambda rows, row_of: pl.BlockSpec((2, rows, tn), lambda n, i: (0, row_of(i), n))
    return pl.pallas_call(
        body, name="ffn_act_bwd", grid=(nb, n_blk),
        in_specs=[pair(tm, lambda i: i), pair(HALO_BF16, next_row(HALO_BF16)), blk(0), blk(nb), blk(0),
                  pl.BlockSpec((HALO, tn), lambda n, i: (next_row(HALO)(i), n)), wsp(0), wsp(nb)],
        out_specs=[pair(tm, lambda i: i), wsp(0), wsp(0)],
        out_shape=[jax.ShapeDtypeStruct((2, s, D_FF), BF16),
                   jax.ShapeDtypeStruct((3, D_FF), F32), jax.ShapeDtypeStruct((3, D_FF), F32)],
        compiler_params=_cp(("parallel", "arbitrary")))(u, u, up, up, dact, dact, w_ffn, w_ffn)


def _adamw(w, g, m, v, name):
    r, c = g.shape
    tr = next((t for t in (512, 352, 256, 128, 64, 32, 16, 8) if r > t and r % t == 0), r)

    def body(w_ref, g_ref, m_ref, v_ref, d_ref, nm_ref, nv_ref):
        gv = g_ref[...]
        m_new = ADAM_B1 * m_ref[...] + (1.0 - ADAM_B1) * gv
        v_new = ADAM_B2 * v_ref[...] + (1.0 - ADAM_B2) * (gv * gv)
        m_hat = m_new / (1.0 - ADAM_B1 ** ADAM_STEP)
        v_hat = v_new / (1.0 - ADAM_B2 ** ADAM_STEP)
        d_ref[...] = -ADAM_LR * (m_hat / (jnp.sqrt(v_hat) + ADAM_EPS) + ADAM_WD * w_ref[...])
        nm_ref[...] = m_new
        nv_ref[...] = v_new

    tc = c if tr < r else next((t for t in (256, 128) if c > t and c % t == 0), c)
    pos = (lambda i: (i, 0)) if tr < r else (lambda i: (0, i))
    spec = pl.BlockSpec((tr, tc), pos)
    w_spec = spec if w.ndim == 2 else pl.BlockSpec((None, tr, tc), lambda i: (0, *pos(i)))
    shp = jax.ShapeDtypeStruct(w.shape, F32)
    return pl.pallas_call(
        body, name=name, grid=(r // tr * (c // tc),), in_specs=[w_spec, spec, w_spec, w_spec], out_specs=[w_spec] * 3,
        out_shape=[shp] * 3, compiler_params=_cp(("parallel",)))(w, g, m, v)


def _sum_rows_block(h):
    return h if h <= 352 else 256


def _pair_sum(view, recv, sel, name):
    n, _, h, c = view.shape
    tr = _sum_rows_block(h)

    def body(sel_ref, a_ref, b_ref, o_ref, ob_ref):
        t = a_ref[...] + b_ref[...]
        o_ref[...] = t
        ob_ref[...] = t.astype(BF16)

    blk = pl.BlockSpec((None, tr, c), lambda j, i, sel_ref: (j, i, 0))
    grid_spec = pltpu.PrefetchScalarGridSpec(
        num_scalar_prefetch=1, grid=(n, h // tr),
        in_specs=[pl.BlockSpec((None, None, tr, c), lambda j, i, sel_ref: (j, sel_ref[0], i, 0)),
                  pl.BlockSpec((None, None, tr, c), lambda j, i, sel_ref: (j, 0, i, 0))],
        out_specs=[blk, blk])
    return pl.pallas_call(
        body, name=name, grid_spec=grid_spec,
        out_shape=[jax.ShapeDtypeStruct((n, h, c), F32), jax.ShapeDtypeStruct((n, h, c), BF16)],
        compiler_params=_cp(("parallel", "parallel")))(sel, view, recv)


def _chip_sum(pair, got, sel, name):
    _, h, c = pair.shape
    tr = _sum_rows_block(h)
    nblk = h // tr

    def body(sel_ref, p_ref, g0_ref, g1_ref, g2_ref, o_ref):
        o_ref[...] = ((p_ref[...] + g0_ref[...].astype(F32)) + g1_ref[...].astype(F32)) + g2_ref[...].astype(F32)

    slot = lambda k: pl.BlockSpec((None, tr, c), lambda i, sel_ref: (k, i, 0))
    grid_spec = pltpu.PrefetchScalarGridSpec(
        num_scalar_prefetch=1, grid=(h // tr,),
        in_specs=[pl.BlockSpec((None, tr, c), lambda i, sel_ref: (sel_ref[1], i, 0)), slot(0), slot(1), slot(2)],
        out_specs=pl.BlockSpec((tr, c), lambda i, sel_ref: (sel_ref[0] * nblk + i, 0)))
    return pl.pallas_call(
        body, name=name, grid_spec=grid_spec, out_shape=jax.ShapeDtypeStruct((2 * h, c), F32),
        compiler_params=_cp(("parallel",)))(sel, pair, got, got, got)


def _place():
    return lax.axis_index("x"), lax.axis_index("y"), lax.axis_index("c")


def _other_chips(x, y):
    return [(1 - x, y), (x, 1 - y), (1 - x, 1 - y)]


def _hbm_specs(n):
    return [pl.BlockSpec(memory_space=pl.ANY)] * n


def _all_gather_weights(bigs, smalls):
    nb, ns = len(bigs), len(smalls)
    n = nb + ns

    def body(*refs):
        start, forward, finish = _gather_phases(refs[:n], refs[2 * n:3 * n], nb, *refs[3 * n:])
        start()
        forward()
        finish()

    arrays, landing, sems = _gather_operands(bigs, smalls)
    return pl.pallas_call(
        body, name="all_gather_weights",
        out_shape=[jax.ShapeDtypeStruct(b.shape, b.dtype) for b in landing],
        in_specs=_hbm_specs(2 * n), out_specs=_hbm_specs(n), input_output_aliases={n + k: k for k in range(n)},
        scratch_shapes=sems)(*arrays, *landing)


def _hosted_gather(refs, n, nb, step, total):
    ins, outs, send_sems, recv_sems = refs
    start, forward, finish = _gather_phases(ins, outs, nb, send_sems, recv_sems)
    pl.when(step == 0)(start)
    pl.when(step == (3 * total) // 4)(forward)
    return lambda: pl.when(step == total - 1)(finish)


def _in_proj(x, g, w_a, w_b, w_c, bigs, smalls):
    s, d = x.shape
    tm, tn = min(TM_MM, s), ATTN_W
    na, nq = w_a.shape[1] // tn, w_b.shape[1] // tn
    steps = na + nq + 1
    total = (s // tm) * steps
    nb, n = len(bigs), len(bigs) + len(smalls)
    arrays, landing, sems = _gather_operands(bigs, smalls)

    def body(x_ref, g_ref, wa_ref, wb_ref, wc_ref, *rest):
        z_ref, qkv_ref, f_ref, h_ref = rest[2 * n:2 * n + 4]
        h_scr = rest[-1]
        m, j = pl.program_id(0), pl.program_id(1)
        finish = _hosted_gather((rest[:n], rest[2 * n + 4:3 * n + 4]) + tuple(rest[3 * n + 4:3 * n + 6]), n, nb,
                                m * steps + j, total)

        @pl.when(j == 0)
        def _():
            xv = x_ref[...]
            hv = (xv * _rstd(xv) * g_ref[...]).astype(BF16)
            h_scr[...] = hv
            h_ref[...] = hv

        h = h_scr[...]

        @pl.when(j < na)
        def _():
            z_ref[...] = lax.dot_general(h, wa_ref[...], _NN, preferred_element_type=F32)

        @pl.when((j >= na) & (j < na + nq))
        def _():
            qkv_ref[...] = lax.dot_general(h, wb_ref[...], _NN, preferred_element_type=F32).astype(BF16)

        @pl.when(j == na + nq)
        def _():
            f_ref[...] = lax.dot_general(h, wc_ref[...], _NN, preferred_element_type=F32)

        finish()

    blk_a = lambda m, j: (m, jnp.minimum(j, na - 1))
    blk_b = lambda m, j: (m, jnp.clip(j - na, 0, nq - 1))
    outs = pl.pallas_call(
        body, name="in_proj", grid=(s // tm, steps),
        in_specs=[pl.BlockSpec((tm, d), lambda m, j: (m, 0)), pl.BlockSpec((1, d), lambda m, j: (0, 0)),
                  pl.BlockSpec((d, tn), lambda m, j: (0, jnp.minimum(j, na - 1))),
                  pl.BlockSpec((d, tn), lambda m, j: (0, jnp.clip(j - na, 0, nq - 1))),
                  pl.BlockSpec((d, LANES), lambda m, j: (0, 0))] + _hbm_specs(2 * n),
        out_specs=[pl.BlockSpec((tm, tn), blk_a), pl.BlockSpec((tm, tn), blk_b),
                   pl.BlockSpec((tm, LANES), lambda m, j: (m, 0)), pl.BlockSpec((tm, d), lambda m, j: (m, 0))]
        + _hbm_specs(n),
        out_shape=[jax.ShapeDtypeStruct((s, w_a.shape[1]), F32), jax.ShapeDtypeStruct((s, w_b.shape[1]), BF16),
                   jax.ShapeDtypeStruct((s, LANES), F32), jax.ShapeDtypeStruct((s, d), BF16)]
        + [jax.ShapeDtypeStruct(b.shape, b.dtype) for b in landing],
        input_output_aliases={5 + n + k: 4 + k for k in range(n)},
        scratch_shapes=sems + [pltpu.VMEM((tm, d), BF16)],
        compiler_params=_cp(("arbitrary", "arbitrary")))(x, g, w_a, w_b, w_c, *arrays, *landing)
    return outs[0], outs[1], outs[2], outs[3], outs[4:]


def _gather_operands(bigs, smalls):
    x, y, _ = _place()
    arrays = list(bigs) + list(smalls)
    landing = [lax.dynamic_update_index_in_dim(lax.empty((N_CHIPS,) + a.shape, a.dtype), a, 2 * x + y, 0)
               for a in arrays]
    n_sems = 6 * len(bigs) + 3 * len(smalls)
    return arrays, landing, [pltpu.SemaphoreType.DMA((n_sems,)), pltpu.SemaphoreType.DMA((n_sems,))]


def _gather_phases(ins, outs, nb, send_sems, recv_sems):
    n = len(ins)
    x, y, c = _place()
    my_chip = 2 * x + y
    chips = _other_chips(x, y)
    sibling = (x, y, 1 - c)

    def rows(k, which):
        h = ins[k].shape[0] // 2
        return pl.ds(which * h, h)

    def copy(sem, src, dst, to):
        return pltpu.make_async_remote_copy(src_ref=src, dst_ref=dst, send_sem=send_sems.at[sem],
                                            recv_sem=recv_sems.at[sem], device_id=to, device_id_type=MESH)

    def sends():
        out = [copy(6 * k + j, ins[k].at[rows(k, c)], outs[k].at[my_chip, rows(k, c)], (cx, cy, c))
               for k in range(nb) for j, (cx, cy) in enumerate(chips)]
        return out + [copy(6 * nb + 3 * (k - nb) + j, ins[k], outs[k].at[my_chip], (cx, cy, c))
                      for k in range(nb, n) for j, (cx, cy) in enumerate(chips)]

    def landed(k, j, which):
        cx, cy = chips[j]
        return outs[k].at[2 * cx + cy, rows(k, which)]

    def forwards():
        return [copy(6 * k + 3 + j, landed(k, j, c), landed(k, j, c), sibling)
                for j in range(3) for k in range(nb)]

    def start():
        for cp in sends():
            cp.start()

    def forward():
        for j in range(3):
            for k in range(nb):
                copy(6 * k + j, landed(k, j, c), landed(k, j, c), (x, y, c)).wait_recv()
                copy(6 * k + 3 + j, landed(k, j, c), landed(k, j, c), sibling).start()

    def finish():
        for j, (cx, cy) in enumerate(chips):
            for k in range(nb):
                copy(6 * k + 3 + j, landed(k, j, 1 - c), landed(k, j, 1 - c), (x, y, c)).wait_recv()
            for k in range(nb, n):
                arrived = outs[k].at[2 * cx + cy]
                copy(6 * nb + 3 * (k - nb) + j, arrived, arrived, (x, y, c)).wait_recv()
        for cp in sends() + forwards():
            cp.wait_send()

    return start, forward, finish


def _pair_exchange(views, name):
    n = len(views)

    def body(*refs):
        ins, outs, send_sems, recv_sems = refs[:n], refs[n:2 * n], refs[2 * n], refs[2 * n + 1]
        x, y, c = _place()
        copies = [pltpu.make_async_remote_copy(
            src_ref=ins[k].at[:, pl.ds(1 - c, 1)], dst_ref=outs[k], send_sem=send_sems.at[k],
            recv_sem=recv_sems.at[k], device_id=(x, y, 1 - c), device_id_type=MESH) for k in range(n)]
        for cp in copies:
            cp.start()
        for cp in copies:
            cp.wait()

    return pl.pallas_call(
        body, name=name,
        out_shape=[jax.ShapeDtypeStruct((v.shape[0], 1) + v.shape[2:], v.dtype) for v in views],
        in_specs=_hbm_specs(n), out_specs=_hbm_specs(n),
        scratch_shapes=[pltpu.SemaphoreType.DMA((n,)), pltpu.SemaphoreType.DMA((n,))])(*views)


def _scatter_operands(parts):
    n = len(parts)
    return ([jax.ShapeDtypeStruct((3,) + p.shape[1:], p.dtype) for p in parts],
            [pltpu.SemaphoreType.DMA((3 * n,)), pltpu.SemaphoreType.DMA((3 * n,))])


def _scatter_phases(ins, outs, send_sems, recv_sems):
    x, y, c = _place()

    def copies():
        return [pltpu.make_async_remote_copy(
            src_ref=ins[k].at[pl.ds(2 * cx + cy, 1)], dst_ref=outs[k].at[pl.ds(r, 1)], send_sem=send_sems.at[3 * k + r],
            recv_sem=recv_sems.at[3 * k + r], device_id=(cx, cy, c), device_id_type=MESH)
            for k in range(len(ins)) for r, (cx, cy) in enumerate(_other_chips(x, y))]

    def start():
        for cp in copies():
            cp.start()

    def finish():
        for cp in copies():
            cp.wait()

    return start, finish


def _hosted_scatter(ins, outs, sems, step, total):
    start, finish = _scatter_phases(ins, outs, *sems)
    pl.when(step == 0)(start)
    return lambda: pl.when(step == total - 1)(finish)


def _join_halves(shards):
    n = len(shards)

    def body(*refs):
        ins, outs, send_sems, recv_sems = refs[:n], refs[n:2 * n], refs[2 * n], refs[2 * n + 1]
        x, y, c = _place()

        def rows(ref, which):
            h = ref.shape[0] // 2
            return ref.at[pl.ds(which * h, h)]

        sent = [pltpu.make_async_remote_copy(
            src_ref=rows(ins[k], c), dst_ref=rows(outs[k], c), send_sem=send_sems.at[k], recv_sem=recv_sems.at[k],
            device_id=(x, y, 1 - c), device_id_type=MESH) for k in range(n)]
        for cp in sent:
            cp.start()
        for k in range(n):
            pltpu.make_async_remote_copy(
                src_ref=rows(ins[k], 1 - c), dst_ref=rows(outs[k], 1 - c), send_sem=send_sems.at[k],
                recv_sem=recv_sems.at[k], device_id=(x, y, 1 - c), device_id_type=MESH).wait_recv()
        for cp in sent:
            cp.wait_send()

    return pl.pallas_call(
        body, name="half_exchange", out_shape=[jax.ShapeDtypeStruct(a.shape, a.dtype) for a in shards],
        in_specs=_hbm_specs(n), out_specs=_hbm_specs(n), input_output_aliases={k: k for k in range(n)},
        scratch_shapes=[pltpu.SemaphoreType.DMA((n,)), pltpu.SemaphoreType.DMA((n,))])(*shards)


def _all_reduce_small(packet):
    rows, width = packet.shape
    n_dev = 8

    def body(x_ref, out_ref, gath, send_sems, recv_sems):
        x, y, c = _place()
        me, sibling = (x, y, c), (x, y, 1 - c)
        chips = _other_chips(x, y)

        def slot(px, py, pc):
            return gath.at[pl.ds((4 * px + 2 * py + pc) * rows, rows), :]

        def copy(k, block, to, src=None):
            return pltpu.make_async_remote_copy(
                src_ref=slot(*block) if src is None else src, dst_ref=slot(*block), send_sem=send_sems.at[k],
                recv_sem=recv_sems.at[k], device_id=to, device_id_type=MESH)

        first = [copy(0, me, sibling, src=x_ref)]
        first += [copy(1 + j, me, (*chip, c), src=x_ref) for j, chip in enumerate(chips)]
        for cp in first:
            cp.start()
        gath[pl.ds((4 * x + 2 * y + c) * rows, rows), :] = x_ref[...]
        passed = [copy(4 + j, (*chip, c), sibling) for j, chip in enumerate(chips)]
        for j, chip in enumerate(chips):
            copy(1 + j, (*chip, c), me).wait_recv()
            passed[j].start()
        copy(0, sibling, me).wait_recv()
        for j, chip in enumerate(chips):
            copy(4 + j, (*chip, 1 - c), me).wait_recv()
        for cp in first + passed:
            cp.wait_send()
        acc = gath[0:rows, :]
        for d in range(1, n_dev):
            acc = acc + gath[d * rows:(d + 1) * rows, :]
        out_ref[...] = acc

    return pl.pallas_call(
        body, name="all_reduce_small", out_shape=jax.ShapeDtypeStruct((rows, width), F32),
        in_specs=[pl.BlockSpec(memory_space=pltpu.VMEM)], out_specs=pl.BlockSpec(memory_space=pltpu.VMEM),
        scratch_shapes=[pltpu.VMEM((n_dev * rows, width), F32), pltpu.SemaphoreType.DMA((7,)),
                        pltpu.SemaphoreType.DMA((7,))])(packet)


def _flat_rows(parts, width, row_multiple):
    flat = jnp.concatenate([p.astype(F32).reshape(-1) for p in parts])
    rows = -(-flat.shape[0] // width)
    rows = -(-rows // row_multiple) * row_multiple
    return jnp.pad(flat, (0, rows * width - flat.shape[0])).reshape(rows, width)


def _unflatten(flat2d, shapes):
    flat = flat2d.reshape(-1)
    out, off = [], 0
    for shp in shapes:
        n = 1
        for dim in shp:
            n *= dim
        out.append(flat[off:off + n].reshape(shp))
        off += n
    return out


def _core_and_chip():
    x, y, c = _place()
    return jnp.stack([c, 2 * x + y]).astype(jnp.int32)


def _pair_sums(chip_major, names, call_name):
    views = [g.reshape(N_CHIPS, 2, g.shape[1] // 2, g.shape[2]) for g in chip_major]
    recv = _pair_exchange(views, call_name)
    sel = _core_and_chip()
    return [_pair_sum(v, r, sel, "pair_sum_" + nm) for v, r, nm in zip(views, recv, names)]


def _finish_grads(pairs, got, names):
    sel = _core_and_chip()
    return _join_halves([_chip_sum(p, g, sel, "chip_sum_" + nm) for (p, _), g, nm in zip(pairs, got, names)])


def kernel(x, g_mix, w_in, b_f, w_conv, g_conv_out, g_attn_out, w_o, g_ffn, w_up, w_ffn_conv, w_down, g_final, loss_target, m_g_mix, m_w_in, m_b_f, m_w_conv, m_g_conv_out, m_g_attn_out, m_w_o, m_g_ffn, m_w_up, m_w_ffn_conv, m_w_down, m_g_final, v_g_mix, v_w_in, v_b_f, v_w_conv, v_g_conv_out, v_g_attn_out, v_w_o, v_g_ffn, v_w_up, v_w_ffn_conv, v_w_down, v_g_final):
    s = x.shape[1]
    x0 = x[0]
    target = loss_target[0]
    d = D_MODEL
    x_pos, y_pos, _ = _place()
    my_chip = 2 * x_pos + y_pos

    (c_in,) = _all_gather_weights([w_in[0].astype(BF16)], [])
    w_in_full = jnp.concatenate([c_in[j] for j in range(N_CHIPS)], axis=1)
    c3 = 3 * CONV_CH
    w_a, w_b = w_in_full[:, :c3], w_in_full[:, c3:c3 + 3 * ATTN_W]
    w_c = jnp.pad(w_in_full[:, c3 + 3 * ATTN_W:], ((0, 0), (0, LANES - N_HEADS)))
    w_q, w_k, w_v = (w_b[:, i * ATTN_W:(i + 1) * ATTN_W] for i in range(3))
    b_pad = jnp.pad(b_f, ((0, 0), (0, LANES - N_HEADS)))

    z_a, qkv, f_log, h1, (c_o, c_up, c_conv, c_ffn) = _in_proj(
        x0, g_mix, w_a, w_b, w_c, [w_o[0].astype(BF16), w_up[0].astype(BF16)], [w_conv[0], w_ffn_conv[0]])
    fb = _gate_fwd(f_log, b_pad)
    qx, kx, kxt, vx, vt, bounds, (c_down,) = _attn_prep(qkv, fb, [w_down[0].astype(BF16)])
    w_o_full = c_o.reshape(d, d)
    w_down_full = c_down.reshape(D_FF, d)
    w_conv_full = jnp.concatenate([c_conv[j] for j in range(N_CHIPS)], axis=1)
    w_ffn_full = jnp.concatenate([c_ffn[j] for j in range(N_CHIPS)], axis=1)
    n_up = c_up.shape[2]
    first_blk, last_blk = _key_block_ranges(bounds)
    o_attn, lse = _attn_fwd_t(qx, kx, vt, first_blk)
    mix = _mixer_fwd(z_a, o_attn, w_conv_full, g_conv_out, g_attn_out)
    x2, h2 = _mm("nn", [mix], [w_o_full], F32, 512, d, "out_proj", add=x0, norm_g=g_ffn)
    up = _mm("nn", [h2], [c_up], BF16, TM_MM, n_up, "up_proj", b_chips=True)
    act, u_conv = _ffn_act_fwd(up, w_ffn_full)
    dx3, dx3_b, loss_row, gg_final = _down_proj_loss(act, w_down_full, x2, target, g_final.reshape(1, d))

    dact = _mm("nt", [dx3_b], [w_down_full], F32, TM_MM, 1408, "d_act")
    gw_down = _mm_tn(act, dx3_b, 1408, 1024, "gw_down")
    dup, gwf_lin, gwf_gate = _ffn_act_bwd(up, u_conv, dact, w_ffn_full)
    dh2 = _mm("nt", [(dup, j // 2, j % 2, n_up) for j in range(N_CHIPS)], [(c_up, j) for j in range(N_CHIPS)],
              F32, TM_MM, 512, "d_h2")
    gw_up = _mm_tn(h2, dup, 1024, n_up, "gw_up", out_chips=True)
    dx2, dx2_b, gg_ffn = _rms_bwd(x2, dh2, g_ffn, dx3, "rms_ffn_bwd", True)
    dmix = _mm("nt", [dx2_b], [w_o_full], F32, TM_MM, 512, "d_mix")
    gw_o = _mm_tn(mix, dx2_b, 1024, 1024, "gw_o")
    early = _pair_sums([gw_o.reshape(N_CHIPS, d // N_CHIPS, d), gw_up, gw_down.reshape(N_CHIPS, D_FF // N_CHIPS, d)],
                       ["w_o", "w_up", "w_down"], "pair_exchange")
    dz_a, dox, gw_conv, gg_conv_out, gg_attn_out, (got_o, got_down) = _mixer_bwd(
        z_a, o_attn, dmix, w_conv_full, g_conv_out, g_attn_out, [early[0][1], early[2][1]])
    dk, dv, dfk, dqt, dfq = _attn_bwd_t(qx, dox, kx, kxt, vx, lse, last_blk)
    dq = _attn_dq_finish(dqt)
    d_f = (jnp.transpose(dfk[:, :, 0:2, :], (1, 3, 0, 2)).reshape(s, N_HEADS)
           + jnp.transpose(jnp.sum(dfq, axis=2), (1, 2, 0)).reshape(s, N_HEADS))
    df_b, gb_f = _gate_bwd(f_log, b_pad, jnp.pad(d_f, ((0, 0), (0, LANES - N_HEADS))))
    gw_a = _mm_tn(h1, dz_a, 1024, c3, "gw_in_conv")
    gw_q = _mm_tn(h1, dq, 1024, ATTN_W, "gw_in_q")
    gw_k = _mm_tn(h1, dk, 1024, ATTN_W, "gw_in_k")
    gw_v = _mm_tn(h1, dv, 1024, ATTN_W, "gw_in_v")
    gw_c = _mm_tn(h1, df_b, 1024, LANES, "gw_in_gate")
    gw_in = jnp.concatenate([gw_a, gw_q, gw_k, gw_v, gw_c[:, :N_HEADS]], axis=1)
    n_in = IN_COLS // N_CHIPS
    gw_in = jnp.stack([gw_in[:, j * n_in:(j + 1) * n_in] for j in range(N_CHIPS)])
    late = _pair_sums([gw_in], ["w_in"], "pair_exchange_w_in")
    dh1, (got_up, got_in) = _mm("nt", [dz_a, dq, dk, dv, df_b], [w_a, w_q, w_k, w_v, w_c], F32, TM_MM, 512, "d_h1",
                                scatter=[early[1][1], late[0][1]])
    grad_x, gg_mix = _rms_bwd(x0, dh1, g_mix, dx2, "rms_mix_bwd", False)
    g_w_in, g_w_o, g_w_up, g_w_down = _finish_grads(late + early, [got_in, got_o, got_up, got_down],
                                                    ["w_in", "w_o", "w_up", "w_down"])

    gw_ffn = jnp.concatenate([gwf_lin, gwf_gate], axis=1)
    small_parts = [gg_mix, gg_conv_out, gg_attn_out, gg_ffn, gg_final, gb_f[:, :N_HEADS], loss_row[:, 0:1], gw_conv,
                   gw_ffn]
    small_shapes = [a.shape for a in small_parts]
    tot = _unflatten(_all_reduce_small(_flat_rows(small_parts, d, 8)), small_shapes)
    g_g_mix, g_g_conv_out, g_g_attn_out, g_g_ffn, g_g_final, g_b_f, loss_sum, g_conv_full, g_ffn_full = tot
    loss = loss_sum[0, 0]
    g_g_final = g_g_final[0]
    g_w_conv = lax.dynamic_slice_in_dim(g_conv_full, my_chip * (CONV_CH // N_CHIPS), CONV_CH // N_CHIPS, axis=1)
    g_w_ffn = lax.dynamic_slice_in_dim(g_ffn_full, my_chip * n_up, n_up, axis=1)

    swap = lambda a: jnp.swapaxes(a, -1, -2)
    u_w_in = tuple(swap(a) for a in _adamw(swap(w_in), swap(g_w_in), swap(m_w_in), swap(v_w_in), "adam_w_in"))
    u_w_o = _adamw(w_o, g_w_o, m_w_o, v_w_o, "adam_w_o")
    u_w_up = _adamw(w_up, g_w_up, m_w_up, v_w_up, "adam_w_up")
    u_w_down = _adamw(w_down, g_w_down, m_w_down, v_w_down, "adam_w_down")

    small_w = [g_mix, b_f, g_conv_out, g_attn_out, g_ffn, g_final, w_conv, w_ffn_conv]
    small_g = [g_g_mix, g_b_f, g_g_conv_out, g_g_attn_out, g_g_ffn, g_g_final, g_w_conv, g_w_ffn]
    small_m = [m_g_mix, m_b_f, m_g_conv_out, m_g_attn_out, m_g_ffn, m_g_final, m_w_conv, m_w_ffn_conv]
    small_v = [v_g_mix, v_b_f, v_g_conv_out, v_g_attn_out, v_g_ffn, v_g_final, v_w_conv, v_w_ffn_conv]
    shapes = [a.shape for a in small_w]
    pack = lambda arrs: _flat_rows(arrs, LANES, 8)
    sd, sm, sv = _adamw(pack(small_w), pack(small_g), pack(small_m), pack(small_v), "adam_small")
    sd, sm, sv = _unflatten(sd, shapes), _unflatten(sm, shapes), _unflatten(sv, shapes)
    (d_g_mix, d_b_f, d_g_conv_out, d_g_attn_out, d_g_ffn, d_g_final, d_w_conv, d_w_ffn) = sd
    (nm_g_mix, nm_b_f, nm_g_conv_out, nm_g_attn_out, nm_g_ffn, nm_g_final, nm_w_conv, nm_w_ffn) = sm
    (nv_g_mix, nv_b_f, nv_g_conv_out, nv_g_attn_out, nv_g_ffn, nv_g_final, nv_w_conv, nv_w_ffn) = sv

    grads = (g_g_mix, g_w_in[None], g_b_f, g_w_conv[None], g_g_conv_out, g_g_attn_out, g_w_o[None], g_g_ffn,
             g_w_up[None], g_w_ffn[None], g_w_down[None], g_g_final)
    deltas = (d_g_mix, u_w_in[0], d_b_f, d_w_conv, d_g_conv_out, d_g_attn_out, u_w_o[0], d_g_ffn, u_w_up[0],
              d_w_ffn, u_w_down[0], d_g_final)
    new_m = (nm_g_mix, u_w_in[1], nm_b_f, nm_w_conv, nm_g_conv_out, nm_g_attn_out, u_w_o[1], nm_g_ffn, u_w_up[1],
             nm_w_ffn, u_w_down[1], nm_g_final)
    new_v = (nv_g_mix, u_w_in[2], nv_b_f, nv_w_conv, nv_g_conv_out, nv_g_attn_out, u_w_o[2], nv_g_ffn, u_w_up[2],
             nv_w_ffn, u_w_down[2], nv_g_final)
    return (loss, grad_x[None], *grads, *deltas, *new_m, *new_v)
```

```python
import jax
import jax.numpy as jnp
from jax import lax
from jax.experimental import pallas as pl
from jax.experimental.pallas import tpu as pltpu

F32, BF16 = jnp.float32, jnp.bfloat16
MESH = pl.DeviceIdType.MESH

D_MODEL = 1024
CONV_CH = 512
ATTN_W = 512
N_HEADS = 8
HEAD_DIM = 64
N_PAIRS = N_HEADS // 2
D_FF = 2816
IN_COLS = 3 * CONV_CH + 3 * ATTN_W + N_HEADS
EPS = 1e-6
Q_SCALE = 0.125
EXP_ZERO = 88.0
N_CHIPS = 4
LANES = 128
HALO = 8
HALO_BF16 = 2 * HALO

ADAM_LR, ADAM_B1, ADAM_B2, ADAM_EPS, ADAM_WD, ADAM_STEP = 0.001, 0.9, 0.999, 1e-08, 0.01, 10

TM_ROWS = 512
TM_MM = 1024
TK_TN = 1024
TQ = 512
TM_FFN = 256
TN_FFN = 1408
VMEM_LIMIT = 52 * 2**20


def _cp(sem, vmem=VMEM_LIMIT):
    return pltpu.CompilerParams(dimension_semantics=sem, vmem_limit_bytes=vmem)


def _bf(a):
    return a if a.dtype == BF16 else a.astype(BF16)


def _mm(mode, a_list, b_list, out_dtype, tm, tn, name, add=None, b_chips=False, scatter=(), norm_g=None):
    n_p = len(a_list)
    a0 = a_list[0]
    m_dim = a0[0].shape[1] if isinstance(a0, tuple) else a0.shape[0]
    b0 = b_list[0]
    if b_chips:
        n_dim = b0.shape[0] * b0.shape[2]
        assert tn == b0.shape[2] and mode == "nn"
    else:
        b0 = b0[0][b0[1]] if isinstance(b0, tuple) else b0
        n_dim = b0.shape[1 if mode == "nn" else 0]
    tm, tn = min(tm, m_dim), min(tn, n_dim)
    assert m_dim % tm == 0 and n_dim % tn == 0
    dims = (((1,), (0,)), ((), ())) if mode == "nn" else (((1,), (1,)), ((), ()))
    in_specs, args = [], []
    for a in a_list:
        if isinstance(a, tuple):
            arr, lead, col, width = a
            in_specs.append(pl.BlockSpec((None, tm, width), lambda m, n, lead=lead, col=col: (lead, m, col)))
        else:
            arr = a
            in_specs.append(pl.BlockSpec((tm, a.shape[1]), lambda m, n: (m, 0)))
        args.append(arr)
    for b in b_list:
        if b_chips:
            arr = b
            in_specs.append(pl.BlockSpec((None, b.shape[1], tn), lambda m, n: (n, 0, 0)))
        elif isinstance(b, tuple):
            arr, lead = b
            if mode == "nn":
                in_specs.append(pl.BlockSpec((None, arr.shape[1], tn), lambda m, n, lead=lead: (lead, 0, n)))
            else:
                in_specs.append(pl.BlockSpec((None, tn, arr.shape[2]), lambda m, n, lead=lead: (lead, n, 0)))
        elif mode == "nn":
            arr = b
            in_specs.append(pl.BlockSpec((b.shape[0], tn), lambda m, n: (0, n)))
        else:
            arr = b
            in_specs.append(pl.BlockSpec((tn, b.shape[1]), lambda m, n: (n, 0)))
        args.append(arr)
    if add is not None:
        in_specs.append(pl.BlockSpec((tm, tn), lambda m, n: (m, n)))
        args.append(add)
    if norm_g is not None:
        assert tn == n_dim and not scatter
        in_specs.append(pl.BlockSpec((1, tn), lambda m, n: (0, 0)))
        args.append(norm_g)

    n_in = len(args)
    n_sc = len(scatter)
    grid = (m_dim // tm, n_dim // tn)

    def body(*refs):
        o_ref = refs[n_in + n_sc]
        if n_sc:
            finish = _hosted_scatter(refs[n_in:n_in + n_sc], refs[n_in + n_sc + 1:n_in + 2 * n_sc + 1],
                                     refs[n_in + 2 * n_sc + 1:], pl.program_id(0) * grid[1] + pl.program_id(1),
                                     grid[0] * grid[1])
        acc = None
        for i in range(n_p):
            d = lax.dot_general(_bf(refs[i][...]), _bf(refs[n_p + i][...]), dims,
                                preferred_element_type=F32)
            acc = d if acc is None else acc + d
        if add is not None:
            acc = refs[2 * n_p][...] + acc
        o_ref[...] = acc.astype(out_dtype)
        if norm_g is not None:
            refs[n_in + 1][...] = (acc * _rstd(acc) * refs[n_in - 1][...]).astype(BF16)
        if n_sc:
            finish()

    main_spec = pl.BlockSpec((tm, tn), lambda m, n: (m, n))
    main_shape = jax.ShapeDtypeStruct((m_dim, n_dim), out_dtype)
    if norm_g is not None:
        return pl.pallas_call(body, name=name, grid=grid, in_specs=in_specs, out_specs=[main_spec, main_spec],
                              out_shape=[main_shape, jax.ShapeDtypeStruct((m_dim, n_dim), BF16)],
                              compiler_params=_cp(("parallel", "parallel")))(*args)
    if not n_sc:
        return pl.pallas_call(body, name=name, grid=grid, in_specs=in_specs, out_specs=main_spec,
                              out_shape=main_shape, compiler_params=_cp(("parallel", "parallel")))(*args)
    got_shapes, sems = _scatter_operands(scatter)
    outs = pl.pallas_call(
        body, name=name, grid=grid, in_specs=in_specs + _hbm_specs(n_sc), out_specs=[main_spec] + _hbm_specs(n_sc),
        out_shape=[main_shape] + got_shapes, scratch_shapes=sems,
        compiler_params=_cp(("arbitrary", "arbitrary")))(*args, *scatter)
    return outs[0], outs[1:]


def _mm_tn(a, b, tm, tn, name, out_chips=False):
    k_dim, m_dim = a.shape
    n_dim = b.shape[-1] * (b.shape[0] if b.ndim == 3 else 1)
    tm, tn, tk = min(tm, m_dim), min(tn, b.shape[-1]), min(TK_TN, k_dim)
    assert m_dim % tm == 0 and b.shape[-1] % tn == 0 and k_dim % tk == 0
    per = b.shape[-1] // tn
    if b.ndim == 3:
        b_spec = pl.BlockSpec((None, tk, tn), lambda m, n, k: (n // per, k, n % per))
    else:
        b_spec = pl.BlockSpec((tk, tn), lambda m, n, k: (k, n))

    def body(a_ref, b_ref, o_ref):
        @pl.when(pl.program_id(2) == 0)
        def _():
            o_ref[...] = jnp.zeros_like(o_ref)
        o_ref[...] += lax.dot_general(_bf(a_ref[...]), _bf(b_ref[...]), (((0,), (0,)), ((), ())),
                                      preferred_element_type=F32)

    return pl.pallas_call(
        body, name=name, grid=(m_dim // tm, n_dim // tn, k_dim // tk),
        in_specs=[pl.BlockSpec((tk, tm), lambda m, n, k: (k, m)), b_spec],
        out_specs=(pl.BlockSpec((None, tm, tn), lambda m, n, k: (n, m, 0)) if out_chips
                   else pl.BlockSpec((tm, tn), lambda m, n, k: (m, n))),
        out_shape=jax.ShapeDtypeStruct((n_dim // tn, m_dim, tn) if out_chips else (m_dim, n_dim), F32),
        compiler_params=_cp(("parallel", "parallel", "arbitrary")))(a, b)


def _rstd(x):
    return lax.rsqrt(jnp.mean(x * x, axis=-1, keepdims=True) + EPS)


def _rms_bwd(x, dh, g, dres, name, with_bf16, scatter=()):
    s, d = x.shape
    tm = min(TM_ROWS, s)
    n_sc = len(scatter)
    n_out = 3 if with_bf16 else 2
    got_shapes, sems = _scatter_operands(scatter) if n_sc else ([], [])

    def body(x_ref, dh_ref, g_ref, dres_ref, *rest):
        dx_ref, gg_ref = rest[n_sc], rest[n_sc + n_out - 1]
        i = pl.program_id(0)
        if n_sc:
            finish = _hosted_scatter(rest[:n_sc], rest[n_sc + n_out:2 * n_sc + n_out], rest[2 * n_sc + n_out:], i,
                                     s // tm)

        @pl.when(i == 0)
        def _():
            gg_ref[...] = jnp.zeros_like(gg_ref)

        xv = x_ref[...]
        xn = xv * _rstd(xv)
        dhv = dh_ref[...]
        gg_ref[...] += jnp.sum(dhv * xn, axis=0, keepdims=True)
        t = dhv * g_ref[...]
        dx = dres_ref[...] + _rstd(xv) * (t - xn * jnp.mean(t * xn, axis=-1, keepdims=True))
        dx_ref[...] = dx
        if with_bf16:
            rest[n_sc + 1][...] = dx.astype(BF16)
        if n_sc:
            finish()

    row = pl.BlockSpec((tm, d), lambda i: (i, 0))
    vec = pl.BlockSpec((1, d), lambda i: (0, 0))
    out_specs = [row] + ([row] if with_bf16 else []) + [vec] + _hbm_specs(n_sc)
    out_shape = ([jax.ShapeDtypeStruct((s, d), F32)] + ([jax.ShapeDtypeStruct((s, d), BF16)] if with_bf16 else [])
                 + [jax.ShapeDtypeStruct((1, d), F32)] + got_shapes)
    outs = pl.pallas_call(
        body, name=name, grid=(s // tm,), in_specs=[row, row, vec, row] + _hbm_specs(n_sc), out_specs=out_specs,
        out_shape=out_shape, scratch_shapes=sems, compiler_params=_cp(("arbitrary",)))(x, dh, g, dres, *scatter)
    return tuple(outs[:n_out]) + ((outs[n_out:],) if n_sc else ())


def _down_proj_loss(act, w_down, x2, target, g):
    s, d = x2.shape
    tm = min(TM_ROWS, s)
    k_dim = act.shape[1]

    def body(a_ref, w_ref, x_ref, t_ref, g_ref, dx_ref, dxb_ref, loss_ref, gg_ref):
        @pl.when(pl.program_id(0) == 0)
        def _():
            gg_ref[...] = jnp.zeros_like(gg_ref)
            loss_ref[...] = jnp.zeros_like(loss_ref)

        xv = x_ref[...] + lax.dot_general(a_ref[...], w_ref[...], (((1,), (0,)), ((), ())),
                                          preferred_element_type=F32)
        r = _rstd(xv)
        xn = xv * r
        gv = g_ref[...]
        err = xn * gv - t_ref[...]
        loss_ref[...] += 0.5 * jnp.sum(jnp.mean(err * err, axis=-1, keepdims=True), axis=0, keepdims=True)
        dy = err * (1.0 / d)
        gg_ref[...] += jnp.sum(dy * xn, axis=0, keepdims=True)
        t = dy * gv
        dx = r * (t - xn * jnp.mean(t * xn, axis=-1, keepdims=True))
        dx_ref[...] = dx
        dxb_ref[...] = dx.astype(BF16)

    row = pl.BlockSpec((tm, d), lambda i: (i, 0))
    vec = pl.BlockSpec((1, d), lambda i: (0, 0))
    return pl.pallas_call(
        body, name="down_proj_loss", grid=(s // tm,),
        in_specs=[pl.BlockSpec((tm, k_dim), lambda i: (i, 0)), pl.BlockSpec((k_dim, d), lambda i: (0, 0)), row, row,
                  vec],
        out_specs=[row, row, pl.BlockSpec((1, LANES), lambda i: (0, 0)), vec],
        out_shape=[jax.ShapeDtypeStruct((s, d), F32), jax.ShapeDtypeStruct((s, d), BF16),
                   jax.ShapeDtypeStruct((1, LANES), F32), jax.ShapeDtypeStruct((1, d), F32)],
        compiler_params=_cp(("arbitrary",)))(act, w_down, x2, target, g)


def _prev_halo_spec(tm, width, col):
    return pl.BlockSpec((HALO, width), lambda i, *_: (jnp.maximum(i * (tm // HALO) - 1, 0), col))


def _next_halo_spec(tm, width, col, s):
    return pl.BlockSpec((HALO, width), lambda i, *_: (jnp.minimum((i + 1) * (tm // HALO), s // HALO - 1), col))


def _shift_down(x, k):
    return pltpu.roll(x, k, 0)


def _shift_up(x, k):
    return pltpu.roll(x, x.shape[0] - k, 0)


def _conv_taps(x_ext, w):
    return w[0:1, :] * _shift_down(x_ext, 2) + w[1:2, :] * _shift_down(x_ext, 1) + w[2:3, :] * x_ext


def _conv_taps_t(d_ext, w):
    return w[2:3, :] * d_ext + w[1:2, :] * _shift_up(d_ext, 1) + w[0:1, :] * _shift_up(d_ext, 2)


def _mixer_fwd(z_a, o_attn, w_conv, g_conv_out, g_attn_out):
    s = z_a.shape[0]
    c = CONV_CH
    tm = min(TM_ROWS, s)

    def body(gb_ref, gc_ref, xc_ref, gcp_ref, xcp_ref, o_ref, w_ref, gco_ref, gao_ref, mix_ref):
        i = pl.program_id(0)
        cx = gc_ref[...] * xc_ref[...]
        cx_prev = jnp.where(i > 0, gcp_ref[...] * xcp_ref[...], 0.0)
        conv = _conv_taps(jnp.concatenate([cx_prev, cx], axis=0), w_ref[...])[HALO:]
        y = gb_ref[...] * conv
        mix_ref[:, 0:c] = (y * _rstd(y) * gco_ref[...]).astype(BF16)
        o = o_ref[...]
        mix_ref[:, c:2 * c] = (o * _rstd(o) * gao_ref[...]).astype(BF16)

    col = lambda j: pl.BlockSpec((tm, c), lambda i: (i, j))
    vec = pl.BlockSpec((1, c), lambda i: (0, 0))
    return pl.pallas_call(
        body, name="mixer_fwd", grid=(s // tm,),
        in_specs=[col(0), col(1), col(2), _prev_halo_spec(tm, c, 1), _prev_halo_spec(tm, c, 2), col(0),
                  pl.BlockSpec((3, c), lambda i: (0, 0)), vec, vec],
        out_specs=pl.BlockSpec((tm, 2 * c), lambda i: (i, 0)),
        out_shape=jax.ShapeDtypeStruct((s, 2 * c), BF16),
        compiler_params=_cp(("parallel",)))(z_a, z_a, z_a, z_a, z_a, o_attn, w_conv, g_conv_out, g_attn_out)


def _mixer_bwd(z_a, o_attn, dmix, w_conv, g_conv_out, g_attn_out, scatter):
    s = z_a.shape[0]
    c = CONV_CH
    tm = min(TM_ROWS, s)
    n_blk = s // tm
    n_sc = len(scatter)
    got_shapes, sems = _scatter_operands(scatter)

    def body(gb_ref, gc_ref, xc_ref, gcp_ref, xcp_ref, gbn_ref, gcn_ref, xcn_ref, o_ref, dnc_ref, dncn_ref, dna_ref,
             w_ref, gco_ref, gao_ref, *rest):
        dz_ref, dox_ref, gw_ref, ggco_ref, ggao_ref = rest[n_sc:n_sc + 5]
        i = pl.program_id(0)
        finish = _hosted_scatter(rest[:n_sc], rest[n_sc + 5:2 * n_sc + 5], rest[2 * n_sc + 5:], i, n_blk)

        @pl.when(i == 0)
        def _():
            gw_ref[...] = jnp.zeros_like(gw_ref)
            ggco_ref[...] = jnp.zeros_like(ggco_ref)
            ggao_ref[...] = jnp.zeros_like(ggao_ref)

        w = w_ref[...]
        zeros = jnp.zeros((HALO, c), F32)
        gb_e = jnp.concatenate([zeros, gb_ref[...], gbn_ref[...]], axis=0)
        cx_prev = jnp.where(i > 0, gcp_ref[...] * xcp_ref[...], 0.0)
        gc_e = jnp.concatenate([zeros, gc_ref[...], gcn_ref[...]], axis=0)
        xc_e = jnp.concatenate([zeros, xc_ref[...], xcn_ref[...]], axis=0)
        cx_e = jnp.concatenate([cx_prev, gc_ref[...] * xc_ref[...], gcn_ref[...] * xcn_ref[...]], axis=0)
        dn_next = jnp.where(i < n_blk - 1, dncn_ref[...], 0.0)
        dn_e = jnp.concatenate([zeros, dnc_ref[...], dn_next], axis=0)

        cx_1, cx_2 = _shift_down(cx_e, 1), _shift_down(cx_e, 2)
        conv_e = w[0:1, :] * cx_2 + w[1:2, :] * cx_1 + w[2:3, :] * cx_e
        y_e = gb_e * conv_e
        r_e = _rstd(y_e)
        yn_e = y_e * r_e
        t_e = dn_e * gco_ref[...]
        dy_e = r_e * (t_e - yn_e * jnp.mean(t_e * yn_e, axis=-1, keepdims=True))
        dconv_e = dy_e * gb_e
        dcx_e = _conv_taps_t(dconv_e, w)
        blk = slice(HALO, HALO + tm)
        dz_ref[:, 0:c] = (dy_e * conv_e)[blk].astype(BF16)
        dz_ref[:, c:2 * c] = (dcx_e * xc_e)[blk].astype(BF16)
        dz_ref[:, 2 * c:3 * c] = (dcx_e * gc_e)[blk].astype(BF16)
        ggco_ref[...] += jnp.sum((dn_e * yn_e)[blk], axis=0, keepdims=True)
        dconv = dconv_e[blk]
        gw_ref[0:1, :] += jnp.sum(dconv * cx_2[blk], axis=0, keepdims=True)
        gw_ref[1:2, :] += jnp.sum(dconv * cx_1[blk], axis=0, keepdims=True)
        gw_ref[2:3, :] += jnp.sum(dconv * cx_e[blk], axis=0, keepdims=True)

        o = o_ref[...]
        ra = _rstd(o)
        on = o * ra
        dna = dna_ref[...]
        ggao_ref[...] += jnp.sum(dna * on, axis=0, keepdims=True)
        ta = dna * gao_ref[...]
        do = ra * (ta - on * jnp.mean(ta * on, axis=-1, keepdims=True))
        prod = do * o
        lane = lax.broadcasted_iota(jnp.int32, (tm, LANES), 1)
        head_a = lane < HEAD_DIM
        for p in range(N_PAIRS):
            cols = slice(p * LANES, (p + 1) * LANES)
            pb, dob = prod[:, cols], do[:, cols]
            for hh in range(2):
                sel = head_a if hh == 0 else jnp.logical_not(head_a)
                delta = jnp.sum(jnp.where(sel, pb, 0.0), axis=-1, keepdims=True)
                neg3 = _split3(-delta)
                do_h = pltpu.roll(dob, HEAD_DIM, 1) if hh else dob
                dox_ref[2 * p + hh] = _aug(do_h, lane, neg3).astype(BF16)
        finish()

    col = lambda j: pl.BlockSpec((tm, c), lambda i: (i, j))
    vec = pl.BlockSpec((1, c), lambda i: (0, 0))
    w3 = pl.BlockSpec((3, c), lambda i: (0, 0))
    outs = pl.pallas_call(
        body, name="mixer_bwd", grid=(n_blk,),
        in_specs=[col(0), col(1), col(2), _prev_halo_spec(tm, c, 1), _prev_halo_spec(tm, c, 2),
                  _next_halo_spec(tm, c, 0, s), _next_halo_spec(tm, c, 1, s), _next_halo_spec(tm, c, 2, s),
                  col(0), col(0), _next_halo_spec(tm, c, 0, s), col(1), w3, vec, vec] + _hbm_specs(n_sc),
        out_specs=[pl.BlockSpec((tm, 3 * c), lambda i: (i, 0)),
                   pl.BlockSpec((N_HEADS, tm, LANES), lambda i: (0, i, 0)), w3, vec, vec] + _hbm_specs(n_sc),
        out_shape=[jax.ShapeDtypeStruct((s, 3 * c), BF16), jax.ShapeDtypeStruct((N_HEADS, s, LANES), BF16),
                   jax.ShapeDtypeStruct((3, c), F32), jax.ShapeDtypeStruct((1, c), F32),
                   jax.ShapeDtypeStruct((1, c), F32)] + got_shapes,
        scratch_shapes=sems, compiler_params=_cp(("arbitrary",)))(
            z_a, z_a, z_a, z_a, z_a, z_a, z_a, z_a, o_attn, dmix, dmix, dmix, w_conv, g_conv_out, g_attn_out,
            *scatter)
    return tuple(outs[:5]) + (outs[5:],)


def _gate_fwd(f, b_pad):
    s = f.shape[0]
    tm = min(TQ, s)

    def body(f_ref, b_ref, fb_ref, carry):
        @pl.when(pl.program_id(0) == 0)
        def _():
            carry[...] = jnp.zeros_like(carry)

        z = f_ref[...] + b_ref[...]
        x = jnp.minimum(z, 0.0) - jnp.log1p(jnp.exp(-jnp.abs(z)))
        row = lax.broadcasted_iota(jnp.int32, (tm, LANES), 0)
        sh = 1
        while sh < tm:
            x = x + jnp.where(row >= sh, _shift_down(x, sh), 0.0)
            sh *= 2
        x = x + carry[0:1, :]
        carry[...] = jnp.broadcast_to(x[tm - 1:tm, :], carry.shape)
        head_a = lax.broadcasted_iota(jnp.int32, (tm, LANES), 1) < HEAD_DIM
        for p in range(N_PAIRS):
            fa = jnp.broadcast_to(x[:, 2 * p:2 * p + 1], (tm, LANES))
            fbv = jnp.broadcast_to(x[:, 2 * p + 1:2 * p + 2], (tm, LANES))
            fb_ref[:, p * LANES:(p + 1) * LANES] = jnp.where(head_a, fa, fbv)

    return pl.pallas_call(
        body, name="gate_fwd", grid=(s // tm,),
        in_specs=[pl.BlockSpec((tm, LANES), lambda i: (i, 0)), pl.BlockSpec((1, LANES), lambda i: (0, 0))],
        out_specs=pl.BlockSpec((tm, N_PAIRS * LANES), lambda i: (i, 0)),
        out_shape=jax.ShapeDtypeStruct((s, N_PAIRS * LANES), F32),
        scratch_shapes=[pltpu.VMEM((HALO, LANES), F32)],
        compiler_params=_cp(("arbitrary",)))(f, b_pad)


def _gate_bwd(f, b_pad, d_f):
    s = f.shape[0]
    tm = min(TQ, s)
    n_blk = s // tm

    def body(f_ref, b_ref, d_ref, df_ref, gb_ref, carry):
        @pl.when(pl.program_id(0) == 0)
        def _():
            carry[...] = jnp.zeros_like(carry)
            gb_ref[...] = jnp.zeros_like(gb_ref)

        x = d_ref[...]
        row = lax.broadcasted_iota(jnp.int32, (tm, LANES), 0)
        sh = 1
        while sh < tm:
            x = x + jnp.where(row < tm - sh, _shift_up(x, sh), 0.0)
            sh *= 2
        x = x + carry[0:1, :]
        carry[...] = jnp.broadcast_to(x[0:1, :], carry.shape)
        z = f_ref[...] + b_ref[...]
        d = x * (1.0 / (1.0 + jnp.exp(z)))
        df_ref[...] = d.astype(BF16)
        gb_ref[...] += jnp.sum(d, axis=0, keepdims=True)

    rev = pl.BlockSpec((tm, LANES), lambda i: (n_blk - 1 - i, 0))
    vec = pl.BlockSpec((1, LANES), lambda i: (0, 0))
    return pl.pallas_call(
        body, name="gate_bwd", grid=(n_blk,), in_specs=[rev, vec, rev], out_specs=[rev, vec],
        out_shape=[jax.ShapeDtypeStruct((s, LANES), BF16), jax.ShapeDtypeStruct((1, LANES), F32)],
        scratch_shapes=[pltpu.VMEM((HALO, LANES), F32)],
        compiler_params=_cp(("arbitrary",)))(f, b_pad, d_f)


_NT = (((1,), (1,)), ((), ()))
_NN = (((1,), (0,)), ((), ()))


AUG = HEAD_DIM
NORM_MARGIN = 1.01


def _split3(x):
    hi = x.astype(BF16).astype(F32)
    r = x - hi
    mid = r.astype(BF16).astype(F32)
    lo = (r - mid).astype(BF16).astype(F32)
    return hi, mid, lo


def _aug(base, lane, vals):
    out = jnp.where(lane < AUG, base, 0.0)
    for k, v in enumerate(vals):
        out = jnp.where(lane == AUG + k, v, out)
    return out


def _attn_prep(qkv, fb, bigs):
    s = qkv.shape[0]
    tq = min(TQ, s)
    n_q = s // tq

    n = len(bigs)
    arrays, landing, sems = _gather_operands(bigs, [])

    def body(q_ref, k_ref, v_ref, fb_ref, *rest):
        qx_ref, kx_ref, kxt_ref, vx_ref, vt_ref, b_ref = rest[2 * n:2 * n + 6]
        finish = _hosted_gather((rest[:n], rest[2 * n + 6:3 * n + 6]) + tuple(rest[3 * n + 6:]), n, n,
                                pl.program_id(0), n_q)
        lane = lax.broadcasted_iota(jnp.int32, (tq, LANES), 1)
        lane8 = lax.broadcasted_iota(jnp.int32, (HALO, LANES), 1)
        head_lanes = lane < AUG
        is_lane = [lane == AUG + k for k in range(6)]
        first3 = (lane >= AUG) & (lane < AUG + 3)
        next3 = (lane >= AUG + 3) & (lane < AUG + 6)
        q_const = jnp.where(first3, -1.0, 0.0)
        k_const = jnp.where(next3, 1.0, 0.0)
        v_const = jnp.where(first3, 1.0, 0.0)
        ones_head = (lax.broadcasted_iota(jnp.int32, (LANES, LANES), 0) < HEAD_DIM).astype(BF16)
        acc = jnp.zeros((HALO, LANES), F32)
        for p in range(N_PAIRS):
            cols = slice(p * LANES, (p + 1) * LANES)
            q2, k2, v2 = (ref[:, cols].astype(F32) for ref in (q_ref, k_ref, v_ref))
            q2 = q2 * Q_SCALE
            f2 = fb_ref[:, cols]
            for hh in range(2):
                h = 2 * p + hh
                q, k, v = ((pltpu.roll(x, HEAD_DIM, 1) if hh else x) for x in (q2, k2, v2))
                f = f2 if hh else pltpu.roll(f2, HEAD_DIM, 1)
                hi, mid, lo = _split3(f)
                q_aug = jnp.where(is_lane[3], hi, jnp.where(is_lane[4], mid, jnp.where(is_lane[5], lo, q_const)))
                k_aug = jnp.where(is_lane[0], hi, jnp.where(is_lane[1], mid, jnp.where(is_lane[2], lo, k_const)))
                kx = jnp.where(head_lanes, k, k_aug)
                vx = jnp.where(head_lanes, v, v_const)
                qx_ref[h] = jnp.where(head_lanes, q, q_aug).astype(BF16)
                kx_ref[h] = kx.astype(BF16)
                vx_ref[h] = vx.astype(BF16)
                kxt_ref[h, 0] = kx.T.astype(BF16)
                vt_ref[h, 0] = vx.T.astype(BF16)
                q_sq = lax.dot_general((q * q).astype(BF16), ones_head, _NN, preferred_element_type=F32)
                k_sq = lax.dot_general((k * k).astype(BF16), ones_head, _NN, preferred_element_type=F32)
                diag = lax.dot_general((q * k).astype(BF16), ones_head, _NN, preferred_element_type=F32)
                diag = diag - jnp.sqrt(q_sq * k_sq) * (NORM_MARGIN - 1.0)
                vals = (jnp.sqrt(jnp.max(q_sq, axis=0, keepdims=True)), jnp.sqrt(jnp.max(k_sq, axis=0, keepdims=True)),
                        jnp.max(f - diag, axis=0, keepdims=True), f[tq - 1:tq, :])
                for slot, val in enumerate(vals):
                    acc = jnp.where(lane8 == slot * N_HEADS + h, val[:, AUG:AUG + 1], acc)
        b_ref[0] = acc
        finish()

    blk = lambda j: pl.BlockSpec((tq, ATTN_W), lambda i: (i, j))
    rows = pl.BlockSpec((N_HEADS, tq, LANES), lambda i: (0, i, 0))
    cols_t = pl.BlockSpec((N_HEADS, 1, LANES, tq), lambda i: (0, i, 0, 0))
    shp = jax.ShapeDtypeStruct((N_HEADS, s, LANES), BF16)
    shp_t = jax.ShapeDtypeStruct((N_HEADS, n_q, LANES, tq), BF16)
    outs = pl.pallas_call(
        body, name="attn_prep", grid=(n_q,), in_specs=[blk(0), blk(1), blk(2), blk(0)] + _hbm_specs(2 * n),
        out_specs=[rows, rows, cols_t, rows, cols_t,
                   pl.BlockSpec((1, HALO, LANES), lambda i: (i, 0, 0))] + _hbm_specs(n),
        out_shape=[shp, shp, shp_t, shp, shp_t, jax.ShapeDtypeStruct((n_q, HALO, LANES), F32)]
        + [jax.ShapeDtypeStruct(b.shape, b.dtype) for b in landing],
        input_output_aliases={4 + n + k: 6 + k for k in range(n)}, scratch_shapes=sems,
        compiler_params=_cp(("arbitrary",)))(qkv, qkv, qkv, fb, *arrays, *landing)
    return tuple(outs[:6]) + (outs[6:],)


def _key_block_ranges(bounds):
    t = bounds[:, 0, :]
    nh = N_HEADS
    a, b, c, e = t[:, 0:nh], t[:, nh:2 * nh], t[:, 2 * nh:3 * nh], t[:, 3 * nh:4 * nh]
    bound = a[:, None, :] * b[None, :, :] * NORM_MARGIN + c[:, None, :] - e[None, :, :]
    n_q = t.shape[0]
    idx = jnp.arange(n_q)
    need = jnp.logical_not(bound < -(EXP_ZERO + 2.0)) | (idx[None, :, None] >= idx[:, None, None])
    first = jnp.argmax(need, axis=1).astype(jnp.int32)
    first = jnp.min(first.reshape(n_q, N_PAIRS, 2), axis=-1)
    visits = (first[:, None, :] <= idx[None, :, None]) & (idx[None, :, None] <= idx[:, None, None])
    last = jnp.max(jnp.where(visits, idx[:, None, None], 0), axis=0).astype(jnp.int32)
    return first.T.reshape(-1), last.T.reshape(-1)


def _attn_fwd_t(qx, kx, vt, first_blk):
    _, s, _ = qx.shape
    tq = min(TQ, s)
    n_q = s // tq
    neg = -1e30

    def body(first_ref, qx_ref, kx_ref, vt_ref, o_ref, lse_ref, acc_ref, m_ref, s_even, s_odd):
        p = pl.program_id(0)
        i = pl.program_id(1)
        acc_ref[...] = jnp.zeros(acc_ref.shape, F32)
        m_ref[...] = jnp.full(m_ref.shape, neg, F32)
        key_le_query = (lax.broadcasted_iota(jnp.int32, (tq, tq), 0) <= lax.broadcasted_iota(jnp.int32, (tq, tq), 1))
        first = first_ref[p * n_q + i]

        def scores(kb, hh, dst):
            rows_k = pl.ds(pl.multiple_of(kb * tq, tq), tq)
            dst[hh] = lax.dot_general(kx_ref[hh, rows_k, :], qx_ref[hh], _NT, preferred_element_type=F32)

        def step(kb, src, nxt):
            for hh in range(2):
                st = src[hh]
                if nxt is None:
                    st = jnp.where(key_le_query, st, -jnp.inf)
                else:
                    scores(kb + 1, hh, nxt)
                m_old = m_ref[hh]
                m_new = jnp.maximum(m_old, jnp.max(st, axis=0, keepdims=True))
                m_ref[hh] = m_new
                pt = jnp.exp(st - m_new).astype(BF16)
                acc_ref[hh] = acc_ref[hh] * jnp.exp(m_old - m_new) + lax.dot_general(
                    vt_ref[hh, kb], pt, _NN, preferred_element_type=F32)

        def by_parity(kb, fn):
            @pl.when(kb % 2 == 0)
            def _():
                fn(s_even, s_odd)

            @pl.when(kb % 2 == 1)
            def _():
                fn(s_odd, s_even)

        def first_scores(src, nxt):
            scores(first, 0, src)
            scores(first, 1, src)

        def unmasked(kb, carry):
            by_parity(kb, lambda src, nxt: step(kb, src, nxt))
            return carry

        by_parity(first, first_scores)
        lax.fori_loop(first, i, unmasked, 0)
        by_parity(i, lambda src, nxt: step(i, src, None))
        outs, lses = [], []
        for hh in range(2):
            acc = acc_ref[hh]
            l = acc[AUG:AUG + 1, :]
            outs.append(acc[0:HEAD_DIM, :] / l)
            lses.append(m_ref[hh] + jnp.log(l))
        o_ref[...] = jnp.concatenate(outs, axis=0).T
        rows8 = lax.broadcasted_iota(jnp.int32, (N_HEADS, tq), 0)
        lse_ref[0, 0] = jnp.where(rows8 == 0, lses[0], jnp.where(rows8 == 1, lses[1], 0.0))

    grid_spec = pltpu.PrefetchScalarGridSpec(
        num_scalar_prefetch=1, grid=(N_PAIRS, n_q),
        in_specs=[pl.BlockSpec((2, tq, LANES), lambda p, i, first: (p, i, 0)),
                  pl.BlockSpec((2, s, LANES), lambda p, i, first: (p, 0, 0)),
                  pl.BlockSpec((2, n_q, LANES, tq), lambda p, i, first: (p, 0, 0, 0))],
        out_specs=[pl.BlockSpec((tq, LANES), lambda p, i, first: (i, p)),
                   pl.BlockSpec((1, 1, N_HEADS, tq), lambda p, i, first: (p, i, 0, 0))],
        scratch_shapes=[pltpu.VMEM((2, LANES, tq), F32), pltpu.VMEM((2, 1, tq), F32),
                        pltpu.VMEM((2, tq, tq), F32), pltpu.VMEM((2, tq, tq), F32)])
    return pl.pallas_call(
        body, name="attn_fwd", grid_spec=grid_spec,
        out_shape=[jax.ShapeDtypeStruct((s, ATTN_W), F32), jax.ShapeDtypeStruct((N_PAIRS, n_q, N_HEADS, tq), F32)],
        compiler_params=_cp(("parallel", "arbitrary")))(first_blk, qx, kx, vt)


def _attn_bwd_t(qx, dox, kx, kxt, vx, lse, last_blk):
    _, s, _ = qx.shape
    tq = min(TQ, s)
    n_q = s // tq

    def body(last_ref, qx_ref, dox_ref, lse_ref, kx_ref, kxt_ref, vx_ref, dk_ref, dv_ref, dfk_ref, dqt_ref, dfq_ref):
        p = pl.program_id(0)
        j = pl.program_id(1)

        @pl.when(j == 0)
        def _():
            dqt_ref[...] = jnp.zeros(dqt_ref.shape, F32)
            dfq_ref[...] = jnp.zeros(dfq_ref.shape, F32)

        key_le_query = (lax.broadcasted_iota(jnp.int32, (tq, tq), 0) <= lax.broadcasted_iota(jnp.int32, (tq, tq), 1))

        def step(i, carry, masked):
            rows_q = pl.ds(pl.multiple_of(i * tq, tq), tq)
            out = []
            for hh in range(2):
                dk, dv, col = carry[3 * hh:3 * hh + 3]
                q, do = qx_ref[hh, rows_q, :], dox_ref[hh, rows_q, :]
                st = lax.dot_general(kx_ref[hh], q, _NT, preferred_element_type=F32)
                pt = jnp.exp(st - lse_ref[0, i, hh:hh + 1, :])
                if masked:
                    pt = jnp.where(key_le_query, pt, 0.0)
                dst = pt * lax.dot_general(vx_ref[hh], do, _NT, preferred_element_type=F32)
                pb, dsb = pt.astype(BF16), dst.astype(BF16)
                dv = dv + lax.dot_general(pb, do, _NN, preferred_element_type=F32)
                dk = dk + lax.dot_general(dsb, q, _NN, preferred_element_type=F32)
                dqt_ref[hh, i] += lax.dot_general(kxt_ref[hh, 0], dsb, _NN, preferred_element_type=F32)
                for cb in range(tq // LANES):
                    col = col + dst[:, cb * LANES:(cb + 1) * LANES]
                dfq_ref[hh, i] += jnp.sum(dst.reshape(tq // HALO, HALO, tq), axis=0)
                out += [dk, dv, col]
            return tuple(out)

        zero = jnp.zeros((tq, LANES), F32)
        carry = step(j, (zero,) * 6, True)
        dk_a, dv_a, col_a, dk_b, dv_b, col_b = lax.fori_loop(j + 1, last_ref[p * n_q + j] + 1,
                                                             lambda i, cr: step(i, cr, False), carry)
        head_a = lax.broadcasted_iota(jnp.int32, (tq, LANES), 1) < HEAD_DIM
        dk_ref[...] = jnp.where(head_a, dk_a, pltpu.roll(dk_b, HEAD_DIM, 1)).astype(BF16)
        dv_ref[...] = jnp.where(head_a, dv_a, pltpu.roll(dv_b, HEAD_DIM, 1)).astype(BF16)
        rows8 = lax.broadcasted_iota(jnp.int32, (N_HEADS, tq), 0)
        dfk_a, dfk_b = (-jnp.sum(c.T, axis=0, keepdims=True) for c in (col_a, col_b))
        dfk_ref[0, 0] = jnp.where(rows8 == 0, dfk_a, jnp.where(rows8 == 1, dfk_b, 0.0))

    resident = pl.BlockSpec((2, s, LANES), lambda p, j, last: (p, 0, 0))
    key_rows = pl.BlockSpec((2, tq, LANES), lambda p, j, last: (p, j, 0))
    pair_out = pl.BlockSpec((tq, LANES), lambda p, j, last: (j, p))
    grid_spec = pltpu.PrefetchScalarGridSpec(
        num_scalar_prefetch=1, grid=(N_PAIRS, n_q),
        in_specs=[resident, resident, pl.BlockSpec((1, n_q, N_HEADS, tq), lambda p, j, last: (p, 0, 0, 0)),
                  key_rows, pl.BlockSpec((2, 1, LANES, tq), lambda p, j, last: (p, j, 0, 0)), key_rows],
        out_specs=[pair_out, pair_out, pl.BlockSpec((1, 1, N_HEADS, tq), lambda p, j, last: (p, j, 0, 0)),
                   pl.BlockSpec((2, n_q, LANES, tq), lambda p, j, last: (p, 0, 0, 0)),
                   pl.BlockSpec((2, n_q, HALO, tq), lambda p, j, last: (p, 0, 0, 0))])
    return pl.pallas_call(
        body, name="attn_bwd", grid_spec=grid_spec,
        out_shape=[jax.ShapeDtypeStruct((s, ATTN_W), BF16), jax.ShapeDtypeStruct((s, ATTN_W), BF16),
                   jax.ShapeDtypeStruct((N_PAIRS, n_q, N_HEADS, tq), F32),
                   jax.ShapeDtypeStruct((N_HEADS, n_q, LANES, tq), F32),
                   jax.ShapeDtypeStruct((N_HEADS, n_q, HALO, tq), F32)],
        compiler_params=_cp(("parallel", "arbitrary")))(last_blk, qx, dox, lse, kx, kxt, vx)


def _attn_dq_finish(dqt):
    _, n_q, _, tq = dqt.shape
    per = 4 if n_q % 4 == 0 else 1

    def body(dqt_ref, dq_ref):
        for b in range(per):
            a, bb = dqt_ref[0, b], dqt_ref[1, b]
            dq_ref[b * tq:(b + 1) * tq, :] = (
                jnp.concatenate([a[0:HEAD_DIM], bb[0:HEAD_DIM]], axis=0).T * Q_SCALE).astype(BF16)

    return pl.pallas_call(
        body, name="attn_dq_finish", grid=(N_PAIRS, n_q // per),
        in_specs=[pl.BlockSpec((2, per, LANES, tq), lambda p, i: (p, i, 0, 0))],
        out_specs=pl.BlockSpec((per * tq, LANES), lambda p, i: (i, p)),
        out_shape=jax.ShapeDtypeStruct((n_q * tq, ATTN_W), BF16),
        compiler_params=_cp(("parallel", "parallel")))(dqt)


def _ffn_act_fwd(up, w_ffn):
    s = up.shape[0]
    tm, tn = min(TM_FFN, s), TN_FFN
    nb = D_FF // tn

    def body(a_ref, g_ref, ap_ref, gp_ref, wa_ref, wg_ref, act_ref, u_ref):
        i = pl.program_id(1)

        def conv(blk_ref, prev_ref, w_ref):
            prev = jnp.where(i > 0, prev_ref[...].astype(F32)[HALO:], 0.0)
            return _conv_taps(jnp.concatenate([prev, blk_ref[...].astype(F32)], axis=0), w_ref[...])[HALO:]

        u_a, u_g = conv(a_ref, ap_ref, wa_ref), conv(g_ref, gp_ref, wg_ref)
        u_ref[0], u_ref[1] = u_a.astype(BF16), u_g.astype(BF16)
        act_ref[...] = (u_g * jax.nn.sigmoid(u_g) * u_a).astype(BF16)

    blk = lambda off: pl.BlockSpec((tm, tn), lambda n, i: (i, off + n))
    prev = lambda off: pl.BlockSpec(
        (HALO_BF16, tn), lambda n, i: (jnp.maximum(i * (tm // HALO_BF16) - 1, 0), off + n))
    wsp = lambda off: pl.BlockSpec((3, tn), lambda n, i: (0, off + n))
    return pl.pallas_call(
        body, name="ffn_act_fwd", grid=(nb, s // tm),
        in_specs=[blk(0), blk(nb), prev(0), prev(nb), wsp(0), wsp(nb)],
        out_specs=[pl.BlockSpec((tm, tn), lambda n, i: (i, n)), pl.BlockSpec((2, tm, tn), lambda n, i: (0, i, n))],
        out_shape=[jax.ShapeDtypeStruct((s, D_FF), BF16), jax.ShapeDtypeStruct((2, s, D_FF), BF16)],
        compiler_params=_cp(("parallel", "parallel")))(up, up, up, up, w_ffn, w_ffn)


def _ffn_act_bwd(up, u, dact, w_ffn):
    s = up.shape[0]
    tm, tn = min(TM_FFN, s), TN_FFN
    nb = D_FF // tn
    n_blk = s // tm

    def body(u_ref, un_ref, a_ref, g_ref, d_ref, dn_ref, wa_ref, wg_ref, dup_ref, gwa_ref, gwg_ref):
        i = pl.program_id(1)

        @pl.when(i == 0)
        def _():
            gwa_ref[...] = jnp.zeros_like(gwa_ref)
            gwg_ref[...] = jnp.zeros_like(gwg_ref)

        ext = lambda rows, next_rows: jnp.concatenate([rows, next_rows], axis=0)
        u_a, u_g = (ext(u_ref[h].astype(F32), un_ref[h].astype(F32)[:HALO]) for h in range(2))
        d_e = ext(d_ref[...], jnp.where(i < n_blk - 1, dn_ref[...], 0.0))
        sig = jax.nn.sigmoid(u_g)
        du_a = d_e * (u_g * sig)
        du_g = d_e * u_a * (sig * (1.0 + u_g * (1.0 - sig)))
        halves = ((gwa_ref, wa_ref[...], a_ref[...].astype(F32), du_a),
                  (gwg_ref, wg_ref[...], g_ref[...].astype(F32), du_g))
        for half, (gw_ref, w, upv, du) in enumerate(halves):
            du0, du1, du2 = du[:tm], _shift_up(du, 1)[:tm], _shift_up(du, 2)[:tm]
            dup_ref[half] = (w[2:3, :] * du0 + w[1:2, :] * du1 + w[0:1, :] * du2).astype(BF16)
            gw_ref[0:1, :] += jnp.sum(upv * du2, axis=0, keepdims=True)
            gw_ref[1:2, :] += jnp.sum(upv * du1, axis=0, keepdims=True)
            gw_ref[2:3, :] += jnp.sum(upv * du0, axis=0, keepdims=True)

    next_row = lambda halo: lambda i: jnp.minimum((i + 1) * (tm // halo), s // halo - 1)
    blk = lambda off: pl.BlockSpec((tm, tn), lambda n, i: (i, off + n))
    wsp = lambda off: pl.BlockSpec((3, tn), lambda n, i: (0, off + n))
    pair = lambda rows, row_of: pl.BlockSpec((2, rows, tn), lambda n, i: (0, row_of(i), n))
    return pl.pallas_call(
        body, name="ffn_act_bwd", grid=(nb, n_blk),
        in_specs=[pair(tm, lambda i: i), pair(HALO_BF16, next_row(HALO_BF16)), blk(0), blk(nb), blk(0),
                  pl.BlockSpec((HALO, tn), lambda n, i: (next_row(HALO)(i), n)), wsp(0), wsp(nb)],
        out_specs=[pair(tm, lambda i: i), wsp(0), wsp(0)],
        out_shape=[jax.ShapeDtypeStruct((2, s, D_FF), BF16),
                   jax.ShapeDtypeStruct((3, D_FF), F32), jax.ShapeDtypeStruct((3, D_FF), F32)],
        compiler_params=_cp(("parallel", "arbitrary")))(u, u, up, up, dact, dact, w_ffn, w_ffn)


def _adamw(w, g, m, v, name, with_grad=False):
    r, c = g.shape
    tr = next((t for t in (352, 256, 128, 64, 32, 16, 8) if r > t and r % t == 0), r)

    def body(w_ref, g_ref, m_ref, v_ref, d_ref, nm_ref, nv_ref, *g_out):
        gv = g_ref[...]
        if with_grad:
            g_out[0][...] = gv
        m_new = ADAM_B1 * m_ref[...] + (1.0 - ADAM_B1) * gv
        v_new = ADAM_B2 * v_ref[...] + (1.0 - ADAM_B2) * (gv * gv)
        m_hat = m_new / (1.0 - ADAM_B1 ** ADAM_STEP)
        v_hat = v_new / (1.0 - ADAM_B2 ** ADAM_STEP)
        d_ref[...] = -ADAM_LR * (m_hat / (jnp.sqrt(v_hat) + ADAM_EPS) + ADAM_WD * w_ref[...])
        nm_ref[...] = m_new
        nv_ref[...] = v_new

    tc = c if tr < r else next((t for t in (256, 128) if c > t and c % t == 0), c)
    pos = (lambda i: (i, 0)) if tr < r else (lambda i: (0, i))
    spec = pl.BlockSpec((tr, tc), pos)
    w_spec = spec if w.ndim == 2 else pl.BlockSpec((None, tr, tc), lambda i: (0, *pos(i)))
    shp = jax.ShapeDtypeStruct(w.shape, F32)
    return pl.pallas_call(
        body, name=name, grid=(r // tr * (c // tc),), in_specs=[w_spec, spec, w_spec, w_spec],
        out_specs=[w_spec] * (3 + with_grad), out_shape=[shp] * (3 + with_grad),
        compiler_params=_cp(("parallel",)))(w, g, m, v)


def _sum_rows_block(h):
    return h if h <= 352 else 256


def _pair_sum(view, recv, sel, name):
    n, _, h, c = view.shape
    tr = _sum_rows_block(h)

    def body(sel_ref, a_ref, b_ref, o_ref, ob_ref):
        t = a_ref[...] + b_ref[...]
        o_ref[...] = t
        ob_ref[...] = t.astype(BF16)

    blk = pl.BlockSpec((None, tr, c), lambda j, i, sel_ref: (j, i, 0))
    grid_spec = pltpu.PrefetchScalarGridSpec(
        num_scalar_prefetch=1, grid=(n, h // tr),
        in_specs=[pl.BlockSpec((None, None, tr, c), lambda j, i, sel_ref: (j, sel_ref[0], i, 0)),
                  pl.BlockSpec((None, None, tr, c), lambda j, i, sel_ref: (j, 0, i, 0))],
        out_specs=[blk, blk])
    return pl.pallas_call(
        body, name=name, grid_spec=grid_spec,
        out_shape=[jax.ShapeDtypeStruct((n, h, c), F32), jax.ShapeDtypeStruct((n, h, c), BF16)],
        compiler_params=_cp(("parallel", "parallel")))(sel, view, recv)


def _chip_sum(pair, got, sel, name):
    _, h, c = pair.shape
    tr = _sum_rows_block(h)
    nblk = h // tr

    def body(sel_ref, p_ref, g0_ref, g1_ref, g2_ref, o_ref):
        o_ref[...] = ((p_ref[...] + g0_ref[...].astype(F32)) + g1_ref[...].astype(F32)) + g2_ref[...].astype(F32)

    slot = lambda k: pl.BlockSpec((None, tr, c), lambda i, sel_ref: (k, i, 0))
    grid_spec = pltpu.PrefetchScalarGridSpec(
        num_scalar_prefetch=1, grid=(h // tr,),
        in_specs=[pl.BlockSpec((None, tr, c), lambda i, sel_ref: (sel_ref[1], i, 0)), slot(0), slot(1), slot(2)],
        out_specs=pl.BlockSpec((tr, c), lambda i, sel_ref: (sel_ref[0] * nblk + i, 0)))
    return pl.pallas_call(
        body, name=name, grid_spec=grid_spec, out_shape=jax.ShapeDtypeStruct((2 * h, c), F32),
        compiler_params=_cp(("parallel",)))(sel, pair, got, got, got)


def _place():
    return lax.axis_index("x"), lax.axis_index("y"), lax.axis_index("c")


def _other_chips(x, y):
    return [(1 - x, y), (x, 1 - y), (1 - x, 1 - y)]


def _hbm_specs(n):
    return [pl.BlockSpec(memory_space=pl.ANY)] * n


def _all_gather_weights(bigs, smalls):
    nb, ns = len(bigs), len(smalls)
    n = nb + ns

    def body(*refs):
        start, forward, finish = _gather_phases(refs[:n], refs[2 * n:3 * n], nb, *refs[3 * n:])
        start()
        forward()
        finish()

    arrays, landing, sems = _gather_operands(bigs, smalls)
    return pl.pallas_call(
        body, name="all_gather_weights",
        out_shape=[jax.ShapeDtypeStruct(b.shape, b.dtype) for b in landing],
        in_specs=_hbm_specs(2 * n), out_specs=_hbm_specs(n), input_output_aliases={n + k: k for k in range(n)},
        scratch_shapes=sems)(*arrays, *landing)


def _hosted_gather(refs, n, nb, step, total):
    ins, outs, send_sems, recv_sems = refs
    start, forward, finish = _gather_phases(ins, outs, nb, send_sems, recv_sems)
    pl.when(step == 0)(start)
    pl.when(step == (3 * total) // 4)(forward)
    return lambda: pl.when(step == total - 1)(finish)


def _in_proj(x, g, w_a, w_b, w_c, bigs, smalls):
    s, d = x.shape
    tm, tn = min(TM_MM, s), ATTN_W
    na, nq = w_a.shape[1] // tn, w_b.shape[1] // tn
    steps = na + nq + 1
    total = (s // tm) * steps
    nb, n = len(bigs), len(bigs) + len(smalls)
    arrays, landing, sems = _gather_operands(bigs, smalls)

    def body(x_ref, g_ref, wa_ref, wb_ref, wc_ref, *rest):
        z_ref, qkv_ref, f_ref, h_ref = rest[2 * n:2 * n + 4]
        h_scr = rest[-1]
        m, j = pl.program_id(0), pl.program_id(1)
        finish = _hosted_gather((rest[:n], rest[2 * n + 4:3 * n + 4]) + tuple(rest[3 * n + 4:3 * n + 6]), n, nb,
                                m * steps + j, total)

        @pl.when(j == 0)
        def _():
            xv = x_ref[...]
            hv = (xv * _rstd(xv) * g_ref[...]).astype(BF16)
            h_scr[...] = hv
            h_ref[...] = hv

        h = h_scr[...]

        @pl.when(j < na)
        def _():
            z_ref[...] = lax.dot_general(h, wa_ref[...], _NN, preferred_element_type=F32)

        @pl.when((j >= na) & (j < na + nq))
        def _():
            qkv_ref[...] = lax.dot_general(h, wb_ref[...], _NN, preferred_element_type=F32).astype(BF16)

        @pl.when(j == na + nq)
        def _():
            f_ref[...] = lax.dot_general(h, wc_ref[...], _NN, preferred_element_type=F32)

        finish()

    blk_a = lambda m, j: (m, jnp.minimum(j, na - 1))
    blk_b = lambda m, j: (m, jnp.clip(j - na, 0, nq - 1))
    outs = pl.pallas_call(
        body, name="in_proj", grid=(s // tm, steps),
        in_specs=[pl.BlockSpec((tm, d), lambda m, j: (m, 0)), pl.BlockSpec((1, d), lambda m, j: (0, 0)),
                  pl.BlockSpec((d, tn), lambda m, j: (0, jnp.minimum(j, na - 1))),
                  pl.BlockSpec((d, tn), lambda m, j: (0, jnp.clip(j - na, 0, nq - 1))),
                  pl.BlockSpec((d, LANES), lambda m, j: (0, 0))] + _hbm_specs(2 * n),
        out_specs=[pl.BlockSpec((tm, tn), blk_a), pl.BlockSpec((tm, tn), blk_b),
                   pl.BlockSpec((tm, LANES), lambda m, j: (m, 0)), pl.BlockSpec((tm, d), lambda m, j: (m, 0))]
        + _hbm_specs(n),
        out_shape=[jax.ShapeDtypeStruct((s, w_a.shape[1]), F32), jax.ShapeDtypeStruct((s, w_b.shape[1]), BF16),
                   jax.ShapeDtypeStruct((s, LANES), F32), jax.ShapeDtypeStruct((s, d), BF16)]
        + [jax.ShapeDtypeStruct(b.shape, b.dtype) for b in landing],
        input_output_aliases={5 + n + k: 4 + k for k in range(n)},
        scratch_shapes=sems + [pltpu.VMEM((tm, d), BF16)],
        compiler_params=_cp(("arbitrary", "arbitrary")))(x, g, w_a, w_b, w_c, *arrays, *landing)
    return outs[0], outs[1], outs[2], outs[3], outs[4:]


def _gather_operands(bigs, smalls):
    x, y, _ = _place()
    arrays = list(bigs) + list(smalls)
    landing = [lax.dynamic_update_index_in_dim(lax.empty((N_CHIPS,) + a.shape, a.dtype), a, 2 * x + y, 0)
               for a in arrays]
    n_sems = 6 * len(bigs) + 3 * len(smalls)
    return arrays, landing, [pltpu.SemaphoreType.DMA((n_sems,)), pltpu.SemaphoreType.DMA((n_sems,))]


def _gather_phases(ins, outs, nb, send_sems, recv_sems):
    n = len(ins)
    x, y, c = _place()
    my_chip = 2 * x + y
    chips = _other_chips(x, y)
    sibling = (x, y, 1 - c)

    def rows(k, which):
        h = ins[k].shape[0] // 2
        return pl.ds(which * h, h)

    def copy(sem, src, dst, to):
        return pltpu.make_async_remote_copy(src_ref=src, dst_ref=dst, send_sem=send_sems.at[sem],
                                            recv_sem=recv_sems.at[sem], device_id=to, device_id_type=MESH)

    def sends():
        out = [copy(6 * k + j, ins[k].at[rows(k, c)], outs[k].at[my_chip, rows(k, c)], (cx, cy, c))
               for k in range(nb) for j, (cx, cy) in enumerate(chips)]
        return out + [copy(6 * nb + 3 * (k - nb) + j, ins[k], outs[k].at[my_chip], (cx, cy, c))
                      for k in range(nb, n) for j, (cx, cy) in enumerate(chips)]

    def landed(k, j, which):
        cx, cy = chips[j]
        return outs[k].at[2 * cx + cy, rows(k, which)]

    def forwards():
        return [copy(6 * k + 3 + j, landed(k, j, c), landed(k, j, c), sibling)
                for j in range(3) for k in range(nb)]

    def start():
        for cp in sends():
            cp.start()

    def forward():
        for j in range(3):
            for k in range(nb):
                copy(6 * k + j, landed(k, j, c), landed(k, j, c), (x, y, c)).wait_recv()
                copy(6 * k + 3 + j, landed(k, j, c), landed(k, j, c), sibling).start()

    def finish():
        for j, (cx, cy) in enumerate(chips):
            for k in range(nb):
                copy(6 * k + 3 + j, landed(k, j, 1 - c), landed(k, j, 1 - c), (x, y, c)).wait_recv()
            for k in range(nb, n):
                arrived = outs[k].at[2 * cx + cy]
                copy(6 * nb + 3 * (k - nb) + j, arrived, arrived, (x, y, c)).wait_recv()
        for cp in sends() + forwards():
            cp.wait_send()

    return start, forward, finish


def _pair_exchange(views, name):
    n = len(views)

    def body(*refs):
        ins, outs, send_sems, recv_sems = refs[:n], refs[n:2 * n], refs[2 * n], refs[2 * n + 1]
        x, y, c = _place()
        copies = [pltpu.make_async_remote_copy(
            src_ref=ins[k].at[:, pl.ds(1 - c, 1)], dst_ref=outs[k], send_sem=send_sems.at[k],
            recv_sem=recv_sems.at[k], device_id=(x, y, 1 - c), device_id_type=MESH) for k in range(n)]
        for cp in copies:
            cp.start()
        for cp in copies:
            cp.wait()

    return pl.pallas_call(
        body, name=name,
        out_shape=[jax.ShapeDtypeStruct((v.shape[0], 1) + v.shape[2:], v.dtype) for v in views],
        in_specs=_hbm_specs(n), out_specs=_hbm_specs(n),
        scratch_shapes=[pltpu.SemaphoreType.DMA((n,)), pltpu.SemaphoreType.DMA((n,))])(*views)


def _scatter_operands(parts):
    n = len(parts)
    return ([jax.ShapeDtypeStruct((3,) + p.shape[1:], p.dtype) for p in parts],
            [pltpu.SemaphoreType.DMA((3 * n,)), pltpu.SemaphoreType.DMA((3 * n,))])


def _scatter_phases(ins, outs, send_sems, recv_sems):
    x, y, c = _place()

    def copies():
        return [pltpu.make_async_remote_copy(
            src_ref=ins[k].at[pl.ds(2 * cx + cy, 1)], dst_ref=outs[k].at[pl.ds(r, 1)], send_sem=send_sems.at[3 * k + r],
            recv_sem=recv_sems.at[3 * k + r], device_id=(cx, cy, c), device_id_type=MESH)
            for k in range(len(ins)) for r, (cx, cy) in enumerate(_other_chips(x, y))]

    def start():
        for cp in copies():
            cp.start()

    def finish():
        for cp in copies():
            cp.wait()

    return start, finish


def _hosted_scatter(ins, outs, sems, step, total):
    start, finish = _scatter_phases(ins, outs, *sems)
    pl.when(step == 0)(start)
    return lambda: pl.when(step == total - 1)(finish)


def _join_halves(shards):
    n = len(shards)

    def body(*refs):
        ins, outs, send_sems, recv_sems = refs[:n], refs[n:2 * n], refs[2 * n], refs[2 * n + 1]
        x, y, c = _place()

        def rows(ref, which):
            h = ref.shape[0] // 2
            return ref.at[pl.ds(which * h, h)]

        sent = [pltpu.make_async_remote_copy(
            src_ref=rows(ins[k], c), dst_ref=rows(outs[k], c), send_sem=send_sems.at[k], recv_sem=recv_sems.at[k],
            device_id=(x, y, 1 - c), device_id_type=MESH) for k in range(n)]
        for cp in sent:
            cp.start()
        for k in range(n):
            pltpu.make_async_remote_copy(
                src_ref=rows(ins[k], 1 - c), dst_ref=rows(outs[k], 1 - c), send_sem=send_sems.at[k],
                recv_sem=recv_sems.at[k], device_id=(x, y, 1 - c), device_id_type=MESH).wait_recv()
        for cp in sent:
            cp.wait_send()

    return pl.pallas_call(
        body, name="half_exchange", out_shape=[jax.ShapeDtypeStruct(a.shape, a.dtype) for a in shards],
        in_specs=_hbm_specs(n), out_specs=_hbm_specs(n), input_output_aliases={k: k for k in range(n)},
        scratch_shapes=[pltpu.SemaphoreType.DMA((n,)), pltpu.SemaphoreType.DMA((n,))])(*shards)


def _all_reduce_small(packet):
    rows, width = packet.shape
    n_dev = 8

    def body(x_ref, out_ref, gath, send_sems, recv_sems):
        x, y, c = _place()
        me, sibling = (x, y, c), (x, y, 1 - c)
        chips = _other_chips(x, y)

        def slot(px, py, pc):
            return gath.at[pl.ds((4 * px + 2 * py + pc) * rows, rows), :]

        def copy(k, block, to, src=None):
            return pltpu.make_async_remote_copy(
                src_ref=slot(*block) if src is None else src, dst_ref=slot(*block), send_sem=send_sems.at[k],
                recv_sem=recv_sems.at[k], device_id=to, device_id_type=MESH)

        first = [copy(0, me, sibling, src=x_ref)]
        first += [copy(1 + j, me, (*chip, c), src=x_ref) for j, chip in enumerate(chips)]
        for cp in first:
            cp.start()
        gath[pl.ds((4 * x + 2 * y + c) * rows, rows), :] = x_ref[...]
        passed = [copy(4 + j, (*chip, c), sibling) for j, chip in enumerate(chips)]
        for j, chip in enumerate(chips):
            copy(1 + j, (*chip, c), me).wait_recv()
            passed[j].start()
        copy(0, sibling, me).wait_recv()
        for j, chip in enumerate(chips):
            copy(4 + j, (*chip, 1 - c), me).wait_recv()
        for cp in first + passed:
            cp.wait_send()
        acc = gath[0:rows, :]
        for d in range(1, n_dev):
            acc = acc + gath[d * rows:(d + 1) * rows, :]
        out_ref[...] = acc

    return pl.pallas_call(
        body, name="all_reduce_small", out_shape=jax.ShapeDtypeStruct((rows, width), F32),
        in_specs=[pl.BlockSpec(memory_space=pltpu.VMEM)], out_specs=pl.BlockSpec(memory_space=pltpu.VMEM),
        scratch_shapes=[pltpu.VMEM((n_dev * rows, width), F32), pltpu.SemaphoreType.DMA((7,)),
                        pltpu.SemaphoreType.DMA((7,))])(packet)


def _flat_rows(parts, width, row_multiple):
    flat = jnp.concatenate([p.astype(F32).reshape(-1) for p in parts])
    rows = -(-flat.shape[0] // width)
    rows = -(-rows // row_multiple) * row_multiple
    return jnp.pad(flat, (0, rows * width - flat.shape[0])).reshape(rows, width)


def _unflatten(flat2d, shapes):
    flat = flat2d.reshape(-1)
    out, off = [], 0
    for shp in shapes:
        n = 1
        for dim in shp:
            n *= dim
        out.append(flat[off:off + n].reshape(shp))
        off += n
    return out


def _core_and_chip():
    x, y, c = _place()
    return jnp.stack([c, 2 * x + y]).astype(jnp.int32)


def _pair_sums(chip_major, names, call_name):
    views = [g.reshape(N_CHIPS, 2, g.shape[1] // 2, g.shape[2]) for g in chip_major]
    recv = _pair_exchange(views, call_name)
    sel = _core_and_chip()
    return [_pair_sum(v, r, sel, "pair_sum_" + nm) for v, r, nm in zip(views, recv, names)]


def _finish_grads(pairs, got, names):
    sel = _core_and_chip()
    return _join_halves([_chip_sum(p, g, sel, "chip_sum_" + nm) for (p, _), g, nm in zip(pairs, got, names)])


def kernel(x, g_mix, w_in, b_f, w_conv, g_conv_out, g_attn_out, w_o, g_ffn, w_up, w_ffn_conv, w_down, g_final, loss_target, m_g_mix, m_w_in, m_b_f, m_w_conv, m_g_conv_out, m_g_attn_out, m_w_o, m_g_ffn, m_w_up, m_w_ffn_conv, m_w_down, m_g_final, v_g_mix, v_w_in, v_b_f, v_w_conv, v_g_conv_out, v_g_attn_out, v_w_o, v_g_ffn, v_w_up, v_w_ffn_conv, v_w_down, v_g_final):
    s = x.shape[1]
    x0 = x[0]
    target = loss_target[0]
    d = D_MODEL
    x_pos, y_pos, _ = _place()
    my_chip = 2 * x_pos + y_pos

    (c_in,) = _all_gather_weights([w_in[0].astype(BF16)], [])
    w_in_full = jnp.concatenate([c_in[j] for j in range(N_CHIPS)], axis=1)
    c3 = 3 * CONV_CH
    w_a, w_b = w_in_full[:, :c3], w_in_full[:, c3:c3 + 3 * ATTN_W]
    w_c = jnp.pad(w_in_full[:, c3 + 3 * ATTN_W:], ((0, 0), (0, LANES - N_HEADS)))
    w_q, w_k, w_v = (w_b[:, i * ATTN_W:(i + 1) * ATTN_W] for i in range(3))
    b_pad = jnp.pad(b_f, ((0, 0), (0, LANES - N_HEADS)))

    z_a, qkv, f_log, h1, (c_o, c_up, c_conv, c_ffn) = _in_proj(
        x0, g_mix, w_a, w_b, w_c, [w_o[0].astype(BF16), w_up[0].astype(BF16)], [w_conv[0], w_ffn_conv[0]])
    fb = _gate_fwd(f_log, b_pad)
    qx, kx, kxt, vx, vt, bounds, (c_down,) = _attn_prep(qkv, fb, [w_down[0].astype(BF16)])
    w_o_full = c_o.reshape(d, d)
    w_down_full = c_down.reshape(D_FF, d)
    w_conv_full = jnp.concatenate([c_conv[j] for j in range(N_CHIPS)], axis=1)
    w_ffn_full = jnp.concatenate([c_ffn[j] for j in range(N_CHIPS)], axis=1)
    n_up = c_up.shape[2]
    first_blk, last_blk = _key_block_ranges(bounds)
    o_attn, lse = _attn_fwd_t(qx, kx, vt, first_blk)
    mix = _mixer_fwd(z_a, o_attn, w_conv_full, g_conv_out, g_attn_out)
    x2, h2 = _mm("nn", [mix], [w_o_full], F32, 512, d, "out_proj", add=x0, norm_g=g_ffn)
    up = _mm("nn", [h2], [c_up], BF16, TM_MM, n_up, "up_proj", b_chips=True)
    act, u_conv = _ffn_act_fwd(up, w_ffn_full)
    dx3, dx3_b, loss_row, gg_final = _down_proj_loss(act, w_down_full, x2, target, g_final.reshape(1, d))

    dact = _mm("nt", [dx3_b], [w_down_full], F32, TM_MM, 1408, "d_act")
    gw_down = _mm_tn(act, dx3_b, 1408, 1024, "gw_down")
    dup, gwf_lin, gwf_gate = _ffn_act_bwd(up, u_conv, dact, w_ffn_full)
    dh2 = _mm("nt", [(dup, j // 2, j % 2, n_up) for j in range(N_CHIPS)], [(c_up, j) for j in range(N_CHIPS)],
              F32, TM_MM, 512, "d_h2")
    gw_up = _mm_tn(h2, dup, 1024, n_up, "gw_up", out_chips=True)
    dx2, dx2_b, gg_ffn = _rms_bwd(x2, dh2, g_ffn, dx3, "rms_ffn_bwd", True)
    dmix = _mm("nt", [dx2_b], [w_o_full], F32, TM_MM, 512, "d_mix")
    gw_o = _mm_tn(mix, dx2_b, 1024, 1024, "gw_o")
    early = _pair_sums([gw_o.reshape(N_CHIPS, d // N_CHIPS, d), gw_up, gw_down.reshape(N_CHIPS, D_FF // N_CHIPS, d)],
                       ["w_o", "w_up", "w_down"], "pair_exchange")
    dz_a, dox, gw_conv, gg_conv_out, gg_attn_out, (got_o, got_down) = _mixer_bwd(
        z_a, o_attn, dmix, w_conv_full, g_conv_out, g_attn_out, [early[0][1], early[2][1]])
    dk, dv, dfk, dqt, dfq = _attn_bwd_t(qx, dox, kx, kxt, vx, lse, last_blk)
    dq = _attn_dq_finish(dqt)
    d_f = (jnp.transpose(dfk[:, :, 0:2, :], (1, 3, 0, 2)).reshape(s, N_HEADS)
           + jnp.transpose(jnp.sum(dfq, axis=2), (1, 2, 0)).reshape(s, N_HEADS))
    df_b, gb_f = _gate_bwd(f_log, b_pad, jnp.pad(d_f, ((0, 0), (0, LANES - N_HEADS))))
    gw_a = _mm_tn(h1, dz_a, 1024, c3, "gw_in_conv")
    gw_q = _mm_tn(h1, dq, 1024, ATTN_W, "gw_in_q")
    gw_k = _mm_tn(h1, dk, 1024, ATTN_W, "gw_in_k")
    gw_v = _mm_tn(h1, dv, 1024, ATTN_W, "gw_in_v")
    gw_c = _mm_tn(h1, df_b, 1024, LANES, "gw_in_gate")
    gw_in = jnp.concatenate([gw_a, gw_q, gw_k, gw_v, gw_c[:, :N_HEADS]], axis=1)
    n_in = IN_COLS // N_CHIPS
    gw_in = jnp.stack([gw_in[:, j * n_in:(j + 1) * n_in] for j in range(N_CHIPS)])
    late = _pair_sums([gw_in], ["w_in"], "pair_exchange_w_in")
    dh1, (got_up, got_in) = _mm("nt", [dz_a, dq, dk, dv, df_b], [w_a, w_q, w_k, w_v, w_c], F32, TM_MM, 512, "d_h1",
                                scatter=[early[1][1], late[0][1]])
    grad_x, gg_mix = _rms_bwd(x0, dh1, g_mix, dx2, "rms_mix_bwd", False)
    g_w_in, g_w_o, g_w_up, g_w_down = _finish_grads(late + early, [got_in, got_o, got_up, got_down],
                                                    ["w_in", "w_o", "w_up", "w_down"])

    gw_ffn = jnp.concatenate([gwf_lin, gwf_gate], axis=1)
    small_parts = [gg_mix, gg_conv_out, gg_attn_out, gg_ffn, gg_final, gb_f[:, :N_HEADS], loss_row[:, 0:1], gw_conv,
                   gw_ffn]
    small_shapes = [a.shape for a in small_parts]
    tot = _unflatten(_all_reduce_small(_flat_rows(small_parts, d, 8)), small_shapes)
    g_g_mix, g_g_conv_out, g_g_attn_out, g_g_ffn, g_g_final, g_b_f, loss_sum, g_conv_full, g_ffn_full = tot
    loss = loss_sum[0, 0]
    g_g_final = g_g_final[0]
    g_w_conv = lax.dynamic_slice_in_dim(g_conv_full, my_chip * (CONV_CH // N_CHIPS), CONV_CH // N_CHIPS, axis=1)
    g_w_ffn = lax.dynamic_slice_in_dim(g_ffn_full, my_chip * n_up, n_up, axis=1)

    swap = lambda a: jnp.swapaxes(a, -1, -2)
    u_w_in = tuple(swap(a) for a in _adamw(swap(w_in), swap(g_w_in), swap(m_w_in), swap(v_w_in), "adam_w_in", True))
    u_w_o = _adamw(w_o, g_w_o, m_w_o, v_w_o, "adam_w_o", True)
    u_w_up = _adamw(w_up, g_w_up, m_w_up, v_w_up, "adam_w_up", True)
    u_w_down = _adamw(w_down, g_w_down, m_w_down, v_w_down, "adam_w_down", True)

    small_w = [g_mix, b_f, g_conv_out, g_attn_out, g_ffn, g_final, w_conv, w_ffn_conv]
    small_g = [g_g_mix, g_b_f, g_g_conv_out, g_g_attn_out, g_g_ffn, g_g_final, g_w_conv, g_w_ffn]
    small_m = [m_g_mix, m_b_f, m_g_conv_out, m_g_attn_out, m_g_ffn, m_g_final, m_w_conv, m_w_ffn_conv]
    small_v = [v_g_mix, v_b_f, v_g_conv_out, v_g_attn_out, v_g_ffn, v_g_final, v_w_conv, v_w_ffn_conv]
    shapes = [a.shape for a in small_w]
    pack = lambda arrs: _flat_rows(arrs, LANES, 8)
    sd, sm, sv = _adamw(pack(small_w), pack(small_g), pack(small_m), pack(small_v), "adam_small")
    sd, sm, sv = _unflatten(sd, shapes), _unflatten(sm, shapes), _unflatten(sv, shapes)
    (d_g_mix, d_b_f, d_g_conv_out, d_g_attn_out, d_g_ffn, d_g_final, d_w_conv, d_w_ffn) = sd
    (nm_g_mix, nm_b_f, nm_g_conv_out, nm_g_attn_out, nm_g_ffn, nm_g_final, nm_w_conv, nm_w_ffn) = sm
    (nv_g_mix, nv_b_f, nv_g_conv_out, nv_g_attn_out, nv_g_ffn, nv_g_final, nv_w_conv, nv_w_ffn) = sv

    grads = (g_g_mix, u_w_in[3], g_b_f, g_w_conv[None], g_g_conv_out, g_g_attn_out, u_w_o[3], g_g_ffn,
             u_w_up[3], g_w_ffn[None], u_w_down[3], g_g_final)
    deltas = (d_g_mix, u_w_in[0], d_b_f, d_w_conv, d_g_conv_out, d_g_attn_out, u_w_o[0], d_g_ffn, u_w_up[0],
              d_w_ffn, u_w_down[0], d_g_final)
    new_m = (nm_g_mix, u_w_in[1], nm_b_f, nm_w_conv, nm_g_conv_out, nm_g_attn_out, u_w_o[1], nm_g_ffn, u_w_up[1],
             nm_w_ffn, u_w_down[1], nm_g_final)
    new_v = (nv_g_mix, u_w_in[2], nv_b_f, nv_w_conv, nv_g_conv_out, nv_g_attn_out, u_w_o[2], nv_g_ffn, u_w_up[2],
             nv_w_ffn, u_w_down[2], nv_g_final)
    return (loss, grad_x[None], *grads, *deltas, *new_m, *new_v)
```

```python
import jax
import jax.numpy as jnp
from jax import lax
from jax.experimental import pallas as pl
from jax.experimental.pallas import tpu as pltpu

F32, BF16 = jnp.float32, jnp.bfloat16
MESH = pl.DeviceIdType.MESH

D_MODEL = 1024
CONV_CH = 512
ATTN_W = 512
N_HEADS = 8
HEAD_DIM = 64
N_PAIRS = N_HEADS // 2
D_FF = 2816
IN_COLS = 3 * CONV_CH + 3 * ATTN_W + N_HEADS
EPS = 1e-6
Q_SCALE = 0.125
EXP_ZERO = 88.0
N_CHIPS = 4
LANES = 128
HALO = 8
HALO_BF16 = 2 * HALO

ADAM_LR, ADAM_B1, ADAM_B2, ADAM_EPS, ADAM_WD, ADAM_STEP = 0.001, 0.9, 0.999, 1e-08, 0.01, 10

TM_ROWS = 512
TM_MM = 1024
TK_TN = 1024
TQ = 512
TM_FFN = 256
TN_FFN = 1408
VMEM_LIMIT = 52 * 2**20


def _cp(sem, vmem=VMEM_LIMIT):
    return pltpu.CompilerParams(dimension_semantics=sem, vmem_limit_bytes=vmem)


def _bf(a):
    return a if a.dtype == BF16 else a.astype(BF16)


def _mm(mode, a_list, b_list, out_dtype, tm, tn, name, add=None, b_chips=False, scatter=(), norm_g=None):
    n_p = len(a_list)
    a0 = a_list[0]
    m_dim = a0[0].shape[1] if isinstance(a0, tuple) else a0.shape[0]
    b0 = b_list[0]
    if b_chips:
        n_dim = b0.shape[0] * b0.shape[2]
        assert tn == b0.shape[2] and mode == "nn"
    else:
        b0 = b0[0][b0[1]] if isinstance(b0, tuple) else b0
        n_dim = b0.shape[1 if mode == "nn" else 0]
    tm, tn = min(tm, m_dim), min(tn, n_dim)
    assert m_dim % tm == 0 and n_dim % tn == 0
    dims = (((1,), (0,)), ((), ())) if mode == "nn" else (((1,), (1,)), ((), ()))
    in_specs, args = [], []
    for a in a_list:
        if isinstance(a, tuple):
            arr, lead, col, width = a
            in_specs.append(pl.BlockSpec((None, tm, width), lambda m, n, lead=lead, col=col: (lead, m, col)))
        else:
            arr = a
            in_specs.append(pl.BlockSpec((tm, a.shape[1]), lambda m, n: (m, 0)))
        args.append(arr)
    for b in b_list:
        if b_chips:
            arr = b
            in_specs.append(pl.BlockSpec((None, b.shape[1], tn), lambda m, n: (n, 0, 0)))
        elif isinstance(b, tuple):
            arr, lead = b
            if mode == "nn":
                in_specs.append(pl.BlockSpec((None, arr.shape[1], tn), lambda m, n, lead=lead: (lead, 0, n)))
            else:
                in_specs.append(pl.BlockSpec((None, tn, arr.shape[2]), lambda m, n, lead=lead: (lead, n, 0)))
        elif mode == "nn":
            arr = b
            in_specs.append(pl.BlockSpec((b.shape[0], tn), lambda m, n: (0, n)))
        else:
            arr = b
            in_specs.append(pl.BlockSpec((tn, b.shape[1]), lambda m, n: (n, 0)))
        args.append(arr)
    if add is not None:
        in_specs.append(pl.BlockSpec((tm, tn), lambda m, n: (m, n)))
        args.append(add)
    if norm_g is not None:
        assert tn == n_dim and not scatter
        in_specs.append(pl.BlockSpec((1, tn), lambda m, n: (0, 0)))
        args.append(norm_g)

    n_in = len(args)
    n_sc = len(scatter)
    grid = (m_dim // tm, n_dim // tn)

    def body(*refs):
        o_ref = refs[n_in + n_sc]
        if n_sc:
            finish = _hosted_scatter(refs[n_in:n_in + n_sc], refs[n_in + n_sc + 1:n_in + 2 * n_sc + 1],
                                     refs[n_in + 2 * n_sc + 1:], pl.program_id(0) * grid[1] + pl.program_id(1),
                                     grid[0] * grid[1])
        acc = None
        for i in range(n_p):
            d = lax.dot_general(_bf(refs[i][...]), _bf(refs[n_p + i][...]), dims,
                                preferred_element_type=F32)
            acc = d if acc is None else acc + d
        if add is not None:
            acc = refs[2 * n_p][...] + acc
        o_ref[...] = acc.astype(out_dtype)
        if norm_g is not None:
            refs[n_in + 1][...] = (acc * _rstd(acc) * refs[n_in - 1][...]).astype(BF16)
        if n_sc:
            finish()

    main_spec = pl.BlockSpec((tm, tn), lambda m, n: (m, n))
    main_shape = jax.ShapeDtypeStruct((m_dim, n_dim), out_dtype)
    if norm_g is not None:
        return pl.pallas_call(body, name=name, grid=grid, in_specs=in_specs, out_specs=[main_spec, main_spec],
                              out_shape=[main_shape, jax.ShapeDtypeStruct((m_dim, n_dim), BF16)],
                              compiler_params=_cp(("parallel", "parallel")))(*args)
    if not n_sc:
        return pl.pallas_call(body, name=name, grid=grid, in_specs=in_specs, out_specs=main_spec,
                              out_shape=main_shape, compiler_params=_cp(("parallel", "parallel")))(*args)
    got_shapes, sems = _scatter_operands(scatter)
    outs = pl.pallas_call(
        body, name=name, grid=grid, in_specs=in_specs + _hbm_specs(n_sc), out_specs=[main_spec] + _hbm_specs(n_sc),
        out_shape=[main_shape] + got_shapes, scratch_shapes=sems,
        compiler_params=_cp(("arbitrary", "arbitrary")))(*args, *scatter)
    return outs[0], outs[1:]


def _mm_tn(a, b, tm, tn, name, out_chips=False):
    k_dim, m_dim = a.shape
    n_dim = b.shape[-1] * (b.shape[0] if b.ndim == 3 else 1)
    tm, tn, tk = min(tm, m_dim), min(tn, b.shape[-1]), min(TK_TN, k_dim)
    assert m_dim % tm == 0 and b.shape[-1] % tn == 0 and k_dim % tk == 0
    per = b.shape[-1] // tn
    if b.ndim == 3:
        b_spec = pl.BlockSpec((None, tk, tn), lambda m, n, k: (n // per, k, n % per))
    else:
        b_spec = pl.BlockSpec((tk, tn), lambda m, n, k: (k, n))

    def body(a_ref, b_ref, o_ref):
        @pl.when(pl.program_id(2) == 0)
        def _():
            o_ref[...] = jnp.zeros_like(o_ref)
        o_ref[...] += lax.dot_general(_bf(a_ref[...]), _bf(b_ref[...]), (((0,), (0,)), ((), ())),
                                      preferred_element_type=F32)

    return pl.pallas_call(
        body, name=name, grid=(m_dim // tm, n_dim // tn, k_dim // tk),
        in_specs=[pl.BlockSpec((tk, tm), lambda m, n, k: (k, m)), b_spec],
        out_specs=(pl.BlockSpec((None, tm, tn), lambda m, n, k: (n, m, 0)) if out_chips
                   else pl.BlockSpec((tm, tn), lambda m, n, k: (m, n))),
        out_shape=jax.ShapeDtypeStruct((n_dim // tn, m_dim, tn) if out_chips else (m_dim, n_dim), F32),
        compiler_params=_cp(("parallel", "parallel", "arbitrary")))(a, b)


def _rstd(x):
    return lax.rsqrt(jnp.mean(x * x, axis=-1, keepdims=True) + EPS)


def _rms_bwd(x, dh, g, dres, name, with_bf16, scatter=()):
    s, d = x.shape
    tm = min(TM_ROWS, s)
    n_sc = len(scatter)
    n_out = 3 if with_bf16 else 2
    got_shapes, sems = _scatter_operands(scatter) if n_sc else ([], [])

    def body(x_ref, dh_ref, g_ref, dres_ref, *rest):
        dx_ref, gg_ref = rest[n_sc], rest[n_sc + n_out - 1]
        i = pl.program_id(0)
        if n_sc:
            finish = _hosted_scatter(rest[:n_sc], rest[n_sc + n_out:2 * n_sc + n_out], rest[2 * n_sc + n_out:], i,
                                     s // tm)

        @pl.when(i == 0)
        def _():
            gg_ref[...] = jnp.zeros_like(gg_ref)

        xv = x_ref[...]
        xn = xv * _rstd(xv)
        dhv = dh_ref[...]
        gg_ref[...] += jnp.sum(dhv * xn, axis=0, keepdims=True)
        t = dhv * g_ref[...]
        dx = dres_ref[...] + _rstd(xv) * (t - xn * jnp.mean(t * xn, axis=-1, keepdims=True))
        dx_ref[...] = dx
        if with_bf16:
            rest[n_sc + 1][...] = dx.astype(BF16)
        if n_sc:
            finish()

    row = pl.BlockSpec((tm, d), lambda i: (i, 0))
    vec = pl.BlockSpec((1, d), lambda i: (0, 0))
    out_specs = [row] + ([row] if with_bf16 else []) + [vec] + _hbm_specs(n_sc)
    out_shape = ([jax.ShapeDtypeStruct((s, d), F32)] + ([jax.ShapeDtypeStruct((s, d), BF16)] if with_bf16 else [])
                 + [jax.ShapeDtypeStruct((1, d), F32)] + got_shapes)
    outs = pl.pallas_call(
        body, name=name, grid=(s // tm,), in_specs=[row, row, vec, row] + _hbm_specs(n_sc), out_specs=out_specs,
        out_shape=out_shape, scratch_shapes=sems, compiler_params=_cp(("arbitrary",)))(x, dh, g, dres, *scatter)
    return tuple(outs[:n_out]) + ((outs[n_out:],) if n_sc else ())


def _down_proj_loss(act, w_down, x2, target, g):
    s, d = x2.shape
    tm = min(TM_ROWS, s)
    k_dim = act.shape[1]

    def body(a_ref, w_ref, x_ref, t_ref, g_ref, dx_ref, dxb_ref, loss_ref, gg_ref):
        @pl.when(pl.program_id(0) == 0)
        def _():
            gg_ref[...] = jnp.zeros_like(gg_ref)
            loss_ref[...] = jnp.zeros_like(loss_ref)

        xv = x_ref[...] + lax.dot_general(a_ref[...], w_ref[...], (((1,), (0,)), ((), ())),
                                          preferred_element_type=F32)
        r = _rstd(xv)
        xn = xv * r
        gv = g_ref[...]
        err = xn * gv - t_ref[...]
        loss_ref[...] += 0.5 * jnp.sum(jnp.mean(err * err, axis=-1, keepdims=True), axis=0, keepdims=True)
        dy = err * (1.0 / d)
        gg_ref[...] += jnp.sum(dy * xn, axis=0, keepdims=True)
        t = dy * gv
        dx = r * (t - xn * jnp.mean(t * xn, axis=-1, keepdims=True))
        dx_ref[...] = dx
        dxb_ref[...] = dx.astype(BF16)

    row = pl.BlockSpec((tm, d), lambda i: (i, 0))
    vec = pl.BlockSpec((1, d), lambda i: (0, 0))
    return pl.pallas_call(
        body, name="down_proj_loss", grid=(s // tm,),
        in_specs=[pl.BlockSpec((tm, k_dim), lambda i: (i, 0)), pl.BlockSpec((k_dim, d), lambda i: (0, 0)), row, row,
                  vec],
        out_specs=[row, row, pl.BlockSpec((1, LANES), lambda i: (0, 0)), vec],
        out_shape=[jax.ShapeDtypeStruct((s, d), F32), jax.ShapeDtypeStruct((s, d), BF16),
                   jax.ShapeDtypeStruct((1, LANES), F32), jax.ShapeDtypeStruct((1, d), F32)],
        compiler_params=_cp(("arbitrary",)))(act, w_down, x2, target, g)


def _prev_halo_spec(tm, width, col):
    return pl.BlockSpec((HALO, width), lambda i, *_: (jnp.maximum(i * (tm // HALO) - 1, 0), col))


def _next_halo_spec(tm, width, col, s):
    return pl.BlockSpec((HALO, width), lambda i, *_: (jnp.minimum((i + 1) * (tm // HALO), s // HALO - 1), col))


def _shift_down(x, k):
    return pltpu.roll(x, k, 0)


def _shift_up(x, k):
    return pltpu.roll(x, x.shape[0] - k, 0)


def _conv_taps(x_ext, w):
    return w[0:1, :] * _shift_down(x_ext, 2) + w[1:2, :] * _shift_down(x_ext, 1) + w[2:3, :] * x_ext


def _conv_taps_t(d_ext, w):
    return w[2:3, :] * d_ext + w[1:2, :] * _shift_up(d_ext, 1) + w[0:1, :] * _shift_up(d_ext, 2)


def _mixer_fwd(z_a, o_attn, w_conv, g_conv_out, g_attn_out):
    s = z_a.shape[0]
    c = CONV_CH
    tm = min(TM_ROWS, s)

    def body(gb_ref, gc_ref, xc_ref, gcp_ref, xcp_ref, o_ref, w_ref, gco_ref, gao_ref, mix_ref):
        i = pl.program_id(0)
        cx = gc_ref[...] * xc_ref[...]
        cx_prev = jnp.where(i > 0, gcp_ref[...] * xcp_ref[...], 0.0)
        conv = _conv_taps(jnp.concatenate([cx_prev, cx], axis=0), w_ref[...])[HALO:]
        y = gb_ref[...] * conv
        mix_ref[:, 0:c] = (y * _rstd(y) * gco_ref[...]).astype(BF16)
        o = o_ref[...]
        mix_ref[:, c:2 * c] = (o * _rstd(o) * gao_ref[...]).astype(BF16)

    col = lambda j: pl.BlockSpec((tm, c), lambda i: (i, j))
    vec = pl.BlockSpec((1, c), lambda i: (0, 0))
    return pl.pallas_call(
        body, name="mixer_fwd", grid=(s // tm,),
        in_specs=[col(0), col(1), col(2), _prev_halo_spec(tm, c, 1), _prev_halo_spec(tm, c, 2), col(0),
                  pl.BlockSpec((3, c), lambda i: (0, 0)), vec, vec],
        out_specs=pl.BlockSpec((tm, 2 * c), lambda i: (i, 0)),
        out_shape=jax.ShapeDtypeStruct((s, 2 * c), BF16),
        compiler_params=_cp(("parallel",)))(z_a, z_a, z_a, z_a, z_a, o_attn, w_conv, g_conv_out, g_attn_out)


def _mixer_bwd(z_a, o_attn, dmix, w_conv, g_conv_out, g_attn_out, scatter):
    s = z_a.shape[0]
    c = CONV_CH
    tm = min(TM_ROWS, s)
    n_blk = s // tm
    n_sc = len(scatter)
    got_shapes, sems = _scatter_operands(scatter)

    def body(gb_ref, gc_ref, xc_ref, gcp_ref, xcp_ref, gbn_ref, gcn_ref, xcn_ref, o_ref, dnc_ref, dncn_ref, dna_ref,
             w_ref, gco_ref, gao_ref, *rest):
        dz_ref, dox_ref, gw_ref, ggco_ref, ggao_ref = rest[n_sc:n_sc + 5]
        i = pl.program_id(0)
        finish = _hosted_scatter(rest[:n_sc], rest[n_sc + 5:2 * n_sc + 5], rest[2 * n_sc + 5:], i, n_blk)

        @pl.when(i == 0)
        def _():
            gw_ref[...] = jnp.zeros_like(gw_ref)
            ggco_ref[...] = jnp.zeros_like(ggco_ref)
            ggao_ref[...] = jnp.zeros_like(ggao_ref)

        w = w_ref[...]
        zeros = jnp.zeros((HALO, c), F32)
        gb_e = jnp.concatenate([zeros, gb_ref[...], gbn_ref[...]], axis=0)
        cx_prev = jnp.where(i > 0, gcp_ref[...] * xcp_ref[...], 0.0)
        gc_e = jnp.concatenate([zeros, gc_ref[...], gcn_ref[...]], axis=0)
        xc_e = jnp.concatenate([zeros, xc_ref[...], xcn_ref[...]], axis=0)
        cx_e = jnp.concatenate([cx_prev, gc_ref[...] * xc_ref[...], gcn_ref[...] * xcn_ref[...]], axis=0)
        dn_next = jnp.where(i < n_blk - 1, dncn_ref[...], 0.0)
        dn_e = jnp.concatenate([zeros, dnc_ref[...], dn_next], axis=0)

        cx_1, cx_2 = _shift_down(cx_e, 1), _shift_down(cx_e, 2)
        conv_e = w[0:1, :] * cx_2 + w[1:2, :] * cx_1 + w[2:3, :] * cx_e
        y_e = gb_e * conv_e
        r_e = _rstd(y_e)
        yn_e = y_e * r_e
        t_e = dn_e * gco_ref[...]
        dy_e = r_e * (t_e - yn_e * jnp.mean(t_e * yn_e, axis=-1, keepdims=True))
        dconv_e = dy_e * gb_e
        dcx_e = _conv_taps_t(dconv_e, w)
        blk = slice(HALO, HALO + tm)
        dz_ref[:, 0:c] = (dy_e * conv_e)[blk].astype(BF16)
        dz_ref[:, c:2 * c] = (dcx_e * xc_e)[blk].astype(BF16)
        dz_ref[:, 2 * c:3 * c] = (dcx_e * gc_e)[blk].astype(BF16)
        ggco_ref[...] += jnp.sum((dn_e * yn_e)[blk], axis=0, keepdims=True)
        dconv = dconv_e[blk]
        gw_ref[0:1, :] += jnp.sum(dconv * cx_2[blk], axis=0, keepdims=True)
        gw_ref[1:2, :] += jnp.sum(dconv * cx_1[blk], axis=0, keepdims=True)
        gw_ref[2:3, :] += jnp.sum(dconv * cx_e[blk], axis=0, keepdims=True)

        o = o_ref[...]
        ra = _rstd(o)
        on = o * ra
        dna = dna_ref[...]
        ggao_ref[...] += jnp.sum(dna * on, axis=0, keepdims=True)
        ta = dna * gao_ref[...]
        do = ra * (ta - on * jnp.mean(ta * on, axis=-1, keepdims=True))
        prod = do * o
        lane = lax.broadcasted_iota(jnp.int32, (tm, LANES), 1)
        head_a = lane < HEAD_DIM
        for p in range(N_PAIRS):
            cols = slice(p * LANES, (p + 1) * LANES)
            pb, dob = prod[:, cols], do[:, cols]
            for hh in range(2):
                sel = head_a if hh == 0 else jnp.logical_not(head_a)
                delta = jnp.sum(jnp.where(sel, pb, 0.0), axis=-1, keepdims=True)
                neg3 = _split3(-delta)
                do_h = pltpu.roll(dob, HEAD_DIM, 1) if hh else dob
                dox_ref[2 * p + hh] = _aug(do_h, lane, neg3).astype(BF16)
        finish()

    col = lambda j: pl.BlockSpec((tm, c), lambda i: (i, j))
    vec = pl.BlockSpec((1, c), lambda i: (0, 0))
    w3 = pl.BlockSpec((3, c), lambda i: (0, 0))
    outs = pl.pallas_call(
        body, name="mixer_bwd", grid=(n_blk,),
        in_specs=[col(0), col(1), col(2), _prev_halo_spec(tm, c, 1), _prev_halo_spec(tm, c, 2),
                  _next_halo_spec(tm, c, 0, s), _next_halo_spec(tm, c, 1, s), _next_halo_spec(tm, c, 2, s),
                  col(0), col(0), _next_halo_spec(tm, c, 0, s), col(1), w3, vec, vec] + _hbm_specs(n_sc),
        out_specs=[pl.BlockSpec((tm, 3 * c), lambda i: (i, 0)),
                   pl.BlockSpec((N_HEADS, tm, LANES), lambda i: (0, i, 0)), w3, vec, vec] + _hbm_specs(n_sc),
        out_shape=[jax.ShapeDtypeStruct((s, 3 * c), BF16), jax.ShapeDtypeStruct((N_HEADS, s, LANES), BF16),
                   jax.ShapeDtypeStruct((3, c), F32), jax.ShapeDtypeStruct((1, c), F32),
                   jax.ShapeDtypeStruct((1, c), F32)] + got_shapes,
        scratch_shapes=sems, compiler_params=_cp(("arbitrary",)))(
            z_a, z_a, z_a, z_a, z_a, z_a, z_a, z_a, o_attn, dmix, dmix, dmix, w_conv, g_conv_out, g_attn_out,
            *scatter)
    return tuple(outs[:5]) + (outs[5:],)


def _gate_fwd(f, b_pad):
    s = f.shape[0]
    tm = min(TQ, s)

    def body(f_ref, b_ref, fb_ref, carry):
        @pl.when(pl.program_id(0) == 0)
        def _():
            carry[...] = jnp.zeros_like(carry)

        z = f_ref[...] + b_ref[...]
        x = jnp.minimum(z, 0.0) - jnp.log1p(jnp.exp(-jnp.abs(z)))
        row = lax.broadcasted_iota(jnp.int32, (tm, LANES), 0)
        sh = 1
        while sh < tm:
            x = x + jnp.where(row >= sh, _shift_down(x, sh), 0.0)
            sh *= 2
        x = x + carry[0:1, :]
        carry[...] = jnp.broadcast_to(x[tm - 1:tm, :], carry.shape)
        head_a = lax.broadcasted_iota(jnp.int32, (tm, LANES), 1) < HEAD_DIM
        for p in range(N_PAIRS):
            fa = jnp.broadcast_to(x[:, 2 * p:2 * p + 1], (tm, LANES))
            fbv = jnp.broadcast_to(x[:, 2 * p + 1:2 * p + 2], (tm, LANES))
            fb_ref[:, p * LANES:(p + 1) * LANES] = jnp.where(head_a, fa, fbv)

    return pl.pallas_call(
        body, name="gate_fwd", grid=(s // tm,),
        in_specs=[pl.BlockSpec((tm, LANES), lambda i: (i, 0)), pl.BlockSpec((1, LANES), lambda i: (0, 0))],
        out_specs=pl.BlockSpec((tm, N_PAIRS * LANES), lambda i: (i, 0)),
        out_shape=jax.ShapeDtypeStruct((s, N_PAIRS * LANES), F32),
        scratch_shapes=[pltpu.VMEM((HALO, LANES), F32)],
        compiler_params=_cp(("arbitrary",)))(f, b_pad)


def _gate_bwd(f, b_pad, d_f):
    s = f.shape[0]
    tm = min(TQ, s)
    n_blk = s // tm

    def body(f_ref, b_ref, d_ref, df_ref, gb_ref, carry):
        @pl.when(pl.program_id(0) == 0)
        def _():
            carry[...] = jnp.zeros_like(carry)
            gb_ref[...] = jnp.zeros_like(gb_ref)

        x = d_ref[...]
        row = lax.broadcasted_iota(jnp.int32, (tm, LANES), 0)
        sh = 1
        while sh < tm:
            x = x + jnp.where(row < tm - sh, _shift_up(x, sh), 0.0)
            sh *= 2
        x = x + carry[0:1, :]
        carry[...] = jnp.broadcast_to(x[0:1, :], carry.shape)
        z = f_ref[...] + b_ref[...]
        d = x * (1.0 / (1.0 + jnp.exp(z)))
        df_ref[...] = d.astype(BF16)
        gb_ref[...] += jnp.sum(d, axis=0, keepdims=True)

    rev = pl.BlockSpec((tm, LANES), lambda i: (n_blk - 1 - i, 0))
    vec = pl.BlockSpec((1, LANES), lambda i: (0, 0))
    return pl.pallas_call(
        body, name="gate_bwd", grid=(n_blk,), in_specs=[rev, vec, rev], out_specs=[rev, vec],
        out_shape=[jax.ShapeDtypeStruct((s, LANES), BF16), jax.ShapeDtypeStruct((1, LANES), F32)],
        scratch_shapes=[pltpu.VMEM((HALO, LANES), F32)],
        compiler_params=_cp(("arbitrary",)))(f, b_pad, d_f)


_NT = (((1,), (1,)), ((), ()))
_NN = (((1,), (0,)), ((), ()))


AUG = HEAD_DIM
NORM_MARGIN = 1.01


def _split3(x):
    hi = x.astype(BF16).astype(F32)
    r = x - hi
    mid = r.astype(BF16).astype(F32)
    lo = (r - mid).astype(BF16).astype(F32)
    return hi, mid, lo


def _aug(base, lane, vals):
    out = jnp.where(lane < AUG, base, 0.0)
    for k, v in enumerate(vals):
        out = jnp.where(lane == AUG + k, v, out)
    return out


def _attn_prep(qkv, fb, bigs):
    s = qkv.shape[0]
    tq = min(TQ, s)
    n_q = s // tq

    n = len(bigs)
    arrays, landing, sems = _gather_operands(bigs, [])

    def body(q_ref, k_ref, v_ref, fb_ref, *rest):
        qx_ref, kx_ref, kxt_ref, vx_ref, vt_ref, b_ref = rest[2 * n:2 * n + 6]
        finish = _hosted_gather((rest[:n], rest[2 * n + 6:3 * n + 6]) + tuple(rest[3 * n + 6:]), n, n,
                                pl.program_id(0), n_q)
        lane = lax.broadcasted_iota(jnp.int32, (tq, LANES), 1)
        lane8 = lax.broadcasted_iota(jnp.int32, (HALO, LANES), 1)
        head_lanes = lane < AUG
        is_lane = [lane == AUG + k for k in range(6)]
        first3 = (lane >= AUG) & (lane < AUG + 3)
        next3 = (lane >= AUG + 3) & (lane < AUG + 6)
        q_const = jnp.where(first3, -1.0, 0.0)
        k_const = jnp.where(next3, 1.0, 0.0)
        v_const = jnp.where(first3, 1.0, 0.0)
        ones_head = (lax.broadcasted_iota(jnp.int32, (LANES, LANES), 0) < HEAD_DIM).astype(BF16)
        acc = jnp.zeros((HALO, LANES), F32)
        for p in range(N_PAIRS):
            cols = slice(p * LANES, (p + 1) * LANES)
            q2, k2, v2 = (ref[:, cols].astype(F32) for ref in (q_ref, k_ref, v_ref))
            q2 = q2 * Q_SCALE
            f2 = fb_ref[:, cols]
            for hh in range(2):
                h = 2 * p + hh
                q, k, v = ((pltpu.roll(x, HEAD_DIM, 1) if hh else x) for x in (q2, k2, v2))
                f = f2 if hh else pltpu.roll(f2, HEAD_DIM, 1)
                hi, mid, lo = _split3(f)
                q_aug = jnp.where(is_lane[3], hi, jnp.where(is_lane[4], mid, jnp.where(is_lane[5], lo, q_const)))
                k_aug = jnp.where(is_lane[0], hi, jnp.where(is_lane[1], mid, jnp.where(is_lane[2], lo, k_const)))
                kx = jnp.where(head_lanes, k, k_aug)
                vx = jnp.where(head_lanes, v, v_const)
                qx_ref[h] = jnp.where(head_lanes, q, q_aug).astype(BF16)
                kx_ref[h] = kx.astype(BF16)
                vx_ref[h] = vx.astype(BF16)
                kxt_ref[h, 0] = kx.T.astype(BF16)
                vt_ref[h, 0] = vx.T.astype(BF16)
                q_sq = lax.dot_general((q * q).astype(BF16), ones_head, _NN, preferred_element_type=F32)
                k_sq = lax.dot_general((k * k).astype(BF16), ones_head, _NN, preferred_element_type=F32)
                diag = lax.dot_general((q * k).astype(BF16), ones_head, _NN, preferred_element_type=F32)
                diag = diag - jnp.sqrt(q_sq * k_sq) * (NORM_MARGIN - 1.0)
                vals = (jnp.sqrt(jnp.max(q_sq, axis=0, keepdims=True)), jnp.sqrt(jnp.max(k_sq, axis=0, keepdims=True)),
                        jnp.max(f - diag, axis=0, keepdims=True), f[tq - 1:tq, :])
                for slot, val in enumerate(vals):
                    acc = jnp.where(lane8 == slot * N_HEADS + h, val[:, AUG:AUG + 1], acc)
        b_ref[0] = acc
        finish()

    blk = lambda j: pl.BlockSpec((tq, ATTN_W), lambda i: (i, j))
    rows = pl.BlockSpec((N_HEADS, tq, LANES), lambda i: (0, i, 0))
    cols_t = pl.BlockSpec((N_HEADS, 1, LANES, tq), lambda i: (0, i, 0, 0))
    shp = jax.ShapeDtypeStruct((N_HEADS, s, LANES), BF16)
    shp_t = jax.ShapeDtypeStruct((N_HEADS, n_q, LANES, tq), BF16)
    outs = pl.pallas_call(
        body, name="attn_prep", grid=(n_q,), in_specs=[blk(0), blk(1), blk(2), blk(0)] + _hbm_specs(2 * n),
        out_specs=[rows, rows, cols_t, rows, cols_t,
                   pl.BlockSpec((1, HALO, LANES), lambda i: (i, 0, 0))] + _hbm_specs(n),
        out_shape=[shp, shp, shp_t, shp, shp_t, jax.ShapeDtypeStruct((n_q, HALO, LANES), F32)]
        + [jax.ShapeDtypeStruct(b.shape, b.dtype) for b in landing],
        input_output_aliases={4 + n + k: 6 + k for k in range(n)}, scratch_shapes=sems,
        compiler_params=_cp(("arbitrary",)))(qkv, qkv, qkv, fb, *arrays, *landing)
    return tuple(outs[:6]) + (outs[6:],)


def _key_block_ranges(bounds):
    t = bounds[:, 0, :]
    nh = N_HEADS
    a, b, c, e = t[:, 0:nh], t[:, nh:2 * nh], t[:, 2 * nh:3 * nh], t[:, 3 * nh:4 * nh]
    bound = a[:, None, :] * b[None, :, :] * NORM_MARGIN + c[:, None, :] - e[None, :, :]
    n_q = t.shape[0]
    idx = jnp.arange(n_q)
    need = jnp.logical_not(bound < -(EXP_ZERO + 2.0)) | (idx[None, :, None] >= idx[:, None, None])
    first = jnp.argmax(need, axis=1).astype(jnp.int32)
    first = jnp.min(first.reshape(n_q, N_PAIRS, 2), axis=-1)
    visits = (first[:, None, :] <= idx[None, :, None]) & (idx[None, :, None] <= idx[:, None, None])
    last = jnp.max(jnp.where(visits, idx[:, None, None], 0), axis=0).astype(jnp.int32)
    return first.T.reshape(-1), last.T.reshape(-1)


def _attn_fwd_t(qx, kx, vt, first_blk):
    _, s, _ = qx.shape
    tq = min(TQ, s)
    n_q = s // tq
    neg = -1e30

    def body(first_ref, qx_ref, kx_ref, vt_ref, o_ref, lse_ref, acc_ref, m_ref, s_even, s_odd):
        p = pl.program_id(0)
        i = pl.program_id(1)
        acc_ref[...] = jnp.zeros(acc_ref.shape, F32)
        m_ref[...] = jnp.full(m_ref.shape, neg, F32)
        key_le_query = (lax.broadcasted_iota(jnp.int32, (tq, tq), 0) <= lax.broadcasted_iota(jnp.int32, (tq, tq), 1))
        first = first_ref[p * n_q + i]

        def scores(kb, hh, dst):
            rows_k = pl.ds(pl.multiple_of(kb * tq, tq), tq)
            dst[hh] = lax.dot_general(kx_ref[hh, rows_k, :], qx_ref[hh], _NT, preferred_element_type=F32)

        def step(kb, src, nxt):
            for hh in range(2):
                st = src[hh]
                if nxt is None:
                    st = jnp.where(key_le_query, st, -jnp.inf)
                else:
                    scores(kb + 1, hh, nxt)
                m_old = m_ref[hh]
                m_new = jnp.maximum(m_old, jnp.max(st, axis=0, keepdims=True))
                m_ref[hh] = m_new
                pt = jnp.exp(st - m_new).astype(BF16)
                acc_ref[hh] = acc_ref[hh] * jnp.exp(m_old - m_new) + lax.dot_general(
                    vt_ref[hh, kb], pt, _NN, preferred_element_type=F32)

        def by_parity(kb, fn):
            @pl.when(kb % 2 == 0)
            def _():
                fn(s_even, s_odd)

            @pl.when(kb % 2 == 1)
            def _():
                fn(s_odd, s_even)

        def first_scores(src, nxt):
            scores(first, 0, src)
            scores(first, 1, src)

        def unmasked(kb, carry):
            by_parity(kb, lambda src, nxt: step(kb, src, nxt))
            return carry

        by_parity(first, first_scores)
        lax.fori_loop(first, i, unmasked, 0)
        by_parity(i, lambda src, nxt: step(i, src, None))
        outs, lses = [], []
        for hh in range(2):
            acc = acc_ref[hh]
            l = acc[AUG:AUG + 1, :]
            outs.append(acc[0:HEAD_DIM, :] / l)
            lses.append(m_ref[hh] + jnp.log(l))
        o_ref[...] = jnp.concatenate(outs, axis=0).T
        rows8 = lax.broadcasted_iota(jnp.int32, (N_HEADS, tq), 0)
        lse_ref[0, 0] = jnp.where(rows8 == 0, lses[0], jnp.where(rows8 == 1, lses[1], 0.0))

    grid_spec = pltpu.PrefetchScalarGridSpec(
        num_scalar_prefetch=1, grid=(N_PAIRS, n_q),
        in_specs=[pl.BlockSpec((2, tq, LANES), lambda p, i, first: (p, i, 0)),
                  pl.BlockSpec((2, s, LANES), lambda p, i, first: (p, 0, 0)),
                  pl.BlockSpec((2, n_q, LANES, tq), lambda p, i, first: (p, 0, 0, 0))],
        out_specs=[pl.BlockSpec((tq, LANES), lambda p, i, first: (i, p)),
                   pl.BlockSpec((1, 1, N_HEADS, tq), lambda p, i, first: (p, i, 0, 0))],
        scratch_shapes=[pltpu.VMEM((2, LANES, tq), F32), pltpu.VMEM((2, 1, tq), F32),
                        pltpu.VMEM((2, tq, tq), F32), pltpu.VMEM((2, tq, tq), F32)])
    return pl.pallas_call(
        body, name="attn_fwd", grid_spec=grid_spec,
        out_shape=[jax.ShapeDtypeStruct((s, ATTN_W), F32), jax.ShapeDtypeStruct((N_PAIRS, n_q, N_HEADS, tq), F32)],
        compiler_params=_cp(("parallel", "arbitrary")))(first_blk, qx, kx, vt)


def _attn_bwd_t(qx, dox, kx, kxt, vx, lse, last_blk):
    _, s, _ = qx.shape
    tq = min(TQ, s)
    n_q = s // tq

    def body(last_ref, qx_ref, dox_ref, lse_ref, kx_ref, kxt_ref, vx_ref, dk_ref, dv_ref, dfk_ref, dqt_ref, dfq_ref):
        p = pl.program_id(0)
        j = pl.program_id(1)

        @pl.when(j == 0)
        def _():
            dqt_ref[...] = jnp.zeros(dqt_ref.shape, F32)
            dfq_ref[...] = jnp.zeros(dfq_ref.shape, F32)

        key_le_query = (lax.broadcasted_iota(jnp.int32, (tq, tq), 0) <= lax.broadcasted_iota(jnp.int32, (tq, tq), 1))

        def step(i, carry, masked):
            rows_q = pl.ds(pl.multiple_of(i * tq, tq), tq)
            out = []
            for hh in range(2):
                dk, dv, col = carry[3 * hh:3 * hh + 3]
                q, do = qx_ref[hh, rows_q, :], dox_ref[hh, rows_q, :]
                st = lax.dot_general(kx_ref[hh], q, _NT, preferred_element_type=F32)
                pt = jnp.exp(st - lse_ref[0, i, hh:hh + 1, :])
                if masked:
                    pt = jnp.where(key_le_query, pt, 0.0)
                dst = pt * lax.dot_general(vx_ref[hh], do, _NT, preferred_element_type=F32)
                pb, dsb = pt.astype(BF16), dst.astype(BF16)
                dv = dv + lax.dot_general(pb, do, _NN, preferred_element_type=F32)
                dk = dk + lax.dot_general(dsb, q, _NN, preferred_element_type=F32)
                dqt_ref[hh, i] += lax.dot_general(kxt_ref[hh, 0], dsb, _NN, preferred_element_type=F32)
                for cb in range(tq // LANES):
                    col = col + dst[:, cb * LANES:(cb + 1) * LANES]
                dfq_ref[hh, i] += jnp.sum(dst.reshape(tq // HALO, HALO, tq), axis=0)
                out += [dk, dv, col]
            return tuple(out)

        zero = jnp.zeros((tq, LANES), F32)
        carry = step(j, (zero,) * 6, True)
        dk_a, dv_a, col_a, dk_b, dv_b, col_b = lax.fori_loop(j + 1, last_ref[p * n_q + j] + 1,
                                                             lambda i, cr: step(i, cr, False), carry)
        head_a = lax.broadcasted_iota(jnp.int32, (tq, LANES), 1) < HEAD_DIM
        dk_ref[...] = jnp.where(head_a, dk_a, pltpu.roll(dk_b, HEAD_DIM, 1)).astype(BF16)
        dv_ref[...] = jnp.where(head_a, dv_a, pltpu.roll(dv_b, HEAD_DIM, 1)).astype(BF16)
        rows8 = lax.broadcasted_iota(jnp.int32, (N_HEADS, tq), 0)
        dfk_a, dfk_b = (-jnp.sum(c.T, axis=0, keepdims=True) for c in (col_a, col_b))
        dfk_ref[0, 0] = jnp.where(rows8 == 0, dfk_a, jnp.where(rows8 == 1, dfk_b, 0.0))

    resident = pl.BlockSpec((2, s, LANES), lambda p, j, last: (p, 0, 0))
    key_rows = pl.BlockSpec((2, tq, LANES), lambda p, j, last: (p, j, 0))
    pair_out = pl.BlockSpec((tq, LANES), lambda p, j, last: (j, p))
    grid_spec = pltpu.PrefetchScalarGridSpec(
        num_scalar_prefetch=1, grid=(N_PAIRS, n_q),
        in_specs=[resident, resident, pl.BlockSpec((1, n_q, N_HEADS, tq), lambda p, j, last: (p, 0, 0, 0)),
                  key_rows, pl.BlockSpec((2, 1, LANES, tq), lambda p, j, last: (p, j, 0, 0)), key_rows],
        out_specs=[pair_out, pair_out, pl.BlockSpec((1, 1, N_HEADS, tq), lambda p, j, last: (p, j, 0, 0)),
                   pl.BlockSpec((2, n_q, LANES, tq), lambda p, j, last: (p, 0, 0, 0)),
                   pl.BlockSpec((2, n_q, HALO, tq), lambda p, j, last: (p, 0, 0, 0))])
    return pl.pallas_call(
        body, name="attn_bwd", grid_spec=grid_spec,
        out_shape=[jax.ShapeDtypeStruct((s, ATTN_W), BF16), jax.ShapeDtypeStruct((s, ATTN_W), BF16),
                   jax.ShapeDtypeStruct((N_PAIRS, n_q, N_HEADS, tq), F32),
                   jax.ShapeDtypeStruct((N_HEADS, n_q, LANES, tq), F32),
                   jax.ShapeDtypeStruct((N_HEADS, n_q, HALO, tq), F32)],
        compiler_params=_cp(("parallel", "arbitrary")))(last_blk, qx, dox, lse, kx, kxt, vx)


def _attn_dq_finish(dqt):
    _, n_q, _, tq = dqt.shape
    per = 4 if n_q % 4 == 0 else 1

    def body(dqt_ref, dq_ref):
        for b in range(per):
            a, bb = dqt_ref[0, b], dqt_ref[1, b]
            dq_ref[b * tq:(b + 1) * tq, :] = (
                jnp.concatenate([a[0:HEAD_DIM], bb[0:HEAD_DIM]], axis=0).T * Q_SCALE).astype(BF16)

    return pl.pallas_call(
        body, name="attn_dq_finish", grid=(N_PAIRS, n_q // per),
        in_specs=[pl.BlockSpec((2, per, LANES, tq), lambda p, i: (p, i, 0, 0))],
        out_specs=pl.BlockSpec((per * tq, LANES), lambda p, i: (i, p)),
        out_shape=jax.ShapeDtypeStruct((n_q * tq, ATTN_W), BF16),
        compiler_params=_cp(("parallel", "parallel")))(dqt)


def _ffn_act_fwd(up, w_ffn):
    s = up.shape[0]
    tm, tn = min(TM_FFN, s), TN_FFN
    nb = D_FF // tn

    def body(a_ref, g_ref, ap_ref, gp_ref, wa_ref, wg_ref, act_ref, u_ref):
        i = pl.program_id(1)

        def conv(blk_ref, prev_ref, w_ref):
            prev = jnp.where(i > 0, prev_ref[...].astype(F32)[HALO:], 0.0)
            return _conv_taps(jnp.concatenate([prev, blk_ref[...].astype(F32)], axis=0), w_ref[...])[HALO:]

        u_a, u_g = conv(a_ref, ap_ref, wa_ref), conv(g_ref, gp_ref, wg_ref)
        u_ref[0], u_ref[1] = u_a.astype(BF16), u_g.astype(BF16)
        act_ref[...] = (u_g * jax.nn.sigmoid(u_g) * u_a).astype(BF16)

    blk = lambda off: pl.BlockSpec((tm, tn), lambda n, i: (i, off + n))
    prev = lambda off: pl.BlockSpec(
        (HALO_BF16, tn), lambda n, i: (jnp.maximum(i * (tm // HALO_BF16) - 1, 0), off + n))
    wsp = lambda off: pl.BlockSpec((3, tn), lambda n, i: (0, off + n))
    return pl.pallas_call(
        body, name="ffn_act_fwd", grid=(nb, s // tm),
        in_specs=[blk(0), blk(nb), prev(0), prev(nb), wsp(0), wsp(nb)],
        out_specs=[pl.BlockSpec((tm, tn), lambda n, i: (i, n)), pl.BlockSpec((2, tm, tn), lambda n, i: (0, i, n))],
        out_shape=[jax.ShapeDtypeStruct((s, D_FF), BF16), jax.ShapeDtypeStruct((2, s, D_FF), BF16)],
        compiler_params=_cp(("parallel", "parallel")))(up, up, up, up, w_ffn, w_ffn)


def _ffn_act_bwd(up, u, dact, w_ffn):
    s = up.shape[0]
    tm, tn = min(TM_FFN, s), TN_FFN
    nb = D_FF // tn
    n_blk = s // tm

    def body(u_ref, un_ref, a_ref, g_ref, d_ref, dn_ref, wa_ref, wg_ref, dup_ref, gwa_ref, gwg_ref):
        i = pl.program_id(1)

        @pl.when(i == 0)
        def _():
            gwa_ref[...] = jnp.zeros_like(gwa_ref)
            gwg_ref[...] = jnp.zeros_like(gwg_ref)

        ext = lambda rows, next_rows: jnp.concatenate([rows, next_rows], axis=0)
        u_a, u_g = (ext(u_ref[h].astype(F32), un_ref[h].astype(F32)[:HALO]) for h in range(2))
        d_e = ext(d_ref[...], jnp.where(i < n_blk - 1, dn_ref[...], 0.0))
        sig = jax.nn.sigmoid(u_g)
        du_a = d_e * (u_g * sig)
        du_g = d_e * u_a * (sig * (1.0 + u_g * (1.0 - sig)))
        halves = ((gwa_ref, wa_ref[...], a_ref[...].astype(F32), du_a),
                  (gwg_ref, wg_ref[...], g_ref[...].astype(F32), du_g))
        for half, (gw_ref, w, upv, du) in enumerate(halves):
            du0, du1, du2 = du[:tm], _shift_up(du, 1)[:tm], _shift_up(du, 2)[:tm]
            dup_ref[half] = (w[2:3, :] * du0 + w[1:2, :] * du1 + w[0:1, :] * du2).astype(BF16)
            gw_ref[0:1, :] += jnp.sum(upv * du2, axis=0, keepdims=True)
            gw_ref[1:2, :] += jnp.sum(upv * du1, axis=0, keepdims=True)
            gw_ref[2:3, :] += jnp.sum(upv * du0, axis=0, keepdims=True)

    next_row = lambda halo: lambda i: jnp.minimum((i + 1) * (tm // halo), s // halo - 1)
    blk = lambda off: pl.BlockSpec((tm, tn), lambda n, i: (i, off + n))
    wsp = lambda off: pl.BlockSpec((3, tn), lambda n, i: (0, off + n))
    pair = lambda rows, row_of: pl.BlockSpec((2, rows, tn), lambda n, i: (0, row_of(i), n))
    return pl.pallas_call(
        body, name="ffn_act_bwd", grid=(nb, n_blk),
        in_specs=[pair(tm, lambda i: i), pair(HALO_BF16, next_row(HALO_BF16)), blk(0), blk(nb), blk(0),
                  pl.BlockSpec((HALO, tn), lambda n, i: (next_row(HALO)(i), n)), wsp(0), wsp(nb)],
        out_specs=[pair(tm, lambda i: i), wsp(0), wsp(0)],
        out_shape=[jax.ShapeDtypeStruct((2, s, D_FF), BF16),
                   jax.ShapeDtypeStruct((3, D_FF), F32), jax.ShapeDtypeStruct((3, D_FF), F32)],
        compiler_params=_cp(("parallel", "arbitrary")))(u, u, up, up, dact, dact, w_ffn, w_ffn)


def _adamw(w, g, m, v, name):
    r, c = g.shape
    tr = next((t for t in (512, 352, 256, 128, 64, 32, 16, 8) if r > t and r % t == 0), r)

    def body(w_ref, g_ref, m_ref, v_ref, d_ref, nm_ref, nv_ref):
        gv = g_ref[...]
        m_new = ADAM_B1 * m_ref[...] + (1.0 - ADAM_B1) * gv
        v_new = ADAM_B2 * v_ref[...] + (1.0 - ADAM_B2) * (gv * gv)
        m_hat = m_new / (1.0 - ADAM_B1 ** ADAM_STEP)
        v_hat = v_new / (1.0 - ADAM_B2 ** ADAM_STEP)
        d_ref[...] = -ADAM_LR * (m_hat / (jnp.sqrt(v_hat) + ADAM_EPS) + ADAM_WD * w_ref[...])
        nm_ref[...] = m_new
        nv_ref[...] = v_new

    tc = c if tr < r else next((t for t in (256, 128) if c > t and c % t == 0), c)
    pos = (lambda i: (i, 0)) if tr < r else (lambda i: (0, i))
    spec = pl.BlockSpec((tr, tc), pos)
    w_spec = spec if w.ndim == 2 else pl.BlockSpec((None, tr, tc), lambda i: (0, *pos(i)))
    shp = jax.ShapeDtypeStruct(w.shape, F32)
    return pl.pallas_call(
        body, name=name, grid=(r // tr * (c // tc),), in_specs=[w_spec, spec, w_spec, w_spec], out_specs=[w_spec] * 3,
        out_shape=[shp] * 3, compiler_params=_cp(("parallel",)))(w, g, m, v)


def _sum_rows_block(h):
    return h if h <= 352 else 256


def _pair_sum(view, recv, sel, name):
    n, _, h, c = view.shape
    tr = _sum_rows_block(h)

    def body(sel_ref, a_ref, b_ref, o_ref, ob_ref):
        t = a_ref[...] + b_ref[...]
        o_ref[...] = t
        ob_ref[...] = t.astype(BF16)

    blk = pl.BlockSpec((None, tr, c), lambda j, i, sel_ref: (j, i, 0))
    grid_spec = pltpu.PrefetchScalarGridSpec(
        num_scalar_prefetch=1, grid=(n, h // tr),
        in_specs=[pl.BlockSpec((None, None, tr, c), lambda j, i, sel_ref: (j, sel_ref[0], i, 0)),
                  pl.BlockSpec((None, None, tr, c), lambda j, i, sel_ref: (j, 0, i, 0))],
        out_specs=[blk, blk])
    return pl.pallas_call(
        body, name=name, grid_spec=grid_spec,
        out_shape=[jax.ShapeDtypeStruct((n, h, c), F32), jax.ShapeDtypeStruct((n, h, c), BF16)],
        compiler_params=_cp(("parallel", "parallel")))(sel, view, recv)


def _chip_sum(pair, got, sel, name):
    _, h, c = pair.shape
    tr = _sum_rows_block(h)
    nblk = h // tr

    def body(sel_ref, p_ref, g0_ref, g1_ref, g2_ref, o_ref):
        o_ref[...] = ((p_ref[...] + g0_ref[...].astype(F32)) + g1_ref[...].astype(F32)) + g2_ref[...].astype(F32)

    slot = lambda k: pl.BlockSpec((None, tr, c), lambda i, sel_ref: (k, i, 0))
    grid_spec = pltpu.PrefetchScalarGridSpec(
        num_scalar_prefetch=1, grid=(h // tr,),
        in_specs=[pl.BlockSpec((None, tr, c), lambda i, sel_ref: (sel_ref[1], i, 0)), slot(0), slot(1), slot(2)],
        out_specs=pl.BlockSpec((tr, c), lambda i, sel_ref: (sel_ref[0] * nblk + i, 0)))
    return pl.pallas_call(
        body, name=name, grid_spec=grid_spec, out_shape=jax.ShapeDtypeStruct((2 * h, c), F32),
        compiler_params=_cp(("parallel",)))(sel, pair, got, got, got)


def _place():
    return lax.axis_index("x"), lax.axis_index("y"), lax.axis_index("c")


def _other_chips(x, y):
    return [(1 - x, y), (x, 1 - y), (1 - x, 1 - y)]


def _hbm_specs(n):
    return [pl.BlockSpec(memory_space=pl.ANY)] * n


def _all_gather_weights(bigs, smalls):
    nb, ns = len(bigs), len(smalls)
    n = nb + ns

    def body(*refs):
        start, forward, finish = _gather_phases(refs[:n], refs[2 * n:3 * n], nb, *refs[3 * n:])
        start()
        forward()
        finish()

    arrays, landing, sems = _gather_operands(bigs, smalls)
    return pl.pallas_call(
        body, name="all_gather_weights",
        out_shape=[jax.ShapeDtypeStruct(b.shape, b.dtype) for b in landing],
        in_specs=_hbm_specs(2 * n), out_specs=_hbm_specs(n), input_output_aliases={n + k: k for k in range(n)},
        scratch_shapes=sems)(*arrays, *landing)


def _hosted_gather(refs, n, nb, step, total):
    ins, outs, send_sems, recv_sems = refs
    start, forward, finish = _gather_phases(ins, outs, nb, send_sems, recv_sems)
    pl.when(step == 0)(start)
    pl.when(step == (3 * total) // 4)(forward)
    return lambda: pl.when(step == total - 1)(finish)


def _in_proj(x, g, w_a, w_b, w_c, bigs, smalls):
    s, d = x.shape
    tm, tn = min(TM_MM, s), ATTN_W
    na, nq = w_a.shape[1] // tn, w_b.shape[1] // tn
    steps = na + nq + 1
    total = (s // tm) * steps
    nb, n = len(bigs), len(bigs) + len(smalls)
    arrays, landing, sems = _gather_operands(bigs, smalls)

    def body(x_ref, g_ref, wa_ref, wb_ref, wc_ref, *rest):
        z_ref, qkv_ref, f_ref, h_ref = rest[2 * n:2 * n + 4]
        h_scr = rest[-1]
        m, j = pl.program_id(0), pl.program_id(1)
        finish = _hosted_gather((rest[:n], rest[2 * n + 4:3 * n + 4]) + tuple(rest[3 * n + 4:3 * n + 6]), n, nb,
                                m * steps + j, total)

        @pl.when(j == 0)
        def _():
            xv = x_ref[...]
            hv = (xv * _rstd(xv) * g_ref[...]).astype(BF16)
            h_scr[...] = hv
            h_ref[...] = hv

        h = h_scr[...]

        @pl.when(j < na)
        def _():
            z_ref[...] = lax.dot_general(h, wa_ref[...], _NN, preferred_element_type=F32)

        @pl.when((j >= na) & (j < na + nq))
        def _():
            qkv_ref[...] = lax.dot_general(h, wb_ref[...], _NN, preferred_element_type=F32).astype(BF16)

        @pl.when(j == na + nq)
        def _():
            f_ref[...] = lax.dot_general(h, wc_ref[...], _NN, preferred_element_type=F32)

        finish()

    blk_a = lambda m, j: (m, jnp.minimum(j, na - 1))
    blk_b = lambda m, j: (m, jnp.clip(j - na, 0, nq - 1))
    outs = pl.pallas_call(
        body, name="in_proj", grid=(s // tm, steps),
        in_specs=[pl.BlockSpec((tm, d), lambda m, j: (m, 0)), pl.BlockSpec((1, d), lambda m, j: (0, 0)),
                  pl.BlockSpec((d, tn), lambda m, j: (0, jnp.minimum(j, na - 1))),
                  pl.BlockSpec((d, tn), lambda m, j: (0, jnp.clip(j - na, 0, nq - 1))),
                  pl.BlockSpec((d, LANES), lambda m, j: (0, 0))] + _hbm_specs(2 * n),
        out_specs=[pl.BlockSpec((tm, tn), blk_a), pl.BlockSpec((tm, tn), blk_b),
                   pl.BlockSpec((tm, LANES), lambda m, j: (m, 0)), pl.BlockSpec((tm, d), lambda m, j: (m, 0))]
        + _hbm_specs(n),
        out_shape=[jax.ShapeDtypeStruct((s, w_a.shape[1]), F32), jax.ShapeDtypeStruct((s, w_b.shape[1]), BF16),
                   jax.ShapeDtypeStruct((s, LANES), F32), jax.ShapeDtypeStruct((s, d), BF16)]
        + [jax.ShapeDtypeStruct(b.shape, b.dtype) for b in landing],
        input_output_aliases={5 + n + k: 4 + k for k in range(n)},
        scratch_shapes=sems + [pltpu.VMEM((tm, d), BF16)],
        compiler_params=_cp(("arbitrary", "arbitrary")))(x, g, w_a, w_b, w_c, *arrays, *landing)
    return outs[0], outs[1], outs[2], outs[3], outs[4:]


def _gather_operands(bigs, smalls):
    x, y, _ = _place()
    arrays = list(bigs) + list(smalls)
    landing = [lax.dynamic_update_index_in_dim(lax.empty((N_CHIPS,) + a.shape, a.dtype), a, 2 * x + y, 0)
               for a in arrays]
    n_sems = 6 * len(bigs) + 3 * len(smalls)
    return arrays, landing, [pltpu.SemaphoreType.DMA((n_sems,)), pltpu.SemaphoreType.DMA((n_sems,))]


def _gather_phases(ins, outs, nb, send_sems, recv_sems):
    n = len(ins)
    x, y, c = _place()
    my_chip = 2 * x + y
    chips = _other_chips(x, y)
    sibling = (x, y, 1 - c)

    def rows(k, which):
        h = ins[k].shape[0] // 2
        return pl.ds(which * h, h)

    def copy(sem, src, dst, to):
        return pltpu.make_async_remote_copy(src_ref=src, dst_ref=dst, send_sem=send_sems.at[sem],
                                            recv_sem=recv_sems.at[sem], device_id=to, device_id_type=MESH)

    def sends():
        out = [copy(6 * k + j, ins[k].at[rows(k, c)], outs[k].at[my_chip, rows(k, c)], (cx, cy, c))
               for k in range(nb) for j, (cx, cy) in enumerate(chips)]
        return out + [copy(6 * nb + 3 * (k - nb) + j, ins[k], outs[k].at[my_chip], (cx, cy, c))
                      for k in range(nb, n) for j, (cx, cy) in enumerate(chips)]

    def landed(k, j, which):
        cx, cy = chips[j]
        return outs[k].at[2 * cx + cy, rows(k, which)]

    def forwards():
        return [copy(6 * k + 3 + j, landed(k, j, c), landed(k, j, c), sibling)
                for j in range(3) for k in range(nb)]

    def start():
        for cp in sends():
            cp.start()

    def forward():
        for j in range(3):
            for k in range(nb):
                copy(6 * k + j, landed(k, j, c), landed(k, j, c), (x, y, c)).wait_recv()
                copy(6 * k + 3 + j, landed(k, j, c), landed(k, j, c), sibling).start()

    def finish():
        for j, (cx, cy) in enumerate(chips):
            for k in range(nb):
                copy(6 * k + 3 + j, landed(k, j, 1 - c), landed(k, j, 1 - c), (x, y, c)).wait_recv()
            for k in range(nb, n):
                arrived = outs[k].at[2 * cx + cy]
                copy(6 * nb + 3 * (k - nb) + j, arrived, arrived, (x, y, c)).wait_recv()
        for cp in sends() + forwards():
            cp.wait_send()

    return start, forward, finish


def _pair_exchange(views, name):
    n = len(views)

    def body(*refs):
        ins, outs, send_sems, recv_sems = refs[:n], refs[n:2 * n], refs[2 * n], refs[2 * n + 1]
        x, y, c = _place()
        copies = [pltpu.make_async_remote_copy(
            src_ref=ins[k].at[:, pl.ds(1 - c, 1)], dst_ref=outs[k], send_sem=send_sems.at[k],
            recv_sem=recv_sems.at[k], device_id=(x, y, 1 - c), device_id_type=MESH) for k in range(n)]
        for cp in copies:
            cp.start()
        for cp in copies:
            cp.wait()

    return pl.pallas_call(
        body, name=name,
        out_shape=[jax.ShapeDtypeStruct((v.shape[0], 1) + v.shape[2:], v.dtype) for v in views],
        in_specs=_hbm_specs(n), out_specs=_hbm_specs(n),
        scratch_shapes=[pltpu.SemaphoreType.DMA((n,)), pltpu.SemaphoreType.DMA((n,))])(*views)


def _scatter_operands(parts):
    n = len(parts)
    return ([jax.ShapeDtypeStruct((3,) + p.shape[1:], p.dtype) for p in parts],
            [pltpu.SemaphoreType.DMA((3 * n,)), pltpu.SemaphoreType.DMA((3 * n,))])


def _scatter_phases(ins, outs, send_sems, recv_sems):
    x, y, c = _place()

    def copies():
        return [pltpu.make_async_remote_copy(
            src_ref=ins[k].at[pl.ds(2 * cx + cy, 1)], dst_ref=outs[k].at[pl.ds(r, 1)], send_sem=send_sems.at[3 * k + r],
            recv_sem=recv_sems.at[3 * k + r], device_id=(cx, cy, c), device_id_type=MESH)
            for k in range(len(ins)) for r, (cx, cy) in enumerate(_other_chips(x, y))]

    def start():
        for cp in copies():
            cp.start()

    def finish():
        for cp in copies():
            cp.wait()

    return start, finish


def _hosted_scatter(ins, outs, sems, step, total):
    start, finish = _scatter_phases(ins, outs, *sems)
    pl.when(step == 0)(start)
    return lambda: pl.when(step == total - 1)(finish)


def _join_halves(shards):
    n = len(shards)

    def body(*refs):
        ins, outs, send_sems, recv_sems = refs[:n], refs[n:2 * n], refs[2 * n], refs[2 * n + 1]
        x, y, c = _place()

        def rows(ref, which):
            h = ref.shape[0] // 2
            return ref.at[pl.ds(which * h, h)]

        sent = [pltpu.make_async_remote_copy(
            src_ref=rows(ins[k], c), dst_ref=rows(outs[k], c), send_sem=send_sems.at[k], recv_sem=recv_sems.at[k],
            device_id=(x, y, 1 - c), device_id_type=MESH) for k in range(n)]
        for cp in sent:
            cp.start()
        for k in range(n):
            pltpu.make_async_remote_copy(
                src_ref=rows(ins[k], 1 - c), dst_ref=rows(outs[k], 1 - c), send_sem=send_sems.at[k],
                recv_sem=recv_sems.at[k], device_id=(x, y, 1 - c), device_id_type=MESH).wait_recv()
        for cp in sent:
            cp.wait_send()

    return pl.pallas_call(
        body, name="half_exchange", out_shape=[jax.ShapeDtypeStruct(a.shape, a.dtype) for a in shards],
        in_specs=_hbm_specs(n), out_specs=_hbm_specs(n), input_output_aliases={k: k for k in range(n)},
        scratch_shapes=[pltpu.SemaphoreType.DMA((n,)), pltpu.SemaphoreType.DMA((n,))])(*shards)


def _all_reduce_small(packet):
    rows, width = packet.shape
    n_dev = 8

    def body(x_ref, out_ref, gath, send_sems, recv_sems):
        x, y, c = _place()
        me, sibling = (x, y, c), (x, y, 1 - c)
        chips = _other_chips(x, y)

        def slot(px, py, pc):
            return gath.at[pl.ds((4 * px + 2 * py + pc) * rows, rows), :]

        def copy(k, block, to, src=None):
            return pltpu.make_async_remote_copy(
                src_ref=slot(*block) if src is None else src, dst_ref=slot(*block), send_sem=send_sems.at[k],
                recv_sem=recv_sems.at[k], device_id=to, device_id_type=MESH)

        first = [copy(0, me, sibling, src=x_ref)]
        first += [copy(1 + j, me, (*chip, c), src=x_ref) for j, chip in enumerate(chips)]
        for cp in first:
            cp.start()
        gath[pl.ds((4 * x + 2 * y + c) * rows, rows), :] = x_ref[...]
        passed = [copy(4 + j, (*chip, c), sibling) for j, chip in enumerate(chips)]
        for j, chip in enumerate(chips):
            copy(1 + j, (*chip, c), me).wait_recv()
            passed[j].start()
        copy(0, sibling, me).wait_recv()
        for j, chip in enumerate(chips):
            copy(4 + j, (*chip, 1 - c), me).wait_recv()
        for cp in first + passed:
            cp.wait_send()
        acc = gath[0:rows, :]
        for d in range(1, n_dev):
            acc = acc + gath[d * rows:(d + 1) * rows, :]
        out_ref[...] = acc

    return pl.pallas_call(
        body, name="all_reduce_small", out_shape=jax.ShapeDtypeStruct((rows, width), F32),
        in_specs=[pl.BlockSpec(memory_space=pltpu.VMEM)], out_specs=pl.BlockSpec(memory_space=pltpu.VMEM),
        scratch_shapes=[pltpu.VMEM((n_dev * rows, width), F32), pltpu.SemaphoreType.DMA((7,)),
                        pltpu.SemaphoreType.DMA((7,))])(packet)


def _flat_rows(parts, width, row_multiple):
    flat = jnp.concatenate([p.astype(F32).reshape(-1) for p in parts])
    rows = -(-flat.shape[0] // width)
    rows = -(-rows // row_multiple) * row_multiple
    return jnp.pad(flat, (0, rows * width - flat.shape[0])).reshape(rows, width)


def _unflatten(flat2d, shapes):
    flat = flat2d.reshape(-1)
    out, off = [], 0
    for shp in shapes:
        n = 1
        for dim in shp:
            n *= dim
        out.append(flat[off:off + n].reshape(shp))
        off += n
    return out


def _core_and_chip():
    x, y, c = _place()
    return jnp.stack([c, 2 * x + y]).astype(jnp.int32)


def _pair_sums(chip_major, names, call_name):
    views = [g.reshape(N_CHIPS, 2, g.shape[1] // 2, g.shape[2]) for g in chip_major]
    recv = _pair_exchange(views, call_name)
    sel = _core_and_chip()
    return [_pair_sum(v, r, sel, "pair_sum_" + nm) for v, r, nm in zip(views, recv, names)]


def _finish_grads(pairs, got, names):
    sel = _core_and_chip()
    return _join_halves([_chip_sum(p, g, sel, "chip_sum_" + nm) for (p, _), g, nm in zip(pairs, got, names)])


def kernel(x, g_mix, w_in, b_f, w_conv, g_conv_out, g_attn_out, w_o, g_ffn, w_up, w_ffn_conv, w_down, g_final, loss_target, m_g_mix, m_w_in, m_b_f, m_w_conv, m_g_conv_out, m_g_attn_out, m_w_o, m_g_ffn, m_w_up, m_w_ffn_conv, m_w_down, m_g_final, v_g_mix, v_w_in, v_b_f, v_w_conv, v_g_conv_out, v_g_attn_out, v_w_o, v_g_ffn, v_w_up, v_w_ffn_conv, v_w_down, v_g_final):
    s = x.shape[1]
    x0 = x[0]
    target = loss_target[0]
    d = D_MODEL
    x_pos, y_pos, _ = _place()
    my_chip = 2 * x_pos + y_pos

    (c_in,) = _all_gather_weights([w_in[0].astype(BF16)], [])
    shard_w = c_in.shape[2]

    def w_in_cols(lo, hi):
        parts = [c_in[j][:, max(lo - j * shard_w, 0):min(hi - j * shard_w, shard_w)]
                 for j in range(N_CHIPS) if lo < (j + 1) * shard_w and hi > j * shard_w]
        return parts[0] if len(parts) == 1 else jnp.concatenate(parts, axis=1)

    c3 = 3 * CONV_CH
    w_a, w_b = w_in_cols(0, c3), w_in_cols(c3, c3 + 3 * ATTN_W)
    w_c = jnp.pad(w_in_cols(c3 + 3 * ATTN_W, N_CHIPS * shard_w), ((0, 0), (0, LANES - N_HEADS)))
    w_q, w_k, w_v = (w_in_cols(c3 + i * ATTN_W, c3 + (i + 1) * ATTN_W) for i in range(3))
    b_pad = jnp.pad(b_f, ((0, 0), (0, LANES - N_HEADS)))

    z_a, qkv, f_log, h1, (c_o, c_up, c_conv, c_ffn) = _in_proj(
        x0, g_mix, w_a, w_b, w_c, [w_o[0].astype(BF16), w_up[0].astype(BF16)], [w_conv[0], w_ffn_conv[0]])
    fb = _gate_fwd(f_log, b_pad)
    qx, kx, kxt, vx, vt, bounds, (c_down,) = _attn_prep(qkv, fb, [w_down[0].astype(BF16)])
    w_o_full = c_o.reshape(d, d)
    w_down_full = c_down.reshape(D_FF, d)
    w_conv_full = jnp.concatenate([c_conv[j] for j in range(N_CHIPS)], axis=1)
    w_ffn_full = jnp.concatenate([c_ffn[j] for j in range(N_CHIPS)], axis=1)
    n_up = c_up.shape[2]
    first_blk, last_blk = _key_block_ranges(bounds)
    o_attn, lse = _attn_fwd_t(qx, kx, vt, first_blk)
    mix = _mixer_fwd(z_a, o_attn, w_conv_full, g_conv_out, g_attn_out)
    x2, h2 = _mm("nn", [mix], [w_o_full], F32, 512, d, "out_proj", add=x0, norm_g=g_ffn)
    up = _mm("nn", [h2], [c_up], BF16, TM_MM, n_up, "up_proj", b_chips=True)
    act, u_conv = _ffn_act_fwd(up, w_ffn_full)
    dx3, dx3_b, loss_row, gg_final = _down_proj_loss(act, w_down_full, x2, target, g_final.reshape(1, d))

    dact = _mm("nt", [dx3_b], [w_down_full], F32, TM_MM, 1408, "d_act")
    gw_down = _mm_tn(act, dx3_b, 1408, 1024, "gw_down")
    dup, gwf_lin, gwf_gate = _ffn_act_bwd(up, u_conv, dact, w_ffn_full)
    dh2 = _mm("nt", [(dup, j // 2, j % 2, n_up) for j in range(N_CHIPS)], [(c_up, j) for j in range(N_CHIPS)],
              F32, TM_MM, 512, "d_h2")
    gw_up = _mm_tn(h2, dup, 1024, n_up, "gw_up", out_chips=True)
    dx2, dx2_b, gg_ffn = _rms_bwd(x2, dh2, g_ffn, dx3, "rms_ffn_bwd", True)
    dmix = _mm("nt", [dx2_b], [w_o_full], F32, TM_MM, 512, "d_mix")
    gw_o = _mm_tn(mix, dx2_b, 1024, 1024, "gw_o")
    early = _pair_sums([gw_o.reshape(N_CHIPS, d // N_CHIPS, d), gw_up, gw_down.reshape(N_CHIPS, D_FF // N_CHIPS, d)],
                       ["w_o", "w_up", "w_down"], "pair_exchange")
    dz_a, dox, gw_conv, gg_conv_out, gg_attn_out, (got_o, got_down) = _mixer_bwd(
        z_a, o_attn, dmix, w_conv_full, g_conv_out, g_attn_out, [early[0][1], early[2][1]])
    dk, dv, dfk, dqt, dfq = _attn_bwd_t(qx, dox, kx, kxt, vx, lse, last_blk)
    dq = _attn_dq_finish(dqt)
    d_f = (jnp.transpose(dfk[:, :, 0:2, :], (1, 3, 0, 2)).reshape(s, N_HEADS)
           + jnp.transpose(jnp.sum(dfq, axis=2), (1, 2, 0)).reshape(s, N_HEADS))
    df_b, gb_f = _gate_bwd(f_log, b_pad, jnp.pad(d_f, ((0, 0), (0, LANES - N_HEADS))))
    gw_a = _mm_tn(h1, dz_a, 1024, c3, "gw_in_conv")
    gw_q = _mm_tn(h1, dq, 1024, ATTN_W, "gw_in_q")
    gw_k = _mm_tn(h1, dk, 1024, ATTN_W, "gw_in_k")
    gw_v = _mm_tn(h1, dv, 1024, ATTN_W, "gw_in_v")
    gw_c = _mm_tn(h1, df_b, 1024, LANES, "gw_in_gate")
    gw_in = jnp.concatenate([gw_a, gw_q, gw_k, gw_v, gw_c[:, :N_HEADS]], axis=1)
    n_in = IN_COLS // N_CHIPS
    gw_in = jnp.stack([gw_in[:, j * n_in:(j + 1) * n_in] for j in range(N_CHIPS)])
    late = _pair_sums([gw_in], ["w_in"], "pair_exchange_w_in")
    dh1, (got_up, got_in) = _mm("nt", [dz_a, dq, dk, dv, df_b], [w_a, w_q, w_k, w_v, w_c], F32, TM_MM, 512, "d_h1",
                                scatter=[early[1][1], late[0][1]])
    grad_x, gg_mix = _rms_bwd(x0, dh1, g_mix, dx2, "rms_mix_bwd", False)
    g_w_in, g_w_o, g_w_up, g_w_down = _finish_grads(late + early, [got_in, got_o, got_up, got_down],
                                                    ["w_in", "w_o", "w_up", "w_down"])

    gw_ffn = jnp.concatenate([gwf_lin, gwf_gate], axis=1)
    small_parts = [gg_mix, gg_conv_out, gg_attn_out, gg_ffn, gg_final, gb_f[:, :N_HEADS], loss_row[:, 0:1], gw_conv,
                   gw_ffn]
    small_shapes = [a.shape for a in small_parts]
    tot = _unflatten(_all_reduce_small(_flat_rows(small_parts, d, 8)), small_shapes)
    g_g_mix, g_g_conv_out, g_g_attn_out, g_g_ffn, g_g_final, g_b_f, loss_sum, g_conv_full, g_ffn_full = tot
    loss = loss_sum[0, 0]
    g_g_final = g_g_final[0]
    g_w_conv = lax.dynamic_slice_in_dim(g_conv_full, my_chip * (CONV_CH // N_CHIPS), CONV_CH // N_CHIPS, axis=1)
    g_w_ffn = lax.dynamic_slice_in_dim(g_ffn_full, my_chip * n_up, n_up, axis=1)

    swap = lambda a: jnp.swapaxes(a, -1, -2)
    u_w_in = tuple(swap(a) for a in _adamw(swap(w_in), swap(g_w_in), swap(m_w_in), swap(v_w_in), "adam_w_in"))
    u_w_o = _adamw(w_o, g_w_o, m_w_o, v_w_o, "adam_w_o")
    u_w_up = _adamw(w_up, g_w_up, m_w_up, v_w_up, "adam_w_up")
    u_w_down = _adamw(w_down, g_w_down, m_w_down, v_w_down, "adam_w_down")

    small_w = [g_mix, b_f, g_conv_out, g_attn_out, g_ffn, g_final, w_conv, w_ffn_conv]
    small_g = [g_g_mix, g_b_f, g_g_conv_out, g_g_attn_out, g_g_ffn, g_g_final, g_w_conv, g_w_ffn]
    small_m = [m_g_mix, m_b_f, m_g_conv_out, m_g_attn_out, m_g_ffn, m_g_final, m_w_conv, m_w_ffn_conv]
    small_v = [v_g_mix, v_b_f, v_g_conv_out, v_g_attn_out, v_g_ffn, v_g_final, v_w_conv, v_w_ffn_conv]
    shapes = [a.shape for a in small_w]
    pack = lambda arrs: _flat_rows(arrs, LANES, 8)
    sd, sm, sv = _adamw(pack(small_w), pack(small_g), pack(small_m), pack(small_v), "adam_small")
    sd, sm, sv = _unflatten(sd, shapes), _unflatten(sm, shapes), _unflatten(sv, shapes)
    (d_g_mix, d_b_f, d_g_conv_out, d_g_attn_out, d_g_ffn, d_g_final, d_w_conv, d_w_ffn) = sd
    (nm_g_mix, nm_b_f, nm_g_conv_out, nm_g_attn_out, nm_g_ffn, nm_g_final, nm_w_conv, nm_w_ffn) = sm
    (nv_g_mix, nv_b_f, nv_g_conv_out, nv_g_attn_out, nv_g_ffn, nv_g_final, nv_w_conv, nv_w_ffn) = sv

    grads = (g_g_mix, g_w_in[None], g_b_f, g_w_conv[None], g_g_conv_out, g_g_attn_out, g_w_o[None], g_g_ffn,
             g_w_up[None], g_w_ffn[None], g_w_down[None], g_g_final)
    deltas = (d_g_mix, u_w_in[0], d_b_f, d_w_conv, d_g_conv_out, d_g_attn_out, u_w_o[0], d_g_ffn, u_w_up[0],
              d_w_ffn, u_w_down[0], d_g_final)
    new_m = (nm_g_mix, u_w_in[1], nm_b_f, nm_w_conv, nm_g_conv_out, nm_g_attn_out, u_w_o[1], nm_g_ffn, u_w_up[1],
             nm_w_ffn, u_w_down[1], nm_g_final)
    new_v = (nv_g_mix, u_w_in[2], nv_b_f, nv_w_conv, nv_g_conv_out, nv_g_attn_out, u_w_o[2], nv_g_ffn, u_w_up[2],
             nv_w_ffn, u_w_down[2], nv_g_final)
    return (loss, grad_x[None], *grads, *deltas, *new_m, *new_v)
```

```python
import jax
import jax.numpy as jnp
from jax import lax
from jax.experimental import pallas as pl
from jax.experimental.pallas import tpu as pltpu

F32, BF16 = jnp.float32, jnp.bfloat16
MESH = pl.DeviceIdType.MESH

D_MODEL = 1024
CONV_CH = 512
ATTN_W = 512
N_HEADS = 8
HEAD_DIM = 64
N_PAIRS = N_HEADS // 2
D_FF = 2816
IN_COLS = 3 * CONV_CH + 3 * ATTN_W + N_HEADS
EPS = 1e-6
Q_SCALE = 0.125
EXP_ZERO = 88.0
N_CHIPS = 4
LANES = 128
HALO = 8
HALO_BF16 = 2 * HALO

ADAM_LR, ADAM_B1, ADAM_B2, ADAM_EPS, ADAM_WD, ADAM_STEP = 0.001, 0.9, 0.999, 1e-08, 0.01, 10

TM_ROWS = 512
TM_MM = 1024
TK_TN = 1024
TQ = 512
TM_FFN = 256
TN_FFN = 1408
VMEM_LIMIT = 52 * 2**20


def _cp(sem, vmem=VMEM_LIMIT):
    return pltpu.CompilerParams(dimension_semantics=sem, vmem_limit_bytes=vmem)


def _bf(a):
    return a if a.dtype == BF16 else a.astype(BF16)


def _mm(mode, a_list, b_list, out_dtype, tm, tn, name, add=None, b_chips=False, scatter=(), norm_g=None):
    n_p = len(a_list)
    a0 = a_list[0]
    m_dim = a0[0].shape[1] if isinstance(a0, tuple) else a0.shape[0]
    b0 = b_list[0]
    if b_chips:
        n_dim = b0.shape[0] * b0.shape[2]
        assert tn == b0.shape[2] and mode == "nn"
    else:
        b0 = b0[0][b0[1]] if isinstance(b0, tuple) else b0
        n_dim = b0.shape[1 if mode == "nn" else 0]
    tm, tn = min(tm, m_dim), min(tn, n_dim)
    assert m_dim % tm == 0 and n_dim % tn == 0
    dims = (((1,), (0,)), ((), ())) if mode == "nn" else (((1,), (1,)), ((), ()))
    in_specs, args = [], []
    for a in a_list:
        if isinstance(a, tuple):
            arr, lead, col, width = a
            in_specs.append(pl.BlockSpec((None, tm, width), lambda m, n, lead=lead, col=col: (lead, m, col)))
        else:
            arr = a
            in_specs.append(pl.BlockSpec((tm, a.shape[1]), lambda m, n: (m, 0)))
        args.append(arr)
    for b in b_list:
        if b_chips:
            arr = b
            in_specs.append(pl.BlockSpec((None, b.shape[1], tn), lambda m, n: (n, 0, 0)))
        elif isinstance(b, tuple):
            arr, lead = b
            if mode == "nn":
                in_specs.append(pl.BlockSpec((None, arr.shape[1], tn), lambda m, n, lead=lead: (lead, 0, n)))
            else:
                in_specs.append(pl.BlockSpec((None, tn, arr.shape[2]), lambda m, n, lead=lead: (lead, n, 0)))
        elif mode == "nn":
            arr = b
            in_specs.append(pl.BlockSpec((b.shape[0], tn), lambda m, n: (0, n)))
        else:
            arr = b
            in_specs.append(pl.BlockSpec((tn, b.shape[1]), lambda m, n: (n, 0)))
        args.append(arr)
    if add is not None:
        in_specs.append(pl.BlockSpec((tm, tn), lambda m, n: (m, n)))
        args.append(add)
    if norm_g is not None:
        assert tn == n_dim and not scatter
        in_specs.append(pl.BlockSpec((1, tn), lambda m, n: (0, 0)))
        args.append(norm_g)

    n_in = len(args)
    n_sc = len(scatter)
    grid = (m_dim // tm, n_dim // tn)

    def body(*refs):
        o_ref = refs[n_in + n_sc]
        if n_sc:
            finish = _hosted_scatter(refs[n_in:n_in + n_sc], refs[n_in + n_sc + 1:n_in + 2 * n_sc + 1],
                                     refs[n_in + 2 * n_sc + 1:], pl.program_id(0) * grid[1] + pl.program_id(1),
                                     grid[0] * grid[1])
        acc = None
        for i in range(n_p):
            d = lax.dot_general(_bf(refs[i][...]), _bf(refs[n_p + i][...]), dims,
                                preferred_element_type=F32)
            acc = d if acc is None else acc + d
        if add is not None:
            acc = refs[2 * n_p][...] + acc
        o_ref[...] = acc.astype(out_dtype)
        if norm_g is not None:
            refs[n_in + 1][...] = (acc * _rstd(acc) * refs[n_in - 1][...]).astype(BF16)
        if n_sc:
            finish()

    main_spec = pl.BlockSpec((tm, tn), lambda m, n: (m, n))
    main_shape = jax.ShapeDtypeStruct((m_dim, n_dim), out_dtype)
    if norm_g is not None:
        return pl.pallas_call(body, name=name, grid=grid, in_specs=in_specs, out_specs=[main_spec, main_spec],
                              out_shape=[main_shape, jax.ShapeDtypeStruct((m_dim, n_dim), BF16)],
                              compiler_params=_cp(("parallel", "parallel")))(*args)
    if not n_sc:
        return pl.pallas_call(body, name=name, grid=grid, in_specs=in_specs, out_specs=main_spec,
                              out_shape=main_shape, compiler_params=_cp(("parallel", "parallel")))(*args)
    got_shapes, sems = _scatter_operands(scatter)
    outs = pl.pallas_call(
        body, name=name, grid=grid, in_specs=in_specs + _hbm_specs(n_sc), out_specs=[main_spec] + _hbm_specs(n_sc),
        out_shape=[main_shape] + got_shapes, scratch_shapes=sems,
        compiler_params=_cp(("arbitrary", "arbitrary")))(*args, *scatter)
    return outs[0], outs[1:]


def _mm_tn(a, b, tm, tn, name, out_chips=False):
    k_dim, m_dim = a.shape
    n_dim = b.shape[-1] * (b.shape[0] if b.ndim == 3 else 1)
    tm, tn, tk = min(tm, m_dim), min(tn, b.shape[-1]), min(TK_TN, k_dim)
    assert m_dim % tm == 0 and b.shape[-1] % tn == 0 and k_dim % tk == 0
    per = b.shape[-1] // tn
    if b.ndim == 3:
        b_spec = pl.BlockSpec((None, tk, tn), lambda m, n, k: (n // per, k, n % per))
    else:
        b_spec = pl.BlockSpec((tk, tn), lambda m, n, k: (k, n))

    def body(a_ref, b_ref, o_ref):
        @pl.when(pl.program_id(2) == 0)
        def _():
            o_ref[...] = jnp.zeros_like(o_ref)
        o_ref[...] += lax.dot_general(_bf(a_ref[...]), _bf(b_ref[...]), (((0,), (0,)), ((), ())),
                                      preferred_element_type=F32)

    return pl.pallas_call(
        body, name=name, grid=(m_dim // tm, n_dim // tn, k_dim // tk),
        in_specs=[pl.BlockSpec((tk, tm), lambda m, n, k: (k, m)), b_spec],
        out_specs=(pl.BlockSpec((None, tm, tn), lambda m, n, k: (n, m, 0)) if out_chips
                   else pl.BlockSpec((tm, tn), lambda m, n, k: (m, n))),
        out_shape=jax.ShapeDtypeStruct((n_dim // tn, m_dim, tn) if out_chips else (m_dim, n_dim), F32),
        compiler_params=_cp(("parallel", "parallel", "arbitrary")))(a, b)


def _rstd(x):
    return lax.rsqrt(jnp.mean(x * x, axis=-1, keepdims=True) + EPS)


def _rms_bwd(x, dh, g, dres, name, with_bf16, scatter=()):
    s, d = x.shape
    tm = min(TM_ROWS, s)
    n_sc = len(scatter)
    n_out = 3 if with_bf16 else 2
    got_shapes, sems = _scatter_operands(scatter) if n_sc else ([], [])

    def body(x_ref, dh_ref, g_ref, dres_ref, *rest):
        dx_ref, gg_ref = rest[n_sc], rest[n_sc + n_out - 1]
        i = pl.program_id(0)
        if n_sc:
            finish = _hosted_scatter(rest[:n_sc], rest[n_sc + n_out:2 * n_sc + n_out], rest[2 * n_sc + n_out:], i,
                                     s // tm)

        @pl.when(i == 0)
        def _():
            gg_ref[...] = jnp.zeros_like(gg_ref)

        xv = x_ref[...]
        xn = xv * _rstd(xv)
        dhv = dh_ref[...]
        gg_ref[...] += jnp.sum(dhv * xn, axis=0, keepdims=True)
        t = dhv * g_ref[...]
        dx = dres_ref[...] + _rstd(xv) * (t - xn * jnp.mean(t * xn, axis=-1, keepdims=True))
        dx_ref[...] = dx
        if with_bf16:
            rest[n_sc + 1][...] = dx.astype(BF16)
        if n_sc:
            finish()

    row = pl.BlockSpec((tm, d), lambda i: (i, 0))
    vec = pl.BlockSpec((1, d), lambda i: (0, 0))
    out_specs = [row] + ([row] if with_bf16 else []) + [vec] + _hbm_specs(n_sc)
    out_shape = ([jax.ShapeDtypeStruct((s, d), F32)] + ([jax.ShapeDtypeStruct((s, d), BF16)] if with_bf16 else [])
                 + [jax.ShapeDtypeStruct((1, d), F32)] + got_shapes)
    outs = pl.pallas_call(
        body, name=name, grid=(s // tm,), in_specs=[row, row, vec, row] + _hbm_specs(n_sc), out_specs=out_specs,
        out_shape=out_shape, scratch_shapes=sems, compiler_params=_cp(("arbitrary",)))(x, dh, g, dres, *scatter)
    return tuple(outs[:n_out]) + ((outs[n_out:],) if n_sc else ())


def _down_proj_loss(act, w_down, x2, target, g):
    s, d = x2.shape
    tm = min(TM_ROWS, s)
    k_dim = act.shape[1]

    def body(a_ref, w_ref, x_ref, t_ref, g_ref, dx_ref, dxb_ref, loss_ref, gg_ref):
        @pl.when(pl.program_id(0) == 0)
        def _():
            gg_ref[...] = jnp.zeros_like(gg_ref)
            loss_ref[...] = jnp.zeros_like(loss_ref)

        xv = x_ref[...] + lax.dot_general(a_ref[...], w_ref[...], (((1,), (0,)), ((), ())),
                                          preferred_element_type=F32)
        r = _rstd(xv)
        xn = xv * r
        gv = g_ref[...]
        err = xn * gv - t_ref[...]
        loss_ref[...] += 0.5 * jnp.sum(jnp.mean(err * err, axis=-1, keepdims=True), axis=0, keepdims=True)
        dy = err * (1.0 / d)
        gg_ref[...] += jnp.sum(dy * xn, axis=0, keepdims=True)
        t = dy * gv
        dx = r * (t - xn * jnp.mean(t * xn, axis=-1, keepdims=True))
        dx_ref[...] = dx
        dxb_ref[...] = dx.astype(BF16)

    row = pl.BlockSpec((tm, d), lambda i: (i, 0))
    vec = pl.BlockSpec((1, d), lambda i: (0, 0))
    return pl.pallas_call(
        body, name="down_proj_loss", grid=(s // tm,),
        in_specs=[pl.BlockSpec((tm, k_dim), lambda i: (i, 0)), pl.BlockSpec((k_dim, d), lambda i: (0, 0)), row, row,
                  vec],
        out_specs=[row, row, pl.BlockSpec((1, LANES), lambda i: (0, 0)), vec],
        out_shape=[jax.ShapeDtypeStruct((s, d), F32), jax.ShapeDtypeStruct((s, d), BF16),
                   jax.ShapeDtypeStruct((1, LANES), F32), jax.ShapeDtypeStruct((1, d), F32)],
        compiler_params=_cp(("arbitrary",)))(act, w_down, x2, target, g)


def _prev_halo_spec(tm, width, col):
    return pl.BlockSpec((HALO, width), lambda i, *_: (jnp.maximum(i * (tm // HALO) - 1, 0), col))


def _next_halo_spec(tm, width, col, s):
    return pl.BlockSpec((HALO, width), lambda i, *_: (jnp.minimum((i + 1) * (tm // HALO), s // HALO - 1), col))


def _shift_down(x, k):
    return pltpu.roll(x, k, 0)


def _shift_up(x, k):
    return pltpu.roll(x, x.shape[0] - k, 0)


def _conv_taps(x_ext, w):
    return w[0:1, :] * _shift_down(x_ext, 2) + w[1:2, :] * _shift_down(x_ext, 1) + w[2:3, :] * x_ext


def _conv_taps_t(d_ext, w):
    return w[2:3, :] * d_ext + w[1:2, :] * _shift_up(d_ext, 1) + w[0:1, :] * _shift_up(d_ext, 2)


def _mixer_fwd(z_a, o_attn, w_conv, g_conv_out, g_attn_out):
    s = z_a.shape[0]
    c = CONV_CH
    tm = min(TM_ROWS, s)

    def body(gb_ref, gc_ref, xc_ref, gcp_ref, xcp_ref, o_ref, w_ref, gco_ref, gao_ref, mix_ref):
        i = pl.program_id(0)
        cx = gc_ref[...] * xc_ref[...]
        cx_prev = jnp.where(i > 0, gcp_ref[...] * xcp_ref[...], 0.0)
        conv = _conv_taps(jnp.concatenate([cx_prev, cx], axis=0), w_ref[...])[HALO:]
        y = gb_ref[...] * conv
        mix_ref[:, 0:c] = (y * _rstd(y) * gco_ref[...]).astype(BF16)
        o = o_ref[...]
        mix_ref[:, c:2 * c] = (o * _rstd(o) * gao_ref[...]).astype(BF16)

    col = lambda j: pl.BlockSpec((tm, c), lambda i: (i, j))
    vec = pl.BlockSpec((1, c), lambda i: (0, 0))
    return pl.pallas_call(
        body, name="mixer_fwd", grid=(s // tm,),
        in_specs=[col(0), col(1), col(2), _prev_halo_spec(tm, c, 1), _prev_halo_spec(tm, c, 2), col(0),
                  pl.BlockSpec((3, c), lambda i: (0, 0)), vec, vec],
        out_specs=pl.BlockSpec((tm, 2 * c), lambda i: (i, 0)),
        out_shape=jax.ShapeDtypeStruct((s, 2 * c), BF16),
        compiler_params=_cp(("parallel",)))(z_a, z_a, z_a, z_a, z_a, o_attn, w_conv, g_conv_out, g_attn_out)


def _mixer_bwd(z_a, o_attn, dmix, w_conv, g_conv_out, g_attn_out, scatter):
    s = z_a.shape[0]
    c = CONV_CH
    tm = min(TM_ROWS, s)
    n_blk = s // tm
    n_sc = len(scatter)
    got_shapes, sems = _scatter_operands(scatter)

    def body(gb_ref, gc_ref, xc_ref, gcp_ref, xcp_ref, gbn_ref, gcn_ref, xcn_ref, o_ref, dnc_ref, dncn_ref, dna_ref,
             w_ref, gco_ref, gao_ref, *rest):
        dz_ref, dox_ref, gw_ref, ggco_ref, ggao_ref = rest[n_sc:n_sc + 5]
        i = pl.program_id(0)
        finish = _hosted_scatter(rest[:n_sc], rest[n_sc + 5:2 * n_sc + 5], rest[2 * n_sc + 5:], i, n_blk)

        @pl.when(i == 0)
        def _():
            gw_ref[...] = jnp.zeros_like(gw_ref)
            ggco_ref[...] = jnp.zeros_like(ggco_ref)
            ggao_ref[...] = jnp.zeros_like(ggao_ref)

        w = w_ref[...]
        zeros = jnp.zeros((HALO, c), F32)
        gb_e = jnp.concatenate([zeros, gb_ref[...], gbn_ref[...]], axis=0)
        cx_prev = jnp.where(i > 0, gcp_ref[...] * xcp_ref[...], 0.0)
        gc_e = jnp.concatenate([zeros, gc_ref[...], gcn_ref[...]], axis=0)
        xc_e = jnp.concatenate([zeros, xc_ref[...], xcn_ref[...]], axis=0)
        cx_e = jnp.concatenate([cx_prev, gc_ref[...] * xc_ref[...], gcn_ref[...] * xcn_ref[...]], axis=0)
        dn_next = jnp.where(i < n_blk - 1, dncn_ref[...], 0.0)
        dn_e = jnp.concatenate([zeros, dnc_ref[...], dn_next], axis=0)

        cx_1, cx_2 = _shift_down(cx_e, 1), _shift_down(cx_e, 2)
        conv_e = w[0:1, :] * cx_2 + w[1:2, :] * cx_1 + w[2:3, :] * cx_e
        y_e = gb_e * conv_e
        r_e = _rstd(y_e)
        yn_e = y_e * r_e
        t_e = dn_e * gco_ref[...]
        dy_e = r_e * (t_e - yn_e * jnp.mean(t_e * yn_e, axis=-1, keepdims=True))
        dconv_e = dy_e * gb_e
        dcx_e = _conv_taps_t(dconv_e, w)
        blk = slice(HALO, HALO + tm)
        dz_ref[:, 0:c] = (dy_e * conv_e)[blk].astype(BF16)
        dz_ref[:, c:2 * c] = (dcx_e * xc_e)[blk].astype(BF16)
        dz_ref[:, 2 * c:3 * c] = (dcx_e * gc_e)[blk].astype(BF16)
        ggco_ref[...] += jnp.sum((dn_e * yn_e)[blk], axis=0, keepdims=True)
        dconv = dconv_e[blk]
        gw_ref[0:1, :] += jnp.sum(dconv * cx_2[blk], axis=0, keepdims=True)
        gw_ref[1:2, :] += jnp.sum(dconv * cx_1[blk], axis=0, keepdims=True)
        gw_ref[2:3, :] += jnp.sum(dconv * cx_e[blk], axis=0, keepdims=True)

        o = o_ref[...]
        ra = _rstd(o)
        on = o * ra
        dna = dna_ref[...]
        ggao_ref[...] += jnp.sum(dna * on, axis=0, keepdims=True)
        ta = dna * gao_ref[...]
        do = ra * (ta - on * jnp.mean(ta * on, axis=-1, keepdims=True))
        prod = do * o
        lane = lax.broadcasted_iota(jnp.int32, (tm, LANES), 1)
        head_a = lane < HEAD_DIM
        for p in range(N_PAIRS):
            cols = slice(p * LANES, (p + 1) * LANES)
            pb, dob = prod[:, cols], do[:, cols]
            for hh in range(2):
                sel = head_a if hh == 0 else jnp.logical_not(head_a)
                delta = jnp.sum(jnp.where(sel, pb, 0.0), axis=-1, keepdims=True)
                neg3 = _split3(-delta)
                do_h = pltpu.roll(dob, HEAD_DIM, 1) if hh else dob
                dox_ref[2 * p + hh] = _aug(do_h, lane, neg3).astype(BF16)
        finish()

    col = lambda j: pl.BlockSpec((tm, c), lambda i: (i, j))
    vec = pl.BlockSpec((1, c), lambda i: (0, 0))
    w3 = pl.BlockSpec((3, c), lambda i: (0, 0))
    outs = pl.pallas_call(
        body, name="mixer_bwd", grid=(n_blk,),
        in_specs=[col(0), col(1), col(2), _prev_halo_spec(tm, c, 1), _prev_halo_spec(tm, c, 2),
                  _next_halo_spec(tm, c, 0, s), _next_halo_spec(tm, c, 1, s), _next_halo_spec(tm, c, 2, s),
                  col(0), col(0), _next_halo_spec(tm, c, 0, s), col(1), w3, vec, vec] + _hbm_specs(n_sc),
        out_specs=[pl.BlockSpec((tm, 3 * c), lambda i: (i, 0)),
                   pl.BlockSpec((N_HEADS, tm, LANES), lambda i: (0, i, 0)), w3, vec, vec] + _hbm_specs(n_sc),
        out_shape=[jax.ShapeDtypeStruct((s, 3 * c), BF16), jax.ShapeDtypeStruct((N_HEADS, s, LANES), BF16),
                   jax.ShapeDtypeStruct((3, c), F32), jax.ShapeDtypeStruct((1, c), F32),
                   jax.ShapeDtypeStruct((1, c), F32)] + got_shapes,
        scratch_shapes=sems, compiler_params=_cp(("arbitrary",)))(
            z_a, z_a, z_a, z_a, z_a, z_a, z_a, z_a, o_attn, dmix, dmix, dmix, w_conv, g_conv_out, g_attn_out,
            *scatter)
    return tuple(outs[:5]) + (outs[5:],)


def _gate_fwd(f, b_pad):
    s = f.shape[0]
    tm = min(TQ, s)

    def body(f_ref, b_ref, fb_ref, carry):
        @pl.when(pl.program_id(0) == 0)
        def _():
            carry[...] = jnp.zeros_like(carry)

        z = f_ref[...] + b_ref[...]
        x = jnp.minimum(z, 0.0) - jnp.log1p(jnp.exp(-jnp.abs(z)))
        row = lax.broadcasted_iota(jnp.int32, (tm, LANES), 0)
        sh = 1
        while sh < tm:
            x = x + jnp.where(row >= sh, _shift_down(x, sh), 0.0)
            sh *= 2
        x = x + carry[0:1, :]
        carry[...] = jnp.broadcast_to(x[tm - 1:tm, :], carry.shape)
        head_a = lax.broadcasted_iota(jnp.int32, (tm, LANES), 1) < HEAD_DIM
        for p in range(N_PAIRS):
            fa = jnp.broadcast_to(x[:, 2 * p:2 * p + 1], (tm, LANES))
            fbv = jnp.broadcast_to(x[:, 2 * p + 1:2 * p + 2], (tm, LANES))
            fb_ref[:, p * LANES:(p + 1) * LANES] = jnp.where(head_a, fa, fbv)

    return pl.pallas_call(
        body, name="gate_fwd", grid=(s // tm,),
        in_specs=[pl.BlockSpec((tm, LANES), lambda i: (i, 0)), pl.BlockSpec((1, LANES), lambda i: (0, 0))],
        out_specs=pl.BlockSpec((tm, N_PAIRS * LANES), lambda i: (i, 0)),
        out_shape=jax.ShapeDtypeStruct((s, N_PAIRS * LANES), F32),
        scratch_shapes=[pltpu.VMEM((HALO, LANES), F32)],
        compiler_params=_cp(("arbitrary",)))(f, b_pad)


def _gate_bwd(f, b_pad, d_f):
    s = f.shape[0]
    tm = min(TQ, s)
    n_blk = s // tm

    def body(f_ref, b_ref, d_ref, df_ref, gb_ref, carry):
        @pl.when(pl.program_id(0) == 0)
        def _():
            carry[...] = jnp.zeros_like(carry)
            gb_ref[...] = jnp.zeros_like(gb_ref)

        x = d_ref[...]
        row = lax.broadcasted_iota(jnp.int32, (tm, LANES), 0)
        sh = 1
        while sh < tm:
            x = x + jnp.where(row < tm - sh, _shift_up(x, sh), 0.0)
            sh *= 2
        x = x + carry[0:1, :]
        carry[...] = jnp.broadcast_to(x[0:1, :], carry.shape)
        z = f_ref[...] + b_ref[...]
        d = x * (1.0 / (1.0 + jnp.exp(z)))
        df_ref[...] = d.astype(BF16)
        gb_ref[...] += jnp.sum(d, axis=0, keepdims=True)

    rev = pl.BlockSpec((tm, LANES), lambda i: (n_blk - 1 - i, 0))
    vec = pl.BlockSpec((1, LANES), lambda i: (0, 0))
    return pl.pallas_call(
        body, name="gate_bwd", grid=(n_blk,), in_specs=[rev, vec, rev], out_specs=[rev, vec],
        out_shape=[jax.ShapeDtypeStruct((s, LANES), BF16), jax.ShapeDtypeStruct((1, LANES), F32)],
        scratch_shapes=[pltpu.VMEM((HALO, LANES), F32)],
        compiler_params=_cp(("arbitrary",)))(f, b_pad, d_f)


_NT = (((1,), (1,)), ((), ()))
_NN = (((1,), (0,)), ((), ()))


AUG = HEAD_DIM
NORM_MARGIN = 1.01


def _split3(x):
    hi = x.astype(BF16).astype(F32)
    r = x - hi
    mid = r.astype(BF16).astype(F32)
    lo = (r - mid).astype(BF16).astype(F32)
    return hi, mid, lo


def _aug(base, lane, vals):
    out = jnp.where(lane < AUG, base, 0.0)
    for k, v in enumerate(vals):
        out = jnp.where(lane == AUG + k, v, out)
    return out


def _attn_prep(qkv, fb, bigs):
    s = qkv.shape[0]
    tq = min(TQ, s)
    n_q = s // tq

    n = len(bigs)
    arrays, landing, sems = _gather_operands(bigs, [])

    def body(q_ref, k_ref, v_ref, fb_ref, *rest):
        qx_ref, kx_ref, kxt_ref, vx_ref, vt_ref, b_ref = rest[2 * n:2 * n + 6]
        finish = _hosted_gather((rest[:n], rest[2 * n + 6:3 * n + 6]) + tuple(rest[3 * n + 6:]), n, n,
                                pl.program_id(0), n_q)
        lane = lax.broadcasted_iota(jnp.int32, (tq, LANES), 1)
        lane8 = lax.broadcasted_iota(jnp.int32, (HALO, LANES), 1)
        head_lanes = lane < AUG
        is_lane = [lane == AUG + k for k in range(6)]
        first3 = (lane >= AUG) & (lane < AUG + 3)
        next3 = (lane >= AUG + 3) & (lane < AUG + 6)
        q_const = jnp.where(first3, -1.0, 0.0)
        k_const = jnp.where(next3, 1.0, 0.0)
        v_const = jnp.where(first3, 1.0, 0.0)
        ones_head = (lax.broadcasted_iota(jnp.int32, (LANES, LANES), 0) < HEAD_DIM).astype(BF16)
        acc = jnp.zeros((HALO, LANES), F32)
        for p in range(N_PAIRS):
            cols = slice(p * LANES, (p + 1) * LANES)
            q2, k2, v2 = (ref[:, cols].astype(F32) for ref in (q_ref, k_ref, v_ref))
            q2 = q2 * Q_SCALE
            f2 = fb_ref[:, cols]
            for hh in range(2):
                h = 2 * p + hh
                q, k, v = ((pltpu.roll(x, HEAD_DIM, 1) if hh else x) for x in (q2, k2, v2))
                f = f2 if hh else pltpu.roll(f2, HEAD_DIM, 1)
                hi, mid, lo = _split3(f)
                q_aug = jnp.where(is_lane[3], hi, jnp.where(is_lane[4], mid, jnp.where(is_lane[5], lo, q_const)))
                k_aug = jnp.where(is_lane[0], hi, jnp.where(is_lane[1], mid, jnp.where(is_lane[2], lo, k_const)))
                kx = jnp.where(head_lanes, k, k_aug)
                vx = jnp.where(head_lanes, v, v_const)
                qx_ref[h] = jnp.where(head_lanes, q, q_aug).astype(BF16)
                kx_ref[h] = kx.astype(BF16)
                vx_ref[h] = vx.astype(BF16)
                kxt_ref[h, 0] = kx.T.astype(BF16)
                vt_ref[h, 0] = vx.T.astype(BF16)
                q_sq = lax.dot_general((q * q).astype(BF16), ones_head, _NN, preferred_element_type=F32)
                k_sq = lax.dot_general((k * k).astype(BF16), ones_head, _NN, preferred_element_type=F32)
                diag = lax.dot_general((q * k).astype(BF16), ones_head, _NN, preferred_element_type=F32)
                diag = diag - jnp.sqrt(q_sq * k_sq) * (NORM_MARGIN - 1.0)
                vals = (jnp.sqrt(jnp.max(q_sq, axis=0, keepdims=True)), jnp.sqrt(jnp.max(k_sq, axis=0, keepdims=True)),
                        jnp.max(f - diag, axis=0, keepdims=True), f[tq - 1:tq, :])
                for slot, val in enumerate(vals):
                    acc = jnp.where(lane8 == slot * N_HEADS + h, val[:, AUG:AUG + 1], acc)
        b_ref[0] = acc
        finish()

    blk = lambda j: pl.BlockSpec((tq, ATTN_W), lambda i: (i, j))
    rows = pl.BlockSpec((N_HEADS, tq, LANES), lambda i: (0, i, 0))
    cols_t = pl.BlockSpec((N_HEADS, 1, LANES, tq), lambda i: (0, i, 0, 0))
    shp = jax.ShapeDtypeStruct((N_HEADS, s, LANES), BF16)
    shp_t = jax.ShapeDtypeStruct((N_HEADS, n_q, LANES, tq), BF16)
    outs = pl.pallas_call(
        body, name="attn_prep", grid=(n_q,), in_specs=[blk(0), blk(1), blk(2), blk(0)] + _hbm_specs(2 * n),
        out_specs=[rows, rows, cols_t, rows, cols_t,
                   pl.BlockSpec((1, HALO, LANES), lambda i: (i, 0, 0))] + _hbm_specs(n),
        out_shape=[shp, shp, shp_t, shp, shp_t, jax.ShapeDtypeStruct((n_q, HALO, LANES), F32)]
        + [jax.ShapeDtypeStruct(b.shape, b.dtype) for b in landing],
        input_output_aliases={4 + n + k: 6 + k for k in range(n)}, scratch_shapes=sems,
        compiler_params=_cp(("arbitrary",)))(qkv, qkv, qkv, fb, *arrays, *landing)
    return tuple(outs[:6]) + (outs[6:],)


def _key_block_ranges(bounds):
    t = bounds[:, 0, :]
    nh = N_HEADS
    a, b, c, e = t[:, 0:nh], t[:, nh:2 * nh], t[:, 2 * nh:3 * nh], t[:, 3 * nh:4 * nh]
    bound = a[:, None, :] * b[None, :, :] * NORM_MARGIN + c[:, None, :] - e[None, :, :]
    n_q = t.shape[0]
    idx = jnp.arange(n_q)
    need = jnp.logical_not(bound < -(EXP_ZERO + 2.0)) | (idx[None, :, None] >= idx[:, None, None])
    first = jnp.argmax(need, axis=1).astype(jnp.int32)
    first = jnp.min(first.reshape(n_q, N_PAIRS, 2), axis=-1)
    visits = (first[:, None, :] <= idx[None, :, None]) & (idx[None, :, None] <= idx[:, None, None])
    last = jnp.max(jnp.where(visits, idx[:, None, None], 0), axis=0).astype(jnp.int32)
    return first.T.reshape(-1), last.T.reshape(-1)


def _attn_fwd_t(qx, kx, vt, first_blk):
    _, s, _ = qx.shape
    tq = min(TQ, s)
    n_q = s // tq
    neg = -1e30

    def body(first_ref, qx_ref, kx_ref, vt_ref, o_ref, lse_ref, acc_ref, m_ref, s_even, s_odd):
        p = pl.program_id(0)
        i = pl.program_id(1)
        acc_ref[...] = jnp.zeros(acc_ref.shape, F32)
        m_ref[...] = jnp.full(m_ref.shape, neg, F32)
        key_le_query = (lax.broadcasted_iota(jnp.int32, (tq, tq), 0) <= lax.broadcasted_iota(jnp.int32, (tq, tq), 1))
        first = first_ref[p * n_q + i]

        def scores(kb, hh, dst):
            rows_k = pl.ds(pl.multiple_of(kb * tq, tq), tq)
            dst[hh] = lax.dot_general(kx_ref[hh, rows_k, :], qx_ref[hh], _NT, preferred_element_type=F32)

        def step(kb, src, nxt):
            for hh in range(2):
                st = src[hh]
                if nxt is None:
                    st = jnp.where(key_le_query, st, -jnp.inf)
                else:
                    scores(kb + 1, hh, nxt)
                m_old = m_ref[hh]
                m_new = jnp.maximum(m_old, jnp.max(st, axis=0, keepdims=True))
                m_ref[hh] = m_new
                pt = jnp.exp(st - m_new).astype(BF16)
                acc_ref[hh] = acc_ref[hh] * jnp.exp(m_old - m_new) + lax.dot_general(
                    vt_ref[hh, kb], pt, _NN, preferred_element_type=F32)

        def by_parity(kb, fn):
            @pl.when(kb % 2 == 0)
            def _():
                fn(s_even, s_odd)

            @pl.when(kb % 2 == 1)
            def _():
                fn(s_odd, s_even)

        def first_scores(src, nxt):
            scores(first, 0, src)
            scores(first, 1, src)

        def unmasked(kb, carry):
            by_parity(kb, lambda src, nxt: step(kb, src, nxt))
            return carry

        by_parity(first, first_scores)
        lax.fori_loop(first, i, unmasked, 0)
        by_parity(i, lambda src, nxt: step(i, src, None))
        outs, lses = [], []
        for hh in range(2):
            acc = acc_ref[hh]
            l = acc[AUG:AUG + 1, :]
            outs.append(acc[0:HEAD_DIM, :] / l)
            lses.append(m_ref[hh] + jnp.log(l))
        o_ref[...] = jnp.concatenate(outs, axis=0).T
        rows8 = lax.broadcasted_iota(jnp.int32, (N_HEADS, tq), 0)
        lse_ref[0, 0] = jnp.where(rows8 == 0, lses[0], jnp.where(rows8 == 1, lses[1], 0.0))

    grid_spec = pltpu.PrefetchScalarGridSpec(
        num_scalar_prefetch=1, grid=(N_PAIRS, n_q),
        in_specs=[pl.BlockSpec((2, tq, LANES), lambda p, i, first: (p, i, 0)),
                  pl.BlockSpec((2, s, LANES), lambda p, i, first: (p, 0, 0)),
                  pl.BlockSpec((2, n_q, LANES, tq), lambda p, i, first: (p, 0, 0, 0))],
        out_specs=[pl.BlockSpec((tq, LANES), lambda p, i, first: (i, p)),
                   pl.BlockSpec((1, 1, N_HEADS, tq), lambda p, i, first: (p, i, 0, 0))],
        scratch_shapes=[pltpu.VMEM((2, LANES, tq), F32), pltpu.VMEM((2, 1, tq), F32),
                        pltpu.VMEM((2, tq, tq), F32), pltpu.VMEM((2, tq, tq), F32)])
    return pl.pallas_call(
        body, name="attn_fwd", grid_spec=grid_spec,
        out_shape=[jax.ShapeDtypeStruct((s, ATTN_W), F32), jax.ShapeDtypeStruct((N_PAIRS, n_q, N_HEADS, tq), F32)],
        compiler_params=_cp(("parallel", "arbitrary")))(first_blk, qx, kx, vt)


def _attn_bwd_t(qx, dox, kx, kxt, vx, lse, last_blk):
    _, s, _ = qx.shape
    tq = min(TQ, s)
    n_q = s // tq

    def body(last_ref, qx_ref, dox_ref, lse_ref, kx_ref, kxt_ref, vx_ref, dk_ref, dv_ref, dfk_ref, dqt_ref, dfq_ref):
        p = pl.program_id(0)
        j = pl.program_id(1)

        @pl.when(j == 0)
        def _():
            dqt_ref[...] = jnp.zeros(dqt_ref.shape, F32)
            dfq_ref[...] = jnp.zeros(dfq_ref.shape, F32)

        key_le_query = (lax.broadcasted_iota(jnp.int32, (tq, tq), 0) <= lax.broadcasted_iota(jnp.int32, (tq, tq), 1))

        def step(i, carry, masked):
            rows_q = pl.ds(pl.multiple_of(i * tq, tq), tq)
            out = []
            for hh in range(2):
                dk, dv, col = carry[3 * hh:3 * hh + 3]
                q, do = qx_ref[hh, rows_q, :], dox_ref[hh, rows_q, :]
                st = lax.dot_general(kx_ref[hh], q, _NT, preferred_element_type=F32)
                pt = jnp.exp(st - lse_ref[0, i, hh:hh + 1, :])
                if masked:
                    pt = jnp.where(key_le_query, pt, 0.0)
                dst = pt * lax.dot_general(vx_ref[hh], do, _NT, preferred_element_type=F32)
                pb, dsb = pt.astype(BF16), dst.astype(BF16)
                dv = dv + lax.dot_general(pb, do, _NN, preferred_element_type=F32)
                dk = dk + lax.dot_general(dsb, q, _NN, preferred_element_type=F32)
                dqt_ref[hh, i] += lax.dot_general(kxt_ref[hh, 0], dsb, _NN, preferred_element_type=F32)
                for cb in range(tq // LANES):
                    col = col + dst[:, cb * LANES:(cb + 1) * LANES]
                dfq_ref[hh, i] += jnp.sum(dst.reshape(tq // HALO, HALO, tq), axis=0)
                out += [dk, dv, col]
            return tuple(out)

        zero = jnp.zeros((tq, LANES), F32)
        carry = step(j, (zero,) * 6, True)
        dk_a, dv_a, col_a, dk_b, dv_b, col_b = lax.fori_loop(j + 1, last_ref[p * n_q + j] + 1,
                                                             lambda i, cr: step(i, cr, False), carry)
        head_a = lax.broadcasted_iota(jnp.int32, (tq, LANES), 1) < HEAD_DIM
        dk_ref[...] = jnp.where(head_a, dk_a, pltpu.roll(dk_b, HEAD_DIM, 1)).astype(BF16)
        dv_ref[...] = jnp.where(head_a, dv_a, pltpu.roll(dv_b, HEAD_DIM, 1)).astype(BF16)
        rows8 = lax.broadcasted_iota(jnp.int32, (N_HEADS, tq), 0)
        dfk_a, dfk_b = (-jnp.sum(c.T, axis=0, keepdims=True) for c in (col_a, col_b))
        dfk_ref[0, 0] = jnp.where(rows8 == 0, dfk_a, jnp.where(rows8 == 1, dfk_b, 0.0))

    resident = pl.BlockSpec((2, s, LANES), lambda p, j, last: (p, 0, 0))
    key_rows = pl.BlockSpec((2, tq, LANES), lambda p, j, last: (p, j, 0))
    pair_out = pl.BlockSpec((tq, LANES), lambda p, j, last: (j, p))
    grid_spec = pltpu.PrefetchScalarGridSpec(
        num_scalar_prefetch=1, grid=(N_PAIRS, n_q),
        in_specs=[resident, resident, pl.BlockSpec((1, n_q, N_HEADS, tq), lambda p, j, last: (p, 0, 0, 0)),
                  key_rows, pl.BlockSpec((2, 1, LANES, tq), lambda p, j, last: (p, j, 0, 0)), key_rows],
        out_specs=[pair_out, pair_out, pl.BlockSpec((1, 1, N_HEADS, tq), lambda p, j, last: (p, j, 0, 0)),
                   pl.BlockSpec((2, n_q, LANES, tq), lambda p, j, last: (p, 0, 0, 0)),
                   pl.BlockSpec((2, n_q, HALO, tq), lambda p, j, last: (p, 0, 0, 0))])
    return pl.pallas_call(
        body, name="attn_bwd", grid_spec=grid_spec,
        out_shape=[jax.ShapeDtypeStruct((s, ATTN_W), BF16), jax.ShapeDtypeStruct((s, ATTN_W), BF16),
                   jax.ShapeDtypeStruct((N_PAIRS, n_q, N_HEADS, tq), F32),
                   jax.ShapeDtypeStruct((N_HEADS, n_q, LANES, tq), F32),
                   jax.ShapeDtypeStruct((N_HEADS, n_q, HALO, tq), F32)],
        compiler_params=_cp(("parallel", "arbitrary")))(last_blk, qx, dox, lse, kx, kxt, vx)


def _attn_dq_finish(dqt):
    _, n_q, _, tq = dqt.shape
    per = 4 if n_q % 4 == 0 else 1

    def body(dqt_ref, dq_ref):
        for b in range(per):
            a, bb = dqt_ref[0, b], dqt_ref[1, b]
            dq_ref[b * tq:(b + 1) * tq, :] = (
                jnp.concatenate([a[0:HEAD_DIM], bb[0:HEAD_DIM]], axis=0).T * Q_SCALE).astype(BF16)

    return pl.pallas_call(
        body, name="attn_dq_finish", grid=(N_PAIRS, n_q // per),
        in_specs=[pl.BlockSpec((2, per, LANES, tq), lambda p, i: (p, i, 0, 0))],
        out_specs=pl.BlockSpec((per * tq, LANES), lambda p, i: (i, p)),
        out_shape=jax.ShapeDtypeStruct((n_q * tq, ATTN_W), BF16),
        compiler_params=_cp(("parallel", "parallel")))(dqt)


def _ffn_act_fwd(up, w_ffn):
    s = up.shape[0]
    tm, tn = min(TM_FFN, s), TN_FFN
    nb = D_FF // tn

    def body(a_ref, g_ref, ap_ref, gp_ref, wa_ref, wg_ref, act_ref, u_ref):
        i = pl.program_id(1)

        def conv(blk_ref, prev_ref, w_ref):
            prev = jnp.where(i > 0, prev_ref[...].astype(F32)[HALO:], 0.0)
            return _conv_taps(jnp.concatenate([prev, blk_ref[...].astype(F32)], axis=0), w_ref[...])[HALO:]

        u_a, u_g = conv(a_ref, ap_ref, wa_ref), conv(g_ref, gp_ref, wg_ref)
        u_ref[0], u_ref[1] = u_a.astype(BF16), u_g.astype(BF16)
        act_ref[...] = (u_g * jax.nn.sigmoid(u_g) * u_a).astype(BF16)

    blk = lambda off: pl.BlockSpec((tm, tn), lambda n, i: (i, off + n))
    prev = lambda off: pl.BlockSpec(
        (HALO_BF16, tn), lambda n, i: (jnp.maximum(i * (tm // HALO_BF16) - 1, 0), off + n))
    wsp = lambda off: pl.BlockSpec((3, tn), lambda n, i: (0, off + n))
    return pl.pallas_call(
        body, name="ffn_act_fwd", grid=(nb, s // tm),
        in_specs=[blk(0), blk(nb), prev(0), prev(nb), wsp(0), wsp(nb)],
        out_specs=[pl.BlockSpec((tm, tn), lambda n, i: (i, n)), pl.BlockSpec((2, tm, tn), lambda n, i: (0, i, n))],
        out_shape=[jax.ShapeDtypeStruct((s, D_FF), BF16), jax.ShapeDtypeStruct((2, s, D_FF), BF16)],
        compiler_params=_cp(("parallel", "parallel")))(up, up, up, up, w_ffn, w_ffn)


def _ffn_act_bwd(up, u, dact, w_ffn):
    s = up.shape[0]
    tm, tn = min(TM_FFN, s), TN_FFN
    nb = D_FF // tn
    n_blk = s // tm

    def body(u_ref, un_ref, a_ref, g_ref, d_ref, dn_ref, wa_ref, wg_ref, dup_ref, gwa_ref, gwg_ref):
        i = pl.program_id(1)

        @pl.when(i == 0)
        def _():
            gwa_ref[...] = jnp.zeros_like(gwa_ref)
            gwg_ref[...] = jnp.zeros_like(gwg_ref)

        ext = lambda rows, next_rows: jnp.concatenate([rows, next_rows], axis=0)
        u_a, u_g = (ext(u_ref[h].astype(F32), un_ref[h].astype(F32)[:HALO]) for h in range(2))
        d_e = ext(d_ref[...].astype(F32), jnp.where(i < n_blk - 1, dn_ref[...].astype(F32)[:HALO], 0.0))
        sig = jax.nn.sigmoid(u_g)
        du_a = d_e * (u_g * sig)
        du_g = d_e * u_a * (sig * (1.0 + u_g * (1.0 - sig)))
        halves = ((gwa_ref, wa_ref[...], a_ref[...].astype(F32), du_a),
                  (gwg_ref, wg_ref[...], g_ref[...].astype(F32), du_g))
        for half, (gw_ref, w, upv, du) in enumerate(halves):
            du0, du1, du2 = du[:tm], _shift_up(du, 1)[:tm], _shift_up(du, 2)[:tm]
            dup_ref[half] = (w[2:3, :] * du0 + w[1:2, :] * du1 + w[0:1, :] * du2).astype(BF16)
            gw_ref[0:1, :] += jnp.sum(upv * du2, axis=0, keepdims=True)
            gw_ref[1:2, :] += jnp.sum(upv * du1, axis=0, keepdims=True)
            gw_ref[2:3, :] += jnp.sum(upv * du0, axis=0, keepdims=True)

    next_row = lambda halo: lambda i: jnp.minimum((i + 1) * (tm // halo), s // halo - 1)
    blk = lambda off: pl.BlockSpec((tm, tn), lambda n, i: (i, off + n))
    wsp = lambda off: pl.BlockSpec((3, tn), lambda n, i: (0, off + n))
    pair = lambda rows, row_of: pl.BlockSpec((2, rows, tn), lambda n, i: (0, row_of(i), n))
    return pl.pallas_call(
        body, name="ffn_act_bwd", grid=(nb, n_blk),
        in_specs=[pair(tm, lambda i: i), pair(HALO_BF16, next_row(HALO_BF16)), blk(0), blk(nb), blk(0),
                  pl.BlockSpec((HALO_BF16, tn), lambda n, i: (next_row(HALO_BF16)(i), n)), wsp(0), wsp(nb)],
        out_specs=[pair(tm, lambda i: i), wsp(0), wsp(0)],
        out_shape=[jax.ShapeDtypeStruct((2, s, D_FF), BF16),
                   jax.ShapeDtypeStruct((3, D_FF), F32), jax.ShapeDtypeStruct((3, D_FF), F32)],
        compiler_params=_cp(("parallel", "arbitrary")))(u, u, up, up, dact, dact, w_ffn, w_ffn)


def _adamw(w, g, m, v, name):
    r, c = g.shape
    tr = next((t for t in (512, 352, 256, 128, 64, 32, 16, 8) if r > t and r % t == 0), r)

    def body(w_ref, g_ref, m_ref, v_ref, d_ref, nm_ref, nv_ref):
        gv = g_ref[...]
        m_new = ADAM_B1 * m_ref[...] + (1.0 - ADAM_B1) * gv
        v_new = ADAM_B2 * v_ref[...] + (1.0 - ADAM_B2) * (gv * gv)
        m_hat = m_new / (1.0 - ADAM_B1 ** ADAM_STEP)
        v_hat = v_new / (1.0 - ADAM_B2 ** ADAM_STEP)
        d_ref[...] = -ADAM_LR * (m_hat / (jnp.sqrt(v_hat) + ADAM_EPS) + ADAM_WD * w_ref[...])
        nm_ref[...] = m_new
        nv_ref[...] = v_new

    tc = c if tr < r else next((t for t in (256, 128) if c > t and c % t == 0), c)
    pos = (lambda i: (i, 0)) if tr < r else (lambda i: (0, i))
    spec = pl.BlockSpec((tr, tc), pos)
    w_spec = spec if w.ndim == 2 else pl.BlockSpec((None, tr, tc), lambda i: (0, *pos(i)))
    shp = jax.ShapeDtypeStruct(w.shape, F32)
    return pl.pallas_call(
        body, name=name, grid=(r // tr * (c // tc),), in_specs=[w_spec, spec, w_spec, w_spec], out_specs=[w_spec] * 3,
        out_shape=[shp] * 3, compiler_params=_cp(("parallel",)))(w, g, m, v)


def _sum_rows_block(h):
    return h if h <= 352 else 256


def _pair_sum(view, recv, sel, name):
    n, _, h, c = view.shape
    tr = _sum_rows_block(h)

    def body(sel_ref, a_ref, b_ref, o_ref, ob_ref):
        t = a_ref[...] + b_ref[...]
        o_ref[...] = t
        ob_ref[...] = t.astype(BF16)

    blk = pl.BlockSpec((None, tr, c), lambda j, i, sel_ref: (j, i, 0))
    grid_spec = pltpu.PrefetchScalarGridSpec(
        num_scalar_prefetch=1, grid=(n, h // tr),
        in_specs=[pl.BlockSpec((None, None, tr, c), lambda j, i, sel_ref: (j, sel_ref[0], i, 0)),
                  pl.BlockSpec((None, None, tr, c), lambda j, i, sel_ref: (j, 0, i, 0))],
        out_specs=[blk, blk])
    return pl.pallas_call(
        body, name=name, grid_spec=grid_spec,
        out_shape=[jax.ShapeDtypeStruct((n, h, c), F32), jax.ShapeDtypeStruct((n, h, c), BF16)],
        compiler_params=_cp(("parallel", "parallel")))(sel, view, recv)


def _chip_sum(pair, got, sel, name):
    _, h, c = pair.shape
    tr = _sum_rows_block(h)
    nblk = h // tr

    def body(sel_ref, p_ref, g0_ref, g1_ref, g2_ref, o_ref):
        o_ref[...] = ((p_ref[...] + g0_ref[...].astype(F32)) + g1_ref[...].astype(F32)) + g2_ref[...].astype(F32)

    slot = lambda k: pl.BlockSpec((None, tr, c), lambda i, sel_ref: (k, i, 0))
    grid_spec = pltpu.PrefetchScalarGridSpec(
        num_scalar_prefetch=1, grid=(h // tr,),
        in_specs=[pl.BlockSpec((None, tr, c), lambda i, sel_ref: (sel_ref[1], i, 0)), slot(0), slot(1), slot(2)],
        out_specs=pl.BlockSpec((tr, c), lambda i, sel_ref: (sel_ref[0] * nblk + i, 0)))
    return pl.pallas_call(
        body, name=name, grid_spec=grid_spec, out_shape=jax.ShapeDtypeStruct((2 * h, c), F32),
        compiler_params=_cp(("parallel",)))(sel, pair, got, got, got)


def _place():
    return lax.axis_index("x"), lax.axis_index("y"), lax.axis_index("c")


def _other_chips(x, y):
    return [(1 - x, y), (x, 1 - y), (1 - x, 1 - y)]


def _hbm_specs(n):
    return [pl.BlockSpec(memory_space=pl.ANY)] * n


def _all_gather_weights(bigs, smalls):
    nb, ns = len(bigs), len(smalls)
    n = nb + ns

    def body(*refs):
        start, forward, finish = _gather_phases(refs[:n], refs[2 * n:3 * n], nb, *refs[3 * n:])
        start()
        forward()
        finish()

    arrays, landing, sems = _gather_operands(bigs, smalls)
    return pl.pallas_call(
        body, name="all_gather_weights",
        out_shape=[jax.ShapeDtypeStruct(b.shape, b.dtype) for b in landing],
        in_specs=_hbm_specs(2 * n), out_specs=_hbm_specs(n), input_output_aliases={n + k: k for k in range(n)},
        scratch_shapes=sems)(*arrays, *landing)


def _hosted_gather(refs, n, nb, step, total):
    ins, outs, send_sems, recv_sems = refs
    start, forward, finish = _gather_phases(ins, outs, nb, send_sems, recv_sems)
    pl.when(step == 0)(start)
    pl.when(step == (3 * total) // 4)(forward)
    return lambda: pl.when(step == total - 1)(finish)


def _in_proj(x, g, w_a, w_b, w_c, bigs, smalls):
    s, d = x.shape
    tm, tn = min(TM_MM, s), ATTN_W
    na, nq = w_a.shape[1] // tn, w_b.shape[1] // tn
    steps = na + nq + 1
    total = (s // tm) * steps
    nb, n = len(bigs), len(bigs) + len(smalls)
    arrays, landing, sems = _gather_operands(bigs, smalls)

    def body(x_ref, g_ref, wa_ref, wb_ref, wc_ref, *rest):
        z_ref, qkv_ref, f_ref, h_ref = rest[2 * n:2 * n + 4]
        h_scr = rest[-1]
        m, j = pl.program_id(0), pl.program_id(1)
        finish = _hosted_gather((rest[:n], rest[2 * n + 4:3 * n + 4]) + tuple(rest[3 * n + 4:3 * n + 6]), n, nb,
                                m * steps + j, total)

        @pl.when(j == 0)
        def _():
            xv = x_ref[...]
            hv = (xv * _rstd(xv) * g_ref[...]).astype(BF16)
            h_scr[...] = hv
            h_ref[...] = hv

        h = h_scr[...]

        @pl.when(j < na)
        def _():
            z_ref[...] = lax.dot_general(h, wa_ref[...], _NN, preferred_element_type=F32)

        @pl.when((j >= na) & (j < na + nq))
        def _():
            qkv_ref[...] = lax.dot_general(h, wb_ref[...], _NN, preferred_element_type=F32).astype(BF16)

        @pl.when(j == na + nq)
        def _():
            f_ref[...] = lax.dot_general(h, wc_ref[...], _NN, preferred_element_type=F32)

        finish()

    blk_a = lambda m, j: (m, jnp.minimum(j, na - 1))
    blk_b = lambda m, j: (m, jnp.clip(j - na, 0, nq - 1))
    outs = pl.pallas_call(
        body, name="in_proj", grid=(s // tm, steps),
        in_specs=[pl.BlockSpec((tm, d), lambda m, j: (m, 0)), pl.BlockSpec((1, d), lambda m, j: (0, 0)),
                  pl.BlockSpec((d, tn), lambda m, j: (0, jnp.minimum(j, na - 1))),
                  pl.BlockSpec((d, tn), lambda m, j: (0, jnp.clip(j - na, 0, nq - 1))),
                  pl.BlockSpec((d, LANES), lambda m, j: (0, 0))] + _hbm_specs(2 * n),
        out_specs=[pl.BlockSpec((tm, tn), blk_a), pl.BlockSpec((tm, tn), blk_b),
                   pl.BlockSpec((tm, LANES), lambda m, j: (m, 0)), pl.BlockSpec((tm, d), lambda m, j: (m, 0))]
        + _hbm_specs(n),
        out_shape=[jax.ShapeDtypeStruct((s, w_a.shape[1]), F32), jax.ShapeDtypeStruct((s, w_b.shape[1]), BF16),
                   jax.ShapeDtypeStruct((s, LANES), F32), jax.ShapeDtypeStruct((s, d), BF16)]
        + [jax.ShapeDtypeStruct(b.shape, b.dtype) for b in landing],
        input_output_aliases={5 + n + k: 4 + k for k in range(n)},
        scratch_shapes=sems + [pltpu.VMEM((tm, d), BF16)],
        compiler_params=_cp(("arbitrary", "arbitrary")))(x, g, w_a, w_b, w_c, *arrays, *landing)
    return outs[0], outs[1], outs[2], outs[3], outs[4:]


def _gather_operands(bigs, smalls):
    x, y, _ = _place()
    arrays = list(bigs) + list(smalls)
    landing = [lax.dynamic_update_index_in_dim(lax.empty((N_CHIPS,) + a.shape, a.dtype), a, 2 * x + y, 0)
               for a in arrays]
    n_sems = 6 * len(bigs) + 3 * len(smalls)
    return arrays, landing, [pltpu.SemaphoreType.DMA((n_sems,)), pltpu.SemaphoreType.DMA((n_sems,))]


def _gather_phases(ins, outs, nb, send_sems, recv_sems):
    n = len(ins)
    x, y, c = _place()
    my_chip = 2 * x + y
    chips = _other_chips(x, y)
    sibling = (x, y, 1 - c)

    def rows(k, which):
        h = ins[k].shape[0] // 2
        return pl.ds(which * h, h)

    def copy(sem, src, dst, to):
        return pltpu.make_async_remote_copy(src_ref=src, dst_ref=dst, send_sem=send_sems.at[sem],
                                            recv_sem=recv_sems.at[sem], device_id=to, device_id_type=MESH)

    def sends():
        out = [copy(6 * k + j, ins[k].at[rows(k, c)], outs[k].at[my_chip, rows(k, c)], (cx, cy, c))
               for k in range(nb) for j, (cx, cy) in enumerate(chips)]
        return out + [copy(6 * nb + 3 * (k - nb) + j, ins[k], outs[k].at[my_chip], (cx, cy, c))
                      for k in range(nb, n) for j, (cx, cy) in enumerate(chips)]

    def landed(k, j, which):
        cx, cy = chips[j]
        return outs[k].at[2 * cx + cy, rows(k, which)]

    def forwards():
        return [copy(6 * k + 3 + j, landed(k, j, c), landed(k, j, c), sibling)
                for j in range(3) for k in range(nb)]

    def start():
        for cp in sends():
            cp.start()

    def forward():
        for j in range(3):
            for k in range(nb):
                copy(6 * k + j, landed(k, j, c), landed(k, j, c), (x, y, c)).wait_recv()
                copy(6 * k + 3 + j, landed(k, j, c), landed(k, j, c), sibling).start()

    def finish():
        for j, (cx, cy) in enumerate(chips):
            for k in range(nb):
                copy(6 * k + 3 + j, landed(k, j, 1 - c), landed(k, j, 1 - c), (x, y, c)).wait_recv()
            for k in range(nb, n):
                arrived = outs[k].at[2 * cx + cy]
                copy(6 * nb + 3 * (k - nb) + j, arrived, arrived, (x, y, c)).wait_recv()
        for cp in sends() + forwards():
            cp.wait_send()

    return start, forward, finish


def _pair_exchange(views, name):
    n = len(views)

    def body(*refs):
        ins, outs, send_sems, recv_sems = refs[:n], refs[n:2 * n], refs[2 * n], refs[2 * n + 1]
        x, y, c = _place()
        copies = [pltpu.make_async_remote_copy(
            src_ref=ins[k].at[:, pl.ds(1 - c, 1)], dst_ref=outs[k], send_sem=send_sems.at[k],
            recv_sem=recv_sems.at[k], device_id=(x, y, 1 - c), device_id_type=MESH) for k in range(n)]
        for cp in copies:
            cp.start()
        for cp in copies:
            cp.wait()

    return pl.pallas_call(
        body, name=name,
        out_shape=[jax.ShapeDtypeStruct((v.shape[0], 1) + v.shape[2:], v.dtype) for v in views],
        in_specs=_hbm_specs(n), out_specs=_hbm_specs(n),
        scratch_shapes=[pltpu.SemaphoreType.DMA((n,)), pltpu.SemaphoreType.DMA((n,))])(*views)


def _scatter_operands(parts):
    n = len(parts)
    return ([jax.ShapeDtypeStruct((3,) + p.shape[1:], p.dtype) for p in parts],
            [pltpu.SemaphoreType.DMA((3 * n,)), pltpu.SemaphoreType.DMA((3 * n,))])


def _scatter_phases(ins, outs, send_sems, recv_sems):
    x, y, c = _place()

    def copies():
        return [pltpu.make_async_remote_copy(
            src_ref=ins[k].at[pl.ds(2 * cx + cy, 1)], dst_ref=outs[k].at[pl.ds(r, 1)], send_sem=send_sems.at[3 * k + r],
            recv_sem=recv_sems.at[3 * k + r], device_id=(cx, cy, c), device_id_type=MESH)
            for k in range(len(ins)) for r, (cx, cy) in enumerate(_other_chips(x, y))]

    def start():
        for cp in copies():
            cp.start()

    def finish():
        for cp in copies():
            cp.wait()

    return start, finish


def _hosted_scatter(ins, outs, sems, step, total):
    start, finish = _scatter_phases(ins, outs, *sems)
    pl.when(step == 0)(start)
    return lambda: pl.when(step == total - 1)(finish)


def _join_halves(shards):
    n = len(shards)

    def body(*refs):
        ins, outs, send_sems, recv_sems = refs[:n], refs[n:2 * n], refs[2 * n], refs[2 * n + 1]
        x, y, c = _place()

        def rows(ref, which):
            h = ref.shape[0] // 2
            return ref.at[pl.ds(which * h, h)]

        sent = [pltpu.make_async_remote_copy(
            src_ref=rows(ins[k], c), dst_ref=rows(outs[k], c), send_sem=send_sems.at[k], recv_sem=recv_sems.at[k],
            device_id=(x, y, 1 - c), device_id_type=MESH) for k in range(n)]
        for cp in sent:
            cp.start()
        for k in range(n):
            pltpu.make_async_remote_copy(
                src_ref=rows(ins[k], 1 - c), dst_ref=rows(outs[k], 1 - c), send_sem=send_sems.at[k],
                recv_sem=recv_sems.at[k], device_id=(x, y, 1 - c), device_id_type=MESH).wait_recv()
        for cp in sent:
            cp.wait_send()

    return pl.pallas_call(
        body, name="half_exchange", out_shape=[jax.ShapeDtypeStruct(a.shape, a.dtype) for a in shards],
        in_specs=_hbm_specs(n), out_specs=_hbm_specs(n), input_output_aliases={k: k for k in range(n)},
        scratch_shapes=[pltpu.SemaphoreType.DMA((n,)), pltpu.SemaphoreType.DMA((n,))])(*shards)


def _all_reduce_small(packet):
    rows, width = packet.shape
    n_dev = 8

    def body(x_ref, out_ref, gath, send_sems, recv_sems):
        x, y, c = _place()
        me, sibling = (x, y, c), (x, y, 1 - c)
        chips = _other_chips(x, y)

        def slot(px, py, pc):
            return gath.at[pl.ds((4 * px + 2 * py + pc) * rows, rows), :]

        def copy(k, block, to, src=None):
            return pltpu.make_async_remote_copy(
                src_ref=slot(*block) if src is None else src, dst_ref=slot(*block), send_sem=send_sems.at[k],
                recv_sem=recv_sems.at[k], device_id=to, device_id_type=MESH)

        first = [copy(0, me, sibling, src=x_ref)]
        first += [copy(1 + j, me, (*chip, c), src=x_ref) for j, chip in enumerate(chips)]
        for cp in first:
            cp.start()
        gath[pl.ds((4 * x + 2 * y + c) * rows, rows), :] = x_ref[...]
        passed = [copy(4 + j, (*chip, c), sibling) for j, chip in enumerate(chips)]
        for j, chip in enumerate(chips):
            copy(1 + j, (*chip, c), me).wait_recv()
            passed[j].start()
        copy(0, sibling, me).wait_recv()
        for j, chip in enumerate(chips):
            copy(4 + j, (*chip, 1 - c), me).wait_recv()
        for cp in first + passed:
            cp.wait_send()
        acc = gath[0:rows, :]
        for d in range(1, n_dev):
            acc = acc + gath[d * rows:(d + 1) * rows, :]
        out_ref[...] = acc

    return pl.pallas_call(
        body, name="all_reduce_small", out_shape=jax.ShapeDtypeStruct((rows, width), F32),
        in_specs=[pl.BlockSpec(memory_space=pltpu.VMEM)], out_specs=pl.BlockSpec(memory_space=pltpu.VMEM),
        scratch_shapes=[pltpu.VMEM((n_dev * rows, width), F32), pltpu.SemaphoreType.DMA((7,)),
                        pltpu.SemaphoreType.DMA((7,))])(packet)


def _flat_rows(parts, width, row_multiple):
    flat = jnp.concatenate([p.astype(F32).reshape(-1) for p in parts])
    rows = -(-flat.shape[0] // width)
    rows = -(-rows // row_multiple) * row_multiple
    return jnp.pad(flat, (0, rows * width - flat.shape[0])).reshape(rows, width)


def _unflatten(flat2d, shapes):
    flat = flat2d.reshape(-1)
    out, off = [], 0
    for shp in shapes:
        n = 1
        for dim in shp:
            n *= dim
        out.append(flat[off:off + n].reshape(shp))
        off += n
    return out


def _core_and_chip():
    x, y, c = _place()
    return jnp.stack([c, 2 * x + y]).astype(jnp.int32)


def _pair_sums(chip_major, names, call_name):
    views = [g.reshape(N_CHIPS, 2, g.shape[1] // 2, g.shape[2]) for g in chip_major]
    recv = _pair_exchange(views, call_name)
    sel = _core_and_chip()
    return [_pair_sum(v, r, sel, "pair_sum_" + nm) for v, r, nm in zip(views, recv, names)]


def _finish_grads(pairs, got, names):
    sel = _core_and_chip()
    return _join_halves([_chip_sum(p, g, sel, "chip_sum_" + nm) for (p, _), g, nm in zip(pairs, got, names)])


def kernel(x, g_mix, w_in, b_f, w_conv, g_conv_out, g_attn_out, w_o, g_ffn, w_up, w_ffn_conv, w_down, g_final, loss_target, m_g_mix, m_w_in, m_b_f, m_w_conv, m_g_conv_out, m_g_attn_out, m_w_o, m_g_ffn, m_w_up, m_w_ffn_conv, m_w_down, m_g_final, v_g_mix, v_w_in, v_b_f, v_w_conv, v_g_conv_out, v_g_attn_out, v_w_o, v_g_ffn, v_w_up, v_w_ffn_conv, v_w_down, v_g_final):
    s = x.shape[1]
    x0 = x[0]
    target = loss_target[0]
    d = D_MODEL
    x_pos, y_pos, _ = _place()
    my_chip = 2 * x_pos + y_pos

    (c_in,) = _all_gather_weights([w_in[0].astype(BF16)], [])
    shard_w = c_in.shape[2]

    def w_in_cols(lo, hi):
        parts = [c_in[j][:, max(lo - j * shard_w, 0):min(hi - j * shard_w, shard_w)]
                 for j in range(N_CHIPS) if lo < (j + 1) * shard_w and hi > j * shard_w]
        return parts[0] if len(parts) == 1 else jnp.concatenate(parts, axis=1)

    c3 = 3 * CONV_CH
    w_a, w_b = w_in_cols(0, c3), w_in_cols(c3, c3 + 3 * ATTN_W)
    w_c = jnp.pad(w_in_cols(c3 + 3 * ATTN_W, N_CHIPS * shard_w), ((0, 0), (0, LANES - N_HEADS)))
    w_q, w_k, w_v = (w_in_cols(c3 + i * ATTN_W, c3 + (i + 1) * ATTN_W) for i in range(3))
    b_pad = jnp.pad(b_f, ((0, 0), (0, LANES - N_HEADS)))

    z_a, qkv, f_log, h1, (c_o, c_up, c_conv, c_ffn) = _in_proj(
        x0, g_mix, w_a, w_b, w_c, [w_o[0].astype(BF16), w_up[0].astype(BF16)], [w_conv[0], w_ffn_conv[0]])
    fb = _gate_fwd(f_log, b_pad)
    qx, kx, kxt, vx, vt, bounds, (c_down,) = _attn_prep(qkv, fb, [w_down[0].astype(BF16)])
    w_o_full = c_o.reshape(d, d)
    w_down_full = c_down.reshape(D_FF, d)
    w_conv_full = jnp.concatenate([c_conv[j] for j in range(N_CHIPS)], axis=1)
    w_ffn_full = jnp.concatenate([c_ffn[j] for j in range(N_CHIPS)], axis=1)
    n_up = c_up.shape[2]
    first_blk, last_blk = _key_block_ranges(bounds)
    o_attn, lse = _attn_fwd_t(qx, kx, vt, first_blk)
    mix = _mixer_fwd(z_a, o_attn, w_conv_full, g_conv_out, g_attn_out)
    x2, h2 = _mm("nn", [mix], [w_o_full], F32, 512, d, "out_proj", add=x0, norm_g=g_ffn)
    up = _mm("nn", [h2], [c_up], BF16, TM_MM, n_up, "up_proj", b_chips=True)
    act, u_conv = _ffn_act_fwd(up, w_ffn_full)
    dx3, dx3_b, loss_row, gg_final = _down_proj_loss(act, w_down_full, x2, target, g_final.reshape(1, d))

    dact = _mm("nt", [dx3_b], [w_down_full], BF16, TM_MM, 1408, "d_act")
    gw_down = _mm_tn(act, dx3_b, 1408, 1024, "gw_down")
    dup, gwf_lin, gwf_gate = _ffn_act_bwd(up, u_conv, dact, w_ffn_full)
    dh2 = _mm("nt", [(dup, j // 2, j % 2, n_up) for j in range(N_CHIPS)], [(c_up, j) for j in range(N_CHIPS)],
              F32, TM_MM, 512, "d_h2")
    gw_up = _mm_tn(h2, dup, 1024, n_up, "gw_up", out_chips=True)
    dx2, dx2_b, gg_ffn = _rms_bwd(x2, dh2, g_ffn, dx3, "rms_ffn_bwd", True)
    dmix = _mm("nt", [dx2_b], [w_o_full], F32, TM_MM, 512, "d_mix")
    gw_o = _mm_tn(mix, dx2_b, 1024, 1024, "gw_o")
    early = _pair_sums([gw_o.reshape(N_CHIPS, d // N_CHIPS, d), gw_up, gw_down.reshape(N_CHIPS, D_FF // N_CHIPS, d)],
                       ["w_o", "w_up", "w_down"], "pair_exchange")
    dz_a, dox, gw_conv, gg_conv_out, gg_attn_out, (got_o, got_down) = _mixer_bwd(
        z_a, o_attn, dmix, w_conv_full, g_conv_out, g_attn_out, [early[0][1], early[2][1]])
    dk, dv, dfk, dqt, dfq = _attn_bwd_t(qx, dox, kx, kxt, vx, lse, last_blk)
    dq = _attn_dq_finish(dqt)
    d_f = (jnp.transpose(dfk[:, :, 0:2, :], (1, 3, 0, 2)).reshape(s, N_HEADS)
           + jnp.transpose(jnp.sum(dfq, axis=2), (1, 2, 0)).reshape(s, N_HEADS))
    df_b, gb_f = _gate_bwd(f_log, b_pad, jnp.pad(d_f, ((0, 0), (0, LANES - N_HEADS))))
    gw_a = _mm_tn(h1, dz_a, 1024, c3, "gw_in_conv")
    gw_q = _mm_tn(h1, dq, 1024, ATTN_W, "gw_in_q")
    gw_k = _mm_tn(h1, dk, 1024, ATTN_W, "gw_in_k")
    gw_v = _mm_tn(h1, dv, 1024, ATTN_W, "gw_in_v")
    gw_c = _mm_tn(h1, df_b, 1024, LANES, "gw_in_gate")
    gw_in = jnp.concatenate([gw_a, gw_q, gw_k, gw_v, gw_c[:, :N_HEADS]], axis=1)
    n_in = IN_COLS // N_CHIPS
    gw_in = jnp.stack([gw_in[:, j * n_in:(j + 1) * n_in] for j in range(N_CHIPS)])
    late = _pair_sums([gw_in], ["w_in"], "pair_exchange_w_in")
    dh1, (got_up, got_in) = _mm("nt", [dz_a, dq, dk, dv, df_b], [w_a, w_q, w_k, w_v, w_c], F32, TM_MM, 512, "d_h1",
                                scatter=[early[1][1], late[0][1]])
    grad_x, gg_mix = _rms_bwd(x0, dh1, g_mix, dx2, "rms_mix_bwd", False)
    g_w_in, g_w_o, g_w_up, g_w_down = _finish_grads(late + early, [got_in, got_o, got_up, got_down],
                                                    ["w_in", "w_o", "w_up", "w_down"])

    gw_ffn = jnp.concatenate([gwf_lin, gwf_gate], axis=1)
    small_parts = [gg_mix, gg_conv_out, gg_attn_out, gg_ffn, gg_final, gb_f[:, :N_HEADS], loss_row[:, 0:1], gw_conv,
                   gw_ffn]
    small_shapes = [a.shape for a in small_parts]
    tot = _unflatten(_all_reduce_small(_flat_rows(small_parts, d, 8)), small_shapes)
    g_g_mix, g_g_conv_out, g_g_attn_out, g_g_ffn, g_g_final, g_b_f, loss_sum, g_conv_full, g_ffn_full = tot
    loss = loss_sum[0, 0]
    g_g_final = g_g_final[0]
    g_w_conv = lax.dynamic_slice_in_dim(g_conv_full, my_chip * (CONV_CH // N_CHIPS), CONV_CH // N_CHIPS, axis=1)
    g_w_ffn = lax.dynamic_slice_in_dim(g_ffn_full, my_chip * n_up, n_up, axis=1)

    swap = lambda a: jnp.swapaxes(a, -1, -2)
    u_w_in = tuple(swap(a) for a in _adamw(swap(w_in), swap(g_w_in), swap(m_w_in), swap(v_w_in), "adam_w_in"))
    u_w_o = _adamw(w_o, g_w_o, m_w_o, v_w_o, "adam_w_o")
    u_w_up = _adamw(w_up, g_w_up, m_w_up, v_w_up, "adam_w_up")
    u_w_down = _adamw(w_down, g_w_down, m_w_down, v_w_down, "adam_w_down")

    small_w = [g_mix, b_f, g_conv_out, g_attn_out, g_ffn, g_final, w_conv, w_ffn_conv]
    small_g = [g_g_mix, g_b_f, g_g_conv_out, g_g_attn_out, g_g_ffn, g_g_final, g_w_conv, g_w_ffn]
    small_m = [m_g_mix, m_b_f, m_g_conv_out, m_g_attn_out, m_g_ffn, m_g_final, m_w_conv, m_w_ffn_conv]
    small_v = [v_g_mix, v_b_f, v_g_conv_out, v_g_attn_out, v_g_ffn, v_g_final, v_w_conv, v_w_ffn_conv]
    shapes = [a.shape for a in small_w]
    pack = lambda arrs: _flat_rows(arrs, LANES, 8)
    sd, sm, sv = _adamw(pack(small_w), pack(small_g), pack(small_m), pack(small_v), "adam_small")
    sd, sm, sv = _unflatten(sd, shapes), _unflatten(sm, shapes), _unflatten(sv, shapes)
    (d_g_mix, d_b_f, d_g_conv_out, d_g_attn_out, d_g_ffn, d_g_final, d_w_conv, d_w_ffn) = sd
    (nm_g_mix, nm_b_f, nm_g_conv_out, nm_g_attn_out, nm_g_ffn, nm_g_final, nm_w_conv, nm_w_ffn) = sm
    (nv_g_mix, nv_b_f, nv_g_conv_out, nv_g_attn_out, nv_g_ffn, nv_g_final, nv_w_conv, nv_w_ffn) = sv

    grads = (g_g_mix, g_w_in[None], g_b_f, g_w_conv[None], g_g_conv_out, g_g_attn_out, g_w_o[None], g_g_ffn,
             g_w_up[None], g_w_ffn[None], g_w_down[None], g_g_final)
    deltas = (d_g_mix, u_w_in[0], d_b_f, d_w_conv, d_g_conv_out, d_g_attn_out, u_w_o[0], d_g_ffn, u_w_up[0],
              d_w_ffn, u_w_down[0], d_g_final)
    new_m = (nm_g_mix, u_w_in[1], nm_b_f, nm_w_conv, nm_g_conv_out, nm_g_attn_out, u_w_o[1], nm_g_ffn, u_w_up[1],
             nm_w_ffn, u_w_down[1], nm_g_final)
    new_v = (nv_g_mix, u_w_in[2], nv_b_f, nv_w_conv, nv_g_conv_out, nv_g_attn_out, u_w_o[2], nv_g_ffn, u_w_up[2],
             nv_w_ffn, u_w_down[2], nv_g_final)
    return (loss, grad_x[None], *grads, *deltas, *new_m, *new_v)
```
